```python
import jax, jax.numpy as jnp
from jax import lax
import numpy as np

D_MODEL = 1024
BATCH = 8
SEQ = 4096
DEPTH = 1

N_MEM = 256
HG_HEADS = 4
HG_DK = 128
HG_DV = 128
ML_HEADS = 4
ML_DK = 128
ML_DV = 128
D_HG = HG_HEADS * HG_DV
D_ML = ML_HEADS * ML_DV
D_MIX = D_HG + D_ML
CHUNK = 64
ML_CONV = 4
FFN_CONV = 3
D_FF = 2816
CA_HEADS = 4
CA_DH = D_MODEL // CA_HEADS
ALPHA = (2.0 * DEPTH) ** 0.25
BETA = (8.0 * DEPTH) ** -0.25
LN_EPS = 1e-5
NEG_BIG = -1e30

IN_SIZES = (HG_HEADS * HG_DK, HG_HEADS * HG_DK, D_HG, D_HG,
            ML_HEADS * ML_DK, ML_HEADS * ML_DK, D_ML, D_ML, ML_HEADS, ML_HEADS)
IN_SPLITS = tuple(int(c) for c in np.cumsum(IN_SIZES)[:-1])
D_IN = int(sum(IN_SIZES))
FG_START = int(sum(IN_SIZES[:-1]))

kernel_name = 'hybrid_hgrn2_mlstm_deepnorm'


def layer_norm(x, g, b):
    xf = x.astype(jnp.float32)
    mu = jnp.mean(xf, axis=-1, keepdims=True)
    var = jnp.mean(jnp.square(xf - mu), axis=-1, keepdims=True)
    return ((xf - mu) * lax.rsqrt(var + LN_EPS) * g + b).astype(x.dtype)


def head_rms_norm(h, w):
    y = h * lax.rsqrt(jnp.mean(h * h, axis=-1, keepdims=True) + LN_EPS)
    return y.reshape(*h.shape[:-2], -1) * w


def head_layer_norm(h, w):
    mu = jnp.mean(h, axis=-1, keepdims=True)
    var = jnp.mean(jnp.square(h - mu), axis=-1, keepdims=True)
    y = (h - mu) * lax.rsqrt(var + LN_EPS)
    return y.reshape(*h.shape[:-2], -1) * w


def causal_dwconv(x, w, b):
    k_w = w.shape[0]
    s = x.shape[1]
    xp = jnp.pad(x, ((0, 0), (k_w - 1, 0), (0, 0)))
    y = b
    for j in range(k_w):
        y = y + xp[:, j:j + s] * w[j]
    return y


def to_chunks(t):
    bsz, s, h = t.shape[:3]
    t = t.reshape(bsz, s // CHUNK, CHUNK, h, *t.shape[3:])
    return jnp.moveaxis(jnp.moveaxis(t, 1, 0), 3, 2)


def from_chunks(t):
    t = jnp.moveaxis(jnp.moveaxis(t, 2, 3), 0, 1)
    return t.reshape(t.shape[0], t.shape[1] * t.shape[2], *t.shape[3:])


def hgrn2_chunkwise(q, k, v, log_f):
    bsz, _, h, dk = q.shape
    dv = v.shape[-1]
    mask = jnp.tril(jnp.ones((CHUNK, CHUNK), dtype=bool))

    def step(state, inp):
        q_, k_, v_, lf = inp
        b = jnp.cumsum(lf, axis=2)
        b_ref = b[:, :, CHUNK // 2 - 1:CHUNK // 2]
        attn = jnp.einsum('bhtd,bhsd->bhts', q_ * jnp.exp(b - b_ref), k_ * jnp.exp(b_ref - b))
        attn = jnp.where(mask, attn, 0.0)
        o = (jnp.einsum('bhts,bhsv->bhtv', attn, v_)
             + jnp.einsum('bhtd,bhdv->bhtv', q_ * jnp.exp(b), state))
        b_last = b[:, :, -1:]
        state = (jnp.exp(b_last)[:, :, 0, :, None] * state
                 + jnp.einsum('bhsd,bhsv->bhdv', k_ * jnp.exp(b_last - b), v_))
        return state, o

    s0 = jnp.zeros((bsz, h, dk, dv), jnp.float32)
    _, o = lax.scan(step, s0, (to_chunks(q), to_chunks(k), to_chunks(v), to_chunks(log_f)))
    return from_chunks(o)


def mlstm_chunkwise(q, k, v, i_log, f_log):
    bsz, _, h, dk = q.shape
    dv = v.shape[-1]
    mask = jnp.tril(jnp.ones((CHUNK, CHUNK), dtype=bool))

    def step(carry, inp):
        c_st, n_st, m_st = carry
        q_, k_, v_, ig, lf = inp
        b = jnp.cumsum(lf, axis=-1)
        g = b[..., -1]
        d = jnp.where(mask, b[..., :, None] - b[..., None, :] + ig[..., None, :], -jnp.inf)
        inter = b + m_st[..., None]
        m_t = jnp.maximum(inter, jnp.max(d, axis=-1))
        w = jnp.exp(d - m_t[..., None])
        s = jnp.einsum('bhtd,bhsd->bhts', q_, k_) * w
        w_inter = jnp.exp(inter - m_t)
        num = (jnp.einsum('bhts,bhsv->bhtv', s, v_)
               + w_inter[..., None] * jnp.einsum('bhtd,bhdv->bhtv', q_, c_st))
        den = jnp.sum(s, axis=-1) + w_inter * jnp.einsum('bhtd,bhd->bht', q_, n_st)
        h_out = num / jnp.maximum(jnp.abs(den), jnp.exp(-m_t))[..., None]
        a = g[..., None] - b + ig
        m_new = jnp.maximum(g + m_st, jnp.max(a, axis=-1))
        decay = jnp.exp(g + m_st - m_new)
        wk = k_ * jnp.exp(a - m_new[..., None])[..., None]
        c_st = decay[..., None, None] * c_st + jnp.einsum('bhsd,bhsv->bhdv', wk, v_)
        n_st = decay[..., None] * n_st + jnp.sum(wk, axis=2)
        return (c_st, n_st, m_new), h_out

    init = (jnp.zeros((bsz, h, dk, dv), jnp.float32),
            jnp.zeros((bsz, h, dk), jnp.float32),
            jnp.full((bsz, h), NEG_BIG, jnp.float32))
    _, o = lax.scan(step, init, (to_chunks(q), to_chunks(k), to_chunks(v),
                                 to_chunks(i_log), to_chunks(f_log)))
    return from_chunks(o)


def hybrid_mixer(x, w_in, b_in, lb, hg_norm_w, ml_conv_w, ml_conv_b, ml_norm_w, w_out):
    bsz, s, _ = x.shape
    proj = x @ w_in + b_in
    hq, hf, hi, hg, mq, mk, mv, mo, mi, mf = jnp.split(proj, IN_SPLITS, axis=-1)
    f32 = lambda t: t.astype(jnp.float32)
    heads = lambda t, nh: t.reshape(bsz, s, nh, -1)
    sig = jax.nn.sigmoid(f32(hf))
    log_f = jnp.log(lb + (1.0 - lb) * sig)
    k_in = (1.0 - lb) * jax.nn.sigmoid(-f32(hf))
    o_hg = hgrn2_chunkwise(heads(jax.nn.silu(f32(hq)), HG_HEADS), heads(k_in, HG_HEADS),
                           heads(f32(hi), HG_HEADS), heads(log_f, HG_HEADS))
    o_hg = head_rms_norm(o_hg, hg_norm_w) * jax.nn.silu(f32(hg))
    qk = jax.nn.silu(f32(causal_dwconv(jnp.concatenate([mq, mk], axis=-1), ml_conv_w, ml_conv_b)))
    q_ml, k_ml = jnp.split(qk, 2, axis=-1)
    h_ml = mlstm_chunkwise(heads(q_ml, ML_HEADS) * (ML_DK ** -0.5), heads(k_ml, ML_HEADS),
                           heads(f32(mv), ML_HEADS), f32(mi), jax.nn.log_sigmoid(f32(mf)))
    o_ml = jax.nn.sigmoid(f32(mo)) * head_layer_norm(h_ml, ml_norm_w)
    y = jnp.concatenate([o_hg, o_ml], axis=-1).astype(x.dtype)
    return y @ w_out


def memory_cross_attention(x, mem, wq, wkv, wo):
    bsz, s, d = x.shape
    q = (x @ wq).reshape(bsz, s, CA_HEADS, CA_DH)
    k, v = jnp.split(mem @ wkv, 2, axis=-1)
    k = k.reshape(bsz, -1, CA_HEADS, CA_DH)
    v = v.reshape(bsz, -1, CA_HEADS, CA_DH)
    sc = jnp.einsum('bshd,bmhd->bhsm', q, k).astype(jnp.float32) * (CA_DH ** -0.5)
    p = jax.nn.softmax(sc, axis=-1).astype(v.dtype)
    o = jnp.einsum('bhsm,bmhd->bshd', p, v).reshape(bsz, s, d)
    return o @ wo


def conv_ffn(x, w_up, conv_w, conv_b, w_down):
    u = causal_dwconv(x @ w_up, conv_w, conv_b)
    gate, val = jnp.split(u, 2, axis=-1)
    return (jax.nn.gelu(gate) * val) @ w_down


def _fwd_setup_inputs(seed: int = 0) -> dict:
    key = jax.random.key(seed)
    ks = jax.random.split(key, 24)
    nrm = lambda k, shape, scale: jax.random.normal(k, shape, jnp.float32) * scale
    b_in = nrm(ks[3], (DEPTH, D_IN), 0.02)
    b_in = b_in.at[:, FG_START:].add(jnp.linspace(3.0, 6.0, ML_HEADS, dtype=jnp.float32))
    return {
        'x': nrm(ks[0], (BATCH, SEQ, D_MODEL), 1.0),
        'mem': nrm(ks[1], (BATCH, N_MEM, D_MODEL), 1.0),
        'w_in': nrm(ks[2], (DEPTH, D_MODEL, D_IN), D_MODEL ** -0.5),
        'b_in': b_in,
        'hg_lb_logits': 1.0 + nrm(ks[4], (DEPTH + 1, D_HG), 0.3),
        'hg_norm_w': 1.0 + nrm(ks[5], (DEPTH, D_HG), 0.02),
        'ml_conv_w': nrm(ks[6], (DEPTH, ML_CONV, 2 * D_ML), ML_CONV ** -0.5),
        'ml_conv_b': nrm(ks[7], (DEPTH, 2 * D_ML), 0.02),
        'ml_norm_w': 1.0 + nrm(ks[8], (DEPTH, D_ML), 0.02),
        'w_out': nrm(ks[9], (DEPTH, D_MIX, D_MODEL), BETA * D_MIX ** -0.5),
        'ln1_g': 1.0 + nrm(ks[10], (DEPTH, D_MODEL), 0.02),
        'ln1_b': nrm(ks[11], (DEPTH, D_MODEL), 0.02),
        'ca_wq': nrm(ks[12], (DEPTH, D_MODEL, D_MODEL), D_MODEL ** -0.5),
        'ca_wkv': nrm(ks[13], (DEPTH, D_MODEL, 2 * D_MODEL), D_MODEL ** -0.5),
        'ca_wo': nrm(ks[14], (DEPTH, D_MODEL, D_MODEL), BETA * D_MODEL ** -0.5),
        'ln2_g': 1.0 + nrm(ks[15], (DEPTH, D_MODEL), 0.02),
        'ln2_b': nrm(ks[16], (DEPTH, D_MODEL), 0.02),
        'ffn_w_up': nrm(ks[17], (DEPTH, D_MODEL, 2 * D_FF), D_MODEL ** -0.5),
        'ffn_conv_w': nrm(ks[18], (DEPTH, FFN_CONV, 2 * D_FF), FFN_CONV ** -0.5),
        'ffn_conv_b': nrm(ks[19], (DEPTH, 2 * D_FF), 0.02),
        'ffn_w_down': nrm(ks[20], (DEPTH, D_FF, D_MODEL), BETA * D_FF ** -0.5),
        'ln3_g': 1.0 + nrm(ks[21], (DEPTH, D_MODEL), 0.02),
        'ln3_b': nrm(ks[22], (DEPTH, D_MODEL), 0.02),
    }


def _fwd_reference(x, mem, w_in, b_in, hg_lb_logits, hg_norm_w, ml_conv_w, ml_conv_b, ml_norm_w,
              w_out, ln1_g, ln1_b, ca_wq, ca_wkv, ca_wo, ln2_g, ln2_b,
              ffn_w_up, ffn_conv_w, ffn_conv_b, ffn_w_down, ln3_g, ln3_b):
    lower_bounds = jnp.cumsum(jax.nn.softmax(hg_lb_logits.astype(jnp.float32), axis=0), axis=0)
    for l in range(DEPTH):
        mix = hybrid_mixer(x, w_in[l], b_in[l], lower_bounds[l], hg_norm_w[l],
                           ml_conv_w[l], ml_conv_b[l], ml_norm_w[l], w_out[l])
        x = layer_norm(ALPHA * x + mix, ln1_g[l], ln1_b[l])
        ca = memory_cross_attention(x, mem, ca_wq[l], ca_wkv[l], ca_wo[l])
        x = layer_norm(ALPHA * x + ca, ln2_g[l], ln2_b[l])
        ff = conv_ffn(x, ffn_w_up[l], ffn_conv_w[l], ffn_conv_b[l], ffn_w_down[l])
        x = layer_norm(ALPHA * x + ff, ln3_g[l], ln3_b[l])
    return x


import jax as _jax
import jax.numpy as _jnp

TWIN_FORMAT = 'train_step'
FWD_PARAMS = ['x', 'mem', 'w_in', 'b_in', 'hg_lb_logits', 'hg_norm_w', 'ml_conv_w', 'ml_conv_b', 'ml_norm_w', 'w_out', 'ln1_g', 'ln1_b', 'ca_wq', 'ca_wkv', 'ca_wo', 'ln2_g', 'ln2_b', 'ffn_w_up', 'ffn_conv_w', 'ffn_conv_b', 'ffn_w_down', 'ln3_g', 'ln3_b']
TWIN_WEIGHTS = ['w_in', 'b_in', 'hg_lb_logits', 'hg_norm_w', 'ml_conv_w', 'ml_conv_b', 'ml_norm_w', 'w_out', 'ln1_g', 'ln1_b', 'ca_wq', 'ca_wkv', 'ca_wo', 'ln2_g', 'ln2_b', 'ffn_w_up', 'ffn_conv_w', 'ffn_conv_b', 'ffn_w_down', 'ln3_g', 'ln3_b']
TWIN_DIFF_INPUT = 'x'
TWIN_INPUTS = ['x', 'mem', 'w_in', 'b_in', 'hg_lb_logits', 'hg_norm_w', 'ml_conv_w', 'ml_conv_b', 'ml_norm_w', 'w_out', 'ln1_g', 'ln1_b', 'ca_wq', 'ca_wkv', 'ca_wo', 'ln2_g', 'ln2_b', 'ffn_w_up', 'ffn_conv_w', 'ffn_conv_b', 'ffn_w_down', 'ln3_g', 'ln3_b', 'loss_target', 'm_w_in', 'm_b_in', 'm_hg_lb_logits', 'm_hg_norm_w', 'm_ml_conv_w', 'm_ml_conv_b', 'm_ml_norm_w', 'm_w_out', 'm_ln1_g', 'm_ln1_b', 'm_ca_wq', 'm_ca_wkv', 'm_ca_wo', 'm_ln2_g', 'm_ln2_b', 'm_ffn_w_up', 'm_ffn_conv_w', 'm_ffn_conv_b', 'm_ffn_w_down', 'm_ln3_g', 'm_ln3_b', 'v_w_in', 'v_b_in', 'v_hg_lb_logits', 'v_hg_norm_w', 'v_ml_conv_w', 'v_ml_conv_b', 'v_ml_norm_w', 'v_w_out', 'v_ln1_g', 'v_ln1_b', 'v_ca_wq', 'v_ca_wkv', 'v_ca_wo', 'v_ln2_g', 'v_ln2_b', 'v_ffn_w_up', 'v_ffn_conv_w', 'v_ffn_conv_b', 'v_ffn_w_down', 'v_ln3_g', 'v_ln3_b']
TWIN_OUTPUTS = ['loss', 'grad_x', 'grad_w_in', 'grad_b_in', 'grad_hg_lb_logits', 'grad_hg_norm_w', 'grad_ml_conv_w', 'grad_ml_conv_b', 'grad_ml_norm_w', 'grad_w_out', 'grad_ln1_g', 'grad_ln1_b', 'grad_ca_wq', 'grad_ca_wkv', 'grad_ca_wo', 'grad_ln2_g', 'grad_ln2_b', 'grad_ffn_w_up', 'grad_ffn_conv_w', 'grad_ffn_conv_b', 'grad_ffn_w_down', 'grad_ln3_g', 'grad_ln3_b', 'delta_w_in', 'delta_b_in', 'delta_hg_lb_logits', 'delta_hg_norm_w', 'delta_ml_conv_w', 'delta_ml_conv_b', 'delta_ml_norm_w', 'delta_w_out', 'delta_ln1_g', 'delta_ln1_b', 'delta_ca_wq', 'delta_ca_wkv', 'delta_ca_wo', 'delta_ln2_g', 'delta_ln2_b', 'delta_ffn_w_up', 'delta_ffn_conv_w', 'delta_ffn_conv_b', 'delta_ffn_w_down', 'delta_ln3_g', 'delta_ln3_b', 'new_m_w_in', 'new_m_b_in', 'new_m_hg_lb_logits', 'new_m_hg_norm_w', 'new_m_ml_conv_w', 'new_m_ml_conv_b', 'new_m_ml_norm_w', 'new_m_w_out', 'new_m_ln1_g', 'new_m_ln1_b', 'new_m_ca_wq', 'new_m_ca_wkv', 'new_m_ca_wo', 'new_m_ln2_g', 'new_m_ln2_b', 'new_m_ffn_w_up', 'new_m_ffn_conv_w', 'new_m_ffn_conv_b', 'new_m_ffn_w_down', 'new_m_ln3_g', 'new_m_ln3_b', 'new_v_w_in', 'new_v_b_in', 'new_v_hg_lb_logits', 'new_v_hg_norm_w', 'new_v_ml_conv_w', 'new_v_ml_conv_b', 'new_v_ml_norm_w', 'new_v_w_out', 'new_v_ln1_g', 'new_v_ln1_b', 'new_v_ca_wq', 'new_v_ca_wkv', 'new_v_ca_wo', 'new_v_ln2_g', 'new_v_ln2_b', 'new_v_ffn_w_up', 'new_v_ffn_conv_w', 'new_v_ffn_conv_b', 'new_v_ffn_w_down', 'new_v_ln3_g', 'new_v_ln3_b']
TWIN_LEAF_KINDS = {'loss': 'loss', 'grad_x': 'grad_x', 'grad_w_in': 'grad_w', 'grad_b_in': 'grad_w', 'grad_hg_lb_logits': 'grad_w', 'grad_hg_norm_w': 'grad_w', 'grad_ml_conv_w': 'grad_w', 'grad_ml_conv_b': 'grad_w', 'grad_ml_norm_w': 'grad_w', 'grad_w_out': 'grad_w', 'grad_ln1_g': 'grad_w', 'grad_ln1_b': 'grad_w', 'grad_ca_wq': 'grad_w', 'grad_ca_wkv': 'grad_w', 'grad_ca_wo': 'grad_w', 'grad_ln2_g': 'grad_w', 'grad_ln2_b': 'grad_w', 'grad_ffn_w_up': 'grad_w', 'grad_ffn_conv_w': 'grad_w', 'grad_ffn_conv_b': 'grad_w', 'grad_ffn_w_down': 'grad_w', 'grad_ln3_g': 'grad_w', 'grad_ln3_b': 'grad_w', 'delta_w_in': 'delta_w', 'delta_b_in': 'delta_w', 'delta_hg_lb_logits': 'delta_w', 'delta_hg_norm_w': 'delta_w', 'delta_ml_conv_w': 'delta_w', 'delta_ml_conv_b': 'delta_w', 'delta_ml_norm_w': 'delta_w', 'delta_w_out': 'delta_w', 'delta_ln1_g': 'delta_w', 'delta_ln1_b': 'delta_w', 'delta_ca_wq': 'delta_w', 'delta_ca_wkv': 'delta_w', 'delta_ca_wo': 'delta_w', 'delta_ln2_g': 'delta_w', 'delta_ln2_b': 'delta_w', 'delta_ffn_w_up': 'delta_w', 'delta_ffn_conv_w': 'delta_w', 'delta_ffn_conv_b': 'delta_w', 'delta_ffn_w_down': 'delta_w', 'delta_ln3_g': 'delta_w', 'delta_ln3_b': 'delta_w', 'new_m_w_in': 'new_m', 'new_m_b_in': 'new_m', 'new_m_hg_lb_logits': 'new_m', 'new_m_hg_norm_w': 'new_m', 'new_m_ml_conv_w': 'new_m', 'new_m_ml_conv_b': 'new_m', 'new_m_ml_norm_w': 'new_m', 'new_m_w_out': 'new_m', 'new_m_ln1_g': 'new_m', 'new_m_ln1_b': 'new_m', 'new_m_ca_wq': 'new_m', 'new_m_ca_wkv': 'new_m', 'new_m_ca_wo': 'new_m', 'new_m_ln2_g': 'new_m', 'new_m_ln2_b': 'new_m', 'new_m_ffn_w_up': 'new_m', 'new_m_ffn_conv_w': 'new_m', 'new_m_ffn_conv_b': 'new_m', 'new_m_ffn_w_down': 'new_m', 'new_m_ln3_g': 'new_m', 'new_m_ln3_b': 'new_m', 'new_v_w_in': 'new_v', 'new_v_b_in': 'new_v', 'new_v_hg_lb_logits': 'new_v', 'new_v_hg_norm_w': 'new_v', 'new_v_ml_conv_w': 'new_v', 'new_v_ml_conv_b': 'new_v', 'new_v_ml_norm_w': 'new_v', 'new_v_w_out': 'new_v', 'new_v_ln1_g': 'new_v', 'new_v_ln1_b': 'new_v', 'new_v_ca_wq': 'new_v', 'new_v_ca_wkv': 'new_v', 'new_v_ca_wo': 'new_v', 'new_v_ln2_g': 'new_v', 'new_v_ln2_b': 'new_v', 'new_v_ffn_w_up': 'new_v', 'new_v_ffn_conv_w': 'new_v', 'new_v_ffn_conv_b': 'new_v', 'new_v_ffn_w_down': 'new_v', 'new_v_ln3_g': 'new_v', 'new_v_ln3_b': 'new_v'}


def _forward(args):
    return _fwd_reference(*[args[k] for k in FWD_PARAMS])


def _output_shape():
    out = _jax.eval_shape(lambda: _forward(_fwd_setup_inputs(0)))
    return out.shape, out.dtype

N_MICROBATCH = 1
ADAM_LR = 0.001
ADAM_B1 = 0.9
ADAM_B2 = 0.999
ADAM_EPS = 1e-08
ADAM_WD = 0.01
ADAM_STEP = 10
PER_EXAMPLE_BATCH_AXIS = {'x': 0, 'mem': 0, 'loss_target': 0}
SHARED_INPUTS = []
_WEIGHT_DTYPES = {'w_in': _jnp.float32, 'b_in': _jnp.float32, 'hg_lb_logits': _jnp.float32, 'hg_norm_w': _jnp.float32, 'ml_conv_w': _jnp.float32, 'ml_conv_b': _jnp.float32, 'ml_norm_w': _jnp.float32, 'w_out': _jnp.float32, 'ln1_g': _jnp.float32, 'ln1_b': _jnp.float32, 'ca_wq': _jnp.float32, 'ca_wkv': _jnp.float32, 'ca_wo': _jnp.float32, 'ln2_g': _jnp.float32, 'ln2_b': _jnp.float32, 'ffn_w_up': _jnp.float32, 'ffn_conv_w': _jnp.float32, 'ffn_conv_b': _jnp.float32, 'ffn_w_down': _jnp.float32, 'ln3_g': _jnp.float32, 'ln3_b': _jnp.float32}
MOMENT_SCALE = {'w_in': 4.158375e-02, 'b_in': 3.107494e-01, 'hg_lb_logits': 4.961707e-03, 'hg_norm_w': 6.123753e-02, 'ml_conv_w': 3.368085e-02, 'ml_conv_b': 3.445159e-02, 'ml_norm_w': 5.170702e-02, 'w_out': 8.838510e-02, 'ln1_g': 9.492659e-01, 'ln1_b': 4.408481e-01, 'ca_wq': 9.759196e-03, 'ca_wkv': 1.036396e-02, 'ca_wo': 1.819813e-02, 'ln2_g': 9.483510e-01, 'ln2_b': 4.396261e-01, 'ffn_w_up': 3.337320e-02, 'ffn_conv_w': 3.463195e-02, 'ffn_conv_b': 4.015573e-02, 'ffn_w_down': 9.189070e-02, 'ln3_g': 3.205528e+01, 'ln3_b': 1.342211e+00}


def _to_microbatches(a, axis):
    t = _jnp.moveaxis(a, axis, 0)
    t = t.reshape((N_MICROBATCH, t.shape[0] // N_MICROBATCH) + t.shape[1:])
    return _jnp.moveaxis(t, 1, axis + 1)


def setup_inputs(seed: int = 0) -> dict:
    inp = _fwd_setup_inputs(seed)
    key = _jax.random.fold_in(_jax.random.key(seed), 7919)
    shape, _ = _output_shape()
    out = dict(inp)
    out["loss_target"] = _jax.random.normal(_jax.random.fold_in(key, 0), shape, _jnp.float32)
    for i, name in enumerate(TWIN_WEIGHTS):
        w = inp[name].astype(_jnp.float32)
        if MOMENT_SCALE is None:
            s = _jnp.sqrt(_jnp.mean(_jnp.square(w)) + 1e-30)
        else:
            s = MOMENT_SCALE[name]
        km, kv = _jax.random.split(_jax.random.fold_in(key, i + 1))
        out[name] = w
        out["m_" + name] = s * _jax.random.normal(km, w.shape, _jnp.float32)
        out["v_" + name] = (s * s) * _jax.random.uniform(kv, w.shape, _jnp.float32, 0.5, 1.5)
    if N_MICROBATCH > 1:
        for name, axis in PER_EXAMPLE_BATCH_AXIS.items():
            out[name] = _to_microbatches(out[name], axis)
    return {'x': out['x'], 'mem': out['mem'], 'w_in': out['w_in'], 'b_in': out['b_in'], 'hg_lb_logits': out['hg_lb_logits'], 'hg_norm_w': out['hg_norm_w'], 'ml_conv_w': out['ml_conv_w'], 'ml_conv_b': out['ml_conv_b'], 'ml_norm_w': out['ml_norm_w'], 'w_out': out['w_out'], 'ln1_g': out['ln1_g'], 'ln1_b': out['ln1_b'], 'ca_wq': out['ca_wq'], 'ca_wkv': out['ca_wkv'], 'ca_wo': out['ca_wo'], 'ln2_g': out['ln2_g'], 'ln2_b': out['ln2_b'], 'ffn_w_up': out['ffn_w_up'], 'ffn_conv_w': out['ffn_conv_w'], 'ffn_conv_b': out['ffn_conv_b'], 'ffn_w_down': out['ffn_w_down'], 'ln3_g': out['ln3_g'], 'ln3_b': out['ln3_b'], 'loss_target': out['loss_target'], 'm_w_in': out['m_w_in'], 'm_b_in': out['m_b_in'], 'm_hg_lb_logits': out['m_hg_lb_logits'], 'm_hg_norm_w': out['m_hg_norm_w'], 'm_ml_conv_w': out['m_ml_conv_w'], 'm_ml_conv_b': out['m_ml_conv_b'], 'm_ml_norm_w': out['m_ml_norm_w'], 'm_w_out': out['m_w_out'], 'm_ln1_g': out['m_ln1_g'], 'm_ln1_b': out['m_ln1_b'], 'm_ca_wq': out['m_ca_wq'], 'm_ca_wkv': out['m_ca_wkv'], 'm_ca_wo': out['m_ca_wo'], 'm_ln2_g': out['m_ln2_g'], 'm_ln2_b': out['m_ln2_b'], 'm_ffn_w_up': out['m_ffn_w_up'], 'm_ffn_conv_w': out['m_ffn_conv_w'], 'm_ffn_conv_b': out['m_ffn_conv_b'], 'm_ffn_w_down': out['m_ffn_w_down'], 'm_ln3_g': out['m_ln3_g'], 'm_ln3_b': out['m_ln3_b'], 'v_w_in': out['v_w_in'], 'v_b_in': out['v_b_in'], 'v_hg_lb_logits': out['v_hg_lb_logits'], 'v_hg_norm_w': out['v_hg_norm_w'], 'v_ml_conv_w': out['v_ml_conv_w'], 'v_ml_conv_b': out['v_ml_conv_b'], 'v_ml_norm_w': out['v_ml_norm_w'], 'v_w_out': out['v_w_out'], 'v_ln1_g': out['v_ln1_g'], 'v_ln1_b': out['v_ln1_b'], 'v_ca_wq': out['v_ca_wq'], 'v_ca_wkv': out['v_ca_wkv'], 'v_ca_wo': out['v_ca_wo'], 'v_ln2_g': out['v_ln2_g'], 'v_ln2_b': out['v_ln2_b'], 'v_ffn_w_up': out['v_ffn_w_up'], 'v_ffn_conv_w': out['v_ffn_conv_w'], 'v_ffn_conv_b': out['v_ffn_conv_b'], 'v_ffn_w_down': out['v_ffn_w_down'], 'v_ln3_g': out['v_ln3_g'], 'v_ln3_b': out['v_ln3_b']}


def _loss(weights, diff, rest, loss_target):
    with _jax.named_scope("forward"):
        args = {**rest, TWIN_DIFF_INPUT: diff, **{k: w.astype(_WEIGHT_DTYPES[k]) for k, w in weights.items()}}
        y = _forward(args)
    with _jax.named_scope("loss_head"):
        err = _jnp.square(y.astype(_jnp.float32) - loss_target)
        return 0.5 * _jnp.sum(_jnp.mean(err, axis=-1)) if err.ndim else 0.5 * err


def _adamw(w, g, m, v):
    m = ADAM_B1 * m + (1.0 - ADAM_B1) * g
    v = ADAM_B2 * v + (1.0 - ADAM_B2) * _jnp.square(g)
    m_hat = m / (1.0 - ADAM_B1 ** ADAM_STEP)
    v_hat = v / (1.0 - ADAM_B2 ** ADAM_STEP)
    delta = -ADAM_LR * (m_hat / (_jnp.sqrt(v_hat) + ADAM_EPS) + ADAM_WD * w)
    return delta, m, v


def reference(x, mem, w_in, b_in, hg_lb_logits, hg_norm_w, ml_conv_w, ml_conv_b, ml_norm_w, w_out, ln1_g, ln1_b, ca_wq, ca_wkv, ca_wo, ln2_g, ln2_b, ffn_w_up, ffn_conv_w, ffn_conv_b, ffn_w_down, ln3_g, ln3_b, loss_target, m_w_in, m_b_in, m_hg_lb_logits, m_hg_norm_w, m_ml_conv_w, m_ml_conv_b, m_ml_norm_w, m_w_out, m_ln1_g, m_ln1_b, m_ca_wq, m_ca_wkv, m_ca_wo, m_ln2_g, m_ln2_b, m_ffn_w_up, m_ffn_conv_w, m_ffn_conv_b, m_ffn_w_down, m_ln3_g, m_ln3_b, v_w_in, v_b_in, v_hg_lb_logits, v_hg_norm_w, v_ml_conv_w, v_ml_conv_b, v_ml_norm_w, v_w_out, v_ln1_g, v_ln1_b, v_ca_wq, v_ca_wkv, v_ca_wo, v_ln2_g, v_ln2_b, v_ffn_w_up, v_ffn_conv_w, v_ffn_conv_b, v_ffn_w_down, v_ln3_g, v_ln3_b):
    given = dict(x=x, mem=mem, w_in=w_in, b_in=b_in, hg_lb_logits=hg_lb_logits, hg_norm_w=hg_norm_w, ml_conv_w=ml_conv_w, ml_conv_b=ml_conv_b, ml_norm_w=ml_norm_w, w_out=w_out, ln1_g=ln1_g, ln1_b=ln1_b, ca_wq=ca_wq, ca_wkv=ca_wkv, ca_wo=ca_wo, ln2_g=ln2_g, ln2_b=ln2_b, ffn_w_up=ffn_w_up, ffn_conv_w=ffn_conv_w, ffn_conv_b=ffn_conv_b, ffn_w_down=ffn_w_down, ln3_g=ln3_g, ln3_b=ln3_b, loss_target=loss_target, m_w_in=m_w_in, m_b_in=m_b_in, m_hg_lb_logits=m_hg_lb_logits, m_hg_norm_w=m_hg_norm_w, m_ml_conv_w=m_ml_conv_w, m_ml_conv_b=m_ml_conv_b, m_ml_norm_w=m_ml_norm_w, m_w_out=m_w_out, m_ln1_g=m_ln1_g, m_ln1_b=m_ln1_b, m_ca_wq=m_ca_wq, m_ca_wkv=m_ca_wkv, m_ca_wo=m_ca_wo, m_ln2_g=m_ln2_g, m_ln2_b=m_ln2_b, m_ffn_w_up=m_ffn_w_up, m_ffn_conv_w=m_ffn_conv_w, m_ffn_conv_b=m_ffn_conv_b, m_ffn_w_down=m_ffn_w_down, m_ln3_g=m_ln3_g, m_ln3_b=m_ln3_b, v_w_in=v_w_in, v_b_in=v_b_in, v_hg_lb_logits=v_hg_lb_logits, v_hg_norm_w=v_hg_norm_w, v_ml_conv_w=v_ml_conv_w, v_ml_conv_b=v_ml_conv_b, v_ml_norm_w=v_ml_norm_w, v_w_out=v_w_out, v_ln1_g=v_ln1_g, v_ln1_b=v_ln1_b, v_ca_wq=v_ca_wq, v_ca_wkv=v_ca_wkv, v_ca_wo=v_ca_wo, v_ln2_g=v_ln2_g, v_ln2_b=v_ln2_b, v_ffn_w_up=v_ffn_w_up, v_ffn_conv_w=v_ffn_conv_w, v_ffn_conv_b=v_ffn_conv_b, v_ffn_w_down=v_ffn_w_down, v_ln3_g=v_ln3_g, v_ln3_b=v_ln3_b)
    weights = {n: given[n] for n in TWIN_WEIGHTS}
    shared = {n: given[n] for n in SHARED_INPUTS}
    per_example = {n: given[n] for n in ['x', 'mem']}
    grad_fn = _jax.value_and_grad(_loss, argnums=(0, 1))

    def one_microbatch(ex, loss_target):
        ex = dict(ex)
        diff = ex.pop(TWIN_DIFF_INPUT)
        return grad_fn(weights, diff, {**shared, **ex}, loss_target)

    if N_MICROBATCH == 1:
        loss, (grad_w, grad_x) = one_microbatch(per_example, given["loss_target"])
    else:
        def body(carry, xs):
            loss_sum, grad_sum = carry
            l_k, (gw_k, gx_k) = one_microbatch(xs[0], xs[1])
            with _jax.named_scope("update"):
                return (loss_sum + l_k, _jax.tree.map(_jnp.add, grad_sum, gw_k)), gx_k

        init = (_jnp.zeros((), _jnp.float32), _jax.tree.map(_jnp.zeros_like, weights))
        (loss, grad_w), grad_x = _jax.lax.scan(body, init, (per_example, given["loss_target"]))
    with _jax.named_scope("update"):
        delta_w, new_m, new_v = {}, {}, {}
        for n in TWIN_WEIGHTS:
            delta_w[n], new_m[n], new_v[n] = _adamw(weights[n], grad_w[n], given["m_" + n], given["v_" + n])
    return (loss, grad_x, *[grad_w[n] for n in TWIN_WEIGHTS], *[delta_w[n] for n in TWIN_WEIGHTS],
            *[new_m[n] for n in TWIN_WEIGHTS], *[new_v[n] for n in TWIN_WEIGHTS])
```

```python
import functools

import jax
import jax.numpy as jnp
from jax import lax
from jax.experimental import pallas as pl
from jax.experimental.pallas import tpu as pltpu

F32 = jnp.float32
BF16 = jnp.bfloat16
HIGHEST = lax.Precision.HIGHEST
MESH = pl.DeviceIdType.MESH

N_DEV = 8
D_MODEL = 1024
N_MEM = 256
N_HEADS = 4
D_HEAD = 128
D_GROUP = N_HEADS * D_HEAD
CHUNK = 64
ML_CONV = 4
FFN_CONV = 3
D_FF = 2816
D_UP = 2 * D_FF
CA_HEADS = 4
CA_DH = D_MODEL // CA_HEADS
D_IN = 8 * D_GROUP + 2 * N_HEADS
D_IN_MAIN = 8 * D_GROUP
LANES = 128
SUBLANES = 8
D_PROJ = D_IN_MAIN + LANES
ALPHA = 2.0 ** 0.25
LN_EPS = 1e-5
NEG_BIG = -1e30
ADAM_LR = 0.001
ADAM_B1 = 0.9
ADAM_B2 = 0.999
ADAM_EPS = 1e-08
ADAM_WD = 0.01
ADAM_STEP = 10
VMEM_LIMIT = 56 * 1024 * 1024

SEG_HQ, SEG_HF, SEG_HI, SEG_HG, SEG_MQ, SEG_MK, SEG_MV, SEG_MO = (4 * i for i in range(8))


def _params(sem):
    return pltpu.CompilerParams(dimension_semantics=sem, vmem_limit_bytes=VMEM_LIMIT)


def _dg(a, b, ca, cb, precision=None):
    return lax.dot_general(a, b, (((ca,), (cb,)), ((), ())), precision=precision,
                           preferred_element_type=F32)


def _nn_raw(a, b):
    return _dg(a.astype(BF16), b.astype(BF16), 1, 0)


def _nt_raw(a, b):
    return _dg(a.astype(BF16), b.astype(BF16), 1, 1)


def _tn_raw(a, b):
    return _dg(a.astype(BF16), b.astype(BF16), 0, 0)


@jax.custom_vjp
def _nn(a, b):
    return _nn_raw(a, b)


_nn.defvjp(lambda a, b: (_nn_raw(a, b), (a, b)),
           lambda res, g: (_nt_raw(g, res[1]), _tn_raw(res[0], g)))


@jax.custom_vjp
def _nt(a, b):
    return _nt_raw(a, b)


_nt.defvjp(lambda a, b: (_nt_raw(a, b), (a, b)),
           lambda res, g: (_nn_raw(g, res[1]), _tn_raw(g, res[0])))


@jax.custom_vjp
def _tn(a, b):
    return _tn_raw(a, b)


_tn.defvjp(lambda a, b: (_tn_raw(a, b), (a, b)),
           lambda res, g: (_nt_raw(res[1], g), _nn_raw(res[0], g)))


def _layer_norm(z, g, b):
    mu = jnp.mean(z, axis=-1, keepdims=True)
    var = jnp.mean(jnp.square(z - mu), axis=-1, keepdims=True)
    return (z - mu) * lax.rsqrt(var + LN_EPS) * g + b


def _matmul_nn(a, w, bias, tm, tn, name):
    m, k = a.shape
    n = w.shape[1]

    def body(*refs):
        a_ref, w_ref = refs[0], refs[1]
        o_ref = refs[-1]
        acc = _nn_raw(a_ref[...], w_ref[...])
        if bias is not None:
            acc = acc + refs[2][...]
        o_ref[...] = acc

    in_specs = [pl.BlockSpec((tm, k), lambda i, j: (i, 0)), pl.BlockSpec((k, tn), lambda i, j: (0, j))]
    args = [a, w]
    if bias is not None:
        in_specs.append(pl.BlockSpec((1, tn), lambda i, j: (0, j)))
        args.append(bias)
    return pl.pallas_call(
        body, name=name, grid=(m // tm, n // tn), in_specs=in_specs,
        out_specs=pl.BlockSpec((tm, tn), lambda i, j: (i, j)),
        out_shape=jax.ShapeDtypeStruct((m, n), F32),
        compiler_params=_params(("parallel", "parallel")),
    )(*args)


def _matmul_nt(d, w, add, scale, tm, tk, name):
    m, n = d.shape
    k = w.shape[0]

    def body(*refs):
        d_ref, w_ref = refs[0], refs[1]
        o_ref = refs[-1]
        acc = _nt_raw(d_ref[...], w_ref[...])
        if add is not None:
            acc = acc + scale * refs[2][...]
        o_ref[...] = acc

    in_specs = [pl.BlockSpec((tm, n), lambda i, j: (i, 0)), pl.BlockSpec((tk, n), lambda i, j: (j, 0))]
    args = [d, w]
    if add is not None:
        in_specs.append(pl.BlockSpec((tm, tk), lambda i, j: (i, j)))
        args.append(add)
    return pl.pallas_call(
        body, name=name, grid=(m // tm, k // tk), in_specs=in_specs,
        out_specs=pl.BlockSpec((tm, tk), lambda i, j: (i, j)),
        out_shape=jax.ShapeDtypeStruct((m, k), F32),
        compiler_params=_params(("parallel", "parallel")),
    )(*args)


def _matmul_tn(a, b, tm, tn, tt, name):
    t, m = a.shape
    n = b.shape[1]

    def body(a_ref, b_ref, o_ref):
        @pl.when(pl.program_id(2) == 0)
        def _():
            o_ref[...] = jnp.zeros_like(o_ref)

        o_ref[...] += _tn_raw(a_ref[...], b_ref[...])

    return pl.pallas_call(
        body, name=name, grid=(m // tm, n // tn, t // tt),
        in_specs=[pl.BlockSpec((tt, tm), lambda i, j, kk: (kk, i)),
                  pl.BlockSpec((tt, tn), lambda i, j, kk: (kk, j))],
        out_specs=pl.BlockSpec((tm, tn), lambda i, j, kk: (i, j)),
        out_shape=jax.ShapeDtypeStruct((m, n), F32),
        compiler_params=_params(("parallel", "parallel", "arbitrary")),
    )(a, b)


def _colsum(a, tn, name):
    t, n = a.shape

    def body(a_ref, o_ref):
        o_ref[...] = jnp.sum(a_ref[...], axis=0, keepdims=True)

    return pl.pallas_call(
        body, name=name, grid=(n // tn,),
        in_specs=[pl.BlockSpec((t, tn), lambda j: (0, j))],
        out_specs=pl.BlockSpec((1, tn), lambda j: (0, j)),
        out_shape=jax.ShapeDtypeStruct((1, n), F32),
        compiler_params=_params(("parallel",)),
    )(a)


ROW_TILE = 512


def _conv_fwd_tile(pad_ref, w_ref, b_ref, r0, rows, taps):
    acc = b_ref[...]
    for j in range(taps):
        acc = acc + pad_ref[pl.ds(SUBLANES - (taps - 1 - j) + r0, rows), :] * w_ref[j:j + 1, :]
    return acc


def _conv_bwd_tile(dpad_ref, w_ref, r0, rows, taps):
    acc = None
    for j in range(taps):
        term = dpad_ref[pl.ds(r0 + (taps - 1 - j), rows), :] * w_ref[j:j + 1, :]
        acc = term if acc is None else acc + term
    return acc


def _ml_conv_fwd(proj, conv_w, conv_b):
    s = proj.shape[0]
    nblk = 2 * D_GROUP // LANES

    def body(x_ref, w_ref, b_ref, o_ref, pad_ref):
        pad_ref[0:SUBLANES, :] = jnp.zeros((SUBLANES, LANES), F32)
        pad_ref[SUBLANES:, :] = x_ref[...]
        for r0 in range(0, s, ROW_TILE):
            rows = min(ROW_TILE, s - r0)
            o_ref[r0:r0 + rows, :] = jax.nn.silu(_conv_fwd_tile(pad_ref, w_ref, b_ref, r0, rows, ML_CONV))

    return pl.pallas_call(
        body, name="ml_conv_fwd", grid=(nblk,),
        in_specs=[pl.BlockSpec((s, LANES), lambda j: (0, SEG_MQ + j)),
                  pl.BlockSpec((ML_CONV, LANES), lambda j: (0, j)),
                  pl.BlockSpec((1, LANES), lambda j: (0, j))],
        out_specs=pl.BlockSpec((s, LANES), lambda j: (0, j)),
        out_shape=jax.ShapeDtypeStruct((s, 2 * D_GROUP), F32),
        scratch_shapes=[pltpu.VMEM((s + SUBLANES, LANES), F32)],
        compiler_params=_params(("parallel",)),
    )(proj, conv_w, conv_b)


def _ml_conv_bwd(proj, conv_w, conv_b, d_qk):
    s = proj.shape[0]
    nblk = 2 * D_GROUP // LANES

    def body(x_ref, w_ref, b_ref, dy_ref, dx_ref, dw_ref, db_ref, pad_ref, dpad_ref):
        pad_ref[0:SUBLANES, :] = jnp.zeros((SUBLANES, LANES), F32)
        pad_ref[SUBLANES:, :] = x_ref[...]
        dpad_ref[s:, :] = jnp.zeros((SUBLANES, LANES), F32)
        db = jnp.zeros((1, LANES), F32)
        for r0 in range(0, s, ROW_TILE):
            rows = min(ROW_TILE, s - r0)
            pre = _conv_fwd_tile(pad_ref, w_ref, b_ref, r0, rows, ML_CONV)
            _, vjp = jax.vjp(jax.nn.silu, pre)
            d_pre, = vjp(dy_ref[r0:r0 + rows, :])
            dpad_ref[r0:r0 + rows, :] = d_pre
            db = db + jnp.sum(d_pre, axis=0, keepdims=True)
        db_ref[...] = db
        dws = [jnp.zeros((1, LANES), F32) for _ in range(ML_CONV)]
        for r0 in range(0, s, ROW_TILE):
            rows = min(ROW_TILE, s - r0)
            dx_ref[r0:r0 + rows, :] = _conv_bwd_tile(dpad_ref, w_ref, r0, rows, ML_CONV)
            d_pre = dpad_ref[r0:r0 + rows, :]
            for j in range(ML_CONV):
                xs = pad_ref[pl.ds(SUBLANES - (ML_CONV - 1 - j) + r0, rows), :]
                dws[j] = dws[j] + jnp.sum(d_pre * xs, axis=0, keepdims=True)
        for j in range(ML_CONV):
            dw_ref[j:j + 1, :] = dws[j]

    return pl.pallas_call(
        body, name="ml_conv_bwd", grid=(nblk,),
        in_specs=[pl.BlockSpec((s, LANES), lambda j: (0, SEG_MQ + j)),
                  pl.BlockSpec((ML_CONV, LANES), lambda j: (0, j)),
                  pl.BlockSpec((1, LANES), lambda j: (0, j)),
                  pl.BlockSpec((s, LANES), lambda j: (0, j))],
        out_specs=[pl.BlockSpec((s, LANES), lambda j: (0, j)),
                   pl.BlockSpec((ML_CONV, LANES), lambda j: (0, j)),
                   pl.BlockSpec((1, LANES), lambda j: (0, j))],
        out_shape=[jax.ShapeDtypeStruct((s, 2 * D_GROUP), F32),
                   jax.ShapeDtypeStruct((ML_CONV, 2 * D_GROUP), F32),
                   jax.ShapeDtypeStruct((1, 2 * D_GROUP), F32)],
        scratch_shapes=[pltpu.VMEM((s + SUBLANES, LANES), F32), pltpu.VMEM((s + SUBLANES, LANES), F32)],
        compiler_params=_params(("parallel",)),
    )(proj, conv_w, conv_b, d_qk)


def _gelu_mul(a, b):
    return jax.nn.gelu(a) * b


def _ffn_conv_fwd(u, conv_w, conv_b):
    s = u.shape[0]
    nblk = D_FF // LANES

    def body(g_ref, v_ref, wg_ref, wv_ref, bg_ref, bv_ref, o_ref, gpad_ref, vpad_ref):
        for pad_ref, x_ref in ((gpad_ref, g_ref), (vpad_ref, v_ref)):
            pad_ref[0:SUBLANES, :] = jnp.zeros((SUBLANES, LANES), F32)
            pad_ref[SUBLANES:, :] = x_ref[...]
        for r0 in range(0, s, ROW_TILE):
            rows = min(ROW_TILE, s - r0)
            ug = _conv_fwd_tile(gpad_ref, wg_ref, bg_ref, r0, rows, FFN_CONV)
            uv = _conv_fwd_tile(vpad_ref, wv_ref, bv_ref, r0, rows, FFN_CONV)
            o_ref[r0:r0 + rows, :] = _gelu_mul(ug, uv)

    col = lambda off: (lambda j: (0, off + j))
    return pl.pallas_call(
        body, name="ffn_conv_fwd", grid=(nblk,),
        in_specs=[pl.BlockSpec((s, LANES), col(0)), pl.BlockSpec((s, LANES), col(nblk)),
                  pl.BlockSpec((FFN_CONV, LANES), col(0)), pl.BlockSpec((FFN_CONV, LANES), col(nblk)),
                  pl.BlockSpec((1, LANES), col(0)), pl.BlockSpec((1, LANES), col(nblk))],
        out_specs=pl.BlockSpec((s, LANES), col(0)),
        out_shape=jax.ShapeDtypeStruct((s, D_FF), F32),
        scratch_shapes=[pltpu.VMEM((s + SUBLANES, LANES), F32), pltpu.VMEM((s + SUBLANES, LANES), F32)],
        compiler_params=_params(("parallel",)),
    )(u, u, conv_w, conv_w, conv_b, conv_b)


def _ffn_conv_bwd(u, conv_w, conv_b, d_h):
    s = u.shape[0]
    nblk = D_FF // LANES

    def body(g_ref, v_ref, wg_ref, wv_ref, bg_ref, bv_ref, dh_ref,
             dug_ref, duv_ref, dwg_ref, dwv_ref, dbg_ref, dbv_ref,
             gpad_ref, vpad_ref, dgpad_ref, dvpad_ref):
        for pad_ref, x_ref in ((gpad_ref, g_ref), (vpad_ref, v_ref)):
            pad_ref[0:SUBLANES, :] = jnp.zeros((SUBLANES, LANES), F32)
            pad_ref[SUBLANES:, :] = x_ref[...]
        dgpad_ref[s:, :] = jnp.zeros((SUBLANES, LANES), F32)
        dvpad_ref[s:, :] = jnp.zeros((SUBLANES, LANES), F32)
        dbg = jnp.zeros((1, LANES), F32)
        dbv = jnp.zeros((1, LANES), F32)
        for r0 in range(0, s, ROW_TILE):
            rows = min(ROW_TILE, s - r0)
            ug = _conv_fwd_tile(gpad_ref, wg_ref, bg_ref, r0, rows, FFN_CONV)
            uv = _conv_fwd_tile(vpad_ref, wv_ref, bv_ref, r0, rows, FFN_CONV)
            _, vjp = jax.vjp(_gelu_mul, ug, uv)
            d_ug, d_uv = vjp(dh_ref[r0:r0 + rows, :])
            dgpad_ref[r0:r0 + rows, :] = d_ug
            dvpad_ref[r0:r0 + rows, :] = d_uv
            dbg = dbg + jnp.sum(d_ug, axis=0, keepdims=True)
            dbv = dbv + jnp.sum(d_uv, axis=0, keepdims=True)
        dbg_ref[...] = dbg
        dbv_ref[...] = dbv
        for pad_ref, dpad_ref, w_ref, dx_ref, dw_ref in ((gpad_ref, dgpad_ref, wg_ref, dug_ref, dwg_ref),
                                                         (vpad_ref, dvpad_ref, wv_ref, duv_ref, dwv_ref)):
            dws = [jnp.zeros((1, LANES), F32) for _ in range(FFN_CONV)]
            for r0 in range(0, s, ROW_TILE):
                rows = min(ROW_TILE, s - r0)
                dx_ref[r0:r0 + rows, :] = _conv_bwd_tile(dpad_ref, w_ref, r0, rows, FFN_CONV)
                d_pre = dpad_ref[r0:r0 + rows, :]
                for j in range(FFN_CONV):
                    xs = pad_ref[pl.ds(SUBLANES - (FFN_CONV - 1 - j) + r0, rows), :]
                    dws[j] = dws[j] + jnp.sum(d_pre * xs, axis=0, keepdims=True)
            for j in range(FFN_CONV):
                dw_ref[j:j + 1, :] = dws[j]

    col = lambda off: (lambda j: (0, off + j))
    seq = pl.BlockSpec((s, LANES), col(0))
    return pl.pallas_call(
        body, name="ffn_conv_bwd", grid=(nblk,),
        in_specs=[pl.BlockSpec((s, LANES), col(0)), pl.BlockSpec((s, LANES), col(nblk)),
                  pl.BlockSpec((FFN_CONV, LANES), col(0)), pl.BlockSpec((FFN_CONV, LANES), col(nblk)),
                  pl.BlockSpec((1, LANES), col(0)), pl.BlockSpec((1, LANES), col(nblk)), seq],
        out_specs=[seq, seq, pl.BlockSpec((FFN_CONV, LANES), col(0)), pl.BlockSpec((FFN_CONV, LANES), col(0)),
                   pl.BlockSpec((1, LANES), col(0)), pl.BlockSpec((1, LANES), col(0))],
        out_shape=[jax.ShapeDtypeStruct((s, D_FF), F32), jax.ShapeDtypeStruct((s, D_FF), F32),
                   jax.ShapeDtypeStruct((FFN_CONV, D_FF), F32), jax.ShapeDtypeStruct((FFN_CONV, D_FF), F32),
                   jax.ShapeDtypeStruct((1, D_FF), F32), jax.ShapeDtypeStruct((1, D_FF), F32)],
        scratch_shapes=[pltpu.VMEM((s + SUBLANES, LANES), F32) for _ in range(4)],
        compiler_params=_params(("parallel",)),
    )(u, u, conv_w, conv_w, conv_b, conv_b, d_h)


def _chunk_masks(c):
    row = lax.broadcasted_iota(jnp.int32, (c, c), 0)
    col = lax.broadcasted_iota(jnp.int32, (c, c), 1)
    return row, col


def _hg_step(hq, hf, hi, hgate, l0, l1, nw, st):
    c = hq.shape[0]
    row, col = _chunk_masks(c)
    mask = col <= row
    mx = lax.stop_gradient(jnp.maximum(l0, l1))
    e0 = jnp.exp(l0 - mx)
    e1 = jnp.exp(l1 - mx)
    lb = e0 / (e0 + e1)
    sig = jax.nn.sigmoid(hf)
    lf = jnp.log(lb + (1.0 - lb) * sig)
    k = (1.0 - lb) * jax.nn.sigmoid(-hf)
    q = jax.nn.silu(hq)
    b = _dg(mask.astype(F32), lf, 1, 0, HIGHEST)
    rid = lax.broadcasted_iota(jnp.int32, b.shape, 0)
    b_ref = jnp.sum(jnp.where(rid == c // 2 - 1, b, 0.0), axis=0, keepdims=True)
    b_last = jnp.sum(jnp.where(rid == c - 1, b, 0.0), axis=0, keepdims=True)
    attn = _nt(q * jnp.exp(b - b_ref), k * jnp.exp(b_ref - b))
    attn = jnp.where(mask, attn, 0.0)
    o = _nn(attn, hi) + _nt(q * jnp.exp(b), st)
    st_new = jnp.exp(b_last) * st + _tn(hi, k * jnp.exp(b_last - b))
    y = o * lax.rsqrt(jnp.mean(o * o, axis=-1, keepdims=True) + LN_EPS) * nw * jax.nn.silu(hgate)
    return y, st_new


def _hgrn2_fwd(proj, logits, norm_w):
    s = proj.shape[0]
    nc = s // CHUNK

    def body(hq_ref, hf_ref, hi_ref, hg_ref, lg_ref, nw_ref, y_ref, st_out_ref, st_scr):
        @pl.when(pl.program_id(1) == 0)
        def _():
            st_scr[...] = jnp.zeros_like(st_scr)

        st = st_scr[...]
        st_out_ref[...] = st
        y, st_new = _hg_step(hq_ref[...], hf_ref[...], hi_ref[...], hg_ref[...],
                             lg_ref[0:1, :], lg_ref[1:2, :], nw_ref[...], st)
        y_ref[...] = y
        st_scr[...] = st_new

    seg = lambda off: pl.BlockSpec((CHUNK, D_HEAD), lambda h, c: (c, off + h))
    return pl.pallas_call(
        body, name="hgrn2_fwd", grid=(N_HEADS, nc),
        in_specs=[seg(SEG_HQ), seg(SEG_HF), seg(SEG_HI), seg(SEG_HG),
                  pl.BlockSpec((2, D_HEAD), lambda h, c: (0, h)),
                  pl.BlockSpec((1, D_HEAD), lambda h, c: (0, h))],
        out_specs=[pl.BlockSpec((CHUNK, D_HEAD), lambda h, c: (c, h)),
                   pl.BlockSpec((None, None, D_HEAD, D_HEAD), lambda h, c: (c, h, 0, 0))],
        out_shape=[jax.ShapeDtypeStruct((s, D_GROUP), F32),
                   jax.ShapeDtypeStruct((nc, N_HEADS, D_HEAD, D_HEAD), F32)],
        scratch_shapes=[pltpu.VMEM((D_HEAD, D_HEAD), F32)],
        compiler_params=_params(("arbitrary", "arbitrary")),
    )(proj, proj, proj, proj, logits, norm_w)


def _hgrn2_bwd(proj, logits, norm_w, states, d_y):
    s = proj.shape[0]
    nc = s // CHUNK

    def body(hq_ref, hf_ref, hi_ref, hg_ref, lg_ref, nw_ref, st_ref, dy_ref,
             dq_ref, df_ref, di_ref, dg_ref, dl_ref, dnw_ref, dst_scr):
        @pl.when(pl.program_id(1) == 0)
        def _():
            dst_scr[...] = jnp.zeros_like(dst_scr)
            dl_ref[...] = jnp.zeros_like(dl_ref)
            dnw_ref[...] = jnp.zeros_like(dnw_ref)

        _, vjp = jax.vjp(_hg_step, hq_ref[...], hf_ref[...], hi_ref[...], hg_ref[...],
                         lg_ref[0:1, :], lg_ref[1:2, :], nw_ref[...], st_ref[...])
        d_hq, d_hf, d_hi, d_hg, d_l0, d_l1, d_nw, d_st = vjp((dy_ref[...], dst_scr[...]))
        dq_ref[...] = d_hq
        df_ref[...] = d_hf
        di_ref[...] = d_hi
        dg_ref[...] = d_hg
        dl_ref[0:1, :] += d_l0
        dl_ref[1:2, :] += d_l1
        dnw_ref[...] += d_nw
        dst_scr[...] = d_st

    rev = lambda c: nc - 1 - c
    seg = lambda off: pl.BlockSpec((CHUNK, D_HEAD), lambda h, c: (rev(c), off + h))
    out_seg = pl.BlockSpec((CHUNK, D_HEAD), lambda h, c: (rev(c), h))
    act = jax.ShapeDtypeStruct((s, D_GROUP), F32)
    return pl.pallas_call(
        body, name="hgrn2_bwd", grid=(N_HEADS, nc),
        in_specs=[seg(SEG_HQ), seg(SEG_HF), seg(SEG_HI), seg(SEG_HG),
                  pl.BlockSpec((2, D_HEAD), lambda h, c: (0, h)),
                  pl.BlockSpec((1, D_HEAD), lambda h, c: (0, h)),
                  pl.BlockSpec((None, None, D_HEAD, D_HEAD), lambda h, c: (rev(c), h, 0, 0)),
                  out_seg],
        out_specs=[out_seg, out_seg, out_seg, out_seg,
                   pl.BlockSpec((2, D_HEAD), lambda h, c: (0, h)),
                   pl.BlockSpec((1, D_HEAD), lambda h, c: (0, h))],
        out_shape=[act, act, act, act, jax.ShapeDtypeStruct((2, D_GROUP), F32),
                   jax.ShapeDtypeStruct((1, D_GROUP), F32)],
        scratch_shapes=[pltpu.VMEM((D_HEAD, D_HEAD), F32)],
        compiler_params=_params(("arbitrary", "arbitrary")),
    )(proj, proj, proj, proj, logits, norm_w, states, d_y)


def _ml_step(qc, kc, v, mo, ig, fr, nw, ct, n, m):
    c = qc.shape[0]
    row, col = _chunk_masks(c)
    mask = col <= row
    eye = col == row
    q = qc * (D_HEAD ** -0.5)
    lf = jax.nn.log_sigmoid(fr)
    to_row = lambda t: jnp.sum(jnp.where(eye, t, 0.0), axis=0, keepdims=True)
    lf_row = to_row(lf)
    ig_row = to_row(ig)
    b_col = jnp.sum(jnp.where(mask, lf_row, 0.0), axis=1, keepdims=True)
    b_row = jnp.sum(jnp.where(row <= col, lf, 0.0), axis=0, keepdims=True)
    g = jnp.sum(lf, axis=0, keepdims=True)
    d = jnp.where(mask, b_col - b_row + ig_row, -jnp.inf)
    inter = b_col + m
    m_t = lax.stop_gradient(jnp.maximum(inter, jnp.max(d, axis=1, keepdims=True)))
    w = jnp.exp(d - m_t)
    sc = _nt(q, kc) * w
    w_inter = jnp.exp(inter - m_t)
    num = _nn(sc, v) + w_inter * _nt(q, ct)
    den = jnp.sum(sc, axis=1, keepdims=True) + w_inter * jnp.sum(q * n, axis=1, keepdims=True)
    h = num / jnp.maximum(jnp.abs(den), jnp.exp(-m_t))
    a = g - b_col + ig
    m_new = lax.stop_gradient(jnp.maximum(g + m, jnp.max(a, axis=0, keepdims=True)))
    decay = jnp.exp(g + m - m_new)
    wk = kc * jnp.exp(a - m_new)
    ct_new = decay * ct + _tn(v, wk)
    n_new = decay * n + jnp.sum(wk, axis=0, keepdims=True)
    mu = jnp.mean(h, axis=-1, keepdims=True)
    var = jnp.mean(jnp.square(h - mu), axis=-1, keepdims=True)
    y = jax.nn.sigmoid(mo) * ((h - mu) * lax.rsqrt(var + LN_EPS) * nw)
    return y, ct_new, n_new, m_new


def _mlstm_fwd(qk, proj, gate_i, gate_f, norm_w):
    s = proj.shape[0]
    nc = s // CHUNK

    def body(q_ref, k_ref, v_ref, o_ref, gi_ref, gf_ref, nw_ref,
             y_ref, ct_out, n_out, m_out, ct_scr, n_scr, m_scr):
        @pl.when(pl.program_id(1) == 0)
        def _():
            ct_scr[...] = jnp.zeros_like(ct_scr)
            n_scr[...] = jnp.zeros_like(n_scr)
            m_scr[...] = jnp.full(m_scr.shape, NEG_BIG, F32)

        ct, n, m = ct_scr[...], n_scr[...], m_scr[...]
        ct_out[...] = ct
        n_out[...] = n
        m_out[...] = m
        y, ct_new, n_new, m_new = _ml_step(q_ref[...], k_ref[...], v_ref[...], o_ref[...],
                                           gi_ref[...], gf_ref[...], nw_ref[...], ct, n, m)
        y_ref[...] = y
        ct_scr[...] = ct_new
        n_scr[...] = n_new
        m_scr[...] = m_new

    blk = lambda off: pl.BlockSpec((CHUNK, D_HEAD), lambda h, c: (c, off + h))
    gate = pl.BlockSpec((None, CHUNK, 1), lambda h, c: (h, c, 0))
    st = lambda r, w: pl.BlockSpec((None, None, r, w), lambda h, c: (c, h, 0, 0))
    return pl.pallas_call(
        body, name="mlstm_fwd", grid=(N_HEADS, nc),
        in_specs=[blk(0), blk(N_HEADS), blk(SEG_MV), blk(SEG_MO), gate, gate,
                  pl.BlockSpec((1, D_HEAD), lambda h, c: (0, h))],
        out_specs=[blk(0), st(D_HEAD, D_HEAD), st(1, D_HEAD), st(1, 1)],
        out_shape=[jax.ShapeDtypeStruct((s, D_GROUP), F32),
                   jax.ShapeDtypeStruct((nc, N_HEADS, D_HEAD, D_HEAD), F32),
                   jax.ShapeDtypeStruct((nc, N_HEADS, 1, D_HEAD), F32),
                   jax.ShapeDtypeStruct((nc, N_HEADS, 1, 1), F32)],
        scratch_shapes=[pltpu.VMEM((D_HEAD, D_HEAD), F32), pltpu.VMEM((1, D_HEAD), F32),
                        pltpu.VMEM((1, 1), F32)],
        compiler_params=_params(("arbitrary", "arbitrary")),
    )(qk, qk, proj, proj, gate_i, gate_f, norm_w)


def _mlstm_bwd(qk, proj, gate_i, gate_f, norm_w, ct_s, n_s, m_s, d_y):
    s = proj.shape[0]
    nc = s // CHUNK

    def body(q_ref, k_ref, v_ref, o_ref, gi_ref, gf_ref, nw_ref, ct_ref, n_ref, m_ref, dy_ref,
             dq_ref, dk_ref, dv_ref, do_ref, dgi_ref, dgf_ref, dnw_ref, dct_scr, dn_scr):
        @pl.when(pl.program_id(1) == 0)
        def _():
            dct_scr[...] = jnp.zeros_like(dct_scr)
            dn_scr[...] = jnp.zeros_like(dn_scr)
            dnw_ref[...] = jnp.zeros_like(dnw_ref)

        m = m_ref[...]
        step = lambda *a: _ml_step(*a, m)[:3]
        _, vjp = jax.vjp(step, q_ref[...], k_ref[...], v_ref[...], o_ref[...],
                         gi_ref[...], gf_ref[...], nw_ref[...], ct_ref[...], n_ref[...])
        d_q, d_k, d_v, d_o, d_gi, d_gf, d_nw, d_ct, d_n = vjp((dy_ref[...], dct_scr[...], dn_scr[...]))
        dq_ref[...] = d_q
        dk_ref[...] = d_k
        dv_ref[...] = d_v
        do_ref[...] = d_o
        dgi_ref[...] = d_gi
        dgf_ref[...] = d_gf
        dnw_ref[...] += d_nw
        dct_scr[...] = d_ct
        dn_scr[...] = d_n

    rev = lambda c: nc - 1 - c
    blk = lambda off: pl.BlockSpec((CHUNK, D_HEAD), lambda h, c: (rev(c), off + h))
    gate = pl.BlockSpec((None, CHUNK, 1), lambda h, c: (h, rev(c), 0))
    st = lambda r, w: pl.BlockSpec((None, None, r, w), lambda h, c: (rev(c), h, 0, 0))
    act = jax.ShapeDtypeStruct((s, D_GROUP), F32)
    gshape = jax.ShapeDtypeStruct((N_HEADS, s, 1), F32)
    return pl.pallas_call(
        body, name="mlstm_bwd", grid=(N_HEADS, nc),
        in_specs=[blk(0), blk(N_HEADS), blk(SEG_MV), blk(SEG_MO), gate, gate,
                  pl.BlockSpec((1, D_HEAD), lambda h, c: (0, h)),
                  st(D_HEAD, D_HEAD), st(1, D_HEAD), st(1, 1), blk(0)],
        out_specs=[blk(0), blk(0), blk(0), blk(0), gate, gate,
                   pl.BlockSpec((1, D_HEAD), lambda h, c: (0, h))],
        out_shape=[act, act, act, act, gshape, gshape, jax.ShapeDtypeStruct((1, D_GROUP), F32)],
        scratch_shapes=[pltpu.VMEM((D_HEAD, D_HEAD), F32), pltpu.VMEM((1, D_HEAD), F32)],
        compiler_params=_params(("arbitrary", "arbitrary")),
    )(qk, qk, proj, proj, gate_i, gate_f, norm_w, ct_s, n_s, m_s, d_y)


LN_TOKENS = 512
ATT_TOKENS = 256


def _res_ln_fwd(xres, branch, g, b, name):
    s, dm = xres.shape
    tb = min(LN_TOKENS, s)

    def body(x_ref, br_ref, g_ref, b_ref, o_ref):
        o_ref[...] = _layer_norm(ALPHA * x_ref[...] + br_ref[...], g_ref[...], b_ref[...])

    tok = pl.BlockSpec((tb, dm), lambda i: (i, 0))
    vec = pl.BlockSpec((1, dm), lambda i: (0, 0))
    return pl.pallas_call(
        body, name=name, grid=(s // tb,), in_specs=[tok, tok, vec, vec], out_specs=tok,
        out_shape=jax.ShapeDtypeStruct((s, dm), F32), compiler_params=_params(("parallel",)),
    )(xres, branch, g, b)


def _res_ln_bwd(xres, branch, g, b, d_out, name):
    s, dm = xres.shape
    tb = min(LN_TOKENS, s)

    def body(x_ref, br_ref, g_ref, b_ref, do_ref, dz_ref, dg_ref, db_ref):
        @pl.when(pl.program_id(0) == 0)
        def _():
            dg_ref[...] = jnp.zeros_like(dg_ref)
            db_ref[...] = jnp.zeros_like(db_ref)

        z = ALPHA * x_ref[...] + br_ref[...]
        _, vjp = jax.vjp(_layer_norm, z, g_ref[...], b_ref[...])
        d_z, d_g, d_b = vjp(do_ref[...])
        dz_ref[...] = d_z
        dg_ref[...] += d_g
        db_ref[...] += d_b

    tok = pl.BlockSpec((tb, dm), lambda i: (i, 0))
    vec = pl.BlockSpec((1, dm), lambda i: (0, 0))
    return pl.pallas_call(
        body, name=name, grid=(s // tb,), in_specs=[tok, tok, vec, vec, tok], out_specs=[tok, vec, vec],
        out_shape=[jax.ShapeDtypeStruct((s, dm), F32), jax.ShapeDtypeStruct((1, dm), F32),
                   jax.ShapeDtypeStruct((1, dm), F32)],
        compiler_params=_params(("arbitrary",)),
    )(xres, branch, g, b, d_out)


def _loss_tail(xres, branch, g, b, target):
    s, dm = xres.shape
    tb = min(LN_TOKENS, s)

    def loss_fn(z, gg, bb, tgt):
        err = jnp.square(_layer_norm(z, gg, bb) - tgt)
        return 0.5 * jnp.sum(jnp.mean(err, axis=-1, keepdims=True), axis=0, keepdims=True)

    def body(x_ref, br_ref, g_ref, b_ref, t_ref, loss_ref, dz_ref, dg_ref, db_ref):
        @pl.when(pl.program_id(0) == 0)
        def _():
            loss_ref[...] = jnp.zeros_like(loss_ref)
            dg_ref[...] = jnp.zeros_like(dg_ref)
            db_ref[...] = jnp.zeros_like(db_ref)

        z = ALPHA * x_ref[...] + br_ref[...]
        tgt = t_ref[...]
        loss, vjp = jax.vjp(lambda zz, gg, bb: loss_fn(zz, gg, bb, tgt), z, g_ref[...], b_ref[...])
        d_z, d_g, d_b = vjp(jnp.ones((1, 1), F32))
        loss_ref[...] += loss
        dz_ref[...] = d_z
        dg_ref[...] += d_g
        db_ref[...] += d_b

    tok = pl.BlockSpec((tb, dm), lambda i: (i, 0))
    vec = pl.BlockSpec((1, dm), lambda i: (0, 0))
    one = pl.BlockSpec((1, 1), lambda i: (0, 0))
    return pl.pallas_call(
        body, name="loss_tail", grid=(s // tb,), in_specs=[tok, tok, vec, vec, tok],
        out_specs=[one, tok, vec, vec],
        out_shape=[jax.ShapeDtypeStruct((1, 1), F32), jax.ShapeDtypeStruct((s, dm), F32),
                   jax.ShapeDtypeStruct((1, dm), F32), jax.ShapeDtypeStruct((1, dm), F32)],
        compiler_params=_params(("arbitrary",)),
    )(xres, branch, g, b, target)


def _att_head(q, k, v):
    sc = _nt(q, k) * (CA_DH ** -0.5)
    return _nn(jax.nn.softmax(sc, axis=-1), v)


def _att_fwd(q, kv):
    s = q.shape[0]
    tb = min(ATT_TOKENS, s)

    def body(q_ref, kv_ref, o_ref):
        for h in range(CA_HEADS):
            lo = h * CA_DH
            o_ref[:, lo:lo + CA_DH] = _att_head(q_ref[:, lo:lo + CA_DH], kv_ref[:, lo:lo + CA_DH],
                                                kv_ref[:, D_MODEL + lo:D_MODEL + lo + CA_DH])

    tok = pl.BlockSpec((tb, D_MODEL), lambda i: (i, 0))
    return pl.pallas_call(
        body, name="att_fwd", grid=(s // tb,),
        in_specs=[tok, pl.BlockSpec((N_MEM, 2 * D_MODEL), lambda i: (0, 0))], out_specs=tok,
        out_shape=jax.ShapeDtypeStruct((s, D_MODEL), F32), compiler_params=_params(("parallel",)),
    )(q, kv)


def _att_bwd(q, kv, d_o):
    s = q.shape[0]
    tb = min(ATT_TOKENS, s)

    def body(q_ref, kv_ref, do_ref, dq_ref, dkv_ref):
        @pl.when(pl.program_id(0) == 0)
        def _():
            dkv_ref[...] = jnp.zeros_like(dkv_ref)

        for h in range(CA_HEADS):
            lo = h * CA_DH
            vlo = D_MODEL + lo
            _, vjp = jax.vjp(_att_head, q_ref[:, lo:lo + CA_DH], kv_ref[:, lo:lo + CA_DH],
                             kv_ref[:, vlo:vlo + CA_DH])
            d_q, d_k, d_v = vjp(do_ref[:, lo:lo + CA_DH])
            dq_ref[:, lo:lo + CA_DH] = d_q
            dkv_ref[:, lo:lo + CA_DH] += d_k
            dkv_ref[:, vlo:vlo + CA_DH] += d_v

    tok = pl.BlockSpec((tb, D_MODEL), lambda i: (i, 0))
    mem = pl.BlockSpec((N_MEM, 2 * D_MODEL), lambda i: (0, 0))
    return pl.pallas_call(
        body, name="att_bwd", grid=(s // tb,), in_specs=[tok, mem, tok], out_specs=[tok, mem],
        out_shape=[jax.ShapeDtypeStruct((s, D_MODEL), F32), jax.ShapeDtypeStruct((N_MEM, 2 * D_MODEL), F32)],
        compiler_params=_params(("arbitrary",)),
    )(q, kv, d_o)


def _gates_to_heads(col):
    return jnp.transpose(col)[:, :, None]


def _local_step(x, mem, target, w):
    s = x.shape[0]
    tm = min(512, s)
    proj = _matmul_nn(x, w["w_in"], w["b_in"], tm, 384, "proj")
    gates = proj[:, D_IN_MAIN:D_IN_MAIN + 2 * N_HEADS]
    gate_i = _gates_to_heads(gates[:, :N_HEADS])
    gate_f = _gates_to_heads(gates[:, N_HEADS:])
    qk = _ml_conv_fwd(proj, w["ml_conv_w"], w["ml_conv_b"])
    y_hg, hg_states = _hgrn2_fwd(proj, w["hg_lb_logits"], w["hg_norm_w"])
    y_ml, ct_s, n_s, m_s = _mlstm_fwd(qk, proj, gate_i, gate_f, w["ml_norm_w"])
    y = jnp.concatenate([y_hg, y_ml], axis=-1)
    mix = _matmul_nn(y, w["w_out"], None, tm, D_MODEL, "mix")
    x1 = _res_ln_fwd(x, mix, w["ln1_g"], w["ln1_b"], "ln1_fwd")
    kv = _matmul_nn(mem, w["ca_wkv"], None, N_MEM, 1024, "kv")
    q = _matmul_nn(x1, w["ca_wq"], None, tm, D_MODEL, "ca_q")
    att = _att_fwd(q, kv)
    ca = _matmul_nn(att, w["ca_wo"], None, tm, D_MODEL, "ca_out")
    x2 = _res_ln_fwd(x1, ca, w["ln2_g"], w["ln2_b"], "ln2_fwd")
    u = _matmul_nn(x2, w["ffn_w_up"], None, tm, 512, "ffn_up")
    hid = _ffn_conv_fwd(u, w["ffn_conv_w"], w["ffn_conv_b"])
    ff = _matmul_nn(hid, w["ffn_w_down"], None, tm, D_MODEL, "ffn_down")
    loss, d_z3, d_ln3_g, d_ln3_b = _loss_tail(x2, ff, w["ln3_g"], w["ln3_b"], target)
    grads = {"ln3_g": d_ln3_g, "ln3_b": d_ln3_b}
    tt = min(512, s)
    grads["ffn_w_down"] = _matmul_tn(hid, d_z3, 1408, D_MODEL, tt, "d_w_down")
    d_hid = _matmul_nt(d_z3, w["ffn_w_down"], None, 1.0, tm, 1408, "d_hid")
    d_ug, d_uv, d_cwg, d_cwv, d_cbg, d_cbv = _ffn_conv_bwd(u, w["ffn_conv_w"], w["ffn_conv_b"], d_hid)
    grads["ffn_conv_w"] = jnp.concatenate([d_cwg, d_cwv], axis=-1)
    grads["ffn_conv_b"] = jnp.concatenate([d_cbg, d_cbv], axis=-1)
    grads["ffn_w_up"] = jnp.concatenate([_matmul_tn(x2, d_ug, D_MODEL, 1408, tt, "d_w_up_gate"),
                                         _matmul_tn(x2, d_uv, D_MODEL, 1408, tt, "d_w_up_val")], axis=-1)
    d_x2 = _matmul_nt(d_ug, w["ffn_w_up"][:, :D_FF], d_z3, ALPHA, tm, 512, "d_x2_gate")
    d_x2 = _matmul_nt(d_uv, w["ffn_w_up"][:, D_FF:], d_x2, 1.0, tm, 512, "d_x2_val")
    d_z2, grads["ln2_g"], grads["ln2_b"] = _res_ln_bwd(x1, ca, w["ln2_g"], w["ln2_b"], d_x2, "ln2_bwd")
    grads["ca_wo"] = _matmul_tn(att, d_z2, D_MODEL, D_MODEL, tt, "d_ca_wo")
    d_att = _matmul_nt(d_z2, w["ca_wo"], None, 1.0, tm, D_MODEL, "d_att")
    d_q, d_kv = _att_bwd(q, kv, d_att)
    grads["ca_wq"] = _matmul_tn(x1, d_q, D_MODEL, D_MODEL, tt, "d_ca_wq")
    grads["ca_wkv"] = _matmul_tn(mem, d_kv, D_MODEL, 1024, N_MEM, "d_ca_wkv")
    d_x1 = _matmul_nt(d_q, w["ca_wq"], d_z2, ALPHA, tm, D_MODEL, "d_x1")
    d_z1, grads["ln1_g"], grads["ln1_b"] = _res_ln_bwd(x, mix, w["ln1_g"], w["ln1_b"], d_x1, "ln1_bwd")
    grads["w_out"] = _matmul_tn(y, d_z1, D_MODEL, D_MODEL, tt, "d_w_out")
    d_y = _matmul_nt(d_z1, w["w_out"], None, 1.0, tm, D_MODEL, "d_y")
    d_hq, d_hf, d_hi, d_hgate, grads["hg_lb_logits"], grads["hg_norm_w"] = _hgrn2_bwd(
        proj, w["hg_lb_logits"], w["hg_norm_w"], hg_states, d_y[:, :D_GROUP])
    d_qc, d_kc, d_mv, d_mo, d_gi, d_gf, grads["ml_norm_w"] = _mlstm_bwd(
        qk, proj, gate_i, gate_f, w["ml_norm_w"], ct_s, n_s, m_s, d_y[:, D_GROUP:])
    d_mqk, grads["ml_conv_w"], grads["ml_conv_b"] = _ml_conv_bwd(
        proj, w["ml_conv_w"], w["ml_conv_b"], jnp.concatenate([d_qc, d_kc], axis=-1))
    d_gates = jnp.concatenate([jnp.transpose(d_gi[:, :, 0]), jnp.transpose(d_gf[:, :, 0]),
                               jnp.zeros((s, LANES - 2 * N_HEADS), F32)], axis=-1)
    d_proj = jnp.concatenate([d_hq, d_hf, d_hi, d_hgate, d_mqk, d_mv, d_mo, d_gates], axis=-1)
    grads["w_in"] = _matmul_tn(x, d_proj, D_MODEL, 384, tt, "d_w_in")
    grads["b_in"] = _colsum(d_proj, 384, "d_b_in")
    grad_x = _matmul_nt(d_proj, w["w_in"], d_z1, ALPHA, tm, D_MODEL, "d_x")
    return loss, grad_x, grads


HBM_SPEC = pl.BlockSpec(memory_space=pltpu.HBM)


def _coords():
    return lax.axis_index("x"), lax.axis_index("y"), lax.axis_index("c")


def _other_chips(x, y):
    return [(1 - x, y), (x, 1 - y), (1 - x, 1 - y)]


def _all_gather_two_level(shard, name):
    r, n = shard.shape

    def body(x_ref, out_ref, send_sems, recv_sems, local_sem):
        x, y, c = _coords()
        me, sibling = (x, y, c), (x, y, 1 - c)
        chips = _other_chips(x, y)

        def slot(px, py, pc):
            return out_ref.at[4 * px + 2 * py + pc]

        def copy(k, block, to, src=None):
            return pltpu.make_async_remote_copy(
                src_ref=slot(*block) if src is None else src, dst_ref=slot(*block),
                send_sem=send_sems.at[k], recv_sem=recv_sems.at[k], device_id=to, device_id_type=MESH)

        mine = pltpu.make_async_copy(x_ref, slot(*me), local_sem)
        mine.start()
        first = [copy(0, me, sibling, src=x_ref)]
        first += [copy(1 + j, me, (*chip, c), src=x_ref) for j, chip in enumerate(chips)]
        for cp in first:
            cp.start()
        passed = [copy(4 + j, (*chip, c), sibling) for j, chip in enumerate(chips)]
        for j, chip in enumerate(chips):
            copy(1 + j, (*chip, c), me).wait_recv()
            passed[j].start()
        copy(0, sibling, me).wait_recv()
        for j, chip in enumerate(chips):
            copy(4 + j, (*chip, 1 - c), me).wait_recv()
        for cp in first + passed:
            cp.wait_send()
        mine.wait()

    return pl.pallas_call(
        body, name=name, out_shape=jax.ShapeDtypeStruct((N_DEV, r, n), shard.dtype),
        in_specs=[HBM_SPEC], out_specs=HBM_SPEC,
        scratch_shapes=[pltpu.SemaphoreType.DMA((7,)), pltpu.SemaphoreType.DMA((7,)), pltpu.SemaphoreType.DMA],
    )(shard)


def _all_gather_direct(vec, name):
    r, n = vec.shape

    def body(x_ref, out_ref, send_sems, recv_sems, local_sem):
        x, y, c = _coords()
        flip = lambda v, bit: 1 - v if bit else v
        me = 4 * x + 2 * y + c
        mine = pltpu.make_async_copy(x_ref, out_ref.at[me], local_sem)
        mine.start()
        copies = []
        for d in range(1, N_DEV):
            peer = (flip(x, d & 4), flip(y, d & 2), flip(c, d & 1))
            peer_slot = 4 * peer[0] + 2 * peer[1] + peer[2]
            out_going = pltpu.make_async_remote_copy(
                src_ref=x_ref, dst_ref=out_ref.at[me], send_sem=send_sems.at[d - 1],
                recv_sem=recv_sems.at[d - 1], device_id=peer, device_id_type=MESH)
            incoming = pltpu.make_async_remote_copy(
                src_ref=x_ref, dst_ref=out_ref.at[peer_slot], send_sem=send_sems.at[d - 1],
                recv_sem=recv_sems.at[d - 1], device_id=peer, device_id_type=MESH)
            out_going.start()
            copies.append((out_going, incoming))
        for out_going, incoming in copies:
            incoming.wait_recv()
            out_going.wait_send()
        mine.wait()

    return pl.pallas_call(
        body, name=name, out_shape=jax.ShapeDtypeStruct((N_DEV, r, n), vec.dtype),
        in_specs=[HBM_SPEC], out_specs=HBM_SPEC,
        scratch_shapes=[pltpu.SemaphoreType.DMA((7,)), pltpu.SemaphoreType.DMA((7,)), pltpu.SemaphoreType.DMA],
    )(vec)


def _exchange_with_sibling(parts):
    _, nchip, r, n = parts.shape

    def body(p_ref, own_ref, got_ref, send_sem, recv_sem, local_sem):
        x, y, c = _coords()
        own = pltpu.make_async_copy(p_ref.at[c], own_ref, local_sem)
        own.start()
        swap = pltpu.make_async_remote_copy(
            src_ref=p_ref.at[1 - c], dst_ref=got_ref, send_sem=send_sem, recv_sem=recv_sem,
            device_id=(x, y, 1 - c), device_id_type=MESH)
        swap.start()
        swap.wait()
        own.wait()

    half = jax.ShapeDtypeStruct((nchip, r, n), parts.dtype)
    return pl.pallas_call(
        body, name="grad_exchange_sibling", out_shape=[half, half],
        in_specs=[HBM_SPEC], out_specs=[HBM_SPEC, HBM_SPEC],
        scratch_shapes=[pltpu.SemaphoreType.DMA, pltpu.SemaphoreType.DMA, pltpu.SemaphoreType.DMA],
    )(parts)


def _exchange_with_chips(sums):
    nchip, r, n = sums.shape

    def body(a_ref, out_ref, send_sems, recv_sems, local_sem):
        x, y, c = _coords()
        own = pltpu.make_async_copy(a_ref.at[2 * x + y], out_ref.at[nchip - 1], local_sem)
        own.start()
        copies = []
        for j, (cx, cy) in enumerate(_other_chips(x, y)):
            cp = pltpu.make_async_remote_copy(
                src_ref=a_ref.at[2 * cx + cy], dst_ref=out_ref.at[j], send_sem=send_sems.at[j],
                recv_sem=recv_sems.at[j], device_id=(cx, cy, c), device_id_type=MESH)
            cp.start()
            copies.append(cp)
        for cp in copies:
            cp.wait()
        own.wait()

    return pl.pallas_call(
        body, name="grad_exchange_chips", out_shape=jax.ShapeDtypeStruct((nchip, r, n), sums.dtype),
        in_specs=[HBM_SPEC], out_specs=HBM_SPEC,
        scratch_shapes=[pltpu.SemaphoreType.DMA((3,)), pltpu.SemaphoreType.DMA((3,)), pltpu.SemaphoreType.DMA],
    )(sums)


def _add_pairs(a, b, tr):
    p, r, n = a.shape

    def body(a_ref, b_ref, o_ref):
        o_ref[...] = a_ref[...] + b_ref[...]

    blk = pl.BlockSpec((None, tr, n), lambda i, j: (i, j, 0))
    return pl.pallas_call(
        body, name="grad_add_sibling", grid=(p, r // tr), in_specs=[blk, blk], out_specs=blk,
        out_shape=jax.ShapeDtypeStruct(a.shape, F32), compiler_params=_params(("parallel", "parallel")),
    )(a, b)


def _adamw(parts, w, m, v, tr, name):
    p, r, n = parts.shape

    def body(p_ref, w_ref, m_ref, v_ref, g_ref, d_ref, nm_ref, nv_ref):
        g = p_ref[0]
        for i in range(1, p):
            g = g + p_ref[i]
        m_new = ADAM_B1 * m_ref[...] + (1.0 - ADAM_B1) * g
        v_new = ADAM_B2 * v_ref[...] + (1.0 - ADAM_B2) * jnp.square(g)
        m_hat = m_new / (1.0 - ADAM_B1 ** ADAM_STEP)
        v_hat = v_new / (1.0 - ADAM_B2 ** ADAM_STEP)
        g_ref[...] = g
        d_ref[...] = -ADAM_LR * (m_hat / (jnp.sqrt(v_hat) + ADAM_EPS) + ADAM_WD * w_ref[...])
        nm_ref[...] = m_new
        nv_ref[...] = v_new

    blk = pl.BlockSpec((tr, n), lambda i: (i, 0))
    out = jax.ShapeDtypeStruct((r, n), F32)
    return pl.pallas_call(
        body, name=name, grid=(r // tr,),
        in_specs=[pl.BlockSpec((p, tr, n), lambda i: (0, i, 0)), blk, blk, blk],
        out_specs=[blk, blk, blk, blk], out_shape=[out, out, out, out],
        compiler_params=_params(("parallel",)),
    )(parts, w, m, v)


PACK_W = D_MODEL
_W_IN_ROWS = D_MODEL * (D_IN // N_DEV) // PACK_W
_CONV_ELEMS = ML_CONV * (2 * D_GROUP // N_DEV) + FFN_CONV * (D_UP // N_DEV)
SHARD_LAYOUT = (("w_in", _W_IN_ROWS, 528), ("w_out", 128, 128), ("ca_wq", 128, 128), ("ca_wo", 128, 128),
                ("ca_wkv", 256, 256), ("ffn_w_up", 704, 704), ("ffn_w_down", 352, 352), ("conv", 6, 16))
SHARD_ROWS = sum(p for _, _, p in SHARD_LAYOUT)
SHARD_TILE = 320
SMALL_NAMES = ("b_in", "hg_lb_logits", "hg_norm_w", "ml_conv_b", "ml_norm_w", "ln1_g", "ln1_b",
               "ln2_g", "ln2_b", "ffn_conv_b", "ln3_g", "ln3_b")
SMALL_ROWS = 24


def _rows(flat, padded_rows):
    flat = flat.reshape(-1)
    return jnp.pad(flat, (0, padded_rows * PACK_W - flat.shape[0])).reshape(padded_rows, PACK_W)


def _pack_shard(p, conv_rows):
    parts = []
    for name, _, padded in SHARD_LAYOUT:
        if name == "conv":
            parts.append(conv_rows)
        else:
            parts.append(_rows(p[name], padded))
    return jnp.concatenate(parts, axis=0)


def _conv_rows_f32(p):
    return _rows(jnp.concatenate([p["ml_conv_w"].reshape(-1), p["ffn_conv_w"].reshape(-1)]), 16)


def _conv_rows_bits(p):
    flat = jnp.concatenate([p["ml_conv_w"].reshape(-1), p["ffn_conv_w"].reshape(-1)])
    return _rows(lax.bitcast_convert_type(flat, BF16), 16)


def _unpack_shard(slab):
    out = {}
    r0 = 0
    shapes = {"w_in": (1, D_MODEL, D_IN // N_DEV), "w_out": (1, 128, D_MODEL), "ca_wq": (1, 128, D_MODEL),
              "ca_wo": (1, 128, D_MODEL), "ca_wkv": (1, D_MODEL, 256), "ffn_w_up": (1, D_MODEL, D_UP // N_DEV),
              "ffn_w_down": (1, D_FF // N_DEV, D_MODEL)}
    for name, rows, padded in SHARD_LAYOUT:
        if name == "conv":
            flat = slab[r0:r0 + padded].reshape(-1)
            n_ml = ML_CONV * (2 * D_GROUP // N_DEV)
            out["ml_conv_w"] = flat[:n_ml].reshape(1, ML_CONV, 2 * D_GROUP // N_DEV)
            out["ffn_conv_w"] = flat[n_ml:_CONV_ELEMS].reshape(1, FFN_CONV, D_UP // N_DEV)
        else:
            out[name] = slab[r0:r0 + rows].reshape(shapes[name])
        r0 += padded
    return out


def _whole_weights(gathered):
    w = {}
    r0 = 0
    seg = {}
    for name, rows, padded in SHARD_LAYOUT:
        seg[name] = gathered[:, r0:r0 + (padded if name == "conv" else rows)]
        r0 += padded
    cols = lambda t, k, per: jnp.transpose(t.reshape(N_DEV, k, per), (1, 0, 2)).reshape(k, N_DEV * per)
    w_in = cols(seg["w_in"], D_MODEL, D_IN // N_DEV)
    w["w_in"] = jnp.concatenate(
        [w_in[:, :D_IN_MAIN], jnp.pad(w_in[:, D_IN_MAIN:], ((0, 0), (0, D_PROJ - D_IN)))], axis=1)
    w["w_out"] = seg["w_out"].reshape(D_MODEL, D_MODEL)
    w["ca_wq"] = seg["ca_wq"].reshape(D_MODEL, D_MODEL)
    w["ca_wo"] = seg["ca_wo"].reshape(D_MODEL, D_MODEL)
    w["ca_wkv"] = cols(seg["ca_wkv"], D_MODEL, 256)
    w["ffn_w_up"] = cols(seg["ffn_w_up"], D_MODEL, D_UP // N_DEV)
    w["ffn_w_down"] = seg["ffn_w_down"].reshape(D_FF, D_MODEL)
    bits = seg["conv"].reshape(N_DEV, -1)[:, :2 * _CONV_ELEMS].reshape(N_DEV, _CONV_ELEMS, 2)
    conv = lax.bitcast_convert_type(bits, F32)
    n_ml = ML_CONV * (2 * D_GROUP // N_DEV)
    w["ml_conv_w"] = cols(conv[:, :n_ml], ML_CONV, 2 * D_GROUP // N_DEV)
    w["ffn_conv_w"] = cols(conv[:, n_ml:], FFN_CONV, D_UP // N_DEV)
    return w


def _pack_grads(grads):
    split_cols = lambda g, per: jnp.transpose(g.reshape(g.shape[0], N_DEV, per), (1, 0, 2))
    pad_rows = lambda t, padded: jnp.pad(t.reshape(N_DEV, -1), ((0, 0), (0, padded * PACK_W - t[0].size))
                                         ).reshape(N_DEV, padded, PACK_W)
    conv = jnp.concatenate([split_cols(grads["ml_conv_w"], 2 * D_GROUP // N_DEV).reshape(N_DEV, -1),
                            split_cols(grads["ffn_conv_w"], D_UP // N_DEV).reshape(N_DEV, -1)], axis=1)
    parts = [pad_rows(split_cols(grads["w_in"][:, :D_IN], D_IN // N_DEV), 528),
             grads["w_out"].reshape(N_DEV, 128, PACK_W), grads["ca_wq"].reshape(N_DEV, 128, PACK_W),
             grads["ca_wo"].reshape(N_DEV, 128, PACK_W),
             pad_rows(split_cols(grads["ca_wkv"], 256), 256),
             pad_rows(split_cols(grads["ffn_w_up"], D_UP // N_DEV), 704),
             grads["ffn_w_down"].reshape(N_DEV, 352, PACK_W), pad_rows(conv, 16)]
    slab = jnp.concatenate(parts, axis=1)
    return jnp.transpose(slab.reshape(4, 2, SHARD_ROWS, PACK_W), (1, 0, 2, 3))


def _pack_small(p, extra=None):
    flat = [p[n].reshape(-1) for n in SMALL_NAMES]
    if extra is not None:
        flat.append(extra.reshape(-1))
    return _rows(jnp.concatenate(flat), SMALL_ROWS)


def _unpack_small(slab, like):
    out = {}
    flat = slab.reshape(-1)
    o = 0
    for n in SMALL_NAMES:
        out[n] = flat[o:o + like[n].size].reshape(like[n].shape)
        o += like[n].size
    return out, flat[o]


SHARDED_NAMES = ("w_in", "ml_conv_w", "w_out", "ca_wq", "ca_wkv", "ca_wo", "ffn_w_up", "ffn_conv_w", "ffn_w_down")
WEIGHT_NAMES = ("w_in", "b_in", "hg_lb_logits", "hg_norm_w", "ml_conv_w", "ml_conv_b", "ml_norm_w", "w_out",
                "ln1_g", "ln1_b", "ca_wq", "ca_wkv", "ca_wo", "ln2_g", "ln2_b", "ffn_w_up", "ffn_conv_w",
                "ffn_conv_b", "ffn_w_down", "ln3_g", "ln3_b")


def kernel(x, mem, w_in, b_in, hg_lb_logits, hg_norm_w, ml_conv_w, ml_conv_b, ml_norm_w, w_out, ln1_g, ln1_b, ca_wq, ca_wkv, ca_wo, ln2_g, ln2_b, ffn_w_up, ffn_conv_w, ffn_conv_b, ffn_w_down, ln3_g, ln3_b, loss_target, m_w_in, m_b_in, m_hg_lb_logits, m_hg_norm_w, m_ml_conv_w, m_ml_conv_b, m_ml_norm_w, m_w_out, m_ln1_g, m_ln1_b, m_ca_wq, m_ca_wkv, m_ca_wo, m_ln2_g, m_ln2_b, m_ffn_w_up, m_ffn_conv_w, m_ffn_conv_b, m_ffn_w_down, m_ln3_g, m_ln3_b, v_w_in, v_b_in, v_hg_lb_logits, v_hg_norm_w, v_ml_conv_w, v_ml_conv_b, v_ml_norm_w, v_w_out, v_ln1_g, v_ln1_b, v_ca_wq, v_ca_wkv, v_ca_wo, v_ln2_g, v_ln2_b, v_ffn_w_up, v_ffn_conv_w, v_ffn_conv_b, v_ffn_w_down, v_ln3_g, v_ln3_b):
    params = dict(w_in=w_in, b_in=b_in, hg_lb_logits=hg_lb_logits, hg_norm_w=hg_norm_w, ml_conv_w=ml_conv_w,
                  ml_conv_b=ml_conv_b, ml_norm_w=ml_norm_w, w_out=w_out, ln1_g=ln1_g, ln1_b=ln1_b, ca_wq=ca_wq,
                  ca_wkv=ca_wkv, ca_wo=ca_wo, ln2_g=ln2_g, ln2_b=ln2_b, ffn_w_up=ffn_w_up, ffn_conv_w=ffn_conv_w,
                  ffn_conv_b=ffn_conv_b, ffn_w_down=ffn_w_down, ln3_g=ln3_g, ln3_b=ln3_b)
    mom1 = dict(w_in=m_w_in, b_in=m_b_in, hg_lb_logits=m_hg_lb_logits, hg_norm_w=m_hg_norm_w,
                ml_conv_w=m_ml_conv_w, ml_conv_b=m_ml_conv_b, ml_norm_w=m_ml_norm_w, w_out=m_w_out, ln1_g=m_ln1_g,
                ln1_b=m_ln1_b, ca_wq=m_ca_wq, ca_wkv=m_ca_wkv, ca_wo=m_ca_wo, ln2_g=m_ln2_g, ln2_b=m_ln2_b,
                ffn_w_up=m_ffn_w_up, ffn_conv_w=m_ffn_conv_w, ffn_conv_b=m_ffn_conv_b, ffn_w_down=m_ffn_w_down,
                ln3_g=m_ln3_g, ln3_b=m_ln3_b)
    mom2 = dict(w_in=v_w_in, b_in=v_b_in, hg_lb_logits=v_hg_lb_logits, hg_norm_w=v_hg_norm_w,
                ml_conv_w=v_ml_conv_w, ml_conv_b=v_ml_conv_b, ml_norm_w=v_ml_norm_w, w_out=v_w_out, ln1_g=v_ln1_g,
                ln1_b=v_ln1_b, ca_wq=v_ca_wq, ca_wkv=v_ca_wkv, ca_wo=v_ca_wo, ln2_g=v_ln2_g, ln2_b=v_ln2_b,
                ffn_w_up=v_ffn_w_up, ffn_conv_w=v_ffn_conv_w, ffn_conv_b=v_ffn_conv_b, ffn_w_down=v_ffn_w_down,
                ln3_g=v_ln3_g, ln3_b=v_ln3_b)

    matrices_bf16 = {n: params[n].astype(BF16) for n in SHARDED_NAMES if "conv" not in n}
    gathered = _all_gather_two_level(_pack_shard(matrices_bf16, _conv_rows_bits(params)), "weights_all_gather")
    whole = _whole_weights(gathered)
    for n in SMALL_NAMES:
        whole[n] = params[n]
    whole["b_in"] = jnp.pad(b_in, ((0, 0), (0, D_PROJ - D_IN)))

    loss, grad_x, grads = _local_step(x[0], mem[0], loss_target[0], whole)

    own_half, sibling_half = _exchange_with_sibling(_pack_grads(grads))
    chip_sums = _add_pairs(own_half, sibling_half, SHARD_TILE)
    shard_parts = _exchange_with_chips(chip_sums)
    f32_shard = lambda p: _pack_shard(p, _conv_rows_f32(p))
    g_sh, d_sh, m_sh, v_sh = _adamw(shard_parts, f32_shard(params), f32_shard(mom1), f32_shard(mom2),
                                    SHARD_TILE, "adamw_sharded")
    grads["b_in"] = grads["b_in"][:, :D_IN]
    small_parts = _all_gather_direct(_pack_small(grads, loss), "small_all_gather")
    g_sm, d_sm, m_sm, v_sm = _adamw(small_parts, _pack_small(params), _pack_small(mom1), _pack_small(mom2),
                                    SMALL_ROWS, "adamw_replicated")

    outs = []
    total_loss = None
    for slab_sh, slab_sm in ((g_sh, g_sm), (d_sh, d_sm), (m_sh, m_sm), (v_sh, v_sm)):
        sharded = _unpack_shard(slab_sh)
        small, extra = _unpack_small(slab_sm, params)
        if total_loss is None:
            total_loss = extra
        outs.extend(sharded[n] if n in sharded else small[n] for n in WEIGHT_NAMES)
    return (total_loss, grad_x[None], *outs)
```

```python
import jax
import jax.numpy as jnp
from jax import lax
from jax.experimental import pallas as pl
from jax.experimental.pallas import tpu as pltpu

F32 = jnp.float32
BF16 = jnp.bfloat16
HIGHEST = lax.Precision.HIGHEST
MESH = pl.DeviceIdType.MESH

N_DEV = 8
N_CHIPS = 4
D_MODEL = 1024
N_MEM = 256
N_HEADS = 4
D_HEAD = 128
D_GROUP = N_HEADS * D_HEAD
CHUNK = 64
ML_CONV = 4
FFN_CONV = 3
D_FF = 2816
D_UP = 2 * D_FF
CA_HEADS = 4
CA_DH = D_MODEL // CA_HEADS
LANES = 128
SUBLANES = 8
D_IN = 8 * D_GROUP + 2 * N_HEADS
D_IN_MAIN = 8 * D_GROUP
W_IN_SHARD = D_IN // N_DEV
W_IN_SHARD_P = 640
UP_SHARD = D_UP // N_DEV
UP_SHARD_P = 768
D_UP_P = N_DEV * UP_SHARD_P
D_FF_P = D_UP_P // 2
ALPHA = 2.0 ** 0.25
LN_EPS = 1e-5
NEG_BIG = -1e30
ADAM_LR = 0.001
ADAM_B1 = 0.9
ADAM_B2 = 0.999
ADAM_EPS = 1e-08
ADAM_WD = 0.01
ADAM_STEP = 10
VMEM_LIMIT = 56 * 1024 * 1024

SEG_HQ, SEG_HF, SEG_HI, SEG_HG, SEG_MQ, SEG_MK, SEG_MV, SEG_MO = (4 * i for i in range(8))


def _params(sem):
    return pltpu.CompilerParams(dimension_semantics=sem, vmem_limit_bytes=VMEM_LIMIT)


def _dg(a, b, ca, cb, precision=None):
    return lax.dot_general(a, b, (((ca,), (cb,)), ((), ())), precision=precision,
                           preferred_element_type=F32)


def _nn_raw(a, b):
    return _dg(a.astype(BF16), b.astype(BF16), 1, 0)


def _nt_raw(a, b):
    return _dg(a.astype(BF16), b.astype(BF16), 1, 1)


def _tn_raw(a, b):
    return _dg(a.astype(BF16), b.astype(BF16), 0, 0)


@jax.custom_vjp
def _nn(a, b):
    return _nn_raw(a, b)


_nn.defvjp(lambda a, b: (_nn_raw(a, b), (a, b)),
           lambda res, g: (_nt_raw(g, res[1]), _tn_raw(res[0], g)))


@jax.custom_vjp
def _nt(a, b):
    return _nt_raw(a, b)


_nt.defvjp(lambda a, b: (_nt_raw(a, b), (a, b)),
           lambda res, g: (_nn_raw(g, res[1]), _tn_raw(g, res[0])))


@jax.custom_vjp
def _tn(a, b):
    return _tn_raw(a, b)


_tn.defvjp(lambda a, b: (_tn_raw(a, b), (a, b)),
           lambda res, g: (_nt_raw(res[1], g), _nn_raw(res[0], g)))


def _layer_norm(z, g, b):
    mu = jnp.mean(z, axis=-1, keepdims=True)
    var = jnp.mean(jnp.square(z - mu), axis=-1, keepdims=True)
    return (z - mu) * lax.rsqrt(var + LN_EPS) * g + b


def _matmul_nn(a, w, bias, tm, tn, name):
    m, k = a.shape
    if w.ndim == 3:
        n = w.shape[0] * w.shape[2]
        assert tn == w.shape[2]
        w_spec = pl.BlockSpec((None, k, tn), lambda i, j: (j, 0, 0))
    else:
        n = w.shape[1]
        w_spec = pl.BlockSpec((k, tn), lambda i, j: (0, j))

    def body(*refs):
        a_ref, w_ref = refs[0], refs[1]
        o_ref = refs[-1]
        acc = _nn_raw(a_ref[...], w_ref[...])
        if bias is not None:
            acc = acc + refs[2][...]
        o_ref[...] = acc

    in_specs = [pl.BlockSpec((tm, k), lambda i, j: (i, 0)), w_spec]
    args = [a, w]
    if bias is not None:
        in_specs.append(pl.BlockSpec((1, tn), lambda i, j: (0, j)))
        args.append(bias)
    return pl.pallas_call(
        body, name=name, grid=(m // tm, n // tn), in_specs=in_specs,
        out_specs=pl.BlockSpec((tm, tn), lambda i, j: (i, j)),
        out_shape=jax.ShapeDtypeStruct((m, n), F32),
        compiler_params=_params(("parallel", "parallel")),
    )(*args)


def _matmul_nt(pairs, add, scale, tm, tk, name):
    m = pairs[0][0].shape[0]
    k = pairs[0][1].shape[-2]
    groups = []
    in_specs, args = [], []
    for pair in pairs:
        d, w = pair[0], pair[1]
        in_specs.append(pl.BlockSpec((tm, d.shape[1]), lambda i, j: (i, 0)))
        if w.ndim == 3:
            g = d.shape[1] // w.shape[2]
            blk = pair[2] // g
            in_specs.append(pl.BlockSpec((g, tk, w.shape[2]), lambda i, j, blk=blk: (blk, j, 0)))
            groups.append((g, w.shape[2]))
        else:
            in_specs.append(pl.BlockSpec((tk, w.shape[1]), lambda i, j: (j, 0)))
            groups.append(None)
        args += [d, w]
    if add is not None:
        in_specs.append(pl.BlockSpec((tm, tk), lambda i, j: (i, j)))
        args.append(add)

    def body(*refs):
        o_ref = refs[-1]
        acc = None
        for p, grp in enumerate(groups):
            d_ref, w_ref = refs[2 * p], refs[2 * p + 1]
            if grp is None:
                terms = [_nt_raw(d_ref[...], w_ref[...])]
            else:
                terms = [_nt_raw(d_ref[:, g * grp[1]:(g + 1) * grp[1]], w_ref[g]) for g in range(grp[0])]
            for t in terms:
                acc = t if acc is None else acc + t
        if add is not None:
            acc = acc + scale * refs[2 * len(groups)][...]
        o_ref[...] = acc

    return pl.pallas_call(
        body, name=name, grid=(m // tm, k // tk), in_specs=in_specs,
        out_specs=pl.BlockSpec((tm, tk), lambda i, j: (i, j)),
        out_shape=jax.ShapeDtypeStruct((m, k), F32),
        compiler_params=_params(("parallel", "parallel")),
    )(*args)


def _matmul_tn(a, b, tm, tn, tt, name, shards=None, shard0=0, into=None, colsum=False):
    t, m = a.shape
    n = b.shape[1]
    assert not colsum or tm == m
    n_in = 2 + (into is not None)

    def body(*refs):
        a_ref, b_ref = refs[0], refs[1]
        o_ref = refs[n_in]
        first = pl.program_id(2) == 0

        @pl.when(first)
        def _():
            o_ref[...] = jnp.zeros_like(o_ref)

        o_ref[...] += _tn_raw(a_ref[...], b_ref[...])
        if colsum:
            s_ref = refs[n_in + 1]

            @pl.when(first)
            def _():
                s_ref[...] = jnp.zeros_like(s_ref)

            s_ref[...] += jnp.sum(b_ref[...], axis=0, keepdims=True)

    in_specs = [pl.BlockSpec((tt, tm), lambda i, j, kk: (kk, i)), pl.BlockSpec((tt, tn), lambda i, j, kk: (kk, j))]
    args = [a, b]
    aliases = {}
    if into is not None:
        in_specs.append(pl.BlockSpec(memory_space=pl.ANY))
        args.append(into)
        aliases = {2: 0}
    if shards is None:
        out_specs = [pl.BlockSpec((tm, tn), lambda i, j, kk: (i, j))]
        out_shape = [jax.ShapeDtypeStruct((m, n), F32)]
    else:
        out_specs = [pl.BlockSpec((None, tm, tn), lambda i, j, kk: (shard0 + j, i, 0))]
        out_shape = [jax.ShapeDtypeStruct((shards, m, tn), F32)]
    if colsum:
        out_specs.append(pl.BlockSpec((1, tn), lambda i, j, kk: (0, j)))
        out_shape.append(jax.ShapeDtypeStruct((1, n), F32))
    res = pl.pallas_call(
        body, name=name, grid=(m // tm, n // tn, t // tt), in_specs=in_specs, out_specs=out_specs,
        out_shape=out_shape, input_output_aliases=aliases,
        compiler_params=_params(("parallel", "parallel", "arbitrary")),
    )(*args)
    return res if colsum else res[0]


ROW_TILE = 512


def _conv_fwd_tile(pad_ref, w_ref, b_ref, r0, rows, taps):
    acc = b_ref[...]
    for j in range(taps):
        acc = acc + pad_ref[pl.ds(SUBLANES - (taps - 1 - j) + r0, rows), :] * w_ref[j:j + 1, :]
    return acc


def _conv_bwd_tile(dpad_ref, w_ref, r0, rows, taps):
    acc = None
    for j in range(taps):
        term = dpad_ref[pl.ds(r0 + (taps - 1 - j), rows), :] * w_ref[j:j + 1, :]
        acc = term if acc is None else acc + term
    return acc


def _conv_grads_tile(pad_ref, dpad_ref, dx_ref, w_ref, dws, r0, rows, taps):
    dx_ref[r0:r0 + rows, :] = _conv_bwd_tile(dpad_ref, w_ref, r0, rows, taps)
    d_pre = dpad_ref[r0:r0 + rows, :]
    for j in range(taps):
        xs = pad_ref[pl.ds(SUBLANES - (taps - 1 - j) + r0, rows), :]
        dws[j] = dws[j] + jnp.sum(d_pre * xs, axis=0, keepdims=True)


def _ml_conv_fwd(proj, conv_w, conv_b):
    s = proj.shape[0]
    nblk = 2 * D_GROUP // LANES

    def body(x_ref, w_ref, b_ref, o_ref, pad_ref):
        pad_ref[0:SUBLANES, :] = jnp.zeros((SUBLANES, LANES), F32)
        pad_ref[SUBLANES:, :] = x_ref[...]
        for r0 in range(0, s, ROW_TILE):
            rows = min(ROW_TILE, s - r0)
            o_ref[r0:r0 + rows, :] = jax.nn.silu(_conv_fwd_tile(pad_ref, w_ref, b_ref, r0, rows, ML_CONV))

    return pl.pallas_call(
        body, name="ml_conv_fwd", grid=(nblk,),
        in_specs=[pl.BlockSpec((s, LANES), lambda j: (0, SEG_MQ + j)),
                  pl.BlockSpec((ML_CONV, LANES), lambda j: (0, j)),
                  pl.BlockSpec((1, LANES), lambda j: (0, j))],
        out_specs=pl.BlockSpec((s, LANES), lambda j: (0, j)),
        out_shape=jax.ShapeDtypeStruct((s, 2 * D_GROUP), F32),
        scratch_shapes=[pltpu.VMEM((s + SUBLANES, LANES), F32)],
        compiler_params=_params(("parallel",)),
    )(proj, conv_w, conv_b)


def _ml_conv_bwd(proj, conv_w, conv_b, d_qk, d_proj):
    s = proj.shape[0]
    nblk = 2 * D_GROUP // LANES

    def body(x_ref, w_ref, b_ref, dy_ref, _, dx_ref, dw_ref, db_ref, pad_ref, dpad_ref):
        pad_ref[0:SUBLANES, :] = jnp.zeros((SUBLANES, LANES), F32)
        pad_ref[SUBLANES:, :] = x_ref[...]
        dpad_ref[s:, :] = jnp.zeros((SUBLANES, LANES), F32)
        db = jnp.zeros((1, LANES), F32)
        for r0 in range(0, s, ROW_TILE):
            rows = min(ROW_TILE, s - r0)
            pre = _conv_fwd_tile(pad_ref, w_ref, b_ref, r0, rows, ML_CONV)
            _, vjp = jax.vjp(jax.nn.silu, pre)
            d_pre, = vjp(dy_ref[r0:r0 + rows, :])
            dpad_ref[r0:r0 + rows, :] = d_pre
            db = db + jnp.sum(d_pre, axis=0, keepdims=True)
        db_ref[...] = db
        dws = [jnp.zeros((1, LANES), F32) for _ in range(ML_CONV)]
        for r0 in range(0, s, ROW_TILE):
            _conv_grads_tile(pad_ref, dpad_ref, dx_ref, w_ref, dws, r0, min(ROW_TILE, s - r0), ML_CONV)
        for j in range(ML_CONV):
            dw_ref[j:j + 1, :] = dws[j]

    return pl.pallas_call(
        body, name="ml_conv_bwd", grid=(nblk,),
        in_specs=[pl.BlockSpec((s, LANES), lambda j: (0, SEG_MQ + j)),
                  pl.BlockSpec((ML_CONV, LANES), lambda j: (0, j)),
                  pl.BlockSpec((1, LANES), lambda j: (0, j)),
                  pl.BlockSpec((s, LANES), lambda j: (0, j)),
                  pl.BlockSpec(memory_space=pl.ANY)],
        out_specs=[pl.BlockSpec((s, LANES), lambda j: (0, SEG_MQ + j)),
                   pl.BlockSpec((ML_CONV, LANES), lambda j: (0, j)),
                   pl.BlockSpec((1, LANES), lambda j: (0, j))],
        out_shape=[jax.ShapeDtypeStruct(d_proj.shape, F32),
                   jax.ShapeDtypeStruct((ML_CONV, 2 * D_GROUP), F32),
                   jax.ShapeDtypeStruct((1, 2 * D_GROUP), F32)],
        input_output_aliases={4: 0},
        scratch_shapes=[pltpu.VMEM((s + SUBLANES, LANES), F32), pltpu.VMEM((s + SUBLANES, LANES), F32)],
        compiler_params=_params(("parallel",)),
    )(proj, conv_w, conv_b, d_qk, d_proj)


def _gelu_mul(a, b):
    return jax.nn.gelu(a) * b


FFN_BLOCKS = D_FF_P // LANES


def _ffn_conv_fwd(u, conv_w, conv_b):
    s = u.shape[0]

    def body(g_ref, v_ref, wg_ref, wv_ref, bg_ref, bv_ref, o_ref, gpad_ref, vpad_ref):
        for pad_ref, x_ref in ((gpad_ref, g_ref), (vpad_ref, v_ref)):
            pad_ref[0:SUBLANES, :] = jnp.zeros((SUBLANES, LANES), F32)
            pad_ref[SUBLANES:, :] = x_ref[...]
        for r0 in range(0, s, ROW_TILE):
            rows = min(ROW_TILE, s - r0)
            ug = _conv_fwd_tile(gpad_ref, wg_ref, bg_ref, r0, rows, FFN_CONV)
            uv = _conv_fwd_tile(vpad_ref, wv_ref, bv_ref, r0, rows, FFN_CONV)
            o_ref[r0:r0 + rows, :] = _gelu_mul(ug, uv)

    col = lambda off: (lambda j: (0, off + j))
    return pl.pallas_call(
        body, name="ffn_conv_fwd", grid=(FFN_BLOCKS,),
        in_specs=[pl.BlockSpec((s, LANES), col(0)), pl.BlockSpec((s, LANES), col(FFN_BLOCKS)),
                  pl.BlockSpec((FFN_CONV, LANES), col(0)), pl.BlockSpec((FFN_CONV, LANES), col(FFN_BLOCKS)),
                  pl.BlockSpec((1, LANES), col(0)), pl.BlockSpec((1, LANES), col(FFN_BLOCKS))],
        out_specs=pl.BlockSpec((s, LANES), col(0)),
        out_shape=jax.ShapeDtypeStruct((s, D_FF_P), F32),
        scratch_shapes=[pltpu.VMEM((s + SUBLANES, LANES), F32), pltpu.VMEM((s + SUBLANES, LANES), F32)],
        compiler_params=_params(("parallel",)),
    )(u, u, conv_w, conv_w, conv_b, conv_b)


def _ffn_conv_bwd(u, conv_w, conv_b, d_h):
    s = u.shape[0]

    def body(g_ref, v_ref, wg_ref, wv_ref, bg_ref, bv_ref, dh_ref,
             dug_ref, duv_ref, dwg_ref, dwv_ref, dbg_ref, dbv_ref,
             gpad_ref, vpad_ref, dgpad_ref, dvpad_ref):
        for pad_ref, x_ref in ((gpad_ref, g_ref), (vpad_ref, v_ref)):
            pad_ref[0:SUBLANES, :] = jnp.zeros((SUBLANES, LANES), F32)
            pad_ref[SUBLANES:, :] = x_ref[...]
        dgpad_ref[s:, :] = jnp.zeros((SUBLANES, LANES), F32)
        dvpad_ref[s:, :] = jnp.zeros((SUBLANES, LANES), F32)
        dbg = jnp.zeros((1, LANES), F32)
        dbv = jnp.zeros((1, LANES), F32)
        for r0 in range(0, s, ROW_TILE):
            rows = min(ROW_TILE, s - r0)
            ug = _conv_fwd_tile(gpad_ref, wg_ref, bg_ref, r0, rows, FFN_CONV)
            uv = _conv_fwd_tile(vpad_ref, wv_ref, bv_ref, r0, rows, FFN_CONV)
            _, vjp = jax.vjp(_gelu_mul, ug, uv)
            d_ug, d_uv = vjp(dh_ref[r0:r0 + rows, :])
            dgpad_ref[r0:r0 + rows, :] = d_ug
            dvpad_ref[r0:r0 + rows, :] = d_uv
            dbg = dbg + jnp.sum(d_ug, axis=0, keepdims=True)
            dbv = dbv + jnp.sum(d_uv, axis=0, keepdims=True)
        dbg_ref[...] = dbg
        dbv_ref[...] = dbv
        for pad_ref, dpad_ref, w_ref, dx_ref, dw_ref in ((gpad_ref, dgpad_ref, wg_ref, dug_ref, dwg_ref),
                                                         (vpad_ref, dvpad_ref, wv_ref, duv_ref, dwv_ref)):
            dws = [jnp.zeros((1, LANES), F32) for _ in range(FFN_CONV)]
            for r0 in range(0, s, ROW_TILE):
                _conv_grads_tile(pad_ref, dpad_ref, dx_ref, w_ref, dws, r0, min(ROW_TILE, s - r0), FFN_CONV)
            for j in range(FFN_CONV):
                dw_ref[j:j + 1, :] = dws[j]

    col = lambda off: (lambda j: (0, off + j))
    seq = pl.BlockSpec((s, LANES), col(0))
    return pl.pallas_call(
        body, name="ffn_conv_bwd", grid=(FFN_BLOCKS,),
        in_specs=[pl.BlockSpec((s, LANES), col(0)), pl.BlockSpec((s, LANES), col(FFN_BLOCKS)),
                  pl.BlockSpec((FFN_CONV, LANES), col(0)), pl.BlockSpec((FFN_CONV, LANES), col(FFN_BLOCKS)),
                  pl.BlockSpec((1, LANES), col(0)), pl.BlockSpec((1, LANES), col(FFN_BLOCKS)), seq],
        out_specs=[seq, seq, pl.BlockSpec((FFN_CONV, LANES), col(0)), pl.BlockSpec((FFN_CONV, LANES), col(0)),
                   pl.BlockSpec((1, LANES), col(0)), pl.BlockSpec((1, LANES), col(0))],
        out_shape=[jax.ShapeDtypeStruct((s, D_FF_P), F32), jax.ShapeDtypeStruct((s, D_FF_P), F32),
                   jax.ShapeDtypeStruct((FFN_CONV, D_FF_P), F32), jax.ShapeDtypeStruct((FFN_CONV, D_FF_P), F32),
                   jax.ShapeDtypeStruct((1, D_FF_P), F32), jax.ShapeDtypeStruct((1, D_FF_P), F32)],
        scratch_shapes=[pltpu.VMEM((s + SUBLANES, LANES), F32) for _ in range(4)],
        compiler_params=_params(("parallel",)),
    )(u, u, conv_w, conv_w, conv_b, conv_b, d_h)


def _chunk_masks(c):
    row = lax.broadcasted_iota(jnp.int32, (c, c), 0)
    col = lax.broadcasted_iota(jnp.int32, (c, c), 1)
    return row, col


def _hg_step(hq, hf, hi, hgate, l0, l1, nw, st):
    c = hq.shape[0]
    row, col = _chunk_masks(c)
    mask = col <= row
    mx = lax.stop_gradient(jnp.maximum(l0, l1))
    e0 = jnp.exp(l0 - mx)
    e1 = jnp.exp(l1 - mx)
    lb = e0 / (e0 + e1)
    sig = jax.nn.sigmoid(hf)
    lf = jnp.log(lb + (1.0 - lb) * sig)
    k = (1.0 - lb) * jax.nn.sigmoid(-hf)
    q = jax.nn.silu(hq)
    b = _dg(mask.astype(F32), lf, 1, 0, HIGHEST)
    rid = lax.broadcasted_iota(jnp.int32, b.shape, 0)
    b_ref = jnp.sum(jnp.where(rid == c // 2 - 1, b, 0.0), axis=0, keepdims=True)
    b_last = jnp.sum(jnp.where(rid == c - 1, b, 0.0), axis=0, keepdims=True)
    attn = _nt(q * jnp.exp(b - b_ref), k * jnp.exp(b_ref - b))
    attn = jnp.where(mask, attn, 0.0)
    o = _nn(attn, hi) + _nt(q * jnp.exp(b), st)
    st_new = jnp.exp(b_last) * st + _tn(hi, k * jnp.exp(b_last - b))
    y = o * lax.rsqrt(jnp.mean(o * o, axis=-1, keepdims=True) + LN_EPS) * nw * jax.nn.silu(hgate)
    return y, st_new


def _head(ref, seg, h):
    lo = seg * D_GROUP + h * D_HEAD
    return ref[:, lo:lo + D_HEAD]


def _hgrn2_fwd(proj, logits, norm_w):
    s = proj.shape[0]
    nc = s // CHUNK

    def body(p_ref, lg_ref, nw_ref, y_ref, st_out_ref, st_scr):
        @pl.when(pl.program_id(0) == 0)
        def _():
            st_scr[...] = jnp.zeros_like(st_scr)

        for h in range(N_HEADS):
            lo = h * D_HEAD
            st = st_scr[h]
            st_out_ref[h] = st
            y, st_new = _hg_step(_head(p_ref, 0, h), _head(p_ref, 1, h), _head(p_ref, 2, h), _head(p_ref, 3, h),
                                 lg_ref[0:1, lo:lo + D_HEAD], lg_ref[1:2, lo:lo + D_HEAD],
                                 nw_ref[:, lo:lo + D_HEAD], st)
            y_ref[:, lo:lo + D_HEAD] = y
            st_scr[h] = st_new

    return pl.pallas_call(
        body, name="hgrn2_fwd", grid=(nc,),
        in_specs=[pl.BlockSpec((CHUNK, 4 * D_GROUP), lambda c: (c, 0)),
                  pl.BlockSpec((2, D_GROUP), lambda c: (0, 0)),
                  pl.BlockSpec((1, D_GROUP), lambda c: (0, 0))],
        out_specs=[pl.BlockSpec((CHUNK, D_GROUP), lambda c: (c, 0)),
                   pl.BlockSpec((None, N_HEADS, D_HEAD, D_HEAD), lambda c: (c, 0, 0, 0))],
        out_shape=[jax.ShapeDtypeStruct((s, 2 * D_GROUP), F32),
                   jax.ShapeDtypeStruct((nc, N_HEADS, D_HEAD, D_HEAD), F32)],
        scratch_shapes=[pltpu.VMEM((N_HEADS, D_HEAD, D_HEAD), F32)],
        compiler_params=_params(("arbitrary",)),
    )(proj, logits, norm_w)


def _hgrn2_bwd(proj, logits, norm_w, states, d_y):
    s = proj.shape[0]
    nc = s // CHUNK

    def body(p_ref, lg_ref, nw_ref, st_ref, dy_ref, dp_ref, dl_ref, dnw_ref, dst_scr):
        @pl.when(pl.program_id(0) == 0)
        def _():
            dst_scr[...] = jnp.zeros_like(dst_scr)
            dl_ref[...] = jnp.zeros_like(dl_ref)
            dnw_ref[...] = jnp.zeros_like(dnw_ref)

        for h in range(N_HEADS):
            lo = h * D_HEAD
            _, vjp = jax.vjp(_hg_step, _head(p_ref, 0, h), _head(p_ref, 1, h), _head(p_ref, 2, h),
                             _head(p_ref, 3, h), lg_ref[0:1, lo:lo + D_HEAD], lg_ref[1:2, lo:lo + D_HEAD],
                             nw_ref[:, lo:lo + D_HEAD], st_ref[h])
            d_hq, d_hf, d_hi, d_hg, d_l0, d_l1, d_nw, d_st = vjp((dy_ref[:, lo:lo + D_HEAD], dst_scr[h]))
            for seg, val in enumerate((d_hq, d_hf, d_hi, d_hg)):
                dp_ref[:, seg * D_GROUP + lo:seg * D_GROUP + lo + D_HEAD] = val
            dl_ref[0:1, lo:lo + D_HEAD] += d_l0
            dl_ref[1:2, lo:lo + D_HEAD] += d_l1
            dnw_ref[:, lo:lo + D_HEAD] += d_nw
            dst_scr[h] = d_st

    rev = lambda c: nc - 1 - c
    return pl.pallas_call(
        body, name="hgrn2_bwd", grid=(nc,),
        in_specs=[pl.BlockSpec((CHUNK, 4 * D_GROUP), lambda c: (rev(c), 0)),
                  pl.BlockSpec((2, D_GROUP), lambda c: (0, 0)),
                  pl.BlockSpec((1, D_GROUP), lambda c: (0, 0)),
                  pl.BlockSpec((None, N_HEADS, D_HEAD, D_HEAD), lambda c: (rev(c), 0, 0, 0)),
                  pl.BlockSpec((CHUNK, D_GROUP), lambda c: (rev(c), 0))],
        out_specs=[pl.BlockSpec((CHUNK, 4 * D_GROUP), lambda c: (rev(c), 0)),
                   pl.BlockSpec((2, D_GROUP), lambda c: (0, 0)),
                   pl.BlockSpec((1, D_GROUP), lambda c: (0, 0))],
        out_shape=[jax.ShapeDtypeStruct((s, D_IN_MAIN), F32), jax.ShapeDtypeStruct((2, D_GROUP), F32),
                   jax.ShapeDtypeStruct((1, D_GROUP), F32)],
        scratch_shapes=[pltpu.VMEM((N_HEADS, D_HEAD, D_HEAD), F32)],
        compiler_params=_params(("arbitrary",)),
    )(proj, logits, norm_w, states, d_y)


def _ml_step(qc, kc, v, mo, ig, fr, nw, ct, n, m):
    c = qc.shape[0]
    row, col = _chunk_masks(c)
    mask = col <= row
    eye = col == row
    q = qc * (D_HEAD ** -0.5)
    lf = jax.nn.log_sigmoid(fr)
    to_row = lambda t: jnp.sum(jnp.where(eye, t, 0.0), axis=0, keepdims=True)
    lf_row = to_row(lf)
    ig_row = to_row(ig)
    b_col = jnp.sum(jnp.where(mask, lf_row, 0.0), axis=1, keepdims=True)
    b_row = jnp.sum(jnp.where(row <= col, lf, 0.0), axis=0, keepdims=True)
    g = jnp.sum(lf, axis=0, keepdims=True)
    d = jnp.where(mask, b_col - b_row + ig_row, -jnp.inf)
    inter = b_col + m
    m_t = lax.stop_gradient(jnp.maximum(inter, jnp.max(d, axis=1, keepdims=True)))
    w = jnp.exp(d - m_t)
    sc = _nt(q, kc) * w
    w_inter = jnp.exp(inter - m_t)
    num = _nn(sc, v) + w_inter * _nt(q, ct)
    den = jnp.sum(sc, axis=1, keepdims=True) + w_inter * jnp.sum(q * n, axis=1, keepdims=True)
    h = num / jnp.maximum(jnp.abs(den), jnp.exp(-m_t))
    a = g - b_col + ig
    m_new = lax.stop_gradient(jnp.maximum(g + m, jnp.max(a, axis=0, keepdims=True)))
    decay = jnp.exp(g + m - m_new)
    wk = kc * jnp.exp(a - m_new)
    ct_new = decay * ct + _tn(v, wk)
    n_new = decay * n + jnp.sum(wk, axis=0, keepdims=True)
    mu = jnp.mean(h, axis=-1, keepdims=True)
    var = jnp.mean(jnp.square(h - mu), axis=-1, keepdims=True)
    y = jax.nn.sigmoid(mo) * ((h - mu) * lax.rsqrt(var + LN_EPS) * nw)
    return y, ct_new, n_new, m_new


def _gate_column(gates, lane, idx):
    return jnp.sum(jnp.where(lane == idx, gates, 0.0), axis=1, keepdims=True)


def _mlstm_fwd(qk, proj, gates, norm_w, y):
    s = proj.shape[0]
    nc = s // CHUNK

    def body(qk_ref, vo_ref, g_ref, nw_ref, _, y_ref, ct_out, n_out, m_out, ct_scr, n_scr, m_scr):
        @pl.when(pl.program_id(0) == 0)
        def _():
            ct_scr[...] = jnp.zeros_like(ct_scr)
            n_scr[...] = jnp.zeros_like(n_scr)
            m_scr[...] = jnp.full(m_scr.shape, NEG_BIG, F32)

        gates_blk = g_ref[...]
        lane = lax.broadcasted_iota(jnp.int32, gates_blk.shape, 1)
        for h in range(N_HEADS):
            lo = h * D_HEAD
            ct, n, m = ct_scr[h], n_scr[h], m_scr[h]
            ct_out[h] = ct
            n_out[h] = n
            m_out[h] = m
            yh, ct_new, n_new, m_new = _ml_step(
                _head(qk_ref, 0, h), _head(qk_ref, 1, h), _head(vo_ref, 0, h), _head(vo_ref, 1, h),
                _gate_column(gates_blk, lane, h), _gate_column(gates_blk, lane, N_HEADS + h),
                nw_ref[:, lo:lo + D_HEAD], ct, n, m)
            y_ref[:, lo:lo + D_HEAD] = yh
            ct_scr[h] = ct_new
            n_scr[h] = n_new
            m_scr[h] = m_new

    st = lambda r, w: pl.BlockSpec((None, N_HEADS, r, w), lambda c: (c, 0, 0, 0))
    return pl.pallas_call(
        body, name="mlstm_fwd", grid=(nc,),
        in_specs=[pl.BlockSpec((CHUNK, 2 * D_GROUP), lambda c: (c, 0)),
                  pl.BlockSpec((CHUNK, 2 * D_GROUP), lambda c: (c, 3)),
                  pl.BlockSpec((CHUNK, LANES), lambda c: (c, 0)),
                  pl.BlockSpec((1, D_GROUP), lambda c: (0, 0)),
                  pl.BlockSpec(memory_space=pl.ANY)],
        out_specs=[pl.BlockSpec((CHUNK, D_GROUP), lambda c: (c, 1)),
                   st(D_HEAD, D_HEAD), st(1, D_HEAD), st(1, 1)],
        out_shape=[jax.ShapeDtypeStruct((s, 2 * D_GROUP), F32),
                   jax.ShapeDtypeStruct((nc, N_HEADS, D_HEAD, D_HEAD), F32),
                   jax.ShapeDtypeStruct((nc, N_HEADS, 1, D_HEAD), F32),
                   jax.ShapeDtypeStruct((nc, N_HEADS, 1, 1), F32)],
        input_output_aliases={4: 0},
        scratch_shapes=[pltpu.VMEM((N_HEADS, D_HEAD, D_HEAD), F32), pltpu.VMEM((N_HEADS, 1, D_HEAD), F32),
                        pltpu.VMEM((N_HEADS, 1, 1), F32)],
        compiler_params=_params(("arbitrary",)),
    )(qk, proj, gates, norm_w, y)


def _mlstm_bwd(qk, proj, gates, norm_w, ct_s, n_s, m_s, d_y, d_proj):
    s = proj.shape[0]
    nc = s // CHUNK

    def body(qk_ref, vo_ref, g_ref, nw_ref, ct_ref, n_ref, m_ref, dy_ref, _,
             dp_ref, dqk_ref, dg_ref, dnw_ref, dct_scr, dn_scr):
        @pl.when(pl.program_id(0) == 0)
        def _():
            dct_scr[...] = jnp.zeros_like(dct_scr)
            dn_scr[...] = jnp.zeros_like(dn_scr)
            dnw_ref[...] = jnp.zeros_like(dnw_ref)

        gates_blk = g_ref[...]
        lane = lax.broadcasted_iota(jnp.int32, gates_blk.shape, 1)
        d_gates = jnp.zeros(gates_blk.shape, F32)
        for h in range(N_HEADS):
            lo = h * D_HEAD
            m = m_ref[h]
            step = lambda *a, m=m: _ml_step(*a, m)[:3]
            _, vjp = jax.vjp(step, _head(qk_ref, 0, h), _head(qk_ref, 1, h), _head(vo_ref, 0, h),
                             _head(vo_ref, 1, h), _gate_column(gates_blk, lane, h),
                             _gate_column(gates_blk, lane, N_HEADS + h), nw_ref[:, lo:lo + D_HEAD],
                             ct_ref[h], n_ref[h])
            d_q, d_k, d_v, d_o, d_gi, d_gf, d_nw, d_ct, d_n = vjp((dy_ref[:, lo:lo + D_HEAD], dct_scr[h], dn_scr[h]))
            dqk_ref[:, lo:lo + D_HEAD] = d_q
            dqk_ref[:, D_GROUP + lo:D_GROUP + lo + D_HEAD] = d_k
            dp_ref[:, lo:lo + D_HEAD] = d_v
            dp_ref[:, D_GROUP + lo:D_GROUP + lo + D_HEAD] = d_o
            d_gates = d_gates + jnp.where(lane == h, d_gi, 0.0) + jnp.where(lane == N_HEADS + h, d_gf, 0.0)
            dnw_ref[:, lo:lo + D_HEAD] += d_nw
            dct_scr[h] = d_ct
            dn_scr[h] = d_n
        dg_ref[...] = d_gates

    rev = lambda c: nc - 1 - c
    st = lambda r, w: pl.BlockSpec((None, N_HEADS, r, w), lambda c: (rev(c), 0, 0, 0))
    return pl.pallas_call(
        body, name="mlstm_bwd", grid=(nc,),
        in_specs=[pl.BlockSpec((CHUNK, 2 * D_GROUP), lambda c: (rev(c), 0)),
                  pl.BlockSpec((CHUNK, 2 * D_GROUP), lambda c: (rev(c), 3)),
                  pl.BlockSpec((CHUNK, LANES), lambda c: (rev(c), 0)),
                  pl.BlockSpec((1, D_GROUP), lambda c: (0, 0)),
                  st(D_HEAD, D_HEAD), st(1, D_HEAD), st(1, 1),
                  pl.BlockSpec((CHUNK, D_GROUP), lambda c: (rev(c), 1)),
                  pl.BlockSpec(memory_space=pl.ANY)],
        out_specs=[pl.BlockSpec((CHUNK, 2 * D_GROUP), lambda c: (rev(c), 3)),
                   pl.BlockSpec((CHUNK, 2 * D_GROUP), lambda c: (rev(c), 0)),
                   pl.BlockSpec((CHUNK, LANES), lambda c: (rev(c), 0)),
                   pl.BlockSpec((1, D_GROUP), lambda c: (0, 0))],
        out_shape=[jax.ShapeDtypeStruct(d_proj.shape, F32), jax.ShapeDtypeStruct((s, 2 * D_GROUP), F32),
                   jax.ShapeDtypeStruct((s, LANES), F32), jax.ShapeDtypeStruct((1, D_GROUP), F32)],
        input_output_aliases={8: 0},
        scratch_shapes=[pltpu.VMEM((N_HEADS, D_HEAD, D_HEAD), F32), pltpu.VMEM((N_HEADS, 1, D_HEAD), F32)],
        compiler_params=_params(("arbitrary",)),
    )(qk, proj, gates, norm_w, ct_s, n_s, m_s, d_y, d_proj)


LN_TOKENS = 512
ATT_TOKENS = 256


def _res_ln_fwd(xres, branch, g, b, name):
    s, dm = xres.shape
    tb = min(LN_TOKENS, s)

    def body(x_ref, br_ref, g_ref, b_ref, o_ref):
        o_ref[...] = _layer_norm(ALPHA * x_ref[...] + br_ref[...], g_ref[...], b_ref[...])

    tok = pl.BlockSpec((tb, dm), lambda i: (i, 0))
    vec = pl.BlockSpec((1, dm), lambda i: (0, 0))
    return pl.pallas_call(
        body, name=name, grid=(s // tb,), in_specs=[tok, tok, vec, vec], out_specs=tok,
        out_shape=jax.ShapeDtypeStruct((s, dm), F32), compiler_params=_params(("parallel",)),
    )(xres, branch, g, b)


def _res_ln_bwd(xres, branch, g, b, d_out, name):
    s, dm = xres.shape
    tb = min(LN_TOKENS, s)

    def body(x_ref, br_ref, g_ref, b_ref, do_ref, dz_ref, dg_ref, db_ref):
        @pl.when(pl.program_id(0) == 0)
        def _():
            dg_ref[...] = jnp.zeros_like(dg_ref)
            db_ref[...] = jnp.zeros_like(db_ref)

        z = ALPHA * x_ref[...] + br_ref[...]
        _, vjp = jax.vjp(_layer_norm, z, g_ref[...], b_ref[...])
        d_z, d_g, d_b = vjp(do_ref[...])
        dz_ref[...] = d_z
        dg_ref[...] += d_g
        db_ref[...] += d_b

    tok = pl.BlockSpec((tb, dm), lambda i: (i, 0))
    vec = pl.BlockSpec((1, dm), lambda i: (0, 0))
    return pl.pallas_call(
        body, name=name, grid=(s // tb,), in_specs=[tok, tok, vec, vec, tok], out_specs=[tok, vec, vec],
        out_shape=[jax.ShapeDtypeStruct((s, dm), F32), jax.ShapeDtypeStruct((1, dm), F32),
                   jax.ShapeDtypeStruct((1, dm), F32)],
        compiler_params=_params(("arbitrary",)),
    )(xres, branch, g, b, d_out)


def _loss_tail(xres, branch, g, b, target):
    s, dm = xres.shape
    tb = min(LN_TOKENS, s)

    def loss_fn(z, gg, bb, tgt):
        err = jnp.square(_layer_norm(z, gg, bb) - tgt)
        return 0.5 * jnp.sum(jnp.mean(err, axis=-1, keepdims=True), axis=0, keepdims=True)

    def body(x_ref, br_ref, g_ref, b_ref, t_ref, loss_ref, dz_ref, dg_ref, db_ref):
        @pl.when(pl.program_id(0) == 0)
        def _():
            loss_ref[...] = jnp.zeros_like(loss_ref)
            dg_ref[...] = jnp.zeros_like(dg_ref)
            db_ref[...] = jnp.zeros_like(db_ref)

        z = ALPHA * x_ref[...] + br_ref[...]
        tgt = t_ref[...]
        loss, vjp = jax.vjp(lambda zz, gg, bb: loss_fn(zz, gg, bb, tgt), z, g_ref[...], b_ref[...])
        d_z, d_g, d_b = vjp(jnp.ones((1, 1), F32))
        loss_ref[...] += loss
        dz_ref[...] = d_z
        dg_ref[...] += d_g
        db_ref[...] += d_b

    tok = pl.BlockSpec((tb, dm), lambda i: (i, 0))
    vec = pl.BlockSpec((1, dm), lambda i: (0, 0))
    one = pl.BlockSpec((1, 1), lambda i: (0, 0))
    return pl.pallas_call(
        body, name="loss_tail", grid=(s // tb,), in_specs=[tok, tok, vec, vec, tok],
        out_specs=[one, tok, vec, vec],
        out_shape=[jax.ShapeDtypeStruct((1, 1), F32), jax.ShapeDtypeStruct((s, dm), F32),
                   jax.ShapeDtypeStruct((1, dm), F32), jax.ShapeDtypeStruct((1, dm), F32)],
        compiler_params=_params(("arbitrary",)),
    )(xres, branch, g, b, target)


def _att_head(q, k, v):
    sc = _nt(q, k) * (CA_DH ** -0.5)
    return _nn(jax.nn.softmax(sc, axis=-1), v)


def _att_fwd(q, kv):
    s = q.shape[0]
    tb = min(ATT_TOKENS, s)

    def body(q_ref, kv_ref, o_ref):
        for h in range(CA_HEADS):
            lo = h * CA_DH
            o_ref[:, lo:lo + CA_DH] = _att_head(q_ref[:, lo:lo + CA_DH], kv_ref[:, lo:lo + CA_DH],
                                                kv_ref[:, D_MODEL + lo:D_MODEL + lo + CA_DH])

    tok = pl.BlockSpec((tb, D_MODEL), lambda i: (i, 0))
    return pl.pallas_call(
        body, name="att_fwd", grid=(s // tb,),
        in_specs=[tok, pl.BlockSpec((N_MEM, 2 * D_MODEL), lambda i: (0, 0))], out_specs=tok,
        out_shape=jax.ShapeDtypeStruct((s, D_MODEL), F32), compiler_params=_params(("parallel",)),
    )(q, kv)


def _att_bwd(q, kv, d_o):
    s = q.shape[0]
    tb = min(ATT_TOKENS, s)

    def body(q_ref, kv_ref, do_ref, dq_ref, dkv_ref):
        @pl.when(pl.program_id(0) == 0)
        def _():
            dkv_ref[...] = jnp.zeros_like(dkv_ref)

        for h in range(CA_HEADS):
            lo = h * CA_DH
            vlo = D_MODEL + lo
            _, vjp = jax.vjp(_att_head, q_ref[:, lo:lo + CA_DH], kv_ref[:, lo:lo + CA_DH],
                             kv_ref[:, vlo:vlo + CA_DH])
            d_q, d_k, d_v = vjp(do_ref[:, lo:lo + CA_DH])
            dq_ref[:, lo:lo + CA_DH] = d_q
            dkv_ref[:, lo:lo + CA_DH] += d_k
            dkv_ref[:, vlo:vlo + CA_DH] += d_v

    tok = pl.BlockSpec((tb, D_MODEL), lambda i: (i, 0))
    mem = pl.BlockSpec((N_MEM, 2 * D_MODEL), lambda i: (0, 0))
    return pl.pallas_call(
        body, name="att_bwd", grid=(s // tb,), in_specs=[tok, mem, tok], out_specs=[tok, mem],
        out_shape=[jax.ShapeDtypeStruct((s, D_MODEL), F32), jax.ShapeDtypeStruct((N_MEM, 2 * D_MODEL), F32)],
        compiler_params=_params(("arbitrary",)),
    )(q, kv, d_o)


def _local_step(x, mem, target, w):
    s = x.shape[0]
    tm = min(512, s)
    tt = min(512, s)
    proj = _matmul_nn(x, w["w_in_main"], w["b_in_main"], tm, 512, "proj")
    gates = _matmul_nn(x, w["w_in_gate"], w["b_in_gate"], tm, LANES, "proj_gates")
    qk = _ml_conv_fwd(proj, w["ml_conv_w"], w["ml_conv_b"])
    y, hg_states = _hgrn2_fwd(proj, w["hg_lb_logits"], w["hg_norm_w"])
    y, ct_s, n_s, m_s = _mlstm_fwd(qk, proj, gates, w["ml_norm_w"], y)
    mix = _matmul_nn(y, w["w_out"], None, tm, D_MODEL, "mix")
    x1 = _res_ln_fwd(x, mix, w["ln1_g"], w["ln1_b"], "ln1_fwd")
    kv = _matmul_nn(mem, w["ca_wkv"], None, N_MEM, CA_DH, "kv")
    q = _matmul_nn(x1, w["ca_wq"], None, tm, D_MODEL, "ca_q")
    att = _att_fwd(q, kv)
    ca = _matmul_nn(att, w["ca_wo"], None, tm, D_MODEL, "ca_out")
    x2 = _res_ln_fwd(x1, ca, w["ln2_g"], w["ln2_b"], "ln2_fwd")
    u = _matmul_nn(x2, w["ffn_w_up"], None, tm, UP_SHARD_P, "ffn_up")
    hid = _ffn_conv_fwd(u, w["ffn_conv_w"], w["ffn_conv_b"])
    ff = _matmul_nn(hid, w["ffn_w_down"], None, tm, D_MODEL, "ffn_down")
    loss, d_z3, d_ln3_g, d_ln3_b = _loss_tail(x2, ff, w["ln3_g"], w["ln3_b"], target)
    grads = {"ln3_g": d_ln3_g, "ln3_b": d_ln3_b}
    grads["ffn_w_down"] = _matmul_tn(hid, d_z3, 1536, D_MODEL, tt, "d_w_down")
    d_hid = _matmul_nt([(d_z3, w["ffn_w_down"])], None, 1.0, tm, 1536, "d_hid")
    d_ug, d_uv, d_cwg, d_cwv, d_cbg, d_cbv = _ffn_conv_bwd(u, w["ffn_conv_w"], w["ffn_conv_b"], d_hid)
    grads["ffn_conv_w"] = jnp.concatenate([d_cwg, d_cwv], axis=-1)
    grads["ffn_conv_b"] = jnp.concatenate([d_cbg, d_cbv], axis=-1)
    d_w_up = _matmul_tn(x2, d_ug, D_MODEL, UP_SHARD_P, tt, "d_w_up_gate", shards=N_DEV)
    grads["ffn_w_up"] = _matmul_tn(x2, d_uv, D_MODEL, UP_SHARD_P, tt, "d_w_up_val", shards=N_DEV,
                                   shard0=N_DEV // 2, into=d_w_up)
    d_x2 = _matmul_nt([(d_ug, w["ffn_w_up"], 0), (d_uv, w["ffn_w_up"], N_DEV // 2)], d_z3, ALPHA,
                      min(256, s), D_MODEL, "d_x2")
    d_z2, grads["ln2_g"], grads["ln2_b"] = _res_ln_bwd(x1, ca, w["ln2_g"], w["ln2_b"], d_x2, "ln2_bwd")
    grads["ca_wo"] = _matmul_tn(att, d_z2, D_MODEL, D_MODEL, tt, "d_ca_wo")
    d_att = _matmul_nt([(d_z2, w["ca_wo"])], None, 1.0, tm, D_MODEL, "d_att")
    d_q, d_kv = _att_bwd(q, kv, d_att)
    grads["ca_wq"] = _matmul_tn(x1, d_q, D_MODEL, D_MODEL, tt, "d_ca_wq")
    grads["ca_wkv"] = _matmul_tn(mem, d_kv, D_MODEL, CA_DH, N_MEM, "d_ca_wkv", shards=N_DEV)
    d_x1 = _matmul_nt([(d_q, w["ca_wq"])], d_z2, ALPHA, tm, D_MODEL, "d_x1")
    d_z1, grads["ln1_g"], grads["ln1_b"] = _res_ln_bwd(x, mix, w["ln1_g"], w["ln1_b"], d_x1, "ln1_bwd")
    grads["w_out"] = _matmul_tn(y, d_z1, D_MODEL, D_MODEL, tt, "d_w_out")
    d_y = _matmul_nt([(d_z1, w["w_out"])], None, 1.0, tm, D_MODEL, "d_y")
    d_proj, grads["hg_lb_logits"], grads["hg_norm_w"] = _hgrn2_bwd(
        proj, w["hg_lb_logits"], w["hg_norm_w"], hg_states, d_y)
    d_proj, d_qk, d_gates, grads["ml_norm_w"] = _mlstm_bwd(
        qk, proj, gates, w["ml_norm_w"], ct_s, n_s, m_s, d_y, d_proj)
    d_proj, grads["ml_conv_w"], grads["ml_conv_b"] = _ml_conv_bwd(
        proj, w["ml_conv_w"], w["ml_conv_b"], d_qk, d_proj)
    grads["w_in_main"], grads["b_in_main"] = _matmul_tn(x, d_proj, D_MODEL, 512, tt, "d_w_in", colsum=True)
    grads["w_in_gate"], grads["b_in_gate"] = _matmul_tn(x, d_gates, D_MODEL, LANES, tt, "d_w_in_gates", colsum=True)
    grad_x = _matmul_nt([(d_proj, w["w_in_main"]), (d_gates, w["w_in_gate"])], d_z1, ALPHA, tm, D_MODEL, "d_x")
    return loss, grad_x, grads


HBM_SPEC = pl.BlockSpec(memory_space=pltpu.HBM)


def _coords():
    return lax.axis_index("x"), lax.axis_index("y"), lax.axis_index("c")


def _other_chips(x, y):
    return [(1 - x, y), (x, 1 - y), (1 - x, 1 - y)]


def _all_gather_two_level(shards, name):
    na = len(shards)

    def body(*refs):
        x_refs, out_refs = refs[:na], refs[na:2 * na]
        send_sems, recv_sems, local_sems = refs[2 * na:]
        x, y, c = _coords()
        me, sibling = (x, y, c), (x, y, 1 - c)
        chips = _other_chips(x, y)

        def copy(a, k, block, to, own=False):
            slot = out_refs[a].at[4 * block[0] + 2 * block[1] + block[2]]
            return pltpu.make_async_remote_copy(
                src_ref=x_refs[a] if own else slot, dst_ref=slot,
                send_sem=send_sems.at[7 * a + k], recv_sem=recv_sems.at[7 * a + k],
                device_id=to, device_id_type=MESH)

        mine = [pltpu.make_async_copy(x_refs[a], out_refs[a].at[4 * x + 2 * y + c], local_sems.at[a])
                for a in range(na)]
        for cp in mine:
            cp.start()
        first = []
        for a in range(na):
            first.append(copy(a, 0, me, sibling, own=True))
            first += [copy(a, 1 + j, me, (*chip, c), own=True) for j, chip in enumerate(chips)]
        for cp in first:
            cp.start()
        passed = []
        for j, chip in enumerate(chips):
            for a in range(na):
                copy(a, 1 + j, (*chip, c), me).wait_recv()
                fwd = copy(a, 4 + j, (*chip, c), sibling)
                fwd.start()
                passed.append(fwd)
        for a in range(na):
            copy(a, 0, sibling, me).wait_recv()
            for j, chip in enumerate(chips):
                copy(a, 4 + j, (*chip, 1 - c), me).wait_recv()
        for cp in first + passed:
            cp.wait_send()
        for cp in mine:
            cp.wait()

    return pl.pallas_call(
        body, name=name,
        out_shape=[jax.ShapeDtypeStruct((N_DEV,) + t.shape, t.dtype) for t in shards],
        in_specs=[HBM_SPEC] * na, out_specs=[HBM_SPEC] * na,
        scratch_shapes=[pltpu.SemaphoreType.DMA((7 * na,)), pltpu.SemaphoreType.DMA((7 * na,)),
                        pltpu.SemaphoreType.DMA((na,))],
    )(*shards)


def _all_gather_direct(vec, name):
    r, n = vec.shape

    def body(x_ref, out_ref, send_sems, recv_sems, local_sem):
        x, y, c = _coords()
        flip = lambda v, bit: 1 - v if bit else v
        me = 4 * x + 2 * y + c
        mine = pltpu.make_async_copy(x_ref, out_ref.at[me], local_sem)
        mine.start()
        copies = []
        for d in range(1, N_DEV):
            peer = (flip(x, d & 4), flip(y, d & 2), flip(c, d & 1))
            peer_slot = 4 * peer[0] + 2 * peer[1] + peer[2]
            out_going = pltpu.make_async_remote_copy(
                src_ref=x_ref, dst_ref=out_ref.at[me], send_sem=send_sems.at[d - 1],
                recv_sem=recv_sems.at[d - 1], device_id=peer, device_id_type=MESH)
            incoming = pltpu.make_async_remote_copy(
                src_ref=x_ref, dst_ref=out_ref.at[peer_slot], send_sem=send_sems.at[d - 1],
                recv_sem=recv_sems.at[d - 1], device_id=peer, device_id_type=MESH)
            out_going.start()
            copies.append((out_going, incoming))
        for out_going, incoming in copies:
            incoming.wait_recv()
            out_going.wait_send()
        mine.wait()

    return pl.pallas_call(
        body, name=name, out_shape=jax.ShapeDtypeStruct((N_DEV, r, n), vec.dtype),
        in_specs=[HBM_SPEC], out_specs=HBM_SPEC,
        scratch_shapes=[pltpu.SemaphoreType.DMA((7,)), pltpu.SemaphoreType.DMA((7,)), pltpu.SemaphoreType.DMA],
    )(vec)


def _exchange_with_sibling(parts):
    na = len(parts)

    def body(*refs):
        p_refs, got_refs = refs[:na], refs[na:2 * na]
        send_sems, recv_sems = refs[2 * na:]
        x, y, c = _coords()
        copies = []
        for a in range(na):
            for chip in range(N_CHIPS):
                cp = pltpu.make_async_remote_copy(
                    src_ref=p_refs[a].at[2 * chip + (1 - c)], dst_ref=got_refs[a].at[chip],
                    send_sem=send_sems.at[N_CHIPS * a + chip], recv_sem=recv_sems.at[N_CHIPS * a + chip],
                    device_id=(x, y, 1 - c), device_id_type=MESH)
                cp.start()
                copies.append(cp)
        for cp in copies:
            cp.wait()

    return pl.pallas_call(
        body, name="grad_exchange_sibling",
        out_shape=[jax.ShapeDtypeStruct((N_CHIPS,) + t.shape[1:], t.dtype) for t in parts],
        in_specs=[HBM_SPEC] * na, out_specs=[HBM_SPEC] * na,
        scratch_shapes=[pltpu.SemaphoreType.DMA((N_CHIPS * na,)), pltpu.SemaphoreType.DMA((N_CHIPS * na,))],
    )(*parts)


def _exchange_with_chips(sums):
    na = len(sums)

    def body(*refs):
        a_refs, out_refs = refs[:na], refs[na:2 * na]
        send_sems, recv_sems = refs[2 * na:]
        x, y, c = _coords()
        copies = []
        for a in range(na):
            for j, (cx, cy) in enumerate(_other_chips(x, y)):
                cp = pltpu.make_async_remote_copy(
                    src_ref=a_refs[a].at[2 * cx + cy], dst_ref=out_refs[a].at[j],
                    send_sem=send_sems.at[3 * a + j], recv_sem=recv_sems.at[3 * a + j],
                    device_id=(cx, cy, c), device_id_type=MESH)
                cp.start()
                copies.append(cp)
        for cp in copies:
            cp.wait()

    return pl.pallas_call(
        body, name="grad_exchange_chips",
        out_shape=[jax.ShapeDtypeStruct((3,) + t.shape[1:], t.dtype) for t in sums],
        in_specs=[HBM_SPEC] * na, out_specs=[HBM_SPEC] * na,
        scratch_shapes=[pltpu.SemaphoreType.DMA((3 * na,)), pltpu.SemaphoreType.DMA((3 * na,))],
    )(*sums)


def _row_tile(rows):
    for t in (256, 176, 128):
        if rows % t == 0 and rows > t:
            return t
    return rows


def _add_sibling(core, parts, got, name):
    _, r, c = parts.shape
    tr = _row_tile(r)

    def body(core_ref, p_ref, g_ref, o_ref):
        o_ref[...] = p_ref[...] + g_ref[...]

    return pl.pallas_call(
        body, name=name,
        grid_spec=pltpu.PrefetchScalarGridSpec(
            num_scalar_prefetch=1, grid=(N_CHIPS, r // tr),
            in_specs=[pl.BlockSpec((None, tr, c), lambda i, j, core_ref: (2 * i + core_ref[0], j, 0)),
                      pl.BlockSpec((None, tr, c), lambda i, j, core_ref: (i, j, 0))],
            out_specs=pl.BlockSpec((None, tr, c), lambda i, j, core_ref: (i, j, 0))),
        out_shape=jax.ShapeDtypeStruct((N_CHIPS, r, c), F32),
        compiler_params=_params(("parallel", "parallel")),
    )(core, parts, got)


def _adamw_math(g, w, m, v):
    m_new = ADAM_B1 * m + (1.0 - ADAM_B1) * g
    v_new = ADAM_B2 * v + (1.0 - ADAM_B2) * jnp.square(g)
    m_hat = m_new / (1.0 - ADAM_B1 ** ADAM_STEP)
    v_hat = v_new / (1.0 - ADAM_B2 ** ADAM_STEP)
    delta = -ADAM_LR * (m_hat / (jnp.sqrt(v_hat) + ADAM_EPS) + ADAM_WD * w)
    return delta, m_new, v_new


def _adamw_sharded(chip, sums, got, w, m, v, name):
    r, c = w.shape
    tr = _row_tile(r)

    def body(chip_ref, s_ref, g_ref, w_ref, m_ref, v_ref, go_ref, d_ref, nm_ref, nv_ref):
        g = s_ref[...] + g_ref[0] + g_ref[1] + g_ref[2]
        delta, m_new, v_new = _adamw_math(g, w_ref[...], m_ref[...], v_ref[...])
        go_ref[...] = g
        d_ref[...] = delta
        nm_ref[...] = m_new
        nv_ref[...] = v_new

    blk = pl.BlockSpec((tr, c), lambda i, chip_ref: (i, 0))
    out = jax.ShapeDtypeStruct((r, c), F32)
    return pl.pallas_call(
        body, name=name,
        grid_spec=pltpu.PrefetchScalarGridSpec(
            num_scalar_prefetch=1, grid=(r // tr,),
            in_specs=[pl.BlockSpec((None, tr, c), lambda i, chip_ref: (chip_ref[0], i, 0)),
                      pl.BlockSpec((3, tr, c), lambda i, chip_ref: (0, i, 0)), blk, blk, blk],
            out_specs=[blk, blk, blk, blk]),
        out_shape=[out, out, out, out],
        compiler_params=_params(("parallel",)),
    )(chip, sums, got, w, m, v)


def _adamw_replicated(parts, w, m, v):
    p, r, c = parts.shape

    def body(p_ref, w_ref, m_ref, v_ref, g_ref, d_ref, nm_ref, nv_ref):
        g = p_ref[0]
        for i in range(1, p):
            g = g + p_ref[i]
        delta, m_new, v_new = _adamw_math(g, w_ref[...], m_ref[...], v_ref[...])
        g_ref[...] = g
        d_ref[...] = delta
        nm_ref[...] = m_new
        nv_ref[...] = v_new

    blk = pl.BlockSpec((r, c), lambda i: (0, 0))
    out = jax.ShapeDtypeStruct((r, c), F32)
    return pl.pallas_call(
        body, name="adamw_replicated", grid=(1,),
        in_specs=[pl.BlockSpec((p, r, c), lambda i: (0, 0, 0)), blk, blk, blk],
        out_specs=[blk, blk, blk, blk], out_shape=[out, out, out, out],
        compiler_params=_params(("arbitrary",)),
    )(parts, w, m, v)


SHARDED_NAMES = ("w_in", "ml_conv_w", "w_out", "ca_wq", "ca_wkv", "ca_wo", "ffn_w_up", "ffn_conv_w", "ffn_w_down")
SMALL_NAMES = ("b_in", "hg_lb_logits", "hg_norm_w", "ml_conv_b", "ml_norm_w", "ln1_g", "ln1_b",
               "ln2_g", "ln2_b", "ffn_conv_b", "ln3_g", "ln3_b")
WEIGHT_NAMES = ("w_in", "b_in", "hg_lb_logits", "hg_norm_w", "ml_conv_w", "ml_conv_b", "ml_norm_w", "w_out",
                "ln1_g", "ln1_b", "ca_wq", "ca_wkv", "ca_wo", "ln2_g", "ln2_b", "ffn_w_up", "ffn_conv_w",
                "ffn_conv_b", "ffn_w_down", "ln3_g", "ln3_b")
PAD_TO = {"w_in": W_IN_SHARD_P, "ffn_w_up": UP_SHARD_P, "ffn_conv_w": UP_SHARD_P}
SMALL_ROWS = 24
SMALL_W = D_MODEL


def _shard_2d(name, block):
    t = block[0]
    if name in PAD_TO:
        t = jnp.pad(t, ((0, 0), (0, PAD_TO[name] - t.shape[1])))
    return t


def _shard_like(name, t, like):
    return t[:, :like.shape[2]][None]


def _pad_cols(t, width):
    return jnp.pad(t, ((0, 0), (0, width - t.shape[1])))


def _whole_weights(g, small):
    w = dict(small)
    w_in = jnp.concatenate([g["w_in"][j, :, :W_IN_SHARD] for j in range(N_DEV)], axis=1)
    w["w_in_main"] = w_in[:, :D_IN_MAIN]
    w["w_in_gate"] = _pad_cols(w_in[:, D_IN_MAIN:], LANES)
    w["b_in_main"] = small["b_in"][:, :D_IN_MAIN]
    w["b_in_gate"] = _pad_cols(small["b_in"][:, D_IN_MAIN:], LANES)
    for n in ("w_out", "ca_wq", "ca_wo"):
        w[n] = g[n].reshape(D_MODEL, D_MODEL)
    w["ca_wkv"] = g["ca_wkv"]
    w["ffn_w_up"] = g["ffn_w_up"]
    down = g["ffn_w_down"].reshape(N_DEV // 2, UP_SHARD, D_MODEL)
    w["ffn_w_down"] = jnp.pad(down, ((0, 0), (0, UP_SHARD_P - UP_SHARD), (0, 0))).reshape(D_FF_P, D_MODEL)
    w["ml_conv_w"] = jnp.transpose(g["ml_conv_w"], (1, 0, 2)).reshape(ML_CONV, 2 * D_GROUP)
    w["ffn_conv_w"] = jnp.transpose(g["ffn_conv_w"], (1, 0, 2)).reshape(FFN_CONV, D_UP_P)
    w["ffn_conv_b"] = _pad_cols(small["ffn_conv_b"].reshape(N_DEV, UP_SHARD), UP_SHARD_P).reshape(1, D_UP_P)
    return w


def _owner_stacks(grads):
    out = {}
    w_in = jnp.concatenate([grads["w_in_main"], grads["w_in_gate"][:, :D_IN - D_IN_MAIN]], axis=1)
    out["w_in"] = jnp.stack([_pad_cols(w_in[:, j * W_IN_SHARD:(j + 1) * W_IN_SHARD], W_IN_SHARD_P)
                             for j in range(N_DEV)])
    for n in ("w_out", "ca_wq", "ca_wo"):
        out[n] = grads[n].reshape(N_DEV, D_MODEL // N_DEV, D_MODEL)
    out["ca_wkv"] = grads["ca_wkv"]
    out["ffn_w_up"] = grads["ffn_w_up"]
    down = grads["ffn_w_down"].reshape(N_DEV // 2, UP_SHARD_P, D_MODEL)[:, :UP_SHARD]
    out["ffn_w_down"] = down.reshape(N_DEV, D_FF // N_DEV, D_MODEL)
    out["ml_conv_w"] = jnp.transpose(grads["ml_conv_w"].reshape(ML_CONV, N_DEV, LANES), (1, 0, 2))
    out["ffn_conv_w"] = jnp.transpose(grads["ffn_conv_w"].reshape(FFN_CONV, N_DEV, UP_SHARD_P), (1, 0, 2))
    return out


def _small_grads(grads):
    out = {n: grads[n] for n in SMALL_NAMES if n in grads}
    out["b_in"] = jnp.concatenate([grads["b_in_main"], grads["b_in_gate"][:, :D_IN - D_IN_MAIN]], axis=1)
    out["ffn_conv_b"] = grads["ffn_conv_b"].reshape(N_DEV, UP_SHARD_P)[:, :UP_SHARD].reshape(1, D_UP)
    return out


def _pack_small(p, extra=None):
    flat = [p[n].reshape(-1) for n in SMALL_NAMES]
    if extra is not None:
        flat.append(extra.reshape(-1))
    flat = jnp.concatenate(flat)
    return jnp.pad(flat, (0, SMALL_ROWS * SMALL_W - flat.shape[0])).reshape(SMALL_ROWS, SMALL_W)


def _unpack_small(slab, like):
    out = {}
    flat = slab.reshape(-1)
    o = 0
    for n in SMALL_NAMES:
        out[n] = flat[o:o + like[n].size].reshape(like[n].shape)
        o += like[n].size
    return out, flat[o]


def kernel(x, mem, w_in, b_in, hg_lb_logits, hg_norm_w, ml_conv_w, ml_conv_b, ml_norm_w, w_out, ln1_g, ln1_b, ca_wq, ca_wkv, ca_wo, ln2_g, ln2_b, ffn_w_up, ffn_conv_w, ffn_conv_b, ffn_w_down, ln3_g, ln3_b, loss_target, m_w_in, m_b_in, m_hg_lb_logits, m_hg_norm_w, m_ml_conv_w, m_ml_conv_b, m_ml_norm_w, m_w_out, m_ln1_g, m_ln1_b, m_ca_wq, m_ca_wkv, m_ca_wo, m_ln2_g, m_ln2_b, m_ffn_w_up, m_ffn_conv_w, m_ffn_conv_b, m_ffn_w_down, m_ln3_g, m_ln3_b, v_w_in, v_b_in, v_hg_lb_logits, v_hg_norm_w, v_ml_conv_w, v_ml_conv_b, v_ml_norm_w, v_w_out, v_ln1_g, v_ln1_b, v_ca_wq, v_ca_wkv, v_ca_wo, v_ln2_g, v_ln2_b, v_ffn_w_up, v_ffn_conv_w, v_ffn_conv_b, v_ffn_w_down, v_ln3_g, v_ln3_b):
    params = dict(w_in=w_in, b_in=b_in, hg_lb_logits=hg_lb_logits, hg_norm_w=hg_norm_w, ml_conv_w=ml_conv_w,
                  ml_conv_b=ml_conv_b, ml_norm_w=ml_norm_w, w_out=w_out, ln1_g=ln1_g, ln1_b=ln1_b, ca_wq=ca_wq,
                  ca_wkv=ca_wkv, ca_wo=ca_wo, ln2_g=ln2_g, ln2_b=ln2_b, ffn_w_up=ffn_w_up, ffn_conv_w=ffn_conv_w,
                  ffn_conv_b=ffn_conv_b, ffn_w_down=ffn_w_down, ln3_g=ln3_g, ln3_b=ln3_b)
    mom1 = dict(w_in=m_w_in, b_in=m_b_in, hg_lb_logits=m_hg_lb_logits, hg_norm_w=m_hg_norm_w,
                ml_conv_w=m_ml_conv_w, ml_conv_b=m_ml_conv_b, ml_norm_w=m_ml_norm_w, w_out=m_w_out, ln1_g=m_ln1_g,
                ln1_b=m_ln1_b, ca_wq=m_ca_wq, ca_wkv=m_ca_wkv, ca_wo=m_ca_wo, ln2_g=m_ln2_g, ln2_b=m_ln2_b,
                ffn_w_up=m_ffn_w_up, ffn_conv_w=m_ffn_conv_w, ffn_conv_b=m_ffn_conv_b, ffn_w_down=m_ffn_w_down,
                ln3_g=m_ln3_g, ln3_b=m_ln3_b)
    mom2 = dict(w_in=v_w_in, b_in=v_b_in, hg_lb_logits=v_hg_lb_logits, hg_norm_w=v_hg_norm_w,
                ml_conv_w=v_ml_conv_w, ml_conv_b=v_ml_conv_b, ml_norm_w=v_ml_norm_w, w_out=v_w_out, ln1_g=v_ln1_g,
                ln1_b=v_ln1_b, ca_wq=v_ca_wq, ca_wkv=v_ca_wkv, ca_wo=v_ca_wo, ln2_g=v_ln2_g, ln2_b=v_ln2_b,
                ffn_w_up=v_ffn_w_up, ffn_conv_w=v_ffn_conv_w, ffn_conv_b=v_ffn_conv_b, ffn_w_down=v_ffn_w_down,
                ln3_g=v_ln3_g, ln3_b=v_ln3_b)

    shards = {n: _shard_2d(n, params[n]) for n in SHARDED_NAMES}
    send = [shards[n] if "conv" in n else shards[n].astype(BF16) for n in SHARDED_NAMES]
    gathered = dict(zip(SHARDED_NAMES, _all_gather_two_level(send, "weights_all_gather")))
    whole = _whole_weights(gathered, {n: params[n] for n in SMALL_NAMES})

    loss, grad_x, grads = _local_step(x[0], mem[0], loss_target[0], whole)

    _, y_idx, c_idx = _coords()
    x_idx = lax.axis_index("x")
    core = jnp.reshape(c_idx, (1,)).astype(jnp.int32)
    chip = jnp.reshape(2 * x_idx + y_idx, (1,)).astype(jnp.int32)
    stacks = _owner_stacks(grads)
    from_sibling = _exchange_with_sibling([stacks[n] for n in SHARDED_NAMES])
    chip_sums = [_add_sibling(core, stacks[n], got, "grad_add_" + n) for n, got in zip(SHARDED_NAMES, from_sibling)]
    from_chips = _exchange_with_chips(chip_sums)
    sharded_out = {}
    for n, sums, got in zip(SHARDED_NAMES, chip_sums, from_chips):
        res = _adamw_sharded(chip, sums, got, shards[n], _shard_2d(n, mom1[n]), _shard_2d(n, mom2[n]), "adamw_" + n)
        sharded_out[n] = [_shard_like(n, t, params[n]) for t in res]
    small_parts = _all_gather_direct(_pack_small(_small_grads(grads), loss), "small_all_gather")
    small_res = _adamw_replicated(small_parts, _pack_small(params), _pack_small(mom1), _pack_small(mom2))

    outs = []
    total_loss = None
    for k in range(4):
        small, extra = _unpack_small(small_res[k], params)
        if total_loss is None:
            total_loss = extra
        outs.extend(sharded_out[n][k] if n in sharded_out else small[n] for n in WEIGHT_NAMES)
    return (total_loss, grad_x[None], *outs)
```

```python
import jax
import jax.numpy as jnp
from jax import lax
from jax.experimental import pallas as pl
from jax.experimental.pallas import tpu as pltpu

F32 = jnp.float32
BF16 = jnp.bfloat16
HIGHEST = lax.Precision.HIGHEST
MESH = pl.DeviceIdType.MESH

N_DEV = 8
N_CHIPS = 4
D_MODEL = 1024
N_MEM = 256
N_HEADS = 4
D_HEAD = 128
D_GROUP = N_HEADS * D_HEAD
CHUNK = 64
ML_CONV = 4
FFN_CONV = 3
D_FF = 2816
D_UP = 2 * D_FF
CA_HEADS = 4
CA_DH = D_MODEL // CA_HEADS
LANES = 128
SUBLANES = 8
D_IN = 8 * D_GROUP + 2 * N_HEADS
D_IN_MAIN = 8 * D_GROUP
W_IN_SHARD = D_IN // N_DEV
W_IN_SHARD_P = 640
UP_SHARD = D_UP // N_DEV
UP_SHARD_P = 768
D_UP_P = N_DEV * UP_SHARD_P
D_FF_P = D_UP_P // 2
ALPHA = 2.0 ** 0.25
LN_EPS = 1e-5
NEG_BIG = -1e30
ADAM_LR = 0.001
ADAM_B1 = 0.9
ADAM_B2 = 0.999
ADAM_EPS = 1e-08
ADAM_WD = 0.01
ADAM_STEP = 10
VMEM_LIMIT = 56 * 1024 * 1024

SEG_HQ, SEG_HF, SEG_HI, SEG_HG, SEG_MQ, SEG_MK, SEG_MV, SEG_MO = (4 * i for i in range(8))


def _params(sem):
    return pltpu.CompilerParams(dimension_semantics=sem, vmem_limit_bytes=VMEM_LIMIT)


def _dg(a, b, ca, cb, precision=None):
    return lax.dot_general(a, b, (((ca,), (cb,)), ((), ())), precision=precision,
                           preferred_element_type=F32)


def _nn_raw(a, b):
    return _dg(a.astype(BF16), b.astype(BF16), 1, 0)


def _nt_raw(a, b):
    return _dg(a.astype(BF16), b.astype(BF16), 1, 1)


def _tn_raw(a, b):
    return _dg(a.astype(BF16), b.astype(BF16), 0, 0)


@jax.custom_vjp
def _nn(a, b):
    return _nn_raw(a, b)


_nn.defvjp(lambda a, b: (_nn_raw(a, b), (a, b)),
           lambda res, g: (_nt_raw(g, res[1]), _tn_raw(res[0], g)))


@jax.custom_vjp
def _nt(a, b):
    return _nt_raw(a, b)


_nt.defvjp(lambda a, b: (_nt_raw(a, b), (a, b)),
           lambda res, g: (_nn_raw(g, res[1]), _tn_raw(g, res[0])))


@jax.custom_vjp
def _tn(a, b):
    return _tn_raw(a, b)


_tn.defvjp(lambda a, b: (_tn_raw(a, b), (a, b)),
           lambda res, g: (_nt_raw(res[1], g), _nn_raw(res[0], g)))


def _layer_norm(z, g, b):
    mu = jnp.mean(z, axis=-1, keepdims=True)
    var = jnp.mean(jnp.square(z - mu), axis=-1, keepdims=True)
    return (z - mu) * lax.rsqrt(var + LN_EPS) * g + b


def _matmul_nn(a, w, bias, tm, tn, name):
    m, k = a.shape
    if w.ndim == 3:
        n = w.shape[0] * w.shape[2]
        assert tn == w.shape[2]
        w_spec = pl.BlockSpec((None, k, tn), lambda i, j: (j, 0, 0))
    else:
        n = w.shape[1]
        w_spec = pl.BlockSpec((k, tn), lambda i, j: (0, j))

    def body(*refs):
        a_ref, w_ref = refs[0], refs[1]
        o_ref = refs[-1]
        acc = _nn_raw(a_ref[...], w_ref[...])
        if bias is not None:
            acc = acc + refs[2][...]
        o_ref[...] = acc

    in_specs = [pl.BlockSpec((tm, k), lambda i, j: (i, 0)), w_spec]
    args = [a, w]
    if bias is not None:
        in_specs.append(pl.BlockSpec((1, tn), lambda i, j: (0, j)))
        args.append(bias)
    return pl.pallas_call(
        body, name=name, grid=(m // tm, n // tn), in_specs=in_specs,
        out_specs=pl.BlockSpec((tm, tn), lambda i, j: (i, j)),
        out_shape=jax.ShapeDtypeStruct((m, n), F32),
        compiler_params=_params(("parallel", "parallel")),
    )(*args)


def _matmul_nt(pairs, add, scale, tm, tk, name):
    m = pairs[0][0].shape[0]
    k = pairs[0][1].shape[-2]
    groups = []
    in_specs, args = [], []
    for pair in pairs:
        d, w = pair[0], pair[1]
        in_specs.append(pl.BlockSpec((tm, d.shape[1]), lambda i, j: (i, 0)))
        if w.ndim == 3:
            g = d.shape[1] // w.shape[2]
            blk = pair[2] // g
            in_specs.append(pl.BlockSpec((g, tk, w.shape[2]), lambda i, j, blk=blk: (blk, j, 0)))
            groups.append((g, w.shape[2]))
        else:
            in_specs.append(pl.BlockSpec((tk, w.shape[1]), lambda i, j: (j, 0)))
            groups.append(None)
        args += [d, w]
    if add is not None:
        in_specs.append(pl.BlockSpec((tm, tk), lambda i, j: (i, j)))
        args.append(add)

    def body(*refs):
        o_ref = refs[-1]
        acc = None
        for p, grp in enumerate(groups):
            d_ref, w_ref = refs[2 * p], refs[2 * p + 1]
            if grp is None:
                terms = [_nt_raw(d_ref[...], w_ref[...])]
            else:
                terms = [_nt_raw(d_ref[:, g * grp[1]:(g + 1) * grp[1]], w_ref[g]) for g in range(grp[0])]
            for t in terms:
                acc = t if acc is None else acc + t
        if add is not None:
            acc = acc + scale * refs[2 * len(groups)][...]
        o_ref[...] = acc

    return pl.pallas_call(
        body, name=name, grid=(m // tm, k // tk), in_specs=in_specs,
        out_specs=pl.BlockSpec((tm, tk), lambda i, j: (i, j)),
        out_shape=jax.ShapeDtypeStruct((m, k), F32),
        compiler_params=_params(("parallel", "parallel")),
    )(*args)


def _matmul_tn(a, b, tm, tn, tt, name, shards=None, shard0=0, into=None, colsum=False):
    t, m = a.shape
    n = b.shape[1]
    assert not colsum or tm == m
    n_in = 2 + (into is not None)

    def body(*refs):
        a_ref, b_ref = refs[0], refs[1]
        o_ref = refs[n_in]
        first = pl.program_id(2) == 0

        @pl.when(first)
        def _():
            o_ref[...] = jnp.zeros_like(o_ref)

        o_ref[...] += _tn_raw(a_ref[...], b_ref[...])
        if colsum:
            s_ref = refs[n_in + 1]

            @pl.when(first)
            def _():
                s_ref[...] = jnp.zeros_like(s_ref)

            s_ref[...] += jnp.sum(b_ref[...], axis=0, keepdims=True)

    in_specs = [pl.BlockSpec((tt, tm), lambda i, j, kk: (kk, i)), pl.BlockSpec((tt, tn), lambda i, j, kk: (kk, j))]
    args = [a, b]
    aliases = {}
    if into is not None:
        in_specs.append(pl.BlockSpec(memory_space=pl.ANY))
        args.append(into)
        aliases = {2: 0}
    if shards is None:
        out_specs = [pl.BlockSpec((tm, tn), lambda i, j, kk: (i, j))]
        out_shape = [jax.ShapeDtypeStruct((m, n), F32)]
    else:
        out_specs = [pl.BlockSpec((None, tm, tn), lambda i, j, kk: (shard0 + j, i, 0))]
        out_shape = [jax.ShapeDtypeStruct((shards, m, tn), F32)]
    if colsum:
        out_specs.append(pl.BlockSpec((1, tn), lambda i, j, kk: (0, j)))
        out_shape.append(jax.ShapeDtypeStruct((1, n), F32))
    res = pl.pallas_call(
        body, name=name, grid=(m // tm, n // tn, t // tt), in_specs=in_specs, out_specs=out_specs,
        out_shape=out_shape, input_output_aliases=aliases,
        compiler_params=_params(("parallel", "parallel", "arbitrary")),
    )(*args)
    return res if colsum else res[0]


ROW_TILE = 512


def _conv_fwd_tile(pad_ref, w_ref, b_ref, r0, rows, taps):
    acc = b_ref[...]
    for j in range(taps):
        acc = acc + pad_ref[pl.ds(SUBLANES - (taps - 1 - j) + r0, rows), :] * w_ref[j:j + 1, :]
    return acc


def _conv_bwd_tile(dpad_ref, w_ref, r0, rows, taps):
    acc = None
    for j in range(taps):
        term = dpad_ref[pl.ds(r0 + (taps - 1 - j), rows), :] * w_ref[j:j + 1, :]
        acc = term if acc is None else acc + term
    return acc


def _conv_grads_tile(pad_ref, dpad_ref, dx_ref, w_ref, dws, r0, rows, taps):
    dx_ref[r0:r0 + rows, :] = _conv_bwd_tile(dpad_ref, w_ref, r0, rows, taps)
    d_pre = dpad_ref[r0:r0 + rows, :]
    for j in range(taps):
        xs = pad_ref[pl.ds(SUBLANES - (taps - 1 - j) + r0, rows), :]
        dws[j] = dws[j] + jnp.sum(d_pre * xs, axis=0, keepdims=True)


def _ml_conv_fwd(proj, conv_w, conv_b):
    s = proj.shape[0]
    nblk = 2 * D_GROUP // LANES

    def body(x_ref, w_ref, b_ref, o_ref, pad_ref):
        pad_ref[0:SUBLANES, :] = jnp.zeros((SUBLANES, LANES), F32)
        pad_ref[SUBLANES:, :] = x_ref[...]
        for r0 in range(0, s, ROW_TILE):
            rows = min(ROW_TILE, s - r0)
            o_ref[r0:r0 + rows, :] = jax.nn.silu(_conv_fwd_tile(pad_ref, w_ref, b_ref, r0, rows, ML_CONV))

    return pl.pallas_call(
        body, name="ml_conv_fwd", grid=(nblk,),
        in_specs=[pl.BlockSpec((s, LANES), lambda j: (0, SEG_MQ + j)),
                  pl.BlockSpec((ML_CONV, LANES), lambda j: (0, j)),
                  pl.BlockSpec((1, LANES), lambda j: (0, j))],
        out_specs=pl.BlockSpec((s, LANES), lambda j: (0, j)),
        out_shape=jax.ShapeDtypeStruct((s, 2 * D_GROUP), F32),
        scratch_shapes=[pltpu.VMEM((s + SUBLANES, LANES), F32)],
        compiler_params=_params(("parallel",)),
    )(proj, conv_w, conv_b)


def _ml_conv_bwd(proj, conv_w, conv_b, d_qk, d_proj):
    s = proj.shape[0]
    nblk = 2 * D_GROUP // LANES

    def body(x_ref, w_ref, b_ref, dy_ref, _, dx_ref, dw_ref, db_ref, pad_ref, dpad_ref):
        pad_ref[0:SUBLANES, :] = jnp.zeros((SUBLANES, LANES), F32)
        pad_ref[SUBLANES:, :] = x_ref[...]
        dpad_ref[s:, :] = jnp.zeros((SUBLANES, LANES), F32)
        db = jnp.zeros((1, LANES), F32)
        for r0 in range(0, s, ROW_TILE):
            rows = min(ROW_TILE, s - r0)
            pre = _conv_fwd_tile(pad_ref, w_ref, b_ref, r0, rows, ML_CONV)
            _, vjp = jax.vjp(jax.nn.silu, pre)
            d_pre, = vjp(dy_ref[r0:r0 + rows, :])
            dpad_ref[r0:r0 + rows, :] = d_pre
            db = db + jnp.sum(d_pre, axis=0, keepdims=True)
        db_ref[...] = db
        dws = [jnp.zeros((1, LANES), F32) for _ in range(ML_CONV)]
        for r0 in range(0, s, ROW_TILE):
            _conv_grads_tile(pad_ref, dpad_ref, dx_ref, w_ref, dws, r0, min(ROW_TILE, s - r0), ML_CONV)
        for j in range(ML_CONV):
            dw_ref[j:j + 1, :] = dws[j]

    return pl.pallas_call(
        body, name="ml_conv_bwd", grid=(nblk,),
        in_specs=[pl.BlockSpec((s, LANES), lambda j: (0, SEG_MQ + j)),
                  pl.BlockSpec((ML_CONV, LANES), lambda j: (0, j)),
                  pl.BlockSpec((1, LANES), lambda j: (0, j)),
                  pl.BlockSpec((s, LANES), lambda j: (0, j)),
                  pl.BlockSpec(memory_space=pl.ANY)],
        out_specs=[pl.BlockSpec((s, LANES), lambda j: (0, SEG_MQ + j)),
                   pl.BlockSpec((ML_CONV, LANES), lambda j: (0, j)),
                   pl.BlockSpec((1, LANES), lambda j: (0, j))],
        out_shape=[jax.ShapeDtypeStruct(d_proj.shape, F32),
                   jax.ShapeDtypeStruct((ML_CONV, 2 * D_GROUP), F32),
                   jax.ShapeDtypeStruct((1, 2 * D_GROUP), F32)],
        input_output_aliases={4: 0},
        scratch_shapes=[pltpu.VMEM((s + SUBLANES, LANES), F32), pltpu.VMEM((s + SUBLANES, LANES), F32)],
        compiler_params=_params(("parallel",)),
    )(proj, conv_w, conv_b, d_qk, d_proj)


def _gelu_mul(a, b):
    return jax.nn.gelu(a) * b


FFN_BLOCKS = D_FF_P // LANES


def _ffn_conv_fwd(u, conv_w, conv_b):
    s = u.shape[0]

    def body(g_ref, v_ref, wg_ref, wv_ref, bg_ref, bv_ref, o_ref, gpad_ref, vpad_ref):
        for pad_ref, x_ref in ((gpad_ref, g_ref), (vpad_ref, v_ref)):
            pad_ref[0:SUBLANES, :] = jnp.zeros((SUBLANES, LANES), F32)
            pad_ref[SUBLANES:, :] = x_ref[...]
        for r0 in range(0, s, ROW_TILE):
            rows = min(ROW_TILE, s - r0)
            ug = _conv_fwd_tile(gpad_ref, wg_ref, bg_ref, r0, rows, FFN_CONV)
            uv = _conv_fwd_tile(vpad_ref, wv_ref, bv_ref, r0, rows, FFN_CONV)
            o_ref[r0:r0 + rows, :] = _gelu_mul(ug, uv)

    col = lambda off: (lambda j: (0, off + j))
    return pl.pallas_call(
        body, name="ffn_conv_fwd", grid=(FFN_BLOCKS,),
        in_specs=[pl.BlockSpec((s, LANES), col(0)), pl.BlockSpec((s, LANES), col(FFN_BLOCKS)),
                  pl.BlockSpec((FFN_CONV, LANES), col(0)), pl.BlockSpec((FFN_CONV, LANES), col(FFN_BLOCKS)),
                  pl.BlockSpec((1, LANES), col(0)), pl.BlockSpec((1, LANES), col(FFN_BLOCKS))],
        out_specs=pl.BlockSpec((s, LANES), col(0)),
        out_shape=jax.ShapeDtypeStruct((s, D_FF_P), F32),
        scratch_shapes=[pltpu.VMEM((s + SUBLANES, LANES), F32), pltpu.VMEM((s + SUBLANES, LANES), F32)],
        compiler_params=_params(("parallel",)),
    )(u, u, conv_w, conv_w, conv_b, conv_b)


def _ffn_conv_bwd(u, conv_w, conv_b, d_h):
    s = u.shape[0]

    def body(g_ref, v_ref, wg_ref, wv_ref, bg_ref, bv_ref, dh_ref,
             dug_ref, duv_ref, dwg_ref, dwv_ref, dbg_ref, dbv_ref,
             gpad_ref, vpad_ref, dgpad_ref, dvpad_ref):
        for pad_ref, x_ref in ((gpad_ref, g_ref), (vpad_ref, v_ref)):
            pad_ref[0:SUBLANES, :] = jnp.zeros((SUBLANES, LANES), F32)
            pad_ref[SUBLANES:, :] = x_ref[...]
        dgpad_ref[s:, :] = jnp.zeros((SUBLANES, LANES), F32)
        dvpad_ref[s:, :] = jnp.zeros((SUBLANES, LANES), F32)
        dbg = jnp.zeros((1, LANES), F32)
        dbv = jnp.zeros((1, LANES), F32)
        for r0 in range(0, s, ROW_TILE):
            rows = min(ROW_TILE, s - r0)
            ug = _conv_fwd_tile(gpad_ref, wg_ref, bg_ref, r0, rows, FFN_CONV)
            uv = _conv_fwd_tile(vpad_ref, wv_ref, bv_ref, r0, rows, FFN_CONV)
            _, vjp = jax.vjp(_gelu_mul, ug, uv)
            d_ug, d_uv = vjp(dh_ref[r0:r0 + rows, :])
            dgpad_ref[r0:r0 + rows, :] = d_ug
            dvpad_ref[r0:r0 + rows, :] = d_uv
            dbg = dbg + jnp.sum(d_ug, axis=0, keepdims=True)
            dbv = dbv + jnp.sum(d_uv, axis=0, keepdims=True)
        dbg_ref[...] = dbg
        dbv_ref[...] = dbv
        for pad_ref, dpad_ref, w_ref, dx_ref, dw_ref in ((gpad_ref, dgpad_ref, wg_ref, dug_ref, dwg_ref),
                                                         (vpad_ref, dvpad_ref, wv_ref, duv_ref, dwv_ref)):
            dws = [jnp.zeros((1, LANES), F32) for _ in range(FFN_CONV)]
            for r0 in range(0, s, ROW_TILE):
                _conv_grads_tile(pad_ref, dpad_ref, dx_ref, w_ref, dws, r0, min(ROW_TILE, s - r0), FFN_CONV)
            for j in range(FFN_CONV):
                dw_ref[j:j + 1, :] = dws[j]

    col = lambda off: (lambda j: (0, off + j))
    seq = pl.BlockSpec((s, LANES), col(0))
    return pl.pallas_call(
        body, name="ffn_conv_bwd", grid=(FFN_BLOCKS,),
        in_specs=[pl.BlockSpec((s, LANES), col(0)), pl.BlockSpec((s, LANES), col(FFN_BLOCKS)),
                  pl.BlockSpec((FFN_CONV, LANES), col(0)), pl.BlockSpec((FFN_CONV, LANES), col(FFN_BLOCKS)),
                  pl.BlockSpec((1, LANES), col(0)), pl.BlockSpec((1, LANES), col(FFN_BLOCKS)), seq],
        out_specs=[seq, seq, pl.BlockSpec((FFN_CONV, LANES), col(0)), pl.BlockSpec((FFN_CONV, LANES), col(0)),
                   pl.BlockSpec((1, LANES), col(0)), pl.BlockSpec((1, LANES), col(0))],
        out_shape=[jax.ShapeDtypeStruct((s, D_FF_P), F32), jax.ShapeDtypeStruct((s, D_FF_P), F32),
                   jax.ShapeDtypeStruct((FFN_CONV, D_FF_P), F32), jax.ShapeDtypeStruct((FFN_CONV, D_FF_P), F32),
                   jax.ShapeDtypeStruct((1, D_FF_P), F32), jax.ShapeDtypeStruct((1, D_FF_P), F32)],
        scratch_shapes=[pltpu.VMEM((s + SUBLANES, LANES), F32) for _ in range(4)],
        compiler_params=_params(("parallel",)),
    )(u, u, conv_w, conv_w, conv_b, conv_b, d_h)


def _chunk_masks(c):
    row = lax.broadcasted_iota(jnp.int32, (c, c), 0)
    col = lax.broadcasted_iota(jnp.int32, (c, c), 1)
    return row, col


def _hg_step(hq, hf, hi, hgate, l0, l1, nw, st):
    c = hq.shape[0]
    row, col = _chunk_masks(c)
    mask = col <= row
    mx = lax.stop_gradient(jnp.maximum(l0, l1))
    e0 = jnp.exp(l0 - mx)
    e1 = jnp.exp(l1 - mx)
    lb = e0 / (e0 + e1)
    sig = jax.nn.sigmoid(hf)
    lf = jnp.log(lb + (1.0 - lb) * sig)
    k = (1.0 - lb) * jax.nn.sigmoid(-hf)
    q = jax.nn.silu(hq)
    b = _dg(mask.astype(F32), lf, 1, 0, HIGHEST)
    rid = lax.broadcasted_iota(jnp.int32, b.shape, 0)
    b_ref = jnp.sum(jnp.where(rid == c // 2 - 1, b, 0.0), axis=0, keepdims=True)
    b_last = jnp.sum(jnp.where(rid == c - 1, b, 0.0), axis=0, keepdims=True)
    attn = _nt(q * jnp.exp(b - b_ref), k * jnp.exp(b_ref - b))
    attn = jnp.where(mask, attn, 0.0)
    o = _nn(attn, hi) + _nt(q * jnp.exp(b), st)
    st_new = jnp.exp(b_last) * st + _tn(hi, k * jnp.exp(b_last - b))
    y = o * lax.rsqrt(jnp.mean(o * o, axis=-1, keepdims=True) + LN_EPS) * nw * jax.nn.silu(hgate)
    return y, st_new


def _head(ref, seg, h):
    lo = seg * D_GROUP + h * D_HEAD
    return ref[:, lo:lo + D_HEAD]


def _hgrn2_fwd(proj, logits, norm_w):
    s = proj.shape[0]
    nc = s // CHUNK

    def body(p_ref, lg_ref, nw_ref, y_ref, st_out_ref, st_scr):
        @pl.when(pl.program_id(0) == 0)
        def _():
            st_scr[...] = jnp.zeros_like(st_scr)

        for h in range(N_HEADS):
            lo = h * D_HEAD
            st = st_scr[h]
            st_out_ref[h] = st
            y, st_new = _hg_step(_head(p_ref, 0, h), _head(p_ref, 1, h), _head(p_ref, 2, h), _head(p_ref, 3, h),
                                 lg_ref[0:1, lo:lo + D_HEAD], lg_ref[1:2, lo:lo + D_HEAD],
                                 nw_ref[:, lo:lo + D_HEAD], st)
            y_ref[:, lo:lo + D_HEAD] = y
            st_scr[h] = st_new

    return pl.pallas_call(
        body, name="hgrn2_fwd", grid=(nc,),
        in_specs=[pl.BlockSpec((CHUNK, 4 * D_GROUP), lambda c: (c, 0)),
                  pl.BlockSpec((2, D_GROUP), lambda c: (0, 0)),
                  pl.BlockSpec((1, D_GROUP), lambda c: (0, 0))],
        out_specs=[pl.BlockSpec((CHUNK, D_GROUP), lambda c: (c, 0)),
                   pl.BlockSpec((None, N_HEADS, D_HEAD, D_HEAD), lambda c: (c, 0, 0, 0))],
        out_shape=[jax.ShapeDtypeStruct((s, 2 * D_GROUP), F32),
                   jax.ShapeDtypeStruct((nc, N_HEADS, D_HEAD, D_HEAD), F32)],
        scratch_shapes=[pltpu.VMEM((N_HEADS, D_HEAD, D_HEAD), F32)],
        compiler_params=_params(("arbitrary",)),
    )(proj, logits, norm_w)


def _hgrn2_bwd(proj, logits, norm_w, states, d_y):
    s = proj.shape[0]
    nc = s // CHUNK

    def body(p_ref, lg_ref, nw_ref, st_ref, dy_ref, dp_ref, dl_ref, dnw_ref, dst_scr):
        @pl.when(pl.program_id(0) == 0)
        def _():
            dst_scr[...] = jnp.zeros_like(dst_scr)
            dl_ref[...] = jnp.zeros_like(dl_ref)
            dnw_ref[...] = jnp.zeros_like(dnw_ref)

        for h in range(N_HEADS):
            lo = h * D_HEAD
            _, vjp = jax.vjp(_hg_step, _head(p_ref, 0, h), _head(p_ref, 1, h), _head(p_ref, 2, h),
                             _head(p_ref, 3, h), lg_ref[0:1, lo:lo + D_HEAD], lg_ref[1:2, lo:lo + D_HEAD],
                             nw_ref[:, lo:lo + D_HEAD], st_ref[h])
            d_hq, d_hf, d_hi, d_hg, d_l0, d_l1, d_nw, d_st = vjp((dy_ref[:, lo:lo + D_HEAD], dst_scr[h]))
            for seg, val in enumerate((d_hq, d_hf, d_hi, d_hg)):
                dp_ref[:, seg * D_GROUP + lo:seg * D_GROUP + lo + D_HEAD] = val
            dl_ref[0:1, lo:lo + D_HEAD] += d_l0
            dl_ref[1:2, lo:lo + D_HEAD] += d_l1
            dnw_ref[:, lo:lo + D_HEAD] += d_nw
            dst_scr[h] = d_st

    rev = lambda c: nc - 1 - c
    return pl.pallas_call(
        body, name="hgrn2_bwd", grid=(nc,),
        in_specs=[pl.BlockSpec((CHUNK, 4 * D_GROUP), lambda c: (rev(c), 0)),
                  pl.BlockSpec((2, D_GROUP), lambda c: (0, 0)),
                  pl.BlockSpec((1, D_GROUP), lambda c: (0, 0)),
                  pl.BlockSpec((None, N_HEADS, D_HEAD, D_HEAD), lambda c: (rev(c), 0, 0, 0)),
                  pl.BlockSpec((CHUNK, D_GROUP), lambda c: (rev(c), 0))],
        out_specs=[pl.BlockSpec((CHUNK, 4 * D_GROUP), lambda c: (rev(c), 0)),
                   pl.BlockSpec((2, D_GROUP), lambda c: (0, 0)),
                   pl.BlockSpec((1, D_GROUP), lambda c: (0, 0))],
        out_shape=[jax.ShapeDtypeStruct((s, D_IN_MAIN), F32), jax.ShapeDtypeStruct((2, D_GROUP), F32),
                   jax.ShapeDtypeStruct((1, D_GROUP), F32)],
        scratch_shapes=[pltpu.VMEM((N_HEADS, D_HEAD, D_HEAD), F32)],
        compiler_params=_params(("arbitrary",)),
    )(proj, logits, norm_w, states, d_y)


def _ml_step(qc, kc, v, mo, ig, fr, nw, ct, n, m):
    c = qc.shape[0]
    row, col = _chunk_masks(c)
    mask = col <= row
    eye = col == row
    q = qc * (D_HEAD ** -0.5)
    lf = jax.nn.log_sigmoid(fr)
    to_row = lambda t: jnp.sum(jnp.where(eye, t, 0.0), axis=0, keepdims=True)
    lf_row = to_row(lf)
    ig_row = to_row(ig)
    b_col = jnp.sum(jnp.where(mask, lf_row, 0.0), axis=1, keepdims=True)
    b_row = jnp.sum(jnp.where(row <= col, lf, 0.0), axis=0, keepdims=True)
    g = jnp.sum(lf, axis=0, keepdims=True)
    d = jnp.where(mask, b_col - b_row + ig_row, -jnp.inf)
    inter = b_col + m
    m_t = lax.stop_gradient(jnp.maximum(inter, jnp.max(d, axis=1, keepdims=True)))
    w = jnp.exp(d - m_t)
    sc = _nt(q, kc) * w
    w_inter = jnp.exp(inter - m_t)
    num = _nn(sc, v) + w_inter * _nt(q, ct)
    den = jnp.sum(sc, axis=1, keepdims=True) + w_inter * jnp.sum(q * n, axis=1, keepdims=True)
    h = num / jnp.maximum(jnp.abs(den), jnp.exp(-m_t))
    a = g - b_col + ig
    m_new = lax.stop_gradient(jnp.maximum(g + m, jnp.max(a, axis=0, keepdims=True)))
    decay = jnp.exp(g + m - m_new)
    wk = kc * jnp.exp(a - m_new)
    ct_new = decay * ct + _tn(v, wk)
    n_new = decay * n + jnp.sum(wk, axis=0, keepdims=True)
    mu = jnp.mean(h, axis=-1, keepdims=True)
    var = jnp.mean(jnp.square(h - mu), axis=-1, keepdims=True)
    y = jax.nn.sigmoid(mo) * ((h - mu) * lax.rsqrt(var + LN_EPS) * nw)
    return y, ct_new, n_new, m_new


def _gate_column(gates, lane, idx):
    return jnp.sum(jnp.where(lane == idx, gates, 0.0), axis=1, keepdims=True)


def _mlstm_fwd(qk, proj, gates, norm_w, y):
    s = proj.shape[0]
    nc = s // CHUNK

    def body(qk_ref, vo_ref, g_ref, nw_ref, _, y_ref, ct_out, n_out, m_out, ct_scr, n_scr, m_scr):
        @pl.when(pl.program_id(0) == 0)
        def _():
            ct_scr[...] = jnp.zeros_like(ct_scr)
            n_scr[...] = jnp.zeros_like(n_scr)
            m_scr[...] = jnp.full(m_scr.shape, NEG_BIG, F32)

        gates_blk = g_ref[...]
        lane = lax.broadcasted_iota(jnp.int32, gates_blk.shape, 1)
        for h in range(N_HEADS):
            lo = h * D_HEAD
            ct, n, m = ct_scr[h], n_scr[h], m_scr[h]
            ct_out[h] = ct
            n_out[h] = n
            m_out[h] = m
            yh, ct_new, n_new, m_new = _ml_step(
                _head(qk_ref, 0, h), _head(qk_ref, 1, h), _head(vo_ref, 0, h), _head(vo_ref, 1, h),
                _gate_column(gates_blk, lane, h), _gate_column(gates_blk, lane, N_HEADS + h),
                nw_ref[:, lo:lo + D_HEAD], ct, n, m)
            y_ref[:, lo:lo + D_HEAD] = yh
            ct_scr[h] = ct_new
            n_scr[h] = n_new
            m_scr[h] = m_new

    st = lambda r, w: pl.BlockSpec((None, N_HEADS, r, w), lambda c: (c, 0, 0, 0))
    return pl.pallas_call(
        body, name="mlstm_fwd", grid=(nc,),
        in_specs=[pl.BlockSpec((CHUNK, 2 * D_GROUP), lambda c: (c, 0)),
                  pl.BlockSpec((CHUNK, 2 * D_GROUP), lambda c: (c, 3)),
                  pl.BlockSpec((CHUNK, LANES), lambda c: (c, 0)),
                  pl.BlockSpec((1, D_GROUP), lambda c: (0, 0)),
                  pl.BlockSpec(memory_space=pl.ANY)],
        out_specs=[pl.BlockSpec((CHUNK, D_GROUP), lambda c: (c, 1)),
                   st(D_HEAD, D_HEAD), st(1, D_HEAD), st(1, 1)],
        out_shape=[jax.ShapeDtypeStruct((s, 2 * D_GROUP), F32),
                   jax.ShapeDtypeStruct((nc, N_HEADS, D_HEAD, D_HEAD), F32),
                   jax.ShapeDtypeStruct((nc, N_HEADS, 1, D_HEAD), F32),
                   jax.ShapeDtypeStruct((nc, N_HEADS, 1, 1), F32)],
        input_output_aliases={4: 0},
        scratch_shapes=[pltpu.VMEM((N_HEADS, D_HEAD, D_HEAD), F32), pltpu.VMEM((N_HEADS, 1, D_HEAD), F32),
                        pltpu.VMEM((N_HEADS, 1, 1), F32)],
        compiler_params=_params(("arbitrary",)),
    )(qk, proj, gates, norm_w, y)


def _mlstm_bwd(qk, proj, gates, norm_w, ct_s, n_s, m_s, d_y, d_proj):
    s = proj.shape[0]
    nc = s // CHUNK

    def body(qk_ref, vo_ref, g_ref, nw_ref, ct_ref, n_ref, m_ref, dy_ref, _,
             dp_ref, dqk_ref, dg_ref, dnw_ref, dct_scr, dn_scr):
        @pl.when(pl.program_id(0) == 0)
        def _():
            dct_scr[...] = jnp.zeros_like(dct_scr)
            dn_scr[...] = jnp.zeros_like(dn_scr)
            dnw_ref[...] = jnp.zeros_like(dnw_ref)

        gates_blk = g_ref[...]
        lane = lax.broadcasted_iota(jnp.int32, gates_blk.shape, 1)
        d_gates = jnp.zeros(gates_blk.shape, F32)
        for h in range(N_HEADS):
            lo = h * D_HEAD
            m = m_ref[h]
            step = lambda *a, m=m: _ml_step(*a, m)[:3]
            _, vjp = jax.vjp(step, _head(qk_ref, 0, h), _head(qk_ref, 1, h), _head(vo_ref, 0, h),
                             _head(vo_ref, 1, h), _gate_column(gates_blk, lane, h),
                             _gate_column(gates_blk, lane, N_HEADS + h), nw_ref[:, lo:lo + D_HEAD],
                             ct_ref[h], n_ref[h])
            d_q, d_k, d_v, d_o, d_gi, d_gf, d_nw, d_ct, d_n = vjp((dy_ref[:, lo:lo + D_HEAD], dct_scr[h], dn_scr[h]))
            dqk_ref[:, lo:lo + D_HEAD] = d_q
            dqk_ref[:, D_GROUP + lo:D_GROUP + lo + D_HEAD] = d_k
            dp_ref[:, lo:lo + D_HEAD] = d_v
            dp_ref[:, D_GROUP + lo:D_GROUP + lo + D_HEAD] = d_o
            d_gates = d_gates + jnp.where(lane == h, d_gi, 0.0) + jnp.where(lane == N_HEADS + h, d_gf, 0.0)
            dnw_ref[:, lo:lo + D_HEAD] += d_nw
            dct_scr[h] = d_ct
            dn_scr[h] = d_n
        dg_ref[...] = d_gates

    rev = lambda c: nc - 1 - c
    st = lambda r, w: pl.BlockSpec((None, N_HEADS, r, w), lambda c: (rev(c), 0, 0, 0))
    return pl.pallas_call(
        body, name="mlstm_bwd", grid=(nc,),
        in_specs=[pl.BlockSpec((CHUNK, 2 * D_GROUP), lambda c: (rev(c), 0)),
                  pl.BlockSpec((CHUNK, 2 * D_GROUP), lambda c: (rev(c), 3)),
                  pl.BlockSpec((CHUNK, LANES), lambda c: (rev(c), 0)),
                  pl.BlockSpec((1, D_GROUP), lambda c: (0, 0)),
                  st(D_HEAD, D_HEAD), st(1, D_HEAD), st(1, 1),
                  pl.BlockSpec((CHUNK, D_GROUP), lambda c: (rev(c), 1)),
                  pl.BlockSpec(memory_space=pl.ANY)],
        out_specs=[pl.BlockSpec((CHUNK, 2 * D_GROUP), lambda c: (rev(c), 3)),
                   pl.BlockSpec((CHUNK, 2 * D_GROUP), lambda c: (rev(c), 0)),
                   pl.BlockSpec((CHUNK, LANES), lambda c: (rev(c), 0)),
                   pl.BlockSpec((1, D_GROUP), lambda c: (0, 0))],
        out_shape=[jax.ShapeDtypeStruct(d_proj.shape, F32), jax.ShapeDtypeStruct((s, 2 * D_GROUP), F32),
                   jax.ShapeDtypeStruct((s, LANES), F32), jax.ShapeDtypeStruct((1, D_GROUP), F32)],
        input_output_aliases={8: 0},
        scratch_shapes=[pltpu.VMEM((N_HEADS, D_HEAD, D_HEAD), F32), pltpu.VMEM((N_HEADS, 1, D_HEAD), F32)],
        compiler_params=_params(("arbitrary",)),
    )(qk, proj, gates, norm_w, ct_s, n_s, m_s, d_y, d_proj)


LN_TOKENS = 512
ATT_TOKENS = 256


def _res_ln_fwd(xres, branch, g, b, name):
    s, dm = xres.shape
    tb = min(LN_TOKENS, s)

    def body(x_ref, br_ref, g_ref, b_ref, o_ref):
        o_ref[...] = _layer_norm(ALPHA * x_ref[...] + br_ref[...], g_ref[...], b_ref[...])

    tok = pl.BlockSpec((tb, dm), lambda i: (i, 0))
    vec = pl.BlockSpec((1, dm), lambda i: (0, 0))
    return pl.pallas_call(
        body, name=name, grid=(s // tb,), in_specs=[tok, tok, vec, vec], out_specs=tok,
        out_shape=jax.ShapeDtypeStruct((s, dm), F32), compiler_params=_params(("parallel",)),
    )(xres, branch, g, b)


def _res_ln_bwd(xres, branch, g, b, d_out, name):
    s, dm = xres.shape
    tb = min(LN_TOKENS, s)

    def body(x_ref, br_ref, g_ref, b_ref, do_ref, dz_ref, dg_ref, db_ref):
        @pl.when(pl.program_id(0) == 0)
        def _():
            dg_ref[...] = jnp.zeros_like(dg_ref)
            db_ref[...] = jnp.zeros_like(db_ref)

        z = ALPHA * x_ref[...] + br_ref[...]
        _, vjp = jax.vjp(_layer_norm, z, g_ref[...], b_ref[...])
        d_z, d_g, d_b = vjp(do_ref[...])
        dz_ref[...] = d_z
        dg_ref[...] += d_g
        db_ref[...] += d_b

    tok = pl.BlockSpec((tb, dm), lambda i: (i, 0))
    vec = pl.BlockSpec((1, dm), lambda i: (0, 0))
    return pl.pallas_call(
        body, name=name, grid=(s // tb,), in_specs=[tok, tok, vec, vec, tok], out_specs=[tok, vec, vec],
        out_shape=[jax.ShapeDtypeStruct((s, dm), F32), jax.ShapeDtypeStruct((1, dm), F32),
                   jax.ShapeDtypeStruct((1, dm), F32)],
        compiler_params=_params(("arbitrary",)),
    )(xres, branch, g, b, d_out)


def _loss_tail(xres, branch, g, b, target):
    s, dm = xres.shape
    tb = min(LN_TOKENS, s)

    def loss_fn(z, gg, bb, tgt):
        err = jnp.square(_layer_norm(z, gg, bb) - tgt)
        return 0.5 * jnp.sum(jnp.mean(err, axis=-1, keepdims=True), axis=0, keepdims=True)

    def body(x_ref, br_ref, g_ref, b_ref, t_ref, loss_ref, dz_ref, dg_ref, db_ref):
        @pl.when(pl.program_id(0) == 0)
        def _():
            loss_ref[...] = jnp.zeros_like(loss_ref)
            dg_ref[...] = jnp.zeros_like(dg_ref)
            db_ref[...] = jnp.zeros_like(db_ref)

        z = ALPHA * x_ref[...] + br_ref[...]
        tgt = t_ref[...]
        loss, vjp = jax.vjp(lambda zz, gg, bb: loss_fn(zz, gg, bb, tgt), z, g_ref[...], b_ref[...])
        d_z, d_g, d_b = vjp(jnp.ones((1, 1), F32))
        loss_ref[...] += loss
        dz_ref[...] = d_z
        dg_ref[...] += d_g
        db_ref[...] += d_b

    tok = pl.BlockSpec((tb, dm), lambda i: (i, 0))
    vec = pl.BlockSpec((1, dm), lambda i: (0, 0))
    one = pl.BlockSpec((1, 1), lambda i: (0, 0))
    return pl.pallas_call(
        body, name="loss_tail", grid=(s // tb,), in_specs=[tok, tok, vec, vec, tok],
        out_specs=[one, tok, vec, vec],
        out_shape=[jax.ShapeDtypeStruct((1, 1), F32), jax.ShapeDtypeStruct((s, dm), F32),
                   jax.ShapeDtypeStruct((1, dm), F32), jax.ShapeDtypeStruct((1, dm), F32)],
        compiler_params=_params(("arbitrary",)),
    )(xres, branch, g, b, target)


def _att_head(q, k, v):
    sc = _nt(q, k) * (CA_DH ** -0.5)
    return _nn(jax.nn.softmax(sc, axis=-1), v)


def _att_fwd(q, kv):
    s = q.shape[0]
    tb = min(ATT_TOKENS, s)

    def body(q_ref, kv_ref, o_ref):
        for h in range(CA_HEADS):
            lo = h * CA_DH
            o_ref[:, lo:lo + CA_DH] = _att_head(q_ref[:, lo:lo + CA_DH], kv_ref[:, lo:lo + CA_DH],
                                                kv_ref[:, D_MODEL + lo:D_MODEL + lo + CA_DH])

    tok = pl.BlockSpec((tb, D_MODEL), lambda i: (i, 0))
    return pl.pallas_call(
        body, name="att_fwd", grid=(s // tb,),
        in_specs=[tok, pl.BlockSpec((N_MEM, 2 * D_MODEL), lambda i: (0, 0))], out_specs=tok,
        out_shape=jax.ShapeDtypeStruct((s, D_MODEL), F32), compiler_params=_params(("parallel",)),
    )(q, kv)


def _att_bwd(q, kv, d_o):
    s = q.shape[0]
    tb = min(ATT_TOKENS, s)

    def body(q_ref, kv_ref, do_ref, dq_ref, dkv_ref):
        @pl.when(pl.program_id(0) == 0)
        def _():
            dkv_ref[...] = jnp.zeros_like(dkv_ref)

        for h in range(CA_HEADS):
            lo = h * CA_DH
            vlo = D_MODEL + lo
            _, vjp = jax.vjp(_att_head, q_ref[:, lo:lo + CA_DH], kv_ref[:, lo:lo + CA_DH],
                             kv_ref[:, vlo:vlo + CA_DH])
            d_q, d_k, d_v = vjp(do_ref[:, lo:lo + CA_DH])
            dq_ref[:, lo:lo + CA_DH] = d_q
            dkv_ref[:, lo:lo + CA_DH] += d_k
            dkv_ref[:, vlo:vlo + CA_DH] += d_v

    tok = pl.BlockSpec((tb, D_MODEL), lambda i: (i, 0))
    mem = pl.BlockSpec((N_MEM, 2 * D_MODEL), lambda i: (0, 0))
    return pl.pallas_call(
        body, name="att_bwd", grid=(s // tb,), in_specs=[tok, mem, tok], out_specs=[tok, mem],
        out_shape=[jax.ShapeDtypeStruct((s, D_MODEL), F32), jax.ShapeDtypeStruct((N_MEM, 2 * D_MODEL), F32)],
        compiler_params=_params(("arbitrary",)),
    )(q, kv, d_o)


def _local_step(x, mem, target, w, late_weights=None, on_ffn_grads=None, on_mid_grads=None):
    w = dict(w)
    s = x.shape[0]
    tm = min(512, s)
    tt = min(512, s)
    proj = _matmul_nn(x, w["w_in_main"], w["b_in_main"], tm, 512, "proj")
    gates = _matmul_nn(x, w["w_in_gate"], w["b_in_gate"], tm, LANES, "proj_gates")
    qk = _ml_conv_fwd(proj, w["ml_conv_w"], w["ml_conv_b"])
    y, hg_states = _hgrn2_fwd(proj, w["hg_lb_logits"], w["hg_norm_w"])
    y, ct_s, n_s, m_s = _mlstm_fwd(qk, proj, gates, w["ml_norm_w"], y)
    if late_weights is not None:
        w.update(late_weights(y))
    mix =_matmul_nn(y, w["w_out"], None, tm, D_MODEL, "mix")
    x1 = _res_ln_fwd(x, mix, w["ln1_g"], w["ln1_b"], "ln1_fwd")
    kv = _matmul_nn(mem, w["ca_wkv"], None, N_MEM, CA_DH, "kv")
    q = _matmul_nn(x1, w["ca_wq"], None, tm, D_MODEL, "ca_q")
    att = _att_fwd(q, kv)
    ca = _matmul_nn(att, w["ca_wo"], None, tm, D_MODEL, "ca_out")
    x2 = _res_ln_fwd(x1, ca, w["ln2_g"], w["ln2_b"], "ln2_fwd")
    u = _matmul_nn(x2, w["ffn_w_up"], None, tm, UP_SHARD_P, "ffn_up")
    hid = _ffn_conv_fwd(u, w["ffn_conv_w"], w["ffn_conv_b"])
    ff = _matmul_nn(hid, w["ffn_w_down"], None, tm, D_MODEL, "ffn_down")
    loss, d_z3, d_ln3_g, d_ln3_b = _loss_tail(x2, ff, w["ln3_g"], w["ln3_b"], target)
    grads = {"ln3_g": d_ln3_g, "ln3_b": d_ln3_b}
    grads["ffn_w_down"] = _matmul_tn(hid, d_z3, 1536, D_MODEL, tt, "d_w_down")
    d_hid = _matmul_nt([(d_z3, w["ffn_w_down"])], None, 1.0, tm, 1536, "d_hid")
    d_ug, d_uv, d_cwg, d_cwv, d_cbg, d_cbv = _ffn_conv_bwd(u, w["ffn_conv_w"], w["ffn_conv_b"], d_hid)
    grads["ffn_conv_w"] = jnp.concatenate([d_cwg, d_cwv], axis=-1)
    grads["ffn_conv_b"] = jnp.concatenate([d_cbg, d_cbv], axis=-1)
    d_w_up = _matmul_tn(x2, d_ug, D_MODEL, UP_SHARD_P, tt, "d_w_up_gate", shards=N_DEV)
    grads["ffn_w_up"] = _matmul_tn(x2, d_uv, D_MODEL, UP_SHARD_P, tt, "d_w_up_val", shards=N_DEV,
                                   shard0=N_DEV // 2, into=d_w_up)
    d_x2 = _matmul_nt([(d_ug, w["ffn_w_up"], 0), (d_uv, w["ffn_w_up"], N_DEV // 2)], d_z3, ALPHA,
                      min(256, s), D_MODEL, "d_x2")
    if on_ffn_grads is not None:
        d_x2 = on_ffn_grads(grads, d_x2)
    d_z2, grads["ln2_g"], grads["ln2_b"] = _res_ln_bwd(x1, ca, w["ln2_g"], w["ln2_b"], d_x2, "ln2_bwd")
    grads["ca_wo"] = _matmul_tn(att, d_z2, D_MODEL, D_MODEL, tt, "d_ca_wo")
    d_att = _matmul_nt([(d_z2, w["ca_wo"])], None, 1.0, tm, D_MODEL, "d_att")
    d_q, d_kv = _att_bwd(q, kv, d_att)
    grads["ca_wq"] = _matmul_tn(x1, d_q, D_MODEL, D_MODEL, tt, "d_ca_wq")
    grads["ca_wkv"] = _matmul_tn(mem, d_kv, D_MODEL, CA_DH, N_MEM, "d_ca_wkv", shards=N_DEV)
    d_x1 = _matmul_nt([(d_q, w["ca_wq"])], d_z2, ALPHA, tm, D_MODEL, "d_x1")
    d_z1, grads["ln1_g"], grads["ln1_b"] = _res_ln_bwd(x, mix, w["ln1_g"], w["ln1_b"], d_x1, "ln1_bwd")
    grads["w_out"] = _matmul_tn(y, d_z1, D_MODEL, D_MODEL, tt, "d_w_out")
    if on_mid_grads is not None:
        d_z1 = on_mid_grads(grads, d_z1)
    d_y = _matmul_nt([(d_z1, w["w_out"])], None, 1.0, tm, D_MODEL, "d_y")
    d_proj, grads["hg_lb_logits"], grads["hg_norm_w"] = _hgrn2_bwd(
        proj, w["hg_lb_logits"], w["hg_norm_w"], hg_states, d_y)
    d_proj, d_qk, d_gates, grads["ml_norm_w"] = _mlstm_bwd(
        qk, proj, gates, w["ml_norm_w"], ct_s, n_s, m_s, d_y, d_proj)
    d_proj, grads["ml_conv_w"], grads["ml_conv_b"] = _ml_conv_bwd(
        proj, w["ml_conv_w"], w["ml_conv_b"], d_qk, d_proj)
    grads["w_in_main"], grads["b_in_main"] = _matmul_tn(x, d_proj, D_MODEL, 512, tt, "d_w_in", colsum=True)
    grads["w_in_gate"], grads["b_in_gate"] = _matmul_tn(x, d_gates, D_MODEL, LANES, tt, "d_w_in_gates", colsum=True)
    grad_x = _matmul_nt([(d_proj, w["w_in_main"]), (d_gates, w["w_in_gate"])], d_z1, ALPHA, tm, D_MODEL, "d_x")
    return loss, grad_x, grads


HBM_SPEC = pl.BlockSpec(memory_space=pltpu.HBM)


def _coords():
    return lax.axis_index("x"), lax.axis_index("y"), lax.axis_index("c")


def _other_chips(x, y):
    return [(1 - x, y), (x, 1 - y), (1 - x, 1 - y)]


def _all_gather_two_level(shards, name):
    na = len(shards)

    def body(*refs):
        x_refs, out_refs = refs[:na], refs[na:2 * na]
        send_sems, recv_sems, local_sems = refs[2 * na:]
        x, y, c = _coords()
        me, sibling = (x, y, c), (x, y, 1 - c)
        chips = _other_chips(x, y)

        def copy(a, k, block, to, own=False):
            slot = out_refs[a].at[4 * block[0] + 2 * block[1] + block[2]]
            return pltpu.make_async_remote_copy(
                src_ref=x_refs[a] if own else slot, dst_ref=slot,
                send_sem=send_sems.at[7 * a + k], recv_sem=recv_sems.at[7 * a + k],
                device_id=to, device_id_type=MESH)

        mine = [pltpu.make_async_copy(x_refs[a], out_refs[a].at[4 * x + 2 * y + c], local_sems.at[a])
                for a in range(na)]
        for cp in mine:
            cp.start()
        first = []
        for a in range(na):
            first.append(copy(a, 0, me, sibling, own=True))
            first += [copy(a, 1 + j, me, (*chip, c), own=True) for j, chip in enumerate(chips)]
        for cp in first:
            cp.start()
        passed = []
        for j, chip in enumerate(chips):
            for a in range(na):
                copy(a, 1 + j, (*chip, c), me).wait_recv()
                fwd = copy(a, 4 + j, (*chip, c), sibling)
                fwd.start()
                passed.append(fwd)
        for a in range(na):
            copy(a, 0, sibling, me).wait_recv()
            for j, chip in enumerate(chips):
                copy(a, 4 + j, (*chip, 1 - c), me).wait_recv()
        for cp in first + passed:
            cp.wait_send()
        for cp in mine:
            cp.wait()

    return pl.pallas_call(
        body, name=name,
        out_shape=[jax.ShapeDtypeStruct((N_DEV,) + t.shape, t.dtype) for t in shards],
        in_specs=[HBM_SPEC] * na, out_specs=[HBM_SPEC] * na,
        scratch_shapes=[pltpu.SemaphoreType.DMA((7 * na,)), pltpu.SemaphoreType.DMA((7 * na,)),
                        pltpu.SemaphoreType.DMA((na,))],
    )(*shards)


def _all_gather_direct(vec, name):
    r, n = vec.shape

    def body(x_ref, out_ref, send_sems, recv_sems, local_sem):
        x, y, c = _coords()
        flip = lambda v, bit: 1 - v if bit else v
        me = 4 * x + 2 * y + c
        mine = pltpu.make_async_copy(x_ref, out_ref.at[me], local_sem)
        mine.start()
        copies = []
        for d in range(1, N_DEV):
            peer = (flip(x, d & 4), flip(y, d & 2), flip(c, d & 1))
            peer_slot = 4 * peer[0] + 2 * peer[1] + peer[2]
            out_going = pltpu.make_async_remote_copy(
                src_ref=x_ref, dst_ref=out_ref.at[me], send_sem=send_sems.at[d - 1],
                recv_sem=recv_sems.at[d - 1], device_id=peer, device_id_type=MESH)
            incoming = pltpu.make_async_remote_copy(
                src_ref=x_ref, dst_ref=out_ref.at[peer_slot], send_sem=send_sems.at[d - 1],
                recv_sem=recv_sems.at[d - 1], device_id=peer, device_id_type=MESH)
            out_going.start()
            copies.append((out_going, incoming))
        for out_going, incoming in copies:
            incoming.wait_recv()
            out_going.wait_send()
        mine.wait()

    return pl.pallas_call(
        body, name=name, out_shape=jax.ShapeDtypeStruct((N_DEV, r, n), vec.dtype),
        in_specs=[HBM_SPEC], out_specs=HBM_SPEC,
        scratch_shapes=[pltpu.SemaphoreType.DMA((7,)), pltpu.SemaphoreType.DMA((7,)), pltpu.SemaphoreType.DMA],
    )(vec)


def _exchange_with_sibling(parts):
    na = len(parts)

    def body(*refs):
        p_refs, got_refs = refs[:na], refs[na:2 * na]
        send_sems, recv_sems = refs[2 * na:]
        x, y, c = _coords()
        copies = []
        for a in range(na):
            for chip in range(N_CHIPS):
                cp = pltpu.make_async_remote_copy(
                    src_ref=p_refs[a].at[2 * chip + (1 - c)], dst_ref=got_refs[a].at[chip],
                    send_sem=send_sems.at[N_CHIPS * a + chip], recv_sem=recv_sems.at[N_CHIPS * a + chip],
                    device_id=(x, y, 1 - c), device_id_type=MESH)
                cp.start()
                copies.append(cp)
        for cp in copies:
            cp.wait()

    return pl.pallas_call(
        body, name="grad_exchange_sibling",
        out_shape=[jax.ShapeDtypeStruct((N_CHIPS,) + t.shape[1:], t.dtype) for t in parts],
        in_specs=[HBM_SPEC] * na, out_specs=[HBM_SPEC] * na,
        scratch_shapes=[pltpu.SemaphoreType.DMA((N_CHIPS * na,)), pltpu.SemaphoreType.DMA((N_CHIPS * na,))],
    )(*parts)


def _exchange_with_chips(sums):
    na = len(sums)

    def body(*refs):
        a_refs, out_refs = refs[:na], refs[na:2 * na]
        send_sems, recv_sems = refs[2 * na:]
        x, y, c = _coords()
        copies = []
        for a in range(na):
            for j, (cx, cy) in enumerate(_other_chips(x, y)):
                cp = pltpu.make_async_remote_copy(
                    src_ref=a_refs[a].at[2 * cx + cy], dst_ref=out_refs[a].at[j],
                    send_sem=send_sems.at[3 * a + j], recv_sem=recv_sems.at[3 * a + j],
                    device_id=(cx, cy, c), device_id_type=MESH)
                cp.start()
                copies.append(cp)
        for cp in copies:
            cp.wait()

    return pl.pallas_call(
        body, name="grad_exchange_chips",
        out_shape=[jax.ShapeDtypeStruct((3,) + t.shape[1:], t.dtype) for t in sums],
        in_specs=[HBM_SPEC] * na, out_specs=[HBM_SPEC] * na,
        scratch_shapes=[pltpu.SemaphoreType.DMA((3 * na,)), pltpu.SemaphoreType.DMA((3 * na,))],
    )(*sums)


SEM_SPEC = pl.BlockSpec(memory_space=pltpu.SEMAPHORE)
ANY_SPEC = pl.BlockSpec(memory_space=pl.ANY)
SIDE_EFFECT = pltpu.SideEffectType.DATAFLOW_SIDE_EFFECTING


def _peer(x, y, c, d):
    flip = lambda v, bit: 1 - v if bit else v
    p = (flip(x, d & 4), flip(y, d & 2), flip(c, d & 1))
    return p, 4 * p[0] + 2 * p[1] + p[2]


def _direct_copies(gather, src_refs, land_refs, send_sems, recv_sems):
    x, y, c = _coords()
    me = 4 * x + 2 * y + c
    copies = []
    for a in range(len(src_refs)):
        for d in range(1, N_DEV):
            peer, peer_slot = _peer(x, y, c, d)
            copies.append(pltpu.make_async_remote_copy(
                src_ref=src_refs[a] if gather else src_refs[a].at[peer_slot],
                dst_ref=land_refs[a].at[me] if gather else land_refs[a].at[d - 1],
                send_sem=send_sems.at[7 * a + d - 1], recv_sem=recv_sems.at[7 * a + d - 1],
                device_id=peer, device_id_type=MESH))
    return copies


def _hbm(t):
    return pltpu.HBM(t.shape, t.dtype)


def _direct_start(gather, arrays, through, name):
    na = len(arrays)
    lands = [lax.empty((N_DEV,) + t.shape if gather else (N_DEV - 1,) + t.shape[1:], t.dtype) for t in arrays]
    n_io = 2 * na + 1

    def body(*refs):
        for cp in _direct_copies(gather, refs[:na], refs[na:2 * na], refs[n_io], refs[n_io + 1]):
            cp.start()

    ins = [pltpu.with_memory_space_constraint(t, pltpu.HBM) for t in (*arrays, *lands, through)]
    sems = pltpu.SemaphoreType.DMA((7 * na,))
    res = pl.pallas_call(
        body, name=name, out_shape=(sems, sems, *[_hbm(t) for t in ins]),
        in_specs=[HBM_SPEC] * n_io, out_specs=(SEM_SPEC, SEM_SPEC, *[HBM_SPEC] * n_io),
        input_output_aliases={i: 2 + i for i in range(n_io)},
        compiler_params=pltpu.CompilerParams(has_side_effects=SIDE_EFFECT),
    )(*ins)
    return (res[0], res[1], list(res[2:2 + na]), list(res[2 + na:2 + 2 * na])), res[2 + 2 * na]


def _direct_wait(gather, started, after, name):
    send_sems, recv_sems, arrays, lands = started
    na = len(arrays)

    def body(*refs):
        for cp in _direct_copies(gather, refs[:na], refs[na:2 * na], refs[2 * na], refs[2 * na + 1]):
            cp.wait_send()
            cp.wait_recv()

    res = pl.pallas_call(
        body, name=name, out_shape=tuple(_hbm(t) for t in (*arrays, *lands)),
        in_specs=[HBM_SPEC] * (2 * na) + [SEM_SPEC, SEM_SPEC, ANY_SPEC], out_specs=tuple([HBM_SPEC] * (2 * na)),
        input_output_aliases={i: i for i in range(2 * na)},
        compiler_params=pltpu.CompilerParams(has_side_effects=SIDE_EFFECT),
    )(*arrays, *lands, send_sems, recv_sems, after)
    return list(res[:na]), list(res[na:])


def _row_tile(rows):
    for t in (256, 176, 128):
        if rows % t == 0 and rows > t:
            return t
    return rows


def _add_sibling(core, parts, got, name):
    _, r, c = parts.shape
    tr = _row_tile(r)

    def body(core_ref, p_ref, g_ref, o_ref):
        o_ref[...] = p_ref[...] + g_ref[...]

    return pl.pallas_call(
        body, name=name,
        grid_spec=pltpu.PrefetchScalarGridSpec(
            num_scalar_prefetch=1, grid=(N_CHIPS, r // tr),
            in_specs=[pl.BlockSpec((None, tr, c), lambda i, j, core_ref: (2 * i + core_ref[0], j, 0)),
                      pl.BlockSpec((None, tr, c), lambda i, j, core_ref: (i, j, 0))],
            out_specs=pl.BlockSpec((None, tr, c), lambda i, j, core_ref: (i, j, 0))),
        out_shape=jax.ShapeDtypeStruct((N_CHIPS, r, c), F32),
        compiler_params=_params(("parallel", "parallel")),
    )(core, parts, got)


def _adamw_math(g, w, m, v):
    m_new = ADAM_B1 * m + (1.0 - ADAM_B1) * g
    v_new = ADAM_B2 * v + (1.0 - ADAM_B2) * jnp.square(g)
    m_hat = m_new / (1.0 - ADAM_B1 ** ADAM_STEP)
    v_hat = v_new / (1.0 - ADAM_B2 ** ADAM_STEP)
    delta = -ADAM_LR * (m_hat / (jnp.sqrt(v_hat) + ADAM_EPS) + ADAM_WD * w)
    return delta, m_new, v_new


def _adamw_sharded(chip, sums, got, w, m, v, name):
    r, c = w.shape
    tr = _row_tile(r)
    n_got = got.shape[0]

    def body(chip_ref, s_ref, g_ref, w_ref, m_ref, v_ref, go_ref, d_ref, nm_ref, nv_ref):
        g = s_ref[...]
        for i in range(n_got):
            g = g + g_ref[i]
        delta, m_new, v_new = _adamw_math(g, w_ref[...], m_ref[...], v_ref[...])
        go_ref[...] = g
        d_ref[...] = delta
        nm_ref[...] = m_new
        nv_ref[...] = v_new

    blk = pl.BlockSpec((tr, c), lambda i, chip_ref: (i, 0))
    out = jax.ShapeDtypeStruct((r, c), F32)
    return pl.pallas_call(
        body, name=name,
        grid_spec=pltpu.PrefetchScalarGridSpec(
            num_scalar_prefetch=1, grid=(r // tr,),
            in_specs=[pl.BlockSpec((None, tr, c), lambda i, chip_ref: (chip_ref[0], i, 0)),
                      pl.BlockSpec((n_got, tr, c), lambda i, chip_ref: (0, i, 0)), blk, blk, blk],
            out_specs=[blk, blk, blk, blk]),
        out_shape=[out, out, out, out],
        compiler_params=_params(("parallel",)),
    )(chip, sums, got, w, m, v)


def _adamw_replicated(parts, w, m, v):
    p, r, c = parts.shape

    def body(p_ref, w_ref, m_ref, v_ref, g_ref, d_ref, nm_ref, nv_ref):
        g = p_ref[0]
        for i in range(1, p):
            g = g + p_ref[i]
        delta, m_new, v_new = _adamw_math(g, w_ref[...], m_ref[...], v_ref[...])
        g_ref[...] = g
        d_ref[...] = delta
        nm_ref[...] = m_new
        nv_ref[...] = v_new

    blk = pl.BlockSpec((r, c), lambda i: (0, 0))
    out = jax.ShapeDtypeStruct((r, c), F32)
    return pl.pallas_call(
        body, name="adamw_replicated", grid=(1,),
        in_specs=[pl.BlockSpec((p, r, c), lambda i: (0, 0, 0)), blk, blk, blk],
        out_specs=[blk, blk, blk, blk], out_shape=[out, out, out, out],
        compiler_params=_params(("arbitrary",)),
    )(parts, w, m, v)


SHARDED_NAMES = ("w_in", "ml_conv_w", "w_out", "ca_wq", "ca_wkv", "ca_wo", "ffn_w_up", "ffn_conv_w", "ffn_w_down")
SMALL_NAMES = ("b_in", "hg_lb_logits", "hg_norm_w", "ml_conv_b", "ml_norm_w", "ln1_g", "ln1_b",
               "ln2_g", "ln2_b", "ffn_conv_b", "ln3_g", "ln3_b")
WEIGHT_NAMES = ("w_in", "b_in", "hg_lb_logits", "hg_norm_w", "ml_conv_w", "ml_conv_b", "ml_norm_w", "w_out",
                "ln1_g", "ln1_b", "ca_wq", "ca_wkv", "ca_wo", "ln2_g", "ln2_b", "ffn_w_up", "ffn_conv_w",
                "ffn_conv_b", "ffn_w_down", "ln3_g", "ln3_b")
PAD_TO = {"w_in": W_IN_SHARD_P, "ffn_w_up": UP_SHARD_P, "ffn_conv_w": UP_SHARD_P}
SMALL_ROWS = 24
SMALL_W = D_MODEL


def _shard_2d(name, block):
    t = block[0]
    if name in PAD_TO:
        t = jnp.pad(t, ((0, 0), (0, PAD_TO[name] - t.shape[1])))
    return t


def _shard_like(name, t, like):
    return t[:, :like.shape[2]][None]


def _pad_cols(t, width):
    return jnp.pad(t, ((0, 0), (0, width - t.shape[1])))


FIRST_NAMES = ("w_in", "ml_conv_w")
LATE_NAMES = ("w_out", "ca_wq", "ca_wkv", "ca_wo", "ffn_w_up", "ffn_conv_w", "ffn_w_down")
FFN_NAMES = ("ffn_w_up", "ffn_w_down", "ffn_conv_w")
MID_NAMES = ("ca_wo", "ca_wq", "ca_wkv", "w_out")


def _first_weights(g, small):
    w = dict(small)
    w_in = jnp.concatenate([g["w_in"][j, :, :W_IN_SHARD] for j in range(N_DEV)], axis=1)
    w["w_in_main"] = w_in[:, :D_IN_MAIN]
    w["w_in_gate"] = _pad_cols(w_in[:, D_IN_MAIN:], LANES)
    w["b_in_main"] = small["b_in"][:, :D_IN_MAIN]
    w["b_in_gate"] = _pad_cols(small["b_in"][:, D_IN_MAIN:], LANES)
    w["ml_conv_w"] = jnp.transpose(g["ml_conv_w"], (1, 0, 2)).reshape(ML_CONV, 2 * D_GROUP)
    return w


def _late_weights(g, small):
    w = {}
    for n in ("w_out", "ca_wq", "ca_wo"):
        w[n] = g[n].reshape(D_MODEL, D_MODEL)
    w["ca_wkv"] = g["ca_wkv"]
    w["ffn_w_up"] = g["ffn_w_up"]
    down = g["ffn_w_down"].reshape(N_DEV // 2, UP_SHARD, D_MODEL)
    w["ffn_w_down"] = jnp.pad(down, ((0, 0), (0, UP_SHARD_P - UP_SHARD), (0, 0))).reshape(D_FF_P, D_MODEL)
    w["ffn_conv_w"] = jnp.transpose(g["ffn_conv_w"], (1, 0, 2)).reshape(FFN_CONV, D_UP_P)
    w["ffn_conv_b"] = _pad_cols(small["ffn_conv_b"].reshape(N_DEV, UP_SHARD), UP_SHARD_P).reshape(1, D_UP_P)
    return w


def _whole_weights(g, small):
    return {**_first_weights(g, small), **_late_weights(g, small)}


def _owner_stack(n, grads):
    if n == "w_in":
        w_in = jnp.concatenate([grads["w_in_main"], grads["w_in_gate"][:, :D_IN - D_IN_MAIN]], axis=1)
        return jnp.stack([_pad_cols(w_in[:, j * W_IN_SHARD:(j + 1) * W_IN_SHARD], W_IN_SHARD_P)
                          for j in range(N_DEV)])
    if n in ("w_out", "ca_wq", "ca_wo"):
        return grads[n].reshape(N_DEV, D_MODEL // N_DEV, D_MODEL)
    if n == "ffn_w_down":
        down = grads[n].reshape(N_DEV // 2, UP_SHARD_P, D_MODEL)[:, :UP_SHARD]
        return down.reshape(N_DEV, D_FF // N_DEV, D_MODEL)
    if n == "ml_conv_w":
        return jnp.transpose(grads[n].reshape(ML_CONV, N_DEV, LANES), (1, 0, 2))
    if n == "ffn_conv_w":
        return jnp.transpose(grads[n].reshape(FFN_CONV, N_DEV, UP_SHARD_P), (1, 0, 2))
    return grads[n]


def _owner_stacks(grads):
    return {n: _owner_stack(n, grads) for n in SHARDED_NAMES}


def _small_grads(grads):
    out = {n: grads[n] for n in SMALL_NAMES if n in grads}
    out["b_in"] = jnp.concatenate([grads["b_in_main"], grads["b_in_gate"][:, :D_IN - D_IN_MAIN]], axis=1)
    out["ffn_conv_b"] = grads["ffn_conv_b"].reshape(N_DEV, UP_SHARD_P)[:, :UP_SHARD].reshape(1, D_UP)
    return out


def _pack_small(p, extra=None):
    flat = [p[n].reshape(-1) for n in SMALL_NAMES]
    if extra is not None:
        flat.append(extra.reshape(-1))
    flat = jnp.concatenate(flat)
    return jnp.pad(flat, (0, SMALL_ROWS * SMALL_W - flat.shape[0])).reshape(SMALL_ROWS, SMALL_W)


def _unpack_small(slab, like):
    out = {}
    flat = slab.reshape(-1)
    o = 0
    for n in SMALL_NAMES:
        out[n] = flat[o:o + like[n].size].reshape(like[n].shape)
        o += like[n].size
    return out, flat[o]


def kernel(x, mem, w_in, b_in, hg_lb_logits, hg_norm_w, ml_conv_w, ml_conv_b, ml_norm_w, w_out, ln1_g, ln1_b, ca_wq, ca_wkv, ca_wo, ln2_g, ln2_b, ffn_w_up, ffn_conv_w, ffn_conv_b, ffn_w_down, ln3_g, ln3_b, loss_target, m_w_in, m_b_in, m_hg_lb_logits, m_hg_norm_w, m_ml_conv_w, m_ml_conv_b, m_ml_norm_w, m_w_out, m_ln1_g, m_ln1_b, m_ca_wq, m_ca_wkv, m_ca_wo, m_ln2_g, m_ln2_b, m_ffn_w_up, m_ffn_conv_w, m_ffn_conv_b, m_ffn_w_down, m_ln3_g, m_ln3_b, v_w_in, v_b_in, v_hg_lb_logits, v_hg_norm_w, v_ml_conv_w, v_ml_conv_b, v_ml_norm_w, v_w_out, v_ln1_g, v_ln1_b, v_ca_wq, v_ca_wkv, v_ca_wo, v_ln2_g, v_ln2_b, v_ffn_w_up, v_ffn_conv_w, v_ffn_conv_b, v_ffn_w_down, v_ln3_g, v_ln3_b):
    params = dict(w_in=w_in, b_in=b_in, hg_lb_logits=hg_lb_logits, hg_norm_w=hg_norm_w, ml_conv_w=ml_conv_w,
                  ml_conv_b=ml_conv_b, ml_norm_w=ml_norm_w, w_out=w_out, ln1_g=ln1_g, ln1_b=ln1_b, ca_wq=ca_wq,
                  ca_wkv=ca_wkv, ca_wo=ca_wo, ln2_g=ln2_g, ln2_b=ln2_b, ffn_w_up=ffn_w_up, ffn_conv_w=ffn_conv_w,
                  ffn_conv_b=ffn_conv_b, ffn_w_down=ffn_w_down, ln3_g=ln3_g, ln3_b=ln3_b)
    mom1 = dict(w_in=m_w_in, b_in=m_b_in, hg_lb_logits=m_hg_lb_logits, hg_norm_w=m_hg_norm_w,
                ml_conv_w=m_ml_conv_w, ml_conv_b=m_ml_conv_b, ml_norm_w=m_ml_norm_w, w_out=m_w_out, ln1_g=m_ln1_g,
                ln1_b=m_ln1_b, ca_wq=m_ca_wq, ca_wkv=m_ca_wkv, ca_wo=m_ca_wo, ln2_g=m_ln2_g, ln2_b=m_ln2_b,
                ffn_w_up=m_ffn_w_up, ffn_conv_w=m_ffn_conv_w, ffn_conv_b=m_ffn_conv_b, ffn_w_down=m_ffn_w_down,
                ln3_g=m_ln3_g, ln3_b=m_ln3_b)
    mom2 = dict(w_in=v_w_in, b_in=v_b_in, hg_lb_logits=v_hg_lb_logits, hg_norm_w=v_hg_norm_w,
                ml_conv_w=v_ml_conv_w, ml_conv_b=v_ml_conv_b, ml_norm_w=v_ml_norm_w, w_out=v_w_out, ln1_g=v_ln1_g,
                ln1_b=v_ln1_b, ca_wq=v_ca_wq, ca_wkv=v_ca_wkv, ca_wo=v_ca_wo, ln2_g=v_ln2_g, ln2_b=v_ln2_b,
                ffn_w_up=v_ffn_w_up, ffn_conv_w=v_ffn_conv_w, ffn_conv_b=v_ffn_conv_b, ffn_w_down=v_ffn_w_down,
                ln3_g=v_ln3_g, ln3_b=v_ln3_b)

    x_idx, y_idx, c_idx = _coords()
    as_index = lambda v: jnp.reshape(v, (1,)).astype(jnp.int32)
    core, chip, me = as_index(c_idx), as_index(2 * x_idx + y_idx), as_index(4 * x_idx + 2 * y_idx + c_idx)
    small_params = {n: params[n] for n in SMALL_NAMES}

    shards = {n: _shard_2d(n, params[n]) for n in SHARDED_NAMES}
    to_send = lambda names: [shards[n] if "conv" in n else shards[n].astype(BF16) for n in names]
    first = dict(zip(FIRST_NAMES, _all_gather_two_level(to_send(FIRST_NAMES), "weights_gather_first")))
    late_started, x_in = _direct_start(True, to_send(LATE_NAMES), x[0], "weights_gather_start")

    def late_weights(y):
        mine, lands = _direct_wait(True, late_started, y, "weights_gather_wait")
        gathered = {n: lax.dynamic_update_index_in_dim(land, own, me[0], 0)
                    for n, own, land in zip(LATE_NAMES, mine, lands)}
        return _late_weights(gathered, small_params)

    started = {}

    def start_group(names, tag):
        def hook(grads, through):
            started[tag], through = _direct_start(False, [_owner_stack(n, grads) for n in names], through,
                                                  "grads_start_" + tag)
            return through
        return hook

    loss, grad_x, grads = _local_step(x_in, mem[0], loss_target[0], _first_weights(first, small_params),
                                      late_weights, start_group(FFN_NAMES, "ffn"), start_group(MID_NAMES, "mid"))

    sharded_out = {}

    def adamw(n, index, own, got):
        res = _adamw_sharded(index, own, got, shards[n], _shard_2d(n, mom1[n]), _shard_2d(n, mom2[n]), "adamw_" + n)
        sharded_out[n] = [_shard_like(n, t, params[n]) for t in res]

    stacks = [_owner_stack(n, grads) for n in FIRST_NAMES]
    from_sibling = _exchange_with_sibling(stacks)
    chip_sums = [_add_sibling(core, st, got, "grad_add_" + n) for n, st, got in zip(FIRST_NAMES, stacks, from_sibling)]
    from_chips = _exchange_with_chips(chip_sums)
    for n, sums, got in zip(FIRST_NAMES, chip_sums, from_chips):
        adamw(n, chip, sums, got)
    for names, tag in ((FFN_NAMES, "ffn"), (MID_NAMES, "mid")):
        own, lands = _direct_wait(False, started[tag], grad_x, "grads_wait_" + tag)
        for n, st, land in zip(names, own, lands):
            adamw(n, me, st, land)
    small_parts = _all_gather_direct(_pack_small(_small_grads(grads), loss), "small_all_gather")
    small_res = _adamw_replicated(small_parts, _pack_small(params), _pack_small(mom1), _pack_small(mom2))

    outs = []
    total_loss = None
    for k in range(4):
        small, extra = _unpack_small(small_res[k], params)
        if total_loss is None:
            total_loss = extra
        outs.extend(sharded_out[n][k] if n in sharded_out else small[n] for n in WEIGHT_NAMES)
    return (total_loss, grad_x[None], *outs)
```

```python
import jax
import jax.numpy as jnp
from jax import lax
from jax.experimental import pallas as pl
from jax.experimental.pallas import tpu as pltpu

F32 = jnp.float32
BF16 = jnp.bfloat16
HIGHEST = lax.Precision.HIGHEST
MESH = pl.DeviceIdType.MESH

N_DEV = 8
N_CHIPS = 4
D_MODEL = 1024
N_MEM = 256
N_HEADS = 4
D_HEAD = 128
D_GROUP = N_HEADS * D_HEAD
CHUNK = 64
ML_CONV = 4
FFN_CONV = 3
D_FF = 2816
D_UP = 2 * D_FF
CA_HEADS = 4
CA_DH = D_MODEL // CA_HEADS
LANES = 128
SUBLANES = 8
D_IN = 8 * D_GROUP + 2 * N_HEADS
D_IN_MAIN = 8 * D_GROUP
W_IN_SHARD = D_IN // N_DEV
W_IN_SHARD_P = 640
UP_SHARD = D_UP // N_DEV
UP_SHARD_P = 768
D_UP_P = N_DEV * UP_SHARD_P
D_FF_P = D_UP_P // 2
ALPHA = 2.0 ** 0.25
LN_EPS = 1e-5
NEG_BIG = -1e30
ADAM_LR = 0.001
ADAM_B1 = 0.9
ADAM_B2 = 0.999
ADAM_EPS = 1e-08
ADAM_WD = 0.01
ADAM_STEP = 10
VMEM_LIMIT = 56 * 1024 * 1024

SEG_HQ, SEG_HF, SEG_HI, SEG_HG, SEG_MQ, SEG_MK, SEG_MV, SEG_MO = (4 * i for i in range(8))


def _params(sem):
    return pltpu.CompilerParams(dimension_semantics=sem, vmem_limit_bytes=VMEM_LIMIT)


def _dg(a, b, ca, cb, precision=None):
    return lax.dot_general(a, b, (((ca,), (cb,)), ((), ())), precision=precision,
                           preferred_element_type=F32)


def _nn_raw(a, b):
    return _dg(a.astype(BF16), b.astype(BF16), 1, 0)


def _nt_raw(a, b):
    return _dg(a.astype(BF16), b.astype(BF16), 1, 1)


def _tn_raw(a, b):
    return _dg(a.astype(BF16), b.astype(BF16), 0, 0)


@jax.custom_vjp
def _nn(a, b):
    return _nn_raw(a, b)


_nn.defvjp(lambda a, b: (_nn_raw(a, b), (a, b)),
           lambda res, g: (_nt_raw(g, res[1]), _tn_raw(res[0], g)))


@jax.custom_vjp
def _nt(a, b):
    return _nt_raw(a, b)


_nt.defvjp(lambda a, b: (_nt_raw(a, b), (a, b)),
           lambda res, g: (_nn_raw(g, res[1]), _tn_raw(g, res[0])))


@jax.custom_vjp
def _tn(a, b):
    return _tn_raw(a, b)


_tn.defvjp(lambda a, b: (_tn_raw(a, b), (a, b)),
           lambda res, g: (_nt_raw(res[1], g), _nn_raw(res[0], g)))


def _layer_norm(z, g, b):
    mu = jnp.mean(z, axis=-1, keepdims=True)
    var = jnp.mean(jnp.square(z - mu), axis=-1, keepdims=True)
    return (z - mu) * lax.rsqrt(var + LN_EPS) * g + b


def _matmul_nn(a, w, bias, tm, tn, name):
    m, k = a.shape
    if w.ndim == 3:
        n = w.shape[0] * w.shape[2]
        assert tn == w.shape[2]
        w_spec = pl.BlockSpec((None, k, tn), lambda i, j: (j, 0, 0))
    else:
        n = w.shape[1]
        w_spec = pl.BlockSpec((k, tn), lambda i, j: (0, j))

    def body(*refs):
        a_ref, w_ref = refs[0], refs[1]
        o_ref = refs[-1]
        acc = _nn_raw(a_ref[...], w_ref[...])
        if bias is not None:
            acc = acc + refs[2][...]
        o_ref[...] = acc

    in_specs = [pl.BlockSpec((tm, k), lambda i, j: (i, 0)), w_spec]
    args = [a, w]
    if bias is not None:
        in_specs.append(pl.BlockSpec((1, tn), lambda i, j: (0, j)))
        args.append(bias)
    return pl.pallas_call(
        body, name=name, grid=(m // tm, n // tn), in_specs=in_specs,
        out_specs=pl.BlockSpec((tm, tn), lambda i, j: (i, j)),
        out_shape=jax.ShapeDtypeStruct((m, n), F32),
        compiler_params=_params(("parallel", "parallel")),
    )(*args)


def _matmul_nt(pairs, add, scale, tm, tk, name):
    m = pairs[0][0].shape[0]
    k = pairs[0][1].shape[-2]
    groups = []
    in_specs, args = [], []
    for pair in pairs:
        d, w = pair[0], pair[1]
        in_specs.append(pl.BlockSpec((tm, d.shape[1]), lambda i, j: (i, 0)))
        if w.ndim == 3:
            g = d.shape[1] // w.shape[2]
            blk = pair[2] // g
            in_specs.append(pl.BlockSpec((g, tk, w.shape[2]), lambda i, j, blk=blk: (blk, j, 0)))
            groups.append((g, w.shape[2]))
        else:
            in_specs.append(pl.BlockSpec((tk, w.shape[1]), lambda i, j: (j, 0)))
            groups.append(None)
        args += [d, w]
    if add is not None:
        in_specs.append(pl.BlockSpec((tm, tk), lambda i, j: (i, j)))
        args.append(add)

    def body(*refs):
        o_ref = refs[-1]
        acc = None
        for p, grp in enumerate(groups):
            d_ref, w_ref = refs[2 * p], refs[2 * p + 1]
            if grp is None:
                terms = [_nt_raw(d_ref[...], w_ref[...])]
            else:
                terms = [_nt_raw(d_ref[:, g * grp[1]:(g + 1) * grp[1]], w_ref[g]) for g in range(grp[0])]
            for t in terms:
                acc = t if acc is None else acc + t
        if add is not None:
            acc = acc + scale * refs[2 * len(groups)][...]
        o_ref[...] = acc

    return pl.pallas_call(
        body, name=name, grid=(m // tm, k // tk), in_specs=in_specs,
        out_specs=pl.BlockSpec((tm, tk), lambda i, j: (i, j)),
        out_shape=jax.ShapeDtypeStruct((m, k), F32),
        compiler_params=_params(("parallel", "parallel")),
    )(*args)


def _matmul_tn(a, b, tm, tn, tt, name, shards=None, shard0=0, into=None, colsum=False):
    t, m = a.shape
    n = b.shape[1]
    assert not colsum or tm == m
    n_in = 2 + (into is not None)

    def body(*refs):
        a_ref, b_ref = refs[0], refs[1]
        o_ref = refs[n_in]
        first = pl.program_id(2) == 0

        @pl.when(first)
        def _():
            o_ref[...] = jnp.zeros_like(o_ref)

        o_ref[...] += _tn_raw(a_ref[...], b_ref[...])
        if colsum:
            s_ref = refs[n_in + 1]

            @pl.when(first)
            def _():
                s_ref[...] = jnp.zeros_like(s_ref)

            s_ref[...] += jnp.sum(b_ref[...], axis=0, keepdims=True)

    in_specs = [pl.BlockSpec((tt, tm), lambda i, j, kk: (kk, i)), pl.BlockSpec((tt, tn), lambda i, j, kk: (kk, j))]
    args = [a, b]
    aliases = {}
    if into is not None:
        in_specs.append(pl.BlockSpec(memory_space=pl.ANY))
        args.append(into)
        aliases = {2: 0}
    if shards is None:
        out_specs = [pl.BlockSpec((tm, tn), lambda i, j, kk: (i, j))]
        out_shape = [jax.ShapeDtypeStruct((m, n), F32)]
    else:
        out_specs = [pl.BlockSpec((None, tm, tn), lambda i, j, kk: (shard0 + j, i, 0))]
        out_shape = [jax.ShapeDtypeStruct((shards, m, tn), F32)]
    if colsum:
        out_specs.append(pl.BlockSpec((1, tn), lambda i, j, kk: (0, j)))
        out_shape.append(jax.ShapeDtypeStruct((1, n), F32))
    res = pl.pallas_call(
        body, name=name, grid=(m // tm, n // tn, t // tt), in_specs=in_specs, out_specs=out_specs,
        out_shape=out_shape, input_output_aliases=aliases,
        compiler_params=_params(("parallel", "parallel", "arbitrary")),
    )(*args)
    return res if colsum else res[0]


ROW_TILE = 512


def _conv_fwd_tile(pad_ref, w_ref, b_ref, r0, rows, taps):
    acc = b_ref[...]
    for j in range(taps):
        acc = acc + pad_ref[pl.ds(SUBLANES - (taps - 1 - j) + r0, rows), :] * w_ref[j:j + 1, :]
    return acc


def _conv_bwd_tile(dpad_ref, w_ref, r0, rows, taps):
    acc = None
    for j in range(taps):
        term = dpad_ref[pl.ds(r0 + (taps - 1 - j), rows), :] * w_ref[j:j + 1, :]
        acc = term if acc is None else acc + term
    return acc


def _conv_grads_tile(pad_ref, dpad_ref, dx_ref, w_ref, dws, r0, rows, taps):
    dx_ref[r0:r0 + rows, :] = _conv_bwd_tile(dpad_ref, w_ref, r0, rows, taps)
    d_pre = dpad_ref[r0:r0 + rows, :]
    for j in range(taps):
        xs = pad_ref[pl.ds(SUBLANES - (taps - 1 - j) + r0, rows), :]
        dws[j] = dws[j] + jnp.sum(d_pre * xs, axis=0, keepdims=True)


def _ml_conv_fwd(proj, conv_w, conv_b):
    s = proj.shape[0]
    nblk = 2 * D_GROUP // LANES

    def body(x_ref, w_ref, b_ref, o_ref, pad_ref):
        pad_ref[0:SUBLANES, :] = jnp.zeros((SUBLANES, LANES), F32)
        pad_ref[SUBLANES:, :] = x_ref[...]
        for r0 in range(0, s, ROW_TILE):
            rows = min(ROW_TILE, s - r0)
            o_ref[r0:r0 + rows, :] = jax.nn.silu(_conv_fwd_tile(pad_ref, w_ref, b_ref, r0, rows, ML_CONV))

    return pl.pallas_call(
        body, name="ml_conv_fwd", grid=(nblk,),
        in_specs=[pl.BlockSpec((s, LANES), lambda j: (0, SEG_MQ + j)),
                  pl.BlockSpec((ML_CONV, LANES), lambda j: (0, j)),
                  pl.BlockSpec((1, LANES), lambda j: (0, j))],
        out_specs=pl.BlockSpec((s, LANES), lambda j: (0, j)),
        out_shape=jax.ShapeDtypeStruct((s, 2 * D_GROUP), F32),
        scratch_shapes=[pltpu.VMEM((s + SUBLANES, LANES), F32)],
        compiler_params=_params(("parallel",)),
    )(proj, conv_w, conv_b)


def _ml_conv_bwd(proj, conv_w, conv_b, d_qk, d_proj):
    s = proj.shape[0]
    nblk = 2 * D_GROUP // LANES

    def body(x_ref, w_ref, b_ref, dy_ref, _, dx_ref, dw_ref, db_ref, pad_ref, dpad_ref):
        pad_ref[0:SUBLANES, :] = jnp.zeros((SUBLANES, LANES), F32)
        pad_ref[SUBLANES:, :] = x_ref[...]
        dpad_ref[s:, :] = jnp.zeros((SUBLANES, LANES), F32)
        db = jnp.zeros((1, LANES), F32)
        for r0 in range(0, s, ROW_TILE):
            rows = min(ROW_TILE, s - r0)
            pre = _conv_fwd_tile(pad_ref, w_ref, b_ref, r0, rows, ML_CONV)
            _, vjp = jax.vjp(jax.nn.silu, pre)
            d_pre, = vjp(dy_ref[r0:r0 + rows, :])
            dpad_ref[r0:r0 + rows, :] = d_pre
            db = db + jnp.sum(d_pre, axis=0, keepdims=True)
        db_ref[...] = db
        dws = [jnp.zeros((1, LANES), F32) for _ in range(ML_CONV)]
        for r0 in range(0, s, ROW_TILE):
            _conv_grads_tile(pad_ref, dpad_ref, dx_ref, w_ref, dws, r0, min(ROW_TILE, s - r0), ML_CONV)
        for j in range(ML_CONV):
            dw_ref[j:j + 1, :] = dws[j]

    return pl.pallas_call(
        body, name="ml_conv_bwd", grid=(nblk,),
        in_specs=[pl.BlockSpec((s, LANES), lambda j: (0, SEG_MQ + j)),
                  pl.BlockSpec((ML_CONV, LANES), lambda j: (0, j)),
                  pl.BlockSpec((1, LANES), lambda j: (0, j)),
                  pl.BlockSpec((s, LANES), lambda j: (0, j)),
                  pl.BlockSpec(memory_space=pl.ANY)],
        out_specs=[pl.BlockSpec((s, LANES), lambda j: (0, SEG_MQ + j)),
                   pl.BlockSpec((ML_CONV, LANES), lambda j: (0, j)),
                   pl.BlockSpec((1, LANES), lambda j: (0, j))],
        out_shape=[jax.ShapeDtypeStruct(d_proj.shape, F32),
                   jax.ShapeDtypeStruct((ML_CONV, 2 * D_GROUP), F32),
                   jax.ShapeDtypeStruct((1, 2 * D_GROUP), F32)],
        input_output_aliases={4: 0},
        scratch_shapes=[pltpu.VMEM((s + SUBLANES, LANES), F32), pltpu.VMEM((s + SUBLANES, LANES), F32)],
        compiler_params=_params(("parallel",)),
    )(proj, conv_w, conv_b, d_qk, d_proj)


def _gelu_mul(a, b):
    return jax.nn.gelu(a) * b


FFN_BLOCKS = D_FF_P // LANES


def _ffn_conv_fwd(u, conv_w, conv_b):
    s = u.shape[0]

    def body(g_ref, v_ref, wg_ref, wv_ref, bg_ref, bv_ref, o_ref, gpad_ref, vpad_ref):
        for pad_ref, x_ref in ((gpad_ref, g_ref), (vpad_ref, v_ref)):
            pad_ref[0:SUBLANES, :] = jnp.zeros((SUBLANES, LANES), F32)
            pad_ref[SUBLANES:, :] = x_ref[...]
        for r0 in range(0, s, ROW_TILE):
            rows = min(ROW_TILE, s - r0)
            ug = _conv_fwd_tile(gpad_ref, wg_ref, bg_ref, r0, rows, FFN_CONV)
            uv = _conv_fwd_tile(vpad_ref, wv_ref, bv_ref, r0, rows, FFN_CONV)
            o_ref[r0:r0 + rows, :] = _gelu_mul(ug, uv)

    col = lambda off: (lambda j: (0, off + j))
    return pl.pallas_call(
        body, name="ffn_conv_fwd", grid=(FFN_BLOCKS,),
        in_specs=[pl.BlockSpec((s, LANES), col(0)), pl.BlockSpec((s, LANES), col(FFN_BLOCKS)),
                  pl.BlockSpec((FFN_CONV, LANES), col(0)), pl.BlockSpec((FFN_CONV, LANES), col(FFN_BLOCKS)),
                  pl.BlockSpec((1, LANES), col(0)), pl.BlockSpec((1, LANES), col(FFN_BLOCKS))],
        out_specs=pl.BlockSpec((s, LANES), col(0)),
        out_shape=jax.ShapeDtypeStruct((s, D_FF_P), F32),
        scratch_shapes=[pltpu.VMEM((s + SUBLANES, LANES), F32), pltpu.VMEM((s + SUBLANES, LANES), F32)],
        compiler_params=_params(("parallel",)),
    )(u, u, conv_w, conv_w, conv_b, conv_b)


def _ffn_conv_bwd(u, conv_w, conv_b, d_h):
    s = u.shape[0]

    def body(g_ref, v_ref, wg_ref, wv_ref, bg_ref, bv_ref, dh_ref,
             dug_ref, duv_ref, dwg_ref, dwv_ref, dbg_ref, dbv_ref,
             gpad_ref, vpad_ref, dgpad_ref, dvpad_ref):
        for pad_ref, x_ref in ((gpad_ref, g_ref), (vpad_ref, v_ref)):
            pad_ref[0:SUBLANES, :] = jnp.zeros((SUBLANES, LANES), F32)
            pad_ref[SUBLANES:, :] = x_ref[...]
        dgpad_ref[s:, :] = jnp.zeros((SUBLANES, LANES), F32)
        dvpad_ref[s:, :] = jnp.zeros((SUBLANES, LANES), F32)
        dbg = jnp.zeros((1, LANES), F32)
        dbv = jnp.zeros((1, LANES), F32)
        for r0 in range(0, s, ROW_TILE):
            rows = min(ROW_TILE, s - r0)
            ug = _conv_fwd_tile(gpad_ref, wg_ref, bg_ref, r0, rows, FFN_CONV)
            uv = _conv_fwd_tile(vpad_ref, wv_ref, bv_ref, r0, rows, FFN_CONV)
            _, vjp = jax.vjp(_gelu_mul, ug, uv)
            d_ug, d_uv = vjp(dh_ref[r0:r0 + rows, :])
            dgpad_ref[r0:r0 + rows, :] = d_ug
            dvpad_ref[r0:r0 + rows, :] = d_uv
            dbg = dbg + jnp.sum(d_ug, axis=0, keepdims=True)
            dbv = dbv + jnp.sum(d_uv, axis=0, keepdims=True)
        dbg_ref[...] = dbg
        dbv_ref[...] = dbv
        for pad_ref, dpad_ref, w_ref, dx_ref, dw_ref in ((gpad_ref, dgpad_ref, wg_ref, dug_ref, dwg_ref),
                                                         (vpad_ref, dvpad_ref, wv_ref, duv_ref, dwv_ref)):
            dws = [jnp.zeros((1, LANES), F32) for _ in range(FFN_CONV)]
            for r0 in range(0, s, ROW_TILE):
                _conv_grads_tile(pad_ref, dpad_ref, dx_ref, w_ref, dws, r0, min(ROW_TILE, s - r0), FFN_CONV)
            for j in range(FFN_CONV):
                dw_ref[j:j + 1, :] = dws[j]

    col = lambda off: (lambda j: (0, off + j))
    seq = pl.BlockSpec((s, LANES), col(0))
    return pl.pallas_call(
        body, name="ffn_conv_bwd", grid=(FFN_BLOCKS,),
        in_specs=[pl.BlockSpec((s, LANES), col(0)), pl.BlockSpec((s, LANES), col(FFN_BLOCKS)),
                  pl.BlockSpec((FFN_CONV, LANES), col(0)), pl.BlockSpec((FFN_CONV, LANES), col(FFN_BLOCKS)),
                  pl.BlockSpec((1, LANES), col(0)), pl.BlockSpec((1, LANES), col(FFN_BLOCKS)), seq],
        out_specs=[seq, seq, pl.BlockSpec((FFN_CONV, LANES), col(0)), pl.BlockSpec((FFN_CONV, LANES), col(0)),
                   pl.BlockSpec((1, LANES), col(0)), pl.BlockSpec((1, LANES), col(0))],
        out_shape=[jax.ShapeDtypeStruct((s, D_FF_P), F32), jax.ShapeDtypeStruct((s, D_FF_P), F32),
                   jax.ShapeDtypeStruct((FFN_CONV, D_FF_P), F32), jax.ShapeDtypeStruct((FFN_CONV, D_FF_P), F32),
                   jax.ShapeDtypeStruct((1, D_FF_P), F32), jax.ShapeDtypeStruct((1, D_FF_P), F32)],
        scratch_shapes=[pltpu.VMEM((s + SUBLANES, LANES), F32) for _ in range(4)],
        compiler_params=_params(("parallel",)),
    )(u, u, conv_w, conv_w, conv_b, conv_b, d_h)


def _chunk_masks(c):
    row = lax.broadcasted_iota(jnp.int32, (c, c), 0)
    col = lax.broadcasted_iota(jnp.int32, (c, c), 1)
    return row, col


def _hg_step(hq, hf, hi, hgate, l0, l1, nw, st):
    c = hq.shape[0]
    row, col = _chunk_masks(c)
    mask = col <= row
    mx = lax.stop_gradient(jnp.maximum(l0, l1))
    e0 = jnp.exp(l0 - mx)
    e1 = jnp.exp(l1 - mx)
    lb = e0 / (e0 + e1)
    sig = jax.nn.sigmoid(hf)
    lf = jnp.log(lb + (1.0 - lb) * sig)
    k = (1.0 - lb) * jax.nn.sigmoid(-hf)
    q = jax.nn.silu(hq)
    b = _dg(mask.astype(F32), lf, 1, 0, HIGHEST)
    rid = lax.broadcasted_iota(jnp.int32, b.shape, 0)
    b_ref = jnp.sum(jnp.where(rid == c // 2 - 1, b, 0.0), axis=0, keepdims=True)
    b_last = jnp.sum(jnp.where(rid == c - 1, b, 0.0), axis=0, keepdims=True)
    attn = _nt(q * jnp.exp(b - b_ref), k * jnp.exp(b_ref - b))
    attn = jnp.where(mask, attn, 0.0)
    o = _nn(attn, hi) + _nt(q * jnp.exp(b), st)
    st_new = jnp.exp(b_last) * st + _tn(hi, k * jnp.exp(b_last - b))
    y = o * lax.rsqrt(jnp.mean(o * o, axis=-1, keepdims=True) + LN_EPS) * nw * jax.nn.silu(hgate)
    return y, st_new


def _head(ref, seg, h):
    lo = seg * D_GROUP + h * D_HEAD
    return ref[:, lo:lo + D_HEAD]


def _hgrn2_fwd(proj, logits, norm_w):
    s = proj.shape[0]
    nc = s // CHUNK

    def body(p_ref, lg_ref, nw_ref, y_ref, st_out_ref, st_scr):
        @pl.when(pl.program_id(0) == 0)
        def _():
            st_scr[...] = jnp.zeros_like(st_scr)

        for h in range(N_HEADS):
            lo = h * D_HEAD
            st = st_scr[h]
            st_out_ref[h] = st
            y, st_new = _hg_step(_head(p_ref, 0, h), _head(p_ref, 1, h), _head(p_ref, 2, h), _head(p_ref, 3, h),
                                 lg_ref[0:1, lo:lo + D_HEAD], lg_ref[1:2, lo:lo + D_HEAD],
                                 nw_ref[:, lo:lo + D_HEAD], st)
            y_ref[:, lo:lo + D_HEAD] = y
            st_scr[h] = st_new

    return pl.pallas_call(
        body, name="hgrn2_fwd", grid=(nc,),
        in_specs=[pl.BlockSpec((CHUNK, 4 * D_GROUP), lambda c: (c, 0)),
                  pl.BlockSpec((2, D_GROUP), lambda c: (0, 0)),
                  pl.BlockSpec((1, D_GROUP), lambda c: (0, 0))],
        out_specs=[pl.BlockSpec((CHUNK, D_GROUP), lambda c: (c, 0)),
                   pl.BlockSpec((None, N_HEADS, D_HEAD, D_HEAD), lambda c: (c, 0, 0, 0))],
        out_shape=[jax.ShapeDtypeStruct((s, 2 * D_GROUP), F32),
                   jax.ShapeDtypeStruct((nc, N_HEADS, D_HEAD, D_HEAD), F32)],
        scratch_shapes=[pltpu.VMEM((N_HEADS, D_HEAD, D_HEAD), F32)],
        compiler_params=_params(("arbitrary",)),
    )(proj, logits, norm_w)


def _hgrn2_bwd(proj, logits, norm_w, states, d_y):
    s = proj.shape[0]
    nc = s // CHUNK

    def body(p_ref, lg_ref, nw_ref, st_ref, dy_ref, dp_ref, dl_ref, dnw_ref, dst_scr):
        @pl.when(pl.program_id(0) == 0)
        def _():
            dst_scr[...] = jnp.zeros_like(dst_scr)
            dl_ref[...] = jnp.zeros_like(dl_ref)
            dnw_ref[...] = jnp.zeros_like(dnw_ref)

        for h in range(N_HEADS):
            lo = h * D_HEAD
            _, vjp = jax.vjp(_hg_step, _head(p_ref, 0, h), _head(p_ref, 1, h), _head(p_ref, 2, h),
                             _head(p_ref, 3, h), lg_ref[0:1, lo:lo + D_HEAD], lg_ref[1:2, lo:lo + D_HEAD],
                             nw_ref[:, lo:lo + D_HEAD], st_ref[h])
            d_hq, d_hf, d_hi, d_hg, d_l0, d_l1, d_nw, d_st = vjp((dy_ref[:, lo:lo + D_HEAD], dst_scr[h]))
            for seg, val in enumerate((d_hq, d_hf, d_hi, d_hg)):
                dp_ref[:, seg * D_GROUP + lo:seg * D_GROUP + lo + D_HEAD] = val
            dl_ref[0:1, lo:lo + D_HEAD] += d_l0
            dl_ref[1:2, lo:lo + D_HEAD] += d_l1
            dnw_ref[:, lo:lo + D_HEAD] += d_nw
            dst_scr[h] = d_st

    rev = lambda c: nc - 1 - c
    return pl.pallas_call(
        body, name="hgrn2_bwd", grid=(nc,),
        in_specs=[pl.BlockSpec((CHUNK, 4 * D_GROUP), lambda c: (rev(c), 0)),
                  pl.BlockSpec((2, D_GROUP), lambda c: (0, 0)),
                  pl.BlockSpec((1, D_GROUP), lambda c: (0, 0)),
                  pl.BlockSpec((None, N_HEADS, D_HEAD, D_HEAD), lambda c: (rev(c), 0, 0, 0)),
                  pl.BlockSpec((CHUNK, D_GROUP), lambda c: (rev(c), 0))],
        out_specs=[pl.BlockSpec((CHUNK, 4 * D_GROUP), lambda c: (rev(c), 0)),
                   pl.BlockSpec((2, D_GROUP), lambda c: (0, 0)),
                   pl.BlockSpec((1, D_GROUP), lambda c: (0, 0))],
        out_shape=[jax.ShapeDtypeStruct((s, D_IN_MAIN), F32), jax.ShapeDtypeStruct((2, D_GROUP), F32),
                   jax.ShapeDtypeStruct((1, D_GROUP), F32)],
        scratch_shapes=[pltpu.VMEM((N_HEADS, D_HEAD, D_HEAD), F32)],
        compiler_params=_params(("arbitrary",)),
    )(proj, logits, norm_w, states, d_y)


def _ml_step(qc, kc, v, mo, ig, fr, nw, ct, n, m):
    c = qc.shape[0]
    row, col = _chunk_masks(c)
    mask = col <= row
    eye = col == row
    q = qc * (D_HEAD ** -0.5)
    lf = jax.nn.log_sigmoid(fr)
    to_row = lambda t: jnp.sum(jnp.where(eye, t, 0.0), axis=0, keepdims=True)
    lf_row = to_row(lf)
    ig_row = to_row(ig)
    b_col = jnp.sum(jnp.where(mask, lf_row, 0.0), axis=1, keepdims=True)
    b_row = jnp.sum(jnp.where(row <= col, lf, 0.0), axis=0, keepdims=True)
    g = jnp.sum(lf, axis=0, keepdims=True)
    d = jnp.where(mask, b_col - b_row + ig_row, -jnp.inf)
    inter = b_col + m
    m_t = lax.stop_gradient(jnp.maximum(inter, jnp.max(d, axis=1, keepdims=True)))
    w = jnp.exp(d - m_t)
    sc = _nt(q, kc) * w
    w_inter = jnp.exp(inter - m_t)
    num = _nn(sc, v) + w_inter * _nt(q, ct)
    den = jnp.sum(sc, axis=1, keepdims=True) + w_inter * jnp.sum(q * n, axis=1, keepdims=True)
    h = num / jnp.maximum(jnp.abs(den), jnp.exp(-m_t))
    a = g - b_col + ig
    m_new = lax.stop_gradient(jnp.maximum(g + m, jnp.max(a, axis=0, keepdims=True)))
    decay = jnp.exp(g + m - m_new)
    wk = kc * jnp.exp(a - m_new)
    ct_new = decay * ct + _tn(v, wk)
    n_new = decay * n + jnp.sum(wk, axis=0, keepdims=True)
    mu = jnp.mean(h, axis=-1, keepdims=True)
    var = jnp.mean(jnp.square(h - mu), axis=-1, keepdims=True)
    y = jax.nn.sigmoid(mo) * ((h - mu) * lax.rsqrt(var + LN_EPS) * nw)
    return y, ct_new, n_new, m_new


def _gate_column(gates, lane, idx):
    return jnp.sum(jnp.where(lane == idx, gates, 0.0), axis=1, keepdims=True)


def _mlstm_fwd(qk, proj, gates, norm_w, y):
    s = proj.shape[0]
    nc = s // CHUNK

    def body(qk_ref, vo_ref, g_ref, nw_ref, _, y_ref, ct_out, n_out, m_out, ct_scr, n_scr, m_scr):
        @pl.when(pl.program_id(0) == 0)
        def _():
            ct_scr[...] = jnp.zeros_like(ct_scr)
            n_scr[...] = jnp.zeros_like(n_scr)
            m_scr[...] = jnp.full(m_scr.shape, NEG_BIG, F32)

        gates_blk = g_ref[...]
        lane = lax.broadcasted_iota(jnp.int32, gates_blk.shape, 1)
        for h in range(N_HEADS):
            lo = h * D_HEAD
            ct, n, m = ct_scr[h], n_scr[h], m_scr[h]
            ct_out[h] = ct
            n_out[h] = n
            m_out[h] = m
            yh, ct_new, n_new, m_new = _ml_step(
                _head(qk_ref, 0, h), _head(qk_ref, 1, h), _head(vo_ref, 0, h), _head(vo_ref, 1, h),
                _gate_column(gates_blk, lane, h), _gate_column(gates_blk, lane, N_HEADS + h),
                nw_ref[:, lo:lo + D_HEAD], ct, n, m)
            y_ref[:, lo:lo + D_HEAD] = yh
            ct_scr[h] = ct_new
            n_scr[h] = n_new
            m_scr[h] = m_new

    st = lambda r, w: pl.BlockSpec((None, N_HEADS, r, w), lambda c: (c, 0, 0, 0))
    return pl.pallas_call(
        body, name="mlstm_fwd", grid=(nc,),
        in_specs=[pl.BlockSpec((CHUNK, 2 * D_GROUP), lambda c: (c, 0)),
                  pl.BlockSpec((CHUNK, 2 * D_GROUP), lambda c: (c, 3)),
                  pl.BlockSpec((CHUNK, LANES), lambda c: (c, 0)),
                  pl.BlockSpec((1, D_GROUP), lambda c: (0, 0)),
                  pl.BlockSpec(memory_space=pl.ANY)],
        out_specs=[pl.BlockSpec((CHUNK, D_GROUP), lambda c: (c, 1)),
                   st(D_HEAD, D_HEAD), st(1, D_HEAD), st(1, 1)],
        out_shape=[jax.ShapeDtypeStruct((s, 2 * D_GROUP), F32),
                   jax.ShapeDtypeStruct((nc, N_HEADS, D_HEAD, D_HEAD), F32),
                   jax.ShapeDtypeStruct((nc, N_HEADS, 1, D_HEAD), F32),
                   jax.ShapeDtypeStruct((nc, N_HEADS, 1, 1), F32)],
        input_output_aliases={4: 0},
        scratch_shapes=[pltpu.VMEM((N_HEADS, D_HEAD, D_HEAD), F32), pltpu.VMEM((N_HEADS, 1, D_HEAD), F32),
                        pltpu.VMEM((N_HEADS, 1, 1), F32)],
        compiler_params=_params(("arbitrary",)),
    )(qk, proj, gates, norm_w, y)


def _mlstm_bwd(qk, proj, gates, norm_w, ct_s, n_s, m_s, d_y, d_proj):
    s = proj.shape[0]
    nc = s // CHUNK

    def body(qk_ref, vo_ref, g_ref, nw_ref, ct_ref, n_ref, m_ref, dy_ref, _,
             dp_ref, dqk_ref, dg_ref, dnw_ref, dct_scr, dn_scr):
        @pl.when(pl.program_id(0) == 0)
        def _():
            dct_scr[...] = jnp.zeros_like(dct_scr)
            dn_scr[...] = jnp.zeros_like(dn_scr)
            dnw_ref[...] = jnp.zeros_like(dnw_ref)

        gates_blk = g_ref[...]
        lane = lax.broadcasted_iota(jnp.int32, gates_blk.shape, 1)
        d_gates = jnp.zeros(gates_blk.shape, F32)
        for h in range(N_HEADS):
            lo = h * D_HEAD
            m = m_ref[h]
            step = lambda *a, m=m: _ml_step(*a, m)[:3]
            _, vjp = jax.vjp(step, _head(qk_ref, 0, h), _head(qk_ref, 1, h), _head(vo_ref, 0, h),
                             _head(vo_ref, 1, h), _gate_column(gates_blk, lane, h),
                             _gate_column(gates_blk, lane, N_HEADS + h), nw_ref[:, lo:lo + D_HEAD],
                             ct_ref[h], n_ref[h])
            d_q, d_k, d_v, d_o, d_gi, d_gf, d_nw, d_ct, d_n = vjp((dy_ref[:, lo:lo + D_HEAD], dct_scr[h], dn_scr[h]))
            dqk_ref[:, lo:lo + D_HEAD] = d_q
            dqk_ref[:, D_GROUP + lo:D_GROUP + lo + D_HEAD] = d_k
            dp_ref[:, lo:lo + D_HEAD] = d_v
            dp_ref[:, D_GROUP + lo:D_GROUP + lo + D_HEAD] = d_o
            d_gates = d_gates + jnp.where(lane == h, d_gi, 0.0) + jnp.where(lane == N_HEADS + h, d_gf, 0.0)
            dnw_ref[:, lo:lo + D_HEAD] += d_nw
            dct_scr[h] = d_ct
            dn_scr[h] = d_n
        dg_ref[...] = d_gates

    rev = lambda c: nc - 1 - c
    st = lambda r, w: pl.BlockSpec((None, N_HEADS, r, w), lambda c: (rev(c), 0, 0, 0))
    return pl.pallas_call(
        body, name="mlstm_bwd", grid=(nc,),
        in_specs=[pl.BlockSpec((CHUNK, 2 * D_GROUP), lambda c: (rev(c), 0)),
                  pl.BlockSpec((CHUNK, 2 * D_GROUP), lambda c: (rev(c), 3)),
                  pl.BlockSpec((CHUNK, LANES), lambda c: (rev(c), 0)),
                  pl.BlockSpec((1, D_GROUP), lambda c: (0, 0)),
                  st(D_HEAD, D_HEAD), st(1, D_HEAD), st(1, 1),
                  pl.BlockSpec((CHUNK, D_GROUP), lambda c: (rev(c), 1)),
                  pl.BlockSpec(memory_space=pl.ANY)],
        out_specs=[pl.BlockSpec((CHUNK, 2 * D_GROUP), lambda c: (rev(c), 3)),
                   pl.BlockSpec((CHUNK, 2 * D_GROUP), lambda c: (rev(c), 0)),
                   pl.BlockSpec((CHUNK, LANES), lambda c: (rev(c), 0)),
                   pl.BlockSpec((1, D_GROUP), lambda c: (0, 0))],
        out_shape=[jax.ShapeDtypeStruct(d_proj.shape, F32), jax.ShapeDtypeStruct((s, 2 * D_GROUP), F32),
                   jax.ShapeDtypeStruct((s, LANES), F32), jax.ShapeDtypeStruct((1, D_GROUP), F32)],
        input_output_aliases={8: 0},
        scratch_shapes=[pltpu.VMEM((N_HEADS, D_HEAD, D_HEAD), F32), pltpu.VMEM((N_HEADS, 1, D_HEAD), F32)],
        compiler_params=_params(("arbitrary",)),
    )(qk, proj, gates, norm_w, ct_s, n_s, m_s, d_y, d_proj)


LN_TOKENS = 512
ATT_TOKENS = 256


def _res_ln_fwd(xres, branch, g, b, name):
    s, dm = xres.shape
    tb = min(LN_TOKENS, s)

    def body(x_ref, br_ref, g_ref, b_ref, o_ref):
        o_ref[...] = _layer_norm(ALPHA * x_ref[...] + br_ref[...], g_ref[...], b_ref[...])

    tok = pl.BlockSpec((tb, dm), lambda i: (i, 0))
    vec = pl.BlockSpec((1, dm), lambda i: (0, 0))
    return pl.pallas_call(
        body, name=name, grid=(s // tb,), in_specs=[tok, tok, vec, vec], out_specs=tok,
        out_shape=jax.ShapeDtypeStruct((s, dm), F32), compiler_params=_params(("parallel",)),
    )(xres, branch, g, b)


def _res_ln_bwd(xres, branch, g, b, d_out, name):
    s, dm = xres.shape
    tb = min(LN_TOKENS, s)

    def body(x_ref, br_ref, g_ref, b_ref, do_ref, dz_ref, dg_ref, db_ref):
        @pl.when(pl.program_id(0) == 0)
        def _():
            dg_ref[...] = jnp.zeros_like(dg_ref)
            db_ref[...] = jnp.zeros_like(db_ref)

        z = ALPHA * x_ref[...] + br_ref[...]
        _, vjp = jax.vjp(_layer_norm, z, g_ref[...], b_ref[...])
        d_z, d_g, d_b = vjp(do_ref[...])
        dz_ref[...] = d_z
        dg_ref[...] += d_g
        db_ref[...] += d_b

    tok = pl.BlockSpec((tb, dm), lambda i: (i, 0))
    vec = pl.BlockSpec((1, dm), lambda i: (0, 0))
    return pl.pallas_call(
        body, name=name, grid=(s // tb,), in_specs=[tok, tok, vec, vec, tok], out_specs=[tok, vec, vec],
        out_shape=[jax.ShapeDtypeStruct((s, dm), F32), jax.ShapeDtypeStruct((1, dm), F32),
                   jax.ShapeDtypeStruct((1, dm), F32)],
        compiler_params=_params(("arbitrary",)),
    )(xres, branch, g, b, d_out)


def _loss_tail(xres, branch, g, b, target):
    s, dm = xres.shape
    tb = min(LN_TOKENS, s)

    def loss_fn(z, gg, bb, tgt):
        err = jnp.square(_layer_norm(z, gg, bb) - tgt)
        return 0.5 * jnp.sum(jnp.mean(err, axis=-1, keepdims=True), axis=0, keepdims=True)

    def body(x_ref, br_ref, g_ref, b_ref, t_ref, loss_ref, dz_ref, dg_ref, db_ref):
        @pl.when(pl.program_id(0) == 0)
        def _():
            loss_ref[...] = jnp.zeros_like(loss_ref)
            dg_ref[...] = jnp.zeros_like(dg_ref)
            db_ref[...] = jnp.zeros_like(db_ref)

        z = ALPHA * x_ref[...] + br_ref[...]
        tgt = t_ref[...]
        loss, vjp = jax.vjp(lambda zz, gg, bb: loss_fn(zz, gg, bb, tgt), z, g_ref[...], b_ref[...])
        d_z, d_g, d_b = vjp(jnp.ones((1, 1), F32))
        loss_ref[...] += loss
        dz_ref[...] = d_z
        dg_ref[...] += d_g
        db_ref[...] += d_b

    tok = pl.BlockSpec((tb, dm), lambda i: (i, 0))
    vec = pl.BlockSpec((1, dm), lambda i: (0, 0))
    one = pl.BlockSpec((1, 1), lambda i: (0, 0))
    return pl.pallas_call(
        body, name="loss_tail", grid=(s // tb,), in_specs=[tok, tok, vec, vec, tok],
        out_specs=[one, tok, vec, vec],
        out_shape=[jax.ShapeDtypeStruct((1, 1), F32), jax.ShapeDtypeStruct((s, dm), F32),
                   jax.ShapeDtypeStruct((1, dm), F32), jax.ShapeDtypeStruct((1, dm), F32)],
        compiler_params=_params(("arbitrary",)),
    )(xres, branch, g, b, target)


def _att_head(q, k, v):
    sc = _nt(q, k) * (CA_DH ** -0.5)
    return _nn(jax.nn.softmax(sc, axis=-1), v)


def _att_fwd(q, kv):
    s = q.shape[0]
    tb = min(ATT_TOKENS, s)

    def body(q_ref, kv_ref, o_ref):
        for h in range(CA_HEADS):
            lo = h * CA_DH
            o_ref[:, lo:lo + CA_DH] = _att_head(q_ref[:, lo:lo + CA_DH], kv_ref[:, lo:lo + CA_DH],
                                                kv_ref[:, D_MODEL + lo:D_MODEL + lo + CA_DH])

    tok = pl.BlockSpec((tb, D_MODEL), lambda i: (i, 0))
    return pl.pallas_call(
        body, name="att_fwd", grid=(s // tb,),
        in_specs=[tok, pl.BlockSpec((N_MEM, 2 * D_MODEL), lambda i: (0, 0))], out_specs=tok,
        out_shape=jax.ShapeDtypeStruct((s, D_MODEL), F32), compiler_params=_params(("parallel",)),
    )(q, kv)


def _att_bwd(q, kv, d_o):
    s = q.shape[0]
    tb = min(ATT_TOKENS, s)

    def body(q_ref, kv_ref, do_ref, dq_ref, dkv_ref):
        @pl.when(pl.program_id(0) == 0)
        def _():
            dkv_ref[...] = jnp.zeros_like(dkv_ref)

        for h in range(CA_HEADS):
            lo = h * CA_DH
            vlo = D_MODEL + lo
            _, vjp = jax.vjp(_att_head, q_ref[:, lo:lo + CA_DH], kv_ref[:, lo:lo + CA_DH],
                             kv_ref[:, vlo:vlo + CA_DH])
            d_q, d_k, d_v = vjp(do_ref[:, lo:lo + CA_DH])
            dq_ref[:, lo:lo + CA_DH] = d_q
            dkv_ref[:, lo:lo + CA_DH] += d_k
            dkv_ref[:, vlo:vlo + CA_DH] += d_v

    tok = pl.BlockSpec((tb, D_MODEL), lambda i: (i, 0))
    mem = pl.BlockSpec((N_MEM, 2 * D_MODEL), lambda i: (0, 0))
    return pl.pallas_call(
        body, name="att_bwd", grid=(s // tb,), in_specs=[tok, mem, tok], out_specs=[tok, mem],
        out_shape=[jax.ShapeDtypeStruct((s, D_MODEL), F32), jax.ShapeDtypeStruct((N_MEM, 2 * D_MODEL), F32)],
        compiler_params=_params(("arbitrary",)),
    )(q, kv, d_o)


def _local_step(x, mem, target, w, late_weights=None, on_ffn_grads=None, on_mid_grads=None):
    w = dict(w)
    s = x.shape[0]
    tm = min(512, s)
    tt = min(512, s)
    proj = _matmul_nn(x, w["w_in_main"], w["b_in_main"], tm, 512, "proj")
    gates = _matmul_nn(x, w["w_in_gate"], w["b_in_gate"], tm, LANES, "proj_gates")
    qk = _ml_conv_fwd(proj, w["ml_conv_w"], w["ml_conv_b"])
    y, hg_states = _hgrn2_fwd(proj, w["hg_lb_logits"], w["hg_norm_w"])
    y, ct_s, n_s, m_s = _mlstm_fwd(qk, proj, gates, w["ml_norm_w"], y)
    if late_weights is not None:
        w.update(late_weights(y))
    mix =_matmul_nn(y, w["w_out"], None, tm, D_MODEL, "mix")
    x1 = _res_ln_fwd(x, mix, w["ln1_g"], w["ln1_b"], "ln1_fwd")
    kv = _matmul_nn(mem, w["ca_wkv"], None, N_MEM, CA_DH, "kv")
    q = _matmul_nn(x1, w["ca_wq"], None, tm, D_MODEL, "ca_q")
    att = _att_fwd(q, kv)
    ca = _matmul_nn(att, w["ca_wo"], None, tm, D_MODEL, "ca_out")
    x2 = _res_ln_fwd(x1, ca, w["ln2_g"], w["ln2_b"], "ln2_fwd")
    u = _matmul_nn(x2, w["ffn_w_up"], None, tm, UP_SHARD_P, "ffn_up")
    hid = _ffn_conv_fwd(u, w["ffn_conv_w"], w["ffn_conv_b"])
    ff = _matmul_nn(hid, w["ffn_w_down"], None, tm, D_MODEL, "ffn_down")
    loss, d_z3, d_ln3_g, d_ln3_b = _loss_tail(x2, ff, w["ln3_g"], w["ln3_b"], target)
    grads = {"ln3_g": d_ln3_g, "ln3_b": d_ln3_b}
    grads["ffn_w_down"] = _matmul_tn(hid, d_z3, 1536, D_MODEL, tt, "d_w_down")
    d_hid = _matmul_nt([(d_z3, w["ffn_w_down"])], None, 1.0, tm, 1536, "d_hid")
    d_ug, d_uv, d_cwg, d_cwv, d_cbg, d_cbv = _ffn_conv_bwd(u, w["ffn_conv_w"], w["ffn_conv_b"], d_hid)
    grads["ffn_conv_w"] = jnp.concatenate([d_cwg, d_cwv], axis=-1)
    grads["ffn_conv_b"] = jnp.concatenate([d_cbg, d_cbv], axis=-1)
    d_w_up = _matmul_tn(x2, d_ug, D_MODEL, UP_SHARD_P, tt, "d_w_up_gate", shards=N_DEV)
    grads["ffn_w_up"] = _matmul_tn(x2, d_uv, D_MODEL, UP_SHARD_P, tt, "d_w_up_val", shards=N_DEV,
                                   shard0=N_DEV // 2, into=d_w_up)
    d_x2 = _matmul_nt([(d_ug, w["ffn_w_up"], 0), (d_uv, w["ffn_w_up"], N_DEV // 2)], d_z3, ALPHA,
                      min(256, s), D_MODEL, "d_x2")
    if on_ffn_grads is not None:
        d_x2 = on_ffn_grads(grads, d_x2)
    d_z2, grads["ln2_g"], grads["ln2_b"] = _res_ln_bwd(x1, ca, w["ln2_g"], w["ln2_b"], d_x2, "ln2_bwd")
    grads["ca_wo"] = _matmul_tn(att, d_z2, D_MODEL, D_MODEL, tt, "d_ca_wo")
    d_att = _matmul_nt([(d_z2, w["ca_wo"])], None, 1.0, tm, D_MODEL, "d_att")
    d_q, d_kv = _att_bwd(q, kv, d_att)
    grads["ca_wq"] = _matmul_tn(x1, d_q, D_MODEL, D_MODEL, tt, "d_ca_wq")
    grads["ca_wkv"] = _matmul_tn(mem, d_kv, D_MODEL, CA_DH, N_MEM, "d_ca_wkv", shards=N_DEV)
    d_x1 = _matmul_nt([(d_q, w["ca_wq"])], d_z2, ALPHA, tm, D_MODEL, "d_x1")
    d_z1, grads["ln1_g"], grads["ln1_b"] = _res_ln_bwd(x, mix, w["ln1_g"], w["ln1_b"], d_x1, "ln1_bwd")
    grads["w_out"] = _matmul_tn(y, d_z1, D_MODEL, D_MODEL, tt, "d_w_out")
    if on_mid_grads is not None:
        d_z1 = on_mid_grads(grads, d_z1)
    d_y = _matmul_nt([(d_z1, w["w_out"])], None, 1.0, tm, D_MODEL, "d_y")
    d_proj, grads["hg_lb_logits"], grads["hg_norm_w"] = _hgrn2_bwd(
        proj, w["hg_lb_logits"], w["hg_norm_w"], hg_states, d_y)
    d_proj, d_qk, d_gates, grads["ml_norm_w"] = _mlstm_bwd(
        qk, proj, gates, w["ml_norm_w"], ct_s, n_s, m_s, d_y, d_proj)
    d_proj, grads["ml_conv_w"], grads["ml_conv_b"] = _ml_conv_bwd(
        proj, w["ml_conv_w"], w["ml_conv_b"], d_qk, d_proj)
    grads["w_in_main"], grads["b_in_main"] = _matmul_tn(x, d_proj, D_MODEL, 512, tt, "d_w_in", colsum=True)
    grads["w_in_gate"], grads["b_in_gate"] = _matmul_tn(x, d_gates, D_MODEL, LANES, tt, "d_w_in_gates", colsum=True)
    grad_x = _matmul_nt([(d_proj, w["w_in_main"]), (d_gates, w["w_in_gate"])], d_z1, ALPHA, tm, D_MODEL, "d_x")
    return loss, grad_x, grads


HBM_SPEC = pl.BlockSpec(memory_space=pltpu.HBM)


def _coords():
    return lax.axis_index("x"), lax.axis_index("y"), lax.axis_index("c")


def _other_chips(x, y):
    return [(1 - x, y), (x, 1 - y), (1 - x, 1 - y)]


def _all_gather_two_level(shards, name):
    na = len(shards)

    def body(*refs):
        x_refs, out_refs = refs[:na], refs[na:2 * na]
        send_sems, recv_sems, local_sems = refs[2 * na:]
        x, y, c = _coords()
        me, sibling = (x, y, c), (x, y, 1 - c)
        chips = _other_chips(x, y)

        def copy(a, k, block, to, own=False):
            slot = out_refs[a].at[4 * block[0] + 2 * block[1] + block[2]]
            return pltpu.make_async_remote_copy(
                src_ref=x_refs[a] if own else slot, dst_ref=slot,
                send_sem=send_sems.at[7 * a + k], recv_sem=recv_sems.at[7 * a + k],
                device_id=to, device_id_type=MESH)

        mine = [pltpu.make_async_copy(x_refs[a], out_refs[a].at[4 * x + 2 * y + c], local_sems.at[a])
                for a in range(na)]
        for cp in mine:
            cp.start()
        first = []
        for a in range(na):
            first.append(copy(a, 0, me, sibling, own=True))
            first += [copy(a, 1 + j, me, (*chip, c), own=True) for j, chip in enumerate(chips)]
        for cp in first:
            cp.start()
        passed = []
        for j, chip in enumerate(chips):
            for a in range(na):
                copy(a, 1 + j, (*chip, c), me).wait_recv()
                fwd = copy(a, 4 + j, (*chip, c), sibling)
                fwd.start()
                passed.append(fwd)
        for a in range(na):
            copy(a, 0, sibling, me).wait_recv()
            for j, chip in enumerate(chips):
                copy(a, 4 + j, (*chip, 1 - c), me).wait_recv()
        for cp in first + passed:
            cp.wait_send()
        for cp in mine:
            cp.wait()

    return pl.pallas_call(
        body, name=name,
        out_shape=[jax.ShapeDtypeStruct((N_DEV,) + t.shape, t.dtype) for t in shards],
        in_specs=[HBM_SPEC] * na, out_specs=[HBM_SPEC] * na,
        scratch_shapes=[pltpu.SemaphoreType.DMA((7 * na,)), pltpu.SemaphoreType.DMA((7 * na,)),
                        pltpu.SemaphoreType.DMA((na,))],
    )(*shards)


def _all_gather_direct(vec, name):
    r, n = vec.shape

    def body(x_ref, out_ref, send_sems, recv_sems, local_sem):
        x, y, c = _coords()
        flip = lambda v, bit: 1 - v if bit else v
        me = 4 * x + 2 * y + c
        mine = pltpu.make_async_copy(x_ref, out_ref.at[me], local_sem)
        mine.start()
        copies = []
        for d in range(1, N_DEV):
            peer = (flip(x, d & 4), flip(y, d & 2), flip(c, d & 1))
            peer_slot = 4 * peer[0] + 2 * peer[1] + peer[2]
            out_going = pltpu.make_async_remote_copy(
                src_ref=x_ref, dst_ref=out_ref.at[me], send_sem=send_sems.at[d - 1],
                recv_sem=recv_sems.at[d - 1], device_id=peer, device_id_type=MESH)
            incoming = pltpu.make_async_remote_copy(
                src_ref=x_ref, dst_ref=out_ref.at[peer_slot], send_sem=send_sems.at[d - 1],
                recv_sem=recv_sems.at[d - 1], device_id=peer, device_id_type=MESH)
            out_going.start()
            copies.append((out_going, incoming))
        for out_going, incoming in copies:
            incoming.wait_recv()
            out_going.wait_send()
        mine.wait()

    return pl.pallas_call(
        body, name=name, out_shape=jax.ShapeDtypeStruct((N_DEV, r, n), vec.dtype),
        in_specs=[HBM_SPEC], out_specs=HBM_SPEC,
        scratch_shapes=[pltpu.SemaphoreType.DMA((7,)), pltpu.SemaphoreType.DMA((7,)), pltpu.SemaphoreType.DMA],
    )(vec)


def _exchange_with_sibling(parts):
    na = len(parts)

    def body(*refs):
        p_refs, got_refs = refs[:na], refs[na:2 * na]
        send_sems, recv_sems = refs[2 * na:]
        x, y, c = _coords()
        copies = []
        for a in range(na):
            for chip in range(N_CHIPS):
                cp = pltpu.make_async_remote_copy(
                    src_ref=p_refs[a].at[2 * chip + (1 - c)], dst_ref=got_refs[a].at[chip],
                    send_sem=send_sems.at[N_CHIPS * a + chip], recv_sem=recv_sems.at[N_CHIPS * a + chip],
                    device_id=(x, y, 1 - c), device_id_type=MESH)
                cp.start()
                copies.append(cp)
        for cp in copies:
            cp.wait()

    return pl.pallas_call(
        body, name="grad_exchange_sibling",
        out_shape=[jax.ShapeDtypeStruct((N_CHIPS,) + t.shape[1:], t.dtype) for t in parts],
        in_specs=[HBM_SPEC] * na, out_specs=[HBM_SPEC] * na,
        scratch_shapes=[pltpu.SemaphoreType.DMA((N_CHIPS * na,)), pltpu.SemaphoreType.DMA((N_CHIPS * na,))],
    )(*parts)


def _exchange_with_chips(sums):
    na = len(sums)

    def body(*refs):
        a_refs, out_refs = refs[:na], refs[na:2 * na]
        send_sems, recv_sems = refs[2 * na:]
        x, y, c = _coords()
        copies = []
        for a in range(na):
            for j, (cx, cy) in enumerate(_other_chips(x, y)):
                cp = pltpu.make_async_remote_copy(
                    src_ref=a_refs[a].at[2 * cx + cy], dst_ref=out_refs[a].at[j],
                    send_sem=send_sems.at[3 * a + j], recv_sem=recv_sems.at[3 * a + j],
                    device_id=(cx, cy, c), device_id_type=MESH)
                cp.start()
                copies.append(cp)
        for cp in copies:
            cp.wait()

    return pl.pallas_call(
        body, name="grad_exchange_chips",
        out_shape=[jax.ShapeDtypeStruct((3,) + t.shape[1:], t.dtype) for t in sums],
        in_specs=[HBM_SPEC] * na, out_specs=[HBM_SPEC] * na,
        scratch_shapes=[pltpu.SemaphoreType.DMA((3 * na,)), pltpu.SemaphoreType.DMA((3 * na,))],
    )(*sums)


SEM_SPEC = pl.BlockSpec(memory_space=pltpu.SEMAPHORE)
ANY_SPEC = pl.BlockSpec(memory_space=pl.ANY)
SIDE_EFFECT = pltpu.SideEffectType.DATAFLOW_SIDE_EFFECTING


def _peer(x, y, c, d):
    flip = lambda v, bit: 1 - v if bit else v
    p = (flip(x, d & 4), flip(y, d & 2), flip(c, d & 1))
    return p, 4 * p[0] + 2 * p[1] + p[2]


def _direct_copies(gather, src_refs, land_refs, send_sems, recv_sems):
    x, y, c = _coords()
    me = 4 * x + 2 * y + c
    copies = []
    for a in range(len(src_refs)):
        for d in range(1, N_DEV):
            peer, peer_slot = _peer(x, y, c, d)
            copies.append(pltpu.make_async_remote_copy(
                src_ref=src_refs[a] if gather else src_refs[a].at[peer_slot],
                dst_ref=land_refs[a].at[me] if gather else land_refs[a].at[d - 1],
                send_sem=send_sems.at[7 * a + d - 1], recv_sem=recv_sems.at[7 * a + d - 1],
                device_id=peer, device_id_type=MESH))
    return copies


def _hbm(t):
    return pltpu.HBM(t.shape, t.dtype)


def _direct_start(gather, arrays, through, name):
    na = len(arrays)
    lands = [lax.empty((N_DEV,) + t.shape if gather else (N_DEV - 1,) + t.shape[1:], t.dtype) for t in arrays]
    n_io = 2 * na + 1

    def body(*refs):
        for cp in _direct_copies(gather, refs[:na], refs[na:2 * na], refs[n_io], refs[n_io + 1]):
            cp.start()

    ins = [pltpu.with_memory_space_constraint(t, pltpu.HBM) for t in (*arrays, *lands, through)]
    sems = pltpu.SemaphoreType.DMA((7 * na,))
    res = pl.pallas_call(
        body, name=name, out_shape=(sems, sems, *[_hbm(t) for t in ins]),
        in_specs=[HBM_SPEC] * n_io, out_specs=(SEM_SPEC, SEM_SPEC, *[HBM_SPEC] * n_io),
        input_output_aliases={i: 2 + i for i in range(n_io)},
        compiler_params=pltpu.CompilerParams(has_side_effects=SIDE_EFFECT),
    )(*ins)
    return (res[0], res[1], list(res[2:2 + na]), list(res[2 + na:2 + 2 * na])), res[2 + 2 * na]


def _direct_wait(gather, started, after, name):
    send_sems, recv_sems, arrays, lands = started
    na = len(arrays)

    def body(*refs):
        for cp in _direct_copies(gather, refs[:na], refs[na:2 * na], refs[2 * na], refs[2 * na + 1]):
            cp.wait_send()
            cp.wait_recv()

    res = pl.pallas_call(
        body, name=name, out_shape=tuple(_hbm(t) for t in (*arrays, *lands)),
        in_specs=[HBM_SPEC] * (2 * na) + [SEM_SPEC, SEM_SPEC, ANY_SPEC], out_specs=tuple([HBM_SPEC] * (2 * na)),
        input_output_aliases={i: i for i in range(2 * na)},
        compiler_params=pltpu.CompilerParams(has_side_effects=SIDE_EFFECT),
    )(*arrays, *lands, send_sems, recv_sems, after)
    return list(res[:na]), list(res[na:])


def _row_tile(rows):
    for t in (256, 176, 128):
        if rows % t == 0 and rows > t:
            return t
    return rows


def _add_sibling(core, parts, got, name):
    _, r, c = parts.shape
    tr = _row_tile(r)

    def body(core_ref, p_ref, g_ref, o_ref):
        o_ref[...] = p_ref[...] + g_ref[...]

    return pl.pallas_call(
        body, name=name,
        grid_spec=pltpu.PrefetchScalarGridSpec(
            num_scalar_prefetch=1, grid=(N_CHIPS, r // tr),
            in_specs=[pl.BlockSpec((None, tr, c), lambda i, j, core_ref: (2 * i + core_ref[0], j, 0)),
                      pl.BlockSpec((None, tr, c), lambda i, j, core_ref: (i, j, 0))],
            out_specs=pl.BlockSpec((None, tr, c), lambda i, j, core_ref: (i, j, 0))),
        out_shape=jax.ShapeDtypeStruct((N_CHIPS, r, c), F32),
        compiler_params=_params(("parallel", "parallel")),
    )(core, parts, got)


def _adamw_math(g, w, m, v):
    m_new = ADAM_B1 * m + (1.0 - ADAM_B1) * g
    v_new = ADAM_B2 * v + (1.0 - ADAM_B2) * jnp.square(g)
    m_hat = m_new / (1.0 - ADAM_B1 ** ADAM_STEP)
    v_hat = v_new / (1.0 - ADAM_B2 ** ADAM_STEP)
    delta = -ADAM_LR * (m_hat / (jnp.sqrt(v_hat) + ADAM_EPS) + ADAM_WD * w)
    return delta, m_new, v_new


def _adamw_sharded(chip, sums, got, w, m, v, name):
    r, c = w.shape
    tr = _row_tile(r)
    n_got = got.shape[0]

    def body(chip_ref, s_ref, g_ref, w_ref, m_ref, v_ref, go_ref, d_ref, nm_ref, nv_ref):
        g = s_ref[...]
        for i in range(n_got):
            g = g + g_ref[i]
        delta, m_new, v_new = _adamw_math(g, w_ref[...], m_ref[...], v_ref[...])
        go_ref[...] = g
        d_ref[...] = delta
        nm_ref[...] = m_new
        nv_ref[...] = v_new

    blk = pl.BlockSpec((tr, c), lambda i, chip_ref: (i, 0))
    out = jax.ShapeDtypeStruct((r, c), F32)
    return pl.pallas_call(
        body, name=name,
        grid_spec=pltpu.PrefetchScalarGridSpec(
            num_scalar_prefetch=1, grid=(r // tr,),
            in_specs=[pl.BlockSpec((None, tr, c), lambda i, chip_ref: (chip_ref[0], i, 0)),
                      pl.BlockSpec((n_got, tr, c), lambda i, chip_ref: (0, i, 0)), blk, blk, blk],
            out_specs=[blk, blk, blk, blk]),
        out_shape=[out, out, out, out],
        compiler_params=_params(("parallel",)),
    )(chip, sums, got, w, m, v)


def _adamw_replicated(parts, w, m, v):
    p, r, c = parts.shape

    def body(p_ref, w_ref, m_ref, v_ref, g_ref, d_ref, nm_ref, nv_ref):
        g = p_ref[0]
        for i in range(1, p):
            g = g + p_ref[i]
        delta, m_new, v_new = _adamw_math(g, w_ref[...], m_ref[...], v_ref[...])
        g_ref[...] = g
        d_ref[...] = delta
        nm_ref[...] = m_new
        nv_ref[...] = v_new

    blk = pl.BlockSpec((r, c), lambda i: (0, 0))
    out = jax.ShapeDtypeStruct((r, c), F32)
    return pl.pallas_call(
        body, name="adamw_replicated", grid=(1,),
        in_specs=[pl.BlockSpec((p, r, c), lambda i: (0, 0, 0)), blk, blk, blk],
        out_specs=[blk, blk, blk, blk], out_shape=[out, out, out, out],
        compiler_params=_params(("arbitrary",)),
    )(parts, w, m, v)


SHARDED_NAMES = ("w_in", "ml_conv_w", "w_out", "ca_wq", "ca_wkv", "ca_wo", "ffn_w_up", "ffn_conv_w", "ffn_w_down")
SMALL_NAMES = ("b_in", "hg_lb_logits", "hg_norm_w", "ml_conv_b", "ml_norm_w", "ln1_g", "ln1_b",
               "ln2_g", "ln2_b", "ffn_conv_b", "ln3_g", "ln3_b")
WEIGHT_NAMES = ("w_in", "b_in", "hg_lb_logits", "hg_norm_w", "ml_conv_w", "ml_conv_b", "ml_norm_w", "w_out",
                "ln1_g", "ln1_b", "ca_wq", "ca_wkv", "ca_wo", "ln2_g", "ln2_b", "ffn_w_up", "ffn_conv_w",
                "ffn_conv_b", "ffn_w_down", "ln3_g", "ln3_b")
PAD_TO = {"w_in": W_IN_SHARD_P, "ffn_w_up": UP_SHARD_P, "ffn_conv_w": UP_SHARD_P}
SMALL_ROWS = 24
SMALL_W = D_MODEL


def _shard_2d(name, block):
    t = block[0]
    if name in PAD_TO:
        t = jnp.pad(t, ((0, 0), (0, PAD_TO[name] - t.shape[1])))
    return t


def _shard_like(name, t, like):
    return t[:, :like.shape[2]][None]


def _pad_cols(t, width):
    return jnp.pad(t, ((0, 0), (0, width - t.shape[1])))


FIRST_NAMES = ("w_in", "ml_conv_w")
LATE_NAMES = ("w_out", "ca_wq", "ca_wkv", "ca_wo", "ffn_w_up", "ffn_conv_w", "ffn_w_down")
FFN_NAMES = ("ffn_w_up", "ffn_w_down", "ffn_conv_w")
MID_NAMES = ("ca_wo", "ca_wq", "ca_wkv", "w_out")


def _first_weights(g, small):
    w = dict(small)
    w_in = jnp.concatenate([g["w_in"][j, :, :W_IN_SHARD] for j in range(N_DEV)], axis=1)
    w["w_in_main"] = w_in[:, :D_IN_MAIN]
    w["w_in_gate"] = _pad_cols(w_in[:, D_IN_MAIN:], LANES)
    w["b_in_main"] = small["b_in"][:, :D_IN_MAIN]
    w["b_in_gate"] = _pad_cols(small["b_in"][:, D_IN_MAIN:], LANES)
    w["ml_conv_w"] = jnp.transpose(g["ml_conv_w"], (1, 0, 2)).reshape(ML_CONV, 2 * D_GROUP)
    return w


def _late_weights(g, small):
    w = {}
    for n in ("w_out", "ca_wq", "ca_wo"):
        w[n] = g[n].reshape(D_MODEL, D_MODEL)
    w["ca_wkv"] = g["ca_wkv"]
    w["ffn_w_up"] = g["ffn_w_up"]
    down = g["ffn_w_down"].reshape(N_DEV // 2, UP_SHARD, D_MODEL)
    w["ffn_w_down"] = jnp.pad(down, ((0, 0), (0, UP_SHARD_P - UP_SHARD), (0, 0))).reshape(D_FF_P, D_MODEL)
    w["ffn_conv_w"] = jnp.transpose(g["ffn_conv_w"], (1, 0, 2)).reshape(FFN_CONV, D_UP_P)
    w["ffn_conv_b"] = _pad_cols(small["ffn_conv_b"].reshape(N_DEV, UP_SHARD), UP_SHARD_P).reshape(1, D_UP_P)
    return w


def _whole_weights(g, small):
    return {**_first_weights(g, small), **_late_weights(g, small)}


def _owner_stack(n, grads):
    if n == "w_in":
        w_in = jnp.concatenate([grads["w_in_main"], grads["w_in_gate"][:, :D_IN - D_IN_MAIN]], axis=1)
        return jnp.stack([_pad_cols(w_in[:, j * W_IN_SHARD:(j + 1) * W_IN_SHARD], W_IN_SHARD_P)
                          for j in range(N_DEV)])
    if n in ("w_out", "ca_wq", "ca_wo"):
        return grads[n].reshape(N_DEV, D_MODEL // N_DEV, D_MODEL)
    if n == "ffn_w_down":
        down = grads[n].reshape(N_DEV // 2, UP_SHARD_P, D_MODEL)[:, :UP_SHARD]
        return down.reshape(N_DEV, D_FF // N_DEV, D_MODEL)
    if n == "ml_conv_w":
        return jnp.transpose(grads[n].reshape(ML_CONV, N_DEV, LANES), (1, 0, 2))
    if n == "ffn_conv_w":
        return jnp.transpose(grads[n].reshape(FFN_CONV, N_DEV, UP_SHARD_P), (1, 0, 2))
    return grads[n]


def _owner_stacks(grads):
    return {n: _owner_stack(n, grads) for n in SHARDED_NAMES}


def _small_grads(grads):
    out = {n: grads[n] for n in SMALL_NAMES if n in grads}
    out["b_in"] = jnp.concatenate([grads["b_in_main"], grads["b_in_gate"][:, :D_IN - D_IN_MAIN]], axis=1)
    out["ffn_conv_b"] = grads["ffn_conv_b"].reshape(N_DEV, UP_SHARD_P)[:, :UP_SHARD].reshape(1, D_UP)
    return out


def _pack_small(p, extra=None):
    flat = [p[n].reshape(-1) for n in SMALL_NAMES]
    if extra is not None:
        flat.append(extra.reshape(-1))
    flat = jnp.concatenate(flat)
    return jnp.pad(flat, (0, SMALL_ROWS * SMALL_W - flat.shape[0])).reshape(SMALL_ROWS, SMALL_W)


def _unpack_small(slab, like):
    out = {}
    flat = slab.reshape(-1)
    o = 0
    for n in SMALL_NAMES:
        out[n] = flat[o:o + like[n].size].reshape(like[n].shape)
        o += like[n].size
    return out, flat[o]


def kernel(x, mem, w_in, b_in, hg_lb_logits, hg_norm_w, ml_conv_w, ml_conv_b, ml_norm_w, w_out, ln1_g, ln1_b, ca_wq, ca_wkv, ca_wo, ln2_g, ln2_b, ffn_w_up, ffn_conv_w, ffn_conv_b, ffn_w_down, ln3_g, ln3_b, loss_target, m_w_in, m_b_in, m_hg_lb_logits, m_hg_norm_w, m_ml_conv_w, m_ml_conv_b, m_ml_norm_w, m_w_out, m_ln1_g, m_ln1_b, m_ca_wq, m_ca_wkv, m_ca_wo, m_ln2_g, m_ln2_b, m_ffn_w_up, m_ffn_conv_w, m_ffn_conv_b, m_ffn_w_down, m_ln3_g, m_ln3_b, v_w_in, v_b_in, v_hg_lb_logits, v_hg_norm_w, v_ml_conv_w, v_ml_conv_b, v_ml_norm_w, v_w_out, v_ln1_g, v_ln1_b, v_ca_wq, v_ca_wkv, v_ca_wo, v_ln2_g, v_ln2_b, v_ffn_w_up, v_ffn_conv_w, v_ffn_conv_b, v_ffn_w_down, v_ln3_g, v_ln3_b):
    params = dict(w_in=w_in, b_in=b_in, hg_lb_logits=hg_lb_logits, hg_norm_w=hg_norm_w, ml_conv_w=ml_conv_w,
                  ml_conv_b=ml_conv_b, ml_norm_w=ml_norm_w, w_out=w_out, ln1_g=ln1_g, ln1_b=ln1_b, ca_wq=ca_wq,
                  ca_wkv=ca_wkv, ca_wo=ca_wo, ln2_g=ln2_g, ln2_b=ln2_b, ffn_w_up=ffn_w_up, ffn_conv_w=ffn_conv_w,
                  ffn_conv_b=ffn_conv_b, ffn_w_down=ffn_w_down, ln3_g=ln3_g, ln3_b=ln3_b)
    mom1 = dict(w_in=m_w_in, b_in=m_b_in, hg_lb_logits=m_hg_lb_logits, hg_norm_w=m_hg_norm_w,
                ml_conv_w=m_ml_conv_w, ml_conv_b=m_ml_conv_b, ml_norm_w=m_ml_norm_w, w_out=m_w_out, ln1_g=m_ln1_g,
                ln1_b=m_ln1_b, ca_wq=m_ca_wq, ca_wkv=m_ca_wkv, ca_wo=m_ca_wo, ln2_g=m_ln2_g, ln2_b=m_ln2_b,
                ffn_w_up=m_ffn_w_up, ffn_conv_w=m_ffn_conv_w, ffn_conv_b=m_ffn_conv_b, ffn_w_down=m_ffn_w_down,
                ln3_g=m_ln3_g, ln3_b=m_ln3_b)
    mom2 = dict(w_in=v_w_in, b_in=v_b_in, hg_lb_logits=v_hg_lb_logits, hg_norm_w=v_hg_norm_w,
                ml_conv_w=v_ml_conv_w, ml_conv_b=v_ml_conv_b, ml_norm_w=v_ml_norm_w, w_out=v_w_out, ln1_g=v_ln1_g,
                ln1_b=v_ln1_b, ca_wq=v_ca_wq, ca_wkv=v_ca_wkv, ca_wo=v_ca_wo, ln2_g=v_ln2_g, ln2_b=v_ln2_b,
                ffn_w_up=v_ffn_w_up, ffn_conv_w=v_ffn_conv_w, ffn_conv_b=v_ffn_conv_b, ffn_w_down=v_ffn_w_down,
                ln3_g=v_ln3_g, ln3_b=v_ln3_b)

    x_idx, y_idx, c_idx = _coords()
    as_index = lambda v: jnp.reshape(v, (1,)).astype(jnp.int32)
    core, chip, me = as_index(c_idx), as_index(2 * x_idx + y_idx), as_index(4 * x_idx + 2 * y_idx + c_idx)
    small_params = {n: params[n] for n in SMALL_NAMES}

    shards = {n: _shard_2d(n, params[n]) for n in SHARDED_NAMES}
    to_send = lambda names: [shards[n] if "conv" in n else shards[n].astype(BF16) for n in names]
    first = dict(zip(FIRST_NAMES, _all_gather_two_level(to_send(FIRST_NAMES), "weights_gather_first")))
    late_started, first["w_in"] = _direct_start(True, to_send(LATE_NAMES), first["w_in"], "weights_gather_start")

    def late_weights(y):
        mine, lands = _direct_wait(True, late_started, y, "weights_gather_wait")
        gathered = {n: lax.dynamic_update_index_in_dim(land, own, me[0], 0)
                    for n, own, land in zip(LATE_NAMES, mine, lands)}
        return _late_weights(gathered, small_params)

    started = {}

    def start_group(names, tag):
        def hook(grads, through):
            started[tag], through = _direct_start(False, [_owner_stack(n, grads) for n in names], through,
                                                  "grads_start_" + tag)
            return through
        return hook

    loss, grad_x, grads = _local_step(x[0], mem[0], loss_target[0], _first_weights(first, small_params),
                                      late_weights, start_group(FFN_NAMES, "ffn"), start_group(MID_NAMES, "mid"))

    sharded_out = {}

    def adamw(n, index, own, got):
        res = _adamw_sharded(index, own, got, shards[n], _shard_2d(n, mom1[n]), _shard_2d(n, mom2[n]), "adamw_" + n)
        sharded_out[n] = [_shard_like(n, t, params[n]) for t in res]

    stacks = [_owner_stack(n, grads) for n in FIRST_NAMES]
    from_sibling = _exchange_with_sibling(stacks)
    chip_sums = [_add_sibling(core, st, got, "grad_add_" + n) for n, st, got in zip(FIRST_NAMES, stacks, from_sibling)]
    from_chips = _exchange_with_chips(chip_sums)
    for n, sums, got in zip(FIRST_NAMES, chip_sums, from_chips):
        adamw(n, chip, sums, got)
    for names, tag in ((FFN_NAMES, "ffn"), (MID_NAMES, "mid")):
        own, lands = _direct_wait(False, started[tag], grad_x, "grads_wait_" + tag)
        for n, st, land in zip(names, own, lands):
            adamw(n, me, st, land)
    small_parts = _all_gather_direct(_pack_small(_small_grads(grads), loss), "small_all_gather")
    small_res = _adamw_replicated(small_parts, _pack_small(params), _pack_small(mom1), _pack_small(mom2))

    outs = []
    total_loss = None
    for k in range(4):
        small, extra = _unpack_small(small_res[k], params)
        if total_loss is None:
            total_loss = extra
        outs.extend(sharded_out[n][k] if n in sharded_out else small[n] for n in WEIGHT_NAMES)
    return (total_loss, grad_x[None], *outs)
```

```python
import jax
import jax.numpy as jnp
from jax import lax
from jax.experimental import pallas as pl
from jax.experimental.pallas import tpu as pltpu

F32 = jnp.float32
BF16 = jnp.bfloat16
HIGHEST = lax.Precision.HIGHEST
MESH = pl.DeviceIdType.MESH

N_DEV = 8
N_CHIPS = 4
D_MODEL = 1024
N_MEM = 256
N_HEADS = 4
D_HEAD = 128
D_GROUP = N_HEADS * D_HEAD
CHUNK = 64
ML_CONV = 4
FFN_CONV = 3
D_FF = 2816
D_UP = 2 * D_FF
CA_HEADS = 4
CA_DH = D_MODEL // CA_HEADS
LANES = 128
SUBLANES = 8
D_IN = 8 * D_GROUP + 2 * N_HEADS
D_IN_MAIN = 8 * D_GROUP
W_IN_SHARD = D_IN // N_DEV
W_IN_SHARD_P = 640
UP_SHARD = D_UP // N_DEV
UP_SHARD_P = 768
D_UP_P = N_DEV * UP_SHARD_P
D_FF_P = D_UP_P // 2
ALPHA = 2.0 ** 0.25
LN_EPS = 1e-5
NEG_BIG = -1e30
ADAM_LR = 0.001
ADAM_B1 = 0.9
ADAM_B2 = 0.999
ADAM_EPS = 1e-08
ADAM_WD = 0.01
ADAM_STEP = 10
VMEM_LIMIT = 56 * 1024 * 1024

SEG_HQ, SEG_HF, SEG_HI, SEG_HG, SEG_MQ, SEG_MK, SEG_MV, SEG_MO = (4 * i for i in range(8))


def _params(sem):
    return pltpu.CompilerParams(dimension_semantics=sem, vmem_limit_bytes=VMEM_LIMIT)


def _dg(a, b, ca, cb, precision=None):
    return lax.dot_general(a, b, (((ca,), (cb,)), ((), ())), precision=precision,
                           preferred_element_type=F32)


def _nn_raw(a, b):
    return _dg(a.astype(BF16), b.astype(BF16), 1, 0)


def _nt_raw(a, b):
    return _dg(a.astype(BF16), b.astype(BF16), 1, 1)


def _tn_raw(a, b):
    return _dg(a.astype(BF16), b.astype(BF16), 0, 0)


@jax.custom_vjp
def _nn(a, b):
    return _nn_raw(a, b)


_nn.defvjp(lambda a, b: (_nn_raw(a, b), (a, b)),
           lambda res, g: (_nt_raw(g, res[1]), _tn_raw(res[0], g)))


@jax.custom_vjp
def _nt(a, b):
    return _nt_raw(a, b)


_nt.defvjp(lambda a, b: (_nt_raw(a, b), (a, b)),
           lambda res, g: (_nn_raw(g, res[1]), _tn_raw(g, res[0])))


@jax.custom_vjp
def _tn(a, b):
    return _tn_raw(a, b)


_tn.defvjp(lambda a, b: (_tn_raw(a, b), (a, b)),
           lambda res, g: (_nt_raw(res[1], g), _nn_raw(res[0], g)))


def _layer_norm(z, g, b):
    mu = jnp.mean(z, axis=-1, keepdims=True)
    var = jnp.mean(jnp.square(z - mu), axis=-1, keepdims=True)
    return (z - mu) * lax.rsqrt(var + LN_EPS) * g + b


def _matmul_nn(a, w, bias, tm, tn, name):
    m, k = a.shape
    if w.ndim == 3:
        n = w.shape[0] * w.shape[2]
        assert tn == w.shape[2]
        w_spec = pl.BlockSpec((None, k, tn), lambda i, j: (j, 0, 0))
    else:
        n = w.shape[1]
        w_spec = pl.BlockSpec((k, tn), lambda i, j: (0, j))

    def body(*refs):
        a_ref, w_ref = refs[0], refs[1]
        o_ref = refs[-1]
        acc = _nn_raw(a_ref[...], w_ref[...])
        if bias is not None:
            acc = acc + refs[2][...]
        o_ref[...] = acc

    in_specs = [pl.BlockSpec((tm, k), lambda i, j: (i, 0)), w_spec]
    args = [a, w]
    if bias is not None:
        in_specs.append(pl.BlockSpec((1, tn), lambda i, j: (0, j)))
        args.append(bias)
    return pl.pallas_call(
        body, name=name, grid=(m // tm, n // tn), in_specs=in_specs,
        out_specs=pl.BlockSpec((tm, tn), lambda i, j: (i, j)),
        out_shape=jax.ShapeDtypeStruct((m, n), F32),
        compiler_params=_params(("parallel", "parallel")),
    )(*args)


def _matmul_nt(pairs, add, scale, tm, tk, name):
    m = pairs[0][0].shape[0]
    k = pairs[0][1].shape[-2]
    groups = []
    in_specs, args = [], []
    for pair in pairs:
        d, w = pair[0], pair[1]
        in_specs.append(pl.BlockSpec((tm, d.shape[1]), lambda i, j: (i, 0)))
        if w.ndim == 3:
            g = d.shape[1] // w.shape[2]
            blk = pair[2] // g
            in_specs.append(pl.BlockSpec((g, tk, w.shape[2]), lambda i, j, blk=blk: (blk, j, 0)))
            groups.append((g, w.shape[2]))
        else:
            in_specs.append(pl.BlockSpec((tk, w.shape[1]), lambda i, j: (j, 0)))
            groups.append(None)
        args += [d, w]
    if add is not None:
        in_specs.append(pl.BlockSpec((tm, tk), lambda i, j: (i, j)))
        args.append(add)

    def body(*refs):
        o_ref = refs[-1]
        acc = None
        for p, grp in enumerate(groups):
            d_ref, w_ref = refs[2 * p], refs[2 * p + 1]
            if grp is None:
                terms = [_nt_raw(d_ref[...], w_ref[...])]
            else:
                terms = [_nt_raw(d_ref[:, g * grp[1]:(g + 1) * grp[1]], w_ref[g]) for g in range(grp[0])]
            for t in terms:
                acc = t if acc is None else acc + t
        if add is not None:
            acc = acc + scale * refs[2 * len(groups)][...]
        o_ref[...] = acc

    return pl.pallas_call(
        body, name=name, grid=(m // tm, k // tk), in_specs=in_specs,
        out_specs=pl.BlockSpec((tm, tk), lambda i, j: (i, j)),
        out_shape=jax.ShapeDtypeStruct((m, k), F32),
        compiler_params=_params(("parallel", "parallel")),
    )(*args)


def _matmul_tn(a, b, tm, tn, tt, name, shards=None, shard0=0, group=1, into=None, colsum=False):
    t, m = a.shape
    n = b.shape[1]
    assert not colsum or tm == m
    n_in = 2 + (into is not None)
    per_step = 1 if shards is None else group
    width = per_step * tn

    def body(*refs):
        a_ref, b_ref = refs[0], refs[1]
        o_ref = refs[n_in]
        first = pl.program_id(2) == 0

        @pl.when(first)
        def _():
            o_ref[...] = jnp.zeros_like(o_ref)

        if shards is None:
            o_ref[...] += _tn_raw(a_ref[...], b_ref[...])
        else:
            lhs = a_ref[...].astype(BF16)
            for g in range(per_step):
                o_ref[g] += _tn_raw(lhs, b_ref[:, g * tn:(g + 1) * tn])
        if colsum:
            s_ref = refs[n_in + 1]

            @pl.when(first)
            def _():
                s_ref[...] = jnp.zeros_like(s_ref)

            s_ref[...] += jnp.sum(b_ref[...], axis=0, keepdims=True)

    in_specs = [pl.BlockSpec((tt, tm), lambda i, j, kk: (kk, i)),
                pl.BlockSpec((tt, width), lambda i, j, kk: (kk, j))]
    args = [a, b]
    aliases = {}
    if into is not None:
        in_specs.append(pl.BlockSpec(memory_space=pl.ANY))
        args.append(into)
        aliases = {2: 0}
    if shards is None:
        out_specs = [pl.BlockSpec((tm, tn), lambda i, j, kk: (i, j))]
        out_shape = [jax.ShapeDtypeStruct((m, n), F32)]
    else:
        out_specs = [pl.BlockSpec((per_step, tm, tn), lambda i, j, kk: (shard0 // per_step + j, i, 0))]
        out_shape = [jax.ShapeDtypeStruct((shards, m, tn), F32)]
    if colsum:
        out_specs.append(pl.BlockSpec((1, tn), lambda i, j, kk: (0, j)))
        out_shape.append(jax.ShapeDtypeStruct((1, n), F32))
    res = pl.pallas_call(
        body, name=name, grid=(m // tm, n // width, t // tt), in_specs=in_specs, out_specs=out_specs,
        out_shape=out_shape, input_output_aliases=aliases,
        compiler_params=_params(("parallel", "parallel", "arbitrary")),
    )(*args)
    return res if colsum else res[0]


ROW_TILE = 512


def _conv_fwd_tile(pad_ref, w_ref, b_ref, r0, rows, taps):
    acc = b_ref[...]
    for j in range(taps):
        acc = acc + pad_ref[pl.ds(SUBLANES - (taps - 1 - j) + r0, rows), :] * w_ref[j:j + 1, :]
    return acc


def _conv_bwd_tile(dpad_ref, w_ref, r0, rows, taps):
    acc = None
    for j in range(taps):
        term = dpad_ref[pl.ds(r0 + (taps - 1 - j), rows), :] * w_ref[j:j + 1, :]
        acc = term if acc is None else acc + term
    return acc


def _conv_grads_tile(pad_ref, dpad_ref, dx_ref, w_ref, dws, r0, rows, taps):
    dx = _conv_bwd_tile(dpad_ref, w_ref, r0, rows, taps)
    dx_ref[r0:r0 + rows, :] = dx.astype(dx_ref.dtype)
    d_pre = dpad_ref[r0:r0 + rows, :]
    for j in range(taps):
        xs = pad_ref[pl.ds(SUBLANES - (taps - 1 - j) + r0, rows), :]
        dws[j] = dws[j] + jnp.sum(d_pre * xs, axis=0, keepdims=True)
    return jnp.sum(dx, axis=0, keepdims=True)


def _ml_conv_fwd(proj, conv_w, conv_b):
    s = proj.shape[0]
    nblk = 2 * D_GROUP // LANES

    def body(x_ref, w_ref, b_ref, o_ref, pad_ref):
        pad_ref[0:SUBLANES, :] = jnp.zeros((SUBLANES, LANES), F32)
        pad_ref[SUBLANES:, :] = x_ref[...]
        for r0 in range(0, s, ROW_TILE):
            rows = min(ROW_TILE, s - r0)
            o_ref[r0:r0 + rows, :] = jax.nn.silu(_conv_fwd_tile(pad_ref, w_ref, b_ref, r0, rows, ML_CONV))

    return pl.pallas_call(
        body, name="ml_conv_fwd", grid=(nblk,),
        in_specs=[pl.BlockSpec((s, LANES), lambda j: (0, SEG_MQ + j)),
                  pl.BlockSpec((ML_CONV, LANES), lambda j: (0, j)),
                  pl.BlockSpec((1, LANES), lambda j: (0, j))],
        out_specs=pl.BlockSpec((s, LANES), lambda j: (0, j)),
        out_shape=jax.ShapeDtypeStruct((s, 2 * D_GROUP), F32),
        scratch_shapes=[pltpu.VMEM((s + SUBLANES, LANES), F32)],
        compiler_params=_params(("parallel",)),
    )(proj, conv_w, conv_b)


def _ml_conv_bwd(proj, conv_w, conv_b, d_qk, d_proj):
    s = proj.shape[0]
    nblk = 2 * D_GROUP // LANES

    def body(x_ref, w_ref, b_ref, dy_ref, _, dx_ref, dw_ref, db_ref, dxs_ref, pad_ref, dpad_ref):
        pad_ref[0:SUBLANES, :] = jnp.zeros((SUBLANES, LANES), F32)
        pad_ref[SUBLANES:, :] = x_ref[...]
        dpad_ref[s:, :] = jnp.zeros((SUBLANES, LANES), F32)
        db = jnp.zeros((1, LANES), F32)
        for r0 in range(0, s, ROW_TILE):
            rows = min(ROW_TILE, s - r0)
            pre = _conv_fwd_tile(pad_ref, w_ref, b_ref, r0, rows, ML_CONV)
            _, vjp = jax.vjp(jax.nn.silu, pre)
            d_pre, = vjp(dy_ref[r0:r0 + rows, :])
            dpad_ref[r0:r0 + rows, :] = d_pre
            db = db + jnp.sum(d_pre, axis=0, keepdims=True)
        db_ref[...] = db
        dws = [jnp.zeros((1, LANES), F32) for _ in range(ML_CONV)]
        dx_sum = jnp.zeros((1, LANES), F32)
        for r0 in range(0, s, ROW_TILE):
            dx_sum = dx_sum + _conv_grads_tile(pad_ref, dpad_ref, dx_ref, w_ref, dws, r0, min(ROW_TILE, s - r0),
                                               ML_CONV)
        dxs_ref[...] = dx_sum
        for j in range(ML_CONV):
            dw_ref[j:j + 1, :] = dws[j]

    return pl.pallas_call(
        body, name="ml_conv_bwd", grid=(nblk,),
        in_specs=[pl.BlockSpec((s, LANES), lambda j: (0, SEG_MQ + j)),
                  pl.BlockSpec((ML_CONV, LANES), lambda j: (0, j)),
                  pl.BlockSpec((1, LANES), lambda j: (0, j)),
                  pl.BlockSpec((s, LANES), lambda j: (0, j)),
                  pl.BlockSpec(memory_space=pl.ANY)],
        out_specs=[pl.BlockSpec((s, LANES), lambda j: (0, SEG_MQ + j)),
                   pl.BlockSpec((ML_CONV, LANES), lambda j: (0, j)),
                   pl.BlockSpec((1, LANES), lambda j: (0, j)),
                   pl.BlockSpec((1, LANES), lambda j: (0, j))],
        out_shape=[jax.ShapeDtypeStruct(d_proj.shape, d_proj.dtype),
                   jax.ShapeDtypeStruct((ML_CONV, 2 * D_GROUP), F32),
                   jax.ShapeDtypeStruct((1, 2 * D_GROUP), F32),
                   jax.ShapeDtypeStruct((1, 2 * D_GROUP), F32)],
        input_output_aliases={4: 0},
        scratch_shapes=[pltpu.VMEM((s + SUBLANES, LANES), F32), pltpu.VMEM((s + SUBLANES, LANES), F32)],
        compiler_params=_params(("parallel",)),
    )(proj, conv_w, conv_b, d_qk, d_proj)


def _gelu_mul(a, b):
    return jax.nn.gelu(a) * b


FFN_BLOCKS = D_FF_P // LANES


def _ffn_conv_fwd(u, conv_w, conv_b):
    s = u.shape[0]

    def body(g_ref, v_ref, wg_ref, wv_ref, bg_ref, bv_ref, o_ref, gpad_ref, vpad_ref):
        for pad_ref, x_ref in ((gpad_ref, g_ref), (vpad_ref, v_ref)):
            pad_ref[0:SUBLANES, :] = jnp.zeros((SUBLANES, LANES), F32)
            pad_ref[SUBLANES:, :] = x_ref[...]
        for r0 in range(0, s, ROW_TILE):
            rows = min(ROW_TILE, s - r0)
            ug = _conv_fwd_tile(gpad_ref, wg_ref, bg_ref, r0, rows, FFN_CONV)
            uv = _conv_fwd_tile(vpad_ref, wv_ref, bv_ref, r0, rows, FFN_CONV)
            o_ref[r0:r0 + rows, :] = _gelu_mul(ug, uv).astype(o_ref.dtype)

    col = lambda off: (lambda j: (0, off + j))
    return pl.pallas_call(
        body, name="ffn_conv_fwd", grid=(FFN_BLOCKS,),
        in_specs=[pl.BlockSpec((s, LANES), col(0)), pl.BlockSpec((s, LANES), col(FFN_BLOCKS)),
                  pl.BlockSpec((FFN_CONV, LANES), col(0)), pl.BlockSpec((FFN_CONV, LANES), col(FFN_BLOCKS)),
                  pl.BlockSpec((1, LANES), col(0)), pl.BlockSpec((1, LANES), col(FFN_BLOCKS))],
        out_specs=pl.BlockSpec((s, LANES), col(0)),
        out_shape=jax.ShapeDtypeStruct((s, D_FF_P), BF16),
        scratch_shapes=[pltpu.VMEM((s + SUBLANES, LANES), F32), pltpu.VMEM((s + SUBLANES, LANES), F32)],
        compiler_params=_params(("parallel",)),
    )(u, u, conv_w, conv_w, conv_b, conv_b)


def _ffn_conv_bwd(u, conv_w, conv_b, d_h):
    s = u.shape[0]

    def body(g_ref, v_ref, wg_ref, wv_ref, bg_ref, bv_ref, dh_ref,
             dug_ref, duv_ref, dwg_ref, dwv_ref, dbg_ref, dbv_ref,
             gpad_ref, vpad_ref, dgpad_ref, dvpad_ref):
        for pad_ref, x_ref in ((gpad_ref, g_ref), (vpad_ref, v_ref)):
            pad_ref[0:SUBLANES, :] = jnp.zeros((SUBLANES, LANES), F32)
            pad_ref[SUBLANES:, :] = x_ref[...]
        dgpad_ref[s:, :] = jnp.zeros((SUBLANES, LANES), F32)
        dvpad_ref[s:, :] = jnp.zeros((SUBLANES, LANES), F32)
        dbg = jnp.zeros((1, LANES), F32)
        dbv = jnp.zeros((1, LANES), F32)
        for r0 in range(0, s, ROW_TILE):
            rows = min(ROW_TILE, s - r0)
            ug = _conv_fwd_tile(gpad_ref, wg_ref, bg_ref, r0, rows, FFN_CONV)
            uv = _conv_fwd_tile(vpad_ref, wv_ref, bv_ref, r0, rows, FFN_CONV)
            _, vjp = jax.vjp(_gelu_mul, ug, uv)
            d_ug, d_uv = vjp(dh_ref[r0:r0 + rows, :])
            dgpad_ref[r0:r0 + rows, :] = d_ug
            dvpad_ref[r0:r0 + rows, :] = d_uv
            dbg = dbg + jnp.sum(d_ug, axis=0, keepdims=True)
            dbv = dbv + jnp.sum(d_uv, axis=0, keepdims=True)
        dbg_ref[...] = dbg
        dbv_ref[...] = dbv
        for pad_ref, dpad_ref, w_ref, dx_ref, dw_ref in ((gpad_ref, dgpad_ref, wg_ref, dug_ref, dwg_ref),
                                                         (vpad_ref, dvpad_ref, wv_ref, duv_ref, dwv_ref)):
            dws = [jnp.zeros((1, LANES), F32) for _ in range(FFN_CONV)]
            for r0 in range(0, s, ROW_TILE):
                _conv_grads_tile(pad_ref, dpad_ref, dx_ref, w_ref, dws, r0, min(ROW_TILE, s - r0), FFN_CONV)
            for j in range(FFN_CONV):
                dw_ref[j:j + 1, :] = dws[j]

    col = lambda off: (lambda j: (0, off + j))
    seq = pl.BlockSpec((s, LANES), col(0))
    return pl.pallas_call(
        body, name="ffn_conv_bwd", grid=(FFN_BLOCKS,),
        in_specs=[pl.BlockSpec((s, LANES), col(0)), pl.BlockSpec((s, LANES), col(FFN_BLOCKS)),
                  pl.BlockSpec((FFN_CONV, LANES), col(0)), pl.BlockSpec((FFN_CONV, LANES), col(FFN_BLOCKS)),
                  pl.BlockSpec((1, LANES), col(0)), pl.BlockSpec((1, LANES), col(FFN_BLOCKS)), seq],
        out_specs=[seq, seq, pl.BlockSpec((FFN_CONV, LANES), col(0)), pl.BlockSpec((FFN_CONV, LANES), col(0)),
                   pl.BlockSpec((1, LANES), col(0)), pl.BlockSpec((1, LANES), col(0))],
        out_shape=[jax.ShapeDtypeStruct((s, D_FF_P), BF16), jax.ShapeDtypeStruct((s, D_FF_P), BF16),
                   jax.ShapeDtypeStruct((FFN_CONV, D_FF_P), F32), jax.ShapeDtypeStruct((FFN_CONV, D_FF_P), F32),
                   jax.ShapeDtypeStruct((1, D_FF_P), F32), jax.ShapeDtypeStruct((1, D_FF_P), F32)],
        scratch_shapes=[pltpu.VMEM((s + SUBLANES, LANES), F32) for _ in range(4)],
        compiler_params=_params(("parallel",)),
    )(u, u, conv_w, conv_w, conv_b, conv_b, d_h)


def _chunk_masks(c):
    row = lax.broadcasted_iota(jnp.int32, (c, c), 0)
    col = lax.broadcasted_iota(jnp.int32, (c, c), 1)
    return row, col


def _hg_step(hq, hf, hi, hgate, l0, l1, nw, st):
    c = hq.shape[0]
    row, col = _chunk_masks(c)
    mask = col <= row
    mx = lax.stop_gradient(jnp.maximum(l0, l1))
    e0 = jnp.exp(l0 - mx)
    e1 = jnp.exp(l1 - mx)
    lb = e0 / (e0 + e1)
    sig = jax.nn.sigmoid(hf)
    lf = jnp.log(lb + (1.0 - lb) * sig)
    k = (1.0 - lb) * jax.nn.sigmoid(-hf)
    q = jax.nn.silu(hq)
    b = _dg(mask.astype(F32), lf, 1, 0, HIGHEST)
    rid = lax.broadcasted_iota(jnp.int32, b.shape, 0)
    b_ref = jnp.sum(jnp.where(rid == c // 2 - 1, b, 0.0), axis=0, keepdims=True)
    b_last = jnp.sum(jnp.where(rid == c - 1, b, 0.0), axis=0, keepdims=True)
    attn = _nt(q * jnp.exp(b - b_ref), k * jnp.exp(b_ref - b))
    attn = jnp.where(mask, attn, 0.0)
    o = _nn(attn, hi) + _nt(q * jnp.exp(b), st)
    st_new = jnp.exp(b_last) * st + _tn(hi, k * jnp.exp(b_last - b))
    y = o * lax.rsqrt(jnp.mean(o * o, axis=-1, keepdims=True) + LN_EPS) * nw * jax.nn.silu(hgate)
    return y, st_new


def _head(ref, seg, h):
    lo = seg * D_GROUP + h * D_HEAD
    return ref[:, lo:lo + D_HEAD]


def _hgrn2_fwd(proj, logits, norm_w):
    s = proj.shape[0]
    nc = s // CHUNK

    def body(p_ref, lg_ref, nw_ref, y_ref, st_out_ref, st_scr):
        @pl.when(pl.program_id(0) == 0)
        def _():
            st_scr[...] = jnp.zeros_like(st_scr)

        for h in range(N_HEADS):
            lo = h * D_HEAD
            st = st_scr[h]
            st_out_ref[h] = st
            y, st_new = _hg_step(_head(p_ref, 0, h), _head(p_ref, 1, h), _head(p_ref, 2, h), _head(p_ref, 3, h),
                                 lg_ref[0:1, lo:lo + D_HEAD], lg_ref[1:2, lo:lo + D_HEAD],
                                 nw_ref[:, lo:lo + D_HEAD], st)
            y_ref[:, lo:lo + D_HEAD] = y.astype(y_ref.dtype)
            st_scr[h] = st_new

    return pl.pallas_call(
        body, name="hgrn2_fwd", grid=(nc,),
        in_specs=[pl.BlockSpec((CHUNK, 4 * D_GROUP), lambda c: (c, 0)),
                  pl.BlockSpec((2, D_GROUP), lambda c: (0, 0)),
                  pl.BlockSpec((1, D_GROUP), lambda c: (0, 0))],
        out_specs=[pl.BlockSpec((CHUNK, D_GROUP), lambda c: (c, 0)),
                   pl.BlockSpec((None, N_HEADS, D_HEAD, D_HEAD), lambda c: (c, 0, 0, 0))],
        out_shape=[jax.ShapeDtypeStruct((s, 2 * D_GROUP), BF16),
                   jax.ShapeDtypeStruct((nc, N_HEADS, D_HEAD, D_HEAD), F32)],
        scratch_shapes=[pltpu.VMEM((N_HEADS, D_HEAD, D_HEAD), F32)],
        compiler_params=_params(("arbitrary",)),
    )(proj, logits, norm_w)


def _hgrn2_bwd(proj, logits, norm_w, states, d_y):
    s = proj.shape[0]
    nc = s // CHUNK

    def body(p_ref, lg_ref, nw_ref, st_ref, dy_ref, dp_ref, dl_ref, dnw_ref, dsum_ref, dst_scr):
        @pl.when(pl.program_id(0) == 0)
        def _():
            dst_scr[...] = jnp.zeros_like(dst_scr)
            dl_ref[...] = jnp.zeros_like(dl_ref)
            dnw_ref[...] = jnp.zeros_like(dnw_ref)
            dsum_ref[...] = jnp.zeros_like(dsum_ref)

        for h in range(N_HEADS):
            lo = h * D_HEAD
            _, vjp = jax.vjp(_hg_step, _head(p_ref, 0, h), _head(p_ref, 1, h), _head(p_ref, 2, h),
                             _head(p_ref, 3, h), lg_ref[0:1, lo:lo + D_HEAD], lg_ref[1:2, lo:lo + D_HEAD],
                             nw_ref[:, lo:lo + D_HEAD], st_ref[h])
            d_hq, d_hf, d_hi, d_hg, d_l0, d_l1, d_nw, d_st = vjp((dy_ref[:, lo:lo + D_HEAD], dst_scr[h]))
            for seg, val in enumerate((d_hq, d_hf, d_hi, d_hg)):
                c0 = seg * D_GROUP + lo
                dp_ref[:, c0:c0 + D_HEAD] = val.astype(dp_ref.dtype)
                dsum_ref[:, c0:c0 + D_HEAD] += jnp.sum(val, axis=0, keepdims=True)
            dl_ref[0:1, lo:lo + D_HEAD] += d_l0
            dl_ref[1:2, lo:lo + D_HEAD] += d_l1
            dnw_ref[:, lo:lo + D_HEAD] += d_nw
            dst_scr[h] = d_st

    rev = lambda c: nc - 1 - c
    return pl.pallas_call(
        body, name="hgrn2_bwd", grid=(nc,),
        in_specs=[pl.BlockSpec((CHUNK, 4 * D_GROUP), lambda c: (rev(c), 0)),
                  pl.BlockSpec((2, D_GROUP), lambda c: (0, 0)),
                  pl.BlockSpec((1, D_GROUP), lambda c: (0, 0)),
                  pl.BlockSpec((None, N_HEADS, D_HEAD, D_HEAD), lambda c: (rev(c), 0, 0, 0)),
                  pl.BlockSpec((CHUNK, D_GROUP), lambda c: (rev(c), 0))],
        out_specs=[pl.BlockSpec((CHUNK, 4 * D_GROUP), lambda c: (rev(c), 0)),
                   pl.BlockSpec((2, D_GROUP), lambda c: (0, 0)),
                   pl.BlockSpec((1, D_GROUP), lambda c: (0, 0)),
                   pl.BlockSpec((1, 4 * D_GROUP), lambda c: (0, 0))],
        out_shape=[jax.ShapeDtypeStruct((s, D_IN_MAIN), BF16), jax.ShapeDtypeStruct((2, D_GROUP), F32),
                   jax.ShapeDtypeStruct((1, D_GROUP), F32), jax.ShapeDtypeStruct((1, 4 * D_GROUP), F32)],
        scratch_shapes=[pltpu.VMEM((N_HEADS, D_HEAD, D_HEAD), F32)],
        compiler_params=_params(("arbitrary",)),
    )(proj, logits, norm_w, states, d_y)


def _ml_step(qc, kc, v, mo, ig, fr, nw, ct, n, m):
    c = qc.shape[0]
    row, col = _chunk_masks(c)
    mask = col <= row
    eye = col == row
    q = qc * (D_HEAD ** -0.5)
    lf = jax.nn.log_sigmoid(fr)
    to_row = lambda t: jnp.sum(jnp.where(eye, t, 0.0), axis=0, keepdims=True)
    lf_row = to_row(lf)
    ig_row = to_row(ig)
    b_col = jnp.sum(jnp.where(mask, lf_row, 0.0), axis=1, keepdims=True)
    b_row = jnp.sum(jnp.where(row <= col, lf, 0.0), axis=0, keepdims=True)
    g = jnp.sum(lf, axis=0, keepdims=True)
    d = jnp.where(mask, b_col - b_row + ig_row, -jnp.inf)
    inter = b_col + m
    m_t = lax.stop_gradient(jnp.maximum(inter, jnp.max(d, axis=1, keepdims=True)))
    w = jnp.exp(d - m_t)
    sc = _nt(q, kc) * w
    w_inter = jnp.exp(inter - m_t)
    num = _nn(sc, v) + w_inter * _nt(q, ct)
    den = jnp.sum(sc, axis=1, keepdims=True) + w_inter * jnp.sum(q * n, axis=1, keepdims=True)
    h = num / jnp.maximum(jnp.abs(den), jnp.exp(-m_t))
    a = g - b_col + ig
    m_new = lax.stop_gradient(jnp.maximum(g + m, jnp.max(a, axis=0, keepdims=True)))
    decay = jnp.exp(g + m - m_new)
    wk = kc * jnp.exp(a - m_new)
    ct_new = decay * ct + _tn(v, wk)
    n_new = decay * n + jnp.sum(wk, axis=0, keepdims=True)
    mu = jnp.mean(h, axis=-1, keepdims=True)
    var = jnp.mean(jnp.square(h - mu), axis=-1, keepdims=True)
    y = jax.nn.sigmoid(mo) * ((h - mu) * lax.rsqrt(var + LN_EPS) * nw)
    return y, ct_new, n_new, m_new


def _gate_column(gates, lane, idx):
    return jnp.sum(jnp.where(lane == idx, gates, 0.0), axis=1, keepdims=True)


def _mlstm_fwd(qk, proj, gates, norm_w, y):
    s = proj.shape[0]
    nc = s // CHUNK

    def body(qk_ref, vo_ref, g_ref, nw_ref, _, y_ref, ct_out, n_out, m_out, ct_scr, n_scr, m_scr):
        @pl.when(pl.program_id(0) == 0)
        def _():
            ct_scr[...] = jnp.zeros_like(ct_scr)
            n_scr[...] = jnp.zeros_like(n_scr)
            m_scr[...] = jnp.full(m_scr.shape, NEG_BIG, F32)

        gates_blk = g_ref[...]
        lane = lax.broadcasted_iota(jnp.int32, gates_blk.shape, 1)
        for h in range(N_HEADS):
            lo = h * D_HEAD
            ct, n, m = ct_scr[h], n_scr[h], m_scr[h]
            ct_out[h] = ct
            n_out[h] = n
            m_out[h] = m
            yh, ct_new, n_new, m_new = _ml_step(
                _head(qk_ref, 0, h), _head(qk_ref, 1, h), _head(vo_ref, 0, h), _head(vo_ref, 1, h),
                _gate_column(gates_blk, lane, h), _gate_column(gates_blk, lane, N_HEADS + h),
                nw_ref[:, lo:lo + D_HEAD], ct, n, m)
            y_ref[:, lo:lo + D_HEAD] = yh.astype(y_ref.dtype)
            ct_scr[h] = ct_new
            n_scr[h] = n_new
            m_scr[h] = m_new

    st = lambda r, w: pl.BlockSpec((None, N_HEADS, r, w), lambda c: (c, 0, 0, 0))
    return pl.pallas_call(
        body, name="mlstm_fwd", grid=(nc,),
        in_specs=[pl.BlockSpec((CHUNK, 2 * D_GROUP), lambda c: (c, 0)),
                  pl.BlockSpec((CHUNK, 2 * D_GROUP), lambda c: (c, 3)),
                  pl.BlockSpec((CHUNK, LANES), lambda c: (c, 0)),
                  pl.BlockSpec((1, D_GROUP), lambda c: (0, 0)),
                  pl.BlockSpec(memory_space=pl.ANY)],
        out_specs=[pl.BlockSpec((CHUNK, D_GROUP), lambda c: (c, 1)),
                   st(D_HEAD, D_HEAD), st(1, D_HEAD), st(1, 1)],
        out_shape=[jax.ShapeDtypeStruct(y.shape, y.dtype),
                   jax.ShapeDtypeStruct((nc, N_HEADS, D_HEAD, D_HEAD), F32),
                   jax.ShapeDtypeStruct((nc, N_HEADS, 1, D_HEAD), F32),
                   jax.ShapeDtypeStruct((nc, N_HEADS, 1, 1), F32)],
        input_output_aliases={4: 0},
        scratch_shapes=[pltpu.VMEM((N_HEADS, D_HEAD, D_HEAD), F32), pltpu.VMEM((N_HEADS, 1, D_HEAD), F32),
                        pltpu.VMEM((N_HEADS, 1, 1), F32)],
        compiler_params=_params(("arbitrary",)),
    )(qk, proj, gates, norm_w, y)


def _mlstm_bwd(qk, proj, gates, norm_w, ct_s, n_s, m_s, d_y, d_proj):
    s = proj.shape[0]
    nc = s // CHUNK

    def body(qk_ref, vo_ref, g_ref, nw_ref, ct_ref, n_ref, m_ref, dy_ref, _,
             dp_ref, dqk_ref, dg_ref, dnw_ref, dsum_ref, dct_scr, dn_scr):
        @pl.when(pl.program_id(0) == 0)
        def _():
            dct_scr[...] = jnp.zeros_like(dct_scr)
            dn_scr[...] = jnp.zeros_like(dn_scr)
            dnw_ref[...] = jnp.zeros_like(dnw_ref)
            dsum_ref[...] = jnp.zeros_like(dsum_ref)

        gates_blk = g_ref[...]
        lane = lax.broadcasted_iota(jnp.int32, gates_blk.shape, 1)
        d_gates = jnp.zeros(gates_blk.shape, F32)
        for h in range(N_HEADS):
            lo = h * D_HEAD
            m = m_ref[h]
            step = lambda *a, m=m: _ml_step(*a, m)[:3]
            _, vjp = jax.vjp(step, _head(qk_ref, 0, h), _head(qk_ref, 1, h), _head(vo_ref, 0, h),
                             _head(vo_ref, 1, h), _gate_column(gates_blk, lane, h),
                             _gate_column(gates_blk, lane, N_HEADS + h), nw_ref[:, lo:lo + D_HEAD],
                             ct_ref[h], n_ref[h])
            d_q, d_k, d_v, d_o, d_gi, d_gf, d_nw, d_ct, d_n = vjp((dy_ref[:, lo:lo + D_HEAD], dct_scr[h], dn_scr[h]))
            dqk_ref[:, lo:lo + D_HEAD] = d_q
            dqk_ref[:, D_GROUP + lo:D_GROUP + lo + D_HEAD] = d_k
            for c0, val in ((lo, d_v), (D_GROUP + lo, d_o)):
                dp_ref[:, c0:c0 + D_HEAD] = val.astype(dp_ref.dtype)
                dsum_ref[:, c0:c0 + D_HEAD] += jnp.sum(val, axis=0, keepdims=True)
            d_gates = d_gates + jnp.where(lane == h, d_gi, 0.0) + jnp.where(lane == N_HEADS + h, d_gf, 0.0)
            dnw_ref[:, lo:lo + D_HEAD] += d_nw
            dct_scr[h] = d_ct
            dn_scr[h] = d_n
        dg_ref[...] = d_gates

    rev = lambda c: nc - 1 - c
    st = lambda r, w: pl.BlockSpec((None, N_HEADS, r, w), lambda c: (rev(c), 0, 0, 0))
    return pl.pallas_call(
        body, name="mlstm_bwd", grid=(nc,),
        in_specs=[pl.BlockSpec((CHUNK, 2 * D_GROUP), lambda c: (rev(c), 0)),
                  pl.BlockSpec((CHUNK, 2 * D_GROUP), lambda c: (rev(c), 3)),
                  pl.BlockSpec((CHUNK, LANES), lambda c: (rev(c), 0)),
                  pl.BlockSpec((1, D_GROUP), lambda c: (0, 0)),
                  st(D_HEAD, D_HEAD), st(1, D_HEAD), st(1, 1),
                  pl.BlockSpec((CHUNK, D_GROUP), lambda c: (rev(c), 1)),
                  pl.BlockSpec(memory_space=pl.ANY)],
        out_specs=[pl.BlockSpec((CHUNK, 2 * D_GROUP), lambda c: (rev(c), 3)),
                   pl.BlockSpec((CHUNK, 2 * D_GROUP), lambda c: (rev(c), 0)),
                   pl.BlockSpec((CHUNK, LANES), lambda c: (rev(c), 0)),
                   pl.BlockSpec((1, D_GROUP), lambda c: (0, 0)),
                   pl.BlockSpec((1, 2 * D_GROUP), lambda c: (0, 0))],
        out_shape=[jax.ShapeDtypeStruct(d_proj.shape, d_proj.dtype), jax.ShapeDtypeStruct((s, 2 * D_GROUP), F32),
                   jax.ShapeDtypeStruct((s, LANES), F32), jax.ShapeDtypeStruct((1, D_GROUP), F32),
                   jax.ShapeDtypeStruct((1, 2 * D_GROUP), F32)],
        input_output_aliases={8: 0},
        scratch_shapes=[pltpu.VMEM((N_HEADS, D_HEAD, D_HEAD), F32), pltpu.VMEM((N_HEADS, 1, D_HEAD), F32)],
        compiler_params=_params(("arbitrary",)),
    )(qk, proj, gates, norm_w, ct_s, n_s, m_s, d_y, d_proj)


LN_TOKENS = 512
ATT_TOKENS = 256


def _res_ln_fwd(xres, branch, g, b, name):
    s, dm = xres.shape
    tb = min(LN_TOKENS, s)

    def body(x_ref, br_ref, g_ref, b_ref, o_ref):
        o_ref[...] = _layer_norm(ALPHA * x_ref[...] + br_ref[...], g_ref[...], b_ref[...])

    tok = pl.BlockSpec((tb, dm), lambda i: (i, 0))
    vec = pl.BlockSpec((1, dm), lambda i: (0, 0))
    return pl.pallas_call(
        body, name=name, grid=(s // tb,), in_specs=[tok, tok, vec, vec], out_specs=tok,
        out_shape=jax.ShapeDtypeStruct((s, dm), F32), compiler_params=_params(("parallel",)),
    )(xres, branch, g, b)


def _res_ln_bwd(xres, branch, g, b, d_out, name):
    s, dm = xres.shape
    tb = min(LN_TOKENS, s)

    def body(x_ref, br_ref, g_ref, b_ref, do_ref, dz_ref, dg_ref, db_ref):
        @pl.when(pl.program_id(0) == 0)
        def _():
            dg_ref[...] = jnp.zeros_like(dg_ref)
            db_ref[...] = jnp.zeros_like(db_ref)

        z = ALPHA * x_ref[...] + br_ref[...]
        _, vjp = jax.vjp(_layer_norm, z, g_ref[...], b_ref[...])
        d_z, d_g, d_b = vjp(do_ref[...])
        dz_ref[...] = d_z
        dg_ref[...] += d_g
        db_ref[...] += d_b

    tok = pl.BlockSpec((tb, dm), lambda i: (i, 0))
    vec = pl.BlockSpec((1, dm), lambda i: (0, 0))
    return pl.pallas_call(
        body, name=name, grid=(s // tb,), in_specs=[tok, tok, vec, vec, tok], out_specs=[tok, vec, vec],
        out_shape=[jax.ShapeDtypeStruct((s, dm), F32), jax.ShapeDtypeStruct((1, dm), F32),
                   jax.ShapeDtypeStruct((1, dm), F32)],
        compiler_params=_params(("arbitrary",)),
    )(xres, branch, g, b, d_out)


def _loss_tail(xres, branch, g, b, target):
    s, dm = xres.shape
    tb = min(LN_TOKENS, s)

    def loss_fn(z, gg, bb, tgt):
        err = jnp.square(_layer_norm(z, gg, bb) - tgt)
        return 0.5 * jnp.sum(jnp.mean(err, axis=-1, keepdims=True), axis=0, keepdims=True)

    def body(x_ref, br_ref, g_ref, b_ref, t_ref, loss_ref, dz_ref, dg_ref, db_ref):
        @pl.when(pl.program_id(0) == 0)
        def _():
            loss_ref[...] = jnp.zeros_like(loss_ref)
            dg_ref[...] = jnp.zeros_like(dg_ref)
            db_ref[...] = jnp.zeros_like(db_ref)

        z = ALPHA * x_ref[...] + br_ref[...]
        tgt = t_ref[...]
        loss, vjp = jax.vjp(lambda zz, gg, bb: loss_fn(zz, gg, bb, tgt), z, g_ref[...], b_ref[...])
        d_z, d_g, d_b = vjp(jnp.ones((1, 1), F32))
        loss_ref[...] += loss
        dz_ref[...] = d_z
        dg_ref[...] += d_g
        db_ref[...] += d_b

    tok = pl.BlockSpec((tb, dm), lambda i: (i, 0))
    vec = pl.BlockSpec((1, dm), lambda i: (0, 0))
    one = pl.BlockSpec((1, 1), lambda i: (0, 0))
    return pl.pallas_call(
        body, name="loss_tail", grid=(s // tb,), in_specs=[tok, tok, vec, vec, tok],
        out_specs=[one, tok, vec, vec],
        out_shape=[jax.ShapeDtypeStruct((1, 1), F32), jax.ShapeDtypeStruct((s, dm), F32),
                   jax.ShapeDtypeStruct((1, dm), F32), jax.ShapeDtypeStruct((1, dm), F32)],
        compiler_params=_params(("arbitrary",)),
    )(xres, branch, g, b, target)


def _att_head(q, k, v):
    sc = _nt(q, k) * (CA_DH ** -0.5)
    return _nn(jax.nn.softmax(sc, axis=-1), v)


def _att_fwd(q, kv):
    s = q.shape[0]
    tb = min(ATT_TOKENS, s)

    def body(q_ref, kv_ref, o_ref):
        for h in range(CA_HEADS):
            lo = h * CA_DH
            o_ref[:, lo:lo + CA_DH] = _att_head(q_ref[:, lo:lo + CA_DH], kv_ref[:, lo:lo + CA_DH],
                                                kv_ref[:, D_MODEL + lo:D_MODEL + lo + CA_DH]).astype(o_ref.dtype)

    tok = pl.BlockSpec((tb, D_MODEL), lambda i: (i, 0))
    return pl.pallas_call(
        body, name="att_fwd", grid=(s // tb,),
        in_specs=[tok, pl.BlockSpec((N_MEM, 2 * D_MODEL), lambda i: (0, 0))], out_specs=tok,
        out_shape=jax.ShapeDtypeStruct((s, D_MODEL), BF16), compiler_params=_params(("parallel",)),
    )(q, kv)


def _att_bwd(q, kv, d_o):
    s = q.shape[0]
    tb = min(ATT_TOKENS, s)

    def body(q_ref, kv_ref, do_ref, dq_ref, dkv_ref):
        @pl.when(pl.program_id(0) == 0)
        def _():
            dkv_ref[...] = jnp.zeros_like(dkv_ref)

        for h in range(CA_HEADS):
            lo = h * CA_DH
            vlo = D_MODEL + lo
            _, vjp = jax.vjp(_att_head, q_ref[:, lo:lo + CA_DH], kv_ref[:, lo:lo + CA_DH],
                             kv_ref[:, vlo:vlo + CA_DH])
            d_q, d_k, d_v = vjp(do_ref[:, lo:lo + CA_DH])
            dq_ref[:, lo:lo + CA_DH] = d_q
            dkv_ref[:, lo:lo + CA_DH] += d_k
            dkv_ref[:, vlo:vlo + CA_DH] += d_v

    tok = pl.BlockSpec((tb, D_MODEL), lambda i: (i, 0))
    mem = pl.BlockSpec((N_MEM, 2 * D_MODEL), lambda i: (0, 0))
    return pl.pallas_call(
        body, name="att_bwd", grid=(s // tb,), in_specs=[tok, mem, tok], out_specs=[tok, mem],
        out_shape=[jax.ShapeDtypeStruct((s, D_MODEL), F32), jax.ShapeDtypeStruct((N_MEM, 2 * D_MODEL), F32)],
        compiler_params=_params(("arbitrary",)),
    )(q, kv, d_o)


def _local_step(x, mem, target, w, late_weights=None, on_ffn_grads=None, on_mid_grads=None):
    w = dict(w)
    s = x.shape[0]
    tm = min(512, s)
    tt = min(512, s)
    proj = _matmul_nn(x, w["w_in_main"], w["b_in_main"], min(2048, s), 512, "proj")
    gates = _matmul_nn(x, w["w_in_gate"], w["b_in_gate"], tm, LANES, "proj_gates")
    qk = _ml_conv_fwd(proj, w["ml_conv_w"], w["ml_conv_b"])
    y, hg_states = _hgrn2_fwd(proj, w["hg_lb_logits"], w["hg_norm_w"])
    y, ct_s, n_s, m_s = _mlstm_fwd(qk, proj, gates, w["ml_norm_w"], y)
    if late_weights is not None:
        w.update(late_weights(y))
    mix =_matmul_nn(y, w["w_out"], None, tm, D_MODEL, "mix")
    x1 = _res_ln_fwd(x, mix, w["ln1_g"], w["ln1_b"], "ln1_fwd")
    kv = _matmul_nn(mem, w["ca_wkv"], None, N_MEM, CA_DH, "kv")
    q = _matmul_nn(x1, w["ca_wq"], None, tm, D_MODEL, "ca_q")
    att = _att_fwd(q, kv)
    ca = _matmul_nn(att, w["ca_wo"], None, tm, D_MODEL, "ca_out")
    x2 = _res_ln_fwd(x1, ca, w["ln2_g"], w["ln2_b"], "ln2_fwd")
    u = _matmul_nn(x2, w["ffn_w_up"], None, min(2048, s), UP_SHARD_P, "ffn_up")
    hid = _ffn_conv_fwd(u, w["ffn_conv_w"], w["ffn_conv_b"])
    ff = _matmul_nn(hid, w["ffn_w_down"], None, tm, D_MODEL, "ffn_down")
    loss, d_z3, d_ln3_g, d_ln3_b = _loss_tail(x2, ff, w["ln3_g"], w["ln3_b"], target)
    grads = {"ln3_g": d_ln3_g, "ln3_b": d_ln3_b}
    grads["ffn_w_down"] = _matmul_tn(hid, d_z3, 1536, D_MODEL, tt, "d_w_down")
    d_hid = _matmul_nt([(d_z3, w["ffn_w_down"])], None, 1.0, tm, D_FF_P, "d_hid")
    d_ug, d_uv, d_cwg, d_cwv, d_cbg, d_cbv = _ffn_conv_bwd(u, w["ffn_conv_w"], w["ffn_conv_b"], d_hid)
    grads["ffn_conv_w"] = jnp.concatenate([d_cwg, d_cwv], axis=-1)
    grads["ffn_conv_b"] = jnp.concatenate([d_cbg, d_cbv], axis=-1)
    half = N_DEV // 2
    d_w_up = _matmul_tn(x2, d_ug, D_MODEL, UP_SHARD_P, tt, "d_w_up_gate", shards=N_DEV, group=half)
    grads["ffn_w_up"] = _matmul_tn(x2, d_uv, D_MODEL, UP_SHARD_P, tt, "d_w_up_val", shards=N_DEV,
                                   shard0=half, group=half, into=d_w_up)
    d_x2 = _matmul_nt([(d_ug, w["ffn_w_up"], 0), (d_uv, w["ffn_w_up"], N_DEV // 2)], d_z3, ALPHA,
                      min(256, s), D_MODEL, "d_x2")
    if on_ffn_grads is not None:
        d_x2 = on_ffn_grads(grads, d_x2)
    d_z2, grads["ln2_g"], grads["ln2_b"] = _res_ln_bwd(x1, ca, w["ln2_g"], w["ln2_b"], d_x2, "ln2_bwd")
    grads["ca_wo"] = _matmul_tn(att, d_z2, D_MODEL, D_MODEL, tt, "d_ca_wo")
    d_att = _matmul_nt([(d_z2, w["ca_wo"])], None, 1.0, tm, D_MODEL, "d_att")
    d_q, d_kv = _att_bwd(q, kv, d_att)
    grads["ca_wq"] = _matmul_tn(x1, d_q, D_MODEL, D_MODEL, tt, "d_ca_wq")
    grads["ca_wkv"] = _matmul_tn(mem, d_kv, D_MODEL, CA_DH, N_MEM, "d_ca_wkv", shards=N_DEV, group=N_DEV)
    d_x1 = _matmul_nt([(d_q, w["ca_wq"])], d_z2, ALPHA, tm, D_MODEL, "d_x1")
    d_z1, grads["ln1_g"], grads["ln1_b"] = _res_ln_bwd(x, mix, w["ln1_g"], w["ln1_b"], d_x1, "ln1_bwd")
    grads["w_out"] = _matmul_tn(y, d_z1, D_MODEL, D_MODEL, tt, "d_w_out")
    if on_mid_grads is not None:
        d_z1 = on_mid_grads(grads, d_z1)
    d_y = _matmul_nt([(d_z1, w["w_out"])], None, 1.0, tm, D_MODEL, "d_y")
    d_proj, grads["hg_lb_logits"], grads["hg_norm_w"], db_hg = _hgrn2_bwd(
        proj, w["hg_lb_logits"], w["hg_norm_w"], hg_states, d_y)
    d_proj, d_qk, d_gates, grads["ml_norm_w"], db_vo = _mlstm_bwd(
        qk, proj, gates, w["ml_norm_w"], ct_s, n_s, m_s, d_y, d_proj)
    d_proj, grads["ml_conv_w"], grads["ml_conv_b"], db_qk = _ml_conv_bwd(
        proj, w["ml_conv_w"], w["ml_conv_b"], d_qk, d_proj)
    grads["b_in_main"] = jnp.concatenate([db_hg, db_qk, db_vo], axis=-1)
    grads["w_in_main"] = _matmul_tn(x, d_proj, D_MODEL, min(2048, D_IN_MAIN), tt, "d_w_in")
    grads["w_in_gate"], grads["b_in_gate"] = _matmul_tn(x, d_gates, D_MODEL, LANES, tt, "d_w_in_gates", colsum=True)
    grad_x = _matmul_nt([(d_proj, w["w_in_main"]), (d_gates, w["w_in_gate"])], d_z1, ALPHA, tm, D_MODEL, "d_x")
    return loss, grad_x, grads


HBM_SPEC = pl.BlockSpec(memory_space=pltpu.HBM)


def _coords():
    return lax.axis_index("x"), lax.axis_index("y"), lax.axis_index("c")


def _other_chips(x, y):
    return [(1 - x, y), (x, 1 - y), (1 - x, 1 - y)]


def _all_gather_two_level(shards, name):
    na = len(shards)

    def body(*refs):
        x_refs, out_refs = refs[:na], refs[na:2 * na]
        send_sems, recv_sems, local_sems = refs[2 * na:]
        x, y, c = _coords()
        me, sibling = (x, y, c), (x, y, 1 - c)
        chips = _other_chips(x, y)

        def copy(a, k, block, to, own=False):
            slot = out_refs[a].at[4 * block[0] + 2 * block[1] + block[2]]
            return pltpu.make_async_remote_copy(
                src_ref=x_refs[a] if own else slot, dst_ref=slot,
                send_sem=send_sems.at[7 * a + k], recv_sem=recv_sems.at[7 * a + k],
                device_id=to, device_id_type=MESH)

        mine = [pltpu.make_async_copy(x_refs[a], out_refs[a].at[4 * x + 2 * y + c], local_sems.at[a])
                for a in range(na)]
        for cp in mine:
            cp.start()
        first = []
        for a in range(na):
            first.append(copy(a, 0, me, sibling, own=True))
            first += [copy(a, 1 + j, me, (*chip, c), own=True) for j, chip in enumerate(chips)]
        for cp in first:
            cp.start()
        passed = []
        for j, chip in enumerate(chips):
            for a in range(na):
                copy(a, 1 + j, (*chip, c), me).wait_recv()
                fwd = copy(a, 4 + j, (*chip, c), sibling)
                fwd.start()
                passed.append(fwd)
        for a in range(na):
            copy(a, 0, sibling, me).wait_recv()
            for j, chip in enumerate(chips):
                copy(a, 4 + j, (*chip, 1 - c), me).wait_recv()
        for cp in first + passed:
            cp.wait_send()
        for cp in mine:
            cp.wait()

    return pl.pallas_call(
        body, name=name,
        out_shape=[jax.ShapeDtypeStruct((N_DEV,) + t.shape, t.dtype) for t in shards],
        in_specs=[HBM_SPEC] * na, out_specs=[HBM_SPEC] * na,
        scratch_shapes=[pltpu.SemaphoreType.DMA((7 * na,)), pltpu.SemaphoreType.DMA((7 * na,)),
                        pltpu.SemaphoreType.DMA((na,))],
    )(*shards)


def _all_gather_direct(vec, name):
    r, n = vec.shape

    def body(x_ref, out_ref, send_sems, recv_sems, local_sem):
        x, y, c = _coords()
        flip = lambda v, bit: 1 - v if bit else v
        me = 4 * x + 2 * y + c
        mine = pltpu.make_async_copy(x_ref, out_ref.at[me], local_sem)
        mine.start()
        copies = []
        for d in range(1, N_DEV):
            peer = (flip(x, d & 4), flip(y, d & 2), flip(c, d & 1))
            peer_slot = 4 * peer[0] + 2 * peer[1] + peer[2]
            out_going = pltpu.make_async_remote_copy(
                src_ref=x_ref, dst_ref=out_ref.at[me], send_sem=send_sems.at[d - 1],
                recv_sem=recv_sems.at[d - 1], device_id=peer, device_id_type=MESH)
            incoming = pltpu.make_async_remote_copy(
                src_ref=x_ref, dst_ref=out_ref.at[peer_slot], send_sem=send_sems.at[d - 1],
                recv_sem=recv_sems.at[d - 1], device_id=peer, device_id_type=MESH)
            out_going.start()
            copies.append((out_going, incoming))
        for out_going, incoming in copies:
            incoming.wait_recv()
            out_going.wait_send()
        mine.wait()

    return pl.pallas_call(
        body, name=name, out_shape=jax.ShapeDtypeStruct((N_DEV, r, n), vec.dtype),
        in_specs=[HBM_SPEC], out_specs=HBM_SPEC,
        scratch_shapes=[pltpu.SemaphoreType.DMA((7,)), pltpu.SemaphoreType.DMA((7,)), pltpu.SemaphoreType.DMA],
    )(vec)


def _exchange_with_sibling(parts):
    na = len(parts)

    def body(*refs):
        p_refs, got_refs = refs[:na], refs[na:2 * na]
        send_sems, recv_sems = refs[2 * na:]
        x, y, c = _coords()
        copies = []
        for a in range(na):
            for chip in range(N_CHIPS):
                cp = pltpu.make_async_remote_copy(
                    src_ref=p_refs[a].at[2 * chip + (1 - c)], dst_ref=got_refs[a].at[chip],
                    send_sem=send_sems.at[N_CHIPS * a + chip], recv_sem=recv_sems.at[N_CHIPS * a + chip],
                    device_id=(x, y, 1 - c), device_id_type=MESH)
                cp.start()
                copies.append(cp)
        for cp in copies:
            cp.wait()

    return pl.pallas_call(
        body, name="grad_exchange_sibling",
        out_shape=[jax.ShapeDtypeStruct((N_CHIPS,) + t.shape[1:], t.dtype) for t in parts],
        in_specs=[HBM_SPEC] * na, out_specs=[HBM_SPEC] * na,
        scratch_shapes=[pltpu.SemaphoreType.DMA((N_CHIPS * na,)), pltpu.SemaphoreType.DMA((N_CHIPS * na,))],
    )(*parts)


def _exchange_with_chips(sums):
    na = len(sums)

    def body(*refs):
        a_refs, out_refs = refs[:na], refs[na:2 * na]
        send_sems, recv_sems = refs[2 * na:]
        x, y, c = _coords()
        copies = []
        for a in range(na):
            for j, (cx, cy) in enumerate(_other_chips(x, y)):
                cp = pltpu.make_async_remote_copy(
                    src_ref=a_refs[a].at[2 * cx + cy], dst_ref=out_refs[a].at[j],
                    send_sem=send_sems.at[3 * a + j], recv_sem=recv_sems.at[3 * a + j],
                    device_id=(cx, cy, c), device_id_type=MESH)
                cp.start()
                copies.append(cp)
        for cp in copies:
            cp.wait()

    return pl.pallas_call(
        body, name="grad_exchange_chips",
        out_shape=[jax.ShapeDtypeStruct((3,) + t.shape[1:], t.dtype) for t in sums],
        in_specs=[HBM_SPEC] * na, out_specs=[HBM_SPEC] * na,
        scratch_shapes=[pltpu.SemaphoreType.DMA((3 * na,)), pltpu.SemaphoreType.DMA((3 * na,))],
    )(*sums)


SEM_SPEC = pl.BlockSpec(memory_space=pltpu.SEMAPHORE)
ANY_SPEC = pl.BlockSpec(memory_space=pl.ANY)
SIDE_EFFECT = pltpu.SideEffectType.DATAFLOW_SIDE_EFFECTING


def _peer(x, y, c, d):
    flip = lambda v, bit: 1 - v if bit else v
    p = (flip(x, d & 4), flip(y, d & 2), flip(c, d & 1))
    return p, 4 * p[0] + 2 * p[1] + p[2]


def _direct_copies(gather, src_refs, land_refs, send_sems, recv_sems):
    x, y, c = _coords()
    me = 4 * x + 2 * y + c
    copies = []
    for a in range(len(src_refs)):
        for d in range(1, N_DEV):
            peer, peer_slot = _peer(x, y, c, d)
            copies.append(pltpu.make_async_remote_copy(
                src_ref=src_refs[a] if gather else src_refs[a].at[peer_slot],
                dst_ref=land_refs[a].at[me] if gather else land_refs[a].at[d - 1],
                send_sem=send_sems.at[7 * a + d - 1], recv_sem=recv_sems.at[7 * a + d - 1],
                device_id=peer, device_id_type=MESH))
    return copies


def _hbm(t):
    return pltpu.HBM(t.shape, t.dtype)


def _direct_start(gather, arrays, through, name):
    na = len(arrays)
    lands = [lax.empty((N_DEV,) + t.shape if gather else (N_DEV - 1,) + t.shape[1:], t.dtype) for t in arrays]
    n_io = 2 * na + 1

    def body(*refs):
        for cp in _direct_copies(gather, refs[:na], refs[na:2 * na], refs[n_io], refs[n_io + 1]):
            cp.start()

    ins = [pltpu.with_memory_space_constraint(t, pltpu.HBM) for t in (*arrays, *lands, through)]
    sems = pltpu.SemaphoreType.DMA((7 * na,))
    res = pl.pallas_call(
        body, name=name, out_shape=(sems, sems, *[_hbm(t) for t in ins]),
        in_specs=[HBM_SPEC] * n_io, out_specs=(SEM_SPEC, SEM_SPEC, *[HBM_SPEC] * n_io),
        input_output_aliases={i: 2 + i for i in range(n_io)},
        compiler_params=pltpu.CompilerParams(has_side_effects=SIDE_EFFECT),
    )(*ins)
    return (res[0], res[1], list(res[2:2 + na]), list(res[2 + na:2 + 2 * na])), res[2 + 2 * na]


def _direct_wait(gather, started, after, name):
    send_sems, recv_sems, arrays, lands = started
    na = len(arrays)

    def body(*refs):
        for cp in _direct_copies(gather, refs[:na], refs[na:2 * na], refs[2 * na], refs[2 * na + 1]):
            cp.wait_send()
            cp.wait_recv()

    res = pl.pallas_call(
        body, name=name, out_shape=tuple(_hbm(t) for t in (*arrays, *lands)),
        in_specs=[HBM_SPEC] * (2 * na) + [SEM_SPEC, SEM_SPEC, ANY_SPEC], out_specs=tuple([HBM_SPEC] * (2 * na)),
        input_output_aliases={i: i for i in range(2 * na)},
        compiler_params=pltpu.CompilerParams(has_side_effects=SIDE_EFFECT),
    )(*arrays, *lands, send_sems, recv_sems, after)
    return list(res[:na]), list(res[na:])


def _row_tile(rows):
    for t in (256, 176, 128):
        if rows % t == 0 and rows > t:
            return t
    return rows


def _add_sibling(core, parts, got, name):
    _, r, c = parts.shape
    tr = _row_tile(r)

    def body(core_ref, p_ref, g_ref, o_ref):
        o_ref[...] = p_ref[...] + g_ref[...]

    return pl.pallas_call(
        body, name=name,
        grid_spec=pltpu.PrefetchScalarGridSpec(
            num_scalar_prefetch=1, grid=(N_CHIPS, r // tr),
            in_specs=[pl.BlockSpec((None, tr, c), lambda i, j, core_ref: (2 * i + core_ref[0], j, 0)),
                      pl.BlockSpec((None, tr, c), lambda i, j, core_ref: (i, j, 0))],
            out_specs=pl.BlockSpec((None, tr, c), lambda i, j, core_ref: (i, j, 0))),
        out_shape=jax.ShapeDtypeStruct((N_CHIPS, r, c), F32),
        compiler_params=_params(("parallel", "parallel")),
    )(core, parts, got)


def _adamw_math(g, w, m, v):
    m_new = ADAM_B1 * m + (1.0 - ADAM_B1) * g
    v_new = ADAM_B2 * v + (1.0 - ADAM_B2) * jnp.square(g)
    m_hat = m_new / (1.0 - ADAM_B1 ** ADAM_STEP)
    v_hat = v_new / (1.0 - ADAM_B2 ** ADAM_STEP)
    delta = -ADAM_LR * (m_hat / (jnp.sqrt(v_hat) + ADAM_EPS) + ADAM_WD * w)
    return delta, m_new, v_new


def _adamw_sharded(chip, sums, got, w, m, v, name):
    r, c = w.shape
    tr = _row_tile(r)
    n_got = got.shape[0]

    def body(chip_ref, s_ref, g_ref, w_ref, m_ref, v_ref, go_ref, d_ref, nm_ref, nv_ref):
        g = s_ref[...]
        for i in range(n_got):
            g = g + g_ref[i]
        delta, m_new, v_new = _adamw_math(g, w_ref[...], m_ref[...], v_ref[...])
        go_ref[...] = g
        d_ref[...] = delta
        nm_ref[...] = m_new
        nv_ref[...] = v_new

    blk = pl.BlockSpec((tr, c), lambda i, chip_ref: (i, 0))
    out = jax.ShapeDtypeStruct((r, c), F32)
    return pl.pallas_call(
        body, name=name,
        grid_spec=pltpu.PrefetchScalarGridSpec(
            num_scalar_prefetch=1, grid=(r // tr,),
            in_specs=[pl.BlockSpec((None, tr, c), lambda i, chip_ref: (chip_ref[0], i, 0)),
                      pl.BlockSpec((n_got, tr, c), lambda i, chip_ref: (0, i, 0)), blk, blk, blk],
            out_specs=[blk, blk, blk, blk]),
        out_shape=[out, out, out, out],
        compiler_params=_params(("parallel",)),
    )(chip, sums, got, w, m, v)


def _adamw_replicated(parts, w, m, v):
    p, r, c = parts.shape

    def body(p_ref, w_ref, m_ref, v_ref, g_ref, d_ref, nm_ref, nv_ref):
        g = p_ref[0]
        for i in range(1, p):
            g = g + p_ref[i]
        delta, m_new, v_new = _adamw_math(g, w_ref[...], m_ref[...], v_ref[...])
        g_ref[...] = g
        d_ref[...] = delta
        nm_ref[...] = m_new
        nv_ref[...] = v_new

    blk = pl.BlockSpec((r, c), lambda i: (0, 0))
    out = jax.ShapeDtypeStruct((r, c), F32)
    return pl.pallas_call(
        body, name="adamw_replicated", grid=(1,),
        in_specs=[pl.BlockSpec((p, r, c), lambda i: (0, 0, 0)), blk, blk, blk],
        out_specs=[blk, blk, blk, blk], out_shape=[out, out, out, out],
        compiler_params=_params(("arbitrary",)),
    )(parts, w, m, v)


SHARDED_NAMES = ("w_in", "ml_conv_w", "w_out", "ca_wq", "ca_wkv", "ca_wo", "ffn_w_up", "ffn_conv_w", "ffn_w_down")
SMALL_NAMES = ("b_in", "hg_lb_logits", "hg_norm_w", "ml_conv_b", "ml_norm_w", "ln1_g", "ln1_b",
               "ln2_g", "ln2_b", "ffn_conv_b", "ln3_g", "ln3_b")
WEIGHT_NAMES = ("w_in", "b_in", "hg_lb_logits", "hg_norm_w", "ml_conv_w", "ml_conv_b", "ml_norm_w", "w_out",
                "ln1_g", "ln1_b", "ca_wq", "ca_wkv", "ca_wo", "ln2_g", "ln2_b", "ffn_w_up", "ffn_conv_w",
                "ffn_conv_b", "ffn_w_down", "ln3_g", "ln3_b")
PAD_TO = {"w_in": W_IN_SHARD_P, "ffn_w_up": UP_SHARD_P, "ffn_conv_w": UP_SHARD_P}
SMALL_ROWS = 24
SMALL_W = D_MODEL


def _shard_2d(name, block):
    t = block[0]
    if name in PAD_TO:
        t = jnp.pad(t, ((0, 0), (0, PAD_TO[name] - t.shape[1])))
    return t


def _shard_like(name, t, like):
    return t[:, :like.shape[2]][None]


def _pad_cols(t, width):
    return jnp.pad(t, ((0, 0), (0, width - t.shape[1])))


FIRST_NAMES = ("w_in", "ml_conv_w")
LATE_NAMES = ("w_out", "ca_wq", "ca_wkv", "ca_wo", "ffn_w_up", "ffn_conv_w", "ffn_w_down")
FFN_NAMES = ("ffn_w_up", "ffn_w_down", "ffn_conv_w")
MID_NAMES = ("ca_wo", "ca_wq", "ca_wkv", "w_out")


def _first_weights(g, small):
    w = dict(small)
    w_in = jnp.concatenate([g["w_in"][j, :, :W_IN_SHARD] for j in range(N_DEV)], axis=1)
    w["w_in_main"] = w_in[:, :D_IN_MAIN]
    w["w_in_gate"] = _pad_cols(w_in[:, D_IN_MAIN:], LANES)
    w["b_in_main"] = small["b_in"][:, :D_IN_MAIN]
    w["b_in_gate"] = _pad_cols(small["b_in"][:, D_IN_MAIN:], LANES)
    w["ml_conv_w"] = jnp.transpose(g["ml_conv_w"], (1, 0, 2)).reshape(ML_CONV, 2 * D_GROUP)
    return w


def _late_weights(g, small):
    w = {}
    for n in ("w_out", "ca_wq", "ca_wo"):
        w[n] = g[n].reshape(D_MODEL, D_MODEL)
    w["ca_wkv"] = g["ca_wkv"]
    w["ffn_w_up"] = g["ffn_w_up"]
    down = g["ffn_w_down"].reshape(N_DEV // 2, UP_SHARD, D_MODEL)
    w["ffn_w_down"] = jnp.pad(down, ((0, 0), (0, UP_SHARD_P - UP_SHARD), (0, 0))).reshape(D_FF_P, D_MODEL)
    w["ffn_conv_w"] = jnp.transpose(g["ffn_conv_w"], (1, 0, 2)).reshape(FFN_CONV, D_UP_P)
    w["ffn_conv_b"] = _pad_cols(small["ffn_conv_b"].reshape(N_DEV, UP_SHARD), UP_SHARD_P).reshape(1, D_UP_P)
    return w


def _whole_weights(g, small):
    return {**_first_weights(g, small), **_late_weights(g, small)}


def _owner_stack(n, grads):
    if n == "w_in":
        w_in = jnp.concatenate([grads["w_in_main"], grads["w_in_gate"][:, :D_IN - D_IN_MAIN]], axis=1)
        return jnp.stack([_pad_cols(w_in[:, j * W_IN_SHARD:(j + 1) * W_IN_SHARD], W_IN_SHARD_P)
                          for j in range(N_DEV)])
    if n in ("w_out", "ca_wq", "ca_wo"):
        return grads[n].reshape(N_DEV, D_MODEL // N_DEV, D_MODEL)
    if n == "ffn_w_down":
        down = grads[n].reshape(N_DEV // 2, UP_SHARD_P, D_MODEL)[:, :UP_SHARD]
        return down.reshape(N_DEV, D_FF // N_DEV, D_MODEL)
    if n == "ml_conv_w":
        return jnp.transpose(grads[n].reshape(ML_CONV, N_DEV, LANES), (1, 0, 2))
    if n == "ffn_conv_w":
        return jnp.transpose(grads[n].reshape(FFN_CONV, N_DEV, UP_SHARD_P), (1, 0, 2))
    return grads[n]


def _owner_stacks(grads):
    return {n: _owner_stack(n, grads) for n in SHARDED_NAMES}


def _small_grads(grads):
    out = {n: grads[n] for n in SMALL_NAMES if n in grads}
    out["b_in"] = jnp.concatenate([grads["b_in_main"], grads["b_in_gate"][:, :D_IN - D_IN_MAIN]], axis=1)
    out["ffn_conv_b"] = grads["ffn_conv_b"].reshape(N_DEV, UP_SHARD_P)[:, :UP_SHARD].reshape(1, D_UP)
    return out


def _pack_small(p, extra=None):
    flat = [p[n].reshape(-1) for n in SMALL_NAMES]
    if extra is not None:
        flat.append(extra.reshape(-1))
    flat = jnp.concatenate(flat)
    return jnp.pad(flat, (0, SMALL_ROWS * SMALL_W - flat.shape[0])).reshape(SMALL_ROWS, SMALL_W)


def _unpack_small(slab, like):
    out = {}
    flat = slab.reshape(-1)
    o = 0
    for n in SMALL_NAMES:
        out[n] = flat[o:o + like[n].size].reshape(like[n].shape)
        o += like[n].size
    return out, flat[o]


def kernel(x, mem, w_in, b_in, hg_lb_logits, hg_norm_w, ml_conv_w, ml_conv_b, ml_norm_w, w_out, ln1_g, ln1_b, ca_wq, ca_wkv, ca_wo, ln2_g, ln2_b, ffn_w_up, ffn_conv_w, ffn_conv_b, ffn_w_down, ln3_g, ln3_b, loss_target, m_w_in, m_b_in, m_hg_lb_logits, m_hg_norm_w, m_ml_conv_w, m_ml_conv_b, m_ml_norm_w, m_w_out, m_ln1_g, m_ln1_b, m_ca_wq, m_ca_wkv, m_ca_wo, m_ln2_g, m_ln2_b, m_ffn_w_up, m_ffn_conv_w, m_ffn_conv_b, m_ffn_w_down, m_ln3_g, m_ln3_b, v_w_in, v_b_in, v_hg_lb_logits, v_hg_norm_w, v_ml_conv_w, v_ml_conv_b, v_ml_norm_w, v_w_out, v_ln1_g, v_ln1_b, v_ca_wq, v_ca_wkv, v_ca_wo, v_ln2_g, v_ln2_b, v_ffn_w_up, v_ffn_conv_w, v_ffn_conv_b, v_ffn_w_down, v_ln3_g, v_ln3_b):
    params = dict(w_in=w_in, b_in=b_in, hg_lb_logits=hg_lb_logits, hg_norm_w=hg_norm_w, ml_conv_w=ml_conv_w,
                  ml_conv_b=ml_conv_b, ml_norm_w=ml_norm_w, w_out=w_out, ln1_g=ln1_g, ln1_b=ln1_b, ca_wq=ca_wq,
                  ca_wkv=ca_wkv, ca_wo=ca_wo, ln2_g=ln2_g, ln2_b=ln2_b, ffn_w_up=ffn_w_up, ffn_conv_w=ffn_conv_w,
                  ffn_conv_b=ffn_conv_b, ffn_w_down=ffn_w_down, ln3_g=ln3_g, ln3_b=ln3_b)
    mom1 = dict(w_in=m_w_in, b_in=m_b_in, hg_lb_logits=m_hg_lb_logits, hg_norm_w=m_hg_norm_w,
                ml_conv_w=m_ml_conv_w, ml_conv_b=m_ml_conv_b, ml_norm_w=m_ml_norm_w, w_out=m_w_out, ln1_g=m_ln1_g,
                ln1_b=m_ln1_b, ca_wq=m_ca_wq, ca_wkv=m_ca_wkv, ca_wo=m_ca_wo, ln2_g=m_ln2_g, ln2_b=m_ln2_b,
                ffn_w_up=m_ffn_w_up, ffn_conv_w=m_ffn_conv_w, ffn_conv_b=m_ffn_conv_b, ffn_w_down=m_ffn_w_down,
                ln3_g=m_ln3_g, ln3_b=m_ln3_b)
    mom2 = dict(w_in=v_w_in, b_in=v_b_in, hg_lb_logits=v_hg_lb_logits, hg_norm_w=v_hg_norm_w,
                ml_conv_w=v_ml_conv_w, ml_conv_b=v_ml_conv_b, ml_norm_w=v_ml_norm_w, w_out=v_w_out, ln1_g=v_ln1_g,
                ln1_b=v_ln1_b, ca_wq=v_ca_wq, ca_wkv=v_ca_wkv, ca_wo=v_ca_wo, ln2_g=v_ln2_g, ln2_b=v_ln2_b,
                ffn_w_up=v_ffn_w_up, ffn_conv_w=v_ffn_conv_w, ffn_conv_b=v_ffn_conv_b, ffn_w_down=v_ffn_w_down,
                ln3_g=v_ln3_g, ln3_b=v_ln3_b)

    x_idx, y_idx, c_idx = _coords()
    as_index = lambda v: jnp.reshape(v, (1,)).astype(jnp.int32)
    core, chip, me = as_index(c_idx), as_index(2 * x_idx + y_idx), as_index(4 * x_idx + 2 * y_idx + c_idx)
    small_params = {n: params[n] for n in SMALL_NAMES}

    shards = {n: _shard_2d(n, params[n]) for n in SHARDED_NAMES}
    to_send = lambda names: [shards[n] if "conv" in n else shards[n].astype(BF16) for n in names]
    first = dict(zip(FIRST_NAMES, _all_gather_two_level(to_send(FIRST_NAMES), "weights_gather_first")))
    late_started, first["w_in"] = _direct_start(True, to_send(LATE_NAMES), first["w_in"], "weights_gather_start")

    def late_weights(y):
        mine, lands = _direct_wait(True, late_started, y, "weights_gather_wait")
        gathered = {n: lax.dynamic_update_index_in_dim(land, own, me[0], 0)
                    for n, own, land in zip(LATE_NAMES, mine, lands)}
        return _late_weights(gathered, small_params)

    started = {}

    def start_group(names, tag):
        def hook(grads, through):
            started[tag], through = _direct_start(False, [_owner_stack(n, grads) for n in names], through,
                                                  "grads_start_" + tag)
            return through
        return hook

    loss, grad_x, grads = _local_step(x[0], mem[0], loss_target[0], _first_weights(first, small_params),
                                      late_weights, start_group(FFN_NAMES, "ffn"), start_group(MID_NAMES, "mid"))

    sharded_out = {}

    def adamw(n, index, own, got):
        res = _adamw_sharded(index, own, got, shards[n], _shard_2d(n, mom1[n]), _shard_2d(n, mom2[n]), "adamw_" + n)
        sharded_out[n] = [_shard_like(n, t, params[n]) for t in res]

    stacks = [_owner_stack(n, grads) for n in FIRST_NAMES]
    from_sibling = _exchange_with_sibling(stacks)
    chip_sums = [_add_sibling(core, st, got, "grad_add_" + n) for n, st, got in zip(FIRST_NAMES, stacks, from_sibling)]
    from_chips = _exchange_with_chips(chip_sums)
    for n, sums, got in zip(FIRST_NAMES, chip_sums, from_chips):
        adamw(n, chip, sums, got)
    for names, tag in ((FFN_NAMES, "ffn"), (MID_NAMES, "mid")):
        own, lands = _direct_wait(False, started[tag], grad_x, "grads_wait_" + tag)
        for n, st, land in zip(names, own, lands):
            adamw(n, me, st, land)
    small_parts = _all_gather_direct(_pack_small(_small_grads(grads), loss), "small_all_gather")
    small_res = _adamw_replicated(small_parts, _pack_small(params), _pack_small(mom1), _pack_small(mom2))

    outs = []
    total_loss = None
    for k in range(4):
        small, extra = _unpack_small(small_res[k], params)
        if total_loss is None:
            total_loss = extra
        outs.extend(sharded_out[n][k] if n in sharded_out else small[n] for n in WEIGHT_NAMES)
    return (total_loss, grad_x[None], *outs)
```

```python
import jax
import jax.numpy as jnp
from jax import lax
from jax.experimental import pallas as pl
from jax.experimental.pallas import tpu as pltpu

F32 = jnp.float32
BF16 = jnp.bfloat16
HIGHEST = lax.Precision.HIGHEST
MESH = pl.DeviceIdType.MESH

N_DEV = 8
N_CHIPS = 4
D_MODEL = 1024
N_MEM = 256
N_HEADS = 4
D_HEAD = 128
D_GROUP = N_HEADS * D_HEAD
CHUNK = 64
ML_CONV = 4
FFN_CONV = 3
D_FF = 2816
D_UP = 2 * D_FF
CA_HEADS = 4
CA_DH = D_MODEL // CA_HEADS
LANES = 128
SUBLANES = 8
D_IN = 8 * D_GROUP + 2 * N_HEADS
D_IN_MAIN = 8 * D_GROUP
W_IN_SHARD = D_IN // N_DEV
W_IN_SHARD_P = 640
UP_SHARD = D_UP // N_DEV
UP_SHARD_P = 768
D_UP_P = N_DEV * UP_SHARD_P
D_FF_P = D_UP_P // 2
ALPHA = 2.0 ** 0.25
LN_EPS = 1e-5
NEG_BIG = -1e30
ADAM_LR = 0.001
ADAM_B1 = 0.9
ADAM_B2 = 0.999
ADAM_EPS = 1e-08
ADAM_WD = 0.01
ADAM_STEP = 10
VMEM_LIMIT = 56 * 1024 * 1024

SEG_HQ, SEG_HF, SEG_HI, SEG_HG, SEG_MQ, SEG_MK, SEG_MV, SEG_MO = (4 * i for i in range(8))


def _params(sem):
    return pltpu.CompilerParams(dimension_semantics=sem, vmem_limit_bytes=VMEM_LIMIT)


def _dg(a, b, ca, cb, precision=None):
    return lax.dot_general(a, b, (((ca,), (cb,)), ((), ())), precision=precision,
                           preferred_element_type=F32)


def _nn_raw(a, b):
    return _dg(a.astype(BF16), b.astype(BF16), 1, 0)


def _nt_raw(a, b):
    return _dg(a.astype(BF16), b.astype(BF16), 1, 1)


def _tn_raw(a, b):
    return _dg(a.astype(BF16), b.astype(BF16), 0, 0)


@jax.custom_vjp
def _nn(a, b):
    return _nn_raw(a, b)


_nn.defvjp(lambda a, b: (_nn_raw(a, b), (a, b)),
           lambda res, g: (_nt_raw(g, res[1]), _tn_raw(res[0], g)))


@jax.custom_vjp
def _nt(a, b):
    return _nt_raw(a, b)


_nt.defvjp(lambda a, b: (_nt_raw(a, b), (a, b)),
           lambda res, g: (_nn_raw(g, res[1]), _tn_raw(g, res[0])))


@jax.custom_vjp
def _tn(a, b):
    return _tn_raw(a, b)


_tn.defvjp(lambda a, b: (_tn_raw(a, b), (a, b)),
           lambda res, g: (_nt_raw(res[1], g), _nn_raw(res[0], g)))


def _layer_norm(z, g, b):
    mu = jnp.mean(z, axis=-1, keepdims=True)
    var = jnp.mean(jnp.square(z - mu), axis=-1, keepdims=True)
    return (z - mu) * lax.rsqrt(var + LN_EPS) * g + b


def _matmul_nn(a, w, bias, tm, tn, name):
    m, k = a.shape
    if w.ndim == 3:
        n = w.shape[0] * w.shape[2]
        assert tn == w.shape[2]
        w_spec = pl.BlockSpec((None, k, tn), lambda i, j: (j, 0, 0))
    else:
        n = w.shape[1]
        w_spec = pl.BlockSpec((k, tn), lambda i, j: (0, j))

    def body(*refs):
        a_ref, w_ref = refs[0], refs[1]
        o_ref = refs[-1]
        acc = _nn_raw(a_ref[...], w_ref[...])
        if bias is not None:
            acc = acc + refs[2][...]
        o_ref[...] = acc

    in_specs = [pl.BlockSpec((tm, k), lambda i, j: (i, 0)), w_spec]
    args = [a, w]
    if bias is not None:
        in_specs.append(pl.BlockSpec((1, tn), lambda i, j: (0, j)))
        args.append(bias)
    return pl.pallas_call(
        body, name=name, grid=(m // tm, n // tn), in_specs=in_specs,
        out_specs=pl.BlockSpec((tm, tn), lambda i, j: (i, j)),
        out_shape=jax.ShapeDtypeStruct((m, n), F32),
        compiler_params=_params(("parallel", "parallel")),
    )(*args)


def _matmul_nt(pairs, add, scale, tm, tk, name):
    m = pairs[0][0].shape[0]
    k = pairs[0][1].shape[-2]
    groups = []
    in_specs, args = [], []
    for pair in pairs:
        d, w = pair[0], pair[1]
        in_specs.append(pl.BlockSpec((tm, d.shape[1]), lambda i, j: (i, 0)))
        if w.ndim == 3:
            g = d.shape[1] // w.shape[2]
            blk = pair[2] // g
            in_specs.append(pl.BlockSpec((g, tk, w.shape[2]), lambda i, j, blk=blk: (blk, j, 0)))
            groups.append((g, w.shape[2]))
        else:
            in_specs.append(pl.BlockSpec((tk, w.shape[1]), lambda i, j: (j, 0)))
            groups.append(None)
        args += [d, w]
    if add is not None:
        in_specs.append(pl.BlockSpec((tm, tk), lambda i, j: (i, j)))
        args.append(add)

    def body(*refs):
        o_ref = refs[-1]
        acc = None
        for p, grp in enumerate(groups):
            d_ref, w_ref = refs[2 * p], refs[2 * p + 1]
            if grp is None:
                terms = [_nt_raw(d_ref[...], w_ref[...])]
            else:
                terms = [_nt_raw(d_ref[:, g * grp[1]:(g + 1) * grp[1]], w_ref[g]) for g in range(grp[0])]
            for t in terms:
                acc = t if acc is None else acc + t
        if add is not None:
            acc = acc + scale * refs[2 * len(groups)][...]
        o_ref[...] = acc

    return pl.pallas_call(
        body, name=name, grid=(m // tm, k // tk), in_specs=in_specs,
        out_specs=pl.BlockSpec((tm, tk), lambda i, j: (i, j)),
        out_shape=jax.ShapeDtypeStruct((m, k), F32),
        compiler_params=_params(("parallel", "parallel")),
    )(*args)


def _matmul_tn(a, b, tm, tn, tt, name, shards=None, shard0=0, group=1, into=None, colsum=False):
    t, m = a.shape
    n = b.shape[1]
    assert not colsum or tm == m
    n_in = 2 + (into is not None)
    per_step = 1 if shards is None else group
    width = per_step * tn

    def body(*refs):
        a_ref, b_ref = refs[0], refs[1]
        o_ref = refs[n_in]
        first = pl.program_id(2) == 0

        @pl.when(first)
        def _():
            o_ref[...] = jnp.zeros_like(o_ref)

        if shards is None:
            o_ref[...] += _tn_raw(a_ref[...], b_ref[...])
        else:
            lhs = a_ref[...].astype(BF16)
            for g in range(per_step):
                o_ref[g] += _tn_raw(lhs, b_ref[:, g * tn:(g + 1) * tn])
        if colsum:
            s_ref = refs[n_in + 1]

            @pl.when(first)
            def _():
                s_ref[...] = jnp.zeros_like(s_ref)

            s_ref[...] += jnp.sum(b_ref[...], axis=0, keepdims=True)

    in_specs = [pl.BlockSpec((tt, tm), lambda i, j, kk: (kk, i)),
                pl.BlockSpec((tt, width), lambda i, j, kk: (kk, j))]
    args = [a, b]
    aliases = {}
    if into is not None:
        in_specs.append(pl.BlockSpec(memory_space=pl.ANY))
        args.append(into)
        aliases = {2: 0}
    if shards is None:
        out_specs = [pl.BlockSpec((tm, tn), lambda i, j, kk: (i, j))]
        out_shape = [jax.ShapeDtypeStruct((m, n), F32)]
    else:
        out_specs = [pl.BlockSpec((per_step, tm, tn), lambda i, j, kk: (shard0 // per_step + j, i, 0))]
        out_shape = [jax.ShapeDtypeStruct((shards, m, tn), F32)]
    if colsum:
        out_specs.append(pl.BlockSpec((1, tn), lambda i, j, kk: (0, j)))
        out_shape.append(jax.ShapeDtypeStruct((1, n), F32))
    res = pl.pallas_call(
        body, name=name, grid=(m // tm, n // width, t // tt), in_specs=in_specs, out_specs=out_specs,
        out_shape=out_shape, input_output_aliases=aliases,
        compiler_params=_params(("parallel", "parallel", "arbitrary")),
    )(*args)
    return res if colsum else res[0]


ROW_TILE = 512


def _conv_fwd_tile(pad_ref, w_ref, b_ref, r0, rows, taps):
    acc = b_ref[...]
    for j in range(taps):
        acc = acc + pad_ref[pl.ds(SUBLANES - (taps - 1 - j) + r0, rows), :] * w_ref[j:j + 1, :]
    return acc


def _conv_bwd_tile(dpad_ref, w_ref, r0, rows, taps):
    acc = None
    for j in range(taps):
        term = dpad_ref[pl.ds(r0 + (taps - 1 - j), rows), :] * w_ref[j:j + 1, :]
        acc = term if acc is None else acc + term
    return acc


def _conv_grads_tile(pad_ref, dpad_ref, dx_ref, w_ref, dws, r0, rows, taps):
    dx = _conv_bwd_tile(dpad_ref, w_ref, r0, rows, taps)
    dx_ref[r0:r0 + rows, :] = dx.astype(dx_ref.dtype)
    d_pre = dpad_ref[r0:r0 + rows, :]
    for j in range(taps):
        xs = pad_ref[pl.ds(SUBLANES - (taps - 1 - j) + r0, rows), :]
        dws[j] = dws[j] + jnp.sum(d_pre * xs, axis=0, keepdims=True)
    return jnp.sum(dx, axis=0, keepdims=True)


def _ml_conv_fwd(proj, conv_w, conv_b):
    s = proj.shape[0]
    nblk = 2 * D_GROUP // LANES

    def body(x_ref, w_ref, b_ref, o_ref, pad_ref):
        pad_ref[0:SUBLANES, :] = jnp.zeros((SUBLANES, LANES), F32)
        pad_ref[SUBLANES:, :] = x_ref[...]
        for r0 in range(0, s, ROW_TILE):
            rows = min(ROW_TILE, s - r0)
            o_ref[r0:r0 + rows, :] = jax.nn.silu(_conv_fwd_tile(pad_ref, w_ref, b_ref, r0, rows, ML_CONV))

    return pl.pallas_call(
        body, name="ml_conv_fwd", grid=(nblk,),
        in_specs=[pl.BlockSpec((s, LANES), lambda j: (0, SEG_MQ + j)),
                  pl.BlockSpec((ML_CONV, LANES), lambda j: (0, j)),
                  pl.BlockSpec((1, LANES), lambda j: (0, j))],
        out_specs=pl.BlockSpec((s, LANES), lambda j: (0, j)),
        out_shape=jax.ShapeDtypeStruct((s, 2 * D_GROUP), F32),
        scratch_shapes=[pltpu.VMEM((s + SUBLANES, LANES), F32)],
        compiler_params=_params(("parallel",)),
    )(proj, conv_w, conv_b)


def _ml_conv_bwd(proj, conv_w, conv_b, d_qk, d_proj):
    s = proj.shape[0]
    nblk = 2 * D_GROUP // LANES

    def body(x_ref, w_ref, b_ref, dy_ref, _, dx_ref, dw_ref, db_ref, dxs_ref, pad_ref, dpad_ref):
        pad_ref[0:SUBLANES, :] = jnp.zeros((SUBLANES, LANES), F32)
        pad_ref[SUBLANES:, :] = x_ref[...]
        dpad_ref[s:, :] = jnp.zeros((SUBLANES, LANES), F32)
        db = jnp.zeros((1, LANES), F32)
        for r0 in range(0, s, ROW_TILE):
            rows = min(ROW_TILE, s - r0)
            pre = _conv_fwd_tile(pad_ref, w_ref, b_ref, r0, rows, ML_CONV)
            _, vjp = jax.vjp(jax.nn.silu, pre)
            d_pre, = vjp(dy_ref[r0:r0 + rows, :])
            dpad_ref[r0:r0 + rows, :] = d_pre
            db = db + jnp.sum(d_pre, axis=0, keepdims=True)
        db_ref[...] = db
        dws = [jnp.zeros((1, LANES), F32) for _ in range(ML_CONV)]
        dx_sum = jnp.zeros((1, LANES), F32)
        for r0 in range(0, s, ROW_TILE):
            dx_sum = dx_sum + _conv_grads_tile(pad_ref, dpad_ref, dx_ref, w_ref, dws, r0, min(ROW_TILE, s - r0),
                                               ML_CONV)
        dxs_ref[...] = dx_sum
        for j in range(ML_CONV):
            dw_ref[j:j + 1, :] = dws[j]

    return pl.pallas_call(
        body, name="ml_conv_bwd", grid=(nblk,),
        in_specs=[pl.BlockSpec((s, LANES), lambda j: (0, SEG_MQ + j)),
                  pl.BlockSpec((ML_CONV, LANES), lambda j: (0, j)),
                  pl.BlockSpec((1, LANES), lambda j: (0, j)),
                  pl.BlockSpec((s, LANES), lambda j: (0, j)),
                  pl.BlockSpec(memory_space=pl.ANY)],
        out_specs=[pl.BlockSpec((s, LANES), lambda j: (0, SEG_MQ + j)),
                   pl.BlockSpec((ML_CONV, LANES), lambda j: (0, j)),
                   pl.BlockSpec((1, LANES), lambda j: (0, j)),
                   pl.BlockSpec((1, LANES), lambda j: (0, j))],
        out_shape=[jax.ShapeDtypeStruct(d_proj.shape, d_proj.dtype),
                   jax.ShapeDtypeStruct((ML_CONV, 2 * D_GROUP), F32),
                   jax.ShapeDtypeStruct((1, 2 * D_GROUP), F32),
                   jax.ShapeDtypeStruct((1, 2 * D_GROUP), F32)],
        input_output_aliases={4: 0},
        scratch_shapes=[pltpu.VMEM((s + SUBLANES, LANES), F32), pltpu.VMEM((s + SUBLANES, LANES), F32)],
        compiler_params=_params(("parallel",)),
    )(proj, conv_w, conv_b, d_qk, d_proj)


def _gelu_mul(a, b):
    return jax.nn.gelu(a) * b


FFN_BLOCKS = D_FF_P // LANES


def _ffn_conv_fwd(u, conv_w, conv_b):
    s = u.shape[0]

    def body(g_ref, v_ref, wg_ref, wv_ref, bg_ref, bv_ref, o_ref, gpad_ref, vpad_ref):
        for pad_ref, x_ref in ((gpad_ref, g_ref), (vpad_ref, v_ref)):
            pad_ref[0:SUBLANES, :] = jnp.zeros((SUBLANES, LANES), F32)
            pad_ref[SUBLANES:, :] = x_ref[...]
        for r0 in range(0, s, ROW_TILE):
            rows = min(ROW_TILE, s - r0)
            ug = _conv_fwd_tile(gpad_ref, wg_ref, bg_ref, r0, rows, FFN_CONV)
            uv = _conv_fwd_tile(vpad_ref, wv_ref, bv_ref, r0, rows, FFN_CONV)
            o_ref[r0:r0 + rows, :] = _gelu_mul(ug, uv).astype(o_ref.dtype)

    col = lambda off: (lambda j: (0, off + j))
    return pl.pallas_call(
        body, name="ffn_conv_fwd", grid=(FFN_BLOCKS,),
        in_specs=[pl.BlockSpec((s, LANES), col(0)), pl.BlockSpec((s, LANES), col(FFN_BLOCKS)),
                  pl.BlockSpec((FFN_CONV, LANES), col(0)), pl.BlockSpec((FFN_CONV, LANES), col(FFN_BLOCKS)),
                  pl.BlockSpec((1, LANES), col(0)), pl.BlockSpec((1, LANES), col(FFN_BLOCKS))],
        out_specs=pl.BlockSpec((s, LANES), col(0)),
        out_shape=jax.ShapeDtypeStruct((s, D_FF_P), BF16),
        scratch_shapes=[pltpu.VMEM((s + SUBLANES, LANES), F32), pltpu.VMEM((s + SUBLANES, LANES), F32)],
        compiler_params=_params(("parallel",)),
    )(u, u, conv_w, conv_w, conv_b, conv_b)


def _ffn_conv_bwd(u, conv_w, conv_b, d_h):
    s = u.shape[0]

    def body(g_ref, v_ref, wg_ref, wv_ref, bg_ref, bv_ref, dh_ref,
             dug_ref, duv_ref, dwg_ref, dwv_ref, dbg_ref, dbv_ref,
             gpad_ref, vpad_ref, dgpad_ref, dvpad_ref):
        for pad_ref, x_ref in ((gpad_ref, g_ref), (vpad_ref, v_ref)):
            pad_ref[0:SUBLANES, :] = jnp.zeros((SUBLANES, LANES), F32)
            pad_ref[SUBLANES:, :] = x_ref[...]
        dgpad_ref[s:, :] = jnp.zeros((SUBLANES, LANES), F32)
        dvpad_ref[s:, :] = jnp.zeros((SUBLANES, LANES), F32)
        dbg = jnp.zeros((1, LANES), F32)
        dbv = jnp.zeros((1, LANES), F32)
        for r0 in range(0, s, ROW_TILE):
            rows = min(ROW_TILE, s - r0)
            ug = _conv_fwd_tile(gpad_ref, wg_ref, bg_ref, r0, rows, FFN_CONV)
            uv = _conv_fwd_tile(vpad_ref, wv_ref, bv_ref, r0, rows, FFN_CONV)
            _, vjp = jax.vjp(_gelu_mul, ug, uv)
            d_ug, d_uv = vjp(dh_ref[r0:r0 + rows, :])
            dgpad_ref[r0:r0 + rows, :] = d_ug
            dvpad_ref[r0:r0 + rows, :] = d_uv
            dbg = dbg + jnp.sum(d_ug, axis=0, keepdims=True)
            dbv = dbv + jnp.sum(d_uv, axis=0, keepdims=True)
        dbg_ref[...] = dbg
        dbv_ref[...] = dbv
        for pad_ref, dpad_ref, w_ref, dx_ref, dw_ref in ((gpad_ref, dgpad_ref, wg_ref, dug_ref, dwg_ref),
                                                         (vpad_ref, dvpad_ref, wv_ref, duv_ref, dwv_ref)):
            dws = [jnp.zeros((1, LANES), F32) for _ in range(FFN_CONV)]
            for r0 in range(0, s, ROW_TILE):
                _conv_grads_tile(pad_ref, dpad_ref, dx_ref, w_ref, dws, r0, min(ROW_TILE, s - r0), FFN_CONV)
            for j in range(FFN_CONV):
                dw_ref[j:j + 1, :] = dws[j]

    col = lambda off: (lambda j: (0, off + j))
    seq = pl.BlockSpec((s, LANES), col(0))
    return pl.pallas_call(
        body, name="ffn_conv_bwd", grid=(FFN_BLOCKS,),
        in_specs=[pl.BlockSpec((s, LANES), col(0)), pl.BlockSpec((s, LANES), col(FFN_BLOCKS)),
                  pl.BlockSpec((FFN_CONV, LANES), col(0)), pl.BlockSpec((FFN_CONV, LANES), col(FFN_BLOCKS)),
                  pl.BlockSpec((1, LANES), col(0)), pl.BlockSpec((1, LANES), col(FFN_BLOCKS)), seq],
        out_specs=[seq, seq, pl.BlockSpec((FFN_CONV, LANES), col(0)), pl.BlockSpec((FFN_CONV, LANES), col(0)),
                   pl.BlockSpec((1, LANES), col(0)), pl.BlockSpec((1, LANES), col(0))],
        out_shape=[jax.ShapeDtypeStruct((s, D_FF_P), BF16), jax.ShapeDtypeStruct((s, D_FF_P), BF16),
                   jax.ShapeDtypeStruct((FFN_CONV, D_FF_P), F32), jax.ShapeDtypeStruct((FFN_CONV, D_FF_P), F32),
                   jax.ShapeDtypeStruct((1, D_FF_P), F32), jax.ShapeDtypeStruct((1, D_FF_P), F32)],
        scratch_shapes=[pltpu.VMEM((s + SUBLANES, LANES), F32) for _ in range(4)],
        compiler_params=_params(("parallel",)),
    )(u, u, conv_w, conv_w, conv_b, conv_b, d_h)


def _chunk_masks(c):
    row = lax.broadcasted_iota(jnp.int32, (c, c), 0)
    col = lax.broadcasted_iota(jnp.int32, (c, c), 1)
    return row, col


@jax.custom_vjp
def _split_heads(x):
    return tuple(x[:, h * D_HEAD:(h + 1) * D_HEAD] for h in range(N_HEADS))


_split_heads.defvjp(lambda x: (_split_heads(x), None), lambda _, gs: (jnp.concatenate(gs, axis=1),))


@jax.custom_vjp
def _merge_heads(xs):
    return jnp.concatenate(xs, axis=1)


_merge_heads.defvjp(lambda xs: (_merge_heads(xs), None), lambda _, g: (_split_heads(g),))

HEADS = range(N_HEADS)


def _hg_chunk(hq, hf, hi, hgate, l0, l1, nw, sts):
    c = hq.shape[0]
    row, col = _chunk_masks(c)
    mask = col <= row
    mx = lax.stop_gradient(jnp.maximum(l0, l1))
    e0 = jnp.exp(l0 - mx)
    e1 = jnp.exp(l1 - mx)
    lb = e0 / (e0 + e1)
    sig = jax.nn.sigmoid(hf)
    lf = jnp.log(lb + (1.0 - lb) * sig)
    k = (1.0 - lb) * jax.nn.sigmoid(-hf)
    q = jax.nn.silu(hq)
    b = _dg(mask.astype(F32), lf, 1, 0, HIGHEST)
    rid = lax.broadcasted_iota(jnp.int32, b.shape, 0)
    b_ref = jnp.sum(jnp.where(rid == c // 2 - 1, b, 0.0), axis=0, keepdims=True)
    b_last = jnp.sum(jnp.where(rid == c - 1, b, 0.0), axis=0, keepdims=True)
    qa = _split_heads(q * jnp.exp(b - b_ref))
    ka = _split_heads(k * jnp.exp(b_ref - b))
    qe = _split_heads(q * jnp.exp(b))
    kd = _split_heads(k * jnp.exp(b_last - b))
    decay = _split_heads(jnp.exp(b_last))
    v = _split_heads(hi)
    attn = [jnp.where(mask, _nt(qa[h], ka[h]), 0.0) for h in HEADS]
    intra = [_nn(attn[h], v[h]) for h in HEADS]
    inter = [_nt(qe[h], sts[h]) for h in HEADS]
    kv = [_tn(v[h], kd[h]) for h in HEADS]
    sts_new = tuple(decay[h] * sts[h] + kv[h] for h in HEADS)
    o = [intra[h] + inter[h] for h in HEADS]
    normed = _merge_heads(tuple(o[h] * lax.rsqrt(jnp.mean(o[h] * o[h], axis=-1, keepdims=True) + LN_EPS)
                                for h in HEADS))
    return normed * nw * jax.nn.silu(hgate), sts_new


def _seg(ref, seg):
    return ref[:, seg * D_GROUP:(seg + 1) * D_GROUP]


def _hgrn2_fwd(proj, logits, norm_w):
    s = proj.shape[0]
    nc = s // CHUNK

    def body(p_ref, lg_ref, nw_ref, y_ref, st_out_ref, st_scr):
        @pl.when(pl.program_id(0) == 0)
        def _():
            st_scr[...] = jnp.zeros_like(st_scr)

        sts = tuple(st_scr[h] for h in HEADS)
        y, sts_new = _hg_chunk(_seg(p_ref, 0), _seg(p_ref, 1), _seg(p_ref, 2), _seg(p_ref, 3),
                               lg_ref[0:1, :], lg_ref[1:2, :], nw_ref[...], sts)
        y_ref[...] = y.astype(y_ref.dtype)
        for h in HEADS:
            st_out_ref[h] = sts[h]
            st_scr[h] = sts_new[h]

    return pl.pallas_call(
        body, name="hgrn2_fwd", grid=(nc,),
        in_specs=[pl.BlockSpec((CHUNK, 4 * D_GROUP), lambda c: (c, 0)),
                  pl.BlockSpec((2, D_GROUP), lambda c: (0, 0)),
                  pl.BlockSpec((1, D_GROUP), lambda c: (0, 0))],
        out_specs=[pl.BlockSpec((CHUNK, D_GROUP), lambda c: (c, 0)),
                   pl.BlockSpec((None, N_HEADS, D_HEAD, D_HEAD), lambda c: (c, 0, 0, 0))],
        out_shape=[jax.ShapeDtypeStruct((s, 2 * D_GROUP), BF16),
                   jax.ShapeDtypeStruct((nc, N_HEADS, D_HEAD, D_HEAD), F32)],
        scratch_shapes=[pltpu.VMEM((N_HEADS, D_HEAD, D_HEAD), F32)],
        compiler_params=_params(("arbitrary",)),
    )(proj, logits, norm_w)


def _hgrn2_bwd(proj, logits, norm_w, states, d_y):
    s = proj.shape[0]
    nc = s // CHUNK

    def body(p_ref, lg_ref, nw_ref, st_ref, dy_ref, dp_ref, dl_ref, dnw_ref, dsum_ref, dst_scr):
        @pl.when(pl.program_id(0) == 0)
        def _():
            dst_scr[...] = jnp.zeros_like(dst_scr)
            dl_ref[...] = jnp.zeros_like(dl_ref)
            dnw_ref[...] = jnp.zeros_like(dnw_ref)
            dsum_ref[...] = jnp.zeros_like(dsum_ref)

        _, vjp = jax.vjp(_hg_chunk, _seg(p_ref, 0), _seg(p_ref, 1), _seg(p_ref, 2), _seg(p_ref, 3),
                         lg_ref[0:1, :], lg_ref[1:2, :], nw_ref[...], tuple(st_ref[h] for h in HEADS))
        d_hq, d_hf, d_hi, d_hg, d_l0, d_l1, d_nw, d_sts = vjp((dy_ref[...], tuple(dst_scr[h] for h in HEADS)))
        for seg, val in enumerate((d_hq, d_hf, d_hi, d_hg)):
            dp_ref[:, seg * D_GROUP:(seg + 1) * D_GROUP] = val.astype(dp_ref.dtype)
            dsum_ref[:, seg * D_GROUP:(seg + 1) * D_GROUP] += jnp.sum(val, axis=0, keepdims=True)
        dl_ref[0:1, :] += d_l0
        dl_ref[1:2, :] += d_l1
        dnw_ref[...] += d_nw
        for h in HEADS:
            dst_scr[h] = d_sts[h]

    rev = lambda c: nc - 1 - c
    return pl.pallas_call(
        body, name="hgrn2_bwd", grid=(nc,),
        in_specs=[pl.BlockSpec((CHUNK, 4 * D_GROUP), lambda c: (rev(c), 0)),
                  pl.BlockSpec((2, D_GROUP), lambda c: (0, 0)),
                  pl.BlockSpec((1, D_GROUP), lambda c: (0, 0)),
                  pl.BlockSpec((None, N_HEADS, D_HEAD, D_HEAD), lambda c: (rev(c), 0, 0, 0)),
                  pl.BlockSpec((CHUNK, D_GROUP), lambda c: (rev(c), 0))],
        out_specs=[pl.BlockSpec((CHUNK, 4 * D_GROUP), lambda c: (rev(c), 0)),
                   pl.BlockSpec((2, D_GROUP), lambda c: (0, 0)),
                   pl.BlockSpec((1, D_GROUP), lambda c: (0, 0)),
                   pl.BlockSpec((1, 4 * D_GROUP), lambda c: (0, 0))],
        out_shape=[jax.ShapeDtypeStruct((s, D_IN_MAIN), BF16), jax.ShapeDtypeStruct((2, D_GROUP), F32),
                   jax.ShapeDtypeStruct((1, D_GROUP), F32), jax.ShapeDtypeStruct((1, 4 * D_GROUP), F32)],
        scratch_shapes=[pltpu.VMEM((N_HEADS, D_HEAD, D_HEAD), F32)],
        compiler_params=_params(("arbitrary",)),
    )(proj, logits, norm_w, states, d_y)


def _gate_column(gates, lane, idx):
    return jnp.sum(jnp.where(lane == idx, gates, 0.0), axis=1, keepdims=True)


def _head_layer_norm(h):
    mu = jnp.mean(h, axis=-1, keepdims=True)
    var = jnp.mean(jnp.square(h - mu), axis=-1, keepdims=True)
    return (h - mu) * lax.rsqrt(var + LN_EPS)


def _ml_chunk(qc, kc, v, mo, gates, nw, cts, ns, ms):
    c = qc.shape[0]
    row, col = _chunk_masks(c)
    mask = col <= row
    eye = col == row
    lane = lax.broadcasted_iota(jnp.int32, gates.shape, 1)
    to_row = lambda t: jnp.sum(jnp.where(eye, t, 0.0), axis=0, keepdims=True)
    q = _split_heads(qc * (D_HEAD ** -0.5))
    k = _split_heads(kc)
    vs = _split_heads(v)
    ig = [_gate_column(gates, lane, h) for h in HEADS]
    lf = [jax.nn.log_sigmoid(_gate_column(gates, lane, N_HEADS + h)) for h in HEADS]
    lf_row = [to_row(lf[h]) for h in HEADS]
    ig_row = [to_row(ig[h]) for h in HEADS]
    b_col = [jnp.sum(jnp.where(mask, lf_row[h], 0.0), axis=1, keepdims=True) for h in HEADS]
    b_row = [jnp.sum(jnp.where(row <= col, lf[h], 0.0), axis=0, keepdims=True) for h in HEADS]
    g = [jnp.sum(lf[h], axis=0, keepdims=True) for h in HEADS]
    d = [jnp.where(mask, b_col[h] - b_row[h] + ig_row[h], -jnp.inf) for h in HEADS]
    inter = [b_col[h] + ms[h] for h in HEADS]
    m_t = [lax.stop_gradient(jnp.maximum(inter[h], jnp.max(d[h], axis=1, keepdims=True))) for h in HEADS]
    qk = [_nt(q[h], k[h]) for h in HEADS]
    qc_state = [_nt(q[h], cts[h]) for h in HEADS]
    sc = [qk[h] * jnp.exp(d[h] - m_t[h]) for h in HEADS]
    w_inter = [jnp.exp(inter[h] - m_t[h]) for h in HEADS]
    sv = [_nn(sc[h], vs[h]) for h in HEADS]
    num = [sv[h] + w_inter[h] * qc_state[h] for h in HEADS]
    den = [jnp.sum(sc[h], axis=1, keepdims=True) + w_inter[h] * jnp.sum(q[h] * ns[h], axis=1, keepdims=True)
           for h in HEADS]
    hh = [num[h] / jnp.maximum(jnp.abs(den[h]), jnp.exp(-m_t[h])) for h in HEADS]
    a = [g[h] - b_col[h] + ig[h] for h in HEADS]
    ms_new = tuple(lax.stop_gradient(jnp.maximum(g[h] + ms[h], jnp.max(a[h], axis=0, keepdims=True)))
                   for h in HEADS)
    decay = [jnp.exp(g[h] + ms[h] - ms_new[h]) for h in HEADS]
    wk = [k[h] * jnp.exp(a[h] - ms_new[h]) for h in HEADS]
    kv = [_tn(vs[h], wk[h]) for h in HEADS]
    cts_new = tuple(decay[h] * cts[h] + kv[h] for h in HEADS)
    ns_new = tuple(decay[h] * ns[h] + jnp.sum(wk[h], axis=0, keepdims=True) for h in HEADS)
    normed = _merge_heads(tuple(_head_layer_norm(hh[h]) for h in HEADS))
    return jax.nn.sigmoid(mo) * (normed * nw), cts_new, ns_new, ms_new


def _mlstm_fwd(qk, proj, gates, norm_w, y):
    s = proj.shape[0]
    nc = s // CHUNK

    def body(qk_ref, vo_ref, g_ref, nw_ref, _, y_ref, ct_out, n_out, m_out, ct_scr, n_scr, m_scr):
        @pl.when(pl.program_id(0) == 0)
        def _():
            ct_scr[...] = jnp.zeros_like(ct_scr)
            n_scr[...] = jnp.zeros_like(n_scr)
            m_scr[...] = jnp.full(m_scr.shape, NEG_BIG, F32)

        cts = tuple(ct_scr[h] for h in HEADS)
        ns = tuple(n_scr[h] for h in HEADS)
        ms = tuple(m_scr[h] for h in HEADS)
        y, cts_new, ns_new, ms_new = _ml_chunk(_seg(qk_ref, 0), _seg(qk_ref, 1), _seg(vo_ref, 0), _seg(vo_ref, 1),
                                               g_ref[...], nw_ref[...], cts, ns, ms)
        y_ref[...] = y.astype(y_ref.dtype)
        for h in HEADS:
            ct_out[h], n_out[h], m_out[h] = cts[h], ns[h], ms[h]
            ct_scr[h], n_scr[h], m_scr[h] = cts_new[h], ns_new[h], ms_new[h]

    st = lambda r, w: pl.BlockSpec((None, N_HEADS, r, w), lambda c: (c, 0, 0, 0))
    return pl.pallas_call(
        body, name="mlstm_fwd", grid=(nc,),
        in_specs=[pl.BlockSpec((CHUNK, 2 * D_GROUP), lambda c: (c, 0)),
                  pl.BlockSpec((CHUNK, 2 * D_GROUP), lambda c: (c, 3)),
                  pl.BlockSpec((CHUNK, LANES), lambda c: (c, 0)),
                  pl.BlockSpec((1, D_GROUP), lambda c: (0, 0)),
                  pl.BlockSpec(memory_space=pl.ANY)],
        out_specs=[pl.BlockSpec((CHUNK, D_GROUP), lambda c: (c, 1)),
                   st(D_HEAD, D_HEAD), st(1, D_HEAD), st(1, 1)],
        out_shape=[jax.ShapeDtypeStruct(y.shape, y.dtype),
                   jax.ShapeDtypeStruct((nc, N_HEADS, D_HEAD, D_HEAD), F32),
                   jax.ShapeDtypeStruct((nc, N_HEADS, 1, D_HEAD), F32),
                   jax.ShapeDtypeStruct((nc, N_HEADS, 1, 1), F32)],
        input_output_aliases={4: 0},
        scratch_shapes=[pltpu.VMEM((N_HEADS, D_HEAD, D_HEAD), F32), pltpu.VMEM((N_HEADS, 1, D_HEAD), F32),
                        pltpu.VMEM((N_HEADS, 1, 1), F32)],
        compiler_params=_params(("arbitrary",)),
    )(qk, proj, gates, norm_w, y)


def _mlstm_bwd(qk, proj, gates, norm_w, ct_s, n_s, m_s, d_y, d_proj):
    s = proj.shape[0]
    nc = s // CHUNK

    def body(qk_ref, vo_ref, g_ref, nw_ref, ct_ref, n_ref, m_ref, dy_ref, _,
             dp_ref, dqk_ref, dg_ref, dnw_ref, dsum_ref, dct_scr, dn_scr):
        @pl.when(pl.program_id(0) == 0)
        def _():
            dct_scr[...] = jnp.zeros_like(dct_scr)
            dn_scr[...] = jnp.zeros_like(dn_scr)
            dnw_ref[...] = jnp.zeros_like(dnw_ref)
            dsum_ref[...] = jnp.zeros_like(dsum_ref)

        ms = tuple(m_ref[h] for h in HEADS)
        step = lambda *a: _ml_chunk(*a, ms)[:3]
        _, vjp = jax.vjp(step, _seg(qk_ref, 0), _seg(qk_ref, 1), _seg(vo_ref, 0), _seg(vo_ref, 1), g_ref[...],
                         nw_ref[...], tuple(ct_ref[h] for h in HEADS), tuple(n_ref[h] for h in HEADS))
        d_q, d_k, d_v, d_o, d_gates, d_nw, d_cts, d_ns = vjp(
            (dy_ref[...], tuple(dct_scr[h] for h in HEADS), tuple(dn_scr[h] for h in HEADS)))
        dqk_ref[:, 0:D_GROUP] = d_q
        dqk_ref[:, D_GROUP:2 * D_GROUP] = d_k
        for seg, val in enumerate((d_v, d_o)):
            dp_ref[:, seg * D_GROUP:(seg + 1) * D_GROUP] = val.astype(dp_ref.dtype)
            dsum_ref[:, seg * D_GROUP:(seg + 1) * D_GROUP] += jnp.sum(val, axis=0, keepdims=True)
        dg_ref[...] = d_gates
        dnw_ref[...] += d_nw
        for h in HEADS:
            dct_scr[h] = d_cts[h]
            dn_scr[h] = d_ns[h]

    rev = lambda c: nc - 1 - c
    st = lambda r, w: pl.BlockSpec((None, N_HEADS, r, w), lambda c: (rev(c), 0, 0, 0))
    return pl.pallas_call(
        body, name="mlstm_bwd", grid=(nc,),
        in_specs=[pl.BlockSpec((CHUNK, 2 * D_GROUP), lambda c: (rev(c), 0)),
                  pl.BlockSpec((CHUNK, 2 * D_GROUP), lambda c: (rev(c), 3)),
                  pl.BlockSpec((CHUNK, LANES), lambda c: (rev(c), 0)),
                  pl.BlockSpec((1, D_GROUP), lambda c: (0, 0)),
                  st(D_HEAD, D_HEAD), st(1, D_HEAD), st(1, 1),
                  pl.BlockSpec((CHUNK, D_GROUP), lambda c: (rev(c), 1)),
                  pl.BlockSpec(memory_space=pl.ANY)],
        out_specs=[pl.BlockSpec((CHUNK, 2 * D_GROUP), lambda c: (rev(c), 3)),
                   pl.BlockSpec((CHUNK, 2 * D_GROUP), lambda c: (rev(c), 0)),
                   pl.BlockSpec((CHUNK, LANES), lambda c: (rev(c), 0)),
                   pl.BlockSpec((1, D_GROUP), lambda c: (0, 0)),
                   pl.BlockSpec((1, 2 * D_GROUP), lambda c: (0, 0))],
        out_shape=[jax.ShapeDtypeStruct(d_proj.shape, d_proj.dtype), jax.ShapeDtypeStruct((s, 2 * D_GROUP), F32),
                   jax.ShapeDtypeStruct((s, LANES), F32), jax.ShapeDtypeStruct((1, D_GROUP), F32),
                   jax.ShapeDtypeStruct((1, 2 * D_GROUP), F32)],
        input_output_aliases={8: 0},
        scratch_shapes=[pltpu.VMEM((N_HEADS, D_HEAD, D_HEAD), F32), pltpu.VMEM((N_HEADS, 1, D_HEAD), F32)],
        compiler_params=_params(("arbitrary",)),
    )(qk, proj, gates, norm_w, ct_s, n_s, m_s, d_y, d_proj)


LN_TOKENS = 512
ATT_TOKENS = 256


def _res_ln_fwd(xres, branch, g, b, name):
    s, dm = xres.shape
    tb = min(LN_TOKENS, s)

    def body(x_ref, br_ref, g_ref, b_ref, o_ref):
        o_ref[...] = _layer_norm(ALPHA * x_ref[...] + br_ref[...], g_ref[...], b_ref[...])

    tok = pl.BlockSpec((tb, dm), lambda i: (i, 0))
    vec = pl.BlockSpec((1, dm), lambda i: (0, 0))
    return pl.pallas_call(
        body, name=name, grid=(s // tb,), in_specs=[tok, tok, vec, vec], out_specs=tok,
        out_shape=jax.ShapeDtypeStruct((s, dm), F32), compiler_params=_params(("parallel",)),
    )(xres, branch, g, b)


def _res_ln_bwd(xres, branch, g, b, d_out, name):
    s, dm = xres.shape
    tb = min(LN_TOKENS, s)

    def body(x_ref, br_ref, g_ref, b_ref, do_ref, dz_ref, dg_ref, db_ref):
        @pl.when(pl.program_id(0) == 0)
        def _():
            dg_ref[...] = jnp.zeros_like(dg_ref)
            db_ref[...] = jnp.zeros_like(db_ref)

        z = ALPHA * x_ref[...] + br_ref[...]
        _, vjp = jax.vjp(_layer_norm, z, g_ref[...], b_ref[...])
        d_z, d_g, d_b = vjp(do_ref[...])
        dz_ref[...] = d_z
        dg_ref[...] += d_g
        db_ref[...] += d_b

    tok = pl.BlockSpec((tb, dm), lambda i: (i, 0))
    vec = pl.BlockSpec((1, dm), lambda i: (0, 0))
    return pl.pallas_call(
        body, name=name, grid=(s // tb,), in_specs=[tok, tok, vec, vec, tok], out_specs=[tok, vec, vec],
        out_shape=[jax.ShapeDtypeStruct((s, dm), F32), jax.ShapeDtypeStruct((1, dm), F32),
                   jax.ShapeDtypeStruct((1, dm), F32)],
        compiler_params=_params(("arbitrary",)),
    )(xres, branch, g, b, d_out)


def _loss_tail(xres, branch, g, b, target):
    s, dm = xres.shape
    tb = min(LN_TOKENS, s)

    def loss_fn(z, gg, bb, tgt):
        err = jnp.square(_layer_norm(z, gg, bb) - tgt)
        return 0.5 * jnp.sum(jnp.mean(err, axis=-1, keepdims=True), axis=0, keepdims=True)

    def body(x_ref, br_ref, g_ref, b_ref, t_ref, loss_ref, dz_ref, dg_ref, db_ref):
        @pl.when(pl.program_id(0) == 0)
        def _():
            loss_ref[...] = jnp.zeros_like(loss_ref)
            dg_ref[...] = jnp.zeros_like(dg_ref)
            db_ref[...] = jnp.zeros_like(db_ref)

        z = ALPHA * x_ref[...] + br_ref[...]
        tgt = t_ref[...]
        loss, vjp = jax.vjp(lambda zz, gg, bb: loss_fn(zz, gg, bb, tgt), z, g_ref[...], b_ref[...])
        d_z, d_g, d_b = vjp(jnp.ones((1, 1), F32))
        loss_ref[...] += loss
        dz_ref[...] = d_z
        dg_ref[...] += d_g
        db_ref[...] += d_b

    tok = pl.BlockSpec((tb, dm), lambda i: (i, 0))
    vec = pl.BlockSpec((1, dm), lambda i: (0, 0))
    one = pl.BlockSpec((1, 1), lambda i: (0, 0))
    return pl.pallas_call(
        body, name="loss_tail", grid=(s // tb,), in_specs=[tok, tok, vec, vec, tok],
        out_specs=[one, tok, vec, vec],
        out_shape=[jax.ShapeDtypeStruct((1, 1), F32), jax.ShapeDtypeStruct((s, dm), F32),
                   jax.ShapeDtypeStruct((1, dm), F32), jax.ShapeDtypeStruct((1, dm), F32)],
        compiler_params=_params(("arbitrary",)),
    )(xres, branch, g, b, target)


def _att_head(q, k, v):
    sc = _nt(q, k) * (CA_DH ** -0.5)
    return _nn(jax.nn.softmax(sc, axis=-1), v)


def _att_fwd(q, kv):
    s = q.shape[0]
    tb = min(ATT_TOKENS, s)

    def body(q_ref, kv_ref, o_ref):
        for h in range(CA_HEADS):
            lo = h * CA_DH
            o_ref[:, lo:lo + CA_DH] = _att_head(q_ref[:, lo:lo + CA_DH], kv_ref[:, lo:lo + CA_DH],
                                                kv_ref[:, D_MODEL + lo:D_MODEL + lo + CA_DH]).astype(o_ref.dtype)

    tok = pl.BlockSpec((tb, D_MODEL), lambda i: (i, 0))
    return pl.pallas_call(
        body, name="att_fwd", grid=(s // tb,),
        in_specs=[tok, pl.BlockSpec((N_MEM, 2 * D_MODEL), lambda i: (0, 0))], out_specs=tok,
        out_shape=jax.ShapeDtypeStruct((s, D_MODEL), BF16), compiler_params=_params(("parallel",)),
    )(q, kv)


def _att_bwd(q, kv, d_o):
    s = q.shape[0]
    tb = min(ATT_TOKENS, s)

    def body(q_ref, kv_ref, do_ref, dq_ref, dkv_ref):
        @pl.when(pl.program_id(0) == 0)
        def _():
            dkv_ref[...] = jnp.zeros_like(dkv_ref)

        for h in range(CA_HEADS):
            lo = h * CA_DH
            vlo = D_MODEL + lo
            _, vjp = jax.vjp(_att_head, q_ref[:, lo:lo + CA_DH], kv_ref[:, lo:lo + CA_DH],
                             kv_ref[:, vlo:vlo + CA_DH])
            d_q, d_k, d_v = vjp(do_ref[:, lo:lo + CA_DH])
            dq_ref[:, lo:lo + CA_DH] = d_q
            dkv_ref[:, lo:lo + CA_DH] += d_k
            dkv_ref[:, vlo:vlo + CA_DH] += d_v

    tok = pl.BlockSpec((tb, D_MODEL), lambda i: (i, 0))
    mem = pl.BlockSpec((N_MEM, 2 * D_MODEL), lambda i: (0, 0))
    return pl.pallas_call(
        body, name="att_bwd", grid=(s // tb,), in_specs=[tok, mem, tok], out_specs=[tok, mem],
        out_shape=[jax.ShapeDtypeStruct((s, D_MODEL), F32), jax.ShapeDtypeStruct((N_MEM, 2 * D_MODEL), F32)],
        compiler_params=_params(("arbitrary",)),
    )(q, kv, d_o)


def _local_step(x, mem, target, w, late_weights=None, on_ffn_grads=None, on_mid_grads=None):
    w = dict(w)
    s = x.shape[0]
    tm = min(512, s)
    tt = min(512, s)
    proj = _matmul_nn(x, w["w_in_main"], w["b_in_main"], min(2048, s), 512, "proj")
    gates = _matmul_nn(x, w["w_in_gate"], w["b_in_gate"], tm, LANES, "proj_gates")
    qk = _ml_conv_fwd(proj, w["ml_conv_w"], w["ml_conv_b"])
    y, hg_states = _hgrn2_fwd(proj, w["hg_lb_logits"], w["hg_norm_w"])
    y, ct_s, n_s, m_s = _mlstm_fwd(qk, proj, gates, w["ml_norm_w"], y)
    if late_weights is not None:
        w.update(late_weights(y))
    mix =_matmul_nn(y, w["w_out"], None, tm, D_MODEL, "mix")
    x1 = _res_ln_fwd(x, mix, w["ln1_g"], w["ln1_b"], "ln1_fwd")
    kv = _matmul_nn(mem, w["ca_wkv"], None, N_MEM, CA_DH, "kv")
    q = _matmul_nn(x1, w["ca_wq"], None, tm, D_MODEL, "ca_q")
    att = _att_fwd(q, kv)
    ca = _matmul_nn(att, w["ca_wo"], None, tm, D_MODEL, "ca_out")
    x2 = _res_ln_fwd(x1, ca, w["ln2_g"], w["ln2_b"], "ln2_fwd")
    u = _matmul_nn(x2, w["ffn_w_up"], None, min(2048, s), UP_SHARD_P, "ffn_up")
    hid = _ffn_conv_fwd(u, w["ffn_conv_w"], w["ffn_conv_b"])
    ff = _matmul_nn(hid, w["ffn_w_down"], None, tm, D_MODEL, "ffn_down")
    loss, d_z3, d_ln3_g, d_ln3_b = _loss_tail(x2, ff, w["ln3_g"], w["ln3_b"], target)
    grads = {"ln3_g": d_ln3_g, "ln3_b": d_ln3_b}
    grads["ffn_w_down"] = _matmul_tn(hid, d_z3, 1536, D_MODEL, tt, "d_w_down")
    d_hid = _matmul_nt([(d_z3, w["ffn_w_down"])], None, 1.0, tm, D_FF_P, "d_hid")
    d_ug, d_uv, d_cwg, d_cwv, d_cbg, d_cbv = _ffn_conv_bwd(u, w["ffn_conv_w"], w["ffn_conv_b"], d_hid)
    grads["ffn_conv_w"] = jnp.concatenate([d_cwg, d_cwv], axis=-1)
    grads["ffn_conv_b"] = jnp.concatenate([d_cbg, d_cbv], axis=-1)
    half = N_DEV // 2
    d_w_up = _matmul_tn(x2, d_ug, D_MODEL, UP_SHARD_P, tt, "d_w_up_gate", shards=N_DEV, group=half)
    grads["ffn_w_up"] = _matmul_tn(x2, d_uv, D_MODEL, UP_SHARD_P, tt, "d_w_up_val", shards=N_DEV,
                                   shard0=half, group=half, into=d_w_up)
    d_x2 = _matmul_nt([(d_ug, w["ffn_w_up"], 0), (d_uv, w["ffn_w_up"], N_DEV // 2)], d_z3, ALPHA,
                      min(256, s), D_MODEL, "d_x2")
    if on_ffn_grads is not None:
        d_x2 = on_ffn_grads(grads, d_x2)
    d_z2, grads["ln2_g"], grads["ln2_b"] = _res_ln_bwd(x1, ca, w["ln2_g"], w["ln2_b"], d_x2, "ln2_bwd")
    grads["ca_wo"] = _matmul_tn(att, d_z2, D_MODEL, D_MODEL, tt, "d_ca_wo")
    d_att = _matmul_nt([(d_z2, w["ca_wo"])], None, 1.0, tm, D_MODEL, "d_att")
    d_q, d_kv = _att_bwd(q, kv, d_att)
    grads["ca_wq"] = _matmul_tn(x1, d_q, D_MODEL, D_MODEL, tt, "d_ca_wq")
    grads["ca_wkv"] = _matmul_tn(mem, d_kv, D_MODEL, CA_DH, N_MEM, "d_ca_wkv", shards=N_DEV, group=N_DEV)
    d_x1 = _matmul_nt([(d_q, w["ca_wq"])], d_z2, ALPHA, tm, D_MODEL, "d_x1")
    d_z1, grads["ln1_g"], grads["ln1_b"] = _res_ln_bwd(x, mix, w["ln1_g"], w["ln1_b"], d_x1, "ln1_bwd")
    grads["w_out"] = _matmul_tn(y, d_z1, D_MODEL, D_MODEL, tt, "d_w_out")
    if on_mid_grads is not None:
        d_z1 = on_mid_grads(grads, d_z1)
    d_y = _matmul_nt([(d_z1, w["w_out"])], None, 1.0, tm, D_MODEL, "d_y")
    d_proj, grads["hg_lb_logits"], grads["hg_norm_w"], db_hg = _hgrn2_bwd(
        proj, w["hg_lb_logits"], w["hg_norm_w"], hg_states, d_y)
    d_proj, d_qk, d_gates, grads["ml_norm_w"], db_vo = _mlstm_bwd(
        qk, proj, gates, w["ml_norm_w"], ct_s, n_s, m_s, d_y, d_proj)
    d_proj, grads["ml_conv_w"], grads["ml_conv_b"], db_qk = _ml_conv_bwd(
        proj, w["ml_conv_w"], w["ml_conv_b"], d_qk, d_proj)
    grads["b_in_main"] = jnp.concatenate([db_hg, db_qk, db_vo], axis=-1)
    grads["w_in_main"] = _matmul_tn(x, d_proj, D_MODEL, min(2048, D_IN_MAIN), tt, "d_w_in")
    grads["w_in_gate"], grads["b_in_gate"] = _matmul_tn(x, d_gates, D_MODEL, LANES, tt, "d_w_in_gates", colsum=True)
    grad_x = _matmul_nt([(d_proj, w["w_in_main"]), (d_gates, w["w_in_gate"])], d_z1, ALPHA, tm, D_MODEL, "d_x")
    return loss, grad_x, grads


HBM_SPEC = pl.BlockSpec(memory_space=pltpu.HBM)


def _coords():
    return lax.axis_index("x"), lax.axis_index("y"), lax.axis_index("c")


def _other_chips(x, y):
    return [(1 - x, y), (x, 1 - y), (1 - x, 1 - y)]


def _all_gather_two_level(shards, name):
    na = len(shards)

    def body(*refs):
        x_refs, out_refs = refs[:na], refs[na:2 * na]
        send_sems, recv_sems, local_sems = refs[2 * na:]
        x, y, c = _coords()
        me, sibling = (x, y, c), (x, y, 1 - c)
        chips = _other_chips(x, y)

        def copy(a, k, block, to, own=False):
            slot = out_refs[a].at[4 * block[0] + 2 * block[1] + block[2]]
            return pltpu.make_async_remote_copy(
                src_ref=x_refs[a] if own else slot, dst_ref=slot,
                send_sem=send_sems.at[7 * a + k], recv_sem=recv_sems.at[7 * a + k],
                device_id=to, device_id_type=MESH)

        mine = [pltpu.make_async_copy(x_refs[a], out_refs[a].at[4 * x + 2 * y + c], local_sems.at[a])
                for a in range(na)]
        for cp in mine:
            cp.start()
        first = []
        for a in range(na):
            first.append(copy(a, 0, me, sibling, own=True))
            first += [copy(a, 1 + j, me, (*chip, c), own=True) for j, chip in enumerate(chips)]
        for cp in first:
            cp.start()
        passed = []
        for j, chip in enumerate(chips):
            for a in range(na):
                copy(a, 1 + j, (*chip, c), me).wait_recv()
                fwd = copy(a, 4 + j, (*chip, c), sibling)
                fwd.start()
                passed.append(fwd)
        for a in range(na):
            copy(a, 0, sibling, me).wait_recv()
            for j, chip in enumerate(chips):
                copy(a, 4 + j, (*chip, 1 - c), me).wait_recv()
        for cp in first + passed:
            cp.wait_send()
        for cp in mine:
            cp.wait()

    return pl.pallas_call(
        body, name=name,
        out_shape=[jax.ShapeDtypeStruct((N_DEV,) + t.shape, t.dtype) for t in shards],
        in_specs=[HBM_SPEC] * na, out_specs=[HBM_SPEC] * na,
        scratch_shapes=[pltpu.SemaphoreType.DMA((7 * na,)), pltpu.SemaphoreType.DMA((7 * na,)),
                        pltpu.SemaphoreType.DMA((na,))],
    )(*shards)


def _all_gather_direct(vec, name):
    r, n = vec.shape

    def body(x_ref, out_ref, send_sems, recv_sems, local_sem):
        x, y, c = _coords()
        flip = lambda v, bit: 1 - v if bit else v
        me = 4 * x + 2 * y + c
        mine = pltpu.make_async_copy(x_ref, out_ref.at[me], local_sem)
        mine.start()
        copies = []
        for d in range(1, N_DEV):
            peer = (flip(x, d & 4), flip(y, d & 2), flip(c, d & 1))
            peer_slot = 4 * peer[0] + 2 * peer[1] + peer[2]
            out_going = pltpu.make_async_remote_copy(
                src_ref=x_ref, dst_ref=out_ref.at[me], send_sem=send_sems.at[d - 1],
                recv_sem=recv_sems.at[d - 1], device_id=peer, device_id_type=MESH)
            incoming = pltpu.make_async_remote_copy(
                src_ref=x_ref, dst_ref=out_ref.at[peer_slot], send_sem=send_sems.at[d - 1],
                recv_sem=recv_sems.at[d - 1], device_id=peer, device_id_type=MESH)
            out_going.start()
            copies.append((out_going, incoming))
        for out_going, incoming in copies:
            incoming.wait_recv()
            out_going.wait_send()
        mine.wait()

    return pl.pallas_call(
        body, name=name, out_shape=jax.ShapeDtypeStruct((N_DEV, r, n), vec.dtype),
        in_specs=[HBM_SPEC], out_specs=HBM_SPEC,
        scratch_shapes=[pltpu.SemaphoreType.DMA((7,)), pltpu.SemaphoreType.DMA((7,)), pltpu.SemaphoreType.DMA],
    )(vec)


def _exchange_with_sibling(parts):
    na = len(parts)

    def body(*refs):
        p_refs, got_refs = refs[:na], refs[na:2 * na]
        send_sems, recv_sems = refs[2 * na:]
        x, y, c = _coords()
        copies = []
        for a in range(na):
            for chip in range(N_CHIPS):
                cp = pltpu.make_async_remote_copy(
                    src_ref=p_refs[a].at[2 * chip + (1 - c)], dst_ref=got_refs[a].at[chip],
                    send_sem=send_sems.at[N_CHIPS * a + chip], recv_sem=recv_sems.at[N_CHIPS * a + chip],
                    device_id=(x, y, 1 - c), device_id_type=MESH)
                cp.start()
                copies.append(cp)
        for cp in copies:
            cp.wait()

    return pl.pallas_call(
        body, name="grad_exchange_sibling",
        out_shape=[jax.ShapeDtypeStruct((N_CHIPS,) + t.shape[1:], t.dtype) for t in parts],
        in_specs=[HBM_SPEC] * na, out_specs=[HBM_SPEC] * na,
        scratch_shapes=[pltpu.SemaphoreType.DMA((N_CHIPS * na,)), pltpu.SemaphoreType.DMA((N_CHIPS * na,))],
    )(*parts)


def _exchange_with_chips(sums):
    na = len(sums)

    def body(*refs):
        a_refs, out_refs = refs[:na], refs[na:2 * na]
        send_sems, recv_sems = refs[2 * na:]
        x, y, c = _coords()
        copies = []
        for a in range(na):
            for j, (cx, cy) in enumerate(_other_chips(x, y)):
                cp = pltpu.make_async_remote_copy(
                    src_ref=a_refs[a].at[2 * cx + cy], dst_ref=out_refs[a].at[j],
                    send_sem=send_sems.at[3 * a + j], recv_sem=recv_sems.at[3 * a + j],
                    device_id=(cx, cy, c), device_id_type=MESH)
                cp.start()
                copies.append(cp)
        for cp in copies:
            cp.wait()

    return pl.pallas_call(
        body, name="grad_exchange_chips",
        out_shape=[jax.ShapeDtypeStruct((3,) + t.shape[1:], t.dtype) for t in sums],
        in_specs=[HBM_SPEC] * na, out_specs=[HBM_SPEC] * na,
        scratch_shapes=[pltpu.SemaphoreType.DMA((3 * na,)), pltpu.SemaphoreType.DMA((3 * na,))],
    )(*sums)


SEM_SPEC = pl.BlockSpec(memory_space=pltpu.SEMAPHORE)
ANY_SPEC = pl.BlockSpec(memory_space=pl.ANY)
SIDE_EFFECT = pltpu.SideEffectType.DATAFLOW_SIDE_EFFECTING


def _peer(x, y, c, d):
    flip = lambda v, bit: 1 - v if bit else v
    p = (flip(x, d & 4), flip(y, d & 2), flip(c, d & 1))
    return p, 4 * p[0] + 2 * p[1] + p[2]


def _direct_copies(gather, src_refs, land_refs, send_sems, recv_sems):
    x, y, c = _coords()
    me = 4 * x + 2 * y + c
    copies = []
    for a in range(len(src_refs)):
        for d in range(1, N_DEV):
            peer, peer_slot = _peer(x, y, c, d)
            copies.append(pltpu.make_async_remote_copy(
                src_ref=src_refs[a] if gather else src_refs[a].at[peer_slot],
                dst_ref=land_refs[a].at[me] if gather else land_refs[a].at[d - 1],
                send_sem=send_sems.at[7 * a + d - 1], recv_sem=recv_sems.at[7 * a + d - 1],
                device_id=peer, device_id_type=MESH))
    return copies


def _hbm(t):
    return pltpu.HBM(t.shape, t.dtype)


def _direct_start(gather, arrays, through, name):
    na = len(arrays)
    lands = [lax.empty((N_DEV,) + t.shape if gather else (N_DEV - 1,) + t.shape[1:], t.dtype) for t in arrays]
    n_io = 2 * na + 1

    def body(*refs):
        for cp in _direct_copies(gather, refs[:na], refs[na:2 * na], refs[n_io], refs[n_io + 1]):
            cp.start()

    ins = [pltpu.with_memory_space_constraint(t, pltpu.HBM) for t in (*arrays, *lands, through)]
    sems = pltpu.SemaphoreType.DMA((7 * na,))
    res = pl.pallas_call(
        body, name=name, out_shape=(sems, sems, *[_hbm(t) for t in ins]),
        in_specs=[HBM_SPEC] * n_io, out_specs=(SEM_SPEC, SEM_SPEC, *[HBM_SPEC] * n_io),
        input_output_aliases={i: 2 + i for i in range(n_io)},
        compiler_params=pltpu.CompilerParams(has_side_effects=SIDE_EFFECT),
    )(*ins)
    return (res[0], res[1], list(res[2:2 + na]), list(res[2 + na:2 + 2 * na])), res[2 + 2 * na]


def _direct_wait(gather, started, after, name):
    send_sems, recv_sems, arrays, lands = started
    na = len(arrays)

    def body(*refs):
        for cp in _direct_copies(gather, refs[:na], refs[na:2 * na], refs[2 * na], refs[2 * na + 1]):
            cp.wait_send()
            cp.wait_recv()

    res = pl.pallas_call(
        body, name=name, out_shape=tuple(_hbm(t) for t in (*arrays, *lands)),
        in_specs=[HBM_SPEC] * (2 * na) + [SEM_SPEC, SEM_SPEC, ANY_SPEC], out_specs=tuple([HBM_SPEC] * (2 * na)),
        input_output_aliases={i: i for i in range(2 * na)},
        compiler_params=pltpu.CompilerParams(has_side_effects=SIDE_EFFECT),
    )(*arrays, *lands, send_sems, recv_sems, after)
    return list(res[:na]), list(res[na:])


def _row_tile(rows):
    for t in (256, 176, 128):
        if rows % t == 0 and rows > t:
            return t
    return rows


def _add_sibling(core, parts, got, name):
    _, r, c = parts.shape
    tr = _row_tile(r)

    def body(core_ref, p_ref, g_ref, o_ref):
        o_ref[...] = p_ref[...] + g_ref[...]

    return pl.pallas_call(
        body, name=name,
        grid_spec=pltpu.PrefetchScalarGridSpec(
            num_scalar_prefetch=1, grid=(N_CHIPS, r // tr),
            in_specs=[pl.BlockSpec((None, tr, c), lambda i, j, core_ref: (2 * i + core_ref[0], j, 0)),
                      pl.BlockSpec((None, tr, c), lambda i, j, core_ref: (i, j, 0))],
            out_specs=pl.BlockSpec((None, tr, c), lambda i, j, core_ref: (i, j, 0))),
        out_shape=jax.ShapeDtypeStruct((N_CHIPS, r, c), F32),
        compiler_params=_params(("parallel", "parallel")),
    )(core, parts, got)


def _adamw_math(g, w, m, v):
    m_new = ADAM_B1 * m + (1.0 - ADAM_B1) * g
    v_new = ADAM_B2 * v + (1.0 - ADAM_B2) * jnp.square(g)
    m_hat = m_new / (1.0 - ADAM_B1 ** ADAM_STEP)
    v_hat = v_new / (1.0 - ADAM_B2 ** ADAM_STEP)
    delta = -ADAM_LR * (m_hat / (jnp.sqrt(v_hat) + ADAM_EPS) + ADAM_WD * w)
    return delta, m_new, v_new


def _adamw_sharded(chip, sums, got, w, m, v, name):
    r, c = w.shape
    tr = _row_tile(r)
    n_got = got.shape[0]

    def body(chip_ref, s_ref, g_ref, w_ref, m_ref, v_ref, go_ref, d_ref, nm_ref, nv_ref):
        g = s_ref[...]
        for i in range(n_got):
            g = g + g_ref[i]
        delta, m_new, v_new = _adamw_math(g, w_ref[...], m_ref[...], v_ref[...])
        go_ref[...] = g
        d_ref[...] = delta
        nm_ref[...] = m_new
        nv_ref[...] = v_new

    blk = pl.BlockSpec((tr, c), lambda i, chip_ref: (i, 0))
    out = jax.ShapeDtypeStruct((r, c), F32)
    return pl.pallas_call(
        body, name=name,
        grid_spec=pltpu.PrefetchScalarGridSpec(
            num_scalar_prefetch=1, grid=(r // tr,),
            in_specs=[pl.BlockSpec((None, tr, c), lambda i, chip_ref: (chip_ref[0], i, 0)),
                      pl.BlockSpec((n_got, tr, c), lambda i, chip_ref: (0, i, 0)), blk, blk, blk],
            out_specs=[blk, blk, blk, blk]),
        out_shape=[out, out, out, out],
        compiler_params=_params(("parallel",)),
    )(chip, sums, got, w, m, v)


def _adamw_replicated(parts, w, m, v):
    p, r, c = parts.shape

    def body(p_ref, w_ref, m_ref, v_ref, g_ref, d_ref, nm_ref, nv_ref):
        g = p_ref[0]
        for i in range(1, p):
            g = g + p_ref[i]
        delta, m_new, v_new = _adamw_math(g, w_ref[...], m_ref[...], v_ref[...])
        g_ref[...] = g
        d_ref[...] = delta
        nm_ref[...] = m_new
        nv_ref[...] = v_new

    blk = pl.BlockSpec((r, c), lambda i: (0, 0))
    out = jax.ShapeDtypeStruct((r, c), F32)
    return pl.pallas_call(
        body, name="adamw_replicated", grid=(1,),
        in_specs=[pl.BlockSpec((p, r, c), lambda i: (0, 0, 0)), blk, blk, blk],
        out_specs=[blk, blk, blk, blk], out_shape=[out, out, out, out],
        compiler_params=_params(("arbitrary",)),
    )(parts, w, m, v)


SHARDED_NAMES = ("w_in", "ml_conv_w", "w_out", "ca_wq", "ca_wkv", "ca_wo", "ffn_w_up", "ffn_conv_w", "ffn_w_down")
SMALL_NAMES = ("b_in", "hg_lb_logits", "hg_norm_w", "ml_conv_b", "ml_norm_w", "ln1_g", "ln1_b",
               "ln2_g", "ln2_b", "ffn_conv_b", "ln3_g", "ln3_b")
WEIGHT_NAMES = ("w_in", "b_in", "hg_lb_logits", "hg_norm_w", "ml_conv_w", "ml_conv_b", "ml_norm_w", "w_out",
                "ln1_g", "ln1_b", "ca_wq", "ca_wkv", "ca_wo", "ln2_g", "ln2_b", "ffn_w_up", "ffn_conv_w",
                "ffn_conv_b", "ffn_w_down", "ln3_g", "ln3_b")
PAD_TO = {"w_in": W_IN_SHARD_P, "ffn_w_up": UP_SHARD_P, "ffn_conv_w": UP_SHARD_P}
SMALL_ROWS = 24
SMALL_W = D_MODEL


def _shard_2d(name, block):
    t = block[0]
    if name in PAD_TO:
        t = jnp.pad(t, ((0, 0), (0, PAD_TO[name] - t.shape[1])))
    return t


def _shard_like(name, t, like):
    return t[:, :like.shape[2]][None]


def _pad_cols(t, width):
    return jnp.pad(t, ((0, 0), (0, width - t.shape[1])))


FIRST_NAMES = ("w_in", "ml_conv_w")
LATE_NAMES = ("w_out", "ca_wq", "ca_wkv", "ca_wo", "ffn_w_up", "ffn_conv_w", "ffn_w_down")
FFN_NAMES = ("ffn_w_up", "ffn_w_down", "ffn_conv_w")
MID_NAMES = ("ca_wo", "ca_wq", "ca_wkv", "w_out")


def _first_weights(g, small):
    w = dict(small)
    w_in = jnp.concatenate([g["w_in"][j, :, :W_IN_SHARD] for j in range(N_DEV)], axis=1)
    w["w_in_main"] = w_in[:, :D_IN_MAIN]
    w["w_in_gate"] = _pad_cols(w_in[:, D_IN_MAIN:], LANES)
    w["b_in_main"] = small["b_in"][:, :D_IN_MAIN]
    w["b_in_gate"] = _pad_cols(small["b_in"][:, D_IN_MAIN:], LANES)
    w["ml_conv_w"] = jnp.transpose(g["ml_conv_w"], (1, 0, 2)).reshape(ML_CONV, 2 * D_GROUP)
    return w


def _late_weights(g, small):
    w = {}
    for n in ("w_out", "ca_wq", "ca_wo"):
        w[n] = g[n].reshape(D_MODEL, D_MODEL)
    w["ca_wkv"] = g["ca_wkv"]
    w["ffn_w_up"] = g["ffn_w_up"]
    down = g["ffn_w_down"].reshape(N_DEV // 2, UP_SHARD, D_MODEL)
    w["ffn_w_down"] = jnp.pad(down, ((0, 0), (0, UP_SHARD_P - UP_SHARD), (0, 0))).reshape(D_FF_P, D_MODEL)
    w["ffn_conv_w"] = jnp.transpose(g["ffn_conv_w"], (1, 0, 2)).reshape(FFN_CONV, D_UP_P)
    w["ffn_conv_b"] = _pad_cols(small["ffn_conv_b"].reshape(N_DEV, UP_SHARD), UP_SHARD_P).reshape(1, D_UP_P)
    return w


def _whole_weights(g, small):
    return {**_first_weights(g, small), **_late_weights(g, small)}


def _owner_stack(n, grads):
    if n == "w_in":
        w_in = jnp.concatenate([grads["w_in_main"], grads["w_in_gate"][:, :D_IN - D_IN_MAIN]], axis=1)
        return jnp.stack([_pad_cols(w_in[:, j * W_IN_SHARD:(j + 1) * W_IN_SHARD], W_IN_SHARD_P)
                          for j in range(N_DEV)])
    if n in ("w_out", "ca_wq", "ca_wo"):
        return grads[n].reshape(N_DEV, D_MODEL // N_DEV, D_MODEL)
    if n == "ffn_w_down":
        down = grads[n].reshape(N_DEV // 2, UP_SHARD_P, D_MODEL)[:, :UP_SHARD]
        return down.reshape(N_DEV, D_FF // N_DEV, D_MODEL)
    if n == "ml_conv_w":
        return jnp.transpose(grads[n].reshape(ML_CONV, N_DEV, LANES), (1, 0, 2))
    if n == "ffn_conv_w":
        return jnp.transpose(grads[n].reshape(FFN_CONV, N_DEV, UP_SHARD_P), (1, 0, 2))
    return grads[n]


def _owner_stacks(grads):
    return {n: _owner_stack(n, grads) for n in SHARDED_NAMES}


def _small_grads(grads):
    out = {n: grads[n] for n in SMALL_NAMES if n in grads}
    out["b_in"] = jnp.concatenate([grads["b_in_main"], grads["b_in_gate"][:, :D_IN - D_IN_MAIN]], axis=1)
    out["ffn_conv_b"] = grads["ffn_conv_b"].reshape(N_DEV, UP_SHARD_P)[:, :UP_SHARD].reshape(1, D_UP)
    return out


def _pack_small(p, extra=None):
    flat = [p[n].reshape(-1) for n in SMALL_NAMES]
    if extra is not None:
        flat.append(extra.reshape(-1))
    flat = jnp.concatenate(flat)
    return jnp.pad(flat, (0, SMALL_ROWS * SMALL_W - flat.shape[0])).reshape(SMALL_ROWS, SMALL_W)


def _unpack_small(slab, like):
    out = {}
    flat = slab.reshape(-1)
    o = 0
    for n in SMALL_NAMES:
        out[n] = flat[o:o + like[n].size].reshape(like[n].shape)
        o += like[n].size
    return out, flat[o]


def kernel(x, mem, w_in, b_in, hg_lb_logits, hg_norm_w, ml_conv_w, ml_conv_b, ml_norm_w, w_out, ln1_g, ln1_b, ca_wq, ca_wkv, ca_wo, ln2_g, ln2_b, ffn_w_up, ffn_conv_w, ffn_conv_b, ffn_w_down, ln3_g, ln3_b, loss_target, m_w_in, m_b_in, m_hg_lb_logits, m_hg_norm_w, m_ml_conv_w, m_ml_conv_b, m_ml_norm_w, m_w_out, m_ln1_g, m_ln1_b, m_ca_wq, m_ca_wkv, m_ca_wo, m_ln2_g, m_ln2_b, m_ffn_w_up, m_ffn_conv_w, m_ffn_conv_b, m_ffn_w_down, m_ln3_g, m_ln3_b, v_w_in, v_b_in, v_hg_lb_logits, v_hg_norm_w, v_ml_conv_w, v_ml_conv_b, v_ml_norm_w, v_w_out, v_ln1_g, v_ln1_b, v_ca_wq, v_ca_wkv, v_ca_wo, v_ln2_g, v_ln2_b, v_ffn_w_up, v_ffn_conv_w, v_ffn_conv_b, v_ffn_w_down, v_ln3_g, v_ln3_b):
    params = dict(w_in=w_in, b_in=b_in, hg_lb_logits=hg_lb_logits, hg_norm_w=hg_norm_w, ml_conv_w=ml_conv_w,
                  ml_conv_b=ml_conv_b, ml_norm_w=ml_norm_w, w_out=w_out, ln1_g=ln1_g, ln1_b=ln1_b, ca_wq=ca_wq,
                  ca_wkv=ca_wkv, ca_wo=ca_wo, ln2_g=ln2_g, ln2_b=ln2_b, ffn_w_up=ffn_w_up, ffn_conv_w=ffn_conv_w,
                  ffn_conv_b=ffn_conv_b, ffn_w_down=ffn_w_down, ln3_g=ln3_g, ln3_b=ln3_b)
    mom1 = dict(w_in=m_w_in, b_in=m_b_in, hg_lb_logits=m_hg_lb_logits, hg_norm_w=m_hg_norm_w,
                ml_conv_w=m_ml_conv_w, ml_conv_b=m_ml_conv_b, ml_norm_w=m_ml_norm_w, w_out=m_w_out, ln1_g=m_ln1_g,
                ln1_b=m_ln1_b, ca_wq=m_ca_wq, ca_wkv=m_ca_wkv, ca_wo=m_ca_wo, ln2_g=m_ln2_g, ln2_b=m_ln2_b,
                ffn_w_up=m_ffn_w_up, ffn_conv_w=m_ffn_conv_w, ffn_conv_b=m_ffn_conv_b, ffn_w_down=m_ffn_w_down,
                ln3_g=m_ln3_g, ln3_b=m_ln3_b)
    mom2 = dict(w_in=v_w_in, b_in=v_b_in, hg_lb_logits=v_hg_lb_logits, hg_norm_w=v_hg_norm_w,
                ml_conv_w=v_ml_conv_w, ml_conv_b=v_ml_conv_b, ml_norm_w=v_ml_norm_w, w_out=v_w_out, ln1_g=v_ln1_g,
                ln1_b=v_ln1_b, ca_wq=v_ca_wq, ca_wkv=v_ca_wkv, ca_wo=v_ca_wo, ln2_g=v_ln2_g, ln2_b=v_ln2_b,
                ffn_w_up=v_ffn_w_up, ffn_conv_w=v_ffn_conv_w, ffn_conv_b=v_ffn_conv_b, ffn_w_down=v_ffn_w_down,
                ln3_g=v_ln3_g, ln3_b=v_ln3_b)

    x_idx, y_idx, c_idx = _coords()
    as_index = lambda v: jnp.reshape(v, (1,)).astype(jnp.int32)
    core, chip, me = as_index(c_idx), as_index(2 * x_idx + y_idx), as_index(4 * x_idx + 2 * y_idx + c_idx)
    small_params = {n: params[n] for n in SMALL_NAMES}

    shards = {n: _shard_2d(n, params[n]) for n in SHARDED_NAMES}
    to_send = lambda names: [shards[n] if "conv" in n else shards[n].astype(BF16) for n in names]
    first = dict(zip(FIRST_NAMES, _all_gather_two_level(to_send(FIRST_NAMES), "weights_gather_first")))
    late_started, first["w_in"] = _direct_start(True, to_send(LATE_NAMES), first["w_in"], "weights_gather_start")

    def late_weights(y):
        mine, lands = _direct_wait(True, late_started, y, "weights_gather_wait")
        gathered = {n: lax.dynamic_update_index_in_dim(land, own, me[0], 0)
                    for n, own, land in zip(LATE_NAMES, mine, lands)}
        return _late_weights(gathered, small_params)

    started = {}

    def start_group(names, tag):
        def hook(grads, through):
            started[tag], through = _direct_start(False, [_owner_stack(n, grads) for n in names], through,
                                                  "grads_start_" + tag)
            return through
        return hook

    loss, grad_x, grads = _local_step(x[0], mem[0], loss_target[0], _first_weights(first, small_params),
                                      late_weights, start_group(FFN_NAMES, "ffn"), start_group(MID_NAMES, "mid"))

    sharded_out = {}

    def adamw(n, index, own, got):
        res = _adamw_sharded(index, own, got, shards[n], _shard_2d(n, mom1[n]), _shard_2d(n, mom2[n]), "adamw_" + n)
        sharded_out[n] = [_shard_like(n, t, params[n]) for t in res]

    stacks = [_owner_stack(n, grads) for n in FIRST_NAMES]
    from_sibling = _exchange_with_sibling(stacks)
    chip_sums = [_add_sibling(core, st, got, "grad_add_" + n) for n, st, got in zip(FIRST_NAMES, stacks, from_sibling)]
    from_chips = _exchange_with_chips(chip_sums)
    for n, sums, got in zip(FIRST_NAMES, chip_sums, from_chips):
        adamw(n, chip, sums, got)
    for names, tag in ((FFN_NAMES, "ffn"), (MID_NAMES, "mid")):
        own, lands = _direct_wait(False, started[tag], grad_x, "grads_wait_" + tag)
        for n, st, land in zip(names, own, lands):
            adamw(n, me, st, land)
    small_parts = _all_gather_direct(_pack_small(_small_grads(grads), loss), "small_all_gather")
    small_res = _adamw_replicated(small_parts, _pack_small(params), _pack_small(mom1), _pack_small(mom2))

    outs = []
    total_loss = None
    for k in range(4):
        small, extra = _unpack_small(small_res[k], params)
        if total_loss is None:
            total_loss = extra
        outs.extend(sharded_out[n][k] if n in sharded_out else small[n] for n in WEIGHT_NAMES)
    return (total_loss, grad_x[None], *outs)
```

```python
import jax
import jax.numpy as jnp
from jax import lax
from jax.experimental import pallas as pl
from jax.experimental.pallas import tpu as pltpu

F32 = jnp.float32
BF16 = jnp.bfloat16
HIGHEST = lax.Precision.HIGHEST
MESH = pl.DeviceIdType.MESH

N_DEV = 8
N_CHIPS = 4
D_MODEL = 1024
N_MEM = 256
N_HEADS = 4
D_HEAD = 128
D_GROUP = N_HEADS * D_HEAD
CHUNK = 64
ML_CONV = 4
FFN_CONV = 3
D_FF = 2816
D_UP = 2 * D_FF
CA_HEADS = 4
CA_DH = D_MODEL // CA_HEADS
LANES = 128
SUBLANES = 8
D_IN = 8 * D_GROUP + 2 * N_HEADS
D_IN_MAIN = 8 * D_GROUP
W_IN_SHARD = D_IN // N_DEV
W_IN_SHARD_P = 640
UP_SHARD = D_UP // N_DEV
UP_SHARD_P = 768
D_UP_P = N_DEV * UP_SHARD_P
D_FF_P = D_UP_P // 2
ALPHA = 2.0 ** 0.25
LN_EPS = 1e-5
NEG_BIG = -1e30
ADAM_LR = 0.001
ADAM_B1 = 0.9
ADAM_B2 = 0.999
ADAM_EPS = 1e-08
ADAM_WD = 0.01
ADAM_STEP = 10
VMEM_LIMIT = 56 * 1024 * 1024

SEG_HQ, SEG_HF, SEG_HI, SEG_HG, SEG_MQ, SEG_MK, SEG_MV, SEG_MO = (4 * i for i in range(8))


def _params(sem):
    return pltpu.CompilerParams(dimension_semantics=sem, vmem_limit_bytes=VMEM_LIMIT)


def _dg(a, b, ca, cb, precision=None):
    return lax.dot_general(a, b, (((ca,), (cb,)), ((), ())), precision=precision,
                           preferred_element_type=F32)


def _nn_raw(a, b):
    return _dg(a.astype(BF16), b.astype(BF16), 1, 0)


def _nt_raw(a, b):
    return _dg(a.astype(BF16), b.astype(BF16), 1, 1)


def _tn_raw(a, b):
    return _dg(a.astype(BF16), b.astype(BF16), 0, 0)


@jax.custom_vjp
def _nn(a, b):
    return _nn_raw(a, b)


_nn.defvjp(lambda a, b: (_nn_raw(a, b), (a, b)),
           lambda res, g: (_nt_raw(g, res[1]), _tn_raw(res[0], g)))


@jax.custom_vjp
def _nt(a, b):
    return _nt_raw(a, b)


_nt.defvjp(lambda a, b: (_nt_raw(a, b), (a, b)),
           lambda res, g: (_nn_raw(g, res[1]), _tn_raw(g, res[0])))


@jax.custom_vjp
def _tn(a, b):
    return _tn_raw(a, b)


_tn.defvjp(lambda a, b: (_tn_raw(a, b), (a, b)),
           lambda res, g: (_nt_raw(res[1], g), _nn_raw(res[0], g)))


def _layer_norm(z, g, b):
    mu = jnp.mean(z, axis=-1, keepdims=True)
    var = jnp.mean(jnp.square(z - mu), axis=-1, keepdims=True)
    return (z - mu) * lax.rsqrt(var + LN_EPS) * g + b


def _matmul_nn(a, w, bias, tm, tn, name):
    m, k = a.shape
    if w.ndim == 3:
        n = w.shape[0] * w.shape[2]
        assert tn == w.shape[2]
        w_spec = pl.BlockSpec((None, k, tn), lambda i, j: (j, 0, 0))
    else:
        n = w.shape[1]
        w_spec = pl.BlockSpec((k, tn), lambda i, j: (0, j))

    def body(*refs):
        a_ref, w_ref = refs[0], refs[1]
        o_ref = refs[-1]
        acc = _nn_raw(a_ref[...], w_ref[...])
        if bias is not None:
            acc = acc + refs[2][...]
        o_ref[...] = acc

    in_specs = [pl.BlockSpec((tm, k), lambda i, j: (i, 0)), w_spec]
    args = [a, w]
    if bias is not None:
        in_specs.append(pl.BlockSpec((1, tn), lambda i, j: (0, j)))
        args.append(bias)
    return pl.pallas_call(
        body, name=name, grid=(m // tm, n // tn), in_specs=in_specs,
        out_specs=pl.BlockSpec((tm, tn), lambda i, j: (i, j)),
        out_shape=jax.ShapeDtypeStruct((m, n), F32),
        compiler_params=_params(("parallel", "parallel")),
    )(*args)


def _matmul_nt(pairs, add, scale, tm, tk, name):
    m = pairs[0][0].shape[0]
    k = pairs[0][1].shape[-2]
    groups = []
    in_specs, args = [], []
    for pair in pairs:
        d, w = pair[0], pair[1]
        in_specs.append(pl.BlockSpec((tm, d.shape[1]), lambda i, j: (i, 0)))
        if w.ndim == 3:
            g = d.shape[1] // w.shape[2]
            blk = pair[2] // g
            in_specs.append(pl.BlockSpec((g, tk, w.shape[2]), lambda i, j, blk=blk: (blk, j, 0)))
            groups.append((g, w.shape[2]))
        else:
            in_specs.append(pl.BlockSpec((tk, w.shape[1]), lambda i, j: (j, 0)))
            groups.append(None)
        args += [d, w]
    if add is not None:
        in_specs.append(pl.BlockSpec((tm, tk), lambda i, j: (i, j)))
        args.append(add)

    def body(*refs):
        o_ref = refs[-1]
        acc = None
        for p, grp in enumerate(groups):
            d_ref, w_ref = refs[2 * p], refs[2 * p + 1]
            if grp is None:
                terms = [_nt_raw(d_ref[...], w_ref[...])]
            else:
                terms = [_nt_raw(d_ref[:, g * grp[1]:(g + 1) * grp[1]], w_ref[g]) for g in range(grp[0])]
            for t in terms:
                acc = t if acc is None else acc + t
        if add is not None:
            acc = acc + scale * refs[2 * len(groups)][...]
        o_ref[...] = acc

    return pl.pallas_call(
        body, name=name, grid=(m // tm, k // tk), in_specs=in_specs,
        out_specs=pl.BlockSpec((tm, tk), lambda i, j: (i, j)),
        out_shape=jax.ShapeDtypeStruct((m, k), F32),
        compiler_params=_params(("parallel", "parallel")),
    )(*args)


def _matmul_tn(a, b, tm, tn, tt, name, shards=None, shard0=0, group=1, into=None, colsum=False):
    t, m = a.shape
    n = b.shape[1]
    assert not colsum or tm == m
    n_in = 2 + (into is not None)
    per_step = 1 if shards is None else group
    width = per_step * tn

    def body(*refs):
        a_ref, b_ref = refs[0], refs[1]
        o_ref = refs[n_in]
        first = pl.program_id(2) == 0

        @pl.when(first)
        def _():
            o_ref[...] = jnp.zeros_like(o_ref)

        if shards is None:
            o_ref[...] += _tn_raw(a_ref[...], b_ref[...])
        else:
            lhs = a_ref[...].astype(BF16)
            for g in range(per_step):
                o_ref[g] += _tn_raw(lhs, b_ref[:, g * tn:(g + 1) * tn])
        if colsum:
            s_ref = refs[n_in + 1]

            @pl.when(first)
            def _():
                s_ref[...] = jnp.zeros_like(s_ref)

            s_ref[...] += jnp.sum(b_ref[...], axis=0, keepdims=True)

    in_specs = [pl.BlockSpec((tt, tm), lambda i, j, kk: (kk, i)),
                pl.BlockSpec((tt, width), lambda i, j, kk: (kk, j))]
    args = [a, b]
    aliases = {}
    if into is not None:
        in_specs.append(pl.BlockSpec(memory_space=pl.ANY))
        args.append(into)
        aliases = {2: 0}
    if shards is None:
        out_specs = [pl.BlockSpec((tm, tn), lambda i, j, kk: (i, j))]
        out_shape = [jax.ShapeDtypeStruct((m, n), F32)]
    else:
        out_specs = [pl.BlockSpec((per_step, tm, tn), lambda i, j, kk: (shard0 // per_step + j, i, 0))]
        out_shape = [jax.ShapeDtypeStruct((shards, m, tn), F32)]
    if colsum:
        out_specs.append(pl.BlockSpec((1, tn), lambda i, j, kk: (0, j)))
        out_shape.append(jax.ShapeDtypeStruct((1, n), F32))
    res = pl.pallas_call(
        body, name=name, grid=(m // tm, n // width, t // tt), in_specs=in_specs, out_specs=out_specs,
        out_shape=out_shape, input_output_aliases=aliases,
        compiler_params=_params(("parallel", "parallel", "arbitrary")),
    )(*args)
    return res if colsum else res[0]


ROW_TILE = 512


def _conv_fwd_tile(pad_ref, w_ref, b_ref, r0, rows, taps):
    acc = b_ref[...]
    for j in range(taps):
        acc = acc + pad_ref[pl.ds(SUBLANES - (taps - 1 - j) + r0, rows), :] * w_ref[j:j + 1, :]
    return acc


def _conv_bwd_tile(dpad_ref, w_ref, r0, rows, taps):
    acc = None
    for j in range(taps):
        term = dpad_ref[pl.ds(r0 + (taps - 1 - j), rows), :] * w_ref[j:j + 1, :]
        acc = term if acc is None else acc + term
    return acc


def _conv_grads_tile(pad_ref, dpad_ref, dx_ref, w_ref, dws, r0, rows, taps):
    dx = _conv_bwd_tile(dpad_ref, w_ref, r0, rows, taps)
    dx_ref[r0:r0 + rows, :] = dx.astype(dx_ref.dtype)
    d_pre = dpad_ref[r0:r0 + rows, :]
    for j in range(taps):
        xs = pad_ref[pl.ds(SUBLANES - (taps - 1 - j) + r0, rows), :]
        dws[j] = dws[j] + jnp.sum(d_pre * xs, axis=0, keepdims=True)
    return jnp.sum(dx, axis=0, keepdims=True)


def _ml_conv_fwd(proj, conv_w, conv_b):
    s = proj.shape[0]
    nblk = 2 * D_GROUP // LANES

    def body(x_ref, w_ref, b_ref, o_ref, pad_ref):
        pad_ref[0:SUBLANES, :] = jnp.zeros((SUBLANES, LANES), F32)
        pad_ref[SUBLANES:, :] = x_ref[...]
        for r0 in range(0, s, ROW_TILE):
            rows = min(ROW_TILE, s - r0)
            o_ref[r0:r0 + rows, :] = jax.nn.silu(_conv_fwd_tile(pad_ref, w_ref, b_ref, r0, rows, ML_CONV))

    return pl.pallas_call(
        body, name="ml_conv_fwd", grid=(nblk,),
        in_specs=[pl.BlockSpec((s, LANES), lambda j: (0, SEG_MQ + j)),
                  pl.BlockSpec((ML_CONV, LANES), lambda j: (0, j)),
                  pl.BlockSpec((1, LANES), lambda j: (0, j))],
        out_specs=pl.BlockSpec((s, LANES), lambda j: (0, j)),
        out_shape=jax.ShapeDtypeStruct((s, 2 * D_GROUP), F32),
        scratch_shapes=[pltpu.VMEM((s + SUBLANES, LANES), F32)],
        compiler_params=_params(("parallel",)),
    )(proj, conv_w, conv_b)


def _ml_conv_bwd(proj, conv_w, conv_b, d_qk, d_proj):
    s = proj.shape[0]
    nblk = 2 * D_GROUP // LANES

    def body(x_ref, w_ref, b_ref, dy_ref, _, dx_ref, dw_ref, db_ref, dxs_ref, pad_ref, dpad_ref):
        pad_ref[0:SUBLANES, :] = jnp.zeros((SUBLANES, LANES), F32)
        pad_ref[SUBLANES:, :] = x_ref[...]
        dpad_ref[s:, :] = jnp.zeros((SUBLANES, LANES), F32)
        db = jnp.zeros((1, LANES), F32)
        for r0 in range(0, s, ROW_TILE):
            rows = min(ROW_TILE, s - r0)
            pre = _conv_fwd_tile(pad_ref, w_ref, b_ref, r0, rows, ML_CONV)
            _, vjp = jax.vjp(jax.nn.silu, pre)
            d_pre, = vjp(dy_ref[r0:r0 + rows, :])
            dpad_ref[r0:r0 + rows, :] = d_pre
            db = db + jnp.sum(d_pre, axis=0, keepdims=True)
        db_ref[...] = db
        dws = [jnp.zeros((1, LANES), F32) for _ in range(ML_CONV)]
        dx_sum = jnp.zeros((1, LANES), F32)
        for r0 in range(0, s, ROW_TILE):
            dx_sum = dx_sum + _conv_grads_tile(pad_ref, dpad_ref, dx_ref, w_ref, dws, r0, min(ROW_TILE, s - r0),
                                               ML_CONV)
        dxs_ref[...] = dx_sum
        for j in range(ML_CONV):
            dw_ref[j:j + 1, :] = dws[j]

    return pl.pallas_call(
        body, name="ml_conv_bwd", grid=(nblk,),
        in_specs=[pl.BlockSpec((s, LANES), lambda j: (0, SEG_MQ + j)),
                  pl.BlockSpec((ML_CONV, LANES), lambda j: (0, j)),
                  pl.BlockSpec((1, LANES), lambda j: (0, j)),
                  pl.BlockSpec((s, LANES), lambda j: (0, j)),
                  pl.BlockSpec(memory_space=pl.ANY)],
        out_specs=[pl.BlockSpec((s, LANES), lambda j: (0, SEG_MQ + j)),
                   pl.BlockSpec((ML_CONV, LANES), lambda j: (0, j)),
                   pl.BlockSpec((1, LANES), lambda j: (0, j)),
                   pl.BlockSpec((1, LANES), lambda j: (0, j))],
        out_shape=[jax.ShapeDtypeStruct(d_proj.shape, d_proj.dtype),
                   jax.ShapeDtypeStruct((ML_CONV, 2 * D_GROUP), F32),
                   jax.ShapeDtypeStruct((1, 2 * D_GROUP), F32),
                   jax.ShapeDtypeStruct((1, 2 * D_GROUP), F32)],
        input_output_aliases={4: 0},
        scratch_shapes=[pltpu.VMEM((s + SUBLANES, LANES), F32), pltpu.VMEM((s + SUBLANES, LANES), F32)],
        compiler_params=_params(("parallel",)),
    )(proj, conv_w, conv_b, d_qk, d_proj)


def _gelu_mul(a, b):
    return jax.nn.gelu(a) * b


FFN_BLOCKS = D_FF_P // LANES


def _ffn_conv_fwd(u, conv_w, conv_b):
    s = u.shape[0]

    def body(g_ref, v_ref, wg_ref, wv_ref, bg_ref, bv_ref, o_ref, gpad_ref, vpad_ref):
        for pad_ref, x_ref in ((gpad_ref, g_ref), (vpad_ref, v_ref)):
            pad_ref[0:SUBLANES, :] = jnp.zeros((SUBLANES, LANES), F32)
            pad_ref[SUBLANES:, :] = x_ref[...]
        for r0 in range(0, s, ROW_TILE):
            rows = min(ROW_TILE, s - r0)
            ug = _conv_fwd_tile(gpad_ref, wg_ref, bg_ref, r0, rows, FFN_CONV)
            uv = _conv_fwd_tile(vpad_ref, wv_ref, bv_ref, r0, rows, FFN_CONV)
            o_ref[r0:r0 + rows, :] = _gelu_mul(ug, uv).astype(o_ref.dtype)

    col = lambda off: (lambda j: (0, off + j))
    return pl.pallas_call(
        body, name="ffn_conv_fwd", grid=(FFN_BLOCKS,),
        in_specs=[pl.BlockSpec((s, LANES), col(0)), pl.BlockSpec((s, LANES), col(FFN_BLOCKS)),
                  pl.BlockSpec((FFN_CONV, LANES), col(0)), pl.BlockSpec((FFN_CONV, LANES), col(FFN_BLOCKS)),
                  pl.BlockSpec((1, LANES), col(0)), pl.BlockSpec((1, LANES), col(FFN_BLOCKS))],
        out_specs=pl.BlockSpec((s, LANES), col(0)),
        out_shape=jax.ShapeDtypeStruct((s, D_FF_P), BF16),
        scratch_shapes=[pltpu.VMEM((s + SUBLANES, LANES), F32), pltpu.VMEM((s + SUBLANES, LANES), F32)],
        compiler_params=_params(("parallel",)),
    )(u, u, conv_w, conv_w, conv_b, conv_b)


def _ffn_conv_bwd(u, conv_w, conv_b, d_h):
    s = u.shape[0]

    def body(g_ref, v_ref, wg_ref, wv_ref, bg_ref, bv_ref, dh_ref,
             dug_ref, duv_ref, dwg_ref, dwv_ref, dbg_ref, dbv_ref,
             gpad_ref, vpad_ref, dgpad_ref, dvpad_ref):
        for pad_ref, x_ref in ((gpad_ref, g_ref), (vpad_ref, v_ref)):
            pad_ref[0:SUBLANES, :] = jnp.zeros((SUBLANES, LANES), F32)
            pad_ref[SUBLANES:, :] = x_ref[...]
        dgpad_ref[s:, :] = jnp.zeros((SUBLANES, LANES), F32)
        dvpad_ref[s:, :] = jnp.zeros((SUBLANES, LANES), F32)
        dbg = jnp.zeros((1, LANES), F32)
        dbv = jnp.zeros((1, LANES), F32)
        for r0 in range(0, s, ROW_TILE):
            rows = min(ROW_TILE, s - r0)
            ug = _conv_fwd_tile(gpad_ref, wg_ref, bg_ref, r0, rows, FFN_CONV)
            uv = _conv_fwd_tile(vpad_ref, wv_ref, bv_ref, r0, rows, FFN_CONV)
            _, vjp = jax.vjp(_gelu_mul, ug, uv)
            d_ug, d_uv = vjp(dh_ref[r0:r0 + rows, :])
            dgpad_ref[r0:r0 + rows, :] = d_ug
            dvpad_ref[r0:r0 + rows, :] = d_uv
            dbg = dbg + jnp.sum(d_ug, axis=0, keepdims=True)
            dbv = dbv + jnp.sum(d_uv, axis=0, keepdims=True)
        dbg_ref[...] = dbg
        dbv_ref[...] = dbv
        for pad_ref, dpad_ref, w_ref, dx_ref, dw_ref in ((gpad_ref, dgpad_ref, wg_ref, dug_ref, dwg_ref),
                                                         (vpad_ref, dvpad_ref, wv_ref, duv_ref, dwv_ref)):
            dws = [jnp.zeros((1, LANES), F32) for _ in range(FFN_CONV)]
            for r0 in range(0, s, ROW_TILE):
                _conv_grads_tile(pad_ref, dpad_ref, dx_ref, w_ref, dws, r0, min(ROW_TILE, s - r0), FFN_CONV)
            for j in range(FFN_CONV):
                dw_ref[j:j + 1, :] = dws[j]

    col = lambda off: (lambda j: (0, off + j))
    seq = pl.BlockSpec((s, LANES), col(0))
    return pl.pallas_call(
        body, name="ffn_conv_bwd", grid=(FFN_BLOCKS,),
        in_specs=[pl.BlockSpec((s, LANES), col(0)), pl.BlockSpec((s, LANES), col(FFN_BLOCKS)),
                  pl.BlockSpec((FFN_CONV, LANES), col(0)), pl.BlockSpec((FFN_CONV, LANES), col(FFN_BLOCKS)),
                  pl.BlockSpec((1, LANES), col(0)), pl.BlockSpec((1, LANES), col(FFN_BLOCKS)), seq],
        out_specs=[seq, seq, pl.BlockSpec((FFN_CONV, LANES), col(0)), pl.BlockSpec((FFN_CONV, LANES), col(0)),
                   pl.BlockSpec((1, LANES), col(0)), pl.BlockSpec((1, LANES), col(0))],
        out_shape=[jax.ShapeDtypeStruct((s, D_FF_P), BF16), jax.ShapeDtypeStruct((s, D_FF_P), BF16),
                   jax.ShapeDtypeStruct((FFN_CONV, D_FF_P), F32), jax.ShapeDtypeStruct((FFN_CONV, D_FF_P), F32),
                   jax.ShapeDtypeStruct((1, D_FF_P), F32), jax.ShapeDtypeStruct((1, D_FF_P), F32)],
        scratch_shapes=[pltpu.VMEM((s + SUBLANES, LANES), F32) for _ in range(4)],
        compiler_params=_params(("parallel",)),
    )(u, u, conv_w, conv_w, conv_b, conv_b, d_h)


def _chunk_masks(c):
    row = lax.broadcasted_iota(jnp.int32, (c, c), 0)
    col = lax.broadcasted_iota(jnp.int32, (c, c), 1)
    return row, col


@jax.custom_vjp
def _split_heads(x):
    return tuple(x[:, h * D_HEAD:(h + 1) * D_HEAD] for h in range(N_HEADS))


_split_heads.defvjp(lambda x: (_split_heads(x), None), lambda _, gs: (jnp.concatenate(gs, axis=1),))


@jax.custom_vjp
def _merge_heads(xs):
    return jnp.concatenate(xs, axis=1)


_merge_heads.defvjp(lambda xs: (_merge_heads(xs), None), lambda _, g: (_split_heads(g),))

HEADS = range(N_HEADS)


def _hg_chunk(hq, hf, hi, hgate, l0, l1, nw, sts):
    c = hq.shape[0]
    row, col = _chunk_masks(c)
    mask = col <= row
    mx = lax.stop_gradient(jnp.maximum(l0, l1))
    e0 = jnp.exp(l0 - mx)
    e1 = jnp.exp(l1 - mx)
    lb = e0 / (e0 + e1)
    sig = jax.nn.sigmoid(hf)
    lf = jnp.log(lb + (1.0 - lb) * sig)
    k = (1.0 - lb) * jax.nn.sigmoid(-hf)
    q = jax.nn.silu(hq)
    b = _dg(mask.astype(F32), lf, 1, 0, HIGHEST)
    rid = lax.broadcasted_iota(jnp.int32, b.shape, 0)
    b_ref = jnp.sum(jnp.where(rid == c // 2 - 1, b, 0.0), axis=0, keepdims=True)
    b_last = jnp.sum(jnp.where(rid == c - 1, b, 0.0), axis=0, keepdims=True)
    qa = _split_heads(q * jnp.exp(b - b_ref))
    ka = _split_heads(k * jnp.exp(b_ref - b))
    qe = _split_heads(q * jnp.exp(b))
    kd = _split_heads(k * jnp.exp(b_last - b))
    decay = _split_heads(jnp.exp(b_last))
    v = _split_heads(hi)
    attn = [jnp.where(mask, _nt(qa[h], ka[h]), 0.0) for h in HEADS]
    intra = [_nn(attn[h], v[h]) for h in HEADS]
    inter = [_nt(qe[h], sts[h]) for h in HEADS]
    kv = [_tn(v[h], kd[h]) for h in HEADS]
    sts_new = tuple(decay[h] * sts[h] + kv[h] for h in HEADS)
    o = [intra[h] + inter[h] for h in HEADS]
    normed = _merge_heads(tuple(o[h] * lax.rsqrt(jnp.mean(o[h] * o[h], axis=-1, keepdims=True) + LN_EPS)
                                for h in HEADS))
    return normed * nw * jax.nn.silu(hgate), sts_new


def _seg(ref, seg):
    return ref[:, seg * D_GROUP:(seg + 1) * D_GROUP]


def _hgrn2_fwd(proj, logits, norm_w):
    s = proj.shape[0]
    nc = s // CHUNK

    def body(p_ref, lg_ref, nw_ref, y_ref, st_out_ref, st_scr):
        @pl.when(pl.program_id(0) == 0)
        def _():
            st_scr[...] = jnp.zeros_like(st_scr)

        sts = tuple(st_scr[h] for h in HEADS)
        y, sts_new = _hg_chunk(_seg(p_ref, 0), _seg(p_ref, 1), _seg(p_ref, 2), _seg(p_ref, 3),
                               lg_ref[0:1, :], lg_ref[1:2, :], nw_ref[...], sts)
        y_ref[...] = y.astype(y_ref.dtype)
        for h in HEADS:
            st_out_ref[h] = sts[h]
            st_scr[h] = sts_new[h]

    return pl.pallas_call(
        body, name="hgrn2_fwd", grid=(nc,),
        in_specs=[pl.BlockSpec((CHUNK, 4 * D_GROUP), lambda c: (c, 0)),
                  pl.BlockSpec((2, D_GROUP), lambda c: (0, 0)),
                  pl.BlockSpec((1, D_GROUP), lambda c: (0, 0))],
        out_specs=[pl.BlockSpec((CHUNK, D_GROUP), lambda c: (c, 0)),
                   pl.BlockSpec((None, N_HEADS, D_HEAD, D_HEAD), lambda c: (c, 0, 0, 0))],
        out_shape=[jax.ShapeDtypeStruct((s, 2 * D_GROUP), BF16),
                   jax.ShapeDtypeStruct((nc, N_HEADS, D_HEAD, D_HEAD), F32)],
        scratch_shapes=[pltpu.VMEM((N_HEADS, D_HEAD, D_HEAD), F32)],
        compiler_params=_params(("arbitrary",)),
    )(proj, logits, norm_w)


def _hgrn2_bwd(proj, logits, norm_w, states, d_y):
    s = proj.shape[0]
    nc = s // CHUNK

    def body(p_ref, lg_ref, nw_ref, st_ref, dy_ref, dp_ref, dl_ref, dnw_ref, dsum_ref, dst_scr):
        @pl.when(pl.program_id(0) == 0)
        def _():
            dst_scr[...] = jnp.zeros_like(dst_scr)
            dl_ref[...] = jnp.zeros_like(dl_ref)
            dnw_ref[...] = jnp.zeros_like(dnw_ref)
            dsum_ref[...] = jnp.zeros_like(dsum_ref)

        _, vjp = jax.vjp(_hg_chunk, _seg(p_ref, 0), _seg(p_ref, 1), _seg(p_ref, 2), _seg(p_ref, 3),
                         lg_ref[0:1, :], lg_ref[1:2, :], nw_ref[...], tuple(st_ref[h] for h in HEADS))
        d_hq, d_hf, d_hi, d_hg, d_l0, d_l1, d_nw, d_sts = vjp((dy_ref[...], tuple(dst_scr[h] for h in HEADS)))
        for seg, val in enumerate((d_hq, d_hf, d_hi, d_hg)):
            dp_ref[:, seg * D_GROUP:(seg + 1) * D_GROUP] = val.astype(dp_ref.dtype)
            dsum_ref[:, seg * D_GROUP:(seg + 1) * D_GROUP] += jnp.sum(val, axis=0, keepdims=True)
        dl_ref[0:1, :] += d_l0
        dl_ref[1:2, :] += d_l1
        dnw_ref[...] += d_nw
        for h in HEADS:
            dst_scr[h] = d_sts[h]

    rev = lambda c: nc - 1 - c
    return pl.pallas_call(
        body, name="hgrn2_bwd", grid=(nc,),
        in_specs=[pl.BlockSpec((CHUNK, 4 * D_GROUP), lambda c: (rev(c), 0)),
                  pl.BlockSpec((2, D_GROUP), lambda c: (0, 0)),
                  pl.BlockSpec((1, D_GROUP), lambda c: (0, 0)),
                  pl.BlockSpec((None, N_HEADS, D_HEAD, D_HEAD), lambda c: (rev(c), 0, 0, 0)),
                  pl.BlockSpec((CHUNK, D_GROUP), lambda c: (rev(c), 0))],
        out_specs=[pl.BlockSpec((CHUNK, 4 * D_GROUP), lambda c: (rev(c), 0)),
                   pl.BlockSpec((2, D_GROUP), lambda c: (0, 0)),
                   pl.BlockSpec((1, D_GROUP), lambda c: (0, 0)),
                   pl.BlockSpec((1, 4 * D_GROUP), lambda c: (0, 0))],
        out_shape=[jax.ShapeDtypeStruct((s, D_IN_MAIN), BF16), jax.ShapeDtypeStruct((2, D_GROUP), F32),
                   jax.ShapeDtypeStruct((1, D_GROUP), F32), jax.ShapeDtypeStruct((1, 4 * D_GROUP), F32)],
        scratch_shapes=[pltpu.VMEM((N_HEADS, D_HEAD, D_HEAD), F32)],
        compiler_params=_params(("arbitrary",)),
    )(proj, logits, norm_w, states, d_y)


def _gate_column(gates, lane, idx):
    return jnp.sum(jnp.where(lane == idx, gates, 0.0), axis=1, keepdims=True)


def _head_layer_norm(h):
    mu = jnp.mean(h, axis=-1, keepdims=True)
    var = jnp.mean(jnp.square(h - mu), axis=-1, keepdims=True)
    return (h - mu) * lax.rsqrt(var + LN_EPS)


def _ml_chunk(qc, kc, v, mo, gates, nw, cts, ns, ms):
    c = qc.shape[0]
    row, col = _chunk_masks(c)
    mask = col <= row
    eye = col == row
    lane = lax.broadcasted_iota(jnp.int32, gates.shape, 1)
    to_row = lambda t: jnp.sum(jnp.where(eye, t, 0.0), axis=0, keepdims=True)
    q = _split_heads(qc * (D_HEAD ** -0.5))
    k = _split_heads(kc)
    vs = _split_heads(v)
    ig = [_gate_column(gates, lane, h) for h in HEADS]
    lf = [jax.nn.log_sigmoid(_gate_column(gates, lane, N_HEADS + h)) for h in HEADS]
    lf_row = [to_row(lf[h]) for h in HEADS]
    ig_row = [to_row(ig[h]) for h in HEADS]
    b_col = [jnp.sum(jnp.where(mask, lf_row[h], 0.0), axis=1, keepdims=True) for h in HEADS]
    b_row = [jnp.sum(jnp.where(row <= col, lf[h], 0.0), axis=0, keepdims=True) for h in HEADS]
    g = [jnp.sum(lf[h], axis=0, keepdims=True) for h in HEADS]
    d = [jnp.where(mask, b_col[h] - b_row[h] + ig_row[h], -jnp.inf) for h in HEADS]
    inter = [b_col[h] + ms[h] for h in HEADS]
    m_t = [lax.stop_gradient(jnp.maximum(inter[h], jnp.max(d[h], axis=1, keepdims=True))) for h in HEADS]
    qk = [_nt(q[h], k[h]) for h in HEADS]
    qc_state = [_nt(q[h], cts[h]) for h in HEADS]
    sc = [qk[h] * jnp.exp(d[h] - m_t[h]) for h in HEADS]
    w_inter = [jnp.exp(inter[h] - m_t[h]) for h in HEADS]
    sv = [_nn(sc[h], vs[h]) for h in HEADS]
    num = [sv[h] + w_inter[h] * qc_state[h] for h in HEADS]
    den = [jnp.sum(sc[h], axis=1, keepdims=True) + w_inter[h] * jnp.sum(q[h] * ns[h], axis=1, keepdims=True)
           for h in HEADS]
    hh = [num[h] / jnp.maximum(jnp.abs(den[h]), jnp.exp(-m_t[h])) for h in HEADS]
    a = [g[h] - b_col[h] + ig[h] for h in HEADS]
    ms_new = tuple(lax.stop_gradient(jnp.maximum(g[h] + ms[h], jnp.max(a[h], axis=0, keepdims=True)))
                   for h in HEADS)
    decay = [jnp.exp(g[h] + ms[h] - ms_new[h]) for h in HEADS]
    wk = [k[h] * jnp.exp(a[h] - ms_new[h]) for h in HEADS]
    kv = [_tn(vs[h], wk[h]) for h in HEADS]
    cts_new = tuple(decay[h] * cts[h] + kv[h] for h in HEADS)
    ns_new = tuple(decay[h] * ns[h] + jnp.sum(wk[h], axis=0, keepdims=True) for h in HEADS)
    normed = _merge_heads(tuple(_head_layer_norm(hh[h]) for h in HEADS))
    return jax.nn.sigmoid(mo) * (normed * nw), cts_new, ns_new, ms_new


def _mlstm_fwd(qk, proj, gates, norm_w, y):
    s = proj.shape[0]
    nc = s // CHUNK

    def body(qk_ref, vo_ref, g_ref, nw_ref, _, y_ref, ct_out, n_out, m_out, ct_scr, n_scr, m_scr):
        @pl.when(pl.program_id(0) == 0)
        def _():
            ct_scr[...] = jnp.zeros_like(ct_scr)
            n_scr[...] = jnp.zeros_like(n_scr)
            m_scr[...] = jnp.full(m_scr.shape, NEG_BIG, F32)

        cts = tuple(ct_scr[h] for h in HEADS)
        ns = tuple(n_scr[h] for h in HEADS)
        ms = tuple(m_scr[h] for h in HEADS)
        y, cts_new, ns_new, ms_new = _ml_chunk(_seg(qk_ref, 0), _seg(qk_ref, 1), _seg(vo_ref, 0), _seg(vo_ref, 1),
                                               g_ref[...], nw_ref[...], cts, ns, ms)
        y_ref[...] = y.astype(y_ref.dtype)
        for h in HEADS:
            ct_out[h], n_out[h], m_out[h] = cts[h], ns[h], ms[h]
            ct_scr[h], n_scr[h], m_scr[h] = cts_new[h], ns_new[h], ms_new[h]

    st = lambda r, w: pl.BlockSpec((None, N_HEADS, r, w), lambda c: (c, 0, 0, 0))
    return pl.pallas_call(
        body, name="mlstm_fwd", grid=(nc,),
        in_specs=[pl.BlockSpec((CHUNK, 2 * D_GROUP), lambda c: (c, 0)),
                  pl.BlockSpec((CHUNK, 2 * D_GROUP), lambda c: (c, 3)),
                  pl.BlockSpec((CHUNK, LANES), lambda c: (c, 0)),
                  pl.BlockSpec((1, D_GROUP), lambda c: (0, 0)),
                  pl.BlockSpec(memory_space=pl.ANY)],
        out_specs=[pl.BlockSpec((CHUNK, D_GROUP), lambda c: (c, 1)),
                   st(D_HEAD, D_HEAD), st(1, D_HEAD), st(1, 1)],
        out_shape=[jax.ShapeDtypeStruct(y.shape, y.dtype),
                   jax.ShapeDtypeStruct((nc, N_HEADS, D_HEAD, D_HEAD), F32),
                   jax.ShapeDtypeStruct((nc, N_HEADS, 1, D_HEAD), F32),
                   jax.ShapeDtypeStruct((nc, N_HEADS, 1, 1), F32)],
        input_output_aliases={4: 0},
        scratch_shapes=[pltpu.VMEM((N_HEADS, D_HEAD, D_HEAD), F32), pltpu.VMEM((N_HEADS, 1, D_HEAD), F32),
                        pltpu.VMEM((N_HEADS, 1, 1), F32)],
        compiler_params=_params(("arbitrary",)),
    )(qk, proj, gates, norm_w, y)


def _mlstm_bwd(qk, proj, gates, norm_w, ct_s, n_s, m_s, d_y, d_proj):
    s = proj.shape[0]
    nc = s // CHUNK

    def body(qk_ref, vo_ref, g_ref, nw_ref, ct_ref, n_ref, m_ref, dy_ref, _,
             dp_ref, dqk_ref, dg_ref, dnw_ref, dsum_ref, dct_scr, dn_scr):
        @pl.when(pl.program_id(0) == 0)
        def _():
            dct_scr[...] = jnp.zeros_like(dct_scr)
            dn_scr[...] = jnp.zeros_like(dn_scr)
            dnw_ref[...] = jnp.zeros_like(dnw_ref)
            dsum_ref[...] = jnp.zeros_like(dsum_ref)

        ms = tuple(m_ref[h] for h in HEADS)
        step = lambda *a: _ml_chunk(*a, ms)[:3]
        _, vjp = jax.vjp(step, _seg(qk_ref, 0), _seg(qk_ref, 1), _seg(vo_ref, 0), _seg(vo_ref, 1), g_ref[...],
                         nw_ref[...], tuple(ct_ref[h] for h in HEADS), tuple(n_ref[h] for h in HEADS))
        d_q, d_k, d_v, d_o, d_gates, d_nw, d_cts, d_ns = vjp(
            (dy_ref[...], tuple(dct_scr[h] for h in HEADS), tuple(dn_scr[h] for h in HEADS)))
        dqk_ref[:, 0:D_GROUP] = d_q
        dqk_ref[:, D_GROUP:2 * D_GROUP] = d_k
        for seg, val in enumerate((d_v, d_o)):
            dp_ref[:, seg * D_GROUP:(seg + 1) * D_GROUP] = val.astype(dp_ref.dtype)
            dsum_ref[:, seg * D_GROUP:(seg + 1) * D_GROUP] += jnp.sum(val, axis=0, keepdims=True)
        dg_ref[...] = d_gates
        dnw_ref[...] += d_nw
        for h in HEADS:
            dct_scr[h] = d_cts[h]
            dn_scr[h] = d_ns[h]

    rev = lambda c: nc - 1 - c
    st = lambda r, w: pl.BlockSpec((None, N_HEADS, r, w), lambda c: (rev(c), 0, 0, 0))
    return pl.pallas_call(
        body, name="mlstm_bwd", grid=(nc,),
        in_specs=[pl.BlockSpec((CHUNK, 2 * D_GROUP), lambda c: (rev(c), 0)),
                  pl.BlockSpec((CHUNK, 2 * D_GROUP), lambda c: (rev(c), 3)),
                  pl.BlockSpec((CHUNK, LANES), lambda c: (rev(c), 0)),
                  pl.BlockSpec((1, D_GROUP), lambda c: (0, 0)),
                  st(D_HEAD, D_HEAD), st(1, D_HEAD), st(1, 1),
                  pl.BlockSpec((CHUNK, D_GROUP), lambda c: (rev(c), 1)),
                  pl.BlockSpec(memory_space=pl.ANY)],
        out_specs=[pl.BlockSpec((CHUNK, 2 * D_GROUP), lambda c: (rev(c), 3)),
                   pl.BlockSpec((CHUNK, 2 * D_GROUP), lambda c: (rev(c), 0)),
                   pl.BlockSpec((CHUNK, LANES), lambda c: (rev(c), 0)),
                   pl.BlockSpec((1, D_GROUP), lambda c: (0, 0)),
                   pl.BlockSpec((1, 2 * D_GROUP), lambda c: (0, 0))],
        out_shape=[jax.ShapeDtypeStruct(d_proj.shape, d_proj.dtype), jax.ShapeDtypeStruct((s, 2 * D_GROUP), F32),
                   jax.ShapeDtypeStruct((s, LANES), F32), jax.ShapeDtypeStruct((1, D_GROUP), F32),
                   jax.ShapeDtypeStruct((1, 2 * D_GROUP), F32)],
        input_output_aliases={8: 0},
        scratch_shapes=[pltpu.VMEM((N_HEADS, D_HEAD, D_HEAD), F32), pltpu.VMEM((N_HEADS, 1, D_HEAD), F32)],
        compiler_params=_params(("arbitrary",)),
    )(qk, proj, gates, norm_w, ct_s, n_s, m_s, d_y, d_proj)


LN_TOKENS = 512
ATT_TOKENS = 256


def _res_ln_fwd(xres, branch, g, b, name):
    s, dm = xres.shape
    tb = min(LN_TOKENS, s)

    def body(x_ref, br_ref, g_ref, b_ref, o_ref):
        o_ref[...] = _layer_norm(ALPHA * x_ref[...] + br_ref[...], g_ref[...], b_ref[...])

    tok = pl.BlockSpec((tb, dm), lambda i: (i, 0))
    vec = pl.BlockSpec((1, dm), lambda i: (0, 0))
    return pl.pallas_call(
        body, name=name, grid=(s // tb,), in_specs=[tok, tok, vec, vec], out_specs=tok,
        out_shape=jax.ShapeDtypeStruct((s, dm), F32), compiler_params=_params(("parallel",)),
    )(xres, branch, g, b)


def _res_ln_bwd(xres, branch, g, b, d_out, name):
    s, dm = xres.shape
    tb = min(LN_TOKENS, s)

    def body(x_ref, br_ref, g_ref, b_ref, do_ref, dz_ref, dg_ref, db_ref):
        @pl.when(pl.program_id(0) == 0)
        def _():
            dg_ref[...] = jnp.zeros_like(dg_ref)
            db_ref[...] = jnp.zeros_like(db_ref)

        z = ALPHA * x_ref[...] + br_ref[...]
        _, vjp = jax.vjp(_layer_norm, z, g_ref[...], b_ref[...])
        d_z, d_g, d_b = vjp(do_ref[...])
        dz_ref[...] = d_z
        dg_ref[...] += d_g
        db_ref[...] += d_b

    tok = pl.BlockSpec((tb, dm), lambda i: (i, 0))
    vec = pl.BlockSpec((1, dm), lambda i: (0, 0))
    return pl.pallas_call(
        body, name=name, grid=(s // tb,), in_specs=[tok, tok, vec, vec, tok], out_specs=[tok, vec, vec],
        out_shape=[jax.ShapeDtypeStruct((s, dm), F32), jax.ShapeDtypeStruct((1, dm), F32),
                   jax.ShapeDtypeStruct((1, dm), F32)],
        compiler_params=_params(("arbitrary",)),
    )(xres, branch, g, b, d_out)


def _loss_tail(xres, branch, g, b, target):
    s, dm = xres.shape
    tb = min(LN_TOKENS, s)

    def loss_fn(z, gg, bb, tgt):
        err = jnp.square(_layer_norm(z, gg, bb) - tgt)
        return 0.5 * jnp.sum(jnp.mean(err, axis=-1, keepdims=True), axis=0, keepdims=True)

    def body(x_ref, br_ref, g_ref, b_ref, t_ref, loss_ref, dz_ref, dg_ref, db_ref):
        @pl.when(pl.program_id(0) == 0)
        def _():
            loss_ref[...] = jnp.zeros_like(loss_ref)
            dg_ref[...] = jnp.zeros_like(dg_ref)
            db_ref[...] = jnp.zeros_like(db_ref)

        z = ALPHA * x_ref[...] + br_ref[...]
        tgt = t_ref[...]
        loss, vjp = jax.vjp(lambda zz, gg, bb: loss_fn(zz, gg, bb, tgt), z, g_ref[...], b_ref[...])
        d_z, d_g, d_b = vjp(jnp.ones((1, 1), F32))
        loss_ref[...] += loss
        dz_ref[...] = d_z
        dg_ref[...] += d_g
        db_ref[...] += d_b

    tok = pl.BlockSpec((tb, dm), lambda i: (i, 0))
    vec = pl.BlockSpec((1, dm), lambda i: (0, 0))
    one = pl.BlockSpec((1, 1), lambda i: (0, 0))
    return pl.pallas_call(
        body, name="loss_tail", grid=(s // tb,), in_specs=[tok, tok, vec, vec, tok],
        out_specs=[one, tok, vec, vec],
        out_shape=[jax.ShapeDtypeStruct((1, 1), F32), jax.ShapeDtypeStruct((s, dm), F32),
                   jax.ShapeDtypeStruct((1, dm), F32), jax.ShapeDtypeStruct((1, dm), F32)],
        compiler_params=_params(("arbitrary",)),
    )(xres, branch, g, b, target)


def _att_head(q, k, v):
    sc = _nt(q, k) * (CA_DH ** -0.5)
    return _nn(jax.nn.softmax(sc, axis=-1), v)


def _att_fwd(q, kv):
    s = q.shape[0]
    tb = min(ATT_TOKENS, s)

    def body(q_ref, kv_ref, o_ref):
        for h in range(CA_HEADS):
            lo = h * CA_DH
            o_ref[:, lo:lo + CA_DH] = _att_head(q_ref[:, lo:lo + CA_DH], kv_ref[:, lo:lo + CA_DH],
                                                kv_ref[:, D_MODEL + lo:D_MODEL + lo + CA_DH]).astype(o_ref.dtype)

    tok = pl.BlockSpec((tb, D_MODEL), lambda i: (i, 0))
    return pl.pallas_call(
        body, name="att_fwd", grid=(s // tb,),
        in_specs=[tok, pl.BlockSpec((N_MEM, 2 * D_MODEL), lambda i: (0, 0))], out_specs=tok,
        out_shape=jax.ShapeDtypeStruct((s, D_MODEL), BF16), compiler_params=_params(("parallel",)),
    )(q, kv)


def _att_bwd(q, kv, d_o):
    s = q.shape[0]
    tb = min(ATT_TOKENS, s)

    def body(q_ref, kv_ref, do_ref, dq_ref, dkv_ref):
        @pl.when(pl.program_id(0) == 0)
        def _():
            dkv_ref[...] = jnp.zeros_like(dkv_ref)

        for h in range(CA_HEADS):
            lo = h * CA_DH
            vlo = D_MODEL + lo
            _, vjp = jax.vjp(_att_head, q_ref[:, lo:lo + CA_DH], kv_ref[:, lo:lo + CA_DH],
                             kv_ref[:, vlo:vlo + CA_DH])
            d_q, d_k, d_v = vjp(do_ref[:, lo:lo + CA_DH])
            dq_ref[:, lo:lo + CA_DH] = d_q
            dkv_ref[:, lo:lo + CA_DH] += d_k
            dkv_ref[:, vlo:vlo + CA_DH] += d_v

    tok = pl.BlockSpec((tb, D_MODEL), lambda i: (i, 0))
    mem = pl.BlockSpec((N_MEM, 2 * D_MODEL), lambda i: (0, 0))
    return pl.pallas_call(
        body, name="att_bwd", grid=(s // tb,), in_specs=[tok, mem, tok], out_specs=[tok, mem],
        out_shape=[jax.ShapeDtypeStruct((s, D_MODEL), F32), jax.ShapeDtypeStruct((N_MEM, 2 * D_MODEL), F32)],
        compiler_params=_params(("arbitrary",)),
    )(q, kv, d_o)


def _local_step(x, mem, target, w, mid_weights=None, ffn_weights=None, on_ffn_grads=None, on_mid_grads=None):
    w = dict(w)
    s = x.shape[0]
    tm = min(512, s)
    tt = min(512, s)
    proj = _matmul_nn(x, w["w_in_main"], w["b_in_main"], min(2048, s), 512, "proj")
    gates = _matmul_nn(x, w["w_in_gate"], w["b_in_gate"], tm, LANES, "proj_gates")
    qk = _ml_conv_fwd(proj, w["ml_conv_w"], w["ml_conv_b"])
    y, hg_states = _hgrn2_fwd(proj, w["hg_lb_logits"], w["hg_norm_w"])
    y, ct_s, n_s, m_s = _mlstm_fwd(qk, proj, gates, w["ml_norm_w"], y)
    if mid_weights is not None:
        w.update(mid_weights(y))
    mix =_matmul_nn(y, w["w_out"], None, tm, D_MODEL, "mix")
    x1 = _res_ln_fwd(x, mix, w["ln1_g"], w["ln1_b"], "ln1_fwd")
    kv = _matmul_nn(mem, w["ca_wkv"], None, N_MEM, CA_DH, "kv")
    q = _matmul_nn(x1, w["ca_wq"], None, tm, D_MODEL, "ca_q")
    att = _att_fwd(q, kv)
    ca = _matmul_nn(att, w["ca_wo"], None, tm, D_MODEL, "ca_out")
    x2 = _res_ln_fwd(x1, ca, w["ln2_g"], w["ln2_b"], "ln2_fwd")
    if ffn_weights is not None:
        w.update(ffn_weights(x2))
    u = _matmul_nn(x2, w["ffn_w_up"], None, min(2048, s), UP_SHARD_P, "ffn_up")
    hid = _ffn_conv_fwd(u, w["ffn_conv_w"], w["ffn_conv_b"])
    ff = _matmul_nn(hid, w["ffn_w_down"], None, tm, D_MODEL, "ffn_down")
    loss, d_z3, d_ln3_g, d_ln3_b = _loss_tail(x2, ff, w["ln3_g"], w["ln3_b"], target)
    grads = {"ln3_g": d_ln3_g, "ln3_b": d_ln3_b}
    grads["ffn_w_down"] = _matmul_tn(hid, d_z3, 1536, D_MODEL, tt, "d_w_down")
    d_hid = _matmul_nt([(d_z3, w["ffn_w_down"])], None, 1.0, tm, D_FF_P, "d_hid")
    d_ug, d_uv, d_cwg, d_cwv, d_cbg, d_cbv = _ffn_conv_bwd(u, w["ffn_conv_w"], w["ffn_conv_b"], d_hid)
    grads["ffn_conv_w"] = jnp.concatenate([d_cwg, d_cwv], axis=-1)
    grads["ffn_conv_b"] = jnp.concatenate([d_cbg, d_cbv], axis=-1)
    half = N_DEV // 2
    d_w_up = _matmul_tn(x2, d_ug, D_MODEL, UP_SHARD_P, tt, "d_w_up_gate", shards=N_DEV, group=half)
    grads["ffn_w_up"] = _matmul_tn(x2, d_uv, D_MODEL, UP_SHARD_P, tt, "d_w_up_val", shards=N_DEV,
                                   shard0=half, group=half, into=d_w_up)
    d_x2 = _matmul_nt([(d_ug, w["ffn_w_up"], 0), (d_uv, w["ffn_w_up"], N_DEV // 2)], d_z3, ALPHA,
                      min(256, s), D_MODEL, "d_x2")
    if on_ffn_grads is not None:
        d_x2 = on_ffn_grads(grads, d_x2)
    d_z2, grads["ln2_g"], grads["ln2_b"] = _res_ln_bwd(x1, ca, w["ln2_g"], w["ln2_b"], d_x2, "ln2_bwd")
    grads["ca_wo"] = _matmul_tn(att, d_z2, D_MODEL, D_MODEL, tt, "d_ca_wo")
    d_att = _matmul_nt([(d_z2, w["ca_wo"])], None, 1.0, tm, D_MODEL, "d_att")
    d_q, d_kv = _att_bwd(q, kv, d_att)
    grads["ca_wq"] = _matmul_tn(x1, d_q, D_MODEL, D_MODEL, tt, "d_ca_wq")
    grads["ca_wkv"] = _matmul_tn(mem, d_kv, D_MODEL, CA_DH, N_MEM, "d_ca_wkv", shards=N_DEV, group=N_DEV)
    d_x1 = _matmul_nt([(d_q, w["ca_wq"])], d_z2, ALPHA, tm, D_MODEL, "d_x1")
    d_z1, grads["ln1_g"], grads["ln1_b"] = _res_ln_bwd(x, mix, w["ln1_g"], w["ln1_b"], d_x1, "ln1_bwd")
    grads["w_out"] = _matmul_tn(y, d_z1, D_MODEL, D_MODEL, tt, "d_w_out")
    if on_mid_grads is not None:
        d_z1 = on_mid_grads(grads, d_z1)
    d_y = _matmul_nt([(d_z1, w["w_out"])], None, 1.0, tm, D_MODEL, "d_y")
    d_proj, grads["hg_lb_logits"], grads["hg_norm_w"], db_hg = _hgrn2_bwd(
        proj, w["hg_lb_logits"], w["hg_norm_w"], hg_states, d_y)
    d_proj, d_qk, d_gates, grads["ml_norm_w"], db_vo = _mlstm_bwd(
        qk, proj, gates, w["ml_norm_w"], ct_s, n_s, m_s, d_y, d_proj)
    d_proj, grads["ml_conv_w"], grads["ml_conv_b"], db_qk = _ml_conv_bwd(
        proj, w["ml_conv_w"], w["ml_conv_b"], d_qk, d_proj)
    grads["b_in_main"] = jnp.concatenate([db_hg, db_qk, db_vo], axis=-1)
    grads["w_in_main"] = _matmul_tn(x, d_proj, D_MODEL, min(2048, D_IN_MAIN), tt, "d_w_in")
    grads["w_in_gate"], grads["b_in_gate"] = _matmul_tn(x, d_gates, D_MODEL, LANES, tt, "d_w_in_gates", colsum=True)
    grad_x = _matmul_nt([(d_proj, w["w_in_main"]), (d_gates, w["w_in_gate"])], d_z1, ALPHA, tm, D_MODEL, "d_x")
    return loss, grad_x, grads


HBM_SPEC = pl.BlockSpec(memory_space=pltpu.HBM)


def _coords():
    return lax.axis_index("x"), lax.axis_index("y"), lax.axis_index("c")


def _other_chips(x, y):
    return [(1 - x, y), (x, 1 - y), (1 - x, 1 - y)]


def _all_gather_two_level(shards, name):
    na = len(shards)

    def body(*refs):
        x_refs, out_refs = refs[:na], refs[na:2 * na]
        send_sems, recv_sems, local_sems = refs[2 * na:]
        x, y, c = _coords()
        me, sibling = (x, y, c), (x, y, 1 - c)
        chips = _other_chips(x, y)

        def copy(a, k, block, to, own=False):
            slot = out_refs[a].at[4 * block[0] + 2 * block[1] + block[2]]
            return pltpu.make_async_remote_copy(
                src_ref=x_refs[a] if own else slot, dst_ref=slot,
                send_sem=send_sems.at[7 * a + k], recv_sem=recv_sems.at[7 * a + k],
                device_id=to, device_id_type=MESH)

        mine = [pltpu.make_async_copy(x_refs[a], out_refs[a].at[4 * x + 2 * y + c], local_sems.at[a])
                for a in range(na)]
        for cp in mine:
            cp.start()
        first = []
        for a in range(na):
            first.append(copy(a, 0, me, sibling, own=True))
            first += [copy(a, 1 + j, me, (*chip, c), own=True) for j, chip in enumerate(chips)]
        for cp in first:
            cp.start()
        passed = []
        for j, chip in enumerate(chips):
            for a in range(na):
                copy(a, 1 + j, (*chip, c), me).wait_recv()
                fwd = copy(a, 4 + j, (*chip, c), sibling)
                fwd.start()
                passed.append(fwd)
        for a in range(na):
            copy(a, 0, sibling, me).wait_recv()
            for j, chip in enumerate(chips):
                copy(a, 4 + j, (*chip, 1 - c), me).wait_recv()
        for cp in first + passed:
            cp.wait_send()
        for cp in mine:
            cp.wait()

    return pl.pallas_call(
        body, name=name,
        out_shape=[jax.ShapeDtypeStruct((N_DEV,) + t.shape, t.dtype) for t in shards],
        in_specs=[HBM_SPEC] * na, out_specs=[HBM_SPEC] * na,
        scratch_shapes=[pltpu.SemaphoreType.DMA((7 * na,)), pltpu.SemaphoreType.DMA((7 * na,)),
                        pltpu.SemaphoreType.DMA((na,))],
    )(*shards)


def _all_gather_direct(vec, name):
    r, n = vec.shape

    def body(x_ref, out_ref, send_sems, recv_sems, local_sem):
        x, y, c = _coords()
        flip = lambda v, bit: 1 - v if bit else v
        me = 4 * x + 2 * y + c
        mine = pltpu.make_async_copy(x_ref, out_ref.at[me], local_sem)
        mine.start()
        copies = []
        for d in range(1, N_DEV):
            peer = (flip(x, d & 4), flip(y, d & 2), flip(c, d & 1))
            peer_slot = 4 * peer[0] + 2 * peer[1] + peer[2]
            out_going = pltpu.make_async_remote_copy(
                src_ref=x_ref, dst_ref=out_ref.at[me], send_sem=send_sems.at[d - 1],
                recv_sem=recv_sems.at[d - 1], device_id=peer, device_id_type=MESH)
            incoming = pltpu.make_async_remote_copy(
                src_ref=x_ref, dst_ref=out_ref.at[peer_slot], send_sem=send_sems.at[d - 1],
                recv_sem=recv_sems.at[d - 1], device_id=peer, device_id_type=MESH)
            out_going.start()
            copies.append((out_going, incoming))
        for out_going, incoming in copies:
            incoming.wait_recv()
            out_going.wait_send()
        mine.wait()

    return pl.pallas_call(
        body, name=name, out_shape=jax.ShapeDtypeStruct((N_DEV, r, n), vec.dtype),
        in_specs=[HBM_SPEC], out_specs=HBM_SPEC,
        scratch_shapes=[pltpu.SemaphoreType.DMA((7,)), pltpu.SemaphoreType.DMA((7,)), pltpu.SemaphoreType.DMA],
    )(vec)


def _exchange_with_sibling(parts):
    na = len(parts)

    def body(*refs):
        p_refs, got_refs = refs[:na], refs[na:2 * na]
        send_sems, recv_sems = refs[2 * na:]
        x, y, c = _coords()
        copies = []
        for a in range(na):
            for chip in range(N_CHIPS):
                cp = pltpu.make_async_remote_copy(
                    src_ref=p_refs[a].at[2 * chip + (1 - c)], dst_ref=got_refs[a].at[chip],
                    send_sem=send_sems.at[N_CHIPS * a + chip], recv_sem=recv_sems.at[N_CHIPS * a + chip],
                    device_id=(x, y, 1 - c), device_id_type=MESH)
                cp.start()
                copies.append(cp)
        for cp in copies:
            cp.wait()

    return pl.pallas_call(
        body, name="grad_exchange_sibling",
        out_shape=[jax.ShapeDtypeStruct((N_CHIPS,) + t.shape[1:], t.dtype) for t in parts],
        in_specs=[HBM_SPEC] * na, out_specs=[HBM_SPEC] * na,
        scratch_shapes=[pltpu.SemaphoreType.DMA((N_CHIPS * na,)), pltpu.SemaphoreType.DMA((N_CHIPS * na,))],
    )(*parts)


def _exchange_with_chips(sums):
    na = len(sums)

    def body(*refs):
        a_refs, out_refs = refs[:na], refs[na:2 * na]
        send_sems, recv_sems = refs[2 * na:]
        x, y, c = _coords()
        copies = []
        for a in range(na):
            for j, (cx, cy) in enumerate(_other_chips(x, y)):
                cp = pltpu.make_async_remote_copy(
                    src_ref=a_refs[a].at[2 * cx + cy], dst_ref=out_refs[a].at[j],
                    send_sem=send_sems.at[3 * a + j], recv_sem=recv_sems.at[3 * a + j],
                    device_id=(cx, cy, c), device_id_type=MESH)
                cp.start()
                copies.append(cp)
        for cp in copies:
            cp.wait()

    return pl.pallas_call(
        body, name="grad_exchange_chips",
        out_shape=[jax.ShapeDtypeStruct((3,) + t.shape[1:], t.dtype) for t in sums],
        in_specs=[HBM_SPEC] * na, out_specs=[HBM_SPEC] * na,
        scratch_shapes=[pltpu.SemaphoreType.DMA((3 * na,)), pltpu.SemaphoreType.DMA((3 * na,))],
    )(*sums)


SEM_SPEC = pl.BlockSpec(memory_space=pltpu.SEMAPHORE)
ANY_SPEC = pl.BlockSpec(memory_space=pl.ANY)
SIDE_EFFECT = pltpu.SideEffectType.DATAFLOW_SIDE_EFFECTING


def _peer(x, y, c, d):
    flip = lambda v, bit: 1 - v if bit else v
    p = (flip(x, d & 4), flip(y, d & 2), flip(c, d & 1))
    return p, 4 * p[0] + 2 * p[1] + p[2]


def _direct_copies(gather, src_refs, land_refs, send_sems, recv_sems):
    x, y, c = _coords()
    me = 4 * x + 2 * y + c
    copies = []
    for a in range(len(src_refs)):
        for d in range(1, N_DEV):
            peer, peer_slot = _peer(x, y, c, d)
            copies.append(pltpu.make_async_remote_copy(
                src_ref=src_refs[a] if gather else src_refs[a].at[peer_slot],
                dst_ref=land_refs[a].at[me] if gather else land_refs[a].at[d - 1],
                send_sem=send_sems.at[7 * a + d - 1], recv_sem=recv_sems.at[7 * a + d - 1],
                device_id=peer, device_id_type=MESH))
    return copies


def _hbm(t):
    return pltpu.HBM(t.shape, t.dtype)


def _direct_start(gather, arrays, through, name):
    na = len(arrays)
    lands = [lax.empty((N_DEV,) + t.shape if gather else (N_DEV - 1,) + t.shape[1:], t.dtype) for t in arrays]
    n_io = 2 * na + 1

    def body(*refs):
        for cp in _direct_copies(gather, refs[:na], refs[na:2 * na], refs[n_io], refs[n_io + 1]):
            cp.start()

    ins = [pltpu.with_memory_space_constraint(t, pltpu.HBM) for t in (*arrays, *lands, through)]
    sems = pltpu.SemaphoreType.DMA((7 * na,))
    res = pl.pallas_call(
        body, name=name, out_shape=(sems, sems, *[_hbm(t) for t in ins]),
        in_specs=[HBM_SPEC] * n_io, out_specs=(SEM_SPEC, SEM_SPEC, *[HBM_SPEC] * n_io),
        input_output_aliases={i: 2 + i for i in range(n_io)},
        compiler_params=pltpu.CompilerParams(has_side_effects=SIDE_EFFECT),
    )(*ins)
    return (res[0], res[1], list(res[2:2 + na]), list(res[2 + na:2 + 2 * na])), res[2 + 2 * na]


def _direct_wait(gather, started, after, name):
    send_sems, recv_sems, arrays, lands = started
    na = len(arrays)

    def body(*refs):
        for cp in _direct_copies(gather, refs[:na], refs[na:2 * na], refs[2 * na], refs[2 * na + 1]):
            cp.wait_send()
            cp.wait_recv()

    res = pl.pallas_call(
        body, name=name, out_shape=tuple(_hbm(t) for t in (*arrays, *lands)),
        in_specs=[HBM_SPEC] * (2 * na) + [SEM_SPEC, SEM_SPEC, ANY_SPEC], out_specs=tuple([HBM_SPEC] * (2 * na)),
        input_output_aliases={i: i for i in range(2 * na)},
        compiler_params=pltpu.CompilerParams(has_side_effects=SIDE_EFFECT),
    )(*arrays, *lands, send_sems, recv_sems, after)
    return list(res[:na]), list(res[na:])


def _row_tile(rows):
    for t in (256, 176, 128):
        if rows % t == 0 and rows > t:
            return t
    return rows


def _add_sibling(core, parts, got, name):
    _, r, c = parts.shape
    tr = _row_tile(r)

    def body(core_ref, p_ref, g_ref, o_ref, low_ref):
        total = p_ref[...] + g_ref[...]
        o_ref[...] = total
        low_ref[...] = total.astype(low_ref.dtype)

    blk = pl.BlockSpec((None, tr, c), lambda i, j, core_ref: (i, j, 0))
    return pl.pallas_call(
        body, name=name,
        grid_spec=pltpu.PrefetchScalarGridSpec(
            num_scalar_prefetch=1, grid=(N_CHIPS, r // tr),
            in_specs=[pl.BlockSpec((None, tr, c), lambda i, j, core_ref: (2 * i + core_ref[0], j, 0)), blk],
            out_specs=[blk, blk]),
        out_shape=[jax.ShapeDtypeStruct((N_CHIPS, r, c), F32), jax.ShapeDtypeStruct((N_CHIPS, r, c), BF16)],
        compiler_params=_params(("parallel", "parallel")),
    )(core, parts, got)


def _adamw_math(g, w, m, v):
    m_new = ADAM_B1 * m + (1.0 - ADAM_B1) * g
    v_new = ADAM_B2 * v + (1.0 - ADAM_B2) * jnp.square(g)
    m_hat = m_new / (1.0 - ADAM_B1 ** ADAM_STEP)
    v_hat = v_new / (1.0 - ADAM_B2 ** ADAM_STEP)
    delta = -ADAM_LR * (m_hat / (jnp.sqrt(v_hat) + ADAM_EPS) + ADAM_WD * w)
    return delta, m_new, v_new


def _adamw_sharded(chip, sums, got, w, m, v, name):
    r, c = w.shape
    tr = _row_tile(r)
    n_got = got.shape[0]

    def body(chip_ref, s_ref, g_ref, w_ref, m_ref, v_ref, go_ref, d_ref, nm_ref, nv_ref):
        g = s_ref[...]
        for i in range(n_got):
            g = g + g_ref[i].astype(F32)
        delta, m_new, v_new = _adamw_math(g, w_ref[...], m_ref[...], v_ref[...])
        go_ref[...] = g
        d_ref[...] = delta
        nm_ref[...] = m_new
        nv_ref[...] = v_new

    blk = pl.BlockSpec((tr, c), lambda i, chip_ref: (i, 0))
    out = jax.ShapeDtypeStruct((r, c), F32)
    return pl.pallas_call(
        body, name=name,
        grid_spec=pltpu.PrefetchScalarGridSpec(
            num_scalar_prefetch=1, grid=(r // tr,),
            in_specs=[pl.BlockSpec((None, tr, c), lambda i, chip_ref: (chip_ref[0], i, 0)),
                      pl.BlockSpec((n_got, tr, c), lambda i, chip_ref: (0, i, 0)), blk, blk, blk],
            out_specs=[blk, blk, blk, blk]),
        out_shape=[out, out, out, out],
        compiler_params=_params(("parallel",)),
    )(chip, sums, got, w, m, v)


def _adamw_replicated(parts, w, m, v):
    p, r, c = parts.shape

    def body(p_ref, w_ref, m_ref, v_ref, g_ref, d_ref, nm_ref, nv_ref):
        g = p_ref[0]
        for i in range(1, p):
            g = g + p_ref[i]
        delta, m_new, v_new = _adamw_math(g, w_ref[...], m_ref[...], v_ref[...])
        g_ref[...] = g
        d_ref[...] = delta
        nm_ref[...] = m_new
        nv_ref[...] = v_new

    blk = pl.BlockSpec((r, c), lambda i: (0, 0))
    out = jax.ShapeDtypeStruct((r, c), F32)
    return pl.pallas_call(
        body, name="adamw_replicated", grid=(1,),
        in_specs=[pl.BlockSpec((p, r, c), lambda i: (0, 0, 0)), blk, blk, blk],
        out_specs=[blk, blk, blk, blk], out_shape=[out, out, out, out],
        compiler_params=_params(("arbitrary",)),
    )(parts, w, m, v)


SHARDED_NAMES = ("w_in", "ml_conv_w", "w_out", "ca_wq", "ca_wkv", "ca_wo", "ffn_w_up", "ffn_conv_w", "ffn_w_down")
SMALL_NAMES = ("b_in", "hg_lb_logits", "hg_norm_w", "ml_conv_b", "ml_norm_w", "ln1_g", "ln1_b",
               "ln2_g", "ln2_b", "ffn_conv_b", "ln3_g", "ln3_b")
WEIGHT_NAMES = ("w_in", "b_in", "hg_lb_logits", "hg_norm_w", "ml_conv_w", "ml_conv_b", "ml_norm_w", "w_out",
                "ln1_g", "ln1_b", "ca_wq", "ca_wkv", "ca_wo", "ln2_g", "ln2_b", "ffn_w_up", "ffn_conv_w",
                "ffn_conv_b", "ffn_w_down", "ln3_g", "ln3_b")
PAD_TO = {"w_in": W_IN_SHARD_P, "ffn_w_up": UP_SHARD_P, "ffn_conv_w": UP_SHARD_P}
SMALL_ROWS = 24
SMALL_W = D_MODEL


def _shard_2d(name, block):
    t = block[0]
    if name in PAD_TO:
        t = jnp.pad(t, ((0, 0), (0, PAD_TO[name] - t.shape[1])))
    return t


def _shard_like(name, t, like):
    return t[:, :like.shape[2]][None]


def _pad_cols(t, width):
    return jnp.pad(t, ((0, 0), (0, width - t.shape[1])))


FIRST_NAMES = ("w_in", "ml_conv_w")
FFN_NAMES = ("ffn_w_up", "ffn_w_down", "ffn_conv_w")
MID_NAMES = ("ca_wo", "ca_wq", "ca_wkv", "w_out")


def _first_weights(g, small):
    w = dict(small)
    w_in = jnp.concatenate([g["w_in"][j, :, :W_IN_SHARD] for j in range(N_DEV)], axis=1)
    w["w_in_main"] = w_in[:, :D_IN_MAIN]
    w["w_in_gate"] = _pad_cols(w_in[:, D_IN_MAIN:], LANES)
    w["b_in_main"] = small["b_in"][:, :D_IN_MAIN]
    w["b_in_gate"] = _pad_cols(small["b_in"][:, D_IN_MAIN:], LANES)
    w["ml_conv_w"] = jnp.transpose(g["ml_conv_w"], (1, 0, 2)).reshape(ML_CONV, 2 * D_GROUP)
    return w


def _mid_weights(g):
    w = {n: g[n].reshape(D_MODEL, D_MODEL) for n in ("w_out", "ca_wq", "ca_wo")}
    w["ca_wkv"] = g["ca_wkv"]
    return w


def _ffn_weights(g, small):
    w = {"ffn_w_up": g["ffn_w_up"]}
    down = g["ffn_w_down"].reshape(N_DEV // 2, UP_SHARD, D_MODEL)
    w["ffn_w_down"] = jnp.pad(down, ((0, 0), (0, UP_SHARD_P - UP_SHARD), (0, 0))).reshape(D_FF_P, D_MODEL)
    w["ffn_conv_w"] = jnp.transpose(g["ffn_conv_w"], (1, 0, 2)).reshape(FFN_CONV, D_UP_P)
    w["ffn_conv_b"] = _pad_cols(small["ffn_conv_b"].reshape(N_DEV, UP_SHARD), UP_SHARD_P).reshape(1, D_UP_P)
    return w


def _whole_weights(g, small):
    return {**_first_weights(g, small), **_mid_weights(g), **_ffn_weights(g, small)}


def _owner_stack(n, grads):
    if n == "w_in":
        w_in = jnp.concatenate([grads["w_in_main"], grads["w_in_gate"][:, :D_IN - D_IN_MAIN]], axis=1)
        return jnp.stack([_pad_cols(w_in[:, j * W_IN_SHARD:(j + 1) * W_IN_SHARD], W_IN_SHARD_P)
                          for j in range(N_DEV)])
    if n in ("w_out", "ca_wq", "ca_wo"):
        return grads[n].reshape(N_DEV, D_MODEL // N_DEV, D_MODEL)
    if n == "ffn_w_down":
        down = grads[n].reshape(N_DEV // 2, UP_SHARD_P, D_MODEL)[:, :UP_SHARD]
        return down.reshape(N_DEV, D_FF // N_DEV, D_MODEL)
    if n == "ml_conv_w":
        return jnp.transpose(grads[n].reshape(ML_CONV, N_DEV, LANES), (1, 0, 2))
    if n == "ffn_conv_w":
        return jnp.transpose(grads[n].reshape(FFN_CONV, N_DEV, UP_SHARD_P), (1, 0, 2))
    return grads[n]


def _owner_stacks(grads):
    return {n: _owner_stack(n, grads) for n in SHARDED_NAMES}


def _small_grads(grads):
    out = {n: grads[n] for n in SMALL_NAMES if n in grads}
    out["b_in"] = jnp.concatenate([grads["b_in_main"], grads["b_in_gate"][:, :D_IN - D_IN_MAIN]], axis=1)
    out["ffn_conv_b"] = grads["ffn_conv_b"].reshape(N_DEV, UP_SHARD_P)[:, :UP_SHARD].reshape(1, D_UP)
    return out


def _pack_small(p, extra=None):
    flat = [p[n].reshape(-1) for n in SMALL_NAMES]
    if extra is not None:
        flat.append(extra.reshape(-1))
    flat = jnp.concatenate(flat)
    return jnp.pad(flat, (0, SMALL_ROWS * SMALL_W - flat.shape[0])).reshape(SMALL_ROWS, SMALL_W)


def _unpack_small(slab, like):
    out = {}
    flat = slab.reshape(-1)
    o = 0
    for n in SMALL_NAMES:
        out[n] = flat[o:o + like[n].size].reshape(like[n].shape)
        o += like[n].size
    return out, flat[o]


def kernel(x, mem, w_in, b_in, hg_lb_logits, hg_norm_w, ml_conv_w, ml_conv_b, ml_norm_w, w_out, ln1_g, ln1_b, ca_wq, ca_wkv, ca_wo, ln2_g, ln2_b, ffn_w_up, ffn_conv_w, ffn_conv_b, ffn_w_down, ln3_g, ln3_b, loss_target, m_w_in, m_b_in, m_hg_lb_logits, m_hg_norm_w, m_ml_conv_w, m_ml_conv_b, m_ml_norm_w, m_w_out, m_ln1_g, m_ln1_b, m_ca_wq, m_ca_wkv, m_ca_wo, m_ln2_g, m_ln2_b, m_ffn_w_up, m_ffn_conv_w, m_ffn_conv_b, m_ffn_w_down, m_ln3_g, m_ln3_b, v_w_in, v_b_in, v_hg_lb_logits, v_hg_norm_w, v_ml_conv_w, v_ml_conv_b, v_ml_norm_w, v_w_out, v_ln1_g, v_ln1_b, v_ca_wq, v_ca_wkv, v_ca_wo, v_ln2_g, v_ln2_b, v_ffn_w_up, v_ffn_conv_w, v_ffn_conv_b, v_ffn_w_down, v_ln3_g, v_ln3_b):
    params = dict(w_in=w_in, b_in=b_in, hg_lb_logits=hg_lb_logits, hg_norm_w=hg_norm_w, ml_conv_w=ml_conv_w,
                  ml_conv_b=ml_conv_b, ml_norm_w=ml_norm_w, w_out=w_out, ln1_g=ln1_g, ln1_b=ln1_b, ca_wq=ca_wq,
                  ca_wkv=ca_wkv, ca_wo=ca_wo, ln2_g=ln2_g, ln2_b=ln2_b, ffn_w_up=ffn_w_up, ffn_conv_w=ffn_conv_w,
                  ffn_conv_b=ffn_conv_b, ffn_w_down=ffn_w_down, ln3_g=ln3_g, ln3_b=ln3_b)
    mom1 = dict(w_in=m_w_in, b_in=m_b_in, hg_lb_logits=m_hg_lb_logits, hg_norm_w=m_hg_norm_w,
                ml_conv_w=m_ml_conv_w, ml_conv_b=m_ml_conv_b, ml_norm_w=m_ml_norm_w, w_out=m_w_out, ln1_g=m_ln1_g,
                ln1_b=m_ln1_b, ca_wq=m_ca_wq, ca_wkv=m_ca_wkv, ca_wo=m_ca_wo, ln2_g=m_ln2_g, ln2_b=m_ln2_b,
                ffn_w_up=m_ffn_w_up, ffn_conv_w=m_ffn_conv_w, ffn_conv_b=m_ffn_conv_b, ffn_w_down=m_ffn_w_down,
                ln3_g=m_ln3_g, ln3_b=m_ln3_b)
    mom2 = dict(w_in=v_w_in, b_in=v_b_in, hg_lb_logits=v_hg_lb_logits, hg_norm_w=v_hg_norm_w,
                ml_conv_w=v_ml_conv_w, ml_conv_b=v_ml_conv_b, ml_norm_w=v_ml_norm_w, w_out=v_w_out, ln1_g=v_ln1_g,
                ln1_b=v_ln1_b, ca_wq=v_ca_wq, ca_wkv=v_ca_wkv, ca_wo=v_ca_wo, ln2_g=v_ln2_g, ln2_b=v_ln2_b,
                ffn_w_up=v_ffn_w_up, ffn_conv_w=v_ffn_conv_w, ffn_conv_b=v_ffn_conv_b, ffn_w_down=v_ffn_w_down,
                ln3_g=v_ln3_g, ln3_b=v_ln3_b)

    x_idx, y_idx, c_idx = _coords()
    as_index = lambda v: jnp.reshape(v, (1,)).astype(jnp.int32)
    core, chip, me = as_index(c_idx), as_index(2 * x_idx + y_idx), as_index(4 * x_idx + 2 * y_idx + c_idx)
    small_params = {n: params[n] for n in SMALL_NAMES}

    shards = {n: _shard_2d(n, params[n]) for n in SHARDED_NAMES}
    to_send = lambda names: [shards[n] if "conv" in n else shards[n].astype(BF16) for n in names]
    first = dict(zip(FIRST_NAMES, _all_gather_two_level(to_send(FIRST_NAMES), "weights_gather_first")))
    mid_started, through = _direct_start(True, to_send(MID_NAMES), first["w_in"], "weights_gather_start_mid")
    ffn_started, first["w_in"] = _direct_start(True, to_send(FFN_NAMES), through, "weights_gather_start_ffn")

    def gathered_weights(names, started, after, tag):
        mine, lands = _direct_wait(True, started, after, "weights_gather_wait_" + tag)
        return {n: lax.dynamic_update_index_in_dim(land, own, me[0], 0) for n, own, land in zip(names, mine, lands)}

    started, own_stacks = {}, {}

    def start_group(names, tag):
        def hook(grads, through):
            own_stacks[tag] = [_owner_stack(n, grads) for n in names]
            started[tag], through = _direct_start(False, [t.astype(BF16) for t in own_stacks[tag]], through,
                                                  "grads_start_" + tag)
            return through
        return hook

    loss, grad_x, grads = _local_step(
        x[0], mem[0], loss_target[0], _first_weights(first, small_params),
        lambda y: _mid_weights(gathered_weights(MID_NAMES, mid_started, y, "mid")),
        lambda x2: _ffn_weights(gathered_weights(FFN_NAMES, ffn_started, x2, "ffn"), small_params),
        start_group(FFN_NAMES, "ffn"), start_group(MID_NAMES, "mid"))

    sharded_out = {}

    def adamw(n, index, own, got):
        res = _adamw_sharded(index, own, got, shards[n], _shard_2d(n, mom1[n]), _shard_2d(n, mom2[n]), "adamw_" + n)
        sharded_out[n] = [_shard_like(n, t, params[n]) for t in res]

    stacks = [_owner_stack(n, grads) for n in FIRST_NAMES]
    from_sibling = _exchange_with_sibling(stacks)
    chip_sums = [_add_sibling(core, st, got, "grad_add_" + n) for n, st, got in zip(FIRST_NAMES, stacks, from_sibling)]
    from_chips = _exchange_with_chips([low for _, low in chip_sums])
    for n, (sums, _), got in zip(FIRST_NAMES, chip_sums, from_chips):
        adamw(n, chip, sums, got)
    for names, tag in ((FFN_NAMES, "ffn"), (MID_NAMES, "mid")):
        _, lands = _direct_wait(False, started[tag], grad_x, "grads_wait_" + tag)
        for n, st, land in zip(names, own_stacks[tag], lands):
            adamw(n, me, st, land)
    small_parts = _all_gather_direct(_pack_small(_small_grads(grads), loss), "small_all_gather")
    small_res = _adamw_replicated(small_parts, _pack_small(params), _pack_small(mom1), _pack_small(mom2))

    outs = []
    total_loss = None
    for k in range(4):
        small, extra = _unpack_small(small_res[k], params)
        if total_loss is None:
            total_loss = extra
        outs.extend(sharded_out[n][k] if n in sharded_out else small[n] for n in WEIGHT_NAMES)
    return (total_loss, grad_x[None], *outs)
```

```python
import jax
import jax.numpy as jnp
from jax import lax
from jax.experimental import pallas as pl
from jax.experimental.pallas import tpu as pltpu

F32 = jnp.float32
BF16 = jnp.bfloat16
HIGHEST = lax.Precision.HIGHEST
MESH = pl.DeviceIdType.MESH

N_DEV = 8
D_MODEL = 1024
N_MEM = 256
N_HEADS = 4
D_HEAD = 128
D_GROUP = N_HEADS * D_HEAD
CHUNK = 64
ML_CONV = 4
FFN_CONV = 3
D_FF = 2816
D_UP = 2 * D_FF
CA_HEADS = 4
CA_DH = D_MODEL // CA_HEADS
LANES = 128
SUBLANES = 8
D_IN = 8 * D_GROUP + 2 * N_HEADS
D_IN_MAIN = 8 * D_GROUP
W_IN_SHARD = D_IN // N_DEV
W_IN_SHARD_P = 640
UP_SHARD = D_UP // N_DEV
UP_SHARD_P = 768
D_UP_P = N_DEV * UP_SHARD_P
D_FF_P = D_UP_P // 2
ALPHA = 2.0 ** 0.25
LN_EPS = 1e-5
NEG_BIG = -1e30
ADAM_LR = 0.001
ADAM_B1 = 0.9
ADAM_B2 = 0.999
ADAM_EPS = 1e-08
ADAM_WD = 0.01
ADAM_STEP = 10
VMEM_LIMIT = 56 * 1024 * 1024

SEG_HQ, SEG_HF, SEG_HI, SEG_HG, SEG_MQ, SEG_MK, SEG_MV, SEG_MO = (4 * i for i in range(8))


def _params(sem):
    return pltpu.CompilerParams(dimension_semantics=sem, vmem_limit_bytes=VMEM_LIMIT)


def _dg(a, b, ca, cb, precision=None):
    return lax.dot_general(a, b, (((ca,), (cb,)), ((), ())), precision=precision,
                           preferred_element_type=F32)


def _nn_raw(a, b):
    return _dg(a.astype(BF16), b.astype(BF16), 1, 0)


def _nt_raw(a, b):
    return _dg(a.astype(BF16), b.astype(BF16), 1, 1)


def _tn_raw(a, b):
    return _dg(a.astype(BF16), b.astype(BF16), 0, 0)


@jax.custom_vjp
def _nn(a, b):
    return _nn_raw(a, b)


_nn.defvjp(lambda a, b: (_nn_raw(a, b), (a, b)),
           lambda res, g: (_nt_raw(g, res[1]), _tn_raw(res[0], g)))


@jax.custom_vjp
def _nt(a, b):
    return _nt_raw(a, b)


_nt.defvjp(lambda a, b: (_nt_raw(a, b), (a, b)),
           lambda res, g: (_nn_raw(g, res[1]), _tn_raw(g, res[0])))


@jax.custom_vjp
def _tn(a, b):
    return _tn_raw(a, b)


_tn.defvjp(lambda a, b: (_tn_raw(a, b), (a, b)),
           lambda res, g: (_nt_raw(res[1], g), _nn_raw(res[0], g)))


def _layer_norm(z, g, b):
    mu = jnp.mean(z, axis=-1, keepdims=True)
    var = jnp.mean(jnp.square(z - mu), axis=-1, keepdims=True)
    return (z - mu) * lax.rsqrt(var + LN_EPS) * g + b


def _matmul_nn(a, w, bias, tm, tn, name):
    m, k = a.shape
    if w.ndim == 3:
        n = w.shape[0] * w.shape[2]
        assert tn == w.shape[2]
        w_spec = pl.BlockSpec((None, k, tn), lambda i, j: (j, 0, 0))
    else:
        n = w.shape[1]
        w_spec = pl.BlockSpec((k, tn), lambda i, j: (0, j))

    def body(*refs):
        a_ref, w_ref = refs[0], refs[1]
        o_ref = refs[-1]
        acc = _nn_raw(a_ref[...], w_ref[...])
        if bias is not None:
            acc = acc + refs[2][...]
        o_ref[...] = acc

    in_specs = [pl.BlockSpec((tm, k), lambda i, j: (i, 0)), w_spec]
    args = [a, w]
    if bias is not None:
        in_specs.append(pl.BlockSpec((1, tn), lambda i, j: (0, j)))
        args.append(bias)
    return pl.pallas_call(
        body, name=name, grid=(m // tm, n // tn), in_specs=in_specs,
        out_specs=pl.BlockSpec((tm, tn), lambda i, j: (i, j)),
        out_shape=jax.ShapeDtypeStruct((m, n), F32),
        compiler_params=_params(("parallel", "parallel")),
    )(*args)


def _matmul_nt(pairs, add, scale, tm, tk, name):
    m = pairs[0][0].shape[0]
    k = pairs[0][1].shape[-2]
    groups = []
    in_specs, args = [], []
    for pair in pairs:
        d, w = pair[0], pair[1]
        in_specs.append(pl.BlockSpec((tm, d.shape[1]), lambda i, j: (i, 0)))
        if w.ndim == 3:
            g = d.shape[1] // w.shape[2]
            blk = pair[2] // g
            in_specs.append(pl.BlockSpec((g, tk, w.shape[2]), lambda i, j, blk=blk: (blk, j, 0)))
            groups.append((g, w.shape[2]))
        else:
            in_specs.append(pl.BlockSpec((tk, w.shape[1]), lambda i, j: (j, 0)))
            groups.append(None)
        args += [d, w]
    if add is not None:
        in_specs.append(pl.BlockSpec((tm, tk), lambda i, j: (i, j)))
        args.append(add)

    def body(*refs):
        o_ref = refs[-1]
        acc = None
        for p, grp in enumerate(groups):
            d_ref, w_ref = refs[2 * p], refs[2 * p + 1]
            if grp is None:
                terms = [_nt_raw(d_ref[...], w_ref[...])]
            else:
                terms = [_nt_raw(d_ref[:, g * grp[1]:(g + 1) * grp[1]], w_ref[g]) for g in range(grp[0])]
            for t in terms:
                acc = t if acc is None else acc + t
        if add is not None:
            acc = acc + scale * refs[2 * len(groups)][...]
        o_ref[...] = acc

    return pl.pallas_call(
        body, name=name, grid=(m // tm, k // tk), in_specs=in_specs,
        out_specs=pl.BlockSpec((tm, tk), lambda i, j: (i, j)),
        out_shape=jax.ShapeDtypeStruct((m, k), F32),
        compiler_params=_params(("parallel", "parallel")),
    )(*args)


def _matmul_tn(a, b, tm, tn, tt, name, shards=None, shard0=0, group=1, into=None, colsum=False):
    t, m = a.shape
    n = b.shape[1]
    assert not colsum or tm == m
    n_in = 2 + (into is not None)
    per_step = 1 if shards is None else group
    width = per_step * tn

    def body(*refs):
        a_ref, b_ref = refs[0], refs[1]
        o_ref = refs[n_in]
        first = pl.program_id(2) == 0

        @pl.when(first)
        def _():
            o_ref[...] = jnp.zeros_like(o_ref)

        if shards is None:
            o_ref[...] += _tn_raw(a_ref[...], b_ref[...])
        else:
            lhs = a_ref[...].astype(BF16)
            for g in range(per_step):
                o_ref[g] += _tn_raw(lhs, b_ref[:, g * tn:(g + 1) * tn])
        if colsum:
            s_ref = refs[n_in + 1]

            @pl.when(first)
            def _():
                s_ref[...] = jnp.zeros_like(s_ref)

            s_ref[...] += jnp.sum(b_ref[...], axis=0, keepdims=True)

    in_specs = [pl.BlockSpec((tt, tm), lambda i, j, kk: (kk, i)),
                pl.BlockSpec((tt, width), lambda i, j, kk: (kk, j))]
    args = [a, b]
    aliases = {}
    if into is not None:
        in_specs.append(pl.BlockSpec(memory_space=pl.ANY))
        args.append(into)
        aliases = {2: 0}
    if shards is None:
        out_specs = [pl.BlockSpec((tm, tn), lambda i, j, kk: (i, j))]
        out_shape = [jax.ShapeDtypeStruct((m, n), F32)]
    else:
        out_specs = [pl.BlockSpec((per_step, tm, tn), lambda i, j, kk: (shard0 // per_step + j, i, 0))]
        out_shape = [jax.ShapeDtypeStruct((shards, m, tn), F32)]
    if colsum:
        out_specs.append(pl.BlockSpec((1, tn), lambda i, j, kk: (0, j)))
        out_shape.append(jax.ShapeDtypeStruct((1, n), F32))
    res = pl.pallas_call(
        body, name=name, grid=(m // tm, n // width, t // tt), in_specs=in_specs, out_specs=out_specs,
        out_shape=out_shape, input_output_aliases=aliases,
        compiler_params=_params(("parallel", "parallel", "arbitrary")),
    )(*args)
    return res if colsum else res[0]


ROW_TILE = 64


def _conv_fwd_tile(pad_ref, w_ref, b_ref, r0, rows, taps):
    acc = b_ref[...]
    for j in range(taps):
        acc = acc + pad_ref[pl.ds(SUBLANES - (taps - 1 - j) + r0, rows), :] * w_ref[j:j + 1, :]
    return acc


def _conv_bwd_tile(dpad_ref, w_ref, r0, rows, taps):
    acc = None
    for j in range(taps):
        term = dpad_ref[pl.ds(r0 + (taps - 1 - j), rows), :] * w_ref[j:j + 1, :]
        acc = term if acc is None else acc + term
    return acc


def _conv_grads_tile(pad_ref, dpad_ref, dx_ref, w_ref, dws, r0, rows, taps):
    dx = _conv_bwd_tile(dpad_ref, w_ref, r0, rows, taps)
    dx_ref[r0:r0 + rows, :] = dx.astype(dx_ref.dtype)
    d_pre = dpad_ref[r0:r0 + rows, :]
    for j in range(taps):
        xs = pad_ref[pl.ds(SUBLANES - (taps - 1 - j) + r0, rows), :]
        dws[j] = dws[j] + jnp.sum(d_pre * xs, axis=0, keepdims=True)
    return jnp.sum(dx, axis=0, keepdims=True)


def _ml_conv_fwd(proj, conv_w, conv_b):
    s = proj.shape[0]
    nblk = 2 * D_GROUP // LANES

    def body(x_ref, w_ref, b_ref, o_ref, pad_ref):
        pad_ref[0:SUBLANES, :] = jnp.zeros((SUBLANES, LANES), F32)
        pad_ref[SUBLANES:, :] = x_ref[...]
        for r0 in range(0, s, ROW_TILE):
            rows = min(ROW_TILE, s - r0)
            o_ref[r0:r0 + rows, :] = jax.nn.silu(_conv_fwd_tile(pad_ref, w_ref, b_ref, r0, rows, ML_CONV))

    return pl.pallas_call(
        body, name="ml_conv_fwd", grid=(nblk,),
        in_specs=[pl.BlockSpec((s, LANES), lambda j: (0, SEG_MQ + j)),
                  pl.BlockSpec((ML_CONV, LANES), lambda j: (0, j)),
                  pl.BlockSpec((1, LANES), lambda j: (0, j))],
        out_specs=pl.BlockSpec((s, LANES), lambda j: (0, j)),
        out_shape=jax.ShapeDtypeStruct((s, 2 * D_GROUP), F32),
        scratch_shapes=[pltpu.VMEM((s + SUBLANES, LANES), F32)],
        compiler_params=_params(("parallel",)),
    )(proj, conv_w, conv_b)


def _ml_conv_bwd(proj, conv_w, conv_b, d_qk, d_proj):
    s = proj.shape[0]
    nblk = 2 * D_GROUP // LANES

    def body(x_ref, w_ref, b_ref, dy_ref, _, dx_ref, dw_ref, db_ref, dxs_ref, pad_ref, dpad_ref):
        pad_ref[0:SUBLANES, :] = jnp.zeros((SUBLANES, LANES), F32)
        pad_ref[SUBLANES:, :] = x_ref[...]
        dpad_ref[s:, :] = jnp.zeros((SUBLANES, LANES), F32)
        db = jnp.zeros((1, LANES), F32)
        for r0 in range(0, s, ROW_TILE):
            rows = min(ROW_TILE, s - r0)
            pre = _conv_fwd_tile(pad_ref, w_ref, b_ref, r0, rows, ML_CONV)
            _, vjp = jax.vjp(jax.nn.silu, pre)
            d_pre, = vjp(dy_ref[r0:r0 + rows, :])
            dpad_ref[r0:r0 + rows, :] = d_pre
            db = db + jnp.sum(d_pre, axis=0, keepdims=True)
        db_ref[...] = db
        dws = [jnp.zeros((1, LANES), F32) for _ in range(ML_CONV)]
        dx_sum = jnp.zeros((1, LANES), F32)
        for r0 in range(0, s, ROW_TILE):
            dx_sum = dx_sum + _conv_grads_tile(pad_ref, dpad_ref, dx_ref, w_ref, dws, r0, min(ROW_TILE, s - r0),
                                               ML_CONV)
        dxs_ref[...] = dx_sum
        for j in range(ML_CONV):
            dw_ref[j:j + 1, :] = dws[j]

    return pl.pallas_call(
        body, name="ml_conv_bwd", grid=(nblk,),
        in_specs=[pl.BlockSpec((s, LANES), lambda j: (0, SEG_MQ + j)),
                  pl.BlockSpec((ML_CONV, LANES), lambda j: (0, j)),
                  pl.BlockSpec((1, LANES), lambda j: (0, j)),
                  pl.BlockSpec((s, LANES), lambda j: (0, j)),
                  pl.BlockSpec(memory_space=pl.ANY)],
        out_specs=[pl.BlockSpec((s, LANES), lambda j: (0, SEG_MQ + j)),
                   pl.BlockSpec((ML_CONV, LANES), lambda j: (0, j)),
                   pl.BlockSpec((1, LANES), lambda j: (0, j)),
                   pl.BlockSpec((1, LANES), lambda j: (0, j))],
        out_shape=[jax.ShapeDtypeStruct(d_proj.shape, d_proj.dtype),
                   jax.ShapeDtypeStruct((ML_CONV, 2 * D_GROUP), F32),
                   jax.ShapeDtypeStruct((1, 2 * D_GROUP), F32),
                   jax.ShapeDtypeStruct((1, 2 * D_GROUP), F32)],
        input_output_aliases={4: 0},
        scratch_shapes=[pltpu.VMEM((s + SUBLANES, LANES), F32), pltpu.VMEM((s + SUBLANES, LANES), F32)],
        compiler_params=_params(("parallel",)),
    )(proj, conv_w, conv_b, d_qk, d_proj)


def _gelu_mul(a, b):
    return jax.nn.gelu(a) * b


FFN_BLOCKS = D_FF_P // LANES


def _ffn_conv_fwd(u, conv_w, conv_b):
    s = u.shape[0]

    def body(g_ref, v_ref, wg_ref, wv_ref, bg_ref, bv_ref, o_ref, gpad_ref, vpad_ref):
        for pad_ref, x_ref in ((gpad_ref, g_ref), (vpad_ref, v_ref)):
            pad_ref[0:SUBLANES, :] = jnp.zeros((SUBLANES, LANES), F32)
            pad_ref[SUBLANES:, :] = x_ref[...]
        for r0 in range(0, s, ROW_TILE):
            rows = min(ROW_TILE, s - r0)
            ug = _conv_fwd_tile(gpad_ref, wg_ref, bg_ref, r0, rows, FFN_CONV)
            uv = _conv_fwd_tile(vpad_ref, wv_ref, bv_ref, r0, rows, FFN_CONV)
            o_ref[r0:r0 + rows, :] = _gelu_mul(ug, uv).astype(o_ref.dtype)

    col = lambda off: (lambda j: (0, off + j))
    return pl.pallas_call(
        body, name="ffn_conv_fwd", grid=(FFN_BLOCKS,),
        in_specs=[pl.BlockSpec((s, LANES), col(0)), pl.BlockSpec((s, LANES), col(FFN_BLOCKS)),
                  pl.BlockSpec((FFN_CONV, LANES), col(0)), pl.BlockSpec((FFN_CONV, LANES), col(FFN_BLOCKS)),
                  pl.BlockSpec((1, LANES), col(0)), pl.BlockSpec((1, LANES), col(FFN_BLOCKS))],
        out_specs=pl.BlockSpec((s, LANES), col(0)),
        out_shape=jax.ShapeDtypeStruct((s, D_FF_P), BF16),
        scratch_shapes=[pltpu.VMEM((s + SUBLANES, LANES), F32), pltpu.VMEM((s + SUBLANES, LANES), F32)],
        compiler_params=_params(("parallel",)),
    )(u, u, conv_w, conv_w, conv_b, conv_b)


def _ffn_conv_bwd(u, conv_w, conv_b, d_h):
    s = u.shape[0]

    def body(g_ref, v_ref, wg_ref, wv_ref, bg_ref, bv_ref, dh_ref,
             dug_ref, duv_ref, dwg_ref, dwv_ref, dbg_ref, dbv_ref,
             gpad_ref, vpad_ref, dgpad_ref, dvpad_ref):
        for pad_ref, x_ref in ((gpad_ref, g_ref), (vpad_ref, v_ref)):
            pad_ref[0:SUBLANES, :] = jnp.zeros((SUBLANES, LANES), F32)
            pad_ref[SUBLANES:, :] = x_ref[...]
        dgpad_ref[s:, :] = jnp.zeros((SUBLANES, LANES), F32)
        dvpad_ref[s:, :] = jnp.zeros((SUBLANES, LANES), F32)
        dbg = jnp.zeros((1, LANES), F32)
        dbv = jnp.zeros((1, LANES), F32)
        for r0 in range(0, s, ROW_TILE):
            rows = min(ROW_TILE, s - r0)
            ug = _conv_fwd_tile(gpad_ref, wg_ref, bg_ref, r0, rows, FFN_CONV)
            uv = _conv_fwd_tile(vpad_ref, wv_ref, bv_ref, r0, rows, FFN_CONV)
            _, vjp = jax.vjp(_gelu_mul, ug, uv)
            d_ug, d_uv = vjp(dh_ref[r0:r0 + rows, :])
            dgpad_ref[r0:r0 + rows, :] = d_ug
            dvpad_ref[r0:r0 + rows, :] = d_uv
            dbg = dbg + jnp.sum(d_ug, axis=0, keepdims=True)
            dbv = dbv + jnp.sum(d_uv, axis=0, keepdims=True)
        dbg_ref[...] = dbg
        dbv_ref[...] = dbv
        for pad_ref, dpad_ref, w_ref, dx_ref, dw_ref in ((gpad_ref, dgpad_ref, wg_ref, dug_ref, dwg_ref),
                                                         (vpad_ref, dvpad_ref, wv_ref, duv_ref, dwv_ref)):
            dws = [jnp.zeros((1, LANES), F32) for _ in range(FFN_CONV)]
            for r0 in range(0, s, ROW_TILE):
                _conv_grads_tile(pad_ref, dpad_ref, dx_ref, w_ref, dws, r0, min(ROW_TILE, s - r0), FFN_CONV)
            for j in range(FFN_CONV):
                dw_ref[j:j + 1, :] = dws[j]

    col = lambda off: (lambda j: (0, off + j))
    seq = pl.BlockSpec((s, LANES), col(0))
    return pl.pallas_call(
        body, name="ffn_conv_bwd", grid=(FFN_BLOCKS,),
        in_specs=[pl.BlockSpec((s, LANES), col(0)), pl.BlockSpec((s, LANES), col(FFN_BLOCKS)),
                  pl.BlockSpec((FFN_CONV, LANES), col(0)), pl.BlockSpec((FFN_CONV, LANES), col(FFN_BLOCKS)),
                  pl.BlockSpec((1, LANES), col(0)), pl.BlockSpec((1, LANES), col(FFN_BLOCKS)), seq],
        out_specs=[seq, seq, pl.BlockSpec((FFN_CONV, LANES), col(0)), pl.BlockSpec((FFN_CONV, LANES), col(0)),
                   pl.BlockSpec((1, LANES), col(0)), pl.BlockSpec((1, LANES), col(0))],
        out_shape=[jax.ShapeDtypeStruct((s, D_FF_P), BF16), jax.ShapeDtypeStruct((s, D_FF_P), BF16),
                   jax.ShapeDtypeStruct((FFN_CONV, D_FF_P), F32), jax.ShapeDtypeStruct((FFN_CONV, D_FF_P), F32),
                   jax.ShapeDtypeStruct((1, D_FF_P), F32), jax.ShapeDtypeStruct((1, D_FF_P), F32)],
        scratch_shapes=[pltpu.VMEM((s + SUBLANES, LANES), F32) for _ in range(4)],
        compiler_params=_params(("parallel",)),
    )(u, u, conv_w, conv_w, conv_b, conv_b, d_h)


def _chunk_masks(c):
    row = lax.broadcasted_iota(jnp.int32, (c, c), 0)
    col = lax.broadcasted_iota(jnp.int32, (c, c), 1)
    return row, col


@jax.custom_vjp
def _split_heads(x):
    return tuple(x[:, h * D_HEAD:(h + 1) * D_HEAD] for h in range(N_HEADS))


_split_heads.defvjp(lambda x: (_split_heads(x), None), lambda _, gs: (jnp.concatenate(gs, axis=1),))


@jax.custom_vjp
def _merge_heads(xs):
    return jnp.concatenate(xs, axis=1)


_merge_heads.defvjp(lambda xs: (_merge_heads(xs), None), lambda _, g: (_split_heads(g),))

HEADS = range(N_HEADS)


def _hg_chunk(hq, hf, hi, hgate, l0, l1, nw, sts):
    c = hq.shape[0]
    row, col = _chunk_masks(c)
    mask = col <= row
    mx = lax.stop_gradient(jnp.maximum(l0, l1))
    e0 = jnp.exp(l0 - mx)
    e1 = jnp.exp(l1 - mx)
    lb = e0 / (e0 + e1)
    sig = jax.nn.sigmoid(hf)
    lf = jnp.log(lb + (1.0 - lb) * sig)
    k = (1.0 - lb) * jax.nn.sigmoid(-hf)
    q = jax.nn.silu(hq)
    b = _dg(mask.astype(F32), lf, 1, 0, HIGHEST)
    rid = lax.broadcasted_iota(jnp.int32, b.shape, 0)
    b_ref = jnp.sum(jnp.where(rid == c // 2 - 1, b, 0.0), axis=0, keepdims=True)
    b_last = jnp.sum(jnp.where(rid == c - 1, b, 0.0), axis=0, keepdims=True)
    qa = _split_heads(q * jnp.exp(b - b_ref))
    ka = _split_heads(k * jnp.exp(b_ref - b))
    qe = _split_heads(q * jnp.exp(b))
    kd = _split_heads(k * jnp.exp(b_last - b))
    decay = _split_heads(jnp.exp(b_last))
    v = _split_heads(hi)
    attn = [jnp.where(mask, _nt(qa[h], ka[h]), 0.0) for h in HEADS]
    intra = [_nn(attn[h], v[h]) for h in HEADS]
    inter = [_nt(qe[h], sts[h]) for h in HEADS]
    kv = [_tn(v[h], kd[h]) for h in HEADS]
    sts_new = tuple(decay[h] * sts[h] + kv[h] for h in HEADS)
    o = [intra[h] + inter[h] for h in HEADS]
    normed = _merge_heads(tuple(o[h] * lax.rsqrt(jnp.mean(o[h] * o[h], axis=-1, keepdims=True) + LN_EPS)
                                for h in HEADS))
    return normed * nw * jax.nn.silu(hgate), sts_new


def _seg(ref, seg):
    return ref[:, seg * D_GROUP:(seg + 1) * D_GROUP]


def _hgrn2_fwd(proj, logits, norm_w):
    s = proj.shape[0]
    nc = s // CHUNK

    def body(p_ref, lg_ref, nw_ref, y_ref, st_out_ref, st_scr):
        @pl.when(pl.program_id(0) == 0)
        def _():
            st_scr[...] = jnp.zeros_like(st_scr)

        sts = tuple(st_scr[h] for h in HEADS)
        y, sts_new = _hg_chunk(_seg(p_ref, 0), _seg(p_ref, 1), _seg(p_ref, 2), _seg(p_ref, 3),
                               lg_ref[0:1, :], lg_ref[1:2, :], nw_ref[...], sts)
        y_ref[...] = y.astype(y_ref.dtype)
        for h in HEADS:
            st_out_ref[h] = sts[h]
            st_scr[h] = sts_new[h]

    return pl.pallas_call(
        body, name="hgrn2_fwd", grid=(nc,),
        in_specs=[pl.BlockSpec((CHUNK, 4 * D_GROUP), lambda c: (c, 0)),
                  pl.BlockSpec((2, D_GROUP), lambda c: (0, 0)),
                  pl.BlockSpec((1, D_GROUP), lambda c: (0, 0))],
        out_specs=[pl.BlockSpec((CHUNK, D_GROUP), lambda c: (c, 0)),
                   pl.BlockSpec((None, N_HEADS, D_HEAD, D_HEAD), lambda c: (c, 0, 0, 0))],
        out_shape=[jax.ShapeDtypeStruct((s, 2 * D_GROUP), BF16),
                   jax.ShapeDtypeStruct((nc, N_HEADS, D_HEAD, D_HEAD), F32)],
        scratch_shapes=[pltpu.VMEM((N_HEADS, D_HEAD, D_HEAD), F32)],
        compiler_params=_params(("arbitrary",)),
    )(proj, logits, norm_w)


def _hgrn2_bwd(proj, logits, norm_w, states, d_y):
    s = proj.shape[0]
    nc = s // CHUNK

    def body(p_ref, lg_ref, nw_ref, st_ref, dy_ref, dp_ref, dl_ref, dnw_ref, dsum_ref, dst_scr):
        @pl.when(pl.program_id(0) == 0)
        def _():
            dst_scr[...] = jnp.zeros_like(dst_scr)
            dl_ref[...] = jnp.zeros_like(dl_ref)
            dnw_ref[...] = jnp.zeros_like(dnw_ref)
            dsum_ref[...] = jnp.zeros_like(dsum_ref)

        _, vjp = jax.vjp(_hg_chunk, _seg(p_ref, 0), _seg(p_ref, 1), _seg(p_ref, 2), _seg(p_ref, 3),
                         lg_ref[0:1, :], lg_ref[1:2, :], nw_ref[...], tuple(st_ref[h] for h in HEADS))
        d_hq, d_hf, d_hi, d_hg, d_l0, d_l1, d_nw, d_sts = vjp((dy_ref[...], tuple(dst_scr[h] for h in HEADS)))
        for seg, val in enumerate((d_hq, d_hf, d_hi, d_hg)):
            dp_ref[:, seg * D_GROUP:(seg + 1) * D_GROUP] = val.astype(dp_ref.dtype)
            dsum_ref[:, seg * D_GROUP:(seg + 1) * D_GROUP] += jnp.sum(val, axis=0, keepdims=True)
        dl_ref[0:1, :] += d_l0
        dl_ref[1:2, :] += d_l1
        dnw_ref[...] += d_nw
        for h in HEADS:
            dst_scr[h] = d_sts[h]

    rev = lambda c: nc - 1 - c
    return pl.pallas_call(
        body, name="hgrn2_bwd", grid=(nc,),
        in_specs=[pl.BlockSpec((CHUNK, 4 * D_GROUP), lambda c: (rev(c), 0)),
                  pl.BlockSpec((2, D_GROUP), lambda c: (0, 0)),
                  pl.BlockSpec((1, D_GROUP), lambda c: (0, 0)),
                  pl.BlockSpec((None, N_HEADS, D_HEAD, D_HEAD), lambda c: (rev(c), 0, 0, 0)),
                  pl.BlockSpec((CHUNK, D_GROUP), lambda c: (rev(c), 0))],
        out_specs=[pl.BlockSpec((CHUNK, 4 * D_GROUP), lambda c: (rev(c), 0)),
                   pl.BlockSpec((2, D_GROUP), lambda c: (0, 0)),
                   pl.BlockSpec((1, D_GROUP), lambda c: (0, 0)),
                   pl.BlockSpec((1, 4 * D_GROUP), lambda c: (0, 0))],
        out_shape=[jax.ShapeDtypeStruct((s, D_IN_MAIN), BF16), jax.ShapeDtypeStruct((2, D_GROUP), F32),
                   jax.ShapeDtypeStruct((1, D_GROUP), F32), jax.ShapeDtypeStruct((1, 4 * D_GROUP), F32)],
        scratch_shapes=[pltpu.VMEM((N_HEADS, D_HEAD, D_HEAD), F32)],
        compiler_params=_params(("arbitrary",)),
    )(proj, logits, norm_w, states, d_y)


def _gate_column(gates, lane, idx):
    return jnp.sum(jnp.where(lane == idx, gates, 0.0), axis=1, keepdims=True)


def _head_layer_norm(h):
    mu = jnp.mean(h, axis=-1, keepdims=True)
    var = jnp.mean(jnp.square(h - mu), axis=-1, keepdims=True)
    return (h - mu) * lax.rsqrt(var + LN_EPS)


def _ml_chunk(qc, kc, v, mo, gates, nw, cts, ns, ms):
    c = qc.shape[0]
    row, col = _chunk_masks(c)
    mask = col <= row
    eye = col == row
    lane = lax.broadcasted_iota(jnp.int32, gates.shape, 1)
    to_row = lambda t: jnp.sum(jnp.where(eye, t, 0.0), axis=0, keepdims=True)
    q = _split_heads(qc * (D_HEAD ** -0.5))
    k = _split_heads(kc)
    vs = _split_heads(v)
    ig = [_gate_column(gates, lane, h) for h in HEADS]
    lf = [jax.nn.log_sigmoid(_gate_column(gates, lane, N_HEADS + h)) for h in HEADS]
    lf_row = [to_row(lf[h]) for h in HEADS]
    ig_row = [to_row(ig[h]) for h in HEADS]
    b_col = [jnp.sum(jnp.where(mask, lf_row[h], 0.0), axis=1, keepdims=True) for h in HEADS]
    b_row = [jnp.sum(jnp.where(row <= col, lf[h], 0.0), axis=0, keepdims=True) for h in HEADS]
    g = [jnp.sum(lf[h], axis=0, keepdims=True) for h in HEADS]
    d = [jnp.where(mask, b_col[h] - b_row[h] + ig_row[h], -jnp.inf) for h in HEADS]
    inter = [b_col[h] + ms[h] for h in HEADS]
    m_t = [lax.stop_gradient(jnp.maximum(inter[h], jnp.max(d[h], axis=1, keepdims=True))) for h in HEADS]
    qk = [_nt(q[h], k[h]) for h in HEADS]
    qc_state = [_nt(q[h], cts[h]) for h in HEADS]
    sc = [qk[h] * jnp.exp(d[h] - m_t[h]) for h in HEADS]
    w_inter = [jnp.exp(inter[h] - m_t[h]) for h in HEADS]
    sv = [_nn(sc[h], vs[h]) for h in HEADS]
    num = [sv[h] + w_inter[h] * qc_state[h] for h in HEADS]
    den = [jnp.sum(sc[h], axis=1, keepdims=True) + w_inter[h] * jnp.sum(q[h] * ns[h], axis=1, keepdims=True)
           for h in HEADS]
    hh = [num[h] / jnp.maximum(jnp.abs(den[h]), jnp.exp(-m_t[h])) for h in HEADS]
    a = [g[h] - b_col[h] + ig[h] for h in HEADS]
    ms_new = tuple(lax.stop_gradient(jnp.maximum(g[h] + ms[h], jnp.max(a[h], axis=0, keepdims=True)))
                   for h in HEADS)
    decay = [jnp.exp(g[h] + ms[h] - ms_new[h]) for h in HEADS]
    wk = [k[h] * jnp.exp(a[h] - ms_new[h]) for h in HEADS]
    kv = [_tn(vs[h], wk[h]) for h in HEADS]
    cts_new = tuple(decay[h] * cts[h] + kv[h] for h in HEADS)
    ns_new = tuple(decay[h] * ns[h] + jnp.sum(wk[h], axis=0, keepdims=True) for h in HEADS)
    normed = _merge_heads(tuple(_head_layer_norm(hh[h]) for h in HEADS))
    return jax.nn.sigmoid(mo) * (normed * nw), cts_new, ns_new, ms_new


def _mlstm_fwd(qk, proj, gates, norm_w, y):
    s = proj.shape[0]
    nc = s // CHUNK

    def body(qk_ref, vo_ref, g_ref, nw_ref, _, y_ref, ct_out, n_out, m_out, ct_scr, n_scr, m_scr):
        @pl.when(pl.program_id(0) == 0)
        def _():
            ct_scr[...] = jnp.zeros_like(ct_scr)
            n_scr[...] = jnp.zeros_like(n_scr)
            m_scr[...] = jnp.full(m_scr.shape, NEG_BIG, F32)

        cts = tuple(ct_scr[h] for h in HEADS)
        ns = tuple(n_scr[h] for h in HEADS)
        ms = tuple(m_scr[h] for h in HEADS)
        y, cts_new, ns_new, ms_new = _ml_chunk(_seg(qk_ref, 0), _seg(qk_ref, 1), _seg(vo_ref, 0), _seg(vo_ref, 1),
                                               g_ref[...], nw_ref[...], cts, ns, ms)
        y_ref[...] = y.astype(y_ref.dtype)
        for h in HEADS:
            ct_out[h], n_out[h], m_out[h] = cts[h], ns[h], ms[h]
            ct_scr[h], n_scr[h], m_scr[h] = cts_new[h], ns_new[h], ms_new[h]

    st = lambda r, w: pl.BlockSpec((None, N_HEADS, r, w), lambda c: (c, 0, 0, 0))
    return pl.pallas_call(
        body, name="mlstm_fwd", grid=(nc,),
        in_specs=[pl.BlockSpec((CHUNK, 2 * D_GROUP), lambda c: (c, 0)),
                  pl.BlockSpec((CHUNK, 2 * D_GROUP), lambda c: (c, 3)),
                  pl.BlockSpec((CHUNK, LANES), lambda c: (c, 0)),
                  pl.BlockSpec((1, D_GROUP), lambda c: (0, 0)),
                  pl.BlockSpec(memory_space=pl.ANY)],
        out_specs=[pl.BlockSpec((CHUNK, D_GROUP), lambda c: (c, 1)),
                   st(D_HEAD, D_HEAD), st(1, D_HEAD), st(1, 1)],
        out_shape=[jax.ShapeDtypeStruct(y.shape, y.dtype),
                   jax.ShapeDtypeStruct((nc, N_HEADS, D_HEAD, D_HEAD), F32),
                   jax.ShapeDtypeStruct((nc, N_HEADS, 1, D_HEAD), F32),
                   jax.ShapeDtypeStruct((nc, N_HEADS, 1, 1), F32)],
        input_output_aliases={4: 0},
        scratch_shapes=[pltpu.VMEM((N_HEADS, D_HEAD, D_HEAD), F32), pltpu.VMEM((N_HEADS, 1, D_HEAD), F32),
                        pltpu.VMEM((N_HEADS, 1, 1), F32)],
        compiler_params=_params(("arbitrary",)),
    )(qk, proj, gates, norm_w, y)


def _mlstm_bwd(qk, proj, gates, norm_w, ct_s, n_s, m_s, d_y, d_proj):
    s = proj.shape[0]
    nc = s // CHUNK

    def body(qk_ref, vo_ref, g_ref, nw_ref, ct_ref, n_ref, m_ref, dy_ref, _,
             dp_ref, dqk_ref, dg_ref, dnw_ref, dsum_ref, dct_scr, dn_scr):
        @pl.when(pl.program_id(0) == 0)
        def _():
            dct_scr[...] = jnp.zeros_like(dct_scr)
            dn_scr[...] = jnp.zeros_like(dn_scr)
            dnw_ref[...] = jnp.zeros_like(dnw_ref)
            dsum_ref[...] = jnp.zeros_like(dsum_ref)

        ms = tuple(m_ref[h] for h in HEADS)
        step = lambda *a: _ml_chunk(*a, ms)[:3]
        _, vjp = jax.vjp(step, _seg(qk_ref, 0), _seg(qk_ref, 1), _seg(vo_ref, 0), _seg(vo_ref, 1), g_ref[...],
                         nw_ref[...], tuple(ct_ref[h] for h in HEADS), tuple(n_ref[h] for h in HEADS))
        d_q, d_k, d_v, d_o, d_gates, d_nw, d_cts, d_ns = vjp(
            (dy_ref[...], tuple(dct_scr[h] for h in HEADS), tuple(dn_scr[h] for h in HEADS)))
        dqk_ref[:, 0:D_GROUP] = d_q
        dqk_ref[:, D_GROUP:2 * D_GROUP] = d_k
        for seg, val in enumerate((d_v, d_o)):
            dp_ref[:, seg * D_GROUP:(seg + 1) * D_GROUP] = val.astype(dp_ref.dtype)
            dsum_ref[:, seg * D_GROUP:(seg + 1) * D_GROUP] += jnp.sum(val, axis=0, keepdims=True)
        dg_ref[...] = d_gates
        dnw_ref[...] += d_nw
        for h in HEADS:
            dct_scr[h] = d_cts[h]
            dn_scr[h] = d_ns[h]

    rev = lambda c: nc - 1 - c
    st = lambda r, w: pl.BlockSpec((None, N_HEADS, r, w), lambda c: (rev(c), 0, 0, 0))
    return pl.pallas_call(
        body, name="mlstm_bwd", grid=(nc,),
        in_specs=[pl.BlockSpec((CHUNK, 2 * D_GROUP), lambda c: (rev(c), 0)),
                  pl.BlockSpec((CHUNK, 2 * D_GROUP), lambda c: (rev(c), 3)),
                  pl.BlockSpec((CHUNK, LANES), lambda c: (rev(c), 0)),
                  pl.BlockSpec((1, D_GROUP), lambda c: (0, 0)),
                  st(D_HEAD, D_HEAD), st(1, D_HEAD), st(1, 1),
                  pl.BlockSpec((CHUNK, D_GROUP), lambda c: (rev(c), 1)),
                  pl.BlockSpec(memory_space=pl.ANY)],
        out_specs=[pl.BlockSpec((CHUNK, 2 * D_GROUP), lambda c: (rev(c), 3)),
                   pl.BlockSpec((CHUNK, 2 * D_GROUP), lambda c: (rev(c), 0)),
                   pl.BlockSpec((CHUNK, LANES), lambda c: (rev(c), 0)),
                   pl.BlockSpec((1, D_GROUP), lambda c: (0, 0)),
                   pl.BlockSpec((1, 2 * D_GROUP), lambda c: (0, 0))],
        out_shape=[jax.ShapeDtypeStruct(d_proj.shape, d_proj.dtype), jax.ShapeDtypeStruct((s, 2 * D_GROUP), F32),
                   jax.ShapeDtypeStruct((s, LANES), F32), jax.ShapeDtypeStruct((1, D_GROUP), F32),
                   jax.ShapeDtypeStruct((1, 2 * D_GROUP), F32)],
        input_output_aliases={8: 0},
        scratch_shapes=[pltpu.VMEM((N_HEADS, D_HEAD, D_HEAD), F32), pltpu.VMEM((N_HEADS, 1, D_HEAD), F32)],
        compiler_params=_params(("arbitrary",)),
    )(qk, proj, gates, norm_w, ct_s, n_s, m_s, d_y, d_proj)


LN_TOKENS = 512
ATT_TOKENS = 256


def _res_ln_fwd(xres, branch, g, b, name):
    s, dm = xres.shape
    tb = min(LN_TOKENS, s)

    def body(x_ref, br_ref, g_ref, b_ref, o_ref):
        o_ref[...] = _layer_norm(ALPHA * x_ref[...] + br_ref[...], g_ref[...], b_ref[...])

    tok = pl.BlockSpec((tb, dm), lambda i: (i, 0))
    vec = pl.BlockSpec((1, dm), lambda i: (0, 0))
    return pl.pallas_call(
        body, name=name, grid=(s // tb,), in_specs=[tok, tok, vec, vec], out_specs=tok,
        out_shape=jax.ShapeDtypeStruct((s, dm), F32), compiler_params=_params(("parallel",)),
    )(xres, branch, g, b)


def _res_ln_bwd(xres, branch, g, b, d_out, name):
    s, dm = xres.shape
    tb = min(LN_TOKENS, s)

    def body(x_ref, br_ref, g_ref, b_ref, do_ref, dz_ref, dg_ref, db_ref):
        @pl.when(pl.program_id(0) == 0)
        def _():
            dg_ref[...] = jnp.zeros_like(dg_ref)
            db_ref[...] = jnp.zeros_like(db_ref)

        z = ALPHA * x_ref[...] + br_ref[...]
        _, vjp = jax.vjp(_layer_norm, z, g_ref[...], b_ref[...])
        d_z, d_g, d_b = vjp(do_ref[...])
        dz_ref[...] = d_z
        dg_ref[...] += d_g
        db_ref[...] += d_b

    tok = pl.BlockSpec((tb, dm), lambda i: (i, 0))
    vec = pl.BlockSpec((1, dm), lambda i: (0, 0))
    return pl.pallas_call(
        body, name=name, grid=(s // tb,), in_specs=[tok, tok, vec, vec, tok], out_specs=[tok, vec, vec],
        out_shape=[jax.ShapeDtypeStruct((s, dm), F32), jax.ShapeDtypeStruct((1, dm), F32),
                   jax.ShapeDtypeStruct((1, dm), F32)],
        compiler_params=_params(("arbitrary",)),
    )(xres, branch, g, b, d_out)


def _loss_tail(xres, branch, g, b, target):
    s, dm = xres.shape
    tb = min(LN_TOKENS, s)

    def loss_fn(z, gg, bb, tgt):
        err = jnp.square(_layer_norm(z, gg, bb) - tgt)
        return 0.5 * jnp.sum(jnp.mean(err, axis=-1, keepdims=True), axis=0, keepdims=True)

    def body(x_ref, br_ref, g_ref, b_ref, t_ref, loss_ref, dz_ref, dg_ref, db_ref):
        @pl.when(pl.program_id(0) == 0)
        def _():
            loss_ref[...] = jnp.zeros_like(loss_ref)
            dg_ref[...] = jnp.zeros_like(dg_ref)
            db_ref[...] = jnp.zeros_like(db_ref)

        z = ALPHA * x_ref[...] + br_ref[...]
        tgt = t_ref[...]
        loss, vjp = jax.vjp(lambda zz, gg, bb: loss_fn(zz, gg, bb, tgt), z, g_ref[...], b_ref[...])
        d_z, d_g, d_b = vjp(jnp.ones((1, 1), F32))
        loss_ref[...] += loss
        dz_ref[...] = d_z
        dg_ref[...] += d_g
        db_ref[...] += d_b

    tok = pl.BlockSpec((tb, dm), lambda i: (i, 0))
    vec = pl.BlockSpec((1, dm), lambda i: (0, 0))
    one = pl.BlockSpec((1, 1), lambda i: (0, 0))
    return pl.pallas_call(
        body, name="loss_tail", grid=(s // tb,), in_specs=[tok, tok, vec, vec, tok],
        out_specs=[one, tok, vec, vec],
        out_shape=[jax.ShapeDtypeStruct((1, 1), F32), jax.ShapeDtypeStruct((s, dm), F32),
                   jax.ShapeDtypeStruct((1, dm), F32), jax.ShapeDtypeStruct((1, dm), F32)],
        compiler_params=_params(("arbitrary",)),
    )(xres, branch, g, b, target)


def _att_head(q, k, v):
    sc = _nt(q, k) * (CA_DH ** -0.5)
    return _nn(jax.nn.softmax(sc, axis=-1), v)


def _att_fwd(q, kv):
    s = q.shape[0]
    tb = min(ATT_TOKENS, s)

    def body(q_ref, kv_ref, o_ref):
        for h in range(CA_HEADS):
            lo = h * CA_DH
            o_ref[:, lo:lo + CA_DH] = _att_head(q_ref[:, lo:lo + CA_DH], kv_ref[:, lo:lo + CA_DH],
                                                kv_ref[:, D_MODEL + lo:D_MODEL + lo + CA_DH]).astype(o_ref.dtype)

    tok = pl.BlockSpec((tb, D_MODEL), lambda i: (i, 0))
    return pl.pallas_call(
        body, name="att_fwd", grid=(s // tb,),
        in_specs=[tok, pl.BlockSpec((N_MEM, 2 * D_MODEL), lambda i: (0, 0))], out_specs=tok,
        out_shape=jax.ShapeDtypeStruct((s, D_MODEL), BF16), compiler_params=_params(("parallel",)),
    )(q, kv)


def _att_bwd(q, kv, d_o):
    s = q.shape[0]
    tb = min(ATT_TOKENS, s)

    def body(q_ref, kv_ref, do_ref, dq_ref, dkv_ref):
        @pl.when(pl.program_id(0) == 0)
        def _():
            dkv_ref[...] = jnp.zeros_like(dkv_ref)

        for h in range(CA_HEADS):
            lo = h * CA_DH
            vlo = D_MODEL + lo
            _, vjp = jax.vjp(_att_head, q_ref[:, lo:lo + CA_DH], kv_ref[:, lo:lo + CA_DH],
                             kv_ref[:, vlo:vlo + CA_DH])
            d_q, d_k, d_v = vjp(do_ref[:, lo:lo + CA_DH])
            dq_ref[:, lo:lo + CA_DH] = d_q
            dkv_ref[:, lo:lo + CA_DH] += d_k
            dkv_ref[:, vlo:vlo + CA_DH] += d_v

    tok = pl.BlockSpec((tb, D_MODEL), lambda i: (i, 0))
    mem = pl.BlockSpec((N_MEM, 2 * D_MODEL), lambda i: (0, 0))
    return pl.pallas_call(
        body, name="att_bwd", grid=(s // tb,), in_specs=[tok, mem, tok], out_specs=[tok, mem],
        out_shape=[jax.ShapeDtypeStruct((s, D_MODEL), F32), jax.ShapeDtypeStruct((N_MEM, 2 * D_MODEL), F32)],
        compiler_params=_params(("arbitrary",)),
    )(q, kv, d_o)


def _local_step(x, mem, target, w, mid_weights=None, ffn_weights=None, on_ffn_grads=None, on_mid_grads=None,
                on_small_grads=None, on_last_grads=None):
    w = dict(w)
    s = x.shape[0]
    tm = min(512, s)
    tt = min(512, s)
    proj = _matmul_nn(x, w["w_in_main"], w["b_in_main"], min(2048, s), 512, "proj")
    gates = _matmul_nn(x, w["w_in_gate"], w["b_in_gate"], tm, LANES, "proj_gates")
    qk = _ml_conv_fwd(proj, w["ml_conv_w"], w["ml_conv_b"])
    y, hg_states = _hgrn2_fwd(proj, w["hg_lb_logits"], w["hg_norm_w"])
    y, ct_s, n_s, m_s = _mlstm_fwd(qk, proj, gates, w["ml_norm_w"], y)
    if mid_weights is not None:
        w.update(mid_weights(y))
    mix =_matmul_nn(y, w["w_out"], None, tm, D_MODEL, "mix")
    x1 = _res_ln_fwd(x, mix, w["ln1_g"], w["ln1_b"], "ln1_fwd")
    kv = _matmul_nn(mem, w["ca_wkv"], None, N_MEM, CA_DH, "kv")
    q = _matmul_nn(x1, w["ca_wq"], None, tm, D_MODEL, "ca_q")
    att = _att_fwd(q, kv)
    ca = _matmul_nn(att, w["ca_wo"], None, tm, D_MODEL, "ca_out")
    x2 = _res_ln_fwd(x1, ca, w["ln2_g"], w["ln2_b"], "ln2_fwd")
    if ffn_weights is not None:
        w.update(ffn_weights(x2))
    u = _matmul_nn(x2, w["ffn_w_up"], None, min(2048, s), UP_SHARD_P, "ffn_up")
    hid = _ffn_conv_fwd(u, w["ffn_conv_w"], w["ffn_conv_b"])
    ff = _matmul_nn(hid, w["ffn_w_down"], None, tm, D_MODEL, "ffn_down")
    loss, d_z3, d_ln3_g, d_ln3_b = _loss_tail(x2, ff, w["ln3_g"], w["ln3_b"], target)
    grads = {"ln3_g": d_ln3_g, "ln3_b": d_ln3_b}
    grads["ffn_w_down"] = _matmul_tn(hid, d_z3, 1536, D_MODEL, tt, "d_w_down")
    d_hid = _matmul_nt([(d_z3, w["ffn_w_down"])], None, 1.0, tm, D_FF_P, "d_hid")
    d_ug, d_uv, d_cwg, d_cwv, d_cbg, d_cbv = _ffn_conv_bwd(u, w["ffn_conv_w"], w["ffn_conv_b"], d_hid)
    grads["ffn_conv_w"] = jnp.concatenate([d_cwg, d_cwv], axis=-1)
    grads["ffn_conv_b"] = jnp.concatenate([d_cbg, d_cbv], axis=-1)
    half = N_DEV // 2
    d_w_up = _matmul_tn(x2, d_ug, D_MODEL, UP_SHARD_P, tt, "d_w_up_gate", shards=N_DEV, group=half)
    grads["ffn_w_up"] = _matmul_tn(x2, d_uv, D_MODEL, UP_SHARD_P, tt, "d_w_up_val", shards=N_DEV,
                                   shard0=half, group=half, into=d_w_up)
    d_x2 = _matmul_nt([(d_ug, w["ffn_w_up"], 0), (d_uv, w["ffn_w_up"], N_DEV // 2)], d_z3, ALPHA,
                      min(256, s), D_MODEL, "d_x2")
    if on_ffn_grads is not None:
        d_x2 = on_ffn_grads(grads, d_x2)
    d_z2, grads["ln2_g"], grads["ln2_b"] = _res_ln_bwd(x1, ca, w["ln2_g"], w["ln2_b"], d_x2, "ln2_bwd")
    grads["ca_wo"] = _matmul_tn(att, d_z2, D_MODEL, D_MODEL, tt, "d_ca_wo")
    d_att = _matmul_nt([(d_z2, w["ca_wo"])], None, 1.0, tm, D_MODEL, "d_att")
    d_q, d_kv = _att_bwd(q, kv, d_att)
    grads["ca_wq"] = _matmul_tn(x1, d_q, D_MODEL, D_MODEL, tt, "d_ca_wq")
    grads["ca_wkv"] = _matmul_tn(mem, d_kv, D_MODEL, CA_DH, N_MEM, "d_ca_wkv", shards=N_DEV, group=N_DEV)
    d_x1 = _matmul_nt([(d_q, w["ca_wq"])], d_z2, ALPHA, tm, D_MODEL, "d_x1")
    d_z1, grads["ln1_g"], grads["ln1_b"] = _res_ln_bwd(x, mix, w["ln1_g"], w["ln1_b"], d_x1, "ln1_bwd")
    grads["w_out"] = _matmul_tn(y, d_z1, D_MODEL, D_MODEL, tt, "d_w_out")
    if on_mid_grads is not None:
        d_z1 = on_mid_grads(grads, d_z1)
    d_y = _matmul_nt([(d_z1, w["w_out"])], None, 1.0, tm, D_MODEL, "d_y")
    d_proj, grads["hg_lb_logits"], grads["hg_norm_w"], db_hg = _hgrn2_bwd(
        proj, w["hg_lb_logits"], w["hg_norm_w"], hg_states, d_y)
    d_proj, d_qk, d_gates, grads["ml_norm_w"], db_vo = _mlstm_bwd(
        qk, proj, gates, w["ml_norm_w"], ct_s, n_s, m_s, d_y, d_proj)
    d_proj, grads["ml_conv_w"], grads["ml_conv_b"], db_qk = _ml_conv_bwd(
        proj, w["ml_conv_w"], w["ml_conv_b"], d_qk, d_proj)
    grads["b_in_main"] = jnp.concatenate([db_hg, db_qk, db_vo], axis=-1)
    grads["w_in_gate"], grads["b_in_gate"] = _matmul_tn(x, d_gates, D_MODEL, LANES, tt, "d_w_in_gates", colsum=True)
    if on_small_grads is not None:
        d_proj = on_small_grads(grads, loss, d_proj)
    grads["w_in_main"] = _matmul_tn(x, d_proj, D_MODEL, min(2048, D_IN_MAIN), tt, "d_w_in")
    if on_last_grads is not None:
        d_z1 = on_last_grads(grads, d_z1)
    grad_x = _matmul_nt([(d_proj, w["w_in_main"]), (d_gates, w["w_in_gate"])], d_z1, ALPHA, tm, D_MODEL, "d_x")
    return loss, grad_x, grads


HBM_SPEC = pl.BlockSpec(memory_space=pltpu.HBM)


def _coords():
    return lax.axis_index("x"), lax.axis_index("y"), lax.axis_index("c")


def _other_chips(x, y):
    return [(1 - x, y), (x, 1 - y), (1 - x, 1 - y)]


def _all_gather_two_level(shards, name):
    na = len(shards)

    def body(*refs):
        x_refs, out_refs = refs[:na], refs[na:2 * na]
        send_sems, recv_sems, local_sems = refs[2 * na:]
        x, y, c = _coords()
        me, sibling = (x, y, c), (x, y, 1 - c)
        chips = _other_chips(x, y)

        def copy(a, k, block, to, own=False):
            slot = out_refs[a].at[4 * block[0] + 2 * block[1] + block[2]]
            return pltpu.make_async_remote_copy(
                src_ref=x_refs[a] if own else slot, dst_ref=slot,
                send_sem=send_sems.at[7 * a + k], recv_sem=recv_sems.at[7 * a + k],
                device_id=to, device_id_type=MESH)

        mine = [pltpu.make_async_copy(x_refs[a], out_refs[a].at[4 * x + 2 * y + c], local_sems.at[a])
                for a in range(na)]
        for cp in mine:
            cp.start()
        first = []
        for a in range(na):
            first.append(copy(a, 0, me, sibling, own=True))
            first += [copy(a, 1 + j, me, (*chip, c), own=True) for j, chip in enumerate(chips)]
        for cp in first:
            cp.start()
        passed = []
        for j, chip in enumerate(chips):
            for a in range(na):
                copy(a, 1 + j, (*chip, c), me).wait_recv()
                fwd = copy(a, 4 + j, (*chip, c), sibling)
                fwd.start()
                passed.append(fwd)
        for a in range(na):
            copy(a, 0, sibling, me).wait_recv()
            for j, chip in enumerate(chips):
                copy(a, 4 + j, (*chip, 1 - c), me).wait_recv()
        for cp in first + passed:
            cp.wait_send()
        for cp in mine:
            cp.wait()

    return pl.pallas_call(
        body, name=name,
        out_shape=[jax.ShapeDtypeStruct((N_DEV,) + t.shape, t.dtype) for t in shards],
        in_specs=[HBM_SPEC] * na, out_specs=[HBM_SPEC] * na,
        scratch_shapes=[pltpu.SemaphoreType.DMA((7 * na,)), pltpu.SemaphoreType.DMA((7 * na,)),
                        pltpu.SemaphoreType.DMA((na,))],
    )(*shards)


SEM_SPEC = pl.BlockSpec(memory_space=pltpu.SEMAPHORE)
ANY_SPEC = pl.BlockSpec(memory_space=pl.ANY)
SIDE_EFFECT = pltpu.SideEffectType.DATAFLOW_SIDE_EFFECTING


def _peer(x, y, c, d):
    flip = lambda v, bit: 1 - v if bit else v
    p = (flip(x, d & 4), flip(y, d & 2), flip(c, d & 1))
    return p, 4 * p[0] + 2 * p[1] + p[2]


def _direct_copies(gather, src_refs, land_refs, send_sems, recv_sems):
    x, y, c = _coords()
    me = 4 * x + 2 * y + c
    copies = []
    for a in range(len(src_refs)):
        for d in range(1, N_DEV):
            peer, peer_slot = _peer(x, y, c, d)
            copies.append(pltpu.make_async_remote_copy(
                src_ref=src_refs[a] if gather else src_refs[a].at[peer_slot],
                dst_ref=land_refs[a].at[me] if gather else land_refs[a].at[d - 1],
                send_sem=send_sems.at[7 * a + d - 1], recv_sem=recv_sems.at[7 * a + d - 1],
                device_id=peer, device_id_type=MESH))
    return copies


def _hbm(t):
    return pltpu.HBM(t.shape, t.dtype)


def _direct_start(gather, arrays, through, name):
    na = len(arrays)
    lands = [lax.empty((N_DEV,) + t.shape if gather else (N_DEV - 1,) + t.shape[1:], t.dtype) for t in arrays]
    n_io = 2 * na + 1

    def body(*refs):
        for cp in _direct_copies(gather, refs[:na], refs[na:2 * na], refs[n_io], refs[n_io + 1]):
            cp.start()

    ins = [pltpu.with_memory_space_constraint(t, pltpu.HBM) for t in (*arrays, *lands, through)]
    sems = pltpu.SemaphoreType.DMA((7 * na,))
    res = pl.pallas_call(
        body, name=name, out_shape=(sems, sems, *[_hbm(t) for t in ins]),
        in_specs=[HBM_SPEC] * n_io, out_specs=(SEM_SPEC, SEM_SPEC, *[HBM_SPEC] * n_io),
        input_output_aliases={i: 2 + i for i in range(n_io)},
        compiler_params=pltpu.CompilerParams(has_side_effects=SIDE_EFFECT),
    )(*ins)
    return (res[0], res[1], list(res[2:2 + na]), list(res[2 + na:2 + 2 * na])), res[2 + 2 * na]


def _direct_wait(gather, started, after, name):
    send_sems, recv_sems, arrays, lands = started
    na = len(arrays)

    def body(*refs):
        for cp in _direct_copies(gather, refs[:na], refs[na:2 * na], refs[2 * na], refs[2 * na + 1]):
            cp.wait_send()
            cp.wait_recv()

    res = pl.pallas_call(
        body, name=name, out_shape=tuple(_hbm(t) for t in (*arrays, *lands)),
        in_specs=[HBM_SPEC] * (2 * na) + [SEM_SPEC, SEM_SPEC, ANY_SPEC], out_specs=tuple([HBM_SPEC] * (2 * na)),
        input_output_aliases={i: i for i in range(2 * na)},
        compiler_params=pltpu.CompilerParams(has_side_effects=SIDE_EFFECT),
    )(*arrays, *lands, send_sems, recv_sems, after)
    return list(res[:na]), list(res[na:])


def _row_tile(rows):
    for t in (256, 176, 128):
        if rows % t == 0 and rows > t:
            return t
    return rows


def _adamw_math(g, w, m, v):
    m_new = ADAM_B1 * m + (1.0 - ADAM_B1) * g
    v_new = ADAM_B2 * v + (1.0 - ADAM_B2) * jnp.square(g)
    m_hat = m_new / (1.0 - ADAM_B1 ** ADAM_STEP)
    v_hat = v_new / (1.0 - ADAM_B2 ** ADAM_STEP)
    delta = -ADAM_LR * (m_hat / (jnp.sqrt(v_hat) + ADAM_EPS) + ADAM_WD * w)
    return delta, m_new, v_new


def _adamw_sharded(chip, sums, got, w, m, v, name):
    r, c = w.shape
    tr = _row_tile(r)
    n_got = got.shape[0]

    def body(chip_ref, s_ref, g_ref, w_ref, m_ref, v_ref, go_ref, d_ref, nm_ref, nv_ref):
        g = s_ref[...]
        for i in range(n_got):
            g = g + g_ref[i].astype(F32)
        delta, m_new, v_new = _adamw_math(g, w_ref[...], m_ref[...], v_ref[...])
        go_ref[...] = g
        d_ref[...] = delta
        nm_ref[...] = m_new
        nv_ref[...] = v_new

    blk = pl.BlockSpec((tr, c), lambda i, chip_ref: (i, 0))
    out = jax.ShapeDtypeStruct((r, c), F32)
    return pl.pallas_call(
        body, name=name,
        grid_spec=pltpu.PrefetchScalarGridSpec(
            num_scalar_prefetch=1, grid=(r // tr,),
            in_specs=[pl.BlockSpec((None, tr, c), lambda i, chip_ref: (chip_ref[0], i, 0)),
                      pl.BlockSpec((n_got, tr, c), lambda i, chip_ref: (0, i, 0)), blk, blk, blk],
            out_specs=[blk, blk, blk, blk]),
        out_shape=[out, out, out, out],
        compiler_params=_params(("parallel",)),
    )(chip, sums, got, w, m, v)


def _adamw_replicated(parts, w, m, v):
    p, r, c = parts.shape

    def body(p_ref, w_ref, m_ref, v_ref, g_ref, d_ref, nm_ref, nv_ref):
        g = p_ref[0]
        for i in range(1, p):
            g = g + p_ref[i]
        delta, m_new, v_new = _adamw_math(g, w_ref[...], m_ref[...], v_ref[...])
        g_ref[...] = g
        d_ref[...] = delta
        nm_ref[...] = m_new
        nv_ref[...] = v_new

    blk = pl.BlockSpec((r, c), lambda i: (0, 0))
    out = jax.ShapeDtypeStruct((r, c), F32)
    return pl.pallas_call(
        body, name="adamw_replicated", grid=(1,),
        in_specs=[pl.BlockSpec((p, r, c), lambda i: (0, 0, 0)), blk, blk, blk],
        out_specs=[blk, blk, blk, blk], out_shape=[out, out, out, out],
        compiler_params=_params(("arbitrary",)),
    )(parts, w, m, v)


SHARDED_NAMES = ("w_in", "ml_conv_w", "w_out", "ca_wq", "ca_wkv", "ca_wo", "ffn_w_up", "ffn_conv_w", "ffn_w_down")
SMALL_NAMES = ("b_in", "hg_lb_logits", "hg_norm_w", "ml_conv_b", "ml_norm_w", "ln1_g", "ln1_b",
               "ln2_g", "ln2_b", "ffn_conv_b", "ln3_g", "ln3_b")
WEIGHT_NAMES = ("w_in", "b_in", "hg_lb_logits", "hg_norm_w", "ml_conv_w", "ml_conv_b", "ml_norm_w", "w_out",
                "ln1_g", "ln1_b", "ca_wq", "ca_wkv", "ca_wo", "ln2_g", "ln2_b", "ffn_w_up", "ffn_conv_w",
                "ffn_conv_b", "ffn_w_down", "ln3_g", "ln3_b")
PAD_TO = {"w_in": W_IN_SHARD_P, "ffn_w_up": UP_SHARD_P, "ffn_conv_w": UP_SHARD_P}
SMALL_ROWS = 24
SMALL_W = D_MODEL


def _shard_2d(name, block):
    t = block[0]
    if name in PAD_TO:
        t = jnp.pad(t, ((0, 0), (0, PAD_TO[name] - t.shape[1])))
    return t


def _shard_like(name, t, like):
    return t[:, :like.shape[2]][None]


def _pad_cols(t, width):
    return jnp.pad(t, ((0, 0), (0, width - t.shape[1])))


FIRST_NAMES = ("w_in", "ml_conv_w")
FFN_NAMES = ("ffn_w_up", "ffn_w_down", "ffn_conv_w")
MID_NAMES = ("ca_wo", "ca_wq", "ca_wkv", "w_out")


def _first_weights(g, small):
    w = dict(small)
    w_in = jnp.concatenate([g["w_in"][j, :, :W_IN_SHARD] for j in range(N_DEV)], axis=1)
    w["w_in_main"] = w_in[:, :D_IN_MAIN]
    w["w_in_gate"] = _pad_cols(w_in[:, D_IN_MAIN:], LANES)
    w["b_in_main"] = small["b_in"][:, :D_IN_MAIN]
    w["b_in_gate"] = _pad_cols(small["b_in"][:, D_IN_MAIN:], LANES)
    w["ml_conv_w"] = jnp.transpose(g["ml_conv_w"], (1, 0, 2)).reshape(ML_CONV, 2 * D_GROUP)
    return w


def _mid_weights(g):
    w = {n: g[n].reshape(D_MODEL, D_MODEL) for n in ("w_out", "ca_wq", "ca_wo")}
    w["ca_wkv"] = g["ca_wkv"]
    return w


def _ffn_weights(g, small):
    w = {"ffn_w_up": g["ffn_w_up"]}
    down = g["ffn_w_down"].reshape(N_DEV // 2, UP_SHARD, D_MODEL)
    w["ffn_w_down"] = jnp.pad(down, ((0, 0), (0, UP_SHARD_P - UP_SHARD), (0, 0))).reshape(D_FF_P, D_MODEL)
    w["ffn_conv_w"] = jnp.transpose(g["ffn_conv_w"], (1, 0, 2)).reshape(FFN_CONV, D_UP_P)
    w["ffn_conv_b"] = _pad_cols(small["ffn_conv_b"].reshape(N_DEV, UP_SHARD), UP_SHARD_P).reshape(1, D_UP_P)
    return w


def _whole_weights(g, small):
    return {**_first_weights(g, small), **_mid_weights(g), **_ffn_weights(g, small)}


def _owner_stack(n, grads):
    if n == "w_in":
        w_in = jnp.concatenate([grads["w_in_main"], grads["w_in_gate"][:, :D_IN - D_IN_MAIN]], axis=1)
        return jnp.stack([_pad_cols(w_in[:, j * W_IN_SHARD:(j + 1) * W_IN_SHARD], W_IN_SHARD_P)
                          for j in range(N_DEV)])
    if n in ("w_out", "ca_wq", "ca_wo"):
        return grads[n].reshape(N_DEV, D_MODEL // N_DEV, D_MODEL)
    if n == "ffn_w_down":
        down = grads[n].reshape(N_DEV // 2, UP_SHARD_P, D_MODEL)[:, :UP_SHARD]
        return down.reshape(N_DEV, D_FF // N_DEV, D_MODEL)
    if n == "ml_conv_w":
        return jnp.transpose(grads[n].reshape(ML_CONV, N_DEV, LANES), (1, 0, 2))
    if n == "ffn_conv_w":
        return jnp.transpose(grads[n].reshape(FFN_CONV, N_DEV, UP_SHARD_P), (1, 0, 2))
    return grads[n]


def _owner_stacks(grads):
    return {n: _owner_stack(n, grads) for n in SHARDED_NAMES}


def _small_grads(grads):
    out = {n: grads[n] for n in SMALL_NAMES if n in grads}
    out["b_in"] = jnp.concatenate([grads["b_in_main"], grads["b_in_gate"][:, :D_IN - D_IN_MAIN]], axis=1)
    out["ffn_conv_b"] = grads["ffn_conv_b"].reshape(N_DEV, UP_SHARD_P)[:, :UP_SHARD].reshape(1, D_UP)
    return out


def _pack_small(p, extra=None):
    flat = [p[n].reshape(-1) for n in SMALL_NAMES]
    if extra is not None:
        flat.append(extra.reshape(-1))
    flat = jnp.concatenate(flat)
    return jnp.pad(flat, (0, SMALL_ROWS * SMALL_W - flat.shape[0])).reshape(SMALL_ROWS, SMALL_W)


def _unpack_small(slab, like):
    out = {}
    flat = slab.reshape(-1)
    o = 0
    for n in SMALL_NAMES:
        out[n] = flat[o:o + like[n].size].reshape(like[n].shape)
        o += like[n].size
    return out, flat[o]


def kernel(x, mem, w_in, b_in, hg_lb_logits, hg_norm_w, ml_conv_w, ml_conv_b, ml_norm_w, w_out, ln1_g, ln1_b, ca_wq, ca_wkv, ca_wo, ln2_g, ln2_b, ffn_w_up, ffn_conv_w, ffn_conv_b, ffn_w_down, ln3_g, ln3_b, loss_target, m_w_in, m_b_in, m_hg_lb_logits, m_hg_norm_w, m_ml_conv_w, m_ml_conv_b, m_ml_norm_w, m_w_out, m_ln1_g, m_ln1_b, m_ca_wq, m_ca_wkv, m_ca_wo, m_ln2_g, m_ln2_b, m_ffn_w_up, m_ffn_conv_w, m_ffn_conv_b, m_ffn_w_down, m_ln3_g, m_ln3_b, v_w_in, v_b_in, v_hg_lb_logits, v_hg_norm_w, v_ml_conv_w, v_ml_conv_b, v_ml_norm_w, v_w_out, v_ln1_g, v_ln1_b, v_ca_wq, v_ca_wkv, v_ca_wo, v_ln2_g, v_ln2_b, v_ffn_w_up, v_ffn_conv_w, v_ffn_conv_b, v_ffn_w_down, v_ln3_g, v_ln3_b):
    params = dict(w_in=w_in, b_in=b_in, hg_lb_logits=hg_lb_logits, hg_norm_w=hg_norm_w, ml_conv_w=ml_conv_w,
                  ml_conv_b=ml_conv_b, ml_norm_w=ml_norm_w, w_out=w_out, ln1_g=ln1_g, ln1_b=ln1_b, ca_wq=ca_wq,
                  ca_wkv=ca_wkv, ca_wo=ca_wo, ln2_g=ln2_g, ln2_b=ln2_b, ffn_w_up=ffn_w_up, ffn_conv_w=ffn_conv_w,
                  ffn_conv_b=ffn_conv_b, ffn_w_down=ffn_w_down, ln3_g=ln3_g, ln3_b=ln3_b)
    mom1 = dict(w_in=m_w_in, b_in=m_b_in, hg_lb_logits=m_hg_lb_logits, hg_norm_w=m_hg_norm_w,
                ml_conv_w=m_ml_conv_w, ml_conv_b=m_ml_conv_b, ml_norm_w=m_ml_norm_w, w_out=m_w_out, ln1_g=m_ln1_g,
                ln1_b=m_ln1_b, ca_wq=m_ca_wq, ca_wkv=m_ca_wkv, ca_wo=m_ca_wo, ln2_g=m_ln2_g, ln2_b=m_ln2_b,
                ffn_w_up=m_ffn_w_up, ffn_conv_w=m_ffn_conv_w, ffn_conv_b=m_ffn_conv_b, ffn_w_down=m_ffn_w_down,
                ln3_g=m_ln3_g, ln3_b=m_ln3_b)
    mom2 = dict(w_in=v_w_in, b_in=v_b_in, hg_lb_logits=v_hg_lb_logits, hg_norm_w=v_hg_norm_w,
                ml_conv_w=v_ml_conv_w, ml_conv_b=v_ml_conv_b, ml_norm_w=v_ml_norm_w, w_out=v_w_out, ln1_g=v_ln1_g,
                ln1_b=v_ln1_b, ca_wq=v_ca_wq, ca_wkv=v_ca_wkv, ca_wo=v_ca_wo, ln2_g=v_ln2_g, ln2_b=v_ln2_b,
                ffn_w_up=v_ffn_w_up, ffn_conv_w=v_ffn_conv_w, ffn_conv_b=v_ffn_conv_b, ffn_w_down=v_ffn_w_down,
                ln3_g=v_ln3_g, ln3_b=v_ln3_b)

    x_idx, y_idx, c_idx = _coords()
    as_index = lambda v: jnp.reshape(v, (1,)).astype(jnp.int32)
    me = as_index(4 * x_idx + 2 * y_idx + c_idx)
    small_params = {n: params[n] for n in SMALL_NAMES}

    shards = {n: _shard_2d(n, params[n]) for n in SHARDED_NAMES}
    to_send = lambda names: [shards[n] if "conv" in n else shards[n].astype(BF16) for n in names]
    first = dict(zip(FIRST_NAMES, _all_gather_two_level(to_send(FIRST_NAMES), "weights_gather_first")))
    mid_started, through = _direct_start(True, to_send(MID_NAMES), first["w_in"], "weights_gather_start_mid")
    ffn_started, first["w_in"] = _direct_start(True, to_send(FFN_NAMES), through, "weights_gather_start_ffn")

    def gathered_weights(names, started, after, tag):
        mine, lands = _direct_wait(True, started, after, "weights_gather_wait_" + tag)
        return {n: lax.dynamic_update_index_in_dim(land, own, me[0], 0) for n, own, land in zip(names, mine, lands)}

    started, own_stacks = {}, {}

    def start_group(names, tag):
        def hook(grads, through):
            own_stacks[tag] = [_owner_stack(n, grads) for n in names]
            started[tag], through = _direct_start(False, [t.astype(BF16) for t in own_stacks[tag]], through,
                                                  "grads_start_" + tag)
            return through
        return hook

    def start_small(grads, loss, through):
        started["small"], through = _direct_start(True, [_pack_small(_small_grads(grads), loss)], through,
                                                  "small_gather_start")
        return through

    loss, grad_x, grads = _local_step(
        x[0], mem[0], loss_target[0], _first_weights(first, small_params),
        lambda y: _mid_weights(gathered_weights(MID_NAMES, mid_started, y, "mid")),
        lambda x2: _ffn_weights(gathered_weights(FFN_NAMES, ffn_started, x2, "ffn"), small_params),
        start_group(FFN_NAMES, "ffn"), start_group(MID_NAMES, "mid"), start_small, start_group(FIRST_NAMES, "last"))

    sharded_out = {}
    after = grad_x
    for names, tag in ((FFN_NAMES, "ffn"), (MID_NAMES, "mid"), (FIRST_NAMES, "last")):
        _, lands = _direct_wait(False, started[tag], after, "grads_wait_" + tag)
        for n, st, land in zip(names, own_stacks[tag], lands):
            res = _adamw_sharded(me, st, land, shards[n], _shard_2d(n, mom1[n]), _shard_2d(n, mom2[n]), "adamw_" + n)
            sharded_out[n] = [_shard_like(n, t, params[n]) for t in res]
            after = res[0]
    own_small, small_lands = _direct_wait(True, started["small"], after, "small_gather_wait")
    small_parts = lax.dynamic_update_index_in_dim(small_lands[0], own_small[0], me[0], 0)
    small_res = _adamw_replicated(small_parts, _pack_small(params), _pack_small(mom1), _pack_small(mom2))

    outs = []
    total_loss = None
    for k in range(4):
        small, extra = _unpack_small(small_res[k], params)
        if total_loss is None:
            total_loss = extra
        outs.extend(sharded_out[n][k] if n in sharded_out else small[n] for n in WEIGHT_NAMES)
    return (total_loss, grad_x[None], *outs)
```

```python
import jax
import jax.numpy as jnp
from jax import lax
from jax.experimental import pallas as pl
from jax.experimental.pallas import tpu as pltpu

F32 = jnp.float32
BF16 = jnp.bfloat16
HIGHEST = lax.Precision.HIGHEST
MESH = pl.DeviceIdType.MESH

N_DEV = 8
D_MODEL = 1024
N_MEM = 256
N_HEADS = 4
D_HEAD = 128
D_GROUP = N_HEADS * D_HEAD
CHUNK = 64
ML_CONV = 4
FFN_CONV = 3
D_FF = 2816
D_UP = 2 * D_FF
CA_HEADS = 4
CA_DH = D_MODEL // CA_HEADS
LANES = 128
SUBLANES = 8
D_IN = 8 * D_GROUP + 2 * N_HEADS
D_IN_MAIN = 8 * D_GROUP
W_IN_SHARD = D_IN // N_DEV
W_IN_SHARD_P = 640
UP_SHARD = D_UP // N_DEV
UP_SHARD_P = 768
D_UP_P = N_DEV * UP_SHARD_P
D_FF_P = D_UP_P // 2
ALPHA = 2.0 ** 0.25
LN_EPS = 1e-5
NEG_BIG = -1e30
ADAM_LR = 0.001
ADAM_B1 = 0.9
ADAM_B2 = 0.999
ADAM_EPS = 1e-08
ADAM_WD = 0.01
ADAM_STEP = 10
VMEM_LIMIT = 56 * 1024 * 1024

SEG_HQ, SEG_HF, SEG_HI, SEG_HG, SEG_MQ, SEG_MK, SEG_MV, SEG_MO = (4 * i for i in range(8))


def _params(sem):
    return pltpu.CompilerParams(dimension_semantics=sem, vmem_limit_bytes=VMEM_LIMIT)


def _dg(a, b, ca, cb, precision=None):
    return lax.dot_general(a, b, (((ca,), (cb,)), ((), ())), precision=precision,
                           preferred_element_type=F32)


def _nn_raw(a, b):
    return _dg(a.astype(BF16), b.astype(BF16), 1, 0)


def _nt_raw(a, b):
    return _dg(a.astype(BF16), b.astype(BF16), 1, 1)


def _tn_raw(a, b):
    return _dg(a.astype(BF16), b.astype(BF16), 0, 0)


@jax.custom_vjp
def _nn(a, b):
    return _nn_raw(a, b)


_nn.defvjp(lambda a, b: (_nn_raw(a, b), (a, b)),
           lambda res, g: (_nt_raw(g, res[1]), _tn_raw(res[0], g)))


@jax.custom_vjp
def _nt(a, b):
    return _nt_raw(a, b)


_nt.defvjp(lambda a, b: (_nt_raw(a, b), (a, b)),
           lambda res, g: (_nn_raw(g, res[1]), _tn_raw(g, res[0])))


@jax.custom_vjp
def _tn(a, b):
    return _tn_raw(a, b)


_tn.defvjp(lambda a, b: (_tn_raw(a, b), (a, b)),
           lambda res, g: (_nt_raw(res[1], g), _nn_raw(res[0], g)))


def _layer_norm(z, g, b):
    mu = jnp.mean(z, axis=-1, keepdims=True)
    var = jnp.mean(jnp.square(z - mu), axis=-1, keepdims=True)
    return (z - mu) * lax.rsqrt(var + LN_EPS) * g + b


def _matmul_nn(a, w, bias, tm, tn, name):
    m, k = a.shape
    if w.ndim == 3:
        n = w.shape[0] * w.shape[2]
        assert tn == w.shape[2]
        w_spec = pl.BlockSpec((None, k, tn), lambda i, j: (j, 0, 0))
    else:
        n = w.shape[1]
        w_spec = pl.BlockSpec((k, tn), lambda i, j: (0, j))

    def body(*refs):
        a_ref, w_ref = refs[0], refs[1]
        o_ref = refs[-1]
        acc = _nn_raw(a_ref[...], w_ref[...])
        if bias is not None:
            acc = acc + refs[2][...]
        o_ref[...] = acc

    in_specs = [pl.BlockSpec((tm, k), lambda i, j: (i, 0)), w_spec]
    args = [a, w]
    if bias is not None:
        in_specs.append(pl.BlockSpec((1, tn), lambda i, j: (0, j)))
        args.append(bias)
    return pl.pallas_call(
        body, name=name, grid=(m // tm, n // tn), in_specs=in_specs,
        out_specs=pl.BlockSpec((tm, tn), lambda i, j: (i, j)),
        out_shape=jax.ShapeDtypeStruct((m, n), F32),
        compiler_params=_params(("parallel", "parallel")),
    )(*args)


def _matmul_nt(pairs, add, scale, tm, tk, name):
    m = pairs[0][0].shape[0]
    k = pairs[0][1].shape[-2]
    groups = []
    in_specs, args = [], []
    for pair in pairs:
        d, w = pair[0], pair[1]
        in_specs.append(pl.BlockSpec((tm, d.shape[1]), lambda i, j: (i, 0)))
        if w.ndim == 3:
            g = d.shape[1] // w.shape[2]
            blk = pair[2] // g
            in_specs.append(pl.BlockSpec((g, tk, w.shape[2]), lambda i, j, blk=blk: (blk, j, 0)))
            groups.append((g, w.shape[2]))
        else:
            in_specs.append(pl.BlockSpec((tk, w.shape[1]), lambda i, j: (j, 0)))
            groups.append(None)
        args += [d, w]
    if add is not None:
        in_specs.append(pl.BlockSpec((tm, tk), lambda i, j: (i, j)))
        args.append(add)

    def body(*refs):
        o_ref = refs[-1]
        acc = None
        for p, grp in enumerate(groups):
            d_ref, w_ref = refs[2 * p], refs[2 * p + 1]
            if grp is None:
                terms = [_nt_raw(d_ref[...], w_ref[...])]
            else:
                terms = [_nt_raw(d_ref[:, g * grp[1]:(g + 1) * grp[1]], w_ref[g]) for g in range(grp[0])]
            for t in terms:
                acc = t if acc is None else acc + t
        if add is not None:
            acc = acc + scale * refs[2 * len(groups)][...]
        o_ref[...] = acc

    return pl.pallas_call(
        body, name=name, grid=(m // tm, k // tk), in_specs=in_specs,
        out_specs=pl.BlockSpec((tm, tk), lambda i, j: (i, j)),
        out_shape=jax.ShapeDtypeStruct((m, k), F32),
        compiler_params=_params(("parallel", "parallel")),
    )(*args)


def _matmul_tn(a, b, tm, tn, tt, name, shards=None, shard0=0, group=1, into=None, colsum=False):
    t, m = a.shape
    n = b.shape[1]
    assert not colsum or tm == m
    n_in = 2 + (into is not None)
    out_dtype = BF16
    per_step = 1 if shards is None else group
    width = per_step * tn

    def body(*refs):
        a_ref, b_ref = refs[0], refs[1]
        o_ref, acc_ref = refs[n_in], refs[-1]
        first = pl.program_id(2) == 0

        @pl.when(first)
        def _():
            acc_ref[...] = jnp.zeros_like(acc_ref)

        if shards is None:
            acc_ref[...] += _tn_raw(a_ref[...], b_ref[...])
        else:
            lhs = a_ref[...].astype(BF16)
            for g in range(per_step):
                acc_ref[g] += _tn_raw(lhs, b_ref[:, g * tn:(g + 1) * tn])

        @pl.when(pl.program_id(2) == t // tt - 1)
        def _():
            o_ref[...] = acc_ref[...].astype(o_ref.dtype)

        if colsum:
            s_ref = refs[n_in + 1]

            @pl.when(first)
            def _():
                s_ref[...] = jnp.zeros_like(s_ref)

            s_ref[...] += jnp.sum(b_ref[...], axis=0, keepdims=True)

    in_specs = [pl.BlockSpec((tt, tm), lambda i, j, kk: (kk, i)),
                pl.BlockSpec((tt, width), lambda i, j, kk: (kk, j))]
    args = [a, b]
    aliases = {}
    if into is not None:
        in_specs.append(pl.BlockSpec(memory_space=pl.ANY))
        args.append(into)
        aliases = {2: 0}
    if shards is None:
        out_specs = [pl.BlockSpec((tm, tn), lambda i, j, kk: (i, j))]
        out_shape = [jax.ShapeDtypeStruct((m, n), out_dtype)]
        acc = pltpu.VMEM((tm, tn), F32)
    else:
        out_specs = [pl.BlockSpec((per_step, tm, tn), lambda i, j, kk: (shard0 // per_step + j, i, 0))]
        out_shape = [jax.ShapeDtypeStruct((shards, m, tn), out_dtype)]
        acc = pltpu.VMEM((per_step, tm, tn), F32)
    if colsum:
        out_specs.append(pl.BlockSpec((1, tn), lambda i, j, kk: (0, j)))
        out_shape.append(jax.ShapeDtypeStruct((1, n), F32))
    res = pl.pallas_call(
        body, name=name, grid=(m // tm, n // width, t // tt), in_specs=in_specs, out_specs=out_specs,
        out_shape=out_shape, input_output_aliases=aliases, scratch_shapes=[acc],
        compiler_params=_params(("parallel", "parallel", "arbitrary")),
    )(*args)
    return res if colsum else res[0]


ROW_TILE = 64


def _conv_fwd_tile(pad_ref, w_ref, b_ref, r0, rows, taps):
    acc = b_ref[...]
    for j in range(taps):
        acc = acc + pad_ref[pl.ds(SUBLANES - (taps - 1 - j) + r0, rows), :] * w_ref[j:j + 1, :]
    return acc


def _conv_bwd_tile(dpad_ref, w_ref, r0, rows, taps):
    acc = None
    for j in range(taps):
        term = dpad_ref[pl.ds(r0 + (taps - 1 - j), rows), :] * w_ref[j:j + 1, :]
        acc = term if acc is None else acc + term
    return acc


def _conv_grads_tile(pad_ref, dpad_ref, dx_ref, w_ref, dws, r0, rows, taps):
    dx = _conv_bwd_tile(dpad_ref, w_ref, r0, rows, taps)
    dx_ref[r0:r0 + rows, :] = dx.astype(dx_ref.dtype)
    d_pre = dpad_ref[r0:r0 + rows, :]
    for j in range(taps):
        xs = pad_ref[pl.ds(SUBLANES - (taps - 1 - j) + r0, rows), :]
        dws[j] = dws[j] + jnp.sum(d_pre * xs, axis=0, keepdims=True)
    return jnp.sum(dx, axis=0, keepdims=True)


def _ml_conv_fwd(proj, conv_w, conv_b):
    s = proj.shape[0]
    nblk = 2 * D_GROUP // LANES

    def body(x_ref, w_ref, b_ref, o_ref, pad_ref):
        pad_ref[0:SUBLANES, :] = jnp.zeros((SUBLANES, LANES), F32)
        pad_ref[SUBLANES:, :] = x_ref[...]
        for r0 in range(0, s, ROW_TILE):
            rows = min(ROW_TILE, s - r0)
            o_ref[r0:r0 + rows, :] = jax.nn.silu(_conv_fwd_tile(pad_ref, w_ref, b_ref, r0, rows, ML_CONV))

    return pl.pallas_call(
        body, name="ml_conv_fwd", grid=(nblk,),
        in_specs=[pl.BlockSpec((s, LANES), lambda j: (0, SEG_MQ + j)),
                  pl.BlockSpec((ML_CONV, LANES), lambda j: (0, j)),
                  pl.BlockSpec((1, LANES), lambda j: (0, j))],
        out_specs=pl.BlockSpec((s, LANES), lambda j: (0, j)),
        out_shape=jax.ShapeDtypeStruct((s, 2 * D_GROUP), F32),
        scratch_shapes=[pltpu.VMEM((s + SUBLANES, LANES), F32)],
        compiler_params=_params(("parallel",)),
    )(proj, conv_w, conv_b)


def _ml_conv_bwd(proj, conv_w, conv_b, d_qk, d_proj):
    s = proj.shape[0]
    nblk = 2 * D_GROUP // LANES

    def body(x_ref, w_ref, b_ref, dy_ref, _, dx_ref, dw_ref, db_ref, dxs_ref, pad_ref, dpad_ref):
        pad_ref[0:SUBLANES, :] = jnp.zeros((SUBLANES, LANES), F32)
        pad_ref[SUBLANES:, :] = x_ref[...]
        dpad_ref[s:, :] = jnp.zeros((SUBLANES, LANES), F32)
        db = jnp.zeros((1, LANES), F32)
        for r0 in range(0, s, ROW_TILE):
            rows = min(ROW_TILE, s - r0)
            pre = _conv_fwd_tile(pad_ref, w_ref, b_ref, r0, rows, ML_CONV)
            _, vjp = jax.vjp(jax.nn.silu, pre)
            d_pre, = vjp(dy_ref[r0:r0 + rows, :])
            dpad_ref[r0:r0 + rows, :] = d_pre
            db = db + jnp.sum(d_pre, axis=0, keepdims=True)
        db_ref[...] = db
        dws = [jnp.zeros((1, LANES), F32) for _ in range(ML_CONV)]
        dx_sum = jnp.zeros((1, LANES), F32)
        for r0 in range(0, s, ROW_TILE):
            dx_sum = dx_sum + _conv_grads_tile(pad_ref, dpad_ref, dx_ref, w_ref, dws, r0, min(ROW_TILE, s - r0),
                                               ML_CONV)
        dxs_ref[...] = dx_sum
        for j in range(ML_CONV):
            dw_ref[j:j + 1, :] = dws[j]

    return pl.pallas_call(
        body, name="ml_conv_bwd", grid=(nblk,),
        in_specs=[pl.BlockSpec((s, LANES), lambda j: (0, SEG_MQ + j)),
                  pl.BlockSpec((ML_CONV, LANES), lambda j: (0, j)),
                  pl.BlockSpec((1, LANES), lambda j: (0, j)),
                  pl.BlockSpec((s, LANES), lambda j: (0, j)),
                  pl.BlockSpec(memory_space=pl.ANY)],
        out_specs=[pl.BlockSpec((s, LANES), lambda j: (0, SEG_MQ + j)),
                   pl.BlockSpec((ML_CONV, LANES), lambda j: (0, j)),
                   pl.BlockSpec((1, LANES), lambda j: (0, j)),
                   pl.BlockSpec((1, LANES), lambda j: (0, j))],
        out_shape=[jax.ShapeDtypeStruct(d_proj.shape, d_proj.dtype),
                   jax.ShapeDtypeStruct((ML_CONV, 2 * D_GROUP), F32),
                   jax.ShapeDtypeStruct((1, 2 * D_GROUP), F32),
                   jax.ShapeDtypeStruct((1, 2 * D_GROUP), F32)],
        input_output_aliases={4: 0},
        scratch_shapes=[pltpu.VMEM((s + SUBLANES, LANES), F32), pltpu.VMEM((s + SUBLANES, LANES), F32)],
        compiler_params=_params(("parallel",)),
    )(proj, conv_w, conv_b, d_qk, d_proj)


def _gelu_mul(a, b):
    return jax.nn.gelu(a) * b


FFN_BLOCKS = D_FF_P // LANES


def _ffn_conv_fwd(u, conv_w, conv_b):
    s = u.shape[0]

    def body(g_ref, v_ref, wg_ref, wv_ref, bg_ref, bv_ref, o_ref, gpad_ref, vpad_ref):
        for pad_ref, x_ref in ((gpad_ref, g_ref), (vpad_ref, v_ref)):
            pad_ref[0:SUBLANES, :] = jnp.zeros((SUBLANES, LANES), F32)
            pad_ref[SUBLANES:, :] = x_ref[...]
        for r0 in range(0, s, ROW_TILE):
            rows = min(ROW_TILE, s - r0)
            ug = _conv_fwd_tile(gpad_ref, wg_ref, bg_ref, r0, rows, FFN_CONV)
            uv = _conv_fwd_tile(vpad_ref, wv_ref, bv_ref, r0, rows, FFN_CONV)
            o_ref[r0:r0 + rows, :] = _gelu_mul(ug, uv).astype(o_ref.dtype)

    col = lambda off: (lambda j: (0, off + j))
    return pl.pallas_call(
        body, name="ffn_conv_fwd", grid=(FFN_BLOCKS,),
        in_specs=[pl.BlockSpec((s, LANES), col(0)), pl.BlockSpec((s, LANES), col(FFN_BLOCKS)),
                  pl.BlockSpec((FFN_CONV, LANES), col(0)), pl.BlockSpec((FFN_CONV, LANES), col(FFN_BLOCKS)),
                  pl.BlockSpec((1, LANES), col(0)), pl.BlockSpec((1, LANES), col(FFN_BLOCKS))],
        out_specs=pl.BlockSpec((s, LANES), col(0)),
        out_shape=jax.ShapeDtypeStruct((s, D_FF_P), BF16),
        scratch_shapes=[pltpu.VMEM((s + SUBLANES, LANES), F32), pltpu.VMEM((s + SUBLANES, LANES), F32)],
        compiler_params=_params(("parallel",)),
    )(u, u, conv_w, conv_w, conv_b, conv_b)


def _ffn_conv_bwd(u, conv_w, conv_b, d_h):
    s = u.shape[0]

    def body(g_ref, v_ref, wg_ref, wv_ref, bg_ref, bv_ref, dh_ref,
             dug_ref, duv_ref, dwg_ref, dwv_ref, dbg_ref, dbv_ref,
             gpad_ref, vpad_ref, dgpad_ref, dvpad_ref):
        for pad_ref, x_ref in ((gpad_ref, g_ref), (vpad_ref, v_ref)):
            pad_ref[0:SUBLANES, :] = jnp.zeros((SUBLANES, LANES), F32)
            pad_ref[SUBLANES:, :] = x_ref[...]
        dgpad_ref[s:, :] = jnp.zeros((SUBLANES, LANES), F32)
        dvpad_ref[s:, :] = jnp.zeros((SUBLANES, LANES), F32)
        dbg = jnp.zeros((1, LANES), F32)
        dbv = jnp.zeros((1, LANES), F32)
        for r0 in range(0, s, ROW_TILE):
            rows = min(ROW_TILE, s - r0)
            ug = _conv_fwd_tile(gpad_ref, wg_ref, bg_ref, r0, rows, FFN_CONV)
            uv = _conv_fwd_tile(vpad_ref, wv_ref, bv_ref, r0, rows, FFN_CONV)
            _, vjp = jax.vjp(_gelu_mul, ug, uv)
            d_ug, d_uv = vjp(dh_ref[r0:r0 + rows, :])
            dgpad_ref[r0:r0 + rows, :] = d_ug
            dvpad_ref[r0:r0 + rows, :] = d_uv
            dbg = dbg + jnp.sum(d_ug, axis=0, keepdims=True)
            dbv = dbv + jnp.sum(d_uv, axis=0, keepdims=True)
        dbg_ref[...] = dbg
        dbv_ref[...] = dbv
        for pad_ref, dpad_ref, w_ref, dx_ref, dw_ref in ((gpad_ref, dgpad_ref, wg_ref, dug_ref, dwg_ref),
                                                         (vpad_ref, dvpad_ref, wv_ref, duv_ref, dwv_ref)):
            dws = [jnp.zeros((1, LANES), F32) for _ in range(FFN_CONV)]
            for r0 in range(0, s, ROW_TILE):
                _conv_grads_tile(pad_ref, dpad_ref, dx_ref, w_ref, dws, r0, min(ROW_TILE, s - r0), FFN_CONV)
            for j in range(FFN_CONV):
                dw_ref[j:j + 1, :] = dws[j]

    col = lambda off: (lambda j: (0, off + j))
    seq = pl.BlockSpec((s, LANES), col(0))
    return pl.pallas_call(
        body, name="ffn_conv_bwd", grid=(FFN_BLOCKS,),
        in_specs=[pl.BlockSpec((s, LANES), col(0)), pl.BlockSpec((s, LANES), col(FFN_BLOCKS)),
                  pl.BlockSpec((FFN_CONV, LANES), col(0)), pl.BlockSpec((FFN_CONV, LANES), col(FFN_BLOCKS)),
                  pl.BlockSpec((1, LANES), col(0)), pl.BlockSpec((1, LANES), col(FFN_BLOCKS)), seq],
        out_specs=[seq, seq, pl.BlockSpec((FFN_CONV, LANES), col(0)), pl.BlockSpec((FFN_CONV, LANES), col(0)),
                   pl.BlockSpec((1, LANES), col(0)), pl.BlockSpec((1, LANES), col(0))],
        out_shape=[jax.ShapeDtypeStruct((s, D_FF_P), BF16), jax.ShapeDtypeStruct((s, D_FF_P), BF16),
                   jax.ShapeDtypeStruct((FFN_CONV, D_FF_P), F32), jax.ShapeDtypeStruct((FFN_CONV, D_FF_P), F32),
                   jax.ShapeDtypeStruct((1, D_FF_P), F32), jax.ShapeDtypeStruct((1, D_FF_P), F32)],
        scratch_shapes=[pltpu.VMEM((s + SUBLANES, LANES), F32) for _ in range(4)],
        compiler_params=_params(("parallel",)),
    )(u, u, conv_w, conv_w, conv_b, conv_b, d_h)


def _chunk_masks(c):
    row = lax.broadcasted_iota(jnp.int32, (c, c), 0)
    col = lax.broadcasted_iota(jnp.int32, (c, c), 1)
    return row, col


@jax.custom_vjp
def _split_heads(x):
    return tuple(x[:, h * D_HEAD:(h + 1) * D_HEAD] for h in range(N_HEADS))


_split_heads.defvjp(lambda x: (_split_heads(x), None), lambda _, gs: (jnp.concatenate(gs, axis=1),))


@jax.custom_vjp
def _merge_heads(xs):
    return jnp.concatenate(xs, axis=1)


_merge_heads.defvjp(lambda xs: (_merge_heads(xs), None), lambda _, g: (_split_heads(g),))

HEADS = range(N_HEADS)


def _hg_chunk(hq, hf, hi, hgate, l0, l1, nw, sts):
    c = hq.shape[0]
    row, col = _chunk_masks(c)
    mask = col <= row
    mx = lax.stop_gradient(jnp.maximum(l0, l1))
    e0 = jnp.exp(l0 - mx)
    e1 = jnp.exp(l1 - mx)
    lb = e0 / (e0 + e1)
    sig = jax.nn.sigmoid(hf)
    lf = jnp.log(lb + (1.0 - lb) * sig)
    k = (1.0 - lb) * jax.nn.sigmoid(-hf)
    q = jax.nn.silu(hq)
    b = _dg(mask.astype(F32), lf, 1, 0, HIGHEST)
    rid = lax.broadcasted_iota(jnp.int32, b.shape, 0)
    b_ref = jnp.sum(jnp.where(rid == c // 2 - 1, b, 0.0), axis=0, keepdims=True)
    b_last = jnp.sum(jnp.where(rid == c - 1, b, 0.0), axis=0, keepdims=True)
    qa = _split_heads(q * jnp.exp(b - b_ref))
    ka = _split_heads(k * jnp.exp(b_ref - b))
    qe = _split_heads(q * jnp.exp(b))
    kd = _split_heads(k * jnp.exp(b_last - b))
    decay = _split_heads(jnp.exp(b_last))
    v = _split_heads(hi)
    attn = [jnp.where(mask, _nt(qa[h], ka[h]), 0.0) for h in HEADS]
    intra = [_nn(attn[h], v[h]) for h in HEADS]
    inter = [_nt(qe[h], sts[h]) for h in HEADS]
    kv = [_tn(v[h], kd[h]) for h in HEADS]
    sts_new = tuple(decay[h] * sts[h] + kv[h] for h in HEADS)
    o = [intra[h] + inter[h] for h in HEADS]
    normed = _merge_heads(tuple(o[h] * lax.rsqrt(jnp.mean(o[h] * o[h], axis=-1, keepdims=True) + LN_EPS)
                                for h in HEADS))
    return normed * nw * jax.nn.silu(hgate), sts_new


def _seg(ref, seg):
    return ref[:, seg * D_GROUP:(seg + 1) * D_GROUP]


def _hgrn2_fwd(proj, logits, norm_w):
    s = proj.shape[0]
    nc = s // CHUNK

    def body(p_ref, lg_ref, nw_ref, y_ref, st_out_ref, st_scr):
        @pl.when(pl.program_id(0) == 0)
        def _():
            st_scr[...] = jnp.zeros_like(st_scr)

        sts = tuple(st_scr[h] for h in HEADS)
        y, sts_new = _hg_chunk(_seg(p_ref, 0), _seg(p_ref, 1), _seg(p_ref, 2), _seg(p_ref, 3),
                               lg_ref[0:1, :], lg_ref[1:2, :], nw_ref[...], sts)
        y_ref[...] = y.astype(y_ref.dtype)
        for h in HEADS:
            st_out_ref[h] = sts[h]
            st_scr[h] = sts_new[h]

    return pl.pallas_call(
        body, name="hgrn2_fwd", grid=(nc,),
        in_specs=[pl.BlockSpec((CHUNK, 4 * D_GROUP), lambda c: (c, 0)),
                  pl.BlockSpec((2, D_GROUP), lambda c: (0, 0)),
                  pl.BlockSpec((1, D_GROUP), lambda c: (0, 0))],
        out_specs=[pl.BlockSpec((CHUNK, D_GROUP), lambda c: (c, 0)),
                   pl.BlockSpec((None, N_HEADS, D_HEAD, D_HEAD), lambda c: (c, 0, 0, 0))],
        out_shape=[jax.ShapeDtypeStruct((s, 2 * D_GROUP), BF16),
                   jax.ShapeDtypeStruct((nc, N_HEADS, D_HEAD, D_HEAD), F32)],
        scratch_shapes=[pltpu.VMEM((N_HEADS, D_HEAD, D_HEAD), F32)],
        compiler_params=_params(("arbitrary",)),
    )(proj, logits, norm_w)


def _hgrn2_bwd(proj, logits, norm_w, states, d_y):
    s = proj.shape[0]
    nc = s // CHUNK

    def body(p_ref, lg_ref, nw_ref, st_ref, dy_ref, dp_ref, dl_ref, dnw_ref, dsum_ref, dst_scr):
        @pl.when(pl.program_id(0) == 0)
        def _():
            dst_scr[...] = jnp.zeros_like(dst_scr)
            dl_ref[...] = jnp.zeros_like(dl_ref)
            dnw_ref[...] = jnp.zeros_like(dnw_ref)
            dsum_ref[...] = jnp.zeros_like(dsum_ref)

        _, vjp = jax.vjp(_hg_chunk, _seg(p_ref, 0), _seg(p_ref, 1), _seg(p_ref, 2), _seg(p_ref, 3),
                         lg_ref[0:1, :], lg_ref[1:2, :], nw_ref[...], tuple(st_ref[h] for h in HEADS))
        d_hq, d_hf, d_hi, d_hg, d_l0, d_l1, d_nw, d_sts = vjp((dy_ref[...], tuple(dst_scr[h] for h in HEADS)))
        for seg, val in enumerate((d_hq, d_hf, d_hi, d_hg)):
            dp_ref[:, seg * D_GROUP:(seg + 1) * D_GROUP] = val.astype(dp_ref.dtype)
            dsum_ref[:, seg * D_GROUP:(seg + 1) * D_GROUP] += jnp.sum(val, axis=0, keepdims=True)
        dl_ref[0:1, :] += d_l0
        dl_ref[1:2, :] += d_l1
        dnw_ref[...] += d_nw
        for h in HEADS:
            dst_scr[h] = d_sts[h]

    rev = lambda c: nc - 1 - c
    return pl.pallas_call(
        body, name="hgrn2_bwd", grid=(nc,),
        in_specs=[pl.BlockSpec((CHUNK, 4 * D_GROUP), lambda c: (rev(c), 0)),
                  pl.BlockSpec((2, D_GROUP), lambda c: (0, 0)),
                  pl.BlockSpec((1, D_GROUP), lambda c: (0, 0)),
                  pl.BlockSpec((None, N_HEADS, D_HEAD, D_HEAD), lambda c: (rev(c), 0, 0, 0)),
                  pl.BlockSpec((CHUNK, D_GROUP), lambda c: (rev(c), 0))],
        out_specs=[pl.BlockSpec((CHUNK, 4 * D_GROUP), lambda c: (rev(c), 0)),
                   pl.BlockSpec((2, D_GROUP), lambda c: (0, 0)),
                   pl.BlockSpec((1, D_GROUP), lambda c: (0, 0)),
                   pl.BlockSpec((1, 4 * D_GROUP), lambda c: (0, 0))],
        out_shape=[jax.ShapeDtypeStruct((s, D_IN_MAIN), BF16), jax.ShapeDtypeStruct((2, D_GROUP), F32),
                   jax.ShapeDtypeStruct((1, D_GROUP), F32), jax.ShapeDtypeStruct((1, 4 * D_GROUP), F32)],
        scratch_shapes=[pltpu.VMEM((N_HEADS, D_HEAD, D_HEAD), F32)],
        compiler_params=_params(("arbitrary",)),
    )(proj, logits, norm_w, states, d_y)


def _gate_column(gates, lane, idx):
    return jnp.sum(jnp.where(lane == idx, gates, 0.0), axis=1, keepdims=True)


def _head_layer_norm(h):
    mu = jnp.mean(h, axis=-1, keepdims=True)
    var = jnp.mean(jnp.square(h - mu), axis=-1, keepdims=True)
    return (h - mu) * lax.rsqrt(var + LN_EPS)


def _ml_chunk(qc, kc, v, mo, gates, nw, cts, ns, ms):
    c = qc.shape[0]
    row, col = _chunk_masks(c)
    mask = col <= row
    eye = col == row
    lane = lax.broadcasted_iota(jnp.int32, gates.shape, 1)
    to_row = lambda t: jnp.sum(jnp.where(eye, t, 0.0), axis=0, keepdims=True)
    q = _split_heads(qc * (D_HEAD ** -0.5))
    k = _split_heads(kc)
    vs = _split_heads(v)
    ig = [_gate_column(gates, lane, h) for h in HEADS]
    lf = [jax.nn.log_sigmoid(_gate_column(gates, lane, N_HEADS + h)) for h in HEADS]
    lf_row = [to_row(lf[h]) for h in HEADS]
    ig_row = [to_row(ig[h]) for h in HEADS]
    b_col = [jnp.sum(jnp.where(mask, lf_row[h], 0.0), axis=1, keepdims=True) for h in HEADS]
    b_row = [jnp.sum(jnp.where(row <= col, lf[h], 0.0), axis=0, keepdims=True) for h in HEADS]
    g = [jnp.sum(lf[h], axis=0, keepdims=True) for h in HEADS]
    d = [jnp.where(mask, b_col[h] - b_row[h] + ig_row[h], -jnp.inf) for h in HEADS]
    inter = [b_col[h] + ms[h] for h in HEADS]
    m_t = [lax.stop_gradient(jnp.maximum(inter[h], jnp.max(d[h], axis=1, keepdims=True))) for h in HEADS]
    qk = [_nt(q[h], k[h]) for h in HEADS]
    qc_state = [_nt(q[h], cts[h]) for h in HEADS]
    sc = [qk[h] * jnp.exp(d[h] - m_t[h]) for h in HEADS]
    w_inter = [jnp.exp(inter[h] - m_t[h]) for h in HEADS]
    sv = [_nn(sc[h], vs[h]) for h in HEADS]
    num = [sv[h] + w_inter[h] * qc_state[h] for h in HEADS]
    den = [jnp.sum(sc[h], axis=1, keepdims=True) + w_inter[h] * jnp.sum(q[h] * ns[h], axis=1, keepdims=True)
           for h in HEADS]
    hh = [num[h] / jnp.maximum(jnp.abs(den[h]), jnp.exp(-m_t[h])) for h in HEADS]
    a = [g[h] - b_col[h] + ig[h] for h in HEADS]
    ms_new = tuple(lax.stop_gradient(jnp.maximum(g[h] + ms[h], jnp.max(a[h], axis=0, keepdims=True)))
                   for h in HEADS)
    decay = [jnp.exp(g[h] + ms[h] - ms_new[h]) for h in HEADS]
    wk = [k[h] * jnp.exp(a[h] - ms_new[h]) for h in HEADS]
    kv = [_tn(vs[h], wk[h]) for h in HEADS]
    cts_new = tuple(decay[h] * cts[h] + kv[h] for h in HEADS)
    ns_new = tuple(decay[h] * ns[h] + jnp.sum(wk[h], axis=0, keepdims=True) for h in HEADS)
    normed = _merge_heads(tuple(_head_layer_norm(hh[h]) for h in HEADS))
    return jax.nn.sigmoid(mo) * (normed * nw), cts_new, ns_new, ms_new


def _mlstm_fwd(qk, proj, gates, norm_w, y):
    s = proj.shape[0]
    nc = s // CHUNK

    def body(qk_ref, vo_ref, g_ref, nw_ref, _, y_ref, ct_out, n_out, m_out, ct_scr, n_scr, m_scr):
        @pl.when(pl.program_id(0) == 0)
        def _():
            ct_scr[...] = jnp.zeros_like(ct_scr)
            n_scr[...] = jnp.zeros_like(n_scr)
            m_scr[...] = jnp.full(m_scr.shape, NEG_BIG, F32)

        cts = tuple(ct_scr[h] for h in HEADS)
        ns = tuple(n_scr[h] for h in HEADS)
        ms = tuple(m_scr[h] for h in HEADS)
        y, cts_new, ns_new, ms_new = _ml_chunk(_seg(qk_ref, 0), _seg(qk_ref, 1), _seg(vo_ref, 0), _seg(vo_ref, 1),
                                               g_ref[...], nw_ref[...], cts, ns, ms)
        y_ref[...] = y.astype(y_ref.dtype)
        for h in HEADS:
            ct_out[h], n_out[h], m_out[h] = cts[h], ns[h], ms[h]
            ct_scr[h], n_scr[h], m_scr[h] = cts_new[h], ns_new[h], ms_new[h]

    st = lambda r, w: pl.BlockSpec((None, N_HEADS, r, w), lambda c: (c, 0, 0, 0))
    return pl.pallas_call(
        body, name="mlstm_fwd", grid=(nc,),
        in_specs=[pl.BlockSpec((CHUNK, 2 * D_GROUP), lambda c: (c, 0)),
                  pl.BlockSpec((CHUNK, 2 * D_GROUP), lambda c: (c, 3)),
                  pl.BlockSpec((CHUNK, LANES), lambda c: (c, 0)),
                  pl.BlockSpec((1, D_GROUP), lambda c: (0, 0)),
                  pl.BlockSpec(memory_space=pl.ANY)],
        out_specs=[pl.BlockSpec((CHUNK, D_GROUP), lambda c: (c, 1)),
                   st(D_HEAD, D_HEAD), st(1, D_HEAD), st(1, 1)],
        out_shape=[jax.ShapeDtypeStruct(y.shape, y.dtype),
                   jax.ShapeDtypeStruct((nc, N_HEADS, D_HEAD, D_HEAD), F32),
                   jax.ShapeDtypeStruct((nc, N_HEADS, 1, D_HEAD), F32),
                   jax.ShapeDtypeStruct((nc, N_HEADS, 1, 1), F32)],
        input_output_aliases={4: 0},
        scratch_shapes=[pltpu.VMEM((N_HEADS, D_HEAD, D_HEAD), F32), pltpu.VMEM((N_HEADS, 1, D_HEAD), F32),
                        pltpu.VMEM((N_HEADS, 1, 1), F32)],
        compiler_params=_params(("arbitrary",)),
    )(qk, proj, gates, norm_w, y)


def _mlstm_bwd(qk, proj, gates, norm_w, ct_s, n_s, m_s, d_y, d_proj):
    s = proj.shape[0]
    nc = s // CHUNK

    def body(qk_ref, vo_ref, g_ref, nw_ref, ct_ref, n_ref, m_ref, dy_ref, _,
             dp_ref, dqk_ref, dg_ref, dnw_ref, dsum_ref, dct_scr, dn_scr):
        @pl.when(pl.program_id(0) == 0)
        def _():
            dct_scr[...] = jnp.zeros_like(dct_scr)
            dn_scr[...] = jnp.zeros_like(dn_scr)
            dnw_ref[...] = jnp.zeros_like(dnw_ref)
            dsum_ref[...] = jnp.zeros_like(dsum_ref)

        ms = tuple(m_ref[h] for h in HEADS)
        step = lambda *a: _ml_chunk(*a, ms)[:3]
        _, vjp = jax.vjp(step, _seg(qk_ref, 0), _seg(qk_ref, 1), _seg(vo_ref, 0), _seg(vo_ref, 1), g_ref[...],
                         nw_ref[...], tuple(ct_ref[h] for h in HEADS), tuple(n_ref[h] for h in HEADS))
        d_q, d_k, d_v, d_o, d_gates, d_nw, d_cts, d_ns = vjp(
            (dy_ref[...], tuple(dct_scr[h] for h in HEADS), tuple(dn_scr[h] for h in HEADS)))
        dqk_ref[:, 0:D_GROUP] = d_q
        dqk_ref[:, D_GROUP:2 * D_GROUP] = d_k
        for seg, val in enumerate((d_v, d_o)):
            dp_ref[:, seg * D_GROUP:(seg + 1) * D_GROUP] = val.astype(dp_ref.dtype)
            dsum_ref[:, seg * D_GROUP:(seg + 1) * D_GROUP] += jnp.sum(val, axis=0, keepdims=True)
        dg_ref[...] = d_gates
        dnw_ref[...] += d_nw
        for h in HEADS:
            dct_scr[h] = d_cts[h]
            dn_scr[h] = d_ns[h]

    rev = lambda c: nc - 1 - c
    st = lambda r, w: pl.BlockSpec((None, N_HEADS, r, w), lambda c: (rev(c), 0, 0, 0))
    return pl.pallas_call(
        body, name="mlstm_bwd", grid=(nc,),
        in_specs=[pl.BlockSpec((CHUNK, 2 * D_GROUP), lambda c: (rev(c), 0)),
                  pl.BlockSpec((CHUNK, 2 * D_GROUP), lambda c: (rev(c), 3)),
                  pl.BlockSpec((CHUNK, LANES), lambda c: (rev(c), 0)),
                  pl.BlockSpec((1, D_GROUP), lambda c: (0, 0)),
                  st(D_HEAD, D_HEAD), st(1, D_HEAD), st(1, 1),
                  pl.BlockSpec((CHUNK, D_GROUP), lambda c: (rev(c), 1)),
                  pl.BlockSpec(memory_space=pl.ANY)],
        out_specs=[pl.BlockSpec((CHUNK, 2 * D_GROUP), lambda c: (rev(c), 3)),
                   pl.BlockSpec((CHUNK, 2 * D_GROUP), lambda c: (rev(c), 0)),
                   pl.BlockSpec((CHUNK, LANES), lambda c: (rev(c), 0)),
                   pl.BlockSpec((1, D_GROUP), lambda c: (0, 0)),
                   pl.BlockSpec((1, 2 * D_GROUP), lambda c: (0, 0))],
        out_shape=[jax.ShapeDtypeStruct(d_proj.shape, d_proj.dtype), jax.ShapeDtypeStruct((s, 2 * D_GROUP), F32),
                   jax.ShapeDtypeStruct((s, LANES), F32), jax.ShapeDtypeStruct((1, D_GROUP), F32),
                   jax.ShapeDtypeStruct((1, 2 * D_GROUP), F32)],
        input_output_aliases={8: 0},
        scratch_shapes=[pltpu.VMEM((N_HEADS, D_HEAD, D_HEAD), F32), pltpu.VMEM((N_HEADS, 1, D_HEAD), F32)],
        compiler_params=_params(("arbitrary",)),
    )(qk, proj, gates, norm_w, ct_s, n_s, m_s, d_y, d_proj)


LN_TOKENS = 512
ATT_TOKENS = 256


def _res_ln_fwd(xres, branch, g, b, name):
    s, dm = xres.shape
    tb = min(LN_TOKENS, s)

    def body(x_ref, br_ref, g_ref, b_ref, o_ref):
        o_ref[...] = _layer_norm(ALPHA * x_ref[...] + br_ref[...], g_ref[...], b_ref[...])

    tok = pl.BlockSpec((tb, dm), lambda i: (i, 0))
    vec = pl.BlockSpec((1, dm), lambda i: (0, 0))
    return pl.pallas_call(
        body, name=name, grid=(s // tb,), in_specs=[tok, tok, vec, vec], out_specs=tok,
        out_shape=jax.ShapeDtypeStruct((s, dm), F32), compiler_params=_params(("parallel",)),
    )(xres, branch, g, b)


def _res_ln_bwd(xres, branch, g, b, d_out, name):
    s, dm = xres.shape
    tb = min(LN_TOKENS, s)

    def body(x_ref, br_ref, g_ref, b_ref, do_ref, dz_ref, dg_ref, db_ref):
        @pl.when(pl.program_id(0) == 0)
        def _():
            dg_ref[...] = jnp.zeros_like(dg_ref)
            db_ref[...] = jnp.zeros_like(db_ref)

        z = ALPHA * x_ref[...] + br_ref[...]
        _, vjp = jax.vjp(_layer_norm, z, g_ref[...], b_ref[...])
        d_z, d_g, d_b = vjp(do_ref[...])
        dz_ref[...] = d_z
        dg_ref[...] += d_g
        db_ref[...] += d_b

    tok = pl.BlockSpec((tb, dm), lambda i: (i, 0))
    vec = pl.BlockSpec((1, dm), lambda i: (0, 0))
    return pl.pallas_call(
        body, name=name, grid=(s // tb,), in_specs=[tok, tok, vec, vec, tok], out_specs=[tok, vec, vec],
        out_shape=[jax.ShapeDtypeStruct((s, dm), F32), jax.ShapeDtypeStruct((1, dm), F32),
                   jax.ShapeDtypeStruct((1, dm), F32)],
        compiler_params=_params(("arbitrary",)),
    )(xres, branch, g, b, d_out)


def _loss_tail(xres, branch, g, b, target):
    s, dm = xres.shape
    tb = min(LN_TOKENS, s)

    def loss_fn(z, gg, bb, tgt):
        err = jnp.square(_layer_norm(z, gg, bb) - tgt)
        return 0.5 * jnp.sum(jnp.mean(err, axis=-1, keepdims=True), axis=0, keepdims=True)

    def body(x_ref, br_ref, g_ref, b_ref, t_ref, loss_ref, dz_ref, dg_ref, db_ref):
        @pl.when(pl.program_id(0) == 0)
        def _():
            loss_ref[...] = jnp.zeros_like(loss_ref)
            dg_ref[...] = jnp.zeros_like(dg_ref)
            db_ref[...] = jnp.zeros_like(db_ref)

        z = ALPHA * x_ref[...] + br_ref[...]
        tgt = t_ref[...]
        loss, vjp = jax.vjp(lambda zz, gg, bb: loss_fn(zz, gg, bb, tgt), z, g_ref[...], b_ref[...])
        d_z, d_g, d_b = vjp(jnp.ones((1, 1), F32))
        loss_ref[...] += loss
        dz_ref[...] = d_z
        dg_ref[...] += d_g
        db_ref[...] += d_b

    tok = pl.BlockSpec((tb, dm), lambda i: (i, 0))
    vec = pl.BlockSpec((1, dm), lambda i: (0, 0))
    one = pl.BlockSpec((1, 1), lambda i: (0, 0))
    return pl.pallas_call(
        body, name="loss_tail", grid=(s // tb,), in_specs=[tok, tok, vec, vec, tok],
        out_specs=[one, tok, vec, vec],
        out_shape=[jax.ShapeDtypeStruct((1, 1), F32), jax.ShapeDtypeStruct((s, dm), F32),
                   jax.ShapeDtypeStruct((1, dm), F32), jax.ShapeDtypeStruct((1, dm), F32)],
        compiler_params=_params(("arbitrary",)),
    )(xres, branch, g, b, target)


def _att_head(q, k, v):
    sc = _nt(q, k) * (CA_DH ** -0.5)
    return _nn(jax.nn.softmax(sc, axis=-1), v)


def _att_fwd(q, kv):
    s = q.shape[0]
    tb = min(ATT_TOKENS, s)

    def body(q_ref, kv_ref, o_ref):
        for h in range(CA_HEADS):
            lo = h * CA_DH
            o_ref[:, lo:lo + CA_DH] = _att_head(q_ref[:, lo:lo + CA_DH], kv_ref[:, lo:lo + CA_DH],
                                                kv_ref[:, D_MODEL + lo:D_MODEL + lo + CA_DH]).astype(o_ref.dtype)

    tok = pl.BlockSpec((tb, D_MODEL), lambda i: (i, 0))
    return pl.pallas_call(
        body, name="att_fwd", grid=(s // tb,),
        in_specs=[tok, pl.BlockSpec((N_MEM, 2 * D_MODEL), lambda i: (0, 0))], out_specs=tok,
        out_shape=jax.ShapeDtypeStruct((s, D_MODEL), BF16), compiler_params=_params(("parallel",)),
    )(q, kv)


def _att_bwd(q, kv, d_o):
    s = q.shape[0]
    tb = min(ATT_TOKENS, s)

    def body(q_ref, kv_ref, do_ref, dq_ref, dkv_ref):
        @pl.when(pl.program_id(0) == 0)
        def _():
            dkv_ref[...] = jnp.zeros_like(dkv_ref)

        for h in range(CA_HEADS):
            lo = h * CA_DH
            vlo = D_MODEL + lo
            _, vjp = jax.vjp(_att_head, q_ref[:, lo:lo + CA_DH], kv_ref[:, lo:lo + CA_DH],
                             kv_ref[:, vlo:vlo + CA_DH])
            d_q, d_k, d_v = vjp(do_ref[:, lo:lo + CA_DH])
            dq_ref[:, lo:lo + CA_DH] = d_q
            dkv_ref[:, lo:lo + CA_DH] += d_k
            dkv_ref[:, vlo:vlo + CA_DH] += d_v

    tok = pl.BlockSpec((tb, D_MODEL), lambda i: (i, 0))
    mem = pl.BlockSpec((N_MEM, 2 * D_MODEL), lambda i: (0, 0))
    return pl.pallas_call(
        body, name="att_bwd", grid=(s // tb,), in_specs=[tok, mem, tok], out_specs=[tok, mem],
        out_shape=[jax.ShapeDtypeStruct((s, D_MODEL), F32), jax.ShapeDtypeStruct((N_MEM, 2 * D_MODEL), F32)],
        compiler_params=_params(("arbitrary",)),
    )(q, kv, d_o)


def _local_step(x, mem, target, w, mid_weights=None, ffn_weights=None, on_ffn_grads=None, on_mid_grads=None,
                on_small_grads=None, on_last_grads=None):
    w = dict(w)
    s = x.shape[0]
    tm = min(512, s)
    tt = min(512, s)
    proj = _matmul_nn(x, w["w_in_main"], w["b_in_main"], min(2048, s), 512, "proj")
    gates = _matmul_nn(x, w["w_in_gate"], w["b_in_gate"], tm, LANES, "proj_gates")
    qk = _ml_conv_fwd(proj, w["ml_conv_w"], w["ml_conv_b"])
    y, hg_states = _hgrn2_fwd(proj, w["hg_lb_logits"], w["hg_norm_w"])
    y, ct_s, n_s, m_s = _mlstm_fwd(qk, proj, gates, w["ml_norm_w"], y)
    if mid_weights is not None:
        w.update(mid_weights(y))
    mix =_matmul_nn(y, w["w_out"], None, tm, D_MODEL, "mix")
    x1 = _res_ln_fwd(x, mix, w["ln1_g"], w["ln1_b"], "ln1_fwd")
    kv = _matmul_nn(mem, w["ca_wkv"], None, N_MEM, CA_DH, "kv")
    q = _matmul_nn(x1, w["ca_wq"], None, tm, D_MODEL, "ca_q")
    att = _att_fwd(q, kv)
    ca = _matmul_nn(att, w["ca_wo"], None, tm, D_MODEL, "ca_out")
    x2 = _res_ln_fwd(x1, ca, w["ln2_g"], w["ln2_b"], "ln2_fwd")
    if ffn_weights is not None:
        w.update(ffn_weights(x2))
    u = _matmul_nn(x2, w["ffn_w_up"], None, min(2048, s), UP_SHARD_P, "ffn_up")
    hid = _ffn_conv_fwd(u, w["ffn_conv_w"], w["ffn_conv_b"])
    ff = _matmul_nn(hid, w["ffn_w_down"], None, tm, D_MODEL, "ffn_down")
    loss, d_z3, d_ln3_g, d_ln3_b = _loss_tail(x2, ff, w["ln3_g"], w["ln3_b"], target)
    grads = {"ln3_g": d_ln3_g, "ln3_b": d_ln3_b}
    grads["ffn_w_down"] = _matmul_tn(hid, d_z3, 1536, D_MODEL, tt, "d_w_down")
    d_hid = _matmul_nt([(d_z3, w["ffn_w_down"])], None, 1.0, tm, D_FF_P, "d_hid")
    d_ug, d_uv, d_cwg, d_cwv, d_cbg, d_cbv = _ffn_conv_bwd(u, w["ffn_conv_w"], w["ffn_conv_b"], d_hid)
    grads["ffn_conv_w"] = jnp.concatenate([d_cwg, d_cwv], axis=-1)
    grads["ffn_conv_b"] = jnp.concatenate([d_cbg, d_cbv], axis=-1)
    half = N_DEV // 2
    d_w_up = _matmul_tn(x2, d_ug, D_MODEL, UP_SHARD_P, tt, "d_w_up_gate", shards=N_DEV, group=half)
    grads["ffn_w_up"] = _matmul_tn(x2, d_uv, D_MODEL, UP_SHARD_P, tt, "d_w_up_val", shards=N_DEV,
                                   shard0=half, group=half, into=d_w_up)
    d_x2 = _matmul_nt([(d_ug, w["ffn_w_up"], 0), (d_uv, w["ffn_w_up"], N_DEV // 2)], d_z3, ALPHA,
                      min(256, s), D_MODEL, "d_x2")
    if on_ffn_grads is not None:
        d_x2 = on_ffn_grads(grads, d_x2)
    d_z2, grads["ln2_g"], grads["ln2_b"] = _res_ln_bwd(x1, ca, w["ln2_g"], w["ln2_b"], d_x2, "ln2_bwd")
    grads["ca_wo"] = _matmul_tn(att, d_z2, D_MODEL, D_MODEL, tt, "d_ca_wo")
    d_att = _matmul_nt([(d_z2, w["ca_wo"])], None, 1.0, tm, D_MODEL, "d_att")
    d_q, d_kv = _att_bwd(q, kv, d_att)
    grads["ca_wq"] = _matmul_tn(x1, d_q, D_MODEL, D_MODEL, tt, "d_ca_wq")
    grads["ca_wkv"] = _matmul_tn(mem, d_kv, D_MODEL, CA_DH, N_MEM, "d_ca_wkv", shards=N_DEV, group=N_DEV)
    d_x1 = _matmul_nt([(d_q, w["ca_wq"])], d_z2, ALPHA, tm, D_MODEL, "d_x1")
    d_z1, grads["ln1_g"], grads["ln1_b"] = _res_ln_bwd(x, mix, w["ln1_g"], w["ln1_b"], d_x1, "ln1_bwd")
    grads["w_out"] = _matmul_tn(y, d_z1, D_MODEL, D_MODEL, tt, "d_w_out")
    if on_mid_grads is not None:
        d_z1 = on_mid_grads(grads, d_z1)
    d_y = _matmul_nt([(d_z1, w["w_out"])], None, 1.0, tm, D_MODEL, "d_y")
    d_proj, grads["hg_lb_logits"], grads["hg_norm_w"], db_hg = _hgrn2_bwd(
        proj, w["hg_lb_logits"], w["hg_norm_w"], hg_states, d_y)
    d_proj, d_qk, d_gates, grads["ml_norm_w"], db_vo = _mlstm_bwd(
        qk, proj, gates, w["ml_norm_w"], ct_s, n_s, m_s, d_y, d_proj)
    d_proj, grads["ml_conv_w"], grads["ml_conv_b"], db_qk = _ml_conv_bwd(
        proj, w["ml_conv_w"], w["ml_conv_b"], d_qk, d_proj)
    grads["b_in_main"] = jnp.concatenate([db_hg, db_qk, db_vo], axis=-1)
    grads["w_in_gate"], grads["b_in_gate"] = _matmul_tn(x, d_gates, D_MODEL, LANES, tt, "d_w_in_gates", colsum=True)
    if on_small_grads is not None:
        d_proj = on_small_grads(grads, loss, d_proj)
    grads["w_in_main"] = _matmul_tn(x, d_proj, D_MODEL, min(2048, D_IN_MAIN), tt, "d_w_in")
    if on_last_grads is not None:
        d_z1 = on_last_grads(grads, d_z1)
    grad_x = _matmul_nt([(d_proj, w["w_in_main"]), (d_gates, w["w_in_gate"])], d_z1, ALPHA, tm, D_MODEL, "d_x")
    return loss, grad_x, grads


HBM_SPEC = pl.BlockSpec(memory_space=pltpu.HBM)


def _coords():
    return lax.axis_index("x"), lax.axis_index("y"), lax.axis_index("c")


def _other_chips(x, y):
    return [(1 - x, y), (x, 1 - y), (1 - x, 1 - y)]


def _all_gather_two_level(shards, name):
    na = len(shards)

    def body(*refs):
        x_refs, out_refs = refs[:na], refs[na:2 * na]
        send_sems, recv_sems, local_sems = refs[2 * na:]
        x, y, c = _coords()
        me, sibling = (x, y, c), (x, y, 1 - c)
        chips = _other_chips(x, y)

        def copy(a, k, block, to, own=False):
            slot = out_refs[a].at[4 * block[0] + 2 * block[1] + block[2]]
            return pltpu.make_async_remote_copy(
                src_ref=x_refs[a] if own else slot, dst_ref=slot,
                send_sem=send_sems.at[7 * a + k], recv_sem=recv_sems.at[7 * a + k],
                device_id=to, device_id_type=MESH)

        mine = [pltpu.make_async_copy(x_refs[a], out_refs[a].at[4 * x + 2 * y + c], local_sems.at[a])
                for a in range(na)]
        for cp in mine:
            cp.start()
        first = []
        for a in range(na):
            first.append(copy(a, 0, me, sibling, own=True))
            first += [copy(a, 1 + j, me, (*chip, c), own=True) for j, chip in enumerate(chips)]
        for cp in first:
            cp.start()
        passed = []
        for j, chip in enumerate(chips):
            for a in range(na):
                copy(a, 1 + j, (*chip, c), me).wait_recv()
                fwd = copy(a, 4 + j, (*chip, c), sibling)
                fwd.start()
                passed.append(fwd)
        for a in range(na):
            copy(a, 0, sibling, me).wait_recv()
            for j, chip in enumerate(chips):
                copy(a, 4 + j, (*chip, 1 - c), me).wait_recv()
        for cp in first + passed:
            cp.wait_send()
        for cp in mine:
            cp.wait()

    return pl.pallas_call(
        body, name=name,
        out_shape=[jax.ShapeDtypeStruct((N_DEV,) + t.shape, t.dtype) for t in shards],
        in_specs=[HBM_SPEC] * na, out_specs=[HBM_SPEC] * na,
        scratch_shapes=[pltpu.SemaphoreType.DMA((7 * na,)), pltpu.SemaphoreType.DMA((7 * na,)),
                        pltpu.SemaphoreType.DMA((na,))],
    )(*shards)


SEM_SPEC = pl.BlockSpec(memory_space=pltpu.SEMAPHORE)
ANY_SPEC = pl.BlockSpec(memory_space=pl.ANY)
SIDE_EFFECT = pltpu.SideEffectType.DATAFLOW_SIDE_EFFECTING


def _peer(x, y, c, d):
    flip = lambda v, bit: 1 - v if bit else v
    p = (flip(x, d & 4), flip(y, d & 2), flip(c, d & 1))
    return p, 4 * p[0] + 2 * p[1] + p[2]


def _direct_copies(gather, src_refs, land_refs, send_sems, recv_sems):
    x, y, c = _coords()
    me = 4 * x + 2 * y + c
    copies = []
    for a in range(len(src_refs)):
        for d in range(1, N_DEV):
            peer, peer_slot = _peer(x, y, c, d)
            copies.append(pltpu.make_async_remote_copy(
                src_ref=src_refs[a] if gather else src_refs[a].at[peer_slot],
                dst_ref=land_refs[a].at[me] if gather else land_refs[a].at[d - 1],
                send_sem=send_sems.at[7 * a + d - 1], recv_sem=recv_sems.at[7 * a + d - 1],
                device_id=peer, device_id_type=MESH))
    return copies


def _hbm(t):
    return pltpu.HBM(t.shape, t.dtype)


def _direct_start(gather, arrays, through, name):
    na = len(arrays)
    lands = [lax.empty((N_DEV,) + t.shape if gather else (N_DEV - 1,) + t.shape[1:], t.dtype) for t in arrays]
    n_io = 2 * na + 1

    def body(*refs):
        for cp in _direct_copies(gather, refs[:na], refs[na:2 * na], refs[n_io], refs[n_io + 1]):
            cp.start()

    ins = [pltpu.with_memory_space_constraint(t, pltpu.HBM) for t in (*arrays, *lands, through)]
    sems = pltpu.SemaphoreType.DMA((7 * na,))
    res = pl.pallas_call(
        body, name=name, out_shape=(sems, sems, *[_hbm(t) for t in ins]),
        in_specs=[HBM_SPEC] * n_io, out_specs=(SEM_SPEC, SEM_SPEC, *[HBM_SPEC] * n_io),
        input_output_aliases={i: 2 + i for i in range(n_io)},
        compiler_params=pltpu.CompilerParams(has_side_effects=SIDE_EFFECT),
    )(*ins)
    return (res[0], res[1], list(res[2:2 + na]), list(res[2 + na:2 + 2 * na])), res[2 + 2 * na]


def _direct_wait(gather, started, after, name):
    send_sems, recv_sems, arrays, lands = started
    na = len(arrays)

    def body(*refs):
        for cp in _direct_copies(gather, refs[:na], refs[na:2 * na], refs[2 * na], refs[2 * na + 1]):
            cp.wait_send()
            cp.wait_recv()

    res = pl.pallas_call(
        body, name=name, out_shape=tuple(_hbm(t) for t in (*arrays, *lands)),
        in_specs=[HBM_SPEC] * (2 * na) + [SEM_SPEC, SEM_SPEC, ANY_SPEC], out_specs=tuple([HBM_SPEC] * (2 * na)),
        input_output_aliases={i: i for i in range(2 * na)},
        compiler_params=pltpu.CompilerParams(has_side_effects=SIDE_EFFECT),
    )(*arrays, *lands, send_sems, recv_sems, after)
    return list(res[:na]), list(res[na:])


def _row_tile(rows):
    for t in (256, 176, 128):
        if rows % t == 0 and rows > t:
            return t
    return rows


def _adamw_math(g, w, m, v):
    m_new = ADAM_B1 * m + (1.0 - ADAM_B1) * g
    v_new = ADAM_B2 * v + (1.0 - ADAM_B2) * jnp.square(g)
    m_hat = m_new / (1.0 - ADAM_B1 ** ADAM_STEP)
    v_hat = v_new / (1.0 - ADAM_B2 ** ADAM_STEP)
    delta = -ADAM_LR * (m_hat / (jnp.sqrt(v_hat) + ADAM_EPS) + ADAM_WD * w)
    return delta, m_new, v_new


def _adamw_sharded(chip, sums, got, w, m, v, name):
    r, c = w.shape
    tr = _row_tile(r)
    n_got = got.shape[0]

    def body(chip_ref, s_ref, g_ref, w_ref, m_ref, v_ref, go_ref, d_ref, nm_ref, nv_ref):
        g = s_ref[...].astype(F32)
        for i in range(n_got):
            g = g + g_ref[i].astype(F32)
        delta, m_new, v_new = _adamw_math(g, w_ref[...], m_ref[...], v_ref[...])
        go_ref[...] = g
        d_ref[...] = delta
        nm_ref[...] = m_new
        nv_ref[...] = v_new

    blk = pl.BlockSpec((tr, c), lambda i, chip_ref: (i, 0))
    out = jax.ShapeDtypeStruct((r, c), F32)
    return pl.pallas_call(
        body, name=name,
        grid_spec=pltpu.PrefetchScalarGridSpec(
            num_scalar_prefetch=1, grid=(r // tr,),
            in_specs=[pl.BlockSpec((None, tr, c), lambda i, chip_ref: (chip_ref[0], i, 0)),
                      pl.BlockSpec((n_got, tr, c), lambda i, chip_ref: (0, i, 0)), blk, blk, blk],
            out_specs=[blk, blk, blk, blk]),
        out_shape=[out, out, out, out],
        compiler_params=_params(("parallel",)),
    )(chip, sums, got, w, m, v)


def _adamw_replicated(parts, w, m, v):
    p, r, c = parts.shape

    def body(p_ref, w_ref, m_ref, v_ref, g_ref, d_ref, nm_ref, nv_ref):
        g = p_ref[0]
        for i in range(1, p):
            g = g + p_ref[i]
        delta, m_new, v_new = _adamw_math(g, w_ref[...], m_ref[...], v_ref[...])
        g_ref[...] = g
        d_ref[...] = delta
        nm_ref[...] = m_new
        nv_ref[...] = v_new

    blk = pl.BlockSpec((r, c), lambda i: (0, 0))
    out = jax.ShapeDtypeStruct((r, c), F32)
    return pl.pallas_call(
        body, name="adamw_replicated", grid=(1,),
        in_specs=[pl.BlockSpec((p, r, c), lambda i: (0, 0, 0)), blk, blk, blk],
        out_specs=[blk, blk, blk, blk], out_shape=[out, out, out, out],
        compiler_params=_params(("arbitrary",)),
    )(parts, w, m, v)


SHARDED_NAMES = ("w_in", "ml_conv_w", "w_out", "ca_wq", "ca_wkv", "ca_wo", "ffn_w_up", "ffn_conv_w", "ffn_w_down")
SMALL_NAMES = ("b_in", "hg_lb_logits", "hg_norm_w", "ml_conv_b", "ml_norm_w", "ln1_g", "ln1_b",
               "ln2_g", "ln2_b", "ffn_conv_b", "ln3_g", "ln3_b")
WEIGHT_NAMES = ("w_in", "b_in", "hg_lb_logits", "hg_norm_w", "ml_conv_w", "ml_conv_b", "ml_norm_w", "w_out",
                "ln1_g", "ln1_b", "ca_wq", "ca_wkv", "ca_wo", "ln2_g", "ln2_b", "ffn_w_up", "ffn_conv_w",
                "ffn_conv_b", "ffn_w_down", "ln3_g", "ln3_b")
PAD_TO = {"w_in": W_IN_SHARD_P, "ffn_w_up": UP_SHARD_P, "ffn_conv_w": UP_SHARD_P}
SMALL_ROWS = 24
SMALL_W = D_MODEL


def _shard_2d(name, block):
    t = block[0]
    if name in PAD_TO:
        t = jnp.pad(t, ((0, 0), (0, PAD_TO[name] - t.shape[1])))
    return t


def _shard_like(name, t, like):
    return t[:, :like.shape[2]][None]


def _pad_cols(t, width):
    return jnp.pad(t, ((0, 0), (0, width - t.shape[1])))


FIRST_NAMES = ("w_in", "ml_conv_w")
FFN_NAMES = ("ffn_w_up", "ffn_w_down", "ffn_conv_w")
MID_NAMES = ("ca_wo", "ca_wq", "ca_wkv", "w_out")


def _first_weights(g, small):
    w = dict(small)
    w_in = jnp.concatenate([g["w_in"][j, :, :W_IN_SHARD] for j in range(N_DEV)], axis=1)
    w["w_in_main"] = w_in[:, :D_IN_MAIN]
    w["w_in_gate"] = _pad_cols(w_in[:, D_IN_MAIN:], LANES)
    w["b_in_main"] = small["b_in"][:, :D_IN_MAIN]
    w["b_in_gate"] = _pad_cols(small["b_in"][:, D_IN_MAIN:], LANES)
    w["ml_conv_w"] = jnp.transpose(g["ml_conv_w"], (1, 0, 2)).reshape(ML_CONV, 2 * D_GROUP)
    return w


def _mid_weights(g):
    w = {n: g[n].reshape(D_MODEL, D_MODEL) for n in ("w_out", "ca_wq", "ca_wo")}
    w["ca_wkv"] = g["ca_wkv"]
    return w


def _ffn_weights(g, small):
    w = {"ffn_w_up": g["ffn_w_up"]}
    down = g["ffn_w_down"].reshape(N_DEV // 2, UP_SHARD, D_MODEL)
    w["ffn_w_down"] = jnp.pad(down, ((0, 0), (0, UP_SHARD_P - UP_SHARD), (0, 0))).reshape(D_FF_P, D_MODEL)
    w["ffn_conv_w"] = jnp.transpose(g["ffn_conv_w"], (1, 0, 2)).reshape(FFN_CONV, D_UP_P)
    w["ffn_conv_b"] = _pad_cols(small["ffn_conv_b"].reshape(N_DEV, UP_SHARD), UP_SHARD_P).reshape(1, D_UP_P)
    return w


def _whole_weights(g, small):
    return {**_first_weights(g, small), **_mid_weights(g), **_ffn_weights(g, small)}


def _owner_stack(n, grads):
    if n == "w_in":
        w_in = jnp.concatenate([grads["w_in_main"], grads["w_in_gate"][:, :D_IN - D_IN_MAIN]], axis=1)
        return jnp.stack([_pad_cols(w_in[:, j * W_IN_SHARD:(j + 1) * W_IN_SHARD], W_IN_SHARD_P)
                          for j in range(N_DEV)])
    if n in ("w_out", "ca_wq", "ca_wo"):
        return grads[n].reshape(N_DEV, D_MODEL // N_DEV, D_MODEL)
    if n == "ffn_w_down":
        down = grads[n].reshape(N_DEV // 2, UP_SHARD_P, D_MODEL)[:, :UP_SHARD]
        return down.reshape(N_DEV, D_FF // N_DEV, D_MODEL)
    if n == "ml_conv_w":
        return jnp.transpose(grads[n].reshape(ML_CONV, N_DEV, LANES), (1, 0, 2))
    if n == "ffn_conv_w":
        return jnp.transpose(grads[n].reshape(FFN_CONV, N_DEV, UP_SHARD_P), (1, 0, 2))
    return grads[n]


def _owner_stacks(grads):
    return {n: _owner_stack(n, grads) for n in SHARDED_NAMES}


def _small_grads(grads):
    out = {n: grads[n] for n in SMALL_NAMES if n in grads}
    out["b_in"] = jnp.concatenate([grads["b_in_main"], grads["b_in_gate"][:, :D_IN - D_IN_MAIN]], axis=1)
    out["ffn_conv_b"] = grads["ffn_conv_b"].reshape(N_DEV, UP_SHARD_P)[:, :UP_SHARD].reshape(1, D_UP)
    return out


def _pack_small(p, extra=None):
    flat = [p[n].reshape(-1) for n in SMALL_NAMES]
    if extra is not None:
        flat.append(extra.reshape(-1))
    flat = jnp.concatenate(flat)
    return jnp.pad(flat, (0, SMALL_ROWS * SMALL_W - flat.shape[0])).reshape(SMALL_ROWS, SMALL_W)


def _unpack_small(slab, like):
    out = {}
    flat = slab.reshape(-1)
    o = 0
    for n in SMALL_NAMES:
        out[n] = flat[o:o + like[n].size].reshape(like[n].shape)
        o += like[n].size
    return out, flat[o]


def kernel(x, mem, w_in, b_in, hg_lb_logits, hg_norm_w, ml_conv_w, ml_conv_b, ml_norm_w, w_out, ln1_g, ln1_b, ca_wq, ca_wkv, ca_wo, ln2_g, ln2_b, ffn_w_up, ffn_conv_w, ffn_conv_b, ffn_w_down, ln3_g, ln3_b, loss_target, m_w_in, m_b_in, m_hg_lb_logits, m_hg_norm_w, m_ml_conv_w, m_ml_conv_b, m_ml_norm_w, m_w_out, m_ln1_g, m_ln1_b, m_ca_wq, m_ca_wkv, m_ca_wo, m_ln2_g, m_ln2_b, m_ffn_w_up, m_ffn_conv_w, m_ffn_conv_b, m_ffn_w_down, m_ln3_g, m_ln3_b, v_w_in, v_b_in, v_hg_lb_logits, v_hg_norm_w, v_ml_conv_w, v_ml_conv_b, v_ml_norm_w, v_w_out, v_ln1_g, v_ln1_b, v_ca_wq, v_ca_wkv, v_ca_wo, v_ln2_g, v_ln2_b, v_ffn_w_up, v_ffn_conv_w, v_ffn_conv_b, v_ffn_w_down, v_ln3_g, v_ln3_b):
    params = dict(w_in=w_in, b_in=b_in, hg_lb_logits=hg_lb_logits, hg_norm_w=hg_norm_w, ml_conv_w=ml_conv_w,
                  ml_conv_b=ml_conv_b, ml_norm_w=ml_norm_w, w_out=w_out, ln1_g=ln1_g, ln1_b=ln1_b, ca_wq=ca_wq,
                  ca_wkv=ca_wkv, ca_wo=ca_wo, ln2_g=ln2_g, ln2_b=ln2_b, ffn_w_up=ffn_w_up, ffn_conv_w=ffn_conv_w,
                  ffn_conv_b=ffn_conv_b, ffn_w_down=ffn_w_down, ln3_g=ln3_g, ln3_b=ln3_b)
    mom1 = dict(w_in=m_w_in, b_in=m_b_in, hg_lb_logits=m_hg_lb_logits, hg_norm_w=m_hg_norm_w,
                ml_conv_w=m_ml_conv_w, ml_conv_b=m_ml_conv_b, ml_norm_w=m_ml_norm_w, w_out=m_w_out, ln1_g=m_ln1_g,
                ln1_b=m_ln1_b, ca_wq=m_ca_wq, ca_wkv=m_ca_wkv, ca_wo=m_ca_wo, ln2_g=m_ln2_g, ln2_b=m_ln2_b,
                ffn_w_up=m_ffn_w_up, ffn_conv_w=m_ffn_conv_w, ffn_conv_b=m_ffn_conv_b, ffn_w_down=m_ffn_w_down,
                ln3_g=m_ln3_g, ln3_b=m_ln3_b)
    mom2 = dict(w_in=v_w_in, b_in=v_b_in, hg_lb_logits=v_hg_lb_logits, hg_norm_w=v_hg_norm_w,
                ml_conv_w=v_ml_conv_w, ml_conv_b=v_ml_conv_b, ml_norm_w=v_ml_norm_w, w_out=v_w_out, ln1_g=v_ln1_g,
                ln1_b=v_ln1_b, ca_wq=v_ca_wq, ca_wkv=v_ca_wkv, ca_wo=v_ca_wo, ln2_g=v_ln2_g, ln2_b=v_ln2_b,
                ffn_w_up=v_ffn_w_up, ffn_conv_w=v_ffn_conv_w, ffn_conv_b=v_ffn_conv_b, ffn_w_down=v_ffn_w_down,
                ln3_g=v_ln3_g, ln3_b=v_ln3_b)

    x_idx, y_idx, c_idx = _coords()
    as_index = lambda v: jnp.reshape(v, (1,)).astype(jnp.int32)
    me = as_index(4 * x_idx + 2 * y_idx + c_idx)
    small_params = {n: params[n] for n in SMALL_NAMES}

    shards = {n: _shard_2d(n, params[n]) for n in SHARDED_NAMES}
    to_send = lambda names: [shards[n] if "conv" in n else shards[n].astype(BF16) for n in names]
    first = dict(zip(FIRST_NAMES, _all_gather_two_level(to_send(FIRST_NAMES), "weights_gather_first")))
    mid_started, through = _direct_start(True, to_send(MID_NAMES), first["w_in"], "weights_gather_start_mid")
    ffn_started, first["w_in"] = _direct_start(True, to_send(FFN_NAMES), through, "weights_gather_start_ffn")

    def gathered_weights(names, started, after, tag):
        mine, lands = _direct_wait(True, started, after, "weights_gather_wait_" + tag)
        return {n: lax.dynamic_update_index_in_dim(land, own, me[0], 0) for n, own, land in zip(names, mine, lands)}

    started, own_stacks = {}, {}

    def start_group(names, tag):
        def hook(grads, through):
            own_stacks[tag] = [_owner_stack(n, grads).astype(BF16) for n in names]
            started[tag], through = _direct_start(False, own_stacks[tag], through, "grads_start_" + tag)
            return through
        return hook

    def start_small(grads, loss, through):
        started["small"], through = _direct_start(True, [_pack_small(_small_grads(grads), loss)], through,
                                                  "small_gather_start")
        return through

    loss, grad_x, grads = _local_step(
        x[0], mem[0], loss_target[0], _first_weights(first, small_params),
        lambda y: _mid_weights(gathered_weights(MID_NAMES, mid_started, y, "mid")),
        lambda x2: _ffn_weights(gathered_weights(FFN_NAMES, ffn_started, x2, "ffn"), small_params),
        start_group(FFN_NAMES, "ffn"), start_group(MID_NAMES, "mid"), start_small, start_group(FIRST_NAMES, "last"))

    sharded_out = {}
    after = grad_x
    for names, tag in ((FFN_NAMES, "ffn"), (MID_NAMES, "mid"), (FIRST_NAMES, "last")):
        _, lands = _direct_wait(False, started[tag], after, "grads_wait_" + tag)
        for n, st, land in zip(names, own_stacks[tag], lands):
            res = _adamw_sharded(me, st, land, shards[n], _shard_2d(n, mom1[n]), _shard_2d(n, mom2[n]), "adamw_" + n)
            sharded_out[n] = [_shard_like(n, t, params[n]) for t in res]
            after = res[0]
    own_small, small_lands = _direct_wait(True, started["small"], after, "small_gather_wait")
    small_parts = lax.dynamic_update_index_in_dim(small_lands[0], own_small[0], me[0], 0)
    small_res = _adamw_replicated(small_parts, _pack_small(params), _pack_small(mom1), _pack_small(mom2))

    outs = []
    total_loss = None
    for k in range(4):
        small, extra = _unpack_small(small_res[k], params)
        if total_loss is None:
            total_loss = extra
        outs.extend(sharded_out[n][k] if n in sharded_out else small[n] for n in WEIGHT_NAMES)
    return (total_loss, grad_x[None], *outs)
```

```python
import functools

import jax
import jax.numpy as jnp
from jax import lax
from jax.experimental import pallas as pl
from jax.experimental.pallas import tpu as pltpu

F32 = jnp.float32
BF16 = jnp.bfloat16
HIGHEST = lax.Precision.HIGHEST
MESH = pl.DeviceIdType.MESH

N_DEV = 8
D_MODEL = 1024
N_MEM = 256
N_HEADS = 4
D_HEAD = 128
D_GROUP = N_HEADS * D_HEAD
CHUNK = 64
ML_CONV = 4
FFN_CONV = 3
D_FF = 2816
D_UP = 2 * D_FF
CA_HEADS = 4
CA_DH = D_MODEL // CA_HEADS
LANES = 128
SUBLANES = 8
D_IN = 8 * D_GROUP + 2 * N_HEADS
D_IN_MAIN = 8 * D_GROUP
W_IN_SHARD = D_IN // N_DEV
W_IN_SHARD_P = 640
UP_SHARD = D_UP // N_DEV
UP_SHARD_P = 768
D_UP_P = N_DEV * UP_SHARD_P
D_FF_P = D_UP_P // 2
ALPHA = 2.0 ** 0.25
LN_EPS = 1e-5
NEG_BIG = -1e30
ADAM_LR = 0.001
ADAM_B1 = 0.9
ADAM_B2 = 0.999
ADAM_EPS = 1e-08
ADAM_WD = 0.01
ADAM_STEP = 10
VMEM_LIMIT = 56 * 1024 * 1024

SEG_HQ, SEG_HF, SEG_HI, SEG_HG, SEG_MQ, SEG_MK, SEG_MV, SEG_MO = (4 * i for i in range(8))


def _params(sem):
    return pltpu.CompilerParams(dimension_semantics=sem, vmem_limit_bytes=VMEM_LIMIT)


def _dg(a, b, ca, cb, precision=None):
    return lax.dot_general(a, b, (((ca,), (cb,)), ((), ())), precision=precision,
                           preferred_element_type=F32)


def _nn_raw(a, b):
    return _dg(a.astype(BF16), b.astype(BF16), 1, 0)


def _nt_raw(a, b):
    return _dg(a.astype(BF16), b.astype(BF16), 1, 1)


def _tn_raw(a, b):
    return _dg(a.astype(BF16), b.astype(BF16), 0, 0)


@jax.custom_vjp
def _nn(a, b):
    return _nn_raw(a, b)


_nn.defvjp(lambda a, b: (_nn_raw(a, b), (a, b)),
           lambda res, g: (_nt_raw(g, res[1]), _tn_raw(res[0], g)))


@jax.custom_vjp
def _nt(a, b):
    return _nt_raw(a, b)


_nt.defvjp(lambda a, b: (_nt_raw(a, b), (a, b)),
           lambda res, g: (_nn_raw(g, res[1]), _tn_raw(g, res[0])))


@jax.custom_vjp
def _tn(a, b):
    return _tn_raw(a, b)


_tn.defvjp(lambda a, b: (_tn_raw(a, b), (a, b)),
           lambda res, g: (_nt_raw(res[1], g), _nn_raw(res[0], g)))


def _layer_norm(z, g, b):
    mu = jnp.mean(z, axis=-1, keepdims=True)
    var = jnp.mean(jnp.square(z - mu), axis=-1, keepdims=True)
    return (z - mu) * lax.rsqrt(var + LN_EPS) * g + b


def _matmul_nn(a, w, bias, tm, tn, name):
    m, k = a.shape
    if w.ndim == 3:
        n = w.shape[0] * w.shape[2]
        assert tn == w.shape[2]
        w_spec = pl.BlockSpec((None, k, tn), lambda i, j: (j, 0, 0))
    else:
        n = w.shape[1]
        w_spec = pl.BlockSpec((k, tn), lambda i, j: (0, j))

    def body(*refs):
        a_ref, w_ref = refs[0], refs[1]
        o_ref = refs[-1]
        acc = _nn_raw(a_ref[...], w_ref[...])
        if bias is not None:
            acc = acc + refs[2][...]
        o_ref[...] = acc

    in_specs = [pl.BlockSpec((tm, k), lambda i, j: (i, 0)), w_spec]
    args = [a, w]
    if bias is not None:
        in_specs.append(pl.BlockSpec((1, tn), lambda i, j: (0, j)))
        args.append(bias)
    return pl.pallas_call(
        body, name=name, grid=(m // tm, n // tn), in_specs=in_specs,
        out_specs=pl.BlockSpec((tm, tn), lambda i, j: (i, j)),
        out_shape=jax.ShapeDtypeStruct((m, n), F32),
        compiler_params=_params(("parallel", "parallel")),
    )(*args)


def _matmul_nt(pairs, add, scale, tm, tk, name):
    m = pairs[0][0].shape[0]
    k = pairs[0][1].shape[-2]
    groups = []
    in_specs, args = [], []
    for pair in pairs:
        d, w = pair[0], pair[1]
        in_specs.append(pl.BlockSpec((tm, d.shape[1]), lambda i, j: (i, 0)))
        if w.ndim == 3:
            g = d.shape[1] // w.shape[2]
            blk = pair[2] // g
            in_specs.append(pl.BlockSpec((g, tk, w.shape[2]), lambda i, j, blk=blk: (blk, j, 0)))
            groups.append((g, w.shape[2]))
        else:
            in_specs.append(pl.BlockSpec((tk, w.shape[1]), lambda i, j: (j, 0)))
            groups.append(None)
        args += [d, w]
    if add is not None:
        in_specs.append(pl.BlockSpec((tm, tk), lambda i, j: (i, j)))
        args.append(add)

    def body(*refs):
        o_ref = refs[-1]
        acc = None
        for p, grp in enumerate(groups):
            d_ref, w_ref = refs[2 * p], refs[2 * p + 1]
            if grp is None:
                terms = [_nt_raw(d_ref[...], w_ref[...])]
            else:
                terms = [_nt_raw(d_ref[:, g * grp[1]:(g + 1) * grp[1]], w_ref[g]) for g in range(grp[0])]
            for t in terms:
                acc = t if acc is None else acc + t
        if add is not None:
            acc = acc + scale * refs[2 * len(groups)][...]
        o_ref[...] = acc

    return pl.pallas_call(
        body, name=name, grid=(m // tm, k // tk), in_specs=in_specs,
        out_specs=pl.BlockSpec((tm, tk), lambda i, j: (i, j)),
        out_shape=jax.ShapeDtypeStruct((m, k), F32),
        compiler_params=_params(("parallel", "parallel")),
    )(*args)


def _matmul_tn(a, b, tm, tn, tt, name, shards=None, shard0=0, group=1, into=None, colsum=False):
    t, m = a.shape
    n = b.shape[1]
    assert not colsum or tm == m
    n_in = 2 + (into is not None)
    out_dtype = BF16
    per_step = 1 if shards is None else group
    width = per_step * tn

    def body(*refs):
        a_ref, b_ref = refs[0], refs[1]
        o_ref, acc_ref = refs[n_in], refs[-1]
        first = pl.program_id(2) == 0

        @pl.when(first)
        def _():
            acc_ref[...] = jnp.zeros_like(acc_ref)

        if shards is None:
            acc_ref[...] += _tn_raw(a_ref[...], b_ref[...])
        else:
            lhs = a_ref[...].astype(BF16)
            for g in range(per_step):
                acc_ref[g] += _tn_raw(lhs, b_ref[:, g * tn:(g + 1) * tn])

        @pl.when(pl.program_id(2) == t // tt - 1)
        def _():
            o_ref[...] = acc_ref[...].astype(o_ref.dtype)

        if colsum:
            s_ref = refs[n_in + 1]

            @pl.when(first)
            def _():
                s_ref[...] = jnp.zeros_like(s_ref)

            s_ref[...] += jnp.sum(b_ref[...], axis=0, keepdims=True)

    in_specs = [pl.BlockSpec((tt, tm), lambda i, j, kk: (kk, i)),
                pl.BlockSpec((tt, width), lambda i, j, kk: (kk, j))]
    args = [a, b]
    aliases = {}
    if into is not None:
        in_specs.append(pl.BlockSpec(memory_space=pl.ANY))
        args.append(into)
        aliases = {2: 0}
    if shards is None:
        out_specs = [pl.BlockSpec((tm, tn), lambda i, j, kk: (i, j))]
        out_shape = [jax.ShapeDtypeStruct((m, n), out_dtype)]
        acc = pltpu.VMEM((tm, tn), F32)
    else:
        out_specs = [pl.BlockSpec((per_step, tm, tn), lambda i, j, kk: (shard0 // per_step + j, i, 0))]
        out_shape = [jax.ShapeDtypeStruct((shards, m, tn), out_dtype)]
        acc = pltpu.VMEM((per_step, tm, tn), F32)
    if colsum:
        out_specs.append(pl.BlockSpec((1, tn), lambda i, j, kk: (0, j)))
        out_shape.append(jax.ShapeDtypeStruct((1, n), F32))
    res = pl.pallas_call(
        body, name=name, grid=(m // tm, n // width, t // tt), in_specs=in_specs, out_specs=out_specs,
        out_shape=out_shape, input_output_aliases=aliases, scratch_shapes=[acc],
        compiler_params=_params(("parallel", "parallel", "arbitrary")),
    )(*args)
    return res if colsum else res[0]


ROW_TILE = 64


def _conv_fwd_tile(pad_ref, w_ref, b_ref, r0, rows, taps):
    acc = b_ref[...]
    for j in range(taps):
        acc = acc + pad_ref[pl.ds(SUBLANES - (taps - 1 - j) + r0, rows), :] * w_ref[j:j + 1, :]
    return acc


def _conv_bwd_tile(dpad_ref, w_ref, r0, rows, taps):
    acc = None
    for j in range(taps):
        term = dpad_ref[pl.ds(r0 + (taps - 1 - j), rows), :] * w_ref[j:j + 1, :]
        acc = term if acc is None else acc + term
    return acc


def _conv_grads_tile(pad_ref, dpad_ref, dx_ref, w_ref, dws, r0, rows, taps):
    dx = _conv_bwd_tile(dpad_ref, w_ref, r0, rows, taps)
    dx_ref[r0:r0 + rows, :] = dx.astype(dx_ref.dtype)
    d_pre = dpad_ref[r0:r0 + rows, :]
    for j in range(taps):
        xs = pad_ref[pl.ds(SUBLANES - (taps - 1 - j) + r0, rows), :]
        dws[j] = dws[j] + jnp.sum(d_pre * xs, axis=0, keepdims=True)
    return jnp.sum(dx, axis=0, keepdims=True)


def _ml_conv_fwd(proj, conv_w, conv_b):
    s = proj.shape[0]
    nblk = 2 * D_GROUP // LANES

    def body(x_ref, w_ref, b_ref, o_ref, pad_ref):
        pad_ref[0:SUBLANES, :] = jnp.zeros((SUBLANES, LANES), F32)
        pad_ref[SUBLANES:, :] = x_ref[...]
        for r0 in range(0, s, ROW_TILE):
            rows = min(ROW_TILE, s - r0)
            o_ref[r0:r0 + rows, :] = jax.nn.silu(_conv_fwd_tile(pad_ref, w_ref, b_ref, r0, rows, ML_CONV))

    return pl.pallas_call(
        body, name="ml_conv_fwd", grid=(nblk,),
        in_specs=[pl.BlockSpec((s, LANES), lambda j: (0, SEG_MQ + j)),
                  pl.BlockSpec((ML_CONV, LANES), lambda j: (0, j)),
                  pl.BlockSpec((1, LANES), lambda j: (0, j))],
        out_specs=pl.BlockSpec((s, LANES), lambda j: (0, j)),
        out_shape=jax.ShapeDtypeStruct((s, 2 * D_GROUP), F32),
        scratch_shapes=[pltpu.VMEM((s + SUBLANES, LANES), F32)],
        compiler_params=_params(("parallel",)),
    )(proj, conv_w, conv_b)


def _ml_conv_bwd(proj, conv_w, conv_b, d_qk, d_proj):
    s = proj.shape[0]
    nblk = 2 * D_GROUP // LANES

    def body(x_ref, w_ref, b_ref, dy_ref, _, dx_ref, dw_ref, db_ref, dxs_ref, pad_ref, dpad_ref):
        pad_ref[0:SUBLANES, :] = jnp.zeros((SUBLANES, LANES), F32)
        pad_ref[SUBLANES:, :] = x_ref[...]
        dpad_ref[s:, :] = jnp.zeros((SUBLANES, LANES), F32)
        db = jnp.zeros((1, LANES), F32)
        for r0 in range(0, s, ROW_TILE):
            rows = min(ROW_TILE, s - r0)
            pre = _conv_fwd_tile(pad_ref, w_ref, b_ref, r0, rows, ML_CONV)
            _, vjp = jax.vjp(jax.nn.silu, pre)
            d_pre, = vjp(dy_ref[r0:r0 + rows, :])
            dpad_ref[r0:r0 + rows, :] = d_pre
            db = db + jnp.sum(d_pre, axis=0, keepdims=True)
        db_ref[...] = db
        dws = [jnp.zeros((1, LANES), F32) for _ in range(ML_CONV)]
        dx_sum = jnp.zeros((1, LANES), F32)
        for r0 in range(0, s, ROW_TILE):
            dx_sum = dx_sum + _conv_grads_tile(pad_ref, dpad_ref, dx_ref, w_ref, dws, r0, min(ROW_TILE, s - r0),
                                               ML_CONV)
        dxs_ref[...] = dx_sum
        for j in range(ML_CONV):
            dw_ref[j:j + 1, :] = dws[j]

    return pl.pallas_call(
        body, name="ml_conv_bwd", grid=(nblk,),
        in_specs=[pl.BlockSpec((s, LANES), lambda j: (0, SEG_MQ + j)),
                  pl.BlockSpec((ML_CONV, LANES), lambda j: (0, j)),
                  pl.BlockSpec((1, LANES), lambda j: (0, j)),
                  pl.BlockSpec((s, LANES), lambda j: (0, j)),
                  pl.BlockSpec(memory_space=pl.ANY)],
        out_specs=[pl.BlockSpec((s, LANES), lambda j: (0, SEG_MQ + j)),
                   pl.BlockSpec((ML_CONV, LANES), lambda j: (0, j)),
                   pl.BlockSpec((1, LANES), lambda j: (0, j)),
                   pl.BlockSpec((1, LANES), lambda j: (0, j))],
        out_shape=[jax.ShapeDtypeStruct(d_proj.shape, d_proj.dtype),
                   jax.ShapeDtypeStruct((ML_CONV, 2 * D_GROUP), F32),
                   jax.ShapeDtypeStruct((1, 2 * D_GROUP), F32),
                   jax.ShapeDtypeStruct((1, 2 * D_GROUP), F32)],
        input_output_aliases={4: 0},
        scratch_shapes=[pltpu.VMEM((s + SUBLANES, LANES), F32), pltpu.VMEM((s + SUBLANES, LANES), F32)],
        compiler_params=_params(("parallel",)),
    )(proj, conv_w, conv_b, d_qk, d_proj)


def _gelu_mul(a, b):
    return jax.nn.gelu(a) * b


FFN_BLOCKS = D_FF_P // LANES


def _ffn_conv_fwd(u, conv_w, conv_b):
    s = u.shape[0]

    def body(g_ref, v_ref, wg_ref, wv_ref, bg_ref, bv_ref, o_ref, gpad_ref, vpad_ref):
        for pad_ref, x_ref in ((gpad_ref, g_ref), (vpad_ref, v_ref)):
            pad_ref[0:SUBLANES, :] = jnp.zeros((SUBLANES, LANES), F32)
            pad_ref[SUBLANES:, :] = x_ref[...]
        for r0 in range(0, s, ROW_TILE):
            rows = min(ROW_TILE, s - r0)
            ug = _conv_fwd_tile(gpad_ref, wg_ref, bg_ref, r0, rows, FFN_CONV)
            uv = _conv_fwd_tile(vpad_ref, wv_ref, bv_ref, r0, rows, FFN_CONV)
            o_ref[r0:r0 + rows, :] = _gelu_mul(ug, uv).astype(o_ref.dtype)

    col = lambda off: (lambda j: (0, off + j))
    return pl.pallas_call(
        body, name="ffn_conv_fwd", grid=(FFN_BLOCKS,),
        in_specs=[pl.BlockSpec((s, LANES), col(0)), pl.BlockSpec((s, LANES), col(FFN_BLOCKS)),
                  pl.BlockSpec((FFN_CONV, LANES), col(0)), pl.BlockSpec((FFN_CONV, LANES), col(FFN_BLOCKS)),
                  pl.BlockSpec((1, LANES), col(0)), pl.BlockSpec((1, LANES), col(FFN_BLOCKS))],
        out_specs=pl.BlockSpec((s, LANES), col(0)),
        out_shape=jax.ShapeDtypeStruct((s, D_FF_P), BF16),
        scratch_shapes=[pltpu.VMEM((s + SUBLANES, LANES), F32), pltpu.VMEM((s + SUBLANES, LANES), F32)],
        compiler_params=_params(("parallel",)),
    )(u, u, conv_w, conv_w, conv_b, conv_b)


def _ffn_conv_bwd(u, conv_w, conv_b, d_h):
    s = u.shape[0]

    def body(g_ref, v_ref, wg_ref, wv_ref, bg_ref, bv_ref, dh_ref,
             dug_ref, duv_ref, dwg_ref, dwv_ref, dbg_ref, dbv_ref,
             gpad_ref, vpad_ref, dgpad_ref, dvpad_ref):
        for pad_ref, x_ref in ((gpad_ref, g_ref), (vpad_ref, v_ref)):
            pad_ref[0:SUBLANES, :] = jnp.zeros((SUBLANES, LANES), F32)
            pad_ref[SUBLANES:, :] = x_ref[...]
        dgpad_ref[s:, :] = jnp.zeros((SUBLANES, LANES), F32)
        dvpad_ref[s:, :] = jnp.zeros((SUBLANES, LANES), F32)
        dbg = jnp.zeros((1, LANES), F32)
        dbv = jnp.zeros((1, LANES), F32)
        for r0 in range(0, s, ROW_TILE):
            rows = min(ROW_TILE, s - r0)
            ug = _conv_fwd_tile(gpad_ref, wg_ref, bg_ref, r0, rows, FFN_CONV)
            uv = _conv_fwd_tile(vpad_ref, wv_ref, bv_ref, r0, rows, FFN_CONV)
            _, vjp = jax.vjp(_gelu_mul, ug, uv)
            d_ug, d_uv = vjp(dh_ref[r0:r0 + rows, :])
            dgpad_ref[r0:r0 + rows, :] = d_ug
            dvpad_ref[r0:r0 + rows, :] = d_uv
            dbg = dbg + jnp.sum(d_ug, axis=0, keepdims=True)
            dbv = dbv + jnp.sum(d_uv, axis=0, keepdims=True)
        dbg_ref[...] = dbg
        dbv_ref[...] = dbv
        for pad_ref, dpad_ref, w_ref, dx_ref, dw_ref in ((gpad_ref, dgpad_ref, wg_ref, dug_ref, dwg_ref),
                                                         (vpad_ref, dvpad_ref, wv_ref, duv_ref, dwv_ref)):
            dws = [jnp.zeros((1, LANES), F32) for _ in range(FFN_CONV)]
            for r0 in range(0, s, ROW_TILE):
                _conv_grads_tile(pad_ref, dpad_ref, dx_ref, w_ref, dws, r0, min(ROW_TILE, s - r0), FFN_CONV)
            for j in range(FFN_CONV):
                dw_ref[j:j + 1, :] = dws[j]

    col = lambda off: (lambda j: (0, off + j))
    seq = pl.BlockSpec((s, LANES), col(0))
    return pl.pallas_call(
        body, name="ffn_conv_bwd", grid=(FFN_BLOCKS,),
        in_specs=[pl.BlockSpec((s, LANES), col(0)), pl.BlockSpec((s, LANES), col(FFN_BLOCKS)),
                  pl.BlockSpec((FFN_CONV, LANES), col(0)), pl.BlockSpec((FFN_CONV, LANES), col(FFN_BLOCKS)),
                  pl.BlockSpec((1, LANES), col(0)), pl.BlockSpec((1, LANES), col(FFN_BLOCKS)), seq],
        out_specs=[seq, seq, pl.BlockSpec((FFN_CONV, LANES), col(0)), pl.BlockSpec((FFN_CONV, LANES), col(0)),
                   pl.BlockSpec((1, LANES), col(0)), pl.BlockSpec((1, LANES), col(0))],
        out_shape=[jax.ShapeDtypeStruct((s, D_FF_P), BF16), jax.ShapeDtypeStruct((s, D_FF_P), BF16),
                   jax.ShapeDtypeStruct((FFN_CONV, D_FF_P), F32), jax.ShapeDtypeStruct((FFN_CONV, D_FF_P), F32),
                   jax.ShapeDtypeStruct((1, D_FF_P), F32), jax.ShapeDtypeStruct((1, D_FF_P), F32)],
        scratch_shapes=[pltpu.VMEM((s + SUBLANES, LANES), F32) for _ in range(4)],
        compiler_params=_params(("parallel",)),
    )(u, u, conv_w, conv_w, conv_b, conv_b, d_h)


def _chunk_masks(c):
    row = lax.broadcasted_iota(jnp.int32, (c, c), 0)
    col = lax.broadcasted_iota(jnp.int32, (c, c), 1)
    return row, col


@jax.custom_vjp
def _split_heads(x):
    return tuple(x[:, h * D_HEAD:(h + 1) * D_HEAD] for h in range(N_HEADS))


_split_heads.defvjp(lambda x: (_split_heads(x), None), lambda _, gs: (jnp.concatenate(gs, axis=1),))


@jax.custom_vjp
def _merge_heads(xs):
    return jnp.concatenate(xs, axis=1)


_merge_heads.defvjp(lambda xs: (_merge_heads(xs), None), lambda _, g: (_split_heads(g),))


@jax.custom_vjp
def _split_chunks(x):
    return tuple(x[i * CHUNK:(i + 1) * CHUNK] for i in range(x.shape[0] // CHUNK))


_split_chunks.defvjp(lambda x: (_split_chunks(x), None), lambda _, gs: (jnp.concatenate(gs, axis=0),))


@jax.custom_vjp
def _merge_chunks(xs):
    return jnp.concatenate(xs, axis=0)


_merge_chunks.defvjp(lambda xs: (_merge_chunks(xs), None), lambda _, g: (_split_chunks(g),))


def _blocks(x):
    return [_split_heads(rows) for rows in _split_chunks(x)]


def _per_chunk_rows(per_chunk, rid):
    out = per_chunk[0]
    for i in range(1, len(per_chunk)):
        out = jnp.where(rid >= i * CHUNK, per_chunk[i], out)
    return out


HEADS = range(N_HEADS)
CHUNKS_PER_STEP = 4
ML_CHUNKS_PER_STEP = 1


def _hg_chunk(hq, hf, hi, hgate, l0, l1, nw, sts):
    n = hq.shape[0] // CHUNK
    row, col = _chunk_masks(n * CHUNK)
    same_chunk = functools.reduce(jnp.logical_or, [(row >= i * CHUNK) & (row < (i + 1) * CHUNK) &
                                                   (col >= i * CHUNK) & (col < (i + 1) * CHUNK) for i in range(n)])
    causal = _chunk_masks(CHUNK)
    causal = causal[1] <= causal[0]
    mx = lax.stop_gradient(jnp.maximum(l0, l1))
    e0 = jnp.exp(l0 - mx)
    e1 = jnp.exp(l1 - mx)
    lb = e0 / (e0 + e1)
    sig = jax.nn.sigmoid(hf)
    lf = jnp.log(lb + (1.0 - lb) * sig)
    k = (1.0 - lb) * jax.nn.sigmoid(-hf)
    q = jax.nn.silu(hq)
    b = _dg(((col <= row) & same_chunk).astype(F32), lf, 1, 0, HIGHEST)
    rid = lax.broadcasted_iota(jnp.int32, b.shape, 0)
    pick = lambda r: jnp.sum(jnp.where(rid == r, b, 0.0), axis=0, keepdims=True)
    b_last_c = [pick(i * CHUNK + CHUNK - 1) for i in range(n)]
    b_ref = _per_chunk_rows([pick(i * CHUNK + CHUNK // 2 - 1) for i in range(n)], rid)
    b_last = _per_chunk_rows(b_last_c, rid)
    qa = _blocks(q * jnp.exp(b - b_ref))
    ka = _blocks(k * jnp.exp(b_ref - b))
    qe = _blocks(q * jnp.exp(b))
    kd = _blocks(k * jnp.exp(b_last - b))
    decay = [_split_heads(jnp.exp(b_last_c[i])) for i in range(n)]
    v = _blocks(hi)
    chunks = range(n)
    attn = [[jnp.where(causal, _nt(qa[i][h], ka[i][h]), 0.0) for h in HEADS] for i in chunks]
    intra = [[_nn(attn[i][h], v[i][h]) for h in HEADS] for i in chunks]
    kv = [[_tn(v[i][h], kd[i][h]) for h in HEADS] for i in chunks]
    normed = []
    for i in chunks:
        inter = [_nt(qe[i][h], sts[h]) for h in HEADS]
        sts = tuple(decay[i][h] * sts[h] + kv[i][h] for h in HEADS)
        o = [intra[i][h] + inter[h] for h in HEADS]
        normed.append(_merge_heads(tuple(o[h] * lax.rsqrt(jnp.mean(o[h] * o[h], axis=-1, keepdims=True) + LN_EPS)
                                         for h in HEADS)))
    return _merge_chunks(tuple(normed)) * nw * jax.nn.silu(hgate), sts


def _seg(ref, seg):
    return ref[:, seg * D_GROUP:(seg + 1) * D_GROUP]


def _hgrn2_fwd(proj, logits, norm_w):
    s = proj.shape[0]
    rows = CHUNKS_PER_STEP * CHUNK
    nc = s // rows

    def body(p_ref, lg_ref, nw_ref, y_ref, st_out_ref, st_scr):
        @pl.when(pl.program_id(0) == 0)
        def _():
            st_scr[...] = jnp.zeros_like(st_scr)

        sts = tuple(st_scr[h] for h in HEADS)
        y, sts_new = _hg_chunk(_seg(p_ref, 0), _seg(p_ref, 1), _seg(p_ref, 2), _seg(p_ref, 3),
                               lg_ref[0:1, :], lg_ref[1:2, :], nw_ref[...], sts)
        y_ref[...] = y.astype(y_ref.dtype)
        for h in HEADS:
            st_out_ref[h] = sts[h]
            st_scr[h] = sts_new[h]

    return pl.pallas_call(
        body, name="hgrn2_fwd", grid=(nc,),
        in_specs=[pl.BlockSpec((rows, 4 * D_GROUP), lambda c: (c, 0)),
                  pl.BlockSpec((2, D_GROUP), lambda c: (0, 0)),
                  pl.BlockSpec((1, D_GROUP), lambda c: (0, 0))],
        out_specs=[pl.BlockSpec((rows, D_GROUP), lambda c: (c, 0)),
                   pl.BlockSpec((None, N_HEADS, D_HEAD, D_HEAD), lambda c: (c, 0, 0, 0))],
        out_shape=[jax.ShapeDtypeStruct((s, 2 * D_GROUP), BF16),
                   jax.ShapeDtypeStruct((nc, N_HEADS, D_HEAD, D_HEAD), F32)],
        scratch_shapes=[pltpu.VMEM((N_HEADS, D_HEAD, D_HEAD), F32)],
        compiler_params=_params(("arbitrary",)),
    )(proj, logits, norm_w)


def _hgrn2_bwd(proj, logits, norm_w, states, d_y):
    s = proj.shape[0]
    rows = CHUNKS_PER_STEP * CHUNK
    nc = s // rows

    def body(p_ref, lg_ref, nw_ref, st_ref, dy_ref, dp_ref, dl_ref, dnw_ref, dsum_ref, dst_scr):
        @pl.when(pl.program_id(0) == 0)
        def _():
            dst_scr[...] = jnp.zeros_like(dst_scr)
            dl_ref[...] = jnp.zeros_like(dl_ref)
            dnw_ref[...] = jnp.zeros_like(dnw_ref)
            dsum_ref[...] = jnp.zeros_like(dsum_ref)

        _, vjp = jax.vjp(_hg_chunk, _seg(p_ref, 0), _seg(p_ref, 1), _seg(p_ref, 2), _seg(p_ref, 3),
                         lg_ref[0:1, :], lg_ref[1:2, :], nw_ref[...], tuple(st_ref[h] for h in HEADS))
        d_hq, d_hf, d_hi, d_hg, d_l0, d_l1, d_nw, d_sts = vjp((dy_ref[...], tuple(dst_scr[h] for h in HEADS)))
        for seg, val in enumerate((d_hq, d_hf, d_hi, d_hg)):
            dp_ref[:, seg * D_GROUP:(seg + 1) * D_GROUP] = val.astype(dp_ref.dtype)
            dsum_ref[:, seg * D_GROUP:(seg + 1) * D_GROUP] += jnp.sum(val, axis=0, keepdims=True)
        dl_ref[0:1, :] += d_l0
        dl_ref[1:2, :] += d_l1
        dnw_ref[...] += d_nw
        for h in HEADS:
            dst_scr[h] = d_sts[h]

    rev = lambda c: nc - 1 - c
    return pl.pallas_call(
        body, name="hgrn2_bwd", grid=(nc,),
        in_specs=[pl.BlockSpec((rows, 4 * D_GROUP), lambda c: (rev(c), 0)),
                  pl.BlockSpec((2, D_GROUP), lambda c: (0, 0)),
                  pl.BlockSpec((1, D_GROUP), lambda c: (0, 0)),
                  pl.BlockSpec((None, N_HEADS, D_HEAD, D_HEAD), lambda c: (rev(c), 0, 0, 0)),
                  pl.BlockSpec((rows, D_GROUP), lambda c: (rev(c), 0))],
        out_specs=[pl.BlockSpec((rows, 4 * D_GROUP), lambda c: (rev(c), 0)),
                   pl.BlockSpec((2, D_GROUP), lambda c: (0, 0)),
                   pl.BlockSpec((1, D_GROUP), lambda c: (0, 0)),
                   pl.BlockSpec((1, 4 * D_GROUP), lambda c: (0, 0))],
        out_shape=[jax.ShapeDtypeStruct((s, D_IN_MAIN), BF16), jax.ShapeDtypeStruct((2, D_GROUP), F32),
                   jax.ShapeDtypeStruct((1, D_GROUP), F32), jax.ShapeDtypeStruct((1, 4 * D_GROUP), F32)],
        scratch_shapes=[pltpu.VMEM((N_HEADS, D_HEAD, D_HEAD), F32)],
        compiler_params=_params(("arbitrary",)),
    )(proj, logits, norm_w, states, d_y)


def _gate_column(gates, lane, idx):
    return jnp.sum(jnp.where(lane == idx, gates, 0.0), axis=1, keepdims=True)


def _head_layer_norm(h):
    mu = jnp.mean(h, axis=-1, keepdims=True)
    var = jnp.mean(jnp.square(h - mu), axis=-1, keepdims=True)
    return (h - mu) * lax.rsqrt(var + LN_EPS)


def _ml_chunk(qc, kc, v, mo, gates, nw, cts, ns, ms):
    n = qc.shape[0] // CHUNK
    row, col = _chunk_masks(CHUNK)
    mask = col <= row
    eye = col == row
    to_row = lambda t: jnp.sum(jnp.where(eye, t, 0.0), axis=0, keepdims=True)
    q = _blocks(qc * (D_HEAD ** -0.5))
    k = _blocks(kc)
    vs = _blocks(v)
    gate_rows = _split_chunks(gates)
    lane = lax.broadcasted_iota(jnp.int32, gate_rows[0].shape, 1)
    each = [(i, h) for i in range(n) for h in HEADS]
    on_each = lambda f: {ih: f(*ih) for ih in each}
    ig = on_each(lambda i, h: _gate_column(gate_rows[i], lane, h))
    lf = on_each(lambda i, h: jax.nn.log_sigmoid(_gate_column(gate_rows[i], lane, N_HEADS + h)))
    lf_row = on_each(lambda i, h: to_row(lf[i, h]))
    ig_row = on_each(lambda i, h: to_row(ig[i, h]))
    b_col = on_each(lambda i, h: jnp.sum(jnp.where(mask, lf_row[i, h], 0.0), axis=1, keepdims=True))
    b_row = on_each(lambda i, h: jnp.sum(jnp.where(row <= col, lf[i, h], 0.0), axis=0, keepdims=True))
    g = on_each(lambda i, h: jnp.sum(lf[i, h], axis=0, keepdims=True))
    d = on_each(lambda i, h: jnp.where(mask, b_col[i, h] - b_row[i, h] + ig_row[i, h], -jnp.inf))
    a = on_each(lambda i, h: g[i, h] - b_col[i, h] + ig[i, h])
    m_at = {(0, h): ms[h] for h in HEADS}
    for i, h in each:
        m_at[i + 1, h] = lax.stop_gradient(jnp.maximum(g[i, h] + m_at[i, h], jnp.max(a[i, h], axis=0, keepdims=True)))
    inter = on_each(lambda i, h: b_col[i, h] + m_at[i, h])
    m_t = on_each(lambda i, h: lax.stop_gradient(jnp.maximum(inter[i, h], jnp.max(d[i, h], axis=1, keepdims=True))))
    qk = on_each(lambda i, h: _nt(q[i][h], k[i][h]))
    sc = on_each(lambda i, h: qk[i, h] * jnp.exp(d[i, h] - m_t[i, h]))
    w_inter = on_each(lambda i, h: jnp.exp(inter[i, h] - m_t[i, h]))
    sv = on_each(lambda i, h: _nn(sc[i, h], vs[i][h]))
    decay = on_each(lambda i, h: jnp.exp(g[i, h] + m_at[i, h] - m_at[i + 1, h]))
    wk = on_each(lambda i, h: k[i][h] * jnp.exp(a[i, h] - m_at[i + 1, h]))
    kv = on_each(lambda i, h: _tn(vs[i][h], wk[i, h]))
    normed = []
    for i in range(n):
        qc_state = [_nt(q[i][h], cts[h]) for h in HEADS]
        num = [sv[i, h] + w_inter[i, h] * qc_state[h] for h in HEADS]
        den = [jnp.sum(sc[i, h], axis=1, keepdims=True)
               + w_inter[i, h] * jnp.sum(q[i][h] * ns[h], axis=1, keepdims=True) for h in HEADS]
        hh = [num[h] / jnp.maximum(jnp.abs(den[h]), jnp.exp(-m_t[i, h])) for h in HEADS]
        cts = tuple(decay[i, h] * cts[h] + kv[i, h] for h in HEADS)
        ns = tuple(decay[i, h] * ns[h] + jnp.sum(wk[i, h], axis=0, keepdims=True) for h in HEADS)
        normed.append(_merge_heads(tuple(_head_layer_norm(hh[h]) for h in HEADS)))
    y = jax.nn.sigmoid(mo) * (_merge_chunks(tuple(normed)) * nw)
    return y, cts, ns, tuple(m_at[n, h] for h in HEADS)


def _mlstm_fwd(qk, proj, gates, norm_w, y):
    s = proj.shape[0]
    rows = ML_CHUNKS_PER_STEP * CHUNK
    nc = s // rows

    def body(qk_ref, vo_ref, g_ref, nw_ref, _, y_ref, ct_out, n_out, m_out, ct_scr, n_scr, m_scr):
        @pl.when(pl.program_id(0) == 0)
        def _():
            ct_scr[...] = jnp.zeros_like(ct_scr)
            n_scr[...] = jnp.zeros_like(n_scr)
            m_scr[...] = jnp.full(m_scr.shape, NEG_BIG, F32)

        cts = tuple(ct_scr[h] for h in HEADS)
        ns = tuple(n_scr[h] for h in HEADS)
        ms = tuple(m_scr[h] for h in HEADS)
        y, cts_new, ns_new, ms_new = _ml_chunk(_seg(qk_ref, 0), _seg(qk_ref, 1), _seg(vo_ref, 0), _seg(vo_ref, 1),
                                               g_ref[...], nw_ref[...], cts, ns, ms)
        y_ref[...] = y.astype(y_ref.dtype)
        for h in HEADS:
            ct_out[h], n_out[h], m_out[h] = cts[h], ns[h], ms[h]
            ct_scr[h], n_scr[h], m_scr[h] = cts_new[h], ns_new[h], ms_new[h]

    st = lambda r, w: pl.BlockSpec((None, N_HEADS, r, w), lambda c: (c, 0, 0, 0))
    return pl.pallas_call(
        body, name="mlstm_fwd", grid=(nc,),
        in_specs=[pl.BlockSpec((rows, 2 * D_GROUP), lambda c: (c, 0)),
                  pl.BlockSpec((rows, 2 * D_GROUP), lambda c: (c, 3)),
                  pl.BlockSpec((rows, LANES), lambda c: (c, 0)),
                  pl.BlockSpec((1, D_GROUP), lambda c: (0, 0)),
                  pl.BlockSpec(memory_space=pl.ANY)],
        out_specs=[pl.BlockSpec((rows, D_GROUP), lambda c: (c, 1)),
                   st(D_HEAD, D_HEAD), st(1, D_HEAD), st(1, 1)],
        out_shape=[jax.ShapeDtypeStruct(y.shape, y.dtype),
                   jax.ShapeDtypeStruct((nc, N_HEADS, D_HEAD, D_HEAD), F32),
                   jax.ShapeDtypeStruct((nc, N_HEADS, 1, D_HEAD), F32),
                   jax.ShapeDtypeStruct((nc, N_HEADS, 1, 1), F32)],
        input_output_aliases={4: 0},
        scratch_shapes=[pltpu.VMEM((N_HEADS, D_HEAD, D_HEAD), F32), pltpu.VMEM((N_HEADS, 1, D_HEAD), F32),
                        pltpu.VMEM((N_HEADS, 1, 1), F32)],
        compiler_params=_params(("arbitrary",)),
    )(qk, proj, gates, norm_w, y)


def _mlstm_bwd(qk, proj, gates, norm_w, ct_s, n_s, m_s, d_y, d_proj):
    s = proj.shape[0]
    rows = ML_CHUNKS_PER_STEP * CHUNK
    nc = s // rows

    def body(qk_ref, vo_ref, g_ref, nw_ref, ct_ref, n_ref, m_ref, dy_ref, _,
             dp_ref, dqk_ref, dg_ref, dnw_ref, dsum_ref, dct_scr, dn_scr):
        @pl.when(pl.program_id(0) == 0)
        def _():
            dct_scr[...] = jnp.zeros_like(dct_scr)
            dn_scr[...] = jnp.zeros_like(dn_scr)
            dnw_ref[...] = jnp.zeros_like(dnw_ref)
            dsum_ref[...] = jnp.zeros_like(dsum_ref)

        ms = tuple(m_ref[h] for h in HEADS)
        step = lambda *a: _ml_chunk(*a, ms)[:3]
        _, vjp = jax.vjp(step, _seg(qk_ref, 0), _seg(qk_ref, 1), _seg(vo_ref, 0), _seg(vo_ref, 1), g_ref[...],
                         nw_ref[...], tuple(ct_ref[h] for h in HEADS), tuple(n_ref[h] for h in HEADS))
        d_q, d_k, d_v, d_o, d_gates, d_nw, d_cts, d_ns = vjp(
            (dy_ref[...], tuple(dct_scr[h] for h in HEADS), tuple(dn_scr[h] for h in HEADS)))
        dqk_ref[:, 0:D_GROUP] = d_q
        dqk_ref[:, D_GROUP:2 * D_GROUP] = d_k
        for seg, val in enumerate((d_v, d_o)):
            dp_ref[:, seg * D_GROUP:(seg + 1) * D_GROUP] = val.astype(dp_ref.dtype)
            dsum_ref[:, seg * D_GROUP:(seg + 1) * D_GROUP] += jnp.sum(val, axis=0, keepdims=True)
        dg_ref[...] = d_gates
        dnw_ref[...] += d_nw
        for h in HEADS:
            dct_scr[h] = d_cts[h]
            dn_scr[h] = d_ns[h]

    rev = lambda c: nc - 1 - c
    st = lambda r, w: pl.BlockSpec((None, N_HEADS, r, w), lambda c: (rev(c), 0, 0, 0))
    return pl.pallas_call(
        body, name="mlstm_bwd", grid=(nc,),
        in_specs=[pl.BlockSpec((rows, 2 * D_GROUP), lambda c: (rev(c), 0)),
                  pl.BlockSpec((rows, 2 * D_GROUP), lambda c: (rev(c), 3)),
                  pl.BlockSpec((rows, LANES), lambda c: (rev(c), 0)),
                  pl.BlockSpec((1, D_GROUP), lambda c: (0, 0)),
                  st(D_HEAD, D_HEAD), st(1, D_HEAD), st(1, 1),
                  pl.BlockSpec((rows, D_GROUP), lambda c: (rev(c), 1)),
                  pl.BlockSpec(memory_space=pl.ANY)],
        out_specs=[pl.BlockSpec((rows, 2 * D_GROUP), lambda c: (rev(c), 3)),
                   pl.BlockSpec((rows, 2 * D_GROUP), lambda c: (rev(c), 0)),
                   pl.BlockSpec((rows, LANES), lambda c: (rev(c), 0)),
                   pl.BlockSpec((1, D_GROUP), lambda c: (0, 0)),
                   pl.BlockSpec((1, 2 * D_GROUP), lambda c: (0, 0))],
        out_shape=[jax.ShapeDtypeStruct(d_proj.shape, d_proj.dtype), jax.ShapeDtypeStruct((s, 2 * D_GROUP), F32),
                   jax.ShapeDtypeStruct((s, LANES), F32), jax.ShapeDtypeStruct((1, D_GROUP), F32),
                   jax.ShapeDtypeStruct((1, 2 * D_GROUP), F32)],
        input_output_aliases={8: 0},
        scratch_shapes=[pltpu.VMEM((N_HEADS, D_HEAD, D_HEAD), F32), pltpu.VMEM((N_HEADS, 1, D_HEAD), F32)],
        compiler_params=_params(("arbitrary",)),
    )(qk, proj, gates, norm_w, ct_s, n_s, m_s, d_y, d_proj)


LN_TOKENS = 512
ATT_TOKENS = 256


def _res_ln_fwd(xres, branch, g, b, name):
    s, dm = xres.shape
    tb = min(LN_TOKENS, s)

    def body(x_ref, br_ref, g_ref, b_ref, o_ref):
        o_ref[...] = _layer_norm(ALPHA * x_ref[...] + br_ref[...], g_ref[...], b_ref[...])

    tok = pl.BlockSpec((tb, dm), lambda i: (i, 0))
    vec = pl.BlockSpec((1, dm), lambda i: (0, 0))
    return pl.pallas_call(
        body, name=name, grid=(s // tb,), in_specs=[tok, tok, vec, vec], out_specs=tok,
        out_shape=jax.ShapeDtypeStruct((s, dm), F32), compiler_params=_params(("parallel",)),
    )(xres, branch, g, b)


def _res_ln_bwd(xres, branch, g, b, d_out, name):
    s, dm = xres.shape
    tb = min(LN_TOKENS, s)

    def body(x_ref, br_ref, g_ref, b_ref, do_ref, dz_ref, dg_ref, db_ref):
        @pl.when(pl.program_id(0) == 0)
        def _():
            dg_ref[...] = jnp.zeros_like(dg_ref)
            db_ref[...] = jnp.zeros_like(db_ref)

        z = ALPHA * x_ref[...] + br_ref[...]
        _, vjp = jax.vjp(_layer_norm, z, g_ref[...], b_ref[...])
        d_z, d_g, d_b = vjp(do_ref[...])
        dz_ref[...] = d_z
        dg_ref[...] += d_g
        db_ref[...] += d_b

    tok = pl.BlockSpec((tb, dm), lambda i: (i, 0))
    vec = pl.BlockSpec((1, dm), lambda i: (0, 0))
    return pl.pallas_call(
        body, name=name, grid=(s // tb,), in_specs=[tok, tok, vec, vec, tok], out_specs=[tok, vec, vec],
        out_shape=[jax.ShapeDtypeStruct((s, dm), F32), jax.ShapeDtypeStruct((1, dm), F32),
                   jax.ShapeDtypeStruct((1, dm), F32)],
        compiler_params=_params(("arbitrary",)),
    )(xres, branch, g, b, d_out)


def _loss_tail(xres, branch, g, b, target):
    s, dm = xres.shape
    tb = min(LN_TOKENS, s)

    def loss_fn(z, gg, bb, tgt):
        err = jnp.square(_layer_norm(z, gg, bb) - tgt)
        return 0.5 * jnp.sum(jnp.mean(err, axis=-1, keepdims=True), axis=0, keepdims=True)

    def body(x_ref, br_ref, g_ref, b_ref, t_ref, loss_ref, dz_ref, dg_ref, db_ref):
        @pl.when(pl.program_id(0) == 0)
        def _():
            loss_ref[...] = jnp.zeros_like(loss_ref)
            dg_ref[...] = jnp.zeros_like(dg_ref)
            db_ref[...] = jnp.zeros_like(db_ref)

        z = ALPHA * x_ref[...] + br_ref[...]
        tgt = t_ref[...]
        loss, vjp = jax.vjp(lambda zz, gg, bb: loss_fn(zz, gg, bb, tgt), z, g_ref[...], b_ref[...])
        d_z, d_g, d_b = vjp(jnp.ones((1, 1), F32))
        loss_ref[...] += loss
        dz_ref[...] = d_z
        dg_ref[...] += d_g
        db_ref[...] += d_b

    tok = pl.BlockSpec((tb, dm), lambda i: (i, 0))
    vec = pl.BlockSpec((1, dm), lambda i: (0, 0))
    one = pl.BlockSpec((1, 1), lambda i: (0, 0))
    return pl.pallas_call(
        body, name="loss_tail", grid=(s // tb,), in_specs=[tok, tok, vec, vec, tok],
        out_specs=[one, tok, vec, vec],
        out_shape=[jax.ShapeDtypeStruct((1, 1), F32), jax.ShapeDtypeStruct((s, dm), F32),
                   jax.ShapeDtypeStruct((1, dm), F32), jax.ShapeDtypeStruct((1, dm), F32)],
        compiler_params=_params(("arbitrary",)),
    )(xres, branch, g, b, target)


def _att_head(q, k, v):
    sc = _nt(q, k) * (CA_DH ** -0.5)
    return _nn(jax.nn.softmax(sc, axis=-1), v)


def _att_fwd(q, kv):
    s = q.shape[0]
    tb = min(ATT_TOKENS, s)

    def body(q_ref, kv_ref, o_ref):
        for h in range(CA_HEADS):
            lo = h * CA_DH
            o_ref[:, lo:lo + CA_DH] = _att_head(q_ref[:, lo:lo + CA_DH], kv_ref[:, lo:lo + CA_DH],
                                                kv_ref[:, D_MODEL + lo:D_MODEL + lo + CA_DH]).astype(o_ref.dtype)

    tok = pl.BlockSpec((tb, D_MODEL), lambda i: (i, 0))
    return pl.pallas_call(
        body, name="att_fwd", grid=(s // tb,),
        in_specs=[tok, pl.BlockSpec((N_MEM, 2 * D_MODEL), lambda i: (0, 0))], out_specs=tok,
        out_shape=jax.ShapeDtypeStruct((s, D_MODEL), BF16), compiler_params=_params(("parallel",)),
    )(q, kv)


def _att_bwd(q, kv, d_o):
    s = q.shape[0]
    tb = min(ATT_TOKENS, s)

    def body(q_ref, kv_ref, do_ref, dq_ref, dkv_ref):
        @pl.when(pl.program_id(0) == 0)
        def _():
            dkv_ref[...] = jnp.zeros_like(dkv_ref)

        for h in range(CA_HEADS):
            lo = h * CA_DH
            vlo = D_MODEL + lo
            _, vjp = jax.vjp(_att_head, q_ref[:, lo:lo + CA_DH], kv_ref[:, lo:lo + CA_DH],
                             kv_ref[:, vlo:vlo + CA_DH])
            d_q, d_k, d_v = vjp(do_ref[:, lo:lo + CA_DH])
            dq_ref[:, lo:lo + CA_DH] = d_q
            dkv_ref[:, lo:lo + CA_DH] += d_k
            dkv_ref[:, vlo:vlo + CA_DH] += d_v

    tok = pl.BlockSpec((tb, D_MODEL), lambda i: (i, 0))
    mem = pl.BlockSpec((N_MEM, 2 * D_MODEL), lambda i: (0, 0))
    return pl.pallas_call(
        body, name="att_bwd", grid=(s // tb,), in_specs=[tok, mem, tok], out_specs=[tok, mem],
        out_shape=[jax.ShapeDtypeStruct((s, D_MODEL), F32), jax.ShapeDtypeStruct((N_MEM, 2 * D_MODEL), F32)],
        compiler_params=_params(("arbitrary",)),
    )(q, kv, d_o)


def _local_step(x, mem, target, w, mid_weights=None, ffn_weights=None, on_ffn_grads=None, on_mid_grads=None,
                on_small_grads=None, on_last_grads=None):
    w = dict(w)
    s = x.shape[0]
    tm = min(512, s)
    tt = min(512, s)
    proj = _matmul_nn(x, w["w_in_main"], w["b_in_main"], min(2048, s), 512, "proj")
    gates = _matmul_nn(x, w["w_in_gate"], w["b_in_gate"], tm, LANES, "proj_gates")
    qk = _ml_conv_fwd(proj, w["ml_conv_w"], w["ml_conv_b"])
    y, hg_states = _hgrn2_fwd(proj, w["hg_lb_logits"], w["hg_norm_w"])
    y, ct_s, n_s, m_s = _mlstm_fwd(qk, proj, gates, w["ml_norm_w"], y)
    if mid_weights is not None:
        w.update(mid_weights(y))
    mix =_matmul_nn(y, w["w_out"], None, tm, D_MODEL, "mix")
    x1 = _res_ln_fwd(x, mix, w["ln1_g"], w["ln1_b"], "ln1_fwd")
    kv = _matmul_nn(mem, w["ca_wkv"], None, N_MEM, CA_DH, "kv")
    q = _matmul_nn(x1, w["ca_wq"], None, tm, D_MODEL, "ca_q")
    att = _att_fwd(q, kv)
    ca = _matmul_nn(att, w["ca_wo"], None, tm, D_MODEL, "ca_out")
    x2 = _res_ln_fwd(x1, ca, w["ln2_g"], w["ln2_b"], "ln2_fwd")
    if ffn_weights is not None:
        w.update(ffn_weights(x2))
    u = _matmul_nn(x2, w["ffn_w_up"], None, min(2048, s), UP_SHARD_P, "ffn_up")
    hid = _ffn_conv_fwd(u, w["ffn_conv_w"], w["ffn_conv_b"])
    ff = _matmul_nn(hid, w["ffn_w_down"], None, tm, D_MODEL, "ffn_down")
    loss, d_z3, d_ln3_g, d_ln3_b = _loss_tail(x2, ff, w["ln3_g"], w["ln3_b"], target)
    grads = {"ln3_g": d_ln3_g, "ln3_b": d_ln3_b}
    grads["ffn_w_down"] = _matmul_tn(hid, d_z3, 1536, D_MODEL, tt, "d_w_down")
    d_hid = _matmul_nt([(d_z3, w["ffn_w_down"])], None, 1.0, tm, D_FF_P, "d_hid")
    d_ug, d_uv, d_cwg, d_cwv, d_cbg, d_cbv = _ffn_conv_bwd(u, w["ffn_conv_w"], w["ffn_conv_b"], d_hid)
    grads["ffn_conv_w"] = jnp.concatenate([d_cwg, d_cwv], axis=-1)
    grads["ffn_conv_b"] = jnp.concatenate([d_cbg, d_cbv], axis=-1)
    half = N_DEV // 2
    d_w_up = _matmul_tn(x2, d_ug, D_MODEL, UP_SHARD_P, tt, "d_w_up_gate", shards=N_DEV, group=half)
    grads["ffn_w_up"] = _matmul_tn(x2, d_uv, D_MODEL, UP_SHARD_P, tt, "d_w_up_val", shards=N_DEV,
                                   shard0=half, group=half, into=d_w_up)
    d_x2 = _matmul_nt([(d_ug, w["ffn_w_up"], 0), (d_uv, w["ffn_w_up"], N_DEV // 2)], d_z3, ALPHA,
                      min(256, s), D_MODEL, "d_x2")
    if on_ffn_grads is not None:
        d_x2 = on_ffn_grads(grads, d_x2)
    d_z2, grads["ln2_g"], grads["ln2_b"] = _res_ln_bwd(x1, ca, w["ln2_g"], w["ln2_b"], d_x2, "ln2_bwd")
    grads["ca_wo"] = _matmul_tn(att, d_z2, D_MODEL, D_MODEL, tt, "d_ca_wo")
    d_att = _matmul_nt([(d_z2, w["ca_wo"])], None, 1.0, tm, D_MODEL, "d_att")
    d_q, d_kv = _att_bwd(q, kv, d_att)
    grads["ca_wq"] = _matmul_tn(x1, d_q, D_MODEL, D_MODEL, tt, "d_ca_wq")
    grads["ca_wkv"] = _matmul_tn(mem, d_kv, D_MODEL, CA_DH, N_MEM, "d_ca_wkv", shards=N_DEV, group=N_DEV)
    d_x1 = _matmul_nt([(d_q, w["ca_wq"])], d_z2, ALPHA, tm, D_MODEL, "d_x1")
    d_z1, grads["ln1_g"], grads["ln1_b"] = _res_ln_bwd(x, mix, w["ln1_g"], w["ln1_b"], d_x1, "ln1_bwd")
    grads["w_out"] = _matmul_tn(y, d_z1, D_MODEL, D_MODEL, tt, "d_w_out")
    if on_mid_grads is not None:
        d_z1 = on_mid_grads(grads, d_z1)
    d_y = _matmul_nt([(d_z1, w["w_out"])], None, 1.0, tm, D_MODEL, "d_y")
    d_proj, grads["hg_lb_logits"], grads["hg_norm_w"], db_hg = _hgrn2_bwd(
        proj, w["hg_lb_logits"], w["hg_norm_w"], hg_states, d_y)
    d_proj, d_qk, d_gates, grads["ml_norm_w"], db_vo = _mlstm_bwd(
        qk, proj, gates, w["ml_norm_w"], ct_s, n_s, m_s, d_y, d_proj)
    d_proj, grads["ml_conv_w"], grads["ml_conv_b"], db_qk = _ml_conv_bwd(
        proj, w["ml_conv_w"], w["ml_conv_b"], d_qk, d_proj)
    grads["b_in_main"] = jnp.concatenate([db_hg, db_qk, db_vo], axis=-1)
    grads["w_in_gate"], grads["b_in_gate"] = _matmul_tn(x, d_gates, D_MODEL, LANES, tt, "d_w_in_gates", colsum=True)
    if on_small_grads is not None:
        d_proj = on_small_grads(grads, loss, d_proj)
    grads["w_in_main"] = _matmul_tn(x, d_proj, D_MODEL, min(2048, D_IN_MAIN), tt, "d_w_in")
    if on_last_grads is not None:
        d_z1 = on_last_grads(grads, d_z1)
    grad_x = _matmul_nt([(d_proj, w["w_in_main"]), (d_gates, w["w_in_gate"])], d_z1, ALPHA, tm, D_MODEL, "d_x")
    return loss, grad_x, grads


HBM_SPEC = pl.BlockSpec(memory_space=pltpu.HBM)


def _coords():
    return lax.axis_index("x"), lax.axis_index("y"), lax.axis_index("c")


def _other_chips(x, y):
    return [(1 - x, y), (x, 1 - y), (1 - x, 1 - y)]


def _all_gather_two_level(shards, name):
    na = len(shards)

    def body(*refs):
        x_refs, out_refs = refs[:na], refs[na:2 * na]
        send_sems, recv_sems, local_sems = refs[2 * na:]
        x, y, c = _coords()
        me, sibling = (x, y, c), (x, y, 1 - c)
        chips = _other_chips(x, y)

        def copy(a, k, block, to, own=False):
            slot = out_refs[a].at[4 * block[0] + 2 * block[1] + block[2]]
            return pltpu.make_async_remote_copy(
                src_ref=x_refs[a] if own else slot, dst_ref=slot,
                send_sem=send_sems.at[7 * a + k], recv_sem=recv_sems.at[7 * a + k],
                device_id=to, device_id_type=MESH)

        mine = [pltpu.make_async_copy(x_refs[a], out_refs[a].at[4 * x + 2 * y + c], local_sems.at[a])
                for a in range(na)]
        for cp in mine:
            cp.start()
        first = []
        for a in range(na):
            first.append(copy(a, 0, me, sibling, own=True))
            first += [copy(a, 1 + j, me, (*chip, c), own=True) for j, chip in enumerate(chips)]
        for cp in first:
            cp.start()
        passed = []
        for j, chip in enumerate(chips):
            for a in range(na):
                copy(a, 1 + j, (*chip, c), me).wait_recv()
                fwd = copy(a, 4 + j, (*chip, c), sibling)
                fwd.start()
                passed.append(fwd)
        for a in range(na):
            copy(a, 0, sibling, me).wait_recv()
            for j, chip in enumerate(chips):
                copy(a, 4 + j, (*chip, 1 - c), me).wait_recv()
        for cp in first + passed:
            cp.wait_send()
        for cp in mine:
            cp.wait()

    return pl.pallas_call(
        body, name=name,
        out_shape=[jax.ShapeDtypeStruct((N_DEV,) + t.shape, t.dtype) for t in shards],
        in_specs=[HBM_SPEC] * na, out_specs=[HBM_SPEC] * na,
        scratch_shapes=[pltpu.SemaphoreType.DMA((7 * na,)), pltpu.SemaphoreType.DMA((7 * na,)),
                        pltpu.SemaphoreType.DMA((na,))],
    )(*shards)


SEM_SPEC = pl.BlockSpec(memory_space=pltpu.SEMAPHORE)
ANY_SPEC = pl.BlockSpec(memory_space=pl.ANY)
SIDE_EFFECT = pltpu.SideEffectType.DATAFLOW_SIDE_EFFECTING


def _peer(x, y, c, d):
    flip = lambda v, bit: 1 - v if bit else v
    p = (flip(x, d & 4), flip(y, d & 2), flip(c, d & 1))
    return p, 4 * p[0] + 2 * p[1] + p[2]


def _direct_copies(gather, src_refs, land_refs, send_sems, recv_sems):
    x, y, c = _coords()
    me = 4 * x + 2 * y + c
    copies = []
    for a in range(len(src_refs)):
        for d in range(1, N_DEV):
            peer, peer_slot = _peer(x, y, c, d)
            copies.append(pltpu.make_async_remote_copy(
                src_ref=src_refs[a] if gather else src_refs[a].at[peer_slot],
                dst_ref=land_refs[a].at[me] if gather else land_refs[a].at[d - 1],
                send_sem=send_sems.at[7 * a + d - 1], recv_sem=recv_sems.at[7 * a + d - 1],
                device_id=peer, device_id_type=MESH))
    return copies


def _hbm(t):
    return pltpu.HBM(t.shape, t.dtype)


def _direct_start(gather, arrays, through, name):
    na = len(arrays)
    lands = [lax.empty((N_DEV,) + t.shape if gather else (N_DEV - 1,) + t.shape[1:], t.dtype) for t in arrays]
    n_io = 2 * na + 1

    def body(*refs):
        for cp in _direct_copies(gather, refs[:na], refs[na:2 * na], refs[n_io], refs[n_io + 1]):
            cp.start()

    ins = [pltpu.with_memory_space_constraint(t, pltpu.HBM) for t in (*arrays, *lands, through)]
    sems = pltpu.SemaphoreType.DMA((7 * na,))
    res = pl.pallas_call(
        body, name=name, out_shape=(sems, sems, *[_hbm(t) for t in ins]),
        in_specs=[HBM_SPEC] * n_io, out_specs=(SEM_SPEC, SEM_SPEC, *[HBM_SPEC] * n_io),
        input_output_aliases={i: 2 + i for i in range(n_io)},
        compiler_params=pltpu.CompilerParams(has_side_effects=SIDE_EFFECT),
    )(*ins)
    return (res[0], res[1], list(res[2:2 + na]), list(res[2 + na:2 + 2 * na])), res[2 + 2 * na]


def _direct_wait(gather, started, after, name):
    send_sems, recv_sems, arrays, lands = started
    na = len(arrays)

    def body(*refs):
        for cp in _direct_copies(gather, refs[:na], refs[na:2 * na], refs[2 * na], refs[2 * na + 1]):
            cp.wait_send()
            cp.wait_recv()

    res = pl.pallas_call(
        body, name=name, out_shape=tuple(_hbm(t) for t in (*arrays, *lands)),
        in_specs=[HBM_SPEC] * (2 * na) + [SEM_SPEC, SEM_SPEC, ANY_SPEC], out_specs=tuple([HBM_SPEC] * (2 * na)),
        input_output_aliases={i: i for i in range(2 * na)},
        compiler_params=pltpu.CompilerParams(has_side_effects=SIDE_EFFECT),
    )(*arrays, *lands, send_sems, recv_sems, after)
    return list(res[:na]), list(res[na:])


def _row_tile(rows):
    for t in (256, 176, 128):
        if rows % t == 0 and rows > t:
            return t
    return rows


def _adamw_math(g, w, m, v):
    m_new = ADAM_B1 * m + (1.0 - ADAM_B1) * g
    v_new = ADAM_B2 * v + (1.0 - ADAM_B2) * jnp.square(g)
    m_hat = m_new / (1.0 - ADAM_B1 ** ADAM_STEP)
    v_hat = v_new / (1.0 - ADAM_B2 ** ADAM_STEP)
    delta = -ADAM_LR * (m_hat / (jnp.sqrt(v_hat) + ADAM_EPS) + ADAM_WD * w)
    return delta, m_new, v_new


def _adamw_sharded(chip, sums, got, w, m, v, name):
    r, c = w.shape
    tr = _row_tile(r)
    n_got = got.shape[0]

    def body(chip_ref, s_ref, g_ref, w_ref, m_ref, v_ref, go_ref, d_ref, nm_ref, nv_ref):
        g = s_ref[...].astype(F32)
        for i in range(n_got):
            g = g + g_ref[i].astype(F32)
        delta, m_new, v_new = _adamw_math(g, w_ref[...], m_ref[...], v_ref[...])
        go_ref[...] = g
        d_ref[...] = delta
        nm_ref[...] = m_new
        nv_ref[...] = v_new

    blk = pl.BlockSpec((tr, c), lambda i, chip_ref: (i, 0))
    out = jax.ShapeDtypeStruct((r, c), F32)
    return pl.pallas_call(
        body, name=name,
        grid_spec=pltpu.PrefetchScalarGridSpec(
            num_scalar_prefetch=1, grid=(r // tr,),
            in_specs=[pl.BlockSpec((None, tr, c), lambda i, chip_ref: (chip_ref[0], i, 0)),
                      pl.BlockSpec((n_got, tr, c), lambda i, chip_ref: (0, i, 0)), blk, blk, blk],
            out_specs=[blk, blk, blk, blk]),
        out_shape=[out, out, out, out],
        compiler_params=_params(("parallel",)),
    )(chip, sums, got, w, m, v)


def _adamw_replicated(parts, w, m, v):
    p, r, c = parts.shape

    def body(p_ref, w_ref, m_ref, v_ref, g_ref, d_ref, nm_ref, nv_ref):
        g = p_ref[0]
        for i in range(1, p):
            g = g + p_ref[i]
        delta, m_new, v_new = _adamw_math(g, w_ref[...], m_ref[...], v_ref[...])
        g_ref[...] = g
        d_ref[...] = delta
        nm_ref[...] = m_new
        nv_ref[...] = v_new

    blk = pl.BlockSpec((r, c), lambda i: (0, 0))
    out = jax.ShapeDtypeStruct((r, c), F32)
    return pl.pallas_call(
        body, name="adamw_replicated", grid=(1,),
        in_specs=[pl.BlockSpec((p, r, c), lambda i: (0, 0, 0)), blk, blk, blk],
        out_specs=[blk, blk, blk, blk], out_shape=[out, out, out, out],
        compiler_params=_params(("arbitrary",)),
    )(parts, w, m, v)


SHARDED_NAMES = ("w_in", "ml_conv_w", "w_out", "ca_wq", "ca_wkv", "ca_wo", "ffn_w_up", "ffn_conv_w", "ffn_w_down")
SMALL_NAMES = ("b_in", "hg_lb_logits", "hg_norm_w", "ml_conv_b", "ml_norm_w", "ln1_g", "ln1_b",
               "ln2_g", "ln2_b", "ffn_conv_b", "ln3_g", "ln3_b")
WEIGHT_NAMES = ("w_in", "b_in", "hg_lb_logits", "hg_norm_w", "ml_conv_w", "ml_conv_b", "ml_norm_w", "w_out",
                "ln1_g", "ln1_b", "ca_wq", "ca_wkv", "ca_wo", "ln2_g", "ln2_b", "ffn_w_up", "ffn_conv_w",
                "ffn_conv_b", "ffn_w_down", "ln3_g", "ln3_b")
PAD_TO = {"w_in": W_IN_SHARD_P, "ffn_w_up": UP_SHARD_P, "ffn_conv_w": UP_SHARD_P}
SMALL_ROWS = 24
SMALL_W = D_MODEL


def _shard_2d(name, block):
    t = block[0]
    if name in PAD_TO:
        t = jnp.pad(t, ((0, 0), (0, PAD_TO[name] - t.shape[1])))
    return t


def _shard_like(name, t, like):
    return t[:, :like.shape[2]][None]


def _pad_cols(t, width):
    return jnp.pad(t, ((0, 0), (0, width - t.shape[1])))


FIRST_NAMES = ("w_in", "ml_conv_w")
FFN_NAMES = ("ffn_w_up", "ffn_w_down", "ffn_conv_w")
MID_NAMES = ("ca_wo", "ca_wq", "ca_wkv", "w_out")


def _first_weights(g, small):
    w = dict(small)
    w_in = jnp.concatenate([g["w_in"][j, :, :W_IN_SHARD] for j in range(N_DEV)], axis=1)
    w["w_in_main"] = w_in[:, :D_IN_MAIN]
    w["w_in_gate"] = _pad_cols(w_in[:, D_IN_MAIN:], LANES)
    w["b_in_main"] = small["b_in"][:, :D_IN_MAIN]
    w["b_in_gate"] = _pad_cols(small["b_in"][:, D_IN_MAIN:], LANES)
    w["ml_conv_w"] = jnp.transpose(g["ml_conv_w"], (1, 0, 2)).reshape(ML_CONV, 2 * D_GROUP)
    return w


def _mid_weights(g):
    w = {n: g[n].reshape(D_MODEL, D_MODEL) for n in ("w_out", "ca_wq", "ca_wo")}
    w["ca_wkv"] = g["ca_wkv"]
    return w


def _ffn_weights(g, small):
    w = {"ffn_w_up": g["ffn_w_up"]}
    down = g["ffn_w_down"].reshape(N_DEV // 2, UP_SHARD, D_MODEL)
    w["ffn_w_down"] = jnp.pad(down, ((0, 0), (0, UP_SHARD_P - UP_SHARD), (0, 0))).reshape(D_FF_P, D_MODEL)
    w["ffn_conv_w"] = jnp.transpose(g["ffn_conv_w"], (1, 0, 2)).reshape(FFN_CONV, D_UP_P)
    w["ffn_conv_b"] = _pad_cols(small["ffn_conv_b"].reshape(N_DEV, UP_SHARD), UP_SHARD_P).reshape(1, D_UP_P)
    return w


def _whole_weights(g, small):
    return {**_first_weights(g, small), **_mid_weights(g), **_ffn_weights(g, small)}


def _owner_stack(n, grads):
    if n == "w_in":
        w_in = jnp.concatenate([grads["w_in_main"], grads["w_in_gate"][:, :D_IN - D_IN_MAIN]], axis=1)
        return jnp.stack([_pad_cols(w_in[:, j * W_IN_SHARD:(j + 1) * W_IN_SHARD], W_IN_SHARD_P)
                          for j in range(N_DEV)])
    if n in ("w_out", "ca_wq", "ca_wo"):
        return grads[n].reshape(N_DEV, D_MODEL // N_DEV, D_MODEL)
    if n == "ffn_w_down":
        down = grads[n].reshape(N_DEV // 2, UP_SHARD_P, D_MODEL)[:, :UP_SHARD]
        return down.reshape(N_DEV, D_FF // N_DEV, D_MODEL)
    if n == "ml_conv_w":
        return jnp.transpose(grads[n].reshape(ML_CONV, N_DEV, LANES), (1, 0, 2))
    if n == "ffn_conv_w":
        return jnp.transpose(grads[n].reshape(FFN_CONV, N_DEV, UP_SHARD_P), (1, 0, 2))
    return grads[n]


def _owner_stacks(grads):
    return {n: _owner_stack(n, grads) for n in SHARDED_NAMES}


def _small_grads(grads):
    out = {n: grads[n] for n in SMALL_NAMES if n in grads}
    out["b_in"] = jnp.concatenate([grads["b_in_main"], grads["b_in_gate"][:, :D_IN - D_IN_MAIN]], axis=1)
    out["ffn_conv_b"] = grads["ffn_conv_b"].reshape(N_DEV, UP_SHARD_P)[:, :UP_SHARD].reshape(1, D_UP)
    return out


def _pack_small(p, extra=None):
    flat = [p[n].reshape(-1) for n in SMALL_NAMES]
    if extra is not None:
        flat.append(extra.reshape(-1))
    flat = jnp.concatenate(flat)
    return jnp.pad(flat, (0, SMALL_ROWS * SMALL_W - flat.shape[0])).reshape(SMALL_ROWS, SMALL_W)


def _unpack_small(slab, like):
    out = {}
    flat = slab.reshape(-1)
    o = 0
    for n in SMALL_NAMES:
        out[n] = flat[o:o + like[n].size].reshape(like[n].shape)
        o += like[n].size
    return out, flat[o]


def kernel(x, mem, w_in, b_in, hg_lb_logits, hg_norm_w, ml_conv_w, ml_conv_b, ml_norm_w, w_out, ln1_g, ln1_b, ca_wq, ca_wkv, ca_wo, ln2_g, ln2_b, ffn_w_up, ffn_conv_w, ffn_conv_b, ffn_w_down, ln3_g, ln3_b, loss_target, m_w_in, m_b_in, m_hg_lb_logits, m_hg_norm_w, m_ml_conv_w, m_ml_conv_b, m_ml_norm_w, m_w_out, m_ln1_g, m_ln1_b, m_ca_wq, m_ca_wkv, m_ca_wo, m_ln2_g, m_ln2_b, m_ffn_w_up, m_ffn_conv_w, m_ffn_conv_b, m_ffn_w_down, m_ln3_g, m_ln3_b, v_w_in, v_b_in, v_hg_lb_logits, v_hg_norm_w, v_ml_conv_w, v_ml_conv_b, v_ml_norm_w, v_w_out, v_ln1_g, v_ln1_b, v_ca_wq, v_ca_wkv, v_ca_wo, v_ln2_g, v_ln2_b, v_ffn_w_up, v_ffn_conv_w, v_ffn_conv_b, v_ffn_w_down, v_ln3_g, v_ln3_b):
    params = dict(w_in=w_in, b_in=b_in, hg_lb_logits=hg_lb_logits, hg_norm_w=hg_norm_w, ml_conv_w=ml_conv_w,
                  ml_conv_b=ml_conv_b, ml_norm_w=ml_norm_w, w_out=w_out, ln1_g=ln1_g, ln1_b=ln1_b, ca_wq=ca_wq,
                  ca_wkv=ca_wkv, ca_wo=ca_wo, ln2_g=ln2_g, ln2_b=ln2_b, ffn_w_up=ffn_w_up, ffn_conv_w=ffn_conv_w,
                  ffn_conv_b=ffn_conv_b, ffn_w_down=ffn_w_down, ln3_g=ln3_g, ln3_b=ln3_b)
    mom1 = dict(w_in=m_w_in, b_in=m_b_in, hg_lb_logits=m_hg_lb_logits, hg_norm_w=m_hg_norm_w,
                ml_conv_w=m_ml_conv_w, ml_conv_b=m_ml_conv_b, ml_norm_w=m_ml_norm_w, w_out=m_w_out, ln1_g=m_ln1_g,
                ln1_b=m_ln1_b, ca_wq=m_ca_wq, ca_wkv=m_ca_wkv, ca_wo=m_ca_wo, ln2_g=m_ln2_g, ln2_b=m_ln2_b,
                ffn_w_up=m_ffn_w_up, ffn_conv_w=m_ffn_conv_w, ffn_conv_b=m_ffn_conv_b, ffn_w_down=m_ffn_w_down,
                ln3_g=m_ln3_g, ln3_b=m_ln3_b)
    mom2 = dict(w_in=v_w_in, b_in=v_b_in, hg_lb_logits=v_hg_lb_logits, hg_norm_w=v_hg_norm_w,
                ml_conv_w=v_ml_conv_w, ml_conv_b=v_ml_conv_b, ml_norm_w=v_ml_norm_w, w_out=v_w_out, ln1_g=v_ln1_g,
                ln1_b=v_ln1_b, ca_wq=v_ca_wq, ca_wkv=v_ca_wkv, ca_wo=v_ca_wo, ln2_g=v_ln2_g, ln2_b=v_ln2_b,
                ffn_w_up=v_ffn_w_up, ffn_conv_w=v_ffn_conv_w, ffn_conv_b=v_ffn_conv_b, ffn_w_down=v_ffn_w_down,
                ln3_g=v_ln3_g, ln3_b=v_ln3_b)

    x_idx, y_idx, c_idx = _coords()
    as_index = lambda v: jnp.reshape(v, (1,)).astype(jnp.int32)
    me = as_index(4 * x_idx + 2 * y_idx + c_idx)
    small_params = {n: params[n] for n in SMALL_NAMES}

    shards = {n: _shard_2d(n, params[n]) for n in SHARDED_NAMES}
    to_send = lambda names: [shards[n] if "conv" in n else shards[n].astype(BF16) for n in names]
    first = dict(zip(FIRST_NAMES, _all_gather_two_level(to_send(FIRST_NAMES), "weights_gather_first")))
    mid_started, through = _direct_start(True, to_send(MID_NAMES), first["w_in"], "weights_gather_start_mid")
    ffn_started, first["w_in"] = _direct_start(True, to_send(FFN_NAMES), through, "weights_gather_start_ffn")

    def gathered_weights(names, started, after, tag):
        mine, lands = _direct_wait(True, started, after, "weights_gather_wait_" + tag)
        return {n: lax.dynamic_update_index_in_dim(land, own, me[0], 0) for n, own, land in zip(names, mine, lands)}

    started, own_stacks = {}, {}

    def start_group(names, tag):
        def hook(grads, through):
            own_stacks[tag] = [_owner_stack(n, grads).astype(BF16) for n in names]
            started[tag], through = _direct_start(False, own_stacks[tag], through, "grads_start_" + tag)
            return through
        return hook

    def start_small(grads, loss, through):
        started["small"], through = _direct_start(True, [_pack_small(_small_grads(grads), loss)], through,
                                                  "small_gather_start")
        return through

    loss, grad_x, grads = _local_step(
        x[0], mem[0], loss_target[0], _first_weights(first, small_params),
        lambda y: _mid_weights(gathered_weights(MID_NAMES, mid_started, y, "mid")),
        lambda x2: _ffn_weights(gathered_weights(FFN_NAMES, ffn_started, x2, "ffn"), small_params),
        start_group(FFN_NAMES, "ffn"), start_group(MID_NAMES, "mid"), start_small, start_group(FIRST_NAMES, "last"))

    sharded_out = {}
    after = grad_x
    for names, tag in ((FFN_NAMES, "ffn"), (MID_NAMES, "mid"), (FIRST_NAMES, "last")):
        _, lands = _direct_wait(False, started[tag], after, "grads_wait_" + tag)
        for n, st, land in zip(names, own_stacks[tag], lands):
            res = _adamw_sharded(me, st, land, shards[n], _shard_2d(n, mom1[n]), _shard_2d(n, mom2[n]), "adamw_" + n)
            sharded_out[n] = [_shard_like(n, t, params[n]) for t in res]
            after = res[0]
    own_small, small_lands = _direct_wait(True, started["small"], after, "small_gather_wait")
    small_parts = lax.dynamic_update_index_in_dim(small_lands[0], own_small[0], me[0], 0)
    small_res = _adamw_replicated(small_parts, _pack_small(params), _pack_small(mom1), _pack_small(mom2))

    outs = []
    total_loss = None
    for k in range(4):
        small, extra = _unpack_small(small_res[k], params)
        if total_loss is None:
            total_loss = extra
        outs.extend(sharded_out[n][k] if n in sharded_out else small[n] for n in WEIGHT_NAMES)
    return (total_loss, grad_x[None], *outs)
```

```python
import functools

import jax
import jax.numpy as jnp
from jax import lax
from jax.experimental import pallas as pl
from jax.experimental.pallas import tpu as pltpu

F32 = jnp.float32
BF16 = jnp.bfloat16
HIGHEST = lax.Precision.HIGHEST
MESH = pl.DeviceIdType.MESH

N_DEV = 8
D_MODEL = 1024
N_MEM = 256
N_HEADS = 4
D_HEAD = 128
D_GROUP = N_HEADS * D_HEAD
CHUNK = 64
ML_CONV = 4
FFN_CONV = 3
D_FF = 2816
D_UP = 2 * D_FF
CA_HEADS = 4
CA_DH = D_MODEL // CA_HEADS
LANES = 128
SUBLANES = 8
D_IN = 8 * D_GROUP + 2 * N_HEADS
D_IN_MAIN = 8 * D_GROUP
W_IN_SHARD = D_IN // N_DEV
W_IN_SHARD_P = 640
UP_SHARD = D_UP // N_DEV
UP_SHARD_P = 768
D_UP_P = N_DEV * UP_SHARD_P
D_FF_P = D_UP_P // 2
ALPHA = 2.0 ** 0.25
LN_EPS = 1e-5
NEG_BIG = -1e30
ADAM_LR = 0.001
ADAM_B1 = 0.9
ADAM_B2 = 0.999
ADAM_EPS = 1e-08
ADAM_WD = 0.01
ADAM_STEP = 10
VMEM_LIMIT = 56 * 1024 * 1024

SEG_HQ, SEG_HF, SEG_HI, SEG_HG, SEG_MQ, SEG_MK, SEG_MV, SEG_MO = (4 * i for i in range(8))


def _params(sem):
    return pltpu.CompilerParams(dimension_semantics=sem, vmem_limit_bytes=VMEM_LIMIT)


def _dg(a, b, ca, cb, precision=None):
    return lax.dot_general(a, b, (((ca,), (cb,)), ((), ())), precision=precision,
                           preferred_element_type=F32)


def _nn_raw(a, b):
    return _dg(a.astype(BF16), b.astype(BF16), 1, 0)


def _nt_raw(a, b):
    return _dg(a.astype(BF16), b.astype(BF16), 1, 1)


def _tn_raw(a, b):
    return _dg(a.astype(BF16), b.astype(BF16), 0, 0)


@jax.custom_vjp
def _nn(a, b):
    return _nn_raw(a, b)


_nn.defvjp(lambda a, b: (_nn_raw(a, b), (a, b)),
           lambda res, g: (_nt_raw(g, res[1]), _tn_raw(res[0], g)))


@jax.custom_vjp
def _nt(a, b):
    return _nt_raw(a, b)


_nt.defvjp(lambda a, b: (_nt_raw(a, b), (a, b)),
           lambda res, g: (_nn_raw(g, res[1]), _tn_raw(g, res[0])))


@jax.custom_vjp
def _tn(a, b):
    return _tn_raw(a, b)


_tn.defvjp(lambda a, b: (_tn_raw(a, b), (a, b)),
           lambda res, g: (_nt_raw(res[1], g), _nn_raw(res[0], g)))


def _layer_norm(z, g, b):
    mu = jnp.mean(z, axis=-1, keepdims=True)
    var = jnp.mean(jnp.square(z - mu), axis=-1, keepdims=True)
    return (z - mu) * lax.rsqrt(var + LN_EPS) * g + b


def _matmul_nn(a, w, bias, tm, tn, name):
    m, k = a.shape
    if w.ndim == 3:
        n = w.shape[0] * w.shape[2]
        assert tn == w.shape[2]
        w_spec = pl.BlockSpec((None, k, tn), lambda i, j: (j, 0, 0))
    else:
        n = w.shape[1]
        w_spec = pl.BlockSpec((k, tn), lambda i, j: (0, j))

    def body(*refs):
        a_ref, w_ref = refs[0], refs[1]
        o_ref = refs[-1]
        acc = _nn_raw(a_ref[...], w_ref[...])
        if bias is not None:
            acc = acc + refs[2][...]
        o_ref[...] = acc

    in_specs = [pl.BlockSpec((tm, k), lambda i, j: (i, 0)), w_spec]
    args = [a, w]
    if bias is not None:
        in_specs.append(pl.BlockSpec((1, tn), lambda i, j: (0, j)))
        args.append(bias)
    return pl.pallas_call(
        body, name=name, grid=(m // tm, n // tn), in_specs=in_specs,
        out_specs=pl.BlockSpec((tm, tn), lambda i, j: (i, j)),
        out_shape=jax.ShapeDtypeStruct((m, n), F32),
        compiler_params=_params(("parallel", "parallel")),
    )(*args)


def _matmul_nt(pairs, add, scale, tm, tk, name):
    m = pairs[0][0].shape[0]
    k = pairs[0][1].shape[-2]
    groups = []
    in_specs, args = [], []
    for pair in pairs:
        d, w = pair[0], pair[1]
        in_specs.append(pl.BlockSpec((tm, d.shape[1]), lambda i, j: (i, 0)))
        if w.ndim == 3:
            g = d.shape[1] // w.shape[2]
            blk = pair[2] // g
            in_specs.append(pl.BlockSpec((g, tk, w.shape[2]), lambda i, j, blk=blk: (blk, j, 0)))
            groups.append((g, w.shape[2]))
        else:
            in_specs.append(pl.BlockSpec((tk, w.shape[1]), lambda i, j: (j, 0)))
            groups.append(None)
        args += [d, w]
    if add is not None:
        in_specs.append(pl.BlockSpec((tm, tk), lambda i, j: (i, j)))
        args.append(add)

    def body(*refs):
        o_ref = refs[-1]
        acc = None
        for p, grp in enumerate(groups):
            d_ref, w_ref = refs[2 * p], refs[2 * p + 1]
            if grp is None:
                terms = [_nt_raw(d_ref[...], w_ref[...])]
            else:
                terms = [_nt_raw(d_ref[:, g * grp[1]:(g + 1) * grp[1]], w_ref[g]) for g in range(grp[0])]
            for t in terms:
                acc = t if acc is None else acc + t
        if add is not None:
            acc = acc + scale * refs[2 * len(groups)][...]
        o_ref[...] = acc

    return pl.pallas_call(
        body, name=name, grid=(m // tm, k // tk), in_specs=in_specs,
        out_specs=pl.BlockSpec((tm, tk), lambda i, j: (i, j)),
        out_shape=jax.ShapeDtypeStruct((m, k), F32),
        compiler_params=_params(("parallel", "parallel")),
    )(*args)


def _matmul_tn(a, b, tm, tn, tt, name, shards=None, shard0=0, group=1, into=None, colsum=False):
    t, m = a.shape
    n = b.shape[1]
    assert not colsum or tm == m
    n_in = 2 + (into is not None)
    out_dtype = BF16
    per_step = 1 if shards is None else group
    width = per_step * tn

    def body(*refs):
        a_ref, b_ref = refs[0], refs[1]
        o_ref, acc_ref = refs[n_in], refs[-1]
        first = pl.program_id(2) == 0

        @pl.when(first)
        def _():
            acc_ref[...] = jnp.zeros_like(acc_ref)

        if shards is None:
            acc_ref[...] += _tn_raw(a_ref[...], b_ref[...])
        else:
            lhs = a_ref[...].astype(BF16)
            for g in range(per_step):
                acc_ref[g] += _tn_raw(lhs, b_ref[:, g * tn:(g + 1) * tn])

        @pl.when(pl.program_id(2) == t // tt - 1)
        def _():
            o_ref[...] = acc_ref[...].astype(o_ref.dtype)

        if colsum:
            s_ref = refs[n_in + 1]

            @pl.when(first)
            def _():
                s_ref[...] = jnp.zeros_like(s_ref)

            s_ref[...] += jnp.sum(b_ref[...], axis=0, keepdims=True)

    in_specs = [pl.BlockSpec((tt, tm), lambda i, j, kk: (kk, i)),
                pl.BlockSpec((tt, width), lambda i, j, kk: (kk, j))]
    args = [a, b]
    aliases = {}
    if into is not None:
        in_specs.append(pl.BlockSpec(memory_space=pl.ANY))
        args.append(into)
        aliases = {2: 0}
    if shards is None:
        out_specs = [pl.BlockSpec((tm, tn), lambda i, j, kk: (i, j))]
        out_shape = [jax.ShapeDtypeStruct((m, n), out_dtype)]
        acc = pltpu.VMEM((tm, tn), F32)
    else:
        out_specs = [pl.BlockSpec((per_step, tm, tn), lambda i, j, kk: (shard0 // per_step + j, i, 0))]
        out_shape = [jax.ShapeDtypeStruct((shards, m, tn), out_dtype)]
        acc = pltpu.VMEM((per_step, tm, tn), F32)
    if colsum:
        out_specs.append(pl.BlockSpec((1, tn), lambda i, j, kk: (0, j)))
        out_shape.append(jax.ShapeDtypeStruct((1, n), F32))
    res = pl.pallas_call(
        body, name=name, grid=(m // tm, n // width, t // tt), in_specs=in_specs, out_specs=out_specs,
        out_shape=out_shape, input_output_aliases=aliases, scratch_shapes=[acc],
        compiler_params=_params(("parallel", "parallel", "arbitrary")),
    )(*args)
    return res if colsum else res[0]


ROW_TILE = 64


def _conv_fwd_tile(pad_ref, w_ref, b_ref, r0, rows, taps):
    acc = b_ref[...]
    for j in range(taps):
        acc = acc + pad_ref[pl.ds(SUBLANES - (taps - 1 - j) + r0, rows), :] * w_ref[j:j + 1, :]
    return acc


def _conv_bwd_tile(dpad_ref, w_ref, r0, rows, taps):
    acc = None
    for j in range(taps):
        term = dpad_ref[pl.ds(r0 + (taps - 1 - j), rows), :] * w_ref[j:j + 1, :]
        acc = term if acc is None else acc + term
    return acc


def _conv_grads_tile(pad_ref, dpad_ref, dx_ref, w_ref, dws, r0, rows, taps):
    dx = _conv_bwd_tile(dpad_ref, w_ref, r0, rows, taps)
    dx_ref[r0:r0 + rows, :] = dx.astype(dx_ref.dtype)
    d_pre = dpad_ref[r0:r0 + rows, :]
    for j in range(taps):
        xs = pad_ref[pl.ds(SUBLANES - (taps - 1 - j) + r0, rows), :]
        dws[j] = dws[j] + jnp.sum(d_pre * xs, axis=0, keepdims=True)
    return jnp.sum(dx, axis=0, keepdims=True)


def _ml_conv_fwd(proj, conv_w, conv_b):
    s = proj.shape[0]
    nblk = 2 * D_GROUP // LANES

    def body(x_ref, w_ref, b_ref, o_ref, pad_ref):
        pad_ref[0:SUBLANES, :] = jnp.zeros((SUBLANES, LANES), F32)
        pad_ref[SUBLANES:, :] = x_ref[...]
        for r0 in range(0, s, ROW_TILE):
            rows = min(ROW_TILE, s - r0)
            o_ref[r0:r0 + rows, :] = jax.nn.silu(_conv_fwd_tile(pad_ref, w_ref, b_ref, r0, rows, ML_CONV))

    return pl.pallas_call(
        body, name="ml_conv_fwd", grid=(nblk,),
        in_specs=[pl.BlockSpec((s, LANES), lambda j: (0, SEG_MQ + j)),
                  pl.BlockSpec((ML_CONV, LANES), lambda j: (0, j)),
                  pl.BlockSpec((1, LANES), lambda j: (0, j))],
        out_specs=pl.BlockSpec((s, LANES), lambda j: (0, j)),
        out_shape=jax.ShapeDtypeStruct((s, 2 * D_GROUP), F32),
        scratch_shapes=[pltpu.VMEM((s + SUBLANES, LANES), F32)],
        compiler_params=_params(("parallel",)),
    )(proj, conv_w, conv_b)


def _ml_conv_bwd(proj, conv_w, conv_b, d_qk, d_proj):
    s = proj.shape[0]
    nblk = 2 * D_GROUP // LANES

    def body(x_ref, w_ref, b_ref, dy_ref, _, dx_ref, dw_ref, db_ref, dxs_ref, pad_ref, dpad_ref):
        pad_ref[0:SUBLANES, :] = jnp.zeros((SUBLANES, LANES), F32)
        pad_ref[SUBLANES:, :] = x_ref[...]
        dpad_ref[s:, :] = jnp.zeros((SUBLANES, LANES), F32)
        db = jnp.zeros((1, LANES), F32)
        for r0 in range(0, s, ROW_TILE):
            rows = min(ROW_TILE, s - r0)
            pre = _conv_fwd_tile(pad_ref, w_ref, b_ref, r0, rows, ML_CONV)
            _, vjp = jax.vjp(jax.nn.silu, pre)
            d_pre, = vjp(dy_ref[r0:r0 + rows, :])
            dpad_ref[r0:r0 + rows, :] = d_pre
            db = db + jnp.sum(d_pre, axis=0, keepdims=True)
        db_ref[...] = db
        dws = [jnp.zeros((1, LANES), F32) for _ in range(ML_CONV)]
        dx_sum = jnp.zeros((1, LANES), F32)
        for r0 in range(0, s, ROW_TILE):
            dx_sum = dx_sum + _conv_grads_tile(pad_ref, dpad_ref, dx_ref, w_ref, dws, r0, min(ROW_TILE, s - r0),
                                               ML_CONV)
        dxs_ref[...] = dx_sum
        for j in range(ML_CONV):
            dw_ref[j:j + 1, :] = dws[j]

    return pl.pallas_call(
        body, name="ml_conv_bwd", grid=(nblk,),
        in_specs=[pl.BlockSpec((s, LANES), lambda j: (0, SEG_MQ + j)),
                  pl.BlockSpec((ML_CONV, LANES), lambda j: (0, j)),
                  pl.BlockSpec((1, LANES), lambda j: (0, j)),
                  pl.BlockSpec((s, LANES), lambda j: (0, j)),
                  pl.BlockSpec(memory_space=pl.ANY)],
        out_specs=[pl.BlockSpec((s, LANES), lambda j: (0, SEG_MQ + j)),
                   pl.BlockSpec((ML_CONV, LANES), lambda j: (0, j)),
                   pl.BlockSpec((1, LANES), lambda j: (0, j)),
                   pl.BlockSpec((1, LANES), lambda j: (0, j))],
        out_shape=[jax.ShapeDtypeStruct(d_proj.shape, d_proj.dtype),
                   jax.ShapeDtypeStruct((ML_CONV, 2 * D_GROUP), F32),
                   jax.ShapeDtypeStruct((1, 2 * D_GROUP), F32),
                   jax.ShapeDtypeStruct((1, 2 * D_GROUP), F32)],
        input_output_aliases={4: 0},
        scratch_shapes=[pltpu.VMEM((s + SUBLANES, LANES), F32), pltpu.VMEM((s + SUBLANES, LANES), F32)],
        compiler_params=_params(("parallel",)),
    )(proj, conv_w, conv_b, d_qk, d_proj)


def _gelu_mul(a, b):
    return jax.nn.gelu(a) * b


FFN_BLOCKS = D_FF_P // LANES


def _ffn_conv_fwd(u, conv_w, conv_b):
    s = u.shape[0]

    def body(g_ref, v_ref, wg_ref, wv_ref, bg_ref, bv_ref, o_ref, gpad_ref, vpad_ref):
        for pad_ref, x_ref in ((gpad_ref, g_ref), (vpad_ref, v_ref)):
            pad_ref[0:SUBLANES, :] = jnp.zeros((SUBLANES, LANES), F32)
            pad_ref[SUBLANES:, :] = x_ref[...]
        for r0 in range(0, s, ROW_TILE):
            rows = min(ROW_TILE, s - r0)
            ug = _conv_fwd_tile(gpad_ref, wg_ref, bg_ref, r0, rows, FFN_CONV)
            uv = _conv_fwd_tile(vpad_ref, wv_ref, bv_ref, r0, rows, FFN_CONV)
            o_ref[r0:r0 + rows, :] = _gelu_mul(ug, uv).astype(o_ref.dtype)

    col = lambda off: (lambda j: (0, off + j))
    return pl.pallas_call(
        body, name="ffn_conv_fwd", grid=(FFN_BLOCKS,),
        in_specs=[pl.BlockSpec((s, LANES), col(0)), pl.BlockSpec((s, LANES), col(FFN_BLOCKS)),
                  pl.BlockSpec((FFN_CONV, LANES), col(0)), pl.BlockSpec((FFN_CONV, LANES), col(FFN_BLOCKS)),
                  pl.BlockSpec((1, LANES), col(0)), pl.BlockSpec((1, LANES), col(FFN_BLOCKS))],
        out_specs=pl.BlockSpec((s, LANES), col(0)),
        out_shape=jax.ShapeDtypeStruct((s, D_FF_P), BF16),
        scratch_shapes=[pltpu.VMEM((s + SUBLANES, LANES), F32), pltpu.VMEM((s + SUBLANES, LANES), F32)],
        compiler_params=_params(("parallel",)),
    )(u, u, conv_w, conv_w, conv_b, conv_b)


def _ffn_conv_bwd(u, conv_w, conv_b, d_h):
    s = u.shape[0]

    def body(g_ref, v_ref, wg_ref, wv_ref, bg_ref, bv_ref, dh_ref,
             dug_ref, duv_ref, dwg_ref, dwv_ref, dbg_ref, dbv_ref,
             gpad_ref, vpad_ref, dgpad_ref, dvpad_ref):
        for pad_ref, x_ref in ((gpad_ref, g_ref), (vpad_ref, v_ref)):
            pad_ref[0:SUBLANES, :] = jnp.zeros((SUBLANES, LANES), F32)
            pad_ref[SUBLANES:, :] = x_ref[...]
        dgpad_ref[s:, :] = jnp.zeros((SUBLANES, LANES), F32)
        dvpad_ref[s:, :] = jnp.zeros((SUBLANES, LANES), F32)
        dbg = jnp.zeros((1, LANES), F32)
        dbv = jnp.zeros((1, LANES), F32)
        for r0 in range(0, s, ROW_TILE):
            rows = min(ROW_TILE, s - r0)
            ug = _conv_fwd_tile(gpad_ref, wg_ref, bg_ref, r0, rows, FFN_CONV)
            uv = _conv_fwd_tile(vpad_ref, wv_ref, bv_ref, r0, rows, FFN_CONV)
            _, vjp = jax.vjp(_gelu_mul, ug, uv)
            d_ug, d_uv = vjp(dh_ref[r0:r0 + rows, :])
            dgpad_ref[r0:r0 + rows, :] = d_ug
            dvpad_ref[r0:r0 + rows, :] = d_uv
            dbg = dbg + jnp.sum(d_ug, axis=0, keepdims=True)
            dbv = dbv + jnp.sum(d_uv, axis=0, keepdims=True)
        dbg_ref[...] = dbg
        dbv_ref[...] = dbv
        for pad_ref, dpad_ref, w_ref, dx_ref, dw_ref in ((gpad_ref, dgpad_ref, wg_ref, dug_ref, dwg_ref),
                                                         (vpad_ref, dvpad_ref, wv_ref, duv_ref, dwv_ref)):
            dws = [jnp.zeros((1, LANES), F32) for _ in range(FFN_CONV)]
            for r0 in range(0, s, ROW_TILE):
                _conv_grads_tile(pad_ref, dpad_ref, dx_ref, w_ref, dws, r0, min(ROW_TILE, s - r0), FFN_CONV)
            for j in range(FFN_CONV):
                dw_ref[j:j + 1, :] = dws[j]

    col = lambda off: (lambda j: (0, off + j))
    seq = pl.BlockSpec((s, LANES), col(0))
    return pl.pallas_call(
        body, name="ffn_conv_bwd", grid=(FFN_BLOCKS,),
        in_specs=[pl.BlockSpec((s, LANES), col(0)), pl.BlockSpec((s, LANES), col(FFN_BLOCKS)),
                  pl.BlockSpec((FFN_CONV, LANES), col(0)), pl.BlockSpec((FFN_CONV, LANES), col(FFN_BLOCKS)),
                  pl.BlockSpec((1, LANES), col(0)), pl.BlockSpec((1, LANES), col(FFN_BLOCKS)), seq],
        out_specs=[seq, seq, pl.BlockSpec((FFN_CONV, LANES), col(0)), pl.BlockSpec((FFN_CONV, LANES), col(0)),
                   pl.BlockSpec((1, LANES), col(0)), pl.BlockSpec((1, LANES), col(0))],
        out_shape=[jax.ShapeDtypeStruct((s, D_FF_P), BF16), jax.ShapeDtypeStruct((s, D_FF_P), BF16),
                   jax.ShapeDtypeStruct((FFN_CONV, D_FF_P), F32), jax.ShapeDtypeStruct((FFN_CONV, D_FF_P), F32),
                   jax.ShapeDtypeStruct((1, D_FF_P), F32), jax.ShapeDtypeStruct((1, D_FF_P), F32)],
        scratch_shapes=[pltpu.VMEM((s + SUBLANES, LANES), F32) for _ in range(4)],
        compiler_params=_params(("parallel",)),
    )(u, u, conv_w, conv_w, conv_b, conv_b, d_h)


def _chunk_masks(c):
    row = lax.broadcasted_iota(jnp.int32, (c, c), 0)
    col = lax.broadcasted_iota(jnp.int32, (c, c), 1)
    return row, col


@jax.custom_vjp
def _split_heads(x):
    return tuple(x[:, h * D_HEAD:(h + 1) * D_HEAD] for h in range(N_HEADS))


_split_heads.defvjp(lambda x: (_split_heads(x), None), lambda _, gs: (jnp.concatenate(gs, axis=1),))


@jax.custom_vjp
def _merge_heads(xs):
    return jnp.concatenate(xs, axis=1)


_merge_heads.defvjp(lambda xs: (_merge_heads(xs), None), lambda _, g: (_split_heads(g),))


@jax.custom_vjp
def _split_chunks(x):
    return tuple(x[i * CHUNK:(i + 1) * CHUNK] for i in range(x.shape[0] // CHUNK))


_split_chunks.defvjp(lambda x: (_split_chunks(x), None), lambda _, gs: (jnp.concatenate(gs, axis=0),))


@jax.custom_vjp
def _merge_chunks(xs):
    return jnp.concatenate(xs, axis=0)


_merge_chunks.defvjp(lambda xs: (_merge_chunks(xs), None), lambda _, g: (_split_chunks(g),))


def _blocks(x):
    return [_split_heads(rows) for rows in _split_chunks(x)]


def _per_chunk_rows(per_chunk, rid):
    out = per_chunk[0]
    for i in range(1, len(per_chunk)):
        out = jnp.where(rid >= i * CHUNK, per_chunk[i], out)
    return out


HEADS = range(N_HEADS)
CHUNKS_PER_STEP = 4
ML_CHUNKS_PER_STEP = 1


def _hg_chunk(hq, hf, hi, hgate, l0, l1, nw, sts):
    n = hq.shape[0] // CHUNK
    row, col = _chunk_masks(n * CHUNK)
    same_chunk = functools.reduce(jnp.logical_or, [(row >= i * CHUNK) & (row < (i + 1) * CHUNK) &
                                                   (col >= i * CHUNK) & (col < (i + 1) * CHUNK) for i in range(n)])
    causal = _chunk_masks(CHUNK)
    causal = causal[1] <= causal[0]
    mx = lax.stop_gradient(jnp.maximum(l0, l1))
    e0 = jnp.exp(l0 - mx)
    e1 = jnp.exp(l1 - mx)
    lb = e0 / (e0 + e1)
    sig = jax.nn.sigmoid(hf)
    lf = jnp.log(lb + (1.0 - lb) * sig)
    k = (1.0 - lb) * jax.nn.sigmoid(-hf)
    q = jax.nn.silu(hq)
    b = _dg(((col <= row) & same_chunk).astype(F32), lf, 1, 0, HIGHEST)
    rid = lax.broadcasted_iota(jnp.int32, b.shape, 0)
    pick = lambda r: jnp.sum(jnp.where(rid == r, b, 0.0), axis=0, keepdims=True)
    b_last_c = [pick(i * CHUNK + CHUNK - 1) for i in range(n)]
    b_ref = _per_chunk_rows([pick(i * CHUNK + CHUNK // 2 - 1) for i in range(n)], rid)
    b_last = _per_chunk_rows(b_last_c, rid)
    qa = _blocks(q * jnp.exp(b - b_ref))
    ka = _blocks(k * jnp.exp(b_ref - b))
    qe = _blocks(q * jnp.exp(b))
    kd = _blocks(k * jnp.exp(b_last - b))
    decay = [_split_heads(jnp.exp(b_last_c[i])) for i in range(n)]
    v = _blocks(hi)
    chunks = range(n)
    attn = [[jnp.where(causal, _nt(qa[i][h], ka[i][h]), 0.0) for h in HEADS] for i in chunks]
    intra = [[_nn(attn[i][h], v[i][h]) for h in HEADS] for i in chunks]
    kv = [[_tn(v[i][h], kd[i][h]) for h in HEADS] for i in chunks]
    normed = []
    for i in chunks:
        inter = [_nt(qe[i][h], sts[h]) for h in HEADS]
        sts = tuple(decay[i][h] * sts[h] + kv[i][h] for h in HEADS)
        o = [intra[i][h] + inter[h] for h in HEADS]
        normed.append(_merge_heads(tuple(o[h] * lax.rsqrt(jnp.mean(o[h] * o[h], axis=-1, keepdims=True) + LN_EPS)
                                         for h in HEADS)))
    return _merge_chunks(tuple(normed)) * nw * jax.nn.silu(hgate), sts


def _seg(ref, seg):
    return ref[:, seg * D_GROUP:(seg + 1) * D_GROUP]


def _hgrn2_fwd(proj, logits, norm_w):
    s = proj.shape[0]
    rows = CHUNKS_PER_STEP * CHUNK
    nc = s // rows

    def body(p_ref, lg_ref, nw_ref, y_ref, st_out_ref, st_scr):
        @pl.when(pl.program_id(0) == 0)
        def _():
            st_scr[...] = jnp.zeros_like(st_scr)

        sts = tuple(st_scr[h] for h in HEADS)
        y, sts_new = _hg_chunk(_seg(p_ref, 0), _seg(p_ref, 1), _seg(p_ref, 2), _seg(p_ref, 3),
                               lg_ref[0:1, :], lg_ref[1:2, :], nw_ref[...], sts)
        y_ref[...] = y.astype(y_ref.dtype)
        for h in HEADS:
            st_out_ref[h] = sts[h]
            st_scr[h] = sts_new[h]

    return pl.pallas_call(
        body, name="hgrn2_fwd", grid=(nc,),
        in_specs=[pl.BlockSpec((rows, 4 * D_GROUP), lambda c: (c, 0)),
                  pl.BlockSpec((2, D_GROUP), lambda c: (0, 0)),
                  pl.BlockSpec((1, D_GROUP), lambda c: (0, 0))],
        out_specs=[pl.BlockSpec((rows, D_GROUP), lambda c: (c, 0)),
                   pl.BlockSpec((None, N_HEADS, D_HEAD, D_HEAD), lambda c: (c, 0, 0, 0))],
        out_shape=[jax.ShapeDtypeStruct((s, 2 * D_GROUP), BF16),
                   jax.ShapeDtypeStruct((nc, N_HEADS, D_HEAD, D_HEAD), F32)],
        scratch_shapes=[pltpu.VMEM((N_HEADS, D_HEAD, D_HEAD), F32)],
        compiler_params=_params(("arbitrary",)),
    )(proj, logits, norm_w)


def _hgrn2_bwd(proj, logits, norm_w, states, d_y):
    s = proj.shape[0]
    rows = CHUNKS_PER_STEP * CHUNK
    nc = s // rows

    def body(p_ref, lg_ref, nw_ref, st_ref, dy_ref, dp_ref, dl_ref, dnw_ref, dsum_ref, dst_scr):
        @pl.when(pl.program_id(0) == 0)
        def _():
            dst_scr[...] = jnp.zeros_like(dst_scr)
            dl_ref[...] = jnp.zeros_like(dl_ref)
            dnw_ref[...] = jnp.zeros_like(dnw_ref)
            dsum_ref[...] = jnp.zeros_like(dsum_ref)

        _, vjp = jax.vjp(_hg_chunk, _seg(p_ref, 0), _seg(p_ref, 1), _seg(p_ref, 2), _seg(p_ref, 3),
                         lg_ref[0:1, :], lg_ref[1:2, :], nw_ref[...], tuple(st_ref[h] for h in HEADS))
        d_hq, d_hf, d_hi, d_hg, d_l0, d_l1, d_nw, d_sts = vjp((dy_ref[...], tuple(dst_scr[h] for h in HEADS)))
        for seg, val in enumerate((d_hq, d_hf, d_hi, d_hg)):
            dp_ref[:, seg * D_GROUP:(seg + 1) * D_GROUP] = val.astype(dp_ref.dtype)
            dsum_ref[:, seg * D_GROUP:(seg + 1) * D_GROUP] += jnp.sum(val, axis=0, keepdims=True)
        dl_ref[0:1, :] += d_l0
        dl_ref[1:2, :] += d_l1
        dnw_ref[...] += d_nw
        for h in HEADS:
            dst_scr[h] = d_sts[h]

    rev = lambda c: nc - 1 - c
    return pl.pallas_call(
        body, name="hgrn2_bwd", grid=(nc,),
        in_specs=[pl.BlockSpec((rows, 4 * D_GROUP), lambda c: (rev(c), 0)),
                  pl.BlockSpec((2, D_GROUP), lambda c: (0, 0)),
                  pl.BlockSpec((1, D_GROUP), lambda c: (0, 0)),
                  pl.BlockSpec((None, N_HEADS, D_HEAD, D_HEAD), lambda c: (rev(c), 0, 0, 0)),
                  pl.BlockSpec((rows, D_GROUP), lambda c: (rev(c), 0))],
        out_specs=[pl.BlockSpec((rows, 4 * D_GROUP), lambda c: (rev(c), 0)),
                   pl.BlockSpec((2, D_GROUP), lambda c: (0, 0)),
                   pl.BlockSpec((1, D_GROUP), lambda c: (0, 0)),
                   pl.BlockSpec((1, 4 * D_GROUP), lambda c: (0, 0))],
        out_shape=[jax.ShapeDtypeStruct((s, D_IN_MAIN), BF16), jax.ShapeDtypeStruct((2, D_GROUP), F32),
                   jax.ShapeDtypeStruct((1, D_GROUP), F32), jax.ShapeDtypeStruct((1, 4 * D_GROUP), F32)],
        scratch_shapes=[pltpu.VMEM((N_HEADS, D_HEAD, D_HEAD), F32)],
        compiler_params=_params(("arbitrary",)),
    )(proj, logits, norm_w, states, d_y)


def _gate_column(gates, lane, idx):
    return jnp.sum(jnp.where(lane == idx, gates, 0.0), axis=1, keepdims=True)


def _head_layer_norm(h):
    mu = jnp.mean(h, axis=-1, keepdims=True)
    var = jnp.mean(jnp.square(h - mu), axis=-1, keepdims=True)
    return (h - mu) * lax.rsqrt(var + LN_EPS)


def _ml_chunk(qc, kc, v, mo, gates, nw, cts, ns, ms):
    n = qc.shape[0] // CHUNK
    row, col = _chunk_masks(CHUNK)
    mask = col <= row
    eye = col == row
    to_row = lambda t: jnp.sum(jnp.where(eye, t, 0.0), axis=0, keepdims=True)
    q = _blocks(qc * (D_HEAD ** -0.5))
    k = _blocks(kc)
    vs = _blocks(v)
    gate_rows = _split_chunks(gates)
    lane = lax.broadcasted_iota(jnp.int32, gate_rows[0].shape, 1)
    each = [(i, h) for i in range(n) for h in HEADS]
    on_each = lambda f: {ih: f(*ih) for ih in each}
    ig = on_each(lambda i, h: _gate_column(gate_rows[i], lane, h))
    lf = on_each(lambda i, h: jax.nn.log_sigmoid(_gate_column(gate_rows[i], lane, N_HEADS + h)))
    lf_row = on_each(lambda i, h: to_row(lf[i, h]))
    ig_row = on_each(lambda i, h: to_row(ig[i, h]))
    b_col = on_each(lambda i, h: jnp.sum(jnp.where(mask, lf_row[i, h], 0.0), axis=1, keepdims=True))
    b_row = on_each(lambda i, h: jnp.sum(jnp.where(row <= col, lf[i, h], 0.0), axis=0, keepdims=True))
    g = on_each(lambda i, h: jnp.sum(lf[i, h], axis=0, keepdims=True))
    d = on_each(lambda i, h: jnp.where(mask, b_col[i, h] - b_row[i, h] + ig_row[i, h], -jnp.inf))
    a = on_each(lambda i, h: g[i, h] - b_col[i, h] + ig[i, h])
    m_at = {(0, h): ms[h] for h in HEADS}
    for i, h in each:
        m_at[i + 1, h] = lax.stop_gradient(jnp.maximum(g[i, h] + m_at[i, h], jnp.max(a[i, h], axis=0, keepdims=True)))
    inter = on_each(lambda i, h: b_col[i, h] + m_at[i, h])
    m_t = on_each(lambda i, h: lax.stop_gradient(jnp.maximum(inter[i, h], jnp.max(d[i, h], axis=1, keepdims=True))))
    qk = on_each(lambda i, h: _nt(q[i][h], k[i][h]))
    sc = on_each(lambda i, h: qk[i, h] * jnp.exp(d[i, h] - m_t[i, h]))
    w_inter = on_each(lambda i, h: jnp.exp(inter[i, h] - m_t[i, h]))
    sv = on_each(lambda i, h: _nn(sc[i, h], vs[i][h]))
    decay = on_each(lambda i, h: jnp.exp(g[i, h] + m_at[i, h] - m_at[i + 1, h]))
    wk = on_each(lambda i, h: k[i][h] * jnp.exp(a[i, h] - m_at[i + 1, h]))
    kv = on_each(lambda i, h: _tn(vs[i][h], wk[i, h]))
    normed = []
    for i in range(n):
        qc_state = [_nt(q[i][h], cts[h]) for h in HEADS]
        num = [sv[i, h] + w_inter[i, h] * qc_state[h] for h in HEADS]
        den = [jnp.sum(sc[i, h], axis=1, keepdims=True)
               + w_inter[i, h] * jnp.sum(q[i][h] * ns[h], axis=1, keepdims=True) for h in HEADS]
        hh = [num[h] / jnp.maximum(jnp.abs(den[h]), jnp.exp(-m_t[i, h])) for h in HEADS]
        cts = tuple(decay[i, h] * cts[h] + kv[i, h] for h in HEADS)
        ns = tuple(decay[i, h] * ns[h] + jnp.sum(wk[i, h], axis=0, keepdims=True) for h in HEADS)
        normed.append(_merge_heads(tuple(_head_layer_norm(hh[h]) for h in HEADS)))
    y = jax.nn.sigmoid(mo) * (_merge_chunks(tuple(normed)) * nw)
    return y, cts, ns, tuple(m_at[n, h] for h in HEADS)


def _mlstm_fwd(qk, proj, gates, norm_w, y):
    s = proj.shape[0]
    rows = ML_CHUNKS_PER_STEP * CHUNK
    nc = s // rows

    def body(qk_ref, vo_ref, g_ref, nw_ref, _, y_ref, ct_out, n_out, m_out, ct_scr, n_scr, m_scr):
        @pl.when(pl.program_id(0) == 0)
        def _():
            ct_scr[...] = jnp.zeros_like(ct_scr)
            n_scr[...] = jnp.zeros_like(n_scr)
            m_scr[...] = jnp.full(m_scr.shape, NEG_BIG, F32)

        cts = tuple(ct_scr[h] for h in HEADS)
        ns = tuple(n_scr[h] for h in HEADS)
        ms = tuple(m_scr[h] for h in HEADS)
        y, cts_new, ns_new, ms_new = _ml_chunk(_seg(qk_ref, 0), _seg(qk_ref, 1), _seg(vo_ref, 0), _seg(vo_ref, 1),
                                               g_ref[...], nw_ref[...], cts, ns, ms)
        y_ref[...] = y.astype(y_ref.dtype)
        for h in HEADS:
            ct_out[h], n_out[h], m_out[h] = cts[h], ns[h], ms[h]
            ct_scr[h], n_scr[h], m_scr[h] = cts_new[h], ns_new[h], ms_new[h]

    st = lambda r, w: pl.BlockSpec((None, N_HEADS, r, w), lambda c: (c, 0, 0, 0))
    return pl.pallas_call(
        body, name="mlstm_fwd", grid=(nc,),
        in_specs=[pl.BlockSpec((rows, 2 * D_GROUP), lambda c: (c, 0)),
                  pl.BlockSpec((rows, 2 * D_GROUP), lambda c: (c, 3)),
                  pl.BlockSpec((rows, LANES), lambda c: (c, 0)),
                  pl.BlockSpec((1, D_GROUP), lambda c: (0, 0)),
                  pl.BlockSpec(memory_space=pl.ANY)],
        out_specs=[pl.BlockSpec((rows, D_GROUP), lambda c: (c, 1)),
                   st(D_HEAD, D_HEAD), st(1, D_HEAD), st(1, 1)],
        out_shape=[jax.ShapeDtypeStruct(y.shape, y.dtype),
                   jax.ShapeDtypeStruct((nc, N_HEADS, D_HEAD, D_HEAD), F32),
                   jax.ShapeDtypeStruct((nc, N_HEADS, 1, D_HEAD), F32),
                   jax.ShapeDtypeStruct((nc, N_HEADS, 1, 1), F32)],
        input_output_aliases={4: 0},
        scratch_shapes=[pltpu.VMEM((N_HEADS, D_HEAD, D_HEAD), F32), pltpu.VMEM((N_HEADS, 1, D_HEAD), F32),
                        pltpu.VMEM((N_HEADS, 1, 1), F32)],
        compiler_params=_params(("arbitrary",)),
    )(qk, proj, gates, norm_w, y)


def _mlstm_bwd(qk, proj, gates, norm_w, ct_s, n_s, m_s, d_y, d_proj):
    s = proj.shape[0]
    rows = ML_CHUNKS_PER_STEP * CHUNK
    nc = s // rows

    def body(qk_ref, vo_ref, g_ref, nw_ref, ct_ref, n_ref, m_ref, dy_ref, _,
             dp_ref, dqk_ref, dg_ref, dnw_ref, dsum_ref, dct_scr, dn_scr):
        @pl.when(pl.program_id(0) == 0)
        def _():
            dct_scr[...] = jnp.zeros_like(dct_scr)
            dn_scr[...] = jnp.zeros_like(dn_scr)
            dnw_ref[...] = jnp.zeros_like(dnw_ref)
            dsum_ref[...] = jnp.zeros_like(dsum_ref)

        ms = tuple(m_ref[h] for h in HEADS)
        step = lambda *a: _ml_chunk(*a, ms)[:3]
        _, vjp = jax.vjp(step, _seg(qk_ref, 0), _seg(qk_ref, 1), _seg(vo_ref, 0), _seg(vo_ref, 1), g_ref[...],
                         nw_ref[...], tuple(ct_ref[h] for h in HEADS), tuple(n_ref[h] for h in HEADS))
        d_q, d_k, d_v, d_o, d_gates, d_nw, d_cts, d_ns = vjp(
            (dy_ref[...], tuple(dct_scr[h] for h in HEADS), tuple(dn_scr[h] for h in HEADS)))
        dqk_ref[:, 0:D_GROUP] = d_q
        dqk_ref[:, D_GROUP:2 * D_GROUP] = d_k
        for seg, val in enumerate((d_v, d_o)):
            dp_ref[:, seg * D_GROUP:(seg + 1) * D_GROUP] = val.astype(dp_ref.dtype)
            dsum_ref[:, seg * D_GROUP:(seg + 1) * D_GROUP] += jnp.sum(val, axis=0, keepdims=True)
        dg_ref[...] = d_gates
        dnw_ref[...] += d_nw
        for h in HEADS:
            dct_scr[h] = d_cts[h]
            dn_scr[h] = d_ns[h]

    rev = lambda c: nc - 1 - c
    st = lambda r, w: pl.BlockSpec((None, N_HEADS, r, w), lambda c: (rev(c), 0, 0, 0))
    return pl.pallas_call(
        body, name="mlstm_bwd", grid=(nc,),
        in_specs=[pl.BlockSpec((rows, 2 * D_GROUP), lambda c: (rev(c), 0)),
                  pl.BlockSpec((rows, 2 * D_GROUP), lambda c: (rev(c), 3)),
                  pl.BlockSpec((rows, LANES), lambda c: (rev(c), 0)),
                  pl.BlockSpec((1, D_GROUP), lambda c: (0, 0)),
                  st(D_HEAD, D_HEAD), st(1, D_HEAD), st(1, 1),
                  pl.BlockSpec((rows, D_GROUP), lambda c: (rev(c), 1)),
                  pl.BlockSpec(memory_space=pl.ANY)],
        out_specs=[pl.BlockSpec((rows, 2 * D_GROUP), lambda c: (rev(c), 3)),
                   pl.BlockSpec((rows, 2 * D_GROUP), lambda c: (rev(c), 0)),
                   pl.BlockSpec((rows, LANES), lambda c: (rev(c), 0)),
                   pl.BlockSpec((1, D_GROUP), lambda c: (0, 0)),
                   pl.BlockSpec((1, 2 * D_GROUP), lambda c: (0, 0))],
        out_shape=[jax.ShapeDtypeStruct(d_proj.shape, d_proj.dtype), jax.ShapeDtypeStruct((s, 2 * D_GROUP), F32),
                   jax.ShapeDtypeStruct((s, LANES), F32), jax.ShapeDtypeStruct((1, D_GROUP), F32),
                   jax.ShapeDtypeStruct((1, 2 * D_GROUP), F32)],
        input_output_aliases={8: 0},
        scratch_shapes=[pltpu.VMEM((N_HEADS, D_HEAD, D_HEAD), F32), pltpu.VMEM((N_HEADS, 1, D_HEAD), F32)],
        compiler_params=_params(("arbitrary",)),
    )(qk, proj, gates, norm_w, ct_s, n_s, m_s, d_y, d_proj)


LN_TOKENS = 512
ATT_TOKENS = 256


def _proj_res_ln(a, w, xres, g, b, name):
    s, dm = xres.shape
    k = a.shape[1]
    tb = min(LN_TOKENS, s)

    def body(a_ref, w_ref, x_ref, g_ref, b_ref, z_ref, o_ref):
        z = ALPHA * x_ref[...] + _nn_raw(a_ref[...], w_ref[...])
        z_ref[...] = z
        o_ref[...] = _layer_norm(z, g_ref[...], b_ref[...])

    tok = pl.BlockSpec((tb, dm), lambda i: (i, 0))
    vec = pl.BlockSpec((1, dm), lambda i: (0, 0))
    act = jax.ShapeDtypeStruct((s, dm), F32)
    return pl.pallas_call(
        body, name=name, grid=(s // tb,),
        in_specs=[pl.BlockSpec((tb, k), lambda i: (i, 0)), pl.BlockSpec((k, dm), lambda i: (0, 0)), tok, vec, vec],
        out_specs=[tok, tok], out_shape=[act, act], compiler_params=_params(("parallel",)),
    )(a, w, xres, g, b)


def _ln_bwd_proj(d_out, z, g, b, w, name):
    s, dm = z.shape
    k = w.shape[0]
    tb = min(LN_TOKENS, s)

    def body(do_ref, z_ref, g_ref, b_ref, w_ref, dz_ref, da_ref, dg_ref, db_ref):
        @pl.when(pl.program_id(0) == 0)
        def _():
            dg_ref[...] = jnp.zeros_like(dg_ref)
            db_ref[...] = jnp.zeros_like(db_ref)

        _, vjp = jax.vjp(_layer_norm, z_ref[...], g_ref[...], b_ref[...])
        d_z, d_g, d_b = vjp(do_ref[...])
        dz_ref[...] = d_z
        da_ref[...] = _nt_raw(d_z, w_ref[...])
        dg_ref[...] += d_g
        db_ref[...] += d_b

    tok = pl.BlockSpec((tb, dm), lambda i: (i, 0))
    vec = pl.BlockSpec((1, dm), lambda i: (0, 0))
    return pl.pallas_call(
        body, name=name, grid=(s // tb,),
        in_specs=[tok, tok, vec, vec, pl.BlockSpec((k, dm), lambda i: (0, 0))],
        out_specs=[tok, pl.BlockSpec((tb, k), lambda i: (i, 0)), vec, vec],
        out_shape=[jax.ShapeDtypeStruct((s, dm), F32), jax.ShapeDtypeStruct((s, k), F32),
                   jax.ShapeDtypeStruct((1, dm), F32), jax.ShapeDtypeStruct((1, dm), F32)],
        compiler_params=_params(("arbitrary",)),
    )(d_out, z, g, b, w)


def _proj_loss_tail(a, w, xres, g, b, target):
    s, dm = xres.shape
    k = a.shape[1]
    tb = min(ATT_TOKENS, s)

    def loss_fn(z, gg, bb, tgt):
        err = jnp.square(_layer_norm(z, gg, bb) - tgt)
        return 0.5 * jnp.sum(jnp.mean(err, axis=-1, keepdims=True), axis=0, keepdims=True)

    def body(a_ref, w_ref, x_ref, g_ref, b_ref, t_ref, loss_ref, dz_ref, dg_ref, db_ref):
        @pl.when(pl.program_id(0) == 0)
        def _():
            loss_ref[...] = jnp.zeros_like(loss_ref)
            dg_ref[...] = jnp.zeros_like(dg_ref)
            db_ref[...] = jnp.zeros_like(db_ref)

        z = ALPHA * x_ref[...] + _nn_raw(a_ref[...], w_ref[...])
        tgt = t_ref[...]
        loss, vjp = jax.vjp(lambda zz, gg, bb: loss_fn(zz, gg, bb, tgt), z, g_ref[...], b_ref[...])
        d_z, d_g, d_b = vjp(jnp.ones((1, 1), F32))
        loss_ref[...] += loss
        dz_ref[...] = d_z
        dg_ref[...] += d_g
        db_ref[...] += d_b

    tok = pl.BlockSpec((tb, dm), lambda i: (i, 0))
    vec = pl.BlockSpec((1, dm), lambda i: (0, 0))
    one = pl.BlockSpec((1, 1), lambda i: (0, 0))
    return pl.pallas_call(
        body, name="ffn_down_loss_tail", grid=(s // tb,),
        in_specs=[pl.BlockSpec((tb, k), lambda i: (i, 0)), pl.BlockSpec((k, dm), lambda i: (0, 0)), tok, vec, vec, tok],
        out_specs=[one, tok, vec, vec],
        out_shape=[jax.ShapeDtypeStruct((1, 1), F32), jax.ShapeDtypeStruct((s, dm), F32),
                   jax.ShapeDtypeStruct((1, dm), F32), jax.ShapeDtypeStruct((1, dm), F32)],
        compiler_params=_params(("arbitrary",)),
    )(a, w, xres, g, b, target)


def _att_head(q, k, v):
    sc = _nt(q, k) * (CA_DH ** -0.5)
    return _nn(jax.nn.softmax(sc, axis=-1), v)


def _cross_attention_fwd(x1, kv, wq, wo, g, b):
    s = x1.shape[0]
    tb = min(ATT_TOKENS, s)

    def body(x_ref, kv_ref, wq_ref, wo_ref, g_ref, b_ref, att_ref, z_ref, o_ref):
        x_blk = x_ref[...]
        q = _nn_raw(x_blk, wq_ref[...])
        heads = []
        for h in range(CA_HEADS):
            lo = h * CA_DH
            heads.append(_att_head(q[:, lo:lo + CA_DH], kv_ref[:, lo:lo + CA_DH],
                                   kv_ref[:, D_MODEL + lo:D_MODEL + lo + CA_DH]))
        att = jnp.concatenate(heads, axis=1)
        att_ref[...] = att.astype(att_ref.dtype)
        z = ALPHA * x_blk + _nn_raw(att, wo_ref[...])
        z_ref[...] = z
        o_ref[...] = _layer_norm(z, g_ref[...], b_ref[...])

    tok = pl.BlockSpec((tb, D_MODEL), lambda i: (i, 0))
    mat = pl.BlockSpec((D_MODEL, D_MODEL), lambda i: (0, 0))
    vec = pl.BlockSpec((1, D_MODEL), lambda i: (0, 0))
    act = jax.ShapeDtypeStruct((s, D_MODEL), F32)
    return pl.pallas_call(
        body, name="cross_attention_fwd", grid=(s // tb,),
        in_specs=[tok, pl.BlockSpec((N_MEM, 2 * D_MODEL), lambda i: (0, 0)), mat, mat, vec, vec],
        out_specs=[tok, tok, tok],
        out_shape=[jax.ShapeDtypeStruct((s, D_MODEL), BF16), act, act],
        compiler_params=_params(("parallel",)),
    )(x1, kv, wq, wo, g, b)


def _cross_attention_bwd(d_x2, x1, z2, kv, wq, wo, g, b):
    s = x1.shape[0]
    tb = min(ATT_TOKENS, s)

    def body(dx2_ref, x_ref, z_ref, kv_ref, wq_ref, wo_ref, g_ref, b_ref,
             dx1_ref, dq_ref, dz_ref, dkv_ref, dg_ref, db_ref):
        @pl.when(pl.program_id(0) == 0)
        def _():
            dkv_ref[...] = jnp.zeros_like(dkv_ref)
            dg_ref[...] = jnp.zeros_like(dg_ref)
            db_ref[...] = jnp.zeros_like(db_ref)

        _, ln_vjp = jax.vjp(_layer_norm, z_ref[...], g_ref[...], b_ref[...])
        d_z, d_g, d_b = ln_vjp(dx2_ref[...])
        dg_ref[...] += d_g
        db_ref[...] += d_b
        dz_ref[...] = d_z.astype(dz_ref.dtype)
        d_att = _nt_raw(d_z, wo_ref[...])
        q = _nn_raw(x_ref[...], wq_ref[...])
        d_q = []
        for h in range(CA_HEADS):
            lo = h * CA_DH
            vlo = D_MODEL + lo
            _, vjp = jax.vjp(_att_head, q[:, lo:lo + CA_DH], kv_ref[:, lo:lo + CA_DH], kv_ref[:, vlo:vlo + CA_DH])
            d_qh, d_k, d_v = vjp(d_att[:, lo:lo + CA_DH])
            d_q.append(d_qh)
            dkv_ref[:, lo:lo + CA_DH] += d_k
            dkv_ref[:, vlo:vlo + CA_DH] += d_v
        d_q = jnp.concatenate(d_q, axis=1)
        dq_ref[...] = d_q.astype(dq_ref.dtype)
        dx1_ref[...] = ALPHA * d_z + _nt_raw(d_q, wq_ref[...])

    tok = pl.BlockSpec((tb, D_MODEL), lambda i: (i, 0))
    mem = pl.BlockSpec((N_MEM, 2 * D_MODEL), lambda i: (0, 0))
    mat = pl.BlockSpec((D_MODEL, D_MODEL), lambda i: (0, 0))
    vec = pl.BlockSpec((1, D_MODEL), lambda i: (0, 0))
    low = jax.ShapeDtypeStruct((s, D_MODEL), BF16)
    return pl.pallas_call(
        body, name="cross_attention_bwd", grid=(s // tb,),
        in_specs=[tok, tok, tok, mem, mat, mat, vec, vec], out_specs=[tok, tok, tok, mem, vec, vec],
        out_shape=[jax.ShapeDtypeStruct((s, D_MODEL), F32), low, low,
                   jax.ShapeDtypeStruct((N_MEM, 2 * D_MODEL), F32),
                   jax.ShapeDtypeStruct((1, D_MODEL), F32), jax.ShapeDtypeStruct((1, D_MODEL), F32)],
        compiler_params=_params(("arbitrary",)),
    )(d_x2, x1, z2, kv, wq, wo, g, b)


def _local_step(x, mem, target, w, mid_weights=None, ffn_weights=None, on_ffn_grads=None, on_mid_grads=None,
                on_small_grads=None, on_last_grads=None):
    w = dict(w)
    s = x.shape[0]
    tm = min(512, s)
    tt = min(512, s)
    proj = _matmul_nn(x, w["w_in_main"], w["b_in_main"], min(2048, s), 512, "proj")
    gates = _matmul_nn(x, w["w_in_gate"], w["b_in_gate"], tm, LANES, "proj_gates")
    qk = _ml_conv_fwd(proj, w["ml_conv_w"], w["ml_conv_b"])
    y, hg_states = _hgrn2_fwd(proj, w["hg_lb_logits"], w["hg_norm_w"])
    y, ct_s, n_s, m_s = _mlstm_fwd(qk, proj, gates, w["ml_norm_w"], y)
    if mid_weights is not None:
        w.update(mid_weights(y))
    z1, x1 = _proj_res_ln(y, w["w_out"], x, w["ln1_g"], w["ln1_b"], "out_proj_ln1")
    kv = _matmul_nn(mem, w["ca_wkv"], None, N_MEM, CA_DH, "kv")
    att, z2, x2 = _cross_attention_fwd(x1, kv, w["ca_wq"], w["ca_wo"], w["ln2_g"], w["ln2_b"])
    if ffn_weights is not None:
        w.update(ffn_weights(x2))
    u = _matmul_nn(x2, w["ffn_w_up"], None, min(2048, s), UP_SHARD_P, "ffn_up")
    hid = _ffn_conv_fwd(u, w["ffn_conv_w"], w["ffn_conv_b"])
    loss, d_z3, d_ln3_g, d_ln3_b = _proj_loss_tail(hid, w["ffn_w_down"], x2, w["ln3_g"], w["ln3_b"], target)
    grads = {"ln3_g": d_ln3_g, "ln3_b": d_ln3_b}
    grads["ffn_w_down"] = _matmul_tn(hid, d_z3, 1536, D_MODEL, tt, "d_w_down")
    d_hid = _matmul_nt([(d_z3, w["ffn_w_down"])], None, 1.0, tm, D_FF_P, "d_hid")
    d_ug, d_uv, d_cwg, d_cwv, d_cbg, d_cbv = _ffn_conv_bwd(u, w["ffn_conv_w"], w["ffn_conv_b"], d_hid)
    grads["ffn_conv_w"] = jnp.concatenate([d_cwg, d_cwv], axis=-1)
    grads["ffn_conv_b"] = jnp.concatenate([d_cbg, d_cbv], axis=-1)
    half = N_DEV // 2
    d_w_up = _matmul_tn(x2, d_ug, D_MODEL, UP_SHARD_P, tt, "d_w_up_gate", shards=N_DEV, group=half)
    grads["ffn_w_up"] = _matmul_tn(x2, d_uv, D_MODEL, UP_SHARD_P, tt, "d_w_up_val", shards=N_DEV,
                                   shard0=half, group=half, into=d_w_up)
    d_x2 = _matmul_nt([(d_ug, w["ffn_w_up"], 0), (d_uv, w["ffn_w_up"], N_DEV // 2)], d_z3, ALPHA,
                      min(256, s), D_MODEL, "d_x2")
    if on_ffn_grads is not None:
        d_x2 = on_ffn_grads(grads, d_x2)
    d_x1, d_q, d_z2, d_kv, grads["ln2_g"], grads["ln2_b"] = _cross_attention_bwd(
        d_x2, x1, z2, kv, w["ca_wq"], w["ca_wo"], w["ln2_g"], w["ln2_b"])
    grads["ca_wo"] = _matmul_tn(att, d_z2, D_MODEL, D_MODEL, tt, "d_ca_wo")
    grads["ca_wq"] = _matmul_tn(x1, d_q, D_MODEL, D_MODEL, tt, "d_ca_wq")
    grads["ca_wkv"] = _matmul_tn(mem, d_kv, D_MODEL, CA_DH, N_MEM, "d_ca_wkv", shards=N_DEV, group=N_DEV)
    d_z1, d_y, grads["ln1_g"], grads["ln1_b"] = _ln_bwd_proj(d_x1, z1, w["ln1_g"], w["ln1_b"], w["w_out"],
                                                             "ln1_bwd_out_proj")
    grads["w_out"] = _matmul_tn(y, d_z1, D_MODEL, D_MODEL, tt, "d_w_out")
    if on_mid_grads is not None:
        d_y = on_mid_grads(grads, d_y)
    d_proj, grads["hg_lb_logits"], grads["hg_norm_w"], db_hg = _hgrn2_bwd(
        proj, w["hg_lb_logits"], w["hg_norm_w"], hg_states, d_y)
    d_proj, d_qk, d_gates, grads["ml_norm_w"], db_vo = _mlstm_bwd(
        qk, proj, gates, w["ml_norm_w"], ct_s, n_s, m_s, d_y, d_proj)
    d_proj, grads["ml_conv_w"], grads["ml_conv_b"], db_qk = _ml_conv_bwd(
        proj, w["ml_conv_w"], w["ml_conv_b"], d_qk, d_proj)
    grads["b_in_main"] = jnp.concatenate([db_hg, db_qk, db_vo], axis=-1)
    grads["w_in_gate"], grads["b_in_gate"] = _matmul_tn(x, d_gates, D_MODEL, LANES, tt, "d_w_in_gates", colsum=True)
    if on_small_grads is not None:
        d_proj = on_small_grads(grads, loss, d_proj)
    grads["w_in_main"] = _matmul_tn(x, d_proj, D_MODEL, min(2048, D_IN_MAIN), tt, "d_w_in")
    if on_last_grads is not None:
        d_z1 = on_last_grads(grads, d_z1)
    grad_x = _matmul_nt([(d_proj, w["w_in_main"]), (d_gates, w["w_in_gate"])], d_z1, ALPHA, tm, D_MODEL, "d_x")
    return loss, grad_x, grads


HBM_SPEC = pl.BlockSpec(memory_space=pltpu.HBM)


def _coords():
    return lax.axis_index("x"), lax.axis_index("y"), lax.axis_index("c")


def _other_chips(x, y):
    return [(1 - x, y), (x, 1 - y), (1 - x, 1 - y)]


def _all_gather_two_level(shards, name):
    na = len(shards)

    def body(*refs):
        x_refs, out_refs = refs[:na], refs[na:2 * na]
        send_sems, recv_sems, local_sems = refs[2 * na:]
        x, y, c = _coords()
        me, sibling = (x, y, c), (x, y, 1 - c)
        chips = _other_chips(x, y)

        def copy(a, k, block, to, own=False):
            slot = out_refs[a].at[4 * block[0] + 2 * block[1] + block[2]]
            return pltpu.make_async_remote_copy(
                src_ref=x_refs[a] if own else slot, dst_ref=slot,
                send_sem=send_sems.at[7 * a + k], recv_sem=recv_sems.at[7 * a + k],
                device_id=to, device_id_type=MESH)

        mine = [pltpu.make_async_copy(x_refs[a], out_refs[a].at[4 * x + 2 * y + c], local_sems.at[a])
                for a in range(na)]
        for cp in mine:
            cp.start()
        first = []
        for a in range(na):
            first.append(copy(a, 0, me, sibling, own=True))
            first += [copy(a, 1 + j, me, (*chip, c), own=True) for j, chip in enumerate(chips)]
        for cp in first:
            cp.start()
        passed = []
        for j, chip in enumerate(chips):
            for a in range(na):
                copy(a, 1 + j, (*chip, c), me).wait_recv()
                fwd = copy(a, 4 + j, (*chip, c), sibling)
                fwd.start()
                passed.append(fwd)
        for a in range(na):
            copy(a, 0, sibling, me).wait_recv()
            for j, chip in enumerate(chips):
                copy(a, 4 + j, (*chip, 1 - c), me).wait_recv()
        for cp in first + passed:
            cp.wait_send()
        for cp in mine:
            cp.wait()

    return pl.pallas_call(
        body, name=name,
        out_shape=[jax.ShapeDtypeStruct((N_DEV,) + t.shape, t.dtype) for t in shards],
        in_specs=[HBM_SPEC] * na, out_specs=[HBM_SPEC] * na,
        scratch_shapes=[pltpu.SemaphoreType.DMA((7 * na,)), pltpu.SemaphoreType.DMA((7 * na,)),
                        pltpu.SemaphoreType.DMA((na,))],
    )(*shards)


SEM_SPEC = pl.BlockSpec(memory_space=pltpu.SEMAPHORE)
ANY_SPEC = pl.BlockSpec(memory_space=pl.ANY)
SIDE_EFFECT = pltpu.SideEffectType.DATAFLOW_SIDE_EFFECTING


def _peer(x, y, c, d):
    flip = lambda v, bit: 1 - v if bit else v
    p = (flip(x, d & 4), flip(y, d & 2), flip(c, d & 1))
    return p, 4 * p[0] + 2 * p[1] + p[2]


def _direct_copies(gather, src_refs, land_refs, send_sems, recv_sems):
    x, y, c = _coords()
    me = 4 * x + 2 * y + c
    copies = []
    for a in range(len(src_refs)):
        for d in range(1, N_DEV):
            peer, peer_slot = _peer(x, y, c, d)
            copies.append(pltpu.make_async_remote_copy(
                src_ref=src_refs[a] if gather else src_refs[a].at[peer_slot],
                dst_ref=land_refs[a].at[me] if gather else land_refs[a].at[d - 1],
                send_sem=send_sems.at[7 * a + d - 1], recv_sem=recv_sems.at[7 * a + d - 1],
                device_id=peer, device_id_type=MESH))
    return copies


def _hbm(t):
    return pltpu.HBM(t.shape, t.dtype)


def _direct_start(gather, arrays, through, name):
    na = len(arrays)
    lands = [lax.empty((N_DEV,) + t.shape if gather else (N_DEV - 1,) + t.shape[1:], t.dtype) for t in arrays]
    n_io = 2 * na + 1

    def body(*refs):
        for cp in _direct_copies(gather, refs[:na], refs[na:2 * na], refs[n_io], refs[n_io + 1]):
            cp.start()

    ins = [pltpu.with_memory_space_constraint(t, pltpu.HBM) for t in (*arrays, *lands, through)]
    sems = pltpu.SemaphoreType.DMA((7 * na,))
    res = pl.pallas_call(
        body, name=name, out_shape=(sems, sems, *[_hbm(t) for t in ins]),
        in_specs=[HBM_SPEC] * n_io, out_specs=(SEM_SPEC, SEM_SPEC, *[HBM_SPEC] * n_io),
        input_output_aliases={i: 2 + i for i in range(n_io)},
        compiler_params=pltpu.CompilerParams(has_side_effects=SIDE_EFFECT),
    )(*ins)
    return (res[0], res[1], list(res[2:2 + na]), list(res[2 + na:2 + 2 * na])), res[2 + 2 * na]


def _direct_wait(gather, started, after, name):
    send_sems, recv_sems, arrays, lands = started
    na = len(arrays)

    def body(*refs):
        for cp in _direct_copies(gather, refs[:na], refs[na:2 * na], refs[2 * na], refs[2 * na + 1]):
            cp.wait_send()
            cp.wait_recv()

    res = pl.pallas_call(
        body, name=name, out_shape=tuple(_hbm(t) for t in (*arrays, *lands)),
        in_specs=[HBM_SPEC] * (2 * na) + [SEM_SPEC, SEM_SPEC, ANY_SPEC], out_specs=tuple([HBM_SPEC] * (2 * na)),
        input_output_aliases={i: i for i in range(2 * na)},
        compiler_params=pltpu.CompilerParams(has_side_effects=SIDE_EFFECT),
    )(*arrays, *lands, send_sems, recv_sems, after)
    return list(res[:na]), list(res[na:])


def _row_tile(rows):
    for t in (256, 176, 128):
        if rows % t == 0 and rows > t:
            return t
    return rows


def _adamw_math(g, w, m, v):
    m_new = ADAM_B1 * m + (1.0 - ADAM_B1) * g
    v_new = ADAM_B2 * v + (1.0 - ADAM_B2) * jnp.square(g)
    m_hat = m_new / (1.0 - ADAM_B1 ** ADAM_STEP)
    v_hat = v_new / (1.0 - ADAM_B2 ** ADAM_STEP)
    delta = -ADAM_LR * (m_hat / (jnp.sqrt(v_hat) + ADAM_EPS) + ADAM_WD * w)
    return delta, m_new, v_new


def _adamw_sharded(chip, sums, got, w, m, v, name):
    r, c = w.shape
    tr = _row_tile(r)
    n_got = got.shape[0]

    def body(chip_ref, s_ref, g_ref, w_ref, m_ref, v_ref, go_ref, d_ref, nm_ref, nv_ref):
        g = s_ref[...].astype(F32)
        for i in range(n_got):
            g = g + g_ref[i].astype(F32)
        delta, m_new, v_new = _adamw_math(g, w_ref[...], m_ref[...], v_ref[...])
        go_ref[...] = g
        d_ref[...] = delta
        nm_ref[...] = m_new
        nv_ref[...] = v_new

    blk = pl.BlockSpec((tr, c), lambda i, chip_ref: (i, 0))
    out = jax.ShapeDtypeStruct((r, c), F32)
    return pl.pallas_call(
        body, name=name,
        grid_spec=pltpu.PrefetchScalarGridSpec(
            num_scalar_prefetch=1, grid=(r // tr,),
            in_specs=[pl.BlockSpec((None, tr, c), lambda i, chip_ref: (chip_ref[0], i, 0)),
                      pl.BlockSpec((n_got, tr, c), lambda i, chip_ref: (0, i, 0)), blk, blk, blk],
            out_specs=[blk, blk, blk, blk]),
        out_shape=[out, out, out, out],
        compiler_params=_params(("parallel",)),
    )(chip, sums, got, w, m, v)


def _adamw_replicated(parts, w, m, v):
    p, r, c = parts.shape

    def body(p_ref, w_ref, m_ref, v_ref, g_ref, d_ref, nm_ref, nv_ref):
        g = p_ref[0]
        for i in range(1, p):
            g = g + p_ref[i]
        delta, m_new, v_new = _adamw_math(g, w_ref[...], m_ref[...], v_ref[...])
        g_ref[...] = g
        d_ref[...] = delta
        nm_ref[...] = m_new
        nv_ref[...] = v_new

    blk = pl.BlockSpec((r, c), lambda i: (0, 0))
    out = jax.ShapeDtypeStruct((r, c), F32)
    return pl.pallas_call(
        body, name="adamw_replicated", grid=(1,),
        in_specs=[pl.BlockSpec((p, r, c), lambda i: (0, 0, 0)), blk, blk, blk],
        out_specs=[blk, blk, blk, blk], out_shape=[out, out, out, out],
        compiler_params=_params(("arbitrary",)),
    )(parts, w, m, v)


SHARDED_NAMES = ("w_in", "ml_conv_w", "w_out", "ca_wq", "ca_wkv", "ca_wo", "ffn_w_up", "ffn_conv_w", "ffn_w_down")
SMALL_NAMES = ("b_in", "hg_lb_logits", "hg_norm_w", "ml_conv_b", "ml_norm_w", "ln1_g", "ln1_b",
               "ln2_g", "ln2_b", "ffn_conv_b", "ln3_g", "ln3_b")
WEIGHT_NAMES = ("w_in", "b_in", "hg_lb_logits", "hg_norm_w", "ml_conv_w", "ml_conv_b", "ml_norm_w", "w_out",
                "ln1_g", "ln1_b", "ca_wq", "ca_wkv", "ca_wo", "ln2_g", "ln2_b", "ffn_w_up", "ffn_conv_w",
                "ffn_conv_b", "ffn_w_down", "ln3_g", "ln3_b")
PAD_TO = {"w_in": W_IN_SHARD_P, "ffn_w_up": UP_SHARD_P, "ffn_conv_w": UP_SHARD_P}
SMALL_ROWS = 24
SMALL_W = D_MODEL


def _shard_2d(name, block):
    t = block[0]
    if name in PAD_TO:
        t = jnp.pad(t, ((0, 0), (0, PAD_TO[name] - t.shape[1])))
    return t


def _shard_like(name, t, like):
    return t[:, :like.shape[2]][None]


def _pad_cols(t, width):
    return jnp.pad(t, ((0, 0), (0, width - t.shape[1])))


FIRST_NAMES = ("w_in", "ml_conv_w")
FFN_NAMES = ("ffn_w_up", "ffn_w_down", "ffn_conv_w")
MID_NAMES = ("ca_wo", "ca_wq", "ca_wkv", "w_out")


def _first_weights(g, small):
    w = dict(small)
    w_in = jnp.concatenate([g["w_in"][j, :, :W_IN_SHARD] for j in range(N_DEV)], axis=1)
    w["w_in_main"] = w_in[:, :D_IN_MAIN]
    w["w_in_gate"] = _pad_cols(w_in[:, D_IN_MAIN:], LANES)
    w["b_in_main"] = small["b_in"][:, :D_IN_MAIN]
    w["b_in_gate"] = _pad_cols(small["b_in"][:, D_IN_MAIN:], LANES)
    w["ml_conv_w"] = jnp.transpose(g["ml_conv_w"], (1, 0, 2)).reshape(ML_CONV, 2 * D_GROUP)
    return w


def _mid_weights(g):
    w = {n: g[n].reshape(D_MODEL, D_MODEL) for n in ("w_out", "ca_wq", "ca_wo")}
    w["ca_wkv"] = g["ca_wkv"]
    return w


def _ffn_weights(g, small):
    w = {"ffn_w_up": g["ffn_w_up"]}
    down = g["ffn_w_down"].reshape(N_DEV // 2, UP_SHARD, D_MODEL)
    w["ffn_w_down"] = jnp.pad(down, ((0, 0), (0, UP_SHARD_P - UP_SHARD), (0, 0))).reshape(D_FF_P, D_MODEL)
    w["ffn_conv_w"] = jnp.transpose(g["ffn_conv_w"], (1, 0, 2)).reshape(FFN_CONV, D_UP_P)
    w["ffn_conv_b"] = _pad_cols(small["ffn_conv_b"].reshape(N_DEV, UP_SHARD), UP_SHARD_P).reshape(1, D_UP_P)
    return w


def _whole_weights(g, small):
    return {**_first_weights(g, small), **_mid_weights(g), **_ffn_weights(g, small)}


def _owner_stack(n, grads):
    if n == "w_in":
        w_in = jnp.concatenate([grads["w_in_main"], grads["w_in_gate"][:, :D_IN - D_IN_MAIN]], axis=1)
        return jnp.stack([_pad_cols(w_in[:, j * W_IN_SHARD:(j + 1) * W_IN_SHARD], W_IN_SHARD_P)
                          for j in range(N_DEV)])
    if n in ("w_out", "ca_wq", "ca_wo"):
        return grads[n].reshape(N_DEV, D_MODEL // N_DEV, D_MODEL)
    if n == "ffn_w_down":
        down = grads[n].reshape(N_DEV // 2, UP_SHARD_P, D_MODEL)[:, :UP_SHARD]
        return down.reshape(N_DEV, D_FF // N_DEV, D_MODEL)
    if n == "ml_conv_w":
        return jnp.transpose(grads[n].reshape(ML_CONV, N_DEV, LANES), (1, 0, 2))
    if n == "ffn_conv_w":
        return jnp.transpose(grads[n].reshape(FFN_CONV, N_DEV, UP_SHARD_P), (1, 0, 2))
    return grads[n]


def _owner_stacks(grads):
    return {n: _owner_stack(n, grads) for n in SHARDED_NAMES}


def _small_grads(grads):
    out = {n: grads[n] for n in SMALL_NAMES if n in grads}
    out["b_in"] = jnp.concatenate([grads["b_in_main"], grads["b_in_gate"][:, :D_IN - D_IN_MAIN]], axis=1)
    out["ffn_conv_b"] = grads["ffn_conv_b"].reshape(N_DEV, UP_SHARD_P)[:, :UP_SHARD].reshape(1, D_UP)
    return out


def _pack_small(p, extra=None):
    flat = [p[n].reshape(-1) for n in SMALL_NAMES]
    if extra is not None:
        flat.append(extra.reshape(-1))
    flat = jnp.concatenate(flat)
    return jnp.pad(flat, (0, SMALL_ROWS * SMALL_W - flat.shape[0])).reshape(SMALL_ROWS, SMALL_W)


def _unpack_small(slab, like):
    out = {}
    flat = slab.reshape(-1)
    o = 0
    for n in SMALL_NAMES:
        out[n] = flat[o:o + like[n].size].reshape(like[n].shape)
        o += like[n].size
    return out, flat[o]


def kernel(x, mem, w_in, b_in, hg_lb_logits, hg_norm_w, ml_conv_w, ml_conv_b, ml_norm_w, w_out, ln1_g, ln1_b, ca_wq, ca_wkv, ca_wo, ln2_g, ln2_b, ffn_w_up, ffn_conv_w, ffn_conv_b, ffn_w_down, ln3_g, ln3_b, loss_target, m_w_in, m_b_in, m_hg_lb_logits, m_hg_norm_w, m_ml_conv_w, m_ml_conv_b, m_ml_norm_w, m_w_out, m_ln1_g, m_ln1_b, m_ca_wq, m_ca_wkv, m_ca_wo, m_ln2_g, m_ln2_b, m_ffn_w_up, m_ffn_conv_w, m_ffn_conv_b, m_ffn_w_down, m_ln3_g, m_ln3_b, v_w_in, v_b_in, v_hg_lb_logits, v_hg_norm_w, v_ml_conv_w, v_ml_conv_b, v_ml_norm_w, v_w_out, v_ln1_g, v_ln1_b, v_ca_wq, v_ca_wkv, v_ca_wo, v_ln2_g, v_ln2_b, v_ffn_w_up, v_ffn_conv_w, v_ffn_conv_b, v_ffn_w_down, v_ln3_g, v_ln3_b):
    params = dict(w_in=w_in, b_in=b_in, hg_lb_logits=hg_lb_logits, hg_norm_w=hg_norm_w, ml_conv_w=ml_conv_w,
                  ml_conv_b=ml_conv_b, ml_norm_w=ml_norm_w, w_out=w_out, ln1_g=ln1_g, ln1_b=ln1_b, ca_wq=ca_wq,
                  ca_wkv=ca_wkv, ca_wo=ca_wo, ln2_g=ln2_g, ln2_b=ln2_b, ffn_w_up=ffn_w_up, ffn_conv_w=ffn_conv_w,
                  ffn_conv_b=ffn_conv_b, ffn_w_down=ffn_w_down, ln3_g=ln3_g, ln3_b=ln3_b)
    mom1 = dict(w_in=m_w_in, b_in=m_b_in, hg_lb_logits=m_hg_lb_logits, hg_norm_w=m_hg_norm_w,
                ml_conv_w=m_ml_conv_w, ml_conv_b=m_ml_conv_b, ml_norm_w=m_ml_norm_w, w_out=m_w_out, ln1_g=m_ln1_g,
                ln1_b=m_ln1_b, ca_wq=m_ca_wq, ca_wkv=m_ca_wkv, ca_wo=m_ca_wo, ln2_g=m_ln2_g, ln2_b=m_ln2_b,
                ffn_w_up=m_ffn_w_up, ffn_conv_w=m_ffn_conv_w, ffn_conv_b=m_ffn_conv_b, ffn_w_down=m_ffn_w_down,
                ln3_g=m_ln3_g, ln3_b=m_ln3_b)
    mom2 = dict(w_in=v_w_in, b_in=v_b_in, hg_lb_logits=v_hg_lb_logits, hg_norm_w=v_hg_norm_w,
                ml_conv_w=v_ml_conv_w, ml_conv_b=v_ml_conv_b, ml_norm_w=v_ml_norm_w, w_out=v_w_out, ln1_g=v_ln1_g,
                ln1_b=v_ln1_b, ca_wq=v_ca_wq, ca_wkv=v_ca_wkv, ca_wo=v_ca_wo, ln2_g=v_ln2_g, ln2_b=v_ln2_b,
                ffn_w_up=v_ffn_w_up, ffn_conv_w=v_ffn_conv_w, ffn_conv_b=v_ffn_conv_b, ffn_w_down=v_ffn_w_down,
                ln3_g=v_ln3_g, ln3_b=v_ln3_b)

    x_idx, y_idx, c_idx = _coords()
    as_index = lambda v: jnp.reshape(v, (1,)).astype(jnp.int32)
    me = as_index(4 * x_idx + 2 * y_idx + c_idx)
    small_params = {n: params[n] for n in SMALL_NAMES}

    shards = {n: _shard_2d(n, params[n]) for n in SHARDED_NAMES}
    to_send = lambda names: [shards[n] if "conv" in n else shards[n].astype(BF16) for n in names]
    first = dict(zip(FIRST_NAMES, _all_gather_two_level(to_send(FIRST_NAMES), "weights_gather_first")))
    mid_started, through = _direct_start(True, to_send(MID_NAMES), first["w_in"], "weights_gather_start_mid")
    ffn_started, first["w_in"] = _direct_start(True, to_send(FFN_NAMES), through, "weights_gather_start_ffn")

    def gathered_weights(names, started, after, tag):
        mine, lands = _direct_wait(True, started, after, "weights_gather_wait_" + tag)
        return {n: lax.dynamic_update_index_in_dim(land, own, me[0], 0) for n, own, land in zip(names, mine, lands)}

    started, own_stacks = {}, {}

    def start_group(names, tag):
        def hook(grads, through):
            own_stacks[tag] = [_owner_stack(n, grads).astype(BF16) for n in names]
            started[tag], through = _direct_start(False, own_stacks[tag], through, "grads_start_" + tag)
            return through
        return hook

    def start_small(grads, loss, through):
        started["small"], through = _direct_start(True, [_pack_small(_small_grads(grads), loss)], through,
                                                  "small_gather_start")
        return through

    loss, grad_x, grads = _local_step(
        x[0], mem[0], loss_target[0], _first_weights(first, small_params),
        lambda y: _mid_weights(gathered_weights(MID_NAMES, mid_started, y, "mid")),
        lambda x2: _ffn_weights(gathered_weights(FFN_NAMES, ffn_started, x2, "ffn"), small_params),
        start_group(FFN_NAMES, "ffn"), start_group(MID_NAMES, "mid"), start_small, start_group(FIRST_NAMES, "last"))

    sharded_out = {}
    after = grad_x
    for names, tag in ((FFN_NAMES, "ffn"), (MID_NAMES, "mid"), (FIRST_NAMES, "last")):
        _, lands = _direct_wait(False, started[tag], after, "grads_wait_" + tag)
        for n, st, land in zip(names, own_stacks[tag], lands):
            res = _adamw_sharded(me, st, land, shards[n], _shard_2d(n, mom1[n]), _shard_2d(n, mom2[n]), "adamw_" + n)
            sharded_out[n] = [_shard_like(n, t, params[n]) for t in res]
            after = res[0]
    own_small, small_lands = _direct_wait(True, started["small"], after, "small_gather_wait")
    small_parts = lax.dynamic_update_index_in_dim(small_lands[0], own_small[0], me[0], 0)
    small_res = _adamw_replicated(small_parts, _pack_small(params), _pack_small(mom1), _pack_small(mom2))

    outs = []
    total_loss = None
    for k in range(4):
        small, extra = _unpack_small(small_res[k], params)
        if total_loss is None:
            total_loss = extra
        outs.extend(sharded_out[n][k] if n in sharded_out else small[n] for n in WEIGHT_NAMES)
    return (total_loss, grad_x[None], *outs)
```

```python
import functools

import jax
import jax.numpy as jnp
from jax import lax
from jax.experimental import pallas as pl
from jax.experimental.pallas import tpu as pltpu

F32 = jnp.float32
BF16 = jnp.bfloat16
HIGHEST = lax.Precision.HIGHEST
MESH = pl.DeviceIdType.MESH

N_DEV = 8
D_MODEL = 1024
N_MEM = 256
N_HEADS = 4
D_HEAD = 128
D_GROUP = N_HEADS * D_HEAD
CHUNK = 64
ML_CONV = 4
FFN_CONV = 3
D_FF = 2816
D_UP = 2 * D_FF
CA_HEADS = 4
CA_DH = D_MODEL // CA_HEADS
LANES = 128
SUBLANES = 8
D_IN = 8 * D_GROUP + 2 * N_HEADS
D_IN_MAIN = 8 * D_GROUP
W_IN_SHARD = D_IN // N_DEV
W_IN_SHARD_P = 640
UP_SHARD = D_UP // N_DEV
UP_SHARD_P = 768
D_UP_P = N_DEV * UP_SHARD_P
D_FF_P = D_UP_P // 2
ALPHA = 2.0 ** 0.25
LN_EPS = 1e-5
NEG_BIG = -1e30
ADAM_LR = 0.001
ADAM_B1 = 0.9
ADAM_B2 = 0.999
ADAM_EPS = 1e-08
ADAM_WD = 0.01
ADAM_STEP = 10
VMEM_LIMIT = 56 * 1024 * 1024

SEG_HQ, SEG_HF, SEG_HI, SEG_HG, SEG_MQ, SEG_MK, SEG_MV, SEG_MO = (4 * i for i in range(8))


def _params(sem):
    return pltpu.CompilerParams(dimension_semantics=sem, vmem_limit_bytes=VMEM_LIMIT)


def _dg(a, b, ca, cb, precision=None):
    return lax.dot_general(a, b, (((ca,), (cb,)), ((), ())), precision=precision,
                           preferred_element_type=F32)


def _nn_raw(a, b):
    return _dg(a.astype(BF16), b.astype(BF16), 1, 0)


def _nt_raw(a, b):
    return _dg(a.astype(BF16), b.astype(BF16), 1, 1)


def _tn_raw(a, b):
    return _dg(a.astype(BF16), b.astype(BF16), 0, 0)


@jax.custom_vjp
def _nn(a, b):
    return _nn_raw(a, b)


_nn.defvjp(lambda a, b: (_nn_raw(a, b), (a, b)),
           lambda res, g: (_nt_raw(g, res[1]), _tn_raw(res[0], g)))


@jax.custom_vjp
def _nt(a, b):
    return _nt_raw(a, b)


_nt.defvjp(lambda a, b: (_nt_raw(a, b), (a, b)),
           lambda res, g: (_nn_raw(g, res[1]), _tn_raw(g, res[0])))


@jax.custom_vjp
def _tn(a, b):
    return _tn_raw(a, b)


_tn.defvjp(lambda a, b: (_tn_raw(a, b), (a, b)),
           lambda res, g: (_nt_raw(res[1], g), _nn_raw(res[0], g)))


def _layer_norm(z, g, b):
    mu = jnp.mean(z, axis=-1, keepdims=True)
    var = jnp.mean(jnp.square(z - mu), axis=-1, keepdims=True)
    return (z - mu) * lax.rsqrt(var + LN_EPS) * g + b


def _matmul_nn(a, w, bias, tm, tn, name, out_dtype=F32):
    m, k = a.shape
    if w.ndim == 3:
        n = w.shape[0] * w.shape[2]
        assert tn == w.shape[2]
        w_spec = pl.BlockSpec((None, k, tn), lambda i, j: (j, 0, 0))
    else:
        n = w.shape[1]
        w_spec = pl.BlockSpec((k, tn), lambda i, j: (0, j))

    def body(*refs):
        a_ref, w_ref = refs[0], refs[1]
        o_ref = refs[-1]
        acc = _nn_raw(a_ref[...], w_ref[...])
        if bias is not None:
            acc = acc + refs[2][...]
        o_ref[...] = acc.astype(o_ref.dtype)

    in_specs = [pl.BlockSpec((tm, k), lambda i, j: (i, 0)), w_spec]
    args = [a, w]
    if bias is not None:
        in_specs.append(pl.BlockSpec((1, tn), lambda i, j: (0, j)))
        args.append(bias)
    return pl.pallas_call(
        body, name=name, grid=(m // tm, n // tn), in_specs=in_specs,
        out_specs=pl.BlockSpec((tm, tn), lambda i, j: (i, j)),
        out_shape=jax.ShapeDtypeStruct((m, n), out_dtype),
        compiler_params=_params(("parallel", "parallel")),
    )(*args)


def _matmul_nt(pairs, add, scale, tm, tk, name, out_dtype=F32):
    m = pairs[0][0].shape[0]
    k = pairs[0][1].shape[-2]
    groups = []
    in_specs, args = [], []
    for pair in pairs:
        d, w = pair[0], pair[1]
        in_specs.append(pl.BlockSpec((tm, d.shape[1]), lambda i, j: (i, 0)))
        if w.ndim == 3:
            g = d.shape[1] // w.shape[2]
            blk = pair[2] // g
            in_specs.append(pl.BlockSpec((g, tk, w.shape[2]), lambda i, j, blk=blk: (blk, j, 0)))
            groups.append((g, w.shape[2]))
        else:
            in_specs.append(pl.BlockSpec((tk, w.shape[1]), lambda i, j: (j, 0)))
            groups.append(None)
        args += [d, w]
    if add is not None:
        in_specs.append(pl.BlockSpec((tm, tk), lambda i, j: (i, j)))
        args.append(add)

    def body(*refs):
        o_ref = refs[-1]
        acc = None
        for p, grp in enumerate(groups):
            d_ref, w_ref = refs[2 * p], refs[2 * p + 1]
            if grp is None:
                terms = [_nt_raw(d_ref[...], w_ref[...])]
            else:
                terms = [_nt_raw(d_ref[:, g * grp[1]:(g + 1) * grp[1]], w_ref[g]) for g in range(grp[0])]
            for t in terms:
                acc = t if acc is None else acc + t
        if add is not None:
            acc = acc + scale * refs[2 * len(groups)][...]
        o_ref[...] = acc.astype(o_ref.dtype)

    return pl.pallas_call(
        body, name=name, grid=(m // tm, k // tk), in_specs=in_specs,
        out_specs=pl.BlockSpec((tm, tk), lambda i, j: (i, j)),
        out_shape=jax.ShapeDtypeStruct((m, k), out_dtype),
        compiler_params=_params(("parallel", "parallel")),
    )(*args)


def _matmul_tn(a, b, tm, tn, tt, name, shards=None, shard0=0, group=1, into=None, colsum=False):
    t, m = a.shape
    n = b.shape[1]
    assert not colsum or tm == m
    n_in = 2 + (into is not None)
    out_dtype = BF16
    per_step = 1 if shards is None else group
    width = per_step * tn

    def body(*refs):
        a_ref, b_ref = refs[0], refs[1]
        o_ref, acc_ref = refs[n_in], refs[-1]
        first = pl.program_id(2) == 0

        @pl.when(first)
        def _():
            acc_ref[...] = jnp.zeros_like(acc_ref)

        if shards is None:
            acc_ref[...] += _tn_raw(a_ref[...], b_ref[...])
        else:
            lhs = a_ref[...].astype(BF16)
            for g in range(per_step):
                acc_ref[g] += _tn_raw(lhs, b_ref[:, g * tn:(g + 1) * tn])

        @pl.when(pl.program_id(2) == t // tt - 1)
        def _():
            o_ref[...] = acc_ref[...].astype(o_ref.dtype)

        if colsum:
            s_ref = refs[n_in + 1]

            @pl.when(first)
            def _():
                s_ref[...] = jnp.zeros_like(s_ref)

            s_ref[...] += jnp.sum(b_ref[...], axis=0, keepdims=True)

    in_specs = [pl.BlockSpec((tt, tm), lambda i, j, kk: (kk, i)),
                pl.BlockSpec((tt, width), lambda i, j, kk: (kk, j))]
    args = [a, b]
    aliases = {}
    if into is not None:
        in_specs.append(pl.BlockSpec(memory_space=pl.ANY))
        args.append(into)
        aliases = {2: 0}
    if shards is None:
        out_specs = [pl.BlockSpec((tm, tn), lambda i, j, kk: (i, j))]
        out_shape = [jax.ShapeDtypeStruct((m, n), out_dtype)]
        acc = pltpu.VMEM((tm, tn), F32)
    else:
        out_specs = [pl.BlockSpec((per_step, tm, tn), lambda i, j, kk: (shard0 // per_step + j, i, 0))]
        out_shape = [jax.ShapeDtypeStruct((shards, m, tn), out_dtype)]
        acc = pltpu.VMEM((per_step, tm, tn), F32)
    if colsum:
        out_specs.append(pl.BlockSpec((1, tn), lambda i, j, kk: (0, j)))
        out_shape.append(jax.ShapeDtypeStruct((1, n), F32))
    res = pl.pallas_call(
        body, name=name, grid=(m // tm, n // width, t // tt), in_specs=in_specs, out_specs=out_specs,
        out_shape=out_shape, input_output_aliases=aliases, scratch_shapes=[acc],
        compiler_params=_params(("parallel", "parallel", "arbitrary")),
    )(*args)
    return res if colsum else res[0]


ROW_TILE = 64


def _conv_fwd_tile(pad_ref, w_ref, b_ref, r0, rows, taps):
    acc = b_ref[...]
    for j in range(taps):
        acc = acc + pad_ref[pl.ds(SUBLANES - (taps - 1 - j) + r0, rows), :] * w_ref[j:j + 1, :]
    return acc


def _conv_bwd_tile(dpad_ref, w_ref, r0, rows, taps):
    acc = None
    for j in range(taps):
        term = dpad_ref[pl.ds(r0 + (taps - 1 - j), rows), :] * w_ref[j:j + 1, :]
        acc = term if acc is None else acc + term
    return acc


def _conv_grads_tile(pad_ref, dpad_ref, dx_ref, w_ref, dws, r0, rows, taps):
    dx = _conv_bwd_tile(dpad_ref, w_ref, r0, rows, taps)
    dx_ref[r0:r0 + rows, :] = dx.astype(dx_ref.dtype)
    d_pre = dpad_ref[r0:r0 + rows, :]
    for j in range(taps):
        xs = pad_ref[pl.ds(SUBLANES - (taps - 1 - j) + r0, rows), :]
        dws[j] = dws[j] + jnp.sum(d_pre * xs, axis=0, keepdims=True)
    return jnp.sum(dx, axis=0, keepdims=True)


def _ml_conv_fwd(proj, conv_w, conv_b):
    s = proj.shape[0]
    nblk = 2 * D_GROUP // LANES

    def body(x_ref, w_ref, b_ref, o_ref, pad_ref):
        pad_ref[0:SUBLANES, :] = jnp.zeros((SUBLANES, LANES), F32)
        pad_ref[SUBLANES:, :] = x_ref[...].astype(F32)
        for r0 in range(0, s, ROW_TILE):
            rows = min(ROW_TILE, s - r0)
            o_ref[r0:r0 + rows, :] = jax.nn.silu(_conv_fwd_tile(pad_ref, w_ref, b_ref, r0, rows, ML_CONV))

    return pl.pallas_call(
        body, name="ml_conv_fwd", grid=(nblk,),
        in_specs=[pl.BlockSpec((s, LANES), lambda j: (0, SEG_MQ + j)),
                  pl.BlockSpec((ML_CONV, LANES), lambda j: (0, j)),
                  pl.BlockSpec((1, LANES), lambda j: (0, j))],
        out_specs=pl.BlockSpec((s, LANES), lambda j: (0, j)),
        out_shape=jax.ShapeDtypeStruct((s, 2 * D_GROUP), F32),
        scratch_shapes=[pltpu.VMEM((s + SUBLANES, LANES), F32)],
        compiler_params=_params(("parallel",)),
    )(proj, conv_w, conv_b)


def _ml_conv_bwd(proj, conv_w, conv_b, d_qk, d_proj):
    s = proj.shape[0]
    nblk = 2 * D_GROUP // LANES

    def body(x_ref, w_ref, b_ref, dy_ref, _, dx_ref, dw_ref, db_ref, dxs_ref, pad_ref, dpad_ref):
        pad_ref[0:SUBLANES, :] = jnp.zeros((SUBLANES, LANES), F32)
        pad_ref[SUBLANES:, :] = x_ref[...].astype(F32)
        dpad_ref[s:, :] = jnp.zeros((SUBLANES, LANES), F32)
        db = jnp.zeros((1, LANES), F32)
        for r0 in range(0, s, ROW_TILE):
            rows = min(ROW_TILE, s - r0)
            pre = _conv_fwd_tile(pad_ref, w_ref, b_ref, r0, rows, ML_CONV)
            _, vjp = jax.vjp(jax.nn.silu, pre)
            d_pre, = vjp(dy_ref[r0:r0 + rows, :])
            dpad_ref[r0:r0 + rows, :] = d_pre
            db = db + jnp.sum(d_pre, axis=0, keepdims=True)
        db_ref[...] = db
        dws = [jnp.zeros((1, LANES), F32) for _ in range(ML_CONV)]
        dx_sum = jnp.zeros((1, LANES), F32)
        for r0 in range(0, s, ROW_TILE):
            dx_sum = dx_sum + _conv_grads_tile(pad_ref, dpad_ref, dx_ref, w_ref, dws, r0, min(ROW_TILE, s - r0),
                                               ML_CONV)
        dxs_ref[...] = dx_sum
        for j in range(ML_CONV):
            dw_ref[j:j + 1, :] = dws[j]

    return pl.pallas_call(
        body, name="ml_conv_bwd", grid=(nblk,),
        in_specs=[pl.BlockSpec((s, LANES), lambda j: (0, SEG_MQ + j)),
                  pl.BlockSpec((ML_CONV, LANES), lambda j: (0, j)),
                  pl.BlockSpec((1, LANES), lambda j: (0, j)),
                  pl.BlockSpec((s, LANES), lambda j: (0, j)),
                  pl.BlockSpec(memory_space=pl.ANY)],
        out_specs=[pl.BlockSpec((s, LANES), lambda j: (0, SEG_MQ + j)),
                   pl.BlockSpec((ML_CONV, LANES), lambda j: (0, j)),
                   pl.BlockSpec((1, LANES), lambda j: (0, j)),
                   pl.BlockSpec((1, LANES), lambda j: (0, j))],
        out_shape=[jax.ShapeDtypeStruct(d_proj.shape, d_proj.dtype),
                   jax.ShapeDtypeStruct((ML_CONV, 2 * D_GROUP), F32),
                   jax.ShapeDtypeStruct((1, 2 * D_GROUP), F32),
                   jax.ShapeDtypeStruct((1, 2 * D_GROUP), F32)],
        input_output_aliases={4: 0},
        scratch_shapes=[pltpu.VMEM((s + SUBLANES, LANES), F32), pltpu.VMEM((s + SUBLANES, LANES), F32)],
        compiler_params=_params(("parallel",)),
    )(proj, conv_w, conv_b, d_qk, d_proj)


def _gelu_mul(a, b):
    return jax.nn.gelu(a) * b


FFN_BLOCKS = D_FF_P // LANES


def _ffn_conv_fwd(u, conv_w, conv_b):
    s = u.shape[0]

    def body(g_ref, v_ref, wg_ref, wv_ref, bg_ref, bv_ref, o_ref, gpad_ref, vpad_ref):
        for pad_ref, x_ref in ((gpad_ref, g_ref), (vpad_ref, v_ref)):
            pad_ref[0:SUBLANES, :] = jnp.zeros((SUBLANES, LANES), F32)
            pad_ref[SUBLANES:, :] = x_ref[...].astype(F32)
        for r0 in range(0, s, ROW_TILE):
            rows = min(ROW_TILE, s - r0)
            ug = _conv_fwd_tile(gpad_ref, wg_ref, bg_ref, r0, rows, FFN_CONV)
            uv = _conv_fwd_tile(vpad_ref, wv_ref, bv_ref, r0, rows, FFN_CONV)
            o_ref[r0:r0 + rows, :] = _gelu_mul(ug, uv).astype(o_ref.dtype)

    col = lambda off: (lambda j: (0, off + j))
    return pl.pallas_call(
        body, name="ffn_conv_fwd", grid=(FFN_BLOCKS,),
        in_specs=[pl.BlockSpec((s, LANES), col(0)), pl.BlockSpec((s, LANES), col(FFN_BLOCKS)),
                  pl.BlockSpec((FFN_CONV, LANES), col(0)), pl.BlockSpec((FFN_CONV, LANES), col(FFN_BLOCKS)),
                  pl.BlockSpec((1, LANES), col(0)), pl.BlockSpec((1, LANES), col(FFN_BLOCKS))],
        out_specs=pl.BlockSpec((s, LANES), col(0)),
        out_shape=jax.ShapeDtypeStruct((s, D_FF_P), BF16),
        scratch_shapes=[pltpu.VMEM((s + SUBLANES, LANES), F32), pltpu.VMEM((s + SUBLANES, LANES), F32)],
        compiler_params=_params(("parallel",)),
    )(u, u, conv_w, conv_w, conv_b, conv_b)


def _ffn_conv_bwd(u, conv_w, conv_b, d_h):
    s = u.shape[0]

    def body(g_ref, v_ref, wg_ref, wv_ref, bg_ref, bv_ref, dh_ref,
             dug_ref, duv_ref, dwg_ref, dwv_ref, dbg_ref, dbv_ref,
             gpad_ref, vpad_ref, dgpad_ref, dvpad_ref):
        for pad_ref, x_ref in ((gpad_ref, g_ref), (vpad_ref, v_ref)):
            pad_ref[0:SUBLANES, :] = jnp.zeros((SUBLANES, LANES), F32)
            pad_ref[SUBLANES:, :] = x_ref[...].astype(F32)
        dgpad_ref[s:, :] = jnp.zeros((SUBLANES, LANES), F32)
        dvpad_ref[s:, :] = jnp.zeros((SUBLANES, LANES), F32)
        dbg = jnp.zeros((1, LANES), F32)
        dbv = jnp.zeros((1, LANES), F32)
        for r0 in range(0, s, ROW_TILE):
            rows = min(ROW_TILE, s - r0)
            ug = _conv_fwd_tile(gpad_ref, wg_ref, bg_ref, r0, rows, FFN_CONV)
            uv = _conv_fwd_tile(vpad_ref, wv_ref, bv_ref, r0, rows, FFN_CONV)
            _, vjp = jax.vjp(_gelu_mul, ug, uv)
            d_ug, d_uv = vjp(dh_ref[r0:r0 + rows, :].astype(F32))
            dgpad_ref[r0:r0 + rows, :] = d_ug
            dvpad_ref[r0:r0 + rows, :] = d_uv
            dbg = dbg + jnp.sum(d_ug, axis=0, keepdims=True)
            dbv = dbv + jnp.sum(d_uv, axis=0, keepdims=True)
        dbg_ref[...] = dbg
        dbv_ref[...] = dbv
        for pad_ref, dpad_ref, w_ref, dx_ref, dw_ref in ((gpad_ref, dgpad_ref, wg_ref, dug_ref, dwg_ref),
                                                         (vpad_ref, dvpad_ref, wv_ref, duv_ref, dwv_ref)):
            dws = [jnp.zeros((1, LANES), F32) for _ in range(FFN_CONV)]
            for r0 in range(0, s, ROW_TILE):
                _conv_grads_tile(pad_ref, dpad_ref, dx_ref, w_ref, dws, r0, min(ROW_TILE, s - r0), FFN_CONV)
            for j in range(FFN_CONV):
                dw_ref[j:j + 1, :] = dws[j]

    col = lambda off: (lambda j: (0, off + j))
    seq = pl.BlockSpec((s, LANES), col(0))
    return pl.pallas_call(
        body, name="ffn_conv_bwd", grid=(FFN_BLOCKS,),
        in_specs=[pl.BlockSpec((s, LANES), col(0)), pl.BlockSpec((s, LANES), col(FFN_BLOCKS)),
                  pl.BlockSpec((FFN_CONV, LANES), col(0)), pl.BlockSpec((FFN_CONV, LANES), col(FFN_BLOCKS)),
                  pl.BlockSpec((1, LANES), col(0)), pl.BlockSpec((1, LANES), col(FFN_BLOCKS)), seq],
        out_specs=[seq, seq, pl.BlockSpec((FFN_CONV, LANES), col(0)), pl.BlockSpec((FFN_CONV, LANES), col(0)),
                   pl.BlockSpec((1, LANES), col(0)), pl.BlockSpec((1, LANES), col(0))],
        out_shape=[jax.ShapeDtypeStruct((s, D_FF_P), BF16), jax.ShapeDtypeStruct((s, D_FF_P), BF16),
                   jax.ShapeDtypeStruct((FFN_CONV, D_FF_P), F32), jax.ShapeDtypeStruct((FFN_CONV, D_FF_P), F32),
                   jax.ShapeDtypeStruct((1, D_FF_P), F32), jax.ShapeDtypeStruct((1, D_FF_P), F32)],
        scratch_shapes=[pltpu.VMEM((s + SUBLANES, LANES), F32) for _ in range(4)],
        compiler_params=_params(("parallel",)),
    )(u, u, conv_w, conv_w, conv_b, conv_b, d_h)


def _chunk_masks(c):
    row = lax.broadcasted_iota(jnp.int32, (c, c), 0)
    col = lax.broadcasted_iota(jnp.int32, (c, c), 1)
    return row, col


@jax.custom_vjp
def _split_heads(x):
    return tuple(x[:, h * D_HEAD:(h + 1) * D_HEAD] for h in range(N_HEADS))


_split_heads.defvjp(lambda x: (_split_heads(x), None), lambda _, gs: (jnp.concatenate(gs, axis=1),))


@jax.custom_vjp
def _merge_heads(xs):
    return jnp.concatenate(xs, axis=1)


_merge_heads.defvjp(lambda xs: (_merge_heads(xs), None), lambda _, g: (_split_heads(g),))


@jax.custom_vjp
def _split_chunks(x):
    return tuple(x[i * CHUNK:(i + 1) * CHUNK] for i in range(x.shape[0] // CHUNK))


_split_chunks.defvjp(lambda x: (_split_chunks(x), None), lambda _, gs: (jnp.concatenate(gs, axis=0),))


@jax.custom_vjp
def _merge_chunks(xs):
    return jnp.concatenate(xs, axis=0)


_merge_chunks.defvjp(lambda xs: (_merge_chunks(xs), None), lambda _, g: (_split_chunks(g),))


def _blocks(x):
    return [_split_heads(rows) for rows in _split_chunks(x)]


def _per_chunk_rows(per_chunk, rid):
    out = per_chunk[0]
    for i in range(1, len(per_chunk)):
        out = jnp.where(rid >= i * CHUNK, per_chunk[i], out)
    return out


HEADS = range(N_HEADS)
CHUNKS_PER_STEP = 4
ML_CHUNKS_PER_STEP = 1


def _hg_chunk(hq, hf, hi, hgate, l0, l1, nw, sts):
    n = hq.shape[0] // CHUNK
    row, col = _chunk_masks(n * CHUNK)
    same_chunk = functools.reduce(jnp.logical_or, [(row >= i * CHUNK) & (row < (i + 1) * CHUNK) &
                                                   (col >= i * CHUNK) & (col < (i + 1) * CHUNK) for i in range(n)])
    causal = _chunk_masks(CHUNK)
    causal = causal[1] <= causal[0]
    mx = lax.stop_gradient(jnp.maximum(l0, l1))
    e0 = jnp.exp(l0 - mx)
    e1 = jnp.exp(l1 - mx)
    lb = e0 / (e0 + e1)
    sig = jax.nn.sigmoid(hf)
    lf = jnp.log(lb + (1.0 - lb) * sig)
    k = (1.0 - lb) * jax.nn.sigmoid(-hf)
    q = jax.nn.silu(hq)
    b = _dg(((col <= row) & same_chunk).astype(F32), lf, 1, 0, HIGHEST)
    rid = lax.broadcasted_iota(jnp.int32, b.shape, 0)
    pick = lambda r: jnp.sum(jnp.where(rid == r, b, 0.0), axis=0, keepdims=True)
    b_last_c = [pick(i * CHUNK + CHUNK - 1) for i in range(n)]
    b_ref = _per_chunk_rows([pick(i * CHUNK + CHUNK // 2 - 1) for i in range(n)], rid)
    b_last = _per_chunk_rows(b_last_c, rid)
    qa = _blocks(q * jnp.exp(b - b_ref))
    ka = _blocks(k * jnp.exp(b_ref - b))
    qe = _blocks(q * jnp.exp(b))
    kd = _blocks(k * jnp.exp(b_last - b))
    decay = [_split_heads(jnp.exp(b_last_c[i])) for i in range(n)]
    v = _blocks(hi)
    chunks = range(n)
    attn = [[jnp.where(causal, _nt(qa[i][h], ka[i][h]), 0.0) for h in HEADS] for i in chunks]
    intra = [[_nn(attn[i][h], v[i][h]) for h in HEADS] for i in chunks]
    kv = [[_tn(v[i][h], kd[i][h]) for h in HEADS] for i in chunks]
    normed = []
    for i in chunks:
        inter = [_nt(qe[i][h], sts[h]) for h in HEADS]
        sts = tuple(decay[i][h] * sts[h] + kv[i][h] for h in HEADS)
        o = [intra[i][h] + inter[h] for h in HEADS]
        normed.append(_merge_heads(tuple(o[h] * lax.rsqrt(jnp.mean(o[h] * o[h], axis=-1, keepdims=True) + LN_EPS)
                                         for h in HEADS)))
    return _merge_chunks(tuple(normed)) * nw * jax.nn.silu(hgate), sts


def _seg(ref, seg):
    return ref[:, seg * D_GROUP:(seg + 1) * D_GROUP]


def _hgrn2_fwd(proj, logits, norm_w):
    s = proj.shape[0]
    rows = CHUNKS_PER_STEP * CHUNK
    nc = s // rows

    def body(p_ref, lg_ref, nw_ref, y_ref, st_out_ref, st_scr):
        @pl.when(pl.program_id(0) == 0)
        def _():
            st_scr[...] = jnp.zeros_like(st_scr)

        sts = tuple(st_scr[h] for h in HEADS)
        y, sts_new = _hg_chunk(_seg(p_ref, 0), _seg(p_ref, 1), _seg(p_ref, 2), _seg(p_ref, 3),
                               lg_ref[0:1, :], lg_ref[1:2, :], nw_ref[...], sts)
        y_ref[...] = y.astype(y_ref.dtype)
        for h in HEADS:
            st_out_ref[h] = sts[h]
            st_scr[h] = sts_new[h]

    return pl.pallas_call(
        body, name="hgrn2_fwd", grid=(nc,),
        in_specs=[pl.BlockSpec((rows, 4 * D_GROUP), lambda c: (c, 0)),
                  pl.BlockSpec((2, D_GROUP), lambda c: (0, 0)),
                  pl.BlockSpec((1, D_GROUP), lambda c: (0, 0))],
        out_specs=[pl.BlockSpec((rows, D_GROUP), lambda c: (c, 0)),
                   pl.BlockSpec((None, N_HEADS, D_HEAD, D_HEAD), lambda c: (c, 0, 0, 0))],
        out_shape=[jax.ShapeDtypeStruct((s, 2 * D_GROUP), BF16),
                   jax.ShapeDtypeStruct((nc, N_HEADS, D_HEAD, D_HEAD), F32)],
        scratch_shapes=[pltpu.VMEM((N_HEADS, D_HEAD, D_HEAD), F32)],
        compiler_params=_params(("arbitrary",)),
    )(proj, logits, norm_w)


def _hgrn2_bwd(proj, logits, norm_w, states, d_y):
    s = proj.shape[0]
    rows = CHUNKS_PER_STEP * CHUNK
    nc = s // rows

    def body(p_ref, lg_ref, nw_ref, st_ref, dy_ref, dp_ref, dl_ref, dnw_ref, dsum_ref, dst_scr):
        @pl.when(pl.program_id(0) == 0)
        def _():
            dst_scr[...] = jnp.zeros_like(dst_scr)
            dl_ref[...] = jnp.zeros_like(dl_ref)
            dnw_ref[...] = jnp.zeros_like(dnw_ref)
            dsum_ref[...] = jnp.zeros_like(dsum_ref)

        _, vjp = jax.vjp(_hg_chunk, _seg(p_ref, 0), _seg(p_ref, 1), _seg(p_ref, 2), _seg(p_ref, 3),
                         lg_ref[0:1, :], lg_ref[1:2, :], nw_ref[...], tuple(st_ref[h] for h in HEADS))
        d_hq, d_hf, d_hi, d_hg, d_l0, d_l1, d_nw, d_sts = vjp((dy_ref[...], tuple(dst_scr[h] for h in HEADS)))
        for seg, val in enumerate((d_hq, d_hf, d_hi, d_hg)):
            dp_ref[:, seg * D_GROUP:(seg + 1) * D_GROUP] = val.astype(dp_ref.dtype)
            dsum_ref[:, seg * D_GROUP:(seg + 1) * D_GROUP] += jnp.sum(val, axis=0, keepdims=True)
        dl_ref[0:1, :] += d_l0
        dl_ref[1:2, :] += d_l1
        dnw_ref[...] += d_nw
        for h in HEADS:
            dst_scr[h] = d_sts[h]

    rev = lambda c: nc - 1 - c
    return pl.pallas_call(
        body, name="hgrn2_bwd", grid=(nc,),
        in_specs=[pl.BlockSpec((rows, 4 * D_GROUP), lambda c: (rev(c), 0)),
                  pl.BlockSpec((2, D_GROUP), lambda c: (0, 0)),
                  pl.BlockSpec((1, D_GROUP), lambda c: (0, 0)),
                  pl.BlockSpec((None, N_HEADS, D_HEAD, D_HEAD), lambda c: (rev(c), 0, 0, 0)),
                  pl.BlockSpec((rows, D_GROUP), lambda c: (rev(c), 0))],
        out_specs=[pl.BlockSpec((rows, 4 * D_GROUP), lambda c: (rev(c), 0)),
                   pl.BlockSpec((2, D_GROUP), lambda c: (0, 0)),
                   pl.BlockSpec((1, D_GROUP), lambda c: (0, 0)),
                   pl.BlockSpec((1, 4 * D_GROUP), lambda c: (0, 0))],
        out_shape=[jax.ShapeDtypeStruct((s, D_IN_MAIN), BF16), jax.ShapeDtypeStruct((2, D_GROUP), F32),
                   jax.ShapeDtypeStruct((1, D_GROUP), F32), jax.ShapeDtypeStruct((1, 4 * D_GROUP), F32)],
        scratch_shapes=[pltpu.VMEM((N_HEADS, D_HEAD, D_HEAD), F32)],
        compiler_params=_params(("arbitrary",)),
    )(proj, logits, norm_w, states, d_y)


def _gate_column(gates, lane, idx):
    return jnp.sum(jnp.where(lane == idx, gates, 0.0), axis=1, keepdims=True)


def _head_layer_norm(h):
    mu = jnp.mean(h, axis=-1, keepdims=True)
    var = jnp.mean(jnp.square(h - mu), axis=-1, keepdims=True)
    return (h - mu) * lax.rsqrt(var + LN_EPS)


def _ml_chunk(qc, kc, v, mo, gates, nw, cts, ns, ms):
    n = qc.shape[0] // CHUNK
    row, col = _chunk_masks(CHUNK)
    mask = col <= row
    eye = col == row
    to_row = lambda t: jnp.sum(jnp.where(eye, t, 0.0), axis=0, keepdims=True)
    q = _blocks(qc * (D_HEAD ** -0.5))
    k = _blocks(kc)
    vs = _blocks(v)
    gate_rows = _split_chunks(gates)
    lane = lax.broadcasted_iota(jnp.int32, gate_rows[0].shape, 1)
    each = [(i, h) for i in range(n) for h in HEADS]
    on_each = lambda f: {ih: f(*ih) for ih in each}
    ig = on_each(lambda i, h: _gate_column(gate_rows[i], lane, h))
    lf = on_each(lambda i, h: jax.nn.log_sigmoid(_gate_column(gate_rows[i], lane, N_HEADS + h)))
    lf_row = on_each(lambda i, h: to_row(lf[i, h]))
    ig_row = on_each(lambda i, h: to_row(ig[i, h]))
    b_col = on_each(lambda i, h: jnp.sum(jnp.where(mask, lf_row[i, h], 0.0), axis=1, keepdims=True))
    b_row = on_each(lambda i, h: jnp.sum(jnp.where(row <= col, lf[i, h], 0.0), axis=0, keepdims=True))
    g = on_each(lambda i, h: jnp.sum(lf[i, h], axis=0, keepdims=True))
    d = on_each(lambda i, h: jnp.where(mask, b_col[i, h] - b_row[i, h] + ig_row[i, h], -jnp.inf))
    a = on_each(lambda i, h: g[i, h] - b_col[i, h] + ig[i, h])
    m_at = {(0, h): ms[h] for h in HEADS}
    for i, h in each:
        m_at[i + 1, h] = lax.stop_gradient(jnp.maximum(g[i, h] + m_at[i, h], jnp.max(a[i, h], axis=0, keepdims=True)))
    inter = on_each(lambda i, h: b_col[i, h] + m_at[i, h])
    m_t = on_each(lambda i, h: lax.stop_gradient(jnp.maximum(inter[i, h], jnp.max(d[i, h], axis=1, keepdims=True))))
    qk = on_each(lambda i, h: _nt(q[i][h], k[i][h]))
    sc = on_each(lambda i, h: qk[i, h] * jnp.exp(d[i, h] - m_t[i, h]))
    w_inter = on_each(lambda i, h: jnp.exp(inter[i, h] - m_t[i, h]))
    sv = on_each(lambda i, h: _nn(sc[i, h], vs[i][h]))
    decay = on_each(lambda i, h: jnp.exp(g[i, h] + m_at[i, h] - m_at[i + 1, h]))
    wk = on_each(lambda i, h: k[i][h] * jnp.exp(a[i, h] - m_at[i + 1, h]))
    kv = on_each(lambda i, h: _tn(vs[i][h], wk[i, h]))
    normed = []
    for i in range(n):
        qc_state = [_nt(q[i][h], cts[h]) for h in HEADS]
        num = [sv[i, h] + w_inter[i, h] * qc_state[h] for h in HEADS]
        den = [jnp.sum(sc[i, h], axis=1, keepdims=True)
               + w_inter[i, h] * jnp.sum(q[i][h] * ns[h], axis=1, keepdims=True) for h in HEADS]
        hh = [num[h] / jnp.maximum(jnp.abs(den[h]), jnp.exp(-m_t[i, h])) for h in HEADS]
        cts = tuple(decay[i, h] * cts[h] + kv[i, h] for h in HEADS)
        ns = tuple(decay[i, h] * ns[h] + jnp.sum(wk[i, h], axis=0, keepdims=True) for h in HEADS)
        normed.append(_merge_heads(tuple(_head_layer_norm(hh[h]) for h in HEADS)))
    y = jax.nn.sigmoid(mo) * (_merge_chunks(tuple(normed)) * nw)
    return y, cts, ns, tuple(m_at[n, h] for h in HEADS)


def _mlstm_fwd(qk, proj, gates, norm_w, y):
    s = proj.shape[0]
    rows = ML_CHUNKS_PER_STEP * CHUNK
    nc = s // rows

    def body(qk_ref, vo_ref, g_ref, nw_ref, _, y_ref, ct_out, n_out, m_out, ct_scr, n_scr, m_scr):
        @pl.when(pl.program_id(0) == 0)
        def _():
            ct_scr[...] = jnp.zeros_like(ct_scr)
            n_scr[...] = jnp.zeros_like(n_scr)
            m_scr[...] = jnp.full(m_scr.shape, NEG_BIG, F32)

        cts = tuple(ct_scr[h] for h in HEADS)
        ns = tuple(n_scr[h] for h in HEADS)
        ms = tuple(m_scr[h] for h in HEADS)
        y, cts_new, ns_new, ms_new = _ml_chunk(_seg(qk_ref, 0), _seg(qk_ref, 1), _seg(vo_ref, 0), _seg(vo_ref, 1),
                                               g_ref[...], nw_ref[...], cts, ns, ms)
        y_ref[...] = y.astype(y_ref.dtype)
        for h in HEADS:
            ct_out[h], n_out[h], m_out[h] = cts[h], ns[h], ms[h]
            ct_scr[h], n_scr[h], m_scr[h] = cts_new[h], ns_new[h], ms_new[h]

    st = lambda r, w: pl.BlockSpec((None, N_HEADS, r, w), lambda c: (c, 0, 0, 0))
    return pl.pallas_call(
        body, name="mlstm_fwd", grid=(nc,),
        in_specs=[pl.BlockSpec((rows, 2 * D_GROUP), lambda c: (c, 0)),
                  pl.BlockSpec((rows, 2 * D_GROUP), lambda c: (c, 3)),
                  pl.BlockSpec((rows, LANES), lambda c: (c, 0)),
                  pl.BlockSpec((1, D_GROUP), lambda c: (0, 0)),
                  pl.BlockSpec(memory_space=pl.ANY)],
        out_specs=[pl.BlockSpec((rows, D_GROUP), lambda c: (c, 1)),
                   st(D_HEAD, D_HEAD), st(1, D_HEAD), st(1, 1)],
        out_shape=[jax.ShapeDtypeStruct(y.shape, y.dtype),
                   jax.ShapeDtypeStruct((nc, N_HEADS, D_HEAD, D_HEAD), F32),
                   jax.ShapeDtypeStruct((nc, N_HEADS, 1, D_HEAD), F32),
                   jax.ShapeDtypeStruct((nc, N_HEADS, 1, 1), F32)],
        input_output_aliases={4: 0},
        scratch_shapes=[pltpu.VMEM((N_HEADS, D_HEAD, D_HEAD), F32), pltpu.VMEM((N_HEADS, 1, D_HEAD), F32),
                        pltpu.VMEM((N_HEADS, 1, 1), F32)],
        compiler_params=_params(("arbitrary",)),
    )(qk, proj, gates, norm_w, y)


def _mlstm_bwd(qk, proj, gates, norm_w, ct_s, n_s, m_s, d_y, d_proj):
    s = proj.shape[0]
    rows = ML_CHUNKS_PER_STEP * CHUNK
    nc = s // rows

    def body(qk_ref, vo_ref, g_ref, nw_ref, ct_ref, n_ref, m_ref, dy_ref, _,
             dp_ref, dqk_ref, dg_ref, dnw_ref, dsum_ref, dct_scr, dn_scr):
        @pl.when(pl.program_id(0) == 0)
        def _():
            dct_scr[...] = jnp.zeros_like(dct_scr)
            dn_scr[...] = jnp.zeros_like(dn_scr)
            dnw_ref[...] = jnp.zeros_like(dnw_ref)
            dsum_ref[...] = jnp.zeros_like(dsum_ref)

        ms = tuple(m_ref[h] for h in HEADS)
        step = lambda *a: _ml_chunk(*a, ms)[:3]
        _, vjp = jax.vjp(step, _seg(qk_ref, 0), _seg(qk_ref, 1), _seg(vo_ref, 0), _seg(vo_ref, 1), g_ref[...],
                         nw_ref[...], tuple(ct_ref[h] for h in HEADS), tuple(n_ref[h] for h in HEADS))
        d_q, d_k, d_v, d_o, d_gates, d_nw, d_cts, d_ns = vjp(
            (dy_ref[...], tuple(dct_scr[h] for h in HEADS), tuple(dn_scr[h] for h in HEADS)))
        dqk_ref[:, 0:D_GROUP] = d_q
        dqk_ref[:, D_GROUP:2 * D_GROUP] = d_k
        for seg, val in enumerate((d_v, d_o)):
            dp_ref[:, seg * D_GROUP:(seg + 1) * D_GROUP] = val.astype(dp_ref.dtype)
            dsum_ref[:, seg * D_GROUP:(seg + 1) * D_GROUP] += jnp.sum(val, axis=0, keepdims=True)
        dg_ref[...] = d_gates
        dnw_ref[...] += d_nw
        for h in HEADS:
            dct_scr[h] = d_cts[h]
            dn_scr[h] = d_ns[h]

    rev = lambda c: nc - 1 - c
    st = lambda r, w: pl.BlockSpec((None, N_HEADS, r, w), lambda c: (rev(c), 0, 0, 0))
    return pl.pallas_call(
        body, name="mlstm_bwd", grid=(nc,),
        in_specs=[pl.BlockSpec((rows, 2 * D_GROUP), lambda c: (rev(c), 0)),
                  pl.BlockSpec((rows, 2 * D_GROUP), lambda c: (rev(c), 3)),
                  pl.BlockSpec((rows, LANES), lambda c: (rev(c), 0)),
                  pl.BlockSpec((1, D_GROUP), lambda c: (0, 0)),
                  st(D_HEAD, D_HEAD), st(1, D_HEAD), st(1, 1),
                  pl.BlockSpec((rows, D_GROUP), lambda c: (rev(c), 1)),
                  pl.BlockSpec(memory_space=pl.ANY)],
        out_specs=[pl.BlockSpec((rows, 2 * D_GROUP), lambda c: (rev(c), 3)),
                   pl.BlockSpec((rows, 2 * D_GROUP), lambda c: (rev(c), 0)),
                   pl.BlockSpec((rows, LANES), lambda c: (rev(c), 0)),
                   pl.BlockSpec((1, D_GROUP), lambda c: (0, 0)),
                   pl.BlockSpec((1, 2 * D_GROUP), lambda c: (0, 0))],
        out_shape=[jax.ShapeDtypeStruct(d_proj.shape, d_proj.dtype), jax.ShapeDtypeStruct((s, 2 * D_GROUP), F32),
                   jax.ShapeDtypeStruct((s, LANES), F32), jax.ShapeDtypeStruct((1, D_GROUP), F32),
                   jax.ShapeDtypeStruct((1, 2 * D_GROUP), F32)],
        input_output_aliases={8: 0},
        scratch_shapes=[pltpu.VMEM((N_HEADS, D_HEAD, D_HEAD), F32), pltpu.VMEM((N_HEADS, 1, D_HEAD), F32)],
        compiler_params=_params(("arbitrary",)),
    )(qk, proj, gates, norm_w, ct_s, n_s, m_s, d_y, d_proj)


LN_TOKENS = 512
ATT_TOKENS = 256


def _proj_res_ln(a, w, xres, g, b, name):
    s, dm = xres.shape
    k = a.shape[1]
    tb = min(LN_TOKENS, s)

    def body(a_ref, w_ref, x_ref, g_ref, b_ref, z_ref, o_ref):
        z = ALPHA * x_ref[...] + _nn_raw(a_ref[...], w_ref[...])
        z_ref[...] = z
        o_ref[...] = _layer_norm(z, g_ref[...], b_ref[...])

    tok = pl.BlockSpec((tb, dm), lambda i: (i, 0))
    vec = pl.BlockSpec((1, dm), lambda i: (0, 0))
    act = jax.ShapeDtypeStruct((s, dm), F32)
    return pl.pallas_call(
        body, name=name, grid=(s // tb,),
        in_specs=[pl.BlockSpec((tb, k), lambda i: (i, 0)), pl.BlockSpec((k, dm), lambda i: (0, 0)), tok, vec, vec],
        out_specs=[tok, tok], out_shape=[act, act], compiler_params=_params(("parallel",)),
    )(a, w, xres, g, b)


def _ln_bwd_proj(d_out, z, g, b, w, name):
    s, dm = z.shape
    k = w.shape[0]
    tb = min(LN_TOKENS, s)

    def body(do_ref, z_ref, g_ref, b_ref, w_ref, dz_ref, da_ref, dg_ref, db_ref):
        @pl.when(pl.program_id(0) == 0)
        def _():
            dg_ref[...] = jnp.zeros_like(dg_ref)
            db_ref[...] = jnp.zeros_like(db_ref)

        _, vjp = jax.vjp(_layer_norm, z_ref[...], g_ref[...], b_ref[...])
        d_z, d_g, d_b = vjp(do_ref[...])
        dz_ref[...] = d_z
        da_ref[...] = _nt_raw(d_z, w_ref[...])
        dg_ref[...] += d_g
        db_ref[...] += d_b

    tok = pl.BlockSpec((tb, dm), lambda i: (i, 0))
    vec = pl.BlockSpec((1, dm), lambda i: (0, 0))
    return pl.pallas_call(
        body, name=name, grid=(s // tb,),
        in_specs=[tok, tok, vec, vec, pl.BlockSpec((k, dm), lambda i: (0, 0))],
        out_specs=[tok, pl.BlockSpec((tb, k), lambda i: (i, 0)), vec, vec],
        out_shape=[jax.ShapeDtypeStruct((s, dm), F32), jax.ShapeDtypeStruct((s, k), F32),
                   jax.ShapeDtypeStruct((1, dm), F32), jax.ShapeDtypeStruct((1, dm), F32)],
        compiler_params=_params(("arbitrary",)),
    )(d_out, z, g, b, w)


def _proj_loss_tail(a, w, xres, g, b, target):
    s, dm = xres.shape
    k = a.shape[1]
    tb = min(ATT_TOKENS, s)

    def loss_fn(z, gg, bb, tgt):
        err = jnp.square(_layer_norm(z, gg, bb) - tgt)
        return 0.5 * jnp.sum(jnp.mean(err, axis=-1, keepdims=True), axis=0, keepdims=True)

    def body(a_ref, w_ref, x_ref, g_ref, b_ref, t_ref, loss_ref, dz_ref, dg_ref, db_ref):
        @pl.when(pl.program_id(0) == 0)
        def _():
            loss_ref[...] = jnp.zeros_like(loss_ref)
            dg_ref[...] = jnp.zeros_like(dg_ref)
            db_ref[...] = jnp.zeros_like(db_ref)

        z = ALPHA * x_ref[...] + _nn_raw(a_ref[...], w_ref[...])
        tgt = t_ref[...]
        loss, vjp = jax.vjp(lambda zz, gg, bb: loss_fn(zz, gg, bb, tgt), z, g_ref[...], b_ref[...])
        d_z, d_g, d_b = vjp(jnp.ones((1, 1), F32))
        loss_ref[...] += loss
        dz_ref[...] = d_z
        dg_ref[...] += d_g
        db_ref[...] += d_b

    tok = pl.BlockSpec((tb, dm), lambda i: (i, 0))
    vec = pl.BlockSpec((1, dm), lambda i: (0, 0))
    one = pl.BlockSpec((1, 1), lambda i: (0, 0))
    return pl.pallas_call(
        body, name="ffn_down_loss_tail", grid=(s // tb,),
        in_specs=[pl.BlockSpec((tb, k), lambda i: (i, 0)), pl.BlockSpec((k, dm), lambda i: (0, 0)), tok, vec, vec, tok],
        out_specs=[one, tok, vec, vec],
        out_shape=[jax.ShapeDtypeStruct((1, 1), F32), jax.ShapeDtypeStruct((s, dm), F32),
                   jax.ShapeDtypeStruct((1, dm), F32), jax.ShapeDtypeStruct((1, dm), F32)],
        compiler_params=_params(("arbitrary",)),
    )(a, w, xres, g, b, target)


def _att_head(q, k, v):
    sc = _nt(q, k) * (CA_DH ** -0.5)
    return _nn(jax.nn.softmax(sc, axis=-1), v)


def _cross_attention_fwd(x1, kv, wq, wo, g, b):
    s = x1.shape[0]
    tb = min(ATT_TOKENS, s)

    def body(x_ref, kv_ref, wq_ref, wo_ref, g_ref, b_ref, att_ref, z_ref, o_ref):
        x_blk = x_ref[...]
        q = _nn_raw(x_blk, wq_ref[...])
        heads = []
        for h in range(CA_HEADS):
            lo = h * CA_DH
            heads.append(_att_head(q[:, lo:lo + CA_DH], kv_ref[:, lo:lo + CA_DH],
                                   kv_ref[:, D_MODEL + lo:D_MODEL + lo + CA_DH]))
        att = jnp.concatenate(heads, axis=1)
        att_ref[...] = att.astype(att_ref.dtype)
        z = ALPHA * x_blk + _nn_raw(att, wo_ref[...])
        z_ref[...] = z
        o_ref[...] = _layer_norm(z, g_ref[...], b_ref[...])

    tok = pl.BlockSpec((tb, D_MODEL), lambda i: (i, 0))
    mat = pl.BlockSpec((D_MODEL, D_MODEL), lambda i: (0, 0))
    vec = pl.BlockSpec((1, D_MODEL), lambda i: (0, 0))
    act = jax.ShapeDtypeStruct((s, D_MODEL), F32)
    return pl.pallas_call(
        body, name="cross_attention_fwd", grid=(s // tb,),
        in_specs=[tok, pl.BlockSpec((N_MEM, 2 * D_MODEL), lambda i: (0, 0)), mat, mat, vec, vec],
        out_specs=[tok, tok, tok],
        out_shape=[jax.ShapeDtypeStruct((s, D_MODEL), BF16), act, act],
        compiler_params=_params(("parallel",)),
    )(x1, kv, wq, wo, g, b)


def _cross_attention_bwd(d_x2, x1, z2, kv, wq, wo, g, b):
    s = x1.shape[0]
    tb = min(ATT_TOKENS, s)

    def body(dx2_ref, x_ref, z_ref, kv_ref, wq_ref, wo_ref, g_ref, b_ref,
             dx1_ref, dq_ref, dz_ref, dkv_ref, dg_ref, db_ref):
        @pl.when(pl.program_id(0) == 0)
        def _():
            dkv_ref[...] = jnp.zeros_like(dkv_ref)
            dg_ref[...] = jnp.zeros_like(dg_ref)
            db_ref[...] = jnp.zeros_like(db_ref)

        _, ln_vjp = jax.vjp(_layer_norm, z_ref[...], g_ref[...], b_ref[...])
        d_z, d_g, d_b = ln_vjp(dx2_ref[...])
        dg_ref[...] += d_g
        db_ref[...] += d_b
        dz_ref[...] = d_z.astype(dz_ref.dtype)
        d_att = _nt_raw(d_z, wo_ref[...])
        q = _nn_raw(x_ref[...], wq_ref[...])
        d_q = []
        for h in range(CA_HEADS):
            lo = h * CA_DH
            vlo = D_MODEL + lo
            _, vjp = jax.vjp(_att_head, q[:, lo:lo + CA_DH], kv_ref[:, lo:lo + CA_DH], kv_ref[:, vlo:vlo + CA_DH])
            d_qh, d_k, d_v = vjp(d_att[:, lo:lo + CA_DH])
            d_q.append(d_qh)
            dkv_ref[:, lo:lo + CA_DH] += d_k
            dkv_ref[:, vlo:vlo + CA_DH] += d_v
        d_q = jnp.concatenate(d_q, axis=1)
        dq_ref[...] = d_q.astype(dq_ref.dtype)
        dx1_ref[...] = ALPHA * d_z + _nt_raw(d_q, wq_ref[...])

    tok = pl.BlockSpec((tb, D_MODEL), lambda i: (i, 0))
    mem = pl.BlockSpec((N_MEM, 2 * D_MODEL), lambda i: (0, 0))
    mat = pl.BlockSpec((D_MODEL, D_MODEL), lambda i: (0, 0))
    vec = pl.BlockSpec((1, D_MODEL), lambda i: (0, 0))
    low = jax.ShapeDtypeStruct((s, D_MODEL), BF16)
    return pl.pallas_call(
        body, name="cross_attention_bwd", grid=(s // tb,),
        in_specs=[tok, tok, tok, mem, mat, mat, vec, vec], out_specs=[tok, tok, tok, mem, vec, vec],
        out_shape=[jax.ShapeDtypeStruct((s, D_MODEL), F32), low, low,
                   jax.ShapeDtypeStruct((N_MEM, 2 * D_MODEL), F32),
                   jax.ShapeDtypeStruct((1, D_MODEL), F32), jax.ShapeDtypeStruct((1, D_MODEL), F32)],
        compiler_params=_params(("arbitrary",)),
    )(d_x2, x1, z2, kv, wq, wo, g, b)


def _local_step(x, mem, target, w, mid_weights=None, ffn_weights=None, on_ffn_grads=None, on_mid_grads=None,
                on_small_grads=None, on_last_grads=None):
    w = dict(w)
    s = x.shape[0]
    tm = min(512, s)
    tt = min(512, s)
    proj = _matmul_nn(x, w["w_in_main"], w["b_in_main"], min(2048, s), 512, "proj")
    gates = _matmul_nn(x, w["w_in_gate"], w["b_in_gate"], tm, LANES, "proj_gates")
    qk = _ml_conv_fwd(proj, w["ml_conv_w"], w["ml_conv_b"])
    y, hg_states = _hgrn2_fwd(proj, w["hg_lb_logits"], w["hg_norm_w"])
    y, ct_s, n_s, m_s = _mlstm_fwd(qk, proj, gates, w["ml_norm_w"], y)
    if mid_weights is not None:
        w.update(mid_weights(y))
    z1, x1 = _proj_res_ln(y, w["w_out"], x, w["ln1_g"], w["ln1_b"], "out_proj_ln1")
    kv = _matmul_nn(mem, w["ca_wkv"], None, N_MEM, CA_DH, "kv")
    att, z2, x2 = _cross_attention_fwd(x1, kv, w["ca_wq"], w["ca_wo"], w["ln2_g"], w["ln2_b"])
    if ffn_weights is not None:
        w.update(ffn_weights(x2))
    u = _matmul_nn(x2, w["ffn_w_up"], None, min(2048, s), UP_SHARD_P, "ffn_up", BF16)
    hid = _ffn_conv_fwd(u, w["ffn_conv_w"], w["ffn_conv_b"])
    loss, d_z3, d_ln3_g, d_ln3_b = _proj_loss_tail(hid, w["ffn_w_down"], x2, w["ln3_g"], w["ln3_b"], target)
    grads = {"ln3_g": d_ln3_g, "ln3_b": d_ln3_b}
    grads["ffn_w_down"] = _matmul_tn(hid, d_z3, 1536, D_MODEL, tt, "d_w_down")
    d_hid = _matmul_nt([(d_z3, w["ffn_w_down"])], None, 1.0, tm, D_FF_P, "d_hid", BF16)
    d_ug, d_uv, d_cwg, d_cwv, d_cbg, d_cbv = _ffn_conv_bwd(u, w["ffn_conv_w"], w["ffn_conv_b"], d_hid)
    grads["ffn_conv_w"] = jnp.concatenate([d_cwg, d_cwv], axis=-1)
    grads["ffn_conv_b"] = jnp.concatenate([d_cbg, d_cbv], axis=-1)
    half = N_DEV // 2
    d_w_up = _matmul_tn(x2, d_ug, D_MODEL, UP_SHARD_P, tt, "d_w_up_gate", shards=N_DEV, group=half)
    grads["ffn_w_up"] = _matmul_tn(x2, d_uv, D_MODEL, UP_SHARD_P, tt, "d_w_up_val", shards=N_DEV,
                                   shard0=half, group=half, into=d_w_up)
    d_x2 = _matmul_nt([(d_ug, w["ffn_w_up"], 0), (d_uv, w["ffn_w_up"], N_DEV // 2)], d_z3, ALPHA,
                      min(256, s), D_MODEL, "d_x2")
    if on_ffn_grads is not None:
        d_x2 = on_ffn_grads(grads, d_x2)
    d_x1, d_q, d_z2, d_kv, grads["ln2_g"], grads["ln2_b"] = _cross_attention_bwd(
        d_x2, x1, z2, kv, w["ca_wq"], w["ca_wo"], w["ln2_g"], w["ln2_b"])
    grads["ca_wo"] = _matmul_tn(att, d_z2, D_MODEL, D_MODEL, tt, "d_ca_wo")
    grads["ca_wq"] = _matmul_tn(x1, d_q, D_MODEL, D_MODEL, tt, "d_ca_wq")
    grads["ca_wkv"] = _matmul_tn(mem, d_kv, D_MODEL, CA_DH, N_MEM, "d_ca_wkv", shards=N_DEV, group=N_DEV)
    d_z1, d_y, grads["ln1_g"], grads["ln1_b"] = _ln_bwd_proj(d_x1, z1, w["ln1_g"], w["ln1_b"], w["w_out"],
                                                             "ln1_bwd_out_proj")
    grads["w_out"] = _matmul_tn(y, d_z1, D_MODEL, D_MODEL, tt, "d_w_out")
    if on_mid_grads is not None:
        d_y = on_mid_grads(grads, d_y)
    d_proj, grads["hg_lb_logits"], grads["hg_norm_w"], db_hg = _hgrn2_bwd(
        proj, w["hg_lb_logits"], w["hg_norm_w"], hg_states, d_y)
    d_proj, d_qk, d_gates, grads["ml_norm_w"], db_vo = _mlstm_bwd(
        qk, proj, gates, w["ml_norm_w"], ct_s, n_s, m_s, d_y, d_proj)
    d_proj, grads["ml_conv_w"], grads["ml_conv_b"], db_qk = _ml_conv_bwd(
        proj, w["ml_conv_w"], w["ml_conv_b"], d_qk, d_proj)
    grads["b_in_main"] = jnp.concatenate([db_hg, db_qk, db_vo], axis=-1)
    grads["w_in_gate"], grads["b_in_gate"] = _matmul_tn(x, d_gates, D_MODEL, LANES, tt, "d_w_in_gates", colsum=True)
    if on_small_grads is not None:
        d_proj = on_small_grads(grads, loss, d_proj)
    grads["w_in_main"] = _matmul_tn(x, d_proj, D_MODEL, min(2048, D_IN_MAIN), tt, "d_w_in")
    if on_last_grads is not None:
        d_z1 = on_last_grads(grads, d_z1)
    grad_x = _matmul_nt([(d_proj, w["w_in_main"]), (d_gates, w["w_in_gate"])], d_z1, ALPHA, tm, D_MODEL, "d_x")
    return loss, grad_x, grads


HBM_SPEC = pl.BlockSpec(memory_space=pltpu.HBM)


def _coords():
    return lax.axis_index("x"), lax.axis_index("y"), lax.axis_index("c")


def _other_chips(x, y):
    return [(1 - x, y), (x, 1 - y), (1 - x, 1 - y)]


def _all_gather_two_level(shards, name):
    na = len(shards)

    def body(*refs):
        x_refs, out_refs = refs[:na], refs[na:2 * na]
        send_sems, recv_sems, local_sems = refs[2 * na:]
        x, y, c = _coords()
        me, sibling = (x, y, c), (x, y, 1 - c)
        chips = _other_chips(x, y)

        def copy(a, k, block, to, own=False):
            slot = out_refs[a].at[4 * block[0] + 2 * block[1] + block[2]]
            return pltpu.make_async_remote_copy(
                src_ref=x_refs[a] if own else slot, dst_ref=slot,
                send_sem=send_sems.at[7 * a + k], recv_sem=recv_sems.at[7 * a + k],
                device_id=to, device_id_type=MESH)

        mine = [pltpu.make_async_copy(x_refs[a], out_refs[a].at[4 * x + 2 * y + c], local_sems.at[a])
                for a in range(na)]
        for cp in mine:
            cp.start()
        first = []
        for a in range(na):
            first.append(copy(a, 0, me, sibling, own=True))
            first += [copy(a, 1 + j, me, (*chip, c), own=True) for j, chip in enumerate(chips)]
        for cp in first:
            cp.start()
        passed = []
        for j, chip in enumerate(chips):
            for a in range(na):
                copy(a, 1 + j, (*chip, c), me).wait_recv()
                fwd = copy(a, 4 + j, (*chip, c), sibling)
                fwd.start()
                passed.append(fwd)
        for a in range(na):
            copy(a, 0, sibling, me).wait_recv()
            for j, chip in enumerate(chips):
                copy(a, 4 + j, (*chip, 1 - c), me).wait_recv()
        for cp in first + passed:
            cp.wait_send()
        for cp in mine:
            cp.wait()

    return pl.pallas_call(
        body, name=name,
        out_shape=[jax.ShapeDtypeStruct((N_DEV,) + t.shape, t.dtype) for t in shards],
        in_specs=[HBM_SPEC] * na, out_specs=[HBM_SPEC] * na,
        scratch_shapes=[pltpu.SemaphoreType.DMA((7 * na,)), pltpu.SemaphoreType.DMA((7 * na,)),
                        pltpu.SemaphoreType.DMA((na,))],
    )(*shards)


SEM_SPEC = pl.BlockSpec(memory_space=pltpu.SEMAPHORE)
ANY_SPEC = pl.BlockSpec(memory_space=pl.ANY)
SIDE_EFFECT = pltpu.SideEffectType.DATAFLOW_SIDE_EFFECTING


def _peer(x, y, c, d):
    flip = lambda v, bit: 1 - v if bit else v
    p = (flip(x, d & 4), flip(y, d & 2), flip(c, d & 1))
    return p, 4 * p[0] + 2 * p[1] + p[2]


def _direct_copies(gather, src_refs, land_refs, send_sems, recv_sems):
    x, y, c = _coords()
    me = 4 * x + 2 * y + c
    copies = []
    for a in range(len(src_refs)):
        for d in range(1, N_DEV):
            peer, peer_slot = _peer(x, y, c, d)
            copies.append(pltpu.make_async_remote_copy(
                src_ref=src_refs[a] if gather else src_refs[a].at[peer_slot],
                dst_ref=land_refs[a].at[me] if gather else land_refs[a].at[d - 1],
                send_sem=send_sems.at[7 * a + d - 1], recv_sem=recv_sems.at[7 * a + d - 1],
                device_id=peer, device_id_type=MESH))
    return copies


def _hbm(t):
    return pltpu.HBM(t.shape, t.dtype)


def _direct_start(gather, arrays, through, name):
    na = len(arrays)
    lands = [lax.empty((N_DEV,) + t.shape if gather else (N_DEV - 1,) + t.shape[1:], t.dtype) for t in arrays]
    n_io = 2 * na + 1

    def body(*refs):
        for cp in _direct_copies(gather, refs[:na], refs[na:2 * na], refs[n_io], refs[n_io + 1]):
            cp.start()

    ins = [pltpu.with_memory_space_constraint(t, pltpu.HBM) for t in (*arrays, *lands, through)]
    sems = pltpu.SemaphoreType.DMA((7 * na,))
    res = pl.pallas_call(
        body, name=name, out_shape=(sems, sems, *[_hbm(t) for t in ins]),
        in_specs=[HBM_SPEC] * n_io, out_specs=(SEM_SPEC, SEM_SPEC, *[HBM_SPEC] * n_io),
        input_output_aliases={i: 2 + i for i in range(n_io)},
        compiler_params=pltpu.CompilerParams(has_side_effects=SIDE_EFFECT),
    )(*ins)
    return (res[0], res[1], list(res[2:2 + na]), list(res[2 + na:2 + 2 * na])), res[2 + 2 * na]


def _direct_wait(gather, started, after, name):
    send_sems, recv_sems, arrays, lands = started
    na = len(arrays)

    def body(*refs):
        for cp in _direct_copies(gather, refs[:na], refs[na:2 * na], refs[2 * na], refs[2 * na + 1]):
            cp.wait_send()
            cp.wait_recv()

    res = pl.pallas_call(
        body, name=name, out_shape=tuple(_hbm(t) for t in (*arrays, *lands)),
        in_specs=[HBM_SPEC] * (2 * na) + [SEM_SPEC, SEM_SPEC, ANY_SPEC], out_specs=tuple([HBM_SPEC] * (2 * na)),
        input_output_aliases={i: i for i in range(2 * na)},
        compiler_params=pltpu.CompilerParams(has_side_effects=SIDE_EFFECT),
    )(*arrays, *lands, send_sems, recv_sems, after)
    return list(res[:na]), list(res[na:])


def _row_tile(rows):
    for t in (256, 176, 128):
        if rows % t == 0 and rows > t:
            return t
    return rows


def _adamw_math(g, w, m, v):
    m_new = ADAM_B1 * m + (1.0 - ADAM_B1) * g
    v_new = ADAM_B2 * v + (1.0 - ADAM_B2) * jnp.square(g)
    m_hat = m_new / (1.0 - ADAM_B1 ** ADAM_STEP)
    v_hat = v_new / (1.0 - ADAM_B2 ** ADAM_STEP)
    delta = -ADAM_LR * (m_hat / (jnp.sqrt(v_hat) + ADAM_EPS) + ADAM_WD * w)
    return delta, m_new, v_new


def _adamw_sharded(chip, sums, got, w, m, v, name):
    r, c = w.shape
    tr = _row_tile(r)
    n_got = got.shape[0]

    def body(chip_ref, s_ref, g_ref, w_ref, m_ref, v_ref, go_ref, d_ref, nm_ref, nv_ref):
        g = s_ref[...].astype(F32)
        for i in range(n_got):
            g = g + g_ref[i].astype(F32)
        delta, m_new, v_new = _adamw_math(g, w_ref[...], m_ref[...], v_ref[...])
        go_ref[...] = g
        d_ref[...] = delta
        nm_ref[...] = m_new
        nv_ref[...] = v_new

    blk = pl.BlockSpec((tr, c), lambda i, chip_ref: (i, 0))
    out = jax.ShapeDtypeStruct((r, c), F32)
    return pl.pallas_call(
        body, name=name,
        grid_spec=pltpu.PrefetchScalarGridSpec(
            num_scalar_prefetch=1, grid=(r // tr,),
            in_specs=[pl.BlockSpec((None, tr, c), lambda i, chip_ref: (chip_ref[0], i, 0)),
                      pl.BlockSpec((n_got, tr, c), lambda i, chip_ref: (0, i, 0)), blk, blk, blk],
            out_specs=[blk, blk, blk, blk]),
        out_shape=[out, out, out, out],
        compiler_params=_params(("parallel",)),
    )(chip, sums, got, w, m, v)


def _adamw_replicated(parts, w, m, v):
    p, r, c = parts.shape

    def body(p_ref, w_ref, m_ref, v_ref, g_ref, d_ref, nm_ref, nv_ref):
        g = p_ref[0]
        for i in range(1, p):
            g = g + p_ref[i]
        delta, m_new, v_new = _adamw_math(g, w_ref[...], m_ref[...], v_ref[...])
        g_ref[...] = g
        d_ref[...] = delta
        nm_ref[...] = m_new
        nv_ref[...] = v_new

    blk = pl.BlockSpec((r, c), lambda i: (0, 0))
    out = jax.ShapeDtypeStruct((r, c), F32)
    return pl.pallas_call(
        body, name="adamw_replicated", grid=(1,),
        in_specs=[pl.BlockSpec((p, r, c), lambda i: (0, 0, 0)), blk, blk, blk],
        out_specs=[blk, blk, blk, blk], out_shape=[out, out, out, out],
        compiler_params=_params(("arbitrary",)),
    )(parts, w, m, v)


SHARDED_NAMES = ("w_in", "ml_conv_w", "w_out", "ca_wq", "ca_wkv", "ca_wo", "ffn_w_up", "ffn_conv_w", "ffn_w_down")
SMALL_NAMES = ("b_in", "hg_lb_logits", "hg_norm_w", "ml_conv_b", "ml_norm_w", "ln1_g", "ln1_b",
               "ln2_g", "ln2_b", "ffn_conv_b", "ln3_g", "ln3_b")
WEIGHT_NAMES = ("w_in", "b_in", "hg_lb_logits", "hg_norm_w", "ml_conv_w", "ml_conv_b", "ml_norm_w", "w_out",
                "ln1_g", "ln1_b", "ca_wq", "ca_wkv", "ca_wo", "ln2_g", "ln2_b", "ffn_w_up", "ffn_conv_w",
                "ffn_conv_b", "ffn_w_down", "ln3_g", "ln3_b")
PAD_TO = {"w_in": W_IN_SHARD_P, "ffn_w_up": UP_SHARD_P, "ffn_conv_w": UP_SHARD_P}
SMALL_ROWS = 24
SMALL_W = D_MODEL


def _shard_2d(name, block):
    t = block[0]
    if name in PAD_TO:
        t = jnp.pad(t, ((0, 0), (0, PAD_TO[name] - t.shape[1])))
    return t


def _shard_like(name, t, like):
    return t[:, :like.shape[2]][None]


def _pad_cols(t, width):
    return jnp.pad(t, ((0, 0), (0, width - t.shape[1])))


FIRST_NAMES = ("w_in", "ml_conv_w")
FFN_NAMES = ("ffn_w_up", "ffn_w_down", "ffn_conv_w")
MID_NAMES = ("ca_wo", "ca_wq", "ca_wkv", "w_out")


def _first_weights(g, small):
    w = dict(small)
    w_in = jnp.concatenate([g["w_in"][j, :, :W_IN_SHARD] for j in range(N_DEV)], axis=1)
    w["w_in_main"] = w_in[:, :D_IN_MAIN]
    w["w_in_gate"] = _pad_cols(w_in[:, D_IN_MAIN:], LANES)
    w["b_in_main"] = small["b_in"][:, :D_IN_MAIN]
    w["b_in_gate"] = _pad_cols(small["b_in"][:, D_IN_MAIN:], LANES)
    w["ml_conv_w"] = jnp.transpose(g["ml_conv_w"], (1, 0, 2)).reshape(ML_CONV, 2 * D_GROUP)
    return w


def _mid_weights(g):
    w = {n: g[n].reshape(D_MODEL, D_MODEL) for n in ("w_out", "ca_wq", "ca_wo")}
    w["ca_wkv"] = g["ca_wkv"]
    return w


def _ffn_weights(g, small):
    w = {"ffn_w_up": g["ffn_w_up"]}
    down = g["ffn_w_down"].reshape(N_DEV // 2, UP_SHARD, D_MODEL)
    w["ffn_w_down"] = jnp.pad(down, ((0, 0), (0, UP_SHARD_P - UP_SHARD), (0, 0))).reshape(D_FF_P, D_MODEL)
    w["ffn_conv_w"] = jnp.transpose(g["ffn_conv_w"], (1, 0, 2)).reshape(FFN_CONV, D_UP_P)
    w["ffn_conv_b"] = _pad_cols(small["ffn_conv_b"].reshape(N_DEV, UP_SHARD), UP_SHARD_P).reshape(1, D_UP_P)
    return w


def _whole_weights(g, small):
    return {**_first_weights(g, small), **_mid_weights(g), **_ffn_weights(g, small)}


def _owner_stack(n, grads):
    if n == "w_in":
        w_in = jnp.concatenate([grads["w_in_main"], grads["w_in_gate"][:, :D_IN - D_IN_MAIN]], axis=1)
        return jnp.stack([_pad_cols(w_in[:, j * W_IN_SHARD:(j + 1) * W_IN_SHARD], W_IN_SHARD_P)
                          for j in range(N_DEV)])
    if n in ("w_out", "ca_wq", "ca_wo"):
        return grads[n].reshape(N_DEV, D_MODEL // N_DEV, D_MODEL)
    if n == "ffn_w_down":
        down = grads[n].reshape(N_DEV // 2, UP_SHARD_P, D_MODEL)[:, :UP_SHARD]
        return down.reshape(N_DEV, D_FF // N_DEV, D_MODEL)
    if n == "ml_conv_w":
        return jnp.transpose(grads[n].reshape(ML_CONV, N_DEV, LANES), (1, 0, 2))
    if n == "ffn_conv_w":
        return jnp.transpose(grads[n].reshape(FFN_CONV, N_DEV, UP_SHARD_P), (1, 0, 2))
    return grads[n]


def _owner_stacks(grads):
    return {n: _owner_stack(n, grads) for n in SHARDED_NAMES}


def _small_grads(grads):
    out = {n: grads[n] for n in SMALL_NAMES if n in grads}
    out["b_in"] = jnp.concatenate([grads["b_in_main"], grads["b_in_gate"][:, :D_IN - D_IN_MAIN]], axis=1)
    out["ffn_conv_b"] = grads["ffn_conv_b"].reshape(N_DEV, UP_SHARD_P)[:, :UP_SHARD].reshape(1, D_UP)
    return out


def _pack_small(p, extra=None):
    flat = [p[n].reshape(-1) for n in SMALL_NAMES]
    if extra is not None:
        flat.append(extra.reshape(-1))
    flat = jnp.concatenate(flat)
    return jnp.pad(flat, (0, SMALL_ROWS * SMALL_W - flat.shape[0])).reshape(SMALL_ROWS, SMALL_W)


def _unpack_small(slab, like):
    out = {}
    flat = slab.reshape(-1)
    o = 0
    for n in SMALL_NAMES:
        out[n] = flat[o:o + like[n].size].reshape(like[n].shape)
        o += like[n].size
    return out, flat[o]


def kernel(x, mem, w_in, b_in, hg_lb_logits, hg_norm_w, ml_conv_w, ml_conv_b, ml_norm_w, w_out, ln1_g, ln1_b, ca_wq, ca_wkv, ca_wo, ln2_g, ln2_b, ffn_w_up, ffn_conv_w, ffn_conv_b, ffn_w_down, ln3_g, ln3_b, loss_target, m_w_in, m_b_in, m_hg_lb_logits, m_hg_norm_w, m_ml_conv_w, m_ml_conv_b, m_ml_norm_w, m_w_out, m_ln1_g, m_ln1_b, m_ca_wq, m_ca_wkv, m_ca_wo, m_ln2_g, m_ln2_b, m_ffn_w_up, m_ffn_conv_w, m_ffn_conv_b, m_ffn_w_down, m_ln3_g, m_ln3_b, v_w_in, v_b_in, v_hg_lb_logits, v_hg_norm_w, v_ml_conv_w, v_ml_conv_b, v_ml_norm_w, v_w_out, v_ln1_g, v_ln1_b, v_ca_wq, v_ca_wkv, v_ca_wo, v_ln2_g, v_ln2_b, v_ffn_w_up, v_ffn_conv_w, v_ffn_conv_b, v_ffn_w_down, v_ln3_g, v_ln3_b):
    params = dict(w_in=w_in, b_in=b_in, hg_lb_logits=hg_lb_logits, hg_norm_w=hg_norm_w, ml_conv_w=ml_conv_w,
                  ml_conv_b=ml_conv_b, ml_norm_w=ml_norm_w, w_out=w_out, ln1_g=ln1_g, ln1_b=ln1_b, ca_wq=ca_wq,
                  ca_wkv=ca_wkv, ca_wo=ca_wo, ln2_g=ln2_g, ln2_b=ln2_b, ffn_w_up=ffn_w_up, ffn_conv_w=ffn_conv_w,
                  ffn_conv_b=ffn_conv_b, ffn_w_down=ffn_w_down, ln3_g=ln3_g, ln3_b=ln3_b)
    mom1 = dict(w_in=m_w_in, b_in=m_b_in, hg_lb_logits=m_hg_lb_logits, hg_norm_w=m_hg_norm_w,
                ml_conv_w=m_ml_conv_w, ml_conv_b=m_ml_conv_b, ml_norm_w=m_ml_norm_w, w_out=m_w_out, ln1_g=m_ln1_g,
                ln1_b=m_ln1_b, ca_wq=m_ca_wq, ca_wkv=m_ca_wkv, ca_wo=m_ca_wo, ln2_g=m_ln2_g, ln2_b=m_ln2_b,
                ffn_w_up=m_ffn_w_up, ffn_conv_w=m_ffn_conv_w, ffn_conv_b=m_ffn_conv_b, ffn_w_down=m_ffn_w_down,
                ln3_g=m_ln3_g, ln3_b=m_ln3_b)
    mom2 = dict(w_in=v_w_in, b_in=v_b_in, hg_lb_logits=v_hg_lb_logits, hg_norm_w=v_hg_norm_w,
                ml_conv_w=v_ml_conv_w, ml_conv_b=v_ml_conv_b, ml_norm_w=v_ml_norm_w, w_out=v_w_out, ln1_g=v_ln1_g,
                ln1_b=v_ln1_b, ca_wq=v_ca_wq, ca_wkv=v_ca_wkv, ca_wo=v_ca_wo, ln2_g=v_ln2_g, ln2_b=v_ln2_b,
                ffn_w_up=v_ffn_w_up, ffn_conv_w=v_ffn_conv_w, ffn_conv_b=v_ffn_conv_b, ffn_w_down=v_ffn_w_down,
                ln3_g=v_ln3_g, ln3_b=v_ln3_b)

    x_idx, y_idx, c_idx = _coords()
    as_index = lambda v: jnp.reshape(v, (1,)).astype(jnp.int32)
    me = as_index(4 * x_idx + 2 * y_idx + c_idx)
    small_params = {n: params[n] for n in SMALL_NAMES}

    shards = {n: _shard_2d(n, params[n]) for n in SHARDED_NAMES}
    to_send = lambda names: [shards[n] if "conv" in n else shards[n].astype(BF16) for n in names]
    first = dict(zip(FIRST_NAMES, _all_gather_two_level(to_send(FIRST_NAMES), "weights_gather_first")))
    mid_started, through = _direct_start(True, to_send(MID_NAMES), first["w_in"], "weights_gather_start_mid")
    ffn_started, first["w_in"] = _direct_start(True, to_send(FFN_NAMES), through, "weights_gather_start_ffn")

    def gathered_weights(names, started, after, tag):
        mine, lands = _direct_wait(True, started, after, "weights_gather_wait_" + tag)
        return {n: lax.dynamic_update_index_in_dim(land, own, me[0], 0) for n, own, land in zip(names, mine, lands)}

    started, own_stacks = {}, {}

    def start_group(names, tag):
        def hook(grads, through):
            own_stacks[tag] = [_owner_stack(n, grads).astype(BF16) for n in names]
            started[tag], through = _direct_start(False, own_stacks[tag], through, "grads_start_" + tag)
            return through
        return hook

    def start_small(grads, loss, through):
        started["small"], through = _direct_start(True, [_pack_small(_small_grads(grads), loss)], through,
                                                  "small_gather_start")
        return through

    loss, grad_x, grads = _local_step(
        x[0], mem[0], loss_target[0], _first_weights(first, small_params),
        lambda y: _mid_weights(gathered_weights(MID_NAMES, mid_started, y, "mid")),
        lambda x2: _ffn_weights(gathered_weights(FFN_NAMES, ffn_started, x2, "ffn"), small_params),
        start_group(FFN_NAMES, "ffn"), start_group(MID_NAMES, "mid"), start_small, start_group(FIRST_NAMES, "last"))

    sharded_out = {}
    after = grad_x
    for names, tag in ((FFN_NAMES, "ffn"), (MID_NAMES, "mid"), (FIRST_NAMES, "last")):
        _, lands = _direct_wait(False, started[tag], after, "grads_wait_" + tag)
        for n, st, land in zip(names, own_stacks[tag], lands):
            res = _adamw_sharded(me, st, land, shards[n], _shard_2d(n, mom1[n]), _shard_2d(n, mom2[n]), "adamw_" + n)
            sharded_out[n] = [_shard_like(n, t, params[n]) for t in res]
            after = res[0]
    own_small, small_lands = _direct_wait(True, started["small"], after, "small_gather_wait")
    small_parts = lax.dynamic_update_index_in_dim(small_lands[0], own_small[0], me[0], 0)
    small_res = _adamw_replicated(small_parts, _pack_small(params), _pack_small(mom1), _pack_small(mom2))

    outs = []
    total_loss = None
    for k in range(4):
        small, extra = _unpack_small(small_res[k], params)
        if total_loss is None:
            total_loss = extra
        outs.extend(sharded_out[n][k] if n in sharded_out else small[n] for n in WEIGHT_NAMES)
    return (total_loss, grad_x[None], *outs)
```

```python
import functools

import jax
import jax.numpy as jnp
from jax import lax
from jax.experimental import pallas as pl
from jax.experimental.pallas import tpu as pltpu

F32 = jnp.float32
BF16 = jnp.bfloat16
HIGHEST = lax.Precision.HIGHEST
MESH = pl.DeviceIdType.MESH

N_DEV = 8
D_MODEL = 1024
N_MEM = 256
N_HEADS = 4
D_HEAD = 128
D_GROUP = N_HEADS * D_HEAD
CHUNK = 64
ML_CONV = 4
FFN_CONV = 3
D_FF = 2816
D_UP = 2 * D_FF
CA_HEADS = 4
CA_DH = D_MODEL // CA_HEADS
LANES = 128
SUBLANES = 8
D_IN = 8 * D_GROUP + 2 * N_HEADS
D_IN_MAIN = 8 * D_GROUP
W_IN_SHARD = D_IN // N_DEV
W_IN_SHARD_P = 640
UP_SHARD = D_UP // N_DEV
UP_SHARD_P = 768
D_UP_P = N_DEV * UP_SHARD_P
D_FF_P = D_UP_P // 2
ALPHA = 2.0 ** 0.25
LN_EPS = 1e-5
NEG_BIG = -1e30
ADAM_LR = 0.001
ADAM_B1 = 0.9
ADAM_B2 = 0.999
ADAM_EPS = 1e-08
ADAM_WD = 0.01
ADAM_STEP = 10
VMEM_LIMIT = 56 * 1024 * 1024

SEG_HQ, SEG_HF, SEG_HI, SEG_HG, SEG_MQ, SEG_MK, SEG_MV, SEG_MO = (4 * i for i in range(8))


def _params(sem):
    return pltpu.CompilerParams(dimension_semantics=sem, vmem_limit_bytes=VMEM_LIMIT)


def _dg(a, b, ca, cb, precision=None):
    return lax.dot_general(a, b, (((ca,), (cb,)), ((), ())), precision=precision,
                           preferred_element_type=F32)


def _nn_raw(a, b):
    return _dg(a.astype(BF16), b.astype(BF16), 1, 0)


def _nt_raw(a, b):
    return _dg(a.astype(BF16), b.astype(BF16), 1, 1)


def _tn_raw(a, b):
    return _dg(a.astype(BF16), b.astype(BF16), 0, 0)


@jax.custom_vjp
def _nn(a, b):
    return _nn_raw(a, b)


_nn.defvjp(lambda a, b: (_nn_raw(a, b), (a, b)),
           lambda res, g: (_nt_raw(g, res[1]), _tn_raw(res[0], g)))


@jax.custom_vjp
def _nt(a, b):
    return _nt_raw(a, b)


_nt.defvjp(lambda a, b: (_nt_raw(a, b), (a, b)),
           lambda res, g: (_nn_raw(g, res[1]), _tn_raw(g, res[0])))


@jax.custom_vjp
def _tn(a, b):
    return _tn_raw(a, b)


_tn.defvjp(lambda a, b: (_tn_raw(a, b), (a, b)),
           lambda res, g: (_nt_raw(res[1], g), _nn_raw(res[0], g)))


def _layer_norm(z, g, b):
    mu = jnp.mean(z, axis=-1, keepdims=True)
    var = jnp.mean(jnp.square(z - mu), axis=-1, keepdims=True)
    return (z - mu) * lax.rsqrt(var + LN_EPS) * g + b


def _matmul_nn(a, w, bias, tm, tn, name, out_dtype=F32):
    m, k = a.shape
    if w.ndim == 3:
        n = w.shape[0] * w.shape[2]
        assert tn == w.shape[2]
        w_spec = pl.BlockSpec((None, k, tn), lambda i, j: (j, 0, 0))
    else:
        n = w.shape[1]
        w_spec = pl.BlockSpec((k, tn), lambda i, j: (0, j))

    def body(*refs):
        a_ref, w_ref = refs[0], refs[1]
        o_ref = refs[-1]
        acc = _nn_raw(a_ref[...], w_ref[...])
        if bias is not None:
            acc = acc + refs[2][...]
        o_ref[...] = acc.astype(o_ref.dtype)

    in_specs = [pl.BlockSpec((tm, k), lambda i, j: (i, 0)), w_spec]
    args = [a, w]
    if bias is not None:
        in_specs.append(pl.BlockSpec((1, tn), lambda i, j: (0, j)))
        args.append(bias)
    return pl.pallas_call(
        body, name=name, grid=(m // tm, n // tn), in_specs=in_specs,
        out_specs=pl.BlockSpec((tm, tn), lambda i, j: (i, j)),
        out_shape=jax.ShapeDtypeStruct((m, n), out_dtype),
        compiler_params=_params(("parallel", "parallel")),
    )(*args)


def _matmul_nt(pairs, add, scale, tm, tk, name, out_dtype=F32):
    m = pairs[0][0].shape[0]
    k = pairs[0][1].shape[-2]
    groups = []
    in_specs, args = [], []
    for pair in pairs:
        d, w = pair[0], pair[1]
        in_specs.append(pl.BlockSpec((tm, d.shape[1]), lambda i, j: (i, 0)))
        if w.ndim == 3:
            g = d.shape[1] // w.shape[2]
            blk = pair[2] // g
            in_specs.append(pl.BlockSpec((g, tk, w.shape[2]), lambda i, j, blk=blk: (blk, j, 0)))
            groups.append((g, w.shape[2]))
        else:
            in_specs.append(pl.BlockSpec((tk, w.shape[1]), lambda i, j: (j, 0)))
            groups.append(None)
        args += [d, w]
    if add is not None:
        in_specs.append(pl.BlockSpec((tm, tk), lambda i, j: (i, j)))
        args.append(add)

    def body(*refs):
        o_ref = refs[-1]
        acc = None
        for p, grp in enumerate(groups):
            d_ref, w_ref = refs[2 * p], refs[2 * p + 1]
            if grp is None:
                terms = [_nt_raw(d_ref[...], w_ref[...])]
            else:
                terms = [_nt_raw(d_ref[:, g * grp[1]:(g + 1) * grp[1]], w_ref[g]) for g in range(grp[0])]
            for t in terms:
                acc = t if acc is None else acc + t
        if add is not None:
            acc = acc + scale * refs[2 * len(groups)][...]
        o_ref[...] = acc.astype(o_ref.dtype)

    return pl.pallas_call(
        body, name=name, grid=(m // tm, k // tk), in_specs=in_specs,
        out_specs=pl.BlockSpec((tm, tk), lambda i, j: (i, j)),
        out_shape=jax.ShapeDtypeStruct((m, k), out_dtype),
        compiler_params=_params(("parallel", "parallel")),
    )(*args)


def _matmul_tn(a, b, tm, tn, tt, name, shards=None, shard0=0, group=1, into=None, colsum=False):
    t, m = a.shape
    n = b.shape[1]
    assert not colsum or tm == m
    n_in = 2 + (into is not None)
    out_dtype = BF16
    per_step = 1 if shards is None else group
    width = per_step * tn

    def body(*refs):
        a_ref, b_ref = refs[0], refs[1]
        o_ref, acc_ref = refs[n_in], refs[-1]
        first = pl.program_id(2) == 0

        @pl.when(first)
        def _():
            acc_ref[...] = jnp.zeros_like(acc_ref)

        if shards is None:
            acc_ref[...] += _tn_raw(a_ref[...], b_ref[...])
        else:
            lhs = a_ref[...].astype(BF16)
            for g in range(per_step):
                acc_ref[g] += _tn_raw(lhs, b_ref[:, g * tn:(g + 1) * tn])

        @pl.when(pl.program_id(2) == t // tt - 1)
        def _():
            o_ref[...] = acc_ref[...].astype(o_ref.dtype)

        if colsum:
            s_ref = refs[n_in + 1]

            @pl.when(first)
            def _():
                s_ref[...] = jnp.zeros_like(s_ref)

            s_ref[...] += jnp.sum(b_ref[...], axis=0, keepdims=True)

    in_specs = [pl.BlockSpec((tt, tm), lambda i, j, kk: (kk, i)),
                pl.BlockSpec((tt, width), lambda i, j, kk: (kk, j))]
    args = [a, b]
    aliases = {}
    if into is not None:
        in_specs.append(pl.BlockSpec(memory_space=pl.ANY))
        args.append(into)
        aliases = {2: 0}
    if shards is None:
        out_specs = [pl.BlockSpec((tm, tn), lambda i, j, kk: (i, j))]
        out_shape = [jax.ShapeDtypeStruct((m, n), out_dtype)]
        acc = pltpu.VMEM((tm, tn), F32)
    else:
        out_specs = [pl.BlockSpec((per_step, tm, tn), lambda i, j, kk: (shard0 // per_step + j, i, 0))]
        out_shape = [jax.ShapeDtypeStruct((shards, m, tn), out_dtype)]
        acc = pltpu.VMEM((per_step, tm, tn), F32)
    if colsum:
        out_specs.append(pl.BlockSpec((1, tn), lambda i, j, kk: (0, j)))
        out_shape.append(jax.ShapeDtypeStruct((1, n), F32))
    res = pl.pallas_call(
        body, name=name, grid=(m // tm, n // width, t // tt), in_specs=in_specs, out_specs=out_specs,
        out_shape=out_shape, input_output_aliases=aliases, scratch_shapes=[acc],
        compiler_params=_params(("parallel", "parallel", "arbitrary")),
    )(*args)
    return res if colsum else res[0]


ROW_TILE = 64


def _stack(ref, start, rows):
    return ref[pl.ds(start, rows), :].astype(F32).reshape(rows // SUBLANES, SUBLANES, LANES)


def _vreg_rows(ref, n):
    return [jnp.broadcast_to(ref[j:j + 1, :], (SUBLANES, LANES))[None] for j in range(n)]


def _column_total(acc):
    return jnp.sum(acc, axis=0, keepdims=True)


def _conv_fwd_tile(pad_ref, taps_w, bias, r0, rows):
    taps = len(taps_w)
    acc = bias
    for j in range(taps):
        acc = acc + _stack(pad_ref, SUBLANES - (taps - 1 - j) + r0, rows) * taps_w[j]
    return acc


def _conv_grads_tile(pad_ref, dpad_ref, dx_ref, taps_w, dws, r0, rows):
    taps = len(taps_w)
    x_rows = _stack(pad_ref, SUBLANES + r0, rows)
    dx = None
    for j in range(taps):
        d_shifted = _stack(dpad_ref, r0 + (taps - 1 - j), rows)
        term = d_shifted * taps_w[j]
        dx = term if dx is None else dx + term
        dws[j] = dws[j] + jnp.sum(d_shifted * x_rows, axis=0)
    dx_ref[r0:r0 + rows, :] = dx.reshape(rows, LANES).astype(dx_ref.dtype)
    return jnp.sum(dx, axis=0)


def _ml_conv_fwd(proj, conv_w, conv_b):
    s = proj.shape[0]
    nblk = 2 * D_GROUP // LANES

    def body(x_ref, w_ref, b_ref, o_ref, pad_ref):
        pad_ref[0:SUBLANES, :] = jnp.zeros((SUBLANES, LANES), F32)
        pad_ref[SUBLANES:, :] = x_ref[...].astype(F32)
        taps_w, bias = _vreg_rows(w_ref, ML_CONV), _vreg_rows(b_ref, 1)[0]
        for r0 in range(0, s, ROW_TILE):
            rows = min(ROW_TILE, s - r0)
            o_ref[r0:r0 + rows, :] = jax.nn.silu(_conv_fwd_tile(pad_ref, taps_w, bias, r0, rows)).reshape(rows, LANES)

    return pl.pallas_call(
        body, name="ml_conv_fwd", grid=(nblk,),
        in_specs=[pl.BlockSpec((s, LANES), lambda j: (0, SEG_MQ + j)),
                  pl.BlockSpec((ML_CONV, LANES), lambda j: (0, j)),
                  pl.BlockSpec((1, LANES), lambda j: (0, j))],
        out_specs=pl.BlockSpec((s, LANES), lambda j: (0, j)),
        out_shape=jax.ShapeDtypeStruct((s, 2 * D_GROUP), F32),
        scratch_shapes=[pltpu.VMEM((s + SUBLANES, LANES), F32)],
        compiler_params=_params(("parallel",)),
    )(proj, conv_w, conv_b)


def _ml_conv_bwd(proj, conv_w, conv_b, d_qk, d_proj):
    s = proj.shape[0]
    nblk = 2 * D_GROUP // LANES

    def body(x_ref, w_ref, b_ref, dy_ref, _, dx_ref, dw_ref, db_ref, dxs_ref, pad_ref, dpad_ref):
        pad_ref[0:SUBLANES, :] = jnp.zeros((SUBLANES, LANES), F32)
        pad_ref[SUBLANES:, :] = x_ref[...].astype(F32)
        dpad_ref[s:, :] = jnp.zeros((SUBLANES, LANES), F32)
        taps_w, bias = _vreg_rows(w_ref, ML_CONV), _vreg_rows(b_ref, 1)[0]
        db = jnp.zeros((SUBLANES, LANES), F32)
        for r0 in range(0, s, ROW_TILE):
            rows = min(ROW_TILE, s - r0)
            pre = _conv_fwd_tile(pad_ref, taps_w, bias, r0, rows)
            _, vjp = jax.vjp(jax.nn.silu, pre)
            d_pre, = vjp(_stack(dy_ref, r0, rows))
            dpad_ref[r0:r0 + rows, :] = d_pre.reshape(rows, LANES)
            db = db + jnp.sum(d_pre, axis=0)
        db_ref[...] = _column_total(db)
        dws = [jnp.zeros((SUBLANES, LANES), F32) for _ in range(ML_CONV)]
        dx_sum = jnp.zeros((SUBLANES, LANES), F32)
        for r0 in range(0, s, ROW_TILE):
            dx_sum = dx_sum + _conv_grads_tile(pad_ref, dpad_ref, dx_ref, taps_w, dws, r0, min(ROW_TILE, s - r0))
        dxs_ref[...] = _column_total(dx_sum)
        for j in range(ML_CONV):
            dw_ref[j:j + 1, :] = _column_total(dws[j])

    return pl.pallas_call(
        body, name="ml_conv_bwd", grid=(nblk,),
        in_specs=[pl.BlockSpec((s, LANES), lambda j: (0, SEG_MQ + j)),
                  pl.BlockSpec((ML_CONV, LANES), lambda j: (0, j)),
                  pl.BlockSpec((1, LANES), lambda j: (0, j)),
                  pl.BlockSpec((s, LANES), lambda j: (0, j)),
                  pl.BlockSpec(memory_space=pl.ANY)],
        out_specs=[pl.BlockSpec((s, LANES), lambda j: (0, SEG_MQ + j)),
                   pl.BlockSpec((ML_CONV, LANES), lambda j: (0, j)),
                   pl.BlockSpec((1, LANES), lambda j: (0, j)),
                   pl.BlockSpec((1, LANES), lambda j: (0, j))],
        out_shape=[jax.ShapeDtypeStruct(d_proj.shape, d_proj.dtype),
                   jax.ShapeDtypeStruct((ML_CONV, 2 * D_GROUP), F32),
                   jax.ShapeDtypeStruct((1, 2 * D_GROUP), F32),
                   jax.ShapeDtypeStruct((1, 2 * D_GROUP), F32)],
        input_output_aliases={4: 0},
        scratch_shapes=[pltpu.VMEM((s + SUBLANES, LANES), F32), pltpu.VMEM((s + SUBLANES, LANES), F32)],
        compiler_params=_params(("parallel",)),
    )(proj, conv_w, conv_b, d_qk, d_proj)


def _gelu_mul(a, b):
    return jax.nn.gelu(a) * b


FFN_BLOCKS = D_FF_P // LANES


def _ffn_conv_fwd(u, conv_w, conv_b):
    s = u.shape[0]

    def body(g_ref, v_ref, wg_ref, wv_ref, bg_ref, bv_ref, o_ref, gpad_ref, vpad_ref):
        for pad_ref, x_ref in ((gpad_ref, g_ref), (vpad_ref, v_ref)):
            pad_ref[0:SUBLANES, :] = jnp.zeros((SUBLANES, LANES), F32)
            pad_ref[SUBLANES:, :] = x_ref[...].astype(F32)
        taps_g, bias_g = _vreg_rows(wg_ref, FFN_CONV), _vreg_rows(bg_ref, 1)[0]
        taps_v, bias_v = _vreg_rows(wv_ref, FFN_CONV), _vreg_rows(bv_ref, 1)[0]
        for r0 in range(0, s, ROW_TILE):
            rows = min(ROW_TILE, s - r0)
            ug = _conv_fwd_tile(gpad_ref, taps_g, bias_g, r0, rows)
            uv = _conv_fwd_tile(vpad_ref, taps_v, bias_v, r0, rows)
            o_ref[r0:r0 + rows, :] = _gelu_mul(ug, uv).reshape(rows, LANES).astype(o_ref.dtype)

    col = lambda off: (lambda j: (0, off + j))
    return pl.pallas_call(
        body, name="ffn_conv_fwd", grid=(FFN_BLOCKS,),
        in_specs=[pl.BlockSpec((s, LANES), col(0)), pl.BlockSpec((s, LANES), col(FFN_BLOCKS)),
                  pl.BlockSpec((FFN_CONV, LANES), col(0)), pl.BlockSpec((FFN_CONV, LANES), col(FFN_BLOCKS)),
                  pl.BlockSpec((1, LANES), col(0)), pl.BlockSpec((1, LANES), col(FFN_BLOCKS))],
        out_specs=pl.BlockSpec((s, LANES), col(0)),
        out_shape=jax.ShapeDtypeStruct((s, D_FF_P), BF16),
        scratch_shapes=[pltpu.VMEM((s + SUBLANES, LANES), F32), pltpu.VMEM((s + SUBLANES, LANES), F32)],
        compiler_params=_params(("parallel",)),
    )(u, u, conv_w, conv_w, conv_b, conv_b)


def _ffn_conv_bwd(u, conv_w, conv_b, d_h):
    s = u.shape[0]

    def body(g_ref, v_ref, wg_ref, wv_ref, bg_ref, bv_ref, dh_ref,
             dug_ref, duv_ref, dwg_ref, dwv_ref, dbg_ref, dbv_ref,
             gpad_ref, vpad_ref, dgpad_ref, dvpad_ref):
        for pad_ref, x_ref in ((gpad_ref, g_ref), (vpad_ref, v_ref)):
            pad_ref[0:SUBLANES, :] = jnp.zeros((SUBLANES, LANES), F32)
            pad_ref[SUBLANES:, :] = x_ref[...].astype(F32)
        dgpad_ref[s:, :] = jnp.zeros((SUBLANES, LANES), F32)
        dvpad_ref[s:, :] = jnp.zeros((SUBLANES, LANES), F32)
        taps_g, bias_g = _vreg_rows(wg_ref, FFN_CONV), _vreg_rows(bg_ref, 1)[0]
        taps_v, bias_v = _vreg_rows(wv_ref, FFN_CONV), _vreg_rows(bv_ref, 1)[0]
        dbg = jnp.zeros((SUBLANES, LANES), F32)
        dbv = jnp.zeros((SUBLANES, LANES), F32)
        for r0 in range(0, s, ROW_TILE):
            rows = min(ROW_TILE, s - r0)
            ug = _conv_fwd_tile(gpad_ref, taps_g, bias_g, r0, rows)
            uv = _conv_fwd_tile(vpad_ref, taps_v, bias_v, r0, rows)
            _, vjp = jax.vjp(_gelu_mul, ug, uv)
            d_ug, d_uv = vjp(_stack(dh_ref, r0, rows))
            dgpad_ref[r0:r0 + rows, :] = d_ug.reshape(rows, LANES)
            dvpad_ref[r0:r0 + rows, :] = d_uv.reshape(rows, LANES)
            dbg = dbg + jnp.sum(d_ug, axis=0)
            dbv = dbv + jnp.sum(d_uv, axis=0)
        dbg_ref[...] = _column_total(dbg)
        dbv_ref[...] = _column_total(dbv)
        for pad_ref, dpad_ref, taps_w, dx_ref, dw_ref in ((gpad_ref, dgpad_ref, taps_g, dug_ref, dwg_ref),
                                                          (vpad_ref, dvpad_ref, taps_v, duv_ref, dwv_ref)):
            dws = [jnp.zeros((SUBLANES, LANES), F32) for _ in range(FFN_CONV)]
            for r0 in range(0, s, ROW_TILE):
                _conv_grads_tile(pad_ref, dpad_ref, dx_ref, taps_w, dws, r0, min(ROW_TILE, s - r0))
            for j in range(FFN_CONV):
                dw_ref[j:j + 1, :] = _column_total(dws[j])

    col = lambda off: (lambda j: (0, off + j))
    seq = pl.BlockSpec((s, LANES), col(0))
    return pl.pallas_call(
        body, name="ffn_conv_bwd", grid=(FFN_BLOCKS,),
        in_specs=[pl.BlockSpec((s, LANES), col(0)), pl.BlockSpec((s, LANES), col(FFN_BLOCKS)),
                  pl.BlockSpec((FFN_CONV, LANES), col(0)), pl.BlockSpec((FFN_CONV, LANES), col(FFN_BLOCKS)),
                  pl.BlockSpec((1, LANES), col(0)), pl.BlockSpec((1, LANES), col(FFN_BLOCKS)), seq],
        out_specs=[seq, seq, pl.BlockSpec((FFN_CONV, LANES), col(0)), pl.BlockSpec((FFN_CONV, LANES), col(0)),
                   pl.BlockSpec((1, LANES), col(0)), pl.BlockSpec((1, LANES), col(0))],
        out_shape=[jax.ShapeDtypeStruct((s, D_FF_P), BF16), jax.ShapeDtypeStruct((s, D_FF_P), BF16),
                   jax.ShapeDtypeStruct((FFN_CONV, D_FF_P), F32), jax.ShapeDtypeStruct((FFN_CONV, D_FF_P), F32),
                   jax.ShapeDtypeStruct((1, D_FF_P), F32), jax.ShapeDtypeStruct((1, D_FF_P), F32)],
        scratch_shapes=[pltpu.VMEM((s + SUBLANES, LANES), F32) for _ in range(4)],
        compiler_params=_params(("parallel",)),
    )(u, u, conv_w, conv_w, conv_b, conv_b, d_h)


def _chunk_masks(c):
    row = lax.broadcasted_iota(jnp.int32, (c, c), 0)
    col = lax.broadcasted_iota(jnp.int32, (c, c), 1)
    return row, col


@jax.custom_vjp
def _split_heads(x):
    return tuple(x[:, h * D_HEAD:(h + 1) * D_HEAD] for h in range(N_HEADS))


_split_heads.defvjp(lambda x: (_split_heads(x), None), lambda _, gs: (jnp.concatenate(gs, axis=1),))


@jax.custom_vjp
def _merge_heads(xs):
    return jnp.concatenate(xs, axis=1)


_merge_heads.defvjp(lambda xs: (_merge_heads(xs), None), lambda _, g: (_split_heads(g),))


@jax.custom_vjp
def _split_chunks(x):
    return tuple(x[i * CHUNK:(i + 1) * CHUNK] for i in range(x.shape[0] // CHUNK))


_split_chunks.defvjp(lambda x: (_split_chunks(x), None), lambda _, gs: (jnp.concatenate(gs, axis=0),))


@jax.custom_vjp
def _merge_chunks(xs):
    return jnp.concatenate(xs, axis=0)


_merge_chunks.defvjp(lambda xs: (_merge_chunks(xs), None), lambda _, g: (_split_chunks(g),))


def _blocks(x):
    return [_split_heads(rows) for rows in _split_chunks(x)]


def _per_chunk_rows(per_chunk, rid):
    out = per_chunk[0]
    for i in range(1, len(per_chunk)):
        out = jnp.where(rid >= i * CHUNK, per_chunk[i], out)
    return out


HEADS = range(N_HEADS)
CHUNKS_PER_STEP = 4
ML_CHUNKS_PER_STEP = 1


def _hg_chunk(hq, hf, hi, hgate, l0, l1, nw, sts):
    n = hq.shape[0] // CHUNK
    row, col = _chunk_masks(n * CHUNK)
    same_chunk = functools.reduce(jnp.logical_or, [(row >= i * CHUNK) & (row < (i + 1) * CHUNK) &
                                                   (col >= i * CHUNK) & (col < (i + 1) * CHUNK) for i in range(n)])
    causal = _chunk_masks(CHUNK)
    causal = causal[1] <= causal[0]
    mx = lax.stop_gradient(jnp.maximum(l0, l1))
    e0 = jnp.exp(l0 - mx)
    e1 = jnp.exp(l1 - mx)
    lb = e0 / (e0 + e1)
    sig = jax.nn.sigmoid(hf)
    lf = jnp.log(lb + (1.0 - lb) * sig)
    k = (1.0 - lb) * jax.nn.sigmoid(-hf)
    q = jax.nn.silu(hq)
    b = _dg(((col <= row) & same_chunk).astype(F32), lf, 1, 0, HIGHEST)
    rid = lax.broadcasted_iota(jnp.int32, b.shape, 0)
    pick = lambda r: jnp.sum(jnp.where(rid == r, b, 0.0), axis=0, keepdims=True)
    b_last_c = [pick(i * CHUNK + CHUNK - 1) for i in range(n)]
    b_ref = _per_chunk_rows([pick(i * CHUNK + CHUNK // 2 - 1) for i in range(n)], rid)
    b_last = _per_chunk_rows(b_last_c, rid)
    qa = _blocks(q * jnp.exp(b - b_ref))
    ka = _blocks(k * jnp.exp(b_ref - b))
    qe = _blocks(q * jnp.exp(b))
    kd = _blocks(k * jnp.exp(b_last - b))
    decay = [_split_heads(jnp.exp(b_last_c[i])) for i in range(n)]
    v = _blocks(hi)
    chunks = range(n)
    attn = [[jnp.where(causal, _nt(qa[i][h], ka[i][h]), 0.0) for h in HEADS] for i in chunks]
    intra = [[_nn(attn[i][h], v[i][h]) for h in HEADS] for i in chunks]
    kv = [[_tn(v[i][h], kd[i][h]) for h in HEADS] for i in chunks]
    normed = []
    for i in chunks:
        inter = [_nt(qe[i][h], sts[h]) for h in HEADS]
        sts = tuple(decay[i][h] * sts[h] + kv[i][h] for h in HEADS)
        o = [intra[i][h] + inter[h] for h in HEADS]
        normed.append(_merge_heads(tuple(o[h] * lax.rsqrt(jnp.mean(o[h] * o[h], axis=-1, keepdims=True) + LN_EPS)
                                         for h in HEADS)))
    return _merge_chunks(tuple(normed)) * nw * jax.nn.silu(hgate), sts


def _seg(ref, seg):
    return ref[:, seg * D_GROUP:(seg + 1) * D_GROUP]


def _hgrn2_fwd(proj, logits, norm_w):
    s = proj.shape[0]
    rows = CHUNKS_PER_STEP * CHUNK
    nc = s // rows

    def body(p_ref, lg_ref, nw_ref, y_ref, st_out_ref, st_scr):
        @pl.when(pl.program_id(0) == 0)
        def _():
            st_scr[...] = jnp.zeros_like(st_scr)

        sts = tuple(st_scr[h] for h in HEADS)
        y, sts_new = _hg_chunk(_seg(p_ref, 0), _seg(p_ref, 1), _seg(p_ref, 2), _seg(p_ref, 3),
                               lg_ref[0:1, :], lg_ref[1:2, :], nw_ref[...], sts)
        y_ref[...] = y.astype(y_ref.dtype)
        for h in HEADS:
            st_out_ref[h] = sts[h]
            st_scr[h] = sts_new[h]

    return pl.pallas_call(
        body, name="hgrn2_fwd", grid=(nc,),
        in_specs=[pl.BlockSpec((rows, 4 * D_GROUP), lambda c: (c, 0)),
                  pl.BlockSpec((2, D_GROUP), lambda c: (0, 0)),
                  pl.BlockSpec((1, D_GROUP), lambda c: (0, 0))],
        out_specs=[pl.BlockSpec((rows, D_GROUP), lambda c: (c, 0)),
                   pl.BlockSpec((None, N_HEADS, D_HEAD, D_HEAD), lambda c: (c, 0, 0, 0))],
        out_shape=[jax.ShapeDtypeStruct((s, 2 * D_GROUP), BF16),
                   jax.ShapeDtypeStruct((nc, N_HEADS, D_HEAD, D_HEAD), F32)],
        scratch_shapes=[pltpu.VMEM((N_HEADS, D_HEAD, D_HEAD), F32)],
        compiler_params=_params(("arbitrary",)),
    )(proj, logits, norm_w)


def _hgrn2_bwd(proj, logits, norm_w, states, d_y):
    s = proj.shape[0]
    rows = CHUNKS_PER_STEP * CHUNK
    nc = s // rows

    def body(p_ref, lg_ref, nw_ref, st_ref, dy_ref, dp_ref, dl_ref, dnw_ref, dsum_ref, dst_scr):
        @pl.when(pl.program_id(0) == 0)
        def _():
            dst_scr[...] = jnp.zeros_like(dst_scr)
            dl_ref[...] = jnp.zeros_like(dl_ref)
            dnw_ref[...] = jnp.zeros_like(dnw_ref)
            dsum_ref[...] = jnp.zeros_like(dsum_ref)

        _, vjp = jax.vjp(_hg_chunk, _seg(p_ref, 0), _seg(p_ref, 1), _seg(p_ref, 2), _seg(p_ref, 3),
                         lg_ref[0:1, :], lg_ref[1:2, :], nw_ref[...], tuple(st_ref[h] for h in HEADS))
        d_hq, d_hf, d_hi, d_hg, d_l0, d_l1, d_nw, d_sts = vjp((dy_ref[...], tuple(dst_scr[h] for h in HEADS)))
        for seg, val in enumerate((d_hq, d_hf, d_hi, d_hg)):
            dp_ref[:, seg * D_GROUP:(seg + 1) * D_GROUP] = val.astype(dp_ref.dtype)
            dsum_ref[:, seg * D_GROUP:(seg + 1) * D_GROUP] += jnp.sum(val, axis=0, keepdims=True)
        dl_ref[0:1, :] += d_l0
        dl_ref[1:2, :] += d_l1
        dnw_ref[...] += d_nw
        for h in HEADS:
            dst_scr[h] = d_sts[h]

    rev = lambda c: nc - 1 - c
    return pl.pallas_call(
        body, name="hgrn2_bwd", grid=(nc,),
        in_specs=[pl.BlockSpec((rows, 4 * D_GROUP), lambda c: (rev(c), 0)),
                  pl.BlockSpec((2, D_GROUP), lambda c: (0, 0)),
                  pl.BlockSpec((1, D_GROUP), lambda c: (0, 0)),
                  pl.BlockSpec((None, N_HEADS, D_HEAD, D_HEAD), lambda c: (rev(c), 0, 0, 0)),
                  pl.BlockSpec((rows, D_GROUP), lambda c: (rev(c), 0))],
        out_specs=[pl.BlockSpec((rows, 4 * D_GROUP), lambda c: (rev(c), 0)),
                   pl.BlockSpec((2, D_GROUP), lambda c: (0, 0)),
                   pl.BlockSpec((1, D_GROUP), lambda c: (0, 0)),
                   pl.BlockSpec((1, 4 * D_GROUP), lambda c: (0, 0))],
        out_shape=[jax.ShapeDtypeStruct((s, D_IN_MAIN), BF16), jax.ShapeDtypeStruct((2, D_GROUP), F32),
                   jax.ShapeDtypeStruct((1, D_GROUP), F32), jax.ShapeDtypeStruct((1, 4 * D_GROUP), F32)],
        scratch_shapes=[pltpu.VMEM((N_HEADS, D_HEAD, D_HEAD), F32)],
        compiler_params=_params(("arbitrary",)),
    )(proj, logits, norm_w, states, d_y)


def _gate_column(gates, lane, idx):
    return jnp.sum(jnp.where(lane == idx, gates, 0.0), axis=1, keepdims=True)


def _head_layer_norm(h):
    mu = jnp.mean(h, axis=-1, keepdims=True)
    var = jnp.mean(jnp.square(h - mu), axis=-1, keepdims=True)
    return (h - mu) * lax.rsqrt(var + LN_EPS)


def _ml_chunk(qc, kc, v, mo, gates, nw, cts, ns, ms):
    n = qc.shape[0] // CHUNK
    row, col = _chunk_masks(CHUNK)
    mask = col <= row
    eye = col == row
    to_row = lambda t: jnp.sum(jnp.where(eye, t, 0.0), axis=0, keepdims=True)
    q = _blocks(qc * (D_HEAD ** -0.5))
    k = _blocks(kc)
    vs = _blocks(v)
    gate_rows = _split_chunks(gates)
    lane = lax.broadcasted_iota(jnp.int32, gate_rows[0].shape, 1)
    each = [(i, h) for i in range(n) for h in HEADS]
    on_each = lambda f: {ih: f(*ih) for ih in each}
    ig = on_each(lambda i, h: _gate_column(gate_rows[i], lane, h))
    lf = on_each(lambda i, h: jax.nn.log_sigmoid(_gate_column(gate_rows[i], lane, N_HEADS + h)))
    lf_row = on_each(lambda i, h: to_row(lf[i, h]))
    ig_row = on_each(lambda i, h: to_row(ig[i, h]))
    b_col = on_each(lambda i, h: jnp.sum(jnp.where(mask, lf_row[i, h], 0.0), axis=1, keepdims=True))
    b_row = on_each(lambda i, h: jnp.sum(jnp.where(row <= col, lf[i, h], 0.0), axis=0, keepdims=True))
    g = on_each(lambda i, h: jnp.sum(lf[i, h], axis=0, keepdims=True))
    d = on_each(lambda i, h: jnp.where(mask, b_col[i, h] - b_row[i, h] + ig_row[i, h], -jnp.inf))
    a = on_each(lambda i, h: g[i, h] - b_col[i, h] + ig[i, h])
    m_at = {(0, h): ms[h] for h in HEADS}
    for i, h in each:
        m_at[i + 1, h] = lax.stop_gradient(jnp.maximum(g[i, h] + m_at[i, h], jnp.max(a[i, h], axis=0, keepdims=True)))
    inter = on_each(lambda i, h: b_col[i, h] + m_at[i, h])
    m_t = on_each(lambda i, h: lax.stop_gradient(jnp.maximum(inter[i, h], jnp.max(d[i, h], axis=1, keepdims=True))))
    qk = on_each(lambda i, h: _nt(q[i][h], k[i][h]))
    sc = on_each(lambda i, h: qk[i, h] * jnp.exp(d[i, h] - m_t[i, h]))
    w_inter = on_each(lambda i, h: jnp.exp(inter[i, h] - m_t[i, h]))
    sv = on_each(lambda i, h: _nn(sc[i, h], vs[i][h]))
    decay = on_each(lambda i, h: jnp.exp(g[i, h] + m_at[i, h] - m_at[i + 1, h]))
    wk = on_each(lambda i, h: k[i][h] * jnp.exp(a[i, h] - m_at[i + 1, h]))
    kv = on_each(lambda i, h: _tn(vs[i][h], wk[i, h]))
    normed = []
    for i in range(n):
        qc_state = [_nt(q[i][h], cts[h]) for h in HEADS]
        num = [sv[i, h] + w_inter[i, h] * qc_state[h] for h in HEADS]
        den = [jnp.sum(sc[i, h], axis=1, keepdims=True)
               + w_inter[i, h] * jnp.sum(q[i][h] * ns[h], axis=1, keepdims=True) for h in HEADS]
        hh = [num[h] / jnp.maximum(jnp.abs(den[h]), jnp.exp(-m_t[i, h])) for h in HEADS]
        cts = tuple(decay[i, h] * cts[h] + kv[i, h] for h in HEADS)
        ns = tuple(decay[i, h] * ns[h] + jnp.sum(wk[i, h], axis=0, keepdims=True) for h in HEADS)
        normed.append(_merge_heads(tuple(_head_layer_norm(hh[h]) for h in HEADS)))
    y = jax.nn.sigmoid(mo) * (_merge_chunks(tuple(normed)) * nw)
    return y, cts, ns, tuple(m_at[n, h] for h in HEADS)


def _mlstm_fwd(qk, proj, gates, norm_w, y):
    s = proj.shape[0]
    rows = ML_CHUNKS_PER_STEP * CHUNK
    nc = s // rows

    def body(qk_ref, vo_ref, g_ref, nw_ref, _, y_ref, ct_out, n_out, m_out, ct_scr, n_scr, m_scr):
        @pl.when(pl.program_id(0) == 0)
        def _():
            ct_scr[...] = jnp.zeros_like(ct_scr)
            n_scr[...] = jnp.zeros_like(n_scr)
            m_scr[...] = jnp.full(m_scr.shape, NEG_BIG, F32)

        cts = tuple(ct_scr[h] for h in HEADS)
        ns = tuple(n_scr[h] for h in HEADS)
        ms = tuple(m_scr[h] for h in HEADS)
        y, cts_new, ns_new, ms_new = _ml_chunk(_seg(qk_ref, 0), _seg(qk_ref, 1), _seg(vo_ref, 0), _seg(vo_ref, 1),
                                               g_ref[...], nw_ref[...], cts, ns, ms)
        y_ref[...] = y.astype(y_ref.dtype)
        for h in HEADS:
            ct_out[h], n_out[h], m_out[h] = cts[h], ns[h], ms[h]
            ct_scr[h], n_scr[h], m_scr[h] = cts_new[h], ns_new[h], ms_new[h]

    st = lambda r, w: pl.BlockSpec((None, N_HEADS, r, w), lambda c: (c, 0, 0, 0))
    return pl.pallas_call(
        body, name="mlstm_fwd", grid=(nc,),
        in_specs=[pl.BlockSpec((rows, 2 * D_GROUP), lambda c: (c, 0)),
                  pl.BlockSpec((rows, 2 * D_GROUP), lambda c: (c, 3)),
                  pl.BlockSpec((rows, LANES), lambda c: (c, 0)),
                  pl.BlockSpec((1, D_GROUP), lambda c: (0, 0)),
                  pl.BlockSpec(memory_space=pl.ANY)],
        out_specs=[pl.BlockSpec((rows, D_GROUP), lambda c: (c, 1)),
                   st(D_HEAD, D_HEAD), st(1, D_HEAD), st(1, 1)],
        out_shape=[jax.ShapeDtypeStruct(y.shape, y.dtype),
                   jax.ShapeDtypeStruct((nc, N_HEADS, D_HEAD, D_HEAD), F32),
                   jax.ShapeDtypeStruct((nc, N_HEADS, 1, D_HEAD), F32),
                   jax.ShapeDtypeStruct((nc, N_HEADS, 1, 1), F32)],
        input_output_aliases={4: 0},
        scratch_shapes=[pltpu.VMEM((N_HEADS, D_HEAD, D_HEAD), F32), pltpu.VMEM((N_HEADS, 1, D_HEAD), F32),
                        pltpu.VMEM((N_HEADS, 1, 1), F32)],
        compiler_params=_params(("arbitrary",)),
    )(qk, proj, gates, norm_w, y)


def _mlstm_bwd(qk, proj, gates, norm_w, ct_s, n_s, m_s, d_y, d_proj):
    s = proj.shape[0]
    rows = ML_CHUNKS_PER_STEP * CHUNK
    nc = s // rows

    def body(qk_ref, vo_ref, g_ref, nw_ref, ct_ref, n_ref, m_ref, dy_ref, _,
             dp_ref, dqk_ref, dg_ref, dnw_ref, dsum_ref, dct_scr, dn_scr):
        @pl.when(pl.program_id(0) == 0)
        def _():
            dct_scr[...] = jnp.zeros_like(dct_scr)
            dn_scr[...] = jnp.zeros_like(dn_scr)
            dnw_ref[...] = jnp.zeros_like(dnw_ref)
            dsum_ref[...] = jnp.zeros_like(dsum_ref)

        ms = tuple(m_ref[h] for h in HEADS)
        step = lambda *a: _ml_chunk(*a, ms)[:3]
        _, vjp = jax.vjp(step, _seg(qk_ref, 0), _seg(qk_ref, 1), _seg(vo_ref, 0), _seg(vo_ref, 1), g_ref[...],
                         nw_ref[...], tuple(ct_ref[h] for h in HEADS), tuple(n_ref[h] for h in HEADS))
        d_q, d_k, d_v, d_o, d_gates, d_nw, d_cts, d_ns = vjp(
            (dy_ref[...], tuple(dct_scr[h] for h in HEADS), tuple(dn_scr[h] for h in HEADS)))
        dqk_ref[:, 0:D_GROUP] = d_q
        dqk_ref[:, D_GROUP:2 * D_GROUP] = d_k
        for seg, val in enumerate((d_v, d_o)):
            dp_ref[:, seg * D_GROUP:(seg + 1) * D_GROUP] = val.astype(dp_ref.dtype)
            dsum_ref[:, seg * D_GROUP:(seg + 1) * D_GROUP] += jnp.sum(val, axis=0, keepdims=True)
        dg_ref[...] = d_gates
        dnw_ref[...] += d_nw
        for h in HEADS:
            dct_scr[h] = d_cts[h]
            dn_scr[h] = d_ns[h]

    rev = lambda c: nc - 1 - c
    st = lambda r, w: pl.BlockSpec((None, N_HEADS, r, w), lambda c: (rev(c), 0, 0, 0))
    return pl.pallas_call(
        body, name="mlstm_bwd", grid=(nc,),
        in_specs=[pl.BlockSpec((rows, 2 * D_GROUP), lambda c: (rev(c), 0)),
                  pl.BlockSpec((rows, 2 * D_GROUP), lambda c: (rev(c), 3)),
                  pl.BlockSpec((rows, LANES), lambda c: (rev(c), 0)),
                  pl.BlockSpec((1, D_GROUP), lambda c: (0, 0)),
                  st(D_HEAD, D_HEAD), st(1, D_HEAD), st(1, 1),
                  pl.BlockSpec((rows, D_GROUP), lambda c: (rev(c), 1)),
                  pl.BlockSpec(memory_space=pl.ANY)],
        out_specs=[pl.BlockSpec((rows, 2 * D_GROUP), lambda c: (rev(c), 3)),
                   pl.BlockSpec((rows, 2 * D_GROUP), lambda c: (rev(c), 0)),
                   pl.BlockSpec((rows, LANES), lambda c: (rev(c), 0)),
                   pl.BlockSpec((1, D_GROUP), lambda c: (0, 0)),
                   pl.BlockSpec((1, 2 * D_GROUP), lambda c: (0, 0))],
        out_shape=[jax.ShapeDtypeStruct(d_proj.shape, d_proj.dtype), jax.ShapeDtypeStruct((s, 2 * D_GROUP), F32),
                   jax.ShapeDtypeStruct((s, LANES), F32), jax.ShapeDtypeStruct((1, D_GROUP), F32),
                   jax.ShapeDtypeStruct((1, 2 * D_GROUP), F32)],
        input_output_aliases={8: 0},
        scratch_shapes=[pltpu.VMEM((N_HEADS, D_HEAD, D_HEAD), F32), pltpu.VMEM((N_HEADS, 1, D_HEAD), F32)],
        compiler_params=_params(("arbitrary",)),
    )(qk, proj, gates, norm_w, ct_s, n_s, m_s, d_y, d_proj)


LN_TOKENS = 512
ATT_TOKENS = 256


def _proj_res_ln(a, w, xres, g, b, name):
    s, dm = xres.shape
    k = a.shape[1]
    tb = min(LN_TOKENS, s)

    def body(a_ref, w_ref, x_ref, g_ref, b_ref, z_ref, o_ref):
        z = ALPHA * x_ref[...] + _nn_raw(a_ref[...], w_ref[...])
        z_ref[...] = z
        o_ref[...] = _layer_norm(z, g_ref[...], b_ref[...])

    tok = pl.BlockSpec((tb, dm), lambda i: (i, 0))
    vec = pl.BlockSpec((1, dm), lambda i: (0, 0))
    act = jax.ShapeDtypeStruct((s, dm), F32)
    return pl.pallas_call(
        body, name=name, grid=(s // tb,),
        in_specs=[pl.BlockSpec((tb, k), lambda i: (i, 0)), pl.BlockSpec((k, dm), lambda i: (0, 0)), tok, vec, vec],
        out_specs=[tok, tok], out_shape=[act, act], compiler_params=_params(("parallel",)),
    )(a, w, xres, g, b)


def _ln_bwd_proj(d_out, z, g, b, w, name):
    s, dm = z.shape
    k = w.shape[0]
    tb = min(LN_TOKENS, s)

    def body(do_ref, z_ref, g_ref, b_ref, w_ref, dz_ref, da_ref, dg_ref, db_ref):
        @pl.when(pl.program_id(0) == 0)
        def _():
            dg_ref[...] = jnp.zeros_like(dg_ref)
            db_ref[...] = jnp.zeros_like(db_ref)

        _, vjp = jax.vjp(_layer_norm, z_ref[...], g_ref[...], b_ref[...])
        d_z, d_g, d_b = vjp(do_ref[...])
        dz_ref[...] = d_z
        da_ref[...] = _nt_raw(d_z, w_ref[...])
        dg_ref[...] += d_g
        db_ref[...] += d_b

    tok = pl.BlockSpec((tb, dm), lambda i: (i, 0))
    vec = pl.BlockSpec((1, dm), lambda i: (0, 0))
    return pl.pallas_call(
        body, name=name, grid=(s // tb,),
        in_specs=[tok, tok, vec, vec, pl.BlockSpec((k, dm), lambda i: (0, 0))],
        out_specs=[tok, pl.BlockSpec((tb, k), lambda i: (i, 0)), vec, vec],
        out_shape=[jax.ShapeDtypeStruct((s, dm), F32), jax.ShapeDtypeStruct((s, k), F32),
                   jax.ShapeDtypeStruct((1, dm), F32), jax.ShapeDtypeStruct((1, dm), F32)],
        compiler_params=_params(("arbitrary",)),
    )(d_out, z, g, b, w)


def _proj_loss_tail(a, w, xres, g, b, target):
    s, dm = xres.shape
    k = a.shape[1]
    tb = min(ATT_TOKENS, s)

    def loss_fn(z, gg, bb, tgt):
        err = jnp.square(_layer_norm(z, gg, bb) - tgt)
        return 0.5 * jnp.sum(jnp.mean(err, axis=-1, keepdims=True), axis=0, keepdims=True)

    def body(a_ref, w_ref, x_ref, g_ref, b_ref, t_ref, loss_ref, dz_ref, dg_ref, db_ref):
        @pl.when(pl.program_id(0) == 0)
        def _():
            loss_ref[...] = jnp.zeros_like(loss_ref)
            dg_ref[...] = jnp.zeros_like(dg_ref)
            db_ref[...] = jnp.zeros_like(db_ref)

        z = ALPHA * x_ref[...] + _nn_raw(a_ref[...], w_ref[...])
        tgt = t_ref[...]
        loss, vjp = jax.vjp(lambda zz, gg, bb: loss_fn(zz, gg, bb, tgt), z, g_ref[...], b_ref[...])
        d_z, d_g, d_b = vjp(jnp.ones((1, 1), F32))
        loss_ref[...] += loss
        dz_ref[...] = d_z
        dg_ref[...] += d_g
        db_ref[...] += d_b

    tok = pl.BlockSpec((tb, dm), lambda i: (i, 0))
    vec = pl.BlockSpec((1, dm), lambda i: (0, 0))
    one = pl.BlockSpec((1, 1), lambda i: (0, 0))
    return pl.pallas_call(
        body, name="ffn_down_loss_tail", grid=(s // tb,),
        in_specs=[pl.BlockSpec((tb, k), lambda i: (i, 0)), pl.BlockSpec((k, dm), lambda i: (0, 0)), tok, vec, vec, tok],
        out_specs=[one, tok, vec, vec],
        out_shape=[jax.ShapeDtypeStruct((1, 1), F32), jax.ShapeDtypeStruct((s, dm), F32),
                   jax.ShapeDtypeStruct((1, dm), F32), jax.ShapeDtypeStruct((1, dm), F32)],
        compiler_params=_params(("arbitrary",)),
    )(a, w, xres, g, b, target)


def _att_head(q, k, v):
    sc = _nt(q, k) * (CA_DH ** -0.5)
    return _nn(jax.nn.softmax(sc, axis=-1), v)


def _cross_attention_fwd(x1, kv, wq, wo, g, b):
    s = x1.shape[0]
    tb = min(ATT_TOKENS, s)

    def body(x_ref, kv_ref, wq_ref, wo_ref, g_ref, b_ref, att_ref, z_ref, o_ref):
        x_blk = x_ref[...]
        q = _nn_raw(x_blk, wq_ref[...])
        heads = []
        for h in range(CA_HEADS):
            lo = h * CA_DH
            heads.append(_att_head(q[:, lo:lo + CA_DH], kv_ref[:, lo:lo + CA_DH],
                                   kv_ref[:, D_MODEL + lo:D_MODEL + lo + CA_DH]))
        att = jnp.concatenate(heads, axis=1)
        att_ref[...] = att.astype(att_ref.dtype)
        z = ALPHA * x_blk + _nn_raw(att, wo_ref[...])
        z_ref[...] = z
        o_ref[...] = _layer_norm(z, g_ref[...], b_ref[...])

    tok = pl.BlockSpec((tb, D_MODEL), lambda i: (i, 0))
    mat = pl.BlockSpec((D_MODEL, D_MODEL), lambda i: (0, 0))
    vec = pl.BlockSpec((1, D_MODEL), lambda i: (0, 0))
    act = jax.ShapeDtypeStruct((s, D_MODEL), F32)
    return pl.pallas_call(
        body, name="cross_attention_fwd", grid=(s // tb,),
        in_specs=[tok, pl.BlockSpec((N_MEM, 2 * D_MODEL), lambda i: (0, 0)), mat, mat, vec, vec],
        out_specs=[tok, tok, tok],
        out_shape=[jax.ShapeDtypeStruct((s, D_MODEL), BF16), act, act],
        compiler_params=_params(("parallel",)),
    )(x1, kv, wq, wo, g, b)


def _cross_attention_bwd(d_x2, x1, z2, kv, wq, wo, g, b):
    s = x1.shape[0]
    tb = min(ATT_TOKENS, s)

    def body(dx2_ref, x_ref, z_ref, kv_ref, wq_ref, wo_ref, g_ref, b_ref,
             dx1_ref, dq_ref, dz_ref, dkv_ref, dg_ref, db_ref):
        @pl.when(pl.program_id(0) == 0)
        def _():
            dkv_ref[...] = jnp.zeros_like(dkv_ref)
            dg_ref[...] = jnp.zeros_like(dg_ref)
            db_ref[...] = jnp.zeros_like(db_ref)

        _, ln_vjp = jax.vjp(_layer_norm, z_ref[...], g_ref[...], b_ref[...])
        d_z, d_g, d_b = ln_vjp(dx2_ref[...])
        dg_ref[...] += d_g
        db_ref[...] += d_b
        dz_ref[...] = d_z.astype(dz_ref.dtype)
        d_att = _nt_raw(d_z, wo_ref[...])
        q = _nn_raw(x_ref[...], wq_ref[...])
        d_q = []
        for h in range(CA_HEADS):
            lo = h * CA_DH
            vlo = D_MODEL + lo
            _, vjp = jax.vjp(_att_head, q[:, lo:lo + CA_DH], kv_ref[:, lo:lo + CA_DH], kv_ref[:, vlo:vlo + CA_DH])
            d_qh, d_k, d_v = vjp(d_att[:, lo:lo + CA_DH])
            d_q.append(d_qh)
            dkv_ref[:, lo:lo + CA_DH] += d_k
            dkv_ref[:, vlo:vlo + CA_DH] += d_v
        d_q = jnp.concatenate(d_q, axis=1)
        dq_ref[...] = d_q.astype(dq_ref.dtype)
        dx1_ref[...] = ALPHA * d_z + _nt_raw(d_q, wq_ref[...])

    tok = pl.BlockSpec((tb, D_MODEL), lambda i: (i, 0))
    mem = pl.BlockSpec((N_MEM, 2 * D_MODEL), lambda i: (0, 0))
    mat = pl.BlockSpec((D_MODEL, D_MODEL), lambda i: (0, 0))
    vec = pl.BlockSpec((1, D_MODEL), lambda i: (0, 0))
    low = jax.ShapeDtypeStruct((s, D_MODEL), BF16)
    return pl.pallas_call(
        body, name="cross_attention_bwd", grid=(s // tb,),
        in_specs=[tok, tok, tok, mem, mat, mat, vec, vec], out_specs=[tok, tok, tok, mem, vec, vec],
        out_shape=[jax.ShapeDtypeStruct((s, D_MODEL), F32), low, low,
                   jax.ShapeDtypeStruct((N_MEM, 2 * D_MODEL), F32),
                   jax.ShapeDtypeStruct((1, D_MODEL), F32), jax.ShapeDtypeStruct((1, D_MODEL), F32)],
        compiler_params=_params(("arbitrary",)),
    )(d_x2, x1, z2, kv, wq, wo, g, b)


def _local_step(x, mem, target, w, mid_weights=None, ffn_weights=None, on_ffn_grads=None, on_mid_grads=None,
                on_small_grads=None, on_last_grads=None):
    w = dict(w)
    s = x.shape[0]
    tm = min(512, s)
    tt = min(512, s)
    proj = _matmul_nn(x, w["w_in_main"], w["b_in_main"], min(2048, s), 512, "proj")
    gates = _matmul_nn(x, w["w_in_gate"], w["b_in_gate"], tm, LANES, "proj_gates")
    qk = _ml_conv_fwd(proj, w["ml_conv_w"], w["ml_conv_b"])
    y, hg_states = _hgrn2_fwd(proj, w["hg_lb_logits"], w["hg_norm_w"])
    y, ct_s, n_s, m_s = _mlstm_fwd(qk, proj, gates, w["ml_norm_w"], y)
    if mid_weights is not None:
        w.update(mid_weights(y))
    z1, x1 = _proj_res_ln(y, w["w_out"], x, w["ln1_g"], w["ln1_b"], "out_proj_ln1")
    kv = _matmul_nn(mem, w["ca_wkv"], None, N_MEM, CA_DH, "kv")
    att, z2, x2 = _cross_attention_fwd(x1, kv, w["ca_wq"], w["ca_wo"], w["ln2_g"], w["ln2_b"])
    if ffn_weights is not None:
        w.update(ffn_weights(x2))
    u = _matmul_nn(x2, w["ffn_w_up"], None, min(2048, s), UP_SHARD_P, "ffn_up", BF16)
    hid = _ffn_conv_fwd(u, w["ffn_conv_w"], w["ffn_conv_b"])
    loss, d_z3, d_ln3_g, d_ln3_b = _proj_loss_tail(hid, w["ffn_w_down"], x2, w["ln3_g"], w["ln3_b"], target)
    grads = {"ln3_g": d_ln3_g, "ln3_b": d_ln3_b}
    grads["ffn_w_down"] = _matmul_tn(hid, d_z3, 1536, D_MODEL, tt, "d_w_down")
    d_hid = _matmul_nt([(d_z3, w["ffn_w_down"])], None, 1.0, tm, D_FF_P, "d_hid", BF16)
    d_ug, d_uv, d_cwg, d_cwv, d_cbg, d_cbv = _ffn_conv_bwd(u, w["ffn_conv_w"], w["ffn_conv_b"], d_hid)
    grads["ffn_conv_w"] = jnp.concatenate([d_cwg, d_cwv], axis=-1)
    grads["ffn_conv_b"] = jnp.concatenate([d_cbg, d_cbv], axis=-1)
    half = N_DEV // 2
    d_w_up = _matmul_tn(x2, d_ug, D_MODEL, UP_SHARD_P, tt, "d_w_up_gate", shards=N_DEV, group=half)
    grads["ffn_w_up"] = _matmul_tn(x2, d_uv, D_MODEL, UP_SHARD_P, tt, "d_w_up_val", shards=N_DEV,
                                   shard0=half, group=half, into=d_w_up)
    d_x2 = _matmul_nt([(d_ug, w["ffn_w_up"], 0), (d_uv, w["ffn_w_up"], N_DEV // 2)], d_z3, ALPHA,
                      min(256, s), D_MODEL, "d_x2")
    if on_ffn_grads is not None:
        d_x2 = on_ffn_grads(grads, d_x2)
    d_x1, d_q, d_z2, d_kv, grads["ln2_g"], grads["ln2_b"] = _cross_attention_bwd(
        d_x2, x1, z2, kv, w["ca_wq"], w["ca_wo"], w["ln2_g"], w["ln2_b"])
    grads["ca_wo"] = _matmul_tn(att, d_z2, D_MODEL, D_MODEL, tt, "d_ca_wo")
    grads["ca_wq"] = _matmul_tn(x1, d_q, D_MODEL, D_MODEL, tt, "d_ca_wq")
    grads["ca_wkv"] = _matmul_tn(mem, d_kv, D_MODEL, CA_DH, N_MEM, "d_ca_wkv", shards=N_DEV, group=N_DEV)
    d_z1, d_y, grads["ln1_g"], grads["ln1_b"] = _ln_bwd_proj(d_x1, z1, w["ln1_g"], w["ln1_b"], w["w_out"],
                                                             "ln1_bwd_out_proj")
    grads["w_out"] = _matmul_tn(y, d_z1, D_MODEL, D_MODEL, tt, "d_w_out")
    if on_mid_grads is not None:
        d_y = on_mid_grads(grads, d_y)
    d_proj, grads["hg_lb_logits"], grads["hg_norm_w"], db_hg = _hgrn2_bwd(
        proj, w["hg_lb_logits"], w["hg_norm_w"], hg_states, d_y)
    d_proj, d_qk, d_gates, grads["ml_norm_w"], db_vo = _mlstm_bwd(
        qk, proj, gates, w["ml_norm_w"], ct_s, n_s, m_s, d_y, d_proj)
    d_proj, grads["ml_conv_w"], grads["ml_conv_b"], db_qk = _ml_conv_bwd(
        proj, w["ml_conv_w"], w["ml_conv_b"], d_qk, d_proj)
    grads["b_in_main"] = jnp.concatenate([db_hg, db_qk, db_vo], axis=-1)
    grads["w_in_gate"], grads["b_in_gate"] = _matmul_tn(x, d_gates, D_MODEL, LANES, tt, "d_w_in_gates", colsum=True)
    if on_small_grads is not None:
        d_proj = on_small_grads(grads, loss, d_proj)
    grads["w_in_main"] = _matmul_tn(x, d_proj, D_MODEL, min(2048, D_IN_MAIN), tt, "d_w_in")
    if on_last_grads is not None:
        d_z1 = on_last_grads(grads, d_z1)
    grad_x = _matmul_nt([(d_proj, w["w_in_main"]), (d_gates, w["w_in_gate"])], d_z1, ALPHA, tm, D_MODEL, "d_x")
    return loss, grad_x, grads


HBM_SPEC = pl.BlockSpec(memory_space=pltpu.HBM)


def _coords():
    return lax.axis_index("x"), lax.axis_index("y"), lax.axis_index("c")


def _other_chips(x, y):
    return [(1 - x, y), (x, 1 - y), (1 - x, 1 - y)]


def _all_gather_two_level(shards, name):
    na = len(shards)

    def body(*refs):
        x_refs, out_refs = refs[:na], refs[na:2 * na]
        send_sems, recv_sems, local_sems = refs[2 * na:]
        x, y, c = _coords()
        me, sibling = (x, y, c), (x, y, 1 - c)
        chips = _other_chips(x, y)

        def copy(a, k, block, to, own=False):
            slot = out_refs[a].at[4 * block[0] + 2 * block[1] + block[2]]
            return pltpu.make_async_remote_copy(
                src_ref=x_refs[a] if own else slot, dst_ref=slot,
                send_sem=send_sems.at[7 * a + k], recv_sem=recv_sems.at[7 * a + k],
                device_id=to, device_id_type=MESH)

        mine = [pltpu.make_async_copy(x_refs[a], out_refs[a].at[4 * x + 2 * y + c], local_sems.at[a])
                for a in range(na)]
        for cp in mine:
            cp.start()
        first = []
        for a in range(na):
            first.append(copy(a, 0, me, sibling, own=True))
            first += [copy(a, 1 + j, me, (*chip, c), own=True) for j, chip in enumerate(chips)]
        for cp in first:
            cp.start()
        passed = []
        for j, chip in enumerate(chips):
            for a in range(na):
                copy(a, 1 + j, (*chip, c), me).wait_recv()
                fwd = copy(a, 4 + j, (*chip, c), sibling)
                fwd.start()
                passed.append(fwd)
        for a in range(na):
            copy(a, 0, sibling, me).wait_recv()
            for j, chip in enumerate(chips):
                copy(a, 4 + j, (*chip, 1 - c), me).wait_recv()
        for cp in first + passed:
            cp.wait_send()
        for cp in mine:
            cp.wait()

    return pl.pallas_call(
        body, name=name,
        out_shape=[jax.ShapeDtypeStruct((N_DEV,) + t.shape, t.dtype) for t in shards],
        in_specs=[HBM_SPEC] * na, out_specs=[HBM_SPEC] * na,
        scratch_shapes=[pltpu.SemaphoreType.DMA((7 * na,)), pltpu.SemaphoreType.DMA((7 * na,)),
                        pltpu.SemaphoreType.DMA((na,))],
    )(*shards)


SEM_SPEC = pl.BlockSpec(memory_space=pltpu.SEMAPHORE)
ANY_SPEC = pl.BlockSpec(memory_space=pl.ANY)
SIDE_EFFECT = pltpu.SideEffectType.DATAFLOW_SIDE_EFFECTING


def _peer(x, y, c, d):
    flip = lambda v, bit: 1 - v if bit else v
    p = (flip(x, d & 4), flip(y, d & 2), flip(c, d & 1))
    return p, 4 * p[0] + 2 * p[1] + p[2]


def _direct_copies(gather, src_refs, land_refs, send_sems, recv_sems):
    x, y, c = _coords()
    me = 4 * x + 2 * y + c
    copies = []
    for a in range(len(src_refs)):
        for d in range(1, N_DEV):
            peer, peer_slot = _peer(x, y, c, d)
            copies.append(pltpu.make_async_remote_copy(
                src_ref=src_refs[a] if gather else src_refs[a].at[peer_slot],
                dst_ref=land_refs[a].at[me] if gather else land_refs[a].at[d - 1],
                send_sem=send_sems.at[7 * a + d - 1], recv_sem=recv_sems.at[7 * a + d - 1],
                device_id=peer, device_id_type=MESH))
    return copies


def _hbm(t):
    return pltpu.HBM(t.shape, t.dtype)


def _direct_start(gather, arrays, through, name):
    na = len(arrays)
    lands = [lax.empty((N_DEV,) + t.shape if gather else (N_DEV - 1,) + t.shape[1:], t.dtype) for t in arrays]
    n_io = 2 * na + 1

    def body(*refs):
        for cp in _direct_copies(gather, refs[:na], refs[na:2 * na], refs[n_io], refs[n_io + 1]):
            cp.start()

    ins = [pltpu.with_memory_space_constraint(t, pltpu.HBM) for t in (*arrays, *lands, through)]
    sems = pltpu.SemaphoreType.DMA((7 * na,))
    res = pl.pallas_call(
        body, name=name, out_shape=(sems, sems, *[_hbm(t) for t in ins]),
        in_specs=[HBM_SPEC] * n_io, out_specs=(SEM_SPEC, SEM_SPEC, *[HBM_SPEC] * n_io),
        input_output_aliases={i: 2 + i for i in range(n_io)},
        compiler_params=pltpu.CompilerParams(has_side_effects=SIDE_EFFECT),
    )(*ins)
    return (res[0], res[1], list(res[2:2 + na]), list(res[2 + na:2 + 2 * na])), res[2 + 2 * na]


def _direct_wait(gather, started, after, name):
    send_sems, recv_sems, arrays, lands = started
    na = len(arrays)

    def body(*refs):
        for cp in _direct_copies(gather, refs[:na], refs[na:2 * na], refs[2 * na], refs[2 * na + 1]):
            cp.wait_send()
            cp.wait_recv()

    res = pl.pallas_call(
        body, name=name, out_shape=tuple(_hbm(t) for t in (*arrays, *lands)),
        in_specs=[HBM_SPEC] * (2 * na) + [SEM_SPEC, SEM_SPEC, ANY_SPEC], out_specs=tuple([HBM_SPEC] * (2 * na)),
        input_output_aliases={i: i for i in range(2 * na)},
        compiler_params=pltpu.CompilerParams(has_side_effects=SIDE_EFFECT),
    )(*arrays, *lands, send_sems, recv_sems, after)
    return list(res[:na]), list(res[na:])


def _row_tile(rows):
    for t in (256, 176, 128):
        if rows % t == 0 and rows > t:
            return t
    return rows


def _adamw_math(g, w, m, v):
    m_new = ADAM_B1 * m + (1.0 - ADAM_B1) * g
    v_new = ADAM_B2 * v + (1.0 - ADAM_B2) * jnp.square(g)
    m_hat = m_new / (1.0 - ADAM_B1 ** ADAM_STEP)
    v_hat = v_new / (1.0 - ADAM_B2 ** ADAM_STEP)
    delta = -ADAM_LR * (m_hat / (jnp.sqrt(v_hat) + ADAM_EPS) + ADAM_WD * w)
    return delta, m_new, v_new


def _adamw_sharded(chip, sums, got, w, m, v, name):
    r, c = w.shape
    tr = _row_tile(r)
    n_got = got.shape[0]

    def body(chip_ref, s_ref, g_ref, w_ref, m_ref, v_ref, go_ref, d_ref, nm_ref, nv_ref):
        g = s_ref[...].astype(F32)
        for i in range(n_got):
            g = g + g_ref[i].astype(F32)
        delta, m_new, v_new = _adamw_math(g, w_ref[...], m_ref[...], v_ref[...])
        go_ref[...] = g
        d_ref[...] = delta
        nm_ref[...] = m_new
        nv_ref[...] = v_new

    blk = pl.BlockSpec((tr, c), lambda i, chip_ref: (i, 0))
    out = jax.ShapeDtypeStruct((r, c), F32)
    return pl.pallas_call(
        body, name=name,
        grid_spec=pltpu.PrefetchScalarGridSpec(
            num_scalar_prefetch=1, grid=(r // tr,),
            in_specs=[pl.BlockSpec((None, tr, c), lambda i, chip_ref: (chip_ref[0], i, 0)),
                      pl.BlockSpec((n_got, tr, c), lambda i, chip_ref: (0, i, 0)), blk, blk, blk],
            out_specs=[blk, blk, blk, blk]),
        out_shape=[out, out, out, out],
        compiler_params=_params(("parallel",)),
    )(chip, sums, got, w, m, v)


def _adamw_replicated(parts, w, m, v):
    p, r, c = parts.shape

    def body(p_ref, w_ref, m_ref, v_ref, g_ref, d_ref, nm_ref, nv_ref):
        g = p_ref[0]
        for i in range(1, p):
            g = g + p_ref[i]
        delta, m_new, v_new = _adamw_math(g, w_ref[...], m_ref[...], v_ref[...])
        g_ref[...] = g
        d_ref[...] = delta
        nm_ref[...] = m_new
        nv_ref[...] = v_new

    blk = pl.BlockSpec((r, c), lambda i: (0, 0))
    out = jax.ShapeDtypeStruct((r, c), F32)
    return pl.pallas_call(
        body, name="adamw_replicated", grid=(1,),
        in_specs=[pl.BlockSpec((p, r, c), lambda i: (0, 0, 0)), blk, blk, blk],
        out_specs=[blk, blk, blk, blk], out_shape=[out, out, out, out],
        compiler_params=_params(("arbitrary",)),
    )(parts, w, m, v)


SHARDED_NAMES = ("w_in", "ml_conv_w", "w_out", "ca_wq", "ca_wkv", "ca_wo", "ffn_w_up", "ffn_conv_w", "ffn_w_down")
SMALL_NAMES = ("b_in", "hg_lb_logits", "hg_norm_w", "ml_conv_b", "ml_norm_w", "ln1_g", "ln1_b",
               "ln2_g", "ln2_b", "ffn_conv_b", "ln3_g", "ln3_b")
WEIGHT_NAMES = ("w_in", "b_in", "hg_lb_logits", "hg_norm_w", "ml_conv_w", "ml_conv_b", "ml_norm_w", "w_out",
                "ln1_g", "ln1_b", "ca_wq", "ca_wkv", "ca_wo", "ln2_g", "ln2_b", "ffn_w_up", "ffn_conv_w",
                "ffn_conv_b", "ffn_w_down", "ln3_g", "ln3_b")
PAD_TO = {"w_in": W_IN_SHARD_P, "ffn_w_up": UP_SHARD_P, "ffn_conv_w": UP_SHARD_P}
SMALL_ROWS = 24
SMALL_W = D_MODEL


def _shard_2d(name, block):
    t = block[0]
    if name in PAD_TO:
        t = jnp.pad(t, ((0, 0), (0, PAD_TO[name] - t.shape[1])))
    return t


def _shard_like(name, t, like):
    return t[:, :like.shape[2]][None]


def _pad_cols(t, width):
    return jnp.pad(t, ((0, 0), (0, width - t.shape[1])))


FIRST_NAMES = ("w_in", "ml_conv_w")
FFN_NAMES = ("ffn_w_up", "ffn_w_down", "ffn_conv_w")
MID_NAMES = ("ca_wo", "ca_wq", "ca_wkv", "w_out")


def _first_weights(g, small):
    w = dict(small)
    w_in = jnp.concatenate([g["w_in"][j, :, :W_IN_SHARD] for j in range(N_DEV)], axis=1)
    w["w_in_main"] = w_in[:, :D_IN_MAIN]
    w["w_in_gate"] = _pad_cols(w_in[:, D_IN_MAIN:], LANES)
    w["b_in_main"] = small["b_in"][:, :D_IN_MAIN]
    w["b_in_gate"] = _pad_cols(small["b_in"][:, D_IN_MAIN:], LANES)
    w["ml_conv_w"] = jnp.transpose(g["ml_conv_w"], (1, 0, 2)).reshape(ML_CONV, 2 * D_GROUP)
    return w


def _mid_weights(g):
    w = {n: g[n].reshape(D_MODEL, D_MODEL) for n in ("w_out", "ca_wq", "ca_wo")}
    w["ca_wkv"] = g["ca_wkv"]
    return w


def _ffn_weights(g, small):
    w = {"ffn_w_up": g["ffn_w_up"]}
    down = g["ffn_w_down"].reshape(N_DEV // 2, UP_SHARD, D_MODEL)
    w["ffn_w_down"] = jnp.pad(down, ((0, 0), (0, UP_SHARD_P - UP_SHARD), (0, 0))).reshape(D_FF_P, D_MODEL)
    w["ffn_conv_w"] = jnp.transpose(g["ffn_conv_w"], (1, 0, 2)).reshape(FFN_CONV, D_UP_P)
    w["ffn_conv_b"] = _pad_cols(small["ffn_conv_b"].reshape(N_DEV, UP_SHARD), UP_SHARD_P).reshape(1, D_UP_P)
    return w


def _whole_weights(g, small):
    return {**_first_weights(g, small), **_mid_weights(g), **_ffn_weights(g, small)}


def _owner_stack(n, grads):
    if n == "w_in":
        w_in = jnp.concatenate([grads["w_in_main"], grads["w_in_gate"][:, :D_IN - D_IN_MAIN]], axis=1)
        return jnp.stack([_pad_cols(w_in[:, j * W_IN_SHARD:(j + 1) * W_IN_SHARD], W_IN_SHARD_P)
                          for j in range(N_DEV)])
    if n in ("w_out", "ca_wq", "ca_wo"):
        return grads[n].reshape(N_DEV, D_MODEL // N_DEV, D_MODEL)
    if n == "ffn_w_down":
        down = grads[n].reshape(N_DEV // 2, UP_SHARD_P, D_MODEL)[:, :UP_SHARD]
        return down.reshape(N_DEV, D_FF // N_DEV, D_MODEL)
    if n == "ml_conv_w":
        return jnp.transpose(grads[n].reshape(ML_CONV, N_DEV, LANES), (1, 0, 2))
    if n == "ffn_conv_w":
        return jnp.transpose(grads[n].reshape(FFN_CONV, N_DEV, UP_SHARD_P), (1, 0, 2))
    return grads[n]


def _owner_stacks(grads):
    return {n: _owner_stack(n, grads) for n in SHARDED_NAMES}


def _small_grads(grads):
    out = {n: grads[n] for n in SMALL_NAMES if n in grads}
    out["b_in"] = jnp.concatenate([grads["b_in_main"], grads["b_in_gate"][:, :D_IN - D_IN_MAIN]], axis=1)
    out["ffn_conv_b"] = grads["ffn_conv_b"].reshape(N_DEV, UP_SHARD_P)[:, :UP_SHARD].reshape(1, D_UP)
    return out


def _pack_small(p, extra=None):
    flat = [p[n].reshape(-1) for n in SMALL_NAMES]
    if extra is not None:
        flat.append(extra.reshape(-1))
    flat = jnp.concatenate(flat)
    return jnp.pad(flat, (0, SMALL_ROWS * SMALL_W - flat.shape[0])).reshape(SMALL_ROWS, SMALL_W)


def _unpack_small(slab, like):
    out = {}
    flat = slab.reshape(-1)
    o = 0
    for n in SMALL_NAMES:
        out[n] = flat[o:o + like[n].size].reshape(like[n].shape)
        o += like[n].size
    return out, flat[o]


def kernel(x, mem, w_in, b_in, hg_lb_logits, hg_norm_w, ml_conv_w, ml_conv_b, ml_norm_w, w_out, ln1_g, ln1_b, ca_wq, ca_wkv, ca_wo, ln2_g, ln2_b, ffn_w_up, ffn_conv_w, ffn_conv_b, ffn_w_down, ln3_g, ln3_b, loss_target, m_w_in, m_b_in, m_hg_lb_logits, m_hg_norm_w, m_ml_conv_w, m_ml_conv_b, m_ml_norm_w, m_w_out, m_ln1_g, m_ln1_b, m_ca_wq, m_ca_wkv, m_ca_wo, m_ln2_g, m_ln2_b, m_ffn_w_up, m_ffn_conv_w, m_ffn_conv_b, m_ffn_w_down, m_ln3_g, m_ln3_b, v_w_in, v_b_in, v_hg_lb_logits, v_hg_norm_w, v_ml_conv_w, v_ml_conv_b, v_ml_norm_w, v_w_out, v_ln1_g, v_ln1_b, v_ca_wq, v_ca_wkv, v_ca_wo, v_ln2_g, v_ln2_b, v_ffn_w_up, v_ffn_conv_w, v_ffn_conv_b, v_ffn_w_down, v_ln3_g, v_ln3_b):
    params = dict(w_in=w_in, b_in=b_in, hg_lb_logits=hg_lb_logits, hg_norm_w=hg_norm_w, ml_conv_w=ml_conv_w,
                  ml_conv_b=ml_conv_b, ml_norm_w=ml_norm_w, w_out=w_out, ln1_g=ln1_g, ln1_b=ln1_b, ca_wq=ca_wq,
                  ca_wkv=ca_wkv, ca_wo=ca_wo, ln2_g=ln2_g, ln2_b=ln2_b, ffn_w_up=ffn_w_up, ffn_conv_w=ffn_conv_w,
                  ffn_conv_b=ffn_conv_b, ffn_w_down=ffn_w_down, ln3_g=ln3_g, ln3_b=ln3_b)
    mom1 = dict(w_in=m_w_in, b_in=m_b_in, hg_lb_logits=m_hg_lb_logits, hg_norm_w=m_hg_norm_w,
                ml_conv_w=m_ml_conv_w, ml_conv_b=m_ml_conv_b, ml_norm_w=m_ml_norm_w, w_out=m_w_out, ln1_g=m_ln1_g,
                ln1_b=m_ln1_b, ca_wq=m_ca_wq, ca_wkv=m_ca_wkv, ca_wo=m_ca_wo, ln2_g=m_ln2_g, ln2_b=m_ln2_b,
                ffn_w_up=m_ffn_w_up, ffn_conv_w=m_ffn_conv_w, ffn_conv_b=m_ffn_conv_b, ffn_w_down=m_ffn_w_down,
                ln3_g=m_ln3_g, ln3_b=m_ln3_b)
    mom2 = dict(w_in=v_w_in, b_in=v_b_in, hg_lb_logits=v_hg_lb_logits, hg_norm_w=v_hg_norm_w,
                ml_conv_w=v_ml_conv_w, ml_conv_b=v_ml_conv_b, ml_norm_w=v_ml_norm_w, w_out=v_w_out, ln1_g=v_ln1_g,
                ln1_b=v_ln1_b, ca_wq=v_ca_wq, ca_wkv=v_ca_wkv, ca_wo=v_ca_wo, ln2_g=v_ln2_g, ln2_b=v_ln2_b,
                ffn_w_up=v_ffn_w_up, ffn_conv_w=v_ffn_conv_w, ffn_conv_b=v_ffn_conv_b, ffn_w_down=v_ffn_w_down,
                ln3_g=v_ln3_g, ln3_b=v_ln3_b)

    x_idx, y_idx, c_idx = _coords()
    as_index = lambda v: jnp.reshape(v, (1,)).astype(jnp.int32)
    me = as_index(4 * x_idx + 2 * y_idx + c_idx)
    small_params = {n: params[n] for n in SMALL_NAMES}

    shards = {n: _shard_2d(n, params[n]) for n in SHARDED_NAMES}
    to_send = lambda names: [shards[n] if "conv" in n else shards[n].astype(BF16) for n in names]
    first = dict(zip(FIRST_NAMES, _all_gather_two_level(to_send(FIRST_NAMES), "weights_gather_first")))
    mid_started, through = _direct_start(True, to_send(MID_NAMES), first["w_in"], "weights_gather_start_mid")
    ffn_started, first["w_in"] = _direct_start(True, to_send(FFN_NAMES), through, "weights_gather_start_ffn")

    def gathered_weights(names, started, after, tag):
        mine, lands = _direct_wait(True, started, after, "weights_gather_wait_" + tag)
        return {n: lax.dynamic_update_index_in_dim(land, own, me[0], 0) for n, own, land in zip(names, mine, lands)}

    started, own_stacks = {}, {}

    def start_group(names, tag):
        def hook(grads, through):
            own_stacks[tag] = [_owner_stack(n, grads).astype(BF16) for n in names]
            started[tag], through = _direct_start(False, own_stacks[tag], through, "grads_start_" + tag)
            return through
        return hook

    def start_small(grads, loss, through):
        started["small"], through = _direct_start(True, [_pack_small(_small_grads(grads), loss)], through,
                                                  "small_gather_start")
        return through

    loss, grad_x, grads = _local_step(
        x[0], mem[0], loss_target[0], _first_weights(first, small_params),
        lambda y: _mid_weights(gathered_weights(MID_NAMES, mid_started, y, "mid")),
        lambda x2: _ffn_weights(gathered_weights(FFN_NAMES, ffn_started, x2, "ffn"), small_params),
        start_group(FFN_NAMES, "ffn"), start_group(MID_NAMES, "mid"), start_small, start_group(FIRST_NAMES, "last"))

    sharded_out = {}
    after = grad_x
    for names, tag in ((FFN_NAMES, "ffn"), (MID_NAMES, "mid"), (FIRST_NAMES, "last")):
        _, lands = _direct_wait(False, started[tag], after, "grads_wait_" + tag)
        for n, st, land in zip(names, own_stacks[tag], lands):
            res = _adamw_sharded(me, st, land, shards[n], _shard_2d(n, mom1[n]), _shard_2d(n, mom2[n]), "adamw_" + n)
            sharded_out[n] = [_shard_like(n, t, params[n]) for t in res]
            after = res[0]
    own_small, small_lands = _direct_wait(True, started["small"], after, "small_gather_wait")
    small_parts = lax.dynamic_update_index_in_dim(small_lands[0], own_small[0], me[0], 0)
    small_res = _adamw_replicated(small_parts, _pack_small(params), _pack_small(mom1), _pack_small(mom2))

    outs = []
    total_loss = None
    for k in range(4):
        small, extra = _unpack_small(small_res[k], params)
        if total_loss is None:
            total_loss = extra
        outs.extend(sharded_out[n][k] if n in sharded_out else small[n] for n in WEIGHT_NAMES)
    return (total_loss, grad_x[None], *outs)
```

```python
import functools
import math

import jax
import jax.numpy as jnp
from jax import lax
from jax.experimental import pallas as pl
from jax.experimental.pallas import tpu as pltpu

F32 = jnp.float32
BF16 = jnp.bfloat16
HIGHEST = lax.Precision.HIGHEST
MESH = pl.DeviceIdType.MESH

N_DEV = 8
D_MODEL = 1024
N_MEM = 256
N_HEADS = 4
D_HEAD = 128
D_GROUP = N_HEADS * D_HEAD
CHUNK = 64
ML_CONV = 4
FFN_CONV = 3
D_FF = 2816
D_UP = 2 * D_FF
CA_HEADS = 4
CA_DH = D_MODEL // CA_HEADS
LANES = 128
SUBLANES = 8
D_IN = 8 * D_GROUP + 2 * N_HEADS
D_IN_MAIN = 8 * D_GROUP
W_IN_SHARD = D_IN // N_DEV
W_IN_SHARD_P = 640
UP_SHARD = D_UP // N_DEV
UP_SHARD_P = 768
D_UP_P = N_DEV * UP_SHARD_P
D_FF_P = D_UP_P // 2
ALPHA = 2.0 ** 0.25
LN_EPS = 1e-5
NEG_BIG = -1e30
ADAM_LR = 0.001
ADAM_B1 = 0.9
ADAM_B2 = 0.999
ADAM_EPS = 1e-08
ADAM_WD = 0.01
ADAM_STEP = 10
VMEM_LIMIT = 56 * 1024 * 1024

SEG_HQ, SEG_HF, SEG_HI, SEG_HG, SEG_MQ, SEG_MK, SEG_MV, SEG_MO = (4 * i for i in range(8))


def _params(sem):
    return pltpu.CompilerParams(dimension_semantics=sem, vmem_limit_bytes=VMEM_LIMIT)


def _dg(a, b, ca, cb, precision=None):
    return lax.dot_general(a, b, (((ca,), (cb,)), ((), ())), precision=precision,
                           preferred_element_type=F32)


def _nn_raw(a, b):
    return _dg(a.astype(BF16), b.astype(BF16), 1, 0)


def _nt_raw(a, b):
    return _dg(a.astype(BF16), b.astype(BF16), 1, 1)


def _tn_raw(a, b):
    return _dg(a.astype(BF16), b.astype(BF16), 0, 0)


@jax.custom_vjp
def _nn(a, b):
    return _nn_raw(a, b)


_nn.defvjp(lambda a, b: (_nn_raw(a, b), (a, b)),
           lambda res, g: (_nt_raw(g, res[1]), _tn_raw(res[0], g)))


@jax.custom_vjp
def _nt(a, b):
    return _nt_raw(a, b)


_nt.defvjp(lambda a, b: (_nt_raw(a, b), (a, b)),
           lambda res, g: (_nn_raw(g, res[1]), _tn_raw(g, res[0])))


@jax.custom_vjp
def _tn(a, b):
    return _tn_raw(a, b)


_tn.defvjp(lambda a, b: (_tn_raw(a, b), (a, b)),
           lambda res, g: (_nt_raw(res[1], g), _nn_raw(res[0], g)))


def _layer_norm(z, g, b):
    mu = jnp.mean(z, axis=-1, keepdims=True)
    var = jnp.mean(jnp.square(z - mu), axis=-1, keepdims=True)
    return (z - mu) * lax.rsqrt(var + LN_EPS) * g + b


def _matmul_nn(a, w, bias, tm, tn, name, out_dtype=F32):
    m, k = a.shape
    if w.ndim == 3:
        n = w.shape[0] * w.shape[2]
        assert tn == w.shape[2]
        w_spec = pl.BlockSpec((None, k, tn), lambda i, j: (j, 0, 0))
    else:
        n = w.shape[1]
        w_spec = pl.BlockSpec((k, tn), lambda i, j: (0, j))

    def body(*refs):
        a_ref, w_ref = refs[0], refs[1]
        o_ref = refs[-1]
        acc = _nn_raw(a_ref[...], w_ref[...])
        if bias is not None:
            acc = acc + refs[2][...]
        o_ref[...] = acc.astype(o_ref.dtype)

    in_specs = [pl.BlockSpec((tm, k), lambda i, j: (i, 0)), w_spec]
    args = [a, w]
    if bias is not None:
        in_specs.append(pl.BlockSpec((1, tn), lambda i, j: (0, j)))
        args.append(bias)
    return pl.pallas_call(
        body, name=name, grid=(m // tm, n // tn), in_specs=in_specs,
        out_specs=pl.BlockSpec((tm, tn), lambda i, j: (i, j)),
        out_shape=jax.ShapeDtypeStruct((m, n), out_dtype),
        compiler_params=_params(("parallel", "parallel")),
    )(*args)


def _matmul_nt(pairs, add, scale, tm, tk, name, out_dtype=F32):
    m = pairs[0][0].shape[0]
    k = pairs[0][1].shape[-2]
    groups = []
    in_specs, args = [], []
    for pair in pairs:
        d, w = pair[0], pair[1]
        in_specs.append(pl.BlockSpec((tm, d.shape[1]), lambda i, j: (i, 0)))
        if w.ndim == 3:
            g = d.shape[1] // w.shape[2]
            blk = pair[2] // g
            in_specs.append(pl.BlockSpec((g, tk, w.shape[2]), lambda i, j, blk=blk: (blk, j, 0)))
            groups.append((g, w.shape[2]))
        else:
            in_specs.append(pl.BlockSpec((tk, w.shape[1]), lambda i, j: (j, 0)))
            groups.append(None)
        args += [d, w]
    if add is not None:
        in_specs.append(pl.BlockSpec((tm, tk), lambda i, j: (i, j)))
        args.append(add)

    def body(*refs):
        o_ref = refs[-1]
        acc = None
        for p, grp in enumerate(groups):
            d_ref, w_ref = refs[2 * p], refs[2 * p + 1]
            if grp is None:
                terms = [_nt_raw(d_ref[...], w_ref[...])]
            else:
                terms = [_nt_raw(d_ref[:, g * grp[1]:(g + 1) * grp[1]], w_ref[g]) for g in range(grp[0])]
            for t in terms:
                acc = t if acc is None else acc + t
        if add is not None:
            acc = acc + scale * refs[2 * len(groups)][...]
        o_ref[...] = acc.astype(o_ref.dtype)

    return pl.pallas_call(
        body, name=name, grid=(m // tm, k // tk), in_specs=in_specs,
        out_specs=pl.BlockSpec((tm, tk), lambda i, j: (i, j)),
        out_shape=jax.ShapeDtypeStruct((m, k), out_dtype),
        compiler_params=_params(("parallel", "parallel")),
    )(*args)


def _matmul_tn(a, b, tm, tn, tt, name, shards=None, shard0=0, group=1, into=None, colsum=False):
    t, m = a.shape
    n = b.shape[1]
    assert not colsum or tm == m
    n_in = 2 + (into is not None)
    out_dtype = BF16
    per_step = 1 if shards is None else group
    width = per_step * tn

    def body(*refs):
        a_ref, b_ref = refs[0], refs[1]
        o_ref, acc_ref = refs[n_in], refs[-1]
        first = pl.program_id(2) == 0

        @pl.when(first)
        def _():
            acc_ref[...] = jnp.zeros_like(acc_ref)

        if shards is None:
            acc_ref[...] += _tn_raw(a_ref[...], b_ref[...])
        else:
            lhs = a_ref[...].astype(BF16)
            for g in range(per_step):
                acc_ref[g] += _tn_raw(lhs, b_ref[:, g * tn:(g + 1) * tn])

        @pl.when(pl.program_id(2) == t // tt - 1)
        def _():
            o_ref[...] = acc_ref[...].astype(o_ref.dtype)

        if colsum:
            s_ref = refs[n_in + 1]

            @pl.when(first)
            def _():
                s_ref[...] = jnp.zeros_like(s_ref)

            s_ref[...] += jnp.sum(b_ref[...], axis=0, keepdims=True)

    in_specs = [pl.BlockSpec((tt, tm), lambda i, j, kk: (kk, i)),
                pl.BlockSpec((tt, width), lambda i, j, kk: (kk, j))]
    args = [a, b]
    aliases = {}
    if into is not None:
        in_specs.append(pl.BlockSpec(memory_space=pl.ANY))
        args.append(into)
        aliases = {2: 0}
    if shards is None:
        out_specs = [pl.BlockSpec((tm, tn), lambda i, j, kk: (i, j))]
        out_shape = [jax.ShapeDtypeStruct((m, n), out_dtype)]
        acc = pltpu.VMEM((tm, tn), F32)
    else:
        out_specs = [pl.BlockSpec((per_step, tm, tn), lambda i, j, kk: (shard0 // per_step + j, i, 0))]
        out_shape = [jax.ShapeDtypeStruct((shards, m, tn), out_dtype)]
        acc = pltpu.VMEM((per_step, tm, tn), F32)
    if colsum:
        out_specs.append(pl.BlockSpec((1, tn), lambda i, j, kk: (0, j)))
        out_shape.append(jax.ShapeDtypeStruct((1, n), F32))
    res = pl.pallas_call(
        body, name=name, grid=(m // tm, n // width, t // tt), in_specs=in_specs, out_specs=out_specs,
        out_shape=out_shape, input_output_aliases=aliases, scratch_shapes=[acc],
        compiler_params=_params(("parallel", "parallel", "arbitrary")),
    )(*args)
    return res if colsum else res[0]


ROW_TILE = 64


def _stack(ref, start, rows):
    return ref[pl.ds(start, rows), :].astype(F32).reshape(rows // SUBLANES, SUBLANES, LANES)


def _vreg_rows(ref, n):
    return [jnp.broadcast_to(ref[j:j + 1, :], (SUBLANES, LANES))[None] for j in range(n)]


def _column_total(acc):
    return jnp.sum(acc, axis=0, keepdims=True)


def _conv_fwd_tile(pad_ref, taps_w, bias, r0, rows):
    taps = len(taps_w)
    acc = bias
    for j in range(taps):
        acc = acc + _stack(pad_ref, SUBLANES - (taps - 1 - j) + r0, rows) * taps_w[j]
    return acc


def _conv_grads_tile(pad_ref, dpad_ref, dx_ref, taps_w, dws, r0, rows):
    taps = len(taps_w)
    x_rows = _stack(pad_ref, SUBLANES + r0, rows)
    dx = None
    for j in range(taps):
        d_shifted = _stack(dpad_ref, r0 + (taps - 1 - j), rows)
        term = d_shifted * taps_w[j]
        dx = term if dx is None else dx + term
        dws[j] = dws[j] + jnp.sum(d_shifted * x_rows, axis=0)
    dx_ref[r0:r0 + rows, :] = dx.reshape(rows, LANES).astype(dx_ref.dtype)
    return jnp.sum(dx, axis=0)


def _ml_conv_fwd(proj, conv_w, conv_b):
    s = proj.shape[0]
    nblk = 2 * D_GROUP // LANES

    def body(x_ref, w_ref, b_ref, o_ref, pad_ref):
        pad_ref[0:SUBLANES, :] = jnp.zeros((SUBLANES, LANES), F32)
        pad_ref[SUBLANES:, :] = x_ref[...].astype(F32)
        taps_w, bias = _vreg_rows(w_ref, ML_CONV), _vreg_rows(b_ref, 1)[0]
        for r0 in range(0, s, ROW_TILE):
            rows = min(ROW_TILE, s - r0)
            o_ref[r0:r0 + rows, :] = jax.nn.silu(_conv_fwd_tile(pad_ref, taps_w, bias, r0, rows)).reshape(rows, LANES)

    return pl.pallas_call(
        body, name="ml_conv_fwd", grid=(nblk,),
        in_specs=[pl.BlockSpec((s, LANES), lambda j: (0, SEG_MQ + j)),
                  pl.BlockSpec((ML_CONV, LANES), lambda j: (0, j)),
                  pl.BlockSpec((1, LANES), lambda j: (0, j))],
        out_specs=pl.BlockSpec((s, LANES), lambda j: (0, j)),
        out_shape=jax.ShapeDtypeStruct((s, 2 * D_GROUP), F32),
        scratch_shapes=[pltpu.VMEM((s + SUBLANES, LANES), F32)],
        compiler_params=_params(("parallel",)),
    )(proj, conv_w, conv_b)


def _ml_conv_bwd(proj, conv_w, conv_b, d_qk, d_proj):
    s = proj.shape[0]
    nblk = 2 * D_GROUP // LANES

    def body(x_ref, w_ref, b_ref, dy_ref, _, dx_ref, dw_ref, db_ref, dxs_ref, pad_ref, dpad_ref):
        pad_ref[0:SUBLANES, :] = jnp.zeros((SUBLANES, LANES), F32)
        pad_ref[SUBLANES:, :] = x_ref[...].astype(F32)
        dpad_ref[s:, :] = jnp.zeros((SUBLANES, LANES), F32)
        taps_w, bias = _vreg_rows(w_ref, ML_CONV), _vreg_rows(b_ref, 1)[0]
        db = jnp.zeros((SUBLANES, LANES), F32)
        for r0 in range(0, s, ROW_TILE):
            rows = min(ROW_TILE, s - r0)
            pre = _conv_fwd_tile(pad_ref, taps_w, bias, r0, rows)
            _, vjp = jax.vjp(jax.nn.silu, pre)
            d_pre, = vjp(_stack(dy_ref, r0, rows))
            dpad_ref[r0:r0 + rows, :] = d_pre.reshape(rows, LANES)
            db = db + jnp.sum(d_pre, axis=0)
        db_ref[...] = _column_total(db)
        dws = [jnp.zeros((SUBLANES, LANES), F32) for _ in range(ML_CONV)]
        dx_sum = jnp.zeros((SUBLANES, LANES), F32)
        for r0 in range(0, s, ROW_TILE):
            dx_sum = dx_sum + _conv_grads_tile(pad_ref, dpad_ref, dx_ref, taps_w, dws, r0, min(ROW_TILE, s - r0))
        dxs_ref[...] = _column_total(dx_sum)
        for j in range(ML_CONV):
            dw_ref[j:j + 1, :] = _column_total(dws[j])

    return pl.pallas_call(
        body, name="ml_conv_bwd", grid=(nblk,),
        in_specs=[pl.BlockSpec((s, LANES), lambda j: (0, SEG_MQ + j)),
                  pl.BlockSpec((ML_CONV, LANES), lambda j: (0, j)),
                  pl.BlockSpec((1, LANES), lambda j: (0, j)),
                  pl.BlockSpec((s, LANES), lambda j: (0, j)),
                  pl.BlockSpec(memory_space=pl.ANY)],
        out_specs=[pl.BlockSpec((s, LANES), lambda j: (0, SEG_MQ + j)),
                   pl.BlockSpec((ML_CONV, LANES), lambda j: (0, j)),
                   pl.BlockSpec((1, LANES), lambda j: (0, j)),
                   pl.BlockSpec((1, LANES), lambda j: (0, j))],
        out_shape=[jax.ShapeDtypeStruct(d_proj.shape, d_proj.dtype),
                   jax.ShapeDtypeStruct((ML_CONV, 2 * D_GROUP), F32),
                   jax.ShapeDtypeStruct((1, 2 * D_GROUP), F32),
                   jax.ShapeDtypeStruct((1, 2 * D_GROUP), F32)],
        input_output_aliases={4: 0},
        scratch_shapes=[pltpu.VMEM((s + SUBLANES, LANES), F32), pltpu.VMEM((s + SUBLANES, LANES), F32)],
        compiler_params=_params(("parallel",)),
    )(proj, conv_w, conv_b, d_qk, d_proj)


def _gelu_mul(a, b):
    return jax.nn.gelu(a) * b


GELU_C = math.sqrt(2.0 / math.pi)
GELU_K = 0.044715


def _gelu_mul_grads(a, b, d):
    a2 = a * a
    t = jnp.tanh(GELU_C * (a + GELU_K * (a * a2)))
    cdf = 0.5 * (1.0 + t)
    slope = cdf + (0.5 * GELU_C) * a * (1.0 - t * t) * (1.0 + (3.0 * GELU_K) * a2)
    return d * b * slope, d * (a * cdf)


FFN_BLOCKS = D_FF_P // LANES


def _ffn_conv_fwd(u, conv_w, conv_b):
    s = u.shape[0]

    def body(g_ref, v_ref, wg_ref, wv_ref, bg_ref, bv_ref, o_ref, gpad_ref, vpad_ref):
        for pad_ref, x_ref in ((gpad_ref, g_ref), (vpad_ref, v_ref)):
            pad_ref[0:SUBLANES, :] = jnp.zeros((SUBLANES, LANES), F32)
            pad_ref[SUBLANES:, :] = x_ref[...].astype(F32)
        taps_g, bias_g = _vreg_rows(wg_ref, FFN_CONV), _vreg_rows(bg_ref, 1)[0]
        taps_v, bias_v = _vreg_rows(wv_ref, FFN_CONV), _vreg_rows(bv_ref, 1)[0]
        for r0 in range(0, s, ROW_TILE):
            rows = min(ROW_TILE, s - r0)
            ug = _conv_fwd_tile(gpad_ref, taps_g, bias_g, r0, rows)
            uv = _conv_fwd_tile(vpad_ref, taps_v, bias_v, r0, rows)
            o_ref[r0:r0 + rows, :] = _gelu_mul(ug, uv).reshape(rows, LANES).astype(o_ref.dtype)

    col = lambda off: (lambda j: (0, off + j))
    return pl.pallas_call(
        body, name="ffn_conv_fwd", grid=(FFN_BLOCKS,),
        in_specs=[pl.BlockSpec((s, LANES), col(0)), pl.BlockSpec((s, LANES), col(FFN_BLOCKS)),
                  pl.BlockSpec((FFN_CONV, LANES), col(0)), pl.BlockSpec((FFN_CONV, LANES), col(FFN_BLOCKS)),
                  pl.BlockSpec((1, LANES), col(0)), pl.BlockSpec((1, LANES), col(FFN_BLOCKS))],
        out_specs=pl.BlockSpec((s, LANES), col(0)),
        out_shape=jax.ShapeDtypeStruct((s, D_FF_P), BF16),
        scratch_shapes=[pltpu.VMEM((s + SUBLANES, LANES), F32), pltpu.VMEM((s + SUBLANES, LANES), F32)],
        compiler_params=_params(("parallel",)),
    )(u, u, conv_w, conv_w, conv_b, conv_b)


def _ffn_conv_bwd(u, conv_w, conv_b, d_h):
    s = u.shape[0]

    def body(g_ref, v_ref, wg_ref, wv_ref, bg_ref, bv_ref, dh_ref,
             dug_ref, duv_ref, dwg_ref, dwv_ref, dbg_ref, dbv_ref,
             gpad_ref, vpad_ref, dgpad_ref, dvpad_ref):
        for pad_ref, x_ref in ((gpad_ref, g_ref), (vpad_ref, v_ref)):
            pad_ref[0:SUBLANES, :] = jnp.zeros((SUBLANES, LANES), F32)
            pad_ref[SUBLANES:, :] = x_ref[...].astype(F32)
        dgpad_ref[s:, :] = jnp.zeros((SUBLANES, LANES), F32)
        dvpad_ref[s:, :] = jnp.zeros((SUBLANES, LANES), F32)
        taps_g, bias_g = _vreg_rows(wg_ref, FFN_CONV), _vreg_rows(bg_ref, 1)[0]
        taps_v, bias_v = _vreg_rows(wv_ref, FFN_CONV), _vreg_rows(bv_ref, 1)[0]
        dbg = jnp.zeros((SUBLANES, LANES), F32)
        dbv = jnp.zeros((SUBLANES, LANES), F32)
        for r0 in range(0, s, ROW_TILE):
            rows = min(ROW_TILE, s - r0)
            ug = _conv_fwd_tile(gpad_ref, taps_g, bias_g, r0, rows)
            uv = _conv_fwd_tile(vpad_ref, taps_v, bias_v, r0, rows)
            d_ug, d_uv = _gelu_mul_grads(ug, uv, _stack(dh_ref, r0, rows))
            dgpad_ref[r0:r0 + rows, :] = d_ug.reshape(rows, LANES)
            dvpad_ref[r0:r0 + rows, :] = d_uv.reshape(rows, LANES)
            dbg = dbg + jnp.sum(d_ug, axis=0)
            dbv = dbv + jnp.sum(d_uv, axis=0)
        dbg_ref[...] = _column_total(dbg)
        dbv_ref[...] = _column_total(dbv)
        for pad_ref, dpad_ref, taps_w, dx_ref, dw_ref in ((gpad_ref, dgpad_ref, taps_g, dug_ref, dwg_ref),
                                                          (vpad_ref, dvpad_ref, taps_v, duv_ref, dwv_ref)):
            dws = [jnp.zeros((SUBLANES, LANES), F32) for _ in range(FFN_CONV)]
            for r0 in range(0, s, ROW_TILE):
                _conv_grads_tile(pad_ref, dpad_ref, dx_ref, taps_w, dws, r0, min(ROW_TILE, s - r0))
            for j in range(FFN_CONV):
                dw_ref[j:j + 1, :] = _column_total(dws[j])

    col = lambda off: (lambda j: (0, off + j))
    seq = pl.BlockSpec((s, LANES), col(0))
    return pl.pallas_call(
        body, name="ffn_conv_bwd", grid=(FFN_BLOCKS,),
        in_specs=[pl.BlockSpec((s, LANES), col(0)), pl.BlockSpec((s, LANES), col(FFN_BLOCKS)),
                  pl.BlockSpec((FFN_CONV, LANES), col(0)), pl.BlockSpec((FFN_CONV, LANES), col(FFN_BLOCKS)),
                  pl.BlockSpec((1, LANES), col(0)), pl.BlockSpec((1, LANES), col(FFN_BLOCKS)), seq],
        out_specs=[seq, seq, pl.BlockSpec((FFN_CONV, LANES), col(0)), pl.BlockSpec((FFN_CONV, LANES), col(0)),
                   pl.BlockSpec((1, LANES), col(0)), pl.BlockSpec((1, LANES), col(0))],
        out_shape=[jax.ShapeDtypeStruct((s, D_FF_P), BF16), jax.ShapeDtypeStruct((s, D_FF_P), BF16),
                   jax.ShapeDtypeStruct((FFN_CONV, D_FF_P), F32), jax.ShapeDtypeStruct((FFN_CONV, D_FF_P), F32),
                   jax.ShapeDtypeStruct((1, D_FF_P), F32), jax.ShapeDtypeStruct((1, D_FF_P), F32)],
        scratch_shapes=[pltpu.VMEM((s + SUBLANES, LANES), F32) for _ in range(4)],
        compiler_params=_params(("parallel",)),
    )(u, u, conv_w, conv_w, conv_b, conv_b, d_h)


def _chunk_masks(c):
    row = lax.broadcasted_iota(jnp.int32, (c, c), 0)
    col = lax.broadcasted_iota(jnp.int32, (c, c), 1)
    return row, col


@jax.custom_vjp
def _split_heads(x):
    return tuple(x[:, h * D_HEAD:(h + 1) * D_HEAD] for h in range(N_HEADS))


_split_heads.defvjp(lambda x: (_split_heads(x), None), lambda _, gs: (jnp.concatenate(gs, axis=1),))


@jax.custom_vjp
def _merge_heads(xs):
    return jnp.concatenate(xs, axis=1)


_merge_heads.defvjp(lambda xs: (_merge_heads(xs), None), lambda _, g: (_split_heads(g),))


@jax.custom_vjp
def _split_chunks(x):
    return tuple(x[i * CHUNK:(i + 1) * CHUNK] for i in range(x.shape[0] // CHUNK))


_split_chunks.defvjp(lambda x: (_split_chunks(x), None), lambda _, gs: (jnp.concatenate(gs, axis=0),))


@jax.custom_vjp
def _merge_chunks(xs):
    return jnp.concatenate(xs, axis=0)


_merge_chunks.defvjp(lambda xs: (_merge_chunks(xs), None), lambda _, g: (_split_chunks(g),))


def _blocks(x):
    return [_split_heads(rows) for rows in _split_chunks(x)]


def _per_chunk_rows(per_chunk, rid):
    out = per_chunk[0]
    for i in range(1, len(per_chunk)):
        out = jnp.where(rid >= i * CHUNK, per_chunk[i], out)
    return out


HEADS = range(N_HEADS)
CHUNKS_PER_STEP = 4
ML_CHUNKS_PER_STEP = 1


def _hg_chunk(hq, hf, hi, hgate, l0, l1, nw, sts):
    n = hq.shape[0] // CHUNK
    row, col = _chunk_masks(n * CHUNK)
    same_chunk = functools.reduce(jnp.logical_or, [(row >= i * CHUNK) & (row < (i + 1) * CHUNK) &
                                                   (col >= i * CHUNK) & (col < (i + 1) * CHUNK) for i in range(n)])
    causal = _chunk_masks(CHUNK)
    causal = causal[1] <= causal[0]
    mx = lax.stop_gradient(jnp.maximum(l0, l1))
    e0 = jnp.exp(l0 - mx)
    e1 = jnp.exp(l1 - mx)
    lb = e0 / (e0 + e1)
    sig = jax.nn.sigmoid(hf)
    lf = jnp.log(lb + (1.0 - lb) * sig)
    k = (1.0 - lb) * jax.nn.sigmoid(-hf)
    q = jax.nn.silu(hq)
    b = _dg(((col <= row) & same_chunk).astype(F32), lf, 1, 0, HIGHEST)
    rid = lax.broadcasted_iota(jnp.int32, b.shape, 0)
    pick = lambda r: jnp.sum(jnp.where(rid == r, b, 0.0), axis=0, keepdims=True)
    b_last_c = [pick(i * CHUNK + CHUNK - 1) for i in range(n)]
    b_ref = _per_chunk_rows([pick(i * CHUNK + CHUNK // 2 - 1) for i in range(n)], rid)
    b_last = _per_chunk_rows(b_last_c, rid)
    qa = _blocks(q * jnp.exp(b - b_ref))
    ka = _blocks(k * jnp.exp(b_ref - b))
    qe = _blocks(q * jnp.exp(b))
    kd = _blocks(k * jnp.exp(b_last - b))
    decay = [_split_heads(jnp.exp(b_last_c[i])) for i in range(n)]
    v = _blocks(hi)
    chunks = range(n)
    attn = [[jnp.where(causal, _nt(qa[i][h], ka[i][h]), 0.0) for h in HEADS] for i in chunks]
    intra = [[_nn(attn[i][h], v[i][h]) for h in HEADS] for i in chunks]
    kv = [[_tn(v[i][h], kd[i][h]) for h in HEADS] for i in chunks]
    normed = []
    for i in chunks:
        inter = [_nt(qe[i][h], sts[h]) for h in HEADS]
        sts = tuple(decay[i][h] * sts[h] + kv[i][h] for h in HEADS)
        o = [intra[i][h] + inter[h] for h in HEADS]
        normed.append(_merge_heads(tuple(o[h] * lax.rsqrt(jnp.mean(o[h] * o[h], axis=-1, keepdims=True) + LN_EPS)
                                         for h in HEADS)))
    return _merge_chunks(tuple(normed)) * nw * jax.nn.silu(hgate), sts


def _seg(ref, seg):
    return ref[:, seg * D_GROUP:(seg + 1) * D_GROUP]


def _hgrn2_fwd(proj, logits, norm_w):
    s = proj.shape[0]
    rows = CHUNKS_PER_STEP * CHUNK
    nc = s // rows

    def body(p_ref, lg_ref, nw_ref, y_ref, st_out_ref, st_scr):
        @pl.when(pl.program_id(0) == 0)
        def _():
            st_scr[...] = jnp.zeros_like(st_scr)

        sts = tuple(st_scr[h] for h in HEADS)
        y, sts_new = _hg_chunk(_seg(p_ref, 0), _seg(p_ref, 1), _seg(p_ref, 2), _seg(p_ref, 3),
                               lg_ref[0:1, :], lg_ref[1:2, :], nw_ref[...], sts)
        y_ref[...] = y.astype(y_ref.dtype)
        for h in HEADS:
            st_out_ref[h] = sts[h]
            st_scr[h] = sts_new[h]

    return pl.pallas_call(
        body, name="hgrn2_fwd", grid=(nc,),
        in_specs=[pl.BlockSpec((rows, 4 * D_GROUP), lambda c: (c, 0)),
                  pl.BlockSpec((2, D_GROUP), lambda c: (0, 0)),
                  pl.BlockSpec((1, D_GROUP), lambda c: (0, 0))],
        out_specs=[pl.BlockSpec((rows, D_GROUP), lambda c: (c, 0)),
                   pl.BlockSpec((None, N_HEADS, D_HEAD, D_HEAD), lambda c: (c, 0, 0, 0))],
        out_shape=[jax.ShapeDtypeStruct((s, 2 * D_GROUP), BF16),
                   jax.ShapeDtypeStruct((nc, N_HEADS, D_HEAD, D_HEAD), F32)],
        scratch_shapes=[pltpu.VMEM((N_HEADS, D_HEAD, D_HEAD), F32)],
        compiler_params=_params(("arbitrary",)),
    )(proj, logits, norm_w)


def _hgrn2_bwd(proj, logits, norm_w, states, d_y):
    s = proj.shape[0]
    rows = CHUNKS_PER_STEP * CHUNK
    nc = s // rows

    def body(p_ref, lg_ref, nw_ref, st_ref, dy_ref, dp_ref, dl_ref, dnw_ref, dsum_ref, dst_scr):
        @pl.when(pl.program_id(0) == 0)
        def _():
            dst_scr[...] = jnp.zeros_like(dst_scr)
            dl_ref[...] = jnp.zeros_like(dl_ref)
            dnw_ref[...] = jnp.zeros_like(dnw_ref)
            dsum_ref[...] = jnp.zeros_like(dsum_ref)

        _, vjp = jax.vjp(_hg_chunk, _seg(p_ref, 0), _seg(p_ref, 1), _seg(p_ref, 2), _seg(p_ref, 3),
                         lg_ref[0:1, :], lg_ref[1:2, :], nw_ref[...], tuple(st_ref[h] for h in HEADS))
        d_hq, d_hf, d_hi, d_hg, d_l0, d_l1, d_nw, d_sts = vjp((dy_ref[...], tuple(dst_scr[h] for h in HEADS)))
        for seg, val in enumerate((d_hq, d_hf, d_hi, d_hg)):
            dp_ref[:, seg * D_GROUP:(seg + 1) * D_GROUP] = val.astype(dp_ref.dtype)
            dsum_ref[:, seg * D_GROUP:(seg + 1) * D_GROUP] += jnp.sum(val, axis=0, keepdims=True)
        dl_ref[0:1, :] += d_l0
        dl_ref[1:2, :] += d_l1
        dnw_ref[...] += d_nw
        for h in HEADS:
            dst_scr[h] = d_sts[h]

    rev = lambda c: nc - 1 - c
    return pl.pallas_call(
        body, name="hgrn2_bwd", grid=(nc,),
        in_specs=[pl.BlockSpec((rows, 4 * D_GROUP), lambda c: (rev(c), 0)),
                  pl.BlockSpec((2, D_GROUP), lambda c: (0, 0)),
                  pl.BlockSpec((1, D_GROUP), lambda c: (0, 0)),
                  pl.BlockSpec((None, N_HEADS, D_HEAD, D_HEAD), lambda c: (rev(c), 0, 0, 0)),
                  pl.BlockSpec((rows, D_GROUP), lambda c: (rev(c), 0))],
        out_specs=[pl.BlockSpec((rows, 4 * D_GROUP), lambda c: (rev(c), 0)),
                   pl.BlockSpec((2, D_GROUP), lambda c: (0, 0)),
                   pl.BlockSpec((1, D_GROUP), lambda c: (0, 0)),
                   pl.BlockSpec((1, 4 * D_GROUP), lambda c: (0, 0))],
        out_shape=[jax.ShapeDtypeStruct((s, D_IN_MAIN), BF16), jax.ShapeDtypeStruct((2, D_GROUP), F32),
                   jax.ShapeDtypeStruct((1, D_GROUP), F32), jax.ShapeDtypeStruct((1, 4 * D_GROUP), F32)],
        scratch_shapes=[pltpu.VMEM((N_HEADS, D_HEAD, D_HEAD), F32)],
        compiler_params=_params(("arbitrary",)),
    )(proj, logits, norm_w, states, d_y)


def _gate_column(gates, lane, idx):
    return jnp.sum(jnp.where(lane == idx, gates, 0.0), axis=1, keepdims=True)


def _head_layer_norm(h):
    mu = jnp.mean(h, axis=-1, keepdims=True)
    var = jnp.mean(jnp.square(h - mu), axis=-1, keepdims=True)
    return (h - mu) * lax.rsqrt(var + LN_EPS)


def _ml_chunk(qc, kc, v, mo, gates, nw, cts, ns, ms):
    n = qc.shape[0] // CHUNK
    row, col = _chunk_masks(CHUNK)
    mask = col <= row
    eye = col == row
    to_row = lambda t: jnp.sum(jnp.where(eye, t, 0.0), axis=0, keepdims=True)
    q = _blocks(qc * (D_HEAD ** -0.5))
    k = _blocks(kc)
    vs = _blocks(v)
    gate_rows = _split_chunks(gates)
    lane = lax.broadcasted_iota(jnp.int32, gate_rows[0].shape, 1)
    each = [(i, h) for i in range(n) for h in HEADS]
    on_each = lambda f: {ih: f(*ih) for ih in each}
    ig = on_each(lambda i, h: _gate_column(gate_rows[i], lane, h))
    lf = on_each(lambda i, h: jax.nn.log_sigmoid(_gate_column(gate_rows[i], lane, N_HEADS + h)))
    lf_row = on_each(lambda i, h: to_row(lf[i, h]))
    ig_row = on_each(lambda i, h: to_row(ig[i, h]))
    b_col = on_each(lambda i, h: jnp.sum(jnp.where(mask, lf_row[i, h], 0.0), axis=1, keepdims=True))
    b_row = on_each(lambda i, h: jnp.sum(jnp.where(row <= col, lf[i, h], 0.0), axis=0, keepdims=True))
    g = on_each(lambda i, h: jnp.sum(lf[i, h], axis=0, keepdims=True))
    d = on_each(lambda i, h: jnp.where(mask, b_col[i, h] - b_row[i, h] + ig_row[i, h], -jnp.inf))
    a = on_each(lambda i, h: g[i, h] - b_col[i, h] + ig[i, h])
    m_at = {(0, h): ms[h] for h in HEADS}
    for i, h in each:
        m_at[i + 1, h] = lax.stop_gradient(jnp.maximum(g[i, h] + m_at[i, h], jnp.max(a[i, h], axis=0, keepdims=True)))
    inter = on_each(lambda i, h: b_col[i, h] + m_at[i, h])
    m_t = on_each(lambda i, h: lax.stop_gradient(jnp.maximum(inter[i, h], jnp.max(d[i, h], axis=1, keepdims=True))))
    qk = on_each(lambda i, h: _nt(q[i][h], k[i][h]))
    sc = on_each(lambda i, h: qk[i, h] * jnp.exp(d[i, h] - m_t[i, h]))
    w_inter = on_each(lambda i, h: jnp.exp(inter[i, h] - m_t[i, h]))
    sv = on_each(lambda i, h: _nn(sc[i, h], vs[i][h]))
    decay = on_each(lambda i, h: jnp.exp(g[i, h] + m_at[i, h] - m_at[i + 1, h]))
    wk = on_each(lambda i, h: k[i][h] * jnp.exp(a[i, h] - m_at[i + 1, h]))
    kv = on_each(lambda i, h: _tn(vs[i][h], wk[i, h]))
    normed = []
    for i in range(n):
        qc_state = [_nt(q[i][h], cts[h]) for h in HEADS]
        num = [sv[i, h] + w_inter[i, h] * qc_state[h] for h in HEADS]
        den = [jnp.sum(sc[i, h], axis=1, keepdims=True)
               + w_inter[i, h] * jnp.sum(q[i][h] * ns[h], axis=1, keepdims=True) for h in HEADS]
        hh = [num[h] / jnp.maximum(jnp.abs(den[h]), jnp.exp(-m_t[i, h])) for h in HEADS]
        cts = tuple(decay[i, h] * cts[h] + kv[i, h] for h in HEADS)
        ns = tuple(decay[i, h] * ns[h] + jnp.sum(wk[i, h], axis=0, keepdims=True) for h in HEADS)
        normed.append(_merge_heads(tuple(_head_layer_norm(hh[h]) for h in HEADS)))
    y = jax.nn.sigmoid(mo) * (_merge_chunks(tuple(normed)) * nw)
    return y, cts, ns, tuple(m_at[n, h] for h in HEADS)


def _mlstm_fwd(qk, proj, gates, norm_w, y):
    s = proj.shape[0]
    rows = ML_CHUNKS_PER_STEP * CHUNK
    nc = s // rows

    def body(qk_ref, vo_ref, g_ref, nw_ref, _, y_ref, ct_out, n_out, m_out, ct_scr, n_scr, m_scr):
        @pl.when(pl.program_id(0) == 0)
        def _():
            ct_scr[...] = jnp.zeros_like(ct_scr)
            n_scr[...] = jnp.zeros_like(n_scr)
            m_scr[...] = jnp.full(m_scr.shape, NEG_BIG, F32)

        cts = tuple(ct_scr[h] for h in HEADS)
        ns = tuple(n_scr[h] for h in HEADS)
        ms = tuple(m_scr[h] for h in HEADS)
        y, cts_new, ns_new, ms_new = _ml_chunk(_seg(qk_ref, 0), _seg(qk_ref, 1), _seg(vo_ref, 0), _seg(vo_ref, 1),
                                               g_ref[...], nw_ref[...], cts, ns, ms)
        y_ref[...] = y.astype(y_ref.dtype)
        for h in HEADS:
            ct_out[h], n_out[h], m_out[h] = cts[h], ns[h], ms[h]
            ct_scr[h], n_scr[h], m_scr[h] = cts_new[h], ns_new[h], ms_new[h]

    st = lambda r, w: pl.BlockSpec((None, N_HEADS, r, w), lambda c: (c, 0, 0, 0))
    return pl.pallas_call(
        body, name="mlstm_fwd", grid=(nc,),
        in_specs=[pl.BlockSpec((rows, 2 * D_GROUP), lambda c: (c, 0)),
                  pl.BlockSpec((rows, 2 * D_GROUP), lambda c: (c, 3)),
                  pl.BlockSpec((rows, LANES), lambda c: (c, 0)),
                  pl.BlockSpec((1, D_GROUP), lambda c: (0, 0)),
                  pl.BlockSpec(memory_space=pl.ANY)],
        out_specs=[pl.BlockSpec((rows, D_GROUP), lambda c: (c, 1)),
                   st(D_HEAD, D_HEAD), st(1, D_HEAD), st(1, 1)],
        out_shape=[jax.ShapeDtypeStruct(y.shape, y.dtype),
                   jax.ShapeDtypeStruct((nc, N_HEADS, D_HEAD, D_HEAD), F32),
                   jax.ShapeDtypeStruct((nc, N_HEADS, 1, D_HEAD), F32),
                   jax.ShapeDtypeStruct((nc, N_HEADS, 1, 1), F32)],
        input_output_aliases={4: 0},
        scratch_shapes=[pltpu.VMEM((N_HEADS, D_HEAD, D_HEAD), F32), pltpu.VMEM((N_HEADS, 1, D_HEAD), F32),
                        pltpu.VMEM((N_HEADS, 1, 1), F32)],
        compiler_params=_params(("arbitrary",)),
    )(qk, proj, gates, norm_w, y)


def _mlstm_bwd(qk, proj, gates, norm_w, ct_s, n_s, m_s, d_y, d_proj):
    s = proj.shape[0]
    rows = ML_CHUNKS_PER_STEP * CHUNK
    nc = s // rows

    def body(qk_ref, vo_ref, g_ref, nw_ref, ct_ref, n_ref, m_ref, dy_ref, _,
             dp_ref, dqk_ref, dg_ref, dnw_ref, dsum_ref, dct_scr, dn_scr):
        @pl.when(pl.program_id(0) == 0)
        def _():
            dct_scr[...] = jnp.zeros_like(dct_scr)
            dn_scr[...] = jnp.zeros_like(dn_scr)
            dnw_ref[...] = jnp.zeros_like(dnw_ref)
            dsum_ref[...] = jnp.zeros_like(dsum_ref)

        ms = tuple(m_ref[h] for h in HEADS)
        step = lambda *a: _ml_chunk(*a, ms)[:3]
        _, vjp = jax.vjp(step, _seg(qk_ref, 0), _seg(qk_ref, 1), _seg(vo_ref, 0), _seg(vo_ref, 1), g_ref[...],
                         nw_ref[...], tuple(ct_ref[h] for h in HEADS), tuple(n_ref[h] for h in HEADS))
        d_q, d_k, d_v, d_o, d_gates, d_nw, d_cts, d_ns = vjp(
            (dy_ref[...], tuple(dct_scr[h] for h in HEADS), tuple(dn_scr[h] for h in HEADS)))
        dqk_ref[:, 0:D_GROUP] = d_q
        dqk_ref[:, D_GROUP:2 * D_GROUP] = d_k
        for seg, val in enumerate((d_v, d_o)):
            dp_ref[:, seg * D_GROUP:(seg + 1) * D_GROUP] = val.astype(dp_ref.dtype)
            dsum_ref[:, seg * D_GROUP:(seg + 1) * D_GROUP] += jnp.sum(val, axis=0, keepdims=True)
        dg_ref[...] = d_gates
        dnw_ref[...] += d_nw
        for h in HEADS:
            dct_scr[h] = d_cts[h]
            dn_scr[h] = d_ns[h]

    rev = lambda c: nc - 1 - c
    st = lambda r, w: pl.BlockSpec((None, N_HEADS, r, w), lambda c: (rev(c), 0, 0, 0))
    return pl.pallas_call(
        body, name="mlstm_bwd", grid=(nc,),
        in_specs=[pl.BlockSpec((rows, 2 * D_GROUP), lambda c: (rev(c), 0)),
                  pl.BlockSpec((rows, 2 * D_GROUP), lambda c: (rev(c), 3)),
                  pl.BlockSpec((rows, LANES), lambda c: (rev(c), 0)),
                  pl.BlockSpec((1, D_GROUP), lambda c: (0, 0)),
                  st(D_HEAD, D_HEAD), st(1, D_HEAD), st(1, 1),
                  pl.BlockSpec((rows, D_GROUP), lambda c: (rev(c), 1)),
                  pl.BlockSpec(memory_space=pl.ANY)],
        out_specs=[pl.BlockSpec((rows, 2 * D_GROUP), lambda c: (rev(c), 3)),
                   pl.BlockSpec((rows, 2 * D_GROUP), lambda c: (rev(c), 0)),
                   pl.BlockSpec((rows, LANES), lambda c: (rev(c), 0)),
                   pl.BlockSpec((1, D_GROUP), lambda c: (0, 0)),
                   pl.BlockSpec((1, 2 * D_GROUP), lambda c: (0, 0))],
        out_shape=[jax.ShapeDtypeStruct(d_proj.shape, d_proj.dtype), jax.ShapeDtypeStruct((s, 2 * D_GROUP), F32),
                   jax.ShapeDtypeStruct((s, LANES), F32), jax.ShapeDtypeStruct((1, D_GROUP), F32),
                   jax.ShapeDtypeStruct((1, 2 * D_GROUP), F32)],
        input_output_aliases={8: 0},
        scratch_shapes=[pltpu.VMEM((N_HEADS, D_HEAD, D_HEAD), F32), pltpu.VMEM((N_HEADS, 1, D_HEAD), F32)],
        compiler_params=_params(("arbitrary",)),
    )(qk, proj, gates, norm_w, ct_s, n_s, m_s, d_y, d_proj)


LN_TOKENS = 512
ATT_TOKENS = 256


def _proj_res_ln(a, w, xres, g, b, name):
    s, dm = xres.shape
    k = a.shape[1]
    tb = min(LN_TOKENS, s)

    def body(a_ref, w_ref, x_ref, g_ref, b_ref, z_ref, o_ref):
        z = ALPHA * x_ref[...] + _nn_raw(a_ref[...], w_ref[...])
        z_ref[...] = z
        o_ref[...] = _layer_norm(z, g_ref[...], b_ref[...])

    tok = pl.BlockSpec((tb, dm), lambda i: (i, 0))
    vec = pl.BlockSpec((1, dm), lambda i: (0, 0))
    act = jax.ShapeDtypeStruct((s, dm), F32)
    return pl.pallas_call(
        body, name=name, grid=(s // tb,),
        in_specs=[pl.BlockSpec((tb, k), lambda i: (i, 0)), pl.BlockSpec((k, dm), lambda i: (0, 0)), tok, vec, vec],
        out_specs=[tok, tok], out_shape=[act, act], compiler_params=_params(("parallel",)),
    )(a, w, xres, g, b)


def _ln_bwd_proj(d_out, z, g, b, w, name):
    s, dm = z.shape
    k = w.shape[0]
    tb = min(LN_TOKENS, s)

    def body(do_ref, z_ref, g_ref, b_ref, w_ref, dz_ref, da_ref, dg_ref, db_ref):
        @pl.when(pl.program_id(0) == 0)
        def _():
            dg_ref[...] = jnp.zeros_like(dg_ref)
            db_ref[...] = jnp.zeros_like(db_ref)

        _, vjp = jax.vjp(_layer_norm, z_ref[...], g_ref[...], b_ref[...])
        d_z, d_g, d_b = vjp(do_ref[...])
        dz_ref[...] = d_z
        da_ref[...] = _nt_raw(d_z, w_ref[...])
        dg_ref[...] += d_g
        db_ref[...] += d_b

    tok = pl.BlockSpec((tb, dm), lambda i: (i, 0))
    vec = pl.BlockSpec((1, dm), lambda i: (0, 0))
    return pl.pallas_call(
        body, name=name, grid=(s // tb,),
        in_specs=[tok, tok, vec, vec, pl.BlockSpec((k, dm), lambda i: (0, 0))],
        out_specs=[tok, pl.BlockSpec((tb, k), lambda i: (i, 0)), vec, vec],
        out_shape=[jax.ShapeDtypeStruct((s, dm), F32), jax.ShapeDtypeStruct((s, k), F32),
                   jax.ShapeDtypeStruct((1, dm), F32), jax.ShapeDtypeStruct((1, dm), F32)],
        compiler_params=_params(("arbitrary",)),
    )(d_out, z, g, b, w)


def _proj_loss_tail(a, w, xres, g, b, target):
    s, dm = xres.shape
    k = a.shape[1]
    tb = min(ATT_TOKENS, s)

    def loss_fn(z, gg, bb, tgt):
        err = jnp.square(_layer_norm(z, gg, bb) - tgt)
        return 0.5 * jnp.sum(jnp.mean(err, axis=-1, keepdims=True), axis=0, keepdims=True)

    def body(a_ref, w_ref, x_ref, g_ref, b_ref, t_ref, loss_ref, dz_ref, dg_ref, db_ref):
        @pl.when(pl.program_id(0) == 0)
        def _():
            loss_ref[...] = jnp.zeros_like(loss_ref)
            dg_ref[...] = jnp.zeros_like(dg_ref)
            db_ref[...] = jnp.zeros_like(db_ref)

        z = ALPHA * x_ref[...] + _nn_raw(a_ref[...], w_ref[...])
        tgt = t_ref[...]
        loss, vjp = jax.vjp(lambda zz, gg, bb: loss_fn(zz, gg, bb, tgt), z, g_ref[...], b_ref[...])
        d_z, d_g, d_b = vjp(jnp.ones((1, 1), F32))
        loss_ref[...] += loss
        dz_ref[...] = d_z
        dg_ref[...] += d_g
        db_ref[...] += d_b

    tok = pl.BlockSpec((tb, dm), lambda i: (i, 0))
    vec = pl.BlockSpec((1, dm), lambda i: (0, 0))
    one = pl.BlockSpec((1, 1), lambda i: (0, 0))
    return pl.pallas_call(
        body, name="ffn_down_loss_tail", grid=(s // tb,),
        in_specs=[pl.BlockSpec((tb, k), lambda i: (i, 0)), pl.BlockSpec((k, dm), lambda i: (0, 0)), tok, vec, vec, tok],
        out_specs=[one, tok, vec, vec],
        out_shape=[jax.ShapeDtypeStruct((1, 1), F32), jax.ShapeDtypeStruct((s, dm), F32),
                   jax.ShapeDtypeStruct((1, dm), F32), jax.ShapeDtypeStruct((1, dm), F32)],
        compiler_params=_params(("arbitrary",)),
    )(a, w, xres, g, b, target)


def _att_head(q, k, v):
    sc = _nt(q, k) * (CA_DH ** -0.5)
    return _nn(jax.nn.softmax(sc, axis=-1), v)


def _cross_attention_fwd(x1, kv, wq, wo, g, b):
    s = x1.shape[0]
    tb = min(ATT_TOKENS, s)

    def body(x_ref, kv_ref, wq_ref, wo_ref, g_ref, b_ref, att_ref, z_ref, o_ref):
        x_blk = x_ref[...]
        q = _nn_raw(x_blk, wq_ref[...])
        heads = []
        for h in range(CA_HEADS):
            lo = h * CA_DH
            heads.append(_att_head(q[:, lo:lo + CA_DH], kv_ref[:, lo:lo + CA_DH],
                                   kv_ref[:, D_MODEL + lo:D_MODEL + lo + CA_DH]))
        att = jnp.concatenate(heads, axis=1)
        att_ref[...] = att.astype(att_ref.dtype)
        z = ALPHA * x_blk + _nn_raw(att, wo_ref[...])
        z_ref[...] = z
        o_ref[...] = _layer_norm(z, g_ref[...], b_ref[...])

    tok = pl.BlockSpec((tb, D_MODEL), lambda i: (i, 0))
    mat = pl.BlockSpec((D_MODEL, D_MODEL), lambda i: (0, 0))
    vec = pl.BlockSpec((1, D_MODEL), lambda i: (0, 0))
    act = jax.ShapeDtypeStruct((s, D_MODEL), F32)
    return pl.pallas_call(
        body, name="cross_attention_fwd", grid=(s // tb,),
        in_specs=[tok, pl.BlockSpec((N_MEM, 2 * D_MODEL), lambda i: (0, 0)), mat, mat, vec, vec],
        out_specs=[tok, tok, tok],
        out_shape=[jax.ShapeDtypeStruct((s, D_MODEL), BF16), act, act],
        compiler_params=_params(("parallel",)),
    )(x1, kv, wq, wo, g, b)


def _cross_attention_bwd(d_x2, x1, z2, kv, wq, wo, g, b):
    s = x1.shape[0]
    tb = min(ATT_TOKENS, s)

    def body(dx2_ref, x_ref, z_ref, kv_ref, wq_ref, wo_ref, g_ref, b_ref,
             dx1_ref, dq_ref, dz_ref, dkv_ref, dg_ref, db_ref):
        @pl.when(pl.program_id(0) == 0)
        def _():
            dkv_ref[...] = jnp.zeros_like(dkv_ref)
            dg_ref[...] = jnp.zeros_like(dg_ref)
            db_ref[...] = jnp.zeros_like(db_ref)

        _, ln_vjp = jax.vjp(_layer_norm, z_ref[...], g_ref[...], b_ref[...])
        d_z, d_g, d_b = ln_vjp(dx2_ref[...])
        dg_ref[...] += d_g
        db_ref[...] += d_b
        dz_ref[...] = d_z.astype(dz_ref.dtype)
        d_att = _nt_raw(d_z, wo_ref[...])
        q = _nn_raw(x_ref[...], wq_ref[...])
        d_q = []
        for h in range(CA_HEADS):
            lo = h * CA_DH
            vlo = D_MODEL + lo
            _, vjp = jax.vjp(_att_head, q[:, lo:lo + CA_DH], kv_ref[:, lo:lo + CA_DH], kv_ref[:, vlo:vlo + CA_DH])
            d_qh, d_k, d_v = vjp(d_att[:, lo:lo + CA_DH])
            d_q.append(d_qh)
            dkv_ref[:, lo:lo + CA_DH] += d_k
            dkv_ref[:, vlo:vlo + CA_DH] += d_v
        d_q = jnp.concatenate(d_q, axis=1)
        dq_ref[...] = d_q.astype(dq_ref.dtype)
        dx1_ref[...] = ALPHA * d_z + _nt_raw(d_q, wq_ref[...])

    tok = pl.BlockSpec((tb, D_MODEL), lambda i: (i, 0))
    mem = pl.BlockSpec((N_MEM, 2 * D_MODEL), lambda i: (0, 0))
    mat = pl.BlockSpec((D_MODEL, D_MODEL), lambda i: (0, 0))
    vec = pl.BlockSpec((1, D_MODEL), lambda i: (0, 0))
    low = jax.ShapeDtypeStruct((s, D_MODEL), BF16)
    return pl.pallas_call(
        body, name="cross_attention_bwd", grid=(s // tb,),
        in_specs=[tok, tok, tok, mem, mat, mat, vec, vec], out_specs=[tok, tok, tok, mem, vec, vec],
        out_shape=[jax.ShapeDtypeStruct((s, D_MODEL), F32), low, low,
                   jax.ShapeDtypeStruct((N_MEM, 2 * D_MODEL), F32),
                   jax.ShapeDtypeStruct((1, D_MODEL), F32), jax.ShapeDtypeStruct((1, D_MODEL), F32)],
        compiler_params=_params(("arbitrary",)),
    )(d_x2, x1, z2, kv, wq, wo, g, b)


def _local_step(x, mem, target, w, mid_weights=None, ffn_weights=None, on_ffn_grads=None, on_mid_grads=None,
                on_small_grads=None, on_last_grads=None):
    w = dict(w)
    s = x.shape[0]
    tm = min(512, s)
    tt = min(512, s)
    proj = _matmul_nn(x, w["w_in_main"], w["b_in_main"], min(2048, s), 512, "proj")
    gates = _matmul_nn(x, w["w_in_gate"], w["b_in_gate"], tm, LANES, "proj_gates")
    qk = _ml_conv_fwd(proj, w["ml_conv_w"], w["ml_conv_b"])
    y, hg_states = _hgrn2_fwd(proj, w["hg_lb_logits"], w["hg_norm_w"])
    y, ct_s, n_s, m_s = _mlstm_fwd(qk, proj, gates, w["ml_norm_w"], y)
    if mid_weights is not None:
        w.update(mid_weights(y))
    z1, x1 = _proj_res_ln(y, w["w_out"], x, w["ln1_g"], w["ln1_b"], "out_proj_ln1")
    kv = _matmul_nn(mem, w["ca_wkv"], None, N_MEM, CA_DH, "kv")
    att, z2, x2 = _cross_attention_fwd(x1, kv, w["ca_wq"], w["ca_wo"], w["ln2_g"], w["ln2_b"])
    if ffn_weights is not None:
        w.update(ffn_weights(x2))
    u = _matmul_nn(x2, w["ffn_w_up"], None, min(2048, s), UP_SHARD_P, "ffn_up", BF16)
    hid = _ffn_conv_fwd(u, w["ffn_conv_w"], w["ffn_conv_b"])
    loss, d_z3, d_ln3_g, d_ln3_b = _proj_loss_tail(hid, w["ffn_w_down"], x2, w["ln3_g"], w["ln3_b"], target)
    grads = {"ln3_g": d_ln3_g, "ln3_b": d_ln3_b}
    grads["ffn_w_down"] = _matmul_tn(hid, d_z3, 1536, D_MODEL, tt, "d_w_down")
    d_hid = _matmul_nt([(d_z3, w["ffn_w_down"])], None, 1.0, tm, D_FF_P, "d_hid", BF16)
    d_ug, d_uv, d_cwg, d_cwv, d_cbg, d_cbv = _ffn_conv_bwd(u, w["ffn_conv_w"], w["ffn_conv_b"], d_hid)
    grads["ffn_conv_w"] = jnp.concatenate([d_cwg, d_cwv], axis=-1)
    grads["ffn_conv_b"] = jnp.concatenate([d_cbg, d_cbv], axis=-1)
    half = N_DEV // 2
    d_w_up = _matmul_tn(x2, d_ug, D_MODEL, UP_SHARD_P, tt, "d_w_up_gate", shards=N_DEV, group=half)
    grads["ffn_w_up"] = _matmul_tn(x2, d_uv, D_MODEL, UP_SHARD_P, tt, "d_w_up_val", shards=N_DEV,
                                   shard0=half, group=half, into=d_w_up)
    d_x2 = _matmul_nt([(d_ug, w["ffn_w_up"], 0), (d_uv, w["ffn_w_up"], N_DEV // 2)], d_z3, ALPHA,
                      min(256, s), D_MODEL, "d_x2")
    if on_ffn_grads is not None:
        d_x2 = on_ffn_grads(grads, d_x2)
    d_x1, d_q, d_z2, d_kv, grads["ln2_g"], grads["ln2_b"] = _cross_attention_bwd(
        d_x2, x1, z2, kv, w["ca_wq"], w["ca_wo"], w["ln2_g"], w["ln2_b"])
    grads["ca_wo"] = _matmul_tn(att, d_z2, D_MODEL, D_MODEL, tt, "d_ca_wo")
    grads["ca_wq"] = _matmul_tn(x1, d_q, D_MODEL, D_MODEL, tt, "d_ca_wq")
    grads["ca_wkv"] = _matmul_tn(mem, d_kv, D_MODEL, CA_DH, N_MEM, "d_ca_wkv", shards=N_DEV, group=N_DEV)
    d_z1, d_y, grads["ln1_g"], grads["ln1_b"] = _ln_bwd_proj(d_x1, z1, w["ln1_g"], w["ln1_b"], w["w_out"],
                                                             "ln1_bwd_out_proj")
    grads["w_out"] = _matmul_tn(y, d_z1, D_MODEL, D_MODEL, tt, "d_w_out")
    if on_mid_grads is not None:
        d_y = on_mid_grads(grads, d_y)
    d_proj, grads["hg_lb_logits"], grads["hg_norm_w"], db_hg = _hgrn2_bwd(
        proj, w["hg_lb_logits"], w["hg_norm_w"], hg_states, d_y)
    d_proj, d_qk, d_gates, grads["ml_norm_w"], db_vo = _mlstm_bwd(
        qk, proj, gates, w["ml_norm_w"], ct_s, n_s, m_s, d_y, d_proj)
    d_proj, grads["ml_conv_w"], grads["ml_conv_b"], db_qk = _ml_conv_bwd(
        proj, w["ml_conv_w"], w["ml_conv_b"], d_qk, d_proj)
    grads["b_in_main"] = jnp.concatenate([db_hg, db_qk, db_vo], axis=-1)
    grads["w_in_gate"], grads["b_in_gate"] = _matmul_tn(x, d_gates, D_MODEL, LANES, tt, "d_w_in_gates", colsum=True)
    if on_small_grads is not None:
        d_proj = on_small_grads(grads, loss, d_proj)
    grads["w_in_main"] = _matmul_tn(x, d_proj, D_MODEL, min(2048, D_IN_MAIN), tt, "d_w_in")
    if on_last_grads is not None:
        d_z1 = on_last_grads(grads, d_z1)
    grad_x = _matmul_nt([(d_proj, w["w_in_main"]), (d_gates, w["w_in_gate"])], d_z1, ALPHA, tm, D_MODEL, "d_x")
    return loss, grad_x, grads


HBM_SPEC = pl.BlockSpec(memory_space=pltpu.HBM)


def _coords():
    return lax.axis_index("x"), lax.axis_index("y"), lax.axis_index("c")


def _other_chips(x, y):
    return [(1 - x, y), (x, 1 - y), (1 - x, 1 - y)]


def _all_gather_two_level(shards, name):
    na = len(shards)

    def body(*refs):
        x_refs, out_refs = refs[:na], refs[na:2 * na]
        send_sems, recv_sems, local_sems = refs[2 * na:]
        x, y, c = _coords()
        me, sibling = (x, y, c), (x, y, 1 - c)
        chips = _other_chips(x, y)

        def copy(a, k, block, to, own=False):
            slot = out_refs[a].at[4 * block[0] + 2 * block[1] + block[2]]
            return pltpu.make_async_remote_copy(
                src_ref=x_refs[a] if own else slot, dst_ref=slot,
                send_sem=send_sems.at[7 * a + k], recv_sem=recv_sems.at[7 * a + k],
                device_id=to, device_id_type=MESH)

        mine = [pltpu.make_async_copy(x_refs[a], out_refs[a].at[4 * x + 2 * y + c], local_sems.at[a])
                for a in range(na)]
        for cp in mine:
            cp.start()
        first = []
        for a in range(na):
            first.append(copy(a, 0, me, sibling, own=True))
            first += [copy(a, 1 + j, me, (*chip, c), own=True) for j, chip in enumerate(chips)]
        for cp in first:
            cp.start()
        passed = []
        for j, chip in enumerate(chips):
            for a in range(na):
                copy(a, 1 + j, (*chip, c), me).wait_recv()
                fwd = copy(a, 4 + j, (*chip, c), sibling)
                fwd.start()
                passed.append(fwd)
        for a in range(na):
            copy(a, 0, sibling, me).wait_recv()
            for j, chip in enumerate(chips):
                copy(a, 4 + j, (*chip, 1 - c), me).wait_recv()
        for cp in first + passed:
            cp.wait_send()
        for cp in mine:
            cp.wait()

    return pl.pallas_call(
        body, name=name,
        out_shape=[jax.ShapeDtypeStruct((N_DEV,) + t.shape, t.dtype) for t in shards],
        in_specs=[HBM_SPEC] * na, out_specs=[HBM_SPEC] * na,
        scratch_shapes=[pltpu.SemaphoreType.DMA((7 * na,)), pltpu.SemaphoreType.DMA((7 * na,)),
                        pltpu.SemaphoreType.DMA((na,))],
    )(*shards)


SEM_SPEC = pl.BlockSpec(memory_space=pltpu.SEMAPHORE)
ANY_SPEC = pl.BlockSpec(memory_space=pl.ANY)
SIDE_EFFECT = pltpu.SideEffectType.DATAFLOW_SIDE_EFFECTING


def _peer(x, y, c, d):
    flip = lambda v, bit: 1 - v if bit else v
    p = (flip(x, d & 4), flip(y, d & 2), flip(c, d & 1))
    return p, 4 * p[0] + 2 * p[1] + p[2]


def _direct_copies(gather, src_refs, land_refs, send_sems, recv_sems):
    x, y, c = _coords()
    me = 4 * x + 2 * y + c
    copies = []
    for a in range(len(src_refs)):
        for d in range(1, N_DEV):
            peer, peer_slot = _peer(x, y, c, d)
            copies.append(pltpu.make_async_remote_copy(
                src_ref=src_refs[a] if gather else src_refs[a].at[peer_slot],
                dst_ref=land_refs[a].at[me] if gather else land_refs[a].at[d - 1],
                send_sem=send_sems.at[7 * a + d - 1], recv_sem=recv_sems.at[7 * a + d - 1],
                device_id=peer, device_id_type=MESH))
    return copies


def _hbm(t):
    return pltpu.HBM(t.shape, t.dtype)


def _direct_start(gather, arrays, through, name):
    na = len(arrays)
    lands = [lax.empty((N_DEV,) + t.shape if gather else (N_DEV - 1,) + t.shape[1:], t.dtype) for t in arrays]
    n_io = 2 * na + 1

    def body(*refs):
        for cp in _direct_copies(gather, refs[:na], refs[na:2 * na], refs[n_io], refs[n_io + 1]):
            cp.start()

    ins = [pltpu.with_memory_space_constraint(t, pltpu.HBM) for t in (*arrays, *lands, through)]
    sems = pltpu.SemaphoreType.DMA((7 * na,))
    res = pl.pallas_call(
        body, name=name, out_shape=(sems, sems, *[_hbm(t) for t in ins]),
        in_specs=[HBM_SPEC] * n_io, out_specs=(SEM_SPEC, SEM_SPEC, *[HBM_SPEC] * n_io),
        input_output_aliases={i: 2 + i for i in range(n_io)},
        compiler_params=pltpu.CompilerParams(has_side_effects=SIDE_EFFECT),
    )(*ins)
    return (res[0], res[1], list(res[2:2 + na]), list(res[2 + na:2 + 2 * na])), res[2 + 2 * na]


def _direct_wait(gather, started, after, name):
    send_sems, recv_sems, arrays, lands = started
    na = len(arrays)

    def body(*refs):
        for cp in _direct_copies(gather, refs[:na], refs[na:2 * na], refs[2 * na], refs[2 * na + 1]):
            cp.wait_send()
            cp.wait_recv()

    res = pl.pallas_call(
        body, name=name, out_shape=tuple(_hbm(t) for t in (*arrays, *lands)),
        in_specs=[HBM_SPEC] * (2 * na) + [SEM_SPEC, SEM_SPEC, ANY_SPEC], out_specs=tuple([HBM_SPEC] * (2 * na)),
        input_output_aliases={i: i for i in range(2 * na)},
        compiler_params=pltpu.CompilerParams(has_side_effects=SIDE_EFFECT),
    )(*arrays, *lands, send_sems, recv_sems, after)
    return list(res[:na]), list(res[na:])


def _row_tile(rows):
    for t in (256, 176, 128):
        if rows % t == 0 and rows > t:
            return t
    return rows


def _adamw_math(g, w, m, v):
    m_new = ADAM_B1 * m + (1.0 - ADAM_B1) * g
    v_new = ADAM_B2 * v + (1.0 - ADAM_B2) * jnp.square(g)
    m_hat = m_new / (1.0 - ADAM_B1 ** ADAM_STEP)
    v_hat = v_new / (1.0 - ADAM_B2 ** ADAM_STEP)
    delta = -ADAM_LR * (m_hat / (jnp.sqrt(v_hat) + ADAM_EPS) + ADAM_WD * w)
    return delta, m_new, v_new


def _adamw_sharded(chip, sums, got, w, m, v, name):
    r, c = w.shape
    tr = _row_tile(r)
    n_got = got.shape[0]

    def body(chip_ref, s_ref, g_ref, w_ref, m_ref, v_ref, go_ref, d_ref, nm_ref, nv_ref):
        g = s_ref[...].astype(F32)
        for i in range(n_got):
            g = g + g_ref[i].astype(F32)
        delta, m_new, v_new = _adamw_math(g, w_ref[...], m_ref[...], v_ref[...])
        go_ref[...] = g
        d_ref[...] = delta
        nm_ref[...] = m_new
        nv_ref[...] = v_new

    blk = pl.BlockSpec((tr, c), lambda i, chip_ref: (i, 0))
    out = jax.ShapeDtypeStruct((r, c), F32)
    return pl.pallas_call(
        body, name=name,
        grid_spec=pltpu.PrefetchScalarGridSpec(
            num_scalar_prefetch=1, grid=(r // tr,),
            in_specs=[pl.BlockSpec((None, tr, c), lambda i, chip_ref: (chip_ref[0], i, 0)),
                      pl.BlockSpec((n_got, tr, c), lambda i, chip_ref: (0, i, 0)), blk, blk, blk],
            out_specs=[blk, blk, blk, blk]),
        out_shape=[out, out, out, out],
        compiler_params=_params(("parallel",)),
    )(chip, sums, got, w, m, v)


def _adamw_replicated(parts, w, m, v):
    p, r, c = parts.shape

    def body(p_ref, w_ref, m_ref, v_ref, g_ref, d_ref, nm_ref, nv_ref):
        g = p_ref[0]
        for i in range(1, p):
            g = g + p_ref[i]
        delta, m_new, v_new = _adamw_math(g, w_ref[...], m_ref[...], v_ref[...])
        g_ref[...] = g
        d_ref[...] = delta
        nm_ref[...] = m_new
        nv_ref[...] = v_new

    blk = pl.BlockSpec((r, c), lambda i: (0, 0))
    out = jax.ShapeDtypeStruct((r, c), F32)
    return pl.pallas_call(
        body, name="adamw_replicated", grid=(1,),
        in_specs=[pl.BlockSpec((p, r, c), lambda i: (0, 0, 0)), blk, blk, blk],
        out_specs=[blk, blk, blk, blk], out_shape=[out, out, out, out],
        compiler_params=_params(("arbitrary",)),
    )(parts, w, m, v)


SHARDED_NAMES = ("w_in", "ml_conv_w", "w_out", "ca_wq", "ca_wkv", "ca_wo", "ffn_w_up", "ffn_conv_w", "ffn_w_down")
SMALL_NAMES = ("b_in", "hg_lb_logits", "hg_norm_w", "ml_conv_b", "ml_norm_w", "ln1_g", "ln1_b",
               "ln2_g", "ln2_b", "ffn_conv_b", "ln3_g", "ln3_b")
WEIGHT_NAMES = ("w_in", "b_in", "hg_lb_logits", "hg_norm_w", "ml_conv_w", "ml_conv_b", "ml_norm_w", "w_out",
                "ln1_g", "ln1_b", "ca_wq", "ca_wkv", "ca_wo", "ln2_g", "ln2_b", "ffn_w_up", "ffn_conv_w",
                "ffn_conv_b", "ffn_w_down", "ln3_g", "ln3_b")
PAD_TO = {"w_in": W_IN_SHARD_P, "ffn_w_up": UP_SHARD_P, "ffn_conv_w": UP_SHARD_P}
SMALL_ROWS = 24
SMALL_W = D_MODEL


def _shard_2d(name, block):
    t = block[0]
    if name in PAD_TO:
        t = jnp.pad(t, ((0, 0), (0, PAD_TO[name] - t.shape[1])))
    return t


def _shard_like(name, t, like):
    return t[:, :like.shape[2]][None]


def _pad_cols(t, width):
    return jnp.pad(t, ((0, 0), (0, width - t.shape[1])))


FIRST_NAMES = ("w_in", "ml_conv_w")
FFN_NAMES = ("ffn_w_up", "ffn_w_down", "ffn_conv_w")
MID_NAMES = ("ca_wo", "ca_wq", "ca_wkv", "w_out")


def _first_weights(g, small):
    w = dict(small)
    w_in = jnp.concatenate([g["w_in"][j, :, :W_IN_SHARD] for j in range(N_DEV)], axis=1)
    w["w_in_main"] = w_in[:, :D_IN_MAIN]
    w["w_in_gate"] = _pad_cols(w_in[:, D_IN_MAIN:], LANES)
    w["b_in_main"] = small["b_in"][:, :D_IN_MAIN]
    w["b_in_gate"] = _pad_cols(small["b_in"][:, D_IN_MAIN:], LANES)
    w["ml_conv_w"] = jnp.transpose(g["ml_conv_w"], (1, 0, 2)).reshape(ML_CONV, 2 * D_GROUP)
    return w


def _mid_weights(g):
    w = {n: g[n].reshape(D_MODEL, D_MODEL) for n in ("w_out", "ca_wq", "ca_wo")}
    w["ca_wkv"] = g["ca_wkv"]
    return w


def _ffn_weights(g, small):
    w = {"ffn_w_up": g["ffn_w_up"]}
    down = g["ffn_w_down"].reshape(N_DEV // 2, UP_SHARD, D_MODEL)
    w["ffn_w_down"] = jnp.pad(down, ((0, 0), (0, UP_SHARD_P - UP_SHARD), (0, 0))).reshape(D_FF_P, D_MODEL)
    w["ffn_conv_w"] = jnp.transpose(g["ffn_conv_w"], (1, 0, 2)).reshape(FFN_CONV, D_UP_P)
    w["ffn_conv_b"] = _pad_cols(small["ffn_conv_b"].reshape(N_DEV, UP_SHARD), UP_SHARD_P).reshape(1, D_UP_P)
    return w


def _whole_weights(g, small):
    return {**_first_weights(g, small), **_mid_weights(g), **_ffn_weights(g, small)}


def _owner_stack(n, grads):
    if n == "w_in":
        w_in = jnp.concatenate([grads["w_in_main"], grads["w_in_gate"][:, :D_IN - D_IN_MAIN]], axis=1)
        return jnp.stack([_pad_cols(w_in[:, j * W_IN_SHARD:(j + 1) * W_IN_SHARD], W_IN_SHARD_P)
                          for j in range(N_DEV)])
    if n in ("w_out", "ca_wq", "ca_wo"):
        return grads[n].reshape(N_DEV, D_MODEL // N_DEV, D_MODEL)
    if n == "ffn_w_down":
        down = grads[n].reshape(N_DEV // 2, UP_SHARD_P, D_MODEL)[:, :UP_SHARD]
        return down.reshape(N_DEV, D_FF // N_DEV, D_MODEL)
    if n == "ml_conv_w":
        return jnp.transpose(grads[n].reshape(ML_CONV, N_DEV, LANES), (1, 0, 2))
    if n == "ffn_conv_w":
        return jnp.transpose(grads[n].reshape(FFN_CONV, N_DEV, UP_SHARD_P), (1, 0, 2))
    return grads[n]


def _owner_stacks(grads):
    return {n: _owner_stack(n, grads) for n in SHARDED_NAMES}


def _small_grads(grads):
    out = {n: grads[n] for n in SMALL_NAMES if n in grads}
    out["b_in"] = jnp.concatenate([grads["b_in_main"], grads["b_in_gate"][:, :D_IN - D_IN_MAIN]], axis=1)
    out["ffn_conv_b"] = grads["ffn_conv_b"].reshape(N_DEV, UP_SHARD_P)[:, :UP_SHARD].reshape(1, D_UP)
    return out


def _pack_small(p, extra=None):
    flat = [p[n].reshape(-1) for n in SMALL_NAMES]
    if extra is not None:
        flat.append(extra.reshape(-1))
    flat = jnp.concatenate(flat)
    return jnp.pad(flat, (0, SMALL_ROWS * SMALL_W - flat.shape[0])).reshape(SMALL_ROWS, SMALL_W)


def _unpack_small(slab, like):
    out = {}
    flat = slab.reshape(-1)
    o = 0
    for n in SMALL_NAMES:
        out[n] = flat[o:o + like[n].size].reshape(like[n].shape)
        o += like[n].size
    return out, flat[o]


def kernel(x, mem, w_in, b_in, hg_lb_logits, hg_norm_w, ml_conv_w, ml_conv_b, ml_norm_w, w_out, ln1_g, ln1_b, ca_wq, ca_wkv, ca_wo, ln2_g, ln2_b, ffn_w_up, ffn_conv_w, ffn_conv_b, ffn_w_down, ln3_g, ln3_b, loss_target, m_w_in, m_b_in, m_hg_lb_logits, m_hg_norm_w, m_ml_conv_w, m_ml_conv_b, m_ml_norm_w, m_w_out, m_ln1_g, m_ln1_b, m_ca_wq, m_ca_wkv, m_ca_wo, m_ln2_g, m_ln2_b, m_ffn_w_up, m_ffn_conv_w, m_ffn_conv_b, m_ffn_w_down, m_ln3_g, m_ln3_b, v_w_in, v_b_in, v_hg_lb_logits, v_hg_norm_w, v_ml_conv_w, v_ml_conv_b, v_ml_norm_w, v_w_out, v_ln1_g, v_ln1_b, v_ca_wq, v_ca_wkv, v_ca_wo, v_ln2_g, v_ln2_b, v_ffn_w_up, v_ffn_conv_w, v_ffn_conv_b, v_ffn_w_down, v_ln3_g, v_ln3_b):
    params = dict(w_in=w_in, b_in=b_in, hg_lb_logits=hg_lb_logits, hg_norm_w=hg_norm_w, ml_conv_w=ml_conv_w,
                  ml_conv_b=ml_conv_b, ml_norm_w=ml_norm_w, w_out=w_out, ln1_g=ln1_g, ln1_b=ln1_b, ca_wq=ca_wq,
                  ca_wkv=ca_wkv, ca_wo=ca_wo, ln2_g=ln2_g, ln2_b=ln2_b, ffn_w_up=ffn_w_up, ffn_conv_w=ffn_conv_w,
                  ffn_conv_b=ffn_conv_b, ffn_w_down=ffn_w_down, ln3_g=ln3_g, ln3_b=ln3_b)
    mom1 = dict(w_in=m_w_in, b_in=m_b_in, hg_lb_logits=m_hg_lb_logits, hg_norm_w=m_hg_norm_w,
                ml_conv_w=m_ml_conv_w, ml_conv_b=m_ml_conv_b, ml_norm_w=m_ml_norm_w, w_out=m_w_out, ln1_g=m_ln1_g,
                ln1_b=m_ln1_b, ca_wq=m_ca_wq, ca_wkv=m_ca_wkv, ca_wo=m_ca_wo, ln2_g=m_ln2_g, ln2_b=m_ln2_b,
                ffn_w_up=m_ffn_w_up, ffn_conv_w=m_ffn_conv_w, ffn_conv_b=m_ffn_conv_b, ffn_w_down=m_ffn_w_down,
                ln3_g=m_ln3_g, ln3_b=m_ln3_b)
    mom2 = dict(w_in=v_w_in, b_in=v_b_in, hg_lb_logits=v_hg_lb_logits, hg_norm_w=v_hg_norm_w,
                ml_conv_w=v_ml_conv_w, ml_conv_b=v_ml_conv_b, ml_norm_w=v_ml_norm_w, w_out=v_w_out, ln1_g=v_ln1_g,
                ln1_b=v_ln1_b, ca_wq=v_ca_wq, ca_wkv=v_ca_wkv, ca_wo=v_ca_wo, ln2_g=v_ln2_g, ln2_b=v_ln2_b,
                ffn_w_up=v_ffn_w_up, ffn_conv_w=v_ffn_conv_w, ffn_conv_b=v_ffn_conv_b, ffn_w_down=v_ffn_w_down,
                ln3_g=v_ln3_g, ln3_b=v_ln3_b)

    x_idx, y_idx, c_idx = _coords()
    as_index = lambda v: jnp.reshape(v, (1,)).astype(jnp.int32)
    me = as_index(4 * x_idx + 2 * y_idx + c_idx)
    small_params = {n: params[n] for n in SMALL_NAMES}

    shards = {n: _shard_2d(n, params[n]) for n in SHARDED_NAMES}
    to_send = lambda names: [shards[n] if "conv" in n else shards[n].astype(BF16) for n in names]
    first = dict(zip(FIRST_NAMES, _all_gather_two_level(to_send(FIRST_NAMES), "weights_gather_first")))
    mid_started, through = _direct_start(True, to_send(MID_NAMES), first["w_in"], "weights_gather_start_mid")
    ffn_started, first["w_in"] = _direct_start(True, to_send(FFN_NAMES), through, "weights_gather_start_ffn")

    def gathered_weights(names, started, after, tag):
        mine, lands = _direct_wait(True, started, after, "weights_gather_wait_" + tag)
        return {n: lax.dynamic_update_index_in_dim(land, own, me[0], 0) for n, own, land in zip(names, mine, lands)}

    started, own_stacks = {}, {}

    def start_group(names, tag):
        def hook(grads, through):
            own_stacks[tag] = [_owner_stack(n, grads).astype(BF16) for n in names]
            started[tag], through = _direct_start(False, own_stacks[tag], through, "grads_start_" + tag)
            return through
        return hook

    def start_small(grads, loss, through):
        started["small"], through = _direct_start(True, [_pack_small(_small_grads(grads), loss)], through,
                                                  "small_gather_start")
        return through

    loss, grad_x, grads = _local_step(
        x[0], mem[0], loss_target[0], _first_weights(first, small_params),
        lambda y: _mid_weights(gathered_weights(MID_NAMES, mid_started, y, "mid")),
        lambda x2: _ffn_weights(gathered_weights(FFN_NAMES, ffn_started, x2, "ffn"), small_params),
        start_group(FFN_NAMES, "ffn"), start_group(MID_NAMES, "mid"), start_small, start_group(FIRST_NAMES, "last"))

    sharded_out = {}
    after = grad_x
    for names, tag in ((FFN_NAMES, "ffn"), (MID_NAMES, "mid"), (FIRST_NAMES, "last")):
        _, lands = _direct_wait(False, started[tag], after, "grads_wait_" + tag)
        for n, st, land in zip(names, own_stacks[tag], lands):
            res = _adamw_sharded(me, st, land, shards[n], _shard_2d(n, mom1[n]), _shard_2d(n, mom2[n]), "adamw_" + n)
            sharded_out[n] = [_shard_like(n, t, params[n]) for t in res]
            after = res[0]
    own_small, small_lands = _direct_wait(True, started["small"], after, "small_gather_wait")
    small_parts = lax.dynamic_update_index_in_dim(small_lands[0], own_small[0], me[0], 0)
    small_res = _adamw_replicated(small_parts, _pack_small(params), _pack_small(mom1), _pack_small(mom2))

    outs = []
    total_loss = None
    for k in range(4):
        small, extra = _unpack_small(small_res[k], params)
        if total_loss is None:
            total_loss = extra
        outs.extend(sharded_out[n][k] if n in sharded_out else small[n] for n in WEIGHT_NAMES)
    return (total_loss, grad_x[None], *outs)
```

```python
import functools
import math

import jax
import jax.numpy as jnp
from jax import lax
from jax.experimental import pallas as pl
from jax.experimental.pallas import tpu as pltpu

F32 = jnp.float32
BF16 = jnp.bfloat16
HIGHEST = lax.Precision.HIGHEST
MESH = pl.DeviceIdType.MESH

N_DEV = 8
D_MODEL = 1024
N_MEM = 256
N_HEADS = 4
D_HEAD = 128
D_GROUP = N_HEADS * D_HEAD
CHUNK = 64
ML_CONV = 4
FFN_CONV = 3
D_FF = 2816
D_UP = 2 * D_FF
CA_HEADS = 4
CA_DH = D_MODEL // CA_HEADS
LANES = 128
SUBLANES = 8
D_IN = 8 * D_GROUP + 2 * N_HEADS
D_IN_MAIN = 8 * D_GROUP
W_IN_SHARD = D_IN // N_DEV
W_IN_SHARD_P = 640
UP_SHARD = D_UP // N_DEV
UP_SHARD_P = 768
D_UP_P = N_DEV * UP_SHARD_P
D_FF_P = D_UP_P // 2
ALPHA = 2.0 ** 0.25
LN_EPS = 1e-5
NEG_BIG = -1e30
ADAM_LR = 0.001
ADAM_B1 = 0.9
ADAM_B2 = 0.999
ADAM_EPS = 1e-08
ADAM_WD = 0.01
ADAM_STEP = 10
VMEM_LIMIT = 56 * 1024 * 1024

SEG_HQ, SEG_HF, SEG_HI, SEG_HG, SEG_MQ, SEG_MK, SEG_MV, SEG_MO = (4 * i for i in range(8))


def _params(sem):
    return pltpu.CompilerParams(dimension_semantics=sem, vmem_limit_bytes=VMEM_LIMIT)


def _dg(a, b, ca, cb, precision=None):
    return lax.dot_general(a, b, (((ca,), (cb,)), ((), ())), precision=precision,
                           preferred_element_type=F32)


def _nn_raw(a, b):
    return _dg(a.astype(BF16), b.astype(BF16), 1, 0)


def _nt_raw(a, b):
    return _dg(a.astype(BF16), b.astype(BF16), 1, 1)


def _tn_raw(a, b):
    return _dg(a.astype(BF16), b.astype(BF16), 0, 0)


@jax.custom_vjp
def _nn(a, b):
    return _nn_raw(a, b)


_nn.defvjp(lambda a, b: (_nn_raw(a, b), (a, b)),
           lambda res, g: (_nt_raw(g, res[1]), _tn_raw(res[0], g)))


@jax.custom_vjp
def _nt(a, b):
    return _nt_raw(a, b)


_nt.defvjp(lambda a, b: (_nt_raw(a, b), (a, b)),
           lambda res, g: (_nn_raw(g, res[1]), _tn_raw(g, res[0])))


@jax.custom_vjp
def _tn(a, b):
    return _tn_raw(a, b)


_tn.defvjp(lambda a, b: (_tn_raw(a, b), (a, b)),
           lambda res, g: (_nt_raw(res[1], g), _nn_raw(res[0], g)))


def _layer_norm(z, g, b):
    mu = jnp.mean(z, axis=-1, keepdims=True)
    var = jnp.mean(jnp.square(z - mu), axis=-1, keepdims=True)
    return (z - mu) * lax.rsqrt(var + LN_EPS) * g + b


def _matmul_nn(a, w, bias, tm, tn, name, out_dtype=F32):
    m, k = a.shape
    if w.ndim == 3:
        n = w.shape[0] * w.shape[2]
        assert tn == w.shape[2]
        w_spec = pl.BlockSpec((None, k, tn), lambda i, j: (j, 0, 0))
    else:
        n = w.shape[1]
        w_spec = pl.BlockSpec((k, tn), lambda i, j: (0, j))

    def body(*refs):
        a_ref, w_ref = refs[0], refs[1]
        o_ref = refs[-1]
        acc = _nn_raw(a_ref[...], w_ref[...])
        if bias is not None:
            acc = acc + refs[2][...]
        o_ref[...] = acc.astype(o_ref.dtype)

    in_specs = [pl.BlockSpec((tm, k), lambda i, j: (i, 0)), w_spec]
    args = [a, w]
    if bias is not None:
        in_specs.append(pl.BlockSpec((1, tn), lambda i, j: (0, j)))
        args.append(bias)
    return pl.pallas_call(
        body, name=name, grid=(m // tm, n // tn), in_specs=in_specs,
        out_specs=pl.BlockSpec((tm, tn), lambda i, j: (i, j)),
        out_shape=jax.ShapeDtypeStruct((m, n), out_dtype),
        compiler_params=_params(("parallel", "parallel")),
    )(*args)


def _matmul_nt(pairs, add, scale, tm, tk, name, out_dtype=F32):
    m = pairs[0][0].shape[0]
    k = pairs[0][1].shape[-2]
    groups = []
    in_specs, args = [], []
    for pair in pairs:
        d, w = pair[0], pair[1]
        in_specs.append(pl.BlockSpec((tm, d.shape[1]), lambda i, j: (i, 0)))
        if w.ndim == 3:
            g = d.shape[1] // w.shape[2]
            blk = pair[2] // g
            in_specs.append(pl.BlockSpec((g, tk, w.shape[2]), lambda i, j, blk=blk: (blk, j, 0)))
            groups.append((g, w.shape[2]))
        else:
            in_specs.append(pl.BlockSpec((tk, w.shape[1]), lambda i, j: (j, 0)))
            groups.append(None)
        args += [d, w]
    if add is not None:
        in_specs.append(pl.BlockSpec((tm, tk), lambda i, j: (i, j)))
        args.append(add)

    def body(*refs):
        o_ref = refs[-1]
        acc = None
        for p, grp in enumerate(groups):
            d_ref, w_ref = refs[2 * p], refs[2 * p + 1]
            if grp is None:
                terms = [_nt_raw(d_ref[...], w_ref[...])]
            else:
                terms = [_nt_raw(d_ref[:, g * grp[1]:(g + 1) * grp[1]], w_ref[g]) for g in range(grp[0])]
            for t in terms:
                acc = t if acc is None else acc + t
        if add is not None:
            acc = acc + scale * refs[2 * len(groups)][...]
        o_ref[...] = acc.astype(o_ref.dtype)

    return pl.pallas_call(
        body, name=name, grid=(m // tm, k // tk), in_specs=in_specs,
        out_specs=pl.BlockSpec((tm, tk), lambda i, j: (i, j)),
        out_shape=jax.ShapeDtypeStruct((m, k), out_dtype),
        compiler_params=_params(("parallel", "parallel")),
    )(*args)


def _matmul_tn(a, b, tm, tn, tt, name, shards=None, shard0=0, group=1, into=None, colsum=False):
    t, m = a.shape
    n = b.shape[1]
    assert not colsum or tm == m
    n_in = 2 + (into is not None)
    out_dtype = BF16
    per_step = 1 if shards is None else group
    width = per_step * tn

    def body(*refs):
        a_ref, b_ref = refs[0], refs[1]
        o_ref, acc_ref = refs[n_in], refs[-1]
        first = pl.program_id(2) == 0

        @pl.when(first)
        def _():
            acc_ref[...] = jnp.zeros_like(acc_ref)

        if shards is None:
            acc_ref[...] += _tn_raw(a_ref[...], b_ref[...])
        else:
            lhs = a_ref[...].astype(BF16)
            for g in range(per_step):
                acc_ref[g] += _tn_raw(lhs, b_ref[:, g * tn:(g + 1) * tn])

        @pl.when(pl.program_id(2) == t // tt - 1)
        def _():
            o_ref[...] = acc_ref[...].astype(o_ref.dtype)

        if colsum:
            s_ref = refs[n_in + 1]

            @pl.when(first)
            def _():
                s_ref[...] = jnp.zeros_like(s_ref)

            s_ref[...] += jnp.sum(b_ref[...], axis=0, keepdims=True)

    in_specs = [pl.BlockSpec((tt, tm), lambda i, j, kk: (kk, i)),
                pl.BlockSpec((tt, width), lambda i, j, kk: (kk, j))]
    args = [a, b]
    aliases = {}
    if into is not None:
        in_specs.append(pl.BlockSpec(memory_space=pl.ANY))
        args.append(into)
        aliases = {2: 0}
    if shards is None:
        out_specs = [pl.BlockSpec((tm, tn), lambda i, j, kk: (i, j))]
        out_shape = [jax.ShapeDtypeStruct((m, n), out_dtype)]
        acc = pltpu.VMEM((tm, tn), F32)
    else:
        out_specs = [pl.BlockSpec((per_step, tm, tn), lambda i, j, kk: (shard0 // per_step + j, i, 0))]
        out_shape = [jax.ShapeDtypeStruct((shards, m, tn), out_dtype)]
        acc = pltpu.VMEM((per_step, tm, tn), F32)
    if colsum:
        out_specs.append(pl.BlockSpec((1, tn), lambda i, j, kk: (0, j)))
        out_shape.append(jax.ShapeDtypeStruct((1, n), F32))
    res = pl.pallas_call(
        body, name=name, grid=(m // tm, n // width, t // tt), in_specs=in_specs, out_specs=out_specs,
        out_shape=out_shape, input_output_aliases=aliases, scratch_shapes=[acc],
        compiler_params=_params(("parallel", "parallel", "arbitrary")),
    )(*args)
    return res if colsum else res[0]


ROW_TILE = 64


def _stack(ref, start, rows):
    return ref[pl.ds(start, rows), :].astype(F32).reshape(rows // SUBLANES, SUBLANES, LANES)


def _vreg_rows(ref, n):
    return [jnp.broadcast_to(ref[j:j + 1, :], (SUBLANES, LANES))[None] for j in range(n)]


def _column_total(acc):
    return jnp.sum(acc, axis=0, keepdims=True)


def _conv_fwd_tile(pad_ref, taps_w, bias, r0, rows):
    taps = len(taps_w)
    acc = bias
    for j in range(taps):
        acc = acc + _stack(pad_ref, SUBLANES - (taps - 1 - j) + r0, rows) * taps_w[j]
    return acc


def _conv_grads_tile(pad_ref, dpad_ref, dx_ref, taps_w, dws, r0, rows):
    taps = len(taps_w)
    x_rows = _stack(pad_ref, SUBLANES + r0, rows)
    dx = None
    for j in range(taps):
        d_shifted = _stack(dpad_ref, r0 + (taps - 1 - j), rows)
        term = d_shifted * taps_w[j]
        dx = term if dx is None else dx + term
        dws[j] = dws[j] + jnp.sum(d_shifted * x_rows, axis=0)
    dx_ref[r0:r0 + rows, :] = dx.reshape(rows, LANES).astype(dx_ref.dtype)
    return jnp.sum(dx, axis=0)


def _ml_conv_fwd(proj, conv_w, conv_b):
    s = proj.shape[0]
    nblk = 2 * D_GROUP // LANES

    def body(x_ref, w_ref, b_ref, o_ref, pad_ref):
        pad_ref[0:SUBLANES, :] = jnp.zeros((SUBLANES, LANES), F32)
        pad_ref[SUBLANES:, :] = x_ref[...].astype(F32)
        taps_w, bias = _vreg_rows(w_ref, ML_CONV), _vreg_rows(b_ref, 1)[0]
        for r0 in range(0, s, ROW_TILE):
            rows = min(ROW_TILE, s - r0)
            o_ref[r0:r0 + rows, :] = jax.nn.silu(_conv_fwd_tile(pad_ref, taps_w, bias, r0, rows)).reshape(rows, LANES)

    return pl.pallas_call(
        body, name="ml_conv_fwd", grid=(nblk,),
        in_specs=[pl.BlockSpec((s, LANES), lambda j: (0, SEG_MQ + j)),
                  pl.BlockSpec((ML_CONV, LANES), lambda j: (0, j)),
                  pl.BlockSpec((1, LANES), lambda j: (0, j))],
        out_specs=pl.BlockSpec((s, LANES), lambda j: (0, j)),
        out_shape=jax.ShapeDtypeStruct((s, 2 * D_GROUP), F32),
        scratch_shapes=[pltpu.VMEM((s + SUBLANES, LANES), F32)],
        compiler_params=_params(("parallel",)),
    )(proj, conv_w, conv_b)


def _ml_conv_bwd(proj, conv_w, conv_b, d_qk, d_proj):
    s = proj.shape[0]
    nblk = 2 * D_GROUP // LANES

    def body(x_ref, w_ref, b_ref, dy_ref, _, dx_ref, dw_ref, db_ref, dxs_ref, pad_ref, dpad_ref):
        pad_ref[0:SUBLANES, :] = jnp.zeros((SUBLANES, LANES), F32)
        pad_ref[SUBLANES:, :] = x_ref[...].astype(F32)
        dpad_ref[s:, :] = jnp.zeros((SUBLANES, LANES), F32)
        taps_w, bias = _vreg_rows(w_ref, ML_CONV), _vreg_rows(b_ref, 1)[0]
        db = jnp.zeros((SUBLANES, LANES), F32)
        for r0 in range(0, s, ROW_TILE):
            rows = min(ROW_TILE, s - r0)
            pre = _conv_fwd_tile(pad_ref, taps_w, bias, r0, rows)
            _, vjp = jax.vjp(jax.nn.silu, pre)
            d_pre, = vjp(_stack(dy_ref, r0, rows))
            dpad_ref[r0:r0 + rows, :] = d_pre.reshape(rows, LANES)
            db = db + jnp.sum(d_pre, axis=0)
        db_ref[...] = _column_total(db)
        dws = [jnp.zeros((SUBLANES, LANES), F32) for _ in range(ML_CONV)]
        dx_sum = jnp.zeros((SUBLANES, LANES), F32)
        for r0 in range(0, s, ROW_TILE):
            dx_sum = dx_sum + _conv_grads_tile(pad_ref, dpad_ref, dx_ref, taps_w, dws, r0, min(ROW_TILE, s - r0))
        dxs_ref[...] = _column_total(dx_sum)
        for j in range(ML_CONV):
            dw_ref[j:j + 1, :] = _column_total(dws[j])

    return pl.pallas_call(
        body, name="ml_conv_bwd", grid=(nblk,),
        in_specs=[pl.BlockSpec((s, LANES), lambda j: (0, SEG_MQ + j)),
                  pl.BlockSpec((ML_CONV, LANES), lambda j: (0, j)),
                  pl.BlockSpec((1, LANES), lambda j: (0, j)),
                  pl.BlockSpec((s, LANES), lambda j: (0, j)),
                  pl.BlockSpec(memory_space=pl.ANY)],
        out_specs=[pl.BlockSpec((s, LANES), lambda j: (0, SEG_MQ + j)),
                   pl.BlockSpec((ML_CONV, LANES), lambda j: (0, j)),
                   pl.BlockSpec((1, LANES), lambda j: (0, j)),
                   pl.BlockSpec((1, LANES), lambda j: (0, j))],
        out_shape=[jax.ShapeDtypeStruct(d_proj.shape, d_proj.dtype),
                   jax.ShapeDtypeStruct((ML_CONV, 2 * D_GROUP), F32),
                   jax.ShapeDtypeStruct((1, 2 * D_GROUP), F32),
                   jax.ShapeDtypeStruct((1, 2 * D_GROUP), F32)],
        input_output_aliases={4: 0},
        scratch_shapes=[pltpu.VMEM((s + SUBLANES, LANES), F32), pltpu.VMEM((s + SUBLANES, LANES), F32)],
        compiler_params=_params(("parallel",)),
    )(proj, conv_w, conv_b, d_qk, d_proj)


def _gelu_mul(a, b):
    return jax.nn.gelu(a) * b


GELU_C = math.sqrt(2.0 / math.pi)
GELU_K = 0.044715


def _gelu_mul_grads(a, b, d):
    a2 = a * a
    t = jnp.tanh(GELU_C * (a + GELU_K * (a * a2)))
    cdf = 0.5 * (1.0 + t)
    slope = cdf + (0.5 * GELU_C) * a * (1.0 - t * t) * (1.0 + (3.0 * GELU_K) * a2)
    return d * b * slope, d * (a * cdf)


FFN_BLOCKS = D_FF_P // LANES


def _ffn_conv_fwd(u, conv_w, conv_b):
    s = u.shape[0]

    def body(g_ref, v_ref, wg_ref, wv_ref, bg_ref, bv_ref, o_ref, gpad_ref, vpad_ref):
        for pad_ref, x_ref in ((gpad_ref, g_ref), (vpad_ref, v_ref)):
            pad_ref[0:SUBLANES, :] = jnp.zeros((SUBLANES, LANES), F32)
            pad_ref[SUBLANES:, :] = x_ref[...].astype(F32)
        taps_g, bias_g = _vreg_rows(wg_ref, FFN_CONV), _vreg_rows(bg_ref, 1)[0]
        taps_v, bias_v = _vreg_rows(wv_ref, FFN_CONV), _vreg_rows(bv_ref, 1)[0]
        for r0 in range(0, s, ROW_TILE):
            rows = min(ROW_TILE, s - r0)
            ug = _conv_fwd_tile(gpad_ref, taps_g, bias_g, r0, rows)
            uv = _conv_fwd_tile(vpad_ref, taps_v, bias_v, r0, rows)
            o_ref[r0:r0 + rows, :] = _gelu_mul(ug, uv).reshape(rows, LANES).astype(o_ref.dtype)

    col = lambda off: (lambda j: (0, off + j))
    return pl.pallas_call(
        body, name="ffn_conv_fwd", grid=(FFN_BLOCKS,),
        in_specs=[pl.BlockSpec((s, LANES), col(0)), pl.BlockSpec((s, LANES), col(FFN_BLOCKS)),
                  pl.BlockSpec((FFN_CONV, LANES), col(0)), pl.BlockSpec((FFN_CONV, LANES), col(FFN_BLOCKS)),
                  pl.BlockSpec((1, LANES), col(0)), pl.BlockSpec((1, LANES), col(FFN_BLOCKS))],
        out_specs=pl.BlockSpec((s, LANES), col(0)),
        out_shape=jax.ShapeDtypeStruct((s, D_FF_P), BF16),
        scratch_shapes=[pltpu.VMEM((s + SUBLANES, LANES), F32), pltpu.VMEM((s + SUBLANES, LANES), F32)],
        compiler_params=_params(("parallel",)),
    )(u, u, conv_w, conv_w, conv_b, conv_b)


def _ffn_conv_bwd(u, conv_w, conv_b, d_h):
    s = u.shape[0]

    def body(g_ref, v_ref, wg_ref, wv_ref, bg_ref, bv_ref, dh_ref,
             dug_ref, duv_ref, dwg_ref, dwv_ref, dbg_ref, dbv_ref,
             gpad_ref, vpad_ref, dgpad_ref, dvpad_ref):
        for pad_ref, x_ref in ((gpad_ref, g_ref), (vpad_ref, v_ref)):
            pad_ref[0:SUBLANES, :] = jnp.zeros((SUBLANES, LANES), F32)
            pad_ref[SUBLANES:, :] = x_ref[...].astype(F32)
        dgpad_ref[s:, :] = jnp.zeros((SUBLANES, LANES), F32)
        dvpad_ref[s:, :] = jnp.zeros((SUBLANES, LANES), F32)
        taps_g, bias_g = _vreg_rows(wg_ref, FFN_CONV), _vreg_rows(bg_ref, 1)[0]
        taps_v, bias_v = _vreg_rows(wv_ref, FFN_CONV), _vreg_rows(bv_ref, 1)[0]
        dbg = jnp.zeros((SUBLANES, LANES), F32)
        dbv = jnp.zeros((SUBLANES, LANES), F32)
        for r0 in range(0, s, ROW_TILE):
            rows = min(ROW_TILE, s - r0)
            ug = _conv_fwd_tile(gpad_ref, taps_g, bias_g, r0, rows)
            uv = _conv_fwd_tile(vpad_ref, taps_v, bias_v, r0, rows)
            d_ug, d_uv = _gelu_mul_grads(ug, uv, _stack(dh_ref, r0, rows))
            dgpad_ref[r0:r0 + rows, :] = d_ug.reshape(rows, LANES)
            dvpad_ref[r0:r0 + rows, :] = d_uv.reshape(rows, LANES)
            dbg = dbg + jnp.sum(d_ug, axis=0)
            dbv = dbv + jnp.sum(d_uv, axis=0)
        dbg_ref[...] = _column_total(dbg)
        dbv_ref[...] = _column_total(dbv)
        for pad_ref, dpad_ref, taps_w, dx_ref, dw_ref in ((gpad_ref, dgpad_ref, taps_g, dug_ref, dwg_ref),
                                                          (vpad_ref, dvpad_ref, taps_v, duv_ref, dwv_ref)):
            dws = [jnp.zeros((SUBLANES, LANES), F32) for _ in range(FFN_CONV)]
            for r0 in range(0, s, ROW_TILE):
                _conv_grads_tile(pad_ref, dpad_ref, dx_ref, taps_w, dws, r0, min(ROW_TILE, s - r0))
            for j in range(FFN_CONV):
                dw_ref[j:j + 1, :] = _column_total(dws[j])

    col = lambda off: (lambda j: (0, off + j))
    seq = pl.BlockSpec((s, LANES), col(0))
    return pl.pallas_call(
        body, name="ffn_conv_bwd", grid=(FFN_BLOCKS,),
        in_specs=[pl.BlockSpec((s, LANES), col(0)), pl.BlockSpec((s, LANES), col(FFN_BLOCKS)),
                  pl.BlockSpec((FFN_CONV, LANES), col(0)), pl.BlockSpec((FFN_CONV, LANES), col(FFN_BLOCKS)),
                  pl.BlockSpec((1, LANES), col(0)), pl.BlockSpec((1, LANES), col(FFN_BLOCKS)), seq],
        out_specs=[seq, seq, pl.BlockSpec((FFN_CONV, LANES), col(0)), pl.BlockSpec((FFN_CONV, LANES), col(0)),
                   pl.BlockSpec((1, LANES), col(0)), pl.BlockSpec((1, LANES), col(0))],
        out_shape=[jax.ShapeDtypeStruct((s, D_FF_P), BF16), jax.ShapeDtypeStruct((s, D_FF_P), BF16),
                   jax.ShapeDtypeStruct((FFN_CONV, D_FF_P), F32), jax.ShapeDtypeStruct((FFN_CONV, D_FF_P), F32),
                   jax.ShapeDtypeStruct((1, D_FF_P), F32), jax.ShapeDtypeStruct((1, D_FF_P), F32)],
        scratch_shapes=[pltpu.VMEM((s + SUBLANES, LANES), F32) for _ in range(4)],
        compiler_params=_params(("parallel",)),
    )(u, u, conv_w, conv_w, conv_b, conv_b, d_h)


def _chunk_masks(c):
    row = lax.broadcasted_iota(jnp.int32, (c, c), 0)
    col = lax.broadcasted_iota(jnp.int32, (c, c), 1)
    return row, col


@jax.custom_vjp
def _split_heads(x):
    return tuple(x[:, h * D_HEAD:(h + 1) * D_HEAD] for h in range(N_HEADS))


_split_heads.defvjp(lambda x: (_split_heads(x), None), lambda _, gs: (jnp.concatenate(gs, axis=1),))


@jax.custom_vjp
def _merge_heads(xs):
    return jnp.concatenate(xs, axis=1)


_merge_heads.defvjp(lambda xs: (_merge_heads(xs), None), lambda _, g: (_split_heads(g),))


@jax.custom_vjp
def _split_chunks(x):
    return tuple(x[i * CHUNK:(i + 1) * CHUNK] for i in range(x.shape[0] // CHUNK))


_split_chunks.defvjp(lambda x: (_split_chunks(x), None), lambda _, gs: (jnp.concatenate(gs, axis=0),))


@jax.custom_vjp
def _merge_chunks(xs):
    return jnp.concatenate(xs, axis=0)


_merge_chunks.defvjp(lambda xs: (_merge_chunks(xs), None), lambda _, g: (_split_chunks(g),))


def _blocks(x):
    return [_split_heads(rows) for rows in _split_chunks(x)]


def _per_chunk_rows(per_chunk, rid):
    out = per_chunk[0]
    for i in range(1, len(per_chunk)):
        out = jnp.where(rid >= i * CHUNK, per_chunk[i], out)
    return out


HEADS = range(N_HEADS)
CHUNKS_PER_STEP = 4
ML_CHUNKS_PER_STEP = 1


def _hg_chunk(hq, hf, hi, hgate, l0, l1, nw, sts):
    n = hq.shape[0] // CHUNK
    row, col = _chunk_masks(n * CHUNK)
    same_chunk = functools.reduce(jnp.logical_or, [(row >= i * CHUNK) & (row < (i + 1) * CHUNK) &
                                                   (col >= i * CHUNK) & (col < (i + 1) * CHUNK) for i in range(n)])
    causal = _chunk_masks(CHUNK)
    causal = causal[1] <= causal[0]
    mx = lax.stop_gradient(jnp.maximum(l0, l1))
    e0 = jnp.exp(l0 - mx)
    e1 = jnp.exp(l1 - mx)
    lb = e0 / (e0 + e1)
    sig = jax.nn.sigmoid(hf)
    lf = jnp.log(lb + (1.0 - lb) * sig)
    k = (1.0 - lb) * jax.nn.sigmoid(-hf)
    q = jax.nn.silu(hq)
    b = _dg(((col <= row) & same_chunk).astype(F32), lf, 1, 0, HIGHEST)
    rid = lax.broadcasted_iota(jnp.int32, b.shape, 0)
    pick = lambda r: jnp.sum(jnp.where(rid == r, b, 0.0), axis=0, keepdims=True)
    b_last_c = [pick(i * CHUNK + CHUNK - 1) for i in range(n)]
    b_ref = _per_chunk_rows([pick(i * CHUNK + CHUNK // 2 - 1) for i in range(n)], rid)
    b_last = _per_chunk_rows(b_last_c, rid)
    qa = _blocks(q * jnp.exp(b - b_ref))
    ka = _blocks(k * jnp.exp(b_ref - b))
    qe = _blocks(q * jnp.exp(b))
    kd = _blocks(k * jnp.exp(b_last - b))
    decay = [_split_heads(jnp.exp(b_last_c[i])) for i in range(n)]
    v = _blocks(hi)
    chunks = range(n)
    attn = [[jnp.where(causal, _nt(qa[i][h], ka[i][h]), 0.0) for h in HEADS] for i in chunks]
    intra = [[_nn(attn[i][h], v[i][h]) for h in HEADS] for i in chunks]
    kv = [[_tn(v[i][h], kd[i][h]) for h in HEADS] for i in chunks]
    normed = []
    for i in chunks:
        inter = [_nt(qe[i][h], sts[h]) for h in HEADS]
        sts = tuple(decay[i][h] * sts[h] + kv[i][h] for h in HEADS)
        o = [intra[i][h] + inter[h] for h in HEADS]
        normed.append(_merge_heads(tuple(o[h] * lax.rsqrt(jnp.mean(o[h] * o[h], axis=-1, keepdims=True) + LN_EPS)
                                         for h in HEADS)))
    return _merge_chunks(tuple(normed)) * nw * jax.nn.silu(hgate), sts


def _seg(ref, seg):
    return ref[:, seg * D_GROUP:(seg + 1) * D_GROUP]


def _hgrn2_fwd(proj, logits, norm_w):
    s = proj.shape[0]
    rows = CHUNKS_PER_STEP * CHUNK
    nc = s // rows

    def body(p_ref, lg_ref, nw_ref, y_ref, st_out_ref, st_scr):
        @pl.when(pl.program_id(0) == 0)
        def _():
            st_scr[...] = jnp.zeros_like(st_scr)

        sts = tuple(st_scr[h] for h in HEADS)
        y, sts_new = _hg_chunk(_seg(p_ref, 0), _seg(p_ref, 1), _seg(p_ref, 2), _seg(p_ref, 3),
                               lg_ref[0:1, :], lg_ref[1:2, :], nw_ref[...], sts)
        y_ref[...] = y.astype(y_ref.dtype)
        for h in HEADS:
            st_out_ref[h] = sts[h]
            st_scr[h] = sts_new[h]

    return pl.pallas_call(
        body, name="hgrn2_fwd", grid=(nc,),
        in_specs=[pl.BlockSpec((rows, 4 * D_GROUP), lambda c: (c, 0)),
                  pl.BlockSpec((2, D_GROUP), lambda c: (0, 0)),
                  pl.BlockSpec((1, D_GROUP), lambda c: (0, 0))],
        out_specs=[pl.BlockSpec((rows, D_GROUP), lambda c: (c, 0)),
                   pl.BlockSpec((None, N_HEADS, D_HEAD, D_HEAD), lambda c: (c, 0, 0, 0))],
        out_shape=[jax.ShapeDtypeStruct((s, 2 * D_GROUP), BF16),
                   jax.ShapeDtypeStruct((nc, N_HEADS, D_HEAD, D_HEAD), F32)],
        scratch_shapes=[pltpu.VMEM((N_HEADS, D_HEAD, D_HEAD), F32)],
        compiler_params=_params(("arbitrary",)),
    )(proj, logits, norm_w)


def _hgrn2_bwd(proj, logits, norm_w, states, d_y):
    s = proj.shape[0]
    rows = CHUNKS_PER_STEP * CHUNK
    nc = s // rows

    def body(p_ref, lg_ref, nw_ref, st_ref, dy_ref, dp_ref, dl_ref, dnw_ref, dsum_ref, dst_scr):
        @pl.when(pl.program_id(0) == 0)
        def _():
            dst_scr[...] = jnp.zeros_like(dst_scr)
            dl_ref[...] = jnp.zeros_like(dl_ref)
            dnw_ref[...] = jnp.zeros_like(dnw_ref)
            dsum_ref[...] = jnp.zeros_like(dsum_ref)

        _, vjp = jax.vjp(_hg_chunk, _seg(p_ref, 0), _seg(p_ref, 1), _seg(p_ref, 2), _seg(p_ref, 3),
                         lg_ref[0:1, :], lg_ref[1:2, :], nw_ref[...], tuple(st_ref[h] for h in HEADS))
        d_hq, d_hf, d_hi, d_hg, d_l0, d_l1, d_nw, d_sts = vjp((dy_ref[...], tuple(dst_scr[h] for h in HEADS)))
        for seg, val in enumerate((d_hq, d_hf, d_hi, d_hg)):
            dp_ref[:, seg * D_GROUP:(seg + 1) * D_GROUP] = val.astype(dp_ref.dtype)
            dsum_ref[:, seg * D_GROUP:(seg + 1) * D_GROUP] += jnp.sum(val, axis=0, keepdims=True)
        dl_ref[0:1, :] += d_l0
        dl_ref[1:2, :] += d_l1
        dnw_ref[...] += d_nw
        for h in HEADS:
            dst_scr[h] = d_sts[h]

    rev = lambda c: nc - 1 - c
    return pl.pallas_call(
        body, name="hgrn2_bwd", grid=(nc,),
        in_specs=[pl.BlockSpec((rows, 4 * D_GROUP), lambda c: (rev(c), 0)),
                  pl.BlockSpec((2, D_GROUP), lambda c: (0, 0)),
                  pl.BlockSpec((1, D_GROUP), lambda c: (0, 0)),
                  pl.BlockSpec((None, N_HEADS, D_HEAD, D_HEAD), lambda c: (rev(c), 0, 0, 0)),
                  pl.BlockSpec((rows, D_GROUP), lambda c: (rev(c), 0))],
        out_specs=[pl.BlockSpec((rows, 4 * D_GROUP), lambda c: (rev(c), 0)),
                   pl.BlockSpec((2, D_GROUP), lambda c: (0, 0)),
                   pl.BlockSpec((1, D_GROUP), lambda c: (0, 0)),
                   pl.BlockSpec((1, 4 * D_GROUP), lambda c: (0, 0))],
        out_shape=[jax.ShapeDtypeStruct((s, D_IN_MAIN), BF16), jax.ShapeDtypeStruct((2, D_GROUP), F32),
                   jax.ShapeDtypeStruct((1, D_GROUP), F32), jax.ShapeDtypeStruct((1, 4 * D_GROUP), F32)],
        scratch_shapes=[pltpu.VMEM((N_HEADS, D_HEAD, D_HEAD), F32)],
        compiler_params=_params(("arbitrary",)),
    )(proj, logits, norm_w, states, d_y)


def _gate_column(gates, lane, idx):
    return jnp.sum(jnp.where(lane == idx, gates, 0.0), axis=1, keepdims=True)


def _head_layer_norm(h):
    mu = jnp.mean(h, axis=-1, keepdims=True)
    var = jnp.mean(jnp.square(h - mu), axis=-1, keepdims=True)
    return (h - mu) * lax.rsqrt(var + LN_EPS)


def _ml_chunk(qc, kc, v, mo, gates, nw, cts, ns, ms):
    n = qc.shape[0] // CHUNK
    row, col = _chunk_masks(CHUNK)
    mask = col <= row
    eye = col == row
    to_row = lambda t: jnp.sum(jnp.where(eye, t, 0.0), axis=0, keepdims=True)
    q = _blocks(qc * (D_HEAD ** -0.5))
    k = _blocks(kc)
    vs = _blocks(v)
    gate_rows = _split_chunks(gates)
    lane = lax.broadcasted_iota(jnp.int32, gate_rows[0].shape, 1)
    each = [(i, h) for i in range(n) for h in HEADS]
    on_each = lambda f: {ih: f(*ih) for ih in each}
    ig = on_each(lambda i, h: _gate_column(gate_rows[i], lane, h))
    lf = on_each(lambda i, h: jax.nn.log_sigmoid(_gate_column(gate_rows[i], lane, N_HEADS + h)))
    lf_row = on_each(lambda i, h: to_row(lf[i, h]))
    ig_row = on_each(lambda i, h: to_row(ig[i, h]))
    b_col = on_each(lambda i, h: jnp.sum(jnp.where(mask, lf_row[i, h], 0.0), axis=1, keepdims=True))
    b_row = on_each(lambda i, h: jnp.sum(jnp.where(row <= col, lf[i, h], 0.0), axis=0, keepdims=True))
    g = on_each(lambda i, h: jnp.sum(lf[i, h], axis=0, keepdims=True))
    d = on_each(lambda i, h: jnp.where(mask, b_col[i, h] - b_row[i, h] + ig_row[i, h], -jnp.inf))
    a = on_each(lambda i, h: g[i, h] - b_col[i, h] + ig[i, h])
    m_at = {(0, h): ms[h] for h in HEADS}
    for i, h in each:
        m_at[i + 1, h] = lax.stop_gradient(jnp.maximum(g[i, h] + m_at[i, h], jnp.max(a[i, h], axis=0, keepdims=True)))
    inter = on_each(lambda i, h: b_col[i, h] + m_at[i, h])
    m_t = on_each(lambda i, h: lax.stop_gradient(jnp.maximum(inter[i, h], jnp.max(d[i, h], axis=1, keepdims=True))))
    qk = on_each(lambda i, h: _nt(q[i][h], k[i][h]))
    sc = on_each(lambda i, h: qk[i, h] * jnp.exp(d[i, h] - m_t[i, h]))
    w_inter = on_each(lambda i, h: jnp.exp(inter[i, h] - m_t[i, h]))
    sv = on_each(lambda i, h: _nn(sc[i, h], vs[i][h]))
    decay = on_each(lambda i, h: jnp.exp(g[i, h] + m_at[i, h] - m_at[i + 1, h]))
    wk = on_each(lambda i, h: k[i][h] * jnp.exp(a[i, h] - m_at[i + 1, h]))
    kv = on_each(lambda i, h: _tn(vs[i][h], wk[i, h]))
    normed = []
    for i in range(n):
        qc_state = [_nt(q[i][h], cts[h]) for h in HEADS]
        num = [sv[i, h] + w_inter[i, h] * qc_state[h] for h in HEADS]
        den = [jnp.sum(sc[i, h], axis=1, keepdims=True)
               + w_inter[i, h] * jnp.sum(q[i][h] * ns[h], axis=1, keepdims=True) for h in HEADS]
        hh = [num[h] / jnp.maximum(jnp.abs(den[h]), jnp.exp(-m_t[i, h])) for h in HEADS]
        cts = tuple(decay[i, h] * cts[h] + kv[i, h] for h in HEADS)
        ns = tuple(decay[i, h] * ns[h] + jnp.sum(wk[i, h], axis=0, keepdims=True) for h in HEADS)
        normed.append(_merge_heads(tuple(_head_layer_norm(hh[h]) for h in HEADS)))
    y = jax.nn.sigmoid(mo) * (_merge_chunks(tuple(normed)) * nw)
    return y, cts, ns, tuple(m_at[n, h] for h in HEADS)


def _mlstm_fwd(qk, proj, gates, norm_w, y):
    s = proj.shape[0]
    rows = ML_CHUNKS_PER_STEP * CHUNK
    nc = s // rows

    def body(qk_ref, vo_ref, g_ref, nw_ref, _, y_ref, ct_out, n_out, m_out, ct_scr, n_scr, m_scr):
        @pl.when(pl.program_id(0) == 0)
        def _():
            ct_scr[...] = jnp.zeros_like(ct_scr)
            n_scr[...] = jnp.zeros_like(n_scr)
            m_scr[...] = jnp.full(m_scr.shape, NEG_BIG, F32)

        cts = tuple(ct_scr[h] for h in HEADS)
        ns = tuple(n_scr[h] for h in HEADS)
        ms = tuple(m_scr[h] for h in HEADS)
        y, cts_new, ns_new, ms_new = _ml_chunk(_seg(qk_ref, 0), _seg(qk_ref, 1), _seg(vo_ref, 0), _seg(vo_ref, 1),
                                               g_ref[...], nw_ref[...], cts, ns, ms)
        y_ref[...] = y.astype(y_ref.dtype)
        for h in HEADS:
            ct_out[h], n_out[h], m_out[h] = cts[h], ns[h], ms[h]
            ct_scr[h], n_scr[h], m_scr[h] = cts_new[h], ns_new[h], ms_new[h]

    st = lambda r, w: pl.BlockSpec((None, N_HEADS, r, w), lambda c: (c, 0, 0, 0))
    return pl.pallas_call(
        body, name="mlstm_fwd", grid=(nc,),
        in_specs=[pl.BlockSpec((rows, 2 * D_GROUP), lambda c: (c, 0)),
                  pl.BlockSpec((rows, 2 * D_GROUP), lambda c: (c, 3)),
                  pl.BlockSpec((rows, LANES), lambda c: (c, 0)),
                  pl.BlockSpec((1, D_GROUP), lambda c: (0, 0)),
                  pl.BlockSpec(memory_space=pl.ANY)],
        out_specs=[pl.BlockSpec((rows, D_GROUP), lambda c: (c, 1)),
                   st(D_HEAD, D_HEAD), st(1, D_HEAD), st(1, 1)],
        out_shape=[jax.ShapeDtypeStruct(y.shape, y.dtype),
                   jax.ShapeDtypeStruct((nc, N_HEADS, D_HEAD, D_HEAD), F32),
                   jax.ShapeDtypeStruct((nc, N_HEADS, 1, D_HEAD), F32),
                   jax.ShapeDtypeStruct((nc, N_HEADS, 1, 1), F32)],
        input_output_aliases={4: 0},
        scratch_shapes=[pltpu.VMEM((N_HEADS, D_HEAD, D_HEAD), F32), pltpu.VMEM((N_HEADS, 1, D_HEAD), F32),
                        pltpu.VMEM((N_HEADS, 1, 1), F32)],
        compiler_params=_params(("arbitrary",)),
    )(qk, proj, gates, norm_w, y)


def _mlstm_bwd(qk, proj, gates, norm_w, ct_s, n_s, m_s, d_y, d_proj):
    s = proj.shape[0]
    rows = ML_CHUNKS_PER_STEP * CHUNK
    nc = s // rows

    def body(qk_ref, vo_ref, g_ref, nw_ref, ct_ref, n_ref, m_ref, dy_ref, _,
             dp_ref, dqk_ref, dg_ref, dnw_ref, dsum_ref, dct_scr, dn_scr):
        @pl.when(pl.program_id(0) == 0)
        def _():
            dct_scr[...] = jnp.zeros_like(dct_scr)
            dn_scr[...] = jnp.zeros_like(dn_scr)
            dnw_ref[...] = jnp.zeros_like(dnw_ref)
            dsum_ref[...] = jnp.zeros_like(dsum_ref)

        ms = tuple(m_ref[h] for h in HEADS)
        step = lambda *a: _ml_chunk(*a, ms)[:3]
        _, vjp = jax.vjp(step, _seg(qk_ref, 0), _seg(qk_ref, 1), _seg(vo_ref, 0), _seg(vo_ref, 1), g_ref[...],
                         nw_ref[...], tuple(ct_ref[h] for h in HEADS), tuple(n_ref[h] for h in HEADS))
        d_q, d_k, d_v, d_o, d_gates, d_nw, d_cts, d_ns = vjp(
            (dy_ref[...], tuple(dct_scr[h] for h in HEADS), tuple(dn_scr[h] for h in HEADS)))
        dqk_ref[:, 0:D_GROUP] = d_q
        dqk_ref[:, D_GROUP:2 * D_GROUP] = d_k
        for seg, val in enumerate((d_v, d_o)):
            dp_ref[:, seg * D_GROUP:(seg + 1) * D_GROUP] = val.astype(dp_ref.dtype)
            dsum_ref[:, seg * D_GROUP:(seg + 1) * D_GROUP] += jnp.sum(val, axis=0, keepdims=True)
        dg_ref[...] = d_gates
        dnw_ref[...] += d_nw
        for h in HEADS:
            dct_scr[h] = d_cts[h]
            dn_scr[h] = d_ns[h]

    rev = lambda c: nc - 1 - c
    st = lambda r, w: pl.BlockSpec((None, N_HEADS, r, w), lambda c: (rev(c), 0, 0, 0))
    return pl.pallas_call(
        body, name="mlstm_bwd", grid=(nc,),
        in_specs=[pl.BlockSpec((rows, 2 * D_GROUP), lambda c: (rev(c), 0)),
                  pl.BlockSpec((rows, 2 * D_GROUP), lambda c: (rev(c), 3)),
                  pl.BlockSpec((rows, LANES), lambda c: (rev(c), 0)),
                  pl.BlockSpec((1, D_GROUP), lambda c: (0, 0)),
                  st(D_HEAD, D_HEAD), st(1, D_HEAD), st(1, 1),
                  pl.BlockSpec((rows, D_GROUP), lambda c: (rev(c), 1)),
                  pl.BlockSpec(memory_space=pl.ANY)],
        out_specs=[pl.BlockSpec((rows, 2 * D_GROUP), lambda c: (rev(c), 3)),
                   pl.BlockSpec((rows, 2 * D_GROUP), lambda c: (rev(c), 0)),
                   pl.BlockSpec((rows, LANES), lambda c: (rev(c), 0)),
                   pl.BlockSpec((1, D_GROUP), lambda c: (0, 0)),
                   pl.BlockSpec((1, 2 * D_GROUP), lambda c: (0, 0))],
        out_shape=[jax.ShapeDtypeStruct(d_proj.shape, d_proj.dtype), jax.ShapeDtypeStruct((s, 2 * D_GROUP), F32),
                   jax.ShapeDtypeStruct((s, LANES), F32), jax.ShapeDtypeStruct((1, D_GROUP), F32),
                   jax.ShapeDtypeStruct((1, 2 * D_GROUP), F32)],
        input_output_aliases={8: 0},
        scratch_shapes=[pltpu.VMEM((N_HEADS, D_HEAD, D_HEAD), F32), pltpu.VMEM((N_HEADS, 1, D_HEAD), F32)],
        compiler_params=_params(("arbitrary",)),
    )(qk, proj, gates, norm_w, ct_s, n_s, m_s, d_y, d_proj)


LN_TOKENS = 512
ATT_TOKENS = 512


def _proj_res_ln(a, w, xres, g, b, name):
    s, dm = xres.shape
    k = a.shape[1]
    tb = min(LN_TOKENS, s)

    def body(a_ref, w_ref, x_ref, g_ref, b_ref, z_ref, o_ref):
        z = ALPHA * x_ref[...] + _nn_raw(a_ref[...], w_ref[...])
        z_ref[...] = z
        o_ref[...] = _layer_norm(z, g_ref[...], b_ref[...])

    tok = pl.BlockSpec((tb, dm), lambda i: (i, 0))
    vec = pl.BlockSpec((1, dm), lambda i: (0, 0))
    act = jax.ShapeDtypeStruct((s, dm), F32)
    return pl.pallas_call(
        body, name=name, grid=(s // tb,),
        in_specs=[pl.BlockSpec((tb, k), lambda i: (i, 0)), pl.BlockSpec((k, dm), lambda i: (0, 0)), tok, vec, vec],
        out_specs=[tok, tok], out_shape=[act, act], compiler_params=_params(("parallel",)),
    )(a, w, xres, g, b)


def _ln_bwd_proj(d_out, z, g, b, w, name):
    s, dm = z.shape
    k = w.shape[0]
    tb = min(LN_TOKENS, s)

    def body(do_ref, z_ref, g_ref, b_ref, w_ref, dz_ref, da_ref, dg_ref, db_ref):
        @pl.when(pl.program_id(0) == 0)
        def _():
            dg_ref[...] = jnp.zeros_like(dg_ref)
            db_ref[...] = jnp.zeros_like(db_ref)

        _, vjp = jax.vjp(_layer_norm, z_ref[...], g_ref[...], b_ref[...])
        d_z, d_g, d_b = vjp(do_ref[...])
        dz_ref[...] = d_z
        da_ref[...] = _nt_raw(d_z, w_ref[...])
        dg_ref[...] += d_g
        db_ref[...] += d_b

    tok = pl.BlockSpec((tb, dm), lambda i: (i, 0))
    vec = pl.BlockSpec((1, dm), lambda i: (0, 0))
    return pl.pallas_call(
        body, name=name, grid=(s // tb,),
        in_specs=[tok, tok, vec, vec, pl.BlockSpec((k, dm), lambda i: (0, 0))],
        out_specs=[tok, pl.BlockSpec((tb, k), lambda i: (i, 0)), vec, vec],
        out_shape=[jax.ShapeDtypeStruct((s, dm), F32), jax.ShapeDtypeStruct((s, k), F32),
                   jax.ShapeDtypeStruct((1, dm), F32), jax.ShapeDtypeStruct((1, dm), F32)],
        compiler_params=_params(("arbitrary",)),
    )(d_out, z, g, b, w)


def _proj_loss_tail(a, w, xres, g, b, target):
    s, dm = xres.shape
    k = a.shape[1]
    tb = min(ATT_TOKENS, s)

    def loss_fn(z, gg, bb, tgt):
        err = jnp.square(_layer_norm(z, gg, bb) - tgt)
        return 0.5 * jnp.sum(jnp.mean(err, axis=-1, keepdims=True), axis=0, keepdims=True)

    def body(a_ref, w_ref, x_ref, g_ref, b_ref, t_ref, loss_ref, dz_ref, dg_ref, db_ref):
        @pl.when(pl.program_id(0) == 0)
        def _():
            loss_ref[...] = jnp.zeros_like(loss_ref)
            dg_ref[...] = jnp.zeros_like(dg_ref)
            db_ref[...] = jnp.zeros_like(db_ref)

        z = ALPHA * x_ref[...] + _nn_raw(a_ref[...], w_ref[...])
        tgt = t_ref[...]
        loss, vjp = jax.vjp(lambda zz, gg, bb: loss_fn(zz, gg, bb, tgt), z, g_ref[...], b_ref[...])
        d_z, d_g, d_b = vjp(jnp.ones((1, 1), F32))
        loss_ref[...] += loss
        dz_ref[...] = d_z
        dg_ref[...] += d_g
        db_ref[...] += d_b

    tok = pl.BlockSpec((tb, dm), lambda i: (i, 0))
    vec = pl.BlockSpec((1, dm), lambda i: (0, 0))
    one = pl.BlockSpec((1, 1), lambda i: (0, 0))
    return pl.pallas_call(
        body, name="ffn_down_loss_tail", grid=(s // tb,),
        in_specs=[pl.BlockSpec((tb, k), lambda i: (i, 0)), pl.BlockSpec((k, dm), lambda i: (0, 0)), tok, vec, vec, tok],
        out_specs=[one, tok, vec, vec],
        out_shape=[jax.ShapeDtypeStruct((1, 1), F32), jax.ShapeDtypeStruct((s, dm), F32),
                   jax.ShapeDtypeStruct((1, dm), F32), jax.ShapeDtypeStruct((1, dm), F32)],
        compiler_params=_params(("arbitrary",)),
    )(a, w, xres, g, b, target)


def _att_heads(qs, ks, vs):
    sc = [_nt(q, k) * (CA_DH ** -0.5) for q, k in zip(qs, ks)]
    p = [jax.nn.softmax(s, axis=-1) for s in sc]
    return tuple(_nn(pp, v) for pp, v in zip(p, vs))


def _head_slices(ref_or_value, offset):
    return tuple(ref_or_value[:, offset + h * CA_DH:offset + (h + 1) * CA_DH] for h in range(CA_HEADS))


def _cross_attention_fwd(x1, kv, wq, wo, g, b):
    s = x1.shape[0]
    tb = min(ATT_TOKENS, s)

    def body(x_ref, kv_ref, wq_ref, wo_ref, g_ref, b_ref, att_ref, z_ref, o_ref):
        x_blk = x_ref[...]
        q = _nn_raw(x_blk, wq_ref[...])
        att = jnp.concatenate(_att_heads(_head_slices(q, 0), _head_slices(kv_ref, 0), _head_slices(kv_ref, D_MODEL)),
                              axis=1)
        att_ref[...] = att.astype(att_ref.dtype)
        z = ALPHA * x_blk + _nn_raw(att, wo_ref[...])
        z_ref[...] = z
        o_ref[...] = _layer_norm(z, g_ref[...], b_ref[...])

    tok = pl.BlockSpec((tb, D_MODEL), lambda i: (i, 0))
    mat = pl.BlockSpec((D_MODEL, D_MODEL), lambda i: (0, 0))
    vec = pl.BlockSpec((1, D_MODEL), lambda i: (0, 0))
    act = jax.ShapeDtypeStruct((s, D_MODEL), F32)
    return pl.pallas_call(
        body, name="cross_attention_fwd", grid=(s // tb,),
        in_specs=[tok, pl.BlockSpec((N_MEM, 2 * D_MODEL), lambda i: (0, 0)), mat, mat, vec, vec],
        out_specs=[tok, tok, tok],
        out_shape=[jax.ShapeDtypeStruct((s, D_MODEL), BF16), act, act],
        compiler_params=_params(("parallel",)),
    )(x1, kv, wq, wo, g, b)


def _cross_attention_bwd(d_x2, x1, z2, kv, wq, wo, g, b):
    s = x1.shape[0]
    tb = min(ATT_TOKENS, s)

    def body(dx2_ref, x_ref, z_ref, kv_ref, wq_ref, wo_ref, g_ref, b_ref,
             dx1_ref, dq_ref, dz_ref, dkv_ref, dg_ref, db_ref):
        @pl.when(pl.program_id(0) == 0)
        def _():
            dkv_ref[...] = jnp.zeros_like(dkv_ref)
            dg_ref[...] = jnp.zeros_like(dg_ref)
            db_ref[...] = jnp.zeros_like(db_ref)

        _, ln_vjp = jax.vjp(_layer_norm, z_ref[...], g_ref[...], b_ref[...])
        d_z, d_g, d_b = ln_vjp(dx2_ref[...])
        dg_ref[...] += d_g
        db_ref[...] += d_b
        dz_ref[...] = d_z.astype(dz_ref.dtype)
        d_att = _nt_raw(d_z, wo_ref[...])
        q = _nn_raw(x_ref[...], wq_ref[...])
        _, vjp = jax.vjp(_att_heads, _head_slices(q, 0), _head_slices(kv_ref, 0), _head_slices(kv_ref, D_MODEL))
        d_qs, d_ks, d_vs = vjp(_head_slices(d_att, 0))
        for h in range(CA_HEADS):
            lo = h * CA_DH
            dkv_ref[:, lo:lo + CA_DH] += d_ks[h]
            dkv_ref[:, D_MODEL + lo:D_MODEL + lo + CA_DH] += d_vs[h]
        d_q = jnp.concatenate(d_qs, axis=1)
        dq_ref[...] = d_q.astype(dq_ref.dtype)
        dx1_ref[...] = ALPHA * d_z + _nt_raw(d_q, wq_ref[...])

    tok = pl.BlockSpec((tb, D_MODEL), lambda i: (i, 0))
    mem = pl.BlockSpec((N_MEM, 2 * D_MODEL), lambda i: (0, 0))
    mat = pl.BlockSpec((D_MODEL, D_MODEL), lambda i: (0, 0))
    vec = pl.BlockSpec((1, D_MODEL), lambda i: (0, 0))
    low = jax.ShapeDtypeStruct((s, D_MODEL), BF16)
    return pl.pallas_call(
        body, name="cross_attention_bwd", grid=(s // tb,),
        in_specs=[tok, tok, tok, mem, mat, mat, vec, vec], out_specs=[tok, tok, tok, mem, vec, vec],
        out_shape=[jax.ShapeDtypeStruct((s, D_MODEL), F32), low, low,
                   jax.ShapeDtypeStruct((N_MEM, 2 * D_MODEL), F32),
                   jax.ShapeDtypeStruct((1, D_MODEL), F32), jax.ShapeDtypeStruct((1, D_MODEL), F32)],
        compiler_params=_params(("arbitrary",)),
    )(d_x2, x1, z2, kv, wq, wo, g, b)


def _local_step(x, mem, target, w, mid_weights=None, ffn_weights=None, on_ffn_grads=None, on_mid_grads=None,
                on_small_grads=None, on_last_grads=None):
    w = dict(w)
    s = x.shape[0]
    tm = min(512, s)
    tt = min(512, s)
    proj = _matmul_nn(x, w["w_in_main"], w["b_in_main"], min(2048, s), 512, "proj")
    gates = _matmul_nn(x, w["w_in_gate"], w["b_in_gate"], tm, LANES, "proj_gates")
    qk = _ml_conv_fwd(proj, w["ml_conv_w"], w["ml_conv_b"])
    y, hg_states = _hgrn2_fwd(proj, w["hg_lb_logits"], w["hg_norm_w"])
    y, ct_s, n_s, m_s = _mlstm_fwd(qk, proj, gates, w["ml_norm_w"], y)
    if mid_weights is not None:
        w.update(mid_weights(y))
    z1, x1 = _proj_res_ln(y, w["w_out"], x, w["ln1_g"], w["ln1_b"], "out_proj_ln1")
    kv = _matmul_nn(mem, w["ca_wkv"], None, N_MEM, CA_DH, "kv")
    att, z2, x2 = _cross_attention_fwd(x1, kv, w["ca_wq"], w["ca_wo"], w["ln2_g"], w["ln2_b"])
    if ffn_weights is not None:
        w.update(ffn_weights(x2))
    u = _matmul_nn(x2, w["ffn_w_up"], None, min(2048, s), UP_SHARD_P, "ffn_up", BF16)
    hid = _ffn_conv_fwd(u, w["ffn_conv_w"], w["ffn_conv_b"])
    loss, d_z3, d_ln3_g, d_ln3_b = _proj_loss_tail(hid, w["ffn_w_down"], x2, w["ln3_g"], w["ln3_b"], target)
    grads = {"ln3_g": d_ln3_g, "ln3_b": d_ln3_b}
    grads["ffn_w_down"] = _matmul_tn(hid, d_z3, 1536, D_MODEL, tt, "d_w_down")
    d_hid = _matmul_nt([(d_z3, w["ffn_w_down"])], None, 1.0, tm, D_FF_P, "d_hid", BF16)
    d_ug, d_uv, d_cwg, d_cwv, d_cbg, d_cbv = _ffn_conv_bwd(u, w["ffn_conv_w"], w["ffn_conv_b"], d_hid)
    grads["ffn_conv_w"] = jnp.concatenate([d_cwg, d_cwv], axis=-1)
    grads["ffn_conv_b"] = jnp.concatenate([d_cbg, d_cbv], axis=-1)
    half = N_DEV // 2
    d_w_up = _matmul_tn(x2, d_ug, D_MODEL, UP_SHARD_P, tt, "d_w_up_gate", shards=N_DEV, group=half)
    grads["ffn_w_up"] = _matmul_tn(x2, d_uv, D_MODEL, UP_SHARD_P, tt, "d_w_up_val", shards=N_DEV,
                                   shard0=half, group=half, into=d_w_up)
    d_x2 = _matmul_nt([(d_ug, w["ffn_w_up"], 0), (d_uv, w["ffn_w_up"], N_DEV // 2)], d_z3, ALPHA,
                      min(256, s), D_MODEL, "d_x2")
    if on_ffn_grads is not None:
        d_x2 = on_ffn_grads(grads, d_x2)
    d_x1, d_q, d_z2, d_kv, grads["ln2_g"], grads["ln2_b"] = _cross_attention_bwd(
        d_x2, x1, z2, kv, w["ca_wq"], w["ca_wo"], w["ln2_g"], w["ln2_b"])
    grads["ca_wo"] = _matmul_tn(att, d_z2, D_MODEL, D_MODEL, tt, "d_ca_wo")
    grads["ca_wq"] = _matmul_tn(x1, d_q, D_MODEL, D_MODEL, tt, "d_ca_wq")
    grads["ca_wkv"] = _matmul_tn(mem, d_kv, D_MODEL, CA_DH, N_MEM, "d_ca_wkv", shards=N_DEV, group=N_DEV)
    d_z1, d_y, grads["ln1_g"], grads["ln1_b"] = _ln_bwd_proj(d_x1, z1, w["ln1_g"], w["ln1_b"], w["w_out"],
                                                             "ln1_bwd_out_proj")
    grads["w_out"] = _matmul_tn(y, d_z1, D_MODEL, D_MODEL, tt, "d_w_out")
    if on_mid_grads is not None:
        d_y = on_mid_grads(grads, d_y)
    d_proj, grads["hg_lb_logits"], grads["hg_norm_w"], db_hg = _hgrn2_bwd(
        proj, w["hg_lb_logits"], w["hg_norm_w"], hg_states, d_y)
    d_proj, d_qk, d_gates, grads["ml_norm_w"], db_vo = _mlstm_bwd(
        qk, proj, gates, w["ml_norm_w"], ct_s, n_s, m_s, d_y, d_proj)
    d_proj, grads["ml_conv_w"], grads["ml_conv_b"], db_qk = _ml_conv_bwd(
        proj, w["ml_conv_w"], w["ml_conv_b"], d_qk, d_proj)
    grads["b_in_main"] = jnp.concatenate([db_hg, db_qk, db_vo], axis=-1)
    grads["w_in_gate"], grads["b_in_gate"] = _matmul_tn(x, d_gates, D_MODEL, LANES, tt, "d_w_in_gates", colsum=True)
    if on_small_grads is not None:
        d_proj = on_small_grads(grads, loss, d_proj)
    grads["w_in_main"] = _matmul_tn(x, d_proj, D_MODEL, min(2048, D_IN_MAIN), tt, "d_w_in")
    if on_last_grads is not None:
        d_z1 = on_last_grads(grads, d_z1)
    grad_x = _matmul_nt([(d_proj, w["w_in_main"]), (d_gates, w["w_in_gate"])], d_z1, ALPHA, tm, D_MODEL, "d_x")
    return loss, grad_x, grads


HBM_SPEC = pl.BlockSpec(memory_space=pltpu.HBM)


def _coords():
    return lax.axis_index("x"), lax.axis_index("y"), lax.axis_index("c")


def _other_chips(x, y):
    return [(1 - x, y), (x, 1 - y), (1 - x, 1 - y)]


def _all_gather_two_level(shards, name):
    na = len(shards)

    def body(*refs):
        x_refs, out_refs = refs[:na], refs[na:2 * na]
        send_sems, recv_sems, local_sems = refs[2 * na:]
        x, y, c = _coords()
        me, sibling = (x, y, c), (x, y, 1 - c)
        chips = _other_chips(x, y)

        def copy(a, k, block, to, own=False):
            slot = out_refs[a].at[4 * block[0] + 2 * block[1] + block[2]]
            return pltpu.make_async_remote_copy(
                src_ref=x_refs[a] if own else slot, dst_ref=slot,
                send_sem=send_sems.at[7 * a + k], recv_sem=recv_sems.at[7 * a + k],
                device_id=to, device_id_type=MESH)

        mine = [pltpu.make_async_copy(x_refs[a], out_refs[a].at[4 * x + 2 * y + c], local_sems.at[a])
                for a in range(na)]
        for cp in mine:
            cp.start()
        first = []
        for a in range(na):
            first.append(copy(a, 0, me, sibling, own=True))
            first += [copy(a, 1 + j, me, (*chip, c), own=True) for j, chip in enumerate(chips)]
        for cp in first:
            cp.start()
        passed = []
        for j, chip in enumerate(chips):
            for a in range(na):
                copy(a, 1 + j, (*chip, c), me).wait_recv()
                fwd = copy(a, 4 + j, (*chip, c), sibling)
                fwd.start()
                passed.append(fwd)
        for a in range(na):
            copy(a, 0, sibling, me).wait_recv()
            for j, chip in enumerate(chips):
                copy(a, 4 + j, (*chip, 1 - c), me).wait_recv()
        for cp in first + passed:
            cp.wait_send()
        for cp in mine:
            cp.wait()

    return pl.pallas_call(
        body, name=name,
        out_shape=[jax.ShapeDtypeStruct((N_DEV,) + t.shape, t.dtype) for t in shards],
        in_specs=[HBM_SPEC] * na, out_specs=[HBM_SPEC] * na,
        scratch_shapes=[pltpu.SemaphoreType.DMA((7 * na,)), pltpu.SemaphoreType.DMA((7 * na,)),
                        pltpu.SemaphoreType.DMA((na,))],
    )(*shards)


SEM_SPEC = pl.BlockSpec(memory_space=pltpu.SEMAPHORE)
ANY_SPEC = pl.BlockSpec(memory_space=pl.ANY)
SIDE_EFFECT = pltpu.SideEffectType.DATAFLOW_SIDE_EFFECTING


def _peer(x, y, c, d):
    flip = lambda v, bit: 1 - v if bit else v
    p = (flip(x, d & 4), flip(y, d & 2), flip(c, d & 1))
    return p, 4 * p[0] + 2 * p[1] + p[2]


def _direct_copies(gather, src_refs, land_refs, send_sems, recv_sems):
    x, y, c = _coords()
    me = 4 * x + 2 * y + c
    copies = []
    for a in range(len(src_refs)):
        for d in range(1, N_DEV):
            peer, peer_slot = _peer(x, y, c, d)
            copies.append(pltpu.make_async_remote_copy(
                src_ref=src_refs[a] if gather else src_refs[a].at[peer_slot],
                dst_ref=land_refs[a].at[me] if gather else land_refs[a].at[d - 1],
                send_sem=send_sems.at[7 * a + d - 1], recv_sem=recv_sems.at[7 * a + d - 1],
                device_id=peer, device_id_type=MESH))
    return copies


def _hbm(t):
    return pltpu.HBM(t.shape, t.dtype)


def _direct_start(gather, arrays, through, name):
    na = len(arrays)
    lands = [lax.empty((N_DEV,) + t.shape if gather else (N_DEV - 1,) + t.shape[1:], t.dtype) for t in arrays]
    n_io = 2 * na + 1

    def body(*refs):
        for cp in _direct_copies(gather, refs[:na], refs[na:2 * na], refs[n_io], refs[n_io + 1]):
            cp.start()

    ins = [pltpu.with_memory_space_constraint(t, pltpu.HBM) for t in (*arrays, *lands, through)]
    sems = pltpu.SemaphoreType.DMA((7 * na,))
    res = pl.pallas_call(
        body, name=name, out_shape=(sems, sems, *[_hbm(t) for t in ins]),
        in_specs=[HBM_SPEC] * n_io, out_specs=(SEM_SPEC, SEM_SPEC, *[HBM_SPEC] * n_io),
        input_output_aliases={i: 2 + i for i in range(n_io)},
        compiler_params=pltpu.CompilerParams(has_side_effects=SIDE_EFFECT),
    )(*ins)
    return (res[0], res[1], list(res[2:2 + na]), list(res[2 + na:2 + 2 * na])), res[2 + 2 * na]


def _direct_wait(gather, started, after, name):
    send_sems, recv_sems, arrays, lands = started
    na = len(arrays)

    def body(*refs):
        for cp in _direct_copies(gather, refs[:na], refs[na:2 * na], refs[2 * na], refs[2 * na + 1]):
            cp.wait_send()
            cp.wait_recv()

    res = pl.pallas_call(
        body, name=name, out_shape=tuple(_hbm(t) for t in (*arrays, *lands)),
        in_specs=[HBM_SPEC] * (2 * na) + [SEM_SPEC, SEM_SPEC, ANY_SPEC], out_specs=tuple([HBM_SPEC] * (2 * na)),
        input_output_aliases={i: i for i in range(2 * na)},
        compiler_params=pltpu.CompilerParams(has_side_effects=SIDE_EFFECT),
    )(*arrays, *lands, send_sems, recv_sems, after)
    return list(res[:na]), list(res[na:])


def _row_tile(rows):
    for t in (256, 176, 128):
        if rows % t == 0 and rows > t:
            return t
    return rows


def _adamw_math(g, w, m, v):
    m_new = ADAM_B1 * m + (1.0 - ADAM_B1) * g
    v_new = ADAM_B2 * v + (1.0 - ADAM_B2) * jnp.square(g)
    m_hat = m_new / (1.0 - ADAM_B1 ** ADAM_STEP)
    v_hat = v_new / (1.0 - ADAM_B2 ** ADAM_STEP)
    delta = -ADAM_LR * (m_hat / (jnp.sqrt(v_hat) + ADAM_EPS) + ADAM_WD * w)
    return delta, m_new, v_new


def _adamw_sharded(chip, sums, got, w, m, v, name):
    r, c = w.shape
    tr = _row_tile(r)
    n_got = got.shape[0]

    def body(chip_ref, s_ref, g_ref, w_ref, m_ref, v_ref, go_ref, d_ref, nm_ref, nv_ref):
        g = s_ref[...].astype(F32)
        for i in range(n_got):
            g = g + g_ref[i].astype(F32)
        delta, m_new, v_new = _adamw_math(g, w_ref[...], m_ref[...], v_ref[...])
        go_ref[...] = g
        d_ref[...] = delta
        nm_ref[...] = m_new
        nv_ref[...] = v_new

    blk = pl.BlockSpec((tr, c), lambda i, chip_ref: (i, 0))
    out = jax.ShapeDtypeStruct((r, c), F32)
    return pl.pallas_call(
        body, name=name,
        grid_spec=pltpu.PrefetchScalarGridSpec(
            num_scalar_prefetch=1, grid=(r // tr,),
            in_specs=[pl.BlockSpec((None, tr, c), lambda i, chip_ref: (chip_ref[0], i, 0)),
                      pl.BlockSpec((n_got, tr, c), lambda i, chip_ref: (0, i, 0)), blk, blk, blk],
            out_specs=[blk, blk, blk, blk]),
        out_shape=[out, out, out, out],
        compiler_params=_params(("parallel",)),
    )(chip, sums, got, w, m, v)


def _adamw_replicated(parts, w, m, v):
    p, r, c = parts.shape

    def body(p_ref, w_ref, m_ref, v_ref, g_ref, d_ref, nm_ref, nv_ref):
        g = p_ref[0]
        for i in range(1, p):
            g = g + p_ref[i]
        delta, m_new, v_new = _adamw_math(g, w_ref[...], m_ref[...], v_ref[...])
        g_ref[...] = g
        d_ref[...] = delta
        nm_ref[...] = m_new
        nv_ref[...] = v_new

    blk = pl.BlockSpec((r, c), lambda i: (0, 0))
    out = jax.ShapeDtypeStruct((r, c), F32)
    return pl.pallas_call(
        body, name="adamw_replicated", grid=(1,),
        in_specs=[pl.BlockSpec((p, r, c), lambda i: (0, 0, 0)), blk, blk, blk],
        out_specs=[blk, blk, blk, blk], out_shape=[out, out, out, out],
        compiler_params=_params(("arbitrary",)),
    )(parts, w, m, v)


SHARDED_NAMES = ("w_in", "ml_conv_w", "w_out", "ca_wq", "ca_wkv", "ca_wo", "ffn_w_up", "ffn_conv_w", "ffn_w_down")
SMALL_NAMES = ("b_in", "hg_lb_logits", "hg_norm_w", "ml_conv_b", "ml_norm_w", "ln1_g", "ln1_b",
               "ln2_g", "ln2_b", "ffn_conv_b", "ln3_g", "ln3_b")
WEIGHT_NAMES = ("w_in", "b_in", "hg_lb_logits", "hg_norm_w", "ml_conv_w", "ml_conv_b", "ml_norm_w", "w_out",
                "ln1_g", "ln1_b", "ca_wq", "ca_wkv", "ca_wo", "ln2_g", "ln2_b", "ffn_w_up", "ffn_conv_w",
                "ffn_conv_b", "ffn_w_down", "ln3_g", "ln3_b")
PAD_TO = {"w_in": W_IN_SHARD_P, "ffn_w_up": UP_SHARD_P, "ffn_conv_w": UP_SHARD_P}
SMALL_ROWS = 24
SMALL_W = D_MODEL


def _shard_2d(name, block):
    t = block[0]
    if name in PAD_TO:
        t = jnp.pad(t, ((0, 0), (0, PAD_TO[name] - t.shape[1])))
    return t


def _shard_like(name, t, like):
    return t[:, :like.shape[2]][None]


def _pad_cols(t, width):
    return jnp.pad(t, ((0, 0), (0, width - t.shape[1])))


FIRST_NAMES = ("w_in", "ml_conv_w")
FFN_NAMES = ("ffn_w_up", "ffn_w_down", "ffn_conv_w")
MID_NAMES = ("ca_wo", "ca_wq", "ca_wkv", "w_out")


def _first_weights(g, small):
    w = dict(small)
    w_in = jnp.concatenate([g["w_in"][j, :, :W_IN_SHARD] for j in range(N_DEV)], axis=1)
    w["w_in_main"] = w_in[:, :D_IN_MAIN]
    w["w_in_gate"] = _pad_cols(w_in[:, D_IN_MAIN:], LANES)
    w["b_in_main"] = small["b_in"][:, :D_IN_MAIN]
    w["b_in_gate"] = _pad_cols(small["b_in"][:, D_IN_MAIN:], LANES)
    w["ml_conv_w"] = jnp.transpose(g["ml_conv_w"], (1, 0, 2)).reshape(ML_CONV, 2 * D_GROUP)
    return w


def _mid_weights(g):
    w = {n: g[n].reshape(D_MODEL, D_MODEL) for n in ("w_out", "ca_wq", "ca_wo")}
    w["ca_wkv"] = g["ca_wkv"]
    return w


def _ffn_weights(g, small):
    w = {"ffn_w_up": g["ffn_w_up"]}
    down = g["ffn_w_down"].reshape(N_DEV // 2, UP_SHARD, D_MODEL)
    w["ffn_w_down"] = jnp.pad(down, ((0, 0), (0, UP_SHARD_P - UP_SHARD), (0, 0))).reshape(D_FF_P, D_MODEL)
    w["ffn_conv_w"] = jnp.transpose(g["ffn_conv_w"], (1, 0, 2)).reshape(FFN_CONV, D_UP_P)
    w["ffn_conv_b"] = _pad_cols(small["ffn_conv_b"].reshape(N_DEV, UP_SHARD), UP_SHARD_P).reshape(1, D_UP_P)
    return w


def _whole_weights(g, small):
    return {**_first_weights(g, small), **_mid_weights(g), **_ffn_weights(g, small)}


def _owner_stack(n, grads):
    if n == "w_in":
        w_in = jnp.concatenate([grads["w_in_main"], grads["w_in_gate"][:, :D_IN - D_IN_MAIN]], axis=1)
        return jnp.stack([_pad_cols(w_in[:, j * W_IN_SHARD:(j + 1) * W_IN_SHARD], W_IN_SHARD_P)
                          for j in range(N_DEV)])
    if n in ("w_out", "ca_wq", "ca_wo"):
        return grads[n].reshape(N_DEV, D_MODEL // N_DEV, D_MODEL)
    if n == "ffn_w_down":
        down = grads[n].reshape(N_DEV // 2, UP_SHARD_P, D_MODEL)[:, :UP_SHARD]
        return down.reshape(N_DEV, D_FF // N_DEV, D_MODEL)
    if n == "ml_conv_w":
        return jnp.transpose(grads[n].reshape(ML_CONV, N_DEV, LANES), (1, 0, 2))
    if n == "ffn_conv_w":
        return jnp.transpose(grads[n].reshape(FFN_CONV, N_DEV, UP_SHARD_P), (1, 0, 2))
    return grads[n]


def _owner_stacks(grads):
    return {n: _owner_stack(n, grads) for n in SHARDED_NAMES}


def _small_grads(grads):
    out = {n: grads[n] for n in SMALL_NAMES if n in grads}
    out["b_in"] = jnp.concatenate([grads["b_in_main"], grads["b_in_gate"][:, :D_IN - D_IN_MAIN]], axis=1)
    out["ffn_conv_b"] = grads["ffn_conv_b"].reshape(N_DEV, UP_SHARD_P)[:, :UP_SHARD].reshape(1, D_UP)
    return out


def _pack_small(p, extra=None):
    flat = [p[n].reshape(-1) for n in SMALL_NAMES]
    if extra is not None:
        flat.append(extra.reshape(-1))
    flat = jnp.concatenate(flat)
    return jnp.pad(flat, (0, SMALL_ROWS * SMALL_W - flat.shape[0])).reshape(SMALL_ROWS, SMALL_W)


def _unpack_small(slab, like):
    out = {}
    flat = slab.reshape(-1)
    o = 0
    for n in SMALL_NAMES:
        out[n] = flat[o:o + like[n].size].reshape(like[n].shape)
        o += like[n].size
    return out, flat[o]


def kernel(x, mem, w_in, b_in, hg_lb_logits, hg_norm_w, ml_conv_w, ml_conv_b, ml_norm_w, w_out, ln1_g, ln1_b, ca_wq, ca_wkv, ca_wo, ln2_g, ln2_b, ffn_w_up, ffn_conv_w, ffn_conv_b, ffn_w_down, ln3_g, ln3_b, loss_target, m_w_in, m_b_in, m_hg_lb_logits, m_hg_norm_w, m_ml_conv_w, m_ml_conv_b, m_ml_norm_w, m_w_out, m_ln1_g, m_ln1_b, m_ca_wq, m_ca_wkv, m_ca_wo, m_ln2_g, m_ln2_b, m_ffn_w_up, m_ffn_conv_w, m_ffn_conv_b, m_ffn_w_down, m_ln3_g, m_ln3_b, v_w_in, v_b_in, v_hg_lb_logits, v_hg_norm_w, v_ml_conv_w, v_ml_conv_b, v_ml_norm_w, v_w_out, v_ln1_g, v_ln1_b, v_ca_wq, v_ca_wkv, v_ca_wo, v_ln2_g, v_ln2_b, v_ffn_w_up, v_ffn_conv_w, v_ffn_conv_b, v_ffn_w_down, v_ln3_g, v_ln3_b):
    params = dict(w_in=w_in, b_in=b_in, hg_lb_logits=hg_lb_logits, hg_norm_w=hg_norm_w, ml_conv_w=ml_conv_w,
                  ml_conv_b=ml_conv_b, ml_norm_w=ml_norm_w, w_out=w_out, ln1_g=ln1_g, ln1_b=ln1_b, ca_wq=ca_wq,
                  ca_wkv=ca_wkv, ca_wo=ca_wo, ln2_g=ln2_g, ln2_b=ln2_b, ffn_w_up=ffn_w_up, ffn_conv_w=ffn_conv_w,
                  ffn_conv_b=ffn_conv_b, ffn_w_down=ffn_w_down, ln3_g=ln3_g, ln3_b=ln3_b)
    mom1 = dict(w_in=m_w_in, b_in=m_b_in, hg_lb_logits=m_hg_lb_logits, hg_norm_w=m_hg_norm_w,
                ml_conv_w=m_ml_conv_w, ml_conv_b=m_ml_conv_b, ml_norm_w=m_ml_norm_w, w_out=m_w_out, ln1_g=m_ln1_g,
                ln1_b=m_ln1_b, ca_wq=m_ca_wq, ca_wkv=m_ca_wkv, ca_wo=m_ca_wo, ln2_g=m_ln2_g, ln2_b=m_ln2_b,
                ffn_w_up=m_ffn_w_up, ffn_conv_w=m_ffn_conv_w, ffn_conv_b=m_ffn_conv_b, ffn_w_down=m_ffn_w_down,
                ln3_g=m_ln3_g, ln3_b=m_ln3_b)
    mom2 = dict(w_in=v_w_in, b_in=v_b_in, hg_lb_logits=v_hg_lb_logits, hg_norm_w=v_hg_norm_w,
                ml_conv_w=v_ml_conv_w, ml_conv_b=v_ml_conv_b, ml_norm_w=v_ml_norm_w, w_out=v_w_out, ln1_g=v_ln1_g,
                ln1_b=v_ln1_b, ca_wq=v_ca_wq, ca_wkv=v_ca_wkv, ca_wo=v_ca_wo, ln2_g=v_ln2_g, ln2_b=v_ln2_b,
                ffn_w_up=v_ffn_w_up, ffn_conv_w=v_ffn_conv_w, ffn_conv_b=v_ffn_conv_b, ffn_w_down=v_ffn_w_down,
                ln3_g=v_ln3_g, ln3_b=v_ln3_b)

    x_idx, y_idx, c_idx = _coords()
    as_index = lambda v: jnp.reshape(v, (1,)).astype(jnp.int32)
    me = as_index(4 * x_idx + 2 * y_idx + c_idx)
    small_params = {n: params[n] for n in SMALL_NAMES}

    shards = {n: _shard_2d(n, params[n]) for n in SHARDED_NAMES}
    to_send = lambda names: [shards[n] if "conv" in n else shards[n].astype(BF16) for n in names]
    first = dict(zip(FIRST_NAMES, _all_gather_two_level(to_send(FIRST_NAMES), "weights_gather_first")))
    mid_started, through = _direct_start(True, to_send(MID_NAMES), first["w_in"], "weights_gather_start_mid")
    ffn_started, first["w_in"] = _direct_start(True, to_send(FFN_NAMES), through, "weights_gather_start_ffn")

    def gathered_weights(names, started, after, tag):
        mine, lands = _direct_wait(True, started, after, "weights_gather_wait_" + tag)
        return {n: lax.dynamic_update_index_in_dim(land, own, me[0], 0) for n, own, land in zip(names, mine, lands)}

    started, own_stacks = {}, {}

    def start_group(names, tag):
        def hook(grads, through):
            own_stacks[tag] = [_owner_stack(n, grads).astype(BF16) for n in names]
            started[tag], through = _direct_start(False, own_stacks[tag], through, "grads_start_" + tag)
            return through
        return hook

    def start_small(grads, loss, through):
        started["small"], through = _direct_start(True, [_pack_small(_small_grads(grads), loss)], through,
                                                  "small_gather_start")
        return through

    loss, grad_x, grads = _local_step(
        x[0], mem[0], loss_target[0], _first_weights(first, small_params),
        lambda y: _mid_weights(gathered_weights(MID_NAMES, mid_started, y, "mid")),
        lambda x2: _ffn_weights(gathered_weights(FFN_NAMES, ffn_started, x2, "ffn"), small_params),
        start_group(FFN_NAMES, "ffn"), start_group(MID_NAMES, "mid"), start_small, start_group(FIRST_NAMES, "last"))

    sharded_out = {}
    after = grad_x
    for names, tag in ((FFN_NAMES, "ffn"), (MID_NAMES, "mid"), (FIRST_NAMES, "last")):
        _, lands = _direct_wait(False, started[tag], after, "grads_wait_" + tag)
        for n, st, land in zip(names, own_stacks[tag], lands):
            res = _adamw_sharded(me, st, land, shards[n], _shard_2d(n, mom1[n]), _shard_2d(n, mom2[n]), "adamw_" + n)
            sharded_out[n] = [_shard_like(n, t, params[n]) for t in res]
            after = res[0]
    own_small, small_lands = _direct_wait(True, started["small"], after, "small_gather_wait")
    small_parts = lax.dynamic_update_index_in_dim(small_lands[0], own_small[0], me[0], 0)
    small_res = _adamw_replicated(small_parts, _pack_small(params), _pack_small(mom1), _pack_small(mom2))

    outs = []
    total_loss = None
    for k in range(4):
        small, extra = _unpack_small(small_res[k], params)
        if total_loss is None:
            total_loss = extra
        outs.extend(sharded_out[n][k] if n in sharded_out else small[n] for n in WEIGHT_NAMES)
    return (total_loss, grad_x[None], *outs)
```

```python
import functools
import math

import jax
import jax.numpy as jnp
from jax import lax
from jax.experimental import pallas as pl
from jax.experimental.pallas import tpu as pltpu

F32 = jnp.float32
BF16 = jnp.bfloat16
HIGHEST = lax.Precision.HIGHEST
MESH = pl.DeviceIdType.MESH

N_DEV = 8
D_MODEL = 1024
N_MEM = 256
N_HEADS = 4
D_HEAD = 128
D_GROUP = N_HEADS * D_HEAD
CHUNK = 64
ML_CONV = 4
FFN_CONV = 3
D_FF = 2816
D_UP = 2 * D_FF
CA_HEADS = 4
CA_DH = D_MODEL // CA_HEADS
LANES = 128
SUBLANES = 8
D_IN = 8 * D_GROUP + 2 * N_HEADS
D_IN_MAIN = 8 * D_GROUP
W_IN_SHARD = D_IN // N_DEV
UP_SHARD = D_UP // N_DEV
UP_SHARD_P = 768
D_UP_P = N_DEV * UP_SHARD_P
D_FF_P = D_UP_P // 2
ALPHA = 2.0 ** 0.25
LN_EPS = 1e-5
NEG_BIG = -1e30
ADAM_LR = 0.001
ADAM_B1 = 0.9
ADAM_B2 = 0.999
ADAM_EPS = 1e-08
ADAM_WD = 0.01
ADAM_STEP = 10
VMEM_LIMIT = 56 * 1024 * 1024

SEG_HQ, SEG_HF, SEG_HI, SEG_HG, SEG_MQ, SEG_MK, SEG_MV, SEG_MO = (4 * i for i in range(8))


def _params(sem):
    return pltpu.CompilerParams(dimension_semantics=sem, vmem_limit_bytes=VMEM_LIMIT)


def _dg(a, b, ca, cb, precision=None):
    return lax.dot_general(a, b, (((ca,), (cb,)), ((), ())), precision=precision,
                           preferred_element_type=F32)


def _nn_raw(a, b):
    return _dg(a.astype(BF16), b.astype(BF16), 1, 0)


def _nt_raw(a, b):
    return _dg(a.astype(BF16), b.astype(BF16), 1, 1)


def _tn_raw(a, b):
    return _dg(a.astype(BF16), b.astype(BF16), 0, 0)


@jax.custom_vjp
def _nn(a, b):
    return _nn_raw(a, b)


_nn.defvjp(lambda a, b: (_nn_raw(a, b), (a, b)),
           lambda res, g: (_nt_raw(g, res[1]), _tn_raw(res[0], g)))


@jax.custom_vjp
def _nt(a, b):
    return _nt_raw(a, b)


_nt.defvjp(lambda a, b: (_nt_raw(a, b), (a, b)),
           lambda res, g: (_nn_raw(g, res[1]), _tn_raw(g, res[0])))


@jax.custom_vjp
def _tn(a, b):
    return _tn_raw(a, b)


_tn.defvjp(lambda a, b: (_tn_raw(a, b), (a, b)),
           lambda res, g: (_nt_raw(res[1], g), _nn_raw(res[0], g)))


def _layer_norm(z, g, b):
    mu = jnp.mean(z, axis=-1, keepdims=True)
    var = jnp.mean(jnp.square(z - mu), axis=-1, keepdims=True)
    return (z - mu) * lax.rsqrt(var + LN_EPS) * g + b


def _matmul_nn(a, w, bias, tm, tn, name, out_dtype=F32):
    m, k = a.shape
    if w.ndim == 3:
        n = w.shape[0] * w.shape[2]
        assert tn == w.shape[2]
        w_spec = pl.BlockSpec((None, k, tn), lambda i, j: (j, 0, 0))
    else:
        n = w.shape[1]
        w_spec = pl.BlockSpec((k, tn), lambda i, j: (0, j))

    def body(*refs):
        a_ref, w_ref = refs[0], refs[1]
        o_ref = refs[-1]
        acc = _nn_raw(a_ref[...], w_ref[...])
        if bias is not None:
            acc = acc + refs[2][...]
        o_ref[...] = acc.astype(o_ref.dtype)

    in_specs = [pl.BlockSpec((tm, k), lambda i, j: (i, 0)), w_spec]
    args = [a, w]
    if bias is not None:
        in_specs.append(pl.BlockSpec((1, tn), lambda i, j: (0, j)))
        args.append(bias)
    return pl.pallas_call(
        body, name=name, grid=(m // tm, n // tn), in_specs=in_specs,
        out_specs=pl.BlockSpec((tm, tn), lambda i, j: (i, j)),
        out_shape=jax.ShapeDtypeStruct((m, n), out_dtype),
        compiler_params=_params(("parallel", "parallel")),
    )(*args)


def _matmul_nt(pairs, add, scale, tm, tk, name, out_dtype=F32):
    m = pairs[0][0].shape[0]
    k = pairs[0][1].shape[-2]
    groups = []
    in_specs, args = [], []
    for pair in pairs:
        d, w = pair[0], pair[1]
        in_specs.append(pl.BlockSpec((tm, d.shape[1]), lambda i, j: (i, 0)))
        if w.ndim == 3:
            g = d.shape[1] // w.shape[2]
            blk = pair[2] // g
            in_specs.append(pl.BlockSpec((g, tk, w.shape[2]), lambda i, j, blk=blk: (blk, j, 0)))
            groups.append((g, w.shape[2]))
        else:
            in_specs.append(pl.BlockSpec((tk, w.shape[1]), lambda i, j: (j, 0)))
            groups.append(None)
        args += [d, w]
    if add is not None:
        in_specs.append(pl.BlockSpec((tm, tk), lambda i, j: (i, j)))
        args.append(add)

    def body(*refs):
        o_ref = refs[-1]
        acc = None
        for p, grp in enumerate(groups):
            d_ref, w_ref = refs[2 * p], refs[2 * p + 1]
            if grp is None:
                terms = [_nt_raw(d_ref[...], w_ref[...])]
            else:
                terms = [_nt_raw(d_ref[:, g * grp[1]:(g + 1) * grp[1]], w_ref[g]) for g in range(grp[0])]
            for t in terms:
                acc = t if acc is None else acc + t
        if add is not None:
            acc = acc + scale * refs[2 * len(groups)][...]
        o_ref[...] = acc.astype(o_ref.dtype)

    return pl.pallas_call(
        body, name=name, grid=(m // tm, k // tk), in_specs=in_specs,
        out_specs=pl.BlockSpec((tm, tk), lambda i, j: (i, j)),
        out_shape=jax.ShapeDtypeStruct((m, k), out_dtype),
        compiler_params=_params(("parallel", "parallel")),
    )(*args)


def _matmul_tn(a, b, tm, tn, tt, name, shards=None, shard0=0, group=1, into=None, colsum=False):
    t, m = a.shape
    n = b.shape[1]
    assert not colsum or tm == m
    n_in = 2 + (into is not None)
    out_dtype = BF16
    per_step = 1 if shards is None else group
    width = per_step * tn

    def body(*refs):
        a_ref, b_ref = refs[0], refs[1]
        o_ref, acc_ref = refs[n_in], refs[-1]
        first = pl.program_id(2) == 0

        @pl.when(first)
        def _():
            acc_ref[...] = jnp.zeros_like(acc_ref)

        if shards is None:
            acc_ref[...] += _tn_raw(a_ref[...], b_ref[...])
        else:
            lhs = a_ref[...].astype(BF16)
            for g in range(per_step):
                acc_ref[g] += _tn_raw(lhs, b_ref[:, g * tn:(g + 1) * tn])

        @pl.when(pl.program_id(2) == t // tt - 1)
        def _():
            o_ref[...] = acc_ref[...].astype(o_ref.dtype)

        if colsum:
            s_ref = refs[n_in + 1]

            @pl.when(first)
            def _():
                s_ref[...] = jnp.zeros_like(s_ref)

            s_ref[...] += jnp.sum(b_ref[...], axis=0, keepdims=True)

    in_specs = [pl.BlockSpec((tt, tm), lambda i, j, kk: (kk, i)),
                pl.BlockSpec((tt, width), lambda i, j, kk: (kk, j))]
    args = [a, b]
    aliases = {}
    if into is not None:
        in_specs.append(pl.BlockSpec(memory_space=pl.ANY))
        args.append(into)
        aliases = {2: 0}
    if shards is None:
        out_specs = [pl.BlockSpec((tm, tn), lambda i, j, kk: (i, j))]
        out_shape = [jax.ShapeDtypeStruct((m, n), out_dtype)]
        acc = pltpu.VMEM((tm, tn), F32)
    else:
        out_specs = [pl.BlockSpec((per_step, tm, tn), lambda i, j, kk: (shard0 // per_step + j, i, 0))]
        out_shape = [jax.ShapeDtypeStruct((shards, m, tn), out_dtype)]
        acc = pltpu.VMEM((per_step, tm, tn), F32)
    if colsum:
        out_specs.append(pl.BlockSpec((1, tn), lambda i, j, kk: (0, j)))
        out_shape.append(jax.ShapeDtypeStruct((1, n), F32))
    res = pl.pallas_call(
        body, name=name, grid=(m // tm, n // width, t // tt), in_specs=in_specs, out_specs=out_specs,
        out_shape=out_shape, input_output_aliases=aliases, scratch_shapes=[acc],
        compiler_params=_params(("parallel", "parallel", "arbitrary")),
    )(*args)
    return res if colsum else res[0]


ROW_TILE = 64


def _stack(ref, start, rows):
    return ref[pl.ds(start, rows), :].astype(F32).reshape(rows // SUBLANES, SUBLANES, LANES)


def _vreg_rows(ref, n):
    return [jnp.broadcast_to(ref[j:j + 1, :], (SUBLANES, LANES))[None] for j in range(n)]


def _column_total(acc):
    return jnp.sum(acc, axis=0, keepdims=True)


def _conv_fwd_tile(pad_ref, taps_w, bias, r0, rows):
    taps = len(taps_w)
    acc = bias
    for j in range(taps):
        acc = acc + _stack(pad_ref, SUBLANES - (taps - 1 - j) + r0, rows) * taps_w[j]
    return acc


def _conv_grads_tile(pad_ref, dpad_ref, dx_ref, taps_w, dws, r0, rows):
    taps = len(taps_w)
    x_rows = _stack(pad_ref, SUBLANES + r0, rows)
    dx = None
    for j in range(taps):
        d_shifted = _stack(dpad_ref, r0 + (taps - 1 - j), rows)
        term = d_shifted * taps_w[j]
        dx = term if dx is None else dx + term
        dws[j] = dws[j] + jnp.sum(d_shifted * x_rows, axis=0)
    dx_ref[r0:r0 + rows, :] = dx.reshape(rows, LANES).astype(dx_ref.dtype)
    return jnp.sum(dx, axis=0)


def _ml_conv_fwd(proj, conv_w, conv_b):
    s = proj.shape[0]
    nblk = 2 * D_GROUP // LANES

    def body(x_ref, w_ref, b_ref, o_ref, pad_ref):
        pad_ref[0:SUBLANES, :] = jnp.zeros((SUBLANES, LANES), F32)
        pad_ref[SUBLANES:, :] = x_ref[...].astype(F32)
        taps_w, bias = _vreg_rows(w_ref, ML_CONV), _vreg_rows(b_ref, 1)[0]
        for r0 in range(0, s, ROW_TILE):
            rows = min(ROW_TILE, s - r0)
            o_ref[r0:r0 + rows, :] = jax.nn.silu(_conv_fwd_tile(pad_ref, taps_w, bias, r0, rows)).reshape(rows, LANES)

    return pl.pallas_call(
        body, name="ml_conv_fwd", grid=(nblk,),
        in_specs=[pl.BlockSpec((s, LANES), lambda j: (0, SEG_MQ + j)),
                  pl.BlockSpec((ML_CONV, LANES), lambda j: (0, j)),
                  pl.BlockSpec((1, LANES), lambda j: (0, j))],
        out_specs=pl.BlockSpec((s, LANES), lambda j: (0, j)),
        out_shape=jax.ShapeDtypeStruct((s, 2 * D_GROUP), F32),
        scratch_shapes=[pltpu.VMEM((s + SUBLANES, LANES), F32)],
        compiler_params=_params(("parallel",)),
    )(proj, conv_w, conv_b)


def _ml_conv_bwd(proj, conv_w, conv_b, d_qk, d_proj):
    s = proj.shape[0]
    nblk = 2 * D_GROUP // LANES

    def body(x_ref, w_ref, b_ref, dy_ref, _, dx_ref, dw_ref, db_ref, dxs_ref, pad_ref, dpad_ref):
        pad_ref[0:SUBLANES, :] = jnp.zeros((SUBLANES, LANES), F32)
        pad_ref[SUBLANES:, :] = x_ref[...].astype(F32)
        dpad_ref[s:, :] = jnp.zeros((SUBLANES, LANES), F32)
        taps_w, bias = _vreg_rows(w_ref, ML_CONV), _vreg_rows(b_ref, 1)[0]
        db = jnp.zeros((SUBLANES, LANES), F32)
        for r0 in range(0, s, ROW_TILE):
            rows = min(ROW_TILE, s - r0)
            pre = _conv_fwd_tile(pad_ref, taps_w, bias, r0, rows)
            _, vjp = jax.vjp(jax.nn.silu, pre)
            d_pre, = vjp(_stack(dy_ref, r0, rows))
            dpad_ref[r0:r0 + rows, :] = d_pre.reshape(rows, LANES)
            db = db + jnp.sum(d_pre, axis=0)
        db_ref[...] = _column_total(db)
        dws = [jnp.zeros((SUBLANES, LANES), F32) for _ in range(ML_CONV)]
        dx_sum = jnp.zeros((SUBLANES, LANES), F32)
        for r0 in range(0, s, ROW_TILE):
            dx_sum = dx_sum + _conv_grads_tile(pad_ref, dpad_ref, dx_ref, taps_w, dws, r0, min(ROW_TILE, s - r0))
        dxs_ref[...] = _column_total(dx_sum)
        for j in range(ML_CONV):
            dw_ref[j:j + 1, :] = _column_total(dws[j])

    return pl.pallas_call(
        body, name="ml_conv_bwd", grid=(nblk,),
        in_specs=[pl.BlockSpec((s, LANES), lambda j: (0, SEG_MQ + j)),
                  pl.BlockSpec((ML_CONV, LANES), lambda j: (0, j)),
                  pl.BlockSpec((1, LANES), lambda j: (0, j)),
                  pl.BlockSpec((s, LANES), lambda j: (0, j)),
                  pl.BlockSpec(memory_space=pl.ANY)],
        out_specs=[pl.BlockSpec((s, LANES), lambda j: (0, SEG_MQ + j)),
                   pl.BlockSpec((ML_CONV, LANES), lambda j: (0, j)),
                   pl.BlockSpec((1, LANES), lambda j: (0, j)),
                   pl.BlockSpec((1, LANES), lambda j: (0, j))],
        out_shape=[jax.ShapeDtypeStruct(d_proj.shape, d_proj.dtype),
                   jax.ShapeDtypeStruct((ML_CONV, 2 * D_GROUP), F32),
                   jax.ShapeDtypeStruct((1, 2 * D_GROUP), F32),
                   jax.ShapeDtypeStruct((1, 2 * D_GROUP), F32)],
        input_output_aliases={4: 0},
        scratch_shapes=[pltpu.VMEM((s + SUBLANES, LANES), F32), pltpu.VMEM((s + SUBLANES, LANES), F32)],
        compiler_params=_params(("parallel",)),
    )(proj, conv_w, conv_b, d_qk, d_proj)


def _gelu_mul(a, b):
    return jax.nn.gelu(a) * b


GELU_C = math.sqrt(2.0 / math.pi)
GELU_K = 0.044715


def _gelu_mul_grads(a, b, d):
    a2 = a * a
    t = jnp.tanh(GELU_C * (a + GELU_K * (a * a2)))
    cdf = 0.5 * (1.0 + t)
    slope = cdf + (0.5 * GELU_C) * a * (1.0 - t * t) * (1.0 + (3.0 * GELU_K) * a2)
    return d * b * slope, d * (a * cdf)


FFN_BLOCKS = D_FF_P // LANES


def _ffn_conv_fwd(u, conv_w, conv_b):
    s = u.shape[0]

    def body(g_ref, v_ref, wg_ref, wv_ref, bg_ref, bv_ref, o_ref, gpad_ref, vpad_ref):
        for pad_ref, x_ref in ((gpad_ref, g_ref), (vpad_ref, v_ref)):
            pad_ref[0:SUBLANES, :] = jnp.zeros((SUBLANES, LANES), F32)
            pad_ref[SUBLANES:, :] = x_ref[...].astype(F32)
        taps_g, bias_g = _vreg_rows(wg_ref, FFN_CONV), _vreg_rows(bg_ref, 1)[0]
        taps_v, bias_v = _vreg_rows(wv_ref, FFN_CONV), _vreg_rows(bv_ref, 1)[0]
        for r0 in range(0, s, ROW_TILE):
            rows = min(ROW_TILE, s - r0)
            ug = _conv_fwd_tile(gpad_ref, taps_g, bias_g, r0, rows)
            uv = _conv_fwd_tile(vpad_ref, taps_v, bias_v, r0, rows)
            o_ref[r0:r0 + rows, :] = _gelu_mul(ug, uv).reshape(rows, LANES).astype(o_ref.dtype)

    col = lambda off: (lambda j: (0, off + j))
    return pl.pallas_call(
        body, name="ffn_conv_fwd", grid=(FFN_BLOCKS,),
        in_specs=[pl.BlockSpec((s, LANES), col(0)), pl.BlockSpec((s, LANES), col(FFN_BLOCKS)),
                  pl.BlockSpec((FFN_CONV, LANES), col(0)), pl.BlockSpec((FFN_CONV, LANES), col(FFN_BLOCKS)),
                  pl.BlockSpec((1, LANES), col(0)), pl.BlockSpec((1, LANES), col(FFN_BLOCKS))],
        out_specs=pl.BlockSpec((s, LANES), col(0)),
        out_shape=jax.ShapeDtypeStruct((s, D_FF_P), BF16),
        scratch_shapes=[pltpu.VMEM((s + SUBLANES, LANES), F32), pltpu.VMEM((s + SUBLANES, LANES), F32)],
        compiler_params=_params(("parallel",)),
    )(u, u, conv_w, conv_w, conv_b, conv_b)


def _ffn_conv_bwd(u, conv_w, conv_b, d_h):
    s = u.shape[0]

    def body(g_ref, v_ref, wg_ref, wv_ref, bg_ref, bv_ref, dh_ref,
             dug_ref, duv_ref, dwg_ref, dwv_ref, dbg_ref, dbv_ref,
             gpad_ref, vpad_ref, dgpad_ref, dvpad_ref):
        for pad_ref, x_ref in ((gpad_ref, g_ref), (vpad_ref, v_ref)):
            pad_ref[0:SUBLANES, :] = jnp.zeros((SUBLANES, LANES), F32)
            pad_ref[SUBLANES:, :] = x_ref[...].astype(F32)
        dgpad_ref[s:, :] = jnp.zeros((SUBLANES, LANES), F32)
        dvpad_ref[s:, :] = jnp.zeros((SUBLANES, LANES), F32)
        taps_g, bias_g = _vreg_rows(wg_ref, FFN_CONV), _vreg_rows(bg_ref, 1)[0]
        taps_v, bias_v = _vreg_rows(wv_ref, FFN_CONV), _vreg_rows(bv_ref, 1)[0]
        dbg = jnp.zeros((SUBLANES, LANES), F32)
        dbv = jnp.zeros((SUBLANES, LANES), F32)
        for r0 in range(0, s, ROW_TILE):
            rows = min(ROW_TILE, s - r0)
            ug = _conv_fwd_tile(gpad_ref, taps_g, bias_g, r0, rows)
            uv = _conv_fwd_tile(vpad_ref, taps_v, bias_v, r0, rows)
            d_ug, d_uv = _gelu_mul_grads(ug, uv, _stack(dh_ref, r0, rows))
            dgpad_ref[r0:r0 + rows, :] = d_ug.reshape(rows, LANES)
            dvpad_ref[r0:r0 + rows, :] = d_uv.reshape(rows, LANES)
            dbg = dbg + jnp.sum(d_ug, axis=0)
            dbv = dbv + jnp.sum(d_uv, axis=0)
        dbg_ref[...] = _column_total(dbg)
        dbv_ref[...] = _column_total(dbv)
        for pad_ref, dpad_ref, taps_w, dx_ref, dw_ref in ((gpad_ref, dgpad_ref, taps_g, dug_ref, dwg_ref),
                                                          (vpad_ref, dvpad_ref, taps_v, duv_ref, dwv_ref)):
            dws = [jnp.zeros((SUBLANES, LANES), F32) for _ in range(FFN_CONV)]
            for r0 in range(0, s, ROW_TILE):
                _conv_grads_tile(pad_ref, dpad_ref, dx_ref, taps_w, dws, r0, min(ROW_TILE, s - r0))
            for j in range(FFN_CONV):
                dw_ref[j:j + 1, :] = _column_total(dws[j])

    col = lambda off: (lambda j: (0, off + j))
    seq = pl.BlockSpec((s, LANES), col(0))
    return pl.pallas_call(
        body, name="ffn_conv_bwd", grid=(FFN_BLOCKS,),
        in_specs=[pl.BlockSpec((s, LANES), col(0)), pl.BlockSpec((s, LANES), col(FFN_BLOCKS)),
                  pl.BlockSpec((FFN_CONV, LANES), col(0)), pl.BlockSpec((FFN_CONV, LANES), col(FFN_BLOCKS)),
                  pl.BlockSpec((1, LANES), col(0)), pl.BlockSpec((1, LANES), col(FFN_BLOCKS)), seq],
        out_specs=[seq, seq, pl.BlockSpec((FFN_CONV, LANES), col(0)), pl.BlockSpec((FFN_CONV, LANES), col(0)),
                   pl.BlockSpec((1, LANES), col(0)), pl.BlockSpec((1, LANES), col(0))],
        out_shape=[jax.ShapeDtypeStruct((s, D_FF_P), BF16), jax.ShapeDtypeStruct((s, D_FF_P), BF16),
                   jax.ShapeDtypeStruct((FFN_CONV, D_FF_P), F32), jax.ShapeDtypeStruct((FFN_CONV, D_FF_P), F32),
                   jax.ShapeDtypeStruct((1, D_FF_P), F32), jax.ShapeDtypeStruct((1, D_FF_P), F32)],
        scratch_shapes=[pltpu.VMEM((s + SUBLANES, LANES), F32) for _ in range(4)],
        compiler_params=_params(("parallel",)),
    )(u, u, conv_w, conv_w, conv_b, conv_b, d_h)


def _chunk_masks(c):
    row = lax.broadcasted_iota(jnp.int32, (c, c), 0)
    col = lax.broadcasted_iota(jnp.int32, (c, c), 1)
    return row, col


@jax.custom_vjp
def _split_heads(x):
    return tuple(x[:, h * D_HEAD:(h + 1) * D_HEAD] for h in range(N_HEADS))


_split_heads.defvjp(lambda x: (_split_heads(x), None), lambda _, gs: (jnp.concatenate(gs, axis=1),))


@jax.custom_vjp
def _merge_heads(xs):
    return jnp.concatenate(xs, axis=1)


_merge_heads.defvjp(lambda xs: (_merge_heads(xs), None), lambda _, g: (_split_heads(g),))


@jax.custom_vjp
def _split_chunks(x):
    return tuple(x[i * CHUNK:(i + 1) * CHUNK] for i in range(x.shape[0] // CHUNK))


_split_chunks.defvjp(lambda x: (_split_chunks(x), None), lambda _, gs: (jnp.concatenate(gs, axis=0),))


@jax.custom_vjp
def _merge_chunks(xs):
    return jnp.concatenate(xs, axis=0)


_merge_chunks.defvjp(lambda xs: (_merge_chunks(xs), None), lambda _, g: (_split_chunks(g),))


def _blocks(x):
    return [_split_heads(rows) for rows in _split_chunks(x)]


def _per_chunk_rows(per_chunk, rid):
    out = per_chunk[0]
    for i in range(1, len(per_chunk)):
        out = jnp.where(rid >= i * CHUNK, per_chunk[i], out)
    return out


HEADS = range(N_HEADS)
CHUNKS_PER_STEP = 4
ML_CHUNKS_PER_STEP = 1


def _hg_chunk(hq, hf, hi, hgate, l0, l1, nw, sts):
    n = hq.shape[0] // CHUNK
    row, col = _chunk_masks(n * CHUNK)
    same_chunk = functools.reduce(jnp.logical_or, [(row >= i * CHUNK) & (row < (i + 1) * CHUNK) &
                                                   (col >= i * CHUNK) & (col < (i + 1) * CHUNK) for i in range(n)])
    causal = _chunk_masks(CHUNK)
    causal = causal[1] <= causal[0]
    mx = lax.stop_gradient(jnp.maximum(l0, l1))
    e0 = jnp.exp(l0 - mx)
    e1 = jnp.exp(l1 - mx)
    lb = e0 / (e0 + e1)
    sig = jax.nn.sigmoid(hf)
    lf = jnp.log(lb + (1.0 - lb) * sig)
    k = (1.0 - lb) * jax.nn.sigmoid(-hf)
    q = jax.nn.silu(hq)
    b = _dg(((col <= row) & same_chunk).astype(F32), lf, 1, 0, HIGHEST)
    rid = lax.broadcasted_iota(jnp.int32, b.shape, 0)
    pick = lambda r: jnp.sum(jnp.where(rid == r, b, 0.0), axis=0, keepdims=True)
    b_last_c = [pick(i * CHUNK + CHUNK - 1) for i in range(n)]
    b_ref = _per_chunk_rows([pick(i * CHUNK + CHUNK // 2 - 1) for i in range(n)], rid)
    b_last = _per_chunk_rows(b_last_c, rid)
    qa = _blocks(q * jnp.exp(b - b_ref))
    ka = _blocks(k * jnp.exp(b_ref - b))
    qe = _blocks(q * jnp.exp(b))
    kd = _blocks(k * jnp.exp(b_last - b))
    decay = [_split_heads(jnp.exp(b_last_c[i])) for i in range(n)]
    v = _blocks(hi)
    chunks = range(n)
    attn = [[jnp.where(causal, _nt(qa[i][h], ka[i][h]), 0.0) for h in HEADS] for i in chunks]
    intra = [[_nn(attn[i][h], v[i][h]) for h in HEADS] for i in chunks]
    kv = [[_tn(v[i][h], kd[i][h]) for h in HEADS] for i in chunks]
    normed = []
    for i in chunks:
        inter = [_nt(qe[i][h], sts[h]) for h in HEADS]
        sts = tuple(decay[i][h] * sts[h] + kv[i][h] for h in HEADS)
        o = [intra[i][h] + inter[h] for h in HEADS]
        normed.append(_merge_heads(tuple(o[h] * lax.rsqrt(jnp.mean(o[h] * o[h], axis=-1, keepdims=True) + LN_EPS)
                                         for h in HEADS)))
    return _merge_chunks(tuple(normed)) * nw * jax.nn.silu(hgate), sts


def _seg(ref, seg):
    return ref[:, seg * D_GROUP:(seg + 1) * D_GROUP]


def _hgrn2_fwd(proj, logits, norm_w):
    s = proj.shape[0]
    rows = CHUNKS_PER_STEP * CHUNK
    nc = s // rows

    def body(p_ref, lg_ref, nw_ref, y_ref, st_out_ref, st_scr):
        @pl.when(pl.program_id(0) == 0)
        def _():
            st_scr[...] = jnp.zeros_like(st_scr)

        sts = tuple(st_scr[h] for h in HEADS)
        y, sts_new = _hg_chunk(_seg(p_ref, 0), _seg(p_ref, 1), _seg(p_ref, 2), _seg(p_ref, 3),
                               lg_ref[0:1, :], lg_ref[1:2, :], nw_ref[...], sts)
        y_ref[...] = y.astype(y_ref.dtype)
        for h in HEADS:
            st_out_ref[h] = sts[h]
            st_scr[h] = sts_new[h]

    return pl.pallas_call(
        body, name="hgrn2_fwd", grid=(nc,),
        in_specs=[pl.BlockSpec((rows, 4 * D_GROUP), lambda c: (c, 0)),
                  pl.BlockSpec((2, D_GROUP), lambda c: (0, 0)),
                  pl.BlockSpec((1, D_GROUP), lambda c: (0, 0))],
        out_specs=[pl.BlockSpec((rows, D_GROUP), lambda c: (c, 0)),
                   pl.BlockSpec((None, N_HEADS, D_HEAD, D_HEAD), lambda c: (c, 0, 0, 0))],
        out_shape=[jax.ShapeDtypeStruct((s, 2 * D_GROUP), BF16),
                   jax.ShapeDtypeStruct((nc, N_HEADS, D_HEAD, D_HEAD), F32)],
        scratch_shapes=[pltpu.VMEM((N_HEADS, D_HEAD, D_HEAD), F32)],
        compiler_params=_params(("arbitrary",)),
    )(proj, logits, norm_w)


def _hgrn2_bwd(proj, logits, norm_w, states, d_y):
    s = proj.shape[0]
    rows = CHUNKS_PER_STEP * CHUNK
    nc = s // rows

    def body(p_ref, lg_ref, nw_ref, st_ref, dy_ref, dp_ref, dl_ref, dnw_ref, dsum_ref, dst_scr):
        @pl.when(pl.program_id(0) == 0)
        def _():
            dst_scr[...] = jnp.zeros_like(dst_scr)
            dl_ref[...] = jnp.zeros_like(dl_ref)
            dnw_ref[...] = jnp.zeros_like(dnw_ref)
            dsum_ref[...] = jnp.zeros_like(dsum_ref)

        _, vjp = jax.vjp(_hg_chunk, _seg(p_ref, 0), _seg(p_ref, 1), _seg(p_ref, 2), _seg(p_ref, 3),
                         lg_ref[0:1, :], lg_ref[1:2, :], nw_ref[...], tuple(st_ref[h] for h in HEADS))
        d_hq, d_hf, d_hi, d_hg, d_l0, d_l1, d_nw, d_sts = vjp((dy_ref[...], tuple(dst_scr[h] for h in HEADS)))
        for seg, val in enumerate((d_hq, d_hf, d_hi, d_hg)):
            dp_ref[:, seg * D_GROUP:(seg + 1) * D_GROUP] = val.astype(dp_ref.dtype)
            dsum_ref[:, seg * D_GROUP:(seg + 1) * D_GROUP] += jnp.sum(val, axis=0, keepdims=True)
        dl_ref[0:1, :] += d_l0
        dl_ref[1:2, :] += d_l1
        dnw_ref[...] += d_nw
        for h in HEADS:
            dst_scr[h] = d_sts[h]

    rev = lambda c: nc - 1 - c
    return pl.pallas_call(
        body, name="hgrn2_bwd", grid=(nc,),
        in_specs=[pl.BlockSpec((rows, 4 * D_GROUP), lambda c: (rev(c), 0)),
                  pl.BlockSpec((2, D_GROUP), lambda c: (0, 0)),
                  pl.BlockSpec((1, D_GROUP), lambda c: (0, 0)),
                  pl.BlockSpec((None, N_HEADS, D_HEAD, D_HEAD), lambda c: (rev(c), 0, 0, 0)),
                  pl.BlockSpec((rows, D_GROUP), lambda c: (rev(c), 0))],
        out_specs=[pl.BlockSpec((rows, 4 * D_GROUP), lambda c: (rev(c), 0)),
                   pl.BlockSpec((2, D_GROUP), lambda c: (0, 0)),
                   pl.BlockSpec((1, D_GROUP), lambda c: (0, 0)),
                   pl.BlockSpec((1, 4 * D_GROUP), lambda c: (0, 0))],
        out_shape=[jax.ShapeDtypeStruct((s, D_IN_MAIN), BF16), jax.ShapeDtypeStruct((2, D_GROUP), F32),
                   jax.ShapeDtypeStruct((1, D_GROUP), F32), jax.ShapeDtypeStruct((1, 4 * D_GROUP), F32)],
        scratch_shapes=[pltpu.VMEM((N_HEADS, D_HEAD, D_HEAD), F32)],
        compiler_params=_params(("arbitrary",)),
    )(proj, logits, norm_w, states, d_y)


def _gate_column(gates, lane, idx):
    return jnp.sum(jnp.where(lane == idx, gates, 0.0), axis=1, keepdims=True)


def _head_layer_norm(h):
    mu = jnp.mean(h, axis=-1, keepdims=True)
    var = jnp.mean(jnp.square(h - mu), axis=-1, keepdims=True)
    return (h - mu) * lax.rsqrt(var + LN_EPS)


def _ml_chunk(qc, kc, v, mo, gates, nw, cts, ns, ms):
    n = qc.shape[0] // CHUNK
    row, col = _chunk_masks(CHUNK)
    mask = col <= row
    eye = col == row
    to_row = lambda t: jnp.sum(jnp.where(eye, t, 0.0), axis=0, keepdims=True)
    q = _blocks(qc * (D_HEAD ** -0.5))
    k = _blocks(kc)
    vs = _blocks(v)
    gate_rows = _split_chunks(gates)
    lane = lax.broadcasted_iota(jnp.int32, gate_rows[0].shape, 1)
    each = [(i, h) for i in range(n) for h in HEADS]
    on_each = lambda f: {ih: f(*ih) for ih in each}
    ig = on_each(lambda i, h: _gate_column(gate_rows[i], lane, h))
    lf = on_each(lambda i, h: jax.nn.log_sigmoid(_gate_column(gate_rows[i], lane, N_HEADS + h)))
    lf_row = on_each(lambda i, h: to_row(lf[i, h]))
    ig_row = on_each(lambda i, h: to_row(ig[i, h]))
    b_col = on_each(lambda i, h: jnp.sum(jnp.where(mask, lf_row[i, h], 0.0), axis=1, keepdims=True))
    b_row = on_each(lambda i, h: jnp.sum(jnp.where(row <= col, lf[i, h], 0.0), axis=0, keepdims=True))
    g = on_each(lambda i, h: jnp.sum(lf[i, h], axis=0, keepdims=True))
    d = on_each(lambda i, h: jnp.where(mask, b_col[i, h] - b_row[i, h] + ig_row[i, h], -jnp.inf))
    a = on_each(lambda i, h: g[i, h] - b_col[i, h] + ig[i, h])
    m_at = {(0, h): ms[h] for h in HEADS}
    for i, h in each:
        m_at[i + 1, h] = lax.stop_gradient(jnp.maximum(g[i, h] + m_at[i, h], jnp.max(a[i, h], axis=0, keepdims=True)))
    inter = on_each(lambda i, h: b_col[i, h] + m_at[i, h])
    m_t = on_each(lambda i, h: lax.stop_gradient(jnp.maximum(inter[i, h], jnp.max(d[i, h], axis=1, keepdims=True))))
    qk = on_each(lambda i, h: _nt(q[i][h], k[i][h]))
    sc = on_each(lambda i, h: qk[i, h] * jnp.exp(d[i, h] - m_t[i, h]))
    w_inter = on_each(lambda i, h: jnp.exp(inter[i, h] - m_t[i, h]))
    sv = on_each(lambda i, h: _nn(sc[i, h], vs[i][h]))
    decay = on_each(lambda i, h: jnp.exp(g[i, h] + m_at[i, h] - m_at[i + 1, h]))
    wk = on_each(lambda i, h: k[i][h] * jnp.exp(a[i, h] - m_at[i + 1, h]))
    kv = on_each(lambda i, h: _tn(vs[i][h], wk[i, h]))
    normed = []
    for i in range(n):
        qc_state = [_nt(q[i][h], cts[h]) for h in HEADS]
        num = [sv[i, h] + w_inter[i, h] * qc_state[h] for h in HEADS]
        den = [jnp.sum(sc[i, h], axis=1, keepdims=True)
               + w_inter[i, h] * jnp.sum(q[i][h] * ns[h], axis=1, keepdims=True) for h in HEADS]
        hh = [num[h] / jnp.maximum(jnp.abs(den[h]), jnp.exp(-m_t[i, h])) for h in HEADS]
        cts = tuple(decay[i, h] * cts[h] + kv[i, h] for h in HEADS)
        ns = tuple(decay[i, h] * ns[h] + jnp.sum(wk[i, h], axis=0, keepdims=True) for h in HEADS)
        normed.append(_merge_heads(tuple(_head_layer_norm(hh[h]) for h in HEADS)))
    y = jax.nn.sigmoid(mo) * (_merge_chunks(tuple(normed)) * nw)
    return y, cts, ns, tuple(m_at[n, h] for h in HEADS)


def _mlstm_fwd(qk, proj, gates, norm_w, y):
    s = proj.shape[0]
    rows = ML_CHUNKS_PER_STEP * CHUNK
    nc = s // rows

    def body(qk_ref, vo_ref, g_ref, nw_ref, _, y_ref, ct_out, n_out, m_out, ct_scr, n_scr, m_scr):
        @pl.when(pl.program_id(0) == 0)
        def _():
            ct_scr[...] = jnp.zeros_like(ct_scr)
            n_scr[...] = jnp.zeros_like(n_scr)
            m_scr[...] = jnp.full(m_scr.shape, NEG_BIG, F32)

        cts = tuple(ct_scr[h] for h in HEADS)
        ns = tuple(n_scr[h] for h in HEADS)
        ms = tuple(m_scr[h] for h in HEADS)
        y, cts_new, ns_new, ms_new = _ml_chunk(_seg(qk_ref, 0), _seg(qk_ref, 1), _seg(vo_ref, 0), _seg(vo_ref, 1),
                                               g_ref[...], nw_ref[...], cts, ns, ms)
        y_ref[...] = y.astype(y_ref.dtype)
        for h in HEADS:
            ct_out[h], n_out[h], m_out[h] = cts[h], ns[h], ms[h]
            ct_scr[h], n_scr[h], m_scr[h] = cts_new[h], ns_new[h], ms_new[h]

    st = lambda r, w: pl.BlockSpec((None, N_HEADS, r, w), lambda c: (c, 0, 0, 0))
    return pl.pallas_call(
        body, name="mlstm_fwd", grid=(nc,),
        in_specs=[pl.BlockSpec((rows, 2 * D_GROUP), lambda c: (c, 0)),
                  pl.BlockSpec((rows, 2 * D_GROUP), lambda c: (c, 3)),
                  pl.BlockSpec((rows, LANES), lambda c: (c, 0)),
                  pl.BlockSpec((1, D_GROUP), lambda c: (0, 0)),
                  pl.BlockSpec(memory_space=pl.ANY)],
        out_specs=[pl.BlockSpec((rows, D_GROUP), lambda c: (c, 1)),
                   st(D_HEAD, D_HEAD), st(1, D_HEAD), st(1, 1)],
        out_shape=[jax.ShapeDtypeStruct(y.shape, y.dtype),
                   jax.ShapeDtypeStruct((nc, N_HEADS, D_HEAD, D_HEAD), F32),
                   jax.ShapeDtypeStruct((nc, N_HEADS, 1, D_HEAD), F32),
                   jax.ShapeDtypeStruct((nc, N_HEADS, 1, 1), F32)],
        input_output_aliases={4: 0},
        scratch_shapes=[pltpu.VMEM((N_HEADS, D_HEAD, D_HEAD), F32), pltpu.VMEM((N_HEADS, 1, D_HEAD), F32),
                        pltpu.VMEM((N_HEADS, 1, 1), F32)],
        compiler_params=_params(("arbitrary",)),
    )(qk, proj, gates, norm_w, y)


def _mlstm_bwd(qk, proj, gates, norm_w, ct_s, n_s, m_s, d_y, d_proj):
    s = proj.shape[0]
    rows = ML_CHUNKS_PER_STEP * CHUNK
    nc = s // rows

    def body(qk_ref, vo_ref, g_ref, nw_ref, ct_ref, n_ref, m_ref, dy_ref, _,
             dp_ref, dqk_ref, dg_ref, dnw_ref, dsum_ref, dct_scr, dn_scr):
        @pl.when(pl.program_id(0) == 0)
        def _():
            dct_scr[...] = jnp.zeros_like(dct_scr)
            dn_scr[...] = jnp.zeros_like(dn_scr)
            dnw_ref[...] = jnp.zeros_like(dnw_ref)
            dsum_ref[...] = jnp.zeros_like(dsum_ref)

        ms = tuple(m_ref[h] for h in HEADS)
        step = lambda *a: _ml_chunk(*a, ms)[:3]
        _, vjp = jax.vjp(step, _seg(qk_ref, 0), _seg(qk_ref, 1), _seg(vo_ref, 0), _seg(vo_ref, 1), g_ref[...],
                         nw_ref[...], tuple(ct_ref[h] for h in HEADS), tuple(n_ref[h] for h in HEADS))
        d_q, d_k, d_v, d_o, d_gates, d_nw, d_cts, d_ns = vjp(
            (dy_ref[...], tuple(dct_scr[h] for h in HEADS), tuple(dn_scr[h] for h in HEADS)))
        dqk_ref[:, 0:D_GROUP] = d_q
        dqk_ref[:, D_GROUP:2 * D_GROUP] = d_k
        for seg, val in enumerate((d_v, d_o)):
            dp_ref[:, seg * D_GROUP:(seg + 1) * D_GROUP] = val.astype(dp_ref.dtype)
            dsum_ref[:, seg * D_GROUP:(seg + 1) * D_GROUP] += jnp.sum(val, axis=0, keepdims=True)
        dg_ref[...] = d_gates
        dnw_ref[...] += d_nw
        for h in HEADS:
            dct_scr[h] = d_cts[h]
            dn_scr[h] = d_ns[h]

    rev = lambda c: nc - 1 - c
    st = lambda r, w: pl.BlockSpec((None, N_HEADS, r, w), lambda c: (rev(c), 0, 0, 0))
    return pl.pallas_call(
        body, name="mlstm_bwd", grid=(nc,),
        in_specs=[pl.BlockSpec((rows, 2 * D_GROUP), lambda c: (rev(c), 0)),
                  pl.BlockSpec((rows, 2 * D_GROUP), lambda c: (rev(c), 3)),
                  pl.BlockSpec((rows, LANES), lambda c: (rev(c), 0)),
                  pl.BlockSpec((1, D_GROUP), lambda c: (0, 0)),
                  st(D_HEAD, D_HEAD), st(1, D_HEAD), st(1, 1),
                  pl.BlockSpec((rows, D_GROUP), lambda c: (rev(c), 1)),
                  pl.BlockSpec(memory_space=pl.ANY)],
        out_specs=[pl.BlockSpec((rows, 2 * D_GROUP), lambda c: (rev(c), 3)),
                   pl.BlockSpec((rows, 2 * D_GROUP), lambda c: (rev(c), 0)),
                   pl.BlockSpec((rows, LANES), lambda c: (rev(c), 0)),
                   pl.BlockSpec((1, D_GROUP), lambda c: (0, 0)),
                   pl.BlockSpec((1, 2 * D_GROUP), lambda c: (0, 0))],
        out_shape=[jax.ShapeDtypeStruct(d_proj.shape, d_proj.dtype), jax.ShapeDtypeStruct((s, 2 * D_GROUP), F32),
                   jax.ShapeDtypeStruct((s, LANES), F32), jax.ShapeDtypeStruct((1, D_GROUP), F32),
                   jax.ShapeDtypeStruct((1, 2 * D_GROUP), F32)],
        input_output_aliases={8: 0},
        scratch_shapes=[pltpu.VMEM((N_HEADS, D_HEAD, D_HEAD), F32), pltpu.VMEM((N_HEADS, 1, D_HEAD), F32)],
        compiler_params=_params(("arbitrary",)),
    )(qk, proj, gates, norm_w, ct_s, n_s, m_s, d_y, d_proj)


LN_TOKENS = 512
ATT_TOKENS = 512


def _proj_res_ln(a, w, xres, g, b, name):
    s, dm = xres.shape
    k = a.shape[1]
    tb = min(LN_TOKENS, s)

    def body(a_ref, w_ref, x_ref, g_ref, b_ref, z_ref, o_ref):
        z = ALPHA * x_ref[...] + _nn_raw(a_ref[...], w_ref[...])
        z_ref[...] = z
        o_ref[...] = _layer_norm(z, g_ref[...], b_ref[...])

    tok = pl.BlockSpec((tb, dm), lambda i: (i, 0))
    vec = pl.BlockSpec((1, dm), lambda i: (0, 0))
    act = jax.ShapeDtypeStruct((s, dm), F32)
    return pl.pallas_call(
        body, name=name, grid=(s // tb,),
        in_specs=[pl.BlockSpec((tb, k), lambda i: (i, 0)), pl.BlockSpec((k, dm), lambda i: (0, 0)), tok, vec, vec],
        out_specs=[tok, tok], out_shape=[act, act], compiler_params=_params(("parallel",)),
    )(a, w, xres, g, b)


def _ln_bwd_proj(d_out, z, g, b, w, name):
    s, dm = z.shape
    k = w.shape[0]
    tb = min(LN_TOKENS, s)

    def body(do_ref, z_ref, g_ref, b_ref, w_ref, dz_ref, da_ref, dg_ref, db_ref):
        @pl.when(pl.program_id(0) == 0)
        def _():
            dg_ref[...] = jnp.zeros_like(dg_ref)
            db_ref[...] = jnp.zeros_like(db_ref)

        _, vjp = jax.vjp(_layer_norm, z_ref[...], g_ref[...], b_ref[...])
        d_z, d_g, d_b = vjp(do_ref[...])
        dz_ref[...] = d_z
        da_ref[...] = _nt_raw(d_z, w_ref[...])
        dg_ref[...] += d_g
        db_ref[...] += d_b

    tok = pl.BlockSpec((tb, dm), lambda i: (i, 0))
    vec = pl.BlockSpec((1, dm), lambda i: (0, 0))
    return pl.pallas_call(
        body, name=name, grid=(s // tb,),
        in_specs=[tok, tok, vec, vec, pl.BlockSpec((k, dm), lambda i: (0, 0))],
        out_specs=[tok, pl.BlockSpec((tb, k), lambda i: (i, 0)), vec, vec],
        out_shape=[jax.ShapeDtypeStruct((s, dm), F32), jax.ShapeDtypeStruct((s, k), F32),
                   jax.ShapeDtypeStruct((1, dm), F32), jax.ShapeDtypeStruct((1, dm), F32)],
        compiler_params=_params(("arbitrary",)),
    )(d_out, z, g, b, w)


def _proj_loss_tail(a, w, xres, g, b, target):
    s, dm = xres.shape
    k = a.shape[1]
    tb = min(ATT_TOKENS, s)

    def loss_fn(z, gg, bb, tgt):
        err = jnp.square(_layer_norm(z, gg, bb) - tgt)
        return 0.5 * jnp.sum(jnp.mean(err, axis=-1, keepdims=True), axis=0, keepdims=True)

    def body(a_ref, w_ref, x_ref, g_ref, b_ref, t_ref, loss_ref, dz_ref, dg_ref, db_ref):
        @pl.when(pl.program_id(0) == 0)
        def _():
            loss_ref[...] = jnp.zeros_like(loss_ref)
            dg_ref[...] = jnp.zeros_like(dg_ref)
            db_ref[...] = jnp.zeros_like(db_ref)

        z = ALPHA * x_ref[...] + _nn_raw(a_ref[...], w_ref[...])
        tgt = t_ref[...]
        loss, vjp = jax.vjp(lambda zz, gg, bb: loss_fn(zz, gg, bb, tgt), z, g_ref[...], b_ref[...])
        d_z, d_g, d_b = vjp(jnp.ones((1, 1), F32))
        loss_ref[...] += loss
        dz_ref[...] = d_z
        dg_ref[...] += d_g
        db_ref[...] += d_b

    tok = pl.BlockSpec((tb, dm), lambda i: (i, 0))
    vec = pl.BlockSpec((1, dm), lambda i: (0, 0))
    one = pl.BlockSpec((1, 1), lambda i: (0, 0))
    return pl.pallas_call(
        body, name="ffn_down_loss_tail", grid=(s // tb,),
        in_specs=[pl.BlockSpec((tb, k), lambda i: (i, 0)), pl.BlockSpec((k, dm), lambda i: (0, 0)), tok, vec, vec, tok],
        out_specs=[one, tok, vec, vec],
        out_shape=[jax.ShapeDtypeStruct((1, 1), F32), jax.ShapeDtypeStruct((s, dm), F32),
                   jax.ShapeDtypeStruct((1, dm), F32), jax.ShapeDtypeStruct((1, dm), F32)],
        compiler_params=_params(("arbitrary",)),
    )(a, w, xres, g, b, target)


def _att_heads(qs, ks, vs):
    sc = [_nt(q, k) * (CA_DH ** -0.5) for q, k in zip(qs, ks)]
    p = [jax.nn.softmax(s, axis=-1) for s in sc]
    return tuple(_nn(pp, v) for pp, v in zip(p, vs))


def _head_slices(ref_or_value, offset):
    return tuple(ref_or_value[:, offset + h * CA_DH:offset + (h + 1) * CA_DH] for h in range(CA_HEADS))


def _cross_attention_fwd(x1, kv, wq, wo, g, b):
    s = x1.shape[0]
    tb = min(ATT_TOKENS, s)

    def body(x_ref, kv_ref, wq_ref, wo_ref, g_ref, b_ref, att_ref, z_ref, o_ref):
        x_blk = x_ref[...]
        q = _nn_raw(x_blk, wq_ref[...])
        att = jnp.concatenate(_att_heads(_head_slices(q, 0), _head_slices(kv_ref, 0), _head_slices(kv_ref, D_MODEL)),
                              axis=1)
        att_ref[...] = att.astype(att_ref.dtype)
        z = ALPHA * x_blk + _nn_raw(att, wo_ref[...])
        z_ref[...] = z
        o_ref[...] = _layer_norm(z, g_ref[...], b_ref[...])

    tok = pl.BlockSpec((tb, D_MODEL), lambda i: (i, 0))
    mat = pl.BlockSpec((D_MODEL, D_MODEL), lambda i: (0, 0))
    vec = pl.BlockSpec((1, D_MODEL), lambda i: (0, 0))
    act = jax.ShapeDtypeStruct((s, D_MODEL), F32)
    return pl.pallas_call(
        body, name="cross_attention_fwd", grid=(s // tb,),
        in_specs=[tok, pl.BlockSpec((N_MEM, 2 * D_MODEL), lambda i: (0, 0)), mat, mat, vec, vec],
        out_specs=[tok, tok, tok],
        out_shape=[jax.ShapeDtypeStruct((s, D_MODEL), BF16), act, act],
        compiler_params=_params(("parallel",)),
    )(x1, kv, wq, wo, g, b)


def _cross_attention_bwd(d_x2, x1, z2, kv, wq, wo, g, b):
    s = x1.shape[0]
    tb = min(ATT_TOKENS, s)

    def body(dx2_ref, x_ref, z_ref, kv_ref, wq_ref, wo_ref, g_ref, b_ref,
             dx1_ref, dq_ref, dz_ref, dkv_ref, dg_ref, db_ref):
        @pl.when(pl.program_id(0) == 0)
        def _():
            dkv_ref[...] = jnp.zeros_like(dkv_ref)
            dg_ref[...] = jnp.zeros_like(dg_ref)
            db_ref[...] = jnp.zeros_like(db_ref)

        _, ln_vjp = jax.vjp(_layer_norm, z_ref[...], g_ref[...], b_ref[...])
        d_z, d_g, d_b = ln_vjp(dx2_ref[...])
        dg_ref[...] += d_g
        db_ref[...] += d_b
        dz_ref[...] = d_z.astype(dz_ref.dtype)
        d_att = _nt_raw(d_z, wo_ref[...])
        q = _nn_raw(x_ref[...], wq_ref[...])
        _, vjp = jax.vjp(_att_heads, _head_slices(q, 0), _head_slices(kv_ref, 0), _head_slices(kv_ref, D_MODEL))
        d_qs, d_ks, d_vs = vjp(_head_slices(d_att, 0))
        for h in range(CA_HEADS):
            lo = h * CA_DH
            dkv_ref[:, lo:lo + CA_DH] += d_ks[h]
            dkv_ref[:, D_MODEL + lo:D_MODEL + lo + CA_DH] += d_vs[h]
        d_q = jnp.concatenate(d_qs, axis=1)
        dq_ref[...] = d_q.astype(dq_ref.dtype)
        dx1_ref[...] = ALPHA * d_z + _nt_raw(d_q, wq_ref[...])

    tok = pl.BlockSpec((tb, D_MODEL), lambda i: (i, 0))
    mem = pl.BlockSpec((N_MEM, 2 * D_MODEL), lambda i: (0, 0))
    mat = pl.BlockSpec((D_MODEL, D_MODEL), lambda i: (0, 0))
    vec = pl.BlockSpec((1, D_MODEL), lambda i: (0, 0))
    low = jax.ShapeDtypeStruct((s, D_MODEL), BF16)
    return pl.pallas_call(
        body, name="cross_attention_bwd", grid=(s // tb,),
        in_specs=[tok, tok, tok, mem, mat, mat, vec, vec], out_specs=[tok, tok, tok, mem, vec, vec],
        out_shape=[jax.ShapeDtypeStruct((s, D_MODEL), F32), low, low,
                   jax.ShapeDtypeStruct((N_MEM, 2 * D_MODEL), F32),
                   jax.ShapeDtypeStruct((1, D_MODEL), F32), jax.ShapeDtypeStruct((1, D_MODEL), F32)],
        compiler_params=_params(("arbitrary",)),
    )(d_x2, x1, z2, kv, wq, wo, g, b)


def _local_step(x, mem, target, w, mid_weights=None, ffn_weights=None, on_ffn_grads=None, on_mid_grads=None,
                on_small_grads=None, on_last_grads=None):
    w = dict(w)
    s = x.shape[0]
    tm = min(512, s)
    tt = min(512, s)
    proj = _matmul_nn(x, w["w_in_main"], w["b_in_main"], min(2048, s), 512, "proj")
    gates = _matmul_nn(x, w["w_in_gate"], w["b_in_gate"], tm, LANES, "proj_gates")
    qk = _ml_conv_fwd(proj, w["ml_conv_w"], w["ml_conv_b"])
    y, hg_states = _hgrn2_fwd(proj, w["hg_lb_logits"], w["hg_norm_w"])
    y, ct_s, n_s, m_s = _mlstm_fwd(qk, proj, gates, w["ml_norm_w"], y)
    if mid_weights is not None:
        w.update(mid_weights(y))
    z1, x1 = _proj_res_ln(y, w["w_out"], x, w["ln1_g"], w["ln1_b"], "out_proj_ln1")
    kv = _matmul_nn(mem, w["ca_wkv"], None, N_MEM, CA_DH, "kv")
    att, z2, x2 = _cross_attention_fwd(x1, kv, w["ca_wq"], w["ca_wo"], w["ln2_g"], w["ln2_b"])
    if ffn_weights is not None:
        w.update(ffn_weights(x2))
    u = _matmul_nn(x2, w["ffn_w_up"], None, min(2048, s), UP_SHARD_P, "ffn_up", BF16)
    hid = _ffn_conv_fwd(u, w["ffn_conv_w"], w["ffn_conv_b"])
    loss, d_z3, d_ln3_g, d_ln3_b = _proj_loss_tail(hid, w["ffn_w_down"], x2, w["ln3_g"], w["ln3_b"], target)
    grads = {"ln3_g": d_ln3_g, "ln3_b": d_ln3_b}
    grads["ffn_w_down"] = _matmul_tn(hid, d_z3, 1536, D_MODEL, tt, "d_w_down")
    d_hid = _matmul_nt([(d_z3, w["ffn_w_down"])], None, 1.0, tm, D_FF_P, "d_hid", BF16)
    d_ug, d_uv, d_cwg, d_cwv, d_cbg, d_cbv = _ffn_conv_bwd(u, w["ffn_conv_w"], w["ffn_conv_b"], d_hid)
    grads["ffn_conv_w"] = jnp.concatenate([d_cwg, d_cwv], axis=-1)
    grads["ffn_conv_b"] = jnp.concatenate([d_cbg, d_cbv], axis=-1)
    half = N_DEV // 2
    d_w_up = _matmul_tn(x2, d_ug, D_MODEL, UP_SHARD_P, tt, "d_w_up_gate", shards=N_DEV, group=half)
    grads["ffn_w_up"] = _matmul_tn(x2, d_uv, D_MODEL, UP_SHARD_P, tt, "d_w_up_val", shards=N_DEV,
                                   shard0=half, group=half, into=d_w_up)
    d_x2 = _matmul_nt([(d_ug, w["ffn_w_up"], 0), (d_uv, w["ffn_w_up"], N_DEV // 2)], d_z3, ALPHA,
                      min(256, s), D_MODEL, "d_x2")
    if on_ffn_grads is not None:
        d_x2 = on_ffn_grads(grads, d_x2)
    d_x1, d_q, d_z2, d_kv, grads["ln2_g"], grads["ln2_b"] = _cross_attention_bwd(
        d_x2, x1, z2, kv, w["ca_wq"], w["ca_wo"], w["ln2_g"], w["ln2_b"])
    grads["ca_wo"] = _matmul_tn(att, d_z2, D_MODEL, D_MODEL, tt, "d_ca_wo")
    grads["ca_wq"] = _matmul_tn(x1, d_q, D_MODEL, D_MODEL, tt, "d_ca_wq")
    grads["ca_wkv"] = _matmul_tn(mem, d_kv, D_MODEL, CA_DH, N_MEM, "d_ca_wkv", shards=N_DEV, group=N_DEV)
    d_z1, d_y, grads["ln1_g"], grads["ln1_b"] = _ln_bwd_proj(d_x1, z1, w["ln1_g"], w["ln1_b"], w["w_out"],
                                                             "ln1_bwd_out_proj")
    grads["w_out"] = _matmul_tn(y, d_z1, D_MODEL, D_MODEL, tt, "d_w_out")
    if on_mid_grads is not None:
        d_y = on_mid_grads(grads, d_y)
    d_proj, grads["hg_lb_logits"], grads["hg_norm_w"], db_hg = _hgrn2_bwd(
        proj, w["hg_lb_logits"], w["hg_norm_w"], hg_states, d_y)
    d_proj, d_qk, d_gates, grads["ml_norm_w"], db_vo = _mlstm_bwd(
        qk, proj, gates, w["ml_norm_w"], ct_s, n_s, m_s, d_y, d_proj)
    d_proj, grads["ml_conv_w"], grads["ml_conv_b"], db_qk = _ml_conv_bwd(
        proj, w["ml_conv_w"], w["ml_conv_b"], d_qk, d_proj)
    grads["b_in_main"] = jnp.concatenate([db_hg, db_qk, db_vo], axis=-1)
    grads["w_in_gate"], grads["b_in_gate"] = _matmul_tn(x, d_gates, D_MODEL, LANES, tt, "d_w_in_gates", colsum=True)
    if on_small_grads is not None:
        d_proj = on_small_grads(grads, loss, d_proj)
    grads["w_in_main"] = _matmul_tn(x, d_proj, D_MODEL, min(2048, D_IN_MAIN), tt, "d_w_in")
    if on_last_grads is not None:
        d_z1 = on_last_grads(grads, d_z1)
    grad_x = _matmul_nt([(d_proj, w["w_in_main"]), (d_gates, w["w_in_gate"])], d_z1, ALPHA, tm, D_MODEL, "d_x")
    return loss, grad_x, grads


HBM_SPEC = pl.BlockSpec(memory_space=pltpu.HBM)


def _coords():
    return lax.axis_index("x"), lax.axis_index("y"), lax.axis_index("c")


def _other_chips(x, y):
    return [(1 - x, y), (x, 1 - y), (1 - x, 1 - y)]


def _all_gather_two_level(shards, name):
    na = len(shards)

    def body(*refs):
        x_refs, out_refs = refs[:na], refs[na:2 * na]
        send_sems, recv_sems, local_sems = refs[2 * na:]
        x, y, c = _coords()
        me, sibling = (x, y, c), (x, y, 1 - c)
        chips = _other_chips(x, y)

        def copy(a, k, block, to, own=False):
            slot = out_refs[a].at[4 * block[0] + 2 * block[1] + block[2]]
            return pltpu.make_async_remote_copy(
                src_ref=x_refs[a] if own else slot, dst_ref=slot,
                send_sem=send_sems.at[7 * a + k], recv_sem=recv_sems.at[7 * a + k],
                device_id=to, device_id_type=MESH)

        mine = [pltpu.make_async_copy(x_refs[a], out_refs[a].at[4 * x + 2 * y + c], local_sems.at[a])
                for a in range(na)]
        for cp in mine:
            cp.start()
        first = []
        for a in range(na):
            first.append(copy(a, 0, me, sibling, own=True))
            first += [copy(a, 1 + j, me, (*chip, c), own=True) for j, chip in enumerate(chips)]
        for cp in first:
            cp.start()
        passed = []
        for j, chip in enumerate(chips):
            for a in range(na):
                copy(a, 1 + j, (*chip, c), me).wait_recv()
                fwd = copy(a, 4 + j, (*chip, c), sibling)
                fwd.start()
                passed.append(fwd)
        for a in range(na):
            copy(a, 0, sibling, me).wait_recv()
            for j, chip in enumerate(chips):
                copy(a, 4 + j, (*chip, 1 - c), me).wait_recv()
        for cp in first + passed:
            cp.wait_send()
        for cp in mine:
            cp.wait()

    return pl.pallas_call(
        body, name=name,
        out_shape=[jax.ShapeDtypeStruct((N_DEV,) + t.shape, t.dtype) for t in shards],
        in_specs=[HBM_SPEC] * na, out_specs=[HBM_SPEC] * na,
        scratch_shapes=[pltpu.SemaphoreType.DMA((7 * na,)), pltpu.SemaphoreType.DMA((7 * na,)),
                        pltpu.SemaphoreType.DMA((na,))],
    )(*shards)


SEM_SPEC = pl.BlockSpec(memory_space=pltpu.SEMAPHORE)
ANY_SPEC = pl.BlockSpec(memory_space=pl.ANY)
SIDE_EFFECT = pltpu.SideEffectType.DATAFLOW_SIDE_EFFECTING


def _peer(x, y, c, d):
    flip = lambda v, bit: 1 - v if bit else v
    p = (flip(x, d & 4), flip(y, d & 2), flip(c, d & 1))
    return p, 4 * p[0] + 2 * p[1] + p[2]


def _direct_copies(gather, src_refs, land_refs, send_sems, recv_sems):
    x, y, c = _coords()
    me = 4 * x + 2 * y + c
    copies = []
    for a in range(len(src_refs)):
        for d in range(1, N_DEV):
            peer, peer_slot = _peer(x, y, c, d)
            copies.append(pltpu.make_async_remote_copy(
                src_ref=src_refs[a] if gather else src_refs[a].at[peer_slot],
                dst_ref=land_refs[a].at[me] if gather else land_refs[a].at[d - 1],
                send_sem=send_sems.at[7 * a + d - 1], recv_sem=recv_sems.at[7 * a + d - 1],
                device_id=peer, device_id_type=MESH))
    return copies


def _hbm(t):
    return pltpu.HBM(t.shape, t.dtype)


def _direct_start(gather, arrays, through, name):
    na = len(arrays)
    lands = [lax.empty((N_DEV,) + t.shape if gather else (N_DEV - 1,) + t.shape[1:], t.dtype) for t in arrays]
    n_io = 2 * na + 1

    def body(*refs):
        for cp in _direct_copies(gather, refs[:na], refs[na:2 * na], refs[n_io], refs[n_io + 1]):
            cp.start()

    ins = [pltpu.with_memory_space_constraint(t, pltpu.HBM) for t in (*arrays, *lands, through)]
    sems = pltpu.SemaphoreType.DMA((7 * na,))
    res = pl.pallas_call(
        body, name=name, out_shape=(sems, sems, *[_hbm(t) for t in ins]),
        in_specs=[HBM_SPEC] * n_io, out_specs=(SEM_SPEC, SEM_SPEC, *[HBM_SPEC] * n_io),
        input_output_aliases={i: 2 + i for i in range(n_io)},
        compiler_params=pltpu.CompilerParams(has_side_effects=SIDE_EFFECT),
    )(*ins)
    return (res[0], res[1], list(res[2:2 + na]), list(res[2 + na:2 + 2 * na])), res[2 + 2 * na]


def _direct_wait(gather, started, after, name):
    send_sems, recv_sems, arrays, lands = started
    na = len(arrays)

    def body(*refs):
        for cp in _direct_copies(gather, refs[:na], refs[na:2 * na], refs[2 * na], refs[2 * na + 1]):
            cp.wait_send()
            cp.wait_recv()

    res = pl.pallas_call(
        body, name=name, out_shape=tuple(_hbm(t) for t in (*arrays, *lands)),
        in_specs=[HBM_SPEC] * (2 * na) + [SEM_SPEC, SEM_SPEC, ANY_SPEC], out_specs=tuple([HBM_SPEC] * (2 * na)),
        input_output_aliases={i: i for i in range(2 * na)},
        compiler_params=pltpu.CompilerParams(has_side_effects=SIDE_EFFECT),
    )(*arrays, *lands, send_sems, recv_sems, after)
    return list(res[:na]), list(res[na:])


def _row_tile(rows):
    for t in (256, 176, 128):
        if rows % t == 0 and rows > t:
            return t
    return rows


def _adamw_math(g, w, m, v):
    m_new = ADAM_B1 * m + (1.0 - ADAM_B1) * g
    v_new = ADAM_B2 * v + (1.0 - ADAM_B2) * jnp.square(g)
    m_hat = m_new / (1.0 - ADAM_B1 ** ADAM_STEP)
    v_hat = v_new / (1.0 - ADAM_B2 ** ADAM_STEP)
    delta = -ADAM_LR * (m_hat / (jnp.sqrt(v_hat) + ADAM_EPS) + ADAM_WD * w)
    return delta, m_new, v_new


def _adamw_sharded(chip, sums, got, w, m, v, name):
    r, c = w.shape
    tr = _row_tile(r)
    n_got = got.shape[0]

    def body(chip_ref, s_ref, g_ref, w_ref, m_ref, v_ref, go_ref, d_ref, nm_ref, nv_ref):
        g = s_ref[...].astype(F32)
        for i in range(n_got):
            g = g + g_ref[i].astype(F32)
        delta, m_new, v_new = _adamw_math(g, w_ref[...], m_ref[...], v_ref[...])
        go_ref[...] = g
        d_ref[...] = delta
        nm_ref[...] = m_new
        nv_ref[...] = v_new

    blk = pl.BlockSpec((tr, c), lambda i, chip_ref: (i, 0))
    out = jax.ShapeDtypeStruct((r, c), F32)
    return pl.pallas_call(
        body, name=name,
        grid_spec=pltpu.PrefetchScalarGridSpec(
            num_scalar_prefetch=1, grid=(r // tr,),
            in_specs=[pl.BlockSpec((None, tr, c), lambda i, chip_ref: (chip_ref[0], i, 0)),
                      pl.BlockSpec((n_got, tr, c), lambda i, chip_ref: (0, i, 0)), blk, blk, blk],
            out_specs=[blk, blk, blk, blk]),
        out_shape=[out, out, out, out],
        compiler_params=_params(("parallel",)),
    )(chip, sums, got, w, m, v)


def _adamw_replicated(parts, w, m, v):
    p, r, c = parts.shape

    def body(p_ref, w_ref, m_ref, v_ref, g_ref, d_ref, nm_ref, nv_ref):
        g = p_ref[0]
        for i in range(1, p):
            g = g + p_ref[i]
        delta, m_new, v_new = _adamw_math(g, w_ref[...], m_ref[...], v_ref[...])
        g_ref[...] = g
        d_ref[...] = delta
        nm_ref[...] = m_new
        nv_ref[...] = v_new

    blk = pl.BlockSpec((r, c), lambda i: (0, 0))
    out = jax.ShapeDtypeStruct((r, c), F32)
    return pl.pallas_call(
        body, name="adamw_replicated", grid=(1,),
        in_specs=[pl.BlockSpec((p, r, c), lambda i: (0, 0, 0)), blk, blk, blk],
        out_specs=[blk, blk, blk, blk], out_shape=[out, out, out, out],
        compiler_params=_params(("arbitrary",)),
    )(parts, w, m, v)


SHARDED_NAMES = ("w_in", "ml_conv_w", "w_out", "ca_wq", "ca_wkv", "ca_wo", "ffn_w_up", "ffn_conv_w", "ffn_w_down")
SMALL_NAMES = ("b_in", "hg_lb_logits", "hg_norm_w", "ml_conv_b", "ml_norm_w", "ln1_g", "ln1_b",
               "ln2_g", "ln2_b", "ffn_conv_b", "ln3_g", "ln3_b")
WEIGHT_NAMES = ("w_in", "b_in", "hg_lb_logits", "hg_norm_w", "ml_conv_w", "ml_conv_b", "ml_norm_w", "w_out",
                "ln1_g", "ln1_b", "ca_wq", "ca_wkv", "ca_wo", "ln2_g", "ln2_b", "ffn_w_up", "ffn_conv_w",
                "ffn_conv_b", "ffn_w_down", "ln3_g", "ln3_b")
PAD_TO = {"ffn_w_up": UP_SHARD_P, "ffn_conv_w": UP_SHARD_P}
SMALL_ROWS = 24
SMALL_W = D_MODEL


def _shard_2d(name, block):
    t = block[0]
    if name in PAD_TO:
        t = jnp.pad(t, ((0, 0), (0, PAD_TO[name] - t.shape[1])))
    return t


def _shard_like(name, t, like):
    return t[:, :like.shape[2]][None]


def _pad_cols(t, width):
    return jnp.pad(t, ((0, 0), (0, width - t.shape[1])))


FIRST_NAMES = ("w_in", "ml_conv_w")
FFN_NAMES = ("ffn_w_up", "ffn_w_down", "ffn_conv_w")
MID_NAMES = ("ca_wo", "ca_wq", "ca_wkv", "w_out")


def _first_weights(g, small):
    w = dict(small)
    w_in = jnp.concatenate([g["w_in"][j] for j in range(N_DEV)], axis=1)
    w["w_in_main"] = w_in[:, :D_IN_MAIN]
    w["w_in_gate"] = _pad_cols(w_in[:, D_IN_MAIN:], LANES)
    w["b_in_main"] = small["b_in"][:, :D_IN_MAIN]
    w["b_in_gate"] = _pad_cols(small["b_in"][:, D_IN_MAIN:], LANES)
    w["ml_conv_w"] = jnp.transpose(g["ml_conv_w"], (1, 0, 2)).reshape(ML_CONV, 2 * D_GROUP)
    return w


def _mid_weights(g):
    w = {n: g[n].reshape(D_MODEL, D_MODEL) for n in ("w_out", "ca_wq", "ca_wo")}
    w["ca_wkv"] = g["ca_wkv"]
    return w


def _ffn_weights(g, small):
    w = {"ffn_w_up": g["ffn_w_up"]}
    down = g["ffn_w_down"].reshape(N_DEV // 2, UP_SHARD, D_MODEL)
    w["ffn_w_down"] = jnp.pad(down, ((0, 0), (0, UP_SHARD_P - UP_SHARD), (0, 0))).reshape(D_FF_P, D_MODEL)
    w["ffn_conv_w"] = jnp.transpose(g["ffn_conv_w"], (1, 0, 2)).reshape(FFN_CONV, D_UP_P)
    w["ffn_conv_b"] = _pad_cols(small["ffn_conv_b"].reshape(N_DEV, UP_SHARD), UP_SHARD_P).reshape(1, D_UP_P)
    return w


def _whole_weights(g, small):
    return {**_first_weights(g, small), **_mid_weights(g), **_ffn_weights(g, small)}


def _owner_stack(n, grads):
    if n == "w_in":
        w_in = jnp.concatenate([grads["w_in_main"], grads["w_in_gate"][:, :D_IN - D_IN_MAIN]], axis=1)
        return jnp.stack([w_in[:, j * W_IN_SHARD:(j + 1) * W_IN_SHARD] for j in range(N_DEV)])
    if n in ("w_out", "ca_wq", "ca_wo"):
        return grads[n].reshape(N_DEV, D_MODEL // N_DEV, D_MODEL)
    if n == "ffn_w_down":
        down = grads[n].reshape(N_DEV // 2, UP_SHARD_P, D_MODEL)[:, :UP_SHARD]
        return down.reshape(N_DEV, D_FF // N_DEV, D_MODEL)
    if n == "ml_conv_w":
        return jnp.transpose(grads[n].reshape(ML_CONV, N_DEV, LANES), (1, 0, 2))
    if n == "ffn_conv_w":
        return jnp.transpose(grads[n].reshape(FFN_CONV, N_DEV, UP_SHARD_P), (1, 0, 2))
    return grads[n]


def _owner_stacks(grads):
    return {n: _owner_stack(n, grads) for n in SHARDED_NAMES}


def _small_grads(grads):
    out = {n: grads[n] for n in SMALL_NAMES if n in grads}
    out["b_in"] = jnp.concatenate([grads["b_in_main"], grads["b_in_gate"][:, :D_IN - D_IN_MAIN]], axis=1)
    out["ffn_conv_b"] = grads["ffn_conv_b"].reshape(N_DEV, UP_SHARD_P)[:, :UP_SHARD].reshape(1, D_UP)
    return out


def _pack_small(p, extra=None):
    flat = [p[n].reshape(-1) for n in SMALL_NAMES]
    if extra is not None:
        flat.append(extra.reshape(-1))
    flat = jnp.concatenate(flat)
    return jnp.pad(flat, (0, SMALL_ROWS * SMALL_W - flat.shape[0])).reshape(SMALL_ROWS, SMALL_W)


def _unpack_small(slab, like):
    out = {}
    flat = slab.reshape(-1)
    o = 0
    for n in SMALL_NAMES:
        out[n] = flat[o:o + like[n].size].reshape(like[n].shape)
        o += like[n].size
    return out, flat[o]


def kernel(x, mem, w_in, b_in, hg_lb_logits, hg_norm_w, ml_conv_w, ml_conv_b, ml_norm_w, w_out, ln1_g, ln1_b, ca_wq, ca_wkv, ca_wo, ln2_g, ln2_b, ffn_w_up, ffn_conv_w, ffn_conv_b, ffn_w_down, ln3_g, ln3_b, loss_target, m_w_in, m_b_in, m_hg_lb_logits, m_hg_norm_w, m_ml_conv_w, m_ml_conv_b, m_ml_norm_w, m_w_out, m_ln1_g, m_ln1_b, m_ca_wq, m_ca_wkv, m_ca_wo, m_ln2_g, m_ln2_b, m_ffn_w_up, m_ffn_conv_w, m_ffn_conv_b, m_ffn_w_down, m_ln3_g, m_ln3_b, v_w_in, v_b_in, v_hg_lb_logits, v_hg_norm_w, v_ml_conv_w, v_ml_conv_b, v_ml_norm_w, v_w_out, v_ln1_g, v_ln1_b, v_ca_wq, v_ca_wkv, v_ca_wo, v_ln2_g, v_ln2_b, v_ffn_w_up, v_ffn_conv_w, v_ffn_conv_b, v_ffn_w_down, v_ln3_g, v_ln3_b):
    params = dict(w_in=w_in, b_in=b_in, hg_lb_logits=hg_lb_logits, hg_norm_w=hg_norm_w, ml_conv_w=ml_conv_w,
                  ml_conv_b=ml_conv_b, ml_norm_w=ml_norm_w, w_out=w_out, ln1_g=ln1_g, ln1_b=ln1_b, ca_wq=ca_wq,
                  ca_wkv=ca_wkv, ca_wo=ca_wo, ln2_g=ln2_g, ln2_b=ln2_b, ffn_w_up=ffn_w_up, ffn_conv_w=ffn_conv_w,
                  ffn_conv_b=ffn_conv_b, ffn_w_down=ffn_w_down, ln3_g=ln3_g, ln3_b=ln3_b)
    mom1 = dict(w_in=m_w_in, b_in=m_b_in, hg_lb_logits=m_hg_lb_logits, hg_norm_w=m_hg_norm_w,
                ml_conv_w=m_ml_conv_w, ml_conv_b=m_ml_conv_b, ml_norm_w=m_ml_norm_w, w_out=m_w_out, ln1_g=m_ln1_g,
                ln1_b=m_ln1_b, ca_wq=m_ca_wq, ca_wkv=m_ca_wkv, ca_wo=m_ca_wo, ln2_g=m_ln2_g, ln2_b=m_ln2_b,
                ffn_w_up=m_ffn_w_up, ffn_conv_w=m_ffn_conv_w, ffn_conv_b=m_ffn_conv_b, ffn_w_down=m_ffn_w_down,
                ln3_g=m_ln3_g, ln3_b=m_ln3_b)
    mom2 = dict(w_in=v_w_in, b_in=v_b_in, hg_lb_logits=v_hg_lb_logits, hg_norm_w=v_hg_norm_w,
                ml_conv_w=v_ml_conv_w, ml_conv_b=v_ml_conv_b, ml_norm_w=v_ml_norm_w, w_out=v_w_out, ln1_g=v_ln1_g,
                ln1_b=v_ln1_b, ca_wq=v_ca_wq, ca_wkv=v_ca_wkv, ca_wo=v_ca_wo, ln2_g=v_ln2_g, ln2_b=v_ln2_b,
                ffn_w_up=v_ffn_w_up, ffn_conv_w=v_ffn_conv_w, ffn_conv_b=v_ffn_conv_b, ffn_w_down=v_ffn_w_down,
                ln3_g=v_ln3_g, ln3_b=v_ln3_b)

    x_idx, y_idx, c_idx = _coords()
    as_index = lambda v: jnp.reshape(v, (1,)).astype(jnp.int32)
    me = as_index(4 * x_idx + 2 * y_idx + c_idx)
    small_params = {n: params[n] for n in SMALL_NAMES}

    shards = {n: _shard_2d(n, params[n]) for n in SHARDED_NAMES}
    to_send = lambda names: [shards[n] if "conv" in n else shards[n].astype(BF16) for n in names]
    first = dict(zip(FIRST_NAMES, _all_gather_two_level(to_send(FIRST_NAMES), "weights_gather_first")))
    mid_started, through = _direct_start(True, to_send(MID_NAMES), first["w_in"], "weights_gather_start_mid")
    ffn_started, first["w_in"] = _direct_start(True, to_send(FFN_NAMES), through, "weights_gather_start_ffn")

    def gathered_weights(names, started, after, tag):
        mine, lands = _direct_wait(True, started, after, "weights_gather_wait_" + tag)
        return {n: lax.dynamic_update_index_in_dim(land, own, me[0], 0) for n, own, land in zip(names, mine, lands)}

    started, own_stacks = {}, {}

    def start_group(names, tag):
        def hook(grads, through):
            own_stacks[tag] = [_owner_stack(n, grads).astype(BF16) for n in names]
            started[tag], through = _direct_start(False, own_stacks[tag], through, "grads_start_" + tag)
            return through
        return hook

    def start_small(grads, loss, through):
        started["small"], through = _direct_start(True, [_pack_small(_small_grads(grads), loss)], through,
                                                  "small_gather_start")
        return through

    loss, grad_x, grads = _local_step(
        x[0], mem[0], loss_target[0], _first_weights(first, small_params),
        lambda y: _mid_weights(gathered_weights(MID_NAMES, mid_started, y, "mid")),
        lambda x2: _ffn_weights(gathered_weights(FFN_NAMES, ffn_started, x2, "ffn"), small_params),
        start_group(FFN_NAMES, "ffn"), start_group(MID_NAMES, "mid"), start_small, start_group(FIRST_NAMES, "last"))

    sharded_out = {}
    after = grad_x
    for names, tag in ((FFN_NAMES, "ffn"), (MID_NAMES, "mid"), (FIRST_NAMES, "last")):
        _, lands = _direct_wait(False, started[tag], after, "grads_wait_" + tag)
        for n, st, land in zip(names, own_stacks[tag], lands):
            res = _adamw_sharded(me, st, land, shards[n], _shard_2d(n, mom1[n]), _shard_2d(n, mom2[n]), "adamw_" + n)
            sharded_out[n] = [_shard_like(n, t, params[n]) for t in res]
            after = res[0]
    own_small, small_lands = _direct_wait(True, started["small"], after, "small_gather_wait")
    small_parts = lax.dynamic_update_index_in_dim(small_lands[0], own_small[0], me[0], 0)
    small_res = _adamw_replicated(small_parts, _pack_small(params), _pack_small(mom1), _pack_small(mom2))

    outs = []
    total_loss = None
    for k in range(4):
        small, extra = _unpack_small(small_res[k], params)
        if total_loss is None:
            total_loss = extra
        outs.extend(sharded_out[n][k] if n in sharded_out else small[n] for n in WEIGHT_NAMES)
    return (total_loss, grad_x[None], *outs)
```

```python
import functools
import math

import jax
import jax.numpy as jnp
from jax import lax
from jax.experimental import pallas as pl
from jax.experimental.pallas import tpu as pltpu

F32 = jnp.float32
BF16 = jnp.bfloat16
HIGHEST = lax.Precision.HIGHEST
MESH = pl.DeviceIdType.MESH

N_DEV = 8
D_MODEL = 1024
N_MEM = 256
N_HEADS = 4
D_HEAD = 128
D_GROUP = N_HEADS * D_HEAD
CHUNK = 64
ML_CONV = 4
FFN_CONV = 3
D_FF = 2816
D_UP = 2 * D_FF
CA_HEADS = 4
CA_DH = D_MODEL // CA_HEADS
LANES = 128
SUBLANES = 8
D_IN = 8 * D_GROUP + 2 * N_HEADS
D_IN_MAIN = 8 * D_GROUP
W_IN_SHARD = D_IN // N_DEV
UP_SHARD = D_UP // N_DEV
UP_SHARD_P = 768
D_UP_P = N_DEV * UP_SHARD_P
D_FF_P = D_UP_P // 2
ALPHA = 2.0 ** 0.25
LN_EPS = 1e-5
NEG_BIG = -1e30
ADAM_LR = 0.001
ADAM_B1 = 0.9
ADAM_B2 = 0.999
ADAM_EPS = 1e-08
ADAM_WD = 0.01
ADAM_STEP = 10
VMEM_LIMIT = 56 * 1024 * 1024

SEG_HQ, SEG_HF, SEG_HI, SEG_HG, SEG_MQ, SEG_MK, SEG_MV, SEG_MO = (4 * i for i in range(8))


def _params(sem):
    return pltpu.CompilerParams(dimension_semantics=sem, vmem_limit_bytes=VMEM_LIMIT)


def _dg(a, b, ca, cb, precision=None):
    return lax.dot_general(a, b, (((ca,), (cb,)), ((), ())), precision=precision,
                           preferred_element_type=F32)


def _nn_raw(a, b):
    return _dg(a.astype(BF16), b.astype(BF16), 1, 0)


def _nt_raw(a, b):
    return _dg(a.astype(BF16), b.astype(BF16), 1, 1)


def _tn_raw(a, b):
    return _dg(a.astype(BF16), b.astype(BF16), 0, 0)


@jax.custom_vjp
def _nn(a, b):
    return _nn_raw(a, b)


_nn.defvjp(lambda a, b: (_nn_raw(a, b), (a, b)),
           lambda res, g: (_nt_raw(g, res[1]), _tn_raw(res[0], g)))


@jax.custom_vjp
def _nt(a, b):
    return _nt_raw(a, b)


_nt.defvjp(lambda a, b: (_nt_raw(a, b), (a, b)),
           lambda res, g: (_nn_raw(g, res[1]), _tn_raw(g, res[0])))


@jax.custom_vjp
def _tn(a, b):
    return _tn_raw(a, b)


_tn.defvjp(lambda a, b: (_tn_raw(a, b), (a, b)),
           lambda res, g: (_nt_raw(res[1], g), _nn_raw(res[0], g)))


def _layer_norm(z, g, b):
    mu = jnp.mean(z, axis=-1, keepdims=True)
    var = jnp.mean(jnp.square(z - mu), axis=-1, keepdims=True)
    return (z - mu) * lax.rsqrt(var + LN_EPS) * g + b


def _matmul_nn(a, w, bias, tm, tn, name, out_dtype=F32):
    m, k = a.shape
    if w.ndim == 3:
        n = w.shape[0] * w.shape[2]
        assert tn == w.shape[2]
        w_spec = pl.BlockSpec((None, k, tn), lambda i, j: (j, 0, 0))
    else:
        n = w.shape[1]
        w_spec = pl.BlockSpec((k, tn), lambda i, j: (0, j))

    def body(*refs):
        a_ref, w_ref = refs[0], refs[1]
        o_ref = refs[-1]
        acc = _nn_raw(a_ref[...], w_ref[...])
        if bias is not None:
            acc = acc + refs[2][...]
        o_ref[...] = acc.astype(o_ref.dtype)

    in_specs = [pl.BlockSpec((tm, k), lambda i, j: (i, 0)), w_spec]
    args = [a, w]
    if bias is not None:
        in_specs.append(pl.BlockSpec((1, tn), lambda i, j: (0, j)))
        args.append(bias)
    return pl.pallas_call(
        body, name=name, grid=(m // tm, n // tn), in_specs=in_specs,
        out_specs=pl.BlockSpec((tm, tn), lambda i, j: (i, j)),
        out_shape=jax.ShapeDtypeStruct((m, n), out_dtype),
        compiler_params=_params(("parallel", "parallel")),
    )(*args)


def _matmul_nt(pairs, add, scale, tm, tk, name, out_dtype=F32):
    m = pairs[0][0].shape[0]
    k = pairs[0][1].shape[-2]
    groups = []
    in_specs, args = [], []
    for pair in pairs:
        d, w = pair[0], pair[1]
        in_specs.append(pl.BlockSpec((tm, d.shape[1]), lambda i, j: (i, 0)))
        if w.ndim == 3:
            g = d.shape[1] // w.shape[2]
            blk = pair[2] // g
            in_specs.append(pl.BlockSpec((g, tk, w.shape[2]), lambda i, j, blk=blk: (blk, j, 0)))
            groups.append((g, w.shape[2]))
        else:
            in_specs.append(pl.BlockSpec((tk, w.shape[1]), lambda i, j: (j, 0)))
            groups.append(None)
        args += [d, w]
    if add is not None:
        in_specs.append(pl.BlockSpec((tm, tk), lambda i, j: (i, j)))
        args.append(add)

    def body(*refs):
        o_ref = refs[-1]
        acc = None
        for p, grp in enumerate(groups):
            d_ref, w_ref = refs[2 * p], refs[2 * p + 1]
            if grp is None:
                terms = [_nt_raw(d_ref[...], w_ref[...])]
            else:
                terms = [_nt_raw(d_ref[:, g * grp[1]:(g + 1) * grp[1]], w_ref[g]) for g in range(grp[0])]
            for t in terms:
                acc = t if acc is None else acc + t
        if add is not None:
            acc = acc + scale * refs[2 * len(groups)][...]
        o_ref[...] = acc.astype(o_ref.dtype)

    return pl.pallas_call(
        body, name=name, grid=(m // tm, k // tk), in_specs=in_specs,
        out_specs=pl.BlockSpec((tm, tk), lambda i, j: (i, j)),
        out_shape=jax.ShapeDtypeStruct((m, k), out_dtype),
        compiler_params=_params(("parallel", "parallel")),
    )(*args)


def _matmul_tn(a, b, tm, tn, tt, name, shards=None, shard0=0, group=1, into=None, colsum=False):
    t, m = a.shape
    n = b.shape[1]
    assert not colsum or tm == m
    n_in = 2 + (into is not None)
    out_dtype = BF16
    per_step = 1 if shards is None else group
    width = per_step * tn

    def body(*refs):
        a_ref, b_ref = refs[0], refs[1]
        o_ref, acc_ref = refs[n_in], refs[-1]
        first = pl.program_id(2) == 0

        @pl.when(first)
        def _():
            acc_ref[...] = jnp.zeros_like(acc_ref)

        if shards is None:
            acc_ref[...] += _tn_raw(a_ref[...], b_ref[...])
        else:
            lhs = a_ref[...].astype(BF16)
            for g in range(per_step):
                acc_ref[g] += _tn_raw(lhs, b_ref[:, g * tn:(g + 1) * tn])

        @pl.when(pl.program_id(2) == t // tt - 1)
        def _():
            o_ref[...] = acc_ref[...].astype(o_ref.dtype)

        if colsum:
            s_ref = refs[n_in + 1]

            @pl.when(first)
            def _():
                s_ref[...] = jnp.zeros_like(s_ref)

            s_ref[...] += jnp.sum(b_ref[...], axis=0, keepdims=True)

    in_specs = [pl.BlockSpec((tt, tm), lambda i, j, kk: (kk, i)),
                pl.BlockSpec((tt, width), lambda i, j, kk: (kk, j))]
    args = [a, b]
    aliases = {}
    if into is not None:
        in_specs.append(pl.BlockSpec(memory_space=pl.ANY))
        args.append(into)
        aliases = {2: 0}
    if shards is None:
        out_specs = [pl.BlockSpec((tm, tn), lambda i, j, kk: (i, j))]
        out_shape = [jax.ShapeDtypeStruct((m, n), out_dtype)]
        acc = pltpu.VMEM((tm, tn), F32)
    else:
        out_specs = [pl.BlockSpec((per_step, tm, tn), lambda i, j, kk: (shard0 // per_step + j, i, 0))]
        out_shape = [jax.ShapeDtypeStruct((shards, m, tn), out_dtype)]
        acc = pltpu.VMEM((per_step, tm, tn), F32)
    if colsum:
        out_specs.append(pl.BlockSpec((1, tn), lambda i, j, kk: (0, j)))
        out_shape.append(jax.ShapeDtypeStruct((1, n), F32))
    res = pl.pallas_call(
        body, name=name, grid=(m // tm, n // width, t // tt), in_specs=in_specs, out_specs=out_specs,
        out_shape=out_shape, input_output_aliases=aliases, scratch_shapes=[acc],
        compiler_params=_params(("parallel", "parallel", "arbitrary")),
    )(*args)
    return res if colsum else res[0]


ROW_TILE = 64


def _stack(ref, start, rows):
    return ref[pl.ds(start, rows), :].astype(F32).reshape(rows // SUBLANES, SUBLANES, LANES)


def _vreg_rows(ref, n):
    return [jnp.broadcast_to(ref[j:j + 1, :], (SUBLANES, LANES))[None] for j in range(n)]


def _column_total(acc):
    return jnp.sum(acc, axis=0, keepdims=True)


def _conv_fwd_tile(pad_ref, taps_w, bias, r0, rows):
    taps = len(taps_w)
    acc = bias
    for j in range(taps):
        acc = acc + _stack(pad_ref, SUBLANES - (taps - 1 - j) + r0, rows) * taps_w[j]
    return acc


def _conv_grads_tile(pad_ref, dpad_ref, dx_ref, taps_w, dws, r0, rows):
    taps = len(taps_w)
    x_rows = _stack(pad_ref, SUBLANES + r0, rows)
    dx = None
    for j in range(taps):
        d_shifted = _stack(dpad_ref, r0 + (taps - 1 - j), rows)
        term = d_shifted * taps_w[j]
        dx = term if dx is None else dx + term
        dws[j] = dws[j] + jnp.sum(d_shifted * x_rows, axis=0)
    dx_ref[r0:r0 + rows, :] = dx.reshape(rows, LANES).astype(dx_ref.dtype)
    return jnp.sum(dx, axis=0)


def _ml_conv_fwd(proj, conv_w, conv_b):
    s = proj.shape[0]
    nblk = 2 * D_GROUP // LANES

    def body(x_ref, w_ref, b_ref, o_ref, pad_ref):
        pad_ref[0:SUBLANES, :] = jnp.zeros((SUBLANES, LANES), F32)
        pad_ref[SUBLANES:, :] = x_ref[...].astype(F32)
        taps_w, bias = _vreg_rows(w_ref, ML_CONV), _vreg_rows(b_ref, 1)[0]
        for r0 in range(0, s, ROW_TILE):
            rows = min(ROW_TILE, s - r0)
            o_ref[r0:r0 + rows, :] = jax.nn.silu(_conv_fwd_tile(pad_ref, taps_w, bias, r0, rows)).reshape(rows, LANES)

    return pl.pallas_call(
        body, name="ml_conv_fwd", grid=(nblk,),
        in_specs=[pl.BlockSpec((s, LANES), lambda j: (0, SEG_MQ + j)),
                  pl.BlockSpec((ML_CONV, LANES), lambda j: (0, j)),
                  pl.BlockSpec((1, LANES), lambda j: (0, j))],
        out_specs=pl.BlockSpec((s, LANES), lambda j: (0, j)),
        out_shape=jax.ShapeDtypeStruct((s, 2 * D_GROUP), F32),
        scratch_shapes=[pltpu.VMEM((s + SUBLANES, LANES), F32)],
        compiler_params=_params(("parallel",)),
    )(proj, conv_w, conv_b)


def _ml_conv_bwd(proj, conv_w, conv_b, d_qk, d_proj):
    s = proj.shape[0]
    nblk = 2 * D_GROUP // LANES

    def body(x_ref, w_ref, b_ref, dy_ref, _, dx_ref, dw_ref, db_ref, dxs_ref, pad_ref, dpad_ref):
        pad_ref[0:SUBLANES, :] = jnp.zeros((SUBLANES, LANES), F32)
        pad_ref[SUBLANES:, :] = x_ref[...].astype(F32)
        dpad_ref[s:, :] = jnp.zeros((SUBLANES, LANES), F32)
        taps_w, bias = _vreg_rows(w_ref, ML_CONV), _vreg_rows(b_ref, 1)[0]
        db = jnp.zeros((SUBLANES, LANES), F32)
        for r0 in range(0, s, ROW_TILE):
            rows = min(ROW_TILE, s - r0)
            pre = _conv_fwd_tile(pad_ref, taps_w, bias, r0, rows)
            _, vjp = jax.vjp(jax.nn.silu, pre)
            d_pre, = vjp(_stack(dy_ref, r0, rows))
            dpad_ref[r0:r0 + rows, :] = d_pre.reshape(rows, LANES)
            db = db + jnp.sum(d_pre, axis=0)
        db_ref[...] = _column_total(db)
        dws = [jnp.zeros((SUBLANES, LANES), F32) for _ in range(ML_CONV)]
        dx_sum = jnp.zeros((SUBLANES, LANES), F32)
        for r0 in range(0, s, ROW_TILE):
            dx_sum = dx_sum + _conv_grads_tile(pad_ref, dpad_ref, dx_ref, taps_w, dws, r0, min(ROW_TILE, s - r0))
        dxs_ref[...] = _column_total(dx_sum)
        for j in range(ML_CONV):
            dw_ref[j:j + 1, :] = _column_total(dws[j])

    return pl.pallas_call(
        body, name="ml_conv_bwd", grid=(nblk,),
        in_specs=[pl.BlockSpec((s, LANES), lambda j: (0, SEG_MQ + j)),
                  pl.BlockSpec((ML_CONV, LANES), lambda j: (0, j)),
                  pl.BlockSpec((1, LANES), lambda j: (0, j)),
                  pl.BlockSpec((s, LANES), lambda j: (0, j)),
                  pl.BlockSpec(memory_space=pl.ANY)],
        out_specs=[pl.BlockSpec((s, LANES), lambda j: (0, SEG_MQ + j)),
                   pl.BlockSpec((ML_CONV, LANES), lambda j: (0, j)),
                   pl.BlockSpec((1, LANES), lambda j: (0, j)),
                   pl.BlockSpec((1, LANES), lambda j: (0, j))],
        out_shape=[jax.ShapeDtypeStruct(d_proj.shape, d_proj.dtype),
                   jax.ShapeDtypeStruct((ML_CONV, 2 * D_GROUP), F32),
                   jax.ShapeDtypeStruct((1, 2 * D_GROUP), F32),
                   jax.ShapeDtypeStruct((1, 2 * D_GROUP), F32)],
        input_output_aliases={4: 0},
        scratch_shapes=[pltpu.VMEM((s + SUBLANES, LANES), F32), pltpu.VMEM((s + SUBLANES, LANES), F32)],
        compiler_params=_params(("parallel",)),
    )(proj, conv_w, conv_b, d_qk, d_proj)


def _gelu_mul(a, b):
    return jax.nn.gelu(a) * b


GELU_C = math.sqrt(2.0 / math.pi)
GELU_K = 0.044715


def _gelu_mul_grads(a, b, d):
    a2 = a * a
    t = jnp.tanh(GELU_C * (a + GELU_K * (a * a2)))
    cdf = 0.5 * (1.0 + t)
    slope = cdf + (0.5 * GELU_C) * a * (1.0 - t * t) * (1.0 + (3.0 * GELU_K) * a2)
    return d * b * slope, d * (a * cdf)


FFN_BLOCKS = D_FF_P // LANES


def _ffn_conv_fwd(u, conv_w, conv_b):
    s = u.shape[0]

    def body(g_ref, v_ref, wg_ref, wv_ref, bg_ref, bv_ref, o_ref, gpad_ref, vpad_ref):
        for pad_ref, x_ref in ((gpad_ref, g_ref), (vpad_ref, v_ref)):
            pad_ref[0:SUBLANES, :] = jnp.zeros((SUBLANES, LANES), F32)
            pad_ref[SUBLANES:, :] = x_ref[...].astype(F32)
        taps_g, bias_g = _vreg_rows(wg_ref, FFN_CONV), _vreg_rows(bg_ref, 1)[0]
        taps_v, bias_v = _vreg_rows(wv_ref, FFN_CONV), _vreg_rows(bv_ref, 1)[0]
        for r0 in range(0, s, ROW_TILE):
            rows = min(ROW_TILE, s - r0)
            ug = _conv_fwd_tile(gpad_ref, taps_g, bias_g, r0, rows)
            uv = _conv_fwd_tile(vpad_ref, taps_v, bias_v, r0, rows)
            o_ref[r0:r0 + rows, :] = _gelu_mul(ug, uv).reshape(rows, LANES).astype(o_ref.dtype)

    col = lambda off: (lambda j: (0, off + j))
    return pl.pallas_call(
        body, name="ffn_conv_fwd", grid=(FFN_BLOCKS,),
        in_specs=[pl.BlockSpec((s, LANES), col(0)), pl.BlockSpec((s, LANES), col(FFN_BLOCKS)),
                  pl.BlockSpec((FFN_CONV, LANES), col(0)), pl.BlockSpec((FFN_CONV, LANES), col(FFN_BLOCKS)),
                  pl.BlockSpec((1, LANES), col(0)), pl.BlockSpec((1, LANES), col(FFN_BLOCKS))],
        out_specs=pl.BlockSpec((s, LANES), col(0)),
        out_shape=jax.ShapeDtypeStruct((s, D_FF_P), BF16),
        scratch_shapes=[pltpu.VMEM((s + SUBLANES, LANES), F32), pltpu.VMEM((s + SUBLANES, LANES), F32)],
        compiler_params=_params(("parallel",)),
    )(u, u, conv_w, conv_w, conv_b, conv_b)


def _ffn_conv_bwd(u, conv_w, conv_b, d_h):
    s = u.shape[0]

    def body(g_ref, v_ref, wg_ref, wv_ref, bg_ref, bv_ref, dh_ref,
             dug_ref, duv_ref, dwg_ref, dwv_ref, dbg_ref, dbv_ref,
             gpad_ref, vpad_ref, dgpad_ref, dvpad_ref):
        for pad_ref, x_ref in ((gpad_ref, g_ref), (vpad_ref, v_ref)):
            pad_ref[0:SUBLANES, :] = jnp.zeros((SUBLANES, LANES), F32)
            pad_ref[SUBLANES:, :] = x_ref[...].astype(F32)
        dgpad_ref[s:, :] = jnp.zeros((SUBLANES, LANES), F32)
        dvpad_ref[s:, :] = jnp.zeros((SUBLANES, LANES), F32)
        taps_g, bias_g = _vreg_rows(wg_ref, FFN_CONV), _vreg_rows(bg_ref, 1)[0]
        taps_v, bias_v = _vreg_rows(wv_ref, FFN_CONV), _vreg_rows(bv_ref, 1)[0]
        dbg = jnp.zeros((SUBLANES, LANES), F32)
        dbv = jnp.zeros((SUBLANES, LANES), F32)
        for r0 in range(0, s, ROW_TILE):
            rows = min(ROW_TILE, s - r0)
            ug = _conv_fwd_tile(gpad_ref, taps_g, bias_g, r0, rows)
            uv = _conv_fwd_tile(vpad_ref, taps_v, bias_v, r0, rows)
            d_ug, d_uv = _gelu_mul_grads(ug, uv, _stack(dh_ref, r0, rows))
            dgpad_ref[r0:r0 + rows, :] = d_ug.reshape(rows, LANES)
            dvpad_ref[r0:r0 + rows, :] = d_uv.reshape(rows, LANES)
            dbg = dbg + jnp.sum(d_ug, axis=0)
            dbv = dbv + jnp.sum(d_uv, axis=0)
        dbg_ref[...] = _column_total(dbg)
        dbv_ref[...] = _column_total(dbv)
        for pad_ref, dpad_ref, taps_w, dx_ref, dw_ref in ((gpad_ref, dgpad_ref, taps_g, dug_ref, dwg_ref),
                                                          (vpad_ref, dvpad_ref, taps_v, duv_ref, dwv_ref)):
            dws = [jnp.zeros((SUBLANES, LANES), F32) for _ in range(FFN_CONV)]
            for r0 in range(0, s, ROW_TILE):
                _conv_grads_tile(pad_ref, dpad_ref, dx_ref, taps_w, dws, r0, min(ROW_TILE, s - r0))
            for j in range(FFN_CONV):
                dw_ref[j:j + 1, :] = _column_total(dws[j])

    col = lambda off: (lambda j: (0, off + j))
    seq = pl.BlockSpec((s, LANES), col(0))
    return pl.pallas_call(
        body, name="ffn_conv_bwd", grid=(FFN_BLOCKS,),
        in_specs=[pl.BlockSpec((s, LANES), col(0)), pl.BlockSpec((s, LANES), col(FFN_BLOCKS)),
                  pl.BlockSpec((FFN_CONV, LANES), col(0)), pl.BlockSpec((FFN_CONV, LANES), col(FFN_BLOCKS)),
                  pl.BlockSpec((1, LANES), col(0)), pl.BlockSpec((1, LANES), col(FFN_BLOCKS)), seq],
        out_specs=[seq, seq, pl.BlockSpec((FFN_CONV, LANES), col(0)), pl.BlockSpec((FFN_CONV, LANES), col(0)),
                   pl.BlockSpec((1, LANES), col(0)), pl.BlockSpec((1, LANES), col(0))],
        out_shape=[jax.ShapeDtypeStruct((s, D_FF_P), BF16), jax.ShapeDtypeStruct((s, D_FF_P), BF16),
                   jax.ShapeDtypeStruct((FFN_CONV, D_FF_P), F32), jax.ShapeDtypeStruct((FFN_CONV, D_FF_P), F32),
                   jax.ShapeDtypeStruct((1, D_FF_P), F32), jax.ShapeDtypeStruct((1, D_FF_P), F32)],
        scratch_shapes=[pltpu.VMEM((s + SUBLANES, LANES), F32) for _ in range(4)],
        compiler_params=_params(("parallel",)),
    )(u, u, conv_w, conv_w, conv_b, conv_b, d_h)


def _chunk_masks(c):
    row = lax.broadcasted_iota(jnp.int32, (c, c), 0)
    col = lax.broadcasted_iota(jnp.int32, (c, c), 1)
    return row, col


@jax.custom_vjp
def _split_heads(x):
    return tuple(x[:, h * D_HEAD:(h + 1) * D_HEAD] for h in range(N_HEADS))


_split_heads.defvjp(lambda x: (_split_heads(x), None), lambda _, gs: (jnp.concatenate(gs, axis=1),))


@jax.custom_vjp
def _merge_heads(xs):
    return jnp.concatenate(xs, axis=1)


_merge_heads.defvjp(lambda xs: (_merge_heads(xs), None), lambda _, g: (_split_heads(g),))


@jax.custom_vjp
def _split_chunks(x):
    return tuple(x[i * CHUNK:(i + 1) * CHUNK] for i in range(x.shape[0] // CHUNK))


_split_chunks.defvjp(lambda x: (_split_chunks(x), None), lambda _, gs: (jnp.concatenate(gs, axis=0),))


@jax.custom_vjp
def _merge_chunks(xs):
    return jnp.concatenate(xs, axis=0)


_merge_chunks.defvjp(lambda xs: (_merge_chunks(xs), None), lambda _, g: (_split_chunks(g),))


def _blocks(x):
    return [_split_heads(rows) for rows in _split_chunks(x)]


def _per_chunk_rows(per_chunk, rid):
    out = per_chunk[0]
    for i in range(1, len(per_chunk)):
        out = jnp.where(rid >= i * CHUNK, per_chunk[i], out)
    return out


HEADS = range(N_HEADS)
CHUNKS_PER_STEP = 4
ML_CHUNKS_PER_STEP = 1


def _hg_chunk(hq, hf, hi, hgate, l0, l1, nw, sts):
    n = hq.shape[0] // CHUNK
    row, col = _chunk_masks(n * CHUNK)
    same_chunk = functools.reduce(jnp.logical_or, [(row >= i * CHUNK) & (row < (i + 1) * CHUNK) &
                                                   (col >= i * CHUNK) & (col < (i + 1) * CHUNK) for i in range(n)])
    causal = _chunk_masks(CHUNK)
    causal = causal[1] <= causal[0]
    mx = lax.stop_gradient(jnp.maximum(l0, l1))
    e0 = jnp.exp(l0 - mx)
    e1 = jnp.exp(l1 - mx)
    lb = e0 / (e0 + e1)
    sig = jax.nn.sigmoid(hf)
    lf = jnp.log(lb + (1.0 - lb) * sig)
    k = (1.0 - lb) * jax.nn.sigmoid(-hf)
    q = jax.nn.silu(hq)
    b = _dg(((col <= row) & same_chunk).astype(F32), lf, 1, 0, HIGHEST)
    rid = lax.broadcasted_iota(jnp.int32, b.shape, 0)
    pick = lambda r: jnp.sum(jnp.where(rid == r, b, 0.0), axis=0, keepdims=True)
    b_last_c = [pick(i * CHUNK + CHUNK - 1) for i in range(n)]
    b_ref = _per_chunk_rows([pick(i * CHUNK + CHUNK // 2 - 1) for i in range(n)], rid)
    b_last = _per_chunk_rows(b_last_c, rid)
    qa = _blocks(q * jnp.exp(b - b_ref))
    ka = _blocks(k * jnp.exp(b_ref - b))
    qe = _blocks(q * jnp.exp(b))
    kd = _blocks(k * jnp.exp(b_last - b))
    decay = [_split_heads(jnp.exp(b_last_c[i])) for i in range(n)]
    v = _blocks(hi)
    chunks = range(n)
    attn = [[jnp.where(causal, _nt(qa[i][h], ka[i][h]), 0.0) for h in HEADS] for i in chunks]
    intra = [[_nn(attn[i][h], v[i][h]) for h in HEADS] for i in chunks]
    kv = [[_tn(v[i][h], kd[i][h]) for h in HEADS] for i in chunks]
    normed = []
    for i in chunks:
        inter = [_nt(qe[i][h], sts[h]) for h in HEADS]
        sts = tuple(decay[i][h] * sts[h] + kv[i][h] for h in HEADS)
        o = [intra[i][h] + inter[h] for h in HEADS]
        normed.append(_merge_heads(tuple(o[h] * lax.rsqrt(jnp.mean(o[h] * o[h], axis=-1, keepdims=True) + LN_EPS)
                                         for h in HEADS)))
    return _merge_chunks(tuple(normed)) * nw * jax.nn.silu(hgate), sts


def _seg(ref, seg):
    return ref[:, seg * D_GROUP:(seg + 1) * D_GROUP]


def _hgrn2_fwd(proj, logits, norm_w):
    s = proj.shape[0]
    rows = CHUNKS_PER_STEP * CHUNK
    nc = s // rows

    def body(p_ref, lg_ref, nw_ref, y_ref, st_out_ref, st_scr):
        @pl.when(pl.program_id(0) == 0)
        def _():
            st_scr[...] = jnp.zeros_like(st_scr)

        sts = tuple(st_scr[h] for h in HEADS)
        y, sts_new = _hg_chunk(_seg(p_ref, 0), _seg(p_ref, 1), _seg(p_ref, 2), _seg(p_ref, 3),
                               lg_ref[0:1, :], lg_ref[1:2, :], nw_ref[...], sts)
        y_ref[...] = y.astype(y_ref.dtype)
        for h in HEADS:
            st_out_ref[h] = sts[h]
            st_scr[h] = sts_new[h]

    return pl.pallas_call(
        body, name="hgrn2_fwd", grid=(nc,),
        in_specs=[pl.BlockSpec((rows, 4 * D_GROUP), lambda c: (c, 0)),
                  pl.BlockSpec((2, D_GROUP), lambda c: (0, 0)),
                  pl.BlockSpec((1, D_GROUP), lambda c: (0, 0))],
        out_specs=[pl.BlockSpec((rows, D_GROUP), lambda c: (c, 0)),
                   pl.BlockSpec((None, N_HEADS, D_HEAD, D_HEAD), lambda c: (c, 0, 0, 0))],
        out_shape=[jax.ShapeDtypeStruct((s, 2 * D_GROUP), BF16),
                   jax.ShapeDtypeStruct((nc, N_HEADS, D_HEAD, D_HEAD), F32)],
        scratch_shapes=[pltpu.VMEM((N_HEADS, D_HEAD, D_HEAD), F32)],
        compiler_params=_params(("arbitrary",)),
    )(proj, logits, norm_w)


def _hgrn2_bwd(proj, logits, norm_w, states, d_y):
    s = proj.shape[0]
    rows = CHUNKS_PER_STEP * CHUNK
    nc = s // rows

    def body(p_ref, lg_ref, nw_ref, st_ref, dy_ref, dp_ref, dl_ref, dnw_ref, dsum_ref, dst_scr):
        @pl.when(pl.program_id(0) == 0)
        def _():
            dst_scr[...] = jnp.zeros_like(dst_scr)
            dl_ref[...] = jnp.zeros_like(dl_ref)
            dnw_ref[...] = jnp.zeros_like(dnw_ref)
            dsum_ref[...] = jnp.zeros_like(dsum_ref)

        _, vjp = jax.vjp(_hg_chunk, _seg(p_ref, 0), _seg(p_ref, 1), _seg(p_ref, 2), _seg(p_ref, 3),
                         lg_ref[0:1, :], lg_ref[1:2, :], nw_ref[...], tuple(st_ref[h] for h in HEADS))
        d_hq, d_hf, d_hi, d_hg, d_l0, d_l1, d_nw, d_sts = vjp((dy_ref[...], tuple(dst_scr[h] for h in HEADS)))
        for seg, val in enumerate((d_hq, d_hf, d_hi, d_hg)):
            dp_ref[:, seg * D_GROUP:(seg + 1) * D_GROUP] = val.astype(dp_ref.dtype)
            dsum_ref[:, seg * D_GROUP:(seg + 1) * D_GROUP] += jnp.sum(val, axis=0, keepdims=True)
        dl_ref[0:1, :] += d_l0
        dl_ref[1:2, :] += d_l1
        dnw_ref[...] += d_nw
        for h in HEADS:
            dst_scr[h] = d_sts[h]

    rev = lambda c: nc - 1 - c
    return pl.pallas_call(
        body, name="hgrn2_bwd", grid=(nc,),
        in_specs=[pl.BlockSpec((rows, 4 * D_GROUP), lambda c: (rev(c), 0)),
                  pl.BlockSpec((2, D_GROUP), lambda c: (0, 0)),
                  pl.BlockSpec((1, D_GROUP), lambda c: (0, 0)),
                  pl.BlockSpec((None, N_HEADS, D_HEAD, D_HEAD), lambda c: (rev(c), 0, 0, 0)),
                  pl.BlockSpec((rows, D_GROUP), lambda c: (rev(c), 0))],
        out_specs=[pl.BlockSpec((rows, 4 * D_GROUP), lambda c: (rev(c), 0)),
                   pl.BlockSpec((2, D_GROUP), lambda c: (0, 0)),
                   pl.BlockSpec((1, D_GROUP), lambda c: (0, 0)),
                   pl.BlockSpec((1, 4 * D_GROUP), lambda c: (0, 0))],
        out_shape=[jax.ShapeDtypeStruct((s, D_IN_MAIN), BF16), jax.ShapeDtypeStruct((2, D_GROUP), F32),
                   jax.ShapeDtypeStruct((1, D_GROUP), F32), jax.ShapeDtypeStruct((1, 4 * D_GROUP), F32)],
        scratch_shapes=[pltpu.VMEM((N_HEADS, D_HEAD, D_HEAD), F32)],
        compiler_params=_params(("arbitrary",)),
    )(proj, logits, norm_w, states, d_y)


def _gate_column(gates, lane, idx):
    return jnp.sum(jnp.where(lane == idx, gates, 0.0), axis=1, keepdims=True)


def _head_layer_norm(h):
    mu = jnp.mean(h, axis=-1, keepdims=True)
    var = jnp.mean(jnp.square(h - mu), axis=-1, keepdims=True)
    return (h - mu) * lax.rsqrt(var + LN_EPS)


def _ml_chunk(qc, kc, v, mo, gates, nw, cts, ns, ms):
    n = qc.shape[0] // CHUNK
    row, col = _chunk_masks(CHUNK)
    mask = col <= row
    eye = col == row
    to_row = lambda t: jnp.sum(jnp.where(eye, t, 0.0), axis=0, keepdims=True)
    q = _blocks(qc * (D_HEAD ** -0.5))
    k = _blocks(kc)
    vs = _blocks(v)
    gate_rows = _split_chunks(gates)
    lane = lax.broadcasted_iota(jnp.int32, gate_rows[0].shape, 1)
    each = [(i, h) for i in range(n) for h in HEADS]
    on_each = lambda f: {ih: f(*ih) for ih in each}
    ig = on_each(lambda i, h: _gate_column(gate_rows[i], lane, h))
    lf = on_each(lambda i, h: jax.nn.log_sigmoid(_gate_column(gate_rows[i], lane, N_HEADS + h)))
    lf_row = on_each(lambda i, h: to_row(lf[i, h]))
    ig_row = on_each(lambda i, h: to_row(ig[i, h]))
    b_col = on_each(lambda i, h: jnp.sum(jnp.where(mask, lf_row[i, h], 0.0), axis=1, keepdims=True))
    b_row = on_each(lambda i, h: jnp.sum(jnp.where(row <= col, lf[i, h], 0.0), axis=0, keepdims=True))
    g = on_each(lambda i, h: jnp.sum(lf[i, h], axis=0, keepdims=True))
    d = on_each(lambda i, h: jnp.where(mask, b_col[i, h] - b_row[i, h] + ig_row[i, h], -jnp.inf))
    a = on_each(lambda i, h: g[i, h] - b_col[i, h] + ig[i, h])
    m_at = {(0, h): ms[h] for h in HEADS}
    for i, h in each:
        m_at[i + 1, h] = lax.stop_gradient(jnp.maximum(g[i, h] + m_at[i, h], jnp.max(a[i, h], axis=0, keepdims=True)))
    inter = on_each(lambda i, h: b_col[i, h] + m_at[i, h])
    m_t = on_each(lambda i, h: lax.stop_gradient(jnp.maximum(inter[i, h], jnp.max(d[i, h], axis=1, keepdims=True))))
    qk = on_each(lambda i, h: _nt(q[i][h], k[i][h]))
    sc = on_each(lambda i, h: qk[i, h] * jnp.exp(d[i, h] - m_t[i, h]))
    w_inter = on_each(lambda i, h: jnp.exp(inter[i, h] - m_t[i, h]))
    sv = on_each(lambda i, h: _nn(sc[i, h], vs[i][h]))
    decay = on_each(lambda i, h: jnp.exp(g[i, h] + m_at[i, h] - m_at[i + 1, h]))
    wk = on_each(lambda i, h: k[i][h] * jnp.exp(a[i, h] - m_at[i + 1, h]))
    kv = on_each(lambda i, h: _tn(vs[i][h], wk[i, h]))
    normed = []
    for i in range(n):
        qc_state = [_nt(q[i][h], cts[h]) for h in HEADS]
        num = [sv[i, h] + w_inter[i, h] * qc_state[h] for h in HEADS]
        den = [jnp.sum(sc[i, h], axis=1, keepdims=True)
               + w_inter[i, h] * jnp.sum(q[i][h] * ns[h], axis=1, keepdims=True) for h in HEADS]
        hh = [num[h] / jnp.maximum(jnp.abs(den[h]), jnp.exp(-m_t[i, h])) for h in HEADS]
        cts = tuple(decay[i, h] * cts[h] + kv[i, h] for h in HEADS)
        ns = tuple(decay[i, h] * ns[h] + jnp.sum(wk[i, h], axis=0, keepdims=True) for h in HEADS)
        normed.append(_merge_heads(tuple(_head_layer_norm(hh[h]) for h in HEADS)))
    y = jax.nn.sigmoid(mo) * (_merge_chunks(tuple(normed)) * nw)
    return y, cts, ns, tuple(m_at[n, h] for h in HEADS)


def _mlstm_fwd(qk, proj, gates, norm_w, y):
    s = proj.shape[0]
    rows = ML_CHUNKS_PER_STEP * CHUNK
    nc = s // rows

    def body(qk_ref, vo_ref, g_ref, nw_ref, _, y_ref, ct_out, n_out, m_out, ct_scr, n_scr, m_scr):
        @pl.when(pl.program_id(0) == 0)
        def _():
            ct_scr[...] = jnp.zeros_like(ct_scr)
            n_scr[...] = jnp.zeros_like(n_scr)
            m_scr[...] = jnp.full(m_scr.shape, NEG_BIG, F32)

        cts = tuple(ct_scr[h] for h in HEADS)
        ns = tuple(n_scr[h] for h in HEADS)
        ms = tuple(m_scr[h] for h in HEADS)
        y, cts_new, ns_new, ms_new = _ml_chunk(_seg(qk_ref, 0), _seg(qk_ref, 1), _seg(vo_ref, 0), _seg(vo_ref, 1),
                                               g_ref[...], nw_ref[...], cts, ns, ms)
        y_ref[...] = y.astype(y_ref.dtype)
        for h in HEADS:
            ct_out[h], n_out[h], m_out[h] = cts[h], ns[h], ms[h]
            ct_scr[h], n_scr[h], m_scr[h] = cts_new[h], ns_new[h], ms_new[h]

    st = lambda r, w: pl.BlockSpec((None, N_HEADS, r, w), lambda c: (c, 0, 0, 0))
    return pl.pallas_call(
        body, name="mlstm_fwd", grid=(nc,),
        in_specs=[pl.BlockSpec((rows, 2 * D_GROUP), lambda c: (c, 0)),
                  pl.BlockSpec((rows, 2 * D_GROUP), lambda c: (c, 3)),
                  pl.BlockSpec((rows, LANES), lambda c: (c, 0)),
                  pl.BlockSpec((1, D_GROUP), lambda c: (0, 0)),
                  pl.BlockSpec(memory_space=pl.ANY)],
        out_specs=[pl.BlockSpec((rows, D_GROUP), lambda c: (c, 1)),
                   st(D_HEAD, D_HEAD), st(1, D_HEAD), st(1, 1)],
        out_shape=[jax.ShapeDtypeStruct(y.shape, y.dtype),
                   jax.ShapeDtypeStruct((nc, N_HEADS, D_HEAD, D_HEAD), F32),
                   jax.ShapeDtypeStruct((nc, N_HEADS, 1, D_HEAD), F32),
                   jax.ShapeDtypeStruct((nc, N_HEADS, 1, 1), F32)],
        input_output_aliases={4: 0},
        scratch_shapes=[pltpu.VMEM((N_HEADS, D_HEAD, D_HEAD), F32), pltpu.VMEM((N_HEADS, 1, D_HEAD), F32),
                        pltpu.VMEM((N_HEADS, 1, 1), F32)],
        compiler_params=_params(("arbitrary",)),
    )(qk, proj, gates, norm_w, y)


def _mlstm_bwd(qk, proj, gates, norm_w, ct_s, n_s, m_s, d_y, d_proj):
    s = proj.shape[0]
    rows = ML_CHUNKS_PER_STEP * CHUNK
    nc = s // rows

    def body(qk_ref, vo_ref, g_ref, nw_ref, ct_ref, n_ref, m_ref, dy_ref, _,
             dp_ref, dqk_ref, dg_ref, dnw_ref, dsum_ref, dct_scr, dn_scr):
        @pl.when(pl.program_id(0) == 0)
        def _():
            dct_scr[...] = jnp.zeros_like(dct_scr)
            dn_scr[...] = jnp.zeros_like(dn_scr)
            dnw_ref[...] = jnp.zeros_like(dnw_ref)
            dsum_ref[...] = jnp.zeros_like(dsum_ref)

        ms = tuple(m_ref[h] for h in HEADS)
        step = lambda *a: _ml_chunk(*a, ms)[:3]
        _, vjp = jax.vjp(step, _seg(qk_ref, 0), _seg(qk_ref, 1), _seg(vo_ref, 0), _seg(vo_ref, 1), g_ref[...],
                         nw_ref[...], tuple(ct_ref[h] for h in HEADS), tuple(n_ref[h] for h in HEADS))
        d_q, d_k, d_v, d_o, d_gates, d_nw, d_cts, d_ns = vjp(
            (dy_ref[...], tuple(dct_scr[h] for h in HEADS), tuple(dn_scr[h] for h in HEADS)))
        dqk_ref[:, 0:D_GROUP] = d_q
        dqk_ref[:, D_GROUP:2 * D_GROUP] = d_k
        for seg, val in enumerate((d_v, d_o)):
            dp_ref[:, seg * D_GROUP:(seg + 1) * D_GROUP] = val.astype(dp_ref.dtype)
            dsum_ref[:, seg * D_GROUP:(seg + 1) * D_GROUP] += jnp.sum(val, axis=0, keepdims=True)
        dg_ref[...] = d_gates
        dnw_ref[...] += d_nw
        for h in HEADS:
            dct_scr[h] = d_cts[h]
            dn_scr[h] = d_ns[h]

    rev = lambda c: nc - 1 - c
    st = lambda r, w: pl.BlockSpec((None, N_HEADS, r, w), lambda c: (rev(c), 0, 0, 0))
    return pl.pallas_call(
        body, name="mlstm_bwd", grid=(nc,),
        in_specs=[pl.BlockSpec((rows, 2 * D_GROUP), lambda c: (rev(c), 0)),
                  pl.BlockSpec((rows, 2 * D_GROUP), lambda c: (rev(c), 3)),
                  pl.BlockSpec((rows, LANES), lambda c: (rev(c), 0)),
                  pl.BlockSpec((1, D_GROUP), lambda c: (0, 0)),
                  st(D_HEAD, D_HEAD), st(1, D_HEAD), st(1, 1),
                  pl.BlockSpec((rows, D_GROUP), lambda c: (rev(c), 1)),
                  pl.BlockSpec(memory_space=pl.ANY)],
        out_specs=[pl.BlockSpec((rows, 2 * D_GROUP), lambda c: (rev(c), 3)),
                   pl.BlockSpec((rows, 2 * D_GROUP), lambda c: (rev(c), 0)),
                   pl.BlockSpec((rows, LANES), lambda c: (rev(c), 0)),
                   pl.BlockSpec((1, D_GROUP), lambda c: (0, 0)),
                   pl.BlockSpec((1, 2 * D_GROUP), lambda c: (0, 0))],
        out_shape=[jax.ShapeDtypeStruct(d_proj.shape, d_proj.dtype), jax.ShapeDtypeStruct((s, 2 * D_GROUP), F32),
                   jax.ShapeDtypeStruct((s, LANES), F32), jax.ShapeDtypeStruct((1, D_GROUP), F32),
                   jax.ShapeDtypeStruct((1, 2 * D_GROUP), F32)],
        input_output_aliases={8: 0},
        scratch_shapes=[pltpu.VMEM((N_HEADS, D_HEAD, D_HEAD), F32), pltpu.VMEM((N_HEADS, 1, D_HEAD), F32)],
        compiler_params=_params(("arbitrary",)),
    )(qk, proj, gates, norm_w, ct_s, n_s, m_s, d_y, d_proj)


LN_TOKENS = 512
ATT_TOKENS = 512


def _proj_res_ln(a, w, xres, g, b, name):
    s, dm = xres.shape
    k = a.shape[1]
    tb = min(LN_TOKENS, s)

    def body(a_ref, w_ref, x_ref, g_ref, b_ref, z_ref, o_ref):
        z = ALPHA * x_ref[...] + _nn_raw(a_ref[...], w_ref[...])
        z_ref[...] = z
        o_ref[...] = _layer_norm(z, g_ref[...], b_ref[...])

    tok = pl.BlockSpec((tb, dm), lambda i: (i, 0))
    vec = pl.BlockSpec((1, dm), lambda i: (0, 0))
    act = jax.ShapeDtypeStruct((s, dm), F32)
    return pl.pallas_call(
        body, name=name, grid=(s // tb,),
        in_specs=[pl.BlockSpec((tb, k), lambda i: (i, 0)), pl.BlockSpec((k, dm), lambda i: (0, 0)), tok, vec, vec],
        out_specs=[tok, tok], out_shape=[act, act], compiler_params=_params(("parallel",)),
    )(a, w, xres, g, b)


def _ln_bwd_proj(d_out, z, g, b, w, name):
    s, dm = z.shape
    k = w.shape[0]
    tb = min(LN_TOKENS, s)

    def body(do_ref, z_ref, g_ref, b_ref, w_ref, dz_ref, da_ref, dg_ref, db_ref):
        @pl.when(pl.program_id(0) == 0)
        def _():
            dg_ref[...] = jnp.zeros_like(dg_ref)
            db_ref[...] = jnp.zeros_like(db_ref)

        _, vjp = jax.vjp(_layer_norm, z_ref[...], g_ref[...], b_ref[...])
        d_z, d_g, d_b = vjp(do_ref[...])
        dz_ref[...] = d_z
        da_ref[...] = _nt_raw(d_z, w_ref[...])
        dg_ref[...] += d_g
        db_ref[...] += d_b

    tok = pl.BlockSpec((tb, dm), lambda i: (i, 0))
    vec = pl.BlockSpec((1, dm), lambda i: (0, 0))
    return pl.pallas_call(
        body, name=name, grid=(s // tb,),
        in_specs=[tok, tok, vec, vec, pl.BlockSpec((k, dm), lambda i: (0, 0))],
        out_specs=[tok, pl.BlockSpec((tb, k), lambda i: (i, 0)), vec, vec],
        out_shape=[jax.ShapeDtypeStruct((s, dm), F32), jax.ShapeDtypeStruct((s, k), F32),
                   jax.ShapeDtypeStruct((1, dm), F32), jax.ShapeDtypeStruct((1, dm), F32)],
        compiler_params=_params(("arbitrary",)),
    )(d_out, z, g, b, w)


def _proj_loss_tail(a, w, xres, g, b, target):
    s, dm = xres.shape
    k = a.shape[1]
    tb = min(ATT_TOKENS, s)

    def loss_fn(z, gg, bb, tgt):
        err = jnp.square(_layer_norm(z, gg, bb) - tgt)
        return 0.5 * jnp.sum(jnp.mean(err, axis=-1, keepdims=True), axis=0, keepdims=True)

    def body(a_ref, w_ref, x_ref, g_ref, b_ref, t_ref, loss_ref, dz_ref, dg_ref, db_ref):
        @pl.when(pl.program_id(0) == 0)
        def _():
            loss_ref[...] = jnp.zeros_like(loss_ref)
            dg_ref[...] = jnp.zeros_like(dg_ref)
            db_ref[...] = jnp.zeros_like(db_ref)

        z = ALPHA * x_ref[...] + _nn_raw(a_ref[...], w_ref[...])
        tgt = t_ref[...]
        loss, vjp = jax.vjp(lambda zz, gg, bb: loss_fn(zz, gg, bb, tgt), z, g_ref[...], b_ref[...])
        d_z, d_g, d_b = vjp(jnp.ones((1, 1), F32))
        loss_ref[...] += loss
        dz_ref[...] = d_z
        dg_ref[...] += d_g
        db_ref[...] += d_b

    tok = pl.BlockSpec((tb, dm), lambda i: (i, 0))
    vec = pl.BlockSpec((1, dm), lambda i: (0, 0))
    one = pl.BlockSpec((1, 1), lambda i: (0, 0))
    return pl.pallas_call(
        body, name="ffn_down_loss_tail", grid=(s // tb,),
        in_specs=[pl.BlockSpec((tb, k), lambda i: (i, 0)), pl.BlockSpec((k, dm), lambda i: (0, 0)), tok, vec, vec, tok],
        out_specs=[one, tok, vec, vec],
        out_shape=[jax.ShapeDtypeStruct((1, 1), F32), jax.ShapeDtypeStruct((s, dm), F32),
                   jax.ShapeDtypeStruct((1, dm), F32), jax.ShapeDtypeStruct((1, dm), F32)],
        compiler_params=_params(("arbitrary",)),
    )(a, w, xres, g, b, target)


def _att_heads(qs, ks, vs):
    sc = [_nt(q, k) * (CA_DH ** -0.5) for q, k in zip(qs, ks)]
    p = [jax.nn.softmax(s, axis=-1) for s in sc]
    return tuple(_nn(pp, v) for pp, v in zip(p, vs))


def _head_slices(ref_or_value, offset):
    return tuple(ref_or_value[:, offset + h * CA_DH:offset + (h + 1) * CA_DH] for h in range(CA_HEADS))


def _cross_attention_fwd(x1, kv, wq, wo, g, b):
    s = x1.shape[0]
    tb = min(ATT_TOKENS, s)

    def body(x_ref, kv_ref, wq_ref, wo_ref, g_ref, b_ref, att_ref, z_ref, o_ref):
        x_blk = x_ref[...]
        q = _nn_raw(x_blk, wq_ref[...])
        att = jnp.concatenate(_att_heads(_head_slices(q, 0), _head_slices(kv_ref, 0), _head_slices(kv_ref, D_MODEL)),
                              axis=1)
        att_ref[...] = att.astype(att_ref.dtype)
        z = ALPHA * x_blk + _nn_raw(att, wo_ref[...])
        z_ref[...] = z
        o_ref[...] = _layer_norm(z, g_ref[...], b_ref[...])

    tok = pl.BlockSpec((tb, D_MODEL), lambda i: (i, 0))
    mat = pl.BlockSpec((D_MODEL, D_MODEL), lambda i: (0, 0))
    vec = pl.BlockSpec((1, D_MODEL), lambda i: (0, 0))
    act = jax.ShapeDtypeStruct((s, D_MODEL), F32)
    return pl.pallas_call(
        body, name="cross_attention_fwd", grid=(s // tb,),
        in_specs=[tok, pl.BlockSpec((N_MEM, 2 * D_MODEL), lambda i: (0, 0)), mat, mat, vec, vec],
        out_specs=[tok, tok, tok],
        out_shape=[jax.ShapeDtypeStruct((s, D_MODEL), BF16), act, act],
        compiler_params=_params(("parallel",)),
    )(x1, kv, wq, wo, g, b)


def _cross_attention_bwd(d_x2, x1, z2, kv, wq, wo, g, b):
    s = x1.shape[0]
    tb = min(ATT_TOKENS, s)

    def body(dx2_ref, x_ref, z_ref, kv_ref, wq_ref, wo_ref, g_ref, b_ref,
             dx1_ref, dq_ref, dz_ref, dkv_ref, dg_ref, db_ref):
        @pl.when(pl.program_id(0) == 0)
        def _():
            dkv_ref[...] = jnp.zeros_like(dkv_ref)
            dg_ref[...] = jnp.zeros_like(dg_ref)
            db_ref[...] = jnp.zeros_like(db_ref)

        _, ln_vjp = jax.vjp(_layer_norm, z_ref[...], g_ref[...], b_ref[...])
        d_z, d_g, d_b = ln_vjp(dx2_ref[...])
        dg_ref[...] += d_g
        db_ref[...] += d_b
        dz_ref[...] = d_z.astype(dz_ref.dtype)
        d_att = _nt_raw(d_z, wo_ref[...])
        q = _nn_raw(x_ref[...], wq_ref[...])
        _, vjp = jax.vjp(_att_heads, _head_slices(q, 0), _head_slices(kv_ref, 0), _head_slices(kv_ref, D_MODEL))
        d_qs, d_ks, d_vs = vjp(_head_slices(d_att, 0))
        for h in range(CA_HEADS):
            lo = h * CA_DH
            dkv_ref[:, lo:lo + CA_DH] += d_ks[h]
            dkv_ref[:, D_MODEL + lo:D_MODEL + lo + CA_DH] += d_vs[h]
        d_q = jnp.concatenate(d_qs, axis=1)
        dq_ref[...] = d_q.astype(dq_ref.dtype)
        dx1_ref[...] = ALPHA * d_z + _nt_raw(d_q, wq_ref[...])

    tok = pl.BlockSpec((tb, D_MODEL), lambda i: (i, 0))
    mem = pl.BlockSpec((N_MEM, 2 * D_MODEL), lambda i: (0, 0))
    mat = pl.BlockSpec((D_MODEL, D_MODEL), lambda i: (0, 0))
    vec = pl.BlockSpec((1, D_MODEL), lambda i: (0, 0))
    low = jax.ShapeDtypeStruct((s, D_MODEL), BF16)
    return pl.pallas_call(
        body, name="cross_attention_bwd", grid=(s // tb,),
        in_specs=[tok, tok, tok, mem, mat, mat, vec, vec], out_specs=[tok, tok, tok, mem, vec, vec],
        out_shape=[jax.ShapeDtypeStruct((s, D_MODEL), F32), low, low,
                   jax.ShapeDtypeStruct((N_MEM, 2 * D_MODEL), F32),
                   jax.ShapeDtypeStruct((1, D_MODEL), F32), jax.ShapeDtypeStruct((1, D_MODEL), F32)],
        compiler_params=_params(("arbitrary",)),
    )(d_x2, x1, z2, kv, wq, wo, g, b)


def _local_step(x, mem, target, w, mid_weights=None, ffn_weights=None, on_ffn_grads=None, on_mid_grads=None,
                on_small_grads=None, on_last_grads=None):
    w = dict(w)
    s = x.shape[0]
    tm = min(512, s)
    tt = min(512, s)
    proj = _matmul_nn(x, w["w_in_main"], w["b_in_main"], min(2048, s), 512, "proj")
    gates = _matmul_nn(x, w["w_in_gate"], w["b_in_gate"], tm, LANES, "proj_gates")
    qk = _ml_conv_fwd(proj, w["ml_conv_w"], w["ml_conv_b"])
    y, hg_states = _hgrn2_fwd(proj, w["hg_lb_logits"], w["hg_norm_w"])
    y, ct_s, n_s, m_s = _mlstm_fwd(qk, proj, gates, w["ml_norm_w"], y)
    if mid_weights is not None:
        w.update(mid_weights(y))
    z1, x1 = _proj_res_ln(y, w["w_out"], x, w["ln1_g"], w["ln1_b"], "out_proj_ln1")
    kv = _matmul_nn(mem, w["ca_wkv"], None, N_MEM, CA_DH, "kv")
    att, z2, x2 = _cross_attention_fwd(x1, kv, w["ca_wq"], w["ca_wo"], w["ln2_g"], w["ln2_b"])
    if ffn_weights is not None:
        w.update(ffn_weights(x2))
    u = _matmul_nn(x2, w["ffn_w_up"], None, min(2048, s), UP_SHARD_P, "ffn_up", BF16)
    hid = _ffn_conv_fwd(u, w["ffn_conv_w"], w["ffn_conv_b"])
    loss, d_z3, d_ln3_g, d_ln3_b = _proj_loss_tail(hid, w["ffn_w_down"], x2, w["ln3_g"], w["ln3_b"], target)
    grads = {"ln3_g": d_ln3_g, "ln3_b": d_ln3_b}
    grads["ffn_w_down"] = _matmul_tn(hid, d_z3, 1536, D_MODEL, tt, "d_w_down")
    d_hid = _matmul_nt([(d_z3, w["ffn_w_down"])], None, 1.0, tm, D_FF_P, "d_hid", BF16)
    d_ug, d_uv, d_cwg, d_cwv, d_cbg, d_cbv = _ffn_conv_bwd(u, w["ffn_conv_w"], w["ffn_conv_b"], d_hid)
    grads["ffn_conv_w"] = jnp.concatenate([d_cwg, d_cwv], axis=-1)
    grads["ffn_conv_b"] = jnp.concatenate([d_cbg, d_cbv], axis=-1)
    half = N_DEV // 2
    d_w_up = _matmul_tn(x2, d_ug, D_MODEL, UP_SHARD_P, tt, "d_w_up_gate", shards=N_DEV, group=half)
    grads["ffn_w_up"] = _matmul_tn(x2, d_uv, D_MODEL, UP_SHARD_P, tt, "d_w_up_val", shards=N_DEV,
                                   shard0=half, group=half, into=d_w_up)
    d_x2 = _matmul_nt([(d_ug, w["ffn_w_up"], 0), (d_uv, w["ffn_w_up"], N_DEV // 2)], d_z3, ALPHA,
                      min(256, s), D_MODEL, "d_x2")
    if on_ffn_grads is not None:
        d_x2 = on_ffn_grads(grads, d_x2)
    d_x1, d_q, d_z2, d_kv, grads["ln2_g"], grads["ln2_b"] = _cross_attention_bwd(
        d_x2, x1, z2, kv, w["ca_wq"], w["ca_wo"], w["ln2_g"], w["ln2_b"])
    grads["ca_wo"] = _matmul_tn(att, d_z2, D_MODEL, D_MODEL, tt, "d_ca_wo")
    grads["ca_wq"] = _matmul_tn(x1, d_q, D_MODEL, D_MODEL, tt, "d_ca_wq")
    grads["ca_wkv"] = _matmul_tn(mem, d_kv, D_MODEL, CA_DH, N_MEM, "d_ca_wkv", shards=N_DEV, group=N_DEV)
    d_z1, d_y, grads["ln1_g"], grads["ln1_b"] = _ln_bwd_proj(d_x1, z1, w["ln1_g"], w["ln1_b"], w["w_out"],
                                                             "ln1_bwd_out_proj")
    grads["w_out"] = _matmul_tn(y, d_z1, D_MODEL, D_MODEL, tt, "d_w_out")
    if on_mid_grads is not None:
        d_y = on_mid_grads(grads, d_y)
    d_proj, grads["hg_lb_logits"], grads["hg_norm_w"], db_hg = _hgrn2_bwd(
        proj, w["hg_lb_logits"], w["hg_norm_w"], hg_states, d_y)
    d_proj, d_qk, d_gates, grads["ml_norm_w"], db_vo = _mlstm_bwd(
        qk, proj, gates, w["ml_norm_w"], ct_s, n_s, m_s, d_y, d_proj)
    d_proj, grads["ml_conv_w"], grads["ml_conv_b"], db_qk = _ml_conv_bwd(
        proj, w["ml_conv_w"], w["ml_conv_b"], d_qk, d_proj)
    grads["b_in_main"] = jnp.concatenate([db_hg, db_qk, db_vo], axis=-1)
    grads["w_in_gate"], grads["b_in_gate"] = _matmul_tn(x, d_gates, D_MODEL, LANES, tt, "d_w_in_gates", colsum=True)
    if on_small_grads is not None:
        d_proj = on_small_grads(grads, loss, d_proj)
    grads["w_in_main"] = _matmul_tn(x, d_proj, D_MODEL, min(2048, D_IN_MAIN), tt, "d_w_in")
    if on_last_grads is not None:
        d_z1 = on_last_grads(grads, d_z1)
    grad_x = _matmul_nt([(d_proj, w["w_in_main"]), (d_gates, w["w_in_gate"])], d_z1, ALPHA, tm, D_MODEL, "d_x")
    return loss, grad_x, grads


HBM_SPEC = pl.BlockSpec(memory_space=pltpu.HBM)


def _coords():
    return lax.axis_index("x"), lax.axis_index("y"), lax.axis_index("c")


def _other_chips(x, y):
    return [(1 - x, y), (x, 1 - y), (1 - x, 1 - y)]


def _all_gather_two_level(shards, name):
    na = len(shards)

    def body(*refs):
        x_refs, out_refs = refs[:na], refs[na:2 * na]
        send_sems, recv_sems, local_sems = refs[2 * na:]
        x, y, c = _coords()
        me, sibling = (x, y, c), (x, y, 1 - c)
        chips = _other_chips(x, y)

        def copy(a, k, block, to, own=False):
            slot = out_refs[a].at[4 * block[0] + 2 * block[1] + block[2]]
            return pltpu.make_async_remote_copy(
                src_ref=x_refs[a] if own else slot, dst_ref=slot,
                send_sem=send_sems.at[7 * a + k], recv_sem=recv_sems.at[7 * a + k],
                device_id=to, device_id_type=MESH)

        mine = [pltpu.make_async_copy(x_refs[a], out_refs[a].at[4 * x + 2 * y + c], local_sems.at[a])
                for a in range(na)]
        for cp in mine:
            cp.start()
        first = []
        for a in range(na):
            first.append(copy(a, 0, me, sibling, own=True))
            first += [copy(a, 1 + j, me, (*chip, c), own=True) for j, chip in enumerate(chips)]
        for cp in first:
            cp.start()
        passed = []
        for j, chip in enumerate(chips):
            for a in range(na):
                copy(a, 1 + j, (*chip, c), me).wait_recv()
                fwd = copy(a, 4 + j, (*chip, c), sibling)
                fwd.start()
                passed.append(fwd)
        for a in range(na):
            copy(a, 0, sibling, me).wait_recv()
            for j, chip in enumerate(chips):
                copy(a, 4 + j, (*chip, 1 - c), me).wait_recv()
        for cp in first + passed:
            cp.wait_send()
        for cp in mine:
            cp.wait()

    return pl.pallas_call(
        body, name=name,
        out_shape=[jax.ShapeDtypeStruct((N_DEV,) + t.shape, t.dtype) for t in shards],
        in_specs=[HBM_SPEC] * na, out_specs=[HBM_SPEC] * na,
        scratch_shapes=[pltpu.SemaphoreType.DMA((7 * na,)), pltpu.SemaphoreType.DMA((7 * na,)),
                        pltpu.SemaphoreType.DMA((na,))],
    )(*shards)


SEM_SPEC = pl.BlockSpec(memory_space=pltpu.SEMAPHORE)
ANY_SPEC = pl.BlockSpec(memory_space=pl.ANY)
SIDE_EFFECT = pltpu.SideEffectType.DATAFLOW_SIDE_EFFECTING


def _peer(x, y, c, d):
    flip = lambda v, bit: 1 - v if bit else v
    p = (flip(x, d & 4), flip(y, d & 2), flip(c, d & 1))
    return p, 4 * p[0] + 2 * p[1] + p[2]


def _direct_copies(gather, src_refs, land_refs, send_sems, recv_sems):
    x, y, c = _coords()
    me = 4 * x + 2 * y + c
    copies = []
    for a in range(len(src_refs)):
        for d in range(1, N_DEV):
            peer, peer_slot = _peer(x, y, c, d)
            copies.append(pltpu.make_async_remote_copy(
                src_ref=src_refs[a] if gather else src_refs[a].at[peer_slot],
                dst_ref=land_refs[a].at[me] if gather else land_refs[a].at[d - 1],
                send_sem=send_sems.at[7 * a + d - 1], recv_sem=recv_sems.at[7 * a + d - 1],
                device_id=peer, device_id_type=MESH))
    return copies


def _hbm(t):
    return pltpu.HBM(t.shape, t.dtype)


def _direct_start(gather, arrays, through, name):
    na = len(arrays)
    lands = [lax.empty((N_DEV,) + t.shape if gather else (N_DEV - 1,) + t.shape[1:], t.dtype) for t in arrays]
    n_io = 2 * na + 1

    def body(*refs):
        for cp in _direct_copies(gather, refs[:na], refs[na:2 * na], refs[n_io], refs[n_io + 1]):
            cp.start()

    ins = [pltpu.with_memory_space_constraint(t, pltpu.HBM) for t in (*arrays, *lands, through)]
    sems = pltpu.SemaphoreType.DMA((7 * na,))
    res = pl.pallas_call(
        body, name=name, out_shape=(sems, sems, *[_hbm(t) for t in ins]),
        in_specs=[HBM_SPEC] * n_io, out_specs=(SEM_SPEC, SEM_SPEC, *[HBM_SPEC] * n_io),
        input_output_aliases={i: 2 + i for i in range(n_io)},
        compiler_params=pltpu.CompilerParams(has_side_effects=SIDE_EFFECT),
    )(*ins)
    return (res[0], res[1], list(res[2:2 + na]), list(res[2 + na:2 + 2 * na])), res[2 + 2 * na]


def _direct_wait(gather, started, after, name):
    send_sems, recv_sems, arrays, lands = started
    na = len(arrays)
    after = list(after) if isinstance(after, (list, tuple)) else [after]

    def body(*refs):
        for cp in _direct_copies(gather, refs[:na], refs[na:2 * na], refs[2 * na], refs[2 * na + 1]):
            cp.wait_send()
            cp.wait_recv()

    res = pl.pallas_call(
        body, name=name, out_shape=tuple(_hbm(t) for t in (*arrays, *lands)),
        in_specs=[HBM_SPEC] * (2 * na) + [SEM_SPEC, SEM_SPEC] + [ANY_SPEC] * len(after),
        out_specs=tuple([HBM_SPEC] * (2 * na)), input_output_aliases={i: i for i in range(2 * na)},
        compiler_params=pltpu.CompilerParams(has_side_effects=SIDE_EFFECT),
    )(*arrays, *lands, send_sems, recv_sems, *after)
    return list(res[:na]), list(res[na:])


def _row_tile(rows):
    for t in (256, 176, 128):
        if rows % t == 0 and rows > t:
            return t
    return rows


def _adamw_math(g, w, m, v):
    m_new = ADAM_B1 * m + (1.0 - ADAM_B1) * g
    v_new = ADAM_B2 * v + (1.0 - ADAM_B2) * jnp.square(g)
    m_hat = m_new / (1.0 - ADAM_B1 ** ADAM_STEP)
    v_hat = v_new / (1.0 - ADAM_B2 ** ADAM_STEP)
    delta = -ADAM_LR * (m_hat / (jnp.sqrt(v_hat) + ADAM_EPS) + ADAM_WD * w)
    return delta, m_new, v_new


def _adamw_sharded(chip, sums, got, w, m, v, name):
    r, c = w.shape
    tr = _row_tile(r)
    n_got = got.shape[0]

    def body(chip_ref, s_ref, g_ref, w_ref, m_ref, v_ref, go_ref, d_ref, nm_ref, nv_ref):
        g = s_ref[...].astype(F32)
        for i in range(n_got):
            g = g + g_ref[i].astype(F32)
        delta, m_new, v_new = _adamw_math(g, w_ref[...], m_ref[...], v_ref[...])
        go_ref[...] = g
        d_ref[...] = delta
        nm_ref[...] = m_new
        nv_ref[...] = v_new

    blk = pl.BlockSpec((tr, c), lambda i, chip_ref: (i, 0))
    out = jax.ShapeDtypeStruct((r, c), F32)
    return pl.pallas_call(
        body, name=name,
        grid_spec=pltpu.PrefetchScalarGridSpec(
            num_scalar_prefetch=1, grid=(r // tr,),
            in_specs=[pl.BlockSpec((None, tr, c), lambda i, chip_ref: (chip_ref[0], i, 0)),
                      pl.BlockSpec((n_got, tr, c), lambda i, chip_ref: (0, i, 0)), blk, blk, blk],
            out_specs=[blk, blk, blk, blk]),
        out_shape=[out, out, out, out],
        compiler_params=_params(("parallel",)),
    )(chip, sums, got, w, m, v)


def _adamw_replicated(parts, w, m, v):
    p, r, c = parts.shape

    def body(p_ref, w_ref, m_ref, v_ref, g_ref, d_ref, nm_ref, nv_ref):
        g = p_ref[0]
        for i in range(1, p):
            g = g + p_ref[i]
        delta, m_new, v_new = _adamw_math(g, w_ref[...], m_ref[...], v_ref[...])
        g_ref[...] = g
        d_ref[...] = delta
        nm_ref[...] = m_new
        nv_ref[...] = v_new

    blk = pl.BlockSpec((r, c), lambda i: (0, 0))
    out = jax.ShapeDtypeStruct((r, c), F32)
    return pl.pallas_call(
        body, name="adamw_replicated", grid=(1,),
        in_specs=[pl.BlockSpec((p, r, c), lambda i: (0, 0, 0)), blk, blk, blk],
        out_specs=[blk, blk, blk, blk], out_shape=[out, out, out, out],
        compiler_params=_params(("arbitrary",)),
    )(parts, w, m, v)


SHARDED_NAMES = ("w_in", "ml_conv_w", "w_out", "ca_wq", "ca_wkv", "ca_wo", "ffn_w_up", "ffn_conv_w", "ffn_w_down")
SMALL_NAMES = ("b_in", "hg_lb_logits", "hg_norm_w", "ml_conv_b", "ml_norm_w", "ln1_g", "ln1_b",
               "ln2_g", "ln2_b", "ffn_conv_b", "ln3_g", "ln3_b")
WEIGHT_NAMES = ("w_in", "b_in", "hg_lb_logits", "hg_norm_w", "ml_conv_w", "ml_conv_b", "ml_norm_w", "w_out",
                "ln1_g", "ln1_b", "ca_wq", "ca_wkv", "ca_wo", "ln2_g", "ln2_b", "ffn_w_up", "ffn_conv_w",
                "ffn_conv_b", "ffn_w_down", "ln3_g", "ln3_b")
PAD_TO = {"ffn_w_up": UP_SHARD_P, "ffn_conv_w": UP_SHARD_P}
SMALL_ROWS = 24
SMALL_W = D_MODEL


def _shard_2d(name, block):
    t = block[0]
    if name in PAD_TO:
        t = jnp.pad(t, ((0, 0), (0, PAD_TO[name] - t.shape[1])))
    return t


def _shard_like(name, t, like):
    return t[:, :like.shape[2]][None]


def _pad_cols(t, width):
    return jnp.pad(t, ((0, 0), (0, width - t.shape[1])))


FIRST_NAMES = ("w_in", "ml_conv_w")
FFN_NAMES = ("ffn_w_up", "ffn_w_down", "ffn_conv_w")
MID_NAMES = ("ca_wo", "ca_wq", "ca_wkv", "w_out")


def _first_weights(g, small):
    w = dict(small)
    w_in = jnp.concatenate([g["w_in"][j] for j in range(N_DEV)], axis=1)
    w["w_in_main"] = w_in[:, :D_IN_MAIN]
    w["w_in_gate"] = _pad_cols(w_in[:, D_IN_MAIN:], LANES)
    w["b_in_main"] = small["b_in"][:, :D_IN_MAIN]
    w["b_in_gate"] = _pad_cols(small["b_in"][:, D_IN_MAIN:], LANES)
    w["ml_conv_w"] = jnp.transpose(g["ml_conv_w"], (1, 0, 2)).reshape(ML_CONV, 2 * D_GROUP)
    return w


def _mid_weights(g):
    w = {n: g[n].reshape(D_MODEL, D_MODEL) for n in ("w_out", "ca_wq", "ca_wo")}
    w["ca_wkv"] = g["ca_wkv"]
    return w


def _ffn_weights(g, small):
    w = {"ffn_w_up": g["ffn_w_up"]}
    down = g["ffn_w_down"].reshape(N_DEV // 2, UP_SHARD, D_MODEL)
    w["ffn_w_down"] = jnp.pad(down, ((0, 0), (0, UP_SHARD_P - UP_SHARD), (0, 0))).reshape(D_FF_P, D_MODEL)
    w["ffn_conv_w"] = jnp.transpose(g["ffn_conv_w"], (1, 0, 2)).reshape(FFN_CONV, D_UP_P)
    w["ffn_conv_b"] = _pad_cols(small["ffn_conv_b"].reshape(N_DEV, UP_SHARD), UP_SHARD_P).reshape(1, D_UP_P)
    return w


def _whole_weights(g, small):
    return {**_first_weights(g, small), **_mid_weights(g), **_ffn_weights(g, small)}


def _owner_stack(n, grads):
    if n == "w_in":
        w_in = jnp.concatenate([grads["w_in_main"], grads["w_in_gate"][:, :D_IN - D_IN_MAIN]], axis=1)
        return jnp.stack([w_in[:, j * W_IN_SHARD:(j + 1) * W_IN_SHARD] for j in range(N_DEV)])
    if n in ("w_out", "ca_wq", "ca_wo"):
        return grads[n].reshape(N_DEV, D_MODEL // N_DEV, D_MODEL)
    if n == "ffn_w_down":
        down = grads[n].reshape(N_DEV // 2, UP_SHARD_P, D_MODEL)[:, :UP_SHARD]
        return down.reshape(N_DEV, D_FF // N_DEV, D_MODEL)
    if n == "ml_conv_w":
        return jnp.transpose(grads[n].reshape(ML_CONV, N_DEV, LANES), (1, 0, 2))
    if n == "ffn_conv_w":
        return jnp.transpose(grads[n].reshape(FFN_CONV, N_DEV, UP_SHARD_P), (1, 0, 2))
    return grads[n]


def _owner_stacks(grads):
    return {n: _owner_stack(n, grads) for n in SHARDED_NAMES}


def _small_grads(grads):
    out = {n: grads[n] for n in SMALL_NAMES if n in grads}
    out["b_in"] = jnp.concatenate([grads["b_in_main"], grads["b_in_gate"][:, :D_IN - D_IN_MAIN]], axis=1)
    out["ffn_conv_b"] = grads["ffn_conv_b"].reshape(N_DEV, UP_SHARD_P)[:, :UP_SHARD].reshape(1, D_UP)
    return out


def _pack_small(p, extra=None):
    flat = [p[n].reshape(-1) for n in SMALL_NAMES]
    if extra is not None:
        flat.append(extra.reshape(-1))
    flat = jnp.concatenate(flat)
    return jnp.pad(flat, (0, SMALL_ROWS * SMALL_W - flat.shape[0])).reshape(SMALL_ROWS, SMALL_W)


def _unpack_small(slab, like):
    out = {}
    flat = slab.reshape(-1)
    o = 0
    for n in SMALL_NAMES:
        out[n] = flat[o:o + like[n].size].reshape(like[n].shape)
        o += like[n].size
    return out, flat[o]


def kernel(x, mem, w_in, b_in, hg_lb_logits, hg_norm_w, ml_conv_w, ml_conv_b, ml_norm_w, w_out, ln1_g, ln1_b, ca_wq, ca_wkv, ca_wo, ln2_g, ln2_b, ffn_w_up, ffn_conv_w, ffn_conv_b, ffn_w_down, ln3_g, ln3_b, loss_target, m_w_in, m_b_in, m_hg_lb_logits, m_hg_norm_w, m_ml_conv_w, m_ml_conv_b, m_ml_norm_w, m_w_out, m_ln1_g, m_ln1_b, m_ca_wq, m_ca_wkv, m_ca_wo, m_ln2_g, m_ln2_b, m_ffn_w_up, m_ffn_conv_w, m_ffn_conv_b, m_ffn_w_down, m_ln3_g, m_ln3_b, v_w_in, v_b_in, v_hg_lb_logits, v_hg_norm_w, v_ml_conv_w, v_ml_conv_b, v_ml_norm_w, v_w_out, v_ln1_g, v_ln1_b, v_ca_wq, v_ca_wkv, v_ca_wo, v_ln2_g, v_ln2_b, v_ffn_w_up, v_ffn_conv_w, v_ffn_conv_b, v_ffn_w_down, v_ln3_g, v_ln3_b):
    params = dict(w_in=w_in, b_in=b_in, hg_lb_logits=hg_lb_logits, hg_norm_w=hg_norm_w, ml_conv_w=ml_conv_w,
                  ml_conv_b=ml_conv_b, ml_norm_w=ml_norm_w, w_out=w_out, ln1_g=ln1_g, ln1_b=ln1_b, ca_wq=ca_wq,
                  ca_wkv=ca_wkv, ca_wo=ca_wo, ln2_g=ln2_g, ln2_b=ln2_b, ffn_w_up=ffn_w_up, ffn_conv_w=ffn_conv_w,
                  ffn_conv_b=ffn_conv_b, ffn_w_down=ffn_w_down, ln3_g=ln3_g, ln3_b=ln3_b)
    mom1 = dict(w_in=m_w_in, b_in=m_b_in, hg_lb_logits=m_hg_lb_logits, hg_norm_w=m_hg_norm_w,
                ml_conv_w=m_ml_conv_w, ml_conv_b=m_ml_conv_b, ml_norm_w=m_ml_norm_w, w_out=m_w_out, ln1_g=m_ln1_g,
                ln1_b=m_ln1_b, ca_wq=m_ca_wq, ca_wkv=m_ca_wkv, ca_wo=m_ca_wo, ln2_g=m_ln2_g, ln2_b=m_ln2_b,
                ffn_w_up=m_ffn_w_up, ffn_conv_w=m_ffn_conv_w, ffn_conv_b=m_ffn_conv_b, ffn_w_down=m_ffn_w_down,
                ln3_g=m_ln3_g, ln3_b=m_ln3_b)
    mom2 = dict(w_in=v_w_in, b_in=v_b_in, hg_lb_logits=v_hg_lb_logits, hg_norm_w=v_hg_norm_w,
                ml_conv_w=v_ml_conv_w, ml_conv_b=v_ml_conv_b, ml_norm_w=v_ml_norm_w, w_out=v_w_out, ln1_g=v_ln1_g,
                ln1_b=v_ln1_b, ca_wq=v_ca_wq, ca_wkv=v_ca_wkv, ca_wo=v_ca_wo, ln2_g=v_ln2_g, ln2_b=v_ln2_b,
                ffn_w_up=v_ffn_w_up, ffn_conv_w=v_ffn_conv_w, ffn_conv_b=v_ffn_conv_b, ffn_w_down=v_ffn_w_down,
                ln3_g=v_ln3_g, ln3_b=v_ln3_b)

    x_idx, y_idx, c_idx = _coords()
    as_index = lambda v: jnp.reshape(v, (1,)).astype(jnp.int32)
    me = as_index(4 * x_idx + 2 * y_idx + c_idx)
    small_params = {n: params[n] for n in SMALL_NAMES}

    shards = {n: _shard_2d(n, params[n]) for n in SHARDED_NAMES}
    to_send = lambda names: [shards[n] if "conv" in n else shards[n].astype(BF16) for n in names]
    first = dict(zip(FIRST_NAMES, _all_gather_two_level(to_send(FIRST_NAMES), "weights_gather_first")))
    mid_started, through = _direct_start(True, to_send(MID_NAMES), first["w_in"], "weights_gather_start_mid")
    ffn_started, first["w_in"] = _direct_start(True, to_send(FFN_NAMES), through, "weights_gather_start_ffn")

    def gathered_weights(names, started, after, tag):
        mine, lands = _direct_wait(True, started, after, "weights_gather_wait_" + tag)
        return {n: lax.dynamic_update_index_in_dim(land, own, me[0], 0) for n, own, land in zip(names, mine, lands)}

    started, own_stacks = {}, {}

    def start_group(names, tag):
        def hook(grads, through):
            own_stacks[tag] = [_owner_stack(n, grads).astype(BF16) for n in names]
            started[tag], through = _direct_start(False, own_stacks[tag], through, "grads_start_" + tag)
            return through
        return hook

    def start_small(grads, loss, through):
        started["small"], through = _direct_start(True, [_pack_small(_small_grads(grads), loss)], through,
                                                  "small_gather_start")
        return through

    loss, grad_x, grads = _local_step(
        x[0], mem[0], loss_target[0], _first_weights(first, small_params),
        lambda y: _mid_weights(gathered_weights(MID_NAMES, mid_started, y, "mid")),
        lambda x2: _ffn_weights(gathered_weights(FFN_NAMES, ffn_started, x2, "ffn"), small_params),
        start_group(FFN_NAMES, "ffn"), start_group(MID_NAMES, "mid"), start_small, start_group(FIRST_NAMES, "last"))

    sharded_out = {}

    def update_group(names, tag, after):
        _, lands = _direct_wait(False, started[tag], after, "grads_wait_" + tag)
        for n, st, land in zip(names, own_stacks[tag], lands):
            res = _adamw_sharded(me, st, land, shards[n], _shard_2d(n, mom1[n]), _shard_2d(n, mom2[n]), "adamw_" + n)
            sharded_out[n] = [_shard_like(n, t, params[n]) for t in res]

    update_group(FFN_NAMES, "ffn", grad_x)
    update_group(MID_NAMES, "mid", grad_x)
    own_small, small_lands = _direct_wait(True, started["small"], grad_x, "small_gather_wait")
    small_parts = lax.dynamic_update_index_in_dim(small_lands[0], own_small[0], me[0], 0)
    small_res = _adamw_replicated(small_parts, _pack_small(params), _pack_small(mom1), _pack_small(mom2))
    small_out = [_unpack_small(slab, params) for slab in small_res]
    done = [t for n in FFN_NAMES + MID_NAMES for t in sharded_out[n]]
    done += [t for small, _ in small_out for t in small.values()]
    update_group(FIRST_NAMES, "last", done)

    outs = []
    for k, (small, _) in enumerate(small_out):
        outs.extend(sharded_out[n][k] if n in sharded_out else small[n] for n in WEIGHT_NAMES)
    return (small_out[0][1], grad_x[None], *outs)
```

```python
import functools
import math

import jax
import jax.numpy as jnp
from jax import lax
from jax.experimental import pallas as pl
from jax.experimental.pallas import tpu as pltpu

F32 = jnp.float32
BF16 = jnp.bfloat16
HIGHEST = lax.Precision.HIGHEST
MESH = pl.DeviceIdType.MESH

N_DEV = 8
D_MODEL = 1024
N_MEM = 256
N_HEADS = 4
D_HEAD = 128
D_GROUP = N_HEADS * D_HEAD
CHUNK = 64
ML_CONV = 4
FFN_CONV = 3
D_FF = 2816
D_UP = 2 * D_FF
CA_HEADS = 4
CA_DH = D_MODEL // CA_HEADS
LANES = 128
SUBLANES = 8
D_IN = 8 * D_GROUP + 2 * N_HEADS
D_IN_MAIN = 8 * D_GROUP
W_IN_SHARD = D_IN // N_DEV
UP_SHARD = D_UP // N_DEV
UP_SHARD_P = 768
D_UP_P = N_DEV * UP_SHARD_P
D_FF_P = D_UP_P // 2
ALPHA = 2.0 ** 0.25
LN_EPS = 1e-5
NEG_BIG = -1e30
ADAM_LR = 0.001
ADAM_B1 = 0.9
ADAM_B2 = 0.999
ADAM_EPS = 1e-08
ADAM_WD = 0.01
ADAM_STEP = 10
VMEM_LIMIT = 56 * 1024 * 1024

SEG_HQ, SEG_HF, SEG_HI, SEG_HG, SEG_MQ, SEG_MK, SEG_MV, SEG_MO = (4 * i for i in range(8))


def _params(sem):
    return pltpu.CompilerParams(dimension_semantics=sem, vmem_limit_bytes=VMEM_LIMIT)


def _dg(a, b, ca, cb, precision=None):
    return lax.dot_general(a, b, (((ca,), (cb,)), ((), ())), precision=precision,
                           preferred_element_type=F32)


def _nn_raw(a, b):
    return _dg(a.astype(BF16), b.astype(BF16), 1, 0)


def _nt_raw(a, b):
    return _dg(a.astype(BF16), b.astype(BF16), 1, 1)


def _tn_raw(a, b):
    return _dg(a.astype(BF16), b.astype(BF16), 0, 0)


@jax.custom_vjp
def _nn(a, b):
    return _nn_raw(a, b)


_nn.defvjp(lambda a, b: (_nn_raw(a, b), (a, b)),
           lambda res, g: (_nt_raw(g, res[1]), _tn_raw(res[0], g)))


@jax.custom_vjp
def _nt(a, b):
    return _nt_raw(a, b)


_nt.defvjp(lambda a, b: (_nt_raw(a, b), (a, b)),
           lambda res, g: (_nn_raw(g, res[1]), _tn_raw(g, res[0])))


@jax.custom_vjp
def _tn(a, b):
    return _tn_raw(a, b)


_tn.defvjp(lambda a, b: (_tn_raw(a, b), (a, b)),
           lambda res, g: (_nt_raw(res[1], g), _nn_raw(res[0], g)))


def _layer_norm(z, g, b):
    mu = jnp.mean(z, axis=-1, keepdims=True)
    var = jnp.mean(jnp.square(z - mu), axis=-1, keepdims=True)
    return (z - mu) * lax.rsqrt(var + LN_EPS) * g + b


def _matmul_nn(a, w, bias, tm, tn, name, out_dtype=F32):
    m, k = a.shape
    if w.ndim == 3:
        n = w.shape[0] * w.shape[2]
        assert tn == w.shape[2]
        w_spec = pl.BlockSpec((None, k, tn), lambda i, j: (j, 0, 0))
    else:
        n = w.shape[1]
        w_spec = pl.BlockSpec((k, tn), lambda i, j: (0, j))

    def body(*refs):
        a_ref, w_ref = refs[0], refs[1]
        o_ref = refs[-1]
        acc = _nn_raw(a_ref[...], w_ref[...])
        if bias is not None:
            acc = acc + refs[2][...]
        o_ref[...] = acc.astype(o_ref.dtype)

    in_specs = [pl.BlockSpec((tm, k), lambda i, j: (i, 0)), w_spec]
    args = [a, w]
    if bias is not None:
        in_specs.append(pl.BlockSpec((1, tn), lambda i, j: (0, j)))
        args.append(bias)
    return pl.pallas_call(
        body, name=name, grid=(m // tm, n // tn), in_specs=in_specs,
        out_specs=pl.BlockSpec((tm, tn), lambda i, j: (i, j)),
        out_shape=jax.ShapeDtypeStruct((m, n), out_dtype),
        compiler_params=_params(("parallel", "parallel")),
    )(*args)


def _matmul_nt(pairs, add, scale, tm, tk, name, out_dtype=F32):
    m = pairs[0][0].shape[0]
    k = pairs[0][1].shape[-2]
    groups = []
    in_specs, args = [], []
    for pair in pairs:
        d, w = pair[0], pair[1]
        in_specs.append(pl.BlockSpec((tm, d.shape[1]), lambda i, j: (i, 0)))
        if w.ndim == 3:
            g = d.shape[1] // w.shape[2]
            blk = pair[2] // g
            in_specs.append(pl.BlockSpec((g, tk, w.shape[2]), lambda i, j, blk=blk: (blk, j, 0)))
            groups.append((g, w.shape[2]))
        else:
            in_specs.append(pl.BlockSpec((tk, w.shape[1]), lambda i, j: (j, 0)))
            groups.append(None)
        args += [d, w]
    if add is not None:
        in_specs.append(pl.BlockSpec((tm, tk), lambda i, j: (i, j)))
        args.append(add)

    def body(*refs):
        o_ref = refs[-1]
        acc = None
        for p, grp in enumerate(groups):
            d_ref, w_ref = refs[2 * p], refs[2 * p + 1]
            if grp is None:
                terms = [_nt_raw(d_ref[...], w_ref[...])]
            else:
                terms = [_nt_raw(d_ref[:, g * grp[1]:(g + 1) * grp[1]], w_ref[g]) for g in range(grp[0])]
            for t in terms:
                acc = t if acc is None else acc + t
        if add is not None:
            acc = acc + scale * refs[2 * len(groups)][...]
        o_ref[...] = acc.astype(o_ref.dtype)

    return pl.pallas_call(
        body, name=name, grid=(m // tm, k // tk), in_specs=in_specs,
        out_specs=pl.BlockSpec((tm, tk), lambda i, j: (i, j)),
        out_shape=jax.ShapeDtypeStruct((m, k), out_dtype),
        compiler_params=_params(("parallel", "parallel")),
    )(*args)


def _matmul_tn(a, b, tm, tn, tt, name, shards=None, shard0=0, group=1, into=None, colsum=False):
    t, m = a.shape
    n = b.shape[1]
    assert not colsum or tm == m
    n_in = 2 + (into is not None)
    out_dtype = BF16
    per_step = 1 if shards is None else group
    width = per_step * tn

    def body(*refs):
        a_ref, b_ref = refs[0], refs[1]
        o_ref, acc_ref = refs[n_in], refs[-1]
        first = pl.program_id(2) == 0

        @pl.when(first)
        def _():
            acc_ref[...] = jnp.zeros_like(acc_ref)

        if shards is None:
            acc_ref[...] += _tn_raw(a_ref[...], b_ref[...])
        else:
            lhs = a_ref[...].astype(BF16)
            for g in range(per_step):
                acc_ref[g] += _tn_raw(lhs, b_ref[:, g * tn:(g + 1) * tn])

        @pl.when(pl.program_id(2) == t // tt - 1)
        def _():
            o_ref[...] = acc_ref[...].astype(o_ref.dtype)

        if colsum:
            s_ref = refs[n_in + 1]

            @pl.when(first)
            def _():
                s_ref[...] = jnp.zeros_like(s_ref)

            s_ref[...] += jnp.sum(b_ref[...], axis=0, keepdims=True)

    in_specs = [pl.BlockSpec((tt, tm), lambda i, j, kk: (kk, i)),
                pl.BlockSpec((tt, width), lambda i, j, kk: (kk, j))]
    args = [a, b]
    aliases = {}
    if into is not None:
        in_specs.append(pl.BlockSpec(memory_space=pl.ANY))
        args.append(into)
        aliases = {2: 0}
    if shards is None:
        out_specs = [pl.BlockSpec((tm, tn), lambda i, j, kk: (i, j))]
        out_shape = [jax.ShapeDtypeStruct((m, n), out_dtype)]
        acc = pltpu.VMEM((tm, tn), F32)
    else:
        out_specs = [pl.BlockSpec((per_step, tm, tn), lambda i, j, kk: (shard0 // per_step + j, i, 0))]
        out_shape = [jax.ShapeDtypeStruct((shards, m, tn), out_dtype)]
        acc = pltpu.VMEM((per_step, tm, tn), F32)
    if colsum:
        out_specs.append(pl.BlockSpec((1, tn), lambda i, j, kk: (0, j)))
        out_shape.append(jax.ShapeDtypeStruct((1, n), F32))
    res = pl.pallas_call(
        body, name=name, grid=(m // tm, n // width, t // tt), in_specs=in_specs, out_specs=out_specs,
        out_shape=out_shape, input_output_aliases=aliases, scratch_shapes=[acc],
        compiler_params=_params(("parallel", "parallel", "arbitrary")),
    )(*args)
    return res if colsum else res[0]


ROW_TILE = 64


def _stack(ref, start, rows):
    return ref[pl.ds(start, rows), :].astype(F32).reshape(rows // SUBLANES, SUBLANES, LANES)


def _vreg_rows(ref, n):
    return [jnp.broadcast_to(ref[j:j + 1, :], (SUBLANES, LANES))[None] for j in range(n)]


def _column_total(acc):
    return jnp.sum(acc, axis=0, keepdims=True)


def _conv_fwd_tile(pad_ref, taps_w, bias, r0, rows):
    taps = len(taps_w)
    acc = bias
    for j in range(taps):
        acc = acc + _stack(pad_ref, SUBLANES - (taps - 1 - j) + r0, rows) * taps_w[j]
    return acc


def _conv_grads_tile(pad_ref, dpad_ref, dx_ref, taps_w, dws, r0, rows):
    taps = len(taps_w)
    x_rows = _stack(pad_ref, SUBLANES + r0, rows)
    dx = None
    for j in range(taps):
        d_shifted = _stack(dpad_ref, r0 + (taps - 1 - j), rows)
        term = d_shifted * taps_w[j]
        dx = term if dx is None else dx + term
        dws[j] = dws[j] + jnp.sum(d_shifted * x_rows, axis=0)
    dx_ref[r0:r0 + rows, :] = dx.reshape(rows, LANES).astype(dx_ref.dtype)
    return jnp.sum(dx, axis=0)


def _ml_conv_fwd(proj, conv_w, conv_b):
    s = proj.shape[0]
    nblk = 2 * D_GROUP // LANES

    def body(x_ref, w_ref, b_ref, o_ref, pad_ref):
        pad_ref[0:SUBLANES, :] = jnp.zeros((SUBLANES, LANES), F32)
        pad_ref[SUBLANES:, :] = x_ref[...].astype(F32)
        taps_w, bias = _vreg_rows(w_ref, ML_CONV), _vreg_rows(b_ref, 1)[0]
        for r0 in range(0, s, ROW_TILE):
            rows = min(ROW_TILE, s - r0)
            o_ref[r0:r0 + rows, :] = jax.nn.silu(_conv_fwd_tile(pad_ref, taps_w, bias, r0, rows)).reshape(rows, LANES)

    return pl.pallas_call(
        body, name="ml_conv_fwd", grid=(nblk,),
        in_specs=[pl.BlockSpec((s, LANES), lambda j: (0, SEG_MQ + j)),
                  pl.BlockSpec((ML_CONV, LANES), lambda j: (0, j)),
                  pl.BlockSpec((1, LANES), lambda j: (0, j))],
        out_specs=pl.BlockSpec((s, LANES), lambda j: (0, j)),
        out_shape=jax.ShapeDtypeStruct((s, 2 * D_GROUP), F32),
        scratch_shapes=[pltpu.VMEM((s + SUBLANES, LANES), F32)],
        compiler_params=_params(("parallel",)),
    )(proj, conv_w, conv_b)


def _ml_conv_bwd(proj, conv_w, conv_b, d_qk, d_proj):
    s = proj.shape[0]
    nblk = 2 * D_GROUP // LANES

    def body(x_ref, w_ref, b_ref, dy_ref, _, dx_ref, dw_ref, db_ref, dxs_ref, pad_ref, dpad_ref):
        pad_ref[0:SUBLANES, :] = jnp.zeros((SUBLANES, LANES), F32)
        pad_ref[SUBLANES:, :] = x_ref[...].astype(F32)
        dpad_ref[s:, :] = jnp.zeros((SUBLANES, LANES), F32)
        taps_w, bias = _vreg_rows(w_ref, ML_CONV), _vreg_rows(b_ref, 1)[0]
        db = jnp.zeros((SUBLANES, LANES), F32)
        for r0 in range(0, s, ROW_TILE):
            rows = min(ROW_TILE, s - r0)
            pre = _conv_fwd_tile(pad_ref, taps_w, bias, r0, rows)
            _, vjp = jax.vjp(jax.nn.silu, pre)
            d_pre, = vjp(_stack(dy_ref, r0, rows))
            dpad_ref[r0:r0 + rows, :] = d_pre.reshape(rows, LANES)
            db = db + jnp.sum(d_pre, axis=0)
        db_ref[...] = _column_total(db)
        dws = [jnp.zeros((SUBLANES, LANES), F32) for _ in range(ML_CONV)]
        dx_sum = jnp.zeros((SUBLANES, LANES), F32)
        for r0 in range(0, s, ROW_TILE):
            dx_sum = dx_sum + _conv_grads_tile(pad_ref, dpad_ref, dx_ref, taps_w, dws, r0, min(ROW_TILE, s - r0))
        dxs_ref[...] = _column_total(dx_sum)
        for j in range(ML_CONV):
            dw_ref[j:j + 1, :] = _column_total(dws[j])

    return pl.pallas_call(
        body, name="ml_conv_bwd", grid=(nblk,),
        in_specs=[pl.BlockSpec((s, LANES), lambda j: (0, SEG_MQ + j)),
                  pl.BlockSpec((ML_CONV, LANES), lambda j: (0, j)),
                  pl.BlockSpec((1, LANES), lambda j: (0, j)),
                  pl.BlockSpec((s, LANES), lambda j: (0, j)),
                  pl.BlockSpec(memory_space=pl.ANY)],
        out_specs=[pl.BlockSpec((s, LANES), lambda j: (0, SEG_MQ + j)),
                   pl.BlockSpec((ML_CONV, LANES), lambda j: (0, j)),
                   pl.BlockSpec((1, LANES), lambda j: (0, j)),
                   pl.BlockSpec((1, LANES), lambda j: (0, j))],
        out_shape=[jax.ShapeDtypeStruct(d_proj.shape, d_proj.dtype),
                   jax.ShapeDtypeStruct((ML_CONV, 2 * D_GROUP), F32),
                   jax.ShapeDtypeStruct((1, 2 * D_GROUP), F32),
                   jax.ShapeDtypeStruct((1, 2 * D_GROUP), F32)],
        input_output_aliases={4: 0},
        scratch_shapes=[pltpu.VMEM((s + SUBLANES, LANES), F32), pltpu.VMEM((s + SUBLANES, LANES), F32)],
        compiler_params=_params(("parallel",)),
    )(proj, conv_w, conv_b, d_qk, d_proj)


def _gelu_mul(a, b):
    return jax.nn.gelu(a) * b


GELU_C = math.sqrt(2.0 / math.pi)
GELU_K = 0.044715


def _gelu_mul_grads(a, b, d):
    a2 = a * a
    t = jnp.tanh(GELU_C * (a + GELU_K * (a * a2)))
    cdf = 0.5 * (1.0 + t)
    slope = cdf + (0.5 * GELU_C) * a * (1.0 - t * t) * (1.0 + (3.0 * GELU_K) * a2)
    return d * b * slope, d * (a * cdf)


FFN_BLOCKS = D_FF_P // LANES


def _ffn_conv_fwd(u, conv_w, conv_b):
    s = u.shape[0]

    def body(g_ref, v_ref, wg_ref, wv_ref, bg_ref, bv_ref, o_ref, gpad_ref, vpad_ref):
        for pad_ref, x_ref in ((gpad_ref, g_ref), (vpad_ref, v_ref)):
            pad_ref[0:SUBLANES, :] = jnp.zeros((SUBLANES, LANES), F32)
            pad_ref[SUBLANES:, :] = x_ref[...].astype(F32)
        taps_g, bias_g = _vreg_rows(wg_ref, FFN_CONV), _vreg_rows(bg_ref, 1)[0]
        taps_v, bias_v = _vreg_rows(wv_ref, FFN_CONV), _vreg_rows(bv_ref, 1)[0]
        for r0 in range(0, s, ROW_TILE):
            rows = min(ROW_TILE, s - r0)
            ug = _conv_fwd_tile(gpad_ref, taps_g, bias_g, r0, rows)
            uv = _conv_fwd_tile(vpad_ref, taps_v, bias_v, r0, rows)
            o_ref[r0:r0 + rows, :] = _gelu_mul(ug, uv).reshape(rows, LANES).astype(o_ref.dtype)

    col = lambda off: (lambda j: (0, off + j))
    return pl.pallas_call(
        body, name="ffn_conv_fwd", grid=(FFN_BLOCKS,),
        in_specs=[pl.BlockSpec((s, LANES), col(0)), pl.BlockSpec((s, LANES), col(FFN_BLOCKS)),
                  pl.BlockSpec((FFN_CONV, LANES), col(0)), pl.BlockSpec((FFN_CONV, LANES), col(FFN_BLOCKS)),
                  pl.BlockSpec((1, LANES), col(0)), pl.BlockSpec((1, LANES), col(FFN_BLOCKS))],
        out_specs=pl.BlockSpec((s, LANES), col(0)),
        out_shape=jax.ShapeDtypeStruct((s, D_FF_P), BF16),
        scratch_shapes=[pltpu.VMEM((s + SUBLANES, LANES), F32), pltpu.VMEM((s + SUBLANES, LANES), F32)],
        compiler_params=_params(("parallel",)),
    )(u, u, conv_w, conv_w, conv_b, conv_b)


def _ffn_conv_bwd(u, conv_w, conv_b, d_h):
    s = u.shape[0]

    def body(g_ref, v_ref, wg_ref, wv_ref, bg_ref, bv_ref, dh_ref,
             dug_ref, duv_ref, dwg_ref, dwv_ref, dbg_ref, dbv_ref,
             gpad_ref, vpad_ref, dgpad_ref, dvpad_ref):
        for pad_ref, x_ref in ((gpad_ref, g_ref), (vpad_ref, v_ref)):
            pad_ref[0:SUBLANES, :] = jnp.zeros((SUBLANES, LANES), F32)
            pad_ref[SUBLANES:, :] = x_ref[...].astype(F32)
        dgpad_ref[s:, :] = jnp.zeros((SUBLANES, LANES), F32)
        dvpad_ref[s:, :] = jnp.zeros((SUBLANES, LANES), F32)
        taps_g, bias_g = _vreg_rows(wg_ref, FFN_CONV), _vreg_rows(bg_ref, 1)[0]
        taps_v, bias_v = _vreg_rows(wv_ref, FFN_CONV), _vreg_rows(bv_ref, 1)[0]
        dbg = jnp.zeros((SUBLANES, LANES), F32)
        dbv = jnp.zeros((SUBLANES, LANES), F32)
        for r0 in range(0, s, ROW_TILE):
            rows = min(ROW_TILE, s - r0)
            ug = _conv_fwd_tile(gpad_ref, taps_g, bias_g, r0, rows)
            uv = _conv_fwd_tile(vpad_ref, taps_v, bias_v, r0, rows)
            d_ug, d_uv = _gelu_mul_grads(ug, uv, _stack(dh_ref, r0, rows))
            dgpad_ref[r0:r0 + rows, :] = d_ug.reshape(rows, LANES)
            dvpad_ref[r0:r0 + rows, :] = d_uv.reshape(rows, LANES)
            dbg = dbg + jnp.sum(d_ug, axis=0)
            dbv = dbv + jnp.sum(d_uv, axis=0)
        dbg_ref[...] = _column_total(dbg)
        dbv_ref[...] = _column_total(dbv)
        for pad_ref, dpad_ref, taps_w, dx_ref, dw_ref in ((gpad_ref, dgpad_ref, taps_g, dug_ref, dwg_ref),
                                                          (vpad_ref, dvpad_ref, taps_v, duv_ref, dwv_ref)):
            dws = [jnp.zeros((SUBLANES, LANES), F32) for _ in range(FFN_CONV)]
            for r0 in range(0, s, ROW_TILE):
                _conv_grads_tile(pad_ref, dpad_ref, dx_ref, taps_w, dws, r0, min(ROW_TILE, s - r0))
            for j in range(FFN_CONV):
                dw_ref[j:j + 1, :] = _column_total(dws[j])

    col = lambda off: (lambda j: (0, off + j))
    seq = pl.BlockSpec((s, LANES), col(0))
    return pl.pallas_call(
        body, name="ffn_conv_bwd", grid=(FFN_BLOCKS,),
        in_specs=[pl.BlockSpec((s, LANES), col(0)), pl.BlockSpec((s, LANES), col(FFN_BLOCKS)),
                  pl.BlockSpec((FFN_CONV, LANES), col(0)), pl.BlockSpec((FFN_CONV, LANES), col(FFN_BLOCKS)),
                  pl.BlockSpec((1, LANES), col(0)), pl.BlockSpec((1, LANES), col(FFN_BLOCKS)), seq],
        out_specs=[seq, seq, pl.BlockSpec((FFN_CONV, LANES), col(0)), pl.BlockSpec((FFN_CONV, LANES), col(0)),
                   pl.BlockSpec((1, LANES), col(0)), pl.BlockSpec((1, LANES), col(0))],
        out_shape=[jax.ShapeDtypeStruct((s, D_FF_P), BF16), jax.ShapeDtypeStruct((s, D_FF_P), BF16),
                   jax.ShapeDtypeStruct((FFN_CONV, D_FF_P), F32), jax.ShapeDtypeStruct((FFN_CONV, D_FF_P), F32),
                   jax.ShapeDtypeStruct((1, D_FF_P), F32), jax.ShapeDtypeStruct((1, D_FF_P), F32)],
        scratch_shapes=[pltpu.VMEM((s + SUBLANES, LANES), F32) for _ in range(4)],
        compiler_params=_params(("parallel",)),
    )(u, u, conv_w, conv_w, conv_b, conv_b, d_h)


def _chunk_masks(c):
    row = lax.broadcasted_iota(jnp.int32, (c, c), 0)
    col = lax.broadcasted_iota(jnp.int32, (c, c), 1)
    return row, col


@jax.custom_vjp
def _split_heads(x):
    return tuple(x[:, h * D_HEAD:(h + 1) * D_HEAD] for h in range(N_HEADS))


_split_heads.defvjp(lambda x: (_split_heads(x), None), lambda _, gs: (jnp.concatenate(gs, axis=1),))


@jax.custom_vjp
def _merge_heads(xs):
    return jnp.concatenate(xs, axis=1)


_merge_heads.defvjp(lambda xs: (_merge_heads(xs), None), lambda _, g: (_split_heads(g),))


@jax.custom_vjp
def _split_chunks(x):
    return tuple(x[i * CHUNK:(i + 1) * CHUNK] for i in range(x.shape[0] // CHUNK))


_split_chunks.defvjp(lambda x: (_split_chunks(x), None), lambda _, gs: (jnp.concatenate(gs, axis=0),))


@jax.custom_vjp
def _merge_chunks(xs):
    return jnp.concatenate(xs, axis=0)


_merge_chunks.defvjp(lambda xs: (_merge_chunks(xs), None), lambda _, g: (_split_chunks(g),))


def _blocks(x):
    return [_split_heads(rows) for rows in _split_chunks(x)]


def _per_chunk_rows(per_chunk, rid):
    out = per_chunk[0]
    for i in range(1, len(per_chunk)):
        out = jnp.where(rid >= i * CHUNK, per_chunk[i], out)
    return out


HEADS = range(N_HEADS)
CHUNKS_PER_STEP = 4
ML_CHUNKS_PER_STEP = 1


def _hg_chunk(hq, hf, hi, hgate, l0, l1, nw, sts):
    n = hq.shape[0] // CHUNK
    row, col = _chunk_masks(n * CHUNK)
    same_chunk = functools.reduce(jnp.logical_or, [(row >= i * CHUNK) & (row < (i + 1) * CHUNK) &
                                                   (col >= i * CHUNK) & (col < (i + 1) * CHUNK) for i in range(n)])
    causal = _chunk_masks(CHUNK)
    causal = causal[1] <= causal[0]
    mx = lax.stop_gradient(jnp.maximum(l0, l1))
    e0 = jnp.exp(l0 - mx)
    e1 = jnp.exp(l1 - mx)
    lb = e0 / (e0 + e1)
    sig = jax.nn.sigmoid(hf)
    lf = jnp.log(lb + (1.0 - lb) * sig)
    k = (1.0 - lb) * jax.nn.sigmoid(-hf)
    q = jax.nn.silu(hq)
    b = _dg(((col <= row) & same_chunk).astype(F32), lf, 1, 0, HIGHEST)
    rid = lax.broadcasted_iota(jnp.int32, b.shape, 0)
    pick = lambda r: jnp.sum(jnp.where(rid == r, b, 0.0), axis=0, keepdims=True)
    b_last_c = [pick(i * CHUNK + CHUNK - 1) for i in range(n)]
    b_ref = _per_chunk_rows([pick(i * CHUNK + CHUNK // 2 - 1) for i in range(n)], rid)
    b_last = _per_chunk_rows(b_last_c, rid)
    qa = _blocks(q * jnp.exp(b - b_ref))
    ka = _blocks(k * jnp.exp(b_ref - b))
    qe = _blocks(q * jnp.exp(b))
    kd = _blocks(k * jnp.exp(b_last - b))
    decay = [_split_heads(jnp.exp(b_last_c[i])) for i in range(n)]
    v = _blocks(hi)
    chunks = range(n)
    attn = [[jnp.where(causal, _nt(qa[i][h], ka[i][h]), 0.0) for h in HEADS] for i in chunks]
    intra = [[_nn(attn[i][h], v[i][h]) for h in HEADS] for i in chunks]
    kv = [[_tn(v[i][h], kd[i][h]) for h in HEADS] for i in chunks]
    normed = []
    for i in chunks:
        inter = [_nt(qe[i][h], sts[h]) for h in HEADS]
        sts = tuple(decay[i][h] * sts[h] + kv[i][h] for h in HEADS)
        o = [intra[i][h] + inter[h] for h in HEADS]
        normed.append(_merge_heads(tuple(o[h] * lax.rsqrt(jnp.mean(o[h] * o[h], axis=-1, keepdims=True) + LN_EPS)
                                         for h in HEADS)))
    return _merge_chunks(tuple(normed)) * nw * jax.nn.silu(hgate), sts


def _seg(ref, seg):
    return ref[:, seg * D_GROUP:(seg + 1) * D_GROUP]


def _hgrn2_fwd(proj, logits, norm_w):
    s = proj.shape[0]
    rows = CHUNKS_PER_STEP * CHUNK
    nc = s // rows

    def body(p_ref, lg_ref, nw_ref, y_ref, st_out_ref, st_scr):
        @pl.when(pl.program_id(0) == 0)
        def _():
            st_scr[...] = jnp.zeros_like(st_scr)

        sts = tuple(st_scr[h] for h in HEADS)
        y, sts_new = _hg_chunk(_seg(p_ref, 0), _seg(p_ref, 1), _seg(p_ref, 2), _seg(p_ref, 3),
                               lg_ref[0:1, :], lg_ref[1:2, :], nw_ref[...], sts)
        y_ref[...] = y.astype(y_ref.dtype)
        for h in HEADS:
            st_out_ref[h] = sts[h]
            st_scr[h] = sts_new[h]

    return pl.pallas_call(
        body, name="hgrn2_fwd", grid=(nc,),
        in_specs=[pl.BlockSpec((rows, 4 * D_GROUP), lambda c: (c, 0)),
                  pl.BlockSpec((2, D_GROUP), lambda c: (0, 0)),
                  pl.BlockSpec((1, D_GROUP), lambda c: (0, 0))],
        out_specs=[pl.BlockSpec((rows, D_GROUP), lambda c: (c, 0)),
                   pl.BlockSpec((None, N_HEADS, D_HEAD, D_HEAD), lambda c: (c, 0, 0, 0))],
        out_shape=[jax.ShapeDtypeStruct((s, 2 * D_GROUP), BF16),
                   jax.ShapeDtypeStruct((nc, N_HEADS, D_HEAD, D_HEAD), F32)],
        scratch_shapes=[pltpu.VMEM((N_HEADS, D_HEAD, D_HEAD), F32)],
        compiler_params=_params(("arbitrary",)),
    )(proj, logits, norm_w)


def _hgrn2_bwd(proj, logits, norm_w, states, d_y):
    s = proj.shape[0]
    rows = CHUNKS_PER_STEP * CHUNK
    nc = s // rows

    def body(p_ref, lg_ref, nw_ref, st_ref, dy_ref, dp_ref, dl_ref, dnw_ref, dsum_ref, dst_scr):
        @pl.when(pl.program_id(0) == 0)
        def _():
            dst_scr[...] = jnp.zeros_like(dst_scr)
            dl_ref[...] = jnp.zeros_like(dl_ref)
            dnw_ref[...] = jnp.zeros_like(dnw_ref)
            dsum_ref[...] = jnp.zeros_like(dsum_ref)

        _, vjp = jax.vjp(_hg_chunk, _seg(p_ref, 0), _seg(p_ref, 1), _seg(p_ref, 2), _seg(p_ref, 3),
                         lg_ref[0:1, :], lg_ref[1:2, :], nw_ref[...], tuple(st_ref[h] for h in HEADS))
        d_hq, d_hf, d_hi, d_hg, d_l0, d_l1, d_nw, d_sts = vjp((dy_ref[...], tuple(dst_scr[h] for h in HEADS)))
        for seg, val in enumerate((d_hq, d_hf, d_hi, d_hg)):
            dp_ref[:, seg * D_GROUP:(seg + 1) * D_GROUP] = val.astype(dp_ref.dtype)
            dsum_ref[:, seg * D_GROUP:(seg + 1) * D_GROUP] += jnp.sum(val, axis=0, keepdims=True)
        dl_ref[0:1, :] += d_l0
        dl_ref[1:2, :] += d_l1
        dnw_ref[...] += d_nw
        for h in HEADS:
            dst_scr[h] = d_sts[h]

    rev = lambda c: nc - 1 - c
    return pl.pallas_call(
        body, name="hgrn2_bwd", grid=(nc,),
        in_specs=[pl.BlockSpec((rows, 4 * D_GROUP), lambda c: (rev(c), 0)),
                  pl.BlockSpec((2, D_GROUP), lambda c: (0, 0)),
                  pl.BlockSpec((1, D_GROUP), lambda c: (0, 0)),
                  pl.BlockSpec((None, N_HEADS, D_HEAD, D_HEAD), lambda c: (rev(c), 0, 0, 0)),
                  pl.BlockSpec((rows, D_GROUP), lambda c: (rev(c), 0))],
        out_specs=[pl.BlockSpec((rows, 4 * D_GROUP), lambda c: (rev(c), 0)),
                   pl.BlockSpec((2, D_GROUP), lambda c: (0, 0)),
                   pl.BlockSpec((1, D_GROUP), lambda c: (0, 0)),
                   pl.BlockSpec((1, 4 * D_GROUP), lambda c: (0, 0))],
        out_shape=[jax.ShapeDtypeStruct((s, D_IN_MAIN), BF16), jax.ShapeDtypeStruct((2, D_GROUP), F32),
                   jax.ShapeDtypeStruct((1, D_GROUP), F32), jax.ShapeDtypeStruct((1, 4 * D_GROUP), F32)],
        scratch_shapes=[pltpu.VMEM((N_HEADS, D_HEAD, D_HEAD), F32)],
        compiler_params=_params(("arbitrary",)),
    )(proj, logits, norm_w, states, d_y)


def _gate_column(gates, lane, idx):
    return jnp.sum(jnp.where(lane == idx, gates, 0.0), axis=1, keepdims=True)


def _head_layer_norm(h):
    mu = jnp.mean(h, axis=-1, keepdims=True)
    var = jnp.mean(jnp.square(h - mu), axis=-1, keepdims=True)
    return (h - mu) * lax.rsqrt(var + LN_EPS)


def _ml_chunk(qc, kc, v, mo, gates, nw, cts, ns, ms):
    n = qc.shape[0] // CHUNK
    row, col = _chunk_masks(CHUNK)
    mask = col <= row
    eye = col == row
    to_row = lambda t: jnp.sum(jnp.where(eye, t, 0.0), axis=0, keepdims=True)
    q = _blocks(qc * (D_HEAD ** -0.5))
    k = _blocks(kc)
    vs = _blocks(v)
    gate_rows = _split_chunks(gates)
    lane = lax.broadcasted_iota(jnp.int32, gate_rows[0].shape, 1)
    each = [(i, h) for i in range(n) for h in HEADS]
    on_each = lambda f: {ih: f(*ih) for ih in each}
    ig = on_each(lambda i, h: _gate_column(gate_rows[i], lane, h))
    lf = on_each(lambda i, h: jax.nn.log_sigmoid(_gate_column(gate_rows[i], lane, N_HEADS + h)))
    lf_row = on_each(lambda i, h: to_row(lf[i, h]))
    ig_row = on_each(lambda i, h: to_row(ig[i, h]))
    b_col = on_each(lambda i, h: jnp.sum(jnp.where(mask, lf_row[i, h], 0.0), axis=1, keepdims=True))
    b_row = on_each(lambda i, h: jnp.sum(jnp.where(row <= col, lf[i, h], 0.0), axis=0, keepdims=True))
    g = on_each(lambda i, h: jnp.sum(lf[i, h], axis=0, keepdims=True))
    d = on_each(lambda i, h: jnp.where(mask, b_col[i, h] - b_row[i, h] + ig_row[i, h], -jnp.inf))
    a = on_each(lambda i, h: g[i, h] - b_col[i, h] + ig[i, h])
    m_at = {(0, h): ms[h] for h in HEADS}
    for i, h in each:
        m_at[i + 1, h] = lax.stop_gradient(jnp.maximum(g[i, h] + m_at[i, h], jnp.max(a[i, h], axis=0, keepdims=True)))
    inter = on_each(lambda i, h: b_col[i, h] + m_at[i, h])
    m_t = on_each(lambda i, h: lax.stop_gradient(jnp.maximum(inter[i, h], jnp.max(d[i, h], axis=1, keepdims=True))))
    qk = on_each(lambda i, h: _nt(q[i][h], k[i][h]))
    sc = on_each(lambda i, h: qk[i, h] * jnp.exp(d[i, h] - m_t[i, h]))
    w_inter = on_each(lambda i, h: jnp.exp(inter[i, h] - m_t[i, h]))
    sv = on_each(lambda i, h: _nn(sc[i, h], vs[i][h]))
    decay = on_each(lambda i, h: jnp.exp(g[i, h] + m_at[i, h] - m_at[i + 1, h]))
    wk = on_each(lambda i, h: k[i][h] * jnp.exp(a[i, h] - m_at[i + 1, h]))
    kv = on_each(lambda i, h: _tn(vs[i][h], wk[i, h]))
    normed = []
    for i in range(n):
        qc_state = [_nt(q[i][h], cts[h]) for h in HEADS]
        num = [sv[i, h] + w_inter[i, h] * qc_state[h] for h in HEADS]
        den = [jnp.sum(sc[i, h], axis=1, keepdims=True)
               + w_inter[i, h] * jnp.sum(q[i][h] * ns[h], axis=1, keepdims=True) for h in HEADS]
        hh = [num[h] / jnp.maximum(jnp.abs(den[h]), jnp.exp(-m_t[i, h])) for h in HEADS]
        cts = tuple(decay[i, h] * cts[h] + kv[i, h] for h in HEADS)
        ns = tuple(decay[i, h] * ns[h] + jnp.sum(wk[i, h], axis=0, keepdims=True) for h in HEADS)
        normed.append(_merge_heads(tuple(_head_layer_norm(hh[h]) for h in HEADS)))
    y = jax.nn.sigmoid(mo) * (_merge_chunks(tuple(normed)) * nw)
    return y, cts, ns, tuple(m_at[n, h] for h in HEADS)


def _mlstm_fwd(qk, proj, gates, norm_w, y):
    s = proj.shape[0]
    rows = ML_CHUNKS_PER_STEP * CHUNK
    nc = s // rows

    def body(qk_ref, vo_ref, g_ref, nw_ref, _, y_ref, ct_out, n_out, m_out, ct_scr, n_scr, m_scr):
        @pl.when(pl.program_id(0) == 0)
        def _():
            ct_scr[...] = jnp.zeros_like(ct_scr)
            n_scr[...] = jnp.zeros_like(n_scr)
            m_scr[...] = jnp.full(m_scr.shape, NEG_BIG, F32)

        cts = tuple(ct_scr[h] for h in HEADS)
        ns = tuple(n_scr[h] for h in HEADS)
        ms = tuple(m_scr[h] for h in HEADS)
        y, cts_new, ns_new, ms_new = _ml_chunk(_seg(qk_ref, 0), _seg(qk_ref, 1), _seg(vo_ref, 0), _seg(vo_ref, 1),
                                               g_ref[...], nw_ref[...], cts, ns, ms)
        y_ref[...] = y.astype(y_ref.dtype)
        for h in HEADS:
            ct_out[h], n_out[h], m_out[h] = cts[h], ns[h], ms[h]
            ct_scr[h], n_scr[h], m_scr[h] = cts_new[h], ns_new[h], ms_new[h]

    st = lambda r, w: pl.BlockSpec((None, N_HEADS, r, w), lambda c: (c, 0, 0, 0))
    return pl.pallas_call(
        body, name="mlstm_fwd", grid=(nc,),
        in_specs=[pl.BlockSpec((rows, 2 * D_GROUP), lambda c: (c, 0)),
                  pl.BlockSpec((rows, 2 * D_GROUP), lambda c: (c, 3)),
                  pl.BlockSpec((rows, LANES), lambda c: (c, 0)),
                  pl.BlockSpec((1, D_GROUP), lambda c: (0, 0)),
                  pl.BlockSpec(memory_space=pl.ANY)],
        out_specs=[pl.BlockSpec((rows, D_GROUP), lambda c: (c, 1)),
                   st(D_HEAD, D_HEAD), st(1, D_HEAD), st(1, 1)],
        out_shape=[jax.ShapeDtypeStruct(y.shape, y.dtype),
                   jax.ShapeDtypeStruct((nc, N_HEADS, D_HEAD, D_HEAD), F32),
                   jax.ShapeDtypeStruct((nc, N_HEADS, 1, D_HEAD), F32),
                   jax.ShapeDtypeStruct((nc, N_HEADS, 1, 1), F32)],
        input_output_aliases={4: 0},
        scratch_shapes=[pltpu.VMEM((N_HEADS, D_HEAD, D_HEAD), F32), pltpu.VMEM((N_HEADS, 1, D_HEAD), F32),
                        pltpu.VMEM((N_HEADS, 1, 1), F32)],
        compiler_params=_params(("arbitrary",)),
    )(qk, proj, gates, norm_w, y)


def _mlstm_bwd(qk, proj, gates, norm_w, ct_s, n_s, m_s, d_y, d_proj):
    s = proj.shape[0]
    rows = ML_CHUNKS_PER_STEP * CHUNK
    nc = s // rows

    def body(qk_ref, vo_ref, g_ref, nw_ref, ct_ref, n_ref, m_ref, dy_ref, _,
             dp_ref, dqk_ref, dg_ref, dnw_ref, dsum_ref, dct_scr, dn_scr):
        @pl.when(pl.program_id(0) == 0)
        def _():
            dct_scr[...] = jnp.zeros_like(dct_scr)
            dn_scr[...] = jnp.zeros_like(dn_scr)
            dnw_ref[...] = jnp.zeros_like(dnw_ref)
            dsum_ref[...] = jnp.zeros_like(dsum_ref)

        ms = tuple(m_ref[h] for h in HEADS)
        step = lambda *a: _ml_chunk(*a, ms)[:3]
        _, vjp = jax.vjp(step, _seg(qk_ref, 0), _seg(qk_ref, 1), _seg(vo_ref, 0), _seg(vo_ref, 1), g_ref[...],
                         nw_ref[...], tuple(ct_ref[h] for h in HEADS), tuple(n_ref[h] for h in HEADS))
        d_q, d_k, d_v, d_o, d_gates, d_nw, d_cts, d_ns = vjp(
            (dy_ref[...], tuple(dct_scr[h] for h in HEADS), tuple(dn_scr[h] for h in HEADS)))
        dqk_ref[:, 0:D_GROUP] = d_q
        dqk_ref[:, D_GROUP:2 * D_GROUP] = d_k
        for seg, val in enumerate((d_v, d_o)):
            dp_ref[:, seg * D_GROUP:(seg + 1) * D_GROUP] = val.astype(dp_ref.dtype)
            dsum_ref[:, seg * D_GROUP:(seg + 1) * D_GROUP] += jnp.sum(val, axis=0, keepdims=True)
        dg_ref[...] = d_gates
        dnw_ref[...] += d_nw
        for h in HEADS:
            dct_scr[h] = d_cts[h]
            dn_scr[h] = d_ns[h]

    rev = lambda c: nc - 1 - c
    st = lambda r, w: pl.BlockSpec((None, N_HEADS, r, w), lambda c: (rev(c), 0, 0, 0))
    return pl.pallas_call(
        body, name="mlstm_bwd", grid=(nc,),
        in_specs=[pl.BlockSpec((rows, 2 * D_GROUP), lambda c: (rev(c), 0)),
                  pl.BlockSpec((rows, 2 * D_GROUP), lambda c: (rev(c), 3)),
                  pl.BlockSpec((rows, LANES), lambda c: (rev(c), 0)),
                  pl.BlockSpec((1, D_GROUP), lambda c: (0, 0)),
                  st(D_HEAD, D_HEAD), st(1, D_HEAD), st(1, 1),
                  pl.BlockSpec((rows, D_GROUP), lambda c: (rev(c), 1)),
                  pl.BlockSpec(memory_space=pl.ANY)],
        out_specs=[pl.BlockSpec((rows, 2 * D_GROUP), lambda c: (rev(c), 3)),
                   pl.BlockSpec((rows, 2 * D_GROUP), lambda c: (rev(c), 0)),
                   pl.BlockSpec((rows, LANES), lambda c: (rev(c), 0)),
                   pl.BlockSpec((1, D_GROUP), lambda c: (0, 0)),
                   pl.BlockSpec((1, 2 * D_GROUP), lambda c: (0, 0))],
        out_shape=[jax.ShapeDtypeStruct(d_proj.shape, d_proj.dtype), jax.ShapeDtypeStruct((s, 2 * D_GROUP), F32),
                   jax.ShapeDtypeStruct((s, LANES), F32), jax.ShapeDtypeStruct((1, D_GROUP), F32),
                   jax.ShapeDtypeStruct((1, 2 * D_GROUP), F32)],
        input_output_aliases={8: 0},
        scratch_shapes=[pltpu.VMEM((N_HEADS, D_HEAD, D_HEAD), F32), pltpu.VMEM((N_HEADS, 1, D_HEAD), F32)],
        compiler_params=_params(("arbitrary",)),
    )(qk, proj, gates, norm_w, ct_s, n_s, m_s, d_y, d_proj)


LN_TOKENS = 512
ATT_TOKENS = 512


def _proj_res_ln(a, w, xres, g, b, name):
    s, dm = xres.shape
    k = a.shape[1]
    tb = min(LN_TOKENS, s)

    def body(a_ref, w_ref, x_ref, g_ref, b_ref, z_ref, o_ref):
        z = ALPHA * x_ref[...] + _nn_raw(a_ref[...], w_ref[...])
        z_ref[...] = z
        o_ref[...] = _layer_norm(z, g_ref[...], b_ref[...])

    tok = pl.BlockSpec((tb, dm), lambda i: (i, 0))
    vec = pl.BlockSpec((1, dm), lambda i: (0, 0))
    act = jax.ShapeDtypeStruct((s, dm), F32)
    return pl.pallas_call(
        body, name=name, grid=(s // tb,),
        in_specs=[pl.BlockSpec((tb, k), lambda i: (i, 0)), pl.BlockSpec((k, dm), lambda i: (0, 0)), tok, vec, vec],
        out_specs=[tok, tok], out_shape=[act, act], compiler_params=_params(("parallel",)),
    )(a, w, xres, g, b)


def _ln_bwd_proj(d_out, z, g, b, w, name):
    s, dm = z.shape
    k = w.shape[0]
    tb = min(LN_TOKENS, s)

    def body(do_ref, z_ref, g_ref, b_ref, w_ref, dz_ref, da_ref, dg_ref, db_ref):
        @pl.when(pl.program_id(0) == 0)
        def _():
            dg_ref[...] = jnp.zeros_like(dg_ref)
            db_ref[...] = jnp.zeros_like(db_ref)

        _, vjp = jax.vjp(_layer_norm, z_ref[...], g_ref[...], b_ref[...])
        d_z, d_g, d_b = vjp(do_ref[...])
        dz_ref[...] = d_z
        da_ref[...] = _nt_raw(d_z, w_ref[...])
        dg_ref[...] += d_g
        db_ref[...] += d_b

    tok = pl.BlockSpec((tb, dm), lambda i: (i, 0))
    vec = pl.BlockSpec((1, dm), lambda i: (0, 0))
    return pl.pallas_call(
        body, name=name, grid=(s // tb,),
        in_specs=[tok, tok, vec, vec, pl.BlockSpec((k, dm), lambda i: (0, 0))],
        out_specs=[tok, pl.BlockSpec((tb, k), lambda i: (i, 0)), vec, vec],
        out_shape=[jax.ShapeDtypeStruct((s, dm), F32), jax.ShapeDtypeStruct((s, k), F32),
                   jax.ShapeDtypeStruct((1, dm), F32), jax.ShapeDtypeStruct((1, dm), F32)],
        compiler_params=_params(("arbitrary",)),
    )(d_out, z, g, b, w)


def _proj_loss_tail(a, w, xres, g, b, target):
    s, dm = xres.shape
    k = a.shape[1]
    tb = min(ATT_TOKENS, s)

    def loss_fn(z, gg, bb, tgt):
        err = jnp.square(_layer_norm(z, gg, bb) - tgt)
        return 0.5 * jnp.sum(jnp.mean(err, axis=-1, keepdims=True), axis=0, keepdims=True)

    def body(a_ref, w_ref, x_ref, g_ref, b_ref, t_ref, loss_ref, dz_ref, dg_ref, db_ref):
        @pl.when(pl.program_id(0) == 0)
        def _():
            loss_ref[...] = jnp.zeros_like(loss_ref)
            dg_ref[...] = jnp.zeros_like(dg_ref)
            db_ref[...] = jnp.zeros_like(db_ref)

        z = ALPHA * x_ref[...] + _nn_raw(a_ref[...], w_ref[...])
        tgt = t_ref[...]
        loss, vjp = jax.vjp(lambda zz, gg, bb: loss_fn(zz, gg, bb, tgt), z, g_ref[...], b_ref[...])
        d_z, d_g, d_b = vjp(jnp.ones((1, 1), F32))
        loss_ref[...] += loss
        dz_ref[...] = d_z
        dg_ref[...] += d_g
        db_ref[...] += d_b

    tok = pl.BlockSpec((tb, dm), lambda i: (i, 0))
    vec = pl.BlockSpec((1, dm), lambda i: (0, 0))
    one = pl.BlockSpec((1, 1), lambda i: (0, 0))
    return pl.pallas_call(
        body, name="ffn_down_loss_tail", grid=(s // tb,),
        in_specs=[pl.BlockSpec((tb, k), lambda i: (i, 0)), pl.BlockSpec((k, dm), lambda i: (0, 0)), tok, vec, vec, tok],
        out_specs=[one, tok, vec, vec],
        out_shape=[jax.ShapeDtypeStruct((1, 1), F32), jax.ShapeDtypeStruct((s, dm), F32),
                   jax.ShapeDtypeStruct((1, dm), F32), jax.ShapeDtypeStruct((1, dm), F32)],
        compiler_params=_params(("arbitrary",)),
    )(a, w, xres, g, b, target)


def _att_heads(qs, ks, vs):
    sc = [_nt(q, k) * (CA_DH ** -0.5) for q, k in zip(qs, ks)]
    p = [jax.nn.softmax(s, axis=-1) for s in sc]
    return tuple(_nn(pp, v) for pp, v in zip(p, vs))


def _head_slices(ref_or_value, offset):
    return tuple(ref_or_value[:, offset + h * CA_DH:offset + (h + 1) * CA_DH] for h in range(CA_HEADS))


def _cross_attention_fwd(x1, kv, wq, wo, g, b):
    s = x1.shape[0]
    tb = min(ATT_TOKENS, s)

    def body(x_ref, kv_ref, wq_ref, wo_ref, g_ref, b_ref, att_ref, z_ref, o_ref):
        x_blk = x_ref[...]
        q = _nn_raw(x_blk, wq_ref[...])
        att = jnp.concatenate(_att_heads(_head_slices(q, 0), _head_slices(kv_ref, 0), _head_slices(kv_ref, D_MODEL)),
                              axis=1)
        att_ref[...] = att.astype(att_ref.dtype)
        z = ALPHA * x_blk + _nn_raw(att, wo_ref[...])
        z_ref[...] = z
        o_ref[...] = _layer_norm(z, g_ref[...], b_ref[...])

    tok = pl.BlockSpec((tb, D_MODEL), lambda i: (i, 0))
    mat = pl.BlockSpec((D_MODEL, D_MODEL), lambda i: (0, 0))
    vec = pl.BlockSpec((1, D_MODEL), lambda i: (0, 0))
    act = jax.ShapeDtypeStruct((s, D_MODEL), F32)
    return pl.pallas_call(
        body, name="cross_attention_fwd", grid=(s // tb,),
        in_specs=[tok, pl.BlockSpec((N_MEM, 2 * D_MODEL), lambda i: (0, 0)), mat, mat, vec, vec],
        out_specs=[tok, tok, tok],
        out_shape=[jax.ShapeDtypeStruct((s, D_MODEL), BF16), act, act],
        compiler_params=_params(("parallel",)),
    )(x1, kv, wq, wo, g, b)


def _cross_attention_bwd(d_x2, x1, z2, kv, wq, wo, g, b):
    s = x1.shape[0]
    tb = min(ATT_TOKENS, s)

    def body(dx2_ref, x_ref, z_ref, kv_ref, wq_ref, wo_ref, g_ref, b_ref,
             dx1_ref, dq_ref, dz_ref, dkv_ref, dg_ref, db_ref):
        @pl.when(pl.program_id(0) == 0)
        def _():
            dkv_ref[...] = jnp.zeros_like(dkv_ref)
            dg_ref[...] = jnp.zeros_like(dg_ref)
            db_ref[...] = jnp.zeros_like(db_ref)

        _, ln_vjp = jax.vjp(_layer_norm, z_ref[...], g_ref[...], b_ref[...])
        d_z, d_g, d_b = ln_vjp(dx2_ref[...])
        dg_ref[...] += d_g
        db_ref[...] += d_b
        dz_ref[...] = d_z.astype(dz_ref.dtype)
        d_att = _nt_raw(d_z, wo_ref[...])
        q = _nn_raw(x_ref[...], wq_ref[...])
        _, vjp = jax.vjp(_att_heads, _head_slices(q, 0), _head_slices(kv_ref, 0), _head_slices(kv_ref, D_MODEL))
        d_qs, d_ks, d_vs = vjp(_head_slices(d_att, 0))
        for h in range(CA_HEADS):
            lo = h * CA_DH
            dkv_ref[:, lo:lo + CA_DH] += d_ks[h]
            dkv_ref[:, D_MODEL + lo:D_MODEL + lo + CA_DH] += d_vs[h]
        d_q = jnp.concatenate(d_qs, axis=1)
        dq_ref[...] = d_q.astype(dq_ref.dtype)
        dx1_ref[...] = ALPHA * d_z + _nt_raw(d_q, wq_ref[...])

    tok = pl.BlockSpec((tb, D_MODEL), lambda i: (i, 0))
    mem = pl.BlockSpec((N_MEM, 2 * D_MODEL), lambda i: (0, 0))
    mat = pl.BlockSpec((D_MODEL, D_MODEL), lambda i: (0, 0))
    vec = pl.BlockSpec((1, D_MODEL), lambda i: (0, 0))
    low = jax.ShapeDtypeStruct((s, D_MODEL), BF16)
    return pl.pallas_call(
        body, name="cross_attention_bwd", grid=(s // tb,),
        in_specs=[tok, tok, tok, mem, mat, mat, vec, vec], out_specs=[tok, tok, tok, mem, vec, vec],
        out_shape=[jax.ShapeDtypeStruct((s, D_MODEL), F32), low, low,
                   jax.ShapeDtypeStruct((N_MEM, 2 * D_MODEL), F32),
                   jax.ShapeDtypeStruct((1, D_MODEL), F32), jax.ShapeDtypeStruct((1, D_MODEL), F32)],
        compiler_params=_params(("arbitrary",)),
    )(d_x2, x1, z2, kv, wq, wo, g, b)


def _local_step(x, mem, target, w, mid_weights=None, ffn_weights=None, down_weights=None, on_ffn_grads=None,
                on_mid_grads=None,
                on_small_grads=None, on_last_grads=None):
    w = dict(w)
    s = x.shape[0]
    tm = min(512, s)
    tt = min(512, s)
    proj = _matmul_nn(x, w["w_in_main"], w["b_in_main"], min(2048, s), 512, "proj")
    gates = _matmul_nn(x, w["w_in_gate"], w["b_in_gate"], tm, LANES, "proj_gates")
    qk = _ml_conv_fwd(proj, w["ml_conv_w"], w["ml_conv_b"])
    y, hg_states = _hgrn2_fwd(proj, w["hg_lb_logits"], w["hg_norm_w"])
    y, ct_s, n_s, m_s = _mlstm_fwd(qk, proj, gates, w["ml_norm_w"], y)
    if mid_weights is not None:
        w.update(mid_weights(y))
    z1, x1 = _proj_res_ln(y, w["w_out"], x, w["ln1_g"], w["ln1_b"], "out_proj_ln1")
    kv = _matmul_nn(mem, w["ca_wkv"], None, N_MEM, CA_DH, "kv")
    att, z2, x2 = _cross_attention_fwd(x1, kv, w["ca_wq"], w["ca_wo"], w["ln2_g"], w["ln2_b"])
    if ffn_weights is not None:
        w.update(ffn_weights(x2))
    u = _matmul_nn(x2, w["ffn_w_up"], None, min(2048, s), UP_SHARD_P, "ffn_up", BF16)
    hid = _ffn_conv_fwd(u, w["ffn_conv_w"], w["ffn_conv_b"])
    if down_weights is not None:
        w.update(down_weights(hid))
    loss, d_z3, d_ln3_g, d_ln3_b = _proj_loss_tail(hid, w["ffn_w_down"], x2, w["ln3_g"], w["ln3_b"], target)
    grads = {"ln3_g": d_ln3_g, "ln3_b": d_ln3_b}
    grads["ffn_w_down"] = _matmul_tn(hid, d_z3, 1536, D_MODEL, tt, "d_w_down")
    d_hid = _matmul_nt([(d_z3, w["ffn_w_down"])], None, 1.0, tm, D_FF_P, "d_hid", BF16)
    d_ug, d_uv, d_cwg, d_cwv, d_cbg, d_cbv = _ffn_conv_bwd(u, w["ffn_conv_w"], w["ffn_conv_b"], d_hid)
    grads["ffn_conv_w"] = jnp.concatenate([d_cwg, d_cwv], axis=-1)
    grads["ffn_conv_b"] = jnp.concatenate([d_cbg, d_cbv], axis=-1)
    half = N_DEV // 2
    d_w_up = _matmul_tn(x2, d_ug, D_MODEL, UP_SHARD_P, tt, "d_w_up_gate", shards=N_DEV, group=half)
    grads["ffn_w_up"] = _matmul_tn(x2, d_uv, D_MODEL, UP_SHARD_P, tt, "d_w_up_val", shards=N_DEV,
                                   shard0=half, group=half, into=d_w_up)
    d_x2 = _matmul_nt([(d_ug, w["ffn_w_up"], 0), (d_uv, w["ffn_w_up"], N_DEV // 2)], d_z3, ALPHA,
                      min(256, s), D_MODEL, "d_x2")
    if on_ffn_grads is not None:
        d_x2 = on_ffn_grads(grads, d_x2)
    d_x1, d_q, d_z2, d_kv, grads["ln2_g"], grads["ln2_b"] = _cross_attention_bwd(
        d_x2, x1, z2, kv, w["ca_wq"], w["ca_wo"], w["ln2_g"], w["ln2_b"])
    grads["ca_wo"] = _matmul_tn(att, d_z2, D_MODEL, D_MODEL, tt, "d_ca_wo")
    grads["ca_wq"] = _matmul_tn(x1, d_q, D_MODEL, D_MODEL, tt, "d_ca_wq")
    grads["ca_wkv"] = _matmul_tn(mem, d_kv, D_MODEL, CA_DH, N_MEM, "d_ca_wkv", shards=N_DEV, group=N_DEV)
    d_z1, d_y, grads["ln1_g"], grads["ln1_b"] = _ln_bwd_proj(d_x1, z1, w["ln1_g"], w["ln1_b"], w["w_out"],
                                                             "ln1_bwd_out_proj")
    grads["w_out"] = _matmul_tn(y, d_z1, D_MODEL, D_MODEL, tt, "d_w_out")
    if on_mid_grads is not None:
        d_y = on_mid_grads(grads, d_y)
    d_proj, grads["hg_lb_logits"], grads["hg_norm_w"], db_hg = _hgrn2_bwd(
        proj, w["hg_lb_logits"], w["hg_norm_w"], hg_states, d_y)
    d_proj, d_qk, d_gates, grads["ml_norm_w"], db_vo = _mlstm_bwd(
        qk, proj, gates, w["ml_norm_w"], ct_s, n_s, m_s, d_y, d_proj)
    d_proj, grads["ml_conv_w"], grads["ml_conv_b"], db_qk = _ml_conv_bwd(
        proj, w["ml_conv_w"], w["ml_conv_b"], d_qk, d_proj)
    grads["b_in_main"] = jnp.concatenate([db_hg, db_qk, db_vo], axis=-1)
    grads["w_in_gate"], grads["b_in_gate"] = _matmul_tn(x, d_gates, D_MODEL, LANES, tt, "d_w_in_gates", colsum=True)
    if on_small_grads is not None:
        d_proj = on_small_grads(grads, loss, d_proj)
    grads["w_in_main"] = _matmul_tn(x, d_proj, D_MODEL, min(2048, D_IN_MAIN), tt, "d_w_in")
    if on_last_grads is not None:
        d_z1 = on_last_grads(grads, d_z1)
    grad_x = _matmul_nt([(d_proj, w["w_in_main"]), (d_gates, w["w_in_gate"])], d_z1, ALPHA, tm, D_MODEL, "d_x")
    return loss, grad_x, grads


HBM_SPEC = pl.BlockSpec(memory_space=pltpu.HBM)


def _coords():
    return lax.axis_index("x"), lax.axis_index("y"), lax.axis_index("c")


def _other_chips(x, y):
    return [(1 - x, y), (x, 1 - y), (1 - x, 1 - y)]


def _all_gather_two_level(shards, name):
    na = len(shards)

    def body(*refs):
        x_refs, out_refs = refs[:na], refs[na:2 * na]
        send_sems, recv_sems, local_sems = refs[2 * na:]
        x, y, c = _coords()
        me, sibling = (x, y, c), (x, y, 1 - c)
        chips = _other_chips(x, y)

        def copy(a, k, block, to, own=False):
            slot = out_refs[a].at[4 * block[0] + 2 * block[1] + block[2]]
            return pltpu.make_async_remote_copy(
                src_ref=x_refs[a] if own else slot, dst_ref=slot,
                send_sem=send_sems.at[7 * a + k], recv_sem=recv_sems.at[7 * a + k],
                device_id=to, device_id_type=MESH)

        mine = [pltpu.make_async_copy(x_refs[a], out_refs[a].at[4 * x + 2 * y + c], local_sems.at[a])
                for a in range(na)]
        for cp in mine:
            cp.start()
        first = []
        for a in range(na):
            first.append(copy(a, 0, me, sibling, own=True))
            first += [copy(a, 1 + j, me, (*chip, c), own=True) for j, chip in enumerate(chips)]
        for cp in first:
            cp.start()
        passed = []
        for j, chip in enumerate(chips):
            for a in range(na):
                copy(a, 1 + j, (*chip, c), me).wait_recv()
                fwd = copy(a, 4 + j, (*chip, c), sibling)
                fwd.start()
                passed.append(fwd)
        for a in range(na):
            copy(a, 0, sibling, me).wait_recv()
            for j, chip in enumerate(chips):
                copy(a, 4 + j, (*chip, 1 - c), me).wait_recv()
        for cp in first + passed:
            cp.wait_send()
        for cp in mine:
            cp.wait()

    return pl.pallas_call(
        body, name=name,
        out_shape=[jax.ShapeDtypeStruct((N_DEV,) + t.shape, t.dtype) for t in shards],
        in_specs=[HBM_SPEC] * na, out_specs=[HBM_SPEC] * na,
        scratch_shapes=[pltpu.SemaphoreType.DMA((7 * na,)), pltpu.SemaphoreType.DMA((7 * na,)),
                        pltpu.SemaphoreType.DMA((na,))],
    )(*shards)


SEM_SPEC = pl.BlockSpec(memory_space=pltpu.SEMAPHORE)
ANY_SPEC = pl.BlockSpec(memory_space=pl.ANY)
SIDE_EFFECT = pltpu.SideEffectType.DATAFLOW_SIDE_EFFECTING


def _peer(x, y, c, d):
    flip = lambda v, bit: 1 - v if bit else v
    p = (flip(x, d & 4), flip(y, d & 2), flip(c, d & 1))
    return p, 4 * p[0] + 2 * p[1] + p[2]


def _direct_copies(gather, src_refs, land_refs, send_sems, recv_sems):
    x, y, c = _coords()
    me = 4 * x + 2 * y + c
    copies = []
    for a in range(len(src_refs)):
        for d in range(1, N_DEV):
            peer, peer_slot = _peer(x, y, c, d)
            copies.append(pltpu.make_async_remote_copy(
                src_ref=src_refs[a] if gather else src_refs[a].at[peer_slot],
                dst_ref=land_refs[a].at[me] if gather else land_refs[a].at[d - 1],
                send_sem=send_sems.at[7 * a + d - 1], recv_sem=recv_sems.at[7 * a + d - 1],
                device_id=peer, device_id_type=MESH))
    return copies


def _hbm(t):
    return pltpu.HBM(t.shape, t.dtype)


def _direct_start(gather, arrays, through, name):
    na = len(arrays)
    lands = [lax.empty((N_DEV,) + t.shape if gather else (N_DEV - 1,) + t.shape[1:], t.dtype) for t in arrays]
    n_io = 2 * na + 1

    def body(*refs):
        for cp in _direct_copies(gather, refs[:na], refs[na:2 * na], refs[n_io], refs[n_io + 1]):
            cp.start()

    ins = [pltpu.with_memory_space_constraint(t, pltpu.HBM) for t in (*arrays, *lands, through)]
    sems = pltpu.SemaphoreType.DMA((7 * na,))
    res = pl.pallas_call(
        body, name=name, out_shape=(sems, sems, *[_hbm(t) for t in ins]),
        in_specs=[HBM_SPEC] * n_io, out_specs=(SEM_SPEC, SEM_SPEC, *[HBM_SPEC] * n_io),
        input_output_aliases={i: 2 + i for i in range(n_io)},
        compiler_params=pltpu.CompilerParams(has_side_effects=SIDE_EFFECT),
    )(*ins)
    return (res[0], res[1], list(res[2:2 + na]), list(res[2 + na:2 + 2 * na])), res[2 + 2 * na]


def _direct_wait(gather, started, after, name):
    send_sems, recv_sems, arrays, lands = started
    na = len(arrays)
    after = list(after) if isinstance(after, (list, tuple)) else [after]

    def body(*refs):
        for cp in _direct_copies(gather, refs[:na], refs[na:2 * na], refs[2 * na], refs[2 * na + 1]):
            cp.wait_send()
            cp.wait_recv()

    res = pl.pallas_call(
        body, name=name, out_shape=tuple(_hbm(t) for t in (*arrays, *lands)),
        in_specs=[HBM_SPEC] * (2 * na) + [SEM_SPEC, SEM_SPEC] + [ANY_SPEC] * len(after),
        out_specs=tuple([HBM_SPEC] * (2 * na)), input_output_aliases={i: i for i in range(2 * na)},
        compiler_params=pltpu.CompilerParams(has_side_effects=SIDE_EFFECT),
    )(*arrays, *lands, send_sems, recv_sems, *after)
    return list(res[:na]), list(res[na:])


def _row_tile(rows):
    for t in (256, 176, 128):
        if rows % t == 0 and rows > t:
            return t
    return rows


def _adamw_math(g, w, m, v):
    m_new = ADAM_B1 * m + (1.0 - ADAM_B1) * g
    v_new = ADAM_B2 * v + (1.0 - ADAM_B2) * jnp.square(g)
    m_hat = m_new / (1.0 - ADAM_B1 ** ADAM_STEP)
    v_hat = v_new / (1.0 - ADAM_B2 ** ADAM_STEP)
    delta = -ADAM_LR * (m_hat / (jnp.sqrt(v_hat) + ADAM_EPS) + ADAM_WD * w)
    return delta, m_new, v_new


def _adamw_sharded(chip, sums, got, w, m, v, name):
    r, c = w.shape
    tr = _row_tile(r)
    n_got = got.shape[0]

    def body(chip_ref, s_ref, g_ref, w_ref, m_ref, v_ref, go_ref, d_ref, nm_ref, nv_ref):
        g = s_ref[...].astype(F32)
        for i in range(n_got):
            g = g + g_ref[i].astype(F32)
        delta, m_new, v_new = _adamw_math(g, w_ref[...], m_ref[...], v_ref[...])
        go_ref[...] = g
        d_ref[...] = delta
        nm_ref[...] = m_new
        nv_ref[...] = v_new

    blk = pl.BlockSpec((tr, c), lambda i, chip_ref: (i, 0))
    out = jax.ShapeDtypeStruct((r, c), F32)
    return pl.pallas_call(
        body, name=name,
        grid_spec=pltpu.PrefetchScalarGridSpec(
            num_scalar_prefetch=1, grid=(r // tr,),
            in_specs=[pl.BlockSpec((None, tr, c), lambda i, chip_ref: (chip_ref[0], i, 0)),
                      pl.BlockSpec((n_got, tr, c), lambda i, chip_ref: (0, i, 0)), blk, blk, blk],
            out_specs=[blk, blk, blk, blk]),
        out_shape=[out, out, out, out],
        compiler_params=_params(("parallel",)),
    )(chip, sums, got, w, m, v)


def _adamw_replicated(parts, w, m, v):
    p, r, c = parts.shape

    def body(p_ref, w_ref, m_ref, v_ref, g_ref, d_ref, nm_ref, nv_ref):
        g = p_ref[0]
        for i in range(1, p):
            g = g + p_ref[i]
        delta, m_new, v_new = _adamw_math(g, w_ref[...], m_ref[...], v_ref[...])
        g_ref[...] = g
        d_ref[...] = delta
        nm_ref[...] = m_new
        nv_ref[...] = v_new

    blk = pl.BlockSpec((r, c), lambda i: (0, 0))
    out = jax.ShapeDtypeStruct((r, c), F32)
    return pl.pallas_call(
        body, name="adamw_replicated", grid=(1,),
        in_specs=[pl.BlockSpec((p, r, c), lambda i: (0, 0, 0)), blk, blk, blk],
        out_specs=[blk, blk, blk, blk], out_shape=[out, out, out, out],
        compiler_params=_params(("arbitrary",)),
    )(parts, w, m, v)


SHARDED_NAMES = ("w_in", "ml_conv_w", "w_out", "ca_wq", "ca_wkv", "ca_wo", "ffn_w_up", "ffn_conv_w", "ffn_w_down")
SMALL_NAMES = ("b_in", "hg_lb_logits", "hg_norm_w", "ml_conv_b", "ml_norm_w", "ln1_g", "ln1_b",
               "ln2_g", "ln2_b", "ffn_conv_b", "ln3_g", "ln3_b")
WEIGHT_NAMES = ("w_in", "b_in", "hg_lb_logits", "hg_norm_w", "ml_conv_w", "ml_conv_b", "ml_norm_w", "w_out",
                "ln1_g", "ln1_b", "ca_wq", "ca_wkv", "ca_wo", "ln2_g", "ln2_b", "ffn_w_up", "ffn_conv_w",
                "ffn_conv_b", "ffn_w_down", "ln3_g", "ln3_b")
PAD_TO = {"ffn_w_up": UP_SHARD_P, "ffn_conv_w": UP_SHARD_P}
SMALL_ROWS = 24
SMALL_W = D_MODEL


def _shard_2d(name, block):
    t = block[0]
    if name in PAD_TO:
        t = jnp.pad(t, ((0, 0), (0, PAD_TO[name] - t.shape[1])))
    return t


def _shard_like(name, t, like):
    return t[:, :like.shape[2]][None]


def _pad_cols(t, width):
    return jnp.pad(t, ((0, 0), (0, width - t.shape[1])))


FIRST_NAMES = ("w_in", "ml_conv_w")
FFN_NAMES = ("ffn_w_up", "ffn_w_down", "ffn_conv_w")
MID_NAMES = ("ca_wo", "ca_wq", "ca_wkv", "w_out")


def _first_weights(g, small):
    w = dict(small)
    w_in = jnp.concatenate([g["w_in"][j] for j in range(N_DEV)], axis=1)
    w["w_in_main"] = w_in[:, :D_IN_MAIN]
    w["w_in_gate"] = _pad_cols(w_in[:, D_IN_MAIN:], LANES)
    w["b_in_main"] = small["b_in"][:, :D_IN_MAIN]
    w["b_in_gate"] = _pad_cols(small["b_in"][:, D_IN_MAIN:], LANES)
    w["ml_conv_w"] = jnp.transpose(g["ml_conv_w"], (1, 0, 2)).reshape(ML_CONV, 2 * D_GROUP)
    return w


def _mid_weights(g):
    w = {n: g[n].reshape(D_MODEL, D_MODEL) for n in ("w_out", "ca_wq", "ca_wo")}
    w["ca_wkv"] = g["ca_wkv"]
    return w


FFN_UP_NAMES = ("ffn_w_up", "ffn_conv_w")
FFN_DOWN_NAMES = ("ffn_w_down",)


def _ffn_up_weights(g, small):
    w = {"ffn_w_up": g["ffn_w_up"]}
    w["ffn_conv_w"] = jnp.transpose(g["ffn_conv_w"], (1, 0, 2)).reshape(FFN_CONV, D_UP_P)
    w["ffn_conv_b"] = _pad_cols(small["ffn_conv_b"].reshape(N_DEV, UP_SHARD), UP_SHARD_P).reshape(1, D_UP_P)
    return w


def _ffn_down_weights(g):
    down = g["ffn_w_down"].reshape(N_DEV // 2, UP_SHARD, D_MODEL)
    return {"ffn_w_down": jnp.pad(down, ((0, 0), (0, UP_SHARD_P - UP_SHARD), (0, 0))).reshape(D_FF_P, D_MODEL)}


def _whole_weights(g, small):
    return {**_first_weights(g, small), **_mid_weights(g), **_ffn_up_weights(g, small), **_ffn_down_weights(g)}


def _owner_stack(n, grads):
    if n == "w_in":
        w_in = jnp.concatenate([grads["w_in_main"], grads["w_in_gate"][:, :D_IN - D_IN_MAIN]], axis=1)
        return jnp.stack([w_in[:, j * W_IN_SHARD:(j + 1) * W_IN_SHARD] for j in range(N_DEV)])
    if n in ("w_out", "ca_wq", "ca_wo"):
        return grads[n].reshape(N_DEV, D_MODEL // N_DEV, D_MODEL)
    if n == "ffn_w_down":
        down = grads[n].reshape(N_DEV // 2, UP_SHARD_P, D_MODEL)[:, :UP_SHARD]
        return down.reshape(N_DEV, D_FF // N_DEV, D_MODEL)
    if n == "ml_conv_w":
        return jnp.transpose(grads[n].reshape(ML_CONV, N_DEV, LANES), (1, 0, 2))
    if n == "ffn_conv_w":
        return jnp.transpose(grads[n].reshape(FFN_CONV, N_DEV, UP_SHARD_P), (1, 0, 2))
    return grads[n]


def _owner_stacks(grads):
    return {n: _owner_stack(n, grads) for n in SHARDED_NAMES}


def _small_grads(grads):
    out = {n: grads[n] for n in SMALL_NAMES if n in grads}
    out["b_in"] = jnp.concatenate([grads["b_in_main"], grads["b_in_gate"][:, :D_IN - D_IN_MAIN]], axis=1)
    out["ffn_conv_b"] = grads["ffn_conv_b"].reshape(N_DEV, UP_SHARD_P)[:, :UP_SHARD].reshape(1, D_UP)
    return out


def _pack_small(p, extra=None):
    flat = [p[n].reshape(-1) for n in SMALL_NAMES]
    if extra is not None:
        flat.append(extra.reshape(-1))
    flat = jnp.concatenate(flat)
    return jnp.pad(flat, (0, SMALL_ROWS * SMALL_W - flat.shape[0])).reshape(SMALL_ROWS, SMALL_W)


def _unpack_small(slab, like):
    out = {}
    flat = slab.reshape(-1)
    o = 0
    for n in SMALL_NAMES:
        out[n] = flat[o:o + like[n].size].reshape(like[n].shape)
        o += like[n].size
    return out, flat[o]


def kernel(x, mem, w_in, b_in, hg_lb_logits, hg_norm_w, ml_conv_w, ml_conv_b, ml_norm_w, w_out, ln1_g, ln1_b, ca_wq, ca_wkv, ca_wo, ln2_g, ln2_b, ffn_w_up, ffn_conv_w, ffn_conv_b, ffn_w_down, ln3_g, ln3_b, loss_target, m_w_in, m_b_in, m_hg_lb_logits, m_hg_norm_w, m_ml_conv_w, m_ml_conv_b, m_ml_norm_w, m_w_out, m_ln1_g, m_ln1_b, m_ca_wq, m_ca_wkv, m_ca_wo, m_ln2_g, m_ln2_b, m_ffn_w_up, m_ffn_conv_w, m_ffn_conv_b, m_ffn_w_down, m_ln3_g, m_ln3_b, v_w_in, v_b_in, v_hg_lb_logits, v_hg_norm_w, v_ml_conv_w, v_ml_conv_b, v_ml_norm_w, v_w_out, v_ln1_g, v_ln1_b, v_ca_wq, v_ca_wkv, v_ca_wo, v_ln2_g, v_ln2_b, v_ffn_w_up, v_ffn_conv_w, v_ffn_conv_b, v_ffn_w_down, v_ln3_g, v_ln3_b):
    params = dict(w_in=w_in, b_in=b_in, hg_lb_logits=hg_lb_logits, hg_norm_w=hg_norm_w, ml_conv_w=ml_conv_w,
                  ml_conv_b=ml_conv_b, ml_norm_w=ml_norm_w, w_out=w_out, ln1_g=ln1_g, ln1_b=ln1_b, ca_wq=ca_wq,
                  ca_wkv=ca_wkv, ca_wo=ca_wo, ln2_g=ln2_g, ln2_b=ln2_b, ffn_w_up=ffn_w_up, ffn_conv_w=ffn_conv_w,
                  ffn_conv_b=ffn_conv_b, ffn_w_down=ffn_w_down, ln3_g=ln3_g, ln3_b=ln3_b)
    mom1 = dict(w_in=m_w_in, b_in=m_b_in, hg_lb_logits=m_hg_lb_logits, hg_norm_w=m_hg_norm_w,
                ml_conv_w=m_ml_conv_w, ml_conv_b=m_ml_conv_b, ml_norm_w=m_ml_norm_w, w_out=m_w_out, ln1_g=m_ln1_g,
                ln1_b=m_ln1_b, ca_wq=m_ca_wq, ca_wkv=m_ca_wkv, ca_wo=m_ca_wo, ln2_g=m_ln2_g, ln2_b=m_ln2_b,
                ffn_w_up=m_ffn_w_up, ffn_conv_w=m_ffn_conv_w, ffn_conv_b=m_ffn_conv_b, ffn_w_down=m_ffn_w_down,
                ln3_g=m_ln3_g, ln3_b=m_ln3_b)
    mom2 = dict(w_in=v_w_in, b_in=v_b_in, hg_lb_logits=v_hg_lb_logits, hg_norm_w=v_hg_norm_w,
                ml_conv_w=v_ml_conv_w, ml_conv_b=v_ml_conv_b, ml_norm_w=v_ml_norm_w, w_out=v_w_out, ln1_g=v_ln1_g,
                ln1_b=v_ln1_b, ca_wq=v_ca_wq, ca_wkv=v_ca_wkv, ca_wo=v_ca_wo, ln2_g=v_ln2_g, ln2_b=v_ln2_b,
                ffn_w_up=v_ffn_w_up, ffn_conv_w=v_ffn_conv_w, ffn_conv_b=v_ffn_conv_b, ffn_w_down=v_ffn_w_down,
                ln3_g=v_ln3_g, ln3_b=v_ln3_b)

    x_idx, y_idx, c_idx = _coords()
    as_index = lambda v: jnp.reshape(v, (1,)).astype(jnp.int32)
    me = as_index(4 * x_idx + 2 * y_idx + c_idx)
    small_params = {n: params[n] for n in SMALL_NAMES}

    shards = {n: _shard_2d(n, params[n]) for n in SHARDED_NAMES}
    to_send = lambda names: [shards[n] if "conv" in n else shards[n].astype(BF16) for n in names]
    first = dict(zip(FIRST_NAMES, _all_gather_two_level(to_send(FIRST_NAMES), "weights_gather_first")))
    mid_started, through = _direct_start(True, to_send(MID_NAMES), first["w_in"], "weights_gather_start_mid")
    ffn_started, through = _direct_start(True, to_send(FFN_UP_NAMES), through, "weights_gather_start_ffn_up")
    down_started, first["w_in"] = _direct_start(True, to_send(FFN_DOWN_NAMES), through,
                                                "weights_gather_start_ffn_down")

    def gathered_weights(names, started, after, tag):
        mine, lands = _direct_wait(True, started, after, "weights_gather_wait_" + tag)
        return {n: lax.dynamic_update_index_in_dim(land, own, me[0], 0) for n, own, land in zip(names, mine, lands)}

    started, own_stacks = {}, {}

    def start_group(names, tag):
        def hook(grads, through):
            own_stacks[tag] = [_owner_stack(n, grads).astype(BF16) for n in names]
            started[tag], through = _direct_start(False, own_stacks[tag], through, "grads_start_" + tag)
            return through
        return hook

    def start_small(grads, loss, through):
        started["small"], through = _direct_start(True, [_pack_small(_small_grads(grads), loss)], through,
                                                  "small_gather_start")
        return through

    loss, grad_x, grads = _local_step(
        x[0], mem[0], loss_target[0], _first_weights(first, small_params),
        lambda y: _mid_weights(gathered_weights(MID_NAMES, mid_started, y, "mid")),
        lambda x2: _ffn_up_weights(gathered_weights(FFN_UP_NAMES, ffn_started, x2, "ffn_up"), small_params),
        lambda hid: _ffn_down_weights(gathered_weights(FFN_DOWN_NAMES, down_started, hid, "ffn_down")),
        start_group(FFN_NAMES, "ffn"), start_group(MID_NAMES, "mid"), start_small, start_group(FIRST_NAMES, "last"))

    sharded_out = {}

    def update_group(names, tag, after):
        _, lands = _direct_wait(False, started[tag], after, "grads_wait_" + tag)
        for n, st, land in zip(names, own_stacks[tag], lands):
            res = _adamw_sharded(me, st, land, shards[n], _shard_2d(n, mom1[n]), _shard_2d(n, mom2[n]), "adamw_" + n)
            sharded_out[n] = [_shard_like(n, t, params[n]) for t in res]

    update_group(FFN_NAMES, "ffn", grad_x)
    update_group(MID_NAMES, "mid", grad_x)
    own_small, small_lands = _direct_wait(True, started["small"], grad_x, "small_gather_wait")
    small_parts = lax.dynamic_update_index_in_dim(small_lands[0], own_small[0], me[0], 0)
    small_res = _adamw_replicated(small_parts, _pack_small(params), _pack_small(mom1), _pack_small(mom2))
    small_out = [_unpack_small(slab, params) for slab in small_res]
    done = [t for n in FFN_NAMES + MID_NAMES for t in sharded_out[n]]
    done += [t for small, _ in small_out for t in small.values()]
    update_group(FIRST_NAMES, "last", done)

    outs = []
    for k, (small, _) in enumerate(small_out):
        outs.extend(sharded_out[n][k] if n in sharded_out else small[n] for n in WEIGHT_NAMES)
    return (small_out[0][1], grad_x[None], *outs)
```

```python
import functools
import math

import jax
import jax.numpy as jnp
from jax import lax
from jax.experimental import pallas as pl
from jax.experimental.pallas import tpu as pltpu

F32 = jnp.float32
BF16 = jnp.bfloat16
HIGHEST = lax.Precision.HIGHEST
MESH = pl.DeviceIdType.MESH

N_DEV = 8
D_MODEL = 1024
N_MEM = 256
N_HEADS = 4
D_HEAD = 128
D_GROUP = N_HEADS * D_HEAD
CHUNK = 64
ML_CONV = 4
FFN_CONV = 3
D_FF = 2816
D_UP = 2 * D_FF
CA_HEADS = 4
CA_DH = D_MODEL // CA_HEADS
LANES = 128
SUBLANES = 8
D_IN = 8 * D_GROUP + 2 * N_HEADS
D_IN_MAIN = 8 * D_GROUP
W_IN_SHARD = D_IN // N_DEV
UP_SHARD = D_UP // N_DEV
UP_SHARD_P = 768
D_UP_P = N_DEV * UP_SHARD_P
D_FF_P = D_UP_P // 2
ALPHA = 2.0 ** 0.25
LN_EPS = 1e-5
NEG_BIG = -1e30
ADAM_LR = 0.001
ADAM_B1 = 0.9
ADAM_B2 = 0.999
ADAM_EPS = 1e-08
ADAM_WD = 0.01
ADAM_STEP = 10
VMEM_LIMIT = 56 * 1024 * 1024

SEG_HQ, SEG_HF, SEG_HI, SEG_HG, SEG_MQ, SEG_MK, SEG_MV, SEG_MO = (4 * i for i in range(8))


def _params(sem):
    return pltpu.CompilerParams(dimension_semantics=sem, vmem_limit_bytes=VMEM_LIMIT)


def _dg(a, b, ca, cb, precision=None):
    return lax.dot_general(a, b, (((ca,), (cb,)), ((), ())), precision=precision,
                           preferred_element_type=F32)


def _nn_raw(a, b):
    return _dg(a.astype(BF16), b.astype(BF16), 1, 0)


def _nt_raw(a, b):
    return _dg(a.astype(BF16), b.astype(BF16), 1, 1)


def _tn_raw(a, b):
    return _dg(a.astype(BF16), b.astype(BF16), 0, 0)


@jax.custom_vjp
def _nn(a, b):
    return _nn_raw(a, b)


_nn.defvjp(lambda a, b: (_nn_raw(a, b), (a, b)),
           lambda res, g: (_nt_raw(g, res[1]), _tn_raw(res[0], g)))


@jax.custom_vjp
def _nt(a, b):
    return _nt_raw(a, b)


_nt.defvjp(lambda a, b: (_nt_raw(a, b), (a, b)),
           lambda res, g: (_nn_raw(g, res[1]), _tn_raw(g, res[0])))


@jax.custom_vjp
def _tn(a, b):
    return _tn_raw(a, b)


_tn.defvjp(lambda a, b: (_tn_raw(a, b), (a, b)),
           lambda res, g: (_nt_raw(res[1], g), _nn_raw(res[0], g)))


def _layer_norm(z, g, b):
    mu = jnp.mean(z, axis=-1, keepdims=True)
    var = jnp.mean(jnp.square(z - mu), axis=-1, keepdims=True)
    return (z - mu) * lax.rsqrt(var + LN_EPS) * g + b


def _matmul_nn(a, w, bias, tm, tn, name, out_dtype=F32):
    m, k = a.shape
    if w.ndim == 3:
        n = w.shape[0] * w.shape[2]
        assert tn == w.shape[2]
        w_spec = pl.BlockSpec((None, k, tn), lambda i, j: (j, 0, 0))
    else:
        n = w.shape[1]
        w_spec = pl.BlockSpec((k, tn), lambda i, j: (0, j))

    def body(*refs):
        a_ref, w_ref = refs[0], refs[1]
        o_ref = refs[-1]
        acc = _nn_raw(a_ref[...], w_ref[...])
        if bias is not None:
            acc = acc + refs[2][...]
        o_ref[...] = acc.astype(o_ref.dtype)

    in_specs = [pl.BlockSpec((tm, k), lambda i, j: (i, 0)), w_spec]
    args = [a, w]
    if bias is not None:
        in_specs.append(pl.BlockSpec((1, tn), lambda i, j: (0, j)))
        args.append(bias)
    return pl.pallas_call(
        body, name=name, grid=(m // tm, n // tn), in_specs=in_specs,
        out_specs=pl.BlockSpec((tm, tn), lambda i, j: (i, j)),
        out_shape=jax.ShapeDtypeStruct((m, n), out_dtype),
        compiler_params=_params(("parallel", "parallel")),
    )(*args)


def _matmul_nt(pairs, add, scale, tm, tk, name, out_dtype=F32):
    m = pairs[0][0].shape[0]
    k = pairs[0][1].shape[-2]
    groups = []
    in_specs, args = [], []
    for pair in pairs:
        d, w = pair[0], pair[1]
        in_specs.append(pl.BlockSpec((tm, d.shape[1]), lambda i, j: (i, 0)))
        if w.ndim == 3:
            g = d.shape[1] // w.shape[2]
            blk = pair[2] // g
            in_specs.append(pl.BlockSpec((g, tk, w.shape[2]), lambda i, j, blk=blk: (blk, j, 0)))
            groups.append((g, w.shape[2]))
        else:
            in_specs.append(pl.BlockSpec((tk, w.shape[1]), lambda i, j: (j, 0)))
            groups.append(None)
        args += [d, w]
    if add is not None:
        in_specs.append(pl.BlockSpec((tm, tk), lambda i, j: (i, j)))
        args.append(add)

    def body(*refs):
        o_ref = refs[-1]
        acc = None
        for p, grp in enumerate(groups):
            d_ref, w_ref = refs[2 * p], refs[2 * p + 1]
            if grp is None:
                terms = [_nt_raw(d_ref[...], w_ref[...])]
            else:
                terms = [_nt_raw(d_ref[:, g * grp[1]:(g + 1) * grp[1]], w_ref[g]) for g in range(grp[0])]
            for t in terms:
                acc = t if acc is None else acc + t
        if add is not None:
            acc = acc + scale * refs[2 * len(groups)][...]
        o_ref[...] = acc.astype(o_ref.dtype)

    return pl.pallas_call(
        body, name=name, grid=(m // tm, k // tk), in_specs=in_specs,
        out_specs=pl.BlockSpec((tm, tk), lambda i, j: (i, j)),
        out_shape=jax.ShapeDtypeStruct((m, k), out_dtype),
        compiler_params=_params(("parallel", "parallel")),
    )(*args)


def _matmul_tn(a, b, tm, tn, tt, name, shards=None, shard0=0, group=1, into=None, colsum=False):
    t, m = a.shape
    n = b.shape[1]
    assert not colsum or tm == m
    n_in = 2 + (into is not None)
    out_dtype = BF16
    per_step = 1 if shards is None else group
    width = per_step * tn

    def body(*refs):
        a_ref, b_ref = refs[0], refs[1]
        o_ref, acc_ref = refs[n_in], refs[-1]
        first = pl.program_id(2) == 0

        @pl.when(first)
        def _():
            acc_ref[...] = jnp.zeros_like(acc_ref)

        if shards is None:
            acc_ref[...] += _tn_raw(a_ref[...], b_ref[...])
        else:
            lhs = a_ref[...].astype(BF16)
            for g in range(per_step):
                acc_ref[g] += _tn_raw(lhs, b_ref[:, g * tn:(g + 1) * tn])

        @pl.when(pl.program_id(2) == t // tt - 1)
        def _():
            o_ref[...] = acc_ref[...].astype(o_ref.dtype)

        if colsum:
            s_ref = refs[n_in + 1]

            @pl.when(first)
            def _():
                s_ref[...] = jnp.zeros_like(s_ref)

            s_ref[...] += jnp.sum(b_ref[...], axis=0, keepdims=True)

    in_specs = [pl.BlockSpec((tt, tm), lambda i, j, kk: (kk, i)),
                pl.BlockSpec((tt, width), lambda i, j, kk: (kk, j))]
    args = [a, b]
    aliases = {}
    if into is not None:
        in_specs.append(pl.BlockSpec(memory_space=pl.ANY))
        args.append(into)
        aliases = {2: 0}
    if shards is None:
        out_specs = [pl.BlockSpec((tm, tn), lambda i, j, kk: (i, j))]
        out_shape = [jax.ShapeDtypeStruct((m, n), out_dtype)]
        acc = pltpu.VMEM((tm, tn), F32)
    else:
        out_specs = [pl.BlockSpec((per_step, tm, tn), lambda i, j, kk: (shard0 // per_step + j, i, 0))]
        out_shape = [jax.ShapeDtypeStruct((shards, m, tn), out_dtype)]
        acc = pltpu.VMEM((per_step, tm, tn), F32)
    if colsum:
        out_specs.append(pl.BlockSpec((1, tn), lambda i, j, kk: (0, j)))
        out_shape.append(jax.ShapeDtypeStruct((1, n), F32))
    res = pl.pallas_call(
        body, name=name, grid=(m // tm, n // width, t // tt), in_specs=in_specs, out_specs=out_specs,
        out_shape=out_shape, input_output_aliases=aliases, scratch_shapes=[acc],
        compiler_params=_params(("parallel", "parallel", "arbitrary")),
    )(*args)
    return res if colsum else res[0]


ROW_TILE = 64


def _stack(ref, start, rows):
    return ref[pl.ds(start, rows), :].astype(F32).reshape(rows // SUBLANES, SUBLANES, LANES)


def _vreg_rows(ref, n):
    return [jnp.broadcast_to(ref[j:j + 1, :], (SUBLANES, LANES))[None] for j in range(n)]


def _column_total(acc):
    return jnp.sum(acc, axis=0, keepdims=True)


def _conv_fwd_tile(pad_ref, taps_w, bias, r0, rows):
    taps = len(taps_w)
    acc = bias
    for j in range(taps):
        acc = acc + _stack(pad_ref, SUBLANES - (taps - 1 - j) + r0, rows) * taps_w[j]
    return acc


def _conv_grads_tile(pad_ref, dpad_ref, dx_ref, taps_w, dws, r0, rows):
    taps = len(taps_w)
    x_rows = _stack(pad_ref, SUBLANES + r0, rows)
    dx = None
    for j in range(taps):
        d_shifted = _stack(dpad_ref, r0 + (taps - 1 - j), rows)
        term = d_shifted * taps_w[j]
        dx = term if dx is None else dx + term
        dws[j] = dws[j] + jnp.sum(d_shifted * x_rows, axis=0)
    dx_ref[r0:r0 + rows, :] = dx.reshape(rows, LANES).astype(dx_ref.dtype)
    return jnp.sum(dx, axis=0)


def _ml_conv_fwd(proj, conv_w, conv_b):
    s = proj.shape[0]
    nblk = 2 * D_GROUP // LANES

    def body(x_ref, w_ref, b_ref, o_ref, pad_ref):
        pad_ref[0:SUBLANES, :] = jnp.zeros((SUBLANES, LANES), F32)
        pad_ref[SUBLANES:, :] = x_ref[...].astype(F32)
        taps_w, bias = _vreg_rows(w_ref, ML_CONV), _vreg_rows(b_ref, 1)[0]
        for r0 in range(0, s, ROW_TILE):
            rows = min(ROW_TILE, s - r0)
            o_ref[r0:r0 + rows, :] = jax.nn.silu(_conv_fwd_tile(pad_ref, taps_w, bias, r0, rows)).reshape(rows, LANES)

    return pl.pallas_call(
        body, name="ml_conv_fwd", grid=(nblk,),
        in_specs=[pl.BlockSpec((s, LANES), lambda j: (0, SEG_MQ + j)),
                  pl.BlockSpec((ML_CONV, LANES), lambda j: (0, j)),
                  pl.BlockSpec((1, LANES), lambda j: (0, j))],
        out_specs=pl.BlockSpec((s, LANES), lambda j: (0, j)),
        out_shape=jax.ShapeDtypeStruct((s, 2 * D_GROUP), F32),
        scratch_shapes=[pltpu.VMEM((s + SUBLANES, LANES), F32)],
        compiler_params=_params(("parallel",)),
    )(proj, conv_w, conv_b)


def _ml_conv_bwd(proj, conv_w, conv_b, d_qk, d_proj):
    s = proj.shape[0]
    nblk = 2 * D_GROUP // LANES

    def body(x_ref, w_ref, b_ref, dy_ref, _, dx_ref, dw_ref, db_ref, dxs_ref, pad_ref, dpad_ref):
        pad_ref[0:SUBLANES, :] = jnp.zeros((SUBLANES, LANES), F32)
        pad_ref[SUBLANES:, :] = x_ref[...].astype(F32)
        dpad_ref[s:, :] = jnp.zeros((SUBLANES, LANES), F32)
        taps_w, bias = _vreg_rows(w_ref, ML_CONV), _vreg_rows(b_ref, 1)[0]
        db = jnp.zeros((SUBLANES, LANES), F32)
        for r0 in range(0, s, ROW_TILE):
            rows = min(ROW_TILE, s - r0)
            pre = _conv_fwd_tile(pad_ref, taps_w, bias, r0, rows)
            _, vjp = jax.vjp(jax.nn.silu, pre)
            d_pre, = vjp(_stack(dy_ref, r0, rows))
            dpad_ref[r0:r0 + rows, :] = d_pre.reshape(rows, LANES)
            db = db + jnp.sum(d_pre, axis=0)
        db_ref[...] = _column_total(db)
        dws = [jnp.zeros((SUBLANES, LANES), F32) for _ in range(ML_CONV)]
        dx_sum = jnp.zeros((SUBLANES, LANES), F32)
        for r0 in range(0, s, ROW_TILE):
            dx_sum = dx_sum + _conv_grads_tile(pad_ref, dpad_ref, dx_ref, taps_w, dws, r0, min(ROW_TILE, s - r0))
        dxs_ref[...] = _column_total(dx_sum)
        for j in range(ML_CONV):
            dw_ref[j:j + 1, :] = _column_total(dws[j])

    return pl.pallas_call(
        body, name="ml_conv_bwd", grid=(nblk,),
        in_specs=[pl.BlockSpec((s, LANES), lambda j: (0, SEG_MQ + j)),
                  pl.BlockSpec((ML_CONV, LANES), lambda j: (0, j)),
                  pl.BlockSpec((1, LANES), lambda j: (0, j)),
                  pl.BlockSpec((s, LANES), lambda j: (0, j)),
                  pl.BlockSpec(memory_space=pl.ANY)],
        out_specs=[pl.BlockSpec((s, LANES), lambda j: (0, SEG_MQ + j)),
                   pl.BlockSpec((ML_CONV, LANES), lambda j: (0, j)),
                   pl.BlockSpec((1, LANES), lambda j: (0, j)),
                   pl.BlockSpec((1, LANES), lambda j: (0, j))],
        out_shape=[jax.ShapeDtypeStruct(d_proj.shape, d_proj.dtype),
                   jax.ShapeDtypeStruct((ML_CONV, 2 * D_GROUP), F32),
                   jax.ShapeDtypeStruct((1, 2 * D_GROUP), F32),
                   jax.ShapeDtypeStruct((1, 2 * D_GROUP), F32)],
        input_output_aliases={4: 0},
        scratch_shapes=[pltpu.VMEM((s + SUBLANES, LANES), F32), pltpu.VMEM((s + SUBLANES, LANES), F32)],
        compiler_params=_params(("parallel",)),
    )(proj, conv_w, conv_b, d_qk, d_proj)


def _gelu_mul(a, b):
    return jax.nn.gelu(a) * b


GELU_C = math.sqrt(2.0 / math.pi)
GELU_K = 0.044715


def _gelu_mul_grads(a, b, d):
    a2 = a * a
    t = jnp.tanh(GELU_C * (a + GELU_K * (a * a2)))
    cdf = 0.5 * (1.0 + t)
    slope = cdf + (0.5 * GELU_C) * a * (1.0 - t * t) * (1.0 + (3.0 * GELU_K) * a2)
    return d * b * slope, d * (a * cdf)


FFN_BLOCKS = D_FF_P // LANES


def _ffn_conv_fwd(u, conv_w, conv_b):
    s = u.shape[0]

    def body(g_ref, v_ref, wg_ref, wv_ref, bg_ref, bv_ref, o_ref, gpad_ref, vpad_ref):
        for pad_ref, x_ref in ((gpad_ref, g_ref), (vpad_ref, v_ref)):
            pad_ref[0:SUBLANES, :] = jnp.zeros((SUBLANES, LANES), F32)
            pad_ref[SUBLANES:, :] = x_ref[...].astype(F32)
        taps_g, bias_g = _vreg_rows(wg_ref, FFN_CONV), _vreg_rows(bg_ref, 1)[0]
        taps_v, bias_v = _vreg_rows(wv_ref, FFN_CONV), _vreg_rows(bv_ref, 1)[0]
        for r0 in range(0, s, ROW_TILE):
            rows = min(ROW_TILE, s - r0)
            ug = _conv_fwd_tile(gpad_ref, taps_g, bias_g, r0, rows)
            uv = _conv_fwd_tile(vpad_ref, taps_v, bias_v, r0, rows)
            o_ref[r0:r0 + rows, :] = _gelu_mul(ug, uv).reshape(rows, LANES).astype(o_ref.dtype)

    col = lambda off: (lambda j: (0, off + j))
    return pl.pallas_call(
        body, name="ffn_conv_fwd", grid=(FFN_BLOCKS,),
        in_specs=[pl.BlockSpec((s, LANES), col(0)), pl.BlockSpec((s, LANES), col(FFN_BLOCKS)),
                  pl.BlockSpec((FFN_CONV, LANES), col(0)), pl.BlockSpec((FFN_CONV, LANES), col(FFN_BLOCKS)),
                  pl.BlockSpec((1, LANES), col(0)), pl.BlockSpec((1, LANES), col(FFN_BLOCKS))],
        out_specs=pl.BlockSpec((s, LANES), col(0)),
        out_shape=jax.ShapeDtypeStruct((s, D_FF_P), BF16),
        scratch_shapes=[pltpu.VMEM((s + SUBLANES, LANES), F32), pltpu.VMEM((s + SUBLANES, LANES), F32)],
        compiler_params=_params(("parallel",)),
    )(u, u, conv_w, conv_w, conv_b, conv_b)


def _ffn_conv_bwd(u, conv_w, conv_b, d_h):
    s = u.shape[0]

    def body(g_ref, v_ref, wg_ref, wv_ref, bg_ref, bv_ref, dh_ref,
             dug_ref, duv_ref, dwg_ref, dwv_ref, dbg_ref, dbv_ref,
             gpad_ref, vpad_ref, dgpad_ref, dvpad_ref):
        for pad_ref, x_ref in ((gpad_ref, g_ref), (vpad_ref, v_ref)):
            pad_ref[0:SUBLANES, :] = jnp.zeros((SUBLANES, LANES), F32)
            pad_ref[SUBLANES:, :] = x_ref[...].astype(F32)
        dgpad_ref[s:, :] = jnp.zeros((SUBLANES, LANES), F32)
        dvpad_ref[s:, :] = jnp.zeros((SUBLANES, LANES), F32)
        taps_g, bias_g = _vreg_rows(wg_ref, FFN_CONV), _vreg_rows(bg_ref, 1)[0]
        taps_v, bias_v = _vreg_rows(wv_ref, FFN_CONV), _vreg_rows(bv_ref, 1)[0]
        dbg = jnp.zeros((SUBLANES, LANES), F32)
        dbv = jnp.zeros((SUBLANES, LANES), F32)
        for r0 in range(0, s, ROW_TILE):
            rows = min(ROW_TILE, s - r0)
            ug = _conv_fwd_tile(gpad_ref, taps_g, bias_g, r0, rows)
            uv = _conv_fwd_tile(vpad_ref, taps_v, bias_v, r0, rows)
            d_ug, d_uv = _gelu_mul_grads(ug, uv, _stack(dh_ref, r0, rows))
            dgpad_ref[r0:r0 + rows, :] = d_ug.reshape(rows, LANES)
            dvpad_ref[r0:r0 + rows, :] = d_uv.reshape(rows, LANES)
            dbg = dbg + jnp.sum(d_ug, axis=0)
            dbv = dbv + jnp.sum(d_uv, axis=0)
        dbg_ref[...] = _column_total(dbg)
        dbv_ref[...] = _column_total(dbv)
        for pad_ref, dpad_ref, taps_w, dx_ref, dw_ref in ((gpad_ref, dgpad_ref, taps_g, dug_ref, dwg_ref),
                                                          (vpad_ref, dvpad_ref, taps_v, duv_ref, dwv_ref)):
            dws = [jnp.zeros((SUBLANES, LANES), F32) for _ in range(FFN_CONV)]
            for r0 in range(0, s, ROW_TILE):
                _conv_grads_tile(pad_ref, dpad_ref, dx_ref, taps_w, dws, r0, min(ROW_TILE, s - r0))
            for j in range(FFN_CONV):
                dw_ref[j:j + 1, :] = _column_total(dws[j])

    col = lambda off: (lambda j: (0, off + j))
    seq = pl.BlockSpec((s, LANES), col(0))
    return pl.pallas_call(
        body, name="ffn_conv_bwd", grid=(FFN_BLOCKS,),
        in_specs=[pl.BlockSpec((s, LANES), col(0)), pl.BlockSpec((s, LANES), col(FFN_BLOCKS)),
                  pl.BlockSpec((FFN_CONV, LANES), col(0)), pl.BlockSpec((FFN_CONV, LANES), col(FFN_BLOCKS)),
                  pl.BlockSpec((1, LANES), col(0)), pl.BlockSpec((1, LANES), col(FFN_BLOCKS)), seq],
        out_specs=[seq, seq, pl.BlockSpec((FFN_CONV, LANES), col(0)), pl.BlockSpec((FFN_CONV, LANES), col(0)),
                   pl.BlockSpec((1, LANES), col(0)), pl.BlockSpec((1, LANES), col(0))],
        out_shape=[jax.ShapeDtypeStruct((s, D_FF_P), BF16), jax.ShapeDtypeStruct((s, D_FF_P), BF16),
                   jax.ShapeDtypeStruct((FFN_CONV, D_FF_P), F32), jax.ShapeDtypeStruct((FFN_CONV, D_FF_P), F32),
                   jax.ShapeDtypeStruct((1, D_FF_P), F32), jax.ShapeDtypeStruct((1, D_FF_P), F32)],
        scratch_shapes=[pltpu.VMEM((s + SUBLANES, LANES), F32) for _ in range(4)],
        compiler_params=_params(("parallel",)),
    )(u, u, conv_w, conv_w, conv_b, conv_b, d_h)


def _chunk_masks(c):
    row = lax.broadcasted_iota(jnp.int32, (c, c), 0)
    col = lax.broadcasted_iota(jnp.int32, (c, c), 1)
    return row, col


@jax.custom_vjp
def _split_heads(x):
    return tuple(x[:, h * D_HEAD:(h + 1) * D_HEAD] for h in range(N_HEADS))


_split_heads.defvjp(lambda x: (_split_heads(x), None), lambda _, gs: (jnp.concatenate(gs, axis=1),))


@jax.custom_vjp
def _merge_heads(xs):
    return jnp.concatenate(xs, axis=1)


_merge_heads.defvjp(lambda xs: (_merge_heads(xs), None), lambda _, g: (_split_heads(g),))


@jax.custom_vjp
def _split_chunks(x):
    return tuple(x[i * CHUNK:(i + 1) * CHUNK] for i in range(x.shape[0] // CHUNK))


_split_chunks.defvjp(lambda x: (_split_chunks(x), None), lambda _, gs: (jnp.concatenate(gs, axis=0),))


@jax.custom_vjp
def _merge_chunks(xs):
    return jnp.concatenate(xs, axis=0)


_merge_chunks.defvjp(lambda xs: (_merge_chunks(xs), None), lambda _, g: (_split_chunks(g),))


def _blocks(x):
    return [_split_heads(rows) for rows in _split_chunks(x)]


def _per_chunk_rows(per_chunk, rid):
    out = per_chunk[0]
    for i in range(1, len(per_chunk)):
        out = jnp.where(rid >= i * CHUNK, per_chunk[i], out)
    return out


HEADS = range(N_HEADS)
CHUNKS_PER_STEP = 4
ML_CHUNKS_PER_STEP = 1


def _hg_chunk(hq, hf, hi, hgate, l0, l1, nw, sts):
    n = hq.shape[0] // CHUNK
    row, col = _chunk_masks(n * CHUNK)
    same_chunk = functools.reduce(jnp.logical_or, [(row >= i * CHUNK) & (row < (i + 1) * CHUNK) &
                                                   (col >= i * CHUNK) & (col < (i + 1) * CHUNK) for i in range(n)])
    causal = _chunk_masks(CHUNK)
    causal = causal[1] <= causal[0]
    mx = lax.stop_gradient(jnp.maximum(l0, l1))
    e0 = jnp.exp(l0 - mx)
    e1 = jnp.exp(l1 - mx)
    lb = e0 / (e0 + e1)
    sig = jax.nn.sigmoid(hf)
    lf = jnp.log(lb + (1.0 - lb) * sig)
    k = (1.0 - lb) * jax.nn.sigmoid(-hf)
    q = jax.nn.silu(hq)
    b = _dg(((col <= row) & same_chunk).astype(F32), lf, 1, 0, HIGHEST)
    rid = lax.broadcasted_iota(jnp.int32, b.shape, 0)
    pick = lambda r: jnp.sum(jnp.where(rid == r, b, 0.0), axis=0, keepdims=True)
    b_last_c = [pick(i * CHUNK + CHUNK - 1) for i in range(n)]
    b_ref = _per_chunk_rows([pick(i * CHUNK + CHUNK // 2 - 1) for i in range(n)], rid)
    b_last = _per_chunk_rows(b_last_c, rid)
    qa = _blocks(q * jnp.exp(b - b_ref))
    ka = _blocks(k * jnp.exp(b_ref - b))
    qe = _blocks(q * jnp.exp(b))
    kd = _blocks(k * jnp.exp(b_last - b))
    decay = [_split_heads(jnp.exp(b_last_c[i])) for i in range(n)]
    v = _blocks(hi)
    chunks = range(n)
    attn = [[jnp.where(causal, _nt(qa[i][h], ka[i][h]), 0.0) for h in HEADS] for i in chunks]
    intra = [[_nn(attn[i][h], v[i][h]) for h in HEADS] for i in chunks]
    kv = [[_tn(v[i][h], kd[i][h]) for h in HEADS] for i in chunks]
    normed = []
    for i in chunks:
        inter = [_nt(qe[i][h], sts[h]) for h in HEADS]
        sts = tuple(decay[i][h] * sts[h] + kv[i][h] for h in HEADS)
        o = [intra[i][h] + inter[h] for h in HEADS]
        normed.append(_merge_heads(tuple(o[h] * lax.rsqrt(jnp.mean(o[h] * o[h], axis=-1, keepdims=True) + LN_EPS)
                                         for h in HEADS)))
    return _merge_chunks(tuple(normed)) * nw * jax.nn.silu(hgate), sts


def _seg(ref, seg):
    return ref[:, seg * D_GROUP:(seg + 1) * D_GROUP]


def _hgrn2_fwd(proj, logits, norm_w):
    s = proj.shape[0]
    rows = CHUNKS_PER_STEP * CHUNK
    nc = s // rows

    def body(p_ref, lg_ref, nw_ref, y_ref, st_out_ref, st_scr):
        @pl.when(pl.program_id(0) == 0)
        def _():
            st_scr[...] = jnp.zeros_like(st_scr)

        sts = tuple(st_scr[h] for h in HEADS)
        y, sts_new = _hg_chunk(_seg(p_ref, 0), _seg(p_ref, 1), _seg(p_ref, 2), _seg(p_ref, 3),
                               lg_ref[0:1, :], lg_ref[1:2, :], nw_ref[...], sts)
        y_ref[...] = y.astype(y_ref.dtype)
        for h in HEADS:
            st_out_ref[h] = sts[h]
            st_scr[h] = sts_new[h]

    return pl.pallas_call(
        body, name="hgrn2_fwd", grid=(nc,),
        in_specs=[pl.BlockSpec((rows, 4 * D_GROUP), lambda c: (c, 0)),
                  pl.BlockSpec((2, D_GROUP), lambda c: (0, 0)),
                  pl.BlockSpec((1, D_GROUP), lambda c: (0, 0))],
        out_specs=[pl.BlockSpec((rows, D_GROUP), lambda c: (c, 0)),
                   pl.BlockSpec((None, N_HEADS, D_HEAD, D_HEAD), lambda c: (c, 0, 0, 0))],
        out_shape=[jax.ShapeDtypeStruct((s, 2 * D_GROUP), BF16),
                   jax.ShapeDtypeStruct((nc, N_HEADS, D_HEAD, D_HEAD), F32)],
        scratch_shapes=[pltpu.VMEM((N_HEADS, D_HEAD, D_HEAD), F32)],
        compiler_params=_params(("arbitrary",)),
    )(proj, logits, norm_w)


def _hgrn2_bwd(proj, logits, norm_w, states, d_y):
    s = proj.shape[0]
    rows = CHUNKS_PER_STEP * CHUNK
    nc = s // rows

    def body(p_ref, lg_ref, nw_ref, st_ref, dy_ref, dp_ref, dl_ref, dnw_ref, dsum_ref, dst_scr):
        @pl.when(pl.program_id(0) == 0)
        def _():
            dst_scr[...] = jnp.zeros_like(dst_scr)
            dl_ref[...] = jnp.zeros_like(dl_ref)
            dnw_ref[...] = jnp.zeros_like(dnw_ref)
            dsum_ref[...] = jnp.zeros_like(dsum_ref)

        _, vjp = jax.vjp(_hg_chunk, _seg(p_ref, 0), _seg(p_ref, 1), _seg(p_ref, 2), _seg(p_ref, 3),
                         lg_ref[0:1, :], lg_ref[1:2, :], nw_ref[...], tuple(st_ref[h] for h in HEADS))
        d_hq, d_hf, d_hi, d_hg, d_l0, d_l1, d_nw, d_sts = vjp((dy_ref[...], tuple(dst_scr[h] for h in HEADS)))
        for seg, val in enumerate((d_hq, d_hf, d_hi, d_hg)):
            dp_ref[:, seg * D_GROUP:(seg + 1) * D_GROUP] = val.astype(dp_ref.dtype)
            dsum_ref[:, seg * D_GROUP:(seg + 1) * D_GROUP] += jnp.sum(val, axis=0, keepdims=True)
        dl_ref[0:1, :] += d_l0
        dl_ref[1:2, :] += d_l1
        dnw_ref[...] += d_nw
        for h in HEADS:
            dst_scr[h] = d_sts[h]

    rev = lambda c: nc - 1 - c
    return pl.pallas_call(
        body, name="hgrn2_bwd", grid=(nc,),
        in_specs=[pl.BlockSpec((rows, 4 * D_GROUP), lambda c: (rev(c), 0)),
                  pl.BlockSpec((2, D_GROUP), lambda c: (0, 0)),
                  pl.BlockSpec((1, D_GROUP), lambda c: (0, 0)),
                  pl.BlockSpec((None, N_HEADS, D_HEAD, D_HEAD), lambda c: (rev(c), 0, 0, 0)),
                  pl.BlockSpec((rows, D_GROUP), lambda c: (rev(c), 0))],
        out_specs=[pl.BlockSpec((rows, 4 * D_GROUP), lambda c: (rev(c), 0)),
                   pl.BlockSpec((2, D_GROUP), lambda c: (0, 0)),
                   pl.BlockSpec((1, D_GROUP), lambda c: (0, 0)),
                   pl.BlockSpec((1, 4 * D_GROUP), lambda c: (0, 0))],
        out_shape=[jax.ShapeDtypeStruct((s, D_IN_MAIN), BF16), jax.ShapeDtypeStruct((2, D_GROUP), F32),
                   jax.ShapeDtypeStruct((1, D_GROUP), F32), jax.ShapeDtypeStruct((1, 4 * D_GROUP), F32)],
        scratch_shapes=[pltpu.VMEM((N_HEADS, D_HEAD, D_HEAD), F32)],
        compiler_params=_params(("arbitrary",)),
    )(proj, logits, norm_w, states, d_y)


def _gate_column(gates, lane, idx):
    return jnp.sum(jnp.where(lane == idx, gates, 0.0), axis=1, keepdims=True)


def _head_layer_norm(h):
    mu = jnp.mean(h, axis=-1, keepdims=True)
    var = jnp.mean(jnp.square(h - mu), axis=-1, keepdims=True)
    return (h - mu) * lax.rsqrt(var + LN_EPS)


def _ml_chunk(qc, kc, v, mo, gates, nw, cts, ns, ms):
    n = qc.shape[0] // CHUNK
    row, col = _chunk_masks(CHUNK)
    mask = col <= row
    eye = col == row
    to_row = lambda t: jnp.sum(jnp.where(eye, t, 0.0), axis=0, keepdims=True)
    q = _blocks(qc * (D_HEAD ** -0.5))
    k = _blocks(kc)
    vs = _blocks(v)
    gate_rows = _split_chunks(gates)
    lane = lax.broadcasted_iota(jnp.int32, gate_rows[0].shape, 1)
    each = [(i, h) for i in range(n) for h in HEADS]
    on_each = lambda f: {ih: f(*ih) for ih in each}
    ig = on_each(lambda i, h: _gate_column(gate_rows[i], lane, h))
    lf = on_each(lambda i, h: jax.nn.log_sigmoid(_gate_column(gate_rows[i], lane, N_HEADS + h)))
    lf_row = on_each(lambda i, h: to_row(lf[i, h]))
    ig_row = on_each(lambda i, h: to_row(ig[i, h]))
    b_col = on_each(lambda i, h: jnp.sum(jnp.where(mask, lf_row[i, h], 0.0), axis=1, keepdims=True))
    b_row = on_each(lambda i, h: jnp.sum(jnp.where(row <= col, lf[i, h], 0.0), axis=0, keepdims=True))
    g = on_each(lambda i, h: jnp.sum(lf[i, h], axis=0, keepdims=True))
    d = on_each(lambda i, h: jnp.where(mask, b_col[i, h] - b_row[i, h] + ig_row[i, h], -jnp.inf))
    a = on_each(lambda i, h: g[i, h] - b_col[i, h] + ig[i, h])
    m_at = {(0, h): ms[h] for h in HEADS}
    for i, h in each:
        m_at[i + 1, h] = lax.stop_gradient(jnp.maximum(g[i, h] + m_at[i, h], jnp.max(a[i, h], axis=0, keepdims=True)))
    inter = on_each(lambda i, h: b_col[i, h] + m_at[i, h])
    m_t = on_each(lambda i, h: lax.stop_gradient(jnp.maximum(inter[i, h], jnp.max(d[i, h], axis=1, keepdims=True))))
    qk = on_each(lambda i, h: _nt(q[i][h], k[i][h]))
    sc = on_each(lambda i, h: qk[i, h] * jnp.exp(d[i, h] - m_t[i, h]))
    w_inter = on_each(lambda i, h: jnp.exp(inter[i, h] - m_t[i, h]))
    sv = on_each(lambda i, h: _nn(sc[i, h], vs[i][h]))
    decay = on_each(lambda i, h: jnp.exp(g[i, h] + m_at[i, h] - m_at[i + 1, h]))
    wk = on_each(lambda i, h: k[i][h] * jnp.exp(a[i, h] - m_at[i + 1, h]))
    kv = on_each(lambda i, h: _tn(vs[i][h], wk[i, h]))
    normed = []
    for i in range(n):
        qc_state = [_nt(q[i][h], cts[h]) for h in HEADS]
        num = [sv[i, h] + w_inter[i, h] * qc_state[h] for h in HEADS]
        den = [jnp.sum(sc[i, h], axis=1, keepdims=True)
               + w_inter[i, h] * jnp.sum(q[i][h] * ns[h], axis=1, keepdims=True) for h in HEADS]
        hh = [num[h] / jnp.maximum(jnp.abs(den[h]), jnp.exp(-m_t[i, h])) for h in HEADS]
        cts = tuple(decay[i, h] * cts[h] + kv[i, h] for h in HEADS)
        ns = tuple(decay[i, h] * ns[h] + jnp.sum(wk[i, h], axis=0, keepdims=True) for h in HEADS)
        normed.append(_merge_heads(tuple(_head_layer_norm(hh[h]) for h in HEADS)))
    y = jax.nn.sigmoid(mo) * (_merge_chunks(tuple(normed)) * nw)
    return y, cts, ns, tuple(m_at[n, h] for h in HEADS)


def _mlstm_fwd(qk, proj, gates, norm_w, y):
    s = proj.shape[0]
    rows = ML_CHUNKS_PER_STEP * CHUNK
    nc = s // rows

    def body(qk_ref, vo_ref, g_ref, nw_ref, _, y_ref, ct_out, n_out, m_out, ct_scr, n_scr, m_scr):
        @pl.when(pl.program_id(0) == 0)
        def _():
            ct_scr[...] = jnp.zeros_like(ct_scr)
            n_scr[...] = jnp.zeros_like(n_scr)
            m_scr[...] = jnp.full(m_scr.shape, NEG_BIG, F32)

        cts = tuple(ct_scr[h] for h in HEADS)
        ns = tuple(n_scr[h] for h in HEADS)
        ms = tuple(m_scr[h] for h in HEADS)
        y, cts_new, ns_new, ms_new = _ml_chunk(_seg(qk_ref, 0), _seg(qk_ref, 1), _seg(vo_ref, 0), _seg(vo_ref, 1),
                                               g_ref[...], nw_ref[...], cts, ns, ms)
        y_ref[...] = y.astype(y_ref.dtype)
        for h in HEADS:
            ct_out[h], n_out[h], m_out[h] = cts[h], ns[h], ms[h]
            ct_scr[h], n_scr[h], m_scr[h] = cts_new[h], ns_new[h], ms_new[h]

    st = lambda r, w: pl.BlockSpec((None, N_HEADS, r, w), lambda c: (c, 0, 0, 0))
    return pl.pallas_call(
        body, name="mlstm_fwd", grid=(nc,),
        in_specs=[pl.BlockSpec((rows, 2 * D_GROUP), lambda c: (c, 0)),
                  pl.BlockSpec((rows, 2 * D_GROUP), lambda c: (c, 3)),
                  pl.BlockSpec((rows, LANES), lambda c: (c, 0)),
                  pl.BlockSpec((1, D_GROUP), lambda c: (0, 0)),
                  pl.BlockSpec(memory_space=pl.ANY)],
        out_specs=[pl.BlockSpec((rows, D_GROUP), lambda c: (c, 1)),
                   st(D_HEAD, D_HEAD), st(1, D_HEAD), st(1, 1)],
        out_shape=[jax.ShapeDtypeStruct(y.shape, y.dtype),
                   jax.ShapeDtypeStruct((nc, N_HEADS, D_HEAD, D_HEAD), F32),
                   jax.ShapeDtypeStruct((nc, N_HEADS, 1, D_HEAD), F32),
                   jax.ShapeDtypeStruct((nc, N_HEADS, 1, 1), F32)],
        input_output_aliases={4: 0},
        scratch_shapes=[pltpu.VMEM((N_HEADS, D_HEAD, D_HEAD), F32), pltpu.VMEM((N_HEADS, 1, D_HEAD), F32),
                        pltpu.VMEM((N_HEADS, 1, 1), F32)],
        compiler_params=_params(("arbitrary",)),
    )(qk, proj, gates, norm_w, y)


def _mlstm_bwd(qk, proj, gates, norm_w, ct_s, n_s, m_s, d_y, d_proj):
    s = proj.shape[0]
    rows = ML_CHUNKS_PER_STEP * CHUNK
    nc = s // rows

    def body(qk_ref, vo_ref, g_ref, nw_ref, ct_ref, n_ref, m_ref, dy_ref, _,
             dp_ref, dqk_ref, dg_ref, dnw_ref, dsum_ref, dct_scr, dn_scr):
        @pl.when(pl.program_id(0) == 0)
        def _():
            dct_scr[...] = jnp.zeros_like(dct_scr)
            dn_scr[...] = jnp.zeros_like(dn_scr)
            dnw_ref[...] = jnp.zeros_like(dnw_ref)
            dsum_ref[...] = jnp.zeros_like(dsum_ref)

        ms = tuple(m_ref[h] for h in HEADS)
        step = lambda *a: _ml_chunk(*a, ms)[:3]
        _, vjp = jax.vjp(step, _seg(qk_ref, 0), _seg(qk_ref, 1), _seg(vo_ref, 0), _seg(vo_ref, 1), g_ref[...],
                         nw_ref[...], tuple(ct_ref[h] for h in HEADS), tuple(n_ref[h] for h in HEADS))
        d_q, d_k, d_v, d_o, d_gates, d_nw, d_cts, d_ns = vjp(
            (dy_ref[...], tuple(dct_scr[h] for h in HEADS), tuple(dn_scr[h] for h in HEADS)))
        dqk_ref[:, 0:D_GROUP] = d_q
        dqk_ref[:, D_GROUP:2 * D_GROUP] = d_k
        for seg, val in enumerate((d_v, d_o)):
            dp_ref[:, seg * D_GROUP:(seg + 1) * D_GROUP] = val.astype(dp_ref.dtype)
            dsum_ref[:, seg * D_GROUP:(seg + 1) * D_GROUP] += jnp.sum(val, axis=0, keepdims=True)
        dg_ref[...] = d_gates
        dnw_ref[...] += d_nw
        for h in HEADS:
            dct_scr[h] = d_cts[h]
            dn_scr[h] = d_ns[h]

    rev = lambda c: nc - 1 - c
    st = lambda r, w: pl.BlockSpec((None, N_HEADS, r, w), lambda c: (rev(c), 0, 0, 0))
    return pl.pallas_call(
        body, name="mlstm_bwd", grid=(nc,),
        in_specs=[pl.BlockSpec((rows, 2 * D_GROUP), lambda c: (rev(c), 0)),
                  pl.BlockSpec((rows, 2 * D_GROUP), lambda c: (rev(c), 3)),
                  pl.BlockSpec((rows, LANES), lambda c: (rev(c), 0)),
                  pl.BlockSpec((1, D_GROUP), lambda c: (0, 0)),
                  st(D_HEAD, D_HEAD), st(1, D_HEAD), st(1, 1),
                  pl.BlockSpec((rows, D_GROUP), lambda c: (rev(c), 1)),
                  pl.BlockSpec(memory_space=pl.ANY)],
        out_specs=[pl.BlockSpec((rows, 2 * D_GROUP), lambda c: (rev(c), 3)),
                   pl.BlockSpec((rows, 2 * D_GROUP), lambda c: (rev(c), 0)),
                   pl.BlockSpec((rows, LANES), lambda c: (rev(c), 0)),
                   pl.BlockSpec((1, D_GROUP), lambda c: (0, 0)),
                   pl.BlockSpec((1, 2 * D_GROUP), lambda c: (0, 0))],
        out_shape=[jax.ShapeDtypeStruct(d_proj.shape, d_proj.dtype), jax.ShapeDtypeStruct((s, 2 * D_GROUP), F32),
                   jax.ShapeDtypeStruct((s, LANES), F32), jax.ShapeDtypeStruct((1, D_GROUP), F32),
                   jax.ShapeDtypeStruct((1, 2 * D_GROUP), F32)],
        input_output_aliases={8: 0},
        scratch_shapes=[pltpu.VMEM((N_HEADS, D_HEAD, D_HEAD), F32), pltpu.VMEM((N_HEADS, 1, D_HEAD), F32)],
        compiler_params=_params(("arbitrary",)),
    )(qk, proj, gates, norm_w, ct_s, n_s, m_s, d_y, d_proj)


LN_TOKENS = 512
ATT_TOKENS = 512


def _proj_res_ln(a, w, xres, g, b, name):
    s, dm = xres.shape
    k = a.shape[1]
    tb = min(LN_TOKENS, s)

    def body(a_ref, w_ref, x_ref, g_ref, b_ref, z_ref, o_ref):
        z = ALPHA * x_ref[...] + _nn_raw(a_ref[...], w_ref[...])
        z_ref[...] = z
        o_ref[...] = _layer_norm(z, g_ref[...], b_ref[...])

    tok = pl.BlockSpec((tb, dm), lambda i: (i, 0))
    vec = pl.BlockSpec((1, dm), lambda i: (0, 0))
    act = jax.ShapeDtypeStruct((s, dm), F32)
    return pl.pallas_call(
        body, name=name, grid=(s // tb,),
        in_specs=[pl.BlockSpec((tb, k), lambda i: (i, 0)), pl.BlockSpec((k, dm), lambda i: (0, 0)), tok, vec, vec],
        out_specs=[tok, tok], out_shape=[act, act], compiler_params=_params(("parallel",)),
    )(a, w, xres, g, b)


def _ln_bwd_proj(d_out, z, g, b, w, name):
    s, dm = z.shape
    k = w.shape[0]
    tb = min(LN_TOKENS, s)

    def body(do_ref, z_ref, g_ref, b_ref, w_ref, dz_ref, da_ref, dg_ref, db_ref):
        @pl.when(pl.program_id(0) == 0)
        def _():
            dg_ref[...] = jnp.zeros_like(dg_ref)
            db_ref[...] = jnp.zeros_like(db_ref)

        _, vjp = jax.vjp(_layer_norm, z_ref[...], g_ref[...], b_ref[...])
        d_z, d_g, d_b = vjp(do_ref[...])
        dz_ref[...] = d_z
        da_ref[...] = _nt_raw(d_z, w_ref[...])
        dg_ref[...] += d_g
        db_ref[...] += d_b

    tok = pl.BlockSpec((tb, dm), lambda i: (i, 0))
    vec = pl.BlockSpec((1, dm), lambda i: (0, 0))
    return pl.pallas_call(
        body, name=name, grid=(s // tb,),
        in_specs=[tok, tok, vec, vec, pl.BlockSpec((k, dm), lambda i: (0, 0))],
        out_specs=[tok, pl.BlockSpec((tb, k), lambda i: (i, 0)), vec, vec],
        out_shape=[jax.ShapeDtypeStruct((s, dm), F32), jax.ShapeDtypeStruct((s, k), F32),
                   jax.ShapeDtypeStruct((1, dm), F32), jax.ShapeDtypeStruct((1, dm), F32)],
        compiler_params=_params(("arbitrary",)),
    )(d_out, z, g, b, w)


def _proj_loss_tail(a, w, xres, g, b, target):
    s, dm = xres.shape
    k = a.shape[1]
    tb = min(ATT_TOKENS, s)

    def loss_fn(z, gg, bb, tgt):
        err = jnp.square(_layer_norm(z, gg, bb) - tgt)
        return 0.5 * jnp.sum(jnp.mean(err, axis=-1, keepdims=True), axis=0, keepdims=True)

    def body(a_ref, w_ref, x_ref, g_ref, b_ref, t_ref, loss_ref, dz_ref, dg_ref, db_ref):
        @pl.when(pl.program_id(0) == 0)
        def _():
            loss_ref[...] = jnp.zeros_like(loss_ref)
            dg_ref[...] = jnp.zeros_like(dg_ref)
            db_ref[...] = jnp.zeros_like(db_ref)

        z = ALPHA * x_ref[...] + _nn_raw(a_ref[...], w_ref[...])
        tgt = t_ref[...]
        loss, vjp = jax.vjp(lambda zz, gg, bb: loss_fn(zz, gg, bb, tgt), z, g_ref[...], b_ref[...])
        d_z, d_g, d_b = vjp(jnp.ones((1, 1), F32))
        loss_ref[...] += loss
        dz_ref[...] = d_z
        dg_ref[...] += d_g
        db_ref[...] += d_b

    tok = pl.BlockSpec((tb, dm), lambda i: (i, 0))
    vec = pl.BlockSpec((1, dm), lambda i: (0, 0))
    one = pl.BlockSpec((1, 1), lambda i: (0, 0))
    return pl.pallas_call(
        body, name="ffn_down_loss_tail", grid=(s // tb,),
        in_specs=[pl.BlockSpec((tb, k), lambda i: (i, 0)), pl.BlockSpec((k, dm), lambda i: (0, 0)), tok, vec, vec, tok],
        out_specs=[one, tok, vec, vec],
        out_shape=[jax.ShapeDtypeStruct((1, 1), F32), jax.ShapeDtypeStruct((s, dm), F32),
                   jax.ShapeDtypeStruct((1, dm), F32), jax.ShapeDtypeStruct((1, dm), F32)],
        compiler_params=_params(("arbitrary",)),
    )(a, w, xres, g, b, target)


def _att_heads(qs, ks, vs):
    sc = [_nt(q, k) * (CA_DH ** -0.5) for q, k in zip(qs, ks)]
    p = [jax.nn.softmax(s, axis=-1) for s in sc]
    return tuple(_nn(pp, v) for pp, v in zip(p, vs))


def _head_slices(ref_or_value, offset):
    return tuple(ref_or_value[:, offset + h * CA_DH:offset + (h + 1) * CA_DH] for h in range(CA_HEADS))


def _cross_attention_fwd(x1, kv, wq, wo, g, b):
    s = x1.shape[0]
    tb = min(ATT_TOKENS, s)

    def body(x_ref, kv_ref, wq_ref, wo_ref, g_ref, b_ref, att_ref, z_ref, o_ref):
        x_blk = x_ref[...]
        q = _nn_raw(x_blk, wq_ref[...])
        att = jnp.concatenate(_att_heads(_head_slices(q, 0), _head_slices(kv_ref, 0), _head_slices(kv_ref, D_MODEL)),
                              axis=1)
        att_ref[...] = att.astype(att_ref.dtype)
        z = ALPHA * x_blk + _nn_raw(att, wo_ref[...])
        z_ref[...] = z
        o_ref[...] = _layer_norm(z, g_ref[...], b_ref[...])

    tok = pl.BlockSpec((tb, D_MODEL), lambda i: (i, 0))
    mat = pl.BlockSpec((D_MODEL, D_MODEL), lambda i: (0, 0))
    vec = pl.BlockSpec((1, D_MODEL), lambda i: (0, 0))
    act = jax.ShapeDtypeStruct((s, D_MODEL), F32)
    return pl.pallas_call(
        body, name="cross_attention_fwd", grid=(s // tb,),
        in_specs=[tok, pl.BlockSpec((N_MEM, 2 * D_MODEL), lambda i: (0, 0)), mat, mat, vec, vec],
        out_specs=[tok, tok, tok],
        out_shape=[jax.ShapeDtypeStruct((s, D_MODEL), BF16), act, act],
        compiler_params=_params(("parallel",)),
    )(x1, kv, wq, wo, g, b)


def _cross_attention_bwd(d_x2, x1, z2, kv, wq, wo, g, b):
    s = x1.shape[0]
    tb = min(ATT_TOKENS, s)

    def body(dx2_ref, x_ref, z_ref, kv_ref, wq_ref, wo_ref, g_ref, b_ref,
             dx1_ref, dq_ref, dz_ref, dkv_ref, dg_ref, db_ref):
        @pl.when(pl.program_id(0) == 0)
        def _():
            dkv_ref[...] = jnp.zeros_like(dkv_ref)
            dg_ref[...] = jnp.zeros_like(dg_ref)
            db_ref[...] = jnp.zeros_like(db_ref)

        _, ln_vjp = jax.vjp(_layer_norm, z_ref[...], g_ref[...], b_ref[...])
        d_z, d_g, d_b = ln_vjp(dx2_ref[...])
        dg_ref[...] += d_g
        db_ref[...] += d_b
        dz_ref[...] = d_z.astype(dz_ref.dtype)
        d_att = _nt_raw(d_z, wo_ref[...])
        q = _nn_raw(x_ref[...], wq_ref[...])
        _, vjp = jax.vjp(_att_heads, _head_slices(q, 0), _head_slices(kv_ref, 0), _head_slices(kv_ref, D_MODEL))
        d_qs, d_ks, d_vs = vjp(_head_slices(d_att, 0))
        for h in range(CA_HEADS):
            lo = h * CA_DH
            dkv_ref[:, lo:lo + CA_DH] += d_ks[h]
            dkv_ref[:, D_MODEL + lo:D_MODEL + lo + CA_DH] += d_vs[h]
        d_q = jnp.concatenate(d_qs, axis=1)
        dq_ref[...] = d_q.astype(dq_ref.dtype)
        dx1_ref[...] = ALPHA * d_z + _nt_raw(d_q, wq_ref[...])

    tok = pl.BlockSpec((tb, D_MODEL), lambda i: (i, 0))
    mem = pl.BlockSpec((N_MEM, 2 * D_MODEL), lambda i: (0, 0))
    mat = pl.BlockSpec((D_MODEL, D_MODEL), lambda i: (0, 0))
    vec = pl.BlockSpec((1, D_MODEL), lambda i: (0, 0))
    low = jax.ShapeDtypeStruct((s, D_MODEL), BF16)
    return pl.pallas_call(
        body, name="cross_attention_bwd", grid=(s // tb,),
        in_specs=[tok, tok, tok, mem, mat, mat, vec, vec], out_specs=[tok, tok, tok, mem, vec, vec],
        out_shape=[jax.ShapeDtypeStruct((s, D_MODEL), F32), low, low,
                   jax.ShapeDtypeStruct((N_MEM, 2 * D_MODEL), F32),
                   jax.ShapeDtypeStruct((1, D_MODEL), F32), jax.ShapeDtypeStruct((1, D_MODEL), F32)],
        compiler_params=_params(("arbitrary",)),
    )(d_x2, x1, z2, kv, wq, wo, g, b)


def _local_step(x, mem, target, w, mid_weights=None, ffn_weights=None, down_weights=None, on_ffn_grads=None,
                on_mid_grads=None,
                on_small_grads=None, on_last_grads=None):
    w = dict(w)
    s = x.shape[0]
    tm = min(512, s)
    tt = min(512, s)
    proj = _matmul_nn(x, w["w_in_main"], w["b_in_main"], min(2048, s), 512, "proj")
    gates = _matmul_nn(x, w["w_in_gate"], w["b_in_gate"], tm, LANES, "proj_gates")
    qk = _ml_conv_fwd(proj, w["ml_conv_w"], w["ml_conv_b"])
    y, hg_states = _hgrn2_fwd(proj, w["hg_lb_logits"], w["hg_norm_w"])
    y, ct_s, n_s, m_s = _mlstm_fwd(qk, proj, gates, w["ml_norm_w"], y)
    if mid_weights is not None:
        w.update(mid_weights(y))
    z1, x1 = _proj_res_ln(y, w["w_out"], x, w["ln1_g"], w["ln1_b"], "out_proj_ln1")
    kv = _matmul_nn(mem, w["ca_wkv"], None, N_MEM, CA_DH, "kv")
    att, z2, x2 = _cross_attention_fwd(x1, kv, w["ca_wq"], w["ca_wo"], w["ln2_g"], w["ln2_b"])
    if ffn_weights is not None:
        w.update(ffn_weights(x2))
    u = _matmul_nn(x2, w["ffn_w_up"], None, min(2048, s), UP_SHARD_P, "ffn_up", BF16)
    hid = _ffn_conv_fwd(u, w["ffn_conv_w"], w["ffn_conv_b"])
    if down_weights is not None:
        w.update(down_weights(hid))
    loss, d_z3, d_ln3_g, d_ln3_b = _proj_loss_tail(hid, w["ffn_w_down"], x2, w["ln3_g"], w["ln3_b"], target)
    grads = {"ln3_g": d_ln3_g, "ln3_b": d_ln3_b}
    grads["ffn_w_down"] = _matmul_tn(hid, d_z3, 1536, D_MODEL, tt, "d_w_down")
    d_hid = _matmul_nt([(d_z3, w["ffn_w_down"])], None, 1.0, tm, D_FF_P, "d_hid", BF16)
    d_ug, d_uv, d_cwg, d_cwv, d_cbg, d_cbv = _ffn_conv_bwd(u, w["ffn_conv_w"], w["ffn_conv_b"], d_hid)
    grads["ffn_conv_w"] = jnp.concatenate([d_cwg, d_cwv], axis=-1)
    grads["ffn_conv_b"] = jnp.concatenate([d_cbg, d_cbv], axis=-1)
    half = N_DEV // 2
    d_w_up = _matmul_tn(x2, d_ug, D_MODEL, UP_SHARD_P, tt, "d_w_up_gate", shards=N_DEV, group=half)
    grads["ffn_w_up"] = _matmul_tn(x2, d_uv, D_MODEL, UP_SHARD_P, tt, "d_w_up_val", shards=N_DEV,
                                   shard0=half, group=half, into=d_w_up)
    d_x2 = _matmul_nt([(d_ug, w["ffn_w_up"], 0), (d_uv, w["ffn_w_up"], N_DEV // 2)], d_z3, ALPHA,
                      min(256, s), D_MODEL, "d_x2")
    if on_ffn_grads is not None:
        d_x2 = on_ffn_grads(grads, d_x2)
    d_x1, d_q, d_z2, d_kv, grads["ln2_g"], grads["ln2_b"] = _cross_attention_bwd(
        d_x2, x1, z2, kv, w["ca_wq"], w["ca_wo"], w["ln2_g"], w["ln2_b"])
    grads["ca_wo"] = _matmul_tn(att, d_z2, D_MODEL, D_MODEL, tt, "d_ca_wo")
    grads["ca_wq"] = _matmul_tn(x1, d_q, D_MODEL, D_MODEL, tt, "d_ca_wq")
    grads["ca_wkv"] = _matmul_tn(mem, d_kv, D_MODEL, CA_DH, N_MEM, "d_ca_wkv", shards=N_DEV, group=N_DEV)
    d_z1, d_y, grads["ln1_g"], grads["ln1_b"] = _ln_bwd_proj(d_x1, z1, w["ln1_g"], w["ln1_b"], w["w_out"],
                                                             "ln1_bwd_out_proj")
    grads["w_out"] = _matmul_tn(y, d_z1, D_MODEL, D_MODEL, tt, "d_w_out")
    if on_mid_grads is not None:
        d_y = on_mid_grads(grads, d_y)
    d_proj, grads["hg_lb_logits"], grads["hg_norm_w"], db_hg = _hgrn2_bwd(
        proj, w["hg_lb_logits"], w["hg_norm_w"], hg_states, d_y)
    d_proj, d_qk, d_gates, grads["ml_norm_w"], db_vo = _mlstm_bwd(
        qk, proj, gates, w["ml_norm_w"], ct_s, n_s, m_s, d_y, d_proj)
    d_proj, grads["ml_conv_w"], grads["ml_conv_b"], db_qk = _ml_conv_bwd(
        proj, w["ml_conv_w"], w["ml_conv_b"], d_qk, d_proj)
    grads["b_in_main"] = jnp.concatenate([db_hg, db_qk, db_vo], axis=-1)
    grads["w_in_gate"], grads["b_in_gate"] = _matmul_tn(x, d_gates, D_MODEL, LANES, tt, "d_w_in_gates", colsum=True)
    if on_small_grads is not None:
        d_proj = on_small_grads(grads, loss, d_proj)
    grads["w_in_main"] = _matmul_tn(x, d_proj, D_MODEL, min(2048, D_IN_MAIN), tt, "d_w_in")
    if on_last_grads is not None:
        d_z1 = on_last_grads(grads, d_z1)
    grad_x = _matmul_nt([(d_proj, w["w_in_main"]), (d_gates, w["w_in_gate"])], d_z1, ALPHA, tm, D_MODEL, "d_x")
    return loss, grad_x, grads


HBM_SPEC = pl.BlockSpec(memory_space=pltpu.HBM)


def _coords():
    return lax.axis_index("x"), lax.axis_index("y"), lax.axis_index("c")


def _other_chips(x, y):
    return [(1 - x, y), (x, 1 - y), (1 - x, 1 - y)]


def _my_slot():
    x, y, c = _coords()
    return 4 * x + 2 * y + c


SEM_SPEC = pl.BlockSpec(memory_space=pltpu.SEMAPHORE)
ANY_SPEC = pl.BlockSpec(memory_space=pl.ANY)
SIDE_EFFECT = pltpu.SideEffectType.DATAFLOW_SIDE_EFFECTING


def _peer(x, y, c, d):
    flip = lambda v, bit: 1 - v if bit else v
    p = (flip(x, d & 4), flip(y, d & 2), flip(c, d & 1))
    return p, 4 * p[0] + 2 * p[1] + p[2]


def _direct_copies(gather, src_refs, land_refs, send_sems, recv_sems):
    x, y, c = _coords()
    me = 4 * x + 2 * y + c
    copies = []
    for a in range(len(src_refs)):
        for d in range(1, N_DEV):
            peer, peer_slot = _peer(x, y, c, d)
            copies.append(pltpu.make_async_remote_copy(
                src_ref=src_refs[a] if gather else src_refs[a].at[peer_slot],
                dst_ref=land_refs[a].at[me] if gather else land_refs[a].at[d - 1],
                send_sem=send_sems.at[7 * a + d - 1], recv_sem=recv_sems.at[7 * a + d - 1],
                device_id=peer, device_id_type=MESH))
    return copies


def _hbm(t):
    return pltpu.HBM(t.shape, t.dtype)


def _chip_copies(src_refs, land_refs, send_sems, recv_sems):
    x, y, c = _coords()
    me = 4 * x + 2 * y + c
    targets = [(x, y, 1 - c)] + [(cx, cy, c) for cx, cy in _other_chips(x, y)]
    return [pltpu.make_async_remote_copy(
        src_ref=src_refs[a], dst_ref=land_refs[a].at[me], send_sem=send_sems.at[4 * a + k],
        recv_sem=recv_sems.at[4 * a + k], device_id=target, device_id_type=MESH)
        for a in range(len(src_refs)) for k, target in enumerate(targets)]


def _forward_copies(land_refs, send_sems, recv_sems):
    x, y, c = _coords()
    return [pltpu.make_async_remote_copy(
        src_ref=land_refs[a].at[4 * cx + 2 * cy + c], dst_ref=land_refs[a].at[4 * cx + 2 * cy + c],
        send_sem=send_sems.at[3 * a + j], recv_sem=recv_sems.at[3 * a + j],
        device_id=(x, y, 1 - c), device_id_type=MESH)
        for a in range(len(land_refs)) for j, (cx, cy) in enumerate(_other_chips(x, y))]


def _split_copy_start(make_copies, n_sems, operands, through, name):
    n_ops = len(operands)

    def body(*refs):
        for cp in make_copies(refs[:n_ops], refs[n_ops + 1], refs[n_ops + 2]):
            cp.start()

    ins = [pltpu.with_memory_space_constraint(t, pltpu.HBM) for t in (*operands, through)]
    sems = pltpu.SemaphoreType.DMA((n_sems,))
    res = pl.pallas_call(
        body, name=name, out_shape=(sems, sems, *[_hbm(t) for t in ins]),
        in_specs=[HBM_SPEC] * (n_ops + 1), out_specs=(SEM_SPEC, SEM_SPEC, *[HBM_SPEC] * (n_ops + 1)),
        input_output_aliases={i: 2 + i for i in range(n_ops + 1)},
        compiler_params=pltpu.CompilerParams(has_side_effects=SIDE_EFFECT),
    )(*ins)
    return (res[0], res[1], list(res[2:2 + n_ops])), res[2 + n_ops]


def _split_copy_wait(make_copies, started, after, name):
    send_sems, recv_sems, operands = started
    n_ops = len(operands)
    after = list(after) if isinstance(after, (list, tuple)) else [after]

    def body(*refs):
        for cp in make_copies(refs[:n_ops], refs[n_ops], refs[n_ops + 1]):
            cp.wait_send()
            cp.wait_recv()

    res = pl.pallas_call(
        body, name=name, out_shape=tuple(_hbm(t) for t in operands),
        in_specs=[HBM_SPEC] * n_ops + [SEM_SPEC, SEM_SPEC] + [ANY_SPEC] * len(after),
        out_specs=tuple([HBM_SPEC] * n_ops), input_output_aliases={i: i for i in range(n_ops)},
        compiler_params=pltpu.CompilerParams(has_side_effects=SIDE_EFFECT),
    )(*operands, send_sems, recv_sems, *after)
    return list(res)


def _halves(make_copies, na):
    return lambda refs, send_sems, recv_sems: make_copies(refs[:na], refs[na:], send_sems, recv_sems)


def _direct_start(gather, arrays, through, name):
    na = len(arrays)
    lands = [lax.empty((N_DEV,) + t.shape if gather else (N_DEV - 1,) + t.shape[1:], t.dtype) for t in arrays]
    return _split_copy_start(_halves(functools.partial(_direct_copies, gather), na), 7 * na, [*arrays, *lands],
                             through, name)


def _direct_wait(gather, started, after, name):
    na = len(started[2]) // 2
    operands = _split_copy_wait(_halves(functools.partial(_direct_copies, gather), na), started, after, name)
    return operands[:na], operands[na:]


def _two_level_gather(shards, glue, name):
    na = len(shards)
    lands = [lax.empty((N_DEV,) + t.shape, t.dtype) for t in shards]
    nothing = jnp.zeros((SUBLANES, LANES), F32)
    started, _ = _split_copy_start(_halves(_chip_copies, na), 4 * na, [*shards, *lands], nothing, name + "_start")
    operands = _split_copy_wait(_halves(_chip_copies, na), started, glue, name + "_wait")
    started, mine = _split_copy_start(_forward_copies, 3 * na, operands[na:], operands[0], name + "_forward_start")
    lands = _split_copy_wait(_forward_copies, started, mine, name + "_forward_wait")
    return [lax.dynamic_update_index_in_dim(land, own, _my_slot(), 0)
            for own, land in zip([mine, *operands[1:na]], lands)]


def _row_tile(rows):
    for t in (256, 176, 128):
        if rows % t == 0 and rows > t:
            return t
    return rows


def _adamw_math(g, w, m, v):
    m_new = ADAM_B1 * m + (1.0 - ADAM_B1) * g
    v_new = ADAM_B2 * v + (1.0 - ADAM_B2) * jnp.square(g)
    m_hat = m_new / (1.0 - ADAM_B1 ** ADAM_STEP)
    v_hat = v_new / (1.0 - ADAM_B2 ** ADAM_STEP)
    delta = -ADAM_LR * (m_hat / (jnp.sqrt(v_hat) + ADAM_EPS) + ADAM_WD * w)
    return delta, m_new, v_new


def _adamw_sharded(chip, sums, got, w, m, v, name):
    r, c = w.shape
    tr = _row_tile(r)
    n_got = got.shape[0]

    def body(chip_ref, s_ref, g_ref, w_ref, m_ref, v_ref, go_ref, d_ref, nm_ref, nv_ref):
        g = s_ref[...].astype(F32)
        for i in range(n_got):
            g = g + g_ref[i].astype(F32)
        delta, m_new, v_new = _adamw_math(g, w_ref[...], m_ref[...], v_ref[...])
        go_ref[...] = g
        d_ref[...] = delta
        nm_ref[...] = m_new
        nv_ref[...] = v_new

    blk = pl.BlockSpec((tr, c), lambda i, chip_ref: (i, 0))
    out = jax.ShapeDtypeStruct((r, c), F32)
    return pl.pallas_call(
        body, name=name,
        grid_spec=pltpu.PrefetchScalarGridSpec(
            num_scalar_prefetch=1, grid=(r // tr,),
            in_specs=[pl.BlockSpec((None, tr, c), lambda i, chip_ref: (chip_ref[0], i, 0)),
                      pl.BlockSpec((n_got, tr, c), lambda i, chip_ref: (0, i, 0)), blk, blk, blk],
            out_specs=[blk, blk, blk, blk]),
        out_shape=[out, out, out, out],
        compiler_params=_params(("parallel",)),
    )(chip, sums, got, w, m, v)


def _adamw_replicated(parts, w, m, v):
    p, r, c = parts.shape

    def body(p_ref, w_ref, m_ref, v_ref, g_ref, d_ref, nm_ref, nv_ref):
        g = p_ref[0]
        for i in range(1, p):
            g = g + p_ref[i]
        delta, m_new, v_new = _adamw_math(g, w_ref[...], m_ref[...], v_ref[...])
        g_ref[...] = g
        d_ref[...] = delta
        nm_ref[...] = m_new
        nv_ref[...] = v_new

    blk = pl.BlockSpec((r, c), lambda i: (0, 0))
    out = jax.ShapeDtypeStruct((r, c), F32)
    return pl.pallas_call(
        body, name="adamw_replicated", grid=(1,),
        in_specs=[pl.BlockSpec((p, r, c), lambda i: (0, 0, 0)), blk, blk, blk],
        out_specs=[blk, blk, blk, blk], out_shape=[out, out, out, out],
        compiler_params=_params(("arbitrary",)),
    )(parts, w, m, v)


SHARDED_NAMES = ("w_in", "ml_conv_w", "w_out", "ca_wq", "ca_wkv", "ca_wo", "ffn_w_up", "ffn_conv_w", "ffn_w_down")
SMALL_NAMES = ("b_in", "hg_lb_logits", "hg_norm_w", "ml_conv_b", "ml_norm_w", "ln1_g", "ln1_b",
               "ln2_g", "ln2_b", "ffn_conv_b", "ln3_g", "ln3_b")
WEIGHT_NAMES = ("w_in", "b_in", "hg_lb_logits", "hg_norm_w", "ml_conv_w", "ml_conv_b", "ml_norm_w", "w_out",
                "ln1_g", "ln1_b", "ca_wq", "ca_wkv", "ca_wo", "ln2_g", "ln2_b", "ffn_w_up", "ffn_conv_w",
                "ffn_conv_b", "ffn_w_down", "ln3_g", "ln3_b")
PAD_TO = {"ffn_w_up": UP_SHARD_P, "ffn_conv_w": UP_SHARD_P}
SMALL_ROWS = 24
SMALL_W = D_MODEL


def _shard_2d(name, block):
    t = block[0]
    if name in PAD_TO:
        t = jnp.pad(t, ((0, 0), (0, PAD_TO[name] - t.shape[1])))
    return t


def _shard_like(name, t, like):
    return t[:, :like.shape[2]][None]


def _pad_cols(t, width):
    return jnp.pad(t, ((0, 0), (0, width - t.shape[1])))


FIRST_NAMES = ("w_in", "ml_conv_w")
FFN_NAMES = ("ffn_w_up", "ffn_w_down", "ffn_conv_w")
MID_NAMES = ("ca_wo", "ca_wq", "ca_wkv", "w_out")


def _first_weights(g, small):
    w = dict(small)
    w_in = jnp.concatenate([g["w_in"][j] for j in range(N_DEV)], axis=1)
    w["w_in_main"] = w_in[:, :D_IN_MAIN]
    w["w_in_gate"] = _pad_cols(w_in[:, D_IN_MAIN:], LANES)
    w["b_in_main"] = small["b_in"][:, :D_IN_MAIN]
    w["b_in_gate"] = _pad_cols(small["b_in"][:, D_IN_MAIN:], LANES)
    w["ml_conv_w"] = jnp.transpose(g["ml_conv_w"], (1, 0, 2)).reshape(ML_CONV, 2 * D_GROUP)
    return w


def _mid_weights(g):
    w = {n: g[n].reshape(D_MODEL, D_MODEL) for n in ("w_out", "ca_wq", "ca_wo")}
    w["ca_wkv"] = g["ca_wkv"]
    return w


FFN_UP_NAMES = ("ffn_w_up", "ffn_conv_w")
FFN_DOWN_NAMES = ("ffn_w_down",)


def _ffn_up_weights(g, small):
    w = {"ffn_w_up": g["ffn_w_up"]}
    w["ffn_conv_w"] = jnp.transpose(g["ffn_conv_w"], (1, 0, 2)).reshape(FFN_CONV, D_UP_P)
    w["ffn_conv_b"] = _pad_cols(small["ffn_conv_b"].reshape(N_DEV, UP_SHARD), UP_SHARD_P).reshape(1, D_UP_P)
    return w


def _ffn_down_weights(g):
    down = g["ffn_w_down"].reshape(N_DEV // 2, UP_SHARD, D_MODEL)
    return {"ffn_w_down": jnp.pad(down, ((0, 0), (0, UP_SHARD_P - UP_SHARD), (0, 0))).reshape(D_FF_P, D_MODEL)}


def _whole_weights(g, small):
    return {**_first_weights(g, small), **_mid_weights(g), **_ffn_up_weights(g, small), **_ffn_down_weights(g)}


def _owner_stack(n, grads):
    if n == "w_in":
        w_in = jnp.concatenate([grads["w_in_main"], grads["w_in_gate"][:, :D_IN - D_IN_MAIN]], axis=1)
        return jnp.stack([w_in[:, j * W_IN_SHARD:(j + 1) * W_IN_SHARD] for j in range(N_DEV)])
    if n in ("w_out", "ca_wq", "ca_wo"):
        return grads[n].reshape(N_DEV, D_MODEL // N_DEV, D_MODEL)
    if n == "ffn_w_down":
        down = grads[n].reshape(N_DEV // 2, UP_SHARD_P, D_MODEL)[:, :UP_SHARD]
        return down.reshape(N_DEV, D_FF // N_DEV, D_MODEL)
    if n == "ml_conv_w":
        return jnp.transpose(grads[n].reshape(ML_CONV, N_DEV, LANES), (1, 0, 2))
    if n == "ffn_conv_w":
        return jnp.transpose(grads[n].reshape(FFN_CONV, N_DEV, UP_SHARD_P), (1, 0, 2))
    return grads[n]


def _owner_stacks(grads):
    return {n: _owner_stack(n, grads) for n in SHARDED_NAMES}


def _small_grads(grads):
    out = {n: grads[n] for n in SMALL_NAMES if n in grads}
    out["b_in"] = jnp.concatenate([grads["b_in_main"], grads["b_in_gate"][:, :D_IN - D_IN_MAIN]], axis=1)
    out["ffn_conv_b"] = grads["ffn_conv_b"].reshape(N_DEV, UP_SHARD_P)[:, :UP_SHARD].reshape(1, D_UP)
    return out


def _pack_small(p, extra=None):
    flat = [p[n].reshape(-1) for n in SMALL_NAMES]
    if extra is not None:
        flat.append(extra.reshape(-1))
    flat = jnp.concatenate(flat)
    return jnp.pad(flat, (0, SMALL_ROWS * SMALL_W - flat.shape[0])).reshape(SMALL_ROWS, SMALL_W)


def _unpack_small(slab, like):
    out = {}
    flat = slab.reshape(-1)
    o = 0
    for n in SMALL_NAMES:
        out[n] = flat[o:o + like[n].size].reshape(like[n].shape)
        o += like[n].size
    return out, flat[o]


def kernel(x, mem, w_in, b_in, hg_lb_logits, hg_norm_w, ml_conv_w, ml_conv_b, ml_norm_w, w_out, ln1_g, ln1_b, ca_wq, ca_wkv, ca_wo, ln2_g, ln2_b, ffn_w_up, ffn_conv_w, ffn_conv_b, ffn_w_down, ln3_g, ln3_b, loss_target, m_w_in, m_b_in, m_hg_lb_logits, m_hg_norm_w, m_ml_conv_w, m_ml_conv_b, m_ml_norm_w, m_w_out, m_ln1_g, m_ln1_b, m_ca_wq, m_ca_wkv, m_ca_wo, m_ln2_g, m_ln2_b, m_ffn_w_up, m_ffn_conv_w, m_ffn_conv_b, m_ffn_w_down, m_ln3_g, m_ln3_b, v_w_in, v_b_in, v_hg_lb_logits, v_hg_norm_w, v_ml_conv_w, v_ml_conv_b, v_ml_norm_w, v_w_out, v_ln1_g, v_ln1_b, v_ca_wq, v_ca_wkv, v_ca_wo, v_ln2_g, v_ln2_b, v_ffn_w_up, v_ffn_conv_w, v_ffn_conv_b, v_ffn_w_down, v_ln3_g, v_ln3_b):
    params = dict(w_in=w_in, b_in=b_in, hg_lb_logits=hg_lb_logits, hg_norm_w=hg_norm_w, ml_conv_w=ml_conv_w,
                  ml_conv_b=ml_conv_b, ml_norm_w=ml_norm_w, w_out=w_out, ln1_g=ln1_g, ln1_b=ln1_b, ca_wq=ca_wq,
                  ca_wkv=ca_wkv, ca_wo=ca_wo, ln2_g=ln2_g, ln2_b=ln2_b, ffn_w_up=ffn_w_up, ffn_conv_w=ffn_conv_w,
                  ffn_conv_b=ffn_conv_b, ffn_w_down=ffn_w_down, ln3_g=ln3_g, ln3_b=ln3_b)
    mom1 = dict(w_in=m_w_in, b_in=m_b_in, hg_lb_logits=m_hg_lb_logits, hg_norm_w=m_hg_norm_w,
                ml_conv_w=m_ml_conv_w, ml_conv_b=m_ml_conv_b, ml_norm_w=m_ml_norm_w, w_out=m_w_out, ln1_g=m_ln1_g,
                ln1_b=m_ln1_b, ca_wq=m_ca_wq, ca_wkv=m_ca_wkv, ca_wo=m_ca_wo, ln2_g=m_ln2_g, ln2_b=m_ln2_b,
                ffn_w_up=m_ffn_w_up, ffn_conv_w=m_ffn_conv_w, ffn_conv_b=m_ffn_conv_b, ffn_w_down=m_ffn_w_down,
                ln3_g=m_ln3_g, ln3_b=m_ln3_b)
    mom2 = dict(w_in=v_w_in, b_in=v_b_in, hg_lb_logits=v_hg_lb_logits, hg_norm_w=v_hg_norm_w,
                ml_conv_w=v_ml_conv_w, ml_conv_b=v_ml_conv_b, ml_norm_w=v_ml_norm_w, w_out=v_w_out, ln1_g=v_ln1_g,
                ln1_b=v_ln1_b, ca_wq=v_ca_wq, ca_wkv=v_ca_wkv, ca_wo=v_ca_wo, ln2_g=v_ln2_g, ln2_b=v_ln2_b,
                ffn_w_up=v_ffn_w_up, ffn_conv_w=v_ffn_conv_w, ffn_conv_b=v_ffn_conv_b, ffn_w_down=v_ffn_w_down,
                ln3_g=v_ln3_g, ln3_b=v_ln3_b)

    x_idx, y_idx, c_idx = _coords()
    as_index = lambda v: jnp.reshape(v, (1,)).astype(jnp.int32)
    me = as_index(4 * x_idx + 2 * y_idx + c_idx)
    small_params = {n: params[n] for n in SMALL_NAMES}

    shards = {n: _shard_2d(n, params[n]) for n in SHARDED_NAMES}
    m_shards = {n: _shard_2d(n, mom1[n]) for n in SHARDED_NAMES}
    v_shards = {n: _shard_2d(n, mom2[n]) for n in SHARDED_NAMES}
    small_slabs = [_pack_small(params), _pack_small(mom1), _pack_small(mom2)]
    outgoing = {n: shards[n] if "conv" in n else shards[n].astype(BF16) for n in SHARDED_NAMES}
    to_send = lambda names: [outgoing[n] for n in names]
    glue = [*m_shards.values(), *v_shards.values(), *small_slabs, *shards.values(),
            *[outgoing[n] for n in SHARDED_NAMES if n not in FIRST_NAMES]]
    first = dict(zip(FIRST_NAMES, _two_level_gather(to_send(FIRST_NAMES), glue, "weights_gather_first")))
    mid_started, through = _direct_start(True, to_send(MID_NAMES), first["w_in"], "weights_gather_start_mid")
    ffn_started, through = _direct_start(True, to_send(FFN_UP_NAMES), through, "weights_gather_start_ffn_up")
    down_started, first["w_in"] = _direct_start(True, to_send(FFN_DOWN_NAMES), through,
                                                "weights_gather_start_ffn_down")

    def gathered_weights(names, started, after, tag):
        mine, lands = _direct_wait(True, started, after, "weights_gather_wait_" + tag)
        return {n: lax.dynamic_update_index_in_dim(land, own, me[0], 0) for n, own, land in zip(names, mine, lands)}

    started, own_stacks = {}, {}

    def start_group(names, tag):
        def hook(grads, through):
            own_stacks[tag] = [_owner_stack(n, grads).astype(BF16) for n in names]
            started[tag], through = _direct_start(False, own_stacks[tag], through, "grads_start_" + tag)
            return through
        return hook

    def start_small(grads, loss, through):
        started["small"], through = _direct_start(True, [_pack_small(_small_grads(grads), loss)], through,
                                                  "small_gather_start")
        return through

    loss, grad_x, grads = _local_step(
        x[0], mem[0], loss_target[0], _first_weights(first, small_params),
        lambda y: _mid_weights(gathered_weights(MID_NAMES, mid_started, y, "mid")),
        lambda x2: _ffn_up_weights(gathered_weights(FFN_UP_NAMES, ffn_started, x2, "ffn_up"), small_params),
        lambda hid: _ffn_down_weights(gathered_weights(FFN_DOWN_NAMES, down_started, hid, "ffn_down")),
        start_group(FFN_NAMES, "ffn"), start_group(MID_NAMES, "mid"), start_small, start_group(FIRST_NAMES, "last"))

    sharded_out = {}

    def update_group(names, tag, after):
        _, lands = _direct_wait(False, started[tag], after, "grads_wait_" + tag)
        for n, st, land in zip(names, own_stacks[tag], lands):
            res = _adamw_sharded(me, st, land, shards[n], m_shards[n], v_shards[n], "adamw_" + n)
            sharded_out[n] = [_shard_like(n, t, params[n]) for t in res]

    update_group(FFN_NAMES, "ffn", grad_x)
    update_group(MID_NAMES, "mid", grad_x)
    own_small, small_lands = _direct_wait(True, started["small"], grad_x, "small_gather_wait")
    small_parts = lax.dynamic_update_index_in_dim(small_lands[0], own_small[0], me[0], 0)
    small_res = _adamw_replicated(small_parts, *small_slabs)
    small_out = [_unpack_small(slab, params) for slab in small_res]
    done = [t for n in FFN_NAMES + MID_NAMES for t in sharded_out[n]]
    done += [t for small, _ in small_out for t in small.values()]
    update_group(FIRST_NAMES, "last", done)

    outs = []
    for k, (small, _) in enumerate(small_out):
        outs.extend(sharded_out[n][k] if n in sharded_out else small[n] for n in WEIGHT_NAMES)
    return (small_out[0][1], grad_x[None], *outs)
```

```python
import functools
import math

import jax
import jax.numpy as jnp
from jax import lax
from jax.experimental import pallas as pl
from jax.experimental.pallas import tpu as pltpu

F32 = jnp.float32
BF16 = jnp.bfloat16
HIGHEST = lax.Precision.HIGHEST
MESH = pl.DeviceIdType.MESH

N_DEV = 8
D_MODEL = 1024
N_MEM = 256
N_HEADS = 4
D_HEAD = 128
D_GROUP = N_HEADS * D_HEAD
CHUNK = 64
ML_CONV = 4
FFN_CONV = 3
D_FF = 2816
D_UP = 2 * D_FF
CA_HEADS = 4
CA_DH = D_MODEL // CA_HEADS
LANES = 128
SUBLANES = 8
D_IN = 8 * D_GROUP + 2 * N_HEADS
D_IN_MAIN = 8 * D_GROUP
W_IN_SHARD = D_IN // N_DEV
UP_SHARD = D_UP // N_DEV
UP_SHARD_P = 768
D_UP_P = N_DEV * UP_SHARD_P
D_FF_P = D_UP_P // 2
ALPHA = 2.0 ** 0.25
LN_EPS = 1e-5
NEG_BIG = -1e30
ADAM_LR = 0.001
ADAM_B1 = 0.9
ADAM_B2 = 0.999
ADAM_EPS = 1e-08
ADAM_WD = 0.01
ADAM_STEP = 10
VMEM_LIMIT = 56 * 1024 * 1024

SEG_HQ, SEG_HF, SEG_HI, SEG_HG, SEG_MQ, SEG_MK, SEG_MV, SEG_MO = (4 * i for i in range(8))


def _params(sem):
    return pltpu.CompilerParams(dimension_semantics=sem, vmem_limit_bytes=VMEM_LIMIT)


def _dg(a, b, ca, cb, precision=None):
    return lax.dot_general(a, b, (((ca,), (cb,)), ((), ())), precision=precision,
                           preferred_element_type=F32)


def _nn_raw(a, b):
    return _dg(a.astype(BF16), b.astype(BF16), 1, 0)


def _nt_raw(a, b):
    return _dg(a.astype(BF16), b.astype(BF16), 1, 1)


def _tn_raw(a, b):
    return _dg(a.astype(BF16), b.astype(BF16), 0, 0)


@jax.custom_vjp
def _nn(a, b):
    return _nn_raw(a, b)


_nn.defvjp(lambda a, b: (_nn_raw(a, b), (a, b)),
           lambda res, g: (_nt_raw(g, res[1]), _tn_raw(res[0], g)))


@jax.custom_vjp
def _nt(a, b):
    return _nt_raw(a, b)


_nt.defvjp(lambda a, b: (_nt_raw(a, b), (a, b)),
           lambda res, g: (_nn_raw(g, res[1]), _tn_raw(g, res[0])))


@jax.custom_vjp
def _tn(a, b):
    return _tn_raw(a, b)


_tn.defvjp(lambda a, b: (_tn_raw(a, b), (a, b)),
           lambda res, g: (_nt_raw(res[1], g), _nn_raw(res[0], g)))


def _layer_norm(z, g, b):
    mu = jnp.mean(z, axis=-1, keepdims=True)
    var = jnp.mean(jnp.square(z - mu), axis=-1, keepdims=True)
    return (z - mu) * lax.rsqrt(var + LN_EPS) * g + b


def _matmul_nn(a, w, bias, tm, tn, name, out_dtype=F32):
    m, k = a.shape
    if w.ndim == 3:
        n = w.shape[0] * w.shape[2]
        assert tn == w.shape[2]
        w_spec = pl.BlockSpec((None, k, tn), lambda i, j: (j, 0, 0))
    else:
        n = w.shape[1]
        w_spec = pl.BlockSpec((k, tn), lambda i, j: (0, j))

    def body(*refs):
        a_ref, w_ref = refs[0], refs[1]
        o_ref = refs[-1]
        acc = _nn_raw(a_ref[...], w_ref[...])
        if bias is not None:
            acc = acc + refs[2][...]
        o_ref[...] = acc.astype(o_ref.dtype)

    in_specs = [pl.BlockSpec((tm, k), lambda i, j: (i, 0)), w_spec]
    args = [a, w]
    if bias is not None:
        in_specs.append(pl.BlockSpec((1, tn), lambda i, j: (0, j)))
        args.append(bias)
    return pl.pallas_call(
        body, name=name, grid=(m // tm, n // tn), in_specs=in_specs,
        out_specs=pl.BlockSpec((tm, tn), lambda i, j: (i, j)),
        out_shape=jax.ShapeDtypeStruct((m, n), out_dtype),
        compiler_params=_params(("parallel", "parallel")),
    )(*args)


def _matmul_nt(pairs, add, scale, tm, tk, name, out_dtype=F32):
    m = pairs[0][0].shape[0]
    k = pairs[0][1].shape[-2]
    groups = []
    in_specs, args = [], []
    for pair in pairs:
        d, w = pair[0], pair[1]
        in_specs.append(pl.BlockSpec((tm, d.shape[1]), lambda i, j: (i, 0)))
        if w.ndim == 3:
            g = d.shape[1] // w.shape[2]
            blk = pair[2] // g
            in_specs.append(pl.BlockSpec((g, tk, w.shape[2]), lambda i, j, blk=blk: (blk, j, 0)))
            groups.append((g, w.shape[2]))
        else:
            in_specs.append(pl.BlockSpec((tk, w.shape[1]), lambda i, j: (j, 0)))
            groups.append(None)
        args += [d, w]
    if add is not None:
        in_specs.append(pl.BlockSpec((tm, tk), lambda i, j: (i, j)))
        args.append(add)

    def body(*refs):
        o_ref = refs[-1]
        acc = None
        for p, grp in enumerate(groups):
            d_ref, w_ref = refs[2 * p], refs[2 * p + 1]
            if grp is None:
                terms = [_nt_raw(d_ref[...], w_ref[...])]
            else:
                terms = [_nt_raw(d_ref[:, g * grp[1]:(g + 1) * grp[1]], w_ref[g]) for g in range(grp[0])]
            for t in terms:
                acc = t if acc is None else acc + t
        if add is not None:
            acc = acc + scale * refs[2 * len(groups)][...]
        o_ref[...] = acc.astype(o_ref.dtype)

    return pl.pallas_call(
        body, name=name, grid=(m // tm, k // tk), in_specs=in_specs,
        out_specs=pl.BlockSpec((tm, tk), lambda i, j: (i, j)),
        out_shape=jax.ShapeDtypeStruct((m, k), out_dtype),
        compiler_params=_params(("parallel", "parallel")),
    )(*args)


def _matmul_tn(a, b, tm, tn, tt, name, shards=None, shard0=0, group=1, into=None, colsum=False):
    t, m = a.shape
    n = b.shape[1]
    assert not colsum or tm == m
    n_in = 2 + (into is not None)
    out_dtype = BF16
    per_step = 1 if shards is None else group
    width = per_step * tn

    def body(*refs):
        a_ref, b_ref = refs[0], refs[1]
        o_ref, acc_ref = refs[n_in], refs[-1]
        first = pl.program_id(2) == 0

        @pl.when(first)
        def _():
            acc_ref[...] = jnp.zeros_like(acc_ref)

        if shards is None:
            acc_ref[...] += _tn_raw(a_ref[...], b_ref[...])
        else:
            lhs = a_ref[...].astype(BF16)
            for g in range(per_step):
                acc_ref[g] += _tn_raw(lhs, b_ref[:, g * tn:(g + 1) * tn])

        @pl.when(pl.program_id(2) == t // tt - 1)
        def _():
            o_ref[...] = acc_ref[...].astype(o_ref.dtype)

        if colsum:
            s_ref = refs[n_in + 1]

            @pl.when(first)
            def _():
                s_ref[...] = jnp.zeros_like(s_ref)

            s_ref[...] += jnp.sum(b_ref[...], axis=0, keepdims=True)

    in_specs = [pl.BlockSpec((tt, tm), lambda i, j, kk: (kk, i)),
                pl.BlockSpec((tt, width), lambda i, j, kk: (kk, j))]
    args = [a, b]
    aliases = {}
    if into is not None:
        in_specs.append(pl.BlockSpec(memory_space=pl.ANY))
        args.append(into)
        aliases = {2: 0}
    if shards is None:
        out_specs = [pl.BlockSpec((tm, tn), lambda i, j, kk: (i, j))]
        out_shape = [jax.ShapeDtypeStruct((m, n), out_dtype)]
        acc = pltpu.VMEM((tm, tn), F32)
    else:
        out_specs = [pl.BlockSpec((per_step, tm, tn), lambda i, j, kk: (shard0 // per_step + j, i, 0))]
        out_shape = [jax.ShapeDtypeStruct((shards, m, tn), out_dtype)]
        acc = pltpu.VMEM((per_step, tm, tn), F32)
    if colsum:
        out_specs.append(pl.BlockSpec((1, tn), lambda i, j, kk: (0, j)))
        out_shape.append(jax.ShapeDtypeStruct((1, n), F32))
    res = pl.pallas_call(
        body, name=name, grid=(m // tm, n // width, t // tt), in_specs=in_specs, out_specs=out_specs,
        out_shape=out_shape, input_output_aliases=aliases, scratch_shapes=[acc],
        compiler_params=_params(("parallel", "parallel", "arbitrary")),
    )(*args)
    return res if colsum else res[0]


ROW_TILE = 64


def _stack(ref, start, rows):
    return ref[pl.ds(start, rows), :].astype(F32).reshape(rows // SUBLANES, SUBLANES, LANES)


def _vreg_rows(ref, n):
    return [jnp.broadcast_to(ref[j:j + 1, :], (SUBLANES, LANES))[None] for j in range(n)]


def _column_total(acc):
    return jnp.sum(acc, axis=0, keepdims=True)


def _conv_fwd_tile(pad_ref, taps_w, bias, r0, rows):
    taps = len(taps_w)
    acc = bias
    for j in range(taps):
        acc = acc + _stack(pad_ref, SUBLANES - (taps - 1 - j) + r0, rows) * taps_w[j]
    return acc


def _conv_grads_tile(pad_ref, dpad_ref, dx_ref, taps_w, dws, r0, rows):
    taps = len(taps_w)
    x_rows = _stack(pad_ref, SUBLANES + r0, rows)
    dx = None
    for j in range(taps):
        d_shifted = _stack(dpad_ref, r0 + (taps - 1 - j), rows)
        term = d_shifted * taps_w[j]
        dx = term if dx is None else dx + term
        dws[j] = dws[j] + jnp.sum(d_shifted * x_rows, axis=0)
    dx_ref[r0:r0 + rows, :] = dx.reshape(rows, LANES).astype(dx_ref.dtype)
    return jnp.sum(dx, axis=0)


def _ml_conv_fwd(proj, conv_w, conv_b):
    s = proj.shape[0]
    nblk = 2 * D_GROUP // LANES

    def body(x_ref, w_ref, b_ref, o_ref, pad_ref):
        pad_ref[0:SUBLANES, :] = jnp.zeros((SUBLANES, LANES), F32)
        pad_ref[SUBLANES:, :] = x_ref[...].astype(F32)
        taps_w, bias = _vreg_rows(w_ref, ML_CONV), _vreg_rows(b_ref, 1)[0]
        for r0 in range(0, s, ROW_TILE):
            rows = min(ROW_TILE, s - r0)
            o_ref[r0:r0 + rows, :] = jax.nn.silu(_conv_fwd_tile(pad_ref, taps_w, bias, r0, rows)).reshape(rows, LANES)

    return pl.pallas_call(
        body, name="ml_conv_fwd", grid=(nblk,),
        in_specs=[pl.BlockSpec((s, LANES), lambda j: (0, SEG_MQ + j)),
                  pl.BlockSpec((ML_CONV, LANES), lambda j: (0, j)),
                  pl.BlockSpec((1, LANES), lambda j: (0, j))],
        out_specs=pl.BlockSpec((s, LANES), lambda j: (0, j)),
        out_shape=jax.ShapeDtypeStruct((s, 2 * D_GROUP), F32),
        scratch_shapes=[pltpu.VMEM((s + SUBLANES, LANES), F32)],
        compiler_params=_params(("parallel",)),
    )(proj, conv_w, conv_b)


def _ml_conv_bwd(proj, conv_w, conv_b, d_qk, d_proj):
    s = proj.shape[0]
    nblk = 2 * D_GROUP // LANES

    def body(x_ref, w_ref, b_ref, dy_ref, _, dx_ref, dw_ref, db_ref, dxs_ref, pad_ref, dpad_ref):
        pad_ref[0:SUBLANES, :] = jnp.zeros((SUBLANES, LANES), F32)
        pad_ref[SUBLANES:, :] = x_ref[...].astype(F32)
        dpad_ref[s:, :] = jnp.zeros((SUBLANES, LANES), F32)
        taps_w, bias = _vreg_rows(w_ref, ML_CONV), _vreg_rows(b_ref, 1)[0]
        db = jnp.zeros((SUBLANES, LANES), F32)
        for r0 in range(0, s, ROW_TILE):
            rows = min(ROW_TILE, s - r0)
            pre = _conv_fwd_tile(pad_ref, taps_w, bias, r0, rows)
            _, vjp = jax.vjp(jax.nn.silu, pre)
            d_pre, = vjp(_stack(dy_ref, r0, rows))
            dpad_ref[r0:r0 + rows, :] = d_pre.reshape(rows, LANES)
            db = db + jnp.sum(d_pre, axis=0)
        db_ref[...] = _column_total(db)
        dws = [jnp.zeros((SUBLANES, LANES), F32) for _ in range(ML_CONV)]
        dx_sum = jnp.zeros((SUBLANES, LANES), F32)
        for r0 in range(0, s, ROW_TILE):
            dx_sum = dx_sum + _conv_grads_tile(pad_ref, dpad_ref, dx_ref, taps_w, dws, r0, min(ROW_TILE, s - r0))
        dxs_ref[...] = _column_total(dx_sum)
        for j in range(ML_CONV):
            dw_ref[j:j + 1, :] = _column_total(dws[j])

    return pl.pallas_call(
        body, name="ml_conv_bwd", grid=(nblk,),
        in_specs=[pl.BlockSpec((s, LANES), lambda j: (0, SEG_MQ + j)),
                  pl.BlockSpec((ML_CONV, LANES), lambda j: (0, j)),
                  pl.BlockSpec((1, LANES), lambda j: (0, j)),
                  pl.BlockSpec((s, LANES), lambda j: (0, j)),
                  pl.BlockSpec(memory_space=pl.ANY)],
        out_specs=[pl.BlockSpec((s, LANES), lambda j: (0, SEG_MQ + j)),
                   pl.BlockSpec((ML_CONV, LANES), lambda j: (0, j)),
                   pl.BlockSpec((1, LANES), lambda j: (0, j)),
                   pl.BlockSpec((1, LANES), lambda j: (0, j))],
        out_shape=[jax.ShapeDtypeStruct(d_proj.shape, d_proj.dtype),
                   jax.ShapeDtypeStruct((ML_CONV, 2 * D_GROUP), F32),
                   jax.ShapeDtypeStruct((1, 2 * D_GROUP), F32),
                   jax.ShapeDtypeStruct((1, 2 * D_GROUP), F32)],
        input_output_aliases={4: 0},
        scratch_shapes=[pltpu.VMEM((s + SUBLANES, LANES), F32), pltpu.VMEM((s + SUBLANES, LANES), F32)],
        compiler_params=_params(("parallel",)),
    )(proj, conv_w, conv_b, d_qk, d_proj)


def _gelu_mul(a, b):
    return jax.nn.gelu(a) * b


GELU_C = math.sqrt(2.0 / math.pi)
GELU_K = 0.044715


def _gelu_mul_grads(a, b, d):
    a2 = a * a
    t = jnp.tanh(GELU_C * (a + GELU_K * (a * a2)))
    cdf = 0.5 * (1.0 + t)
    slope = cdf + (0.5 * GELU_C) * a * (1.0 - t * t) * (1.0 + (3.0 * GELU_K) * a2)
    return d * b * slope, d * (a * cdf)


FFN_BLOCKS = D_FF_P // LANES


def _ffn_conv_fwd(u, conv_w, conv_b):
    s = u.shape[0]

    def body(g_ref, v_ref, wg_ref, wv_ref, bg_ref, bv_ref, o_ref, gpad_ref, vpad_ref):
        for pad_ref, x_ref in ((gpad_ref, g_ref), (vpad_ref, v_ref)):
            pad_ref[0:SUBLANES, :] = jnp.zeros((SUBLANES, LANES), F32)
            pad_ref[SUBLANES:, :] = x_ref[...].astype(F32)
        taps_g, bias_g = _vreg_rows(wg_ref, FFN_CONV), _vreg_rows(bg_ref, 1)[0]
        taps_v, bias_v = _vreg_rows(wv_ref, FFN_CONV), _vreg_rows(bv_ref, 1)[0]
        for r0 in range(0, s, ROW_TILE):
            rows = min(ROW_TILE, s - r0)
            ug = _conv_fwd_tile(gpad_ref, taps_g, bias_g, r0, rows)
            uv = _conv_fwd_tile(vpad_ref, taps_v, bias_v, r0, rows)
            o_ref[r0:r0 + rows, :] = _gelu_mul(ug, uv).reshape(rows, LANES).astype(o_ref.dtype)

    col = lambda off: (lambda j: (0, off + j))
    return pl.pallas_call(
        body, name="ffn_conv_fwd", grid=(FFN_BLOCKS,),
        in_specs=[pl.BlockSpec((s, LANES), col(0)), pl.BlockSpec((s, LANES), col(FFN_BLOCKS)),
                  pl.BlockSpec((FFN_CONV, LANES), col(0)), pl.BlockSpec((FFN_CONV, LANES), col(FFN_BLOCKS)),
                  pl.BlockSpec((1, LANES), col(0)), pl.BlockSpec((1, LANES), col(FFN_BLOCKS))],
        out_specs=pl.BlockSpec((s, LANES), col(0)),
        out_shape=jax.ShapeDtypeStruct((s, D_FF_P), BF16),
        scratch_shapes=[pltpu.VMEM((s + SUBLANES, LANES), F32), pltpu.VMEM((s + SUBLANES, LANES), F32)],
        compiler_params=_params(("parallel",)),
    )(u, u, conv_w, conv_w, conv_b, conv_b)


def _ffn_conv_bwd(u, conv_w, conv_b, d_h):
    s = u.shape[0]

    def body(g_ref, v_ref, wg_ref, wv_ref, bg_ref, bv_ref, dh_ref,
             dug_ref, duv_ref, dwg_ref, dwv_ref, dbg_ref, dbv_ref,
             gpad_ref, vpad_ref, dgpad_ref, dvpad_ref):
        for pad_ref, x_ref in ((gpad_ref, g_ref), (vpad_ref, v_ref)):
            pad_ref[0:SUBLANES, :] = jnp.zeros((SUBLANES, LANES), F32)
            pad_ref[SUBLANES:, :] = x_ref[...].astype(F32)
        dgpad_ref[s:, :] = jnp.zeros((SUBLANES, LANES), F32)
        dvpad_ref[s:, :] = jnp.zeros((SUBLANES, LANES), F32)
        taps_g, bias_g = _vreg_rows(wg_ref, FFN_CONV), _vreg_rows(bg_ref, 1)[0]
        taps_v, bias_v = _vreg_rows(wv_ref, FFN_CONV), _vreg_rows(bv_ref, 1)[0]
        dbg = jnp.zeros((SUBLANES, LANES), F32)
        dbv = jnp.zeros((SUBLANES, LANES), F32)
        for r0 in range(0, s, ROW_TILE):
            rows = min(ROW_TILE, s - r0)
            ug = _conv_fwd_tile(gpad_ref, taps_g, bias_g, r0, rows)
            uv = _conv_fwd_tile(vpad_ref, taps_v, bias_v, r0, rows)
            d_ug, d_uv = _gelu_mul_grads(ug, uv, _stack(dh_ref, r0, rows))
            dgpad_ref[r0:r0 + rows, :] = d_ug.reshape(rows, LANES)
            dvpad_ref[r0:r0 + rows, :] = d_uv.reshape(rows, LANES)
            dbg = dbg + jnp.sum(d_ug, axis=0)
            dbv = dbv + jnp.sum(d_uv, axis=0)
        dbg_ref[...] = _column_total(dbg)
        dbv_ref[...] = _column_total(dbv)
        for pad_ref, dpad_ref, taps_w, dx_ref, dw_ref in ((gpad_ref, dgpad_ref, taps_g, dug_ref, dwg_ref),
                                                          (vpad_ref, dvpad_ref, taps_v, duv_ref, dwv_ref)):
            dws = [jnp.zeros((SUBLANES, LANES), F32) for _ in range(FFN_CONV)]
            for r0 in range(0, s, ROW_TILE):
                _conv_grads_tile(pad_ref, dpad_ref, dx_ref, taps_w, dws, r0, min(ROW_TILE, s - r0))
            for j in range(FFN_CONV):
                dw_ref[j:j + 1, :] = _column_total(dws[j])

    col = lambda off: (lambda j: (0, off + j))
    seq = pl.BlockSpec((s, LANES), col(0))
    return pl.pallas_call(
        body, name="ffn_conv_bwd", grid=(FFN_BLOCKS,),
        in_specs=[pl.BlockSpec((s, LANES), col(0)), pl.BlockSpec((s, LANES), col(FFN_BLOCKS)),
                  pl.BlockSpec((FFN_CONV, LANES), col(0)), pl.BlockSpec((FFN_CONV, LANES), col(FFN_BLOCKS)),
                  pl.BlockSpec((1, LANES), col(0)), pl.BlockSpec((1, LANES), col(FFN_BLOCKS)), seq],
        out_specs=[seq, seq, pl.BlockSpec((FFN_CONV, LANES), col(0)), pl.BlockSpec((FFN_CONV, LANES), col(0)),
                   pl.BlockSpec((1, LANES), col(0)), pl.BlockSpec((1, LANES), col(0))],
        out_shape=[jax.ShapeDtypeStruct((s, D_FF_P), BF16), jax.ShapeDtypeStruct((s, D_FF_P), BF16),
                   jax.ShapeDtypeStruct((FFN_CONV, D_FF_P), F32), jax.ShapeDtypeStruct((FFN_CONV, D_FF_P), F32),
                   jax.ShapeDtypeStruct((1, D_FF_P), F32), jax.ShapeDtypeStruct((1, D_FF_P), F32)],
        scratch_shapes=[pltpu.VMEM((s + SUBLANES, LANES), F32) for _ in range(4)],
        compiler_params=_params(("parallel",)),
    )(u, u, conv_w, conv_w, conv_b, conv_b, d_h)


def _chunk_masks(c):
    row = lax.broadcasted_iota(jnp.int32, (c, c), 0)
    col = lax.broadcasted_iota(jnp.int32, (c, c), 1)
    return row, col


@jax.custom_vjp
def _split_heads(x):
    return tuple(x[:, h * D_HEAD:(h + 1) * D_HEAD] for h in range(N_HEADS))


_split_heads.defvjp(lambda x: (_split_heads(x), None), lambda _, gs: (jnp.concatenate(gs, axis=1),))


@jax.custom_vjp
def _merge_heads(xs):
    return jnp.concatenate(xs, axis=1)


_merge_heads.defvjp(lambda xs: (_merge_heads(xs), None), lambda _, g: (_split_heads(g),))


@jax.custom_vjp
def _split_chunks(x):
    return tuple(x[i * CHUNK:(i + 1) * CHUNK] for i in range(x.shape[0] // CHUNK))


_split_chunks.defvjp(lambda x: (_split_chunks(x), None), lambda _, gs: (jnp.concatenate(gs, axis=0),))


@jax.custom_vjp
def _merge_chunks(xs):
    return jnp.concatenate(xs, axis=0)


_merge_chunks.defvjp(lambda xs: (_merge_chunks(xs), None), lambda _, g: (_split_chunks(g),))


def _blocks(x):
    return [_split_heads(rows) for rows in _split_chunks(x)]


def _per_chunk_rows(per_chunk, rid):
    out = per_chunk[0]
    for i in range(1, len(per_chunk)):
        out = jnp.where(rid >= i * CHUNK, per_chunk[i], out)
    return out


HEADS = range(N_HEADS)
CHUNKS_PER_STEP = 4
ML_CHUNKS_PER_STEP = 1


def _hg_chunk(hq, hf, hi, hgate, l0, l1, nw, sts):
    n = hq.shape[0] // CHUNK
    causal = _chunk_masks(CHUNK)
    causal = causal[1] <= causal[0]
    mx = lax.stop_gradient(jnp.maximum(l0, l1))
    e0 = jnp.exp(l0 - mx)
    e1 = jnp.exp(l1 - mx)
    lb = e0 / (e0 + e1)
    sig = jax.nn.sigmoid(hf)
    lf = jnp.log(lb + (1.0 - lb) * sig)
    k = (1.0 - lb) * jax.nn.sigmoid(-hf)
    q = jax.nn.silu(hq)
    tri = causal.astype(F32)
    b = _merge_chunks(tuple(_dg(tri, rows, 1, 0, HIGHEST) for rows in _split_chunks(lf)))
    rid = lax.broadcasted_iota(jnp.int32, b.shape, 0)
    pick = lambda r: jnp.sum(jnp.where(rid == r, b, 0.0), axis=0, keepdims=True)
    b_last_c = [pick(i * CHUNK + CHUNK - 1) for i in range(n)]
    b_ref = _per_chunk_rows([pick(i * CHUNK + CHUNK // 2 - 1) for i in range(n)], rid)
    b_last = _per_chunk_rows(b_last_c, rid)
    qa = _blocks(q * jnp.exp(b - b_ref))
    ka = _blocks(k * jnp.exp(b_ref - b))
    qe = _blocks(q * jnp.exp(b))
    kd = _blocks(k * jnp.exp(b_last - b))
    decay = [_split_heads(jnp.exp(b_last_c[i])) for i in range(n)]
    v = _blocks(hi)
    chunks = range(n)
    attn = [[jnp.where(causal, _nt(qa[i][h], ka[i][h]), 0.0) for h in HEADS] for i in chunks]
    intra = [[_nn(attn[i][h], v[i][h]) for h in HEADS] for i in chunks]
    kv = [[_tn(v[i][h], kd[i][h]) for h in HEADS] for i in chunks]
    normed = []
    for i in chunks:
        inter = [_nt(qe[i][h], sts[h]) for h in HEADS]
        sts = tuple(decay[i][h] * sts[h] + kv[i][h] for h in HEADS)
        o = [intra[i][h] + inter[h] for h in HEADS]
        normed.append(_merge_heads(tuple(o[h] * lax.rsqrt(jnp.mean(o[h] * o[h], axis=-1, keepdims=True) + LN_EPS)
                                         for h in HEADS)))
    return _merge_chunks(tuple(normed)) * nw * jax.nn.silu(hgate), sts


def _seg(ref, seg):
    return ref[:, seg * D_GROUP:(seg + 1) * D_GROUP]


def _hgrn2_fwd(proj, logits, norm_w):
    s = proj.shape[0]
    rows = CHUNKS_PER_STEP * CHUNK
    nc = s // rows

    def body(p_ref, lg_ref, nw_ref, y_ref, st_out_ref, st_scr):
        @pl.when(pl.program_id(0) == 0)
        def _():
            st_scr[...] = jnp.zeros_like(st_scr)

        sts = tuple(st_scr[h] for h in HEADS)
        y, sts_new = _hg_chunk(_seg(p_ref, 0), _seg(p_ref, 1), _seg(p_ref, 2), _seg(p_ref, 3),
                               lg_ref[0:1, :], lg_ref[1:2, :], nw_ref[...], sts)
        y_ref[...] = y.astype(y_ref.dtype)
        for h in HEADS:
            st_out_ref[h] = sts[h]
            st_scr[h] = sts_new[h]

    return pl.pallas_call(
        body, name="hgrn2_fwd", grid=(nc,),
        in_specs=[pl.BlockSpec((rows, 4 * D_GROUP), lambda c: (c, 0)),
                  pl.BlockSpec((2, D_GROUP), lambda c: (0, 0)),
                  pl.BlockSpec((1, D_GROUP), lambda c: (0, 0))],
        out_specs=[pl.BlockSpec((rows, D_GROUP), lambda c: (c, 0)),
                   pl.BlockSpec((None, N_HEADS, D_HEAD, D_HEAD), lambda c: (c, 0, 0, 0))],
        out_shape=[jax.ShapeDtypeStruct((s, 2 * D_GROUP), BF16),
                   jax.ShapeDtypeStruct((nc, N_HEADS, D_HEAD, D_HEAD), F32)],
        scratch_shapes=[pltpu.VMEM((N_HEADS, D_HEAD, D_HEAD), F32)],
        compiler_params=_params(("arbitrary",)),
    )(proj, logits, norm_w)


def _hgrn2_bwd(proj, logits, norm_w, states, d_y):
    s = proj.shape[0]
    rows = CHUNKS_PER_STEP * CHUNK
    nc = s // rows

    def body(p_ref, lg_ref, nw_ref, st_ref, dy_ref, dp_ref, dl_ref, dnw_ref, dsum_ref, dst_scr):
        @pl.when(pl.program_id(0) == 0)
        def _():
            dst_scr[...] = jnp.zeros_like(dst_scr)
            dl_ref[...] = jnp.zeros_like(dl_ref)
            dnw_ref[...] = jnp.zeros_like(dnw_ref)
            dsum_ref[...] = jnp.zeros_like(dsum_ref)

        _, vjp = jax.vjp(_hg_chunk, _seg(p_ref, 0), _seg(p_ref, 1), _seg(p_ref, 2), _seg(p_ref, 3),
                         lg_ref[0:1, :], lg_ref[1:2, :], nw_ref[...], tuple(st_ref[h] for h in HEADS))
        d_hq, d_hf, d_hi, d_hg, d_l0, d_l1, d_nw, d_sts = vjp((dy_ref[...], tuple(dst_scr[h] for h in HEADS)))
        for seg, val in enumerate((d_hq, d_hf, d_hi, d_hg)):
            dp_ref[:, seg * D_GROUP:(seg + 1) * D_GROUP] = val.astype(dp_ref.dtype)
            dsum_ref[:, seg * D_GROUP:(seg + 1) * D_GROUP] += jnp.sum(val, axis=0, keepdims=True)
        dl_ref[0:1, :] += d_l0
        dl_ref[1:2, :] += d_l1
        dnw_ref[...] += d_nw
        for h in HEADS:
            dst_scr[h] = d_sts[h]

    rev = lambda c: nc - 1 - c
    return pl.pallas_call(
        body, name="hgrn2_bwd", grid=(nc,),
        in_specs=[pl.BlockSpec((rows, 4 * D_GROUP), lambda c: (rev(c), 0)),
                  pl.BlockSpec((2, D_GROUP), lambda c: (0, 0)),
                  pl.BlockSpec((1, D_GROUP), lambda c: (0, 0)),
                  pl.BlockSpec((None, N_HEADS, D_HEAD, D_HEAD), lambda c: (rev(c), 0, 0, 0)),
                  pl.BlockSpec((rows, D_GROUP), lambda c: (rev(c), 0))],
        out_specs=[pl.BlockSpec((rows, 4 * D_GROUP), lambda c: (rev(c), 0)),
                   pl.BlockSpec((2, D_GROUP), lambda c: (0, 0)),
                   pl.BlockSpec((1, D_GROUP), lambda c: (0, 0)),
                   pl.BlockSpec((1, 4 * D_GROUP), lambda c: (0, 0))],
        out_shape=[jax.ShapeDtypeStruct((s, D_IN_MAIN), BF16), jax.ShapeDtypeStruct((2, D_GROUP), F32),
                   jax.ShapeDtypeStruct((1, D_GROUP), F32), jax.ShapeDtypeStruct((1, 4 * D_GROUP), F32)],
        scratch_shapes=[pltpu.VMEM((N_HEADS, D_HEAD, D_HEAD), F32)],
        compiler_params=_params(("arbitrary",)),
    )(proj, logits, norm_w, states, d_y)


def _gate_column(gates, lane, idx):
    return jnp.sum(jnp.where(lane == idx, gates, 0.0), axis=1, keepdims=True)


def _head_layer_norm(h):
    mu = jnp.mean(h, axis=-1, keepdims=True)
    var = jnp.mean(jnp.square(h - mu), axis=-1, keepdims=True)
    return (h - mu) * lax.rsqrt(var + LN_EPS)


def _ml_chunk(qc, kc, v, mo, gates, nw, cts, ns, ms):
    n = qc.shape[0] // CHUNK
    row, col = _chunk_masks(CHUNK)
    mask = col <= row
    eye = col == row
    to_row = lambda t: jnp.sum(jnp.where(eye, t, 0.0), axis=0, keepdims=True)
    q = _blocks(qc * (D_HEAD ** -0.5))
    k = _blocks(kc)
    vs = _blocks(v)
    gate_rows = _split_chunks(gates)
    lane = lax.broadcasted_iota(jnp.int32, gate_rows[0].shape, 1)
    each = [(i, h) for i in range(n) for h in HEADS]
    on_each = lambda f: {ih: f(*ih) for ih in each}
    ig = on_each(lambda i, h: _gate_column(gate_rows[i], lane, h))
    lf = on_each(lambda i, h: jax.nn.log_sigmoid(_gate_column(gate_rows[i], lane, N_HEADS + h)))
    lf_row = on_each(lambda i, h: to_row(lf[i, h]))
    ig_row = on_each(lambda i, h: to_row(ig[i, h]))
    b_col = on_each(lambda i, h: jnp.sum(jnp.where(mask, lf_row[i, h], 0.0), axis=1, keepdims=True))
    b_row = on_each(lambda i, h: jnp.sum(jnp.where(row <= col, lf[i, h], 0.0), axis=0, keepdims=True))
    g = on_each(lambda i, h: jnp.sum(lf[i, h], axis=0, keepdims=True))
    d = on_each(lambda i, h: jnp.where(mask, b_col[i, h] - b_row[i, h] + ig_row[i, h], -jnp.inf))
    a = on_each(lambda i, h: g[i, h] - b_col[i, h] + ig[i, h])
    m_at = {(0, h): ms[h] for h in HEADS}
    for i, h in each:
        m_at[i + 1, h] = lax.stop_gradient(jnp.maximum(g[i, h] + m_at[i, h], jnp.max(a[i, h], axis=0, keepdims=True)))
    inter = on_each(lambda i, h: b_col[i, h] + m_at[i, h])
    m_t = on_each(lambda i, h: lax.stop_gradient(jnp.maximum(inter[i, h], jnp.max(d[i, h], axis=1, keepdims=True))))
    qk = on_each(lambda i, h: _nt(q[i][h], k[i][h]))
    sc = on_each(lambda i, h: qk[i, h] * jnp.exp(d[i, h] - m_t[i, h]))
    w_inter = on_each(lambda i, h: jnp.exp(inter[i, h] - m_t[i, h]))
    sv = on_each(lambda i, h: _nn(sc[i, h], vs[i][h]))
    decay = on_each(lambda i, h: jnp.exp(g[i, h] + m_at[i, h] - m_at[i + 1, h]))
    wk = on_each(lambda i, h: k[i][h] * jnp.exp(a[i, h] - m_at[i + 1, h]))
    kv = on_each(lambda i, h: _tn(vs[i][h], wk[i, h]))
    normed = []
    for i in range(n):
        qc_state = [_nt(q[i][h], cts[h]) for h in HEADS]
        num = [sv[i, h] + w_inter[i, h] * qc_state[h] for h in HEADS]
        den = [jnp.sum(sc[i, h], axis=1, keepdims=True)
               + w_inter[i, h] * jnp.sum(q[i][h] * ns[h], axis=1, keepdims=True) for h in HEADS]
        hh = [num[h] / jnp.maximum(jnp.abs(den[h]), jnp.exp(-m_t[i, h])) for h in HEADS]
        cts = tuple(decay[i, h] * cts[h] + kv[i, h] for h in HEADS)
        ns = tuple(decay[i, h] * ns[h] + jnp.sum(wk[i, h], axis=0, keepdims=True) for h in HEADS)
        normed.append(_merge_heads(tuple(_head_layer_norm(hh[h]) for h in HEADS)))
    y = jax.nn.sigmoid(mo) * (_merge_chunks(tuple(normed)) * nw)
    return y, cts, ns, tuple(m_at[n, h] for h in HEADS)


def _mlstm_fwd(qk, proj, gates, norm_w, y):
    s = proj.shape[0]
    rows = ML_CHUNKS_PER_STEP * CHUNK
    nc = s // rows

    def body(qk_ref, vo_ref, g_ref, nw_ref, _, y_ref, ct_out, n_out, m_out, ct_scr, n_scr, m_scr):
        @pl.when(pl.program_id(0) == 0)
        def _():
            ct_scr[...] = jnp.zeros_like(ct_scr)
            n_scr[...] = jnp.zeros_like(n_scr)
            m_scr[...] = jnp.full(m_scr.shape, NEG_BIG, F32)

        cts = tuple(ct_scr[h] for h in HEADS)
        ns = tuple(n_scr[h] for h in HEADS)
        ms = tuple(m_scr[h] for h in HEADS)
        y, cts_new, ns_new, ms_new = _ml_chunk(_seg(qk_ref, 0), _seg(qk_ref, 1), _seg(vo_ref, 0), _seg(vo_ref, 1),
                                               g_ref[...], nw_ref[...], cts, ns, ms)
        y_ref[...] = y.astype(y_ref.dtype)
        for h in HEADS:
            ct_out[h], n_out[h], m_out[h] = cts[h], ns[h], ms[h]
            ct_scr[h], n_scr[h], m_scr[h] = cts_new[h], ns_new[h], ms_new[h]

    st = lambda r, w: pl.BlockSpec((None, N_HEADS, r, w), lambda c: (c, 0, 0, 0))
    return pl.pallas_call(
        body, name="mlstm_fwd", grid=(nc,),
        in_specs=[pl.BlockSpec((rows, 2 * D_GROUP), lambda c: (c, 0)),
                  pl.BlockSpec((rows, 2 * D_GROUP), lambda c: (c, 3)),
                  pl.BlockSpec((rows, LANES), lambda c: (c, 0)),
                  pl.BlockSpec((1, D_GROUP), lambda c: (0, 0)),
                  pl.BlockSpec(memory_space=pl.ANY)],
        out_specs=[pl.BlockSpec((rows, D_GROUP), lambda c: (c, 1)),
                   st(D_HEAD, D_HEAD), st(1, D_HEAD), st(1, 1)],
        out_shape=[jax.ShapeDtypeStruct(y.shape, y.dtype),
                   jax.ShapeDtypeStruct((nc, N_HEADS, D_HEAD, D_HEAD), F32),
                   jax.ShapeDtypeStruct((nc, N_HEADS, 1, D_HEAD), F32),
                   jax.ShapeDtypeStruct((nc, N_HEADS, 1, 1), F32)],
        input_output_aliases={4: 0},
        scratch_shapes=[pltpu.VMEM((N_HEADS, D_HEAD, D_HEAD), F32), pltpu.VMEM((N_HEADS, 1, D_HEAD), F32),
                        pltpu.VMEM((N_HEADS, 1, 1), F32)],
        compiler_params=_params(("arbitrary",)),
    )(qk, proj, gates, norm_w, y)


def _mlstm_bwd(qk, proj, gates, norm_w, ct_s, n_s, m_s, d_y, d_proj):
    s = proj.shape[0]
    rows = ML_CHUNKS_PER_STEP * CHUNK
    nc = s // rows

    def body(qk_ref, vo_ref, g_ref, nw_ref, ct_ref, n_ref, m_ref, dy_ref, _,
             dp_ref, dqk_ref, dg_ref, dnw_ref, dsum_ref, dct_scr, dn_scr):
        @pl.when(pl.program_id(0) == 0)
        def _():
            dct_scr[...] = jnp.zeros_like(dct_scr)
            dn_scr[...] = jnp.zeros_like(dn_scr)
            dnw_ref[...] = jnp.zeros_like(dnw_ref)
            dsum_ref[...] = jnp.zeros_like(dsum_ref)

        ms = tuple(m_ref[h] for h in HEADS)
        step = lambda *a: _ml_chunk(*a, ms)[:3]
        _, vjp = jax.vjp(step, _seg(qk_ref, 0), _seg(qk_ref, 1), _seg(vo_ref, 0), _seg(vo_ref, 1), g_ref[...],
                         nw_ref[...], tuple(ct_ref[h] for h in HEADS), tuple(n_ref[h] for h in HEADS))
        d_q, d_k, d_v, d_o, d_gates, d_nw, d_cts, d_ns = vjp(
            (dy_ref[...], tuple(dct_scr[h] for h in HEADS), tuple(dn_scr[h] for h in HEADS)))
        dqk_ref[:, 0:D_GROUP] = d_q
        dqk_ref[:, D_GROUP:2 * D_GROUP] = d_k
        for seg, val in enumerate((d_v, d_o)):
            dp_ref[:, seg * D_GROUP:(seg + 1) * D_GROUP] = val.astype(dp_ref.dtype)
            dsum_ref[:, seg * D_GROUP:(seg + 1) * D_GROUP] += jnp.sum(val, axis=0, keepdims=True)
        dg_ref[...] = d_gates
        dnw_ref[...] += d_nw
        for h in HEADS:
            dct_scr[h] = d_cts[h]
            dn_scr[h] = d_ns[h]

    rev = lambda c: nc - 1 - c
    st = lambda r, w: pl.BlockSpec((None, N_HEADS, r, w), lambda c: (rev(c), 0, 0, 0))
    return pl.pallas_call(
        body, name="mlstm_bwd", grid=(nc,),
        in_specs=[pl.BlockSpec((rows, 2 * D_GROUP), lambda c: (rev(c), 0)),
                  pl.BlockSpec((rows, 2 * D_GROUP), lambda c: (rev(c), 3)),
                  pl.BlockSpec((rows, LANES), lambda c: (rev(c), 0)),
                  pl.BlockSpec((1, D_GROUP), lambda c: (0, 0)),
                  st(D_HEAD, D_HEAD), st(1, D_HEAD), st(1, 1),
                  pl.BlockSpec((rows, D_GROUP), lambda c: (rev(c), 1)),
                  pl.BlockSpec(memory_space=pl.ANY)],
        out_specs=[pl.BlockSpec((rows, 2 * D_GROUP), lambda c: (rev(c), 3)),
                   pl.BlockSpec((rows, 2 * D_GROUP), lambda c: (rev(c), 0)),
                   pl.BlockSpec((rows, LANES), lambda c: (rev(c), 0)),
                   pl.BlockSpec((1, D_GROUP), lambda c: (0, 0)),
                   pl.BlockSpec((1, 2 * D_GROUP), lambda c: (0, 0))],
        out_shape=[jax.ShapeDtypeStruct(d_proj.shape, d_proj.dtype), jax.ShapeDtypeStruct((s, 2 * D_GROUP), F32),
                   jax.ShapeDtypeStruct((s, LANES), F32), jax.ShapeDtypeStruct((1, D_GROUP), F32),
                   jax.ShapeDtypeStruct((1, 2 * D_GROUP), F32)],
        input_output_aliases={8: 0},
        scratch_shapes=[pltpu.VMEM((N_HEADS, D_HEAD, D_HEAD), F32), pltpu.VMEM((N_HEADS, 1, D_HEAD), F32)],
        compiler_params=_params(("arbitrary",)),
    )(qk, proj, gates, norm_w, ct_s, n_s, m_s, d_y, d_proj)


LN_TOKENS = 512
ATT_TOKENS = 512


def _proj_res_ln(a, w, xres, g, b, name):
    s, dm = xres.shape
    k = a.shape[1]
    tb = min(LN_TOKENS, s)

    def body(a_ref, w_ref, x_ref, g_ref, b_ref, z_ref, o_ref):
        z = ALPHA * x_ref[...] + _nn_raw(a_ref[...], w_ref[...])
        z_ref[...] = z
        o_ref[...] = _layer_norm(z, g_ref[...], b_ref[...])

    tok = pl.BlockSpec((tb, dm), lambda i: (i, 0))
    vec = pl.BlockSpec((1, dm), lambda i: (0, 0))
    act = jax.ShapeDtypeStruct((s, dm), F32)
    return pl.pallas_call(
        body, name=name, grid=(s // tb,),
        in_specs=[pl.BlockSpec((tb, k), lambda i: (i, 0)), pl.BlockSpec((k, dm), lambda i: (0, 0)), tok, vec, vec],
        out_specs=[tok, tok], out_shape=[act, act], compiler_params=_params(("parallel",)),
    )(a, w, xres, g, b)


def _ln_bwd_proj(d_out, z, g, b, w, name):
    s, dm = z.shape
    k = w.shape[0]
    tb = min(LN_TOKENS, s)

    def body(do_ref, z_ref, g_ref, b_ref, w_ref, dz_ref, da_ref, dg_ref, db_ref):
        @pl.when(pl.program_id(0) == 0)
        def _():
            dg_ref[...] = jnp.zeros_like(dg_ref)
            db_ref[...] = jnp.zeros_like(db_ref)

        _, vjp = jax.vjp(_layer_norm, z_ref[...], g_ref[...], b_ref[...])
        d_z, d_g, d_b = vjp(do_ref[...])
        dz_ref[...] = d_z
        da_ref[...] = _nt_raw(d_z, w_ref[...])
        dg_ref[...] += d_g
        db_ref[...] += d_b

    tok = pl.BlockSpec((tb, dm), lambda i: (i, 0))
    vec = pl.BlockSpec((1, dm), lambda i: (0, 0))
    return pl.pallas_call(
        body, name=name, grid=(s // tb,),
        in_specs=[tok, tok, vec, vec, pl.BlockSpec((k, dm), lambda i: (0, 0))],
        out_specs=[tok, pl.BlockSpec((tb, k), lambda i: (i, 0)), vec, vec],
        out_shape=[jax.ShapeDtypeStruct((s, dm), F32), jax.ShapeDtypeStruct((s, k), F32),
                   jax.ShapeDtypeStruct((1, dm), F32), jax.ShapeDtypeStruct((1, dm), F32)],
        compiler_params=_params(("arbitrary",)),
    )(d_out, z, g, b, w)


def _proj_loss_tail(a, w, xres, g, b, target):
    s, dm = xres.shape
    k = a.shape[1]
    tb = min(ATT_TOKENS, s)

    def loss_fn(z, gg, bb, tgt):
        err = jnp.square(_layer_norm(z, gg, bb) - tgt)
        return 0.5 * jnp.sum(jnp.mean(err, axis=-1, keepdims=True), axis=0, keepdims=True)

    def body(a_ref, w_ref, x_ref, g_ref, b_ref, t_ref, loss_ref, dz_ref, dg_ref, db_ref):
        @pl.when(pl.program_id(0) == 0)
        def _():
            loss_ref[...] = jnp.zeros_like(loss_ref)
            dg_ref[...] = jnp.zeros_like(dg_ref)
            db_ref[...] = jnp.zeros_like(db_ref)

        z = ALPHA * x_ref[...] + _nn_raw(a_ref[...], w_ref[...])
        tgt = t_ref[...]
        loss, vjp = jax.vjp(lambda zz, gg, bb: loss_fn(zz, gg, bb, tgt), z, g_ref[...], b_ref[...])
        d_z, d_g, d_b = vjp(jnp.ones((1, 1), F32))
        loss_ref[...] += loss
        dz_ref[...] = d_z
        dg_ref[...] += d_g
        db_ref[...] += d_b

    tok = pl.BlockSpec((tb, dm), lambda i: (i, 0))
    vec = pl.BlockSpec((1, dm), lambda i: (0, 0))
    one = pl.BlockSpec((1, 1), lambda i: (0, 0))
    return pl.pallas_call(
        body, name="ffn_down_loss_tail", grid=(s // tb,),
        in_specs=[pl.BlockSpec((tb, k), lambda i: (i, 0)), pl.BlockSpec((k, dm), lambda i: (0, 0)), tok, vec, vec, tok],
        out_specs=[one, tok, vec, vec],
        out_shape=[jax.ShapeDtypeStruct((1, 1), F32), jax.ShapeDtypeStruct((s, dm), F32),
                   jax.ShapeDtypeStruct((1, dm), F32), jax.ShapeDtypeStruct((1, dm), F32)],
        compiler_params=_params(("arbitrary",)),
    )(a, w, xres, g, b, target)


def _att_heads(qs, ks, vs):
    sc = [_nt(q, k) * (CA_DH ** -0.5) for q, k in zip(qs, ks)]
    p = [jax.nn.softmax(s, axis=-1) for s in sc]
    return tuple(_nn(pp, v) for pp, v in zip(p, vs))


def _head_slices(ref_or_value, offset):
    return tuple(ref_or_value[:, offset + h * CA_DH:offset + (h + 1) * CA_DH] for h in range(CA_HEADS))


def _cross_attention_fwd(x1, kv, wq, wo, g, b):
    s = x1.shape[0]
    tb = min(ATT_TOKENS, s)

    def body(x_ref, kv_ref, wq_ref, wo_ref, g_ref, b_ref, att_ref, z_ref, o_ref):
        x_blk = x_ref[...]
        q = _nn_raw(x_blk, wq_ref[...])
        att = jnp.concatenate(_att_heads(_head_slices(q, 0), _head_slices(kv_ref, 0), _head_slices(kv_ref, D_MODEL)),
                              axis=1)
        att_ref[...] = att.astype(att_ref.dtype)
        z = ALPHA * x_blk + _nn_raw(att, wo_ref[...])
        z_ref[...] = z
        o_ref[...] = _layer_norm(z, g_ref[...], b_ref[...])

    tok = pl.BlockSpec((tb, D_MODEL), lambda i: (i, 0))
    mat = pl.BlockSpec((D_MODEL, D_MODEL), lambda i: (0, 0))
    vec = pl.BlockSpec((1, D_MODEL), lambda i: (0, 0))
    act = jax.ShapeDtypeStruct((s, D_MODEL), F32)
    return pl.pallas_call(
        body, name="cross_attention_fwd", grid=(s // tb,),
        in_specs=[tok, pl.BlockSpec((N_MEM, 2 * D_MODEL), lambda i: (0, 0)), mat, mat, vec, vec],
        out_specs=[tok, tok, tok],
        out_shape=[jax.ShapeDtypeStruct((s, D_MODEL), BF16), act, act],
        compiler_params=_params(("parallel",)),
    )(x1, kv, wq, wo, g, b)


def _cross_attention_bwd(d_x2, x1, z2, kv, wq, wo, g, b):
    s = x1.shape[0]
    tb = min(ATT_TOKENS, s)

    def body(dx2_ref, x_ref, z_ref, kv_ref, wq_ref, wo_ref, g_ref, b_ref,
             dx1_ref, dq_ref, dz_ref, dkv_ref, dg_ref, db_ref):
        @pl.when(pl.program_id(0) == 0)
        def _():
            dkv_ref[...] = jnp.zeros_like(dkv_ref)
            dg_ref[...] = jnp.zeros_like(dg_ref)
            db_ref[...] = jnp.zeros_like(db_ref)

        _, ln_vjp = jax.vjp(_layer_norm, z_ref[...], g_ref[...], b_ref[...])
        d_z, d_g, d_b = ln_vjp(dx2_ref[...])
        dg_ref[...] += d_g
        db_ref[...] += d_b
        dz_ref[...] = d_z.astype(dz_ref.dtype)
        d_att = _nt_raw(d_z, wo_ref[...])
        q = _nn_raw(x_ref[...], wq_ref[...])
        _, vjp = jax.vjp(_att_heads, _head_slices(q, 0), _head_slices(kv_ref, 0), _head_slices(kv_ref, D_MODEL))
        d_qs, d_ks, d_vs = vjp(_head_slices(d_att, 0))
        for h in range(CA_HEADS):
            lo = h * CA_DH
            dkv_ref[:, lo:lo + CA_DH] += d_ks[h]
            dkv_ref[:, D_MODEL + lo:D_MODEL + lo + CA_DH] += d_vs[h]
        d_q = jnp.concatenate(d_qs, axis=1)
        dq_ref[...] = d_q.astype(dq_ref.dtype)
        dx1_ref[...] = ALPHA * d_z + _nt_raw(d_q, wq_ref[...])

    tok = pl.BlockSpec((tb, D_MODEL), lambda i: (i, 0))
    mem = pl.BlockSpec((N_MEM, 2 * D_MODEL), lambda i: (0, 0))
    mat = pl.BlockSpec((D_MODEL, D_MODEL), lambda i: (0, 0))
    vec = pl.BlockSpec((1, D_MODEL), lambda i: (0, 0))
    low = jax.ShapeDtypeStruct((s, D_MODEL), BF16)
    return pl.pallas_call(
        body, name="cross_attention_bwd", grid=(s // tb,),
        in_specs=[tok, tok, tok, mem, mat, mat, vec, vec], out_specs=[tok, tok, tok, mem, vec, vec],
        out_shape=[jax.ShapeDtypeStruct((s, D_MODEL), F32), low, low,
                   jax.ShapeDtypeStruct((N_MEM, 2 * D_MODEL), F32),
                   jax.ShapeDtypeStruct((1, D_MODEL), F32), jax.ShapeDtypeStruct((1, D_MODEL), F32)],
        compiler_params=_params(("arbitrary",)),
    )(d_x2, x1, z2, kv, wq, wo, g, b)


def _local_step(x, mem, target, w, mid_weights=None, ffn_weights=None, down_weights=None, on_ffn_grads=None,
                on_mid_grads=None,
                on_small_grads=None, on_last_grads=None):
    w = dict(w)
    s = x.shape[0]
    tm = min(512, s)
    tt = min(512, s)
    proj = _matmul_nn(x, w["w_in_main"], w["b_in_main"], min(2048, s), 512, "proj")
    gates = _matmul_nn(x, w["w_in_gate"], w["b_in_gate"], tm, LANES, "proj_gates")
    qk = _ml_conv_fwd(proj, w["ml_conv_w"], w["ml_conv_b"])
    y, hg_states = _hgrn2_fwd(proj, w["hg_lb_logits"], w["hg_norm_w"])
    y, ct_s, n_s, m_s = _mlstm_fwd(qk, proj, gates, w["ml_norm_w"], y)
    if mid_weights is not None:
        w.update(mid_weights(y))
    z1, x1 = _proj_res_ln(y, w["w_out"], x, w["ln1_g"], w["ln1_b"], "out_proj_ln1")
    kv = _matmul_nn(mem, w["ca_wkv"], None, N_MEM, CA_DH, "kv")
    att, z2, x2 = _cross_attention_fwd(x1, kv, w["ca_wq"], w["ca_wo"], w["ln2_g"], w["ln2_b"])
    if ffn_weights is not None:
        w.update(ffn_weights(x2))
    u = _matmul_nn(x2, w["ffn_w_up"], None, min(2048, s), UP_SHARD_P, "ffn_up", BF16)
    hid = _ffn_conv_fwd(u, w["ffn_conv_w"], w["ffn_conv_b"])
    if down_weights is not None:
        w.update(down_weights(hid))
    loss, d_z3, d_ln3_g, d_ln3_b = _proj_loss_tail(hid, w["ffn_w_down"], x2, w["ln3_g"], w["ln3_b"], target)
    grads = {"ln3_g": d_ln3_g, "ln3_b": d_ln3_b}
    grads["ffn_w_down"] = _matmul_tn(hid, d_z3, 1536, D_MODEL, tt, "d_w_down")
    d_hid = _matmul_nt([(d_z3, w["ffn_w_down"])], None, 1.0, tm, D_FF_P, "d_hid", BF16)
    d_ug, d_uv, d_cwg, d_cwv, d_cbg, d_cbv = _ffn_conv_bwd(u, w["ffn_conv_w"], w["ffn_conv_b"], d_hid)
    grads["ffn_conv_w"] = jnp.concatenate([d_cwg, d_cwv], axis=-1)
    grads["ffn_conv_b"] = jnp.concatenate([d_cbg, d_cbv], axis=-1)
    half = N_DEV // 2
    d_w_up = _matmul_tn(x2, d_ug, D_MODEL, UP_SHARD_P, tt, "d_w_up_gate", shards=N_DEV, group=half)
    grads["ffn_w_up"] = _matmul_tn(x2, d_uv, D_MODEL, UP_SHARD_P, tt, "d_w_up_val", shards=N_DEV,
                                   shard0=half, group=half, into=d_w_up)
    d_x2 = _matmul_nt([(d_ug, w["ffn_w_up"], 0), (d_uv, w["ffn_w_up"], N_DEV // 2)], d_z3, ALPHA,
                      min(256, s), D_MODEL, "d_x2")
    if on_ffn_grads is not None:
        d_x2 = on_ffn_grads(grads, d_x2)
    d_x1, d_q, d_z2, d_kv, grads["ln2_g"], grads["ln2_b"] = _cross_attention_bwd(
        d_x2, x1, z2, kv, w["ca_wq"], w["ca_wo"], w["ln2_g"], w["ln2_b"])
    grads["ca_wo"] = _matmul_tn(att, d_z2, D_MODEL, D_MODEL, tt, "d_ca_wo")
    grads["ca_wq"] = _matmul_tn(x1, d_q, D_MODEL, D_MODEL, tt, "d_ca_wq")
    grads["ca_wkv"] = _matmul_tn(mem, d_kv, D_MODEL, CA_DH, N_MEM, "d_ca_wkv", shards=N_DEV, group=N_DEV)
    d_z1, d_y, grads["ln1_g"], grads["ln1_b"] = _ln_bwd_proj(d_x1, z1, w["ln1_g"], w["ln1_b"], w["w_out"],
                                                             "ln1_bwd_out_proj")
    grads["w_out"] = _matmul_tn(y, d_z1, D_MODEL, D_MODEL, tt, "d_w_out")
    if on_mid_grads is not None:
        d_y = on_mid_grads(grads, d_y)
    d_proj, grads["hg_lb_logits"], grads["hg_norm_w"], db_hg = _hgrn2_bwd(
        proj, w["hg_lb_logits"], w["hg_norm_w"], hg_states, d_y)
    d_proj, d_qk, d_gates, grads["ml_norm_w"], db_vo = _mlstm_bwd(
        qk, proj, gates, w["ml_norm_w"], ct_s, n_s, m_s, d_y, d_proj)
    d_proj, grads["ml_conv_w"], grads["ml_conv_b"], db_qk = _ml_conv_bwd(
        proj, w["ml_conv_w"], w["ml_conv_b"], d_qk, d_proj)
    grads["b_in_main"] = jnp.concatenate([db_hg, db_qk, db_vo], axis=-1)
    grads["w_in_gate"], grads["b_in_gate"] = _matmul_tn(x, d_gates, D_MODEL, LANES, tt, "d_w_in_gates", colsum=True)
    if on_small_grads is not None:
        d_proj = on_small_grads(grads, loss, d_proj)
    grads["w_in_main"] = _matmul_tn(x, d_proj, D_MODEL, min(2048, D_IN_MAIN), tt, "d_w_in")
    if on_last_grads is not None:
        d_z1 = on_last_grads(grads, d_z1)
    grad_x = _matmul_nt([(d_proj, w["w_in_main"]), (d_gates, w["w_in_gate"])], d_z1, ALPHA, tm, D_MODEL, "d_x")
    return loss, grad_x, grads


HBM_SPEC = pl.BlockSpec(memory_space=pltpu.HBM)


def _coords():
    return lax.axis_index("x"), lax.axis_index("y"), lax.axis_index("c")


def _other_chips(x, y):
    return [(1 - x, y), (x, 1 - y), (1 - x, 1 - y)]


def _my_slot():
    x, y, c = _coords()
    return 4 * x + 2 * y + c


SEM_SPEC = pl.BlockSpec(memory_space=pltpu.SEMAPHORE)
ANY_SPEC = pl.BlockSpec(memory_space=pl.ANY)
SIDE_EFFECT = pltpu.SideEffectType.DATAFLOW_SIDE_EFFECTING


def _peer(x, y, c, d):
    flip = lambda v, bit: 1 - v if bit else v
    p = (flip(x, d & 4), flip(y, d & 2), flip(c, d & 1))
    return p, 4 * p[0] + 2 * p[1] + p[2]


def _direct_copies(gather, src_refs, land_refs, send_sems, recv_sems):
    x, y, c = _coords()
    me = 4 * x + 2 * y + c
    copies = []
    for a in range(len(src_refs)):
        for d in range(1, N_DEV):
            peer, peer_slot = _peer(x, y, c, d)
            copies.append(pltpu.make_async_remote_copy(
                src_ref=src_refs[a] if gather else src_refs[a].at[peer_slot],
                dst_ref=land_refs[a].at[me] if gather else land_refs[a].at[d - 1],
                send_sem=send_sems.at[7 * a + d - 1], recv_sem=recv_sems.at[7 * a + d - 1],
                device_id=peer, device_id_type=MESH))
    return copies


def _hbm(t):
    return pltpu.HBM(t.shape, t.dtype)


def _chip_copies(src_refs, land_refs, send_sems, recv_sems):
    x, y, c = _coords()
    me = 4 * x + 2 * y + c
    targets = [(x, y, 1 - c)] + [(cx, cy, c) for cx, cy in _other_chips(x, y)]
    return [pltpu.make_async_remote_copy(
        src_ref=src_refs[a], dst_ref=land_refs[a].at[me], send_sem=send_sems.at[4 * a + k],
        recv_sem=recv_sems.at[4 * a + k], device_id=target, device_id_type=MESH)
        for a in range(len(src_refs)) for k, target in enumerate(targets)]


def _forward_copies(land_refs, send_sems, recv_sems):
    x, y, c = _coords()
    return [pltpu.make_async_remote_copy(
        src_ref=land_refs[a].at[4 * cx + 2 * cy + c], dst_ref=land_refs[a].at[4 * cx + 2 * cy + c],
        send_sem=send_sems.at[3 * a + j], recv_sem=recv_sems.at[3 * a + j],
        device_id=(x, y, 1 - c), device_id_type=MESH)
        for a in range(len(land_refs)) for j, (cx, cy) in enumerate(_other_chips(x, y))]


def _split_copy_start(make_copies, n_sems, operands, through, name):
    n_ops = len(operands)

    def body(*refs):
        for cp in make_copies(refs[:n_ops], refs[n_ops + 1], refs[n_ops + 2]):
            cp.start()

    ins = [pltpu.with_memory_space_constraint(t, pltpu.HBM) for t in (*operands, through)]
    sems = pltpu.SemaphoreType.DMA((n_sems,))
    res = pl.pallas_call(
        body, name=name, out_shape=(sems, sems, *[_hbm(t) for t in ins]),
        in_specs=[HBM_SPEC] * (n_ops + 1), out_specs=(SEM_SPEC, SEM_SPEC, *[HBM_SPEC] * (n_ops + 1)),
        input_output_aliases={i: 2 + i for i in range(n_ops + 1)},
        compiler_params=pltpu.CompilerParams(has_side_effects=SIDE_EFFECT),
    )(*ins)
    return (res[0], res[1], list(res[2:2 + n_ops])), res[2 + n_ops]


def _split_copy_wait(make_copies, started, after, name):
    send_sems, recv_sems, operands = started
    n_ops = len(operands)
    after = list(after) if isinstance(after, (list, tuple)) else [after]

    def body(*refs):
        for cp in make_copies(refs[:n_ops], refs[n_ops], refs[n_ops + 1]):
            cp.wait_send()
            cp.wait_recv()

    res = pl.pallas_call(
        body, name=name, out_shape=tuple(_hbm(t) for t in operands),
        in_specs=[HBM_SPEC] * n_ops + [SEM_SPEC, SEM_SPEC] + [ANY_SPEC] * len(after),
        out_specs=tuple([HBM_SPEC] * n_ops), input_output_aliases={i: i for i in range(n_ops)},
        compiler_params=pltpu.CompilerParams(has_side_effects=SIDE_EFFECT),
    )(*operands, send_sems, recv_sems, *after)
    return list(res)


def _halves(make_copies, na):
    return lambda refs, send_sems, recv_sems: make_copies(refs[:na], refs[na:], send_sems, recv_sems)


def _direct_start(gather, arrays, through, name):
    na = len(arrays)
    lands = [lax.empty((N_DEV,) + t.shape if gather else (N_DEV - 1,) + t.shape[1:], t.dtype) for t in arrays]
    return _split_copy_start(_halves(functools.partial(_direct_copies, gather), na), 7 * na, [*arrays, *lands],
                             through, name)


def _direct_wait(gather, started, after, name):
    na = len(started[2]) // 2
    operands = _split_copy_wait(_halves(functools.partial(_direct_copies, gather), na), started, after, name)
    return operands[:na], operands[na:]


def _two_level_gather(shards, glue, name):
    na = len(shards)
    lands = [lax.empty((N_DEV,) + t.shape, t.dtype) for t in shards]
    nothing = jnp.zeros((SUBLANES, LANES), F32)
    started, _ = _split_copy_start(_halves(_chip_copies, na), 4 * na, [*shards, *lands], nothing, name + "_start")
    operands = _split_copy_wait(_halves(_chip_copies, na), started, glue, name + "_wait")
    started, mine = _split_copy_start(_forward_copies, 3 * na, operands[na:], operands[0], name + "_forward_start")
    lands = _split_copy_wait(_forward_copies, started, mine, name + "_forward_wait")
    return [lax.dynamic_update_index_in_dim(land, own, _my_slot(), 0)
            for own, land in zip([mine, *operands[1:na]], lands)]


def _row_tile(rows):
    for t in (256, 176, 128):
        if rows % t == 0 and rows > t:
            return t
    return rows


def _adamw_math(g, w, m, v):
    m_new = ADAM_B1 * m + (1.0 - ADAM_B1) * g
    v_new = ADAM_B2 * v + (1.0 - ADAM_B2) * jnp.square(g)
    m_hat = m_new / (1.0 - ADAM_B1 ** ADAM_STEP)
    v_hat = v_new / (1.0 - ADAM_B2 ** ADAM_STEP)
    delta = -ADAM_LR * (m_hat / (jnp.sqrt(v_hat) + ADAM_EPS) + ADAM_WD * w)
    return delta, m_new, v_new


def _adamw_sharded(chip, sums, got, w, m, v, name):
    r, c = w.shape
    tr = _row_tile(r)
    n_got = got.shape[0]

    def body(chip_ref, s_ref, g_ref, w_ref, m_ref, v_ref, go_ref, d_ref, nm_ref, nv_ref):
        g = s_ref[...].astype(F32)
        for i in range(n_got):
            g = g + g_ref[i].astype(F32)
        delta, m_new, v_new = _adamw_math(g, w_ref[...], m_ref[...], v_ref[...])
        go_ref[...] = g
        d_ref[...] = delta
        nm_ref[...] = m_new
        nv_ref[...] = v_new

    blk = pl.BlockSpec((tr, c), lambda i, chip_ref: (i, 0))
    out = jax.ShapeDtypeStruct((r, c), F32)
    return pl.pallas_call(
        body, name=name,
        grid_spec=pltpu.PrefetchScalarGridSpec(
            num_scalar_prefetch=1, grid=(r // tr,),
            in_specs=[pl.BlockSpec((None, tr, c), lambda i, chip_ref: (chip_ref[0], i, 0)),
                      pl.BlockSpec((n_got, tr, c), lambda i, chip_ref: (0, i, 0)), blk, blk, blk],
            out_specs=[blk, blk, blk, blk]),
        out_shape=[out, out, out, out],
        compiler_params=_params(("parallel",)),
    )(chip, sums, got, w, m, v)


def _adamw_replicated(parts, w, m, v):
    p, r, c = parts.shape

    def body(p_ref, w_ref, m_ref, v_ref, g_ref, d_ref, nm_ref, nv_ref):
        g = p_ref[0]
        for i in range(1, p):
            g = g + p_ref[i]
        delta, m_new, v_new = _adamw_math(g, w_ref[...], m_ref[...], v_ref[...])
        g_ref[...] = g
        d_ref[...] = delta
        nm_ref[...] = m_new
        nv_ref[...] = v_new

    blk = pl.BlockSpec((r, c), lambda i: (0, 0))
    out = jax.ShapeDtypeStruct((r, c), F32)
    return pl.pallas_call(
        body, name="adamw_replicated", grid=(1,),
        in_specs=[pl.BlockSpec((p, r, c), lambda i: (0, 0, 0)), blk, blk, blk],
        out_specs=[blk, blk, blk, blk], out_shape=[out, out, out, out],
        compiler_params=_params(("arbitrary",)),
    )(parts, w, m, v)


SHARDED_NAMES = ("w_in", "ml_conv_w", "w_out", "ca_wq", "ca_wkv", "ca_wo", "ffn_w_up", "ffn_conv_w", "ffn_w_down")
SMALL_NAMES = ("b_in", "hg_lb_logits", "hg_norm_w", "ml_conv_b", "ml_norm_w", "ln1_g", "ln1_b",
               "ln2_g", "ln2_b", "ffn_conv_b", "ln3_g", "ln3_b")
WEIGHT_NAMES = ("w_in", "b_in", "hg_lb_logits", "hg_norm_w", "ml_conv_w", "ml_conv_b", "ml_norm_w", "w_out",
                "ln1_g", "ln1_b", "ca_wq", "ca_wkv", "ca_wo", "ln2_g", "ln2_b", "ffn_w_up", "ffn_conv_w",
                "ffn_conv_b", "ffn_w_down", "ln3_g", "ln3_b")
PAD_TO = {"ffn_w_up": UP_SHARD_P, "ffn_conv_w": UP_SHARD_P}
SMALL_ROWS = 24
SMALL_W = D_MODEL


def _shard_2d(name, block):
    t = block[0]
    if name in PAD_TO:
        t = jnp.pad(t, ((0, 0), (0, PAD_TO[name] - t.shape[1])))
    return t


def _shard_like(name, t, like):
    return t[:, :like.shape[2]][None]


def _pad_cols(t, width):
    return jnp.pad(t, ((0, 0), (0, width - t.shape[1])))


FIRST_NAMES = ("w_in", "ml_conv_w")
FFN_NAMES = ("ffn_w_up", "ffn_w_down", "ffn_conv_w")
MID_NAMES = ("ca_wo", "ca_wq", "ca_wkv", "w_out")


def _first_weights(g, small):
    w = dict(small)
    w_in = jnp.concatenate([g["w_in"][j] for j in range(N_DEV)], axis=1)
    w["w_in_main"] = w_in[:, :D_IN_MAIN]
    w["w_in_gate"] = _pad_cols(w_in[:, D_IN_MAIN:], LANES)
    w["b_in_main"] = small["b_in"][:, :D_IN_MAIN]
    w["b_in_gate"] = _pad_cols(small["b_in"][:, D_IN_MAIN:], LANES)
    w["ml_conv_w"] = jnp.transpose(g["ml_conv_w"], (1, 0, 2)).reshape(ML_CONV, 2 * D_GROUP)
    return w


def _mid_weights(g):
    w = {n: g[n].reshape(D_MODEL, D_MODEL) for n in ("w_out", "ca_wq", "ca_wo")}
    w["ca_wkv"] = g["ca_wkv"]
    return w


FFN_UP_NAMES = ("ffn_w_up", "ffn_conv_w")
FFN_DOWN_NAMES = ("ffn_w_down",)


def _ffn_up_weights(g, small):
    w = {"ffn_w_up": g["ffn_w_up"]}
    w["ffn_conv_w"] = jnp.transpose(g["ffn_conv_w"], (1, 0, 2)).reshape(FFN_CONV, D_UP_P)
    w["ffn_conv_b"] = _pad_cols(small["ffn_conv_b"].reshape(N_DEV, UP_SHARD), UP_SHARD_P).reshape(1, D_UP_P)
    return w


def _ffn_down_weights(g):
    down = g["ffn_w_down"].reshape(N_DEV // 2, UP_SHARD, D_MODEL)
    return {"ffn_w_down": jnp.pad(down, ((0, 0), (0, UP_SHARD_P - UP_SHARD), (0, 0))).reshape(D_FF_P, D_MODEL)}


def _whole_weights(g, small):
    return {**_first_weights(g, small), **_mid_weights(g), **_ffn_up_weights(g, small), **_ffn_down_weights(g)}


def _owner_stack(n, grads):
    if n == "w_in":
        w_in = jnp.concatenate([grads["w_in_main"], grads["w_in_gate"][:, :D_IN - D_IN_MAIN]], axis=1)
        return jnp.stack([w_in[:, j * W_IN_SHARD:(j + 1) * W_IN_SHARD] for j in range(N_DEV)])
    if n in ("w_out", "ca_wq", "ca_wo"):
        return grads[n].reshape(N_DEV, D_MODEL // N_DEV, D_MODEL)
    if n == "ffn_w_down":
        down = grads[n].reshape(N_DEV // 2, UP_SHARD_P, D_MODEL)[:, :UP_SHARD]
        return down.reshape(N_DEV, D_FF // N_DEV, D_MODEL)
    if n == "ml_conv_w":
        return jnp.transpose(grads[n].reshape(ML_CONV, N_DEV, LANES), (1, 0, 2))
    if n == "ffn_conv_w":
        return jnp.transpose(grads[n].reshape(FFN_CONV, N_DEV, UP_SHARD_P), (1, 0, 2))
    return grads[n]


def _owner_stacks(grads):
    return {n: _owner_stack(n, grads) for n in SHARDED_NAMES}


def _small_grads(grads):
    out = {n: grads[n] for n in SMALL_NAMES if n in grads}
    out["b_in"] = jnp.concatenate([grads["b_in_main"], grads["b_in_gate"][:, :D_IN - D_IN_MAIN]], axis=1)
    out["ffn_conv_b"] = grads["ffn_conv_b"].reshape(N_DEV, UP_SHARD_P)[:, :UP_SHARD].reshape(1, D_UP)
    return out


def _pack_small(p, extra=None):
    flat = [p[n].reshape(-1) for n in SMALL_NAMES]
    if extra is not None:
        flat.append(extra.reshape(-1))
    flat = jnp.concatenate(flat)
    return jnp.pad(flat, (0, SMALL_ROWS * SMALL_W - flat.shape[0])).reshape(SMALL_ROWS, SMALL_W)


def _unpack_small(slab, like):
    out = {}
    flat = slab.reshape(-1)
    o = 0
    for n in SMALL_NAMES:
        out[n] = flat[o:o + like[n].size].reshape(like[n].shape)
        o += like[n].size
    return out, flat[o]


def kernel(x, mem, w_in, b_in, hg_lb_logits, hg_norm_w, ml_conv_w, ml_conv_b, ml_norm_w, w_out, ln1_g, ln1_b, ca_wq, ca_wkv, ca_wo, ln2_g, ln2_b, ffn_w_up, ffn_conv_w, ffn_conv_b, ffn_w_down, ln3_g, ln3_b, loss_target, m_w_in, m_b_in, m_hg_lb_logits, m_hg_norm_w, m_ml_conv_w, m_ml_conv_b, m_ml_norm_w, m_w_out, m_ln1_g, m_ln1_b, m_ca_wq, m_ca_wkv, m_ca_wo, m_ln2_g, m_ln2_b, m_ffn_w_up, m_ffn_conv_w, m_ffn_conv_b, m_ffn_w_down, m_ln3_g, m_ln3_b, v_w_in, v_b_in, v_hg_lb_logits, v_hg_norm_w, v_ml_conv_w, v_ml_conv_b, v_ml_norm_w, v_w_out, v_ln1_g, v_ln1_b, v_ca_wq, v_ca_wkv, v_ca_wo, v_ln2_g, v_ln2_b, v_ffn_w_up, v_ffn_conv_w, v_ffn_conv_b, v_ffn_w_down, v_ln3_g, v_ln3_b):
    params = dict(w_in=w_in, b_in=b_in, hg_lb_logits=hg_lb_logits, hg_norm_w=hg_norm_w, ml_conv_w=ml_conv_w,
                  ml_conv_b=ml_conv_b, ml_norm_w=ml_norm_w, w_out=w_out, ln1_g=ln1_g, ln1_b=ln1_b, ca_wq=ca_wq,
                  ca_wkv=ca_wkv, ca_wo=ca_wo, ln2_g=ln2_g, ln2_b=ln2_b, ffn_w_up=ffn_w_up, ffn_conv_w=ffn_conv_w,
                  ffn_conv_b=ffn_conv_b, ffn_w_down=ffn_w_down, ln3_g=ln3_g, ln3_b=ln3_b)
    mom1 = dict(w_in=m_w_in, b_in=m_b_in, hg_lb_logits=m_hg_lb_logits, hg_norm_w=m_hg_norm_w,
                ml_conv_w=m_ml_conv_w, ml_conv_b=m_ml_conv_b, ml_norm_w=m_ml_norm_w, w_out=m_w_out, ln1_g=m_ln1_g,
                ln1_b=m_ln1_b, ca_wq=m_ca_wq, ca_wkv=m_ca_wkv, ca_wo=m_ca_wo, ln2_g=m_ln2_g, ln2_b=m_ln2_b,
                ffn_w_up=m_ffn_w_up, ffn_conv_w=m_ffn_conv_w, ffn_conv_b=m_ffn_conv_b, ffn_w_down=m_ffn_w_down,
                ln3_g=m_ln3_g, ln3_b=m_ln3_b)
    mom2 = dict(w_in=v_w_in, b_in=v_b_in, hg_lb_logits=v_hg_lb_logits, hg_norm_w=v_hg_norm_w,
                ml_conv_w=v_ml_conv_w, ml_conv_b=v_ml_conv_b, ml_norm_w=v_ml_norm_w, w_out=v_w_out, ln1_g=v_ln1_g,
                ln1_b=v_ln1_b, ca_wq=v_ca_wq, ca_wkv=v_ca_wkv, ca_wo=v_ca_wo, ln2_g=v_ln2_g, ln2_b=v_ln2_b,
                ffn_w_up=v_ffn_w_up, ffn_conv_w=v_ffn_conv_w, ffn_conv_b=v_ffn_conv_b, ffn_w_down=v_ffn_w_down,
                ln3_g=v_ln3_g, ln3_b=v_ln3_b)

    x_idx, y_idx, c_idx = _coords()
    as_index = lambda v: jnp.reshape(v, (1,)).astype(jnp.int32)
    me = as_index(4 * x_idx + 2 * y_idx + c_idx)
    small_params = {n: params[n] for n in SMALL_NAMES}

    shards = {n: _shard_2d(n, params[n]) for n in SHARDED_NAMES}
    m_shards = {n: _shard_2d(n, mom1[n]) for n in SHARDED_NAMES}
    v_shards = {n: _shard_2d(n, mom2[n]) for n in SHARDED_NAMES}
    small_slabs = [_pack_small(params), _pack_small(mom1), _pack_small(mom2)]
    outgoing = {n: shards[n] if "conv" in n else shards[n].astype(BF16) for n in SHARDED_NAMES}
    to_send = lambda names: [outgoing[n] for n in names]
    glue = [*m_shards.values(), *v_shards.values(), *small_slabs, *shards.values(),
            *[outgoing[n] for n in SHARDED_NAMES if n not in FIRST_NAMES]]
    first = dict(zip(FIRST_NAMES, _two_level_gather(to_send(FIRST_NAMES), glue, "weights_gather_first")))
    mid_started, through = _direct_start(True, to_send(MID_NAMES), first["w_in"], "weights_gather_start_mid")
    ffn_started, through = _direct_start(True, to_send(FFN_UP_NAMES), through, "weights_gather_start_ffn_up")
    down_started, first["w_in"] = _direct_start(True, to_send(FFN_DOWN_NAMES), through,
                                                "weights_gather_start_ffn_down")

    def gathered_weights(names, started, after, tag):
        mine, lands = _direct_wait(True, started, after, "weights_gather_wait_" + tag)
        return {n: lax.dynamic_update_index_in_dim(land, own, me[0], 0) for n, own, land in zip(names, mine, lands)}

    started, own_stacks = {}, {}

    def start_group(names, tag):
        def hook(grads, through):
            own_stacks[tag] = [_owner_stack(n, grads).astype(BF16) for n in names]
            started[tag], through = _direct_start(False, own_stacks[tag], through, "grads_start_" + tag)
            return through
        return hook

    def start_small(grads, loss, through):
        started["small"], through = _direct_start(True, [_pack_small(_small_grads(grads), loss)], through,
                                                  "small_gather_start")
        return through

    loss, grad_x, grads = _local_step(
        x[0], mem[0], loss_target[0], _first_weights(first, small_params),
        lambda y: _mid_weights(gathered_weights(MID_NAMES, mid_started, y, "mid")),
        lambda x2: _ffn_up_weights(gathered_weights(FFN_UP_NAMES, ffn_started, x2, "ffn_up"), small_params),
        lambda hid: _ffn_down_weights(gathered_weights(FFN_DOWN_NAMES, down_started, hid, "ffn_down")),
        start_group(FFN_NAMES, "ffn"), start_group(MID_NAMES, "mid"), start_small, start_group(FIRST_NAMES, "last"))

    sharded_out = {}

    def update_group(names, tag, after):
        _, lands = _direct_wait(False, started[tag], after, "grads_wait_" + tag)
        for n, st, land in zip(names, own_stacks[tag], lands):
            res = _adamw_sharded(me, st, land, shards[n], m_shards[n], v_shards[n], "adamw_" + n)
            sharded_out[n] = [_shard_like(n, t, params[n]) for t in res]

    update_group(FFN_NAMES, "ffn", grad_x)
    update_group(MID_NAMES, "mid", grad_x)
    own_small, small_lands = _direct_wait(True, started["small"], grad_x, "small_gather_wait")
    small_parts = lax.dynamic_update_index_in_dim(small_lands[0], own_small[0], me[0], 0)
    small_res = _adamw_replicated(small_parts, *small_slabs)
    small_out = [_unpack_small(slab, params) for slab in small_res]
    done = [t for n in FFN_NAMES + MID_NAMES for t in sharded_out[n]]
    done += [t for small, _ in small_out for t in small.values()]
    update_group(FIRST_NAMES, "last", done)

    outs = []
    for k, (small, _) in enumerate(small_out):
        outs.extend(sharded_out[n][k] if n in sharded_out else small[n] for n in WEIGHT_NAMES)
    return (small_out[0][1], grad_x[None], *outs)
```

```python
import functools
import math

import jax
import jax.numpy as jnp
from jax import lax
from jax.experimental import pallas as pl
from jax.experimental.pallas import tpu as pltpu

F32 = jnp.float32
BF16 = jnp.bfloat16
HIGHEST = lax.Precision.HIGHEST
MESH = pl.DeviceIdType.MESH

N_DEV = 8
D_MODEL = 1024
N_MEM = 256
N_HEADS = 4
D_HEAD = 128
D_GROUP = N_HEADS * D_HEAD
CHUNK = 64
ML_CONV = 4
FFN_CONV = 3
D_FF = 2816
D_UP = 2 * D_FF
CA_HEADS = 4
CA_DH = D_MODEL // CA_HEADS
LANES = 128
SUBLANES = 8
D_IN = 8 * D_GROUP + 2 * N_HEADS
D_IN_MAIN = 8 * D_GROUP
W_IN_SHARD = D_IN // N_DEV
UP_SHARD = D_UP // N_DEV
UP_SHARD_P = 768
D_UP_P = N_DEV * UP_SHARD_P
D_FF_P = D_UP_P // 2
ALPHA = 2.0 ** 0.25
LN_EPS = 1e-5
NEG_BIG = -1e30
ADAM_LR = 0.001
ADAM_B1 = 0.9
ADAM_B2 = 0.999
ADAM_EPS = 1e-08
ADAM_WD = 0.01
ADAM_STEP = 10
VMEM_LIMIT = 56 * 1024 * 1024

SEG_HQ, SEG_HF, SEG_HI, SEG_HG, SEG_MQ, SEG_MK, SEG_MV, SEG_MO = (4 * i for i in range(8))


def _params(sem):
    return pltpu.CompilerParams(dimension_semantics=sem, vmem_limit_bytes=VMEM_LIMIT)


def _dg(a, b, ca, cb, precision=None):
    return lax.dot_general(a, b, (((ca,), (cb,)), ((), ())), precision=precision,
                           preferred_element_type=F32)


def _nn_raw(a, b):
    return _dg(a.astype(BF16), b.astype(BF16), 1, 0)


def _nt_raw(a, b):
    return _dg(a.astype(BF16), b.astype(BF16), 1, 1)


def _tn_raw(a, b):
    return _dg(a.astype(BF16), b.astype(BF16), 0, 0)


@jax.custom_vjp
def _nn(a, b):
    return _nn_raw(a, b)


_nn.defvjp(lambda a, b: (_nn_raw(a, b), (a, b)),
           lambda res, g: (_nt_raw(g, res[1]), _tn_raw(res[0], g)))


@jax.custom_vjp
def _nt(a, b):
    return _nt_raw(a, b)


_nt.defvjp(lambda a, b: (_nt_raw(a, b), (a, b)),
           lambda res, g: (_nn_raw(g, res[1]), _tn_raw(g, res[0])))


@jax.custom_vjp
def _tn(a, b):
    return _tn_raw(a, b)


_tn.defvjp(lambda a, b: (_tn_raw(a, b), (a, b)),
           lambda res, g: (_nt_raw(res[1], g), _nn_raw(res[0], g)))


def _layer_norm(z, g, b):
    mu = jnp.mean(z, axis=-1, keepdims=True)
    var = jnp.mean(jnp.square(z - mu), axis=-1, keepdims=True)
    return (z - mu) * lax.rsqrt(var + LN_EPS) * g + b


def _matmul_nn(a, w, bias, tm, tn, name, out_dtype=F32):
    m, k = a.shape
    if w.ndim == 3:
        n = w.shape[0] * w.shape[2]
        assert tn == w.shape[2]
        w_spec = pl.BlockSpec((None, k, tn), lambda i, j: (j, 0, 0))
    else:
        n = w.shape[1]
        w_spec = pl.BlockSpec((k, tn), lambda i, j: (0, j))

    def body(*refs):
        a_ref, w_ref = refs[0], refs[1]
        o_ref = refs[-1]
        acc = _nn_raw(a_ref[...], w_ref[...])
        if bias is not None:
            acc = acc + refs[2][...]
        o_ref[...] = acc.astype(o_ref.dtype)

    in_specs = [pl.BlockSpec((tm, k), lambda i, j: (i, 0)), w_spec]
    args = [a, w]
    if bias is not None:
        in_specs.append(pl.BlockSpec((1, tn), lambda i, j: (0, j)))
        args.append(bias)
    return pl.pallas_call(
        body, name=name, grid=(m // tm, n // tn), in_specs=in_specs,
        out_specs=pl.BlockSpec((tm, tn), lambda i, j: (i, j)),
        out_shape=jax.ShapeDtypeStruct((m, n), out_dtype),
        compiler_params=_params(("parallel", "parallel")),
    )(*args)


def _matmul_nt(pairs, add, scale, tm, tk, name, out_dtype=F32):
    m = pairs[0][0].shape[0]
    k = pairs[0][1].shape[-2]
    groups = []
    in_specs, args = [], []
    for pair in pairs:
        d, w = pair[0], pair[1]
        in_specs.append(pl.BlockSpec((tm, d.shape[1]), lambda i, j: (i, 0)))
        if w.ndim == 3:
            g = d.shape[1] // w.shape[2]
            blk = pair[2] // g
            in_specs.append(pl.BlockSpec((g, tk, w.shape[2]), lambda i, j, blk=blk: (blk, j, 0)))
            groups.append((g, w.shape[2]))
        else:
            in_specs.append(pl.BlockSpec((tk, w.shape[1]), lambda i, j: (j, 0)))
            groups.append(None)
        args += [d, w]
    if add is not None:
        in_specs.append(pl.BlockSpec((tm, tk), lambda i, j: (i, j)))
        args.append(add)

    def body(*refs):
        o_ref = refs[-1]
        acc = None
        for p, grp in enumerate(groups):
            d_ref, w_ref = refs[2 * p], refs[2 * p + 1]
            if grp is None:
                terms = [_nt_raw(d_ref[...], w_ref[...])]
            else:
                terms = [_nt_raw(d_ref[:, g * grp[1]:(g + 1) * grp[1]], w_ref[g]) for g in range(grp[0])]
            for t in terms:
                acc = t if acc is None else acc + t
        if add is not None:
            acc = acc + scale * refs[2 * len(groups)][...]
        o_ref[...] = acc.astype(o_ref.dtype)

    return pl.pallas_call(
        body, name=name, grid=(m // tm, k // tk), in_specs=in_specs,
        out_specs=pl.BlockSpec((tm, tk), lambda i, j: (i, j)),
        out_shape=jax.ShapeDtypeStruct((m, k), out_dtype),
        compiler_params=_params(("parallel", "parallel")),
    )(*args)


def _matmul_tn(a, b, tm, tn, tt, name, shards=None, shard0=0, group=1, into=None, colsum=False):
    t, m = a.shape
    n = b.shape[1]
    assert not colsum or tm == m
    n_in = 2 + (into is not None)
    out_dtype = BF16
    per_step = 1 if shards is None else group
    width = per_step * tn

    def body(*refs):
        a_ref, b_ref = refs[0], refs[1]
        o_ref, acc_ref = refs[n_in], refs[-1]
        first = pl.program_id(2) == 0

        @pl.when(first)
        def _():
            acc_ref[...] = jnp.zeros_like(acc_ref)

        if shards is None:
            acc_ref[...] += _tn_raw(a_ref[...], b_ref[...])
        else:
            lhs = a_ref[...].astype(BF16)
            for g in range(per_step):
                acc_ref[g] += _tn_raw(lhs, b_ref[:, g * tn:(g + 1) * tn])

        @pl.when(pl.program_id(2) == t // tt - 1)
        def _():
            o_ref[...] = acc_ref[...].astype(o_ref.dtype)

        if colsum:
            s_ref = refs[n_in + 1]

            @pl.when(first)
            def _():
                s_ref[...] = jnp.zeros_like(s_ref)

            s_ref[...] += jnp.sum(b_ref[...], axis=0, keepdims=True)

    in_specs = [pl.BlockSpec((tt, tm), lambda i, j, kk: (kk, i)),
                pl.BlockSpec((tt, width), lambda i, j, kk: (kk, j))]
    args = [a, b]
    aliases = {}
    if into is not None:
        in_specs.append(pl.BlockSpec(memory_space=pl.ANY))
        args.append(into)
        aliases = {2: 0}
    if shards is None:
        out_specs = [pl.BlockSpec((tm, tn), lambda i, j, kk: (i, j))]
        out_shape = [jax.ShapeDtypeStruct((m, n), out_dtype)]
        acc = pltpu.VMEM((tm, tn), F32)
    else:
        out_specs = [pl.BlockSpec((per_step, tm, tn), lambda i, j, kk: (shard0 // per_step + j, i, 0))]
        out_shape = [jax.ShapeDtypeStruct((shards, m, tn), out_dtype)]
        acc = pltpu.VMEM((per_step, tm, tn), F32)
    if colsum:
        out_specs.append(pl.BlockSpec((1, tn), lambda i, j, kk: (0, j)))
        out_shape.append(jax.ShapeDtypeStruct((1, n), F32))
    res = pl.pallas_call(
        body, name=name, grid=(m // tm, n // width, t // tt), in_specs=in_specs, out_specs=out_specs,
        out_shape=out_shape, input_output_aliases=aliases, scratch_shapes=[acc],
        compiler_params=_params(("parallel", "parallel", "arbitrary")),
    )(*args)
    return res if colsum else res[0]


ROW_TILE = 64


def _stack(ref, start, rows):
    return ref[pl.ds(start, rows), :].astype(F32).reshape(rows // SUBLANES, SUBLANES, LANES)


def _vreg_rows(ref, n):
    return [jnp.broadcast_to(ref[j:j + 1, :], (SUBLANES, LANES))[None] for j in range(n)]


def _column_total(acc):
    return jnp.sum(acc, axis=0, keepdims=True)


def _conv_fwd_tile(pad_ref, taps_w, bias, r0, rows):
    taps = len(taps_w)
    acc = bias
    for j in range(taps):
        acc = acc + _stack(pad_ref, SUBLANES - (taps - 1 - j) + r0, rows) * taps_w[j]
    return acc


def _conv_grads_tile(pad_ref, dpad_ref, dx_ref, taps_w, dws, r0, rows):
    taps = len(taps_w)
    x_rows = _stack(pad_ref, SUBLANES + r0, rows)
    dx = None
    for j in range(taps):
        d_shifted = _stack(dpad_ref, r0 + (taps - 1 - j), rows)
        term = d_shifted * taps_w[j]
        dx = term if dx is None else dx + term
        dws[j] = dws[j] + jnp.sum(d_shifted * x_rows, axis=0)
    dx_ref[r0:r0 + rows, :] = dx.reshape(rows, LANES).astype(dx_ref.dtype)
    return jnp.sum(dx, axis=0)


def _ml_conv_fwd(proj, conv_w, conv_b):
    s = proj.shape[0]
    nblk = 2 * D_GROUP // LANES

    def body(x_ref, w_ref, b_ref, o_ref, pad_ref):
        pad_ref[0:SUBLANES, :] = jnp.zeros((SUBLANES, LANES), F32)
        pad_ref[SUBLANES:, :] = x_ref[...].astype(F32)
        taps_w, bias = _vreg_rows(w_ref, ML_CONV), _vreg_rows(b_ref, 1)[0]
        for r0 in range(0, s, ROW_TILE):
            rows = min(ROW_TILE, s - r0)
            o_ref[r0:r0 + rows, :] = jax.nn.silu(_conv_fwd_tile(pad_ref, taps_w, bias, r0, rows)).reshape(rows, LANES)

    return pl.pallas_call(
        body, name="ml_conv_fwd", grid=(nblk,),
        in_specs=[pl.BlockSpec((s, LANES), lambda j: (0, SEG_MQ + j)),
                  pl.BlockSpec((ML_CONV, LANES), lambda j: (0, j)),
                  pl.BlockSpec((1, LANES), lambda j: (0, j))],
        out_specs=pl.BlockSpec((s, LANES), lambda j: (0, j)),
        out_shape=jax.ShapeDtypeStruct((s, 2 * D_GROUP), F32),
        scratch_shapes=[pltpu.VMEM((s + SUBLANES, LANES), F32)],
        compiler_params=_params(("parallel",)),
    )(proj, conv_w, conv_b)


def _ml_conv_bwd(proj, conv_w, conv_b, d_qk, d_proj):
    s = proj.shape[0]
    nblk = 2 * D_GROUP // LANES

    def body(x_ref, w_ref, b_ref, dy_ref, _, dx_ref, dw_ref, db_ref, dxs_ref, pad_ref, dpad_ref):
        pad_ref[0:SUBLANES, :] = jnp.zeros((SUBLANES, LANES), F32)
        pad_ref[SUBLANES:, :] = x_ref[...].astype(F32)
        dpad_ref[s:, :] = jnp.zeros((SUBLANES, LANES), F32)
        taps_w, bias = _vreg_rows(w_ref, ML_CONV), _vreg_rows(b_ref, 1)[0]
        db = jnp.zeros((SUBLANES, LANES), F32)
        for r0 in range(0, s, ROW_TILE):
            rows = min(ROW_TILE, s - r0)
            pre = _conv_fwd_tile(pad_ref, taps_w, bias, r0, rows)
            _, vjp = jax.vjp(jax.nn.silu, pre)
            d_pre, = vjp(_stack(dy_ref, r0, rows))
            dpad_ref[r0:r0 + rows, :] = d_pre.reshape(rows, LANES)
            db = db + jnp.sum(d_pre, axis=0)
        db_ref[...] = _column_total(db)
        dws = [jnp.zeros((SUBLANES, LANES), F32) for _ in range(ML_CONV)]
        dx_sum = jnp.zeros((SUBLANES, LANES), F32)
        for r0 in range(0, s, ROW_TILE):
            dx_sum = dx_sum + _conv_grads_tile(pad_ref, dpad_ref, dx_ref, taps_w, dws, r0, min(ROW_TILE, s - r0))
        dxs_ref[...] = _column_total(dx_sum)
        for j in range(ML_CONV):
            dw_ref[j:j + 1, :] = _column_total(dws[j])

    return pl.pallas_call(
        body, name="ml_conv_bwd", grid=(nblk,),
        in_specs=[pl.BlockSpec((s, LANES), lambda j: (0, SEG_MQ + j)),
                  pl.BlockSpec((ML_CONV, LANES), lambda j: (0, j)),
                  pl.BlockSpec((1, LANES), lambda j: (0, j)),
                  pl.BlockSpec((s, LANES), lambda j: (0, j)),
                  pl.BlockSpec(memory_space=pl.ANY)],
        out_specs=[pl.BlockSpec((s, LANES), lambda j: (0, SEG_MQ + j)),
                   pl.BlockSpec((ML_CONV, LANES), lambda j: (0, j)),
                   pl.BlockSpec((1, LANES), lambda j: (0, j)),
                   pl.BlockSpec((1, LANES), lambda j: (0, j))],
        out_shape=[jax.ShapeDtypeStruct(d_proj.shape, d_proj.dtype),
                   jax.ShapeDtypeStruct((ML_CONV, 2 * D_GROUP), F32),
                   jax.ShapeDtypeStruct((1, 2 * D_GROUP), F32),
                   jax.ShapeDtypeStruct((1, 2 * D_GROUP), F32)],
        input_output_aliases={4: 0},
        scratch_shapes=[pltpu.VMEM((s + SUBLANES, LANES), F32), pltpu.VMEM((s + SUBLANES, LANES), F32)],
        compiler_params=_params(("parallel",)),
    )(proj, conv_w, conv_b, d_qk, d_proj)


def _gelu_mul(a, b):
    return jax.nn.gelu(a) * b


GELU_C = math.sqrt(2.0 / math.pi)
GELU_K = 0.044715


def _gelu_mul_grads(a, b, d):
    a2 = a * a
    t = jnp.tanh(GELU_C * (a + GELU_K * (a * a2)))
    cdf = 0.5 * (1.0 + t)
    slope = cdf + (0.5 * GELU_C) * a * (1.0 - t * t) * (1.0 + (3.0 * GELU_K) * a2)
    return d * b * slope, d * (a * cdf)


FFN_BLOCKS = D_FF_P // LANES


def _ffn_conv_fwd(u, conv_w, conv_b):
    s = u.shape[0]

    def body(g_ref, v_ref, wg_ref, wv_ref, bg_ref, bv_ref, o_ref, gpad_ref, vpad_ref):
        for pad_ref, x_ref in ((gpad_ref, g_ref), (vpad_ref, v_ref)):
            pad_ref[0:SUBLANES, :] = jnp.zeros((SUBLANES, LANES), F32)
            pad_ref[SUBLANES:, :] = x_ref[...].astype(F32)
        taps_g, bias_g = _vreg_rows(wg_ref, FFN_CONV), _vreg_rows(bg_ref, 1)[0]
        taps_v, bias_v = _vreg_rows(wv_ref, FFN_CONV), _vreg_rows(bv_ref, 1)[0]
        for r0 in range(0, s, ROW_TILE):
            rows = min(ROW_TILE, s - r0)
            ug = _conv_fwd_tile(gpad_ref, taps_g, bias_g, r0, rows)
            uv = _conv_fwd_tile(vpad_ref, taps_v, bias_v, r0, rows)
            o_ref[r0:r0 + rows, :] = _gelu_mul(ug, uv).reshape(rows, LANES).astype(o_ref.dtype)

    col = lambda off: (lambda j: (0, off + j))
    return pl.pallas_call(
        body, name="ffn_conv_fwd", grid=(FFN_BLOCKS,),
        in_specs=[pl.BlockSpec((s, LANES), col(0)), pl.BlockSpec((s, LANES), col(FFN_BLOCKS)),
                  pl.BlockSpec((FFN_CONV, LANES), col(0)), pl.BlockSpec((FFN_CONV, LANES), col(FFN_BLOCKS)),
                  pl.BlockSpec((1, LANES), col(0)), pl.BlockSpec((1, LANES), col(FFN_BLOCKS))],
        out_specs=pl.BlockSpec((s, LANES), col(0)),
        out_shape=jax.ShapeDtypeStruct((s, D_FF_P), BF16),
        scratch_shapes=[pltpu.VMEM((s + SUBLANES, LANES), F32), pltpu.VMEM((s + SUBLANES, LANES), F32)],
        compiler_params=_params(("parallel",)),
    )(u, u, conv_w, conv_w, conv_b, conv_b)


def _ffn_conv_bwd(u, conv_w, conv_b, d_h):
    s = u.shape[0]

    def body(g_ref, v_ref, wg_ref, wv_ref, bg_ref, bv_ref, dh_ref,
             dug_ref, duv_ref, dwg_ref, dwv_ref, dbg_ref, dbv_ref,
             gpad_ref, vpad_ref, dgpad_ref, dvpad_ref):
        for pad_ref, x_ref in ((gpad_ref, g_ref), (vpad_ref, v_ref)):
            pad_ref[0:SUBLANES, :] = jnp.zeros((SUBLANES, LANES), F32)
            pad_ref[SUBLANES:, :] = x_ref[...].astype(F32)
        dgpad_ref[s:, :] = jnp.zeros((SUBLANES, LANES), F32)
        dvpad_ref[s:, :] = jnp.zeros((SUBLANES, LANES), F32)
        taps_g, bias_g = _vreg_rows(wg_ref, FFN_CONV), _vreg_rows(bg_ref, 1)[0]
        taps_v, bias_v = _vreg_rows(wv_ref, FFN_CONV), _vreg_rows(bv_ref, 1)[0]
        dbg = jnp.zeros((SUBLANES, LANES), F32)
        dbv = jnp.zeros((SUBLANES, LANES), F32)
        for r0 in range(0, s, ROW_TILE):
            rows = min(ROW_TILE, s - r0)
            ug = _conv_fwd_tile(gpad_ref, taps_g, bias_g, r0, rows)
            uv = _conv_fwd_tile(vpad_ref, taps_v, bias_v, r0, rows)
            d_ug, d_uv = _gelu_mul_grads(ug, uv, _stack(dh_ref, r0, rows))
            dgpad_ref[r0:r0 + rows, :] = d_ug.reshape(rows, LANES)
            dvpad_ref[r0:r0 + rows, :] = d_uv.reshape(rows, LANES)
            dbg = dbg + jnp.sum(d_ug, axis=0)
            dbv = dbv + jnp.sum(d_uv, axis=0)
        dbg_ref[...] = _column_total(dbg)
        dbv_ref[...] = _column_total(dbv)
        for pad_ref, dpad_ref, taps_w, dx_ref, dw_ref in ((gpad_ref, dgpad_ref, taps_g, dug_ref, dwg_ref),
                                                          (vpad_ref, dvpad_ref, taps_v, duv_ref, dwv_ref)):
            dws = [jnp.zeros((SUBLANES, LANES), F32) for _ in range(FFN_CONV)]
            for r0 in range(0, s, ROW_TILE):
                _conv_grads_tile(pad_ref, dpad_ref, dx_ref, taps_w, dws, r0, min(ROW_TILE, s - r0))
            for j in range(FFN_CONV):
                dw_ref[j:j + 1, :] = _column_total(dws[j])

    col = lambda off: (lambda j: (0, off + j))
    seq = pl.BlockSpec((s, LANES), col(0))
    return pl.pallas_call(
        body, name="ffn_conv_bwd", grid=(FFN_BLOCKS,),
        in_specs=[pl.BlockSpec((s, LANES), col(0)), pl.BlockSpec((s, LANES), col(FFN_BLOCKS)),
                  pl.BlockSpec((FFN_CONV, LANES), col(0)), pl.BlockSpec((FFN_CONV, LANES), col(FFN_BLOCKS)),
                  pl.BlockSpec((1, LANES), col(0)), pl.BlockSpec((1, LANES), col(FFN_BLOCKS)), seq],
        out_specs=[seq, seq, pl.BlockSpec((FFN_CONV, LANES), col(0)), pl.BlockSpec((FFN_CONV, LANES), col(0)),
                   pl.BlockSpec((1, LANES), col(0)), pl.BlockSpec((1, LANES), col(0))],
        out_shape=[jax.ShapeDtypeStruct((s, D_FF_P), BF16), jax.ShapeDtypeStruct((s, D_FF_P), BF16),
                   jax.ShapeDtypeStruct((FFN_CONV, D_FF_P), F32), jax.ShapeDtypeStruct((FFN_CONV, D_FF_P), F32),
                   jax.ShapeDtypeStruct((1, D_FF_P), F32), jax.ShapeDtypeStruct((1, D_FF_P), F32)],
        scratch_shapes=[pltpu.VMEM((s + SUBLANES, LANES), F32) for _ in range(4)],
        compiler_params=_params(("parallel",)),
    )(u, u, conv_w, conv_w, conv_b, conv_b, d_h)


def _chunk_masks(c):
    row = lax.broadcasted_iota(jnp.int32, (c, c), 0)
    col = lax.broadcasted_iota(jnp.int32, (c, c), 1)
    return row, col


@jax.custom_vjp
def _split_heads(x):
    return tuple(x[:, h * D_HEAD:(h + 1) * D_HEAD] for h in range(N_HEADS))


_split_heads.defvjp(lambda x: (_split_heads(x), None), lambda _, gs: (jnp.concatenate(gs, axis=1),))


@jax.custom_vjp
def _merge_heads(xs):
    return jnp.concatenate(xs, axis=1)


_merge_heads.defvjp(lambda xs: (_merge_heads(xs), None), lambda _, g: (_split_heads(g),))


@jax.custom_vjp
def _split_chunks(x):
    return tuple(x[i * CHUNK:(i + 1) * CHUNK] for i in range(x.shape[0] // CHUNK))


_split_chunks.defvjp(lambda x: (_split_chunks(x), None), lambda _, gs: (jnp.concatenate(gs, axis=0),))


@jax.custom_vjp
def _merge_chunks(xs):
    return jnp.concatenate(xs, axis=0)


_merge_chunks.defvjp(lambda xs: (_merge_chunks(xs), None), lambda _, g: (_split_chunks(g),))


def _blocks(x):
    return [_split_heads(rows) for rows in _split_chunks(x)]


def _per_chunk_rows(per_chunk, rid):
    out = per_chunk[0]
    for i in range(1, len(per_chunk)):
        out = jnp.where(rid >= i * CHUNK, per_chunk[i], out)
    return out


HEADS = range(N_HEADS)
CHUNKS_PER_STEP = 4
ML_CHUNKS_PER_STEP = 1


def _hg_chunk(hq, hf, hi, hgate, l0, l1, nw, sts):
    n = hq.shape[0] // CHUNK
    causal = _chunk_masks(CHUNK)
    causal = causal[1] <= causal[0]
    mx = lax.stop_gradient(jnp.maximum(l0, l1))
    e0 = jnp.exp(l0 - mx)
    e1 = jnp.exp(l1 - mx)
    lb = e0 / (e0 + e1)
    sig = jax.nn.sigmoid(hf)
    lf = jnp.log(lb + (1.0 - lb) * sig)
    k = (1.0 - lb) * jax.nn.sigmoid(-hf)
    q = jax.nn.silu(hq)
    tri = causal.astype(F32)
    b = _merge_chunks(tuple(_dg(tri, rows, 1, 0, HIGHEST) for rows in _split_chunks(lf)))
    rid = lax.broadcasted_iota(jnp.int32, b.shape, 0)
    pick = lambda r: jnp.sum(jnp.where(rid == r, b, 0.0), axis=0, keepdims=True)
    b_last_c = [pick(i * CHUNK + CHUNK - 1) for i in range(n)]
    b_ref = _per_chunk_rows([pick(i * CHUNK + CHUNK // 2 - 1) for i in range(n)], rid)
    b_last = _per_chunk_rows(b_last_c, rid)
    qa = _blocks(q * jnp.exp(b - b_ref))
    ka = _blocks(k * jnp.exp(b_ref - b))
    qe = _blocks(q * jnp.exp(b))
    kd = _blocks(k * jnp.exp(b_last - b))
    decay = [_split_heads(jnp.exp(b_last_c[i])) for i in range(n)]
    v = _blocks(hi)
    chunks = range(n)
    attn = [[jnp.where(causal, _nt(qa[i][h], ka[i][h]), 0.0) for h in HEADS] for i in chunks]
    intra = [[_nn(attn[i][h], v[i][h]) for h in HEADS] for i in chunks]
    kv = [[_tn(v[i][h], kd[i][h]) for h in HEADS] for i in chunks]
    normed = []
    for i in chunks:
        inter = [_nt(qe[i][h], sts[h]) for h in HEADS]
        sts = tuple(decay[i][h] * sts[h] + kv[i][h] for h in HEADS)
        o = [intra[i][h] + inter[h] for h in HEADS]
        normed.append(_merge_heads(tuple(o[h] * lax.rsqrt(jnp.mean(o[h] * o[h], axis=-1, keepdims=True) + LN_EPS)
                                         for h in HEADS)))
    return _merge_chunks(tuple(normed)) * nw * jax.nn.silu(hgate), sts


def _seg(ref, seg):
    return ref[:, seg * D_GROUP:(seg + 1) * D_GROUP]


def _hgrn2_fwd(proj, logits, norm_w):
    s = proj.shape[0]
    rows = CHUNKS_PER_STEP * CHUNK
    nc = s // rows

    def body(p_ref, lg_ref, nw_ref, y_ref, st_out_ref, st_scr):
        @pl.when(pl.program_id(0) == 0)
        def _():
            st_scr[...] = jnp.zeros_like(st_scr)

        sts = tuple(st_scr[h] for h in HEADS)
        y, sts_new = _hg_chunk(_seg(p_ref, 0), _seg(p_ref, 1), _seg(p_ref, 2), _seg(p_ref, 3),
                               lg_ref[0:1, :], lg_ref[1:2, :], nw_ref[...], sts)
        y_ref[...] = y.astype(y_ref.dtype)
        for h in HEADS:
            st_out_ref[h] = sts[h]
            st_scr[h] = sts_new[h]

    return pl.pallas_call(
        body, name="hgrn2_fwd", grid=(nc,),
        in_specs=[pl.BlockSpec((rows, 4 * D_GROUP), lambda c: (c, 0)),
                  pl.BlockSpec((2, D_GROUP), lambda c: (0, 0)),
                  pl.BlockSpec((1, D_GROUP), lambda c: (0, 0))],
        out_specs=[pl.BlockSpec((rows, D_GROUP), lambda c: (c, 0)),
                   pl.BlockSpec((None, N_HEADS, D_HEAD, D_HEAD), lambda c: (c, 0, 0, 0))],
        out_shape=[jax.ShapeDtypeStruct((s, 2 * D_GROUP), BF16),
                   jax.ShapeDtypeStruct((nc, N_HEADS, D_HEAD, D_HEAD), F32)],
        scratch_shapes=[pltpu.VMEM((N_HEADS, D_HEAD, D_HEAD), F32)],
        compiler_params=_params(("arbitrary",)),
    )(proj, logits, norm_w)


def _hgrn2_bwd(proj, logits, norm_w, states, d_y):
    s = proj.shape[0]
    rows = CHUNKS_PER_STEP * CHUNK
    nc = s // rows

    def body(p_ref, lg_ref, nw_ref, st_ref, dy_ref, dp_ref, dl_ref, dnw_ref, dsum_ref, dst_scr):
        @pl.when(pl.program_id(0) == 0)
        def _():
            dst_scr[...] = jnp.zeros_like(dst_scr)
            dl_ref[...] = jnp.zeros_like(dl_ref)
            dnw_ref[...] = jnp.zeros_like(dnw_ref)
            dsum_ref[...] = jnp.zeros_like(dsum_ref)

        _, vjp = jax.vjp(_hg_chunk, _seg(p_ref, 0), _seg(p_ref, 1), _seg(p_ref, 2), _seg(p_ref, 3),
                         lg_ref[0:1, :], lg_ref[1:2, :], nw_ref[...], tuple(st_ref[h] for h in HEADS))
        d_hq, d_hf, d_hi, d_hg, d_l0, d_l1, d_nw, d_sts = vjp((dy_ref[...], tuple(dst_scr[h] for h in HEADS)))
        for seg, val in enumerate((d_hq, d_hf, d_hi, d_hg)):
            dp_ref[:, seg * D_GROUP:(seg + 1) * D_GROUP] = val.astype(dp_ref.dtype)
            dsum_ref[:, seg * D_GROUP:(seg + 1) * D_GROUP] += jnp.sum(val, axis=0, keepdims=True)
        dl_ref[0:1, :] += d_l0
        dl_ref[1:2, :] += d_l1
        dnw_ref[...] += d_nw
        for h in HEADS:
            dst_scr[h] = d_sts[h]

    rev = lambda c: nc - 1 - c
    return pl.pallas_call(
        body, name="hgrn2_bwd", grid=(nc,),
        in_specs=[pl.BlockSpec((rows, 4 * D_GROUP), lambda c: (rev(c), 0)),
                  pl.BlockSpec((2, D_GROUP), lambda c: (0, 0)),
                  pl.BlockSpec((1, D_GROUP), lambda c: (0, 0)),
                  pl.BlockSpec((None, N_HEADS, D_HEAD, D_HEAD), lambda c: (rev(c), 0, 0, 0)),
                  pl.BlockSpec((rows, D_GROUP), lambda c: (rev(c), 0))],
        out_specs=[pl.BlockSpec((rows, 4 * D_GROUP), lambda c: (rev(c), 0)),
                   pl.BlockSpec((2, D_GROUP), lambda c: (0, 0)),
                   pl.BlockSpec((1, D_GROUP), lambda c: (0, 0)),
                   pl.BlockSpec((1, 4 * D_GROUP), lambda c: (0, 0))],
        out_shape=[jax.ShapeDtypeStruct((s, D_IN_MAIN), BF16), jax.ShapeDtypeStruct((2, D_GROUP), F32),
                   jax.ShapeDtypeStruct((1, D_GROUP), F32), jax.ShapeDtypeStruct((1, 4 * D_GROUP), F32)],
        scratch_shapes=[pltpu.VMEM((N_HEADS, D_HEAD, D_HEAD), F32)],
        compiler_params=_params(("arbitrary",)),
    )(proj, logits, norm_w, states, d_y)


def _gate_column(gates, lane, idx):
    return jnp.sum(jnp.where(lane == idx, gates, 0.0), axis=1, keepdims=True)


def _head_layer_norm(h):
    mu = jnp.mean(h, axis=-1, keepdims=True)
    var = jnp.mean(jnp.square(h - mu), axis=-1, keepdims=True)
    return (h - mu) * lax.rsqrt(var + LN_EPS)


def _ml_chunk(qc, kc, v, mo, gates, nw, cts, ns, ms):
    n = qc.shape[0] // CHUNK
    row, col = _chunk_masks(CHUNK)
    mask = col <= row
    eye = col == row
    to_row = lambda t: jnp.sum(jnp.where(eye, t, 0.0), axis=0, keepdims=True)
    q = _blocks(qc * (D_HEAD ** -0.5))
    k = _blocks(kc)
    vs = _blocks(v)
    gate_rows = _split_chunks(gates)
    lane = lax.broadcasted_iota(jnp.int32, gate_rows[0].shape, 1)
    each = [(i, h) for i in range(n) for h in HEADS]
    on_each = lambda f: {ih: f(*ih) for ih in each}
    ig = on_each(lambda i, h: _gate_column(gate_rows[i], lane, h))
    lf = on_each(lambda i, h: jax.nn.log_sigmoid(_gate_column(gate_rows[i], lane, N_HEADS + h)))
    lf_row = on_each(lambda i, h: to_row(lf[i, h]))
    ig_row = on_each(lambda i, h: to_row(ig[i, h]))
    b_col = on_each(lambda i, h: jnp.sum(jnp.where(mask, lf_row[i, h], 0.0), axis=1, keepdims=True))
    b_row = on_each(lambda i, h: jnp.sum(jnp.where(row <= col, lf[i, h], 0.0), axis=0, keepdims=True))
    g = on_each(lambda i, h: jnp.sum(lf[i, h], axis=0, keepdims=True))
    d = on_each(lambda i, h: jnp.where(mask, b_col[i, h] - b_row[i, h] + ig_row[i, h], -jnp.inf))
    a = on_each(lambda i, h: g[i, h] - b_col[i, h] + ig[i, h])
    m_at = {(0, h): ms[h] for h in HEADS}
    for i, h in each:
        m_at[i + 1, h] = lax.stop_gradient(jnp.maximum(g[i, h] + m_at[i, h], jnp.max(a[i, h], axis=0, keepdims=True)))
    inter = on_each(lambda i, h: b_col[i, h] + m_at[i, h])
    m_t = on_each(lambda i, h: lax.stop_gradient(jnp.maximum(inter[i, h], jnp.max(d[i, h], axis=1, keepdims=True))))
    qk = on_each(lambda i, h: _nt(q[i][h], k[i][h]))
    sc = on_each(lambda i, h: qk[i, h] * jnp.exp(d[i, h] - m_t[i, h]))
    w_inter = on_each(lambda i, h: jnp.exp(inter[i, h] - m_t[i, h]))
    sv = on_each(lambda i, h: _nn(sc[i, h], vs[i][h]))
    decay = on_each(lambda i, h: jnp.exp(g[i, h] + m_at[i, h] - m_at[i + 1, h]))
    wk = on_each(lambda i, h: k[i][h] * jnp.exp(a[i, h] - m_at[i + 1, h]))
    kv = on_each(lambda i, h: _tn(vs[i][h], wk[i, h]))
    normed = []
    for i in range(n):
        qc_state = [_nt(q[i][h], cts[h]) for h in HEADS]
        num = [sv[i, h] + w_inter[i, h] * qc_state[h] for h in HEADS]
        den = [jnp.sum(sc[i, h], axis=1, keepdims=True)
               + w_inter[i, h] * jnp.sum(q[i][h] * ns[h], axis=1, keepdims=True) for h in HEADS]
        hh = [num[h] / jnp.maximum(jnp.abs(den[h]), jnp.exp(-m_t[i, h])) for h in HEADS]
        cts = tuple(decay[i, h] * cts[h] + kv[i, h] for h in HEADS)
        ns = tuple(decay[i, h] * ns[h] + jnp.sum(wk[i, h], axis=0, keepdims=True) for h in HEADS)
        normed.append(_merge_heads(tuple(_head_layer_norm(hh[h]) for h in HEADS)))
    y = jax.nn.sigmoid(mo) * (_merge_chunks(tuple(normed)) * nw)
    return y, cts, ns, tuple(m_at[n, h] for h in HEADS)


def _mlstm_fwd(qk, proj, gates, norm_w, y):
    s = proj.shape[0]
    rows = ML_CHUNKS_PER_STEP * CHUNK
    nc = s // rows

    def body(qk_ref, vo_ref, g_ref, nw_ref, _, y_ref, ct_out, n_out, m_out, ct_scr, n_scr, m_scr):
        @pl.when(pl.program_id(0) == 0)
        def _():
            ct_scr[...] = jnp.zeros_like(ct_scr)
            n_scr[...] = jnp.zeros_like(n_scr)
            m_scr[...] = jnp.full(m_scr.shape, NEG_BIG, F32)

        cts = tuple(ct_scr[h] for h in HEADS)
        ns = tuple(n_scr[h] for h in HEADS)
        ms = tuple(m_scr[h] for h in HEADS)
        y, cts_new, ns_new, ms_new = _ml_chunk(_seg(qk_ref, 0), _seg(qk_ref, 1), _seg(vo_ref, 0), _seg(vo_ref, 1),
                                               g_ref[...], nw_ref[...], cts, ns, ms)
        y_ref[...] = y.astype(y_ref.dtype)
        for h in HEADS:
            ct_out[h], n_out[h], m_out[h] = cts[h], ns[h], ms[h]
            ct_scr[h], n_scr[h], m_scr[h] = cts_new[h], ns_new[h], ms_new[h]

    st = lambda r, w: pl.BlockSpec((None, N_HEADS, r, w), lambda c: (c, 0, 0, 0))
    return pl.pallas_call(
        body, name="mlstm_fwd", grid=(nc,),
        in_specs=[pl.BlockSpec((rows, 2 * D_GROUP), lambda c: (c, 0)),
                  pl.BlockSpec((rows, 2 * D_GROUP), lambda c: (c, 3)),
                  pl.BlockSpec((rows, LANES), lambda c: (c, 0)),
                  pl.BlockSpec((1, D_GROUP), lambda c: (0, 0)),
                  pl.BlockSpec(memory_space=pl.ANY)],
        out_specs=[pl.BlockSpec((rows, D_GROUP), lambda c: (c, 1)),
                   st(D_HEAD, D_HEAD), st(1, D_HEAD), st(1, 1)],
        out_shape=[jax.ShapeDtypeStruct(y.shape, y.dtype),
                   jax.ShapeDtypeStruct((nc, N_HEADS, D_HEAD, D_HEAD), F32),
                   jax.ShapeDtypeStruct((nc, N_HEADS, 1, D_HEAD), F32),
                   jax.ShapeDtypeStruct((nc, N_HEADS, 1, 1), F32)],
        input_output_aliases={4: 0},
        scratch_shapes=[pltpu.VMEM((N_HEADS, D_HEAD, D_HEAD), F32), pltpu.VMEM((N_HEADS, 1, D_HEAD), F32),
                        pltpu.VMEM((N_HEADS, 1, 1), F32)],
        compiler_params=_params(("arbitrary",)),
    )(qk, proj, gates, norm_w, y)


def _mlstm_bwd(qk, proj, gates, norm_w, ct_s, n_s, m_s, d_y, d_proj):
    s = proj.shape[0]
    rows = ML_CHUNKS_PER_STEP * CHUNK
    nc = s // rows

    def body(qk_ref, vo_ref, g_ref, nw_ref, ct_ref, n_ref, m_ref, dy_ref, _,
             dp_ref, dqk_ref, dg_ref, dnw_ref, dsum_ref, dct_scr, dn_scr):
        @pl.when(pl.program_id(0) == 0)
        def _():
            dct_scr[...] = jnp.zeros_like(dct_scr)
            dn_scr[...] = jnp.zeros_like(dn_scr)
            dnw_ref[...] = jnp.zeros_like(dnw_ref)
            dsum_ref[...] = jnp.zeros_like(dsum_ref)

        ms = tuple(m_ref[h] for h in HEADS)
        step = lambda *a: _ml_chunk(*a, ms)[:3]
        _, vjp = jax.vjp(step, _seg(qk_ref, 0), _seg(qk_ref, 1), _seg(vo_ref, 0), _seg(vo_ref, 1), g_ref[...],
                         nw_ref[...], tuple(ct_ref[h] for h in HEADS), tuple(n_ref[h] for h in HEADS))
        d_q, d_k, d_v, d_o, d_gates, d_nw, d_cts, d_ns = vjp(
            (dy_ref[...], tuple(dct_scr[h] for h in HEADS), tuple(dn_scr[h] for h in HEADS)))
        dqk_ref[:, 0:D_GROUP] = d_q
        dqk_ref[:, D_GROUP:2 * D_GROUP] = d_k
        for seg, val in enumerate((d_v, d_o)):
            dp_ref[:, seg * D_GROUP:(seg + 1) * D_GROUP] = val.astype(dp_ref.dtype)
            dsum_ref[:, seg * D_GROUP:(seg + 1) * D_GROUP] += jnp.sum(val, axis=0, keepdims=True)
        dg_ref[...] = d_gates
        dnw_ref[...] += d_nw
        for h in HEADS:
            dct_scr[h] = d_cts[h]
            dn_scr[h] = d_ns[h]

    rev = lambda c: nc - 1 - c
    st = lambda r, w: pl.BlockSpec((None, N_HEADS, r, w), lambda c: (rev(c), 0, 0, 0))
    return pl.pallas_call(
        body, name="mlstm_bwd", grid=(nc,),
        in_specs=[pl.BlockSpec((rows, 2 * D_GROUP), lambda c: (rev(c), 0)),
                  pl.BlockSpec((rows, 2 * D_GROUP), lambda c: (rev(c), 3)),
                  pl.BlockSpec((rows, LANES), lambda c: (rev(c), 0)),
                  pl.BlockSpec((1, D_GROUP), lambda c: (0, 0)),
                  st(D_HEAD, D_HEAD), st(1, D_HEAD), st(1, 1),
                  pl.BlockSpec((rows, D_GROUP), lambda c: (rev(c), 1)),
                  pl.BlockSpec(memory_space=pl.ANY)],
        out_specs=[pl.BlockSpec((rows, 2 * D_GROUP), lambda c: (rev(c), 3)),
                   pl.BlockSpec((rows, 2 * D_GROUP), lambda c: (rev(c), 0)),
                   pl.BlockSpec((rows, LANES), lambda c: (rev(c), 0)),
                   pl.BlockSpec((1, D_GROUP), lambda c: (0, 0)),
                   pl.BlockSpec((1, 2 * D_GROUP), lambda c: (0, 0))],
        out_shape=[jax.ShapeDtypeStruct(d_proj.shape, d_proj.dtype), jax.ShapeDtypeStruct((s, 2 * D_GROUP), F32),
                   jax.ShapeDtypeStruct((s, LANES), F32), jax.ShapeDtypeStruct((1, D_GROUP), F32),
                   jax.ShapeDtypeStruct((1, 2 * D_GROUP), F32)],
        input_output_aliases={8: 0},
        scratch_shapes=[pltpu.VMEM((N_HEADS, D_HEAD, D_HEAD), F32), pltpu.VMEM((N_HEADS, 1, D_HEAD), F32)],
        compiler_params=_params(("arbitrary",)),
    )(qk, proj, gates, norm_w, ct_s, n_s, m_s, d_y, d_proj)


LN_TOKENS = 512
ATT_TOKENS = 512


def _proj_res_ln(a, w, xres, g, b, name):
    s, dm = xres.shape
    k = a.shape[1]
    tb = min(LN_TOKENS, s)

    def body(a_ref, w_ref, x_ref, g_ref, b_ref, z_ref, o_ref):
        z = ALPHA * x_ref[...] + _nn_raw(a_ref[...], w_ref[...])
        z_ref[...] = z
        o_ref[...] = _layer_norm(z, g_ref[...], b_ref[...])

    tok = pl.BlockSpec((tb, dm), lambda i: (i, 0))
    vec = pl.BlockSpec((1, dm), lambda i: (0, 0))
    act = jax.ShapeDtypeStruct((s, dm), F32)
    return pl.pallas_call(
        body, name=name, grid=(s // tb,),
        in_specs=[pl.BlockSpec((tb, k), lambda i: (i, 0)), pl.BlockSpec((k, dm), lambda i: (0, 0)), tok, vec, vec],
        out_specs=[tok, tok], out_shape=[act, act], compiler_params=_params(("parallel",)),
    )(a, w, xres, g, b)


def _ln_bwd_proj(d_out, z, g, b, w, name):
    s, dm = z.shape
    k = w.shape[0]
    tb = min(LN_TOKENS, s)

    def body(do_ref, z_ref, g_ref, b_ref, w_ref, dz_ref, da_ref, dg_ref, db_ref):
        @pl.when(pl.program_id(0) == 0)
        def _():
            dg_ref[...] = jnp.zeros_like(dg_ref)
            db_ref[...] = jnp.zeros_like(db_ref)

        _, vjp = jax.vjp(_layer_norm, z_ref[...], g_ref[...], b_ref[...])
        d_z, d_g, d_b = vjp(do_ref[...])
        dz_ref[...] = d_z
        da_ref[...] = _nt_raw(d_z, w_ref[...])
        dg_ref[...] += d_g
        db_ref[...] += d_b

    tok = pl.BlockSpec((tb, dm), lambda i: (i, 0))
    vec = pl.BlockSpec((1, dm), lambda i: (0, 0))
    return pl.pallas_call(
        body, name=name, grid=(s // tb,),
        in_specs=[tok, tok, vec, vec, pl.BlockSpec((k, dm), lambda i: (0, 0))],
        out_specs=[tok, pl.BlockSpec((tb, k), lambda i: (i, 0)), vec, vec],
        out_shape=[jax.ShapeDtypeStruct((s, dm), F32), jax.ShapeDtypeStruct((s, k), F32),
                   jax.ShapeDtypeStruct((1, dm), F32), jax.ShapeDtypeStruct((1, dm), F32)],
        compiler_params=_params(("arbitrary",)),
    )(d_out, z, g, b, w)


def _proj_loss_tail(a, w, xres, g, b, target):
    s, dm = xres.shape
    k = a.shape[1]
    tb = min(ATT_TOKENS, s)

    def loss_fn(z, gg, bb, tgt):
        err = jnp.square(_layer_norm(z, gg, bb) - tgt)
        return 0.5 * jnp.sum(jnp.mean(err, axis=-1, keepdims=True), axis=0, keepdims=True)

    def body(a_ref, w_ref, x_ref, g_ref, b_ref, t_ref, loss_ref, dz_ref, dg_ref, db_ref):
        @pl.when(pl.program_id(0) == 0)
        def _():
            loss_ref[...] = jnp.zeros_like(loss_ref)
            dg_ref[...] = jnp.zeros_like(dg_ref)
            db_ref[...] = jnp.zeros_like(db_ref)

        halves = [slice(0, tb // 2), slice(tb // 2, tb)]
        zs = [ALPHA * x_ref[rows, :] + _nn_raw(a_ref[rows, :], w_ref[...]) for rows in halves]
        for rows, z in zip(halves, zs):
            tgt = t_ref[rows, :]
            loss, vjp = jax.vjp(lambda zz, gg, bb, tgt=tgt: loss_fn(zz, gg, bb, tgt), z, g_ref[...], b_ref[...])
            d_z, d_g, d_b = vjp(jnp.ones((1, 1), F32))
            loss_ref[...] += loss
            dz_ref[rows, :] = d_z
            dg_ref[...] += d_g
            db_ref[...] += d_b

    tok = pl.BlockSpec((tb, dm), lambda i: (i, 0))
    vec = pl.BlockSpec((1, dm), lambda i: (0, 0))
    one = pl.BlockSpec((1, 1), lambda i: (0, 0))
    return pl.pallas_call(
        body, name="ffn_down_loss_tail", grid=(s // tb,),
        in_specs=[pl.BlockSpec((tb, k), lambda i: (i, 0)), pl.BlockSpec((k, dm), lambda i: (0, 0)), tok, vec, vec, tok],
        out_specs=[one, tok, vec, vec],
        out_shape=[jax.ShapeDtypeStruct((1, 1), F32), jax.ShapeDtypeStruct((s, dm), F32),
                   jax.ShapeDtypeStruct((1, dm), F32), jax.ShapeDtypeStruct((1, dm), F32)],
        compiler_params=_params(("arbitrary",)),
    )(a, w, xres, g, b, target)


def _att_heads(qs, ks, vs):
    sc = [_nt(q, k) * (CA_DH ** -0.5) for q, k in zip(qs, ks)]
    p = [jax.nn.softmax(s, axis=-1) for s in sc]
    return tuple(_nn(pp, v) for pp, v in zip(p, vs))


def _head_slices(ref_or_value, offset):
    return tuple(ref_or_value[:, offset + h * CA_DH:offset + (h + 1) * CA_DH] for h in range(CA_HEADS))


def _cross_attention_fwd(x1, kv, wq, wo, g, b):
    s = x1.shape[0]
    tb = min(ATT_TOKENS, s)

    def body(x_ref, kv_ref, wq_ref, wo_ref, g_ref, b_ref, att_ref, z_ref, o_ref):
        x_blk = x_ref[...]
        q = _nn_raw(x_blk, wq_ref[...])
        att = jnp.concatenate(_att_heads(_head_slices(q, 0), _head_slices(kv_ref, 0), _head_slices(kv_ref, D_MODEL)),
                              axis=1)
        att_ref[...] = att.astype(att_ref.dtype)
        z = ALPHA * x_blk + _nn_raw(att, wo_ref[...])
        z_ref[...] = z
        o_ref[...] = _layer_norm(z, g_ref[...], b_ref[...])

    tok = pl.BlockSpec((tb, D_MODEL), lambda i: (i, 0))
    mat = pl.BlockSpec((D_MODEL, D_MODEL), lambda i: (0, 0))
    vec = pl.BlockSpec((1, D_MODEL), lambda i: (0, 0))
    act = jax.ShapeDtypeStruct((s, D_MODEL), F32)
    return pl.pallas_call(
        body, name="cross_attention_fwd", grid=(s // tb,),
        in_specs=[tok, pl.BlockSpec((N_MEM, 2 * D_MODEL), lambda i: (0, 0)), mat, mat, vec, vec],
        out_specs=[tok, tok, tok],
        out_shape=[jax.ShapeDtypeStruct((s, D_MODEL), BF16), act, act],
        compiler_params=_params(("parallel",)),
    )(x1, kv, wq, wo, g, b)


def _cross_attention_bwd(d_x2, x1, z2, kv, wq, wo, g, b):
    s = x1.shape[0]
    tb = min(ATT_TOKENS, s)

    def body(dx2_ref, x_ref, z_ref, kv_ref, wq_ref, wo_ref, g_ref, b_ref,
             dx1_ref, dq_ref, dz_ref, dkv_ref, dg_ref, db_ref):
        @pl.when(pl.program_id(0) == 0)
        def _():
            dkv_ref[...] = jnp.zeros_like(dkv_ref)
            dg_ref[...] = jnp.zeros_like(dg_ref)
            db_ref[...] = jnp.zeros_like(db_ref)

        _, ln_vjp = jax.vjp(_layer_norm, z_ref[...], g_ref[...], b_ref[...])
        d_z, d_g, d_b = ln_vjp(dx2_ref[...])
        dg_ref[...] += d_g
        db_ref[...] += d_b
        dz_ref[...] = d_z.astype(dz_ref.dtype)
        d_att = _nt_raw(d_z, wo_ref[...])
        q = _nn_raw(x_ref[...], wq_ref[...])
        _, vjp = jax.vjp(_att_heads, _head_slices(q, 0), _head_slices(kv_ref, 0), _head_slices(kv_ref, D_MODEL))
        d_qs, d_ks, d_vs = vjp(_head_slices(d_att, 0))
        for h in range(CA_HEADS):
            lo = h * CA_DH
            dkv_ref[:, lo:lo + CA_DH] += d_ks[h]
            dkv_ref[:, D_MODEL + lo:D_MODEL + lo + CA_DH] += d_vs[h]
        d_q = jnp.concatenate(d_qs, axis=1)
        dq_ref[...] = d_q.astype(dq_ref.dtype)
        dx1_ref[...] = ALPHA * d_z + _nt_raw(d_q, wq_ref[...])

    tok = pl.BlockSpec((tb, D_MODEL), lambda i: (i, 0))
    mem = pl.BlockSpec((N_MEM, 2 * D_MODEL), lambda i: (0, 0))
    mat = pl.BlockSpec((D_MODEL, D_MODEL), lambda i: (0, 0))
    vec = pl.BlockSpec((1, D_MODEL), lambda i: (0, 0))
    low = jax.ShapeDtypeStruct((s, D_MODEL), BF16)
    return pl.pallas_call(
        body, name="cross_attention_bwd", grid=(s // tb,),
        in_specs=[tok, tok, tok, mem, mat, mat, vec, vec], out_specs=[tok, tok, tok, mem, vec, vec],
        out_shape=[jax.ShapeDtypeStruct((s, D_MODEL), F32), low, low,
                   jax.ShapeDtypeStruct((N_MEM, 2 * D_MODEL), F32),
                   jax.ShapeDtypeStruct((1, D_MODEL), F32), jax.ShapeDtypeStruct((1, D_MODEL), F32)],
        compiler_params=_params(("arbitrary",)),
    )(d_x2, x1, z2, kv, wq, wo, g, b)


def _local_step(x, mem, target, w, mid_weights=None, ffn_weights=None, down_weights=None, on_ffn_grads=None,
                on_mid_grads=None,
                on_small_grads=None, on_last_grads=None):
    w = dict(w)
    s = x.shape[0]
    tm = min(512, s)
    tt = min(512, s)
    proj = _matmul_nn(x, w["w_in_main"], w["b_in_main"], min(2048, s), 512, "proj")
    gates = _matmul_nn(x, w["w_in_gate"], w["b_in_gate"], tm, LANES, "proj_gates")
    qk = _ml_conv_fwd(proj, w["ml_conv_w"], w["ml_conv_b"])
    y, hg_states = _hgrn2_fwd(proj, w["hg_lb_logits"], w["hg_norm_w"])
    y, ct_s, n_s, m_s = _mlstm_fwd(qk, proj, gates, w["ml_norm_w"], y)
    if mid_weights is not None:
        w.update(mid_weights(y))
    z1, x1 = _proj_res_ln(y, w["w_out"], x, w["ln1_g"], w["ln1_b"], "out_proj_ln1")
    kv = _matmul_nn(mem, w["ca_wkv"], None, N_MEM, CA_DH, "kv")
    att, z2, x2 = _cross_attention_fwd(x1, kv, w["ca_wq"], w["ca_wo"], w["ln2_g"], w["ln2_b"])
    if ffn_weights is not None:
        w.update(ffn_weights(x2))
    u = _matmul_nn(x2, w["ffn_w_up"], None, min(2048, s), UP_SHARD_P, "ffn_up", BF16)
    hid = _ffn_conv_fwd(u, w["ffn_conv_w"], w["ffn_conv_b"])
    if down_weights is not None:
        w.update(down_weights(hid))
    loss, d_z3, d_ln3_g, d_ln3_b = _proj_loss_tail(hid, w["ffn_w_down"], x2, w["ln3_g"], w["ln3_b"], target)
    grads = {"ln3_g": d_ln3_g, "ln3_b": d_ln3_b}
    grads["ffn_w_down"] = _matmul_tn(hid, d_z3, 1536, D_MODEL, tt, "d_w_down")
    d_hid = _matmul_nt([(d_z3, w["ffn_w_down"])], None, 1.0, tm, D_FF_P, "d_hid", BF16)
    d_ug, d_uv, d_cwg, d_cwv, d_cbg, d_cbv = _ffn_conv_bwd(u, w["ffn_conv_w"], w["ffn_conv_b"], d_hid)
    grads["ffn_conv_w"] = jnp.concatenate([d_cwg, d_cwv], axis=-1)
    grads["ffn_conv_b"] = jnp.concatenate([d_cbg, d_cbv], axis=-1)
    half = N_DEV // 2
    d_w_up = _matmul_tn(x2, d_ug, D_MODEL, UP_SHARD_P, tt, "d_w_up_gate", shards=N_DEV, group=half)
    grads["ffn_w_up"] = _matmul_tn(x2, d_uv, D_MODEL, UP_SHARD_P, tt, "d_w_up_val", shards=N_DEV,
                                   shard0=half, group=half, into=d_w_up)
    d_x2 = _matmul_nt([(d_ug, w["ffn_w_up"], 0), (d_uv, w["ffn_w_up"], N_DEV // 2)], d_z3, ALPHA,
                      min(256, s), D_MODEL, "d_x2")
    if on_ffn_grads is not None:
        d_x2 = on_ffn_grads(grads, d_x2)
    d_x1, d_q, d_z2, d_kv, grads["ln2_g"], grads["ln2_b"] = _cross_attention_bwd(
        d_x2, x1, z2, kv, w["ca_wq"], w["ca_wo"], w["ln2_g"], w["ln2_b"])
    grads["ca_wo"] = _matmul_tn(att, d_z2, D_MODEL, D_MODEL, tt, "d_ca_wo")
    grads["ca_wq"] = _matmul_tn(x1, d_q, D_MODEL, D_MODEL, tt, "d_ca_wq")
    grads["ca_wkv"] = _matmul_tn(mem, d_kv, D_MODEL, CA_DH, N_MEM, "d_ca_wkv", shards=N_DEV, group=N_DEV)
    d_z1, d_y, grads["ln1_g"], grads["ln1_b"] = _ln_bwd_proj(d_x1, z1, w["ln1_g"], w["ln1_b"], w["w_out"],
                                                             "ln1_bwd_out_proj")
    grads["w_out"] = _matmul_tn(y, d_z1, D_MODEL, D_MODEL, tt, "d_w_out")
    if on_mid_grads is not None:
        d_y = on_mid_grads(grads, d_y)
    d_proj, grads["hg_lb_logits"], grads["hg_norm_w"], db_hg = _hgrn2_bwd(
        proj, w["hg_lb_logits"], w["hg_norm_w"], hg_states, d_y)
    d_proj, d_qk, d_gates, grads["ml_norm_w"], db_vo = _mlstm_bwd(
        qk, proj, gates, w["ml_norm_w"], ct_s, n_s, m_s, d_y, d_proj)
    d_proj, grads["ml_conv_w"], grads["ml_conv_b"], db_qk = _ml_conv_bwd(
        proj, w["ml_conv_w"], w["ml_conv_b"], d_qk, d_proj)
    grads["b_in_main"] = jnp.concatenate([db_hg, db_qk, db_vo], axis=-1)
    grads["w_in_gate"], grads["b_in_gate"] = _matmul_tn(x, d_gates, D_MODEL, LANES, tt, "d_w_in_gates", colsum=True)
    if on_small_grads is not None:
        d_proj = on_small_grads(grads, loss, d_proj)
    grads["w_in_main"] = _matmul_tn(x, d_proj, D_MODEL, min(2048, D_IN_MAIN), tt, "d_w_in")
    if on_last_grads is not None:
        d_z1 = on_last_grads(grads, d_z1)
    grad_x = _matmul_nt([(d_proj, w["w_in_main"]), (d_gates, w["w_in_gate"])], d_z1, ALPHA, tm, D_MODEL, "d_x")
    return loss, grad_x, grads


HBM_SPEC = pl.BlockSpec(memory_space=pltpu.HBM)


def _coords():
    return lax.axis_index("x"), lax.axis_index("y"), lax.axis_index("c")


def _other_chips(x, y):
    return [(1 - x, y), (x, 1 - y), (1 - x, 1 - y)]


def _my_slot():
    x, y, c = _coords()
    return 4 * x + 2 * y + c


SEM_SPEC = pl.BlockSpec(memory_space=pltpu.SEMAPHORE)
ANY_SPEC = pl.BlockSpec(memory_space=pl.ANY)
SIDE_EFFECT = pltpu.SideEffectType.DATAFLOW_SIDE_EFFECTING


def _peer(x, y, c, d):
    flip = lambda v, bit: 1 - v if bit else v
    p = (flip(x, d & 4), flip(y, d & 2), flip(c, d & 1))
    return p, 4 * p[0] + 2 * p[1] + p[2]


def _direct_copies(gather, src_refs, land_refs, send_sems, recv_sems):
    x, y, c = _coords()
    me = 4 * x + 2 * y + c
    copies = []
    for a in range(len(src_refs)):
        for d in range(1, N_DEV):
            peer, peer_slot = _peer(x, y, c, d)
            copies.append(pltpu.make_async_remote_copy(
                src_ref=src_refs[a] if gather else src_refs[a].at[peer_slot],
                dst_ref=land_refs[a].at[me] if gather else land_refs[a].at[d - 1],
                send_sem=send_sems.at[7 * a + d - 1], recv_sem=recv_sems.at[7 * a + d - 1],
                device_id=peer, device_id_type=MESH))
    return copies


def _hbm(t):
    return pltpu.HBM(t.shape, t.dtype)


def _chip_copies(src_refs, land_refs, send_sems, recv_sems):
    x, y, c = _coords()
    me = 4 * x + 2 * y + c
    targets = [(x, y, 1 - c)] + [(cx, cy, c) for cx, cy in _other_chips(x, y)]
    return [pltpu.make_async_remote_copy(
        src_ref=src_refs[a], dst_ref=land_refs[a].at[me], send_sem=send_sems.at[4 * a + k],
        recv_sem=recv_sems.at[4 * a + k], device_id=target, device_id_type=MESH)
        for a in range(len(src_refs)) for k, target in enumerate(targets)]


def _forward_copies(land_refs, send_sems, recv_sems):
    x, y, c = _coords()
    return [pltpu.make_async_remote_copy(
        src_ref=land_refs[a].at[4 * cx + 2 * cy + c], dst_ref=land_refs[a].at[4 * cx + 2 * cy + c],
        send_sem=send_sems.at[3 * a + j], recv_sem=recv_sems.at[3 * a + j],
        device_id=(x, y, 1 - c), device_id_type=MESH)
        for a in range(len(land_refs)) for j, (cx, cy) in enumerate(_other_chips(x, y))]


def _split_copy_start(make_copies, n_sems, operands, through, name):
    n_ops = len(operands)

    def body(*refs):
        for cp in make_copies(refs[:n_ops], refs[n_ops + 1], refs[n_ops + 2]):
            cp.start()

    ins = [pltpu.with_memory_space_constraint(t, pltpu.HBM) for t in (*operands, through)]
    sems = pltpu.SemaphoreType.DMA((n_sems,))
    res = pl.pallas_call(
        body, name=name, out_shape=(sems, sems, *[_hbm(t) for t in ins]),
        in_specs=[HBM_SPEC] * (n_ops + 1), out_specs=(SEM_SPEC, SEM_SPEC, *[HBM_SPEC] * (n_ops + 1)),
        input_output_aliases={i: 2 + i for i in range(n_ops + 1)},
        compiler_params=pltpu.CompilerParams(has_side_effects=SIDE_EFFECT),
    )(*ins)
    return (res[0], res[1], list(res[2:2 + n_ops])), res[2 + n_ops]


def _split_copy_wait(make_copies, started, after, name):
    send_sems, recv_sems, operands = started
    n_ops = len(operands)
    after = list(after) if isinstance(after, (list, tuple)) else [after]

    def body(*refs):
        for cp in make_copies(refs[:n_ops], refs[n_ops], refs[n_ops + 1]):
            cp.wait_send()
            cp.wait_recv()

    res = pl.pallas_call(
        body, name=name, out_shape=tuple(_hbm(t) for t in operands),
        in_specs=[HBM_SPEC] * n_ops + [SEM_SPEC, SEM_SPEC] + [ANY_SPEC] * len(after),
        out_specs=tuple([HBM_SPEC] * n_ops), input_output_aliases={i: i for i in range(n_ops)},
        compiler_params=pltpu.CompilerParams(has_side_effects=SIDE_EFFECT),
    )(*operands, send_sems, recv_sems, *after)
    return list(res)


def _halves(make_copies, na):
    return lambda refs, send_sems, recv_sems: make_copies(refs[:na], refs[na:], send_sems, recv_sems)


def _direct_start(gather, arrays, through, name):
    na = len(arrays)
    lands = [lax.empty((N_DEV,) + t.shape if gather else (N_DEV - 1,) + t.shape[1:], t.dtype) for t in arrays]
    return _split_copy_start(_halves(functools.partial(_direct_copies, gather), na), 7 * na, [*arrays, *lands],
                             through, name)


def _direct_wait(gather, started, after, name):
    na = len(started[2]) // 2
    operands = _split_copy_wait(_halves(functools.partial(_direct_copies, gather), na), started, after, name)
    return operands[:na], operands[na:]


def _two_level_gather(shards, glue, name):
    na = len(shards)
    lands = [lax.empty((N_DEV,) + t.shape, t.dtype) for t in shards]
    nothing = jnp.zeros((SUBLANES, LANES), F32)
    started, _ = _split_copy_start(_halves(_chip_copies, na), 4 * na, [*shards, *lands], nothing, name + "_start")
    operands = _split_copy_wait(_halves(_chip_copies, na), started, glue, name + "_wait")
    started, mine = _split_copy_start(_forward_copies, 3 * na, operands[na:], operands[0], name + "_forward_start")
    lands = _split_copy_wait(_forward_copies, started, mine, name + "_forward_wait")
    return [lax.dynamic_update_index_in_dim(land, own, _my_slot(), 0)
            for own, land in zip([mine, *operands[1:na]], lands)]


def _row_tile(rows):
    for t in (256, 176, 128):
        if rows % t == 0 and rows > t:
            return t
    return rows


def _adamw_math(g, w, m, v):
    m_new = ADAM_B1 * m + (1.0 - ADAM_B1) * g
    v_new = ADAM_B2 * v + (1.0 - ADAM_B2) * jnp.square(g)
    m_hat = m_new / (1.0 - ADAM_B1 ** ADAM_STEP)
    v_hat = v_new / (1.0 - ADAM_B2 ** ADAM_STEP)
    delta = -ADAM_LR * (m_hat / (jnp.sqrt(v_hat) + ADAM_EPS) + ADAM_WD * w)
    return delta, m_new, v_new


def _adamw_sharded(chip, sums, got, w, m, v, name):
    r, c = w.shape
    tr = _row_tile(r)
    n_got = got.shape[0]

    def body(chip_ref, s_ref, g_ref, w_ref, m_ref, v_ref, go_ref, d_ref, nm_ref, nv_ref):
        g = s_ref[...].astype(F32)
        for i in range(n_got):
            g = g + g_ref[i].astype(F32)
        delta, m_new, v_new = _adamw_math(g, w_ref[...], m_ref[...], v_ref[...])
        go_ref[...] = g
        d_ref[...] = delta
        nm_ref[...] = m_new
        nv_ref[...] = v_new

    blk = pl.BlockSpec((tr, c), lambda i, chip_ref: (i, 0))
    out = jax.ShapeDtypeStruct((r, c), F32)
    return pl.pallas_call(
        body, name=name,
        grid_spec=pltpu.PrefetchScalarGridSpec(
            num_scalar_prefetch=1, grid=(r // tr,),
            in_specs=[pl.BlockSpec((None, tr, c), lambda i, chip_ref: (chip_ref[0], i, 0)),
                      pl.BlockSpec((n_got, tr, c), lambda i, chip_ref: (0, i, 0)), blk, blk, blk],
            out_specs=[blk, blk, blk, blk]),
        out_shape=[out, out, out, out],
        compiler_params=_params(("parallel",)),
    )(chip, sums, got, w, m, v)


def _adamw_replicated(parts, w, m, v):
    p, r, c = parts.shape

    def body(p_ref, w_ref, m_ref, v_ref, g_ref, d_ref, nm_ref, nv_ref):
        g = p_ref[0]
        for i in range(1, p):
            g = g + p_ref[i]
        delta, m_new, v_new = _adamw_math(g, w_ref[...], m_ref[...], v_ref[...])
        g_ref[...] = g
        d_ref[...] = delta
        nm_ref[...] = m_new
        nv_ref[...] = v_new

    blk = pl.BlockSpec((r, c), lambda i: (0, 0))
    out = jax.ShapeDtypeStruct((r, c), F32)
    return pl.pallas_call(
        body, name="adamw_replicated", grid=(1,),
        in_specs=[pl.BlockSpec((p, r, c), lambda i: (0, 0, 0)), blk, blk, blk],
        out_specs=[blk, blk, blk, blk], out_shape=[out, out, out, out],
        compiler_params=_params(("arbitrary",)),
    )(parts, w, m, v)


SHARDED_NAMES = ("w_in", "ml_conv_w", "w_out", "ca_wq", "ca_wkv", "ca_wo", "ffn_w_up", "ffn_conv_w", "ffn_w_down")
SMALL_NAMES = ("b_in", "hg_lb_logits", "hg_norm_w", "ml_conv_b", "ml_norm_w", "ln1_g", "ln1_b",
               "ln2_g", "ln2_b", "ffn_conv_b", "ln3_g", "ln3_b")
WEIGHT_NAMES = ("w_in", "b_in", "hg_lb_logits", "hg_norm_w", "ml_conv_w", "ml_conv_b", "ml_norm_w", "w_out",
                "ln1_g", "ln1_b", "ca_wq", "ca_wkv", "ca_wo", "ln2_g", "ln2_b", "ffn_w_up", "ffn_conv_w",
                "ffn_conv_b", "ffn_w_down", "ln3_g", "ln3_b")
PAD_TO = {"ffn_w_up": UP_SHARD_P, "ffn_conv_w": UP_SHARD_P}
SMALL_ROWS = 24
SMALL_W = D_MODEL


def _shard_2d(name, block):
    t = block[0]
    if name in PAD_TO:
        t = jnp.pad(t, ((0, 0), (0, PAD_TO[name] - t.shape[1])))
    return t


def _shard_like(name, t, like):
    return t[:, :like.shape[2]][None]


def _pad_cols(t, width):
    return jnp.pad(t, ((0, 0), (0, width - t.shape[1])))


FIRST_NAMES = ("w_in", "ml_conv_w")
FFN_NAMES = ("ffn_w_up", "ffn_w_down", "ffn_conv_w")
MID_NAMES = ("ca_wo", "ca_wq", "ca_wkv", "w_out")


def _first_weights(g, small):
    w = dict(small)
    last = g["w_in"][N_DEV - 1]
    split = D_IN_MAIN - (N_DEV - 1) * W_IN_SHARD
    w["w_in_main"] = jnp.concatenate([*[g["w_in"][j] for j in range(N_DEV - 1)], last[:, :split]], axis=1)
    w["w_in_gate"] = _pad_cols(last[:, split:], LANES)
    w["b_in_main"] = small["b_in"][:, :D_IN_MAIN]
    w["b_in_gate"] = _pad_cols(small["b_in"][:, D_IN_MAIN:], LANES)
    w["ml_conv_w"] = jnp.transpose(g["ml_conv_w"], (1, 0, 2)).reshape(ML_CONV, 2 * D_GROUP)
    return w


def _mid_weights(g):
    w = {n: g[n].reshape(D_MODEL, D_MODEL) for n in ("w_out", "ca_wq", "ca_wo")}
    w["ca_wkv"] = g["ca_wkv"]
    return w


FFN_UP_NAMES = ("ffn_w_up", "ffn_conv_w")
FFN_DOWN_NAMES = ("ffn_w_down",)


def _ffn_up_weights(g, small):
    w = {"ffn_w_up": g["ffn_w_up"]}
    w["ffn_conv_w"] = jnp.transpose(g["ffn_conv_w"], (1, 0, 2)).reshape(FFN_CONV, D_UP_P)
    w["ffn_conv_b"] = _pad_cols(small["ffn_conv_b"].reshape(N_DEV, UP_SHARD), UP_SHARD_P).reshape(1, D_UP_P)
    return w


def _ffn_down_weights(g):
    down = g["ffn_w_down"].reshape(N_DEV // 2, UP_SHARD, D_MODEL)
    return {"ffn_w_down": jnp.pad(down, ((0, 0), (0, UP_SHARD_P - UP_SHARD), (0, 0))).reshape(D_FF_P, D_MODEL)}


def _whole_weights(g, small):
    return {**_first_weights(g, small), **_mid_weights(g), **_ffn_up_weights(g, small), **_ffn_down_weights(g)}


def _owner_stack(n, grads):
    if n == "w_in":
        main, gate = grads["w_in_main"], grads["w_in_gate"][:, :D_IN - D_IN_MAIN]
        last = jnp.concatenate([main[:, (N_DEV - 1) * W_IN_SHARD:], gate], axis=1)
        return jnp.stack([*[main[:, j * W_IN_SHARD:(j + 1) * W_IN_SHARD] for j in range(N_DEV - 1)], last])
    if n in ("w_out", "ca_wq", "ca_wo"):
        return grads[n].reshape(N_DEV, D_MODEL // N_DEV, D_MODEL)
    if n == "ffn_w_down":
        down = grads[n].reshape(N_DEV // 2, UP_SHARD_P, D_MODEL)[:, :UP_SHARD]
        return down.reshape(N_DEV, D_FF // N_DEV, D_MODEL)
    if n == "ml_conv_w":
        return jnp.transpose(grads[n].reshape(ML_CONV, N_DEV, LANES), (1, 0, 2))
    if n == "ffn_conv_w":
        return jnp.transpose(grads[n].reshape(FFN_CONV, N_DEV, UP_SHARD_P), (1, 0, 2))
    return grads[n]


def _owner_stacks(grads):
    return {n: _owner_stack(n, grads) for n in SHARDED_NAMES}


def _small_grads(grads):
    out = {n: grads[n] for n in SMALL_NAMES if n in grads}
    out["b_in"] = jnp.concatenate([grads["b_in_main"], grads["b_in_gate"][:, :D_IN - D_IN_MAIN]], axis=1)
    out["ffn_conv_b"] = grads["ffn_conv_b"].reshape(N_DEV, UP_SHARD_P)[:, :UP_SHARD].reshape(1, D_UP)
    return out


def _pack_small(p, extra=None):
    flat = [p[n].reshape(-1) for n in SMALL_NAMES]
    if extra is not None:
        flat.append(extra.reshape(-1))
    flat = jnp.concatenate(flat)
    return jnp.pad(flat, (0, SMALL_ROWS * SMALL_W - flat.shape[0])).reshape(SMALL_ROWS, SMALL_W)


def _unpack_small(slab, like):
    out = {}
    flat = slab.reshape(-1)
    o = 0
    for n in SMALL_NAMES:
        out[n] = flat[o:o + like[n].size].reshape(like[n].shape)
        o += like[n].size
    return out, flat[o]


def kernel(x, mem, w_in, b_in, hg_lb_logits, hg_norm_w, ml_conv_w, ml_conv_b, ml_norm_w, w_out, ln1_g, ln1_b, ca_wq, ca_wkv, ca_wo, ln2_g, ln2_b, ffn_w_up, ffn_conv_w, ffn_conv_b, ffn_w_down, ln3_g, ln3_b, loss_target, m_w_in, m_b_in, m_hg_lb_logits, m_hg_norm_w, m_ml_conv_w, m_ml_conv_b, m_ml_norm_w, m_w_out, m_ln1_g, m_ln1_b, m_ca_wq, m_ca_wkv, m_ca_wo, m_ln2_g, m_ln2_b, m_ffn_w_up, m_ffn_conv_w, m_ffn_conv_b, m_ffn_w_down, m_ln3_g, m_ln3_b, v_w_in, v_b_in, v_hg_lb_logits, v_hg_norm_w, v_ml_conv_w, v_ml_conv_b, v_ml_norm_w, v_w_out, v_ln1_g, v_ln1_b, v_ca_wq, v_ca_wkv, v_ca_wo, v_ln2_g, v_ln2_b, v_ffn_w_up, v_ffn_conv_w, v_ffn_conv_b, v_ffn_w_down, v_ln3_g, v_ln3_b):
    params = dict(w_in=w_in, b_in=b_in, hg_lb_logits=hg_lb_logits, hg_norm_w=hg_norm_w, ml_conv_w=ml_conv_w,
                  ml_conv_b=ml_conv_b, ml_norm_w=ml_norm_w, w_out=w_out, ln1_g=ln1_g, ln1_b=ln1_b, ca_wq=ca_wq,
                  ca_wkv=ca_wkv, ca_wo=ca_wo, ln2_g=ln2_g, ln2_b=ln2_b, ffn_w_up=ffn_w_up, ffn_conv_w=ffn_conv_w,
                  ffn_conv_b=ffn_conv_b, ffn_w_down=ffn_w_down, ln3_g=ln3_g, ln3_b=ln3_b)
    mom1 = dict(w_in=m_w_in, b_in=m_b_in, hg_lb_logits=m_hg_lb_logits, hg_norm_w=m_hg_norm_w,
                ml_conv_w=m_ml_conv_w, ml_conv_b=m_ml_conv_b, ml_norm_w=m_ml_norm_w, w_out=m_w_out, ln1_g=m_ln1_g,
                ln1_b=m_ln1_b, ca_wq=m_ca_wq, ca_wkv=m_ca_wkv, ca_wo=m_ca_wo, ln2_g=m_ln2_g, ln2_b=m_ln2_b,
                ffn_w_up=m_ffn_w_up, ffn_conv_w=m_ffn_conv_w, ffn_conv_b=m_ffn_conv_b, ffn_w_down=m_ffn_w_down,
                ln3_g=m_ln3_g, ln3_b=m_ln3_b)
    mom2 = dict(w_in=v_w_in, b_in=v_b_in, hg_lb_logits=v_hg_lb_logits, hg_norm_w=v_hg_norm_w,
                ml_conv_w=v_ml_conv_w, ml_conv_b=v_ml_conv_b, ml_norm_w=v_ml_norm_w, w_out=v_w_out, ln1_g=v_ln1_g,
                ln1_b=v_ln1_b, ca_wq=v_ca_wq, ca_wkv=v_ca_wkv, ca_wo=v_ca_wo, ln2_g=v_ln2_g, ln2_b=v_ln2_b,
                ffn_w_up=v_ffn_w_up, ffn_conv_w=v_ffn_conv_w, ffn_conv_b=v_ffn_conv_b, ffn_w_down=v_ffn_w_down,
                ln3_g=v_ln3_g, ln3_b=v_ln3_b)

    x_idx, y_idx, c_idx = _coords()
    as_index = lambda v: jnp.reshape(v, (1,)).astype(jnp.int32)
    me = as_index(4 * x_idx + 2 * y_idx + c_idx)
    small_params = {n: params[n] for n in SMALL_NAMES}

    shards = {n: _shard_2d(n, params[n]) for n in SHARDED_NAMES}
    m_shards = {n: _shard_2d(n, mom1[n]) for n in SHARDED_NAMES}
    v_shards = {n: _shard_2d(n, mom2[n]) for n in SHARDED_NAMES}
    small_slabs = [_pack_small(params), _pack_small(mom1), _pack_small(mom2)]
    outgoing = {n: shards[n] if "conv" in n else shards[n].astype(BF16) for n in SHARDED_NAMES}
    to_send = lambda names: [outgoing[n] for n in names]
    glue = [*m_shards.values(), *v_shards.values(), *small_slabs, *shards.values(),
            *[outgoing[n] for n in SHARDED_NAMES if n not in FIRST_NAMES]]
    first = dict(zip(FIRST_NAMES, _two_level_gather(to_send(FIRST_NAMES), glue, "weights_gather_first")))
    mid_started, through = _direct_start(True, to_send(MID_NAMES), first["w_in"], "weights_gather_start_mid")
    ffn_started, through = _direct_start(True, to_send(FFN_UP_NAMES), through, "weights_gather_start_ffn_up")
    down_started, first["w_in"] = _direct_start(True, to_send(FFN_DOWN_NAMES), through,
                                                "weights_gather_start_ffn_down")

    def gathered_weights(names, started, after, tag):
        mine, lands = _direct_wait(True, started, after, "weights_gather_wait_" + tag)
        return {n: lax.dynamic_update_index_in_dim(land, own, me[0], 0) for n, own, land in zip(names, mine, lands)}

    started, own_stacks = {}, {}

    def start_group(names, tag):
        def hook(grads, through):
            own_stacks[tag] = [_owner_stack(n, grads).astype(BF16) for n in names]
            started[tag], through = _direct_start(False, own_stacks[tag], through, "grads_start_" + tag)
            return through
        return hook

    def start_small(grads, loss, through):
        started["small"], through = _direct_start(True, [_pack_small(_small_grads(grads), loss)], through,
                                                  "small_gather_start")
        return through

    loss, grad_x, grads = _local_step(
        x[0], mem[0], loss_target[0], _first_weights(first, small_params),
        lambda y: _mid_weights(gathered_weights(MID_NAMES, mid_started, y, "mid")),
        lambda x2: _ffn_up_weights(gathered_weights(FFN_UP_NAMES, ffn_started, x2, "ffn_up"), small_params),
        lambda hid: _ffn_down_weights(gathered_weights(FFN_DOWN_NAMES, down_started, hid, "ffn_down")),
        start_group(FFN_NAMES, "ffn"), start_group(MID_NAMES, "mid"), start_small, start_group(FIRST_NAMES, "last"))

    sharded_out = {}

    def update_group(names, tag, after):
        _, lands = _direct_wait(False, started[tag], after, "grads_wait_" + tag)
        for n, st, land in zip(names, own_stacks[tag], lands):
            res = _adamw_sharded(me, st, land, shards[n], m_shards[n], v_shards[n], "adamw_" + n)
            sharded_out[n] = [_shard_like(n, t, params[n]) for t in res]

    update_group(FFN_NAMES, "ffn", grad_x)
    update_group(MID_NAMES, "mid", grad_x)
    own_small, small_lands = _direct_wait(True, started["small"], grad_x, "small_gather_wait")
    small_parts = lax.dynamic_update_index_in_dim(small_lands[0], own_small[0], me[0], 0)
    small_res = _adamw_replicated(small_parts, *small_slabs)
    small_out = [_unpack_small(slab, params) for slab in small_res]
    done = [t for n in FFN_NAMES + MID_NAMES for t in sharded_out[n]]
    done += [t for small, _ in small_out for t in small.values()]
    update_group(FIRST_NAMES, "last", done)

    outs = []
    for k, (small, _) in enumerate(small_out):
        outs.extend(sharded_out[n][k] if n in sharded_out else small[n] for n in WEIGHT_NAMES)
    return (small_out[0][1], grad_x[None], *outs)
```

```python
import functools
import math

import jax
import jax.numpy as jnp
from jax import lax
from jax.experimental import pallas as pl
from jax.experimental.pallas import tpu as pltpu

F32 = jnp.float32
BF16 = jnp.bfloat16
HIGHEST = lax.Precision.HIGHEST
MESH = pl.DeviceIdType.MESH

N_DEV = 8
D_MODEL = 1024
N_MEM = 256
N_HEADS = 4
D_HEAD = 128
D_GROUP = N_HEADS * D_HEAD
CHUNK = 64
ML_CONV = 4
FFN_CONV = 3
D_FF = 2816
D_UP = 2 * D_FF
CA_HEADS = 4
CA_DH = D_MODEL // CA_HEADS
LANES = 128
SUBLANES = 8
D_IN = 8 * D_GROUP + 2 * N_HEADS
D_IN_MAIN = 8 * D_GROUP
W_IN_SHARD = D_IN // N_DEV
UP_SHARD = D_UP // N_DEV
UP_SHARD_P = 768
D_UP_P = N_DEV * UP_SHARD_P
D_FF_P = D_UP_P // 2
ALPHA = 2.0 ** 0.25
LN_EPS = 1e-5
NEG_BIG = -1e30
ADAM_LR = 0.001
ADAM_B1 = 0.9
ADAM_B2 = 0.999
ADAM_EPS = 1e-08
ADAM_WD = 0.01
ADAM_STEP = 10
VMEM_LIMIT = 56 * 1024 * 1024

SEG_HQ, SEG_HF, SEG_HI, SEG_HG, SEG_MQ, SEG_MK, SEG_MV, SEG_MO = (4 * i for i in range(8))


def _params(sem):
    return pltpu.CompilerParams(dimension_semantics=sem, vmem_limit_bytes=VMEM_LIMIT)


def _dg(a, b, ca, cb, precision=None):
    return lax.dot_general(a, b, (((ca,), (cb,)), ((), ())), precision=precision,
                           preferred_element_type=F32)


def _nn_raw(a, b):
    return _dg(a.astype(BF16), b.astype(BF16), 1, 0)


def _nt_raw(a, b):
    return _dg(a.astype(BF16), b.astype(BF16), 1, 1)


def _tn_raw(a, b):
    return _dg(a.astype(BF16), b.astype(BF16), 0, 0)


@jax.custom_vjp
def _nn(a, b):
    return _nn_raw(a, b)


_nn.defvjp(lambda a, b: (_nn_raw(a, b), (a, b)),
           lambda res, g: (_nt_raw(g, res[1]), _tn_raw(res[0], g)))


@jax.custom_vjp
def _nt(a, b):
    return _nt_raw(a, b)


_nt.defvjp(lambda a, b: (_nt_raw(a, b), (a, b)),
           lambda res, g: (_nn_raw(g, res[1]), _tn_raw(g, res[0])))


@jax.custom_vjp
def _tn(a, b):
    return _tn_raw(a, b)


_tn.defvjp(lambda a, b: (_tn_raw(a, b), (a, b)),
           lambda res, g: (_nt_raw(res[1], g), _nn_raw(res[0], g)))


def _layer_norm(z, g, b):
    mu = jnp.mean(z, axis=-1, keepdims=True)
    var = jnp.mean(jnp.square(z - mu), axis=-1, keepdims=True)
    return (z - mu) * lax.rsqrt(var + LN_EPS) * g + b


def _matmul_nn(a, w, bias, tm, tn, name, out_dtype=F32):
    m, k = a.shape
    if w.ndim == 3:
        n = w.shape[0] * w.shape[2]
        assert tn == w.shape[2]
        w_spec = pl.BlockSpec((None, k, tn), lambda i, j: (j, 0, 0))
    else:
        n = w.shape[1]
        w_spec = pl.BlockSpec((k, tn), lambda i, j: (0, j))

    def body(*refs):
        a_ref, w_ref = refs[0], refs[1]
        o_ref = refs[-1]
        acc = _nn_raw(a_ref[...], w_ref[...])
        if bias is not None:
            acc = acc + refs[2][...]
        o_ref[...] = acc.astype(o_ref.dtype)

    in_specs = [pl.BlockSpec((tm, k), lambda i, j: (i, 0)), w_spec]
    args = [a, w]
    if bias is not None:
        in_specs.append(pl.BlockSpec((1, tn), lambda i, j: (0, j)))
        args.append(bias)
    return pl.pallas_call(
        body, name=name, grid=(m // tm, n // tn), in_specs=in_specs,
        out_specs=pl.BlockSpec((tm, tn), lambda i, j: (i, j)),
        out_shape=jax.ShapeDtypeStruct((m, n), out_dtype),
        compiler_params=_params(("parallel", "parallel")),
    )(*args)


def _matmul_nt(pairs, add, scale, tm, tk, name, out_dtype=F32):
    m = pairs[0][0].shape[0]
    k = pairs[0][1].shape[-2]
    groups = []
    in_specs, args = [], []
    for pair in pairs:
        d, w = pair[0], pair[1]
        in_specs.append(pl.BlockSpec((tm, d.shape[1]), lambda i, j: (i, 0)))
        if w.ndim == 3:
            g = d.shape[1] // w.shape[2]
            blk = pair[2] // g
            in_specs.append(pl.BlockSpec((g, tk, w.shape[2]), lambda i, j, blk=blk: (blk, j, 0)))
            groups.append((g, w.shape[2]))
        else:
            in_specs.append(pl.BlockSpec((tk, w.shape[1]), lambda i, j: (j, 0)))
            groups.append(None)
        args += [d, w]
    if add is not None:
        in_specs.append(pl.BlockSpec((tm, tk), lambda i, j: (i, j)))
        args.append(add)

    def body(*refs):
        o_ref = refs[-1]
        acc = None
        for p, grp in enumerate(groups):
            d_ref, w_ref = refs[2 * p], refs[2 * p + 1]
            if grp is None:
                terms = [_nt_raw(d_ref[...], w_ref[...])]
            else:
                terms = [_nt_raw(d_ref[:, g * grp[1]:(g + 1) * grp[1]], w_ref[g]) for g in range(grp[0])]
            for t in terms:
                acc = t if acc is None else acc + t
        if add is not None:
            acc = acc + scale * refs[2 * len(groups)][...]
        o_ref[...] = acc.astype(o_ref.dtype)

    return pl.pallas_call(
        body, name=name, grid=(m // tm, k // tk), in_specs=in_specs,
        out_specs=pl.BlockSpec((tm, tk), lambda i, j: (i, j)),
        out_shape=jax.ShapeDtypeStruct((m, k), out_dtype),
        compiler_params=_params(("parallel", "parallel")),
    )(*args)


def _matmul_tn(a, b, tm, tn, tt, name, shards=None, shard0=0, group=1, into=None, colsum=False):
    t, m = a.shape
    n = b.shape[1]
    assert not colsum or tm == m
    n_in = 2 + (into is not None)
    out_dtype = BF16
    per_step = 1 if shards is None else group
    width = per_step * tn

    def body(*refs):
        a_ref, b_ref = refs[0], refs[1]
        o_ref, acc_ref = refs[n_in], refs[-1]
        first = pl.program_id(2) == 0

        @pl.when(first)
        def _():
            acc_ref[...] = jnp.zeros_like(acc_ref)

        if shards is None:
            acc_ref[...] += _tn_raw(a_ref[...], b_ref[...])
        else:
            lhs = a_ref[...].astype(BF16)
            for g in range(per_step):
                acc_ref[g] += _tn_raw(lhs, b_ref[:, g * tn:(g + 1) * tn])

        @pl.when(pl.program_id(2) == t // tt - 1)
        def _():
            o_ref[...] = acc_ref[...].astype(o_ref.dtype)

        if colsum:
            s_ref = refs[n_in + 1]

            @pl.when(first)
            def _():
                s_ref[...] = jnp.zeros_like(s_ref)

            s_ref[...] += jnp.sum(b_ref[...], axis=0, keepdims=True)

    in_specs = [pl.BlockSpec((tt, tm), lambda i, j, kk: (kk, i)),
                pl.BlockSpec((tt, width), lambda i, j, kk: (kk, j))]
    args = [a, b]
    aliases = {}
    if into is not None:
        in_specs.append(pl.BlockSpec(memory_space=pl.ANY))
        args.append(into)
        aliases = {2: 0}
    if shards is None:
        out_specs = [pl.BlockSpec((tm, tn), lambda i, j, kk: (i, j))]
        out_shape = [jax.ShapeDtypeStruct((m, n), out_dtype)]
        acc = pltpu.VMEM((tm, tn), F32)
    else:
        out_specs = [pl.BlockSpec((per_step, tm, tn), lambda i, j, kk: (shard0 // per_step + j, i, 0))]
        out_shape = [jax.ShapeDtypeStruct((shards, m, tn), out_dtype)]
        acc = pltpu.VMEM((per_step, tm, tn), F32)
    if colsum:
        out_specs.append(pl.BlockSpec((1, tn), lambda i, j, kk: (0, j)))
        out_shape.append(jax.ShapeDtypeStruct((1, n), F32))
    res = pl.pallas_call(
        body, name=name, grid=(m // tm, n // width, t // tt), in_specs=in_specs, out_specs=out_specs,
        out_shape=out_shape, input_output_aliases=aliases, scratch_shapes=[acc],
        compiler_params=_params(("parallel", "parallel", "arbitrary")),
    )(*args)
    return res if colsum else res[0]


ROW_TILE = 64


def _stack(ref, start, rows):
    return ref[pl.ds(start, rows), :].astype(F32).reshape(rows // SUBLANES, SUBLANES, LANES)


def _vreg_rows(ref, n):
    return [jnp.broadcast_to(ref[j:j + 1, :], (SUBLANES, LANES))[None] for j in range(n)]


def _column_total(acc):
    return jnp.sum(acc, axis=0, keepdims=True)


def _conv_fwd_tile(pad_ref, taps_w, bias, r0, rows):
    taps = len(taps_w)
    acc = bias
    for j in range(taps):
        acc = acc + _stack(pad_ref, SUBLANES - (taps - 1 - j) + r0, rows) * taps_w[j]
    return acc


def _conv_grads_tile(pad_ref, dpad_ref, dx_ref, taps_w, dws, r0, rows):
    taps = len(taps_w)
    x_rows = _stack(pad_ref, SUBLANES + r0, rows)
    dx = None
    for j in range(taps):
        d_shifted = _stack(dpad_ref, r0 + (taps - 1 - j), rows)
        term = d_shifted * taps_w[j]
        dx = term if dx is None else dx + term
        dws[j] = dws[j] + jnp.sum(d_shifted * x_rows, axis=0)
    dx_ref[r0:r0 + rows, :] = dx.reshape(rows, LANES).astype(dx_ref.dtype)
    return jnp.sum(dx, axis=0)


def _ml_conv_fwd(proj, conv_w, conv_b):
    s = proj.shape[0]
    nblk = 2 * D_GROUP // LANES

    def body(x_ref, w_ref, b_ref, o_ref, pad_ref):
        pad_ref[0:SUBLANES, :] = jnp.zeros((SUBLANES, LANES), F32)
        pad_ref[SUBLANES:, :] = x_ref[...].astype(F32)
        taps_w, bias = _vreg_rows(w_ref, ML_CONV), _vreg_rows(b_ref, 1)[0]
        for r0 in range(0, s, ROW_TILE):
            rows = min(ROW_TILE, s - r0)
            o_ref[r0:r0 + rows, :] = jax.nn.silu(_conv_fwd_tile(pad_ref, taps_w, bias, r0, rows)).reshape(rows, LANES)

    return pl.pallas_call(
        body, name="ml_conv_fwd", grid=(nblk,),
        in_specs=[pl.BlockSpec((s, LANES), lambda j: (0, SEG_MQ + j)),
                  pl.BlockSpec((ML_CONV, LANES), lambda j: (0, j)),
                  pl.BlockSpec((1, LANES), lambda j: (0, j))],
        out_specs=pl.BlockSpec((s, LANES), lambda j: (0, j)),
        out_shape=jax.ShapeDtypeStruct((s, 2 * D_GROUP), F32),
        scratch_shapes=[pltpu.VMEM((s + SUBLANES, LANES), F32)],
        compiler_params=_params(("parallel",)),
    )(proj, conv_w, conv_b)


def _ml_conv_bwd(proj, conv_w, conv_b, d_qk, d_proj):
    s = proj.shape[0]
    nblk = 2 * D_GROUP // LANES

    def body(x_ref, w_ref, b_ref, dy_ref, _, dx_ref, dw_ref, db_ref, dxs_ref, pad_ref, dpad_ref):
        pad_ref[0:SUBLANES, :] = jnp.zeros((SUBLANES, LANES), F32)
        pad_ref[SUBLANES:, :] = x_ref[...].astype(F32)
        dpad_ref[s:, :] = jnp.zeros((SUBLANES, LANES), F32)
        taps_w, bias = _vreg_rows(w_ref, ML_CONV), _vreg_rows(b_ref, 1)[0]
        db = jnp.zeros((SUBLANES, LANES), F32)
        for r0 in range(0, s, ROW_TILE):
            rows = min(ROW_TILE, s - r0)
            pre = _conv_fwd_tile(pad_ref, taps_w, bias, r0, rows)
            _, vjp = jax.vjp(jax.nn.silu, pre)
            d_pre, = vjp(_stack(dy_ref, r0, rows))
            dpad_ref[r0:r0 + rows, :] = d_pre.reshape(rows, LANES)
            db = db + jnp.sum(d_pre, axis=0)
        db_ref[...] = _column_total(db)
        dws = [jnp.zeros((SUBLANES, LANES), F32) for _ in range(ML_CONV)]
        dx_sum = jnp.zeros((SUBLANES, LANES), F32)
        for r0 in range(0, s, ROW_TILE):
            dx_sum = dx_sum + _conv_grads_tile(pad_ref, dpad_ref, dx_ref, taps_w, dws, r0, min(ROW_TILE, s - r0))
        dxs_ref[...] = _column_total(dx_sum)
        for j in range(ML_CONV):
            dw_ref[j:j + 1, :] = _column_total(dws[j])

    return pl.pallas_call(
        body, name="ml_conv_bwd", grid=(nblk,),
        in_specs=[pl.BlockSpec((s, LANES), lambda j: (0, SEG_MQ + j)),
                  pl.BlockSpec((ML_CONV, LANES), lambda j: (0, j)),
                  pl.BlockSpec((1, LANES), lambda j: (0, j)),
                  pl.BlockSpec((s, LANES), lambda j: (0, j)),
                  pl.BlockSpec(memory_space=pl.ANY)],
        out_specs=[pl.BlockSpec((s, LANES), lambda j: (0, SEG_MQ + j)),
                   pl.BlockSpec((ML_CONV, LANES), lambda j: (0, j)),
                   pl.BlockSpec((1, LANES), lambda j: (0, j)),
                   pl.BlockSpec((1, LANES), lambda j: (0, j))],
        out_shape=[jax.ShapeDtypeStruct(d_proj.shape, d_proj.dtype),
                   jax.ShapeDtypeStruct((ML_CONV, 2 * D_GROUP), F32),
                   jax.ShapeDtypeStruct((1, 2 * D_GROUP), F32),
                   jax.ShapeDtypeStruct((1, 2 * D_GROUP), F32)],
        input_output_aliases={4: 0},
        scratch_shapes=[pltpu.VMEM((s + SUBLANES, LANES), F32), pltpu.VMEM((s + SUBLANES, LANES), F32)],
        compiler_params=_params(("parallel",)),
    )(proj, conv_w, conv_b, d_qk, d_proj)


def _gelu_mul(a, b):
    return jax.nn.gelu(a) * b


GELU_C = math.sqrt(2.0 / math.pi)
GELU_K = 0.044715


def _gelu_mul_grads(a, b, d):
    a2 = a * a
    t = jnp.tanh(GELU_C * (a + GELU_K * (a * a2)))
    cdf = 0.5 * (1.0 + t)
    slope = cdf + (0.5 * GELU_C) * a * (1.0 - t * t) * (1.0 + (3.0 * GELU_K) * a2)
    return d * b * slope, d * (a * cdf)


FFN_BLOCKS = D_FF_P // LANES


def _ffn_conv_fwd(u, conv_w, conv_b):
    s = u.shape[0]

    def body(g_ref, v_ref, wg_ref, wv_ref, bg_ref, bv_ref, o_ref, gpad_ref, vpad_ref):
        for pad_ref, x_ref in ((gpad_ref, g_ref), (vpad_ref, v_ref)):
            pad_ref[0:SUBLANES, :] = jnp.zeros((SUBLANES, LANES), F32)
            pad_ref[SUBLANES:, :] = x_ref[...].astype(F32)
        taps_g, bias_g = _vreg_rows(wg_ref, FFN_CONV), _vreg_rows(bg_ref, 1)[0]
        taps_v, bias_v = _vreg_rows(wv_ref, FFN_CONV), _vreg_rows(bv_ref, 1)[0]
        for r0 in range(0, s, ROW_TILE):
            rows = min(ROW_TILE, s - r0)
            ug = _conv_fwd_tile(gpad_ref, taps_g, bias_g, r0, rows)
            uv = _conv_fwd_tile(vpad_ref, taps_v, bias_v, r0, rows)
            o_ref[r0:r0 + rows, :] = _gelu_mul(ug, uv).reshape(rows, LANES).astype(o_ref.dtype)

    col = lambda off: (lambda j: (0, off + j))
    return pl.pallas_call(
        body, name="ffn_conv_fwd", grid=(FFN_BLOCKS,),
        in_specs=[pl.BlockSpec((s, LANES), col(0)), pl.BlockSpec((s, LANES), col(FFN_BLOCKS)),
                  pl.BlockSpec((FFN_CONV, LANES), col(0)), pl.BlockSpec((FFN_CONV, LANES), col(FFN_BLOCKS)),
                  pl.BlockSpec((1, LANES), col(0)), pl.BlockSpec((1, LANES), col(FFN_BLOCKS))],
        out_specs=pl.BlockSpec((s, LANES), col(0)),
        out_shape=jax.ShapeDtypeStruct((s, D_FF_P), BF16),
        scratch_shapes=[pltpu.VMEM((s + SUBLANES, LANES), F32), pltpu.VMEM((s + SUBLANES, LANES), F32)],
        compiler_params=_params(("parallel",)),
    )(u, u, conv_w, conv_w, conv_b, conv_b)


def _ffn_conv_bwd(u, conv_w, conv_b, d_h):
    s = u.shape[0]

    def body(g_ref, v_ref, wg_ref, wv_ref, bg_ref, bv_ref, dh_ref,
             dug_ref, duv_ref, dwg_ref, dwv_ref, dbg_ref, dbv_ref,
             gpad_ref, vpad_ref, dgpad_ref, dvpad_ref):
        for pad_ref, x_ref in ((gpad_ref, g_ref), (vpad_ref, v_ref)):
            pad_ref[0:SUBLANES, :] = jnp.zeros((SUBLANES, LANES), F32)
            pad_ref[SUBLANES:, :] = x_ref[...].astype(F32)
        dgpad_ref[s:, :] = jnp.zeros((SUBLANES, LANES), F32)
        dvpad_ref[s:, :] = jnp.zeros((SUBLANES, LANES), F32)
        taps_g, bias_g = _vreg_rows(wg_ref, FFN_CONV), _vreg_rows(bg_ref, 1)[0]
        taps_v, bias_v = _vreg_rows(wv_ref, FFN_CONV), _vreg_rows(bv_ref, 1)[0]
        dbg = jnp.zeros((SUBLANES, LANES), F32)
        dbv = jnp.zeros((SUBLANES, LANES), F32)
        for r0 in range(0, s, ROW_TILE):
            rows = min(ROW_TILE, s - r0)
            ug = _conv_fwd_tile(gpad_ref, taps_g, bias_g, r0, rows)
            uv = _conv_fwd_tile(vpad_ref, taps_v, bias_v, r0, rows)
            d_ug, d_uv = _gelu_mul_grads(ug, uv, _stack(dh_ref, r0, rows))
            dgpad_ref[r0:r0 + rows, :] = d_ug.reshape(rows, LANES)
            dvpad_ref[r0:r0 + rows, :] = d_uv.reshape(rows, LANES)
            dbg = dbg + jnp.sum(d_ug, axis=0)
            dbv = dbv + jnp.sum(d_uv, axis=0)
        dbg_ref[...] = _column_total(dbg)
        dbv_ref[...] = _column_total(dbv)
        for pad_ref, dpad_ref, taps_w, dx_ref, dw_ref in ((gpad_ref, dgpad_ref, taps_g, dug_ref, dwg_ref),
                                                          (vpad_ref, dvpad_ref, taps_v, duv_ref, dwv_ref)):
            dws = [jnp.zeros((SUBLANES, LANES), F32) for _ in range(FFN_CONV)]
            for r0 in range(0, s, ROW_TILE):
                _conv_grads_tile(pad_ref, dpad_ref, dx_ref, taps_w, dws, r0, min(ROW_TILE, s - r0))
            for j in range(FFN_CONV):
                dw_ref[j:j + 1, :] = _column_total(dws[j])

    col = lambda off: (lambda j: (0, off + j))
    seq = pl.BlockSpec((s, LANES), col(0))
    return pl.pallas_call(
        body, name="ffn_conv_bwd", grid=(FFN_BLOCKS,),
        in_specs=[pl.BlockSpec((s, LANES), col(0)), pl.BlockSpec((s, LANES), col(FFN_BLOCKS)),
                  pl.BlockSpec((FFN_CONV, LANES), col(0)), pl.BlockSpec((FFN_CONV, LANES), col(FFN_BLOCKS)),
                  pl.BlockSpec((1, LANES), col(0)), pl.BlockSpec((1, LANES), col(FFN_BLOCKS)), seq],
        out_specs=[seq, seq, pl.BlockSpec((FFN_CONV, LANES), col(0)), pl.BlockSpec((FFN_CONV, LANES), col(0)),
                   pl.BlockSpec((1, LANES), col(0)), pl.BlockSpec((1, LANES), col(0))],
        out_shape=[jax.ShapeDtypeStruct((s, D_FF_P), BF16), jax.ShapeDtypeStruct((s, D_FF_P), BF16),
                   jax.ShapeDtypeStruct((FFN_CONV, D_FF_P), F32), jax.ShapeDtypeStruct((FFN_CONV, D_FF_P), F32),
                   jax.ShapeDtypeStruct((1, D_FF_P), F32), jax.ShapeDtypeStruct((1, D_FF_P), F32)],
        scratch_shapes=[pltpu.VMEM((s + SUBLANES, LANES), F32) for _ in range(4)],
        compiler_params=_params(("parallel",)),
    )(u, u, conv_w, conv_w, conv_b, conv_b, d_h)


def _chunk_masks(c):
    row = lax.broadcasted_iota(jnp.int32, (c, c), 0)
    col = lax.broadcasted_iota(jnp.int32, (c, c), 1)
    return row, col


@jax.custom_vjp
def _split_heads(x):
    return tuple(x[:, h * D_HEAD:(h + 1) * D_HEAD] for h in range(N_HEADS))


_split_heads.defvjp(lambda x: (_split_heads(x), None), lambda _, gs: (jnp.concatenate(gs, axis=1),))


@jax.custom_vjp
def _merge_heads(xs):
    return jnp.concatenate(xs, axis=1)


_merge_heads.defvjp(lambda xs: (_merge_heads(xs), None), lambda _, g: (_split_heads(g),))


@jax.custom_vjp
def _split_chunks(x):
    return tuple(x[i * CHUNK:(i + 1) * CHUNK] for i in range(x.shape[0] // CHUNK))


_split_chunks.defvjp(lambda x: (_split_chunks(x), None), lambda _, gs: (jnp.concatenate(gs, axis=0),))


@jax.custom_vjp
def _merge_chunks(xs):
    return jnp.concatenate(xs, axis=0)


_merge_chunks.defvjp(lambda xs: (_merge_chunks(xs), None), lambda _, g: (_split_chunks(g),))


def _blocks(x):
    return [_split_heads(rows) for rows in _split_chunks(x)]


def _per_chunk_rows(per_chunk, rid):
    out = per_chunk[0]
    for i in range(1, len(per_chunk)):
        out = jnp.where(rid >= i * CHUNK, per_chunk[i], out)
    return out


HEADS = range(N_HEADS)
CHUNKS_PER_STEP = 4
ML_CHUNKS_PER_STEP = 1


def _hg_chunk(hq, hf, hi, hgate, l0, l1, nw, sts):
    n = hq.shape[0] // CHUNK
    causal = _chunk_masks(CHUNK)
    causal = causal[1] <= causal[0]
    mx = lax.stop_gradient(jnp.maximum(l0, l1))
    e0 = jnp.exp(l0 - mx)
    e1 = jnp.exp(l1 - mx)
    lb = e0 / (e0 + e1)
    sig = jax.nn.sigmoid(hf)
    lf = jnp.log(lb + (1.0 - lb) * sig)
    k = (1.0 - lb) * jax.nn.sigmoid(-hf)
    q = jax.nn.silu(hq)
    tri = causal.astype(F32)
    b = _merge_chunks(tuple(_dg(tri, rows, 1, 0, HIGHEST) for rows in _split_chunks(lf)))
    rid = lax.broadcasted_iota(jnp.int32, b.shape, 0)
    pick = lambda r: jnp.sum(jnp.where(rid == r, b, 0.0), axis=0, keepdims=True)
    b_last_c = [pick(i * CHUNK + CHUNK - 1) for i in range(n)]
    b_ref = _per_chunk_rows([pick(i * CHUNK + CHUNK // 2 - 1) for i in range(n)], rid)
    b_last = _per_chunk_rows(b_last_c, rid)
    qa = _blocks(q * jnp.exp(b - b_ref))
    ka = _blocks(k * jnp.exp(b_ref - b))
    qe = _blocks(q * jnp.exp(b))
    kd = _blocks(k * jnp.exp(b_last - b))
    decay = [_split_heads(jnp.exp(b_last_c[i])) for i in range(n)]
    v = _blocks(hi)
    chunks = range(n)
    attn = [[jnp.where(causal, _nt(qa[i][h], ka[i][h]), 0.0) for h in HEADS] for i in chunks]
    intra = [[_nn(attn[i][h], v[i][h]) for h in HEADS] for i in chunks]
    kv = [[_tn(v[i][h], kd[i][h]) for h in HEADS] for i in chunks]
    normed = []
    for i in chunks:
        inter = [_nt(qe[i][h], sts[h]) for h in HEADS]
        sts = tuple(decay[i][h] * sts[h] + kv[i][h] for h in HEADS)
        o = [intra[i][h] + inter[h] for h in HEADS]
        normed.append(_merge_heads(tuple(o[h] * lax.rsqrt(jnp.mean(o[h] * o[h], axis=-1, keepdims=True) + LN_EPS)
                                         for h in HEADS)))
    return _merge_chunks(tuple(normed)) * nw * jax.nn.silu(hgate), sts


def _seg(ref, seg):
    return ref[:, seg * D_GROUP:(seg + 1) * D_GROUP]


def _hgrn2_fwd(proj, logits, norm_w):
    s = proj.shape[0]
    rows = CHUNKS_PER_STEP * CHUNK
    nc = s // rows

    def body(p_ref, lg_ref, nw_ref, y_ref, st_out_ref, st_scr):
        @pl.when(pl.program_id(0) == 0)
        def _():
            st_scr[...] = jnp.zeros_like(st_scr)

        sts = tuple(st_scr[h] for h in HEADS)
        y, sts_new = _hg_chunk(_seg(p_ref, 0), _seg(p_ref, 1), _seg(p_ref, 2), _seg(p_ref, 3),
                               lg_ref[0:1, :], lg_ref[1:2, :], nw_ref[...], sts)
        y_ref[...] = y.astype(y_ref.dtype)
        for h in HEADS:
            st_out_ref[h] = sts[h]
            st_scr[h] = sts_new[h]

    return pl.pallas_call(
        body, name="hgrn2_fwd", grid=(nc,),
        in_specs=[pl.BlockSpec((rows, 4 * D_GROUP), lambda c: (c, 0)),
                  pl.BlockSpec((2, D_GROUP), lambda c: (0, 0)),
                  pl.BlockSpec((1, D_GROUP), lambda c: (0, 0))],
        out_specs=[pl.BlockSpec((rows, D_GROUP), lambda c: (c, 0)),
                   pl.BlockSpec((None, N_HEADS, D_HEAD, D_HEAD), lambda c: (c, 0, 0, 0))],
        out_shape=[jax.ShapeDtypeStruct((s, 2 * D_GROUP), BF16),
                   jax.ShapeDtypeStruct((nc, N_HEADS, D_HEAD, D_HEAD), F32)],
        scratch_shapes=[pltpu.VMEM((N_HEADS, D_HEAD, D_HEAD), F32)],
        compiler_params=_params(("arbitrary",)),
    )(proj, logits, norm_w)


def _hgrn2_bwd(proj, logits, norm_w, states, d_y):
    s = proj.shape[0]
    rows = CHUNKS_PER_STEP * CHUNK
    nc = s // rows

    def body(p_ref, lg_ref, nw_ref, st_ref, dy_ref, dp_ref, dl_ref, dnw_ref, dsum_ref, dst_scr):
        @pl.when(pl.program_id(0) == 0)
        def _():
            dst_scr[...] = jnp.zeros_like(dst_scr)
            dl_ref[...] = jnp.zeros_like(dl_ref)
            dnw_ref[...] = jnp.zeros_like(dnw_ref)
            dsum_ref[...] = jnp.zeros_like(dsum_ref)

        _, vjp = jax.vjp(_hg_chunk, _seg(p_ref, 0), _seg(p_ref, 1), _seg(p_ref, 2), _seg(p_ref, 3),
                         lg_ref[0:1, :], lg_ref[1:2, :], nw_ref[...], tuple(st_ref[h] for h in HEADS))
        d_hq, d_hf, d_hi, d_hg, d_l0, d_l1, d_nw, d_sts = vjp((dy_ref[...], tuple(dst_scr[h] for h in HEADS)))
        for seg, val in enumerate((d_hq, d_hf, d_hi, d_hg)):
            dp_ref[:, seg * D_GROUP:(seg + 1) * D_GROUP] = val.astype(dp_ref.dtype)
            dsum_ref[:, seg * D_GROUP:(seg + 1) * D_GROUP] += jnp.sum(val, axis=0, keepdims=True)
        dl_ref[0:1, :] += d_l0
        dl_ref[1:2, :] += d_l1
        dnw_ref[...] += d_nw
        for h in HEADS:
            dst_scr[h] = d_sts[h]

    rev = lambda c: nc - 1 - c
    return pl.pallas_call(
        body, name="hgrn2_bwd", grid=(nc,),
        in_specs=[pl.BlockSpec((rows, 4 * D_GROUP), lambda c: (rev(c), 0)),
                  pl.BlockSpec((2, D_GROUP), lambda c: (0, 0)),
                  pl.BlockSpec((1, D_GROUP), lambda c: (0, 0)),
                  pl.BlockSpec((None, N_HEADS, D_HEAD, D_HEAD), lambda c: (rev(c), 0, 0, 0)),
                  pl.BlockSpec((rows, D_GROUP), lambda c: (rev(c), 0))],
        out_specs=[pl.BlockSpec((rows, 4 * D_GROUP), lambda c: (rev(c), 0)),
                   pl.BlockSpec((2, D_GROUP), lambda c: (0, 0)),
                   pl.BlockSpec((1, D_GROUP), lambda c: (0, 0)),
                   pl.BlockSpec((1, 4 * D_GROUP), lambda c: (0, 0))],
        out_shape=[jax.ShapeDtypeStruct((s, D_IN_MAIN), BF16), jax.ShapeDtypeStruct((2, D_GROUP), F32),
                   jax.ShapeDtypeStruct((1, D_GROUP), F32), jax.ShapeDtypeStruct((1, 4 * D_GROUP), F32)],
        scratch_shapes=[pltpu.VMEM((N_HEADS, D_HEAD, D_HEAD), F32)],
        compiler_params=_params(("arbitrary",)),
    )(proj, logits, norm_w, states, d_y)


def _gate_column(gates, lane, idx):
    return jnp.sum(jnp.where(lane == idx, gates, 0.0), axis=1, keepdims=True)


def _head_layer_norm(h):
    mu = jnp.mean(h, axis=-1, keepdims=True)
    var = jnp.mean(jnp.square(h - mu), axis=-1, keepdims=True)
    return (h - mu) * lax.rsqrt(var + LN_EPS)


def _ml_chunk(qc, kc, v, mo, gates, nw, cts, ns, ms):
    n = qc.shape[0] // CHUNK
    row, col = _chunk_masks(CHUNK)
    mask = col <= row
    eye = col == row
    to_row = lambda t: jnp.sum(jnp.where(eye, t, 0.0), axis=0, keepdims=True)
    q = _blocks(qc * (D_HEAD ** -0.5))
    k = _blocks(kc)
    vs = _blocks(v)
    gate_rows = _split_chunks(gates)
    lane = lax.broadcasted_iota(jnp.int32, gate_rows[0].shape, 1)
    each = [(i, h) for i in range(n) for h in HEADS]
    on_each = lambda f: {ih: f(*ih) for ih in each}
    ig = on_each(lambda i, h: _gate_column(gate_rows[i], lane, h))
    lf = on_each(lambda i, h: jax.nn.log_sigmoid(_gate_column(gate_rows[i], lane, N_HEADS + h)))
    lf_row = on_each(lambda i, h: to_row(lf[i, h]))
    ig_row = on_each(lambda i, h: to_row(ig[i, h]))
    b_col = on_each(lambda i, h: jnp.sum(jnp.where(mask, lf_row[i, h], 0.0), axis=1, keepdims=True))
    b_row = on_each(lambda i, h: jnp.sum(jnp.where(row <= col, lf[i, h], 0.0), axis=0, keepdims=True))
    g = on_each(lambda i, h: jnp.sum(lf[i, h], axis=0, keepdims=True))
    d = on_each(lambda i, h: jnp.where(mask, b_col[i, h] - b_row[i, h] + ig_row[i, h], -jnp.inf))
    a = on_each(lambda i, h: g[i, h] - b_col[i, h] + ig[i, h])
    m_at = {(0, h): ms[h] for h in HEADS}
    for i, h in each:
        m_at[i + 1, h] = lax.stop_gradient(jnp.maximum(g[i, h] + m_at[i, h], jnp.max(a[i, h], axis=0, keepdims=True)))
    inter = on_each(lambda i, h: b_col[i, h] + m_at[i, h])
    m_t = on_each(lambda i, h: lax.stop_gradient(jnp.maximum(inter[i, h], jnp.max(d[i, h], axis=1, keepdims=True))))
    qk = on_each(lambda i, h: _nt(q[i][h], k[i][h]))
    sc = on_each(lambda i, h: qk[i, h] * jnp.exp(d[i, h] - m_t[i, h]))
    w_inter = on_each(lambda i, h: jnp.exp(inter[i, h] - m_t[i, h]))
    sv = on_each(lambda i, h: _nn(sc[i, h], vs[i][h]))
    decay = on_each(lambda i, h: jnp.exp(g[i, h] + m_at[i, h] - m_at[i + 1, h]))
    wk = on_each(lambda i, h: k[i][h] * jnp.exp(a[i, h] - m_at[i + 1, h]))
    kv = on_each(lambda i, h: _tn(vs[i][h], wk[i, h]))
    normed = []
    for i in range(n):
        qc_state = [_nt(q[i][h], cts[h]) for h in HEADS]
        num = [sv[i, h] + w_inter[i, h] * qc_state[h] for h in HEADS]
        den = [jnp.sum(sc[i, h], axis=1, keepdims=True)
               + w_inter[i, h] * jnp.sum(q[i][h] * ns[h], axis=1, keepdims=True) for h in HEADS]
        hh = [num[h] / jnp.maximum(jnp.abs(den[h]), jnp.exp(-m_t[i, h])) for h in HEADS]
        cts = tuple(decay[i, h] * cts[h] + kv[i, h] for h in HEADS)
        ns = tuple(decay[i, h] * ns[h] + jnp.sum(wk[i, h], axis=0, keepdims=True) for h in HEADS)
        normed.append(_merge_heads(tuple(_head_layer_norm(hh[h]) for h in HEADS)))
    y = jax.nn.sigmoid(mo) * (_merge_chunks(tuple(normed)) * nw)
    return y, cts, ns, tuple(m_at[n, h] for h in HEADS)


def _mlstm_fwd(qk, proj, gates, norm_w, y):
    s = proj.shape[0]
    rows = ML_CHUNKS_PER_STEP * CHUNK
    nc = s // rows

    def body(qk_ref, vo_ref, g_ref, nw_ref, _, y_ref, ct_out, n_out, m_out, ct_scr, n_scr, m_scr):
        @pl.when(pl.program_id(0) == 0)
        def _():
            ct_scr[...] = jnp.zeros_like(ct_scr)
            n_scr[...] = jnp.zeros_like(n_scr)
            m_scr[...] = jnp.full(m_scr.shape, NEG_BIG, F32)

        cts = tuple(ct_scr[h] for h in HEADS)
        ns = tuple(n_scr[h] for h in HEADS)
        ms = tuple(m_scr[h] for h in HEADS)
        y, cts_new, ns_new, ms_new = _ml_chunk(_seg(qk_ref, 0), _seg(qk_ref, 1), _seg(vo_ref, 0), _seg(vo_ref, 1),
                                               g_ref[...], nw_ref[...], cts, ns, ms)
        y_ref[...] = y.astype(y_ref.dtype)
        for h in HEADS:
            ct_out[h], n_out[h], m_out[h] = cts[h], ns[h], ms[h]
            ct_scr[h], n_scr[h], m_scr[h] = cts_new[h], ns_new[h], ms_new[h]

    st = lambda r, w: pl.BlockSpec((None, N_HEADS, r, w), lambda c: (c, 0, 0, 0))
    return pl.pallas_call(
        body, name="mlstm_fwd", grid=(nc,),
        in_specs=[pl.BlockSpec((rows, 2 * D_GROUP), lambda c: (c, 0)),
                  pl.BlockSpec((rows, 2 * D_GROUP), lambda c: (c, 3)),
                  pl.BlockSpec((rows, LANES), lambda c: (c, 0)),
                  pl.BlockSpec((1, D_GROUP), lambda c: (0, 0)),
                  pl.BlockSpec(memory_space=pl.ANY)],
        out_specs=[pl.BlockSpec((rows, D_GROUP), lambda c: (c, 1)),
                   st(D_HEAD, D_HEAD), st(1, D_HEAD), st(1, 1)],
        out_shape=[jax.ShapeDtypeStruct(y.shape, y.dtype),
                   jax.ShapeDtypeStruct((nc, N_HEADS, D_HEAD, D_HEAD), F32),
                   jax.ShapeDtypeStruct((nc, N_HEADS, 1, D_HEAD), F32),
                   jax.ShapeDtypeStruct((nc, N_HEADS, 1, 1), F32)],
        input_output_aliases={4: 0},
        scratch_shapes=[pltpu.VMEM((N_HEADS, D_HEAD, D_HEAD), F32), pltpu.VMEM((N_HEADS, 1, D_HEAD), F32),
                        pltpu.VMEM((N_HEADS, 1, 1), F32)],
        compiler_params=_params(("arbitrary",)),
    )(qk, proj, gates, norm_w, y)


def _mlstm_bwd(qk, proj, gates, norm_w, ct_s, n_s, m_s, d_y, d_proj):
    s = proj.shape[0]
    rows = ML_CHUNKS_PER_STEP * CHUNK
    nc = s // rows

    def body(qk_ref, vo_ref, g_ref, nw_ref, ct_ref, n_ref, m_ref, dy_ref, _,
             dp_ref, dqk_ref, dg_ref, dnw_ref, dsum_ref, dct_scr, dn_scr):
        @pl.when(pl.program_id(0) == 0)
        def _():
            dct_scr[...] = jnp.zeros_like(dct_scr)
            dn_scr[...] = jnp.zeros_like(dn_scr)
            dnw_ref[...] = jnp.zeros_like(dnw_ref)
            dsum_ref[...] = jnp.zeros_like(dsum_ref)

        ms = tuple(m_ref[h] for h in HEADS)
        step = lambda *a: _ml_chunk(*a, ms)[:3]
        _, vjp = jax.vjp(step, _seg(qk_ref, 0), _seg(qk_ref, 1), _seg(vo_ref, 0), _seg(vo_ref, 1), g_ref[...],
                         nw_ref[...], tuple(ct_ref[h] for h in HEADS), tuple(n_ref[h] for h in HEADS))
        d_q, d_k, d_v, d_o, d_gates, d_nw, d_cts, d_ns = vjp(
            (dy_ref[...], tuple(dct_scr[h] for h in HEADS), tuple(dn_scr[h] for h in HEADS)))
        dqk_ref[:, 0:D_GROUP] = d_q
        dqk_ref[:, D_GROUP:2 * D_GROUP] = d_k
        for seg, val in enumerate((d_v, d_o)):
            dp_ref[:, seg * D_GROUP:(seg + 1) * D_GROUP] = val.astype(dp_ref.dtype)
            dsum_ref[:, seg * D_GROUP:(seg + 1) * D_GROUP] += jnp.sum(val, axis=0, keepdims=True)
        dg_ref[...] = d_gates
        dnw_ref[...] += d_nw
        for h in HEADS:
            dct_scr[h] = d_cts[h]
            dn_scr[h] = d_ns[h]

    rev = lambda c: nc - 1 - c
    st = lambda r, w: pl.BlockSpec((None, N_HEADS, r, w), lambda c: (rev(c), 0, 0, 0))
    return pl.pallas_call(
        body, name="mlstm_bwd", grid=(nc,),
        in_specs=[pl.BlockSpec((rows, 2 * D_GROUP), lambda c: (rev(c), 0)),
                  pl.BlockSpec((rows, 2 * D_GROUP), lambda c: (rev(c), 3)),
                  pl.BlockSpec((rows, LANES), lambda c: (rev(c), 0)),
                  pl.BlockSpec((1, D_GROUP), lambda c: (0, 0)),
                  st(D_HEAD, D_HEAD), st(1, D_HEAD), st(1, 1),
                  pl.BlockSpec((rows, D_GROUP), lambda c: (rev(c), 1)),
                  pl.BlockSpec(memory_space=pl.ANY)],
        out_specs=[pl.BlockSpec((rows, 2 * D_GROUP), lambda c: (rev(c), 3)),
                   pl.BlockSpec((rows, 2 * D_GROUP), lambda c: (rev(c), 0)),
                   pl.BlockSpec((rows, LANES), lambda c: (rev(c), 0)),
                   pl.BlockSpec((1, D_GROUP), lambda c: (0, 0)),
                   pl.BlockSpec((1, 2 * D_GROUP), lambda c: (0, 0))],
        out_shape=[jax.ShapeDtypeStruct(d_proj.shape, d_proj.dtype), jax.ShapeDtypeStruct((s, 2 * D_GROUP), F32),
                   jax.ShapeDtypeStruct((s, LANES), F32), jax.ShapeDtypeStruct((1, D_GROUP), F32),
                   jax.ShapeDtypeStruct((1, 2 * D_GROUP), F32)],
        input_output_aliases={8: 0},
        scratch_shapes=[pltpu.VMEM((N_HEADS, D_HEAD, D_HEAD), F32), pltpu.VMEM((N_HEADS, 1, D_HEAD), F32)],
        compiler_params=_params(("arbitrary",)),
    )(qk, proj, gates, norm_w, ct_s, n_s, m_s, d_y, d_proj)


LN_TOKENS = 512
ATT_TOKENS = 512


def _proj_res_ln(a, w, xres, g, b, name):
    s, dm = xres.shape
    k = a.shape[1]
    tb = min(LN_TOKENS, s)

    def body(a_ref, w_ref, x_ref, g_ref, b_ref, z_ref, o_ref):
        z = ALPHA * x_ref[...] + _nn_raw(a_ref[...], w_ref[...])
        z_ref[...] = z
        o_ref[...] = _layer_norm(z, g_ref[...], b_ref[...])

    tok = pl.BlockSpec((tb, dm), lambda i: (i, 0))
    vec = pl.BlockSpec((1, dm), lambda i: (0, 0))
    act = jax.ShapeDtypeStruct((s, dm), F32)
    return pl.pallas_call(
        body, name=name, grid=(s // tb,),
        in_specs=[pl.BlockSpec((tb, k), lambda i: (i, 0)), pl.BlockSpec((k, dm), lambda i: (0, 0)), tok, vec, vec],
        out_specs=[tok, tok], out_shape=[act, act], compiler_params=_params(("parallel",)),
    )(a, w, xres, g, b)


def _ln_bwd_proj(d_out, z, g, b, w, name):
    s, dm = z.shape
    k = w.shape[0]
    tb = min(LN_TOKENS, s)

    def body(do_ref, z_ref, g_ref, b_ref, w_ref, dz_ref, da_ref, dg_ref, db_ref):
        @pl.when(pl.program_id(0) == 0)
        def _():
            dg_ref[...] = jnp.zeros_like(dg_ref)
            db_ref[...] = jnp.zeros_like(db_ref)

        _, vjp = jax.vjp(_layer_norm, z_ref[...], g_ref[...], b_ref[...])
        d_z, d_g, d_b = vjp(do_ref[...])
        dz_ref[...] = d_z
        da_ref[...] = _nt_raw(d_z, w_ref[...])
        dg_ref[...] += d_g
        db_ref[...] += d_b

    tok = pl.BlockSpec((tb, dm), lambda i: (i, 0))
    vec = pl.BlockSpec((1, dm), lambda i: (0, 0))
    return pl.pallas_call(
        body, name=name, grid=(s // tb,),
        in_specs=[tok, tok, vec, vec, pl.BlockSpec((k, dm), lambda i: (0, 0))],
        out_specs=[tok, pl.BlockSpec((tb, k), lambda i: (i, 0)), vec, vec],
        out_shape=[jax.ShapeDtypeStruct((s, dm), F32), jax.ShapeDtypeStruct((s, k), F32),
                   jax.ShapeDtypeStruct((1, dm), F32), jax.ShapeDtypeStruct((1, dm), F32)],
        compiler_params=_params(("arbitrary",)),
    )(d_out, z, g, b, w)


def _proj_loss_tail(a, w, xres, g, b, target):
    s, dm = xres.shape
    k = a.shape[1]
    tb = min(ATT_TOKENS, s)

    def loss_fn(z, gg, bb, tgt):
        err = jnp.square(_layer_norm(z, gg, bb) - tgt)
        return 0.5 * jnp.sum(jnp.mean(err, axis=-1, keepdims=True), axis=0, keepdims=True)

    def body(a_ref, w_ref, x_ref, g_ref, b_ref, t_ref, loss_ref, dz_ref, dg_ref, db_ref):
        @pl.when(pl.program_id(0) == 0)
        def _():
            loss_ref[...] = jnp.zeros_like(loss_ref)
            dg_ref[...] = jnp.zeros_like(dg_ref)
            db_ref[...] = jnp.zeros_like(db_ref)

        halves = [slice(0, tb // 2), slice(tb // 2, tb)]
        zs = [ALPHA * x_ref[rows, :] + _nn_raw(a_ref[rows, :], w_ref[...]) for rows in halves]
        for rows, z in zip(halves, zs):
            tgt = t_ref[rows, :]
            loss, vjp = jax.vjp(lambda zz, gg, bb, tgt=tgt: loss_fn(zz, gg, bb, tgt), z, g_ref[...], b_ref[...])
            d_z, d_g, d_b = vjp(jnp.ones((1, 1), F32))
            loss_ref[...] += loss
            dz_ref[rows, :] = d_z
            dg_ref[...] += d_g
            db_ref[...] += d_b

    tok = pl.BlockSpec((tb, dm), lambda i: (i, 0))
    vec = pl.BlockSpec((1, dm), lambda i: (0, 0))
    one = pl.BlockSpec((1, 1), lambda i: (0, 0))
    return pl.pallas_call(
        body, name="ffn_down_loss_tail", grid=(s // tb,),
        in_specs=[pl.BlockSpec((tb, k), lambda i: (i, 0)), pl.BlockSpec((k, dm), lambda i: (0, 0)), tok, vec, vec, tok],
        out_specs=[one, tok, vec, vec],
        out_shape=[jax.ShapeDtypeStruct((1, 1), F32), jax.ShapeDtypeStruct((s, dm), F32),
                   jax.ShapeDtypeStruct((1, dm), F32), jax.ShapeDtypeStruct((1, dm), F32)],
        compiler_params=_params(("arbitrary",)),
    )(a, w, xres, g, b, target)


def _att_heads(qs, ks, vs):
    sc = [_nt(q, k) * (CA_DH ** -0.5) for q, k in zip(qs, ks)]
    p = [jax.nn.softmax(s, axis=-1) for s in sc]
    return tuple(_nn(pp, v) for pp, v in zip(p, vs))


def _head_slices(ref_or_value, offset):
    return tuple(ref_or_value[:, offset + h * CA_DH:offset + (h + 1) * CA_DH] for h in range(CA_HEADS))


def _cross_attention_fwd(x1, kv, wq, wo, g, b):
    s = x1.shape[0]
    tb = min(ATT_TOKENS, s)

    def body(x_ref, kv_ref, wq_ref, wo_ref, g_ref, b_ref, att_ref, z_ref, o_ref):
        x_blk = x_ref[...]
        q = _nn_raw(x_blk, wq_ref[...])
        att = jnp.concatenate(_att_heads(_head_slices(q, 0), _head_slices(kv_ref, 0), _head_slices(kv_ref, D_MODEL)),
                              axis=1)
        att_ref[...] = att.astype(att_ref.dtype)
        z = ALPHA * x_blk + _nn_raw(att, wo_ref[...])
        z_ref[...] = z
        o_ref[...] = _layer_norm(z, g_ref[...], b_ref[...])

    tok = pl.BlockSpec((tb, D_MODEL), lambda i: (i, 0))
    mat = pl.BlockSpec((D_MODEL, D_MODEL), lambda i: (0, 0))
    vec = pl.BlockSpec((1, D_MODEL), lambda i: (0, 0))
    act = jax.ShapeDtypeStruct((s, D_MODEL), F32)
    return pl.pallas_call(
        body, name="cross_attention_fwd", grid=(s // tb,),
        in_specs=[tok, pl.BlockSpec((N_MEM, 2 * D_MODEL), lambda i: (0, 0)), mat, mat, vec, vec],
        out_specs=[tok, tok, tok],
        out_shape=[jax.ShapeDtypeStruct((s, D_MODEL), BF16), act, act],
        compiler_params=_params(("parallel",)),
    )(x1, kv, wq, wo, g, b)


def _cross_attention_bwd(d_x2, x1, z2, kv, wq, wo, g, b):
    s = x1.shape[0]
    tb = min(ATT_TOKENS, s)

    def body(dx2_ref, x_ref, z_ref, kv_ref, wq_ref, wo_ref, g_ref, b_ref,
             dx1_ref, dq_ref, dz_ref, dkv_ref, dg_ref, db_ref):
        @pl.when(pl.program_id(0) == 0)
        def _():
            dkv_ref[...] = jnp.zeros_like(dkv_ref)
            dg_ref[...] = jnp.zeros_like(dg_ref)
            db_ref[...] = jnp.zeros_like(db_ref)

        _, ln_vjp = jax.vjp(_layer_norm, z_ref[...], g_ref[...], b_ref[...])
        d_z, d_g, d_b = ln_vjp(dx2_ref[...])
        dg_ref[...] += d_g
        db_ref[...] += d_b
        dz_ref[...] = d_z.astype(dz_ref.dtype)
        d_att = _nt_raw(d_z, wo_ref[...])
        q = _nn_raw(x_ref[...], wq_ref[...])
        _, vjp = jax.vjp(_att_heads, _head_slices(q, 0), _head_slices(kv_ref, 0), _head_slices(kv_ref, D_MODEL))
        d_qs, d_ks, d_vs = vjp(_head_slices(d_att, 0))
        for h in range(CA_HEADS):
            lo = h * CA_DH
            dkv_ref[:, lo:lo + CA_DH] += d_ks[h]
            dkv_ref[:, D_MODEL + lo:D_MODEL + lo + CA_DH] += d_vs[h]
        d_q = jnp.concatenate(d_qs, axis=1)
        dq_ref[...] = d_q.astype(dq_ref.dtype)
        dx1_ref[...] = ALPHA * d_z + _nt_raw(d_q, wq_ref[...])

    tok = pl.BlockSpec((tb, D_MODEL), lambda i: (i, 0))
    mem = pl.BlockSpec((N_MEM, 2 * D_MODEL), lambda i: (0, 0))
    mat = pl.BlockSpec((D_MODEL, D_MODEL), lambda i: (0, 0))
    vec = pl.BlockSpec((1, D_MODEL), lambda i: (0, 0))
    low = jax.ShapeDtypeStruct((s, D_MODEL), BF16)
    return pl.pallas_call(
        body, name="cross_attention_bwd", grid=(s // tb,),
        in_specs=[tok, tok, tok, mem, mat, mat, vec, vec], out_specs=[tok, tok, tok, mem, vec, vec],
        out_shape=[jax.ShapeDtypeStruct((s, D_MODEL), F32), low, low,
                   jax.ShapeDtypeStruct((N_MEM, 2 * D_MODEL), F32),
                   jax.ShapeDtypeStruct((1, D_MODEL), F32), jax.ShapeDtypeStruct((1, D_MODEL), F32)],
        compiler_params=_params(("arbitrary",)),
    )(d_x2, x1, z2, kv, wq, wo, g, b)


def _local_step(x, mem, target, w, mid_weights=None, ffn_weights=None, down_weights=None, on_ffn_grads=None,
                on_mid_grads=None,
                on_small_grads=None, on_last_grads=None):
    w = dict(w)
    s = x.shape[0]
    tm = min(512, s)
    tt = min(512, s)
    tt_big = min(1024, s)
    proj = _matmul_nn(x, w["w_in_main"], w["b_in_main"], min(2048, s), 512, "proj")
    gates = _matmul_nn(x, w["w_in_gate"], w["b_in_gate"], tm, LANES, "proj_gates")
    qk = _ml_conv_fwd(proj, w["ml_conv_w"], w["ml_conv_b"])
    y, hg_states = _hgrn2_fwd(proj, w["hg_lb_logits"], w["hg_norm_w"])
    y, ct_s, n_s, m_s = _mlstm_fwd(qk, proj, gates, w["ml_norm_w"], y)
    if mid_weights is not None:
        w.update(mid_weights(y))
    z1, x1 = _proj_res_ln(y, w["w_out"], x, w["ln1_g"], w["ln1_b"], "out_proj_ln1")
    kv = _matmul_nn(mem, w["ca_wkv"], None, N_MEM, CA_DH, "kv")
    att, z2, x2 = _cross_attention_fwd(x1, kv, w["ca_wq"], w["ca_wo"], w["ln2_g"], w["ln2_b"])
    if ffn_weights is not None:
        w.update(ffn_weights(x2))
    u = _matmul_nn(x2, w["ffn_w_up"], None, min(2048, s), UP_SHARD_P, "ffn_up", BF16)
    hid = _ffn_conv_fwd(u, w["ffn_conv_w"], w["ffn_conv_b"])
    if down_weights is not None:
        w.update(down_weights(hid))
    loss, d_z3, d_ln3_g, d_ln3_b = _proj_loss_tail(hid, w["ffn_w_down"], x2, w["ln3_g"], w["ln3_b"], target)
    grads = {"ln3_g": d_ln3_g, "ln3_b": d_ln3_b}
    grads["ffn_w_down"] = _matmul_tn(hid, d_z3, 1536, D_MODEL, tt_big, "d_w_down")
    d_hid = _matmul_nt([(d_z3, w["ffn_w_down"])], None, 1.0, tm, D_FF_P, "d_hid", BF16)
    d_ug, d_uv, d_cwg, d_cwv, d_cbg, d_cbv = _ffn_conv_bwd(u, w["ffn_conv_w"], w["ffn_conv_b"], d_hid)
    grads["ffn_conv_w"] = jnp.concatenate([d_cwg, d_cwv], axis=-1)
    grads["ffn_conv_b"] = jnp.concatenate([d_cbg, d_cbv], axis=-1)
    half = N_DEV // 2
    d_w_up = _matmul_tn(x2, d_ug, D_MODEL, UP_SHARD_P, tt_big, "d_w_up_gate", shards=N_DEV, group=half)
    grads["ffn_w_up"] = _matmul_tn(x2, d_uv, D_MODEL, UP_SHARD_P, tt_big, "d_w_up_val", shards=N_DEV,
                                   shard0=half, group=half, into=d_w_up)
    d_x2 = _matmul_nt([(d_ug, w["ffn_w_up"], 0), (d_uv, w["ffn_w_up"], N_DEV // 2)], d_z3, ALPHA,
                      min(256, s), D_MODEL, "d_x2")
    if on_ffn_grads is not None:
        d_x2 = on_ffn_grads(grads, d_x2)
    d_x1, d_q, d_z2, d_kv, grads["ln2_g"], grads["ln2_b"] = _cross_attention_bwd(
        d_x2, x1, z2, kv, w["ca_wq"], w["ca_wo"], w["ln2_g"], w["ln2_b"])
    grads["ca_wo"] = _matmul_tn(att, d_z2, D_MODEL, D_MODEL, tt, "d_ca_wo")
    grads["ca_wq"] = _matmul_tn(x1, d_q, D_MODEL, D_MODEL, tt, "d_ca_wq")
    grads["ca_wkv"] = _matmul_tn(mem, d_kv, D_MODEL, CA_DH, N_MEM, "d_ca_wkv", shards=N_DEV, group=N_DEV)
    d_z1, d_y, grads["ln1_g"], grads["ln1_b"] = _ln_bwd_proj(d_x1, z1, w["ln1_g"], w["ln1_b"], w["w_out"],
                                                             "ln1_bwd_out_proj")
    grads["w_out"] = _matmul_tn(y, d_z1, D_MODEL, D_MODEL, tt, "d_w_out")
    if on_mid_grads is not None:
        d_y = on_mid_grads(grads, d_y)
    d_proj, grads["hg_lb_logits"], grads["hg_norm_w"], db_hg = _hgrn2_bwd(
        proj, w["hg_lb_logits"], w["hg_norm_w"], hg_states, d_y)
    d_proj, d_qk, d_gates, grads["ml_norm_w"], db_vo = _mlstm_bwd(
        qk, proj, gates, w["ml_norm_w"], ct_s, n_s, m_s, d_y, d_proj)
    d_proj, grads["ml_conv_w"], grads["ml_conv_b"], db_qk = _ml_conv_bwd(
        proj, w["ml_conv_w"], w["ml_conv_b"], d_qk, d_proj)
    grads["b_in_main"] = jnp.concatenate([db_hg, db_qk, db_vo], axis=-1)
    grads["w_in_gate"], grads["b_in_gate"] = _matmul_tn(x, d_gates, D_MODEL, LANES, tt, "d_w_in_gates", colsum=True)
    if on_small_grads is not None:
        d_proj = on_small_grads(grads, loss, d_proj)
    grads["w_in_main"] = _matmul_tn(x, d_proj, D_MODEL, min(2048, D_IN_MAIN), tt_big, "d_w_in")
    if on_last_grads is not None:
        d_z1 = on_last_grads(grads, d_z1)
    grad_x = _matmul_nt([(d_proj, w["w_in_main"]), (d_gates, w["w_in_gate"])], d_z1, ALPHA, tm, D_MODEL, "d_x")
    return loss, grad_x, grads


HBM_SPEC = pl.BlockSpec(memory_space=pltpu.HBM)


def _coords():
    return lax.axis_index("x"), lax.axis_index("y"), lax.axis_index("c")


def _other_chips(x, y):
    return [(1 - x, y), (x, 1 - y), (1 - x, 1 - y)]


def _my_slot():
    x, y, c = _coords()
    return 4 * x + 2 * y + c


SEM_SPEC = pl.BlockSpec(memory_space=pltpu.SEMAPHORE)
ANY_SPEC = pl.BlockSpec(memory_space=pl.ANY)
SIDE_EFFECT = pltpu.SideEffectType.DATAFLOW_SIDE_EFFECTING


def _peer(x, y, c, d):
    flip = lambda v, bit: 1 - v if bit else v
    p = (flip(x, d & 4), flip(y, d & 2), flip(c, d & 1))
    return p, 4 * p[0] + 2 * p[1] + p[2]


def _direct_copies(gather, src_refs, land_refs, send_sems, recv_sems):
    x, y, c = _coords()
    me = 4 * x + 2 * y + c
    copies = []
    for a in range(len(src_refs)):
        for d in range(1, N_DEV):
            peer, peer_slot = _peer(x, y, c, d)
            copies.append(pltpu.make_async_remote_copy(
                src_ref=src_refs[a] if gather else src_refs[a].at[peer_slot],
                dst_ref=land_refs[a].at[me] if gather else land_refs[a].at[d - 1],
                send_sem=send_sems.at[7 * a + d - 1], recv_sem=recv_sems.at[7 * a + d - 1],
                device_id=peer, device_id_type=MESH))
    return copies


def _hbm(t):
    return pltpu.HBM(t.shape, t.dtype)


def _chip_copies(src_refs, land_refs, send_sems, recv_sems):
    x, y, c = _coords()
    me = 4 * x + 2 * y + c
    targets = [(x, y, 1 - c)] + [(cx, cy, c) for cx, cy in _other_chips(x, y)]
    return [pltpu.make_async_remote_copy(
        src_ref=src_refs[a], dst_ref=land_refs[a].at[me], send_sem=send_sems.at[4 * a + k],
        recv_sem=recv_sems.at[4 * a + k], device_id=target, device_id_type=MESH)
        for a in range(len(src_refs)) for k, target in enumerate(targets)]


def _forward_copies(land_refs, send_sems, recv_sems):
    x, y, c = _coords()
    return [pltpu.make_async_remote_copy(
        src_ref=land_refs[a].at[4 * cx + 2 * cy + c], dst_ref=land_refs[a].at[4 * cx + 2 * cy + c],
        send_sem=send_sems.at[3 * a + j], recv_sem=recv_sems.at[3 * a + j],
        device_id=(x, y, 1 - c), device_id_type=MESH)
        for a in range(len(land_refs)) for j, (cx, cy) in enumerate(_other_chips(x, y))]


def _split_copy_start(make_copies, n_sems, operands, through, name):
    n_ops = len(operands)

    def body(*refs):
        for cp in make_copies(refs[:n_ops], refs[n_ops + 1], refs[n_ops + 2]):
            cp.start()

    ins = [pltpu.with_memory_space_constraint(t, pltpu.HBM) for t in (*operands, through)]
    sems = pltpu.SemaphoreType.DMA((n_sems,))
    res = pl.pallas_call(
        body, name=name, out_shape=(sems, sems, *[_hbm(t) for t in ins]),
        in_specs=[HBM_SPEC] * (n_ops + 1), out_specs=(SEM_SPEC, SEM_SPEC, *[HBM_SPEC] * (n_ops + 1)),
        input_output_aliases={i: 2 + i for i in range(n_ops + 1)},
        compiler_params=pltpu.CompilerParams(has_side_effects=SIDE_EFFECT),
    )(*ins)
    return (res[0], res[1], list(res[2:2 + n_ops])), res[2 + n_ops]


def _split_copy_wait(make_copies, started, after, name):
    send_sems, recv_sems, operands = started
    n_ops = len(operands)
    after = list(after) if isinstance(after, (list, tuple)) else [after]

    def body(*refs):
        for cp in make_copies(refs[:n_ops], refs[n_ops], refs[n_ops + 1]):
            cp.wait_send()
            cp.wait_recv()

    res = pl.pallas_call(
        body, name=name, out_shape=tuple(_hbm(t) for t in operands),
        in_specs=[HBM_SPEC] * n_ops + [SEM_SPEC, SEM_SPEC] + [ANY_SPEC] * len(after),
        out_specs=tuple([HBM_SPEC] * n_ops), input_output_aliases={i: i for i in range(n_ops)},
        compiler_params=pltpu.CompilerParams(has_side_effects=SIDE_EFFECT),
    )(*operands, send_sems, recv_sems, *after)
    return list(res)


def _halves(make_copies, na):
    return lambda refs, send_sems, recv_sems: make_copies(refs[:na], refs[na:], send_sems, recv_sems)


def _direct_start(gather, arrays, through, name):
    na = len(arrays)
    lands = [lax.empty((N_DEV,) + t.shape if gather else (N_DEV - 1,) + t.shape[1:], t.dtype) for t in arrays]
    return _split_copy_start(_halves(functools.partial(_direct_copies, gather), na), 7 * na, [*arrays, *lands],
                             through, name)


def _direct_wait(gather, started, after, name):
    na = len(started[2]) // 2
    operands = _split_copy_wait(_halves(functools.partial(_direct_copies, gather), na), started, after, name)
    return operands[:na], operands[na:]


def _two_level_gather(shards, glue, name):
    na = len(shards)
    lands = [lax.empty((N_DEV,) + t.shape, t.dtype) for t in shards]
    nothing = jnp.zeros((SUBLANES, LANES), F32)
    started, _ = _split_copy_start(_halves(_chip_copies, na), 4 * na, [*shards, *lands], nothing, name + "_start")
    operands = _split_copy_wait(_halves(_chip_copies, na), started, glue, name + "_wait")
    started, mine = _split_copy_start(_forward_copies, 3 * na, operands[na:], operands[0], name + "_forward_start")
    lands = _split_copy_wait(_forward_copies, started, mine, name + "_forward_wait")
    return [lax.dynamic_update_index_in_dim(land, own, _my_slot(), 0)
            for own, land in zip([mine, *operands[1:na]], lands)]


def _row_tile(rows):
    for t in (256, 176, 128):
        if rows % t == 0 and rows > t:
            return t
    return rows


def _adamw_math(g, w, m, v):
    m_new = ADAM_B1 * m + (1.0 - ADAM_B1) * g
    v_new = ADAM_B2 * v + (1.0 - ADAM_B2) * jnp.square(g)
    m_hat = m_new / (1.0 - ADAM_B1 ** ADAM_STEP)
    v_hat = v_new / (1.0 - ADAM_B2 ** ADAM_STEP)
    delta = -ADAM_LR * (m_hat / (jnp.sqrt(v_hat) + ADAM_EPS) + ADAM_WD * w)
    return delta, m_new, v_new


def _adamw_sharded(chip, sums, got, w, m, v, name):
    r, c = w.shape
    tr = _row_tile(r)
    n_got = got.shape[0]

    def body(chip_ref, s_ref, g_ref, w_ref, m_ref, v_ref, go_ref, d_ref, nm_ref, nv_ref):
        g = s_ref[...].astype(F32)
        for i in range(n_got):
            g = g + g_ref[i].astype(F32)
        delta, m_new, v_new = _adamw_math(g, w_ref[...], m_ref[...], v_ref[...])
        go_ref[...] = g
        d_ref[...] = delta
        nm_ref[...] = m_new
        nv_ref[...] = v_new

    blk = pl.BlockSpec((tr, c), lambda i, chip_ref: (i, 0))
    out = jax.ShapeDtypeStruct((r, c), F32)
    return pl.pallas_call(
        body, name=name,
        grid_spec=pltpu.PrefetchScalarGridSpec(
            num_scalar_prefetch=1, grid=(r // tr,),
            in_specs=[pl.BlockSpec((None, tr, c), lambda i, chip_ref: (chip_ref[0], i, 0)),
                      pl.BlockSpec((n_got, tr, c), lambda i, chip_ref: (0, i, 0)), blk, blk, blk],
            out_specs=[blk, blk, blk, blk]),
        out_shape=[out, out, out, out],
        compiler_params=_params(("parallel",)),
    )(chip, sums, got, w, m, v)


def _adamw_replicated(parts, w, m, v):
    p, r, c = parts.shape

    def body(p_ref, w_ref, m_ref, v_ref, g_ref, d_ref, nm_ref, nv_ref):
        g = p_ref[0]
        for i in range(1, p):
            g = g + p_ref[i]
        delta, m_new, v_new = _adamw_math(g, w_ref[...], m_ref[...], v_ref[...])
        g_ref[...] = g
        d_ref[...] = delta
        nm_ref[...] = m_new
        nv_ref[...] = v_new

    blk = pl.BlockSpec((r, c), lambda i: (0, 0))
    out = jax.ShapeDtypeStruct((r, c), F32)
    return pl.pallas_call(
        body, name="adamw_replicated", grid=(1,),
        in_specs=[pl.BlockSpec((p, r, c), lambda i: (0, 0, 0)), blk, blk, blk],
        out_specs=[blk, blk, blk, blk], out_shape=[out, out, out, out],
        compiler_params=_params(("arbitrary",)),
    )(parts, w, m, v)


SHARDED_NAMES = ("w_in", "ml_conv_w", "w_out", "ca_wq", "ca_wkv", "ca_wo", "ffn_w_up", "ffn_conv_w", "ffn_w_down")
SMALL_NAMES = ("b_in", "hg_lb_logits", "hg_norm_w", "ml_conv_b", "ml_norm_w", "ln1_g", "ln1_b",
               "ln2_g", "ln2_b", "ffn_conv_b", "ln3_g", "ln3_b")
WEIGHT_NAMES = ("w_in", "b_in", "hg_lb_logits", "hg_norm_w", "ml_conv_w", "ml_conv_b", "ml_norm_w", "w_out",
                "ln1_g", "ln1_b", "ca_wq", "ca_wkv", "ca_wo", "ln2_g", "ln2_b", "ffn_w_up", "ffn_conv_w",
                "ffn_conv_b", "ffn_w_down", "ln3_g", "ln3_b")
PAD_TO = {"ffn_w_up": UP_SHARD_P, "ffn_conv_w": UP_SHARD_P}
SMALL_ROWS = 24
SMALL_W = D_MODEL


def _shard_2d(name, block):
    t = block[0]
    if name in PAD_TO:
        t = jnp.pad(t, ((0, 0), (0, PAD_TO[name] - t.shape[1])))
    return t


def _shard_like(name, t, like):
    return t[:, :like.shape[2]][None]


def _pad_cols(t, width):
    return jnp.pad(t, ((0, 0), (0, width - t.shape[1])))


FIRST_NAMES = ("w_in", "ml_conv_w")
FFN_NAMES = ("ffn_w_up", "ffn_w_down", "ffn_conv_w")
MID_NAMES = ("ca_wo", "ca_wq", "ca_wkv", "w_out")


def _first_weights(g, small):
    w = dict(small)
    last = g["w_in"][N_DEV - 1]
    split = D_IN_MAIN - (N_DEV - 1) * W_IN_SHARD
    w["w_in_main"] = jnp.concatenate([*[g["w_in"][j] for j in range(N_DEV - 1)], last[:, :split]], axis=1)
    w["w_in_gate"] = _pad_cols(last[:, split:], LANES)
    w["b_in_main"] = small["b_in"][:, :D_IN_MAIN]
    w["b_in_gate"] = _pad_cols(small["b_in"][:, D_IN_MAIN:], LANES)
    w["ml_conv_w"] = jnp.transpose(g["ml_conv_w"], (1, 0, 2)).reshape(ML_CONV, 2 * D_GROUP)
    return w


def _mid_weights(g):
    w = {n: g[n].reshape(D_MODEL, D_MODEL) for n in ("w_out", "ca_wq", "ca_wo")}
    w["ca_wkv"] = g["ca_wkv"]
    return w


FFN_UP_NAMES = ("ffn_w_up", "ffn_conv_w")
FFN_DOWN_NAMES = ("ffn_w_down",)


def _ffn_up_weights(g, small):
    w = {"ffn_w_up": g["ffn_w_up"]}
    w["ffn_conv_w"] = jnp.transpose(g["ffn_conv_w"], (1, 0, 2)).reshape(FFN_CONV, D_UP_P)
    w["ffn_conv_b"] = _pad_cols(small["ffn_conv_b"].reshape(N_DEV, UP_SHARD), UP_SHARD_P).reshape(1, D_UP_P)
    return w


def _ffn_down_weights(g):
    down = g["ffn_w_down"].reshape(N_DEV // 2, UP_SHARD, D_MODEL)
    return {"ffn_w_down": jnp.pad(down, ((0, 0), (0, UP_SHARD_P - UP_SHARD), (0, 0))).reshape(D_FF_P, D_MODEL)}


def _whole_weights(g, small):
    return {**_first_weights(g, small), **_mid_weights(g), **_ffn_up_weights(g, small), **_ffn_down_weights(g)}


def _owner_stack(n, grads):
    if n == "w_in":
        main, gate = grads["w_in_main"], grads["w_in_gate"][:, :D_IN - D_IN_MAIN]
        last = jnp.concatenate([main[:, (N_DEV - 1) * W_IN_SHARD:], gate], axis=1)
        return jnp.stack([*[main[:, j * W_IN_SHARD:(j + 1) * W_IN_SHARD] for j in range(N_DEV - 1)], last])
    if n in ("w_out", "ca_wq", "ca_wo"):
        return grads[n].reshape(N_DEV, D_MODEL // N_DEV, D_MODEL)
    if n == "ffn_w_down":
        down = grads[n].reshape(N_DEV // 2, UP_SHARD_P, D_MODEL)[:, :UP_SHARD]
        return down.reshape(N_DEV, D_FF // N_DEV, D_MODEL)
    if n == "ml_conv_w":
        return jnp.transpose(grads[n].reshape(ML_CONV, N_DEV, LANES), (1, 0, 2))
    if n == "ffn_conv_w":
        return jnp.transpose(grads[n].reshape(FFN_CONV, N_DEV, UP_SHARD_P), (1, 0, 2))
    return grads[n]


def _owner_stacks(grads):
    return {n: _owner_stack(n, grads) for n in SHARDED_NAMES}


def _small_grads(grads):
    out = {n: grads[n] for n in SMALL_NAMES if n in grads}
    out["b_in"] = jnp.concatenate([grads["b_in_main"], grads["b_in_gate"][:, :D_IN - D_IN_MAIN]], axis=1)
    out["ffn_conv_b"] = grads["ffn_conv_b"].reshape(N_DEV, UP_SHARD_P)[:, :UP_SHARD].reshape(1, D_UP)
    return out


def _pack_small(p, extra=None):
    flat = [p[n].reshape(-1) for n in SMALL_NAMES]
    if extra is not None:
        flat.append(extra.reshape(-1))
    flat = jnp.concatenate(flat)
    return jnp.pad(flat, (0, SMALL_ROWS * SMALL_W - flat.shape[0])).reshape(SMALL_ROWS, SMALL_W)


def _unpack_small(slab, like):
    out = {}
    flat = slab.reshape(-1)
    o = 0
    for n in SMALL_NAMES:
        out[n] = flat[o:o + like[n].size].reshape(like[n].shape)
        o += like[n].size
    return out, flat[o]


def kernel(x, mem, w_in, b_in, hg_lb_logits, hg_norm_w, ml_conv_w, ml_conv_b, ml_norm_w, w_out, ln1_g, ln1_b, ca_wq, ca_wkv, ca_wo, ln2_g, ln2_b, ffn_w_up, ffn_conv_w, ffn_conv_b, ffn_w_down, ln3_g, ln3_b, loss_target, m_w_in, m_b_in, m_hg_lb_logits, m_hg_norm_w, m_ml_conv_w, m_ml_conv_b, m_ml_norm_w, m_w_out, m_ln1_g, m_ln1_b, m_ca_wq, m_ca_wkv, m_ca_wo, m_ln2_g, m_ln2_b, m_ffn_w_up, m_ffn_conv_w, m_ffn_conv_b, m_ffn_w_down, m_ln3_g, m_ln3_b, v_w_in, v_b_in, v_hg_lb_logits, v_hg_norm_w, v_ml_conv_w, v_ml_conv_b, v_ml_norm_w, v_w_out, v_ln1_g, v_ln1_b, v_ca_wq, v_ca_wkv, v_ca_wo, v_ln2_g, v_ln2_b, v_ffn_w_up, v_ffn_conv_w, v_ffn_conv_b, v_ffn_w_down, v_ln3_g, v_ln3_b):
    params = dict(w_in=w_in, b_in=b_in, hg_lb_logits=hg_lb_logits, hg_norm_w=hg_norm_w, ml_conv_w=ml_conv_w,
                  ml_conv_b=ml_conv_b, ml_norm_w=ml_norm_w, w_out=w_out, ln1_g=ln1_g, ln1_b=ln1_b, ca_wq=ca_wq,
                  ca_wkv=ca_wkv, ca_wo=ca_wo, ln2_g=ln2_g, ln2_b=ln2_b, ffn_w_up=ffn_w_up, ffn_conv_w=ffn_conv_w,
                  ffn_conv_b=ffn_conv_b, ffn_w_down=ffn_w_down, ln3_g=ln3_g, ln3_b=ln3_b)
    mom1 = dict(w_in=m_w_in, b_in=m_b_in, hg_lb_logits=m_hg_lb_logits, hg_norm_w=m_hg_norm_w,
                ml_conv_w=m_ml_conv_w, ml_conv_b=m_ml_conv_b, ml_norm_w=m_ml_norm_w, w_out=m_w_out, ln1_g=m_ln1_g,
                ln1_b=m_ln1_b, ca_wq=m_ca_wq, ca_wkv=m_ca_wkv, ca_wo=m_ca_wo, ln2_g=m_ln2_g, ln2_b=m_ln2_b,
                ffn_w_up=m_ffn_w_up, ffn_conv_w=m_ffn_conv_w, ffn_conv_b=m_ffn_conv_b, ffn_w_down=m_ffn_w_down,
                ln3_g=m_ln3_g, ln3_b=m_ln3_b)
    mom2 = dict(w_in=v_w_in, b_in=v_b_in, hg_lb_logits=v_hg_lb_logits, hg_norm_w=v_hg_norm_w,
                ml_conv_w=v_ml_conv_w, ml_conv_b=v_ml_conv_b, ml_norm_w=v_ml_norm_w, w_out=v_w_out, ln1_g=v_ln1_g,
                ln1_b=v_ln1_b, ca_wq=v_ca_wq, ca_wkv=v_ca_wkv, ca_wo=v_ca_wo, ln2_g=v_ln2_g, ln2_b=v_ln2_b,
                ffn_w_up=v_ffn_w_up, ffn_conv_w=v_ffn_conv_w, ffn_conv_b=v_ffn_conv_b, ffn_w_down=v_ffn_w_down,
                ln3_g=v_ln3_g, ln3_b=v_ln3_b)

    x_idx, y_idx, c_idx = _coords()
    as_index = lambda v: jnp.reshape(v, (1,)).astype(jnp.int32)
    me = as_index(4 * x_idx + 2 * y_idx + c_idx)
    small_params = {n: params[n] for n in SMALL_NAMES}

    shards = {n: _shard_2d(n, params[n]) for n in SHARDED_NAMES}
    m_shards = {n: _shard_2d(n, mom1[n]) for n in SHARDED_NAMES}
    v_shards = {n: _shard_2d(n, mom2[n]) for n in SHARDED_NAMES}
    small_slabs = [_pack_small(params), _pack_small(mom1), _pack_small(mom2)]
    outgoing = {n: shards[n] if "conv" in n else shards[n].astype(BF16) for n in SHARDED_NAMES}
    to_send = lambda names: [outgoing[n] for n in names]
    glue = [*m_shards.values(), *v_shards.values(), *small_slabs, *shards.values(),
            *[outgoing[n] for n in SHARDED_NAMES if n not in FIRST_NAMES]]
    first = dict(zip(FIRST_NAMES, _two_level_gather(to_send(FIRST_NAMES), glue, "weights_gather_first")))
    mid_started, through = _direct_start(True, to_send(MID_NAMES), first["w_in"], "weights_gather_start_mid")
    ffn_started, through = _direct_start(True, to_send(FFN_UP_NAMES), through, "weights_gather_start_ffn_up")
    down_started, first["w_in"] = _direct_start(True, to_send(FFN_DOWN_NAMES), through,
                                                "weights_gather_start_ffn_down")

    def gathered_weights(names, started, after, tag):
        mine, lands = _direct_wait(True, started, after, "weights_gather_wait_" + tag)
        return {n: lax.dynamic_update_index_in_dim(land, own, me[0], 0) for n, own, land in zip(names, mine, lands)}

    started, own_stacks = {}, {}

    def start_group(names, tag):
        def hook(grads, through):
            own_stacks[tag] = [_owner_stack(n, grads).astype(BF16) for n in names]
            started[tag], through = _direct_start(False, own_stacks[tag], through, "grads_start_" + tag)
            return through
        return hook

    def start_small(grads, loss, through):
        started["small"], through = _direct_start(True, [_pack_small(_small_grads(grads), loss)], through,
                                                  "small_gather_start")
        return through

    loss, grad_x, grads = _local_step(
        x[0], mem[0], loss_target[0], _first_weights(first, small_params),
        lambda y: _mid_weights(gathered_weights(MID_NAMES, mid_started, y, "mid")),
        lambda x2: _ffn_up_weights(gathered_weights(FFN_UP_NAMES, ffn_started, x2, "ffn_up"), small_params),
        lambda hid: _ffn_down_weights(gathered_weights(FFN_DOWN_NAMES, down_started, hid, "ffn_down")),
        start_group(FFN_NAMES, "ffn"), start_group(MID_NAMES, "mid"), start_small, start_group(FIRST_NAMES, "last"))

    sharded_out = {}

    def update_group(names, tag, after):
        _, lands = _direct_wait(False, started[tag], after, "grads_wait_" + tag)
        for n, st, land in zip(names, own_stacks[tag], lands):
            res = _adamw_sharded(me, st, land, shards[n], m_shards[n], v_shards[n], "adamw_" + n)
            sharded_out[n] = [_shard_like(n, t, params[n]) for t in res]

    update_group(FFN_NAMES, "ffn", grad_x)
    update_group(MID_NAMES, "mid", grad_x)
    own_small, small_lands = _direct_wait(True, started["small"], grad_x, "small_gather_wait")
    small_parts = lax.dynamic_update_index_in_dim(small_lands[0], own_small[0], me[0], 0)
    small_res = _adamw_replicated(small_parts, *small_slabs)
    small_out = [_unpack_small(slab, params) for slab in small_res]
    done = [t for n in FFN_NAMES + MID_NAMES for t in sharded_out[n]]
    done += [t for small, _ in small_out for t in small.values()]
    update_group(FIRST_NAMES, "last", done)

    outs = []
    for k, (small, _) in enumerate(small_out):
        outs.extend(sharded_out[n][k] if n in sharded_out else small[n] for n in WEIGHT_NAMES)
    return (small_out[0][1], grad_x[None], *outs)
```

```python
import functools
import math

import jax
import jax.numpy as jnp
from jax import lax
from jax.experimental import pallas as pl
from jax.experimental.pallas import tpu as pltpu

F32 = jnp.float32
BF16 = jnp.bfloat16
HIGHEST = lax.Precision.HIGHEST
MESH = pl.DeviceIdType.MESH

N_DEV = 8
D_MODEL = 1024
N_MEM = 256
N_HEADS = 4
D_HEAD = 128
D_GROUP = N_HEADS * D_HEAD
CHUNK = 64
ML_CONV = 4
FFN_CONV = 3
D_FF = 2816
D_UP = 2 * D_FF
CA_HEADS = 4
CA_DH = D_MODEL // CA_HEADS
LANES = 128
SUBLANES = 8
D_IN = 8 * D_GROUP + 2 * N_HEADS
D_IN_MAIN = 8 * D_GROUP
W_IN_SHARD = D_IN // N_DEV
UP_SHARD = D_UP // N_DEV
UP_SHARD_P = 768
D_UP_P = N_DEV * UP_SHARD_P
D_FF_P = D_UP_P // 2
ALPHA = 2.0 ** 0.25
LN_EPS = 1e-5
NEG_BIG = -1e30
ADAM_LR = 0.001
ADAM_B1 = 0.9
ADAM_B2 = 0.999
ADAM_EPS = 1e-08
ADAM_WD = 0.01
ADAM_STEP = 10
VMEM_LIMIT = 56 * 1024 * 1024

SEG_HQ, SEG_HF, SEG_HI, SEG_HG, SEG_MQ, SEG_MK, SEG_MV, SEG_MO = (4 * i for i in range(8))


def _params(sem):
    return pltpu.CompilerParams(dimension_semantics=sem, vmem_limit_bytes=VMEM_LIMIT)


def _dg(a, b, ca, cb, precision=None):
    return lax.dot_general(a, b, (((ca,), (cb,)), ((), ())), precision=precision,
                           preferred_element_type=F32)


def _nn_raw(a, b):
    return _dg(a.astype(BF16), b.astype(BF16), 1, 0)


def _nt_raw(a, b):
    return _dg(a.astype(BF16), b.astype(BF16), 1, 1)


def _tn_raw(a, b):
    return _dg(a.astype(BF16), b.astype(BF16), 0, 0)


@jax.custom_vjp
def _nn(a, b):
    return _nn_raw(a, b)


_nn.defvjp(lambda a, b: (_nn_raw(a, b), (a, b)),
           lambda res, g: (_nt_raw(g, res[1]), _tn_raw(res[0], g)))


@jax.custom_vjp
def _nt(a, b):
    return _nt_raw(a, b)


_nt.defvjp(lambda a, b: (_nt_raw(a, b), (a, b)),
           lambda res, g: (_nn_raw(g, res[1]), _tn_raw(g, res[0])))


@jax.custom_vjp
def _tn(a, b):
    return _tn_raw(a, b)


_tn.defvjp(lambda a, b: (_tn_raw(a, b), (a, b)),
           lambda res, g: (_nt_raw(res[1], g), _nn_raw(res[0], g)))


def _layer_norm(z, g, b):
    mu = jnp.mean(z, axis=-1, keepdims=True)
    var = jnp.mean(jnp.square(z - mu), axis=-1, keepdims=True)
    return (z - mu) * lax.rsqrt(var + LN_EPS) * g + b


def _matmul_nn(a, w, bias, tm, tn, name, out_dtype=F32):
    m, k = a.shape
    if w.ndim == 3:
        n = w.shape[0] * w.shape[2]
        assert tn == w.shape[2]
        w_spec = pl.BlockSpec((None, k, tn), lambda i, j: (j, 0, 0))
    else:
        n = w.shape[1]
        w_spec = pl.BlockSpec((k, tn), lambda i, j: (0, j))

    def body(*refs):
        a_ref, w_ref = refs[0], refs[1]
        o_ref = refs[-1]
        acc = _nn_raw(a_ref[...], w_ref[...])
        if bias is not None:
            acc = acc + refs[2][...]
        o_ref[...] = acc.astype(o_ref.dtype)

    in_specs = [pl.BlockSpec((tm, k), lambda i, j: (i, 0)), w_spec]
    args = [a, w]
    if bias is not None:
        in_specs.append(pl.BlockSpec((1, tn), lambda i, j: (0, j)))
        args.append(bias)
    return pl.pallas_call(
        body, name=name, grid=(m // tm, n // tn), in_specs=in_specs,
        out_specs=pl.BlockSpec((tm, tn), lambda i, j: (i, j)),
        out_shape=jax.ShapeDtypeStruct((m, n), out_dtype),
        compiler_params=_params(("parallel", "parallel")),
    )(*args)


def _matmul_nt(pairs, add, scale, tm, tk, name, out_dtype=F32):
    m = pairs[0][0].shape[0]
    k = pairs[0][1].shape[-2]
    groups = []
    in_specs, args = [], []
    for pair in pairs:
        d, w = pair[0], pair[1]
        in_specs.append(pl.BlockSpec((tm, d.shape[1]), lambda i, j: (i, 0)))
        if w.ndim == 3:
            g = d.shape[1] // w.shape[2]
            blk = pair[2] // g
            in_specs.append(pl.BlockSpec((g, tk, w.shape[2]), lambda i, j, blk=blk: (blk, j, 0)))
            groups.append((g, w.shape[2]))
        else:
            in_specs.append(pl.BlockSpec((tk, w.shape[1]), lambda i, j: (j, 0)))
            groups.append(None)
        args += [d, w]
    if add is not None:
        in_specs.append(pl.BlockSpec((tm, tk), lambda i, j: (i, j)))
        args.append(add)

    def body(*refs):
        o_ref = refs[-1]
        acc = None
        for p, grp in enumerate(groups):
            d_ref, w_ref = refs[2 * p], refs[2 * p + 1]
            if grp is None:
                terms = [_nt_raw(d_ref[...], w_ref[...])]
            else:
                terms = [_nt_raw(d_ref[:, g * grp[1]:(g + 1) * grp[1]], w_ref[g]) for g in range(grp[0])]
            for t in terms:
                acc = t if acc is None else acc + t
        if add is not None:
            acc = acc + scale * refs[2 * len(groups)][...]
        o_ref[...] = acc.astype(o_ref.dtype)

    return pl.pallas_call(
        body, name=name, grid=(m // tm, k // tk), in_specs=in_specs,
        out_specs=pl.BlockSpec((tm, tk), lambda i, j: (i, j)),
        out_shape=jax.ShapeDtypeStruct((m, k), out_dtype),
        compiler_params=_params(("parallel", "parallel")),
    )(*args)


def _matmul_tn(a, b, tm, tn, tt, name, shards=None, shard0=0, group=1, into=None, colsum=False):
    t, m = a.shape
    n = b.shape[1]
    assert not colsum or tm == m
    n_in = 2 + (into is not None)
    out_dtype = BF16
    per_step = 1 if shards is None else group
    width = per_step * tn

    def body(*refs):
        a_ref, b_ref = refs[0], refs[1]
        o_ref, acc_ref = refs[n_in], refs[-1]
        first = pl.program_id(2) == 0

        @pl.when(first)
        def _():
            acc_ref[...] = jnp.zeros_like(acc_ref)

        if shards is None:
            acc_ref[...] += _tn_raw(a_ref[...], b_ref[...])
        else:
            lhs = a_ref[...].astype(BF16)
            for g in range(per_step):
                acc_ref[g] += _tn_raw(lhs, b_ref[:, g * tn:(g + 1) * tn])

        @pl.when(pl.program_id(2) == t // tt - 1)
        def _():
            o_ref[...] = acc_ref[...].astype(o_ref.dtype)

        if colsum:
            s_ref = refs[n_in + 1]

            @pl.when(first)
            def _():
                s_ref[...] = jnp.zeros_like(s_ref)

            s_ref[...] += jnp.sum(b_ref[...], axis=0, keepdims=True)

    in_specs = [pl.BlockSpec((tt, tm), lambda i, j, kk: (kk, i)),
                pl.BlockSpec((tt, width), lambda i, j, kk: (kk, j))]
    args = [a, b]
    aliases = {}
    if into is not None:
        in_specs.append(pl.BlockSpec(memory_space=pl.ANY))
        args.append(into)
        aliases = {2: 0}
    if shards is None:
        out_specs = [pl.BlockSpec((tm, tn), lambda i, j, kk: (i, j))]
        out_shape = [jax.ShapeDtypeStruct((m, n), out_dtype)]
        acc = pltpu.VMEM((tm, tn), F32)
    else:
        out_specs = [pl.BlockSpec((per_step, tm, tn), lambda i, j, kk: (shard0 // per_step + j, i, 0))]
        out_shape = [jax.ShapeDtypeStruct((shards, m, tn), out_dtype)]
        acc = pltpu.VMEM((per_step, tm, tn), F32)
    if colsum:
        out_specs.append(pl.BlockSpec((1, tn), lambda i, j, kk: (0, j)))
        out_shape.append(jax.ShapeDtypeStruct((1, n), F32))
    res = pl.pallas_call(
        body, name=name, grid=(m // tm, n // width, t // tt), in_specs=in_specs, out_specs=out_specs,
        out_shape=out_shape, input_output_aliases=aliases, scratch_shapes=[acc],
        compiler_params=_params(("parallel", "parallel", "arbitrary")),
    )(*args)
    return res if colsum else res[0]


ROW_TILE = 64


def _stack(ref, start, rows):
    return ref[pl.ds(start, rows), :].astype(F32).reshape(rows // SUBLANES, SUBLANES, LANES)


def _vreg_rows(ref, n):
    return [jnp.broadcast_to(ref[j:j + 1, :], (SUBLANES, LANES))[None] for j in range(n)]


def _column_total(acc):
    return jnp.sum(acc, axis=0, keepdims=True)


def _conv_fwd_tile(pad_ref, taps_w, bias, r0, rows):
    taps = len(taps_w)
    acc = bias
    for j in range(taps):
        acc = acc + _stack(pad_ref, SUBLANES - (taps - 1 - j) + r0, rows) * taps_w[j]
    return acc


def _conv_grads_tile(pad_ref, dpad_ref, dx_ref, taps_w, dws, r0, rows):
    taps = len(taps_w)
    x_rows = _stack(pad_ref, SUBLANES + r0, rows)
    dx = None
    for j in range(taps):
        d_shifted = _stack(dpad_ref, r0 + (taps - 1 - j), rows)
        term = d_shifted * taps_w[j]
        dx = term if dx is None else dx + term
        dws[j] = dws[j] + jnp.sum(d_shifted * x_rows, axis=0)
    dx_ref[r0:r0 + rows, :] = dx.reshape(rows, LANES).astype(dx_ref.dtype)
    return jnp.sum(dx, axis=0)


def _ml_conv_fwd(proj, conv_w, conv_b):
    s = proj.shape[0]
    nblk = 2 * D_GROUP // LANES

    def body(x_ref, w_ref, b_ref, o_ref, pad_ref):
        pad_ref[0:SUBLANES, :] = jnp.zeros((SUBLANES, LANES), F32)
        pad_ref[SUBLANES:, :] = x_ref[...].astype(F32)
        taps_w, bias = _vreg_rows(w_ref, ML_CONV), _vreg_rows(b_ref, 1)[0]
        for r0 in range(0, s, ROW_TILE):
            rows = min(ROW_TILE, s - r0)
            o_ref[r0:r0 + rows, :] = jax.nn.silu(_conv_fwd_tile(pad_ref, taps_w, bias, r0, rows)).reshape(rows, LANES)

    return pl.pallas_call(
        body, name="ml_conv_fwd", grid=(nblk,),
        in_specs=[pl.BlockSpec((s, LANES), lambda j: (0, SEG_MQ + j)),
                  pl.BlockSpec((ML_CONV, LANES), lambda j: (0, j)),
                  pl.BlockSpec((1, LANES), lambda j: (0, j))],
        out_specs=pl.BlockSpec((s, LANES), lambda j: (0, j)),
        out_shape=jax.ShapeDtypeStruct((s, 2 * D_GROUP), F32),
        scratch_shapes=[pltpu.VMEM((s + SUBLANES, LANES), F32)],
        compiler_params=_params(("parallel",)),
    )(proj, conv_w, conv_b)


def _ml_conv_bwd(proj, conv_w, conv_b, d_qk, d_proj):
    s = proj.shape[0]
    nblk = 2 * D_GROUP // LANES

    def body(x_ref, w_ref, b_ref, dy_ref, _, dx_ref, dw_ref, db_ref, dxs_ref, pad_ref, dpad_ref):
        pad_ref[0:SUBLANES, :] = jnp.zeros((SUBLANES, LANES), F32)
        pad_ref[SUBLANES:, :] = x_ref[...].astype(F32)
        dpad_ref[s:, :] = jnp.zeros((SUBLANES, LANES), F32)
        taps_w, bias = _vreg_rows(w_ref, ML_CONV), _vreg_rows(b_ref, 1)[0]
        db = jnp.zeros((SUBLANES, LANES), F32)
        for r0 in range(0, s, ROW_TILE):
            rows = min(ROW_TILE, s - r0)
            pre = _conv_fwd_tile(pad_ref, taps_w, bias, r0, rows)
            _, vjp = jax.vjp(jax.nn.silu, pre)
            d_pre, = vjp(_stack(dy_ref, r0, rows))
            dpad_ref[r0:r0 + rows, :] = d_pre.reshape(rows, LANES)
            db = db + jnp.sum(d_pre, axis=0)
        db_ref[...] = _column_total(db)
        dws = [jnp.zeros((SUBLANES, LANES), F32) for _ in range(ML_CONV)]
        dx_sum = jnp.zeros((SUBLANES, LANES), F32)
        for r0 in range(0, s, ROW_TILE):
            dx_sum = dx_sum + _conv_grads_tile(pad_ref, dpad_ref, dx_ref, taps_w, dws, r0, min(ROW_TILE, s - r0))
        dxs_ref[...] = _column_total(dx_sum)
        for j in range(ML_CONV):
            dw_ref[j:j + 1, :] = _column_total(dws[j])

    return pl.pallas_call(
        body, name="ml_conv_bwd", grid=(nblk,),
        in_specs=[pl.BlockSpec((s, LANES), lambda j: (0, SEG_MQ + j)),
                  pl.BlockSpec((ML_CONV, LANES), lambda j: (0, j)),
                  pl.BlockSpec((1, LANES), lambda j: (0, j)),
                  pl.BlockSpec((s, LANES), lambda j: (0, j)),
                  pl.BlockSpec(memory_space=pl.ANY)],
        out_specs=[pl.BlockSpec((s, LANES), lambda j: (0, SEG_MQ + j)),
                   pl.BlockSpec((ML_CONV, LANES), lambda j: (0, j)),
                   pl.BlockSpec((1, LANES), lambda j: (0, j)),
                   pl.BlockSpec((1, LANES), lambda j: (0, j))],
        out_shape=[jax.ShapeDtypeStruct(d_proj.shape, d_proj.dtype),
                   jax.ShapeDtypeStruct((ML_CONV, 2 * D_GROUP), F32),
                   jax.ShapeDtypeStruct((1, 2 * D_GROUP), F32),
                   jax.ShapeDtypeStruct((1, 2 * D_GROUP), F32)],
        input_output_aliases={4: 0},
        scratch_shapes=[pltpu.VMEM((s + SUBLANES, LANES), F32), pltpu.VMEM((s + SUBLANES, LANES), F32)],
        compiler_params=_params(("parallel",)),
    )(proj, conv_w, conv_b, d_qk, d_proj)


def _gelu_mul(a, b):
    return jax.nn.gelu(a) * b


GELU_C = math.sqrt(2.0 / math.pi)
GELU_K = 0.044715


def _gelu_mul_grads(a, b, d):
    a2 = a * a
    t = jnp.tanh(GELU_C * (a + GELU_K * (a * a2)))
    cdf = 0.5 * (1.0 + t)
    slope = cdf + (0.5 * GELU_C) * a * (1.0 - t * t) * (1.0 + (3.0 * GELU_K) * a2)
    return d * b * slope, d * (a * cdf)


FFN_BLOCKS = D_FF_P // LANES


def _ffn_conv_fwd(u, conv_w, conv_b):
    s = u.shape[0]

    def body(g_ref, v_ref, wg_ref, wv_ref, bg_ref, bv_ref, o_ref, gpad_ref, vpad_ref):
        for pad_ref, x_ref in ((gpad_ref, g_ref), (vpad_ref, v_ref)):
            pad_ref[0:SUBLANES, :] = jnp.zeros((SUBLANES, LANES), F32)
            pad_ref[SUBLANES:, :] = x_ref[...].astype(F32)
        taps_g, bias_g = _vreg_rows(wg_ref, FFN_CONV), _vreg_rows(bg_ref, 1)[0]
        taps_v, bias_v = _vreg_rows(wv_ref, FFN_CONV), _vreg_rows(bv_ref, 1)[0]
        for r0 in range(0, s, ROW_TILE):
            rows = min(ROW_TILE, s - r0)
            ug = _conv_fwd_tile(gpad_ref, taps_g, bias_g, r0, rows)
            uv = _conv_fwd_tile(vpad_ref, taps_v, bias_v, r0, rows)
            o_ref[r0:r0 + rows, :] = _gelu_mul(ug, uv).reshape(rows, LANES).astype(o_ref.dtype)

    col = lambda off: (lambda j: (0, off + j))
    return pl.pallas_call(
        body, name="ffn_conv_fwd", grid=(FFN_BLOCKS,),
        in_specs=[pl.BlockSpec((s, LANES), col(0)), pl.BlockSpec((s, LANES), col(FFN_BLOCKS)),
                  pl.BlockSpec((FFN_CONV, LANES), col(0)), pl.BlockSpec((FFN_CONV, LANES), col(FFN_BLOCKS)),
                  pl.BlockSpec((1, LANES), col(0)), pl.BlockSpec((1, LANES), col(FFN_BLOCKS))],
        out_specs=pl.BlockSpec((s, LANES), col(0)),
        out_shape=jax.ShapeDtypeStruct((s, D_FF_P), BF16),
        scratch_shapes=[pltpu.VMEM((s + SUBLANES, LANES), F32), pltpu.VMEM((s + SUBLANES, LANES), F32)],
        compiler_params=_params(("parallel",)),
    )(u, u, conv_w, conv_w, conv_b, conv_b)


def _ffn_conv_bwd(u, conv_w, conv_b, d_h):
    s = u.shape[0]

    def body(g_ref, v_ref, wg_ref, wv_ref, bg_ref, bv_ref, dh_ref,
             dug_ref, duv_ref, dwg_ref, dwv_ref, dbg_ref, dbv_ref,
             gpad_ref, vpad_ref, dgpad_ref, dvpad_ref):
        for pad_ref, x_ref in ((gpad_ref, g_ref), (vpad_ref, v_ref)):
            pad_ref[0:SUBLANES, :] = jnp.zeros((SUBLANES, LANES), F32)
            pad_ref[SUBLANES:, :] = x_ref[...].astype(F32)
        dgpad_ref[s:, :] = jnp.zeros((SUBLANES, LANES), F32)
        dvpad_ref[s:, :] = jnp.zeros((SUBLANES, LANES), F32)
        taps_g, bias_g = _vreg_rows(wg_ref, FFN_CONV), _vreg_rows(bg_ref, 1)[0]
        taps_v, bias_v = _vreg_rows(wv_ref, FFN_CONV), _vreg_rows(bv_ref, 1)[0]
        dbg = jnp.zeros((SUBLANES, LANES), F32)
        dbv = jnp.zeros((SUBLANES, LANES), F32)
        for r0 in range(0, s, ROW_TILE):
            rows = min(ROW_TILE, s - r0)
            ug = _conv_fwd_tile(gpad_ref, taps_g, bias_g, r0, rows)
            uv = _conv_fwd_tile(vpad_ref, taps_v, bias_v, r0, rows)
            d_ug, d_uv = _gelu_mul_grads(ug, uv, _stack(dh_ref, r0, rows))
            dgpad_ref[r0:r0 + rows, :] = d_ug.reshape(rows, LANES)
            dvpad_ref[r0:r0 + rows, :] = d_uv.reshape(rows, LANES)
            dbg = dbg + jnp.sum(d_ug, axis=0)
            dbv = dbv + jnp.sum(d_uv, axis=0)
        dbg_ref[...] = _column_total(dbg)
        dbv_ref[...] = _column_total(dbv)
        for pad_ref, dpad_ref, taps_w, dx_ref, dw_ref in ((gpad_ref, dgpad_ref, taps_g, dug_ref, dwg_ref),
                                                          (vpad_ref, dvpad_ref, taps_v, duv_ref, dwv_ref)):
            dws = [jnp.zeros((SUBLANES, LANES), F32) for _ in range(FFN_CONV)]
            for r0 in range(0, s, ROW_TILE):
                _conv_grads_tile(pad_ref, dpad_ref, dx_ref, taps_w, dws, r0, min(ROW_TILE, s - r0))
            for j in range(FFN_CONV):
                dw_ref[j:j + 1, :] = _column_total(dws[j])

    col = lambda off: (lambda j: (0, off + j))
    seq = pl.BlockSpec((s, LANES), col(0))
    return pl.pallas_call(
        body, name="ffn_conv_bwd", grid=(FFN_BLOCKS,),
        in_specs=[pl.BlockSpec((s, LANES), col(0)), pl.BlockSpec((s, LANES), col(FFN_BLOCKS)),
                  pl.BlockSpec((FFN_CONV, LANES), col(0)), pl.BlockSpec((FFN_CONV, LANES), col(FFN_BLOCKS)),
                  pl.BlockSpec((1, LANES), col(0)), pl.BlockSpec((1, LANES), col(FFN_BLOCKS)), seq],
        out_specs=[seq, seq, pl.BlockSpec((FFN_CONV, LANES), col(0)), pl.BlockSpec((FFN_CONV, LANES), col(0)),
                   pl.BlockSpec((1, LANES), col(0)), pl.BlockSpec((1, LANES), col(0))],
        out_shape=[jax.ShapeDtypeStruct((s, D_FF_P), BF16), jax.ShapeDtypeStruct((s, D_FF_P), BF16),
                   jax.ShapeDtypeStruct((FFN_CONV, D_FF_P), F32), jax.ShapeDtypeStruct((FFN_CONV, D_FF_P), F32),
                   jax.ShapeDtypeStruct((1, D_FF_P), F32), jax.ShapeDtypeStruct((1, D_FF_P), F32)],
        scratch_shapes=[pltpu.VMEM((s + SUBLANES, LANES), F32) for _ in range(4)],
        compiler_params=_params(("parallel",)),
    )(u, u, conv_w, conv_w, conv_b, conv_b, d_h)


def _chunk_masks(c):
    row = lax.broadcasted_iota(jnp.int32, (c, c), 0)
    col = lax.broadcasted_iota(jnp.int32, (c, c), 1)
    return row, col


@jax.custom_vjp
def _split_heads(x):
    return tuple(x[:, h * D_HEAD:(h + 1) * D_HEAD] for h in range(N_HEADS))


_split_heads.defvjp(lambda x: (_split_heads(x), None), lambda _, gs: (jnp.concatenate(gs, axis=1),))


@jax.custom_vjp
def _merge_heads(xs):
    return jnp.concatenate(xs, axis=1)


_merge_heads.defvjp(lambda xs: (_merge_heads(xs), None), lambda _, g: (_split_heads(g),))


@jax.custom_vjp
def _split_chunks(x):
    return tuple(x[i * CHUNK:(i + 1) * CHUNK] for i in range(x.shape[0] // CHUNK))


_split_chunks.defvjp(lambda x: (_split_chunks(x), None), lambda _, gs: (jnp.concatenate(gs, axis=0),))


@jax.custom_vjp
def _merge_chunks(xs):
    return jnp.concatenate(xs, axis=0)


_merge_chunks.defvjp(lambda xs: (_merge_chunks(xs), None), lambda _, g: (_split_chunks(g),))


def _blocks(x):
    return [_split_heads(rows) for rows in _split_chunks(x)]


def _per_chunk_rows(per_chunk, rid):
    out = per_chunk[0]
    for i in range(1, len(per_chunk)):
        out = jnp.where(rid >= i * CHUNK, per_chunk[i], out)
    return out


HEADS = range(N_HEADS)
CHUNKS_PER_STEP = 8
ML_CHUNKS_PER_STEP = 1


def _hg_chunk(hq, hf, hi, hgate, l0, l1, nw, sts):
    n = hq.shape[0] // CHUNK
    causal = _chunk_masks(CHUNK)
    causal = causal[1] <= causal[0]
    mx = lax.stop_gradient(jnp.maximum(l0, l1))
    e0 = jnp.exp(l0 - mx)
    e1 = jnp.exp(l1 - mx)
    lb = e0 / (e0 + e1)
    sig = jax.nn.sigmoid(hf)
    lf = jnp.log(lb + (1.0 - lb) * sig)
    k = (1.0 - lb) * jax.nn.sigmoid(-hf)
    q = jax.nn.silu(hq)
    tri = causal.astype(F32)
    b = _merge_chunks(tuple(_dg(tri, rows, 1, 0, HIGHEST) for rows in _split_chunks(lf)))
    rid = lax.broadcasted_iota(jnp.int32, b.shape, 0)
    pick = lambda r: jnp.sum(jnp.where(rid == r, b, 0.0), axis=0, keepdims=True)
    b_last_c = [pick(i * CHUNK + CHUNK - 1) for i in range(n)]
    b_ref = _per_chunk_rows([pick(i * CHUNK + CHUNK // 2 - 1) for i in range(n)], rid)
    b_last = _per_chunk_rows(b_last_c, rid)
    qa = _blocks(q * jnp.exp(b - b_ref))
    ka = _blocks(k * jnp.exp(b_ref - b))
    qe = _blocks(q * jnp.exp(b))
    kd = _blocks(k * jnp.exp(b_last - b))
    decay = [_split_heads(jnp.exp(b_last_c[i])) for i in range(n)]
    v = _blocks(hi)
    chunks = range(n)
    attn = [[jnp.where(causal, _nt(qa[i][h], ka[i][h]), 0.0) for h in HEADS] for i in chunks]
    intra = [[_nn(attn[i][h], v[i][h]) for h in HEADS] for i in chunks]
    kv = [[_tn(v[i][h], kd[i][h]) for h in HEADS] for i in chunks]
    normed = []
    for i in chunks:
        inter = [_nt(qe[i][h], sts[h]) for h in HEADS]
        sts = tuple(decay[i][h] * sts[h] + kv[i][h] for h in HEADS)
        o = [intra[i][h] + inter[h] for h in HEADS]
        normed.append(_merge_heads(tuple(o[h] * lax.rsqrt(jnp.mean(o[h] * o[h], axis=-1, keepdims=True) + LN_EPS)
                                         for h in HEADS)))
    return _merge_chunks(tuple(normed)) * nw * jax.nn.silu(hgate), sts


def _seg(ref, seg):
    return ref[:, seg * D_GROUP:(seg + 1) * D_GROUP]


def _hgrn2_fwd(proj, logits, norm_w):
    s = proj.shape[0]
    rows = CHUNKS_PER_STEP * CHUNK
    nc = s // rows

    def body(p_ref, lg_ref, nw_ref, y_ref, st_out_ref, st_scr):
        @pl.when(pl.program_id(0) == 0)
        def _():
            st_scr[...] = jnp.zeros_like(st_scr)

        sts = tuple(st_scr[h] for h in HEADS)
        y, sts_new = _hg_chunk(_seg(p_ref, 0), _seg(p_ref, 1), _seg(p_ref, 2), _seg(p_ref, 3),
                               lg_ref[0:1, :], lg_ref[1:2, :], nw_ref[...], sts)
        y_ref[...] = y.astype(y_ref.dtype)
        for h in HEADS:
            st_out_ref[h] = sts[h]
            st_scr[h] = sts_new[h]

    return pl.pallas_call(
        body, name="hgrn2_fwd", grid=(nc,),
        in_specs=[pl.BlockSpec((rows, 4 * D_GROUP), lambda c: (c, 0)),
                  pl.BlockSpec((2, D_GROUP), lambda c: (0, 0)),
                  pl.BlockSpec((1, D_GROUP), lambda c: (0, 0))],
        out_specs=[pl.BlockSpec((rows, D_GROUP), lambda c: (c, 0)),
                   pl.BlockSpec((None, N_HEADS, D_HEAD, D_HEAD), lambda c: (c, 0, 0, 0))],
        out_shape=[jax.ShapeDtypeStruct((s, 2 * D_GROUP), BF16),
                   jax.ShapeDtypeStruct((nc, N_HEADS, D_HEAD, D_HEAD), F32)],
        scratch_shapes=[pltpu.VMEM((N_HEADS, D_HEAD, D_HEAD), F32)],
        compiler_params=_params(("arbitrary",)),
    )(proj, logits, norm_w)


def _hgrn2_bwd(proj, logits, norm_w, states, d_y):
    s = proj.shape[0]
    rows = CHUNKS_PER_STEP * CHUNK
    nc = s // rows

    def body(p_ref, lg_ref, nw_ref, st_ref, dy_ref, dp_ref, dl_ref, dnw_ref, dsum_ref, dst_scr):
        @pl.when(pl.program_id(0) == 0)
        def _():
            dst_scr[...] = jnp.zeros_like(dst_scr)
            dl_ref[...] = jnp.zeros_like(dl_ref)
            dnw_ref[...] = jnp.zeros_like(dnw_ref)
            dsum_ref[...] = jnp.zeros_like(dsum_ref)

        _, vjp = jax.vjp(_hg_chunk, _seg(p_ref, 0), _seg(p_ref, 1), _seg(p_ref, 2), _seg(p_ref, 3),
                         lg_ref[0:1, :], lg_ref[1:2, :], nw_ref[...], tuple(st_ref[h] for h in HEADS))
        d_hq, d_hf, d_hi, d_hg, d_l0, d_l1, d_nw, d_sts = vjp((dy_ref[...], tuple(dst_scr[h] for h in HEADS)))
        for seg, val in enumerate((d_hq, d_hf, d_hi, d_hg)):
            dp_ref[:, seg * D_GROUP:(seg + 1) * D_GROUP] = val.astype(dp_ref.dtype)
            dsum_ref[:, seg * D_GROUP:(seg + 1) * D_GROUP] += jnp.sum(val, axis=0, keepdims=True)
        dl_ref[0:1, :] += d_l0
        dl_ref[1:2, :] += d_l1
        dnw_ref[...] += d_nw
        for h in HEADS:
            dst_scr[h] = d_sts[h]

    rev = lambda c: nc - 1 - c
    return pl.pallas_call(
        body, name="hgrn2_bwd", grid=(nc,),
        in_specs=[pl.BlockSpec((rows, 4 * D_GROUP), lambda c: (rev(c), 0)),
                  pl.BlockSpec((2, D_GROUP), lambda c: (0, 0)),
                  pl.BlockSpec((1, D_GROUP), lambda c: (0, 0)),
                  pl.BlockSpec((None, N_HEADS, D_HEAD, D_HEAD), lambda c: (rev(c), 0, 0, 0)),
                  pl.BlockSpec((rows, D_GROUP), lambda c: (rev(c), 0))],
        out_specs=[pl.BlockSpec((rows, 4 * D_GROUP), lambda c: (rev(c), 0)),
                   pl.BlockSpec((2, D_GROUP), lambda c: (0, 0)),
                   pl.BlockSpec((1, D_GROUP), lambda c: (0, 0)),
                   pl.BlockSpec((1, 4 * D_GROUP), lambda c: (0, 0))],
        out_shape=[jax.ShapeDtypeStruct((s, D_IN_MAIN), BF16), jax.ShapeDtypeStruct((2, D_GROUP), F32),
                   jax.ShapeDtypeStruct((1, D_GROUP), F32), jax.ShapeDtypeStruct((1, 4 * D_GROUP), F32)],
        scratch_shapes=[pltpu.VMEM((N_HEADS, D_HEAD, D_HEAD), F32)],
        compiler_params=_params(("arbitrary",)),
    )(proj, logits, norm_w, states, d_y)


def _gate_column(gates, lane, idx):
    return jnp.sum(jnp.where(lane == idx, gates, 0.0), axis=1, keepdims=True)


def _head_layer_norm(h):
    mu = jnp.mean(h, axis=-1, keepdims=True)
    var = jnp.mean(jnp.square(h - mu), axis=-1, keepdims=True)
    return (h - mu) * lax.rsqrt(var + LN_EPS)


def _ml_chunk(qc, kc, v, mo, gates, nw, cts, ns, ms):
    n = qc.shape[0] // CHUNK
    row, col = _chunk_masks(CHUNK)
    mask = col <= row
    eye = col == row
    to_row = lambda t: jnp.sum(jnp.where(eye, t, 0.0), axis=0, keepdims=True)
    q = _blocks(qc * (D_HEAD ** -0.5))
    k = _blocks(kc)
    vs = _blocks(v)
    gate_rows = _split_chunks(gates)
    lane = lax.broadcasted_iota(jnp.int32, gate_rows[0].shape, 1)
    each = [(i, h) for i in range(n) for h in HEADS]
    on_each = lambda f: {ih: f(*ih) for ih in each}
    ig = on_each(lambda i, h: _gate_column(gate_rows[i], lane, h))
    lf = on_each(lambda i, h: jax.nn.log_sigmoid(_gate_column(gate_rows[i], lane, N_HEADS + h)))
    lf_row = on_each(lambda i, h: to_row(lf[i, h]))
    ig_row = on_each(lambda i, h: to_row(ig[i, h]))
    b_col = on_each(lambda i, h: jnp.sum(jnp.where(mask, lf_row[i, h], 0.0), axis=1, keepdims=True))
    b_row = on_each(lambda i, h: jnp.sum(jnp.where(row <= col, lf[i, h], 0.0), axis=0, keepdims=True))
    g = on_each(lambda i, h: jnp.sum(lf[i, h], axis=0, keepdims=True))
    d = on_each(lambda i, h: jnp.where(mask, b_col[i, h] - b_row[i, h] + ig_row[i, h], -jnp.inf))
    a = on_each(lambda i, h: g[i, h] - b_col[i, h] + ig[i, h])
    m_at = {(0, h): ms[h] for h in HEADS}
    for i, h in each:
        m_at[i + 1, h] = lax.stop_gradient(jnp.maximum(g[i, h] + m_at[i, h], jnp.max(a[i, h], axis=0, keepdims=True)))
    inter = on_each(lambda i, h: b_col[i, h] + m_at[i, h])
    m_t = on_each(lambda i, h: lax.stop_gradient(jnp.maximum(inter[i, h], jnp.max(d[i, h], axis=1, keepdims=True))))
    qk = on_each(lambda i, h: _nt(q[i][h], k[i][h]))
    sc = on_each(lambda i, h: qk[i, h] * jnp.exp(d[i, h] - m_t[i, h]))
    w_inter = on_each(lambda i, h: jnp.exp(inter[i, h] - m_t[i, h]))
    sv = on_each(lambda i, h: _nn(sc[i, h], vs[i][h]))
    decay = on_each(lambda i, h: jnp.exp(g[i, h] + m_at[i, h] - m_at[i + 1, h]))
    wk = on_each(lambda i, h: k[i][h] * jnp.exp(a[i, h] - m_at[i + 1, h]))
    kv = on_each(lambda i, h: _tn(vs[i][h], wk[i, h]))
    normed = []
    for i in range(n):
        qc_state = [_nt(q[i][h], cts[h]) for h in HEADS]
        num = [sv[i, h] + w_inter[i, h] * qc_state[h] for h in HEADS]
        den = [jnp.sum(sc[i, h], axis=1, keepdims=True)
               + w_inter[i, h] * jnp.sum(q[i][h] * ns[h], axis=1, keepdims=True) for h in HEADS]
        hh = [num[h] / jnp.maximum(jnp.abs(den[h]), jnp.exp(-m_t[i, h])) for h in HEADS]
        cts = tuple(decay[i, h] * cts[h] + kv[i, h] for h in HEADS)
        ns = tuple(decay[i, h] * ns[h] + jnp.sum(wk[i, h], axis=0, keepdims=True) for h in HEADS)
        normed.append(_merge_heads(tuple(_head_layer_norm(hh[h]) for h in HEADS)))
    y = jax.nn.sigmoid(mo) * (_merge_chunks(tuple(normed)) * nw)
    return y, cts, ns, tuple(m_at[n, h] for h in HEADS)


def _mlstm_fwd(qk, proj, gates, norm_w, y):
    s = proj.shape[0]
    rows = ML_CHUNKS_PER_STEP * CHUNK
    nc = s // rows

    def body(qk_ref, vo_ref, g_ref, nw_ref, _, y_ref, ct_out, n_out, m_out, ct_scr, n_scr, m_scr):
        @pl.when(pl.program_id(0) == 0)
        def _():
            ct_scr[...] = jnp.zeros_like(ct_scr)
            n_scr[...] = jnp.zeros_like(n_scr)
            m_scr[...] = jnp.full(m_scr.shape, NEG_BIG, F32)

        cts = tuple(ct_scr[h] for h in HEADS)
        ns = tuple(n_scr[h] for h in HEADS)
        ms = tuple(m_scr[h] for h in HEADS)
        y, cts_new, ns_new, ms_new = _ml_chunk(_seg(qk_ref, 0), _seg(qk_ref, 1), _seg(vo_ref, 0), _seg(vo_ref, 1),
                                               g_ref[...], nw_ref[...], cts, ns, ms)
        y_ref[...] = y.astype(y_ref.dtype)
        for h in HEADS:
            ct_out[h], n_out[h], m_out[h] = cts[h], ns[h], ms[h]
            ct_scr[h], n_scr[h], m_scr[h] = cts_new[h], ns_new[h], ms_new[h]

    st = lambda r, w: pl.BlockSpec((None, N_HEADS, r, w), lambda c: (c, 0, 0, 0))
    return pl.pallas_call(
        body, name="mlstm_fwd", grid=(nc,),
        in_specs=[pl.BlockSpec((rows, 2 * D_GROUP), lambda c: (c, 0)),
                  pl.BlockSpec((rows, 2 * D_GROUP), lambda c: (c, 3)),
                  pl.BlockSpec((rows, LANES), lambda c: (c, 0)),
                  pl.BlockSpec((1, D_GROUP), lambda c: (0, 0)),
                  pl.BlockSpec(memory_space=pl.ANY)],
        out_specs=[pl.BlockSpec((rows, D_GROUP), lambda c: (c, 1)),
                   st(D_HEAD, D_HEAD), st(1, D_HEAD), st(1, 1)],
        out_shape=[jax.ShapeDtypeStruct(y.shape, y.dtype),
                   jax.ShapeDtypeStruct((nc, N_HEADS, D_HEAD, D_HEAD), F32),
                   jax.ShapeDtypeStruct((nc, N_HEADS, 1, D_HEAD), F32),
                   jax.ShapeDtypeStruct((nc, N_HEADS, 1, 1), F32)],
        input_output_aliases={4: 0},
        scratch_shapes=[pltpu.VMEM((N_HEADS, D_HEAD, D_HEAD), F32), pltpu.VMEM((N_HEADS, 1, D_HEAD), F32),
                        pltpu.VMEM((N_HEADS, 1, 1), F32)],
        compiler_params=_params(("arbitrary",)),
    )(qk, proj, gates, norm_w, y)


def _mlstm_bwd(qk, proj, gates, norm_w, ct_s, n_s, m_s, d_y, d_proj):
    s = proj.shape[0]
    rows = ML_CHUNKS_PER_STEP * CHUNK
    nc = s // rows

    def body(qk_ref, vo_ref, g_ref, nw_ref, ct_ref, n_ref, m_ref, dy_ref, _,
             dp_ref, dqk_ref, dg_ref, dnw_ref, dsum_ref, dct_scr, dn_scr):
        @pl.when(pl.program_id(0) == 0)
        def _():
            dct_scr[...] = jnp.zeros_like(dct_scr)
            dn_scr[...] = jnp.zeros_like(dn_scr)
            dnw_ref[...] = jnp.zeros_like(dnw_ref)
            dsum_ref[...] = jnp.zeros_like(dsum_ref)

        ms = tuple(m_ref[h] for h in HEADS)
        step = lambda *a: _ml_chunk(*a, ms)[:3]
        _, vjp = jax.vjp(step, _seg(qk_ref, 0), _seg(qk_ref, 1), _seg(vo_ref, 0), _seg(vo_ref, 1), g_ref[...],
                         nw_ref[...], tuple(ct_ref[h] for h in HEADS), tuple(n_ref[h] for h in HEADS))
        d_q, d_k, d_v, d_o, d_gates, d_nw, d_cts, d_ns = vjp(
            (dy_ref[...], tuple(dct_scr[h] for h in HEADS), tuple(dn_scr[h] for h in HEADS)))
        dqk_ref[:, 0:D_GROUP] = d_q
        dqk_ref[:, D_GROUP:2 * D_GROUP] = d_k
        for seg, val in enumerate((d_v, d_o)):
            dp_ref[:, seg * D_GROUP:(seg + 1) * D_GROUP] = val.astype(dp_ref.dtype)
            dsum_ref[:, seg * D_GROUP:(seg + 1) * D_GROUP] += jnp.sum(val, axis=0, keepdims=True)
        dg_ref[...] = d_gates
        dnw_ref[...] += d_nw
        for h in HEADS:
            dct_scr[h] = d_cts[h]
            dn_scr[h] = d_ns[h]

    rev = lambda c: nc - 1 - c
    st = lambda r, w: pl.BlockSpec((None, N_HEADS, r, w), lambda c: (rev(c), 0, 0, 0))
    return pl.pallas_call(
        body, name="mlstm_bwd", grid=(nc,),
        in_specs=[pl.BlockSpec((rows, 2 * D_GROUP), lambda c: (rev(c), 0)),
                  pl.BlockSpec((rows, 2 * D_GROUP), lambda c: (rev(c), 3)),
                  pl.BlockSpec((rows, LANES), lambda c: (rev(c), 0)),
                  pl.BlockSpec((1, D_GROUP), lambda c: (0, 0)),
                  st(D_HEAD, D_HEAD), st(1, D_HEAD), st(1, 1),
                  pl.BlockSpec((rows, D_GROUP), lambda c: (rev(c), 1)),
                  pl.BlockSpec(memory_space=pl.ANY)],
        out_specs=[pl.BlockSpec((rows, 2 * D_GROUP), lambda c: (rev(c), 3)),
                   pl.BlockSpec((rows, 2 * D_GROUP), lambda c: (rev(c), 0)),
                   pl.BlockSpec((rows, LANES), lambda c: (rev(c), 0)),
                   pl.BlockSpec((1, D_GROUP), lambda c: (0, 0)),
                   pl.BlockSpec((1, 2 * D_GROUP), lambda c: (0, 0))],
        out_shape=[jax.ShapeDtypeStruct(d_proj.shape, d_proj.dtype), jax.ShapeDtypeStruct((s, 2 * D_GROUP), F32),
                   jax.ShapeDtypeStruct((s, LANES), F32), jax.ShapeDtypeStruct((1, D_GROUP), F32),
                   jax.ShapeDtypeStruct((1, 2 * D_GROUP), F32)],
        input_output_aliases={8: 0},
        scratch_shapes=[pltpu.VMEM((N_HEADS, D_HEAD, D_HEAD), F32), pltpu.VMEM((N_HEADS, 1, D_HEAD), F32)],
        compiler_params=_params(("arbitrary",)),
    )(qk, proj, gates, norm_w, ct_s, n_s, m_s, d_y, d_proj)


LN_TOKENS = 512
ATT_TOKENS = 512


def _proj_res_ln(a, w, xres, g, b, name):
    s, dm = xres.shape
    k = a.shape[1]
    tb = min(LN_TOKENS, s)

    def body(a_ref, w_ref, x_ref, g_ref, b_ref, z_ref, o_ref):
        z = ALPHA * x_ref[...] + _nn_raw(a_ref[...], w_ref[...])
        z_ref[...] = z
        o_ref[...] = _layer_norm(z, g_ref[...], b_ref[...])

    tok = pl.BlockSpec((tb, dm), lambda i: (i, 0))
    vec = pl.BlockSpec((1, dm), lambda i: (0, 0))
    act = jax.ShapeDtypeStruct((s, dm), F32)
    return pl.pallas_call(
        body, name=name, grid=(s // tb,),
        in_specs=[pl.BlockSpec((tb, k), lambda i: (i, 0)), pl.BlockSpec((k, dm), lambda i: (0, 0)), tok, vec, vec],
        out_specs=[tok, tok], out_shape=[act, act], compiler_params=_params(("parallel",)),
    )(a, w, xres, g, b)


def _ln_bwd_proj(d_out, z, g, b, w, name):
    s, dm = z.shape
    k = w.shape[0]
    tb = min(LN_TOKENS, s)

    def body(do_ref, z_ref, g_ref, b_ref, w_ref, dz_ref, da_ref, dg_ref, db_ref):
        @pl.when(pl.program_id(0) == 0)
        def _():
            dg_ref[...] = jnp.zeros_like(dg_ref)
            db_ref[...] = jnp.zeros_like(db_ref)

        _, vjp = jax.vjp(_layer_norm, z_ref[...], g_ref[...], b_ref[...])
        d_z, d_g, d_b = vjp(do_ref[...])
        dz_ref[...] = d_z
        da_ref[...] = _nt_raw(d_z, w_ref[...])
        dg_ref[...] += d_g
        db_ref[...] += d_b

    tok = pl.BlockSpec((tb, dm), lambda i: (i, 0))
    vec = pl.BlockSpec((1, dm), lambda i: (0, 0))
    return pl.pallas_call(
        body, name=name, grid=(s // tb,),
        in_specs=[tok, tok, vec, vec, pl.BlockSpec((k, dm), lambda i: (0, 0))],
        out_specs=[tok, pl.BlockSpec((tb, k), lambda i: (i, 0)), vec, vec],
        out_shape=[jax.ShapeDtypeStruct((s, dm), F32), jax.ShapeDtypeStruct((s, k), F32),
                   jax.ShapeDtypeStruct((1, dm), F32), jax.ShapeDtypeStruct((1, dm), F32)],
        compiler_params=_params(("arbitrary",)),
    )(d_out, z, g, b, w)


def _proj_loss_tail(a, w, xres, g, b, target):
    s, dm = xres.shape
    k = a.shape[1]
    tb = min(ATT_TOKENS, s)

    def loss_fn(z, gg, bb, tgt):
        err = jnp.square(_layer_norm(z, gg, bb) - tgt)
        return 0.5 * jnp.sum(jnp.mean(err, axis=-1, keepdims=True), axis=0, keepdims=True)

    def body(a_ref, w_ref, x_ref, g_ref, b_ref, t_ref, loss_ref, dz_ref, dg_ref, db_ref):
        @pl.when(pl.program_id(0) == 0)
        def _():
            loss_ref[...] = jnp.zeros_like(loss_ref)
            dg_ref[...] = jnp.zeros_like(dg_ref)
            db_ref[...] = jnp.zeros_like(db_ref)

        halves = [slice(0, tb // 2), slice(tb // 2, tb)]
        zs = [ALPHA * x_ref[rows, :] + _nn_raw(a_ref[rows, :], w_ref[...]) for rows in halves]
        for rows, z in zip(halves, zs):
            tgt = t_ref[rows, :]
            loss, vjp = jax.vjp(lambda zz, gg, bb, tgt=tgt: loss_fn(zz, gg, bb, tgt), z, g_ref[...], b_ref[...])
            d_z, d_g, d_b = vjp(jnp.ones((1, 1), F32))
            loss_ref[...] += loss
            dz_ref[rows, :] = d_z
            dg_ref[...] += d_g
            db_ref[...] += d_b

    tok = pl.BlockSpec((tb, dm), lambda i: (i, 0))
    vec = pl.BlockSpec((1, dm), lambda i: (0, 0))
    one = pl.BlockSpec((1, 1), lambda i: (0, 0))
    return pl.pallas_call(
        body, name="ffn_down_loss_tail", grid=(s // tb,),
        in_specs=[pl.BlockSpec((tb, k), lambda i: (i, 0)), pl.BlockSpec((k, dm), lambda i: (0, 0)), tok, vec, vec, tok],
        out_specs=[one, tok, vec, vec],
        out_shape=[jax.ShapeDtypeStruct((1, 1), F32), jax.ShapeDtypeStruct((s, dm), F32),
                   jax.ShapeDtypeStruct((1, dm), F32), jax.ShapeDtypeStruct((1, dm), F32)],
        compiler_params=_params(("arbitrary",)),
    )(a, w, xres, g, b, target)


def _att_heads(qs, ks, vs):
    sc = [_nt(q, k) * (CA_DH ** -0.5) for q, k in zip(qs, ks)]
    p = [jax.nn.softmax(s, axis=-1) for s in sc]
    return tuple(_nn(pp, v) for pp, v in zip(p, vs))


def _head_slices(ref_or_value, offset):
    return tuple(ref_or_value[:, offset + h * CA_DH:offset + (h + 1) * CA_DH] for h in range(CA_HEADS))


def _cross_attention_fwd(x1, kv, wq, wo, g, b):
    s = x1.shape[0]
    tb = min(ATT_TOKENS, s)

    def body(x_ref, kv_ref, wq_ref, wo_ref, g_ref, b_ref, att_ref, z_ref, o_ref):
        x_blk = x_ref[...]
        q = _nn_raw(x_blk, wq_ref[...])
        att = jnp.concatenate(_att_heads(_head_slices(q, 0), _head_slices(kv_ref, 0), _head_slices(kv_ref, D_MODEL)),
                              axis=1)
        att_ref[...] = att.astype(att_ref.dtype)
        z = ALPHA * x_blk + _nn_raw(att, wo_ref[...])
        z_ref[...] = z
        o_ref[...] = _layer_norm(z, g_ref[...], b_ref[...])

    tok = pl.BlockSpec((tb, D_MODEL), lambda i: (i, 0))
    mat = pl.BlockSpec((D_MODEL, D_MODEL), lambda i: (0, 0))
    vec = pl.BlockSpec((1, D_MODEL), lambda i: (0, 0))
    act = jax.ShapeDtypeStruct((s, D_MODEL), F32)
    return pl.pallas_call(
        body, name="cross_attention_fwd", grid=(s // tb,),
        in_specs=[tok, pl.BlockSpec((N_MEM, 2 * D_MODEL), lambda i: (0, 0)), mat, mat, vec, vec],
        out_specs=[tok, tok, tok],
        out_shape=[jax.ShapeDtypeStruct((s, D_MODEL), BF16), act, act],
        compiler_params=_params(("parallel",)),
    )(x1, kv, wq, wo, g, b)


def _cross_attention_bwd(d_x2, x1, z2, kv, wq, wo, g, b):
    s = x1.shape[0]
    tb = min(ATT_TOKENS, s)

    def body(dx2_ref, x_ref, z_ref, kv_ref, wq_ref, wo_ref, g_ref, b_ref,
             dx1_ref, dq_ref, dz_ref, dkv_ref, dg_ref, db_ref):
        @pl.when(pl.program_id(0) == 0)
        def _():
            dkv_ref[...] = jnp.zeros_like(dkv_ref)
            dg_ref[...] = jnp.zeros_like(dg_ref)
            db_ref[...] = jnp.zeros_like(db_ref)

        _, ln_vjp = jax.vjp(_layer_norm, z_ref[...], g_ref[...], b_ref[...])
        d_z, d_g, d_b = ln_vjp(dx2_ref[...])
        dg_ref[...] += d_g
        db_ref[...] += d_b
        dz_ref[...] = d_z.astype(dz_ref.dtype)
        d_att = _nt_raw(d_z, wo_ref[...])
        q = _nn_raw(x_ref[...], wq_ref[...])
        _, vjp = jax.vjp(_att_heads, _head_slices(q, 0), _head_slices(kv_ref, 0), _head_slices(kv_ref, D_MODEL))
        d_qs, d_ks, d_vs = vjp(_head_slices(d_att, 0))
        for h in range(CA_HEADS):
            lo = h * CA_DH
            dkv_ref[:, lo:lo + CA_DH] += d_ks[h]
            dkv_ref[:, D_MODEL + lo:D_MODEL + lo + CA_DH] += d_vs[h]
        d_q = jnp.concatenate(d_qs, axis=1)
        dq_ref[...] = d_q.astype(dq_ref.dtype)
        dx1_ref[...] = ALPHA * d_z + _nt_raw(d_q, wq_ref[...])

    tok = pl.BlockSpec((tb, D_MODEL), lambda i: (i, 0))
    mem = pl.BlockSpec((N_MEM, 2 * D_MODEL), lambda i: (0, 0))
    mat = pl.BlockSpec((D_MODEL, D_MODEL), lambda i: (0, 0))
    vec = pl.BlockSpec((1, D_MODEL), lambda i: (0, 0))
    low = jax.ShapeDtypeStruct((s, D_MODEL), BF16)
    return pl.pallas_call(
        body, name="cross_attention_bwd", grid=(s // tb,),
        in_specs=[tok, tok, tok, mem, mat, mat, vec, vec], out_specs=[tok, tok, tok, mem, vec, vec],
        out_shape=[jax.ShapeDtypeStruct((s, D_MODEL), F32), low, low,
                   jax.ShapeDtypeStruct((N_MEM, 2 * D_MODEL), F32),
                   jax.ShapeDtypeStruct((1, D_MODEL), F32), jax.ShapeDtypeStruct((1, D_MODEL), F32)],
        compiler_params=_params(("arbitrary",)),
    )(d_x2, x1, z2, kv, wq, wo, g, b)


def _local_step(x, mem, target, w, mid_weights=None, ffn_weights=None, down_weights=None, on_ffn_grads=None,
                on_mid_grads=None,
                on_small_grads=None, on_last_grads=None):
    w = dict(w)
    s = x.shape[0]
    tm = min(512, s)
    tt_big = min(1024, s)
    proj = _matmul_nn(x, w["w_in_main"], w["b_in_main"], min(2048, s), 512, "proj")
    gates = _matmul_nn(x, w["w_in_gate"], w["b_in_gate"], tm, LANES, "proj_gates")
    qk = _ml_conv_fwd(proj, w["ml_conv_w"], w["ml_conv_b"])
    y, hg_states = _hgrn2_fwd(proj, w["hg_lb_logits"], w["hg_norm_w"])
    y, ct_s, n_s, m_s = _mlstm_fwd(qk, proj, gates, w["ml_norm_w"], y)
    if mid_weights is not None:
        w.update(mid_weights(y))
    z1, x1 = _proj_res_ln(y, w["w_out"], x, w["ln1_g"], w["ln1_b"], "out_proj_ln1")
    kv = _matmul_nn(mem, w["ca_wkv"], None, N_MEM, CA_DH, "kv")
    att, z2, x2 = _cross_attention_fwd(x1, kv, w["ca_wq"], w["ca_wo"], w["ln2_g"], w["ln2_b"])
    if ffn_weights is not None:
        w.update(ffn_weights(x2))
    u = _matmul_nn(x2, w["ffn_w_up"], None, min(2048, s), UP_SHARD_P, "ffn_up", BF16)
    hid = _ffn_conv_fwd(u, w["ffn_conv_w"], w["ffn_conv_b"])
    if down_weights is not None:
        w.update(down_weights(hid))
    loss, d_z3, d_ln3_g, d_ln3_b = _proj_loss_tail(hid, w["ffn_w_down"], x2, w["ln3_g"], w["ln3_b"], target)
    grads = {"ln3_g": d_ln3_g, "ln3_b": d_ln3_b}
    grads["ffn_w_down"] = _matmul_tn(hid, d_z3, 1536, D_MODEL, tt_big, "d_w_down")
    d_hid = _matmul_nt([(d_z3, w["ffn_w_down"])], None, 1.0, tm, D_FF_P, "d_hid", BF16)
    d_ug, d_uv, d_cwg, d_cwv, d_cbg, d_cbv = _ffn_conv_bwd(u, w["ffn_conv_w"], w["ffn_conv_b"], d_hid)
    grads["ffn_conv_w"] = jnp.concatenate([d_cwg, d_cwv], axis=-1)
    grads["ffn_conv_b"] = jnp.concatenate([d_cbg, d_cbv], axis=-1)
    half = N_DEV // 2
    d_w_up = _matmul_tn(x2, d_ug, D_MODEL, UP_SHARD_P, tt_big, "d_w_up_gate", shards=N_DEV, group=half)
    grads["ffn_w_up"] = _matmul_tn(x2, d_uv, D_MODEL, UP_SHARD_P, tt_big, "d_w_up_val", shards=N_DEV,
                                   shard0=half, group=half, into=d_w_up)
    d_x2 = _matmul_nt([(d_ug, w["ffn_w_up"], 0), (d_uv, w["ffn_w_up"], N_DEV // 2)], d_z3, ALPHA,
                      min(256, s), D_MODEL, "d_x2")
    if on_ffn_grads is not None:
        d_x2 = on_ffn_grads(grads, d_x2)
    d_x1, d_q, d_z2, d_kv, grads["ln2_g"], grads["ln2_b"] = _cross_attention_bwd(
        d_x2, x1, z2, kv, w["ca_wq"], w["ca_wo"], w["ln2_g"], w["ln2_b"])
    grads["ca_wo"] = _matmul_tn(att, d_z2, D_MODEL, D_MODEL, tt_big, "d_ca_wo")
    grads["ca_wq"] = _matmul_tn(x1, d_q, D_MODEL, D_MODEL, tt_big, "d_ca_wq")
    grads["ca_wkv"] = _matmul_tn(mem, d_kv, D_MODEL, CA_DH, N_MEM, "d_ca_wkv", shards=N_DEV, group=N_DEV)
    d_z1, d_y, grads["ln1_g"], grads["ln1_b"] = _ln_bwd_proj(d_x1, z1, w["ln1_g"], w["ln1_b"], w["w_out"],
                                                             "ln1_bwd_out_proj")
    grads["w_out"] = _matmul_tn(y, d_z1, D_MODEL, D_MODEL, tt_big, "d_w_out")
    if on_mid_grads is not None:
        d_y = on_mid_grads(grads, d_y)
    d_proj, grads["hg_lb_logits"], grads["hg_norm_w"], db_hg = _hgrn2_bwd(
        proj, w["hg_lb_logits"], w["hg_norm_w"], hg_states, d_y)
    d_proj, d_qk, d_gates, grads["ml_norm_w"], db_vo = _mlstm_bwd(
        qk, proj, gates, w["ml_norm_w"], ct_s, n_s, m_s, d_y, d_proj)
    d_proj, grads["ml_conv_w"], grads["ml_conv_b"], db_qk = _ml_conv_bwd(
        proj, w["ml_conv_w"], w["ml_conv_b"], d_qk, d_proj)
    grads["b_in_main"] = jnp.concatenate([db_hg, db_qk, db_vo], axis=-1)
    grads["w_in_gate"], grads["b_in_gate"] = _matmul_tn(x, d_gates, D_MODEL, LANES, tt_big, "d_w_in_gates",
                                                        colsum=True)
    if on_small_grads is not None:
        d_proj = on_small_grads(grads, loss, d_proj)
    grads["w_in_main"] = _matmul_tn(x, d_proj, D_MODEL, min(2048, D_IN_MAIN), tt_big, "d_w_in")
    if on_last_grads is not None:
        d_z1 = on_last_grads(grads, d_z1)
    grad_x = _matmul_nt([(d_proj, w["w_in_main"]), (d_gates, w["w_in_gate"])], d_z1, ALPHA, tm, D_MODEL, "d_x")
    return loss, grad_x, grads


HBM_SPEC = pl.BlockSpec(memory_space=pltpu.HBM)


def _coords():
    return lax.axis_index("x"), lax.axis_index("y"), lax.axis_index("c")


def _other_chips(x, y):
    return [(1 - x, y), (x, 1 - y), (1 - x, 1 - y)]


def _my_slot():
    x, y, c = _coords()
    return 4 * x + 2 * y + c


SEM_SPEC = pl.BlockSpec(memory_space=pltpu.SEMAPHORE)
ANY_SPEC = pl.BlockSpec(memory_space=pl.ANY)
SIDE_EFFECT = pltpu.SideEffectType.DATAFLOW_SIDE_EFFECTING


def _peer(x, y, c, d):
    flip = lambda v, bit: 1 - v if bit else v
    p = (flip(x, d & 4), flip(y, d & 2), flip(c, d & 1))
    return p, 4 * p[0] + 2 * p[1] + p[2]


def _direct_copies(gather, src_refs, land_refs, send_sems, recv_sems):
    x, y, c = _coords()
    me = 4 * x + 2 * y + c
    copies = []
    for a in range(len(src_refs)):
        for d in range(1, N_DEV):
            peer, peer_slot = _peer(x, y, c, d)
            copies.append(pltpu.make_async_remote_copy(
                src_ref=src_refs[a] if gather else src_refs[a].at[peer_slot],
                dst_ref=land_refs[a].at[me] if gather else land_refs[a].at[d - 1],
                send_sem=send_sems.at[7 * a + d - 1], recv_sem=recv_sems.at[7 * a + d - 1],
                device_id=peer, device_id_type=MESH))
    return copies


def _hbm(t):
    return pltpu.HBM(t.shape, t.dtype)


def _chip_copies(src_refs, land_refs, send_sems, recv_sems):
    x, y, c = _coords()
    me = 4 * x + 2 * y + c
    targets = [(x, y, 1 - c)] + [(cx, cy, c) for cx, cy in _other_chips(x, y)]
    return [pltpu.make_async_remote_copy(
        src_ref=src_refs[a], dst_ref=land_refs[a].at[me], send_sem=send_sems.at[4 * a + k],
        recv_sem=recv_sems.at[4 * a + k], device_id=target, device_id_type=MESH)
        for a in range(len(src_refs)) for k, target in enumerate(targets)]


def _forward_copies(land_refs, send_sems, recv_sems):
    x, y, c = _coords()
    return [pltpu.make_async_remote_copy(
        src_ref=land_refs[a].at[4 * cx + 2 * cy + c], dst_ref=land_refs[a].at[4 * cx + 2 * cy + c],
        send_sem=send_sems.at[3 * a + j], recv_sem=recv_sems.at[3 * a + j],
        device_id=(x, y, 1 - c), device_id_type=MESH)
        for a in range(len(land_refs)) for j, (cx, cy) in enumerate(_other_chips(x, y))]


def _split_copy_start(make_copies, n_sems, operands, through, name):
    n_ops = len(operands)

    def body(*refs):
        for cp in make_copies(refs[:n_ops], refs[n_ops + 1], refs[n_ops + 2]):
            cp.start()

    ins = [pltpu.with_memory_space_constraint(t, pltpu.HBM) for t in (*operands, through)]
    sems = pltpu.SemaphoreType.DMA((n_sems,))
    res = pl.pallas_call(
        body, name=name, out_shape=(sems, sems, *[_hbm(t) for t in ins]),
        in_specs=[HBM_SPEC] * (n_ops + 1), out_specs=(SEM_SPEC, SEM_SPEC, *[HBM_SPEC] * (n_ops + 1)),
        input_output_aliases={i: 2 + i for i in range(n_ops + 1)},
        compiler_params=pltpu.CompilerParams(has_side_effects=SIDE_EFFECT),
    )(*ins)
    return (res[0], res[1], list(res[2:2 + n_ops])), res[2 + n_ops]


def _split_copy_wait(make_copies, started, after, name):
    send_sems, recv_sems, operands = started
    n_ops = len(operands)
    after = list(after) if isinstance(after, (list, tuple)) else [after]

    def body(*refs):
        for cp in make_copies(refs[:n_ops], refs[n_ops], refs[n_ops + 1]):
            cp.wait_send()
            cp.wait_recv()

    res = pl.pallas_call(
        body, name=name, out_shape=tuple(_hbm(t) for t in operands),
        in_specs=[HBM_SPEC] * n_ops + [SEM_SPEC, SEM_SPEC] + [ANY_SPEC] * len(after),
        out_specs=tuple([HBM_SPEC] * n_ops), input_output_aliases={i: i for i in range(n_ops)},
        compiler_params=pltpu.CompilerParams(has_side_effects=SIDE_EFFECT),
    )(*operands, send_sems, recv_sems, *after)
    return list(res)


def _halves(make_copies, na):
    return lambda refs, send_sems, recv_sems: make_copies(refs[:na], refs[na:], send_sems, recv_sems)


def _direct_start(gather, arrays, through, name):
    na = len(arrays)
    lands = [lax.empty((N_DEV,) + t.shape if gather else (N_DEV - 1,) + t.shape[1:], t.dtype) for t in arrays]
    return _split_copy_start(_halves(functools.partial(_direct_copies, gather), na), 7 * na, [*arrays, *lands],
                             through, name)


def _direct_wait(gather, started, after, name):
    na = len(started[2]) // 2
    operands = _split_copy_wait(_halves(functools.partial(_direct_copies, gather), na), started, after, name)
    return operands[:na], operands[na:]


def _two_level_gather(shards, glue, name):
    na = len(shards)
    lands = [lax.empty((N_DEV,) + t.shape, t.dtype) for t in shards]
    nothing = jnp.zeros((SUBLANES, LANES), F32)
    started, _ = _split_copy_start(_halves(_chip_copies, na), 4 * na, [*shards, *lands], nothing, name + "_start")
    operands = _split_copy_wait(_halves(_chip_copies, na), started, glue, name + "_wait")
    started, mine = _split_copy_start(_forward_copies, 3 * na, operands[na:], operands[0], name + "_forward_start")
    lands = _split_copy_wait(_forward_copies, started, mine, name + "_forward_wait")
    return [lax.dynamic_update_index_in_dim(land, own, _my_slot(), 0)
            for own, land in zip([mine, *operands[1:na]], lands)]


def _row_tile(rows):
    for t in (256, 176, 128):
        if rows % t == 0 and rows > t:
            return t
    return rows


def _adamw_math(g, w, m, v):
    m_new = ADAM_B1 * m + (1.0 - ADAM_B1) * g
    v_new = ADAM_B2 * v + (1.0 - ADAM_B2) * jnp.square(g)
    m_hat = m_new / (1.0 - ADAM_B1 ** ADAM_STEP)
    v_hat = v_new / (1.0 - ADAM_B2 ** ADAM_STEP)
    delta = -ADAM_LR * (m_hat / (jnp.sqrt(v_hat) + ADAM_EPS) + ADAM_WD * w)
    return delta, m_new, v_new


def _adamw_sharded(chip, sums, got, w, m, v, name):
    r, c = w.shape
    tr = _row_tile(r)
    n_got = got.shape[0]

    def body(chip_ref, s_ref, g_ref, w_ref, m_ref, v_ref, go_ref, d_ref, nm_ref, nv_ref):
        g = s_ref[...].astype(F32)
        for i in range(n_got):
            g = g + g_ref[i].astype(F32)
        delta, m_new, v_new = _adamw_math(g, w_ref[...], m_ref[...], v_ref[...])
        go_ref[...] = g
        d_ref[...] = delta
        nm_ref[...] = m_new
        nv_ref[...] = v_new

    blk = pl.BlockSpec((tr, c), lambda i, chip_ref: (i, 0))
    out = jax.ShapeDtypeStruct((r, c), F32)
    return pl.pallas_call(
        body, name=name,
        grid_spec=pltpu.PrefetchScalarGridSpec(
            num_scalar_prefetch=1, grid=(r // tr,),
            in_specs=[pl.BlockSpec((None, tr, c), lambda i, chip_ref: (chip_ref[0], i, 0)),
                      pl.BlockSpec((n_got, tr, c), lambda i, chip_ref: (0, i, 0)), blk, blk, blk],
            out_specs=[blk, blk, blk, blk]),
        out_shape=[out, out, out, out],
        compiler_params=_params(("parallel",)),
    )(chip, sums, got, w, m, v)


def _adamw_replicated(parts, w, m, v):
    p, r, c = parts.shape

    def body(p_ref, w_ref, m_ref, v_ref, g_ref, d_ref, nm_ref, nv_ref):
        g = p_ref[0]
        for i in range(1, p):
            g = g + p_ref[i]
        delta, m_new, v_new = _adamw_math(g, w_ref[...], m_ref[...], v_ref[...])
        g_ref[...] = g
        d_ref[...] = delta
        nm_ref[...] = m_new
        nv_ref[...] = v_new

    blk = pl.BlockSpec((r, c), lambda i: (0, 0))
    out = jax.ShapeDtypeStruct((r, c), F32)
    return pl.pallas_call(
        body, name="adamw_replicated", grid=(1,),
        in_specs=[pl.BlockSpec((p, r, c), lambda i: (0, 0, 0)), blk, blk, blk],
        out_specs=[blk, blk, blk, blk], out_shape=[out, out, out, out],
        compiler_params=_params(("arbitrary",)),
    )(parts, w, m, v)


SHARDED_NAMES = ("w_in", "ml_conv_w", "w_out", "ca_wq", "ca_wkv", "ca_wo", "ffn_w_up", "ffn_conv_w", "ffn_w_down")
SMALL_NAMES = ("b_in", "hg_lb_logits", "hg_norm_w", "ml_conv_b", "ml_norm_w", "ln1_g", "ln1_b",
               "ln2_g", "ln2_b", "ffn_conv_b", "ln3_g", "ln3_b")
WEIGHT_NAMES = ("w_in", "b_in", "hg_lb_logits", "hg_norm_w", "ml_conv_w", "ml_conv_b", "ml_norm_w", "w_out",
                "ln1_g", "ln1_b", "ca_wq", "ca_wkv", "ca_wo", "ln2_g", "ln2_b", "ffn_w_up", "ffn_conv_w",
                "ffn_conv_b", "ffn_w_down", "ln3_g", "ln3_b")
PAD_TO = {"ffn_w_up": UP_SHARD_P, "ffn_conv_w": UP_SHARD_P}
SMALL_ROWS = 24
SMALL_W = D_MODEL


def _shard_2d(name, block):
    t = block[0]
    if name in PAD_TO:
        t = jnp.pad(t, ((0, 0), (0, PAD_TO[name] - t.shape[1])))
    return t


def _shard_like(name, t, like):
    return t[:, :like.shape[2]][None]


def _pad_cols(t, width):
    return jnp.pad(t, ((0, 0), (0, width - t.shape[1])))


FIRST_NAMES = ("w_in", "ml_conv_w")
FFN_NAMES = ("ffn_w_up", "ffn_w_down", "ffn_conv_w")
MID_NAMES = ("ca_wo", "ca_wq", "ca_wkv", "w_out")


def _first_weights(g, small):
    w = dict(small)
    last = g["w_in"][N_DEV - 1]
    split = D_IN_MAIN - (N_DEV - 1) * W_IN_SHARD
    w["w_in_main"] = jnp.concatenate([*[g["w_in"][j] for j in range(N_DEV - 1)], last[:, :split]], axis=1)
    w["w_in_gate"] = _pad_cols(last[:, split:], LANES)
    w["b_in_main"] = small["b_in"][:, :D_IN_MAIN]
    w["b_in_gate"] = _pad_cols(small["b_in"][:, D_IN_MAIN:], LANES)
    w["ml_conv_w"] = jnp.transpose(g["ml_conv_w"], (1, 0, 2)).reshape(ML_CONV, 2 * D_GROUP)
    return w


def _mid_weights(g):
    w = {n: g[n].reshape(D_MODEL, D_MODEL) for n in ("w_out", "ca_wq", "ca_wo")}
    w["ca_wkv"] = g["ca_wkv"]
    return w


FFN_UP_NAMES = ("ffn_w_up", "ffn_conv_w")
FFN_DOWN_NAMES = ("ffn_w_down",)


def _ffn_up_weights(g, small):
    w = {"ffn_w_up": g["ffn_w_up"]}
    w["ffn_conv_w"] = jnp.transpose(g["ffn_conv_w"], (1, 0, 2)).reshape(FFN_CONV, D_UP_P)
    w["ffn_conv_b"] = _pad_cols(small["ffn_conv_b"].reshape(N_DEV, UP_SHARD), UP_SHARD_P).reshape(1, D_UP_P)
    return w


def _ffn_down_weights(g):
    down = g["ffn_w_down"].reshape(N_DEV // 2, UP_SHARD, D_MODEL)
    return {"ffn_w_down": jnp.pad(down, ((0, 0), (0, UP_SHARD_P - UP_SHARD), (0, 0))).reshape(D_FF_P, D_MODEL)}


def _whole_weights(g, small):
    return {**_first_weights(g, small), **_mid_weights(g), **_ffn_up_weights(g, small), **_ffn_down_weights(g)}


def _owner_stack(n, grads):
    if n == "w_in":
        main, gate = grads["w_in_main"], grads["w_in_gate"][:, :D_IN - D_IN_MAIN]
        last = jnp.concatenate([main[:, (N_DEV - 1) * W_IN_SHARD:], gate], axis=1)
        return jnp.stack([*[main[:, j * W_IN_SHARD:(j + 1) * W_IN_SHARD] for j in range(N_DEV - 1)], last])
    if n in ("w_out", "ca_wq", "ca_wo"):
        return grads[n].reshape(N_DEV, D_MODEL // N_DEV, D_MODEL)
    if n == "ffn_w_down":
        down = grads[n].reshape(N_DEV // 2, UP_SHARD_P, D_MODEL)[:, :UP_SHARD]
        return down.reshape(N_DEV, D_FF // N_DEV, D_MODEL)
    if n == "ml_conv_w":
        return jnp.transpose(grads[n].reshape(ML_CONV, N_DEV, LANES), (1, 0, 2))
    if n == "ffn_conv_w":
        return jnp.transpose(grads[n].reshape(FFN_CONV, N_DEV, UP_SHARD_P), (1, 0, 2))
    return grads[n]


def _owner_stacks(grads):
    return {n: _owner_stack(n, grads) for n in SHARDED_NAMES}


def _small_grads(grads):
    out = {n: grads[n] for n in SMALL_NAMES if n in grads}
    out["b_in"] = jnp.concatenate([grads["b_in_main"], grads["b_in_gate"][:, :D_IN - D_IN_MAIN]], axis=1)
    out["ffn_conv_b"] = grads["ffn_conv_b"].reshape(N_DEV, UP_SHARD_P)[:, :UP_SHARD].reshape(1, D_UP)
    return out


def _pack_small(p, extra=None):
    flat = [p[n].reshape(-1) for n in SMALL_NAMES]
    if extra is not None:
        flat.append(extra.reshape(-1))
    flat = jnp.concatenate(flat)
    return jnp.pad(flat, (0, SMALL_ROWS * SMALL_W - flat.shape[0])).reshape(SMALL_ROWS, SMALL_W)


def _unpack_small(slab, like):
    out = {}
    flat = slab.reshape(-1)
    o = 0
    for n in SMALL_NAMES:
        out[n] = flat[o:o + like[n].size].reshape(like[n].shape)
        o += like[n].size
    return out, flat[o]


def kernel(x, mem, w_in, b_in, hg_lb_logits, hg_norm_w, ml_conv_w, ml_conv_b, ml_norm_w, w_out, ln1_g, ln1_b, ca_wq, ca_wkv, ca_wo, ln2_g, ln2_b, ffn_w_up, ffn_conv_w, ffn_conv_b, ffn_w_down, ln3_g, ln3_b, loss_target, m_w_in, m_b_in, m_hg_lb_logits, m_hg_norm_w, m_ml_conv_w, m_ml_conv_b, m_ml_norm_w, m_w_out, m_ln1_g, m_ln1_b, m_ca_wq, m_ca_wkv, m_ca_wo, m_ln2_g, m_ln2_b, m_ffn_w_up, m_ffn_conv_w, m_ffn_conv_b, m_ffn_w_down, m_ln3_g, m_ln3_b, v_w_in, v_b_in, v_hg_lb_logits, v_hg_norm_w, v_ml_conv_w, v_ml_conv_b, v_ml_norm_w, v_w_out, v_ln1_g, v_ln1_b, v_ca_wq, v_ca_wkv, v_ca_wo, v_ln2_g, v_ln2_b, v_ffn_w_up, v_ffn_conv_w, v_ffn_conv_b, v_ffn_w_down, v_ln3_g, v_ln3_b):
    params = dict(w_in=w_in, b_in=b_in, hg_lb_logits=hg_lb_logits, hg_norm_w=hg_norm_w, ml_conv_w=ml_conv_w,
                  ml_conv_b=ml_conv_b, ml_norm_w=ml_norm_w, w_out=w_out, ln1_g=ln1_g, ln1_b=ln1_b, ca_wq=ca_wq,
                  ca_wkv=ca_wkv, ca_wo=ca_wo, ln2_g=ln2_g, ln2_b=ln2_b, ffn_w_up=ffn_w_up, ffn_conv_w=ffn_conv_w,
                  ffn_conv_b=ffn_conv_b, ffn_w_down=ffn_w_down, ln3_g=ln3_g, ln3_b=ln3_b)
    mom1 = dict(w_in=m_w_in, b_in=m_b_in, hg_lb_logits=m_hg_lb_logits, hg_norm_w=m_hg_norm_w,
                ml_conv_w=m_ml_conv_w, ml_conv_b=m_ml_conv_b, ml_norm_w=m_ml_norm_w, w_out=m_w_out, ln1_g=m_ln1_g,
                ln1_b=m_ln1_b, ca_wq=m_ca_wq, ca_wkv=m_ca_wkv, ca_wo=m_ca_wo, ln2_g=m_ln2_g, ln2_b=m_ln2_b,
                ffn_w_up=m_ffn_w_up, ffn_conv_w=m_ffn_conv_w, ffn_conv_b=m_ffn_conv_b, ffn_w_down=m_ffn_w_down,
                ln3_g=m_ln3_g, ln3_b=m_ln3_b)
    mom2 = dict(w_in=v_w_in, b_in=v_b_in, hg_lb_logits=v_hg_lb_logits, hg_norm_w=v_hg_norm_w,
                ml_conv_w=v_ml_conv_w, ml_conv_b=v_ml_conv_b, ml_norm_w=v_ml_norm_w, w_out=v_w_out, ln1_g=v_ln1_g,
                ln1_b=v_ln1_b, ca_wq=v_ca_wq, ca_wkv=v_ca_wkv, ca_wo=v_ca_wo, ln2_g=v_ln2_g, ln2_b=v_ln2_b,
                ffn_w_up=v_ffn_w_up, ffn_conv_w=v_ffn_conv_w, ffn_conv_b=v_ffn_conv_b, ffn_w_down=v_ffn_w_down,
                ln3_g=v_ln3_g, ln3_b=v_ln3_b)

    x_idx, y_idx, c_idx = _coords()
    as_index = lambda v: jnp.reshape(v, (1,)).astype(jnp.int32)
    me = as_index(4 * x_idx + 2 * y_idx + c_idx)
    small_params = {n: params[n] for n in SMALL_NAMES}

    shards = {n: _shard_2d(n, params[n]) for n in SHARDED_NAMES}
    m_shards = {n: _shard_2d(n, mom1[n]) for n in SHARDED_NAMES}
    v_shards = {n: _shard_2d(n, mom2[n]) for n in SHARDED_NAMES}
    small_slabs = [_pack_small(params), _pack_small(mom1), _pack_small(mom2)]
    outgoing = {n: shards[n] if "conv" in n else shards[n].astype(BF16) for n in SHARDED_NAMES}
    to_send = lambda names: [outgoing[n] for n in names]
    glue = [*m_shards.values(), *v_shards.values(), *small_slabs, *shards.values(),
            *[outgoing[n] for n in SHARDED_NAMES if n not in FIRST_NAMES]]
    first = dict(zip(FIRST_NAMES, _two_level_gather(to_send(FIRST_NAMES), glue, "weights_gather_first")))
    mid_started, through = _direct_start(True, to_send(MID_NAMES), first["w_in"], "weights_gather_start_mid")
    ffn_started, through = _direct_start(True, to_send(FFN_UP_NAMES), through, "weights_gather_start_ffn_up")
    down_started, first["w_in"] = _direct_start(True, to_send(FFN_DOWN_NAMES), through,
                                                "weights_gather_start_ffn_down")

    def gathered_weights(names, started, after, tag):
        mine, lands = _direct_wait(True, started, after, "weights_gather_wait_" + tag)
        return {n: lax.dynamic_update_index_in_dim(land, own, me[0], 0) for n, own, land in zip(names, mine, lands)}

    started, own_stacks = {}, {}

    def start_group(names, tag):
        def hook(grads, through):
            own_stacks[tag] = [_owner_stack(n, grads).astype(BF16) for n in names]
            started[tag], through = _direct_start(False, own_stacks[tag], through, "grads_start_" + tag)
            return through
        return hook

    def start_small(grads, loss, through):
        started["small"], through = _direct_start(True, [_pack_small(_small_grads(grads), loss)], through,
                                                  "small_gather_start")
        return through

    loss, grad_x, grads = _local_step(
        x[0], mem[0], loss_target[0], _first_weights(first, small_params),
        lambda y: _mid_weights(gathered_weights(MID_NAMES, mid_started, y, "mid")),
        lambda x2: _ffn_up_weights(gathered_weights(FFN_UP_NAMES, ffn_started, x2, "ffn_up"), small_params),
        lambda hid: _ffn_down_weights(gathered_weights(FFN_DOWN_NAMES, down_started, hid, "ffn_down")),
        start_group(FFN_NAMES, "ffn"), start_group(MID_NAMES, "mid"), start_small, start_group(FIRST_NAMES, "last"))

    sharded_out = {}

    def update_group(names, tag, after):
        _, lands = _direct_wait(False, started[tag], after, "grads_wait_" + tag)
        for n, st, land in zip(names, own_stacks[tag], lands):
            res = _adamw_sharded(me, st, land, shards[n], m_shards[n], v_shards[n], "adamw_" + n)
            sharded_out[n] = [_shard_like(n, t, params[n]) for t in res]

    update_group(FFN_NAMES, "ffn", grad_x)
    update_group(MID_NAMES, "mid", grad_x)
    own_small, small_lands = _direct_wait(True, started["small"], grad_x, "small_gather_wait")
    small_parts = lax.dynamic_update_index_in_dim(small_lands[0], own_small[0], me[0], 0)
    small_res = _adamw_replicated(small_parts, *small_slabs)
    small_out = [_unpack_small(slab, params) for slab in small_res]
    done = [t for n in FFN_NAMES + MID_NAMES for t in sharded_out[n]]
    done += [t for small, _ in small_out for t in small.values()]
    update_group(FIRST_NAMES, "last", done)

    outs = []
    for k, (small, _) in enumerate(small_out):
        outs.extend(sharded_out[n][k] if n in sharded_out else small[n] for n in WEIGHT_NAMES)
    return (small_out[0][1], grad_x[None], *outs)
```

```python
import functools
import math

import jax
import jax.numpy as jnp
from jax import lax
from jax.experimental import pallas as pl
from jax.experimental.pallas import tpu as pltpu

F32 = jnp.float32
BF16 = jnp.bfloat16
HIGHEST = lax.Precision.HIGHEST
MESH = pl.DeviceIdType.MESH

N_DEV = 8
D_MODEL = 1024
N_MEM = 256
N_HEADS = 4
D_HEAD = 128
D_GROUP = N_HEADS * D_HEAD
CHUNK = 64
ML_CONV = 4
FFN_CONV = 3
D_FF = 2816
D_UP = 2 * D_FF
CA_HEADS = 4
CA_DH = D_MODEL // CA_HEADS
LANES = 128
SUBLANES = 8
D_IN = 8 * D_GROUP + 2 * N_HEADS
D_IN_MAIN = 8 * D_GROUP
W_IN_SHARD = D_IN // N_DEV
UP_SHARD = D_UP // N_DEV
UP_SHARD_P = 768
D_UP_P = N_DEV * UP_SHARD_P
D_FF_P = D_UP_P // 2
ALPHA = 2.0 ** 0.25
LN_EPS = 1e-5
NEG_BIG = -1e30
ADAM_LR = 0.001
ADAM_B1 = 0.9
ADAM_B2 = 0.999
ADAM_EPS = 1e-08
ADAM_WD = 0.01
ADAM_STEP = 10
VMEM_LIMIT = 56 * 1024 * 1024

SEG_HQ, SEG_HF, SEG_HI, SEG_HG, SEG_MQ, SEG_MK, SEG_MV, SEG_MO = (4 * i for i in range(8))


def _params(sem):
    return pltpu.CompilerParams(dimension_semantics=sem, vmem_limit_bytes=VMEM_LIMIT)


def _dg(a, b, ca, cb, precision=None):
    return lax.dot_general(a, b, (((ca,), (cb,)), ((), ())), precision=precision,
                           preferred_element_type=F32)


def _nn_raw(a, b):
    return _dg(a.astype(BF16), b.astype(BF16), 1, 0)


def _nt_raw(a, b):
    return _dg(a.astype(BF16), b.astype(BF16), 1, 1)


def _tn_raw(a, b):
    return _dg(a.astype(BF16), b.astype(BF16), 0, 0)


@jax.custom_vjp
def _nn(a, b):
    return _nn_raw(a, b)


_nn.defvjp(lambda a, b: (_nn_raw(a, b), (a, b)),
           lambda res, g: (_nt_raw(g, res[1]), _tn_raw(res[0], g)))


@jax.custom_vjp
def _nt(a, b):
    return _nt_raw(a, b)


_nt.defvjp(lambda a, b: (_nt_raw(a, b), (a, b)),
           lambda res, g: (_nn_raw(g, res[1]), _tn_raw(g, res[0])))


@jax.custom_vjp
def _tn(a, b):
    return _tn_raw(a, b)


_tn.defvjp(lambda a, b: (_tn_raw(a, b), (a, b)),
           lambda res, g: (_nt_raw(res[1], g), _nn_raw(res[0], g)))


def _layer_norm(z, g, b):
    mu = jnp.mean(z, axis=-1, keepdims=True)
    var = jnp.mean(jnp.square(z - mu), axis=-1, keepdims=True)
    return (z - mu) * lax.rsqrt(var + LN_EPS) * g + b


def _matmul_nn(a, w, bias, tm, tn, name, out_dtype=F32):
    m, k = a.shape
    if w.ndim == 3:
        n = w.shape[0] * w.shape[2]
        assert tn == w.shape[2]
        w_spec = pl.BlockSpec((None, k, tn), lambda i, j: (j, 0, 0))
    else:
        n = w.shape[1]
        w_spec = pl.BlockSpec((k, tn), lambda i, j: (0, j))

    def body(*refs):
        a_ref, w_ref = refs[0], refs[1]
        o_ref = refs[-1]
        acc = _nn_raw(a_ref[...], w_ref[...])
        if bias is not None:
            acc = acc + refs[2][...]
        o_ref[...] = acc.astype(o_ref.dtype)

    in_specs = [pl.BlockSpec((tm, k), lambda i, j: (i, 0)), w_spec]
    args = [a, w]
    if bias is not None:
        in_specs.append(pl.BlockSpec((1, tn), lambda i, j: (0, j)))
        args.append(bias)
    return pl.pallas_call(
        body, name=name, grid=(m // tm, n // tn), in_specs=in_specs,
        out_specs=pl.BlockSpec((tm, tn), lambda i, j: (i, j)),
        out_shape=jax.ShapeDtypeStruct((m, n), out_dtype),
        compiler_params=_params(("parallel", "parallel")),
    )(*args)


def _matmul_nt(pairs, add, scale, tm, tk, name, out_dtype=F32):
    m = pairs[0][0].shape[0]
    k = pairs[0][1].shape[-2]
    groups = []
    in_specs, args = [], []
    for pair in pairs:
        d, w = pair[0], pair[1]
        in_specs.append(pl.BlockSpec((tm, d.shape[1]), lambda i, j: (i, 0)))
        if w.ndim == 3:
            g = d.shape[1] // w.shape[2]
            blk = pair[2] // g
            in_specs.append(pl.BlockSpec((g, tk, w.shape[2]), lambda i, j, blk=blk: (blk, j, 0)))
            groups.append((g, w.shape[2]))
        else:
            in_specs.append(pl.BlockSpec((tk, w.shape[1]), lambda i, j: (j, 0)))
            groups.append(None)
        args += [d, w]
    if add is not None:
        in_specs.append(pl.BlockSpec((tm, tk), lambda i, j: (i, j)))
        args.append(add)

    def body(*refs):
        o_ref = refs[-1]
        acc = None
        for p, grp in enumerate(groups):
            d_ref, w_ref = refs[2 * p], refs[2 * p + 1]
            if grp is None:
                terms = [_nt_raw(d_ref[...], w_ref[...])]
            else:
                terms = [_nt_raw(d_ref[:, g * grp[1]:(g + 1) * grp[1]], w_ref[g]) for g in range(grp[0])]
            for t in terms:
                acc = t if acc is None else acc + t
        if add is not None:
            acc = acc + scale * refs[2 * len(groups)][...]
        o_ref[...] = acc.astype(o_ref.dtype)

    return pl.pallas_call(
        body, name=name, grid=(m // tm, k // tk), in_specs=in_specs,
        out_specs=pl.BlockSpec((tm, tk), lambda i, j: (i, j)),
        out_shape=jax.ShapeDtypeStruct((m, k), out_dtype),
        compiler_params=_params(("parallel", "parallel")),
    )(*args)


def _matmul_tn(a, b, tm, tn, tt, name, shards=None, shard0=0, group=1, into=None, colsum=False):
    t, m = a.shape
    n = b.shape[1]
    assert not colsum or tm == m
    n_in = 2 + (into is not None)
    out_dtype = BF16
    per_step = 1 if shards is None else group
    width = per_step * tn

    def body(*refs):
        a_ref, b_ref = refs[0], refs[1]
        o_ref, acc_ref = refs[n_in], refs[-1]
        first = pl.program_id(2) == 0

        @pl.when(first)
        def _():
            acc_ref[...] = jnp.zeros_like(acc_ref)

        if shards is None:
            acc_ref[...] += _tn_raw(a_ref[...], b_ref[...])
        else:
            lhs = a_ref[...].astype(BF16)
            for g in range(per_step):
                acc_ref[g] += _tn_raw(lhs, b_ref[:, g * tn:(g + 1) * tn])

        @pl.when(pl.program_id(2) == t // tt - 1)
        def _():
            o_ref[...] = acc_ref[...].astype(o_ref.dtype)

        if colsum:
            s_ref = refs[n_in + 1]

            @pl.when(first)
            def _():
                s_ref[...] = jnp.zeros_like(s_ref)

            s_ref[...] += jnp.sum(b_ref[...], axis=0, keepdims=True)

    in_specs = [pl.BlockSpec((tt, tm), lambda i, j, kk: (kk, i)),
                pl.BlockSpec((tt, width), lambda i, j, kk: (kk, j))]
    args = [a, b]
    aliases = {}
    if into is not None:
        in_specs.append(pl.BlockSpec(memory_space=pl.ANY))
        args.append(into)
        aliases = {2: 0}
    if shards is None:
        out_specs = [pl.BlockSpec((tm, tn), lambda i, j, kk: (i, j))]
        out_shape = [jax.ShapeDtypeStruct((m, n), out_dtype)]
        acc = pltpu.VMEM((tm, tn), F32)
    else:
        out_specs = [pl.BlockSpec((per_step, tm, tn), lambda i, j, kk: (shard0 // per_step + j, i, 0))]
        out_shape = [jax.ShapeDtypeStruct((shards, m, tn), out_dtype)]
        acc = pltpu.VMEM((per_step, tm, tn), F32)
    if colsum:
        out_specs.append(pl.BlockSpec((1, tn), lambda i, j, kk: (0, j)))
        out_shape.append(jax.ShapeDtypeStruct((1, n), F32))
    res = pl.pallas_call(
        body, name=name, grid=(m // tm, n // width, t // tt), in_specs=in_specs, out_specs=out_specs,
        out_shape=out_shape, input_output_aliases=aliases, scratch_shapes=[acc],
        compiler_params=_params(("parallel", "parallel", "arbitrary")),
    )(*args)
    return res if colsum else res[0]


ROW_TILE = 64


def _stack(ref, start, rows):
    return ref[pl.ds(start, rows), :].astype(F32).reshape(rows // SUBLANES, SUBLANES, LANES)


def _vreg_rows(ref, n):
    return [jnp.broadcast_to(ref[j:j + 1, :], (SUBLANES, LANES))[None] for j in range(n)]


def _column_total(acc):
    return jnp.sum(acc, axis=0, keepdims=True)


def _conv_fwd_tile(pad_ref, taps_w, bias, r0, rows):
    taps = len(taps_w)
    acc = bias
    for j in range(taps):
        acc = acc + _stack(pad_ref, SUBLANES - (taps - 1 - j) + r0, rows) * taps_w[j]
    return acc


def _conv_grads_tile(pad_ref, dpad_ref, dx_ref, taps_w, dws, r0, rows):
    taps = len(taps_w)
    x_rows = _stack(pad_ref, SUBLANES + r0, rows)
    dx = None
    for j in range(taps):
        d_shifted = _stack(dpad_ref, r0 + (taps - 1 - j), rows)
        term = d_shifted * taps_w[j]
        dx = term if dx is None else dx + term
        dws[j] = dws[j] + jnp.sum(d_shifted * x_rows, axis=0)
    dx_ref[r0:r0 + rows, :] = dx.reshape(rows, LANES).astype(dx_ref.dtype)
    return jnp.sum(dx, axis=0)


def _ml_conv_fwd(proj, conv_w, conv_b):
    s = proj.shape[0]
    nblk = 2 * D_GROUP // LANES

    def body(x_ref, w_ref, b_ref, o_ref, pad_ref):
        pad_ref[0:SUBLANES, :] = jnp.zeros((SUBLANES, LANES), F32)
        pad_ref[SUBLANES:, :] = x_ref[...].astype(F32)
        taps_w, bias = _vreg_rows(w_ref, ML_CONV), _vreg_rows(b_ref, 1)[0]
        for r0 in range(0, s, ROW_TILE):
            rows = min(ROW_TILE, s - r0)
            o_ref[r0:r0 + rows, :] = jax.nn.silu(_conv_fwd_tile(pad_ref, taps_w, bias, r0, rows)).reshape(rows, LANES)

    return pl.pallas_call(
        body, name="ml_conv_fwd", grid=(nblk,),
        in_specs=[pl.BlockSpec((s, LANES), lambda j: (0, SEG_MQ + j)),
                  pl.BlockSpec((ML_CONV, LANES), lambda j: (0, j)),
                  pl.BlockSpec((1, LANES), lambda j: (0, j))],
        out_specs=pl.BlockSpec((s, LANES), lambda j: (0, j)),
        out_shape=jax.ShapeDtypeStruct((s, 2 * D_GROUP), F32),
        scratch_shapes=[pltpu.VMEM((s + SUBLANES, LANES), F32)],
        compiler_params=_params(("parallel",)),
    )(proj, conv_w, conv_b)


def _ml_conv_bwd(proj, conv_w, conv_b, d_qk, d_proj):
    s = proj.shape[0]
    nblk = 2 * D_GROUP // LANES

    def body(x_ref, w_ref, b_ref, dy_ref, _, dx_ref, dw_ref, db_ref, dxs_ref, pad_ref, dpad_ref):
        pad_ref[0:SUBLANES, :] = jnp.zeros((SUBLANES, LANES), F32)
        pad_ref[SUBLANES:, :] = x_ref[...].astype(F32)
        dpad_ref[s:, :] = jnp.zeros((SUBLANES, LANES), F32)
        taps_w, bias = _vreg_rows(w_ref, ML_CONV), _vreg_rows(b_ref, 1)[0]
        db = jnp.zeros((SUBLANES, LANES), F32)
        for r0 in range(0, s, ROW_TILE):
            rows = min(ROW_TILE, s - r0)
            pre = _conv_fwd_tile(pad_ref, taps_w, bias, r0, rows)
            _, vjp = jax.vjp(jax.nn.silu, pre)
            d_pre, = vjp(_stack(dy_ref, r0, rows))
            dpad_ref[r0:r0 + rows, :] = d_pre.reshape(rows, LANES)
            db = db + jnp.sum(d_pre, axis=0)
        db_ref[...] = _column_total(db)
        dws = [jnp.zeros((SUBLANES, LANES), F32) for _ in range(ML_CONV)]
        dx_sum = jnp.zeros((SUBLANES, LANES), F32)
        for r0 in range(0, s, ROW_TILE):
            dx_sum = dx_sum + _conv_grads_tile(pad_ref, dpad_ref, dx_ref, taps_w, dws, r0, min(ROW_TILE, s - r0))
        dxs_ref[...] = _column_total(dx_sum)
        for j in range(ML_CONV):
            dw_ref[j:j + 1, :] = _column_total(dws[j])

    return pl.pallas_call(
        body, name="ml_conv_bwd", grid=(nblk,),
        in_specs=[pl.BlockSpec((s, LANES), lambda j: (0, SEG_MQ + j)),
                  pl.BlockSpec((ML_CONV, LANES), lambda j: (0, j)),
                  pl.BlockSpec((1, LANES), lambda j: (0, j)),
                  pl.BlockSpec((s, LANES), lambda j: (0, j)),
                  pl.BlockSpec(memory_space=pl.ANY)],
        out_specs=[pl.BlockSpec((s, LANES), lambda j: (0, SEG_MQ + j)),
                   pl.BlockSpec((ML_CONV, LANES), lambda j: (0, j)),
                   pl.BlockSpec((1, LANES), lambda j: (0, j)),
                   pl.BlockSpec((1, LANES), lambda j: (0, j))],
        out_shape=[jax.ShapeDtypeStruct(d_proj.shape, d_proj.dtype),
                   jax.ShapeDtypeStruct((ML_CONV, 2 * D_GROUP), F32),
                   jax.ShapeDtypeStruct((1, 2 * D_GROUP), F32),
                   jax.ShapeDtypeStruct((1, 2 * D_GROUP), F32)],
        input_output_aliases={4: 0},
        scratch_shapes=[pltpu.VMEM((s + SUBLANES, LANES), F32), pltpu.VMEM((s + SUBLANES, LANES), F32)],
        compiler_params=_params(("parallel",)),
    )(proj, conv_w, conv_b, d_qk, d_proj)


def _gelu_mul(a, b):
    return jax.nn.gelu(a) * b


GELU_C = math.sqrt(2.0 / math.pi)
GELU_K = 0.044715


def _gelu_mul_grads(a, b, d):
    a2 = a * a
    t = jnp.tanh(GELU_C * (a + GELU_K * (a * a2)))
    cdf = 0.5 * (1.0 + t)
    slope = cdf + (0.5 * GELU_C) * a * (1.0 - t * t) * (1.0 + (3.0 * GELU_K) * a2)
    return d * b * slope, d * (a * cdf)


FFN_BLOCKS = D_FF_P // LANES


def _ffn_conv_fwd(u, conv_w, conv_b):
    s = u.shape[0]

    def body(g_ref, v_ref, wg_ref, wv_ref, bg_ref, bv_ref, o_ref, gpad_ref, vpad_ref):
        for pad_ref, x_ref in ((gpad_ref, g_ref), (vpad_ref, v_ref)):
            pad_ref[0:SUBLANES, :] = jnp.zeros((SUBLANES, LANES), F32)
            pad_ref[SUBLANES:, :] = x_ref[...].astype(F32)
        taps_g, bias_g = _vreg_rows(wg_ref, FFN_CONV), _vreg_rows(bg_ref, 1)[0]
        taps_v, bias_v = _vreg_rows(wv_ref, FFN_CONV), _vreg_rows(bv_ref, 1)[0]
        for r0 in range(0, s, ROW_TILE):
            rows = min(ROW_TILE, s - r0)
            ug = _conv_fwd_tile(gpad_ref, taps_g, bias_g, r0, rows)
            uv = _conv_fwd_tile(vpad_ref, taps_v, bias_v, r0, rows)
            o_ref[r0:r0 + rows, :] = _gelu_mul(ug, uv).reshape(rows, LANES).astype(o_ref.dtype)

    col = lambda off: (lambda j: (0, off + j))
    return pl.pallas_call(
        body, name="ffn_conv_fwd", grid=(FFN_BLOCKS,),
        in_specs=[pl.BlockSpec((s, LANES), col(0)), pl.BlockSpec((s, LANES), col(FFN_BLOCKS)),
                  pl.BlockSpec((FFN_CONV, LANES), col(0)), pl.BlockSpec((FFN_CONV, LANES), col(FFN_BLOCKS)),
                  pl.BlockSpec((1, LANES), col(0)), pl.BlockSpec((1, LANES), col(FFN_BLOCKS))],
        out_specs=pl.BlockSpec((s, LANES), col(0)),
        out_shape=jax.ShapeDtypeStruct((s, D_FF_P), BF16),
        scratch_shapes=[pltpu.VMEM((s + SUBLANES, LANES), F32), pltpu.VMEM((s + SUBLANES, LANES), F32)],
        compiler_params=_params(("parallel",)),
    )(u, u, conv_w, conv_w, conv_b, conv_b)


def _ffn_conv_bwd(u, conv_w, conv_b, d_h):
    s = u.shape[0]

    def body(g_ref, v_ref, wg_ref, wv_ref, bg_ref, bv_ref, dh_ref,
             dug_ref, duv_ref, dwg_ref, dwv_ref, dbg_ref, dbv_ref,
             gpad_ref, vpad_ref, dgpad_ref, dvpad_ref):
        for pad_ref, x_ref in ((gpad_ref, g_ref), (vpad_ref, v_ref)):
            pad_ref[0:SUBLANES, :] = jnp.zeros((SUBLANES, LANES), F32)
            pad_ref[SUBLANES:, :] = x_ref[...].astype(F32)
        dgpad_ref[s:, :] = jnp.zeros((SUBLANES, LANES), F32)
        dvpad_ref[s:, :] = jnp.zeros((SUBLANES, LANES), F32)
        taps_g, bias_g = _vreg_rows(wg_ref, FFN_CONV), _vreg_rows(bg_ref, 1)[0]
        taps_v, bias_v = _vreg_rows(wv_ref, FFN_CONV), _vreg_rows(bv_ref, 1)[0]
        dbg = jnp.zeros((SUBLANES, LANES), F32)
        dbv = jnp.zeros((SUBLANES, LANES), F32)
        for r0 in range(0, s, ROW_TILE):
            rows = min(ROW_TILE, s - r0)
            ug = _conv_fwd_tile(gpad_ref, taps_g, bias_g, r0, rows)
            uv = _conv_fwd_tile(vpad_ref, taps_v, bias_v, r0, rows)
            d_ug, d_uv = _gelu_mul_grads(ug, uv, _stack(dh_ref, r0, rows))
            dgpad_ref[r0:r0 + rows, :] = d_ug.reshape(rows, LANES)
            dvpad_ref[r0:r0 + rows, :] = d_uv.reshape(rows, LANES)
            dbg = dbg + jnp.sum(d_ug, axis=0)
            dbv = dbv + jnp.sum(d_uv, axis=0)
        dbg_ref[...] = _column_total(dbg)
        dbv_ref[...] = _column_total(dbv)
        for pad_ref, dpad_ref, taps_w, dx_ref, dw_ref in ((gpad_ref, dgpad_ref, taps_g, dug_ref, dwg_ref),
                                                          (vpad_ref, dvpad_ref, taps_v, duv_ref, dwv_ref)):
            dws = [jnp.zeros((SUBLANES, LANES), F32) for _ in range(FFN_CONV)]
            for r0 in range(0, s, ROW_TILE):
                _conv_grads_tile(pad_ref, dpad_ref, dx_ref, taps_w, dws, r0, min(ROW_TILE, s - r0))
            for j in range(FFN_CONV):
                dw_ref[j:j + 1, :] = _column_total(dws[j])

    col = lambda off: (lambda j: (0, off + j))
    seq = pl.BlockSpec((s, LANES), col(0))
    return pl.pallas_call(
        body, name="ffn_conv_bwd", grid=(FFN_BLOCKS,),
        in_specs=[pl.BlockSpec((s, LANES), col(0)), pl.BlockSpec((s, LANES), col(FFN_BLOCKS)),
                  pl.BlockSpec((FFN_CONV, LANES), col(0)), pl.BlockSpec((FFN_CONV, LANES), col(FFN_BLOCKS)),
                  pl.BlockSpec((1, LANES), col(0)), pl.BlockSpec((1, LANES), col(FFN_BLOCKS)), seq],
        out_specs=[seq, seq, pl.BlockSpec((FFN_CONV, LANES), col(0)), pl.BlockSpec((FFN_CONV, LANES), col(0)),
                   pl.BlockSpec((1, LANES), col(0)), pl.BlockSpec((1, LANES), col(0))],
        out_shape=[jax.ShapeDtypeStruct((s, D_FF_P), BF16), jax.ShapeDtypeStruct((s, D_FF_P), BF16),
                   jax.ShapeDtypeStruct((FFN_CONV, D_FF_P), F32), jax.ShapeDtypeStruct((FFN_CONV, D_FF_P), F32),
                   jax.ShapeDtypeStruct((1, D_FF_P), F32), jax.ShapeDtypeStruct((1, D_FF_P), F32)],
        scratch_shapes=[pltpu.VMEM((s + SUBLANES, LANES), F32) for _ in range(4)],
        compiler_params=_params(("parallel",)),
    )(u, u, conv_w, conv_w, conv_b, conv_b, d_h)


def _chunk_masks(c):
    row = lax.broadcasted_iota(jnp.int32, (c, c), 0)
    col = lax.broadcasted_iota(jnp.int32, (c, c), 1)
    return row, col


@jax.custom_vjp
def _split_heads(x):
    return tuple(x[:, h * D_HEAD:(h + 1) * D_HEAD] for h in range(N_HEADS))


_split_heads.defvjp(lambda x: (_split_heads(x), None), lambda _, gs: (jnp.concatenate(gs, axis=1),))


@jax.custom_vjp
def _merge_heads(xs):
    return jnp.concatenate(xs, axis=1)


_merge_heads.defvjp(lambda xs: (_merge_heads(xs), None), lambda _, g: (_split_heads(g),))


@jax.custom_vjp
def _split_chunks(x):
    return tuple(x[i * CHUNK:(i + 1) * CHUNK] for i in range(x.shape[0] // CHUNK))


_split_chunks.defvjp(lambda x: (_split_chunks(x), None), lambda _, gs: (jnp.concatenate(gs, axis=0),))


@jax.custom_vjp
def _merge_chunks(xs):
    return jnp.concatenate(xs, axis=0)


_merge_chunks.defvjp(lambda xs: (_merge_chunks(xs), None), lambda _, g: (_split_chunks(g),))


def _blocks(x):
    return [_split_heads(rows) for rows in _split_chunks(x)]


def _per_chunk_rows(per_chunk, rid):
    out = per_chunk[0]
    for i in range(1, len(per_chunk)):
        out = jnp.where(rid >= i * CHUNK, per_chunk[i], out)
    return out


HEADS = range(N_HEADS)
CHUNKS_PER_STEP = 8
ML_CHUNKS_PER_STEP = 1


def _hg_chunk(hq, hf, hi, hgate, l0, l1, nw, sts):
    n = hq.shape[0] // CHUNK
    causal = _chunk_masks(CHUNK)
    causal = causal[1] <= causal[0]
    mx = lax.stop_gradient(jnp.maximum(l0, l1))
    e0 = jnp.exp(l0 - mx)
    e1 = jnp.exp(l1 - mx)
    lb = e0 / (e0 + e1)
    sig = jax.nn.sigmoid(hf)
    lf = jnp.log(lb + (1.0 - lb) * sig)
    k = (1.0 - lb) * jax.nn.sigmoid(-hf)
    q = jax.nn.silu(hq)
    tri = causal.astype(F32)
    b = _merge_chunks(tuple(_dg(tri, rows, 1, 0, HIGHEST) for rows in _split_chunks(lf)))
    rid = lax.broadcasted_iota(jnp.int32, b.shape, 0)
    pick = lambda r: jnp.sum(jnp.where(rid == r, b, 0.0), axis=0, keepdims=True)
    b_last_c = [pick(i * CHUNK + CHUNK - 1) for i in range(n)]
    b_ref = _per_chunk_rows([pick(i * CHUNK + CHUNK // 2 - 1) for i in range(n)], rid)
    b_last = _per_chunk_rows(b_last_c, rid)
    qa = _blocks(q * jnp.exp(b - b_ref))
    ka = _blocks(k * jnp.exp(b_ref - b))
    qe = _blocks(q * jnp.exp(b))
    kd = _blocks(k * jnp.exp(b_last - b))
    decay = [_split_heads(jnp.exp(b_last_c[i])) for i in range(n)]
    v = _blocks(hi)
    chunks = range(n)
    attn = [[jnp.where(causal, _nt(qa[i][h], ka[i][h]), 0.0) for h in HEADS] for i in chunks]
    intra = [[_nn(attn[i][h], v[i][h]) for h in HEADS] for i in chunks]
    kv = [[_tn(v[i][h], kd[i][h]) for h in HEADS] for i in chunks]
    normed = []
    for i in chunks:
        inter = [_nt(qe[i][h], sts[h]) for h in HEADS]
        sts = tuple(decay[i][h] * sts[h] + kv[i][h] for h in HEADS)
        o = [intra[i][h] + inter[h] for h in HEADS]
        normed.append(_merge_heads(tuple(o[h] * lax.rsqrt(jnp.mean(o[h] * o[h], axis=-1, keepdims=True) + LN_EPS)
                                         for h in HEADS)))
    return _merge_chunks(tuple(normed)) * nw * jax.nn.silu(hgate), sts


def _seg(ref, seg):
    return ref[:, seg * D_GROUP:(seg + 1) * D_GROUP]


def _hgrn2_fwd(proj, logits, norm_w):
    s = proj.shape[0]
    rows = CHUNKS_PER_STEP * CHUNK
    nc = s // rows

    def body(p_ref, lg_ref, nw_ref, y_ref, st_out_ref, st_scr):
        @pl.when(pl.program_id(0) == 0)
        def _():
            st_scr[...] = jnp.zeros_like(st_scr)

        sts = tuple(st_scr[h] for h in HEADS)
        y, sts_new = _hg_chunk(_seg(p_ref, 0), _seg(p_ref, 1), _seg(p_ref, 2), _seg(p_ref, 3),
                               lg_ref[0:1, :], lg_ref[1:2, :], nw_ref[...], sts)
        y_ref[...] = y.astype(y_ref.dtype)
        for h in HEADS:
            st_out_ref[h] = sts[h]
            st_scr[h] = sts_new[h]

    return pl.pallas_call(
        body, name="hgrn2_fwd", grid=(nc,),
        in_specs=[pl.BlockSpec((rows, 4 * D_GROUP), lambda c: (c, 0)),
                  pl.BlockSpec((2, D_GROUP), lambda c: (0, 0)),
                  pl.BlockSpec((1, D_GROUP), lambda c: (0, 0))],
        out_specs=[pl.BlockSpec((rows, D_GROUP), lambda c: (c, 0)),
                   pl.BlockSpec((None, N_HEADS, D_HEAD, D_HEAD), lambda c: (c, 0, 0, 0))],
        out_shape=[jax.ShapeDtypeStruct((s, 2 * D_GROUP), BF16),
                   jax.ShapeDtypeStruct((nc, N_HEADS, D_HEAD, D_HEAD), F32)],
        scratch_shapes=[pltpu.VMEM((N_HEADS, D_HEAD, D_HEAD), F32)],
        compiler_params=_params(("arbitrary",)),
    )(proj, logits, norm_w)


def _hgrn2_bwd(proj, logits, norm_w, states, d_y):
    s = proj.shape[0]
    rows = CHUNKS_PER_STEP * CHUNK
    nc = s // rows

    def body(p_ref, lg_ref, nw_ref, st_ref, dy_ref, dp_ref, dl_ref, dnw_ref, dsum_ref, dst_scr):
        @pl.when(pl.program_id(0) == 0)
        def _():
            dst_scr[...] = jnp.zeros_like(dst_scr)
            dl_ref[...] = jnp.zeros_like(dl_ref)
            dnw_ref[...] = jnp.zeros_like(dnw_ref)
            dsum_ref[...] = jnp.zeros_like(dsum_ref)

        _, vjp = jax.vjp(_hg_chunk, _seg(p_ref, 0), _seg(p_ref, 1), _seg(p_ref, 2), _seg(p_ref, 3),
                         lg_ref[0:1, :], lg_ref[1:2, :], nw_ref[...], tuple(st_ref[h] for h in HEADS))
        d_hq, d_hf, d_hi, d_hg, d_l0, d_l1, d_nw, d_sts = vjp((dy_ref[...], tuple(dst_scr[h] for h in HEADS)))
        for seg, val in enumerate((d_hq, d_hf, d_hi, d_hg)):
            dp_ref[:, seg * D_GROUP:(seg + 1) * D_GROUP] = val.astype(dp_ref.dtype)
            dsum_ref[:, seg * D_GROUP:(seg + 1) * D_GROUP] += jnp.sum(val, axis=0, keepdims=True)
        dl_ref[0:1, :] += d_l0
        dl_ref[1:2, :] += d_l1
        dnw_ref[...] += d_nw
        for h in HEADS:
            dst_scr[h] = d_sts[h]

    rev = lambda c: nc - 1 - c
    return pl.pallas_call(
        body, name="hgrn2_bwd", grid=(nc,),
        in_specs=[pl.BlockSpec((rows, 4 * D_GROUP), lambda c: (rev(c), 0)),
                  pl.BlockSpec((2, D_GROUP), lambda c: (0, 0)),
                  pl.BlockSpec((1, D_GROUP), lambda c: (0, 0)),
                  pl.BlockSpec((None, N_HEADS, D_HEAD, D_HEAD), lambda c: (rev(c), 0, 0, 0)),
                  pl.BlockSpec((rows, D_GROUP), lambda c: (rev(c), 0))],
        out_specs=[pl.BlockSpec((rows, 4 * D_GROUP), lambda c: (rev(c), 0)),
                   pl.BlockSpec((2, D_GROUP), lambda c: (0, 0)),
                   pl.BlockSpec((1, D_GROUP), lambda c: (0, 0)),
                   pl.BlockSpec((1, 4 * D_GROUP), lambda c: (0, 0))],
        out_shape=[jax.ShapeDtypeStruct((s, D_IN_MAIN), BF16), jax.ShapeDtypeStruct((2, D_GROUP), F32),
                   jax.ShapeDtypeStruct((1, D_GROUP), F32), jax.ShapeDtypeStruct((1, 4 * D_GROUP), F32)],
        scratch_shapes=[pltpu.VMEM((N_HEADS, D_HEAD, D_HEAD), F32)],
        compiler_params=_params(("arbitrary",)),
    )(proj, logits, norm_w, states, d_y)


def _gate_column(gates, lane, idx):
    return jnp.sum(jnp.where(lane == idx, gates, 0.0), axis=1, keepdims=True)


def _head_layer_norm(h):
    mu = jnp.mean(h, axis=-1, keepdims=True)
    var = jnp.mean(jnp.square(h - mu), axis=-1, keepdims=True)
    return (h - mu) * lax.rsqrt(var + LN_EPS)


def _ml_chunk(qc, kc, v, mo, gates, nw, cts, ns, ms):
    n = qc.shape[0] // CHUNK
    row, col = _chunk_masks(CHUNK)
    mask = col <= row
    eye = col == row
    to_row = lambda t: jnp.sum(jnp.where(eye, t, 0.0), axis=0, keepdims=True)
    q = _blocks(qc * (D_HEAD ** -0.5))
    k = _blocks(kc)
    vs = _blocks(v)
    gate_rows = _split_chunks(gates)
    lane = lax.broadcasted_iota(jnp.int32, gate_rows[0].shape, 1)
    each = [(i, h) for i in range(n) for h in HEADS]
    on_each = lambda f: {ih: f(*ih) for ih in each}
    ig = on_each(lambda i, h: _gate_column(gate_rows[i], lane, h))
    lf = on_each(lambda i, h: jax.nn.log_sigmoid(_gate_column(gate_rows[i], lane, N_HEADS + h)))
    lf_row = on_each(lambda i, h: to_row(lf[i, h]))
    ig_row = on_each(lambda i, h: to_row(ig[i, h]))
    b_col = on_each(lambda i, h: jnp.sum(jnp.where(mask, lf_row[i, h], 0.0), axis=1, keepdims=True))
    b_row = on_each(lambda i, h: jnp.sum(jnp.where(row <= col, lf[i, h], 0.0), axis=0, keepdims=True))
    g = on_each(lambda i, h: jnp.sum(lf[i, h], axis=0, keepdims=True))
    d = on_each(lambda i, h: jnp.where(mask, b_col[i, h] - b_row[i, h] + ig_row[i, h], -jnp.inf))
    a = on_each(lambda i, h: g[i, h] - b_col[i, h] + ig[i, h])
    m_at = {(0, h): ms[h] for h in HEADS}
    for i, h in each:
        m_at[i + 1, h] = lax.stop_gradient(jnp.maximum(g[i, h] + m_at[i, h], jnp.max(a[i, h], axis=0, keepdims=True)))
    inter = on_each(lambda i, h: b_col[i, h] + m_at[i, h])
    m_t = on_each(lambda i, h: lax.stop_gradient(jnp.maximum(inter[i, h], jnp.max(d[i, h], axis=1, keepdims=True))))
    qk = on_each(lambda i, h: _nt(q[i][h], k[i][h]))
    sc = on_each(lambda i, h: qk[i, h] * jnp.exp(d[i, h] - m_t[i, h]))
    w_inter = on_each(lambda i, h: jnp.exp(inter[i, h] - m_t[i, h]))
    sv = on_each(lambda i, h: _nn(sc[i, h], vs[i][h]))
    decay = on_each(lambda i, h: jnp.exp(g[i, h] + m_at[i, h] - m_at[i + 1, h]))
    wk = on_each(lambda i, h: k[i][h] * jnp.exp(a[i, h] - m_at[i + 1, h]))
    kv = on_each(lambda i, h: _tn(vs[i][h], wk[i, h]))
    normed = []
    for i in range(n):
        qc_state = [_nt(q[i][h], cts[h]) for h in HEADS]
        num = [sv[i, h] + w_inter[i, h] * qc_state[h] for h in HEADS]
        den = [jnp.sum(sc[i, h], axis=1, keepdims=True)
               + w_inter[i, h] * jnp.sum(q[i][h] * ns[h], axis=1, keepdims=True) for h in HEADS]
        hh = [num[h] / jnp.maximum(jnp.abs(den[h]), jnp.exp(-m_t[i, h])) for h in HEADS]
        cts = tuple(decay[i, h] * cts[h] + kv[i, h] for h in HEADS)
        ns = tuple(decay[i, h] * ns[h] + jnp.sum(wk[i, h], axis=0, keepdims=True) for h in HEADS)
        normed.append(_merge_heads(tuple(_head_layer_norm(hh[h]) for h in HEADS)))
    y = jax.nn.sigmoid(mo) * (_merge_chunks(tuple(normed)) * nw)
    return y, cts, ns, tuple(m_at[n, h] for h in HEADS)


def _mlstm_fwd(qk, proj, gates, norm_w, y):
    s = proj.shape[0]
    rows = ML_CHUNKS_PER_STEP * CHUNK
    nc = s // rows

    def body(qk_ref, vo_ref, g_ref, nw_ref, _, y_ref, ct_out, n_out, m_out, ct_scr, n_scr, m_scr):
        @pl.when(pl.program_id(0) == 0)
        def _():
            ct_scr[...] = jnp.zeros_like(ct_scr)
            n_scr[...] = jnp.zeros_like(n_scr)
            m_scr[...] = jnp.full(m_scr.shape, NEG_BIG, F32)

        cts = tuple(ct_scr[h] for h in HEADS)
        ns = tuple(n_scr[h] for h in HEADS)
        ms = tuple(m_scr[h] for h in HEADS)
        y, cts_new, ns_new, ms_new = _ml_chunk(_seg(qk_ref, 0), _seg(qk_ref, 1), _seg(vo_ref, 0), _seg(vo_ref, 1),
                                               g_ref[...], nw_ref[...], cts, ns, ms)
        y_ref[...] = y.astype(y_ref.dtype)
        for h in HEADS:
            ct_out[h], n_out[h], m_out[h] = cts[h], ns[h], ms[h]
            ct_scr[h], n_scr[h], m_scr[h] = cts_new[h], ns_new[h], ms_new[h]

    st = lambda r, w: pl.BlockSpec((None, N_HEADS, r, w), lambda c: (c, 0, 0, 0))
    return pl.pallas_call(
        body, name="mlstm_fwd", grid=(nc,),
        in_specs=[pl.BlockSpec((rows, 2 * D_GROUP), lambda c: (c, 0)),
                  pl.BlockSpec((rows, 2 * D_GROUP), lambda c: (c, 3)),
                  pl.BlockSpec((rows, LANES), lambda c: (c, 0)),
                  pl.BlockSpec((1, D_GROUP), lambda c: (0, 0)),
                  pl.BlockSpec(memory_space=pl.ANY)],
        out_specs=[pl.BlockSpec((rows, D_GROUP), lambda c: (c, 1)),
                   st(D_HEAD, D_HEAD), st(1, D_HEAD), st(1, 1)],
        out_shape=[jax.ShapeDtypeStruct(y.shape, y.dtype),
                   jax.ShapeDtypeStruct((nc, N_HEADS, D_HEAD, D_HEAD), F32),
                   jax.ShapeDtypeStruct((nc, N_HEADS, 1, D_HEAD), F32),
                   jax.ShapeDtypeStruct((nc, N_HEADS, 1, 1), F32)],
        input_output_aliases={4: 0},
        scratch_shapes=[pltpu.VMEM((N_HEADS, D_HEAD, D_HEAD), F32), pltpu.VMEM((N_HEADS, 1, D_HEAD), F32),
                        pltpu.VMEM((N_HEADS, 1, 1), F32)],
        compiler_params=_params(("arbitrary",)),
    )(qk, proj, gates, norm_w, y)


def _mlstm_bwd(qk, proj, gates, norm_w, ct_s, n_s, m_s, d_y, d_proj):
    s = proj.shape[0]
    rows = ML_CHUNKS_PER_STEP * CHUNK
    nc = s // rows

    def body(qk_ref, vo_ref, g_ref, nw_ref, ct_ref, n_ref, m_ref, dy_ref, _,
             dp_ref, dqk_ref, dg_ref, dnw_ref, dsum_ref, dct_scr, dn_scr):
        @pl.when(pl.program_id(0) == 0)
        def _():
            dct_scr[...] = jnp.zeros_like(dct_scr)
            dn_scr[...] = jnp.zeros_like(dn_scr)
            dnw_ref[...] = jnp.zeros_like(dnw_ref)
            dsum_ref[...] = jnp.zeros_like(dsum_ref)

        ms = tuple(m_ref[h] for h in HEADS)
        step = lambda *a: _ml_chunk(*a, ms)[:3]
        _, vjp = jax.vjp(step, _seg(qk_ref, 0), _seg(qk_ref, 1), _seg(vo_ref, 0), _seg(vo_ref, 1), g_ref[...],
                         nw_ref[...], tuple(ct_ref[h] for h in HEADS), tuple(n_ref[h] for h in HEADS))
        d_q, d_k, d_v, d_o, d_gates, d_nw, d_cts, d_ns = vjp(
            (dy_ref[...], tuple(dct_scr[h] for h in HEADS), tuple(dn_scr[h] for h in HEADS)))
        dqk_ref[:, 0:D_GROUP] = d_q
        dqk_ref[:, D_GROUP:2 * D_GROUP] = d_k
        for seg, val in enumerate((d_v, d_o)):
            dp_ref[:, seg * D_GROUP:(seg + 1) * D_GROUP] = val.astype(dp_ref.dtype)
            dsum_ref[:, seg * D_GROUP:(seg + 1) * D_GROUP] += jnp.sum(val, axis=0, keepdims=True)
        dg_ref[...] = d_gates
        dnw_ref[...] += d_nw
        for h in HEADS:
            dct_scr[h] = d_cts[h]
            dn_scr[h] = d_ns[h]

    rev = lambda c: nc - 1 - c
    st = lambda r, w: pl.BlockSpec((None, N_HEADS, r, w), lambda c: (rev(c), 0, 0, 0))
    return pl.pallas_call(
        body, name="mlstm_bwd", grid=(nc,),
        in_specs=[pl.BlockSpec((rows, 2 * D_GROUP), lambda c: (rev(c), 0)),
                  pl.BlockSpec((rows, 2 * D_GROUP), lambda c: (rev(c), 3)),
                  pl.BlockSpec((rows, LANES), lambda c: (rev(c), 0)),
                  pl.BlockSpec((1, D_GROUP), lambda c: (0, 0)),
                  st(D_HEAD, D_HEAD), st(1, D_HEAD), st(1, 1),
                  pl.BlockSpec((rows, D_GROUP), lambda c: (rev(c), 1)),
                  pl.BlockSpec(memory_space=pl.ANY)],
        out_specs=[pl.BlockSpec((rows, 2 * D_GROUP), lambda c: (rev(c), 3)),
                   pl.BlockSpec((rows, 2 * D_GROUP), lambda c: (rev(c), 0)),
                   pl.BlockSpec((rows, LANES), lambda c: (rev(c), 0)),
                   pl.BlockSpec((1, D_GROUP), lambda c: (0, 0)),
                   pl.BlockSpec((1, 2 * D_GROUP), lambda c: (0, 0))],
        out_shape=[jax.ShapeDtypeStruct(d_proj.shape, d_proj.dtype), jax.ShapeDtypeStruct((s, 2 * D_GROUP), F32),
                   jax.ShapeDtypeStruct((s, LANES), F32), jax.ShapeDtypeStruct((1, D_GROUP), F32),
                   jax.ShapeDtypeStruct((1, 2 * D_GROUP), F32)],
        input_output_aliases={8: 0},
        scratch_shapes=[pltpu.VMEM((N_HEADS, D_HEAD, D_HEAD), F32), pltpu.VMEM((N_HEADS, 1, D_HEAD), F32)],
        compiler_params=_params(("arbitrary",)),
    )(qk, proj, gates, norm_w, ct_s, n_s, m_s, d_y, d_proj)


LN_TOKENS = 512
ATT_TOKENS = 512


def _proj_res_ln(a, w, xres, g, b, name):
    s, dm = xres.shape
    k = a.shape[1]
    tb = min(LN_TOKENS, s)

    def body(a_ref, w_ref, x_ref, g_ref, b_ref, z_ref, o_ref):
        z = ALPHA * x_ref[...] + _nn_raw(a_ref[...], w_ref[...])
        z_ref[...] = z
        o_ref[...] = _layer_norm(z, g_ref[...], b_ref[...])

    tok = pl.BlockSpec((tb, dm), lambda i: (i, 0))
    vec = pl.BlockSpec((1, dm), lambda i: (0, 0))
    act = jax.ShapeDtypeStruct((s, dm), F32)
    return pl.pallas_call(
        body, name=name, grid=(s // tb,),
        in_specs=[pl.BlockSpec((tb, k), lambda i: (i, 0)), pl.BlockSpec((k, dm), lambda i: (0, 0)), tok, vec, vec],
        out_specs=[tok, tok], out_shape=[act, act], compiler_params=_params(("parallel",)),
    )(a, w, xres, g, b)


def _ln_bwd_proj(d_out, z, g, b, w, name):
    s, dm = z.shape
    k = w.shape[0]
    tb = min(LN_TOKENS, s)

    def body(do_ref, z_ref, g_ref, b_ref, w_ref, dz_ref, da_ref, dg_ref, db_ref):
        @pl.when(pl.program_id(0) == 0)
        def _():
            dg_ref[...] = jnp.zeros_like(dg_ref)
            db_ref[...] = jnp.zeros_like(db_ref)

        _, vjp = jax.vjp(_layer_norm, z_ref[...], g_ref[...], b_ref[...])
        d_z, d_g, d_b = vjp(do_ref[...])
        dz_ref[...] = d_z
        da_ref[...] = _nt_raw(d_z, w_ref[...])
        dg_ref[...] += d_g
        db_ref[...] += d_b

    tok = pl.BlockSpec((tb, dm), lambda i: (i, 0))
    vec = pl.BlockSpec((1, dm), lambda i: (0, 0))
    return pl.pallas_call(
        body, name=name, grid=(s // tb,),
        in_specs=[tok, tok, vec, vec, pl.BlockSpec((k, dm), lambda i: (0, 0))],
        out_specs=[tok, pl.BlockSpec((tb, k), lambda i: (i, 0)), vec, vec],
        out_shape=[jax.ShapeDtypeStruct((s, dm), F32), jax.ShapeDtypeStruct((s, k), F32),
                   jax.ShapeDtypeStruct((1, dm), F32), jax.ShapeDtypeStruct((1, dm), F32)],
        compiler_params=_params(("arbitrary",)),
    )(d_out, z, g, b, w)


def _proj_loss_tail(a, w, xres, g, b, target):
    s, dm = xres.shape
    k = a.shape[1]
    tb = min(ATT_TOKENS, s)

    def loss_fn(z, gg, bb, tgt):
        err = jnp.square(_layer_norm(z, gg, bb) - tgt)
        return 0.5 * jnp.sum(jnp.mean(err, axis=-1, keepdims=True), axis=0, keepdims=True)

    def body(a_ref, w_ref, x_ref, g_ref, b_ref, t_ref, loss_ref, dz_ref, dg_ref, db_ref):
        @pl.when(pl.program_id(0) == 0)
        def _():
            loss_ref[...] = jnp.zeros_like(loss_ref)
            dg_ref[...] = jnp.zeros_like(dg_ref)
            db_ref[...] = jnp.zeros_like(db_ref)

        halves = [slice(0, tb // 2), slice(tb // 2, tb)]
        zs = [ALPHA * x_ref[rows, :] + _nn_raw(a_ref[rows, :], w_ref[...]) for rows in halves]
        for rows, z in zip(halves, zs):
            tgt = t_ref[rows, :]
            loss, vjp = jax.vjp(lambda zz, gg, bb, tgt=tgt: loss_fn(zz, gg, bb, tgt), z, g_ref[...], b_ref[...])
            d_z, d_g, d_b = vjp(jnp.ones((1, 1), F32))
            loss_ref[...] += loss
            dz_ref[rows, :] = d_z
            dg_ref[...] += d_g
            db_ref[...] += d_b

    tok = pl.BlockSpec((tb, dm), lambda i: (i, 0))
    vec = pl.BlockSpec((1, dm), lambda i: (0, 0))
    one = pl.BlockSpec((1, 1), lambda i: (0, 0))
    return pl.pallas_call(
        body, name="ffn_down_loss_tail", grid=(s // tb,),
        in_specs=[pl.BlockSpec((tb, k), lambda i: (i, 0)), pl.BlockSpec((k, dm), lambda i: (0, 0)), tok, vec, vec, tok],
        out_specs=[one, tok, vec, vec],
        out_shape=[jax.ShapeDtypeStruct((1, 1), F32), jax.ShapeDtypeStruct((s, dm), F32),
                   jax.ShapeDtypeStruct((1, dm), F32), jax.ShapeDtypeStruct((1, dm), F32)],
        compiler_params=_params(("arbitrary",)),
    )(a, w, xres, g, b, target)


def _att_heads(qs, ks, vs):
    sc = [_nt(q, k) * (CA_DH ** -0.5) for q, k in zip(qs, ks)]
    p = [jax.nn.softmax(s, axis=-1) for s in sc]
    return tuple(_nn(pp, v) for pp, v in zip(p, vs))


def _head_slices(ref_or_value, offset):
    return tuple(ref_or_value[:, offset + h * CA_DH:offset + (h + 1) * CA_DH] for h in range(CA_HEADS))


def _cross_attention_fwd(x1, kv, wq, wo, g, b):
    s = x1.shape[0]
    tb = min(ATT_TOKENS, s)

    def body(x_ref, kv_ref, wq_ref, wo_ref, g_ref, b_ref, att_ref, z_ref, o_ref):
        x_blk = x_ref[...]
        q = _nn_raw(x_blk, wq_ref[...])
        att = jnp.concatenate(_att_heads(_head_slices(q, 0), _head_slices(kv_ref, 0), _head_slices(kv_ref, D_MODEL)),
                              axis=1)
        att_ref[...] = att.astype(att_ref.dtype)
        z = ALPHA * x_blk + _nn_raw(att, wo_ref[...])
        z_ref[...] = z
        o_ref[...] = _layer_norm(z, g_ref[...], b_ref[...])

    tok = pl.BlockSpec((tb, D_MODEL), lambda i: (i, 0))
    mat = pl.BlockSpec((D_MODEL, D_MODEL), lambda i: (0, 0))
    vec = pl.BlockSpec((1, D_MODEL), lambda i: (0, 0))
    act = jax.ShapeDtypeStruct((s, D_MODEL), F32)
    return pl.pallas_call(
        body, name="cross_attention_fwd", grid=(s // tb,),
        in_specs=[tok, pl.BlockSpec((N_MEM, 2 * D_MODEL), lambda i: (0, 0)), mat, mat, vec, vec],
        out_specs=[tok, tok, tok],
        out_shape=[jax.ShapeDtypeStruct((s, D_MODEL), BF16), act, act],
        compiler_params=_params(("parallel",)),
    )(x1, kv, wq, wo, g, b)


def _cross_attention_bwd(d_x2, x1, z2, kv, wq, wo, g, b):
    s = x1.shape[0]
    tb = min(ATT_TOKENS, s)

    def body(dx2_ref, x_ref, z_ref, kv_ref, wq_ref, wo_ref, g_ref, b_ref,
             dx1_ref, dq_ref, dz_ref, dkv_ref, dg_ref, db_ref):
        @pl.when(pl.program_id(0) == 0)
        def _():
            dkv_ref[...] = jnp.zeros_like(dkv_ref)
            dg_ref[...] = jnp.zeros_like(dg_ref)
            db_ref[...] = jnp.zeros_like(db_ref)

        _, ln_vjp = jax.vjp(_layer_norm, z_ref[...], g_ref[...], b_ref[...])
        d_z, d_g, d_b = ln_vjp(dx2_ref[...])
        dg_ref[...] += d_g
        db_ref[...] += d_b
        dz_ref[...] = d_z.astype(dz_ref.dtype)
        d_att = _nt_raw(d_z, wo_ref[...])
        q = _nn_raw(x_ref[...], wq_ref[...])
        _, vjp = jax.vjp(_att_heads, _head_slices(q, 0), _head_slices(kv_ref, 0), _head_slices(kv_ref, D_MODEL))
        d_qs, d_ks, d_vs = vjp(_head_slices(d_att, 0))
        for h in range(CA_HEADS):
            lo = h * CA_DH
            dkv_ref[:, lo:lo + CA_DH] += d_ks[h]
            dkv_ref[:, D_MODEL + lo:D_MODEL + lo + CA_DH] += d_vs[h]
        d_q = jnp.concatenate(d_qs, axis=1)
        dq_ref[...] = d_q.astype(dq_ref.dtype)
        dx1_ref[...] = ALPHA * d_z + _nt_raw(d_q, wq_ref[...])

    tok = pl.BlockSpec((tb, D_MODEL), lambda i: (i, 0))
    mem = pl.BlockSpec((N_MEM, 2 * D_MODEL), lambda i: (0, 0))
    mat = pl.BlockSpec((D_MODEL, D_MODEL), lambda i: (0, 0))
    vec = pl.BlockSpec((1, D_MODEL), lambda i: (0, 0))
    low = jax.ShapeDtypeStruct((s, D_MODEL), BF16)
    return pl.pallas_call(
        body, name="cross_attention_bwd", grid=(s // tb,),
        in_specs=[tok, tok, tok, mem, mat, mat, vec, vec], out_specs=[tok, tok, tok, mem, vec, vec],
        out_shape=[jax.ShapeDtypeStruct((s, D_MODEL), F32), low, low,
                   jax.ShapeDtypeStruct((N_MEM, 2 * D_MODEL), F32),
                   jax.ShapeDtypeStruct((1, D_MODEL), F32), jax.ShapeDtypeStruct((1, D_MODEL), F32)],
        compiler_params=_params(("arbitrary",)),
    )(d_x2, x1, z2, kv, wq, wo, g, b)


def _local_step(x, mem, target, w, mid_weights=None, ffn_weights=None, down_weights=None, on_ffn_grads=None,
                on_mid_grads=None,
                on_small_grads=None, on_last_grads=None):
    w = dict(w)
    s = x.shape[0]
    tm = min(512, s)
    tt_big = min(1024, s)
    proj = _matmul_nn(x, w["w_in_main"], w["b_in_main"], min(2048, s), 512, "proj")
    gates = _matmul_nn(x, w["w_in_gate"], w["b_in_gate"], tm, LANES, "proj_gates")
    qk = _ml_conv_fwd(proj, w["ml_conv_w"], w["ml_conv_b"])
    y, hg_states = _hgrn2_fwd(proj, w["hg_lb_logits"], w["hg_norm_w"])
    y, ct_s, n_s, m_s = _mlstm_fwd(qk, proj, gates, w["ml_norm_w"], y)
    if mid_weights is not None:
        w.update(mid_weights(y))
    z1, x1 = _proj_res_ln(y, w["w_out"], x, w["ln1_g"], w["ln1_b"], "out_proj_ln1")
    kv = _matmul_nn(mem, w["ca_wkv"], None, N_MEM, CA_DH, "kv")
    att, z2, x2 = _cross_attention_fwd(x1, kv, w["ca_wq"], w["ca_wo"], w["ln2_g"], w["ln2_b"])
    if ffn_weights is not None:
        w.update(ffn_weights(x2))
    u = _matmul_nn(x2, w["ffn_w_up"], None, min(2048, s), UP_SHARD_P, "ffn_up", BF16)
    hid = _ffn_conv_fwd(u, w["ffn_conv_w"], w["ffn_conv_b"])
    if down_weights is not None:
        w.update(down_weights(hid))
    loss, d_z3, d_ln3_g, d_ln3_b = _proj_loss_tail(hid, w["ffn_w_down"], x2, w["ln3_g"], w["ln3_b"], target)
    grads = {"ln3_g": d_ln3_g, "ln3_b": d_ln3_b}
    grads["ffn_w_down"] = _matmul_tn(hid, d_z3, 1536, D_MODEL, tt_big, "d_w_down")
    d_hid = _matmul_nt([(d_z3, w["ffn_w_down"])], None, 1.0, tm, D_FF_P, "d_hid", BF16)
    d_ug, d_uv, d_cwg, d_cwv, d_cbg, d_cbv = _ffn_conv_bwd(u, w["ffn_conv_w"], w["ffn_conv_b"], d_hid)
    grads["ffn_conv_w"] = jnp.concatenate([d_cwg, d_cwv], axis=-1)
    grads["ffn_conv_b"] = jnp.concatenate([d_cbg, d_cbv], axis=-1)
    half = N_DEV // 2
    d_w_up = _matmul_tn(x2, d_ug, D_MODEL, UP_SHARD_P, tt_big, "d_w_up_gate", shards=N_DEV, group=half)
    grads["ffn_w_up"] = _matmul_tn(x2, d_uv, D_MODEL, UP_SHARD_P, tt_big, "d_w_up_val", shards=N_DEV,
                                   shard0=half, group=half, into=d_w_up)
    d_x2 = _matmul_nt([(d_ug, w["ffn_w_up"], 0), (d_uv, w["ffn_w_up"], N_DEV // 2)], d_z3, ALPHA,
                      min(256, s), D_MODEL, "d_x2")
    if on_ffn_grads is not None:
        d_x2 = on_ffn_grads(grads, d_x2)
    d_x1, d_q, d_z2, d_kv, grads["ln2_g"], grads["ln2_b"] = _cross_attention_bwd(
        d_x2, x1, z2, kv, w["ca_wq"], w["ca_wo"], w["ln2_g"], w["ln2_b"])
    grads["ca_wo"] = _matmul_tn(att, d_z2, D_MODEL, D_MODEL, tt_big, "d_ca_wo")
    grads["ca_wq"] = _matmul_tn(x1, d_q, D_MODEL, D_MODEL, tt_big, "d_ca_wq")
    grads["ca_wkv"] = _matmul_tn(mem, d_kv, D_MODEL, CA_DH, N_MEM, "d_ca_wkv", shards=N_DEV, group=N_DEV)
    d_z1, d_y, grads["ln1_g"], grads["ln1_b"] = _ln_bwd_proj(d_x1, z1, w["ln1_g"], w["ln1_b"], w["w_out"],
                                                             "ln1_bwd_out_proj")
    grads["w_out"] = _matmul_tn(y, d_z1, D_MODEL, D_MODEL, tt_big, "d_w_out")
    if on_mid_grads is not None:
        d_y = on_mid_grads(grads, d_y)
    d_proj, grads["hg_lb_logits"], grads["hg_norm_w"], db_hg = _hgrn2_bwd(
        proj, w["hg_lb_logits"], w["hg_norm_w"], hg_states, d_y)
    d_proj, d_qk, d_gates, grads["ml_norm_w"], db_vo = _mlstm_bwd(
        qk, proj, gates, w["ml_norm_w"], ct_s, n_s, m_s, d_y, d_proj)
    d_proj, grads["ml_conv_w"], grads["ml_conv_b"], db_qk = _ml_conv_bwd(
        proj, w["ml_conv_w"], w["ml_conv_b"], d_qk, d_proj)
    grads["b_in_main"] = jnp.concatenate([db_hg, db_qk, db_vo], axis=-1)
    grads["w_in_gate"], grads["b_in_gate"] = _matmul_tn(x, d_gates, D_MODEL, LANES, tt_big, "d_w_in_gates",
                                                        colsum=True)
    if on_small_grads is not None:
        d_proj = on_small_grads(grads, loss, d_proj)
    grads["w_in_main"] = _matmul_tn(x, d_proj, D_MODEL, min(2048, D_IN_MAIN), tt_big, "d_w_in")
    if on_last_grads is not None:
        d_z1 = on_last_grads(grads, d_z1)
    grad_x = _matmul_nt([(d_proj, w["w_in_main"]), (d_gates, w["w_in_gate"])], d_z1, ALPHA, tm, D_MODEL, "d_x")
    return loss, grad_x, grads


HBM_SPEC = pl.BlockSpec(memory_space=pltpu.HBM)


def _coords():
    return lax.axis_index("x"), lax.axis_index("y"), lax.axis_index("c")


def _other_chips(x, y):
    return [(1 - x, y), (x, 1 - y), (1 - x, 1 - y)]


def _my_slot():
    x, y, c = _coords()
    return 4 * x + 2 * y + c


SEM_SPEC = pl.BlockSpec(memory_space=pltpu.SEMAPHORE)
ANY_SPEC = pl.BlockSpec(memory_space=pl.ANY)
SIDE_EFFECT = pltpu.SideEffectType.DATAFLOW_SIDE_EFFECTING


def _peer(x, y, c, d):
    flip = lambda v, bit: 1 - v if bit else v
    p = (flip(x, d & 4), flip(y, d & 2), flip(c, d & 1))
    return p, 4 * p[0] + 2 * p[1] + p[2]


def _direct_copies(gather, src_refs, land_refs, send_sems, recv_sems):
    x, y, c = _coords()
    me = 4 * x + 2 * y + c
    copies = []
    for a in range(len(src_refs)):
        for d in range(1, N_DEV):
            peer, peer_slot = _peer(x, y, c, d)
            copies.append(pltpu.make_async_remote_copy(
                src_ref=src_refs[a] if gather else src_refs[a].at[peer_slot],
                dst_ref=land_refs[a].at[me] if gather else land_refs[a].at[d - 1],
                send_sem=send_sems.at[7 * a + d - 1], recv_sem=recv_sems.at[7 * a + d - 1],
                device_id=peer, device_id_type=MESH))
    return copies


def _hbm(t):
    return pltpu.HBM(t.shape, t.dtype)


def _chip_copies(src_refs, land_refs, send_sems, recv_sems):
    x, y, c = _coords()
    me = 4 * x + 2 * y + c
    targets = [(x, y, 1 - c)] + [(cx, cy, c) for cx, cy in _other_chips(x, y)]
    return [pltpu.make_async_remote_copy(
        src_ref=src_refs[a], dst_ref=land_refs[a].at[me], send_sem=send_sems.at[4 * a + k],
        recv_sem=recv_sems.at[4 * a + k], device_id=target, device_id_type=MESH)
        for a in range(len(src_refs)) for k, target in enumerate(targets)]


def _forward_copies(land_refs, send_sems, recv_sems):
    x, y, c = _coords()
    return [pltpu.make_async_remote_copy(
        src_ref=land_refs[a].at[4 * cx + 2 * cy + c], dst_ref=land_refs[a].at[4 * cx + 2 * cy + c],
        send_sem=send_sems.at[3 * a + j], recv_sem=recv_sems.at[3 * a + j],
        device_id=(x, y, 1 - c), device_id_type=MESH)
        for a in range(len(land_refs)) for j, (cx, cy) in enumerate(_other_chips(x, y))]


def _split_copy_start(make_copies, n_sems, operands, through, name):
    n_ops = len(operands)

    def body(*refs):
        for cp in make_copies(refs[:n_ops], refs[n_ops + 1], refs[n_ops + 2]):
            cp.start()

    ins = [pltpu.with_memory_space_constraint(t, pltpu.HBM) for t in (*operands, through)]
    sems = pltpu.SemaphoreType.DMA((n_sems,))
    res = pl.pallas_call(
        body, name=name, out_shape=(sems, sems, *[_hbm(t) for t in ins]),
        in_specs=[HBM_SPEC] * (n_ops + 1), out_specs=(SEM_SPEC, SEM_SPEC, *[HBM_SPEC] * (n_ops + 1)),
        input_output_aliases={i: 2 + i for i in range(n_ops + 1)},
        compiler_params=pltpu.CompilerParams(has_side_effects=SIDE_EFFECT),
    )(*ins)
    return (res[0], res[1], list(res[2:2 + n_ops])), res[2 + n_ops]


def _split_copy_wait(make_copies, started, after, name):
    send_sems, recv_sems, operands = started
    n_ops = len(operands)
    after = list(after) if isinstance(after, (list, tuple)) else [after]

    def body(*refs):
        for cp in make_copies(refs[:n_ops], refs[n_ops], refs[n_ops + 1]):
            cp.wait_send()
            cp.wait_recv()

    res = pl.pallas_call(
        body, name=name, out_shape=tuple(_hbm(t) for t in operands),
        in_specs=[HBM_SPEC] * n_ops + [SEM_SPEC, SEM_SPEC] + [ANY_SPEC] * len(after),
        out_specs=tuple([HBM_SPEC] * n_ops), input_output_aliases={i: i for i in range(n_ops)},
        compiler_params=pltpu.CompilerParams(has_side_effects=SIDE_EFFECT),
    )(*operands, send_sems, recv_sems, *after)
    return list(res)


def _halves(make_copies, na):
    return lambda refs, send_sems, recv_sems: make_copies(refs[:na], refs[na:], send_sems, recv_sems)


def _direct_start(gather, arrays, through, name):
    na = len(arrays)
    lands = [lax.empty((N_DEV,) + t.shape if gather else (N_DEV - 1,) + t.shape[1:], t.dtype) for t in arrays]
    return _split_copy_start(_halves(functools.partial(_direct_copies, gather), na), 7 * na, [*arrays, *lands],
                             through, name)


def _direct_wait(gather, started, after, name):
    na = len(started[2]) // 2
    operands = _split_copy_wait(_halves(functools.partial(_direct_copies, gather), na), started, after, name)
    return operands[:na], operands[na:]


def _two_level_gather(shards, glue, name):
    na = len(shards)
    lands = [lax.empty((N_DEV,) + t.shape, t.dtype) for t in shards]
    nothing = jnp.zeros((SUBLANES, LANES), F32)
    started, _ = _split_copy_start(_halves(_chip_copies, na), 4 * na, [*shards, *lands], nothing, name + "_start")
    operands = _split_copy_wait(_halves(_chip_copies, na), started, glue, name + "_wait")
    started, mine = _split_copy_start(_forward_copies, 3 * na, operands[na:], operands[0], name + "_forward_start")
    lands = _split_copy_wait(_forward_copies, started, mine, name + "_forward_wait")
    return [lax.dynamic_update_index_in_dim(land, own, _my_slot(), 0)
            for own, land in zip([mine, *operands[1:na]], lands)]


def _row_tile(rows):
    for t in (256, 176, 128):
        if rows % t == 0 and rows > t:
            return t
    return rows


def _adamw_math(g, w, m, v):
    m_new = ADAM_B1 * m + (1.0 - ADAM_B1) * g
    v_new = ADAM_B2 * v + (1.0 - ADAM_B2) * jnp.square(g)
    m_hat = m_new / (1.0 - ADAM_B1 ** ADAM_STEP)
    v_hat = v_new / (1.0 - ADAM_B2 ** ADAM_STEP)
    delta = -ADAM_LR * (m_hat / (jnp.sqrt(v_hat) + ADAM_EPS) + ADAM_WD * w)
    return delta, m_new, v_new


def _adamw_sharded(chip, sums, got, w, m, v, name):
    r, c = w.shape
    tr = _row_tile(r)
    n_got = got.shape[0]

    def body(chip_ref, s_ref, g_ref, w_ref, m_ref, v_ref, go_ref, d_ref, nm_ref, nv_ref):
        g = s_ref[...].astype(F32)
        for i in range(n_got):
            g = g + g_ref[i].astype(F32)
        delta, m_new, v_new = _adamw_math(g, w_ref[...], m_ref[...], v_ref[...])
        go_ref[...] = g
        d_ref[...] = delta
        nm_ref[...] = m_new
        nv_ref[...] = v_new

    blk = pl.BlockSpec((tr, c), lambda i, chip_ref: (i, 0))
    out = jax.ShapeDtypeStruct((r, c), F32)
    return pl.pallas_call(
        body, name=name,
        grid_spec=pltpu.PrefetchScalarGridSpec(
            num_scalar_prefetch=1, grid=(r // tr,),
            in_specs=[pl.BlockSpec((None, tr, c), lambda i, chip_ref: (chip_ref[0], i, 0)),
                      pl.BlockSpec((n_got, tr, c), lambda i, chip_ref: (0, i, 0)), blk, blk, blk],
            out_specs=[blk, blk, blk, blk]),
        out_shape=[out, out, out, out],
        compiler_params=_params(("parallel",)),
    )(chip, sums, got, w, m, v)


def _adamw_replicated(parts, w, m, v):
    p, r, c = parts.shape
    names = SMALL_NAMES
    shapes = [w[n].shape for n in names]

    def body(*refs):
        p_ref = refs[0]
        ins = refs[1:1 + 3 * len(names)]
        outs = refs[1 + 3 * len(names):-2]
        loss_ref, sum_scr = refs[-2], refs[-1]
        total = p_ref[0]
        for i in range(1, p):
            total = total + p_ref[i]
        sum_scr[...] = total
        for k, n in enumerate(names):
            w_ref, m_ref, v_ref = ins[3 * k:3 * k + 3]
            g_ref, d_ref, nm_ref, nv_ref = outs[4 * k:4 * k + 4]
            for row, lane0, width, src_row in _small_pieces(n, shapes[k]):
                here = (slice(row, row + 1), slice(lane0, lane0 + width))
                g = sum_scr[src_row:src_row + 1, 0:width]
                delta, m_new, v_new = _adamw_math(g, w_ref[here], m_ref[here], v_ref[here])
                g_ref[here] = g
                d_ref[here] = delta
                nm_ref[here] = m_new
                nv_ref[here] = v_new
        loss_ref[...] = sum_scr[SMALL_LOSS_ROW:SMALL_LOSS_ROW + 1, 0:1]

    whole = lambda shape: pl.BlockSpec(shape, lambda i: (0,) * len(shape))
    args = [parts] + [t[n] for n in names for t in (w, m, v)]
    out_shape = [jax.ShapeDtypeStruct(s, F32) for s in shapes for _ in range(4)] + [jax.ShapeDtypeStruct((1, 1), F32)]
    res = pl.pallas_call(
        body, name="adamw_replicated", grid=(1,),
        in_specs=[whole(t.shape) for t in args], out_specs=[whole(s.shape) for s in out_shape],
        out_shape=out_shape, scratch_shapes=[pltpu.VMEM((r, c), F32)],
        compiler_params=_params(("arbitrary",)),
    )(*args)
    results = [{n: res[4 * k + j] for k, n in enumerate(names)} for j in range(4)]
    return results, res[-1]


SHARDED_NAMES = ("w_in", "ml_conv_w", "w_out", "ca_wq", "ca_wkv", "ca_wo", "ffn_w_up", "ffn_conv_w", "ffn_w_down")
SMALL_NAMES = ("b_in", "hg_lb_logits", "hg_norm_w", "ml_conv_b", "ml_norm_w", "ln1_g", "ln1_b",
               "ln2_g", "ln2_b", "ffn_conv_b", "ln3_g", "ln3_b")
WEIGHT_NAMES = ("w_in", "b_in", "hg_lb_logits", "hg_norm_w", "ml_conv_w", "ml_conv_b", "ml_norm_w", "w_out",
                "ln1_g", "ln1_b", "ca_wq", "ca_wkv", "ca_wo", "ln2_g", "ln2_b", "ffn_w_up", "ffn_conv_w",
                "ffn_conv_b", "ffn_w_down", "ln3_g", "ln3_b")
PAD_TO = {"ffn_w_up": UP_SHARD_P, "ffn_conv_w": UP_SHARD_P}
SMALL_ROWS = 24
SMALL_W = D_MODEL
SMALL_SHAPES = {"b_in": (1, D_IN), "hg_lb_logits": (2, D_GROUP), "hg_norm_w": (1, D_GROUP),
                "ml_conv_b": (1, 2 * D_GROUP), "ml_norm_w": (1, D_GROUP), "ln1_g": (1, D_MODEL), "ln1_b": (1, D_MODEL),
                "ln2_g": (1, D_MODEL), "ln2_b": (1, D_MODEL), "ffn_conv_b": (1, D_UP), "ln3_g": (1, D_MODEL),
                "ln3_b": (1, D_MODEL)}


def _shard_2d(name, block):
    t = block[0]
    if name in PAD_TO:
        t = jnp.pad(t, ((0, 0), (0, PAD_TO[name] - t.shape[1])))
    return t


def _shard_like(name, t, like):
    return t[:, :like.shape[2]][None]


def _pad_cols(t, width):
    return jnp.pad(t, ((0, 0), (0, width - t.shape[1])))


FIRST_NAMES = ("w_in", "ml_conv_w")
FFN_NAMES = ("ffn_w_up", "ffn_w_down", "ffn_conv_w")
MID_NAMES = ("ca_wo", "ca_wq", "ca_wkv", "w_out")


def _first_weights(g, small):
    w = dict(small)
    last = g["w_in"][N_DEV - 1]
    split = D_IN_MAIN - (N_DEV - 1) * W_IN_SHARD
    w["w_in_main"] = jnp.concatenate([*[g["w_in"][j] for j in range(N_DEV - 1)], last[:, :split]], axis=1)
    w["w_in_gate"] = _pad_cols(last[:, split:], LANES)
    w["b_in_main"] = small["b_in"][:, :D_IN_MAIN]
    w["b_in_gate"] = _pad_cols(small["b_in"][:, D_IN_MAIN:], LANES)
    w["ml_conv_w"] = jnp.transpose(g["ml_conv_w"], (1, 0, 2)).reshape(ML_CONV, 2 * D_GROUP)
    return w


def _mid_weights(g):
    w = {n: g[n].reshape(D_MODEL, D_MODEL) for n in ("w_out", "ca_wq", "ca_wo")}
    w["ca_wkv"] = g["ca_wkv"]
    return w


FFN_UP_NAMES = ("ffn_w_up", "ffn_conv_w")
FFN_DOWN_NAMES = ("ffn_w_down",)


def _ffn_up_weights(g, small):
    w = {"ffn_w_up": g["ffn_w_up"]}
    w["ffn_conv_w"] = jnp.transpose(g["ffn_conv_w"], (1, 0, 2)).reshape(FFN_CONV, D_UP_P)
    w["ffn_conv_b"] = _pad_cols(small["ffn_conv_b"].reshape(N_DEV, UP_SHARD), UP_SHARD_P).reshape(1, D_UP_P)
    return w


def _ffn_down_weights(g):
    down = g["ffn_w_down"].reshape(N_DEV // 2, UP_SHARD, D_MODEL)
    return {"ffn_w_down": jnp.pad(down, ((0, 0), (0, UP_SHARD_P - UP_SHARD), (0, 0))).reshape(D_FF_P, D_MODEL)}


def _whole_weights(g, small):
    return {**_first_weights(g, small), **_mid_weights(g), **_ffn_up_weights(g, small), **_ffn_down_weights(g)}


def _owner_stack(n, grads):
    if n == "w_in":
        main, gate = grads["w_in_main"], grads["w_in_gate"][:, :D_IN - D_IN_MAIN]
        last = jnp.concatenate([main[:, (N_DEV - 1) * W_IN_SHARD:], gate], axis=1)
        return jnp.stack([*[main[:, j * W_IN_SHARD:(j + 1) * W_IN_SHARD] for j in range(N_DEV - 1)], last])
    if n in ("w_out", "ca_wq", "ca_wo"):
        return grads[n].reshape(N_DEV, D_MODEL // N_DEV, D_MODEL)
    if n == "ffn_w_down":
        down = grads[n].reshape(N_DEV // 2, UP_SHARD_P, D_MODEL)[:, :UP_SHARD]
        return down.reshape(N_DEV, D_FF // N_DEV, D_MODEL)
    if n == "ml_conv_w":
        return jnp.transpose(grads[n].reshape(ML_CONV, N_DEV, LANES), (1, 0, 2))
    if n == "ffn_conv_w":
        return jnp.transpose(grads[n].reshape(FFN_CONV, N_DEV, UP_SHARD_P), (1, 0, 2))
    return grads[n]


def _owner_stacks(grads):
    return {n: _owner_stack(n, grads) for n in SHARDED_NAMES}


def _small_grads(grads):
    out = {n: grads[n] for n in SMALL_NAMES if n in grads}
    out["b_in"] = jnp.concatenate([grads["b_in_main"], grads["b_in_gate"][:, :D_IN - D_IN_MAIN]], axis=1)
    out["ffn_conv_b"] = grads["ffn_conv_b"].reshape(N_DEV, UP_SHARD_P)[:, :UP_SHARD].reshape(1, D_UP)
    return out


def _small_rows(shape):
    return shape[0] if shape[1] <= SMALL_W else -(-shape[1] // SMALL_W)


SMALL_BASE = {n: sum(_small_rows(SMALL_SHAPES[k]) for k in SMALL_NAMES[:i]) for i, n in enumerate(SMALL_NAMES)}
SMALL_LOSS_ROW = sum(_small_rows(SMALL_SHAPES[n]) for n in SMALL_NAMES)
assert SMALL_LOSS_ROW < SMALL_ROWS


def _small_pieces(name, shape):
    base = SMALL_BASE[name]
    if shape[1] <= SMALL_W:
        return [(i, 0, shape[1], base + i) for i in range(shape[0])]
    return [(0, k * SMALL_W, min(SMALL_W, shape[1] - k * SMALL_W), base + k) for k in range(_small_rows(shape))]


def _pack_small(p, loss):
    rows = []
    for n in SMALL_NAMES:
        t = p[n]
        nrows = _small_rows(t.shape)
        if t.shape[1] <= SMALL_W:
            rows.append(_pad_cols(t, SMALL_W))
        else:
            rows.append(_pad_cols(t, nrows * SMALL_W).reshape(nrows, SMALL_W))
    rows.append(_pad_cols(loss, SMALL_W))
    slab = jnp.concatenate(rows, axis=0)
    return jnp.pad(slab, ((0, SMALL_ROWS - slab.shape[0]), (0, 0)))


def kernel(x, mem, w_in, b_in, hg_lb_logits, hg_norm_w, ml_conv_w, ml_conv_b, ml_norm_w, w_out, ln1_g, ln1_b, ca_wq, ca_wkv, ca_wo, ln2_g, ln2_b, ffn_w_up, ffn_conv_w, ffn_conv_b, ffn_w_down, ln3_g, ln3_b, loss_target, m_w_in, m_b_in, m_hg_lb_logits, m_hg_norm_w, m_ml_conv_w, m_ml_conv_b, m_ml_norm_w, m_w_out, m_ln1_g, m_ln1_b, m_ca_wq, m_ca_wkv, m_ca_wo, m_ln2_g, m_ln2_b, m_ffn_w_up, m_ffn_conv_w, m_ffn_conv_b, m_ffn_w_down, m_ln3_g, m_ln3_b, v_w_in, v_b_in, v_hg_lb_logits, v_hg_norm_w, v_ml_conv_w, v_ml_conv_b, v_ml_norm_w, v_w_out, v_ln1_g, v_ln1_b, v_ca_wq, v_ca_wkv, v_ca_wo, v_ln2_g, v_ln2_b, v_ffn_w_up, v_ffn_conv_w, v_ffn_conv_b, v_ffn_w_down, v_ln3_g, v_ln3_b):
    params = dict(w_in=w_in, b_in=b_in, hg_lb_logits=hg_lb_logits, hg_norm_w=hg_norm_w, ml_conv_w=ml_conv_w,
                  ml_conv_b=ml_conv_b, ml_norm_w=ml_norm_w, w_out=w_out, ln1_g=ln1_g, ln1_b=ln1_b, ca_wq=ca_wq,
                  ca_wkv=ca_wkv, ca_wo=ca_wo, ln2_g=ln2_g, ln2_b=ln2_b, ffn_w_up=ffn_w_up, ffn_conv_w=ffn_conv_w,
                  ffn_conv_b=ffn_conv_b, ffn_w_down=ffn_w_down, ln3_g=ln3_g, ln3_b=ln3_b)
    mom1 = dict(w_in=m_w_in, b_in=m_b_in, hg_lb_logits=m_hg_lb_logits, hg_norm_w=m_hg_norm_w,
                ml_conv_w=m_ml_conv_w, ml_conv_b=m_ml_conv_b, ml_norm_w=m_ml_norm_w, w_out=m_w_out, ln1_g=m_ln1_g,
                ln1_b=m_ln1_b, ca_wq=m_ca_wq, ca_wkv=m_ca_wkv, ca_wo=m_ca_wo, ln2_g=m_ln2_g, ln2_b=m_ln2_b,
                ffn_w_up=m_ffn_w_up, ffn_conv_w=m_ffn_conv_w, ffn_conv_b=m_ffn_conv_b, ffn_w_down=m_ffn_w_down,
                ln3_g=m_ln3_g, ln3_b=m_ln3_b)
    mom2 = dict(w_in=v_w_in, b_in=v_b_in, hg_lb_logits=v_hg_lb_logits, hg_norm_w=v_hg_norm_w,
                ml_conv_w=v_ml_conv_w, ml_conv_b=v_ml_conv_b, ml_norm_w=v_ml_norm_w, w_out=v_w_out, ln1_g=v_ln1_g,
                ln1_b=v_ln1_b, ca_wq=v_ca_wq, ca_wkv=v_ca_wkv, ca_wo=v_ca_wo, ln2_g=v_ln2_g, ln2_b=v_ln2_b,
                ffn_w_up=v_ffn_w_up, ffn_conv_w=v_ffn_conv_w, ffn_conv_b=v_ffn_conv_b, ffn_w_down=v_ffn_w_down,
                ln3_g=v_ln3_g, ln3_b=v_ln3_b)

    x_idx, y_idx, c_idx = _coords()
    as_index = lambda v: jnp.reshape(v, (1,)).astype(jnp.int32)
    me = as_index(4 * x_idx + 2 * y_idx + c_idx)
    small_params = {n: params[n] for n in SMALL_NAMES}

    shards = {n: _shard_2d(n, params[n]) for n in SHARDED_NAMES}
    m_shards = {n: _shard_2d(n, mom1[n]) for n in SHARDED_NAMES}
    v_shards = {n: _shard_2d(n, mom2[n]) for n in SHARDED_NAMES}
    outgoing = {n: shards[n] if "conv" in n else shards[n].astype(BF16) for n in SHARDED_NAMES}
    to_send = lambda names: [outgoing[n] for n in names]
    glue = [*m_shards.values(), *v_shards.values(), *shards.values(),
            *[outgoing[n] for n in SHARDED_NAMES if n not in FIRST_NAMES]]
    first = dict(zip(FIRST_NAMES, _two_level_gather(to_send(FIRST_NAMES), glue, "weights_gather_first")))
    mid_started, through = _direct_start(True, to_send(MID_NAMES), first["w_in"], "weights_gather_start_mid")
    ffn_started, through = _direct_start(True, to_send(FFN_UP_NAMES), through, "weights_gather_start_ffn_up")
    down_started, first["w_in"] = _direct_start(True, to_send(FFN_DOWN_NAMES), through,
                                                "weights_gather_start_ffn_down")

    def gathered_weights(names, started, after, tag):
        mine, lands = _direct_wait(True, started, after, "weights_gather_wait_" + tag)
        return {n: lax.dynamic_update_index_in_dim(land, own, me[0], 0) for n, own, land in zip(names, mine, lands)}

    started, own_stacks = {}, {}

    def start_group(names, tag):
        def hook(grads, through):
            own_stacks[tag] = [_owner_stack(n, grads).astype(BF16) for n in names]
            started[tag], through = _direct_start(False, own_stacks[tag], through, "grads_start_" + tag)
            return through
        return hook

    def start_small(grads, loss, through):
        started["small"], through = _direct_start(True, [_pack_small(_small_grads(grads), loss)], through,
                                                  "small_gather_start")
        return through

    loss, grad_x, grads = _local_step(
        x[0], mem[0], loss_target[0], _first_weights(first, small_params),
        lambda y: _mid_weights(gathered_weights(MID_NAMES, mid_started, y, "mid")),
        lambda x2: _ffn_up_weights(gathered_weights(FFN_UP_NAMES, ffn_started, x2, "ffn_up"), small_params),
        lambda hid: _ffn_down_weights(gathered_weights(FFN_DOWN_NAMES, down_started, hid, "ffn_down")),
        start_group(FFN_NAMES, "ffn"), start_group(MID_NAMES, "mid"), start_small, start_group(FIRST_NAMES, "last"))

    sharded_out = {}

    def update_group(names, tag, after):
        _, lands = _direct_wait(False, started[tag], after, "grads_wait_" + tag)
        for n, st, land in zip(names, own_stacks[tag], lands):
            res = _adamw_sharded(me, st, land, shards[n], m_shards[n], v_shards[n], "adamw_" + n)
            sharded_out[n] = [_shard_like(n, t, params[n]) for t in res]

    update_group(FFN_NAMES, "ffn", grad_x)
    update_group(MID_NAMES, "mid", grad_x)
    own_small, small_lands = _direct_wait(True, started["small"], grad_x, "small_gather_wait")
    small_parts = lax.dynamic_update_index_in_dim(small_lands[0], own_small[0], me[0], 0)
    small_out, total_loss = _adamw_replicated(small_parts, small_params, {n: mom1[n] for n in SMALL_NAMES},
                                              {n: mom2[n] for n in SMALL_NAMES})
    done = [t for n in FFN_NAMES + MID_NAMES for t in sharded_out[n]]
    done += [t for small in small_out for t in small.values()]
    update_group(FIRST_NAMES, "last", done)

    outs = []
    for k, small in enumerate(small_out):
        outs.extend(sharded_out[n][k] if n in sharded_out else small[n] for n in WEIGHT_NAMES)
    return (total_loss[0, 0], grad_x[None], *outs)
```

```python
import functools
import math

import jax
import jax.numpy as jnp
from jax import lax
from jax.experimental import pallas as pl
from jax.experimental.pallas import tpu as pltpu

F32 = jnp.float32
BF16 = jnp.bfloat16
HIGHEST = lax.Precision.HIGHEST
MESH = pl.DeviceIdType.MESH

N_DEV = 8
D_MODEL = 1024
N_MEM = 256
N_HEADS = 4
D_HEAD = 128
D_GROUP = N_HEADS * D_HEAD
CHUNK = 64
ML_CONV = 4
FFN_CONV = 3
D_FF = 2816
D_UP = 2 * D_FF
CA_HEADS = 4
CA_DH = D_MODEL // CA_HEADS
LANES = 128
SUBLANES = 8
D_IN = 8 * D_GROUP + 2 * N_HEADS
D_IN_MAIN = 8 * D_GROUP
W_IN_SHARD = D_IN // N_DEV
UP_SHARD = D_UP // N_DEV
UP_SHARD_P = 768
D_UP_P = N_DEV * UP_SHARD_P
D_FF_P = D_UP_P // 2
ALPHA = 2.0 ** 0.25
LN_EPS = 1e-5
NEG_BIG = -1e30
ADAM_LR = 0.001
ADAM_B1 = 0.9
ADAM_B2 = 0.999
ADAM_EPS = 1e-08
ADAM_WD = 0.01
ADAM_STEP = 10
VMEM_LIMIT = 56 * 1024 * 1024

SEG_HQ, SEG_HF, SEG_HI, SEG_HG, SEG_MQ, SEG_MK, SEG_MV, SEG_MO = (4 * i for i in range(8))


def _params(sem):
    return pltpu.CompilerParams(dimension_semantics=sem, vmem_limit_bytes=VMEM_LIMIT)


def _dg(a, b, ca, cb, precision=None):
    return lax.dot_general(a, b, (((ca,), (cb,)), ((), ())), precision=precision,
                           preferred_element_type=F32)


def _nn_raw(a, b):
    return _dg(a.astype(BF16), b.astype(BF16), 1, 0)


def _nt_raw(a, b):
    return _dg(a.astype(BF16), b.astype(BF16), 1, 1)


def _tn_raw(a, b):
    return _dg(a.astype(BF16), b.astype(BF16), 0, 0)


@jax.custom_vjp
def _nn(a, b):
    return _nn_raw(a, b)


_nn.defvjp(lambda a, b: (_nn_raw(a, b), (a, b)),
           lambda res, g: (_nt_raw(g, res[1]), _tn_raw(res[0], g)))


@jax.custom_vjp
def _nt(a, b):
    return _nt_raw(a, b)


_nt.defvjp(lambda a, b: (_nt_raw(a, b), (a, b)),
           lambda res, g: (_nn_raw(g, res[1]), _tn_raw(g, res[0])))


@jax.custom_vjp
def _tn(a, b):
    return _tn_raw(a, b)


_tn.defvjp(lambda a, b: (_tn_raw(a, b), (a, b)),
           lambda res, g: (_nt_raw(res[1], g), _nn_raw(res[0], g)))


def _layer_norm(z, g, b):
    mu = jnp.mean(z, axis=-1, keepdims=True)
    var = jnp.mean(jnp.square(z - mu), axis=-1, keepdims=True)
    return (z - mu) * lax.rsqrt(var + LN_EPS) * g + b


def _matmul_nn(a, w, bias, tm, tn, name, out_dtype=F32):
    m, k = a.shape
    if w.ndim == 3:
        n = w.shape[0] * w.shape[2]
        assert tn == w.shape[2]
        w_spec = pl.BlockSpec((None, k, tn), lambda i, j: (j, 0, 0))
    else:
        n = w.shape[1]
        w_spec = pl.BlockSpec((k, tn), lambda i, j: (0, j))

    def body(*refs):
        a_ref, w_ref = refs[0], refs[1]
        o_ref = refs[-1]
        acc = _nn_raw(a_ref[...], w_ref[...])
        if bias is not None:
            acc = acc + refs[2][...]
        o_ref[...] = acc.astype(o_ref.dtype)

    in_specs = [pl.BlockSpec((tm, k), lambda i, j: (i, 0)), w_spec]
    args = [a, w]
    if bias is not None:
        in_specs.append(pl.BlockSpec((1, tn), lambda i, j: (0, j)))
        args.append(bias)
    return pl.pallas_call(
        body, name=name, grid=(m // tm, n // tn), in_specs=in_specs,
        out_specs=pl.BlockSpec((tm, tn), lambda i, j: (i, j)),
        out_shape=jax.ShapeDtypeStruct((m, n), out_dtype),
        compiler_params=_params(("parallel", "parallel")),
    )(*args)


def _matmul_nt(pairs, add, scale, tm, tk, name, out_dtype=F32):
    m = pairs[0][0].shape[0]
    k = pairs[0][1].shape[-2]
    groups = []
    in_specs, args = [], []
    for pair in pairs:
        d, w = pair[0], pair[1]
        in_specs.append(pl.BlockSpec((tm, d.shape[1]), lambda i, j: (i, 0)))
        if w.ndim == 3:
            g = d.shape[1] // w.shape[2]
            blk = pair[2] // g
            in_specs.append(pl.BlockSpec((g, tk, w.shape[2]), lambda i, j, blk=blk: (blk, j, 0)))
            groups.append((g, w.shape[2]))
        else:
            in_specs.append(pl.BlockSpec((tk, w.shape[1]), lambda i, j: (j, 0)))
            groups.append(None)
        args += [d, w]
    if add is not None:
        in_specs.append(pl.BlockSpec((tm, tk), lambda i, j: (i, j)))
        args.append(add)

    def body(*refs):
        o_ref = refs[-1]
        acc = None
        for p, grp in enumerate(groups):
            d_ref, w_ref = refs[2 * p], refs[2 * p + 1]
            if grp is None:
                terms = [_nt_raw(d_ref[...], w_ref[...])]
            else:
                terms = [_nt_raw(d_ref[:, g * grp[1]:(g + 1) * grp[1]], w_ref[g]) for g in range(grp[0])]
            for t in terms:
                acc = t if acc is None else acc + t
        if add is not None:
            acc = acc + scale * refs[2 * len(groups)][...]
        o_ref[...] = acc.astype(o_ref.dtype)

    return pl.pallas_call(
        body, name=name, grid=(m // tm, k // tk), in_specs=in_specs,
        out_specs=pl.BlockSpec((tm, tk), lambda i, j: (i, j)),
        out_shape=jax.ShapeDtypeStruct((m, k), out_dtype),
        compiler_params=_params(("parallel", "parallel")),
    )(*args)


def _matmul_tn(a, b, tm, tn, tt, name, shards=None, shard0=0, group=1, into=None, colsum=False):
    t, m = a.shape
    n = b.shape[1]
    assert not colsum or tm == m
    n_in = 2 + (into is not None)
    out_dtype = BF16
    per_step = 1 if shards is None else group
    width = per_step * tn

    def body(*refs):
        a_ref, b_ref = refs[0], refs[1]
        o_ref, acc_ref = refs[n_in], refs[-1]
        first = pl.program_id(2) == 0

        @pl.when(first)
        def _():
            acc_ref[...] = jnp.zeros_like(acc_ref)

        if shards is None:
            acc_ref[...] += _tn_raw(a_ref[...], b_ref[...])
        else:
            lhs = a_ref[...].astype(BF16)
            for g in range(per_step):
                acc_ref[g] += _tn_raw(lhs, b_ref[:, g * tn:(g + 1) * tn])

        @pl.when(pl.program_id(2) == t // tt - 1)
        def _():
            o_ref[...] = acc_ref[...].astype(o_ref.dtype)

        if colsum:
            s_ref = refs[n_in + 1]

            @pl.when(first)
            def _():
                s_ref[...] = jnp.zeros_like(s_ref)

            s_ref[...] += jnp.sum(b_ref[...], axis=0, keepdims=True)

    in_specs = [pl.BlockSpec((tt, tm), lambda i, j, kk: (kk, i)),
                pl.BlockSpec((tt, width), lambda i, j, kk: (kk, j))]
    args = [a, b]
    aliases = {}
    if into is not None:
        in_specs.append(pl.BlockSpec(memory_space=pl.ANY))
        args.append(into)
        aliases = {2: 0}
    if shards is None:
        out_specs = [pl.BlockSpec((tm, tn), lambda i, j, kk: (i, j))]
        out_shape = [jax.ShapeDtypeStruct((m, n), out_dtype)]
        acc = pltpu.VMEM((tm, tn), F32)
    else:
        out_specs = [pl.BlockSpec((per_step, tm, tn), lambda i, j, kk: (shard0 // per_step + j, i, 0))]
        out_shape = [jax.ShapeDtypeStruct((shards, m, tn), out_dtype)]
        acc = pltpu.VMEM((per_step, tm, tn), F32)
    if colsum:
        out_specs.append(pl.BlockSpec((1, tn), lambda i, j, kk: (0, j)))
        out_shape.append(jax.ShapeDtypeStruct((1, n), F32))
    res = pl.pallas_call(
        body, name=name, grid=(m // tm, n // width, t // tt), in_specs=in_specs, out_specs=out_specs,
        out_shape=out_shape, input_output_aliases=aliases, scratch_shapes=[acc],
        compiler_params=_params(("parallel", "parallel", "arbitrary")),
    )(*args)
    return res if colsum else res[0]


ROW_TILE = 64


def _stack(ref, start, rows):
    return ref[pl.ds(start, rows), :].astype(F32).reshape(rows // SUBLANES, SUBLANES, LANES)


def _vreg_rows(ref, n):
    return [jnp.broadcast_to(ref[j:j + 1, :], (SUBLANES, LANES))[None] for j in range(n)]


def _column_total(acc):
    return jnp.sum(acc, axis=0, keepdims=True)


def _conv_fwd_tile(pad_ref, taps_w, bias, r0, rows):
    taps = len(taps_w)
    acc = bias
    for j in range(taps):
        acc = acc + _stack(pad_ref, SUBLANES - (taps - 1 - j) + r0, rows) * taps_w[j]
    return acc


def _conv_grads_tile(pad_ref, dpad_ref, dx_ref, taps_w, dws, r0, rows):
    taps = len(taps_w)
    x_rows = _stack(pad_ref, SUBLANES + r0, rows)
    dx = None
    for j in range(taps):
        d_shifted = _stack(dpad_ref, r0 + (taps - 1 - j), rows)
        term = d_shifted * taps_w[j]
        dx = term if dx is None else dx + term
        dws[j] = dws[j] + jnp.sum(d_shifted * x_rows, axis=0)
    dx_ref[r0:r0 + rows, :] = dx.reshape(rows, LANES).astype(dx_ref.dtype)
    return jnp.sum(dx, axis=0)


def _ml_conv_fwd(proj, conv_w, conv_b):
    s = proj.shape[0]
    nblk = 2 * D_GROUP // LANES

    def body(x_ref, w_ref, b_ref, o_ref, pad_ref):
        pad_ref[0:SUBLANES, :] = jnp.zeros((SUBLANES, LANES), F32)
        pad_ref[SUBLANES:, :] = x_ref[...].astype(F32)
        taps_w, bias = _vreg_rows(w_ref, ML_CONV), _vreg_rows(b_ref, 1)[0]
        for r0 in range(0, s, ROW_TILE):
            rows = min(ROW_TILE, s - r0)
            o_ref[r0:r0 + rows, :] = jax.nn.silu(_conv_fwd_tile(pad_ref, taps_w, bias, r0, rows)).reshape(rows, LANES)

    return pl.pallas_call(
        body, name="ml_conv_fwd", grid=(nblk,),
        in_specs=[pl.BlockSpec((s, LANES), lambda j: (0, SEG_MQ + j)),
                  pl.BlockSpec((ML_CONV, LANES), lambda j: (0, j)),
                  pl.BlockSpec((1, LANES), lambda j: (0, j))],
        out_specs=pl.BlockSpec((s, LANES), lambda j: (0, j)),
        out_shape=jax.ShapeDtypeStruct((s, 2 * D_GROUP), F32),
        scratch_shapes=[pltpu.VMEM((s + SUBLANES, LANES), F32)],
        compiler_params=_params(("parallel",)),
    )(proj, conv_w, conv_b)


def _ml_conv_bwd(proj, conv_w, conv_b, d_qk, d_proj):
    s = proj.shape[0]
    nblk = 2 * D_GROUP // LANES

    def body(x_ref, w_ref, b_ref, dy_ref, _, dx_ref, dw_ref, db_ref, dxs_ref, pad_ref, dpad_ref):
        pad_ref[0:SUBLANES, :] = jnp.zeros((SUBLANES, LANES), F32)
        pad_ref[SUBLANES:, :] = x_ref[...].astype(F32)
        dpad_ref[s:, :] = jnp.zeros((SUBLANES, LANES), F32)
        taps_w, bias = _vreg_rows(w_ref, ML_CONV), _vreg_rows(b_ref, 1)[0]
        db = jnp.zeros((SUBLANES, LANES), F32)
        for r0 in range(0, s, ROW_TILE):
            rows = min(ROW_TILE, s - r0)
            pre = _conv_fwd_tile(pad_ref, taps_w, bias, r0, rows)
            _, vjp = jax.vjp(jax.nn.silu, pre)
            d_pre, = vjp(_stack(dy_ref, r0, rows))
            dpad_ref[r0:r0 + rows, :] = d_pre.reshape(rows, LANES)
            db = db + jnp.sum(d_pre, axis=0)
        db_ref[...] = _column_total(db)
        dws = [jnp.zeros((SUBLANES, LANES), F32) for _ in range(ML_CONV)]
        dx_sum = jnp.zeros((SUBLANES, LANES), F32)
        for r0 in range(0, s, ROW_TILE):
            dx_sum = dx_sum + _conv_grads_tile(pad_ref, dpad_ref, dx_ref, taps_w, dws, r0, min(ROW_TILE, s - r0))
        dxs_ref[...] = _column_total(dx_sum)
        for j in range(ML_CONV):
            dw_ref[j:j + 1, :] = _column_total(dws[j])

    return pl.pallas_call(
        body, name="ml_conv_bwd", grid=(nblk,),
        in_specs=[pl.BlockSpec((s, LANES), lambda j: (0, SEG_MQ + j)),
                  pl.BlockSpec((ML_CONV, LANES), lambda j: (0, j)),
                  pl.BlockSpec((1, LANES), lambda j: (0, j)),
                  pl.BlockSpec((s, LANES), lambda j: (0, j)),
                  pl.BlockSpec(memory_space=pl.ANY)],
        out_specs=[pl.BlockSpec((s, LANES), lambda j: (0, SEG_MQ + j)),
                   pl.BlockSpec((ML_CONV, LANES), lambda j: (0, j)),
                   pl.BlockSpec((1, LANES), lambda j: (0, j)),
                   pl.BlockSpec((1, LANES), lambda j: (0, j))],
        out_shape=[jax.ShapeDtypeStruct(d_proj.shape, d_proj.dtype),
                   jax.ShapeDtypeStruct((ML_CONV, 2 * D_GROUP), F32),
                   jax.ShapeDtypeStruct((1, 2 * D_GROUP), F32),
                   jax.ShapeDtypeStruct((1, 2 * D_GROUP), F32)],
        input_output_aliases={4: 0},
        scratch_shapes=[pltpu.VMEM((s + SUBLANES, LANES), F32), pltpu.VMEM((s + SUBLANES, LANES), F32)],
        compiler_params=_params(("parallel",)),
    )(proj, conv_w, conv_b, d_qk, d_proj)


def _gelu_mul(a, b):
    return jax.nn.gelu(a) * b


GELU_C = math.sqrt(2.0 / math.pi)
GELU_K = 0.044715


def _gelu_mul_grads(a, b, d):
    a2 = a * a
    t = jnp.tanh(GELU_C * (a + GELU_K * (a * a2)))
    cdf = 0.5 * (1.0 + t)
    slope = cdf + (0.5 * GELU_C) * a * (1.0 - t * t) * (1.0 + (3.0 * GELU_K) * a2)
    return d * b * slope, d * (a * cdf)


FFN_BLOCKS = D_FF_P // LANES


def _ffn_conv_fwd(u, conv_w, conv_b):
    s = u.shape[0]

    def body(g_ref, v_ref, wg_ref, wv_ref, bg_ref, bv_ref, o_ref, gpad_ref, vpad_ref):
        for pad_ref, x_ref in ((gpad_ref, g_ref), (vpad_ref, v_ref)):
            pad_ref[0:SUBLANES, :] = jnp.zeros((SUBLANES, LANES), F32)
            pad_ref[SUBLANES:, :] = x_ref[...].astype(F32)
        taps_g, bias_g = _vreg_rows(wg_ref, FFN_CONV), _vreg_rows(bg_ref, 1)[0]
        taps_v, bias_v = _vreg_rows(wv_ref, FFN_CONV), _vreg_rows(bv_ref, 1)[0]
        for r0 in range(0, s, ROW_TILE):
            rows = min(ROW_TILE, s - r0)
            ug = _conv_fwd_tile(gpad_ref, taps_g, bias_g, r0, rows)
            uv = _conv_fwd_tile(vpad_ref, taps_v, bias_v, r0, rows)
            o_ref[r0:r0 + rows, :] = _gelu_mul(ug, uv).reshape(rows, LANES).astype(o_ref.dtype)

    col = lambda off: (lambda j: (0, off + j))
    return pl.pallas_call(
        body, name="ffn_conv_fwd", grid=(FFN_BLOCKS,),
        in_specs=[pl.BlockSpec((s, LANES), col(0)), pl.BlockSpec((s, LANES), col(FFN_BLOCKS)),
                  pl.BlockSpec((FFN_CONV, LANES), col(0)), pl.BlockSpec((FFN_CONV, LANES), col(FFN_BLOCKS)),
                  pl.BlockSpec((1, LANES), col(0)), pl.BlockSpec((1, LANES), col(FFN_BLOCKS))],
        out_specs=pl.BlockSpec((s, LANES), col(0)),
        out_shape=jax.ShapeDtypeStruct((s, D_FF_P), BF16),
        scratch_shapes=[pltpu.VMEM((s + SUBLANES, LANES), F32), pltpu.VMEM((s + SUBLANES, LANES), F32)],
        compiler_params=_params(("parallel",)),
    )(u, u, conv_w, conv_w, conv_b, conv_b)


def _ffn_conv_bwd(u, conv_w, conv_b, d_h):
    s = u.shape[0]

    def body(g_ref, v_ref, wg_ref, wv_ref, bg_ref, bv_ref, dh_ref,
             dug_ref, duv_ref, dwg_ref, dwv_ref, dbg_ref, dbv_ref,
             gpad_ref, vpad_ref, dgpad_ref, dvpad_ref):
        for pad_ref, x_ref in ((gpad_ref, g_ref), (vpad_ref, v_ref)):
            pad_ref[0:SUBLANES, :] = jnp.zeros((SUBLANES, LANES), F32)
            pad_ref[SUBLANES:, :] = x_ref[...].astype(F32)
        dgpad_ref[s:, :] = jnp.zeros((SUBLANES, LANES), F32)
        dvpad_ref[s:, :] = jnp.zeros((SUBLANES, LANES), F32)
        taps_g, bias_g = _vreg_rows(wg_ref, FFN_CONV), _vreg_rows(bg_ref, 1)[0]
        taps_v, bias_v = _vreg_rows(wv_ref, FFN_CONV), _vreg_rows(bv_ref, 1)[0]
        dbg = jnp.zeros((SUBLANES, LANES), F32)
        dbv = jnp.zeros((SUBLANES, LANES), F32)
        for r0 in range(0, s, ROW_TILE):
            rows = min(ROW_TILE, s - r0)
            ug = _conv_fwd_tile(gpad_ref, taps_g, bias_g, r0, rows)
            uv = _conv_fwd_tile(vpad_ref, taps_v, bias_v, r0, rows)
            d_ug, d_uv = _gelu_mul_grads(ug, uv, _stack(dh_ref, r0, rows))
            dgpad_ref[r0:r0 + rows, :] = d_ug.reshape(rows, LANES)
            dvpad_ref[r0:r0 + rows, :] = d_uv.reshape(rows, LANES)
            dbg = dbg + jnp.sum(d_ug, axis=0)
            dbv = dbv + jnp.sum(d_uv, axis=0)
        dbg_ref[...] = _column_total(dbg)
        dbv_ref[...] = _column_total(dbv)
        for pad_ref, dpad_ref, taps_w, dx_ref, dw_ref in ((gpad_ref, dgpad_ref, taps_g, dug_ref, dwg_ref),
                                                          (vpad_ref, dvpad_ref, taps_v, duv_ref, dwv_ref)):
            dws = [jnp.zeros((SUBLANES, LANES), F32) for _ in range(FFN_CONV)]
            for r0 in range(0, s, ROW_TILE):
                _conv_grads_tile(pad_ref, dpad_ref, dx_ref, taps_w, dws, r0, min(ROW_TILE, s - r0))
            for j in range(FFN_CONV):
                dw_ref[j:j + 1, :] = _column_total(dws[j])

    col = lambda off: (lambda j: (0, off + j))
    seq = pl.BlockSpec((s, LANES), col(0))
    return pl.pallas_call(
        body, name="ffn_conv_bwd", grid=(FFN_BLOCKS,),
        in_specs=[pl.BlockSpec((s, LANES), col(0)), pl.BlockSpec((s, LANES), col(FFN_BLOCKS)),
                  pl.BlockSpec((FFN_CONV, LANES), col(0)), pl.BlockSpec((FFN_CONV, LANES), col(FFN_BLOCKS)),
                  pl.BlockSpec((1, LANES), col(0)), pl.BlockSpec((1, LANES), col(FFN_BLOCKS)), seq],
        out_specs=[seq, seq, pl.BlockSpec((FFN_CONV, LANES), col(0)), pl.BlockSpec((FFN_CONV, LANES), col(0)),
                   pl.BlockSpec((1, LANES), col(0)), pl.BlockSpec((1, LANES), col(0))],
        out_shape=[jax.ShapeDtypeStruct((s, D_FF_P), BF16), jax.ShapeDtypeStruct((s, D_FF_P), BF16),
                   jax.ShapeDtypeStruct((FFN_CONV, D_FF_P), F32), jax.ShapeDtypeStruct((FFN_CONV, D_FF_P), F32),
                   jax.ShapeDtypeStruct((1, D_FF_P), F32), jax.ShapeDtypeStruct((1, D_FF_P), F32)],
        scratch_shapes=[pltpu.VMEM((s + SUBLANES, LANES), F32) for _ in range(4)],
        compiler_params=_params(("parallel",)),
    )(u, u, conv_w, conv_w, conv_b, conv_b, d_h)


def _chunk_masks(c):
    row = lax.broadcasted_iota(jnp.int32, (c, c), 0)
    col = lax.broadcasted_iota(jnp.int32, (c, c), 1)
    return row, col


@jax.custom_vjp
def _split_heads(x):
    return tuple(x[:, h * D_HEAD:(h + 1) * D_HEAD] for h in range(N_HEADS))


_split_heads.defvjp(lambda x: (_split_heads(x), None), lambda _, gs: (jnp.concatenate(gs, axis=1),))


@jax.custom_vjp
def _merge_heads(xs):
    return jnp.concatenate(xs, axis=1)


_merge_heads.defvjp(lambda xs: (_merge_heads(xs), None), lambda _, g: (_split_heads(g),))


@jax.custom_vjp
def _split_chunks(x):
    return tuple(x[i * CHUNK:(i + 1) * CHUNK] for i in range(x.shape[0] // CHUNK))


_split_chunks.defvjp(lambda x: (_split_chunks(x), None), lambda _, gs: (jnp.concatenate(gs, axis=0),))


@jax.custom_vjp
def _merge_chunks(xs):
    return jnp.concatenate(xs, axis=0)


_merge_chunks.defvjp(lambda xs: (_merge_chunks(xs), None), lambda _, g: (_split_chunks(g),))


def _blocks(x):
    return [_split_heads(rows) for rows in _split_chunks(x)]


def _per_chunk_rows(per_chunk, rid):
    out = per_chunk[0]
    for i in range(1, len(per_chunk)):
        out = jnp.where(rid >= i * CHUNK, per_chunk[i], out)
    return out


HEADS = range(N_HEADS)
CHUNKS_PER_STEP = 8
ML_CHUNKS_PER_STEP = 1


def _hg_chunk(hq, hf, hi, hgate, l0, l1, nw, sts):
    n = hq.shape[0] // CHUNK
    causal = _chunk_masks(CHUNK)
    causal = causal[1] <= causal[0]
    mx = lax.stop_gradient(jnp.maximum(l0, l1))
    e0 = jnp.exp(l0 - mx)
    e1 = jnp.exp(l1 - mx)
    lb = e0 / (e0 + e1)
    sig = jax.nn.sigmoid(hf)
    lf = jnp.log(lb + (1.0 - lb) * sig)
    k = (1.0 - lb) * jax.nn.sigmoid(-hf)
    q = jax.nn.silu(hq)
    tri = causal.astype(F32)
    b = _merge_chunks(tuple(_dg(tri, rows, 1, 0, HIGHEST) for rows in _split_chunks(lf)))
    rid = lax.broadcasted_iota(jnp.int32, b.shape, 0)
    pick = lambda r: jnp.sum(jnp.where(rid == r, b, 0.0), axis=0, keepdims=True)
    b_last_c = [pick(i * CHUNK + CHUNK - 1) for i in range(n)]
    b_ref = _per_chunk_rows([pick(i * CHUNK + CHUNK // 2 - 1) for i in range(n)], rid)
    b_last = _per_chunk_rows(b_last_c, rid)
    qa = _blocks(q * jnp.exp(b - b_ref))
    ka = _blocks(k * jnp.exp(b_ref - b))
    qe = _blocks(q * jnp.exp(b))
    kd = _blocks(k * jnp.exp(b_last - b))
    decay = [_split_heads(jnp.exp(b_last_c[i])) for i in range(n)]
    v = _blocks(hi)
    chunks = range(n)
    attn = [[jnp.where(causal, _nt(qa[i][h], ka[i][h]), 0.0) for h in HEADS] for i in chunks]
    intra = [[_nn(attn[i][h], v[i][h]) for h in HEADS] for i in chunks]
    kv = [[_tn(v[i][h], kd[i][h]) for h in HEADS] for i in chunks]
    normed = []
    for i in chunks:
        inter = [_nt(qe[i][h], sts[h]) for h in HEADS]
        sts = tuple(decay[i][h] * sts[h] + kv[i][h] for h in HEADS)
        o = [intra[i][h] + inter[h] for h in HEADS]
        normed.append(_merge_heads(tuple(o[h] * lax.rsqrt(jnp.mean(o[h] * o[h], axis=-1, keepdims=True) + LN_EPS)
                                         for h in HEADS)))
    return _merge_chunks(tuple(normed)) * nw * jax.nn.silu(hgate), sts


def _seg(ref, seg):
    return ref[:, seg * D_GROUP:(seg + 1) * D_GROUP]


def _hgrn2_fwd(proj, logits, norm_w):
    s = proj.shape[0]
    rows = CHUNKS_PER_STEP * CHUNK
    nc = s // rows

    def body(p_ref, lg_ref, nw_ref, y_ref, st_out_ref, st_scr):
        @pl.when(pl.program_id(0) == 0)
        def _():
            st_scr[...] = jnp.zeros_like(st_scr)

        sts = tuple(st_scr[h] for h in HEADS)
        y, sts_new = _hg_chunk(_seg(p_ref, 0), _seg(p_ref, 1), _seg(p_ref, 2), _seg(p_ref, 3),
                               lg_ref[0:1, :], lg_ref[1:2, :], nw_ref[...], sts)
        y_ref[...] = y.astype(y_ref.dtype)
        for h in HEADS:
            st_out_ref[h] = sts[h]
            st_scr[h] = sts_new[h]

    return pl.pallas_call(
        body, name="hgrn2_fwd", grid=(nc,),
        in_specs=[pl.BlockSpec((rows, 4 * D_GROUP), lambda c: (c, 0)),
                  pl.BlockSpec((2, D_GROUP), lambda c: (0, 0)),
                  pl.BlockSpec((1, D_GROUP), lambda c: (0, 0))],
        out_specs=[pl.BlockSpec((rows, D_GROUP), lambda c: (c, 0)),
                   pl.BlockSpec((None, N_HEADS, D_HEAD, D_HEAD), lambda c: (c, 0, 0, 0))],
        out_shape=[jax.ShapeDtypeStruct((s, 2 * D_GROUP), BF16),
                   jax.ShapeDtypeStruct((nc, N_HEADS, D_HEAD, D_HEAD), F32)],
        scratch_shapes=[pltpu.VMEM((N_HEADS, D_HEAD, D_HEAD), F32)],
        compiler_params=_params(("arbitrary",)),
    )(proj, logits, norm_w)


def _hgrn2_bwd(proj, logits, norm_w, states, d_y):
    s = proj.shape[0]
    rows = CHUNKS_PER_STEP * CHUNK
    nc = s // rows

    def body(p_ref, lg_ref, nw_ref, st_ref, dy_ref, dp_ref, dl_ref, dnw_ref, dsum_ref, dst_scr):
        @pl.when(pl.program_id(0) == 0)
        def _():
            dst_scr[...] = jnp.zeros_like(dst_scr)
            dl_ref[...] = jnp.zeros_like(dl_ref)
            dnw_ref[...] = jnp.zeros_like(dnw_ref)
            dsum_ref[...] = jnp.zeros_like(dsum_ref)

        _, vjp = jax.vjp(_hg_chunk, _seg(p_ref, 0), _seg(p_ref, 1), _seg(p_ref, 2), _seg(p_ref, 3),
                         lg_ref[0:1, :], lg_ref[1:2, :], nw_ref[...], tuple(st_ref[h] for h in HEADS))
        d_hq, d_hf, d_hi, d_hg, d_l0, d_l1, d_nw, d_sts = vjp((dy_ref[...], tuple(dst_scr[h] for h in HEADS)))
        for seg, val in enumerate((d_hq, d_hf, d_hi, d_hg)):
            dp_ref[:, seg * D_GROUP:(seg + 1) * D_GROUP] = val.astype(dp_ref.dtype)
            dsum_ref[:, seg * D_GROUP:(seg + 1) * D_GROUP] += jnp.sum(val, axis=0, keepdims=True)
        dl_ref[0:1, :] += d_l0
        dl_ref[1:2, :] += d_l1
        dnw_ref[...] += d_nw
        for h in HEADS:
            dst_scr[h] = d_sts[h]

    rev = lambda c: nc - 1 - c
    return pl.pallas_call(
        body, name="hgrn2_bwd", grid=(nc,),
        in_specs=[pl.BlockSpec((rows, 4 * D_GROUP), lambda c: (rev(c), 0)),
                  pl.BlockSpec((2, D_GROUP), lambda c: (0, 0)),
                  pl.BlockSpec((1, D_GROUP), lambda c: (0, 0)),
                  pl.BlockSpec((None, N_HEADS, D_HEAD, D_HEAD), lambda c: (rev(c), 0, 0, 0)),
                  pl.BlockSpec((rows, D_GROUP), lambda c: (rev(c), 0))],
        out_specs=[pl.BlockSpec((rows, 4 * D_GROUP), lambda c: (rev(c), 0)),
                   pl.BlockSpec((2, D_GROUP), lambda c: (0, 0)),
                   pl.BlockSpec((1, D_GROUP), lambda c: (0, 0)),
                   pl.BlockSpec((1, 4 * D_GROUP), lambda c: (0, 0))],
        out_shape=[jax.ShapeDtypeStruct((s, D_IN_MAIN), BF16), jax.ShapeDtypeStruct((2, D_GROUP), F32),
                   jax.ShapeDtypeStruct((1, D_GROUP), F32), jax.ShapeDtypeStruct((1, 4 * D_GROUP), F32)],
        scratch_shapes=[pltpu.VMEM((N_HEADS, D_HEAD, D_HEAD), F32)],
        compiler_params=_params(("arbitrary",)),
    )(proj, logits, norm_w, states, d_y)


def _gate_column(gates, lane, idx):
    return jnp.sum(jnp.where(lane == idx, gates, 0.0), axis=1, keepdims=True)


def _head_layer_norm(h):
    mu = jnp.mean(h, axis=-1, keepdims=True)
    var = jnp.mean(jnp.square(h - mu), axis=-1, keepdims=True)
    return (h - mu) * lax.rsqrt(var + LN_EPS)


def _ml_chunk(qc, kc, v, mo, gates, nw, cts, ns, ms):
    n = qc.shape[0] // CHUNK
    row, col = _chunk_masks(CHUNK)
    mask = col <= row
    eye = col == row
    to_row = lambda t: jnp.sum(jnp.where(eye, t, 0.0), axis=0, keepdims=True)
    q = _blocks(qc * (D_HEAD ** -0.5))
    k = _blocks(kc)
    vs = _blocks(v)
    gate_rows = _split_chunks(gates)
    lane = lax.broadcasted_iota(jnp.int32, gate_rows[0].shape, 1)
    each = [(i, h) for i in range(n) for h in HEADS]
    on_each = lambda f: {ih: f(*ih) for ih in each}
    ig = on_each(lambda i, h: _gate_column(gate_rows[i], lane, h))
    lf = on_each(lambda i, h: jax.nn.log_sigmoid(_gate_column(gate_rows[i], lane, N_HEADS + h)))
    lf_row = on_each(lambda i, h: to_row(lf[i, h]))
    ig_row = on_each(lambda i, h: to_row(ig[i, h]))
    b_col = on_each(lambda i, h: jnp.sum(jnp.where(mask, lf_row[i, h], 0.0), axis=1, keepdims=True))
    b_row = on_each(lambda i, h: jnp.sum(jnp.where(row <= col, lf[i, h], 0.0), axis=0, keepdims=True))
    g = on_each(lambda i, h: jnp.sum(lf[i, h], axis=0, keepdims=True))
    d = on_each(lambda i, h: jnp.where(mask, b_col[i, h] - b_row[i, h] + ig_row[i, h], -jnp.inf))
    a = on_each(lambda i, h: g[i, h] - b_col[i, h] + ig[i, h])
    m_at = {(0, h): ms[h] for h in HEADS}
    for i, h in each:
        m_at[i + 1, h] = lax.stop_gradient(jnp.maximum(g[i, h] + m_at[i, h], jnp.max(a[i, h], axis=0, keepdims=True)))
    inter = on_each(lambda i, h: b_col[i, h] + m_at[i, h])
    m_t = on_each(lambda i, h: lax.stop_gradient(jnp.maximum(inter[i, h], jnp.max(d[i, h], axis=1, keepdims=True))))
    qk = on_each(lambda i, h: _nt(q[i][h], k[i][h]))
    sc = on_each(lambda i, h: qk[i, h] * jnp.exp(d[i, h] - m_t[i, h]))
    w_inter = on_each(lambda i, h: jnp.exp(inter[i, h] - m_t[i, h]))
    sv = on_each(lambda i, h: _nn(sc[i, h], vs[i][h]))
    decay = on_each(lambda i, h: jnp.exp(g[i, h] + m_at[i, h] - m_at[i + 1, h]))
    wk = on_each(lambda i, h: k[i][h] * jnp.exp(a[i, h] - m_at[i + 1, h]))
    kv = on_each(lambda i, h: _tn(vs[i][h], wk[i, h]))
    normed = []
    for i in range(n):
        qc_state = [_nt(q[i][h], cts[h]) for h in HEADS]
        num = [sv[i, h] + w_inter[i, h] * qc_state[h] for h in HEADS]
        den = [jnp.sum(sc[i, h], axis=1, keepdims=True)
               + w_inter[i, h] * jnp.sum(q[i][h] * ns[h], axis=1, keepdims=True) for h in HEADS]
        hh = [num[h] / jnp.maximum(jnp.abs(den[h]), jnp.exp(-m_t[i, h])) for h in HEADS]
        cts = tuple(decay[i, h] * cts[h] + kv[i, h] for h in HEADS)
        ns = tuple(decay[i, h] * ns[h] + jnp.sum(wk[i, h], axis=0, keepdims=True) for h in HEADS)
        normed.append(_merge_heads(tuple(_head_layer_norm(hh[h]) for h in HEADS)))
    y = jax.nn.sigmoid(mo) * (_merge_chunks(tuple(normed)) * nw)
    return y, cts, ns, tuple(m_at[n, h] for h in HEADS)


def _mlstm_fwd(qk, proj, gates, norm_w, y):
    s = proj.shape[0]
    rows = ML_CHUNKS_PER_STEP * CHUNK
    nc = s // rows

    def body(qk_ref, vo_ref, g_ref, nw_ref, _, y_ref, ct_out, n_out, m_out, ct_scr, n_scr, m_scr):
        @pl.when(pl.program_id(0) == 0)
        def _():
            ct_scr[...] = jnp.zeros_like(ct_scr)
            n_scr[...] = jnp.zeros_like(n_scr)
            m_scr[...] = jnp.full(m_scr.shape, NEG_BIG, F32)

        cts = tuple(ct_scr[h] for h in HEADS)
        ns = tuple(n_scr[h] for h in HEADS)
        ms = tuple(m_scr[h] for h in HEADS)
        y, cts_new, ns_new, ms_new = _ml_chunk(_seg(qk_ref, 0), _seg(qk_ref, 1), _seg(vo_ref, 0), _seg(vo_ref, 1),
                                               g_ref[...], nw_ref[...], cts, ns, ms)
        y_ref[...] = y.astype(y_ref.dtype)
        for h in HEADS:
            ct_out[h], n_out[h], m_out[h] = cts[h], ns[h], ms[h]
            ct_scr[h], n_scr[h], m_scr[h] = cts_new[h], ns_new[h], ms_new[h]

    st = lambda r, w: pl.BlockSpec((None, N_HEADS, r, w), lambda c: (c, 0, 0, 0))
    return pl.pallas_call(
        body, name="mlstm_fwd", grid=(nc,),
        in_specs=[pl.BlockSpec((rows, 2 * D_GROUP), lambda c: (c, 0)),
                  pl.BlockSpec((rows, 2 * D_GROUP), lambda c: (c, 3)),
                  pl.BlockSpec((rows, LANES), lambda c: (c, 0)),
                  pl.BlockSpec((1, D_GROUP), lambda c: (0, 0)),
                  pl.BlockSpec(memory_space=pl.ANY)],
        out_specs=[pl.BlockSpec((rows, D_GROUP), lambda c: (c, 1)),
                   st(D_HEAD, D_HEAD), st(1, D_HEAD), st(1, 1)],
        out_shape=[jax.ShapeDtypeStruct(y.shape, y.dtype),
                   jax.ShapeDtypeStruct((nc, N_HEADS, D_HEAD, D_HEAD), F32),
                   jax.ShapeDtypeStruct((nc, N_HEADS, 1, D_HEAD), F32),
                   jax.ShapeDtypeStruct((nc, N_HEADS, 1, 1), F32)],
        input_output_aliases={4: 0},
        scratch_shapes=[pltpu.VMEM((N_HEADS, D_HEAD, D_HEAD), F32), pltpu.VMEM((N_HEADS, 1, D_HEAD), F32),
                        pltpu.VMEM((N_HEADS, 1, 1), F32)],
        compiler_params=_params(("arbitrary",)),
    )(qk, proj, gates, norm_w, y)


def _mlstm_bwd(qk, proj, gates, norm_w, ct_s, n_s, m_s, d_y, d_proj):
    s = proj.shape[0]
    rows = ML_CHUNKS_PER_STEP * CHUNK
    nc = s // rows

    def body(qk_ref, vo_ref, g_ref, nw_ref, ct_ref, n_ref, m_ref, dy_ref, _,
             dp_ref, dqk_ref, dg_ref, dnw_ref, dsum_ref, dct_scr, dn_scr):
        @pl.when(pl.program_id(0) == 0)
        def _():
            dct_scr[...] = jnp.zeros_like(dct_scr)
            dn_scr[...] = jnp.zeros_like(dn_scr)
            dnw_ref[...] = jnp.zeros_like(dnw_ref)
            dsum_ref[...] = jnp.zeros_like(dsum_ref)

        ms = tuple(m_ref[h] for h in HEADS)
        step = lambda *a: _ml_chunk(*a, ms)[:3]
        _, vjp = jax.vjp(step, _seg(qk_ref, 0), _seg(qk_ref, 1), _seg(vo_ref, 0), _seg(vo_ref, 1), g_ref[...],
                         nw_ref[...], tuple(ct_ref[h] for h in HEADS), tuple(n_ref[h] for h in HEADS))
        d_q, d_k, d_v, d_o, d_gates, d_nw, d_cts, d_ns = vjp(
            (dy_ref[...], tuple(dct_scr[h] for h in HEADS), tuple(dn_scr[h] for h in HEADS)))
        dqk_ref[:, 0:D_GROUP] = d_q
        dqk_ref[:, D_GROUP:2 * D_GROUP] = d_k
        for seg, val in enumerate((d_v, d_o)):
            dp_ref[:, seg * D_GROUP:(seg + 1) * D_GROUP] = val.astype(dp_ref.dtype)
            dsum_ref[:, seg * D_GROUP:(seg + 1) * D_GROUP] += jnp.sum(val, axis=0, keepdims=True)
        dg_ref[...] = d_gates
        dnw_ref[...] += d_nw
        for h in HEADS:
            dct_scr[h] = d_cts[h]
            dn_scr[h] = d_ns[h]

    rev = lambda c: nc - 1 - c
    st = lambda r, w: pl.BlockSpec((None, N_HEADS, r, w), lambda c: (rev(c), 0, 0, 0))
    return pl.pallas_call(
        body, name="mlstm_bwd", grid=(nc,),
        in_specs=[pl.BlockSpec((rows, 2 * D_GROUP), lambda c: (rev(c), 0)),
                  pl.BlockSpec((rows, 2 * D_GROUP), lambda c: (rev(c), 3)),
                  pl.BlockSpec((rows, LANES), lambda c: (rev(c), 0)),
                  pl.BlockSpec((1, D_GROUP), lambda c: (0, 0)),
                  st(D_HEAD, D_HEAD), st(1, D_HEAD), st(1, 1),
                  pl.BlockSpec((rows, D_GROUP), lambda c: (rev(c), 1)),
                  pl.BlockSpec(memory_space=pl.ANY)],
        out_specs=[pl.BlockSpec((rows, 2 * D_GROUP), lambda c: (rev(c), 3)),
                   pl.BlockSpec((rows, 2 * D_GROUP), lambda c: (rev(c), 0)),
                   pl.BlockSpec((rows, LANES), lambda c: (rev(c), 0)),
                   pl.BlockSpec((1, D_GROUP), lambda c: (0, 0)),
                   pl.BlockSpec((1, 2 * D_GROUP), lambda c: (0, 0))],
        out_shape=[jax.ShapeDtypeStruct(d_proj.shape, d_proj.dtype), jax.ShapeDtypeStruct((s, 2 * D_GROUP), F32),
                   jax.ShapeDtypeStruct((s, LANES), F32), jax.ShapeDtypeStruct((1, D_GROUP), F32),
                   jax.ShapeDtypeStruct((1, 2 * D_GROUP), F32)],
        input_output_aliases={8: 0},
        scratch_shapes=[pltpu.VMEM((N_HEADS, D_HEAD, D_HEAD), F32), pltpu.VMEM((N_HEADS, 1, D_HEAD), F32)],
        compiler_params=_params(("arbitrary",)),
    )(qk, proj, gates, norm_w, ct_s, n_s, m_s, d_y, d_proj)


LN_TOKENS = 512
ATT_TOKENS = 512


def _proj_res_ln(a, w, xres, g, b, name):
    s, dm = xres.shape
    k = a.shape[1]
    tb = min(LN_TOKENS, s)

    def body(a_ref, w_ref, x_ref, g_ref, b_ref, z_ref, o_ref):
        halves = [slice(0, tb // 2), slice(tb // 2, tb)]
        zs = [ALPHA * x_ref[rows, :] + _nn_raw(a_ref[rows, :], w_ref[...]) for rows in halves]
        for rows, z in zip(halves, zs):
            z_ref[rows, :] = z
            o_ref[rows, :] = _layer_norm(z, g_ref[...], b_ref[...])

    tok = pl.BlockSpec((tb, dm), lambda i: (i, 0))
    vec = pl.BlockSpec((1, dm), lambda i: (0, 0))
    act = jax.ShapeDtypeStruct((s, dm), F32)
    return pl.pallas_call(
        body, name=name, grid=(s // tb,),
        in_specs=[pl.BlockSpec((tb, k), lambda i: (i, 0)), pl.BlockSpec((k, dm), lambda i: (0, 0)), tok, vec, vec],
        out_specs=[tok, tok], out_shape=[act, act], compiler_params=_params(("parallel",)),
    )(a, w, xres, g, b)


def _ln_bwd_proj(d_out, z, g, b, w, name):
    s, dm = z.shape
    k = w.shape[0]
    tb = min(LN_TOKENS, s)

    def body(do_ref, z_ref, g_ref, b_ref, w_ref, dz_ref, da_ref, dg_ref, db_ref):
        @pl.when(pl.program_id(0) == 0)
        def _():
            dg_ref[...] = jnp.zeros_like(dg_ref)
            db_ref[...] = jnp.zeros_like(db_ref)

        halves = [slice(0, tb // 2), slice(tb // 2, tb)]
        d_zs = []
        for rows in halves:
            _, vjp = jax.vjp(_layer_norm, z_ref[rows, :], g_ref[...], b_ref[...])
            d_z, d_g, d_b = vjp(do_ref[rows, :])
            dz_ref[rows, :] = d_z
            dg_ref[...] += d_g
            db_ref[...] += d_b
            d_zs.append(d_z)
        for rows, d_z in zip(halves, d_zs):
            da_ref[rows, :] = _nt_raw(d_z, w_ref[...])

    tok = pl.BlockSpec((tb, dm), lambda i: (i, 0))
    vec = pl.BlockSpec((1, dm), lambda i: (0, 0))
    return pl.pallas_call(
        body, name=name, grid=(s // tb,),
        in_specs=[tok, tok, vec, vec, pl.BlockSpec((k, dm), lambda i: (0, 0))],
        out_specs=[tok, pl.BlockSpec((tb, k), lambda i: (i, 0)), vec, vec],
        out_shape=[jax.ShapeDtypeStruct((s, dm), F32), jax.ShapeDtypeStruct((s, k), F32),
                   jax.ShapeDtypeStruct((1, dm), F32), jax.ShapeDtypeStruct((1, dm), F32)],
        compiler_params=_params(("arbitrary",)),
    )(d_out, z, g, b, w)


def _proj_loss_tail(a, w, xres, g, b, target):
    s, dm = xres.shape
    k = a.shape[1]
    tb = min(ATT_TOKENS, s)

    def loss_fn(z, gg, bb, tgt):
        err = jnp.square(_layer_norm(z, gg, bb) - tgt)
        return 0.5 * jnp.sum(jnp.mean(err, axis=-1, keepdims=True), axis=0, keepdims=True)

    def body(a_ref, w_ref, x_ref, g_ref, b_ref, t_ref, loss_ref, dz_ref, dg_ref, db_ref):
        @pl.when(pl.program_id(0) == 0)
        def _():
            loss_ref[...] = jnp.zeros_like(loss_ref)
            dg_ref[...] = jnp.zeros_like(dg_ref)
            db_ref[...] = jnp.zeros_like(db_ref)

        halves = [slice(0, tb // 2), slice(tb // 2, tb)]
        zs = [ALPHA * x_ref[rows, :] + _nn_raw(a_ref[rows, :], w_ref[...]) for rows in halves]
        for rows, z in zip(halves, zs):
            tgt = t_ref[rows, :]
            loss, vjp = jax.vjp(lambda zz, gg, bb, tgt=tgt: loss_fn(zz, gg, bb, tgt), z, g_ref[...], b_ref[...])
            d_z, d_g, d_b = vjp(jnp.ones((1, 1), F32))
            loss_ref[...] += loss
            dz_ref[rows, :] = d_z
            dg_ref[...] += d_g
            db_ref[...] += d_b

    tok = pl.BlockSpec((tb, dm), lambda i: (i, 0))
    vec = pl.BlockSpec((1, dm), lambda i: (0, 0))
    one = pl.BlockSpec((1, 1), lambda i: (0, 0))
    return pl.pallas_call(
        body, name="ffn_down_loss_tail", grid=(s // tb,),
        in_specs=[pl.BlockSpec((tb, k), lambda i: (i, 0)), pl.BlockSpec((k, dm), lambda i: (0, 0)), tok, vec, vec, tok],
        out_specs=[one, tok, vec, vec],
        out_shape=[jax.ShapeDtypeStruct((1, 1), F32), jax.ShapeDtypeStruct((s, dm), F32),
                   jax.ShapeDtypeStruct((1, dm), F32), jax.ShapeDtypeStruct((1, dm), F32)],
        compiler_params=_params(("arbitrary",)),
    )(a, w, xres, g, b, target)


def _att_heads(qs, ks, vs):
    sc = [_nt(q, k) * (CA_DH ** -0.5) for q, k in zip(qs, ks)]
    p = [jax.nn.softmax(s, axis=-1) for s in sc]
    return tuple(_nn(pp, v) for pp, v in zip(p, vs))


def _head_slices(ref_or_value, offset):
    return tuple(ref_or_value[:, offset + h * CA_DH:offset + (h + 1) * CA_DH] for h in range(CA_HEADS))


def _cross_attention_fwd(x1, kv, wq, wo, g, b):
    s = x1.shape[0]
    tb = min(ATT_TOKENS, s)

    def body(x_ref, kv_ref, wq_ref, wo_ref, g_ref, b_ref, att_ref, z_ref, o_ref):
        x_blk = x_ref[...]
        q = _nn_raw(x_blk, wq_ref[...])
        att = jnp.concatenate(_att_heads(_head_slices(q, 0), _head_slices(kv_ref, 0), _head_slices(kv_ref, D_MODEL)),
                              axis=1)
        att_ref[...] = att.astype(att_ref.dtype)
        z = ALPHA * x_blk + _nn_raw(att, wo_ref[...])
        z_ref[...] = z
        o_ref[...] = _layer_norm(z, g_ref[...], b_ref[...])

    tok = pl.BlockSpec((tb, D_MODEL), lambda i: (i, 0))
    mat = pl.BlockSpec((D_MODEL, D_MODEL), lambda i: (0, 0))
    vec = pl.BlockSpec((1, D_MODEL), lambda i: (0, 0))
    act = jax.ShapeDtypeStruct((s, D_MODEL), F32)
    return pl.pallas_call(
        body, name="cross_attention_fwd", grid=(s // tb,),
        in_specs=[tok, pl.BlockSpec((N_MEM, 2 * D_MODEL), lambda i: (0, 0)), mat, mat, vec, vec],
        out_specs=[tok, tok, tok],
        out_shape=[jax.ShapeDtypeStruct((s, D_MODEL), BF16), act, act],
        compiler_params=_params(("parallel",)),
    )(x1, kv, wq, wo, g, b)


def _cross_attention_bwd(d_x2, x1, z2, kv, wq, wo, g, b):
    s = x1.shape[0]
    tb = min(ATT_TOKENS, s)

    def body(dx2_ref, x_ref, z_ref, kv_ref, wq_ref, wo_ref, g_ref, b_ref,
             dx1_ref, dq_ref, dz_ref, dkv_ref, dg_ref, db_ref):
        @pl.when(pl.program_id(0) == 0)
        def _():
            dkv_ref[...] = jnp.zeros_like(dkv_ref)
            dg_ref[...] = jnp.zeros_like(dg_ref)
            db_ref[...] = jnp.zeros_like(db_ref)

        q = _nn_raw(x_ref[...], wq_ref[...])
        _, ln_vjp = jax.vjp(_layer_norm, z_ref[...], g_ref[...], b_ref[...])
        d_z, d_g, d_b = ln_vjp(dx2_ref[...])
        dg_ref[...] += d_g
        db_ref[...] += d_b
        dz_ref[...] = d_z.astype(dz_ref.dtype)
        d_att = _nt_raw(d_z, wo_ref[...])
        _, vjp = jax.vjp(_att_heads, _head_slices(q, 0), _head_slices(kv_ref, 0), _head_slices(kv_ref, D_MODEL))
        d_qs, d_ks, d_vs = vjp(_head_slices(d_att, 0))
        for h in range(CA_HEADS):
            lo = h * CA_DH
            dkv_ref[:, lo:lo + CA_DH] += d_ks[h]
            dkv_ref[:, D_MODEL + lo:D_MODEL + lo + CA_DH] += d_vs[h]
        d_q = jnp.concatenate(d_qs, axis=1)
        dq_ref[...] = d_q.astype(dq_ref.dtype)
        dx1_ref[...] = ALPHA * d_z + _nt_raw(d_q, wq_ref[...])

    tok = pl.BlockSpec((tb, D_MODEL), lambda i: (i, 0))
    mem = pl.BlockSpec((N_MEM, 2 * D_MODEL), lambda i: (0, 0))
    mat = pl.BlockSpec((D_MODEL, D_MODEL), lambda i: (0, 0))
    vec = pl.BlockSpec((1, D_MODEL), lambda i: (0, 0))
    low = jax.ShapeDtypeStruct((s, D_MODEL), BF16)
    return pl.pallas_call(
        body, name="cross_attention_bwd", grid=(s // tb,),
        in_specs=[tok, tok, tok, mem, mat, mat, vec, vec], out_specs=[tok, tok, tok, mem, vec, vec],
        out_shape=[jax.ShapeDtypeStruct((s, D_MODEL), F32), low, low,
                   jax.ShapeDtypeStruct((N_MEM, 2 * D_MODEL), F32),
                   jax.ShapeDtypeStruct((1, D_MODEL), F32), jax.ShapeDtypeStruct((1, D_MODEL), F32)],
        compiler_params=_params(("arbitrary",)),
    )(d_x2, x1, z2, kv, wq, wo, g, b)


def _local_step(x, mem, target, w, mid_weights=None, ffn_weights=None, down_weights=None, on_ffn_grads=None,
                on_mid_grads=None,
                on_small_grads=None, on_last_grads=None):
    w = dict(w)
    s = x.shape[0]
    tm = min(512, s)
    tt_big = min(1024, s)
    proj = _matmul_nn(x, w["w_in_main"], w["b_in_main"], min(2048, s), 512, "proj")
    gates = _matmul_nn(x, w["w_in_gate"], w["b_in_gate"], tm, LANES, "proj_gates")
    qk = _ml_conv_fwd(proj, w["ml_conv_w"], w["ml_conv_b"])
    y, hg_states = _hgrn2_fwd(proj, w["hg_lb_logits"], w["hg_norm_w"])
    y, ct_s, n_s, m_s = _mlstm_fwd(qk, proj, gates, w["ml_norm_w"], y)
    if mid_weights is not None:
        w.update(mid_weights(y))
    z1, x1 = _proj_res_ln(y, w["w_out"], x, w["ln1_g"], w["ln1_b"], "out_proj_ln1")
    kv = _matmul_nn(mem, w["ca_wkv"], None, N_MEM, CA_DH, "kv")
    att, z2, x2 = _cross_attention_fwd(x1, kv, w["ca_wq"], w["ca_wo"], w["ln2_g"], w["ln2_b"])
    if ffn_weights is not None:
        w.update(ffn_weights(x2))
    u = _matmul_nn(x2, w["ffn_w_up"], None, min(2048, s), UP_SHARD_P, "ffn_up", BF16)
    hid = _ffn_conv_fwd(u, w["ffn_conv_w"], w["ffn_conv_b"])
    if down_weights is not None:
        w.update(down_weights(hid))
    loss, d_z3, d_ln3_g, d_ln3_b = _proj_loss_tail(hid, w["ffn_w_down"], x2, w["ln3_g"], w["ln3_b"], target)
    grads = {"ln3_g": d_ln3_g, "ln3_b": d_ln3_b}
    grads["ffn_w_down"] = _matmul_tn(hid, d_z3, 1536, D_MODEL, tt_big, "d_w_down")
    d_hid = _matmul_nt([(d_z3, w["ffn_w_down"])], None, 1.0, tm, D_FF_P, "d_hid", BF16)
    d_ug, d_uv, d_cwg, d_cwv, d_cbg, d_cbv = _ffn_conv_bwd(u, w["ffn_conv_w"], w["ffn_conv_b"], d_hid)
    grads["ffn_conv_w"] = jnp.concatenate([d_cwg, d_cwv], axis=-1)
    grads["ffn_conv_b"] = jnp.concatenate([d_cbg, d_cbv], axis=-1)
    half = N_DEV // 2
    d_w_up = _matmul_tn(x2, d_ug, D_MODEL, UP_SHARD_P, tt_big, "d_w_up_gate", shards=N_DEV, group=half)
    grads["ffn_w_up"] = _matmul_tn(x2, d_uv, D_MODEL, UP_SHARD_P, tt_big, "d_w_up_val", shards=N_DEV,
                                   shard0=half, group=half, into=d_w_up)
    d_x2 = _matmul_nt([(d_ug, w["ffn_w_up"], 0), (d_uv, w["ffn_w_up"], N_DEV // 2)], d_z3, ALPHA,
                      min(256, s), D_MODEL, "d_x2")
    if on_ffn_grads is not None:
        d_x2 = on_ffn_grads(grads, d_x2)
    d_x1, d_q, d_z2, d_kv, grads["ln2_g"], grads["ln2_b"] = _cross_attention_bwd(
        d_x2, x1, z2, kv, w["ca_wq"], w["ca_wo"], w["ln2_g"], w["ln2_b"])
    grads["ca_wo"] = _matmul_tn(att, d_z2, D_MODEL, D_MODEL, tt_big, "d_ca_wo")
    grads["ca_wq"] = _matmul_tn(x1, d_q, D_MODEL, D_MODEL, tt_big, "d_ca_wq")
    grads["ca_wkv"] = _matmul_tn(mem, d_kv, D_MODEL, CA_DH, N_MEM, "d_ca_wkv", shards=N_DEV, group=N_DEV)
    d_z1, d_y, grads["ln1_g"], grads["ln1_b"] = _ln_bwd_proj(d_x1, z1, w["ln1_g"], w["ln1_b"], w["w_out"],
                                                             "ln1_bwd_out_proj")
    grads["w_out"] = _matmul_tn(y, d_z1, D_MODEL, D_MODEL, tt_big, "d_w_out")
    if on_mid_grads is not None:
        d_y = on_mid_grads(grads, d_y)
    d_proj, grads["hg_lb_logits"], grads["hg_norm_w"], db_hg = _hgrn2_bwd(
        proj, w["hg_lb_logits"], w["hg_norm_w"], hg_states, d_y)
    d_proj, d_qk, d_gates, grads["ml_norm_w"], db_vo = _mlstm_bwd(
        qk, proj, gates, w["ml_norm_w"], ct_s, n_s, m_s, d_y, d_proj)
    d_proj, grads["ml_conv_w"], grads["ml_conv_b"], db_qk = _ml_conv_bwd(
        proj, w["ml_conv_w"], w["ml_conv_b"], d_qk, d_proj)
    grads["b_in_main"] = jnp.concatenate([db_hg, db_qk, db_vo], axis=-1)
    grads["w_in_gate"], grads["b_in_gate"] = _matmul_tn(x, d_gates, D_MODEL, LANES, tt_big, "d_w_in_gates",
                                                        colsum=True)
    if on_small_grads is not None:
        d_proj = on_small_grads(grads, loss, d_proj)
    grads["w_in_main"] = _matmul_tn(x, d_proj, D_MODEL, min(2048, D_IN_MAIN), tt_big, "d_w_in")
    if on_last_grads is not None:
        d_z1 = on_last_grads(grads, d_z1)
    grad_x = _matmul_nt([(d_proj, w["w_in_main"]), (d_gates, w["w_in_gate"])], d_z1, ALPHA, tm, D_MODEL, "d_x")
    return loss, grad_x, grads


HBM_SPEC = pl.BlockSpec(memory_space=pltpu.HBM)


def _coords():
    return lax.axis_index("x"), lax.axis_index("y"), lax.axis_index("c")


def _other_chips(x, y):
    return [(1 - x, y), (x, 1 - y), (1 - x, 1 - y)]


def _my_slot():
    x, y, c = _coords()
    return 4 * x + 2 * y + c


SEM_SPEC = pl.BlockSpec(memory_space=pltpu.SEMAPHORE)
ANY_SPEC = pl.BlockSpec(memory_space=pl.ANY)
SIDE_EFFECT = pltpu.SideEffectType.DATAFLOW_SIDE_EFFECTING


def _peer(x, y, c, d):
    flip = lambda v, bit: 1 - v if bit else v
    p = (flip(x, d & 4), flip(y, d & 2), flip(c, d & 1))
    return p, 4 * p[0] + 2 * p[1] + p[2]


def _direct_copies(gather, src_refs, land_refs, send_sems, recv_sems):
    x, y, c = _coords()
    me = 4 * x + 2 * y + c
    copies = []
    for a in range(len(src_refs)):
        for d in range(1, N_DEV):
            peer, peer_slot = _peer(x, y, c, d)
            copies.append(pltpu.make_async_remote_copy(
                src_ref=src_refs[a] if gather else src_refs[a].at[peer_slot],
                dst_ref=land_refs[a].at[me] if gather else land_refs[a].at[d - 1],
                send_sem=send_sems.at[7 * a + d - 1], recv_sem=recv_sems.at[7 * a + d - 1],
                device_id=peer, device_id_type=MESH))
    return copies


def _hbm(t):
    return pltpu.HBM(t.shape, t.dtype)


def _chip_copies(src_refs, land_refs, send_sems, recv_sems):
    x, y, c = _coords()
    me = 4 * x + 2 * y + c
    targets = [(x, y, 1 - c)] + [(cx, cy, c) for cx, cy in _other_chips(x, y)]
    return [pltpu.make_async_remote_copy(
        src_ref=src_refs[a], dst_ref=land_refs[a].at[me], send_sem=send_sems.at[4 * a + k],
        recv_sem=recv_sems.at[4 * a + k], device_id=target, device_id_type=MESH)
        for a in range(len(src_refs)) for k, target in enumerate(targets)]


def _forward_copies(land_refs, send_sems, recv_sems):
    x, y, c = _coords()
    return [pltpu.make_async_remote_copy(
        src_ref=land_refs[a].at[4 * cx + 2 * cy + c], dst_ref=land_refs[a].at[4 * cx + 2 * cy + c],
        send_sem=send_sems.at[3 * a + j], recv_sem=recv_sems.at[3 * a + j],
        device_id=(x, y, 1 - c), device_id_type=MESH)
        for a in range(len(land_refs)) for j, (cx, cy) in enumerate(_other_chips(x, y))]


def _split_copy_start(make_copies, n_sems, operands, through, name):
    n_ops = len(operands)

    def body(*refs):
        for cp in make_copies(refs[:n_ops], refs[n_ops + 1], refs[n_ops + 2]):
            cp.start()

    ins = [pltpu.with_memory_space_constraint(t, pltpu.HBM) for t in (*operands, through)]
    sems = pltpu.SemaphoreType.DMA((n_sems,))
    res = pl.pallas_call(
        body, name=name, out_shape=(sems, sems, *[_hbm(t) for t in ins]),
        in_specs=[HBM_SPEC] * (n_ops + 1), out_specs=(SEM_SPEC, SEM_SPEC, *[HBM_SPEC] * (n_ops + 1)),
        input_output_aliases={i: 2 + i for i in range(n_ops + 1)},
        compiler_params=pltpu.CompilerParams(has_side_effects=SIDE_EFFECT),
    )(*ins)
    return (res[0], res[1], list(res[2:2 + n_ops])), res[2 + n_ops]


def _split_copy_wait(make_copies, started, after, name):
    send_sems, recv_sems, operands = started
    n_ops = len(operands)
    after = list(after) if isinstance(after, (list, tuple)) else [after]

    def body(*refs):
        for cp in make_copies(refs[:n_ops], refs[n_ops], refs[n_ops + 1]):
            cp.wait_send()
            cp.wait_recv()

    res = pl.pallas_call(
        body, name=name, out_shape=tuple(_hbm(t) for t in operands),
        in_specs=[HBM_SPEC] * n_ops + [SEM_SPEC, SEM_SPEC] + [ANY_SPEC] * len(after),
        out_specs=tuple([HBM_SPEC] * n_ops), input_output_aliases={i: i for i in range(n_ops)},
        compiler_params=pltpu.CompilerParams(has_side_effects=SIDE_EFFECT),
    )(*operands, send_sems, recv_sems, *after)
    return list(res)


def _halves(make_copies, na):
    return lambda refs, send_sems, recv_sems: make_copies(refs[:na], refs[na:], send_sems, recv_sems)


def _direct_start(gather, arrays, through, name):
    na = len(arrays)
    lands = [lax.empty((N_DEV,) + t.shape if gather else (N_DEV - 1,) + t.shape[1:], t.dtype) for t in arrays]
    return _split_copy_start(_halves(functools.partial(_direct_copies, gather), na), 7 * na, [*arrays, *lands],
                             through, name)


def _direct_wait(gather, started, after, name):
    na = len(started[2]) // 2
    operands = _split_copy_wait(_halves(functools.partial(_direct_copies, gather), na), started, after, name)
    return operands[:na], operands[na:]


def _two_level_gather(shards, glue, name):
    na = len(shards)
    lands = [lax.empty((N_DEV,) + t.shape, t.dtype) for t in shards]
    nothing = jnp.zeros((SUBLANES, LANES), F32)
    started, _ = _split_copy_start(_halves(_chip_copies, na), 4 * na, [*shards, *lands], nothing, name + "_start")
    operands = _split_copy_wait(_halves(_chip_copies, na), started, glue, name + "_wait")
    started, mine = _split_copy_start(_forward_copies, 3 * na, operands[na:], operands[0], name + "_forward_start")
    lands = _split_copy_wait(_forward_copies, started, mine, name + "_forward_wait")
    return [lax.dynamic_update_index_in_dim(land, own, _my_slot(), 0)
            for own, land in zip([mine, *operands[1:na]], lands)]


def _row_tile(rows):
    for t in (256, 176, 128):
        if rows % t == 0 and rows > t:
            return t
    return rows


def _adamw_math(g, w, m, v):
    m_new = ADAM_B1 * m + (1.0 - ADAM_B1) * g
    v_new = ADAM_B2 * v + (1.0 - ADAM_B2) * jnp.square(g)
    m_hat = m_new / (1.0 - ADAM_B1 ** ADAM_STEP)
    v_hat = v_new / (1.0 - ADAM_B2 ** ADAM_STEP)
    delta = -ADAM_LR * (m_hat / (jnp.sqrt(v_hat) + ADAM_EPS) + ADAM_WD * w)
    return delta, m_new, v_new


def _adamw_sharded(chip, sums, got, w, m, v, name):
    r, c = w.shape
    tr = _row_tile(r)
    n_got = got.shape[0]

    def body(chip_ref, s_ref, g_ref, w_ref, m_ref, v_ref, go_ref, d_ref, nm_ref, nv_ref):
        g = s_ref[...].astype(F32)
        for i in range(n_got):
            g = g + g_ref[i].astype(F32)
        delta, m_new, v_new = _adamw_math(g, w_ref[...], m_ref[...], v_ref[...])
        go_ref[...] = g
        d_ref[...] = delta
        nm_ref[...] = m_new
        nv_ref[...] = v_new

    blk = pl.BlockSpec((tr, c), lambda i, chip_ref: (i, 0))
    out = jax.ShapeDtypeStruct((r, c), F32)
    return pl.pallas_call(
        body, name=name,
        grid_spec=pltpu.PrefetchScalarGridSpec(
            num_scalar_prefetch=1, grid=(r // tr,),
            in_specs=[pl.BlockSpec((None, tr, c), lambda i, chip_ref: (chip_ref[0], i, 0)),
                      pl.BlockSpec((n_got, tr, c), lambda i, chip_ref: (0, i, 0)), blk, blk, blk],
            out_specs=[blk, blk, blk, blk]),
        out_shape=[out, out, out, out],
        compiler_params=_params(("parallel",)),
    )(chip, sums, got, w, m, v)


def _adamw_replicated(parts, w, m, v):
    p, r, c = parts.shape
    names = SMALL_NAMES
    shapes = [w[n].shape for n in names]

    def body(*refs):
        p_ref = refs[0]
        ins = refs[1:1 + 3 * len(names)]
        outs = refs[1 + 3 * len(names):-2]
        loss_ref, sum_scr = refs[-2], refs[-1]
        total = p_ref[0]
        for i in range(1, p):
            total = total + p_ref[i]
        sum_scr[...] = total
        for k, n in enumerate(names):
            w_ref, m_ref, v_ref = ins[3 * k:3 * k + 3]
            g_ref, d_ref, nm_ref, nv_ref = outs[4 * k:4 * k + 4]
            for row, lane0, width, src_row in _small_pieces(n, shapes[k]):
                here = (slice(row, row + 1), slice(lane0, lane0 + width))
                g = sum_scr[src_row:src_row + 1, 0:width]
                delta, m_new, v_new = _adamw_math(g, w_ref[here], m_ref[here], v_ref[here])
                g_ref[here] = g
                d_ref[here] = delta
                nm_ref[here] = m_new
                nv_ref[here] = v_new
        loss_ref[...] = sum_scr[SMALL_LOSS_ROW:SMALL_LOSS_ROW + 1, 0:1]

    whole = lambda shape: pl.BlockSpec(shape, lambda i: (0,) * len(shape))
    args = [parts] + [t[n] for n in names for t in (w, m, v)]
    out_shape = [jax.ShapeDtypeStruct(s, F32) for s in shapes for _ in range(4)] + [jax.ShapeDtypeStruct((1, 1), F32)]
    res = pl.pallas_call(
        body, name="adamw_replicated", grid=(1,),
        in_specs=[whole(t.shape) for t in args], out_specs=[whole(s.shape) for s in out_shape],
        out_shape=out_shape, scratch_shapes=[pltpu.VMEM((r, c), F32)],
        compiler_params=_params(("arbitrary",)),
    )(*args)
    results = [{n: res[4 * k + j] for k, n in enumerate(names)} for j in range(4)]
    return results, res[-1]


SHARDED_NAMES = ("w_in", "ml_conv_w", "w_out", "ca_wq", "ca_wkv", "ca_wo", "ffn_w_up", "ffn_conv_w", "ffn_w_down")
SMALL_NAMES = ("b_in", "hg_lb_logits", "hg_norm_w", "ml_conv_b", "ml_norm_w", "ln1_g", "ln1_b",
               "ln2_g", "ln2_b", "ffn_conv_b", "ln3_g", "ln3_b")
WEIGHT_NAMES = ("w_in", "b_in", "hg_lb_logits", "hg_norm_w", "ml_conv_w", "ml_conv_b", "ml_norm_w", "w_out",
                "ln1_g", "ln1_b", "ca_wq", "ca_wkv", "ca_wo", "ln2_g", "ln2_b", "ffn_w_up", "ffn_conv_w",
                "ffn_conv_b", "ffn_w_down", "ln3_g", "ln3_b")
PAD_TO = {"ffn_w_up": UP_SHARD_P, "ffn_conv_w": UP_SHARD_P}
SMALL_ROWS = 24
SMALL_W = D_MODEL
SMALL_SHAPES = {"b_in": (1, D_IN), "hg_lb_logits": (2, D_GROUP), "hg_norm_w": (1, D_GROUP),
                "ml_conv_b": (1, 2 * D_GROUP), "ml_norm_w": (1, D_GROUP), "ln1_g": (1, D_MODEL), "ln1_b": (1, D_MODEL),
                "ln2_g": (1, D_MODEL), "ln2_b": (1, D_MODEL), "ffn_conv_b": (1, D_UP), "ln3_g": (1, D_MODEL),
                "ln3_b": (1, D_MODEL)}


def _shard_2d(name, block):
    t = block[0]
    if name in PAD_TO:
        t = jnp.pad(t, ((0, 0), (0, PAD_TO[name] - t.shape[1])))
    return t


def _shard_like(name, t, like):
    return t[:, :like.shape[2]][None]


def _pad_cols(t, width):
    return jnp.pad(t, ((0, 0), (0, width - t.shape[1])))


FIRST_NAMES = ("w_in", "ml_conv_w")
FFN_NAMES = ("ffn_w_up", "ffn_w_down", "ffn_conv_w")
MID_NAMES = ("ca_wo", "ca_wq", "ca_wkv", "w_out")


def _first_weights(g, small):
    w = dict(small)
    last = g["w_in"][N_DEV - 1]
    split = D_IN_MAIN - (N_DEV - 1) * W_IN_SHARD
    w["w_in_main"] = jnp.concatenate([*[g["w_in"][j] for j in range(N_DEV - 1)], last[:, :split]], axis=1)
    w["w_in_gate"] = _pad_cols(last[:, split:], LANES)
    w["b_in_main"] = small["b_in"][:, :D_IN_MAIN]
    w["b_in_gate"] = _pad_cols(small["b_in"][:, D_IN_MAIN:], LANES)
    w["ml_conv_w"] = jnp.transpose(g["ml_conv_w"], (1, 0, 2)).reshape(ML_CONV, 2 * D_GROUP)
    return w


def _mid_weights(g):
    w = {n: g[n].reshape(D_MODEL, D_MODEL) for n in ("w_out", "ca_wq", "ca_wo")}
    w["ca_wkv"] = g["ca_wkv"]
    return w


FFN_UP_NAMES = ("ffn_w_up", "ffn_conv_w")
FFN_DOWN_NAMES = ("ffn_w_down",)


def _ffn_up_weights(g, small):
    w = {"ffn_w_up": g["ffn_w_up"]}
    w["ffn_conv_w"] = jnp.transpose(g["ffn_conv_w"], (1, 0, 2)).reshape(FFN_CONV, D_UP_P)
    w["ffn_conv_b"] = _pad_cols(small["ffn_conv_b"].reshape(N_DEV, UP_SHARD), UP_SHARD_P).reshape(1, D_UP_P)
    return w


def _ffn_down_weights(g):
    down = g["ffn_w_down"].reshape(N_DEV // 2, UP_SHARD, D_MODEL)
    return {"ffn_w_down": jnp.pad(down, ((0, 0), (0, UP_SHARD_P - UP_SHARD), (0, 0))).reshape(D_FF_P, D_MODEL)}


def _whole_weights(g, small):
    return {**_first_weights(g, small), **_mid_weights(g), **_ffn_up_weights(g, small), **_ffn_down_weights(g)}


def _owner_stack(n, grads):
    if n == "w_in":
        main, gate = grads["w_in_main"], grads["w_in_gate"][:, :D_IN - D_IN_MAIN]
        last = jnp.concatenate([main[:, (N_DEV - 1) * W_IN_SHARD:], gate], axis=1)
        return jnp.stack([*[main[:, j * W_IN_SHARD:(j + 1) * W_IN_SHARD] for j in range(N_DEV - 1)], last])
    if n in ("w_out", "ca_wq", "ca_wo"):
        return grads[n].reshape(N_DEV, D_MODEL // N_DEV, D_MODEL)
    if n == "ffn_w_down":
        down = grads[n].reshape(N_DEV // 2, UP_SHARD_P, D_MODEL)[:, :UP_SHARD]
        return down.reshape(N_DEV, D_FF // N_DEV, D_MODEL)
    if n == "ml_conv_w":
        return jnp.transpose(grads[n].reshape(ML_CONV, N_DEV, LANES), (1, 0, 2))
    if n == "ffn_conv_w":
        return jnp.transpose(grads[n].reshape(FFN_CONV, N_DEV, UP_SHARD_P), (1, 0, 2))
    return grads[n]


def _owner_stacks(grads):
    return {n: _owner_stack(n, grads) for n in SHARDED_NAMES}


def _small_grads(grads):
    out = {n: grads[n] for n in SMALL_NAMES if n in grads}
    out["b_in"] = jnp.concatenate([grads["b_in_main"], grads["b_in_gate"][:, :D_IN - D_IN_MAIN]], axis=1)
    out["ffn_conv_b"] = grads["ffn_conv_b"].reshape(N_DEV, UP_SHARD_P)[:, :UP_SHARD].reshape(1, D_UP)
    return out


def _small_rows(shape):
    return shape[0] if shape[1] <= SMALL_W else -(-shape[1] // SMALL_W)


SMALL_BASE = {n: sum(_small_rows(SMALL_SHAPES[k]) for k in SMALL_NAMES[:i]) for i, n in enumerate(SMALL_NAMES)}
SMALL_LOSS_ROW = sum(_small_rows(SMALL_SHAPES[n]) for n in SMALL_NAMES)
assert SMALL_LOSS_ROW < SMALL_ROWS


def _small_pieces(name, shape):
    base = SMALL_BASE[name]
    if shape[1] <= SMALL_W:
        return [(i, 0, shape[1], base + i) for i in range(shape[0])]
    return [(0, k * SMALL_W, min(SMALL_W, shape[1] - k * SMALL_W), base + k) for k in range(_small_rows(shape))]


def _pack_small(p, loss):
    rows = []
    for n in SMALL_NAMES:
        t = p[n]
        nrows = _small_rows(t.shape)
        if t.shape[1] <= SMALL_W:
            rows.append(_pad_cols(t, SMALL_W))
        else:
            rows.append(_pad_cols(t, nrows * SMALL_W).reshape(nrows, SMALL_W))
    rows.append(_pad_cols(loss, SMALL_W))
    slab = jnp.concatenate(rows, axis=0)
    return jnp.pad(slab, ((0, SMALL_ROWS - slab.shape[0]), (0, 0)))


def kernel(x, mem, w_in, b_in, hg_lb_logits, hg_norm_w, ml_conv_w, ml_conv_b, ml_norm_w, w_out, ln1_g, ln1_b, ca_wq, ca_wkv, ca_wo, ln2_g, ln2_b, ffn_w_up, ffn_conv_w, ffn_conv_b, ffn_w_down, ln3_g, ln3_b, loss_target, m_w_in, m_b_in, m_hg_lb_logits, m_hg_norm_w, m_ml_conv_w, m_ml_conv_b, m_ml_norm_w, m_w_out, m_ln1_g, m_ln1_b, m_ca_wq, m_ca_wkv, m_ca_wo, m_ln2_g, m_ln2_b, m_ffn_w_up, m_ffn_conv_w, m_ffn_conv_b, m_ffn_w_down, m_ln3_g, m_ln3_b, v_w_in, v_b_in, v_hg_lb_logits, v_hg_norm_w, v_ml_conv_w, v_ml_conv_b, v_ml_norm_w, v_w_out, v_ln1_g, v_ln1_b, v_ca_wq, v_ca_wkv, v_ca_wo, v_ln2_g, v_ln2_b, v_ffn_w_up, v_ffn_conv_w, v_ffn_conv_b, v_ffn_w_down, v_ln3_g, v_ln3_b):
    params = dict(w_in=w_in, b_in=b_in, hg_lb_logits=hg_lb_logits, hg_norm_w=hg_norm_w, ml_conv_w=ml_conv_w,
                  ml_conv_b=ml_conv_b, ml_norm_w=ml_norm_w, w_out=w_out, ln1_g=ln1_g, ln1_b=ln1_b, ca_wq=ca_wq,
                  ca_wkv=ca_wkv, ca_wo=ca_wo, ln2_g=ln2_g, ln2_b=ln2_b, ffn_w_up=ffn_w_up, ffn_conv_w=ffn_conv_w,
                  ffn_conv_b=ffn_conv_b, ffn_w_down=ffn_w_down, ln3_g=ln3_g, ln3_b=ln3_b)
    mom1 = dict(w_in=m_w_in, b_in=m_b_in, hg_lb_logits=m_hg_lb_logits, hg_norm_w=m_hg_norm_w,
                ml_conv_w=m_ml_conv_w, ml_conv_b=m_ml_conv_b, ml_norm_w=m_ml_norm_w, w_out=m_w_out, ln1_g=m_ln1_g,
                ln1_b=m_ln1_b, ca_wq=m_ca_wq, ca_wkv=m_ca_wkv, ca_wo=m_ca_wo, ln2_g=m_ln2_g, ln2_b=m_ln2_b,
                ffn_w_up=m_ffn_w_up, ffn_conv_w=m_ffn_conv_w, ffn_conv_b=m_ffn_conv_b, ffn_w_down=m_ffn_w_down,
                ln3_g=m_ln3_g, ln3_b=m_ln3_b)
    mom2 = dict(w_in=v_w_in, b_in=v_b_in, hg_lb_logits=v_hg_lb_logits, hg_norm_w=v_hg_norm_w,
                ml_conv_w=v_ml_conv_w, ml_conv_b=v_ml_conv_b, ml_norm_w=v_ml_norm_w, w_out=v_w_out, ln1_g=v_ln1_g,
                ln1_b=v_ln1_b, ca_wq=v_ca_wq, ca_wkv=v_ca_wkv, ca_wo=v_ca_wo, ln2_g=v_ln2_g, ln2_b=v_ln2_b,
                ffn_w_up=v_ffn_w_up, ffn_conv_w=v_ffn_conv_w, ffn_conv_b=v_ffn_conv_b, ffn_w_down=v_ffn_w_down,
                ln3_g=v_ln3_g, ln3_b=v_ln3_b)

    x_idx, y_idx, c_idx = _coords()
    as_index = lambda v: jnp.reshape(v, (1,)).astype(jnp.int32)
    me = as_index(4 * x_idx + 2 * y_idx + c_idx)
    small_params = {n: params[n] for n in SMALL_NAMES}

    shards = {n: _shard_2d(n, params[n]) for n in SHARDED_NAMES}
    m_shards = {n: _shard_2d(n, mom1[n]) for n in SHARDED_NAMES}
    v_shards = {n: _shard_2d(n, mom2[n]) for n in SHARDED_NAMES}
    outgoing = {n: shards[n] if "conv" in n else shards[n].astype(BF16) for n in SHARDED_NAMES}
    to_send = lambda names: [outgoing[n] for n in names]
    glue = [*m_shards.values(), *v_shards.values(), *shards.values(),
            *[outgoing[n] for n in SHARDED_NAMES if n not in FIRST_NAMES]]
    first = dict(zip(FIRST_NAMES, _two_level_gather(to_send(FIRST_NAMES), glue, "weights_gather_first")))
    mid_started, through = _direct_start(True, to_send(MID_NAMES), first["w_in"], "weights_gather_start_mid")
    ffn_started, through = _direct_start(True, to_send(FFN_UP_NAMES), through, "weights_gather_start_ffn_up")
    down_started, first["w_in"] = _direct_start(True, to_send(FFN_DOWN_NAMES), through,
                                                "weights_gather_start_ffn_down")

    def gathered_weights(names, started, after, tag):
        mine, lands = _direct_wait(True, started, after, "weights_gather_wait_" + tag)
        return {n: lax.dynamic_update_index_in_dim(land, own, me[0], 0) for n, own, land in zip(names, mine, lands)}

    started, own_stacks = {}, {}

    def start_group(names, tag):
        def hook(grads, through):
            own_stacks[tag] = [_owner_stack(n, grads).astype(BF16) for n in names]
            started[tag], through = _direct_start(False, own_stacks[tag], through, "grads_start_" + tag)
            return through
        return hook

    def start_small(grads, loss, through):
        started["small"], through = _direct_start(True, [_pack_small(_small_grads(grads), loss)], through,
                                                  "small_gather_start")
        return through

    loss, grad_x, grads = _local_step(
        x[0], mem[0], loss_target[0], _first_weights(first, small_params),
        lambda y: _mid_weights(gathered_weights(MID_NAMES, mid_started, y, "mid")),
        lambda x2: _ffn_up_weights(gathered_weights(FFN_UP_NAMES, ffn_started, x2, "ffn_up"), small_params),
        lambda hid: _ffn_down_weights(gathered_weights(FFN_DOWN_NAMES, down_started, hid, "ffn_down")),
        start_group(FFN_NAMES, "ffn"), start_group(MID_NAMES, "mid"), start_small, start_group(FIRST_NAMES, "last"))

    sharded_out = {}

    def update_group(names, tag, after):
        _, lands = _direct_wait(False, started[tag], after, "grads_wait_" + tag)
        for n, st, land in zip(names, own_stacks[tag], lands):
            res = _adamw_sharded(me, st, land, shards[n], m_shards[n], v_shards[n], "adamw_" + n)
            sharded_out[n] = [_shard_like(n, t, params[n]) for t in res]

    update_group(FFN_NAMES, "ffn", grad_x)
    update_group(MID_NAMES, "mid", grad_x)
    own_small, small_lands = _direct_wait(True, started["small"], grad_x, "small_gather_wait")
    small_parts = lax.dynamic_update_index_in_dim(small_lands[0], own_small[0], me[0], 0)
    small_out, total_loss = _adamw_replicated(small_parts, small_params, {n: mom1[n] for n in SMALL_NAMES},
                                              {n: mom2[n] for n in SMALL_NAMES})
    done = [t for n in FFN_NAMES + MID_NAMES for t in sharded_out[n]]
    done += [t for small in small_out for t in small.values()]
    update_group(FIRST_NAMES, "last", done)

    outs = []
    for k, small in enumerate(small_out):
        outs.extend(sharded_out[n][k] if n in sharded_out else small[n] for n in WEIGHT_NAMES)
    return (total_loss[0, 0], grad_x[None], *outs)
```

```python
import functools
import math

import jax
import jax.numpy as jnp
from jax import lax
from jax.experimental import pallas as pl
from jax.experimental.pallas import tpu as pltpu

F32 = jnp.float32
BF16 = jnp.bfloat16
HIGHEST = lax.Precision.HIGHEST
MESH = pl.DeviceIdType.MESH

N_DEV = 8
D_MODEL = 1024
N_MEM = 256
N_HEADS = 4
D_HEAD = 128
D_GROUP = N_HEADS * D_HEAD
CHUNK = 64
ML_CONV = 4
FFN_CONV = 3
D_FF = 2816
D_UP = 2 * D_FF
CA_HEADS = 4
CA_DH = D_MODEL // CA_HEADS
LANES = 128
SUBLANES = 8
D_IN = 8 * D_GROUP + 2 * N_HEADS
D_IN_MAIN = 8 * D_GROUP
W_IN_SHARD = D_IN // N_DEV
UP_SHARD = D_UP // N_DEV
UP_SHARD_P = 768
D_UP_P = N_DEV * UP_SHARD_P
D_FF_P = D_UP_P // 2
ALPHA = 2.0 ** 0.25
LN_EPS = 1e-5
NEG_BIG = -1e30
ADAM_LR = 0.001
ADAM_B1 = 0.9
ADAM_B2 = 0.999
ADAM_EPS = 1e-08
ADAM_WD = 0.01
ADAM_STEP = 10
VMEM_LIMIT = 56 * 1024 * 1024

SEG_HQ, SEG_HF, SEG_HI, SEG_HG, SEG_MQ, SEG_MK, SEG_MV, SEG_MO = (4 * i for i in range(8))


def _params(sem):
    return pltpu.CompilerParams(dimension_semantics=sem, vmem_limit_bytes=VMEM_LIMIT)


def _dg(a, b, ca, cb, precision=None):
    return lax.dot_general(a, b, (((ca,), (cb,)), ((), ())), precision=precision,
                           preferred_element_type=F32)


def _nn_raw(a, b):
    return _dg(a.astype(BF16), b.astype(BF16), 1, 0)


def _nt_raw(a, b):
    return _dg(a.astype(BF16), b.astype(BF16), 1, 1)


def _tn_raw(a, b):
    return _dg(a.astype(BF16), b.astype(BF16), 0, 0)


@jax.custom_vjp
def _nn(a, b):
    return _nn_raw(a, b)


_nn.defvjp(lambda a, b: (_nn_raw(a, b), (a, b)),
           lambda res, g: (_nt_raw(g, res[1]), _tn_raw(res[0], g)))


@jax.custom_vjp
def _nt(a, b):
    return _nt_raw(a, b)


_nt.defvjp(lambda a, b: (_nt_raw(a, b), (a, b)),
           lambda res, g: (_nn_raw(g, res[1]), _tn_raw(g, res[0])))


@jax.custom_vjp
def _tn(a, b):
    return _tn_raw(a, b)


_tn.defvjp(lambda a, b: (_tn_raw(a, b), (a, b)),
           lambda res, g: (_nt_raw(res[1], g), _nn_raw(res[0], g)))


def _layer_norm(z, g, b):
    mu = jnp.mean(z, axis=-1, keepdims=True)
    var = jnp.mean(jnp.square(z - mu), axis=-1, keepdims=True)
    return (z - mu) * lax.rsqrt(var + LN_EPS) * g + b


def _matmul_nn(a, w, bias, tm, tn, name, out_dtype=F32):
    m, k = a.shape
    if w.ndim == 3:
        n = w.shape[0] * w.shape[2]
        assert tn == w.shape[2]
        w_spec = pl.BlockSpec((None, k, tn), lambda i, j: (j, 0, 0))
    else:
        n = w.shape[1]
        w_spec = pl.BlockSpec((k, tn), lambda i, j: (0, j))

    def body(*refs):
        a_ref, w_ref = refs[0], refs[1]
        o_ref = refs[-1]
        acc = _nn_raw(a_ref[...], w_ref[...])
        if bias is not None:
            acc = acc + refs[2][...]
        o_ref[...] = acc.astype(o_ref.dtype)

    in_specs = [pl.BlockSpec((tm, k), lambda i, j: (i, 0)), w_spec]
    args = [a, w]
    if bias is not None:
        in_specs.append(pl.BlockSpec((1, tn), lambda i, j: (0, j)))
        args.append(bias)
    return pl.pallas_call(
        body, name=name, grid=(m // tm, n // tn), in_specs=in_specs,
        out_specs=pl.BlockSpec((tm, tn), lambda i, j: (i, j)),
        out_shape=jax.ShapeDtypeStruct((m, n), out_dtype),
        compiler_params=_params(("parallel", "parallel")),
    )(*args)


def _matmul_nt(pairs, add, scale, tm, tk, name, out_dtype=F32):
    m = pairs[0][0].shape[0]
    k = pairs[0][1].shape[-2]
    groups = []
    in_specs, args = [], []
    for pair in pairs:
        d, w = pair[0], pair[1]
        in_specs.append(pl.BlockSpec((tm, d.shape[1]), lambda i, j: (i, 0)))
        if w.ndim == 3:
            g = d.shape[1] // w.shape[2]
            blk = pair[2] // g
            in_specs.append(pl.BlockSpec((g, tk, w.shape[2]), lambda i, j, blk=blk: (blk, j, 0)))
            groups.append((g, w.shape[2]))
        else:
            in_specs.append(pl.BlockSpec((tk, w.shape[1]), lambda i, j: (j, 0)))
            groups.append(None)
        args += [d, w]
    if add is not None:
        in_specs.append(pl.BlockSpec((tm, tk), lambda i, j: (i, j)))
        args.append(add)

    def body(*refs):
        o_ref = refs[-1]
        acc = None
        for p, grp in enumerate(groups):
            d_ref, w_ref = refs[2 * p], refs[2 * p + 1]
            if grp is None:
                terms = [_nt_raw(d_ref[...], w_ref[...])]
            else:
                terms = [_nt_raw(d_ref[:, g * grp[1]:(g + 1) * grp[1]], w_ref[g]) for g in range(grp[0])]
            for t in terms:
                acc = t if acc is None else acc + t
        if add is not None:
            acc = acc + scale * refs[2 * len(groups)][...]
        o_ref[...] = acc.astype(o_ref.dtype)

    return pl.pallas_call(
        body, name=name, grid=(m // tm, k // tk), in_specs=in_specs,
        out_specs=pl.BlockSpec((tm, tk), lambda i, j: (i, j)),
        out_shape=jax.ShapeDtypeStruct((m, k), out_dtype),
        compiler_params=_params(("parallel", "parallel")),
    )(*args)


def _matmul_tn(a, b, tm, tn, tt, name, shards=None, shard0=0, group=1, into=None, colsum=False, rows=None, row0=0):
    t, m = a.shape
    n = b.shape[1]
    assert not colsum or tm == m
    n_in = 2 + (into is not None)
    out_dtype = BF16
    per_step = 1 if shards is None else group
    width = per_step * tn

    def body(*refs):
        a_ref, b_ref = refs[0], refs[1]
        o_ref, acc_ref = refs[n_in], refs[-1]
        first = pl.program_id(2) == 0

        @pl.when(first)
        def _():
            acc_ref[...] = jnp.zeros_like(acc_ref)

        if shards is None:
            acc_ref[...] += _tn_raw(a_ref[...], b_ref[...])
        else:
            lhs = a_ref[...].astype(BF16)
            for g in range(per_step):
                acc_ref[g] += _tn_raw(lhs, b_ref[:, g * tn:(g + 1) * tn])

        @pl.when(pl.program_id(2) == t // tt - 1)
        def _():
            o_ref[...] = acc_ref[...].astype(o_ref.dtype)

        if colsum:
            s_ref = refs[n_in + 1]

            @pl.when(first)
            def _():
                s_ref[...] = jnp.zeros_like(s_ref)

            s_ref[...] += jnp.sum(b_ref[...], axis=0, keepdims=True)

    in_specs = [pl.BlockSpec((tt, tm), lambda i, j, kk: (kk, i)),
                pl.BlockSpec((tt, width), lambda i, j, kk: (kk, j))]
    args = [a, b]
    aliases = {}
    if into is not None:
        in_specs.append(pl.BlockSpec(memory_space=pl.ANY))
        args.append(into)
        aliases = {2: 0}
    if shards is None:
        out_specs = [pl.BlockSpec((tm, tn), lambda i, j, kk: (row0 // tm + i, j))]
        out_shape = [jax.ShapeDtypeStruct((rows or m, n), out_dtype)]
        acc = pltpu.VMEM((tm, tn), F32)
    else:
        out_specs = [pl.BlockSpec((per_step, tm, tn), lambda i, j, kk: (shard0 // per_step + j, i, 0))]
        out_shape = [jax.ShapeDtypeStruct((shards, m, tn), out_dtype)]
        acc = pltpu.VMEM((per_step, tm, tn), F32)
    if colsum:
        out_specs.append(pl.BlockSpec((1, tn), lambda i, j, kk: (0, j)))
        out_shape.append(jax.ShapeDtypeStruct((1, n), F32))
    res = pl.pallas_call(
        body, name=name, grid=(m // tm, n // width, t // tt), in_specs=in_specs, out_specs=out_specs,
        out_shape=out_shape, input_output_aliases=aliases, scratch_shapes=[acc],
        compiler_params=_params(("parallel", "parallel", "arbitrary")),
    )(*args)
    return res if colsum else res[0]


ROW_TILE = 64


def _stack(ref, start, rows):
    return ref[pl.ds(start, rows), :].astype(F32).reshape(rows // SUBLANES, SUBLANES, LANES)


def _vreg_rows(ref, n):
    return [jnp.broadcast_to(ref[j:j + 1, :], (SUBLANES, LANES))[None] for j in range(n)]


def _column_total(acc):
    return jnp.sum(acc, axis=0, keepdims=True)


def _conv_fwd_tile(pad_ref, taps_w, bias, r0, rows):
    taps = len(taps_w)
    acc = bias
    for j in range(taps):
        acc = acc + _stack(pad_ref, SUBLANES - (taps - 1 - j) + r0, rows) * taps_w[j]
    return acc


def _conv_grads_tile(pad_ref, dpad_ref, dx_ref, taps_w, dws, r0, rows):
    taps = len(taps_w)
    x_rows = _stack(pad_ref, SUBLANES + r0, rows)
    dx = None
    for j in range(taps):
        d_shifted = _stack(dpad_ref, r0 + (taps - 1 - j), rows)
        term = d_shifted * taps_w[j]
        dx = term if dx is None else dx + term
        dws[j] = dws[j] + jnp.sum(d_shifted * x_rows, axis=0)
    dx_ref[r0:r0 + rows, :] = dx.reshape(rows, LANES).astype(dx_ref.dtype)
    return jnp.sum(dx, axis=0)


def _ml_conv_fwd(proj, conv_w, conv_b):
    s = proj.shape[0]
    nblk = 2 * D_GROUP // LANES

    def body(x_ref, w_ref, b_ref, o_ref, pad_ref):
        pad_ref[0:SUBLANES, :] = jnp.zeros((SUBLANES, LANES), F32)
        pad_ref[SUBLANES:, :] = x_ref[...].astype(F32)
        taps_w, bias = _vreg_rows(w_ref, ML_CONV), _vreg_rows(b_ref, 1)[0]
        for r0 in range(0, s, ROW_TILE):
            rows = min(ROW_TILE, s - r0)
            o_ref[r0:r0 + rows, :] = jax.nn.silu(_conv_fwd_tile(pad_ref, taps_w, bias, r0, rows)).reshape(rows, LANES)

    return pl.pallas_call(
        body, name="ml_conv_fwd", grid=(nblk,),
        in_specs=[pl.BlockSpec((s, LANES), lambda j: (0, SEG_MQ + j)),
                  pl.BlockSpec((ML_CONV, LANES), lambda j: (0, j)),
                  pl.BlockSpec((1, LANES), lambda j: (0, j))],
        out_specs=pl.BlockSpec((s, LANES), lambda j: (0, j)),
        out_shape=jax.ShapeDtypeStruct((s, 2 * D_GROUP), F32),
        scratch_shapes=[pltpu.VMEM((s + SUBLANES, LANES), F32)],
        compiler_params=_params(("parallel",)),
    )(proj, conv_w, conv_b)


def _ml_conv_bwd(proj, conv_w, conv_b, d_qk, d_proj):
    s = proj.shape[0]
    nblk = 2 * D_GROUP // LANES

    def body(x_ref, w_ref, b_ref, dy_ref, _, dx_ref, dw_ref, db_ref, dxs_ref, pad_ref, dpad_ref):
        pad_ref[0:SUBLANES, :] = jnp.zeros((SUBLANES, LANES), F32)
        pad_ref[SUBLANES:, :] = x_ref[...].astype(F32)
        dpad_ref[s:, :] = jnp.zeros((SUBLANES, LANES), F32)
        taps_w, bias = _vreg_rows(w_ref, ML_CONV), _vreg_rows(b_ref, 1)[0]
        db = jnp.zeros((SUBLANES, LANES), F32)
        for r0 in range(0, s, ROW_TILE):
            rows = min(ROW_TILE, s - r0)
            pre = _conv_fwd_tile(pad_ref, taps_w, bias, r0, rows)
            _, vjp = jax.vjp(jax.nn.silu, pre)
            d_pre, = vjp(_stack(dy_ref, r0, rows))
            dpad_ref[r0:r0 + rows, :] = d_pre.reshape(rows, LANES)
            db = db + jnp.sum(d_pre, axis=0)
        db_ref[...] = _column_total(db)
        dws = [jnp.zeros((SUBLANES, LANES), F32) for _ in range(ML_CONV)]
        dx_sum = jnp.zeros((SUBLANES, LANES), F32)
        for r0 in range(0, s, ROW_TILE):
            dx_sum = dx_sum + _conv_grads_tile(pad_ref, dpad_ref, dx_ref, taps_w, dws, r0, min(ROW_TILE, s - r0))
        dxs_ref[...] = _column_total(dx_sum)
        for j in range(ML_CONV):
            dw_ref[j:j + 1, :] = _column_total(dws[j])

    return pl.pallas_call(
        body, name="ml_conv_bwd", grid=(nblk,),
        in_specs=[pl.BlockSpec((s, LANES), lambda j: (0, SEG_MQ + j)),
                  pl.BlockSpec((ML_CONV, LANES), lambda j: (0, j)),
                  pl.BlockSpec((1, LANES), lambda j: (0, j)),
                  pl.BlockSpec((s, LANES), lambda j: (0, j)),
                  pl.BlockSpec(memory_space=pl.ANY)],
        out_specs=[pl.BlockSpec((s, LANES), lambda j: (0, SEG_MQ + j)),
                   pl.BlockSpec((ML_CONV, LANES), lambda j: (0, j)),
                   pl.BlockSpec((1, LANES), lambda j: (0, j)),
                   pl.BlockSpec((1, LANES), lambda j: (0, j))],
        out_shape=[jax.ShapeDtypeStruct(d_proj.shape, d_proj.dtype),
                   jax.ShapeDtypeStruct((ML_CONV, 2 * D_GROUP), F32),
                   jax.ShapeDtypeStruct((1, 2 * D_GROUP), F32),
                   jax.ShapeDtypeStruct((1, 2 * D_GROUP), F32)],
        input_output_aliases={4: 0},
        scratch_shapes=[pltpu.VMEM((s + SUBLANES, LANES), F32), pltpu.VMEM((s + SUBLANES, LANES), F32)],
        compiler_params=_params(("parallel",)),
    )(proj, conv_w, conv_b, d_qk, d_proj)


def _gelu_mul(a, b):
    return jax.nn.gelu(a) * b


GELU_C = math.sqrt(2.0 / math.pi)
GELU_K = 0.044715


def _gelu_mul_grads(a, b, d):
    a2 = a * a
    t = jnp.tanh(GELU_C * (a + GELU_K * (a * a2)))
    cdf = 0.5 * (1.0 + t)
    slope = cdf + (0.5 * GELU_C) * a * (1.0 - t * t) * (1.0 + (3.0 * GELU_K) * a2)
    return d * b * slope, d * (a * cdf)


FFN_BLOCKS = D_FF_P // LANES


def _ffn_conv_fwd(u, conv_w, conv_b):
    s = u.shape[0]

    def body(g_ref, v_ref, wg_ref, wv_ref, bg_ref, bv_ref, o_ref, gpad_ref, vpad_ref):
        for pad_ref, x_ref in ((gpad_ref, g_ref), (vpad_ref, v_ref)):
            pad_ref[0:SUBLANES, :] = jnp.zeros((SUBLANES, LANES), F32)
            pad_ref[SUBLANES:, :] = x_ref[...].astype(F32)
        taps_g, bias_g = _vreg_rows(wg_ref, FFN_CONV), _vreg_rows(bg_ref, 1)[0]
        taps_v, bias_v = _vreg_rows(wv_ref, FFN_CONV), _vreg_rows(bv_ref, 1)[0]
        for r0 in range(0, s, ROW_TILE):
            rows = min(ROW_TILE, s - r0)
            ug = _conv_fwd_tile(gpad_ref, taps_g, bias_g, r0, rows)
            uv = _conv_fwd_tile(vpad_ref, taps_v, bias_v, r0, rows)
            o_ref[r0:r0 + rows, :] = _gelu_mul(ug, uv).reshape(rows, LANES).astype(o_ref.dtype)

    col = lambda off: (lambda j: (0, off + j))
    return pl.pallas_call(
        body, name="ffn_conv_fwd", grid=(FFN_BLOCKS,),
        in_specs=[pl.BlockSpec((s, LANES), col(0)), pl.BlockSpec((s, LANES), col(FFN_BLOCKS)),
                  pl.BlockSpec((FFN_CONV, LANES), col(0)), pl.BlockSpec((FFN_CONV, LANES), col(FFN_BLOCKS)),
                  pl.BlockSpec((1, LANES), col(0)), pl.BlockSpec((1, LANES), col(FFN_BLOCKS))],
        out_specs=pl.BlockSpec((s, LANES), col(0)),
        out_shape=jax.ShapeDtypeStruct((s, D_FF_P), BF16),
        scratch_shapes=[pltpu.VMEM((s + SUBLANES, LANES), F32), pltpu.VMEM((s + SUBLANES, LANES), F32)],
        compiler_params=_params(("parallel",)),
    )(u, u, conv_w, conv_w, conv_b, conv_b)


def _ffn_conv_bwd(u, conv_w, conv_b, d_h):
    s = u.shape[0]

    def body(g_ref, v_ref, wg_ref, wv_ref, bg_ref, bv_ref, dh_ref,
             dug_ref, duv_ref, dwg_ref, dwv_ref, dbg_ref, dbv_ref,
             gpad_ref, vpad_ref, dgpad_ref, dvpad_ref):
        for pad_ref, x_ref in ((gpad_ref, g_ref), (vpad_ref, v_ref)):
            pad_ref[0:SUBLANES, :] = jnp.zeros((SUBLANES, LANES), F32)
            pad_ref[SUBLANES:, :] = x_ref[...].astype(F32)
        dgpad_ref[s:, :] = jnp.zeros((SUBLANES, LANES), F32)
        dvpad_ref[s:, :] = jnp.zeros((SUBLANES, LANES), F32)
        taps_g, bias_g = _vreg_rows(wg_ref, FFN_CONV), _vreg_rows(bg_ref, 1)[0]
        taps_v, bias_v = _vreg_rows(wv_ref, FFN_CONV), _vreg_rows(bv_ref, 1)[0]
        dbg = jnp.zeros((SUBLANES, LANES), F32)
        dbv = jnp.zeros((SUBLANES, LANES), F32)
        for r0 in range(0, s, ROW_TILE):
            rows = min(ROW_TILE, s - r0)
            ug = _conv_fwd_tile(gpad_ref, taps_g, bias_g, r0, rows)
            uv = _conv_fwd_tile(vpad_ref, taps_v, bias_v, r0, rows)
            d_ug, d_uv = _gelu_mul_grads(ug, uv, _stack(dh_ref, r0, rows))
            dgpad_ref[r0:r0 + rows, :] = d_ug.reshape(rows, LANES)
            dvpad_ref[r0:r0 + rows, :] = d_uv.reshape(rows, LANES)
            dbg = dbg + jnp.sum(d_ug, axis=0)
            dbv = dbv + jnp.sum(d_uv, axis=0)
        dbg_ref[...] = _column_total(dbg)
        dbv_ref[...] = _column_total(dbv)
        for pad_ref, dpad_ref, taps_w, dx_ref, dw_ref in ((gpad_ref, dgpad_ref, taps_g, dug_ref, dwg_ref),
                                                          (vpad_ref, dvpad_ref, taps_v, duv_ref, dwv_ref)):
            dws = [jnp.zeros((SUBLANES, LANES), F32) for _ in range(FFN_CONV)]
            for r0 in range(0, s, ROW_TILE):
                _conv_grads_tile(pad_ref, dpad_ref, dx_ref, taps_w, dws, r0, min(ROW_TILE, s - r0))
            for j in range(FFN_CONV):
                dw_ref[j:j + 1, :] = _column_total(dws[j])

    col = lambda off: (lambda j: (0, off + j))
    seq = pl.BlockSpec((s, LANES), col(0))
    return pl.pallas_call(
        body, name="ffn_conv_bwd", grid=(FFN_BLOCKS,),
        in_specs=[pl.BlockSpec((s, LANES), col(0)), pl.BlockSpec((s, LANES), col(FFN_BLOCKS)),
                  pl.BlockSpec((FFN_CONV, LANES), col(0)), pl.BlockSpec((FFN_CONV, LANES), col(FFN_BLOCKS)),
                  pl.BlockSpec((1, LANES), col(0)), pl.BlockSpec((1, LANES), col(FFN_BLOCKS)), seq],
        out_specs=[seq, seq, pl.BlockSpec((FFN_CONV, LANES), col(0)), pl.BlockSpec((FFN_CONV, LANES), col(0)),
                   pl.BlockSpec((1, LANES), col(0)), pl.BlockSpec((1, LANES), col(0))],
        out_shape=[jax.ShapeDtypeStruct((s, D_FF_P), BF16), jax.ShapeDtypeStruct((s, D_FF_P), BF16),
                   jax.ShapeDtypeStruct((FFN_CONV, D_FF_P), F32), jax.ShapeDtypeStruct((FFN_CONV, D_FF_P), F32),
                   jax.ShapeDtypeStruct((1, D_FF_P), F32), jax.ShapeDtypeStruct((1, D_FF_P), F32)],
        scratch_shapes=[pltpu.VMEM((s + SUBLANES, LANES), F32) for _ in range(4)],
        compiler_params=_params(("parallel",)),
    )(u, u, conv_w, conv_w, conv_b, conv_b, d_h)


def _chunk_masks(c):
    row = lax.broadcasted_iota(jnp.int32, (c, c), 0)
    col = lax.broadcasted_iota(jnp.int32, (c, c), 1)
    return row, col


@jax.custom_vjp
def _split_heads(x):
    return tuple(x[:, h * D_HEAD:(h + 1) * D_HEAD] for h in range(N_HEADS))


_split_heads.defvjp(lambda x: (_split_heads(x), None), lambda _, gs: (jnp.concatenate(gs, axis=1),))


@jax.custom_vjp
def _merge_heads(xs):
    return jnp.concatenate(xs, axis=1)


_merge_heads.defvjp(lambda xs: (_merge_heads(xs), None), lambda _, g: (_split_heads(g),))


@jax.custom_vjp
def _split_chunks(x):
    return tuple(x[i * CHUNK:(i + 1) * CHUNK] for i in range(x.shape[0] // CHUNK))


_split_chunks.defvjp(lambda x: (_split_chunks(x), None), lambda _, gs: (jnp.concatenate(gs, axis=0),))


@jax.custom_vjp
def _merge_chunks(xs):
    return jnp.concatenate(xs, axis=0)


_merge_chunks.defvjp(lambda xs: (_merge_chunks(xs), None), lambda _, g: (_split_chunks(g),))


def _blocks(x):
    return [_split_heads(rows) for rows in _split_chunks(x)]


def _per_chunk_rows(per_chunk, rid):
    out = per_chunk[0]
    for i in range(1, len(per_chunk)):
        out = jnp.where(rid >= i * CHUNK, per_chunk[i], out)
    return out


HEADS = range(N_HEADS)
CHUNKS_PER_STEP = 8
ML_CHUNKS_PER_STEP = 1


def _hg_chunk(hq, hf, hi, hgate, l0, l1, nw, sts):
    n = hq.shape[0] // CHUNK
    causal = _chunk_masks(CHUNK)
    causal = causal[1] <= causal[0]
    mx = lax.stop_gradient(jnp.maximum(l0, l1))
    e0 = jnp.exp(l0 - mx)
    e1 = jnp.exp(l1 - mx)
    lb = e0 / (e0 + e1)
    sig = jax.nn.sigmoid(hf)
    lf = jnp.log(lb + (1.0 - lb) * sig)
    k = (1.0 - lb) * jax.nn.sigmoid(-hf)
    q = jax.nn.silu(hq)
    tri = causal.astype(F32)
    b = _merge_chunks(tuple(_dg(tri, rows, 1, 0, HIGHEST) for rows in _split_chunks(lf)))
    rid = lax.broadcasted_iota(jnp.int32, b.shape, 0)
    pick = lambda r: jnp.sum(jnp.where(rid == r, b, 0.0), axis=0, keepdims=True)
    b_last_c = [pick(i * CHUNK + CHUNK - 1) for i in range(n)]
    b_ref = _per_chunk_rows([pick(i * CHUNK + CHUNK // 2 - 1) for i in range(n)], rid)
    b_last = _per_chunk_rows(b_last_c, rid)
    qa = _blocks(q * jnp.exp(b - b_ref))
    ka = _blocks(k * jnp.exp(b_ref - b))
    qe = _blocks(q * jnp.exp(b))
    kd = _blocks(k * jnp.exp(b_last - b))
    decay = [_split_heads(jnp.exp(b_last_c[i])) for i in range(n)]
    v = _blocks(hi)
    chunks = range(n)
    attn = [[jnp.where(causal, _nt(qa[i][h], ka[i][h]), 0.0) for h in HEADS] for i in chunks]
    intra = [[_nn(attn[i][h], v[i][h]) for h in HEADS] for i in chunks]
    kv = [[_tn(v[i][h], kd[i][h]) for h in HEADS] for i in chunks]
    normed = []
    for i in chunks:
        inter = [_nt(qe[i][h], sts[h]) for h in HEADS]
        sts = tuple(decay[i][h] * sts[h] + kv[i][h] for h in HEADS)
        o = [intra[i][h] + inter[h] for h in HEADS]
        normed.append(_merge_heads(tuple(o[h] * lax.rsqrt(jnp.mean(o[h] * o[h], axis=-1, keepdims=True) + LN_EPS)
                                         for h in HEADS)))
    return _merge_chunks(tuple(normed)) * nw * jax.nn.silu(hgate), sts


def _seg(ref, seg):
    return ref[:, seg * D_GROUP:(seg + 1) * D_GROUP]


def _hgrn2_fwd(proj, logits, norm_w):
    s = proj.shape[0]
    rows = CHUNKS_PER_STEP * CHUNK
    nc = s // rows

    def body(p_ref, lg_ref, nw_ref, y_ref, st_out_ref, st_scr):
        @pl.when(pl.program_id(0) == 0)
        def _():
            st_scr[...] = jnp.zeros_like(st_scr)

        sts = tuple(st_scr[h] for h in HEADS)
        y, sts_new = _hg_chunk(_seg(p_ref, 0), _seg(p_ref, 1), _seg(p_ref, 2), _seg(p_ref, 3),
                               lg_ref[0:1, :], lg_ref[1:2, :], nw_ref[...], sts)
        y_ref[...] = y.astype(y_ref.dtype)
        for h in HEADS:
            st_out_ref[h] = sts[h]
            st_scr[h] = sts_new[h]

    return pl.pallas_call(
        body, name="hgrn2_fwd", grid=(nc,),
        in_specs=[pl.BlockSpec((rows, 4 * D_GROUP), lambda c: (c, 0)),
                  pl.BlockSpec((2, D_GROUP), lambda c: (0, 0)),
                  pl.BlockSpec((1, D_GROUP), lambda c: (0, 0))],
        out_specs=[pl.BlockSpec((rows, D_GROUP), lambda c: (c, 0)),
                   pl.BlockSpec((None, N_HEADS, D_HEAD, D_HEAD), lambda c: (c, 0, 0, 0))],
        out_shape=[jax.ShapeDtypeStruct((s, 2 * D_GROUP), BF16),
                   jax.ShapeDtypeStruct((nc, N_HEADS, D_HEAD, D_HEAD), F32)],
        scratch_shapes=[pltpu.VMEM((N_HEADS, D_HEAD, D_HEAD), F32)],
        compiler_params=_params(("arbitrary",)),
    )(proj, logits, norm_w)


def _hgrn2_bwd(proj, logits, norm_w, states, d_y):
    s = proj.shape[0]
    rows = CHUNKS_PER_STEP * CHUNK
    nc = s // rows

    def body(p_ref, lg_ref, nw_ref, st_ref, dy_ref, dp_ref, dl_ref, dnw_ref, dsum_ref, dst_scr):
        @pl.when(pl.program_id(0) == 0)
        def _():
            dst_scr[...] = jnp.zeros_like(dst_scr)
            dl_ref[...] = jnp.zeros_like(dl_ref)
            dnw_ref[...] = jnp.zeros_like(dnw_ref)
            dsum_ref[...] = jnp.zeros_like(dsum_ref)

        _, vjp = jax.vjp(_hg_chunk, _seg(p_ref, 0), _seg(p_ref, 1), _seg(p_ref, 2), _seg(p_ref, 3),
                         lg_ref[0:1, :], lg_ref[1:2, :], nw_ref[...], tuple(st_ref[h] for h in HEADS))
        d_hq, d_hf, d_hi, d_hg, d_l0, d_l1, d_nw, d_sts = vjp((dy_ref[...], tuple(dst_scr[h] for h in HEADS)))
        for seg, val in enumerate((d_hq, d_hf, d_hi, d_hg)):
            dp_ref[:, seg * D_GROUP:(seg + 1) * D_GROUP] = val.astype(dp_ref.dtype)
            dsum_ref[:, seg * D_GROUP:(seg + 1) * D_GROUP] += jnp.sum(val, axis=0, keepdims=True)
        dl_ref[0:1, :] += d_l0
        dl_ref[1:2, :] += d_l1
        dnw_ref[...] += d_nw
        for h in HEADS:
            dst_scr[h] = d_sts[h]

    rev = lambda c: nc - 1 - c
    return pl.pallas_call(
        body, name="hgrn2_bwd", grid=(nc,),
        in_specs=[pl.BlockSpec((rows, 4 * D_GROUP), lambda c: (rev(c), 0)),
                  pl.BlockSpec((2, D_GROUP), lambda c: (0, 0)),
                  pl.BlockSpec((1, D_GROUP), lambda c: (0, 0)),
                  pl.BlockSpec((None, N_HEADS, D_HEAD, D_HEAD), lambda c: (rev(c), 0, 0, 0)),
                  pl.BlockSpec((rows, D_GROUP), lambda c: (rev(c), 0))],
        out_specs=[pl.BlockSpec((rows, 4 * D_GROUP), lambda c: (rev(c), 0)),
                   pl.BlockSpec((2, D_GROUP), lambda c: (0, 0)),
                   pl.BlockSpec((1, D_GROUP), lambda c: (0, 0)),
                   pl.BlockSpec((1, 4 * D_GROUP), lambda c: (0, 0))],
        out_shape=[jax.ShapeDtypeStruct((s, D_IN_MAIN), BF16), jax.ShapeDtypeStruct((2, D_GROUP), F32),
                   jax.ShapeDtypeStruct((1, D_GROUP), F32), jax.ShapeDtypeStruct((1, 4 * D_GROUP), F32)],
        scratch_shapes=[pltpu.VMEM((N_HEADS, D_HEAD, D_HEAD), F32)],
        compiler_params=_params(("arbitrary",)),
    )(proj, logits, norm_w, states, d_y)


def _gate_column(gates, lane, idx):
    return jnp.sum(jnp.where(lane == idx, gates, 0.0), axis=1, keepdims=True)


def _head_layer_norm(h):
    mu = jnp.mean(h, axis=-1, keepdims=True)
    var = jnp.mean(jnp.square(h - mu), axis=-1, keepdims=True)
    return (h - mu) * lax.rsqrt(var + LN_EPS)


def _ml_chunk(qc, kc, v, mo, gates, nw, cts, ns, ms):
    n = qc.shape[0] // CHUNK
    row, col = _chunk_masks(CHUNK)
    mask = col <= row
    eye = col == row
    to_row = lambda t: jnp.sum(jnp.where(eye, t, 0.0), axis=0, keepdims=True)
    q = _blocks(qc * (D_HEAD ** -0.5))
    k = _blocks(kc)
    vs = _blocks(v)
    gate_rows = _split_chunks(gates)
    lane = lax.broadcasted_iota(jnp.int32, gate_rows[0].shape, 1)
    each = [(i, h) for i in range(n) for h in HEADS]
    on_each = lambda f: {ih: f(*ih) for ih in each}
    ig = on_each(lambda i, h: _gate_column(gate_rows[i], lane, h))
    lf = on_each(lambda i, h: jax.nn.log_sigmoid(_gate_column(gate_rows[i], lane, N_HEADS + h)))
    lf_row = on_each(lambda i, h: to_row(lf[i, h]))
    ig_row = on_each(lambda i, h: to_row(ig[i, h]))
    b_col = on_each(lambda i, h: jnp.sum(jnp.where(mask, lf_row[i, h], 0.0), axis=1, keepdims=True))
    b_row = on_each(lambda i, h: jnp.sum(jnp.where(row <= col, lf[i, h], 0.0), axis=0, keepdims=True))
    g = on_each(lambda i, h: jnp.sum(lf[i, h], axis=0, keepdims=True))
    d = on_each(lambda i, h: jnp.where(mask, b_col[i, h] - b_row[i, h] + ig_row[i, h], -jnp.inf))
    a = on_each(lambda i, h: g[i, h] - b_col[i, h] + ig[i, h])
    m_at = {(0, h): ms[h] for h in HEADS}
    for i, h in each:
        m_at[i + 1, h] = lax.stop_gradient(jnp.maximum(g[i, h] + m_at[i, h], jnp.max(a[i, h], axis=0, keepdims=True)))
    inter = on_each(lambda i, h: b_col[i, h] + m_at[i, h])
    m_t = on_each(lambda i, h: lax.stop_gradient(jnp.maximum(inter[i, h], jnp.max(d[i, h], axis=1, keepdims=True))))
    qk = on_each(lambda i, h: _nt(q[i][h], k[i][h]))
    sc = on_each(lambda i, h: qk[i, h] * jnp.exp(d[i, h] - m_t[i, h]))
    w_inter = on_each(lambda i, h: jnp.exp(inter[i, h] - m_t[i, h]))
    sv = on_each(lambda i, h: _nn(sc[i, h], vs[i][h]))
    decay = on_each(lambda i, h: jnp.exp(g[i, h] + m_at[i, h] - m_at[i + 1, h]))
    wk = on_each(lambda i, h: k[i][h] * jnp.exp(a[i, h] - m_at[i + 1, h]))
    kv = on_each(lambda i, h: _tn(vs[i][h], wk[i, h]))
    normed = []
    for i in range(n):
        qc_state = [_nt(q[i][h], cts[h]) for h in HEADS]
        num = [sv[i, h] + w_inter[i, h] * qc_state[h] for h in HEADS]
        den = [jnp.sum(sc[i, h], axis=1, keepdims=True)
               + w_inter[i, h] * jnp.sum(q[i][h] * ns[h], axis=1, keepdims=True) for h in HEADS]
        hh = [num[h] / jnp.maximum(jnp.abs(den[h]), jnp.exp(-m_t[i, h])) for h in HEADS]
        cts = tuple(decay[i, h] * cts[h] + kv[i, h] for h in HEADS)
        ns = tuple(decay[i, h] * ns[h] + jnp.sum(wk[i, h], axis=0, keepdims=True) for h in HEADS)
        normed.append(_merge_heads(tuple(_head_layer_norm(hh[h]) for h in HEADS)))
    y = jax.nn.sigmoid(mo) * (_merge_chunks(tuple(normed)) * nw)
    return y, cts, ns, tuple(m_at[n, h] for h in HEADS)


def _mlstm_fwd(qk, proj, gates, norm_w, y):
    s = proj.shape[0]
    rows = ML_CHUNKS_PER_STEP * CHUNK
    nc = s // rows

    def body(qk_ref, vo_ref, g_ref, nw_ref, _, y_ref, ct_out, n_out, m_out, ct_scr, n_scr, m_scr):
        @pl.when(pl.program_id(0) == 0)
        def _():
            ct_scr[...] = jnp.zeros_like(ct_scr)
            n_scr[...] = jnp.zeros_like(n_scr)
            m_scr[...] = jnp.full(m_scr.shape, NEG_BIG, F32)

        cts = tuple(ct_scr[h] for h in HEADS)
        ns = tuple(n_scr[h] for h in HEADS)
        ms = tuple(m_scr[h] for h in HEADS)
        y, cts_new, ns_new, ms_new = _ml_chunk(_seg(qk_ref, 0), _seg(qk_ref, 1), _seg(vo_ref, 0), _seg(vo_ref, 1),
                                               g_ref[...], nw_ref[...], cts, ns, ms)
        y_ref[...] = y.astype(y_ref.dtype)
        for h in HEADS:
            ct_out[h], n_out[h], m_out[h] = cts[h], ns[h], ms[h]
            ct_scr[h], n_scr[h], m_scr[h] = cts_new[h], ns_new[h], ms_new[h]

    st = lambda r, w: pl.BlockSpec((None, N_HEADS, r, w), lambda c: (c, 0, 0, 0))
    return pl.pallas_call(
        body, name="mlstm_fwd", grid=(nc,),
        in_specs=[pl.BlockSpec((rows, 2 * D_GROUP), lambda c: (c, 0)),
                  pl.BlockSpec((rows, 2 * D_GROUP), lambda c: (c, 3)),
                  pl.BlockSpec((rows, LANES), lambda c: (c, 0)),
                  pl.BlockSpec((1, D_GROUP), lambda c: (0, 0)),
                  pl.BlockSpec(memory_space=pl.ANY)],
        out_specs=[pl.BlockSpec((rows, D_GROUP), lambda c: (c, 1)),
                   st(D_HEAD, D_HEAD), st(1, D_HEAD), st(1, 1)],
        out_shape=[jax.ShapeDtypeStruct(y.shape, y.dtype),
                   jax.ShapeDtypeStruct((nc, N_HEADS, D_HEAD, D_HEAD), F32),
                   jax.ShapeDtypeStruct((nc, N_HEADS, 1, D_HEAD), F32),
                   jax.ShapeDtypeStruct((nc, N_HEADS, 1, 1), F32)],
        input_output_aliases={4: 0},
        scratch_shapes=[pltpu.VMEM((N_HEADS, D_HEAD, D_HEAD), F32), pltpu.VMEM((N_HEADS, 1, D_HEAD), F32),
                        pltpu.VMEM((N_HEADS, 1, 1), F32)],
        compiler_params=_params(("arbitrary",)),
    )(qk, proj, gates, norm_w, y)


def _mlstm_bwd(qk, proj, gates, norm_w, ct_s, n_s, m_s, d_y, d_proj):
    s = proj.shape[0]
    rows = ML_CHUNKS_PER_STEP * CHUNK
    nc = s // rows

    def body(qk_ref, vo_ref, g_ref, nw_ref, ct_ref, n_ref, m_ref, dy_ref, _,
             dp_ref, dqk_ref, dg_ref, dnw_ref, dsum_ref, dct_scr, dn_scr):
        @pl.when(pl.program_id(0) == 0)
        def _():
            dct_scr[...] = jnp.zeros_like(dct_scr)
            dn_scr[...] = jnp.zeros_like(dn_scr)
            dnw_ref[...] = jnp.zeros_like(dnw_ref)
            dsum_ref[...] = jnp.zeros_like(dsum_ref)

        ms = tuple(m_ref[h] for h in HEADS)
        step = lambda *a: _ml_chunk(*a, ms)[:3]
        _, vjp = jax.vjp(step, _seg(qk_ref, 0), _seg(qk_ref, 1), _seg(vo_ref, 0), _seg(vo_ref, 1), g_ref[...],
                         nw_ref[...], tuple(ct_ref[h] for h in HEADS), tuple(n_ref[h] for h in HEADS))
        d_q, d_k, d_v, d_o, d_gates, d_nw, d_cts, d_ns = vjp(
            (dy_ref[...], tuple(dct_scr[h] for h in HEADS), tuple(dn_scr[h] for h in HEADS)))
        dqk_ref[:, 0:D_GROUP] = d_q
        dqk_ref[:, D_GROUP:2 * D_GROUP] = d_k
        for seg, val in enumerate((d_v, d_o)):
            dp_ref[:, seg * D_GROUP:(seg + 1) * D_GROUP] = val.astype(dp_ref.dtype)
            dsum_ref[:, seg * D_GROUP:(seg + 1) * D_GROUP] += jnp.sum(val, axis=0, keepdims=True)
        dg_ref[...] = d_gates
        dnw_ref[...] += d_nw
        for h in HEADS:
            dct_scr[h] = d_cts[h]
            dn_scr[h] = d_ns[h]

    rev = lambda c: nc - 1 - c
    st = lambda r, w: pl.BlockSpec((None, N_HEADS, r, w), lambda c: (rev(c), 0, 0, 0))
    return pl.pallas_call(
        body, name="mlstm_bwd", grid=(nc,),
        in_specs=[pl.BlockSpec((rows, 2 * D_GROUP), lambda c: (rev(c), 0)),
                  pl.BlockSpec((rows, 2 * D_GROUP), lambda c: (rev(c), 3)),
                  pl.BlockSpec((rows, LANES), lambda c: (rev(c), 0)),
                  pl.BlockSpec((1, D_GROUP), lambda c: (0, 0)),
                  st(D_HEAD, D_HEAD), st(1, D_HEAD), st(1, 1),
                  pl.BlockSpec((rows, D_GROUP), lambda c: (rev(c), 1)),
                  pl.BlockSpec(memory_space=pl.ANY)],
        out_specs=[pl.BlockSpec((rows, 2 * D_GROUP), lambda c: (rev(c), 3)),
                   pl.BlockSpec((rows, 2 * D_GROUP), lambda c: (rev(c), 0)),
                   pl.BlockSpec((rows, LANES), lambda c: (rev(c), 0)),
                   pl.BlockSpec((1, D_GROUP), lambda c: (0, 0)),
                   pl.BlockSpec((1, 2 * D_GROUP), lambda c: (0, 0))],
        out_shape=[jax.ShapeDtypeStruct(d_proj.shape, d_proj.dtype), jax.ShapeDtypeStruct((s, 2 * D_GROUP), F32),
                   jax.ShapeDtypeStruct((s, LANES), F32), jax.ShapeDtypeStruct((1, D_GROUP), F32),
                   jax.ShapeDtypeStruct((1, 2 * D_GROUP), F32)],
        input_output_aliases={8: 0},
        scratch_shapes=[pltpu.VMEM((N_HEADS, D_HEAD, D_HEAD), F32), pltpu.VMEM((N_HEADS, 1, D_HEAD), F32)],
        compiler_params=_params(("arbitrary",)),
    )(qk, proj, gates, norm_w, ct_s, n_s, m_s, d_y, d_proj)


LN_TOKENS = 512
ATT_TOKENS = 512


def _proj_res_ln(a, w, xres, g, b, name):
    s, dm = xres.shape
    k = a.shape[1]
    tb = min(LN_TOKENS, s)

    def body(a_ref, w_ref, x_ref, g_ref, b_ref, z_ref, o_ref):
        halves = [slice(0, tb // 2), slice(tb // 2, tb)]
        zs = [ALPHA * x_ref[rows, :] + _nn_raw(a_ref[rows, :], w_ref[...]) for rows in halves]
        for rows, z in zip(halves, zs):
            z_ref[rows, :] = z
            o_ref[rows, :] = _layer_norm(z, g_ref[...], b_ref[...])

    tok = pl.BlockSpec((tb, dm), lambda i: (i, 0))
    vec = pl.BlockSpec((1, dm), lambda i: (0, 0))
    act = jax.ShapeDtypeStruct((s, dm), F32)
    return pl.pallas_call(
        body, name=name, grid=(s // tb,),
        in_specs=[pl.BlockSpec((tb, k), lambda i: (i, 0)), pl.BlockSpec((k, dm), lambda i: (0, 0)), tok, vec, vec],
        out_specs=[tok, tok], out_shape=[act, act], compiler_params=_params(("parallel",)),
    )(a, w, xres, g, b)


def _ln_bwd_proj(d_out, z, g, b, w, name):
    s, dm = z.shape
    k = w.shape[0]
    tb = min(LN_TOKENS, s)

    def body(do_ref, z_ref, g_ref, b_ref, w_ref, dz_ref, da_ref, dg_ref, db_ref):
        @pl.when(pl.program_id(0) == 0)
        def _():
            dg_ref[...] = jnp.zeros_like(dg_ref)
            db_ref[...] = jnp.zeros_like(db_ref)

        halves = [slice(0, tb // 2), slice(tb // 2, tb)]
        d_zs = []
        for rows in halves:
            _, vjp = jax.vjp(_layer_norm, z_ref[rows, :], g_ref[...], b_ref[...])
            d_z, d_g, d_b = vjp(do_ref[rows, :])
            dz_ref[rows, :] = d_z
            dg_ref[...] += d_g
            db_ref[...] += d_b
            d_zs.append(d_z)
        for rows, d_z in zip(halves, d_zs):
            da_ref[rows, :] = _nt_raw(d_z, w_ref[...])

    tok = pl.BlockSpec((tb, dm), lambda i: (i, 0))
    vec = pl.BlockSpec((1, dm), lambda i: (0, 0))
    return pl.pallas_call(
        body, name=name, grid=(s // tb,),
        in_specs=[tok, tok, vec, vec, pl.BlockSpec((k, dm), lambda i: (0, 0))],
        out_specs=[tok, pl.BlockSpec((tb, k), lambda i: (i, 0)), vec, vec],
        out_shape=[jax.ShapeDtypeStruct((s, dm), F32), jax.ShapeDtypeStruct((s, k), F32),
                   jax.ShapeDtypeStruct((1, dm), F32), jax.ShapeDtypeStruct((1, dm), F32)],
        compiler_params=_params(("arbitrary",)),
    )(d_out, z, g, b, w)


def _proj_loss_tail(a, w, xres, g, b, target):
    s, dm = xres.shape
    k = a.shape[1]
    tb = min(ATT_TOKENS, s)

    def loss_fn(z, gg, bb, tgt):
        err = jnp.square(_layer_norm(z, gg, bb) - tgt)
        return 0.5 * jnp.sum(jnp.mean(err, axis=-1, keepdims=True), axis=0, keepdims=True)

    def body(a_ref, w_ref, x_ref, g_ref, b_ref, t_ref, loss_ref, dz_ref, dg_ref, db_ref):
        @pl.when(pl.program_id(0) == 0)
        def _():
            loss_ref[...] = jnp.zeros_like(loss_ref)
            dg_ref[...] = jnp.zeros_like(dg_ref)
            db_ref[...] = jnp.zeros_like(db_ref)

        halves = [slice(0, tb // 2), slice(tb // 2, tb)]
        zs = [ALPHA * x_ref[rows, :] + _nn_raw(a_ref[rows, :], w_ref[...]) for rows in halves]
        for rows, z in zip(halves, zs):
            tgt = t_ref[rows, :]
            loss, vjp = jax.vjp(lambda zz, gg, bb, tgt=tgt: loss_fn(zz, gg, bb, tgt), z, g_ref[...], b_ref[...])
            d_z, d_g, d_b = vjp(jnp.ones((1, 1), F32))
            loss_ref[...] += loss
            dz_ref[rows, :] = d_z
            dg_ref[...] += d_g
            db_ref[...] += d_b

    tok = pl.BlockSpec((tb, dm), lambda i: (i, 0))
    vec = pl.BlockSpec((1, dm), lambda i: (0, 0))
    one = pl.BlockSpec((1, 1), lambda i: (0, 0))
    return pl.pallas_call(
        body, name="ffn_down_loss_tail", grid=(s // tb,),
        in_specs=[pl.BlockSpec((tb, k), lambda i: (i, 0)), pl.BlockSpec((k, dm), lambda i: (0, 0)), tok, vec, vec, tok],
        out_specs=[one, tok, vec, vec],
        out_shape=[jax.ShapeDtypeStruct((1, 1), F32), jax.ShapeDtypeStruct((s, dm), F32),
                   jax.ShapeDtypeStruct((1, dm), F32), jax.ShapeDtypeStruct((1, dm), F32)],
        compiler_params=_params(("arbitrary",)),
    )(a, w, xres, g, b, target)


def _att_heads(qs, ks, vs):
    sc = [_nt(q, k) * (CA_DH ** -0.5) for q, k in zip(qs, ks)]
    p = [jax.nn.softmax(s, axis=-1) for s in sc]
    return tuple(_nn(pp, v) for pp, v in zip(p, vs))


def _head_slices(ref_or_value, offset):
    return tuple(ref_or_value[:, offset + h * CA_DH:offset + (h + 1) * CA_DH] for h in range(CA_HEADS))


def _cross_attention_fwd(x1, kv, wq, wo, g, b):
    s = x1.shape[0]
    tb = min(ATT_TOKENS, s)

    def body(x_ref, kv_ref, wq_ref, wo_ref, g_ref, b_ref, att_ref, z_ref, o_ref):
        x_blk = x_ref[...]
        q = _nn_raw(x_blk, wq_ref[...])
        att = jnp.concatenate(_att_heads(_head_slices(q, 0), _head_slices(kv_ref, 0), _head_slices(kv_ref, D_MODEL)),
                              axis=1)
        att_ref[...] = att.astype(att_ref.dtype)
        z = ALPHA * x_blk + _nn_raw(att, wo_ref[...])
        z_ref[...] = z
        o_ref[...] = _layer_norm(z, g_ref[...], b_ref[...])

    tok = pl.BlockSpec((tb, D_MODEL), lambda i: (i, 0))
    mat = pl.BlockSpec((D_MODEL, D_MODEL), lambda i: (0, 0))
    vec = pl.BlockSpec((1, D_MODEL), lambda i: (0, 0))
    act = jax.ShapeDtypeStruct((s, D_MODEL), F32)
    return pl.pallas_call(
        body, name="cross_attention_fwd", grid=(s // tb,),
        in_specs=[tok, pl.BlockSpec((N_MEM, 2 * D_MODEL), lambda i: (0, 0)), mat, mat, vec, vec],
        out_specs=[tok, tok, tok],
        out_shape=[jax.ShapeDtypeStruct((s, D_MODEL), BF16), act, act],
        compiler_params=_params(("parallel",)),
    )(x1, kv, wq, wo, g, b)


def _cross_attention_bwd(d_x2, x1, z2, kv, wq, wo, g, b):
    s = x1.shape[0]
    tb = min(ATT_TOKENS, s)

    def body(dx2_ref, x_ref, z_ref, kv_ref, wq_ref, wo_ref, g_ref, b_ref,
             dx1_ref, dq_ref, dz_ref, dkv_ref, dg_ref, db_ref):
        @pl.when(pl.program_id(0) == 0)
        def _():
            dkv_ref[...] = jnp.zeros_like(dkv_ref)
            dg_ref[...] = jnp.zeros_like(dg_ref)
            db_ref[...] = jnp.zeros_like(db_ref)

        q = _nn_raw(x_ref[...], wq_ref[...])
        _, ln_vjp = jax.vjp(_layer_norm, z_ref[...], g_ref[...], b_ref[...])
        d_z, d_g, d_b = ln_vjp(dx2_ref[...])
        dg_ref[...] += d_g
        db_ref[...] += d_b
        dz_ref[...] = d_z.astype(dz_ref.dtype)
        d_att = _nt_raw(d_z, wo_ref[...])
        _, vjp = jax.vjp(_att_heads, _head_slices(q, 0), _head_slices(kv_ref, 0), _head_slices(kv_ref, D_MODEL))
        d_qs, d_ks, d_vs = vjp(_head_slices(d_att, 0))
        for h in range(CA_HEADS):
            lo = h * CA_DH
            dkv_ref[:, lo:lo + CA_DH] += d_ks[h]
            dkv_ref[:, D_MODEL + lo:D_MODEL + lo + CA_DH] += d_vs[h]
        d_q = jnp.concatenate(d_qs, axis=1)
        dq_ref[...] = d_q.astype(dq_ref.dtype)
        dx1_ref[...] = ALPHA * d_z + _nt_raw(d_q, wq_ref[...])

    tok = pl.BlockSpec((tb, D_MODEL), lambda i: (i, 0))
    mem = pl.BlockSpec((N_MEM, 2 * D_MODEL), lambda i: (0, 0))
    mat = pl.BlockSpec((D_MODEL, D_MODEL), lambda i: (0, 0))
    vec = pl.BlockSpec((1, D_MODEL), lambda i: (0, 0))
    low = jax.ShapeDtypeStruct((s, D_MODEL), BF16)
    return pl.pallas_call(
        body, name="cross_attention_bwd", grid=(s // tb,),
        in_specs=[tok, tok, tok, mem, mat, mat, vec, vec], out_specs=[tok, tok, tok, mem, vec, vec],
        out_shape=[jax.ShapeDtypeStruct((s, D_MODEL), F32), low, low,
                   jax.ShapeDtypeStruct((N_MEM, 2 * D_MODEL), F32),
                   jax.ShapeDtypeStruct((1, D_MODEL), F32), jax.ShapeDtypeStruct((1, D_MODEL), F32)],
        compiler_params=_params(("arbitrary",)),
    )(d_x2, x1, z2, kv, wq, wo, g, b)


def _local_step(x, mem, target, w, mid_weights=None, ffn_weights=None, down_weights=None, on_ffn_grads=None,
                on_mid_grads=None,
                on_small_grads=None, on_last_grads=None):
    w = dict(w)
    s = x.shape[0]
    tm = min(512, s)
    tt_big = min(1024, s)
    proj = _matmul_nn(x, w["w_in_main"], w["b_in_main"], min(2048, s), 512, "proj")
    gates = _matmul_nn(x, w["w_in_gate"], w["b_in_gate"], tm, LANES, "proj_gates")
    qk = _ml_conv_fwd(proj, w["ml_conv_w"], w["ml_conv_b"])
    y, hg_states = _hgrn2_fwd(proj, w["hg_lb_logits"], w["hg_norm_w"])
    y, ct_s, n_s, m_s = _mlstm_fwd(qk, proj, gates, w["ml_norm_w"], y)
    if mid_weights is not None:
        w.update(mid_weights(y))
    z1, x1 = _proj_res_ln(y, w["w_out"], x, w["ln1_g"], w["ln1_b"], "out_proj_ln1")
    kv = _matmul_nn(mem, w["ca_wkv"], None, N_MEM, CA_DH, "kv")
    att, z2, x2 = _cross_attention_fwd(x1, kv, w["ca_wq"], w["ca_wo"], w["ln2_g"], w["ln2_b"])
    if ffn_weights is not None:
        w.update(ffn_weights(x2))
    u = _matmul_nn(x2, w["ffn_w_up"], None, min(2048, s), UP_SHARD_P, "ffn_up", BF16)
    hid = _ffn_conv_fwd(u, w["ffn_conv_w"], w["ffn_conv_b"])
    if down_weights is not None:
        w.update(down_weights(hid))
    loss, d_z3, d_ln3_g, d_ln3_b = _proj_loss_tail(hid, w["ffn_w_down"], x2, w["ln3_g"], w["ln3_b"], target)
    grads = {"ln3_g": d_ln3_g, "ln3_b": d_ln3_b}
    grads["ffn_w_down"] = _matmul_tn(hid, d_z3, 1536, D_MODEL, tt_big, "d_w_down")
    d_hid = _matmul_nt([(d_z3, w["ffn_w_down"])], None, 1.0, tm, D_FF_P, "d_hid", BF16)
    d_ug, d_uv, d_cwg, d_cwv, d_cbg, d_cbv = _ffn_conv_bwd(u, w["ffn_conv_w"], w["ffn_conv_b"], d_hid)
    grads["ffn_conv_w"] = jnp.concatenate([d_cwg, d_cwv], axis=-1)
    grads["ffn_conv_b"] = jnp.concatenate([d_cbg, d_cbv], axis=-1)
    d_w_up = _matmul_tn(d_ug, x2, D_FF_P // 2, D_MODEL, tt_big, "d_w_up_gate", rows=D_UP_P)
    grads["ffn_w_up"] = _matmul_tn(d_uv, x2, D_FF_P // 2, D_MODEL, tt_big, "d_w_up_val", rows=D_UP_P, row0=D_FF_P,
                                   into=d_w_up)
    d_x2 = _matmul_nt([(d_ug, w["ffn_w_up"], 0), (d_uv, w["ffn_w_up"], N_DEV // 2)], d_z3, ALPHA,
                      min(256, s), D_MODEL, "d_x2")
    if on_ffn_grads is not None:
        d_x2 = on_ffn_grads(grads, d_x2)
    d_x1, d_q, d_z2, d_kv, grads["ln2_g"], grads["ln2_b"] = _cross_attention_bwd(
        d_x2, x1, z2, kv, w["ca_wq"], w["ca_wo"], w["ln2_g"], w["ln2_b"])
    grads["ca_wo"] = _matmul_tn(att, d_z2, D_MODEL, D_MODEL, tt_big, "d_ca_wo")
    grads["ca_wq"] = _matmul_tn(x1, d_q, D_MODEL, D_MODEL, tt_big, "d_ca_wq")
    grads["ca_wkv"] = _matmul_tn(mem, d_kv, D_MODEL, CA_DH, N_MEM, "d_ca_wkv", shards=N_DEV, group=N_DEV)
    d_z1, d_y, grads["ln1_g"], grads["ln1_b"] = _ln_bwd_proj(d_x1, z1, w["ln1_g"], w["ln1_b"], w["w_out"],
                                                             "ln1_bwd_out_proj")
    grads["w_out"] = _matmul_tn(y, d_z1, D_MODEL, D_MODEL, tt_big, "d_w_out")
    if on_mid_grads is not None:
        d_y = on_mid_grads(grads, d_y)
    d_proj, grads["hg_lb_logits"], grads["hg_norm_w"], db_hg = _hgrn2_bwd(
        proj, w["hg_lb_logits"], w["hg_norm_w"], hg_states, d_y)
    d_proj, d_qk, d_gates, grads["ml_norm_w"], db_vo = _mlstm_bwd(
        qk, proj, gates, w["ml_norm_w"], ct_s, n_s, m_s, d_y, d_proj)
    d_proj, grads["ml_conv_w"], grads["ml_conv_b"], db_qk = _ml_conv_bwd(
        proj, w["ml_conv_w"], w["ml_conv_b"], d_qk, d_proj)
    grads["b_in_main"] = jnp.concatenate([db_hg, db_qk, db_vo], axis=-1)
    grads["w_in_gate"] = _matmul_tn(d_gates, x, LANES, D_MODEL, tt_big, "d_w_in_gates")
    grads["b_in_gate"] = jnp.sum(d_gates, axis=0, keepdims=True)
    if on_small_grads is not None:
        d_proj = on_small_grads(grads, loss, d_proj)
    grads["w_in_main"] = _matmul_tn(d_proj, x, min(2048, D_IN_MAIN), D_MODEL, tt_big, "d_w_in")
    if on_last_grads is not None:
        d_z1 = on_last_grads(grads, d_z1)
    grad_x = _matmul_nt([(d_proj, w["w_in_main"]), (d_gates, w["w_in_gate"])], d_z1, ALPHA, tm, D_MODEL, "d_x")
    return loss, grad_x, grads


HBM_SPEC = pl.BlockSpec(memory_space=pltpu.HBM)


def _coords():
    return lax.axis_index("x"), lax.axis_index("y"), lax.axis_index("c")


def _other_chips(x, y):
    return [(1 - x, y), (x, 1 - y), (1 - x, 1 - y)]


def _my_slot():
    x, y, c = _coords()
    return 4 * x + 2 * y + c


SEM_SPEC = pl.BlockSpec(memory_space=pltpu.SEMAPHORE)
ANY_SPEC = pl.BlockSpec(memory_space=pl.ANY)
SIDE_EFFECT = pltpu.SideEffectType.DATAFLOW_SIDE_EFFECTING


def _peer(x, y, c, d):
    flip = lambda v, bit: 1 - v if bit else v
    p = (flip(x, d & 4), flip(y, d & 2), flip(c, d & 1))
    return p, 4 * p[0] + 2 * p[1] + p[2]


def _direct_copies(gather, src_refs, land_refs, send_sems, recv_sems):
    x, y, c = _coords()
    me = 4 * x + 2 * y + c
    copies = []
    for a in range(len(src_refs)):
        for d in range(1, N_DEV):
            peer, peer_slot = _peer(x, y, c, d)
            copies.append(pltpu.make_async_remote_copy(
                src_ref=src_refs[a] if gather else src_refs[a].at[peer_slot],
                dst_ref=land_refs[a].at[me] if gather else land_refs[a].at[d - 1],
                send_sem=send_sems.at[7 * a + d - 1], recv_sem=recv_sems.at[7 * a + d - 1],
                device_id=peer, device_id_type=MESH))
    return copies


def _hbm(t):
    return pltpu.HBM(t.shape, t.dtype)


def _chip_copies(src_refs, land_refs, send_sems, recv_sems):
    x, y, c = _coords()
    me = 4 * x + 2 * y + c
    targets = [(x, y, 1 - c)] + [(cx, cy, c) for cx, cy in _other_chips(x, y)]
    return [pltpu.make_async_remote_copy(
        src_ref=src_refs[a], dst_ref=land_refs[a].at[me], send_sem=send_sems.at[4 * a + k],
        recv_sem=recv_sems.at[4 * a + k], device_id=target, device_id_type=MESH)
        for a in range(len(src_refs)) for k, target in enumerate(targets)]


def _forward_copies(land_refs, send_sems, recv_sems):
    x, y, c = _coords()
    return [pltpu.make_async_remote_copy(
        src_ref=land_refs[a].at[4 * cx + 2 * cy + c], dst_ref=land_refs[a].at[4 * cx + 2 * cy + c],
        send_sem=send_sems.at[3 * a + j], recv_sem=recv_sems.at[3 * a + j],
        device_id=(x, y, 1 - c), device_id_type=MESH)
        for a in range(len(land_refs)) for j, (cx, cy) in enumerate(_other_chips(x, y))]


def _split_copy_start(make_copies, n_sems, operands, through, name):
    n_ops = len(operands)

    def body(*refs):
        for cp in make_copies(refs[:n_ops], refs[n_ops + 1], refs[n_ops + 2]):
            cp.start()

    ins = [pltpu.with_memory_space_constraint(t, pltpu.HBM) for t in (*operands, through)]
    sems = pltpu.SemaphoreType.DMA((n_sems,))
    res = pl.pallas_call(
        body, name=name, out_shape=(sems, sems, *[_hbm(t) for t in ins]),
        in_specs=[HBM_SPEC] * (n_ops + 1), out_specs=(SEM_SPEC, SEM_SPEC, *[HBM_SPEC] * (n_ops + 1)),
        input_output_aliases={i: 2 + i for i in range(n_ops + 1)},
        compiler_params=pltpu.CompilerParams(has_side_effects=SIDE_EFFECT),
    )(*ins)
    return (res[0], res[1], list(res[2:2 + n_ops])), res[2 + n_ops]


def _split_copy_wait(make_copies, started, after, name):
    send_sems, recv_sems, operands = started
    n_ops = len(operands)
    after = list(after) if isinstance(after, (list, tuple)) else [after]

    def body(*refs):
        for cp in make_copies(refs[:n_ops], refs[n_ops], refs[n_ops + 1]):
            cp.wait_send()
            cp.wait_recv()

    res = pl.pallas_call(
        body, name=name, out_shape=tuple(_hbm(t) for t in operands),
        in_specs=[HBM_SPEC] * n_ops + [SEM_SPEC, SEM_SPEC] + [ANY_SPEC] * len(after),
        out_specs=tuple([HBM_SPEC] * n_ops), input_output_aliases={i: i for i in range(n_ops)},
        compiler_params=pltpu.CompilerParams(has_side_effects=SIDE_EFFECT),
    )(*operands, send_sems, recv_sems, *after)
    return list(res)


def _halves(make_copies, na):
    return lambda refs, send_sems, recv_sems: make_copies(refs[:na], refs[na:], send_sems, recv_sems)


def _direct_start(gather, arrays, through, name):
    na = len(arrays)
    lands = [lax.empty((N_DEV,) + t.shape if gather else (N_DEV - 1,) + t.shape[1:], t.dtype) for t in arrays]
    return _split_copy_start(_halves(functools.partial(_direct_copies, gather), na), 7 * na, [*arrays, *lands],
                             through, name)


def _direct_wait(gather, started, after, name):
    na = len(started[2]) // 2
    operands = _split_copy_wait(_halves(functools.partial(_direct_copies, gather), na), started, after, name)
    return operands[:na], operands[na:]


def _two_level_gather(shards, glue, name):
    na = len(shards)
    lands = [lax.empty((N_DEV,) + t.shape, t.dtype) for t in shards]
    nothing = jnp.zeros((SUBLANES, LANES), F32)
    started, _ = _split_copy_start(_halves(_chip_copies, na), 4 * na, [*shards, *lands], nothing, name + "_start")
    operands = _split_copy_wait(_halves(_chip_copies, na), started, glue, name + "_wait")
    started, mine = _split_copy_start(_forward_copies, 3 * na, operands[na:], operands[0], name + "_forward_start")
    lands = _split_copy_wait(_forward_copies, started, mine, name + "_forward_wait")
    return [lax.dynamic_update_index_in_dim(land, own, _my_slot(), 0)
            for own, land in zip([mine, *operands[1:na]], lands)]


def _row_tile(rows):
    for t in (256, 176, 128):
        if rows % t == 0 and rows > t:
            return t
    return rows


def _adamw_math(g, w, m, v):
    m_new = ADAM_B1 * m + (1.0 - ADAM_B1) * g
    v_new = ADAM_B2 * v + (1.0 - ADAM_B2) * jnp.square(g)
    m_hat = m_new / (1.0 - ADAM_B1 ** ADAM_STEP)
    v_hat = v_new / (1.0 - ADAM_B2 ** ADAM_STEP)
    delta = -ADAM_LR * (m_hat / (jnp.sqrt(v_hat) + ADAM_EPS) + ADAM_WD * w)
    return delta, m_new, v_new


def _adamw_sharded(chip, sums, got, w, m, v, name):
    r, c = w.shape
    tr = _row_tile(r)
    n_got = got.shape[0]

    def body(chip_ref, s_ref, g_ref, w_ref, m_ref, v_ref, go_ref, d_ref, nm_ref, nv_ref):
        g = s_ref[...].astype(F32)
        for i in range(n_got):
            g = g + g_ref[i].astype(F32)
        delta, m_new, v_new = _adamw_math(g, w_ref[...], m_ref[...], v_ref[...])
        go_ref[...] = g
        d_ref[...] = delta
        nm_ref[...] = m_new
        nv_ref[...] = v_new

    blk = pl.BlockSpec((tr, c), lambda i, chip_ref: (i, 0))
    out = jax.ShapeDtypeStruct((r, c), F32)
    return pl.pallas_call(
        body, name=name,
        grid_spec=pltpu.PrefetchScalarGridSpec(
            num_scalar_prefetch=1, grid=(r // tr,),
            in_specs=[pl.BlockSpec((None, tr, c), lambda i, chip_ref: (chip_ref[0], i, 0)),
                      pl.BlockSpec((n_got, tr, c), lambda i, chip_ref: (0, i, 0)), blk, blk, blk],
            out_specs=[blk, blk, blk, blk]),
        out_shape=[out, out, out, out],
        compiler_params=_params(("parallel",)),
    )(chip, sums, got, w, m, v)


def _adamw_replicated(parts, w, m, v):
    p, r, c = parts.shape
    names = SMALL_NAMES
    shapes = [w[n].shape for n in names]

    def body(*refs):
        p_ref = refs[0]
        ins = refs[1:1 + 3 * len(names)]
        outs = refs[1 + 3 * len(names):-2]
        loss_ref, sum_scr = refs[-2], refs[-1]
        total = p_ref[0]
        for i in range(1, p):
            total = total + p_ref[i]
        sum_scr[...] = total
        for k, n in enumerate(names):
            w_ref, m_ref, v_ref = ins[3 * k:3 * k + 3]
            g_ref, d_ref, nm_ref, nv_ref = outs[4 * k:4 * k + 4]
            for row, lane0, width, src_row in _small_pieces(n, shapes[k]):
                here = (slice(row, row + 1), slice(lane0, lane0 + width))
                g = sum_scr[src_row:src_row + 1, 0:width]
                delta, m_new, v_new = _adamw_math(g, w_ref[here], m_ref[here], v_ref[here])
                g_ref[here] = g
                d_ref[here] = delta
                nm_ref[here] = m_new
                nv_ref[here] = v_new
        loss_ref[...] = sum_scr[SMALL_LOSS_ROW:SMALL_LOSS_ROW + 1, 0:1]

    whole = lambda shape: pl.BlockSpec(shape, lambda i: (0,) * len(shape))
    args = [parts] + [t[n] for n in names for t in (w, m, v)]
    out_shape = [jax.ShapeDtypeStruct(s, F32) for s in shapes for _ in range(4)] + [jax.ShapeDtypeStruct((1, 1), F32)]
    res = pl.pallas_call(
        body, name="adamw_replicated", grid=(1,),
        in_specs=[whole(t.shape) for t in args], out_specs=[whole(s.shape) for s in out_shape],
        out_shape=out_shape, scratch_shapes=[pltpu.VMEM((r, c), F32)],
        compiler_params=_params(("arbitrary",)),
    )(*args)
    results = [{n: res[4 * k + j] for k, n in enumerate(names)} for j in range(4)]
    return results, res[-1]


SHARDED_NAMES = ("w_in", "ml_conv_w", "w_out", "ca_wq", "ca_wkv", "ca_wo", "ffn_w_up", "ffn_conv_w", "ffn_w_down")
SMALL_NAMES = ("b_in", "hg_lb_logits", "hg_norm_w", "ml_conv_b", "ml_norm_w", "ln1_g", "ln1_b",
               "ln2_g", "ln2_b", "ffn_conv_b", "ln3_g", "ln3_b")
WEIGHT_NAMES = ("w_in", "b_in", "hg_lb_logits", "hg_norm_w", "ml_conv_w", "ml_conv_b", "ml_norm_w", "w_out",
                "ln1_g", "ln1_b", "ca_wq", "ca_wkv", "ca_wo", "ln2_g", "ln2_b", "ffn_w_up", "ffn_conv_w",
                "ffn_conv_b", "ffn_w_down", "ln3_g", "ln3_b")
PAD_TO = {"ffn_w_up": UP_SHARD_P, "ffn_conv_w": UP_SHARD_P}
SMALL_ROWS = 24
SMALL_W = D_MODEL
SMALL_SHAPES = {"b_in": (1, D_IN), "hg_lb_logits": (2, D_GROUP), "hg_norm_w": (1, D_GROUP),
                "ml_conv_b": (1, 2 * D_GROUP), "ml_norm_w": (1, D_GROUP), "ln1_g": (1, D_MODEL), "ln1_b": (1, D_MODEL),
                "ln2_g": (1, D_MODEL), "ln2_b": (1, D_MODEL), "ffn_conv_b": (1, D_UP), "ln3_g": (1, D_MODEL),
                "ln3_b": (1, D_MODEL)}


def _shard_2d(name, block):
    t = block[0]
    if name in PAD_TO:
        t = jnp.pad(t, ((0, 0), (0, PAD_TO[name] - t.shape[1])))
    return t


TRANSPOSED = ("w_in", "ffn_w_up")


def _update_shard(name, block):
    if name not in TRANSPOSED:
        return _shard_2d(name, block)
    t = jnp.transpose(block[0])
    rows = PAD_TO.get(name, t.shape[0])
    return jnp.pad(t, ((0, rows - t.shape[0]), (0, 0)))


def _shard_like(name, t, like):
    if name in TRANSPOSED:
        return jnp.transpose(t[:like.shape[2]])[None]
    return t[:, :like.shape[2]][None]


def _pad_cols(t, width):
    return jnp.pad(t, ((0, 0), (0, width - t.shape[1])))


FIRST_NAMES = ("w_in", "ml_conv_w")
FFN_NAMES = ("ffn_w_up", "ffn_w_down", "ffn_conv_w")
MID_NAMES = ("ca_wo", "ca_wq", "ca_wkv", "w_out")


def _first_weights(g, small):
    w = dict(small)
    last = g["w_in"][N_DEV - 1]
    split = D_IN_MAIN - (N_DEV - 1) * W_IN_SHARD
    w["w_in_main"] = jnp.concatenate([*[g["w_in"][j] for j in range(N_DEV - 1)], last[:, :split]], axis=1)
    w["w_in_gate"] = _pad_cols(last[:, split:], LANES)
    w["b_in_main"] = small["b_in"][:, :D_IN_MAIN]
    w["b_in_gate"] = _pad_cols(small["b_in"][:, D_IN_MAIN:], LANES)
    w["ml_conv_w"] = jnp.transpose(g["ml_conv_w"], (1, 0, 2)).reshape(ML_CONV, 2 * D_GROUP)
    return w


def _mid_weights(g):
    w = {n: g[n].reshape(D_MODEL, D_MODEL) for n in ("w_out", "ca_wq", "ca_wo")}
    w["ca_wkv"] = g["ca_wkv"]
    return w


FFN_UP_NAMES = ("ffn_w_up", "ffn_conv_w")
FFN_DOWN_NAMES = ("ffn_w_down",)


def _ffn_up_weights(g, small):
    w = {"ffn_w_up": g["ffn_w_up"]}
    w["ffn_conv_w"] = jnp.transpose(g["ffn_conv_w"], (1, 0, 2)).reshape(FFN_CONV, D_UP_P)
    w["ffn_conv_b"] = _pad_cols(small["ffn_conv_b"].reshape(N_DEV, UP_SHARD), UP_SHARD_P).reshape(1, D_UP_P)
    return w


def _ffn_down_weights(g):
    down = g["ffn_w_down"].reshape(N_DEV // 2, UP_SHARD, D_MODEL)
    return {"ffn_w_down": jnp.pad(down, ((0, 0), (0, UP_SHARD_P - UP_SHARD), (0, 0))).reshape(D_FF_P, D_MODEL)}


def _whole_weights(g, small):
    return {**_first_weights(g, small), **_mid_weights(g), **_ffn_up_weights(g, small), **_ffn_down_weights(g)}


def _owner_stack(n, grads):
    if n == "w_in":
        rows = jnp.concatenate([grads["w_in_main"], grads["w_in_gate"][:D_IN - D_IN_MAIN]], axis=0)
        return rows.reshape(N_DEV, W_IN_SHARD, D_MODEL)
    if n == "ffn_w_up":
        return grads[n].reshape(N_DEV, UP_SHARD_P, D_MODEL)
    if n in ("w_out", "ca_wq", "ca_wo"):
        return grads[n].reshape(N_DEV, D_MODEL // N_DEV, D_MODEL)
    if n == "ffn_w_down":
        down = grads[n].reshape(N_DEV // 2, UP_SHARD_P, D_MODEL)[:, :UP_SHARD]
        return down.reshape(N_DEV, D_FF // N_DEV, D_MODEL)
    if n == "ml_conv_w":
        return jnp.transpose(grads[n].reshape(ML_CONV, N_DEV, LANES), (1, 0, 2))
    if n == "ffn_conv_w":
        return jnp.transpose(grads[n].reshape(FFN_CONV, N_DEV, UP_SHARD_P), (1, 0, 2))
    return grads[n]


def _owner_stacks(grads):
    return {n: _owner_stack(n, grads) for n in SHARDED_NAMES}


def _small_grads(grads):
    out = {n: grads[n] for n in SMALL_NAMES if n in grads}
    out["b_in"] = jnp.concatenate([grads["b_in_main"], grads["b_in_gate"][:, :D_IN - D_IN_MAIN]], axis=1)
    out["ffn_conv_b"] = grads["ffn_conv_b"].reshape(N_DEV, UP_SHARD_P)[:, :UP_SHARD].reshape(1, D_UP)
    return out


def _small_rows(shape):
    return shape[0] if shape[1] <= SMALL_W else -(-shape[1] // SMALL_W)


SMALL_BASE = {n: sum(_small_rows(SMALL_SHAPES[k]) for k in SMALL_NAMES[:i]) for i, n in enumerate(SMALL_NAMES)}
SMALL_LOSS_ROW = sum(_small_rows(SMALL_SHAPES[n]) for n in SMALL_NAMES)
assert SMALL_LOSS_ROW < SMALL_ROWS


def _small_pieces(name, shape):
    base = SMALL_BASE[name]
    if shape[1] <= SMALL_W:
        return [(i, 0, shape[1], base + i) for i in range(shape[0])]
    return [(0, k * SMALL_W, min(SMALL_W, shape[1] - k * SMALL_W), base + k) for k in range(_small_rows(shape))]


def _pack_small(p, loss):
    rows = []
    for n in SMALL_NAMES:
        t = p[n]
        nrows = _small_rows(t.shape)
        if t.shape[1] <= SMALL_W:
            rows.append(_pad_cols(t, SMALL_W))
        else:
            rows.append(_pad_cols(t, nrows * SMALL_W).reshape(nrows, SMALL_W))
    rows.append(_pad_cols(loss, SMALL_W))
    slab = jnp.concatenate(rows, axis=0)
    return jnp.pad(slab, ((0, SMALL_ROWS - slab.shape[0]), (0, 0)))


def kernel(x, mem, w_in, b_in, hg_lb_logits, hg_norm_w, ml_conv_w, ml_conv_b, ml_norm_w, w_out, ln1_g, ln1_b, ca_wq, ca_wkv, ca_wo, ln2_g, ln2_b, ffn_w_up, ffn_conv_w, ffn_conv_b, ffn_w_down, ln3_g, ln3_b, loss_target, m_w_in, m_b_in, m_hg_lb_logits, m_hg_norm_w, m_ml_conv_w, m_ml_conv_b, m_ml_norm_w, m_w_out, m_ln1_g, m_ln1_b, m_ca_wq, m_ca_wkv, m_ca_wo, m_ln2_g, m_ln2_b, m_ffn_w_up, m_ffn_conv_w, m_ffn_conv_b, m_ffn_w_down, m_ln3_g, m_ln3_b, v_w_in, v_b_in, v_hg_lb_logits, v_hg_norm_w, v_ml_conv_w, v_ml_conv_b, v_ml_norm_w, v_w_out, v_ln1_g, v_ln1_b, v_ca_wq, v_ca_wkv, v_ca_wo, v_ln2_g, v_ln2_b, v_ffn_w_up, v_ffn_conv_w, v_ffn_conv_b, v_ffn_w_down, v_ln3_g, v_ln3_b):
    params = dict(w_in=w_in, b_in=b_in, hg_lb_logits=hg_lb_logits, hg_norm_w=hg_norm_w, ml_conv_w=ml_conv_w,
                  ml_conv_b=ml_conv_b, ml_norm_w=ml_norm_w, w_out=w_out, ln1_g=ln1_g, ln1_b=ln1_b, ca_wq=ca_wq,
                  ca_wkv=ca_wkv, ca_wo=ca_wo, ln2_g=ln2_g, ln2_b=ln2_b, ffn_w_up=ffn_w_up, ffn_conv_w=ffn_conv_w,
                  ffn_conv_b=ffn_conv_b, ffn_w_down=ffn_w_down, ln3_g=ln3_g, ln3_b=ln3_b)
    mom1 = dict(w_in=m_w_in, b_in=m_b_in, hg_lb_logits=m_hg_lb_logits, hg_norm_w=m_hg_norm_w,
                ml_conv_w=m_ml_conv_w, ml_conv_b=m_ml_conv_b, ml_norm_w=m_ml_norm_w, w_out=m_w_out, ln1_g=m_ln1_g,
                ln1_b=m_ln1_b, ca_wq=m_ca_wq, ca_wkv=m_ca_wkv, ca_wo=m_ca_wo, ln2_g=m_ln2_g, ln2_b=m_ln2_b,
                ffn_w_up=m_ffn_w_up, ffn_conv_w=m_ffn_conv_w, ffn_conv_b=m_ffn_conv_b, ffn_w_down=m_ffn_w_down,
                ln3_g=m_ln3_g, ln3_b=m_ln3_b)
    mom2 = dict(w_in=v_w_in, b_in=v_b_in, hg_lb_logits=v_hg_lb_logits, hg_norm_w=v_hg_norm_w,
                ml_conv_w=v_ml_conv_w, ml_conv_b=v_ml_conv_b, ml_norm_w=v_ml_norm_w, w_out=v_w_out, ln1_g=v_ln1_g,
                ln1_b=v_ln1_b, ca_wq=v_ca_wq, ca_wkv=v_ca_wkv, ca_wo=v_ca_wo, ln2_g=v_ln2_g, ln2_b=v_ln2_b,
                ffn_w_up=v_ffn_w_up, ffn_conv_w=v_ffn_conv_w, ffn_conv_b=v_ffn_conv_b, ffn_w_down=v_ffn_w_down,
                ln3_g=v_ln3_g, ln3_b=v_ln3_b)

    x_idx, y_idx, c_idx = _coords()
    as_index = lambda v: jnp.reshape(v, (1,)).astype(jnp.int32)
    me = as_index(4 * x_idx + 2 * y_idx + c_idx)
    small_params = {n: params[n] for n in SMALL_NAMES}

    shards = {n: _update_shard(n, params[n]) for n in SHARDED_NAMES}
    m_shards = {n: _update_shard(n, mom1[n]) for n in SHARDED_NAMES}
    v_shards = {n: _update_shard(n, mom2[n]) for n in SHARDED_NAMES}
    outgoing = {n: _shard_2d(n, params[n]) if "conv" in n else _shard_2d(n, params[n]).astype(BF16)
                for n in SHARDED_NAMES}
    to_send = lambda names: [outgoing[n] for n in names]
    glue = [*m_shards.values(), *v_shards.values(), *shards.values(),
            *[outgoing[n] for n in SHARDED_NAMES if n not in FIRST_NAMES]]
    first = dict(zip(FIRST_NAMES, _two_level_gather(to_send(FIRST_NAMES), glue, "weights_gather_first")))
    mid_started, through = _direct_start(True, to_send(MID_NAMES), first["w_in"], "weights_gather_start_mid")
    ffn_started, through = _direct_start(True, to_send(FFN_UP_NAMES), through, "weights_gather_start_ffn_up")
    down_started, first["w_in"] = _direct_start(True, to_send(FFN_DOWN_NAMES), through,
                                                "weights_gather_start_ffn_down")

    def gathered_weights(names, started, after, tag):
        mine, lands = _direct_wait(True, started, after, "weights_gather_wait_" + tag)
        return {n: lax.dynamic_update_index_in_dim(land, own, me[0], 0) for n, own, land in zip(names, mine, lands)}

    started, own_stacks = {}, {}

    def start_group(names, tag):
        def hook(grads, through):
            own_stacks[tag] = [_owner_stack(n, grads).astype(BF16) for n in names]
            started[tag], through = _direct_start(False, own_stacks[tag], through, "grads_start_" + tag)
            return through
        return hook

    def start_small(grads, loss, through):
        started["small"], through = _direct_start(True, [_pack_small(_small_grads(grads), loss)], through,
                                                  "small_gather_start")
        return through

    loss, grad_x, grads = _local_step(
        x[0], mem[0], loss_target[0], _first_weights(first, small_params),
        lambda y: _mid_weights(gathered_weights(MID_NAMES, mid_started, y, "mid")),
        lambda x2: _ffn_up_weights(gathered_weights(FFN_UP_NAMES, ffn_started, x2, "ffn_up"), small_params),
        lambda hid: _ffn_down_weights(gathered_weights(FFN_DOWN_NAMES, down_started, hid, "ffn_down")),
        start_group(FFN_NAMES, "ffn"), start_group(MID_NAMES, "mid"), start_small, start_group(FIRST_NAMES, "last"))

    sharded_out = {}

    def update_group(names, tag, after):
        _, lands = _direct_wait(False, started[tag], after, "grads_wait_" + tag)
        for n, st, land in zip(names, own_stacks[tag], lands):
            res = _adamw_sharded(me, st, land, shards[n], m_shards[n], v_shards[n], "adamw_" + n)
            sharded_out[n] = [_shard_like(n, t, params[n]) for t in res]

    update_group(FFN_NAMES, "ffn", grad_x)
    update_group(MID_NAMES, "mid", grad_x)
    own_small, small_lands = _direct_wait(True, started["small"], grad_x, "small_gather_wait")
    small_parts = lax.dynamic_update_index_in_dim(small_lands[0], own_small[0], me[0], 0)
    small_out, total_loss = _adamw_replicated(small_parts, small_params, {n: mom1[n] for n in SMALL_NAMES},
                                              {n: mom2[n] for n in SMALL_NAMES})
    done = [t for n in FFN_NAMES + MID_NAMES for t in sharded_out[n]]
    done += [t for small in small_out for t in small.values()]
    update_group(FIRST_NAMES, "last", done)

    outs = []
    for k, small in enumerate(small_out):
        outs.extend(sharded_out[n][k] if n in sharded_out else small[n] for n in WEIGHT_NAMES)
    return (total_loss[0, 0], grad_x[None], *outs)
```

```python
import functools
import math

import jax
import jax.numpy as jnp
from jax import lax
from jax.experimental import pallas as pl
from jax.experimental.pallas import tpu as pltpu

F32 = jnp.float32
BF16 = jnp.bfloat16
HIGHEST = lax.Precision.HIGHEST
MESH = pl.DeviceIdType.MESH

N_DEV = 8
D_MODEL = 1024
N_MEM = 256
N_HEADS = 4
D_HEAD = 128
D_GROUP = N_HEADS * D_HEAD
CHUNK = 64
ML_CONV = 4
FFN_CONV = 3
D_FF = 2816
D_UP = 2 * D_FF
CA_HEADS = 4
CA_DH = D_MODEL // CA_HEADS
LANES = 128
SUBLANES = 8
D_IN = 8 * D_GROUP + 2 * N_HEADS
D_IN_MAIN = 8 * D_GROUP
W_IN_SHARD = D_IN // N_DEV
UP_SHARD = D_UP // N_DEV
UP_SHARD_P = 768
D_UP_P = N_DEV * UP_SHARD_P
D_FF_P = D_UP_P // 2
ALPHA = 2.0 ** 0.25
LN_EPS = 1e-5
NEG_BIG = -1e30
ADAM_LR = 0.001
ADAM_B1 = 0.9
ADAM_B2 = 0.999
ADAM_EPS = 1e-08
ADAM_WD = 0.01
ADAM_STEP = 10
VMEM_LIMIT = 56 * 1024 * 1024

SEG_HQ, SEG_HF, SEG_HI, SEG_HG, SEG_MQ, SEG_MK, SEG_MV, SEG_MO = (4 * i for i in range(8))


def _params(sem):
    return pltpu.CompilerParams(dimension_semantics=sem, vmem_limit_bytes=VMEM_LIMIT)


def _dg(a, b, ca, cb, precision=None):
    return lax.dot_general(a, b, (((ca,), (cb,)), ((), ())), precision=precision,
                           preferred_element_type=F32)


def _nn_raw(a, b):
    return _dg(a.astype(BF16), b.astype(BF16), 1, 0)


def _nt_raw(a, b):
    return _dg(a.astype(BF16), b.astype(BF16), 1, 1)


def _tn_raw(a, b):
    return _dg(a.astype(BF16), b.astype(BF16), 0, 0)


@jax.custom_vjp
def _nn(a, b):
    return _nn_raw(a, b)


_nn.defvjp(lambda a, b: (_nn_raw(a, b), (a, b)),
           lambda res, g: (_nt_raw(g, res[1]), _tn_raw(res[0], g)))


@jax.custom_vjp
def _nt(a, b):
    return _nt_raw(a, b)


_nt.defvjp(lambda a, b: (_nt_raw(a, b), (a, b)),
           lambda res, g: (_nn_raw(g, res[1]), _tn_raw(g, res[0])))


@jax.custom_vjp
def _tn(a, b):
    return _tn_raw(a, b)


_tn.defvjp(lambda a, b: (_tn_raw(a, b), (a, b)),
           lambda res, g: (_nt_raw(res[1], g), _nn_raw(res[0], g)))


def _layer_norm(z, g, b):
    mu = jnp.mean(z, axis=-1, keepdims=True)
    var = jnp.mean(jnp.square(z - mu), axis=-1, keepdims=True)
    return (z - mu) * lax.rsqrt(var + LN_EPS) * g + b


def _matmul_nn(a, w, bias, tm, tn, name, out_dtype=F32):
    m, k = a.shape
    if w.ndim == 3:
        n = w.shape[0] * w.shape[2]
        assert tn == w.shape[2]
        w_spec = pl.BlockSpec((None, k, tn), lambda i, j: (j, 0, 0))
    else:
        n = w.shape[1]
        w_spec = pl.BlockSpec((k, tn), lambda i, j: (0, j))

    def body(*refs):
        a_ref, w_ref = refs[0], refs[1]
        o_ref = refs[-1]
        acc = _nn_raw(a_ref[...], w_ref[...])
        if bias is not None:
            acc = acc + refs[2][...]
        o_ref[...] = acc.astype(o_ref.dtype)

    in_specs = [pl.BlockSpec((tm, k), lambda i, j: (i, 0)), w_spec]
    args = [a, w]
    if bias is not None:
        in_specs.append(pl.BlockSpec((1, tn), lambda i, j: (0, j)))
        args.append(bias)
    return pl.pallas_call(
        body, name=name, grid=(m // tm, n // tn), in_specs=in_specs,
        out_specs=pl.BlockSpec((tm, tn), lambda i, j: (i, j)),
        out_shape=jax.ShapeDtypeStruct((m, n), out_dtype),
        compiler_params=_params(("parallel", "parallel")),
    )(*args)


def _matmul_nt(d, w, tm, tk, name, k_out=None, bias=None, out_dtype=F32):
    m, n = d.shape
    k = k_out or w.shape[0]

    def body(*refs):
        acc = _nt_raw(refs[0][...], refs[1][...])
        if bias is not None:
            acc = acc + refs[2][...]
        refs[-1][...] = acc.astype(refs[-1].dtype)

    in_specs = [pl.BlockSpec((tm, n), lambda i, j: (i, 0)), pl.BlockSpec((tk, n), lambda i, j: (j, 0))]
    args = [d, w]
    if bias is not None:
        in_specs.append(pl.BlockSpec((1, tk), lambda i, j: (0, j)))
        args.append(bias)
    return pl.pallas_call(
        body, name=name, grid=(m // tm, k // tk), in_specs=in_specs,
        out_specs=pl.BlockSpec((tm, tk), lambda i, j: (i, j)),
        out_shape=jax.ShapeDtypeStruct((m, k), out_dtype),
        compiler_params=_params(("parallel", "parallel")),
    )(*args)


def _matmul_nn_sum(pairs, add, scale, tm, name):
    m = pairs[0][0].shape[0]
    n = pairs[0][1].shape[1]
    in_specs, args = [], []
    for a, w, row0 in pairs:
        kk = a.shape[1]
        in_specs += [pl.BlockSpec((tm, kk), lambda i: (i, 0)),
                     pl.BlockSpec((kk, n), lambda i, blk=row0 // kk: (blk, 0))]
        args += [a, w]
    if add is not None:
        in_specs.append(pl.BlockSpec((tm, n), lambda i: (i, 0)))
        args.append(add)

    def body(*refs):
        acc = None
        for p in range(len(pairs)):
            term = _nn_raw(refs[2 * p][...], refs[2 * p + 1][...])
            acc = term if acc is None else acc + term
        if add is not None:
            acc = acc + scale * refs[2 * len(pairs)][...]
        refs[-1][...] = acc

    return pl.pallas_call(
        body, name=name, grid=(m // tm,), in_specs=in_specs,
        out_specs=pl.BlockSpec((tm, n), lambda i: (i, 0)),
        out_shape=jax.ShapeDtypeStruct((m, n), F32),
        compiler_params=_params(("parallel",)),
    )(*args)


def _matmul_tn(a, b, tm, tn, tt, name, shards=None, shard0=0, group=1, into=None, colsum=False, rows=None, row0=0):
    t, m = a.shape
    n = b.shape[1]
    assert not colsum or tm == m
    n_in = 2 + (into is not None)
    out_dtype = BF16
    per_step = 1 if shards is None else group
    width = per_step * tn

    def body(*refs):
        a_ref, b_ref = refs[0], refs[1]
        o_ref, acc_ref = refs[n_in], refs[-1]
        first = pl.program_id(2) == 0

        @pl.when(first)
        def _():
            acc_ref[...] = jnp.zeros_like(acc_ref)

        if shards is None:
            acc_ref[...] += _tn_raw(a_ref[...], b_ref[...])
        else:
            lhs = a_ref[...].astype(BF16)
            for g in range(per_step):
                acc_ref[g] += _tn_raw(lhs, b_ref[:, g * tn:(g + 1) * tn])

        @pl.when(pl.program_id(2) == t // tt - 1)
        def _():
            o_ref[...] = acc_ref[...].astype(o_ref.dtype)

        if colsum:
            s_ref = refs[n_in + 1]

            @pl.when(first)
            def _():
                s_ref[...] = jnp.zeros_like(s_ref)

            s_ref[...] += jnp.sum(b_ref[...], axis=0, keepdims=True)

    in_specs = [pl.BlockSpec((tt, tm), lambda i, j, kk: (kk, i)),
                pl.BlockSpec((tt, width), lambda i, j, kk: (kk, j))]
    args = [a, b]
    aliases = {}
    if into is not None:
        in_specs.append(pl.BlockSpec(memory_space=pl.ANY))
        args.append(into)
        aliases = {2: 0}
    if shards is None:
        out_specs = [pl.BlockSpec((tm, tn), lambda i, j, kk: (row0 // tm + i, j))]
        out_shape = [jax.ShapeDtypeStruct((rows or m, n), out_dtype)]
        acc = pltpu.VMEM((tm, tn), F32)
    else:
        out_specs = [pl.BlockSpec((per_step, tm, tn), lambda i, j, kk: (shard0 // per_step + j, i, 0))]
        out_shape = [jax.ShapeDtypeStruct((shards, m, tn), out_dtype)]
        acc = pltpu.VMEM((per_step, tm, tn), F32)
    if colsum:
        out_specs.append(pl.BlockSpec((1, tn), lambda i, j, kk: (0, j)))
        out_shape.append(jax.ShapeDtypeStruct((1, n), F32))
    res = pl.pallas_call(
        body, name=name, grid=(m // tm, n // width, t // tt), in_specs=in_specs, out_specs=out_specs,
        out_shape=out_shape, input_output_aliases=aliases, scratch_shapes=[acc],
        compiler_params=_params(("parallel", "parallel", "arbitrary")),
    )(*args)
    return res if colsum else res[0]


ROW_TILE = 64


def _stack(ref, start, rows):
    return ref[pl.ds(start, rows), :].astype(F32).reshape(rows // SUBLANES, SUBLANES, LANES)


def _vreg_rows(ref, n):
    return [jnp.broadcast_to(ref[j:j + 1, :], (SUBLANES, LANES))[None] for j in range(n)]


def _column_total(acc):
    return jnp.sum(acc, axis=0, keepdims=True)


def _conv_fwd_tile(pad_ref, taps_w, bias, r0, rows):
    taps = len(taps_w)
    acc = bias
    for j in range(taps):
        acc = acc + _stack(pad_ref, SUBLANES - (taps - 1 - j) + r0, rows) * taps_w[j]
    return acc


def _conv_grads_tile(pad_ref, dpad_ref, dx_ref, taps_w, dws, r0, rows):
    taps = len(taps_w)
    x_rows = _stack(pad_ref, SUBLANES + r0, rows)
    dx = None
    for j in range(taps):
        d_shifted = _stack(dpad_ref, r0 + (taps - 1 - j), rows)
        term = d_shifted * taps_w[j]
        dx = term if dx is None else dx + term
        dws[j] = dws[j] + jnp.sum(d_shifted * x_rows, axis=0)
    dx_ref[r0:r0 + rows, :] = dx.reshape(rows, LANES).astype(dx_ref.dtype)
    return jnp.sum(dx, axis=0)


def _ml_conv_fwd(proj, conv_w, conv_b):
    s = proj.shape[0]
    nblk = 2 * D_GROUP // LANES

    def body(x_ref, w_ref, b_ref, o_ref, pad_ref):
        pad_ref[0:SUBLANES, :] = jnp.zeros((SUBLANES, LANES), F32)
        pad_ref[SUBLANES:, :] = x_ref[...].astype(F32)
        taps_w, bias = _vreg_rows(w_ref, ML_CONV), _vreg_rows(b_ref, 1)[0]
        for r0 in range(0, s, ROW_TILE):
            rows = min(ROW_TILE, s - r0)
            o_ref[r0:r0 + rows, :] = jax.nn.silu(_conv_fwd_tile(pad_ref, taps_w, bias, r0, rows)).reshape(rows, LANES)

    return pl.pallas_call(
        body, name="ml_conv_fwd", grid=(nblk,),
        in_specs=[pl.BlockSpec((s, LANES), lambda j: (0, SEG_MQ + j)),
                  pl.BlockSpec((ML_CONV, LANES), lambda j: (0, j)),
                  pl.BlockSpec((1, LANES), lambda j: (0, j))],
        out_specs=pl.BlockSpec((s, LANES), lambda j: (0, j)),
        out_shape=jax.ShapeDtypeStruct((s, 2 * D_GROUP), F32),
        scratch_shapes=[pltpu.VMEM((s + SUBLANES, LANES), F32)],
        compiler_params=_params(("parallel",)),
    )(proj, conv_w, conv_b)


def _ml_conv_bwd(proj, conv_w, conv_b, d_qk, d_proj):
    s = proj.shape[0]
    nblk = 2 * D_GROUP // LANES

    def body(x_ref, w_ref, b_ref, dy_ref, _, dx_ref, dw_ref, db_ref, dxs_ref, pad_ref, dpad_ref):
        pad_ref[0:SUBLANES, :] = jnp.zeros((SUBLANES, LANES), F32)
        pad_ref[SUBLANES:, :] = x_ref[...].astype(F32)
        dpad_ref[s:, :] = jnp.zeros((SUBLANES, LANES), F32)
        taps_w, bias = _vreg_rows(w_ref, ML_CONV), _vreg_rows(b_ref, 1)[0]
        db = jnp.zeros((SUBLANES, LANES), F32)
        for r0 in range(0, s, ROW_TILE):
            rows = min(ROW_TILE, s - r0)
            pre = _conv_fwd_tile(pad_ref, taps_w, bias, r0, rows)
            _, vjp = jax.vjp(jax.nn.silu, pre)
            d_pre, = vjp(_stack(dy_ref, r0, rows))
            dpad_ref[r0:r0 + rows, :] = d_pre.reshape(rows, LANES)
            db = db + jnp.sum(d_pre, axis=0)
        db_ref[...] = _column_total(db)
        dws = [jnp.zeros((SUBLANES, LANES), F32) for _ in range(ML_CONV)]
        dx_sum = jnp.zeros((SUBLANES, LANES), F32)
        for r0 in range(0, s, ROW_TILE):
            dx_sum = dx_sum + _conv_grads_tile(pad_ref, dpad_ref, dx_ref, taps_w, dws, r0, min(ROW_TILE, s - r0))
        dxs_ref[...] = _column_total(dx_sum)
        for j in range(ML_CONV):
            dw_ref[j:j + 1, :] = _column_total(dws[j])

    return pl.pallas_call(
        body, name="ml_conv_bwd", grid=(nblk,),
        in_specs=[pl.BlockSpec((s, LANES), lambda j: (0, SEG_MQ + j)),
                  pl.BlockSpec((ML_CONV, LANES), lambda j: (0, j)),
                  pl.BlockSpec((1, LANES), lambda j: (0, j)),
                  pl.BlockSpec((s, LANES), lambda j: (0, j)),
                  pl.BlockSpec(memory_space=pl.ANY)],
        out_specs=[pl.BlockSpec((s, LANES), lambda j: (0, SEG_MQ + j)),
                   pl.BlockSpec((ML_CONV, LANES), lambda j: (0, j)),
                   pl.BlockSpec((1, LANES), lambda j: (0, j)),
                   pl.BlockSpec((1, LANES), lambda j: (0, j))],
        out_shape=[jax.ShapeDtypeStruct(d_proj.shape, d_proj.dtype),
                   jax.ShapeDtypeStruct((ML_CONV, 2 * D_GROUP), F32),
                   jax.ShapeDtypeStruct((1, 2 * D_GROUP), F32),
                   jax.ShapeDtypeStruct((1, 2 * D_GROUP), F32)],
        input_output_aliases={4: 0},
        scratch_shapes=[pltpu.VMEM((s + SUBLANES, LANES), F32), pltpu.VMEM((s + SUBLANES, LANES), F32)],
        compiler_params=_params(("parallel",)),
    )(proj, conv_w, conv_b, d_qk, d_proj)


def _gelu_mul(a, b):
    return jax.nn.gelu(a) * b


GELU_C = math.sqrt(2.0 / math.pi)
GELU_K = 0.044715


def _gelu_mul_grads(a, b, d):
    a2 = a * a
    t = jnp.tanh(GELU_C * (a + GELU_K * (a * a2)))
    cdf = 0.5 * (1.0 + t)
    slope = cdf + (0.5 * GELU_C) * a * (1.0 - t * t) * (1.0 + (3.0 * GELU_K) * a2)
    return d * b * slope, d * (a * cdf)


FFN_BLOCKS = D_FF_P // LANES


def _ffn_conv_fwd(u, conv_w, conv_b):
    s = u.shape[0]

    def body(g_ref, v_ref, wg_ref, wv_ref, bg_ref, bv_ref, o_ref, gpad_ref, vpad_ref):
        for pad_ref, x_ref in ((gpad_ref, g_ref), (vpad_ref, v_ref)):
            pad_ref[0:SUBLANES, :] = jnp.zeros((SUBLANES, LANES), F32)
            pad_ref[SUBLANES:, :] = x_ref[...].astype(F32)
        taps_g, bias_g = _vreg_rows(wg_ref, FFN_CONV), _vreg_rows(bg_ref, 1)[0]
        taps_v, bias_v = _vreg_rows(wv_ref, FFN_CONV), _vreg_rows(bv_ref, 1)[0]
        for r0 in range(0, s, ROW_TILE):
            rows = min(ROW_TILE, s - r0)
            ug = _conv_fwd_tile(gpad_ref, taps_g, bias_g, r0, rows)
            uv = _conv_fwd_tile(vpad_ref, taps_v, bias_v, r0, rows)
            o_ref[r0:r0 + rows, :] = _gelu_mul(ug, uv).reshape(rows, LANES).astype(o_ref.dtype)

    col = lambda off: (lambda j: (0, off + j))
    return pl.pallas_call(
        body, name="ffn_conv_fwd", grid=(FFN_BLOCKS,),
        in_specs=[pl.BlockSpec((s, LANES), col(0)), pl.BlockSpec((s, LANES), col(FFN_BLOCKS)),
                  pl.BlockSpec((FFN_CONV, LANES), col(0)), pl.BlockSpec((FFN_CONV, LANES), col(FFN_BLOCKS)),
                  pl.BlockSpec((1, LANES), col(0)), pl.BlockSpec((1, LANES), col(FFN_BLOCKS))],
        out_specs=pl.BlockSpec((s, LANES), col(0)),
        out_shape=jax.ShapeDtypeStruct((s, D_FF_P), BF16),
        scratch_shapes=[pltpu.VMEM((s + SUBLANES, LANES), F32), pltpu.VMEM((s + SUBLANES, LANES), F32)],
        compiler_params=_params(("parallel",)),
    )(u, u, conv_w, conv_w, conv_b, conv_b)


def _ffn_conv_bwd(u, conv_w, conv_b, d_h):
    s = u.shape[0]

    def body(g_ref, v_ref, wg_ref, wv_ref, bg_ref, bv_ref, dh_ref,
             dug_ref, duv_ref, dwg_ref, dwv_ref, dbg_ref, dbv_ref,
             gpad_ref, vpad_ref, dgpad_ref, dvpad_ref):
        for pad_ref, x_ref in ((gpad_ref, g_ref), (vpad_ref, v_ref)):
            pad_ref[0:SUBLANES, :] = jnp.zeros((SUBLANES, LANES), F32)
            pad_ref[SUBLANES:, :] = x_ref[...].astype(F32)
        dgpad_ref[s:, :] = jnp.zeros((SUBLANES, LANES), F32)
        dvpad_ref[s:, :] = jnp.zeros((SUBLANES, LANES), F32)
        taps_g, bias_g = _vreg_rows(wg_ref, FFN_CONV), _vreg_rows(bg_ref, 1)[0]
        taps_v, bias_v = _vreg_rows(wv_ref, FFN_CONV), _vreg_rows(bv_ref, 1)[0]
        dbg = jnp.zeros((SUBLANES, LANES), F32)
        dbv = jnp.zeros((SUBLANES, LANES), F32)
        for r0 in range(0, s, ROW_TILE):
            rows = min(ROW_TILE, s - r0)
            ug = _conv_fwd_tile(gpad_ref, taps_g, bias_g, r0, rows)
            uv = _conv_fwd_tile(vpad_ref, taps_v, bias_v, r0, rows)
            d_ug, d_uv = _gelu_mul_grads(ug, uv, _stack(dh_ref, r0, rows))
            dgpad_ref[r0:r0 + rows, :] = d_ug.reshape(rows, LANES)
            dvpad_ref[r0:r0 + rows, :] = d_uv.reshape(rows, LANES)
            dbg = dbg + jnp.sum(d_ug, axis=0)
            dbv = dbv + jnp.sum(d_uv, axis=0)
        dbg_ref[...] = _column_total(dbg)
        dbv_ref[...] = _column_total(dbv)
        for pad_ref, dpad_ref, taps_w, dx_ref, dw_ref in ((gpad_ref, dgpad_ref, taps_g, dug_ref, dwg_ref),
                                                          (vpad_ref, dvpad_ref, taps_v, duv_ref, dwv_ref)):
            dws = [jnp.zeros((SUBLANES, LANES), F32) for _ in range(FFN_CONV)]
            for r0 in range(0, s, ROW_TILE):
                _conv_grads_tile(pad_ref, dpad_ref, dx_ref, taps_w, dws, r0, min(ROW_TILE, s - r0))
            for j in range(FFN_CONV):
                dw_ref[j:j + 1, :] = _column_total(dws[j])

    col = lambda off: (lambda j: (0, off + j))
    seq = pl.BlockSpec((s, LANES), col(0))
    return pl.pallas_call(
        body, name="ffn_conv_bwd", grid=(FFN_BLOCKS,),
        in_specs=[pl.BlockSpec((s, LANES), col(0)), pl.BlockSpec((s, LANES), col(FFN_BLOCKS)),
                  pl.BlockSpec((FFN_CONV, LANES), col(0)), pl.BlockSpec((FFN_CONV, LANES), col(FFN_BLOCKS)),
                  pl.BlockSpec((1, LANES), col(0)), pl.BlockSpec((1, LANES), col(FFN_BLOCKS)), seq],
        out_specs=[seq, seq, pl.BlockSpec((FFN_CONV, LANES), col(0)), pl.BlockSpec((FFN_CONV, LANES), col(0)),
                   pl.BlockSpec((1, LANES), col(0)), pl.BlockSpec((1, LANES), col(0))],
        out_shape=[jax.ShapeDtypeStruct((s, D_FF_P), BF16), jax.ShapeDtypeStruct((s, D_FF_P), BF16),
                   jax.ShapeDtypeStruct((FFN_CONV, D_FF_P), F32), jax.ShapeDtypeStruct((FFN_CONV, D_FF_P), F32),
                   jax.ShapeDtypeStruct((1, D_FF_P), F32), jax.ShapeDtypeStruct((1, D_FF_P), F32)],
        scratch_shapes=[pltpu.VMEM((s + SUBLANES, LANES), F32) for _ in range(4)],
        compiler_params=_params(("parallel",)),
    )(u, u, conv_w, conv_w, conv_b, conv_b, d_h)


def _chunk_masks(c):
    row = lax.broadcasted_iota(jnp.int32, (c, c), 0)
    col = lax.broadcasted_iota(jnp.int32, (c, c), 1)
    return row, col


@jax.custom_vjp
def _split_heads(x):
    return tuple(x[:, h * D_HEAD:(h + 1) * D_HEAD] for h in range(N_HEADS))


_split_heads.defvjp(lambda x: (_split_heads(x), None), lambda _, gs: (jnp.concatenate(gs, axis=1),))


@jax.custom_vjp
def _merge_heads(xs):
    return jnp.concatenate(xs, axis=1)


_merge_heads.defvjp(lambda xs: (_merge_heads(xs), None), lambda _, g: (_split_heads(g),))


@jax.custom_vjp
def _split_chunks(x):
    return tuple(x[i * CHUNK:(i + 1) * CHUNK] for i in range(x.shape[0] // CHUNK))


_split_chunks.defvjp(lambda x: (_split_chunks(x), None), lambda _, gs: (jnp.concatenate(gs, axis=0),))


@jax.custom_vjp
def _merge_chunks(xs):
    return jnp.concatenate(xs, axis=0)


_merge_chunks.defvjp(lambda xs: (_merge_chunks(xs), None), lambda _, g: (_split_chunks(g),))


def _blocks(x):
    return [_split_heads(rows) for rows in _split_chunks(x)]


def _per_chunk_rows(per_chunk, rid):
    out = per_chunk[0]
    for i in range(1, len(per_chunk)):
        out = jnp.where(rid >= i * CHUNK, per_chunk[i], out)
    return out


HEADS = range(N_HEADS)
CHUNKS_PER_STEP = 8
ML_CHUNKS_PER_STEP = 1


def _hg_chunk(hq, hf, hi, hgate, l0, l1, nw, sts):
    n = hq.shape[0] // CHUNK
    causal = _chunk_masks(CHUNK)
    causal = causal[1] <= causal[0]
    mx = lax.stop_gradient(jnp.maximum(l0, l1))
    e0 = jnp.exp(l0 - mx)
    e1 = jnp.exp(l1 - mx)
    lb = e0 / (e0 + e1)
    sig = jax.nn.sigmoid(hf)
    lf = jnp.log(lb + (1.0 - lb) * sig)
    k = (1.0 - lb) * jax.nn.sigmoid(-hf)
    q = jax.nn.silu(hq)
    tri = causal.astype(F32)
    b = _merge_chunks(tuple(_dg(tri, rows, 1, 0, HIGHEST) for rows in _split_chunks(lf)))
    rid = lax.broadcasted_iota(jnp.int32, b.shape, 0)
    pick = lambda r: jnp.sum(jnp.where(rid == r, b, 0.0), axis=0, keepdims=True)
    b_last_c = [pick(i * CHUNK + CHUNK - 1) for i in range(n)]
    b_ref = _per_chunk_rows([pick(i * CHUNK + CHUNK // 2 - 1) for i in range(n)], rid)
    b_last = _per_chunk_rows(b_last_c, rid)
    qa = _blocks(q * jnp.exp(b - b_ref))
    ka = _blocks(k * jnp.exp(b_ref - b))
    qe = _blocks(q * jnp.exp(b))
    kd = _blocks(k * jnp.exp(b_last - b))
    decay = [_split_heads(jnp.exp(b_last_c[i])) for i in range(n)]
    v = _blocks(hi)
    chunks = range(n)
    attn = [[jnp.where(causal, _nt(qa[i][h], ka[i][h]), 0.0) for h in HEADS] for i in chunks]
    intra = [[_nn(attn[i][h], v[i][h]) for h in HEADS] for i in chunks]
    kv = [[_tn(v[i][h], kd[i][h]) for h in HEADS] for i in chunks]
    normed = []
    for i in chunks:
        inter = [_nt(qe[i][h], sts[h]) for h in HEADS]
        sts = tuple(decay[i][h] * sts[h] + kv[i][h] for h in HEADS)
        o = [intra[i][h] + inter[h] for h in HEADS]
        normed.append(_merge_heads(tuple(o[h] * lax.rsqrt(jnp.mean(o[h] * o[h], axis=-1, keepdims=True) + LN_EPS)
                                         for h in HEADS)))
    return _merge_chunks(tuple(normed)) * nw * jax.nn.silu(hgate), sts


def _seg(ref, seg):
    return ref[:, seg * D_GROUP:(seg + 1) * D_GROUP]


def _hgrn2_fwd(proj, logits, norm_w):
    s = proj.shape[0]
    rows = CHUNKS_PER_STEP * CHUNK
    nc = s // rows

    def body(p_ref, lg_ref, nw_ref, y_ref, st_out_ref, st_scr):
        @pl.when(pl.program_id(0) == 0)
        def _():
            st_scr[...] = jnp.zeros_like(st_scr)

        sts = tuple(st_scr[h] for h in HEADS)
        y, sts_new = _hg_chunk(_seg(p_ref, 0), _seg(p_ref, 1), _seg(p_ref, 2), _seg(p_ref, 3),
                               lg_ref[0:1, :], lg_ref[1:2, :], nw_ref[...], sts)
        y_ref[...] = y.astype(y_ref.dtype)
        for h in HEADS:
            st_out_ref[h] = sts[h]
            st_scr[h] = sts_new[h]

    return pl.pallas_call(
        body, name="hgrn2_fwd", grid=(nc,),
        in_specs=[pl.BlockSpec((rows, 4 * D_GROUP), lambda c: (c, 0)),
                  pl.BlockSpec((2, D_GROUP), lambda c: (0, 0)),
                  pl.BlockSpec((1, D_GROUP), lambda c: (0, 0))],
        out_specs=[pl.BlockSpec((rows, D_GROUP), lambda c: (c, 0)),
                   pl.BlockSpec((None, N_HEADS, D_HEAD, D_HEAD), lambda c: (c, 0, 0, 0))],
        out_shape=[jax.ShapeDtypeStruct((s, 2 * D_GROUP), BF16),
                   jax.ShapeDtypeStruct((nc, N_HEADS, D_HEAD, D_HEAD), F32)],
        scratch_shapes=[pltpu.VMEM((N_HEADS, D_HEAD, D_HEAD), F32)],
        compiler_params=_params(("arbitrary",)),
    )(proj, logits, norm_w)


def _hgrn2_bwd(proj, logits, norm_w, states, d_y):
    s = proj.shape[0]
    rows = CHUNKS_PER_STEP * CHUNK
    nc = s // rows

    def body(p_ref, lg_ref, nw_ref, st_ref, dy_ref, dp_ref, dl_ref, dnw_ref, dsum_ref, dst_scr):
        @pl.when(pl.program_id(0) == 0)
        def _():
            dst_scr[...] = jnp.zeros_like(dst_scr)
            dl_ref[...] = jnp.zeros_like(dl_ref)
            dnw_ref[...] = jnp.zeros_like(dnw_ref)
            dsum_ref[...] = jnp.zeros_like(dsum_ref)

        _, vjp = jax.vjp(_hg_chunk, _seg(p_ref, 0), _seg(p_ref, 1), _seg(p_ref, 2), _seg(p_ref, 3),
                         lg_ref[0:1, :], lg_ref[1:2, :], nw_ref[...], tuple(st_ref[h] for h in HEADS))
        d_hq, d_hf, d_hi, d_hg, d_l0, d_l1, d_nw, d_sts = vjp((dy_ref[...], tuple(dst_scr[h] for h in HEADS)))
        for seg, val in enumerate((d_hq, d_hf, d_hi, d_hg)):
            dp_ref[:, seg * D_GROUP:(seg + 1) * D_GROUP] = val.astype(dp_ref.dtype)
            dsum_ref[:, seg * D_GROUP:(seg + 1) * D_GROUP] += jnp.sum(val, axis=0, keepdims=True)
        dl_ref[0:1, :] += d_l0
        dl_ref[1:2, :] += d_l1
        dnw_ref[...] += d_nw
        for h in HEADS:
            dst_scr[h] = d_sts[h]

    rev = lambda c: nc - 1 - c
    return pl.pallas_call(
        body, name="hgrn2_bwd", grid=(nc,),
        in_specs=[pl.BlockSpec((rows, 4 * D_GROUP), lambda c: (rev(c), 0)),
                  pl.BlockSpec((2, D_GROUP), lambda c: (0, 0)),
                  pl.BlockSpec((1, D_GROUP), lambda c: (0, 0)),
                  pl.BlockSpec((None, N_HEADS, D_HEAD, D_HEAD), lambda c: (rev(c), 0, 0, 0)),
                  pl.BlockSpec((rows, D_GROUP), lambda c: (rev(c), 0))],
        out_specs=[pl.BlockSpec((rows, 4 * D_GROUP), lambda c: (rev(c), 0)),
                   pl.BlockSpec((2, D_GROUP), lambda c: (0, 0)),
                   pl.BlockSpec((1, D_GROUP), lambda c: (0, 0)),
                   pl.BlockSpec((1, 4 * D_GROUP), lambda c: (0, 0))],
        out_shape=[jax.ShapeDtypeStruct((s, D_IN_MAIN), BF16), jax.ShapeDtypeStruct((2, D_GROUP), F32),
                   jax.ShapeDtypeStruct((1, D_GROUP), F32), jax.ShapeDtypeStruct((1, 4 * D_GROUP), F32)],
        scratch_shapes=[pltpu.VMEM((N_HEADS, D_HEAD, D_HEAD), F32)],
        compiler_params=_params(("arbitrary",)),
    )(proj, logits, norm_w, states, d_y)


def _gate_column(gates, lane, idx):
    return jnp.sum(jnp.where(lane == idx, gates, 0.0), axis=1, keepdims=True)


def _head_layer_norm(h):
    mu = jnp.mean(h, axis=-1, keepdims=True)
    var = jnp.mean(jnp.square(h - mu), axis=-1, keepdims=True)
    return (h - mu) * lax.rsqrt(var + LN_EPS)


def _ml_chunk(qc, kc, v, mo, gates, nw, cts, ns, ms):
    n = qc.shape[0] // CHUNK
    row, col = _chunk_masks(CHUNK)
    mask = col <= row
    eye = col == row
    to_row = lambda t: jnp.sum(jnp.where(eye, t, 0.0), axis=0, keepdims=True)
    q = _blocks(qc * (D_HEAD ** -0.5))
    k = _blocks(kc)
    vs = _blocks(v)
    gate_rows = _split_chunks(gates)
    lane = lax.broadcasted_iota(jnp.int32, gate_rows[0].shape, 1)
    each = [(i, h) for i in range(n) for h in HEADS]
    on_each = lambda f: {ih: f(*ih) for ih in each}
    ig = on_each(lambda i, h: _gate_column(gate_rows[i], lane, h))
    lf = on_each(lambda i, h: jax.nn.log_sigmoid(_gate_column(gate_rows[i], lane, N_HEADS + h)))
    lf_row = on_each(lambda i, h: to_row(lf[i, h]))
    ig_row = on_each(lambda i, h: to_row(ig[i, h]))
    b_col = on_each(lambda i, h: jnp.sum(jnp.where(mask, lf_row[i, h], 0.0), axis=1, keepdims=True))
    b_row = on_each(lambda i, h: jnp.sum(jnp.where(row <= col, lf[i, h], 0.0), axis=0, keepdims=True))
    g = on_each(lambda i, h: jnp.sum(lf[i, h], axis=0, keepdims=True))
    d = on_each(lambda i, h: jnp.where(mask, b_col[i, h] - b_row[i, h] + ig_row[i, h], -jnp.inf))
    a = on_each(lambda i, h: g[i, h] - b_col[i, h] + ig[i, h])
    m_at = {(0, h): ms[h] for h in HEADS}
    for i, h in each:
        m_at[i + 1, h] = lax.stop_gradient(jnp.maximum(g[i, h] + m_at[i, h], jnp.max(a[i, h], axis=0, keepdims=True)))
    inter = on_each(lambda i, h: b_col[i, h] + m_at[i, h])
    m_t = on_each(lambda i, h: lax.stop_gradient(jnp.maximum(inter[i, h], jnp.max(d[i, h], axis=1, keepdims=True))))
    qk = on_each(lambda i, h: _nt(q[i][h], k[i][h]))
    sc = on_each(lambda i, h: qk[i, h] * jnp.exp(d[i, h] - m_t[i, h]))
    w_inter = on_each(lambda i, h: jnp.exp(inter[i, h] - m_t[i, h]))
    sv = on_each(lambda i, h: _nn(sc[i, h], vs[i][h]))
    decay = on_each(lambda i, h: jnp.exp(g[i, h] + m_at[i, h] - m_at[i + 1, h]))
    wk = on_each(lambda i, h: k[i][h] * jnp.exp(a[i, h] - m_at[i + 1, h]))
    kv = on_each(lambda i, h: _tn(vs[i][h], wk[i, h]))
    normed = []
    for i in range(n):
        qc_state = [_nt(q[i][h], cts[h]) for h in HEADS]
        num = [sv[i, h] + w_inter[i, h] * qc_state[h] for h in HEADS]
        den = [jnp.sum(sc[i, h], axis=1, keepdims=True)
               + w_inter[i, h] * jnp.sum(q[i][h] * ns[h], axis=1, keepdims=True) for h in HEADS]
        hh = [num[h] / jnp.maximum(jnp.abs(den[h]), jnp.exp(-m_t[i, h])) for h in HEADS]
        cts = tuple(decay[i, h] * cts[h] + kv[i, h] for h in HEADS)
        ns = tuple(decay[i, h] * ns[h] + jnp.sum(wk[i, h], axis=0, keepdims=True) for h in HEADS)
        normed.append(_merge_heads(tuple(_head_layer_norm(hh[h]) for h in HEADS)))
    y = jax.nn.sigmoid(mo) * (_merge_chunks(tuple(normed)) * nw)
    return y, cts, ns, tuple(m_at[n, h] for h in HEADS)


def _mlstm_fwd(qk, proj, gates, norm_w, y):
    s = proj.shape[0]
    rows = ML_CHUNKS_PER_STEP * CHUNK
    nc = s // rows

    def body(qk_ref, vo_ref, g_ref, nw_ref, _, y_ref, ct_out, n_out, m_out, ct_scr, n_scr, m_scr):
        @pl.when(pl.program_id(0) == 0)
        def _():
            ct_scr[...] = jnp.zeros_like(ct_scr)
            n_scr[...] = jnp.zeros_like(n_scr)
            m_scr[...] = jnp.full(m_scr.shape, NEG_BIG, F32)

        cts = tuple(ct_scr[h] for h in HEADS)
        ns = tuple(n_scr[h] for h in HEADS)
        ms = tuple(m_scr[h] for h in HEADS)
        y, cts_new, ns_new, ms_new = _ml_chunk(_seg(qk_ref, 0), _seg(qk_ref, 1), _seg(vo_ref, 0), _seg(vo_ref, 1),
                                               g_ref[...], nw_ref[...], cts, ns, ms)
        y_ref[...] = y.astype(y_ref.dtype)
        for h in HEADS:
            ct_out[h], n_out[h], m_out[h] = cts[h], ns[h], ms[h]
            ct_scr[h], n_scr[h], m_scr[h] = cts_new[h], ns_new[h], ms_new[h]

    st = lambda r, w: pl.BlockSpec((None, N_HEADS, r, w), lambda c: (c, 0, 0, 0))
    return pl.pallas_call(
        body, name="mlstm_fwd", grid=(nc,),
        in_specs=[pl.BlockSpec((rows, 2 * D_GROUP), lambda c: (c, 0)),
                  pl.BlockSpec((rows, 2 * D_GROUP), lambda c: (c, 3)),
                  pl.BlockSpec((rows, LANES), lambda c: (c, 0)),
                  pl.BlockSpec((1, D_GROUP), lambda c: (0, 0)),
                  pl.BlockSpec(memory_space=pl.ANY)],
        out_specs=[pl.BlockSpec((rows, D_GROUP), lambda c: (c, 1)),
                   st(D_HEAD, D_HEAD), st(1, D_HEAD), st(1, 1)],
        out_shape=[jax.ShapeDtypeStruct(y.shape, y.dtype),
                   jax.ShapeDtypeStruct((nc, N_HEADS, D_HEAD, D_HEAD), F32),
                   jax.ShapeDtypeStruct((nc, N_HEADS, 1, D_HEAD), F32),
                   jax.ShapeDtypeStruct((nc, N_HEADS, 1, 1), F32)],
        input_output_aliases={4: 0},
        scratch_shapes=[pltpu.VMEM((N_HEADS, D_HEAD, D_HEAD), F32), pltpu.VMEM((N_HEADS, 1, D_HEAD), F32),
                        pltpu.VMEM((N_HEADS, 1, 1), F32)],
        compiler_params=_params(("arbitrary",)),
    )(qk, proj, gates, norm_w, y)


def _mlstm_bwd(qk, proj, gates, norm_w, ct_s, n_s, m_s, d_y, d_proj):
    s = proj.shape[0]
    rows = ML_CHUNKS_PER_STEP * CHUNK
    nc = s // rows

    def body(qk_ref, vo_ref, g_ref, nw_ref, ct_ref, n_ref, m_ref, dy_ref, _,
             dp_ref, dqk_ref, dg_ref, dnw_ref, dsum_ref, dct_scr, dn_scr):
        @pl.when(pl.program_id(0) == 0)
        def _():
            dct_scr[...] = jnp.zeros_like(dct_scr)
            dn_scr[...] = jnp.zeros_like(dn_scr)
            dnw_ref[...] = jnp.zeros_like(dnw_ref)
            dsum_ref[...] = jnp.zeros_like(dsum_ref)

        ms = tuple(m_ref[h] for h in HEADS)
        step = lambda *a: _ml_chunk(*a, ms)[:3]
        _, vjp = jax.vjp(step, _seg(qk_ref, 0), _seg(qk_ref, 1), _seg(vo_ref, 0), _seg(vo_ref, 1), g_ref[...],
                         nw_ref[...], tuple(ct_ref[h] for h in HEADS), tuple(n_ref[h] for h in HEADS))
        d_q, d_k, d_v, d_o, d_gates, d_nw, d_cts, d_ns = vjp(
            (dy_ref[...], tuple(dct_scr[h] for h in HEADS), tuple(dn_scr[h] for h in HEADS)))
        dqk_ref[:, 0:D_GROUP] = d_q
        dqk_ref[:, D_GROUP:2 * D_GROUP] = d_k
        for seg, val in enumerate((d_v, d_o)):
            dp_ref[:, seg * D_GROUP:(seg + 1) * D_GROUP] = val.astype(dp_ref.dtype)
            dsum_ref[:, seg * D_GROUP:(seg + 1) * D_GROUP] += jnp.sum(val, axis=0, keepdims=True)
        dg_ref[...] = d_gates
        dnw_ref[...] += d_nw
        for h in HEADS:
            dct_scr[h] = d_cts[h]
            dn_scr[h] = d_ns[h]

    rev = lambda c: nc - 1 - c
    st = lambda r, w: pl.BlockSpec((None, N_HEADS, r, w), lambda c: (rev(c), 0, 0, 0))
    return pl.pallas_call(
        body, name="mlstm_bwd", grid=(nc,),
        in_specs=[pl.BlockSpec((rows, 2 * D_GROUP), lambda c: (rev(c), 0)),
                  pl.BlockSpec((rows, 2 * D_GROUP), lambda c: (rev(c), 3)),
                  pl.BlockSpec((rows, LANES), lambda c: (rev(c), 0)),
                  pl.BlockSpec((1, D_GROUP), lambda c: (0, 0)),
                  st(D_HEAD, D_HEAD), st(1, D_HEAD), st(1, 1),
                  pl.BlockSpec((rows, D_GROUP), lambda c: (rev(c), 1)),
                  pl.BlockSpec(memory_space=pl.ANY)],
        out_specs=[pl.BlockSpec((rows, 2 * D_GROUP), lambda c: (rev(c), 3)),
                   pl.BlockSpec((rows, 2 * D_GROUP), lambda c: (rev(c), 0)),
                   pl.BlockSpec((rows, LANES), lambda c: (rev(c), 0)),
                   pl.BlockSpec((1, D_GROUP), lambda c: (0, 0)),
                   pl.BlockSpec((1, 2 * D_GROUP), lambda c: (0, 0))],
        out_shape=[jax.ShapeDtypeStruct(d_proj.shape, d_proj.dtype), jax.ShapeDtypeStruct((s, 2 * D_GROUP), F32),
                   jax.ShapeDtypeStruct((s, LANES), F32), jax.ShapeDtypeStruct((1, D_GROUP), F32),
                   jax.ShapeDtypeStruct((1, 2 * D_GROUP), F32)],
        input_output_aliases={8: 0},
        scratch_shapes=[pltpu.VMEM((N_HEADS, D_HEAD, D_HEAD), F32), pltpu.VMEM((N_HEADS, 1, D_HEAD), F32)],
        compiler_params=_params(("arbitrary",)),
    )(qk, proj, gates, norm_w, ct_s, n_s, m_s, d_y, d_proj)


LN_TOKENS = 512
ATT_TOKENS = 512


def _proj_res_ln(a, w, xres, g, b, name):
    s, dm = xres.shape
    k = a.shape[1]
    tb = min(LN_TOKENS, s)

    def body(a_ref, w_ref, x_ref, g_ref, b_ref, z_ref, o_ref):
        halves = [slice(0, tb // 2), slice(tb // 2, tb)]
        zs = [ALPHA * x_ref[rows, :] + _nn_raw(a_ref[rows, :], w_ref[...]) for rows in halves]
        for rows, z in zip(halves, zs):
            z_ref[rows, :] = z
            o_ref[rows, :] = _layer_norm(z, g_ref[...], b_ref[...])

    tok = pl.BlockSpec((tb, dm), lambda i: (i, 0))
    vec = pl.BlockSpec((1, dm), lambda i: (0, 0))
    act = jax.ShapeDtypeStruct((s, dm), F32)
    return pl.pallas_call(
        body, name=name, grid=(s // tb,),
        in_specs=[pl.BlockSpec((tb, k), lambda i: (i, 0)), pl.BlockSpec((k, dm), lambda i: (0, 0)), tok, vec, vec],
        out_specs=[tok, tok], out_shape=[act, act], compiler_params=_params(("parallel",)),
    )(a, w, xres, g, b)


def _ln_bwd_proj(d_out, z, g, b, w, name):
    s, dm = z.shape
    k = w.shape[0]
    tb = min(LN_TOKENS, s)

    def body(do_ref, z_ref, g_ref, b_ref, w_ref, dz_ref, da_ref, dg_ref, db_ref):
        @pl.when(pl.program_id(0) == 0)
        def _():
            dg_ref[...] = jnp.zeros_like(dg_ref)
            db_ref[...] = jnp.zeros_like(db_ref)

        halves = [slice(0, tb // 2), slice(tb // 2, tb)]
        d_zs = []
        for rows in halves:
            _, vjp = jax.vjp(_layer_norm, z_ref[rows, :], g_ref[...], b_ref[...])
            d_z, d_g, d_b = vjp(do_ref[rows, :])
            dz_ref[rows, :] = d_z
            dg_ref[...] += d_g
            db_ref[...] += d_b
            d_zs.append(d_z)
        for rows, d_z in zip(halves, d_zs):
            da_ref[rows, :] = _nt_raw(d_z, w_ref[...])

    tok = pl.BlockSpec((tb, dm), lambda i: (i, 0))
    vec = pl.BlockSpec((1, dm), lambda i: (0, 0))
    return pl.pallas_call(
        body, name=name, grid=(s // tb,),
        in_specs=[tok, tok, vec, vec, pl.BlockSpec((k, dm), lambda i: (0, 0))],
        out_specs=[tok, pl.BlockSpec((tb, k), lambda i: (i, 0)), vec, vec],
        out_shape=[jax.ShapeDtypeStruct((s, dm), F32), jax.ShapeDtypeStruct((s, k), F32),
                   jax.ShapeDtypeStruct((1, dm), F32), jax.ShapeDtypeStruct((1, dm), F32)],
        compiler_params=_params(("arbitrary",)),
    )(d_out, z, g, b, w)


def _proj_loss_tail(a, w, xres, g, b, target):
    s, dm = xres.shape
    k = a.shape[1]
    tb = min(ATT_TOKENS, s)

    def loss_fn(z, gg, bb, tgt):
        err = jnp.square(_layer_norm(z, gg, bb) - tgt)
        return 0.5 * jnp.sum(jnp.mean(err, axis=-1, keepdims=True), axis=0, keepdims=True)

    def body(a_ref, w_ref, x_ref, g_ref, b_ref, t_ref, loss_ref, dz_ref, dg_ref, db_ref):
        @pl.when(pl.program_id(0) == 0)
        def _():
            loss_ref[...] = jnp.zeros_like(loss_ref)
            dg_ref[...] = jnp.zeros_like(dg_ref)
            db_ref[...] = jnp.zeros_like(db_ref)

        halves = [slice(0, tb // 2), slice(tb // 2, tb)]
        zs = [ALPHA * x_ref[rows, :] + _nn_raw(a_ref[rows, :], w_ref[...]) for rows in halves]
        for rows, z in zip(halves, zs):
            tgt = t_ref[rows, :]
            loss, vjp = jax.vjp(lambda zz, gg, bb, tgt=tgt: loss_fn(zz, gg, bb, tgt), z, g_ref[...], b_ref[...])
            d_z, d_g, d_b = vjp(jnp.ones((1, 1), F32))
            loss_ref[...] += loss
            dz_ref[rows, :] = d_z
            dg_ref[...] += d_g
            db_ref[...] += d_b

    tok = pl.BlockSpec((tb, dm), lambda i: (i, 0))
    vec = pl.BlockSpec((1, dm), lambda i: (0, 0))
    one = pl.BlockSpec((1, 1), lambda i: (0, 0))
    return pl.pallas_call(
        body, name="ffn_down_loss_tail", grid=(s // tb,),
        in_specs=[pl.BlockSpec((tb, k), lambda i: (i, 0)), pl.BlockSpec((k, dm), lambda i: (0, 0)), tok, vec, vec, tok],
        out_specs=[one, tok, vec, vec],
        out_shape=[jax.ShapeDtypeStruct((1, 1), F32), jax.ShapeDtypeStruct((s, dm), F32),
                   jax.ShapeDtypeStruct((1, dm), F32), jax.ShapeDtypeStruct((1, dm), F32)],
        compiler_params=_params(("arbitrary",)),
    )(a, w, xres, g, b, target)


def _att_heads(qs, ks, vs):
    sc = [_nt(q, k) * (CA_DH ** -0.5) for q, k in zip(qs, ks)]
    p = [jax.nn.softmax(s, axis=-1) for s in sc]
    return tuple(_nn(pp, v) for pp, v in zip(p, vs))


def _head_slices(ref_or_value, offset):
    return tuple(ref_or_value[:, offset + h * CA_DH:offset + (h + 1) * CA_DH] for h in range(CA_HEADS))


def _cross_attention_fwd(x1, kv, wq, wo, g, b):
    s = x1.shape[0]
    tb = min(ATT_TOKENS, s)

    def body(x_ref, kv_ref, wq_ref, wo_ref, g_ref, b_ref, att_ref, z_ref, o_ref):
        x_blk = x_ref[...]
        q = _nn_raw(x_blk, wq_ref[...])
        att = jnp.concatenate(_att_heads(_head_slices(q, 0), _head_slices(kv_ref, 0), _head_slices(kv_ref, D_MODEL)),
                              axis=1)
        att_ref[...] = att.astype(att_ref.dtype)
        z = ALPHA * x_blk + _nn_raw(att, wo_ref[...])
        z_ref[...] = z
        o_ref[...] = _layer_norm(z, g_ref[...], b_ref[...])

    tok = pl.BlockSpec((tb, D_MODEL), lambda i: (i, 0))
    mat = pl.BlockSpec((D_MODEL, D_MODEL), lambda i: (0, 0))
    vec = pl.BlockSpec((1, D_MODEL), lambda i: (0, 0))
    act = jax.ShapeDtypeStruct((s, D_MODEL), F32)
    return pl.pallas_call(
        body, name="cross_attention_fwd", grid=(s // tb,),
        in_specs=[tok, pl.BlockSpec((N_MEM, 2 * D_MODEL), lambda i: (0, 0)), mat, mat, vec, vec],
        out_specs=[tok, tok, tok],
        out_shape=[jax.ShapeDtypeStruct((s, D_MODEL), BF16), act, act],
        compiler_params=_params(("parallel",)),
    )(x1, kv, wq, wo, g, b)


def _cross_attention_bwd(d_x2, x1, z2, kv, wq, wo, g, b):
    s = x1.shape[0]
    tb = min(ATT_TOKENS, s)

    def body(dx2_ref, x_ref, z_ref, kv_ref, wq_ref, wo_ref, g_ref, b_ref,
             dx1_ref, dq_ref, dz_ref, dkv_ref, dg_ref, db_ref):
        @pl.when(pl.program_id(0) == 0)
        def _():
            dkv_ref[...] = jnp.zeros_like(dkv_ref)
            dg_ref[...] = jnp.zeros_like(dg_ref)
            db_ref[...] = jnp.zeros_like(db_ref)

        q = _nn_raw(x_ref[...], wq_ref[...])
        _, ln_vjp = jax.vjp(_layer_norm, z_ref[...], g_ref[...], b_ref[...])
        d_z, d_g, d_b = ln_vjp(dx2_ref[...])
        dg_ref[...] += d_g
        db_ref[...] += d_b
        dz_ref[...] = d_z.astype(dz_ref.dtype)
        d_att = _nt_raw(d_z, wo_ref[...])
        _, vjp = jax.vjp(_att_heads, _head_slices(q, 0), _head_slices(kv_ref, 0), _head_slices(kv_ref, D_MODEL))
        d_qs, d_ks, d_vs = vjp(_head_slices(d_att, 0))
        for h in range(CA_HEADS):
            lo = h * CA_DH
            dkv_ref[:, lo:lo + CA_DH] += d_ks[h]
            dkv_ref[:, D_MODEL + lo:D_MODEL + lo + CA_DH] += d_vs[h]
        d_q = jnp.concatenate(d_qs, axis=1)
        dq_ref[...] = d_q.astype(dq_ref.dtype)
        dx1_ref[...] = ALPHA * d_z + _nt_raw(d_q, wq_ref[...])

    tok = pl.BlockSpec((tb, D_MODEL), lambda i: (i, 0))
    mem = pl.BlockSpec((N_MEM, 2 * D_MODEL), lambda i: (0, 0))
    mat = pl.BlockSpec((D_MODEL, D_MODEL), lambda i: (0, 0))
    vec = pl.BlockSpec((1, D_MODEL), lambda i: (0, 0))
    low = jax.ShapeDtypeStruct((s, D_MODEL), BF16)
    return pl.pallas_call(
        body, name="cross_attention_bwd", grid=(s // tb,),
        in_specs=[tok, tok, tok, mem, mat, mat, vec, vec], out_specs=[tok, tok, tok, mem, vec, vec],
        out_shape=[jax.ShapeDtypeStruct((s, D_MODEL), F32), low, low,
                   jax.ShapeDtypeStruct((N_MEM, 2 * D_MODEL), F32),
                   jax.ShapeDtypeStruct((1, D_MODEL), F32), jax.ShapeDtypeStruct((1, D_MODEL), F32)],
        compiler_params=_params(("arbitrary",)),
    )(d_x2, x1, z2, kv, wq, wo, g, b)


def _local_step(x, mem, target, w, mid_weights=None, ffn_weights=None, down_weights=None, on_ffn_grads=None,
                on_mid_grads=None,
                on_small_grads=None, on_last_grads=None):
    w = dict(w)
    s = x.shape[0]
    tm = min(512, s)
    tt_big = min(1024, s)
    proj = _matmul_nt(x, w["w_in_t"], min(2048, s), 512, "proj", D_IN_MAIN, w["b_in_main"])
    gates = _matmul_nt(x, w["w_in_gate_t"], tm, LANES, "proj_gates", bias=w["b_in_gate"])
    qk = _ml_conv_fwd(proj, w["ml_conv_w"], w["ml_conv_b"])
    y, hg_states = _hgrn2_fwd(proj, w["hg_lb_logits"], w["hg_norm_w"])
    y, ct_s, n_s, m_s = _mlstm_fwd(qk, proj, gates, w["ml_norm_w"], y)
    if mid_weights is not None:
        w.update(mid_weights(y))
    z1, x1 = _proj_res_ln(y, w["w_out"], x, w["ln1_g"], w["ln1_b"], "out_proj_ln1")
    kv = _matmul_nn(mem, w["ca_wkv"], None, N_MEM, CA_DH, "kv")
    att, z2, x2 = _cross_attention_fwd(x1, kv, w["ca_wq"], w["ca_wo"], w["ln2_g"], w["ln2_b"])
    if ffn_weights is not None:
        w.update(ffn_weights(x2))
    u = _matmul_nt(x2, w["ffn_w_up_t"], min(2048, s), UP_SHARD_P, "ffn_up", out_dtype=BF16)
    hid = _ffn_conv_fwd(u, w["ffn_conv_w"], w["ffn_conv_b"])
    if down_weights is not None:
        w.update(down_weights(hid))
    loss, d_z3, d_ln3_g, d_ln3_b = _proj_loss_tail(hid, w["ffn_w_down"], x2, w["ln3_g"], w["ln3_b"], target)
    grads = {"ln3_g": d_ln3_g, "ln3_b": d_ln3_b}
    grads["ffn_w_down"] = _matmul_tn(hid, d_z3, 1536, D_MODEL, tt_big, "d_w_down")
    d_hid = _matmul_nt(d_z3, w["ffn_w_down"], tm, D_FF_P, "d_hid", out_dtype=BF16)
    d_ug, d_uv, d_cwg, d_cwv, d_cbg, d_cbv = _ffn_conv_bwd(u, w["ffn_conv_w"], w["ffn_conv_b"], d_hid)
    grads["ffn_conv_w"] = jnp.concatenate([d_cwg, d_cwv], axis=-1)
    grads["ffn_conv_b"] = jnp.concatenate([d_cbg, d_cbv], axis=-1)
    d_w_up = _matmul_tn(d_ug, x2, D_FF_P // 2, D_MODEL, tt_big, "d_w_up_gate", rows=D_UP_P)
    grads["ffn_w_up"] = _matmul_tn(d_uv, x2, D_FF_P // 2, D_MODEL, tt_big, "d_w_up_val", rows=D_UP_P, row0=D_FF_P,
                                   into=d_w_up)
    d_x2 = _matmul_nn_sum([(d_ug, w["ffn_w_up_t"], 0), (d_uv, w["ffn_w_up_t"], D_FF_P)], d_z3, ALPHA,
                          min(256, s), "d_x2")
    if on_ffn_grads is not None:
        d_x2 = on_ffn_grads(grads, d_x2)
    d_x1, d_q, d_z2, d_kv, grads["ln2_g"], grads["ln2_b"] = _cross_attention_bwd(
        d_x2, x1, z2, kv, w["ca_wq"], w["ca_wo"], w["ln2_g"], w["ln2_b"])
    grads["ca_wo"] = _matmul_tn(att, d_z2, D_MODEL, D_MODEL, tt_big, "d_ca_wo")
    grads["ca_wq"] = _matmul_tn(x1, d_q, D_MODEL, D_MODEL, tt_big, "d_ca_wq")
    grads["ca_wkv"] = _matmul_tn(mem, d_kv, D_MODEL, CA_DH, N_MEM, "d_ca_wkv", shards=N_DEV, group=N_DEV)
    d_z1, d_y, grads["ln1_g"], grads["ln1_b"] = _ln_bwd_proj(d_x1, z1, w["ln1_g"], w["ln1_b"], w["w_out"],
                                                             "ln1_bwd_out_proj")
    grads["w_out"] = _matmul_tn(y, d_z1, D_MODEL, D_MODEL, tt_big, "d_w_out")
    if on_mid_grads is not None:
        d_y = on_mid_grads(grads, d_y)
    d_proj, grads["hg_lb_logits"], grads["hg_norm_w"], db_hg = _hgrn2_bwd(
        proj, w["hg_lb_logits"], w["hg_norm_w"], hg_states, d_y)
    d_proj, d_qk, d_gates, grads["ml_norm_w"], db_vo = _mlstm_bwd(
        qk, proj, gates, w["ml_norm_w"], ct_s, n_s, m_s, d_y, d_proj)
    d_proj, grads["ml_conv_w"], grads["ml_conv_b"], db_qk = _ml_conv_bwd(
        proj, w["ml_conv_w"], w["ml_conv_b"], d_qk, d_proj)
    grads["b_in_main"] = jnp.concatenate([db_hg, db_qk, db_vo], axis=-1)
    grads["w_in_gate"] = _matmul_tn(d_gates, x, LANES, D_MODEL, tt_big, "d_w_in_gates")
    grads["b_in_gate"] = jnp.sum(d_gates, axis=0, keepdims=True)
    if on_small_grads is not None:
        d_proj = on_small_grads(grads, loss, d_proj)
    grads["w_in_main"] = _matmul_tn(d_proj, x, min(2048, D_IN_MAIN), D_MODEL, tt_big, "d_w_in")
    if on_last_grads is not None:
        d_z1 = on_last_grads(grads, d_z1)
    grad_x = _matmul_nn_sum([(d_proj, w["w_in_t"], 0), (d_gates, w["w_in_gate_t"], 0)], d_z1, ALPHA, tm, "d_x")
    return loss, grad_x, grads


HBM_SPEC = pl.BlockSpec(memory_space=pltpu.HBM)


def _coords():
    return lax.axis_index("x"), lax.axis_index("y"), lax.axis_index("c")


def _other_chips(x, y):
    return [(1 - x, y), (x, 1 - y), (1 - x, 1 - y)]


def _my_slot():
    x, y, c = _coords()
    return 4 * x + 2 * y + c


SEM_SPEC = pl.BlockSpec(memory_space=pltpu.SEMAPHORE)
ANY_SPEC = pl.BlockSpec(memory_space=pl.ANY)
SIDE_EFFECT = pltpu.SideEffectType.DATAFLOW_SIDE_EFFECTING


def _peer(x, y, c, d):
    flip = lambda v, bit: 1 - v if bit else v
    p = (flip(x, d & 4), flip(y, d & 2), flip(c, d & 1))
    return p, 4 * p[0] + 2 * p[1] + p[2]


def _direct_copies(gather, src_refs, land_refs, send_sems, recv_sems):
    x, y, c = _coords()
    me = 4 * x + 2 * y + c
    copies = []
    for a in range(len(src_refs)):
        for d in range(1, N_DEV):
            peer, peer_slot = _peer(x, y, c, d)
            copies.append(pltpu.make_async_remote_copy(
                src_ref=src_refs[a] if gather else src_refs[a].at[peer_slot],
                dst_ref=land_refs[a].at[me] if gather else land_refs[a].at[d - 1],
                send_sem=send_sems.at[7 * a + d - 1], recv_sem=recv_sems.at[7 * a + d - 1],
                device_id=peer, device_id_type=MESH))
    return copies


def _hbm(t):
    return pltpu.HBM(t.shape, t.dtype)


def _chip_copies(src_refs, land_refs, send_sems, recv_sems):
    x, y, c = _coords()
    me = 4 * x + 2 * y + c
    targets = [(x, y, 1 - c)] + [(cx, cy, c) for cx, cy in _other_chips(x, y)]
    return [pltpu.make_async_remote_copy(
        src_ref=src_refs[a], dst_ref=land_refs[a].at[me], send_sem=send_sems.at[4 * a + k],
        recv_sem=recv_sems.at[4 * a + k], device_id=target, device_id_type=MESH)
        for a in range(len(src_refs)) for k, target in enumerate(targets)]


def _forward_copies(land_refs, send_sems, recv_sems):
    x, y, c = _coords()
    return [pltpu.make_async_remote_copy(
        src_ref=land_refs[a].at[4 * cx + 2 * cy + c], dst_ref=land_refs[a].at[4 * cx + 2 * cy + c],
        send_sem=send_sems.at[3 * a + j], recv_sem=recv_sems.at[3 * a + j],
        device_id=(x, y, 1 - c), device_id_type=MESH)
        for a in range(len(land_refs)) for j, (cx, cy) in enumerate(_other_chips(x, y))]


def _split_copy_start(make_copies, n_sems, operands, through, name):
    n_ops = len(operands)

    def body(*refs):
        for cp in make_copies(refs[:n_ops], refs[n_ops + 1], refs[n_ops + 2]):
            cp.start()

    ins = [pltpu.with_memory_space_constraint(t, pltpu.HBM) for t in (*operands, through)]
    sems = pltpu.SemaphoreType.DMA((n_sems,))
    res = pl.pallas_call(
        body, name=name, out_shape=(sems, sems, *[_hbm(t) for t in ins]),
        in_specs=[HBM_SPEC] * (n_ops + 1), out_specs=(SEM_SPEC, SEM_SPEC, *[HBM_SPEC] * (n_ops + 1)),
        input_output_aliases={i: 2 + i for i in range(n_ops + 1)},
        compiler_params=pltpu.CompilerParams(has_side_effects=SIDE_EFFECT),
    )(*ins)
    return (res[0], res[1], list(res[2:2 + n_ops])), res[2 + n_ops]


def _split_copy_wait(make_copies, started, after, name):
    send_sems, recv_sems, operands = started
    n_ops = len(operands)
    after = list(after) if isinstance(after, (list, tuple)) else [after]

    def body(*refs):
        for cp in make_copies(refs[:n_ops], refs[n_ops], refs[n_ops + 1]):
            cp.wait_send()
            cp.wait_recv()

    res = pl.pallas_call(
        body, name=name, out_shape=tuple(_hbm(t) for t in operands),
        in_specs=[HBM_SPEC] * n_ops + [SEM_SPEC, SEM_SPEC] + [ANY_SPEC] * len(after),
        out_specs=tuple([HBM_SPEC] * n_ops), input_output_aliases={i: i for i in range(n_ops)},
        compiler_params=pltpu.CompilerParams(has_side_effects=SIDE_EFFECT),
    )(*operands, send_sems, recv_sems, *after)
    return list(res)


def _halves(make_copies, na):
    return lambda refs, send_sems, recv_sems: make_copies(refs[:na], refs[na:], send_sems, recv_sems)


def _direct_start(gather, arrays, through, name):
    na = len(arrays)
    lands = [lax.empty((N_DEV,) + t.shape if gather else (N_DEV - 1,) + t.shape[1:], t.dtype) for t in arrays]
    return _split_copy_start(_halves(functools.partial(_direct_copies, gather), na), 7 * na, [*arrays, *lands],
                             through, name)


def _direct_wait(gather, started, after, name):
    na = len(started[2]) // 2
    operands = _split_copy_wait(_halves(functools.partial(_direct_copies, gather), na), started, after, name)
    return operands[:na], operands[na:]


def _two_level_gather(shards, glue, name):
    na = len(shards)
    lands = [lax.empty((N_DEV,) + t.shape, t.dtype) for t in shards]
    nothing = jnp.zeros((SUBLANES, LANES), F32)
    started, _ = _split_copy_start(_halves(_chip_copies, na), 4 * na, [*shards, *lands], nothing, name + "_start")
    operands = _split_copy_wait(_halves(_chip_copies, na), started, glue, name + "_wait")
    started, mine = _split_copy_start(_forward_copies, 3 * na, operands[na:], operands[0], name + "_forward_start")
    lands = _split_copy_wait(_forward_copies, started, mine, name + "_forward_wait")
    return [lax.dynamic_update_index_in_dim(land, own, _my_slot(), 0)
            for own, land in zip([mine, *operands[1:na]], lands)]


def _row_tile(rows):
    for t in (256, 176, 128):
        if rows % t == 0 and rows > t:
            return t
    return rows


def _adamw_math(g, w, m, v):
    m_new = ADAM_B1 * m + (1.0 - ADAM_B1) * g
    v_new = ADAM_B2 * v + (1.0 - ADAM_B2) * jnp.square(g)
    m_hat = m_new / (1.0 - ADAM_B1 ** ADAM_STEP)
    v_hat = v_new / (1.0 - ADAM_B2 ** ADAM_STEP)
    delta = -ADAM_LR * (m_hat / (jnp.sqrt(v_hat) + ADAM_EPS) + ADAM_WD * w)
    return delta, m_new, v_new


def _adamw_sharded(chip, sums, got, w, m, v, name):
    r, c = w.shape
    tr = _row_tile(r)
    n_got = got.shape[0]

    def body(chip_ref, s_ref, g_ref, w_ref, m_ref, v_ref, go_ref, d_ref, nm_ref, nv_ref):
        g = s_ref[...].astype(F32)
        for i in range(n_got):
            g = g + g_ref[i].astype(F32)
        delta, m_new, v_new = _adamw_math(g, w_ref[...], m_ref[...], v_ref[...])
        go_ref[...] = g
        d_ref[...] = delta
        nm_ref[...] = m_new
        nv_ref[...] = v_new

    blk = pl.BlockSpec((tr, c), lambda i, chip_ref: (i, 0))
    out = jax.ShapeDtypeStruct((r, c), F32)
    return pl.pallas_call(
        body, name=name,
        grid_spec=pltpu.PrefetchScalarGridSpec(
            num_scalar_prefetch=1, grid=(r // tr,),
            in_specs=[pl.BlockSpec((None, tr, c), lambda i, chip_ref: (chip_ref[0], i, 0)),
                      pl.BlockSpec((n_got, tr, c), lambda i, chip_ref: (0, i, 0)), blk, blk, blk],
            out_specs=[blk, blk, blk, blk]),
        out_shape=[out, out, out, out],
        compiler_params=_params(("parallel",)),
    )(chip, sums, got, w, m, v)


def _adamw_replicated(parts, w, m, v):
    p, r, c = parts.shape
    names = SMALL_NAMES
    shapes = [w[n].shape for n in names]

    def body(*refs):
        p_ref = refs[0]
        ins = refs[1:1 + 3 * len(names)]
        outs = refs[1 + 3 * len(names):-2]
        loss_ref, sum_scr = refs[-2], refs[-1]
        total = p_ref[0]
        for i in range(1, p):
            total = total + p_ref[i]
        sum_scr[...] = total
        for k, n in enumerate(names):
            w_ref, m_ref, v_ref = ins[3 * k:3 * k + 3]
            g_ref, d_ref, nm_ref, nv_ref = outs[4 * k:4 * k + 4]
            for row, lane0, width, src_row in _small_pieces(n, shapes[k]):
                here = (slice(row, row + 1), slice(lane0, lane0 + width))
                g = sum_scr[src_row:src_row + 1, 0:width]
                delta, m_new, v_new = _adamw_math(g, w_ref[here], m_ref[here], v_ref[here])
                g_ref[here] = g
                d_ref[here] = delta
                nm_ref[here] = m_new
                nv_ref[here] = v_new
        loss_ref[...] = sum_scr[SMALL_LOSS_ROW:SMALL_LOSS_ROW + 1, 0:1]

    whole = lambda shape: pl.BlockSpec(shape, lambda i: (0,) * len(shape))
    args = [parts] + [t[n] for n in names for t in (w, m, v)]
    out_shape = [jax.ShapeDtypeStruct(s, F32) for s in shapes for _ in range(4)] + [jax.ShapeDtypeStruct((1, 1), F32)]
    res = pl.pallas_call(
        body, name="adamw_replicated", grid=(1,),
        in_specs=[whole(t.shape) for t in args], out_specs=[whole(s.shape) for s in out_shape],
        out_shape=out_shape, scratch_shapes=[pltpu.VMEM((r, c), F32)],
        compiler_params=_params(("arbitrary",)),
    )(*args)
    results = [{n: res[4 * k + j] for k, n in enumerate(names)} for j in range(4)]
    return results, res[-1]


SHARDED_NAMES = ("w_in", "ml_conv_w", "w_out", "ca_wq", "ca_wkv", "ca_wo", "ffn_w_up", "ffn_conv_w", "ffn_w_down")
SMALL_NAMES = ("b_in", "hg_lb_logits", "hg_norm_w", "ml_conv_b", "ml_norm_w", "ln1_g", "ln1_b",
               "ln2_g", "ln2_b", "ffn_conv_b", "ln3_g", "ln3_b")
WEIGHT_NAMES = ("w_in", "b_in", "hg_lb_logits", "hg_norm_w", "ml_conv_w", "ml_conv_b", "ml_norm_w", "w_out",
                "ln1_g", "ln1_b", "ca_wq", "ca_wkv", "ca_wo", "ln2_g", "ln2_b", "ffn_w_up", "ffn_conv_w",
                "ffn_conv_b", "ffn_w_down", "ln3_g", "ln3_b")
PAD_TO = {"ffn_w_up": UP_SHARD_P, "ffn_conv_w": UP_SHARD_P}
SMALL_ROWS = 24
SMALL_W = D_MODEL
SMALL_SHAPES = {"b_in": (1, D_IN), "hg_lb_logits": (2, D_GROUP), "hg_norm_w": (1, D_GROUP),
                "ml_conv_b": (1, 2 * D_GROUP), "ml_norm_w": (1, D_GROUP), "ln1_g": (1, D_MODEL), "ln1_b": (1, D_MODEL),
                "ln2_g": (1, D_MODEL), "ln2_b": (1, D_MODEL), "ffn_conv_b": (1, D_UP), "ln3_g": (1, D_MODEL),
                "ln3_b": (1, D_MODEL)}


def _shard_2d(name, block):
    t = block[0]
    if name in PAD_TO:
        t = jnp.pad(t, ((0, 0), (0, PAD_TO[name] - t.shape[1])))
    return t


TRANSPOSED = ("w_in", "ffn_w_up")


def _update_shard(name, block):
    if name not in TRANSPOSED:
        return _shard_2d(name, block)
    t = jnp.transpose(block[0])
    rows = PAD_TO.get(name, t.shape[0])
    return jnp.pad(t, ((0, rows - t.shape[0]), (0, 0)))


def _shard_like(name, t, like):
    if name in TRANSPOSED:
        return jnp.transpose(t[:like.shape[2]])[None]
    return t[:, :like.shape[2]][None]


def _pad_cols(t, width):
    return jnp.pad(t, ((0, 0), (0, width - t.shape[1])))


FIRST_NAMES = ("w_in", "ml_conv_w")
FFN_NAMES = ("ffn_w_up", "ffn_w_down", "ffn_conv_w")
MID_NAMES = ("ca_wo", "ca_wq", "ca_wkv", "w_out")


def _first_weights(g, small):
    w = dict(small)
    w["w_in_t"] = g["w_in"].reshape(D_IN, D_MODEL)
    w["w_in_gate_t"] = jnp.pad(w["w_in_t"][D_IN_MAIN:], ((0, LANES - (D_IN - D_IN_MAIN)), (0, 0)))
    w["b_in_main"] = small["b_in"][:, :D_IN_MAIN]
    w["b_in_gate"] = _pad_cols(small["b_in"][:, D_IN_MAIN:], LANES)
    w["ml_conv_w"] = jnp.transpose(g["ml_conv_w"], (1, 0, 2)).reshape(ML_CONV, 2 * D_GROUP)
    return w


def _mid_weights(g):
    w = {n: g[n].reshape(D_MODEL, D_MODEL) for n in ("w_out", "ca_wq", "ca_wo")}
    w["ca_wkv"] = g["ca_wkv"]
    return w


FFN_UP_NAMES = ("ffn_w_up", "ffn_conv_w")
FFN_DOWN_NAMES = ("ffn_w_down",)


def _ffn_up_weights(g, small):
    w = {"ffn_w_up_t": g["ffn_w_up"].reshape(D_UP_P, D_MODEL)}
    w["ffn_conv_w"] = jnp.transpose(g["ffn_conv_w"], (1, 0, 2)).reshape(FFN_CONV, D_UP_P)
    w["ffn_conv_b"] = _pad_cols(small["ffn_conv_b"].reshape(N_DEV, UP_SHARD), UP_SHARD_P).reshape(1, D_UP_P)
    return w


def _ffn_down_weights(g):
    down = g["ffn_w_down"].reshape(N_DEV // 2, UP_SHARD, D_MODEL)
    return {"ffn_w_down": jnp.pad(down, ((0, 0), (0, UP_SHARD_P - UP_SHARD), (0, 0))).reshape(D_FF_P, D_MODEL)}


def _whole_weights(g, small):
    return {**_first_weights(g, small), **_mid_weights(g), **_ffn_up_weights(g, small), **_ffn_down_weights(g)}


def _owner_stack(n, grads):
    if n == "w_in":
        rows = jnp.concatenate([grads["w_in_main"], grads["w_in_gate"][:D_IN - D_IN_MAIN]], axis=0)
        return rows.reshape(N_DEV, W_IN_SHARD, D_MODEL)
    if n == "ffn_w_up":
        return grads[n].reshape(N_DEV, UP_SHARD_P, D_MODEL)
    if n in ("w_out", "ca_wq", "ca_wo"):
        return grads[n].reshape(N_DEV, D_MODEL // N_DEV, D_MODEL)
    if n == "ffn_w_down":
        down = grads[n].reshape(N_DEV // 2, UP_SHARD_P, D_MODEL)[:, :UP_SHARD]
        return down.reshape(N_DEV, D_FF // N_DEV, D_MODEL)
    if n == "ml_conv_w":
        return jnp.transpose(grads[n].reshape(ML_CONV, N_DEV, LANES), (1, 0, 2))
    if n == "ffn_conv_w":
        return jnp.transpose(grads[n].reshape(FFN_CONV, N_DEV, UP_SHARD_P), (1, 0, 2))
    return grads[n]


def _owner_stacks(grads):
    return {n: _owner_stack(n, grads) for n in SHARDED_NAMES}


def _small_grads(grads):
    out = {n: grads[n] for n in SMALL_NAMES if n in grads}
    out["b_in"] = jnp.concatenate([grads["b_in_main"], grads["b_in_gate"][:, :D_IN - D_IN_MAIN]], axis=1)
    out["ffn_conv_b"] = grads["ffn_conv_b"].reshape(N_DEV, UP_SHARD_P)[:, :UP_SHARD].reshape(1, D_UP)
    return out


def _small_rows(shape):
    return shape[0] if shape[1] <= SMALL_W else -(-shape[1] // SMALL_W)


SMALL_BASE = {n: sum(_small_rows(SMALL_SHAPES[k]) for k in SMALL_NAMES[:i]) for i, n in enumerate(SMALL_NAMES)}
SMALL_LOSS_ROW = sum(_small_rows(SMALL_SHAPES[n]) for n in SMALL_NAMES)
assert SMALL_LOSS_ROW < SMALL_ROWS


def _small_pieces(name, shape):
    base = SMALL_BASE[name]
    if shape[1] <= SMALL_W:
        return [(i, 0, shape[1], base + i) for i in range(shape[0])]
    return [(0, k * SMALL_W, min(SMALL_W, shape[1] - k * SMALL_W), base + k) for k in range(_small_rows(shape))]


def _pack_small(p, loss):
    rows = []
    for n in SMALL_NAMES:
        t = p[n]
        nrows = _small_rows(t.shape)
        if t.shape[1] <= SMALL_W:
            rows.append(_pad_cols(t, SMALL_W))
        else:
            rows.append(_pad_cols(t, nrows * SMALL_W).reshape(nrows, SMALL_W))
    rows.append(_pad_cols(loss, SMALL_W))
    slab = jnp.concatenate(rows, axis=0)
    return jnp.pad(slab, ((0, SMALL_ROWS - slab.shape[0]), (0, 0)))


def kernel(x, mem, w_in, b_in, hg_lb_logits, hg_norm_w, ml_conv_w, ml_conv_b, ml_norm_w, w_out, ln1_g, ln1_b, ca_wq, ca_wkv, ca_wo, ln2_g, ln2_b, ffn_w_up, ffn_conv_w, ffn_conv_b, ffn_w_down, ln3_g, ln3_b, loss_target, m_w_in, m_b_in, m_hg_lb_logits, m_hg_norm_w, m_ml_conv_w, m_ml_conv_b, m_ml_norm_w, m_w_out, m_ln1_g, m_ln1_b, m_ca_wq, m_ca_wkv, m_ca_wo, m_ln2_g, m_ln2_b, m_ffn_w_up, m_ffn_conv_w, m_ffn_conv_b, m_ffn_w_down, m_ln3_g, m_ln3_b, v_w_in, v_b_in, v_hg_lb_logits, v_hg_norm_w, v_ml_conv_w, v_ml_conv_b, v_ml_norm_w, v_w_out, v_ln1_g, v_ln1_b, v_ca_wq, v_ca_wkv, v_ca_wo, v_ln2_g, v_ln2_b, v_ffn_w_up, v_ffn_conv_w, v_ffn_conv_b, v_ffn_w_down, v_ln3_g, v_ln3_b):
    params = dict(w_in=w_in, b_in=b_in, hg_lb_logits=hg_lb_logits, hg_norm_w=hg_norm_w, ml_conv_w=ml_conv_w,
                  ml_conv_b=ml_conv_b, ml_norm_w=ml_norm_w, w_out=w_out, ln1_g=ln1_g, ln1_b=ln1_b, ca_wq=ca_wq,
                  ca_wkv=ca_wkv, ca_wo=ca_wo, ln2_g=ln2_g, ln2_b=ln2_b, ffn_w_up=ffn_w_up, ffn_conv_w=ffn_conv_w,
                  ffn_conv_b=ffn_conv_b, ffn_w_down=ffn_w_down, ln3_g=ln3_g, ln3_b=ln3_b)
    mom1 = dict(w_in=m_w_in, b_in=m_b_in, hg_lb_logits=m_hg_lb_logits, hg_norm_w=m_hg_norm_w,
                ml_conv_w=m_ml_conv_w, ml_conv_b=m_ml_conv_b, ml_norm_w=m_ml_norm_w, w_out=m_w_out, ln1_g=m_ln1_g,
                ln1_b=m_ln1_b, ca_wq=m_ca_wq, ca_wkv=m_ca_wkv, ca_wo=m_ca_wo, ln2_g=m_ln2_g, ln2_b=m_ln2_b,
                ffn_w_up=m_ffn_w_up, ffn_conv_w=m_ffn_conv_w, ffn_conv_b=m_ffn_conv_b, ffn_w_down=m_ffn_w_down,
                ln3_g=m_ln3_g, ln3_b=m_ln3_b)
    mom2 = dict(w_in=v_w_in, b_in=v_b_in, hg_lb_logits=v_hg_lb_logits, hg_norm_w=v_hg_norm_w,
                ml_conv_w=v_ml_conv_w, ml_conv_b=v_ml_conv_b, ml_norm_w=v_ml_norm_w, w_out=v_w_out, ln1_g=v_ln1_g,
                ln1_b=v_ln1_b, ca_wq=v_ca_wq, ca_wkv=v_ca_wkv, ca_wo=v_ca_wo, ln2_g=v_ln2_g, ln2_b=v_ln2_b,
                ffn_w_up=v_ffn_w_up, ffn_conv_w=v_ffn_conv_w, ffn_conv_b=v_ffn_conv_b, ffn_w_down=v_ffn_w_down,
                ln3_g=v_ln3_g, ln3_b=v_ln3_b)

    x_idx, y_idx, c_idx = _coords()
    as_index = lambda v: jnp.reshape(v, (1,)).astype(jnp.int32)
    me = as_index(4 * x_idx + 2 * y_idx + c_idx)
    small_params = {n: params[n] for n in SMALL_NAMES}

    shards = {n: _update_shard(n, params[n]) for n in SHARDED_NAMES}
    m_shards = {n: _update_shard(n, mom1[n]) for n in SHARDED_NAMES}
    v_shards = {n: _update_shard(n, mom2[n]) for n in SHARDED_NAMES}
    outgoing = {n: shards[n] if "conv" in n else shards[n].astype(BF16) for n in SHARDED_NAMES}
    to_send = lambda names: [outgoing[n] for n in names]
    glue = [*m_shards.values(), *v_shards.values(), *shards.values(),
            *[outgoing[n] for n in SHARDED_NAMES if n not in FIRST_NAMES]]
    first = dict(zip(FIRST_NAMES, _two_level_gather(to_send(FIRST_NAMES), glue, "weights_gather_first")))
    mid_started, through = _direct_start(True, to_send(MID_NAMES), first["w_in"], "weights_gather_start_mid")
    ffn_started, through = _direct_start(True, to_send(FFN_UP_NAMES), through, "weights_gather_start_ffn_up")
    down_started, first["w_in"] = _direct_start(True, to_send(FFN_DOWN_NAMES), through,
                                                "weights_gather_start_ffn_down")

    def gathered_weights(names, started, after, tag):
        mine, lands = _direct_wait(True, started, after, "weights_gather_wait_" + tag)
        return {n: lax.dynamic_update_index_in_dim(land, own, me[0], 0) for n, own, land in zip(names, mine, lands)}

    started, own_stacks = {}, {}

    def start_group(names, tag):
        def hook(grads, through):
            own_stacks[tag] = [_owner_stack(n, grads).astype(BF16) for n in names]
            started[tag], through = _direct_start(False, own_stacks[tag], through, "grads_start_" + tag)
            return through
        return hook

    def start_small(grads, loss, through):
        started["small"], through = _direct_start(True, [_pack_small(_small_grads(grads), loss)], through,
                                                  "small_gather_start")
        return through

    loss, grad_x, grads = _local_step(
        x[0], mem[0], loss_target[0], _first_weights(first, small_params),
        lambda y: _mid_weights(gathered_weights(MID_NAMES, mid_started, y, "mid")),
        lambda x2: _ffn_up_weights(gathered_weights(FFN_UP_NAMES, ffn_started, x2, "ffn_up"), small_params),
        lambda hid: _ffn_down_weights(gathered_weights(FFN_DOWN_NAMES, down_started, hid, "ffn_down")),
        start_group(FFN_NAMES, "ffn"), start_group(MID_NAMES, "mid"), start_small, start_group(FIRST_NAMES, "last"))

    sharded_out = {}

    def update_group(names, tag, after):
        _, lands = _direct_wait(False, started[tag], after, "grads_wait_" + tag)
        for n, st, land in zip(names, own_stacks[tag], lands):
            res = _adamw_sharded(me, st, land, shards[n], m_shards[n], v_shards[n], "adamw_" + n)
            sharded_out[n] = [_shard_like(n, t, params[n]) for t in res]

    update_group(FFN_NAMES, "ffn", grad_x)
    update_group(MID_NAMES, "mid", grad_x)
    own_small, small_lands = _direct_wait(True, started["small"], grad_x, "small_gather_wait")
    small_parts = lax.dynamic_update_index_in_dim(small_lands[0], own_small[0], me[0], 0)
    small_out, total_loss = _adamw_replicated(small_parts, small_params, {n: mom1[n] for n in SMALL_NAMES},
                                              {n: mom2[n] for n in SMALL_NAMES})
    done = [t for n in FFN_NAMES + MID_NAMES for t in sharded_out[n]]
    done += [t for small in small_out for t in small.values()]
    update_group(FIRST_NAMES, "last", done)

    outs = []
    for k, small in enumerate(small_out):
        outs.extend(sharded_out[n][k] if n in sharded_out else small[n] for n in WEIGHT_NAMES)
    return (total_loss[0, 0], grad_x[None], *outs)
```

```python
import functools
import math

import jax
import jax.numpy as jnp
from jax import lax
from jax.experimental import pallas as pl
from jax.experimental.pallas import tpu as pltpu

F32 = jnp.float32
BF16 = jnp.bfloat16
HIGHEST = lax.Precision.HIGHEST
MESH = pl.DeviceIdType.MESH

N_DEV = 8
D_MODEL = 1024
N_MEM = 256
N_HEADS = 4
D_HEAD = 128
D_GROUP = N_HEADS * D_HEAD
CHUNK = 64
ML_CONV = 4
FFN_CONV = 3
D_FF = 2816
D_UP = 2 * D_FF
CA_HEADS = 4
CA_DH = D_MODEL // CA_HEADS
LANES = 128
SUBLANES = 8
D_IN = 8 * D_GROUP + 2 * N_HEADS
D_IN_MAIN = 8 * D_GROUP
W_IN_SHARD = D_IN // N_DEV
UP_SHARD = D_UP // N_DEV
UP_SHARD_P = 768
D_UP_P = N_DEV * UP_SHARD_P
D_FF_P = D_UP_P // 2
ALPHA = 2.0 ** 0.25
LN_EPS = 1e-5
NEG_BIG = -1e30
ADAM_LR = 0.001
ADAM_B1 = 0.9
ADAM_B2 = 0.999
ADAM_EPS = 1e-08
ADAM_WD = 0.01
ADAM_STEP = 10
VMEM_LIMIT = 56 * 1024 * 1024

SEG_MQ = 4 * D_GROUP // LANES
VO_BLOCK = 3


def _params(sem):
    return pltpu.CompilerParams(dimension_semantics=sem, vmem_limit_bytes=VMEM_LIMIT)


def _dg(a, b, ca, cb, precision=None):
    return lax.dot_general(a, b, (((ca,), (cb,)), ((), ())), precision=precision,
                           preferred_element_type=F32)


def _nn_raw(a, b):
    return _dg(a.astype(BF16), b.astype(BF16), 1, 0)


def _nt_raw(a, b):
    return _dg(a.astype(BF16), b.astype(BF16), 1, 1)


def _tn_raw(a, b):
    return _dg(a.astype(BF16), b.astype(BF16), 0, 0)


@jax.custom_vjp
def _nn(a, b):
    return _nn_raw(a, b)


_nn.defvjp(lambda a, b: (_nn_raw(a, b), (a, b)),
           lambda res, g: (_nt_raw(g, res[1]), _tn_raw(res[0], g)))


@jax.custom_vjp
def _nt(a, b):
    return _nt_raw(a, b)


_nt.defvjp(lambda a, b: (_nt_raw(a, b), (a, b)),
           lambda res, g: (_nn_raw(g, res[1]), _tn_raw(g, res[0])))


@jax.custom_vjp
def _tn(a, b):
    return _tn_raw(a, b)


_tn.defvjp(lambda a, b: (_tn_raw(a, b), (a, b)),
           lambda res, g: (_nt_raw(res[1], g), _nn_raw(res[0], g)))


def _layer_norm(z, g, b):
    mu = jnp.mean(z, axis=-1, keepdims=True)
    var = jnp.mean(jnp.square(z - mu), axis=-1, keepdims=True)
    return (z - mu) * lax.rsqrt(var + LN_EPS) * g + b


def _matmul_nn(a, w, bias, tm, tn, name, out_dtype=F32):
    m, k = a.shape
    if w.ndim == 3:
        n = w.shape[0] * w.shape[2]
        assert tn == w.shape[2]
        w_spec = pl.BlockSpec((None, k, tn), lambda i, j: (j, 0, 0))
    else:
        n = w.shape[1]
        w_spec = pl.BlockSpec((k, tn), lambda i, j: (0, j))

    def body(*refs):
        a_ref, w_ref = refs[0], refs[1]
        o_ref = refs[-1]
        acc = _nn_raw(a_ref[...], w_ref[...])
        if bias is not None:
            acc = acc + refs[2][...]
        o_ref[...] = acc.astype(o_ref.dtype)

    in_specs = [pl.BlockSpec((tm, k), lambda i, j: (i, 0)), w_spec]
    args = [a, w]
    if bias is not None:
        in_specs.append(pl.BlockSpec((1, tn), lambda i, j: (0, j)))
        args.append(bias)
    return pl.pallas_call(
        body, name=name, grid=(m // tm, n // tn), in_specs=in_specs,
        out_specs=pl.BlockSpec((tm, tn), lambda i, j: (i, j)),
        out_shape=jax.ShapeDtypeStruct((m, n), out_dtype),
        compiler_params=_params(("parallel", "parallel")),
    )(*args)


def _matmul_nt(d, w, tm, tk, name, k_out=None, bias=None, out_dtype=F32):
    m, n = d.shape
    k = k_out or w.shape[0]

    def body(*refs):
        acc = _nt_raw(refs[0][...], refs[1][...])
        if bias is not None:
            acc = acc + refs[2][...]
        refs[-1][...] = acc.astype(refs[-1].dtype)

    in_specs = [pl.BlockSpec((tm, n), lambda i, j: (i, 0)), pl.BlockSpec((tk, n), lambda i, j: (j, 0))]
    args = [d, w]
    if bias is not None:
        in_specs.append(pl.BlockSpec((1, tk), lambda i, j: (0, j)))
        args.append(bias)
    return pl.pallas_call(
        body, name=name, grid=(m // tm, k // tk), in_specs=in_specs,
        out_specs=pl.BlockSpec((tm, tk), lambda i, j: (i, j)),
        out_shape=jax.ShapeDtypeStruct((m, k), out_dtype),
        compiler_params=_params(("parallel", "parallel")),
    )(*args)


def _input_projection(x, w_t, w_gate_t, b_main, b_gate, tm, tk):
    m, n = x.shape

    def body(x_ref, w_ref, wg_ref, b_ref, bg_ref, o_ref, g_ref):
        lhs = x_ref[...].astype(BF16)
        o_ref[...] = _nt_raw(lhs, w_ref[...]) + b_ref[...]

        @pl.when(pl.program_id(1) == 0)
        def _():
            g_ref[...] = _nt_raw(lhs, wg_ref[...]) + bg_ref[...]

    return pl.pallas_call(
        body, name="proj", grid=(m // tm, D_IN_MAIN // tk),
        in_specs=[pl.BlockSpec((tm, n), lambda i, j: (i, 0)), pl.BlockSpec((tk, n), lambda i, j: (j, 0)),
                  pl.BlockSpec((LANES, n), lambda i, j: (0, 0)), pl.BlockSpec((1, tk), lambda i, j: (0, j)),
                  pl.BlockSpec((1, LANES), lambda i, j: (0, 0))],
        out_specs=[pl.BlockSpec((tm, tk), lambda i, j: (i, j)), pl.BlockSpec((tm, LANES), lambda i, j: (i, 0))],
        out_shape=[jax.ShapeDtypeStruct((m, D_IN_MAIN), F32), jax.ShapeDtypeStruct((m, LANES), F32)],
        compiler_params=_params(("parallel", "arbitrary")),
    )(x, w_t, w_gate_t, b_main, b_gate)


def _input_projection_grads(d_proj, d_gates, x, tm, tt):
    t, m = d_proj.shape
    n = x.shape[1]
    last = t // tt - 1

    def body(a_ref, g_ref, x_ref, o_ref, og_ref, acc_ref, accg_ref):
        i, kk = pl.program_id(0), pl.program_id(1)

        @pl.when(kk == 0)
        def _():
            acc_ref[...] = jnp.zeros_like(acc_ref)

        @pl.when((kk == 0) & (i == 0))
        def _():
            accg_ref[...] = jnp.zeros_like(accg_ref)

        rhs = x_ref[...].astype(BF16)
        acc_ref[...] += _tn_raw(a_ref[...], rhs)

        @pl.when(i == 0)
        def _():
            accg_ref[...] += _tn_raw(g_ref[...], rhs)

        @pl.when(kk == last)
        def _():
            o_ref[...] = acc_ref[...].astype(o_ref.dtype)

        @pl.when((kk == last) & (i == 0))
        def _():
            og_ref[...] = accg_ref[...].astype(og_ref.dtype)

    return pl.pallas_call(
        body, name="d_w_in", grid=(m // tm, t // tt),
        in_specs=[pl.BlockSpec((tt, tm), lambda i, kk: (kk, i)), pl.BlockSpec((tt, LANES), lambda i, kk: (kk, 0)),
                  pl.BlockSpec((tt, n), lambda i, kk: (kk, 0))],
        out_specs=[pl.BlockSpec((tm, n), lambda i, kk: (i, 0)), pl.BlockSpec((LANES, n), lambda i, kk: (0, 0))],
        out_shape=[jax.ShapeDtypeStruct((m, n), BF16), jax.ShapeDtypeStruct((LANES, n), BF16)],
        scratch_shapes=[pltpu.VMEM((tm, n), F32), pltpu.VMEM((LANES, n), F32)],
        compiler_params=_params(("arbitrary", "arbitrary")),
    )(d_proj, d_gates, x)


def _matmul_nn_sum(pairs, add, scale, tm, name):
    m = pairs[0][0].shape[0]
    n = pairs[0][1].shape[1]
    in_specs, args = [], []
    for a, w, row0 in pairs:
        kk = a.shape[1]
        in_specs += [pl.BlockSpec((tm, kk), lambda i: (i, 0)),
                     pl.BlockSpec((kk, n), lambda i, blk=row0 // kk: (blk, 0))]
        args += [a, w]
    if add is not None:
        in_specs.append(pl.BlockSpec((tm, n), lambda i: (i, 0)))
        args.append(add)

    def body(*refs):
        acc = None
        for p in range(len(pairs)):
            term = _nn_raw(refs[2 * p][...], refs[2 * p + 1][...])
            acc = term if acc is None else acc + term
        if add is not None:
            acc = acc + scale * refs[2 * len(pairs)][...]
        refs[-1][...] = acc

    return pl.pallas_call(
        body, name=name, grid=(m // tm,), in_specs=in_specs,
        out_specs=pl.BlockSpec((tm, n), lambda i: (i, 0)),
        out_shape=jax.ShapeDtypeStruct((m, n), F32),
        compiler_params=_params(("parallel",)),
    )(*args)


def _matmul_tn(a, b, tm, tn, tt, name, shards=None, shard0=0, group=1, into=None, colsum=False, rows=None, row0=0):
    t, m = a.shape
    n = b.shape[1]
    assert not colsum or tm == m
    n_in = 2 + (into is not None)
    out_dtype = BF16
    per_step = 1 if shards is None else group
    width = per_step * tn

    def body(*refs):
        a_ref, b_ref = refs[0], refs[1]
        o_ref, acc_ref = refs[n_in], refs[-1]
        first = pl.program_id(2) == 0

        @pl.when(first)
        def _():
            acc_ref[...] = jnp.zeros_like(acc_ref)

        if shards is None:
            acc_ref[...] += _tn_raw(a_ref[...], b_ref[...])
        else:
            lhs = a_ref[...].astype(BF16)
            for g in range(per_step):
                acc_ref[g] += _tn_raw(lhs, b_ref[:, g * tn:(g + 1) * tn])

        @pl.when(pl.program_id(2) == t // tt - 1)
        def _():
            o_ref[...] = acc_ref[...].astype(o_ref.dtype)

        if colsum:
            s_ref = refs[n_in + 1]

            @pl.when(first)
            def _():
                s_ref[...] = jnp.zeros_like(s_ref)

            s_ref[...] += jnp.sum(b_ref[...], axis=0, keepdims=True)

    in_specs = [pl.BlockSpec((tt, tm), lambda i, j, kk: (kk, i)),
                pl.BlockSpec((tt, width), lambda i, j, kk: (kk, j))]
    args = [a, b]
    aliases = {}
    if into is not None:
        in_specs.append(pl.BlockSpec(memory_space=pl.ANY))
        args.append(into)
        aliases = {2: 0}
    if shards is None:
        out_specs = [pl.BlockSpec((tm, tn), lambda i, j, kk: (row0 // tm + i, j))]
        out_shape = [jax.ShapeDtypeStruct((rows or m, n), out_dtype)]
        acc = pltpu.VMEM((tm, tn), F32)
    else:
        out_specs = [pl.BlockSpec((per_step, tm, tn), lambda i, j, kk: (shard0 // per_step + j, i, 0))]
        out_shape = [jax.ShapeDtypeStruct((shards, m, tn), out_dtype)]
        acc = pltpu.VMEM((per_step, tm, tn), F32)
    if colsum:
        out_specs.append(pl.BlockSpec((1, tn), lambda i, j, kk: (0, j)))
        out_shape.append(jax.ShapeDtypeStruct((1, n), F32))
    res = pl.pallas_call(
        body, name=name, grid=(m // tm, n // width, t // tt), in_specs=in_specs, out_specs=out_specs,
        out_shape=out_shape, input_output_aliases=aliases, scratch_shapes=[acc],
        compiler_params=_params(("parallel", "parallel", "arbitrary")),
    )(*args)
    return res if colsum else res[0]


ROW_TILE = 64


def _stack(ref, start, rows):
    return ref[pl.ds(start, rows), :].astype(F32).reshape(rows // SUBLANES, SUBLANES, LANES)


def _vreg_rows(ref, n):
    return [jnp.broadcast_to(ref[j:j + 1, :], (SUBLANES, LANES))[None] for j in range(n)]


def _column_total(acc):
    return jnp.sum(acc, axis=0, keepdims=True)


def _conv_fwd_tile(pad_ref, taps_w, bias, r0, rows):
    taps = len(taps_w)
    acc = bias
    for j in range(taps):
        acc = acc + _stack(pad_ref, SUBLANES - (taps - 1 - j) + r0, rows) * taps_w[j]
    return acc


def _conv_grads_tile(pad_ref, dpad_ref, dx_ref, taps_w, dws, r0, rows):
    taps = len(taps_w)
    x_rows = _stack(pad_ref, SUBLANES + r0, rows)
    dx = None
    for j in range(taps):
        d_shifted = _stack(dpad_ref, r0 + (taps - 1 - j), rows)
        term = d_shifted * taps_w[j]
        dx = term if dx is None else dx + term
        dws[j] = dws[j] + jnp.sum(d_shifted * x_rows, axis=0)
    dx_ref[r0:r0 + rows, :] = dx.reshape(rows, LANES).astype(dx_ref.dtype)
    return jnp.sum(dx, axis=0)


def _ml_conv_fwd(proj, conv_w, conv_b):
    s = proj.shape[0]
    nblk = 2 * D_GROUP // LANES

    def body(x_ref, w_ref, b_ref, o_ref, pad_ref):
        pad_ref[0:SUBLANES, :] = jnp.zeros((SUBLANES, LANES), F32)
        pad_ref[SUBLANES:, :] = x_ref[...].astype(F32)
        taps_w, bias = _vreg_rows(w_ref, ML_CONV), _vreg_rows(b_ref, 1)[0]
        for r0 in range(0, s, ROW_TILE):
            rows = min(ROW_TILE, s - r0)
            o_ref[r0:r0 + rows, :] = jax.nn.silu(_conv_fwd_tile(pad_ref, taps_w, bias, r0, rows)).reshape(rows, LANES)

    return pl.pallas_call(
        body, name="ml_conv_fwd", grid=(nblk,),
        in_specs=[pl.BlockSpec((s, LANES), lambda j: (0, SEG_MQ + j)),
                  pl.BlockSpec((ML_CONV, LANES), lambda j: (0, j)),
                  pl.BlockSpec((1, LANES), lambda j: (0, j))],
        out_specs=pl.BlockSpec((s, LANES), lambda j: (0, j)),
        out_shape=jax.ShapeDtypeStruct((s, 2 * D_GROUP), F32),
        scratch_shapes=[pltpu.VMEM((s + SUBLANES, LANES), F32)],
        compiler_params=_params(("parallel",)),
    )(proj, conv_w, conv_b)


def _ml_conv_bwd(proj, conv_w, conv_b, d_qk, d_proj):
    s = proj.shape[0]
    nblk = 2 * D_GROUP // LANES

    def body(x_ref, w_ref, b_ref, dy_ref, _, dx_ref, dw_ref, db_ref, dxs_ref, pad_ref, dpad_ref):
        pad_ref[0:SUBLANES, :] = jnp.zeros((SUBLANES, LANES), F32)
        pad_ref[SUBLANES:, :] = x_ref[...].astype(F32)
        dpad_ref[s:, :] = jnp.zeros((SUBLANES, LANES), F32)
        taps_w, bias = _vreg_rows(w_ref, ML_CONV), _vreg_rows(b_ref, 1)[0]
        db = jnp.zeros((SUBLANES, LANES), F32)
        for r0 in range(0, s, ROW_TILE):
            rows = min(ROW_TILE, s - r0)
            pre = _conv_fwd_tile(pad_ref, taps_w, bias, r0, rows)
            _, vjp = jax.vjp(jax.nn.silu, pre)
            d_pre, = vjp(_stack(dy_ref, r0, rows))
            dpad_ref[r0:r0 + rows, :] = d_pre.reshape(rows, LANES)
            db = db + jnp.sum(d_pre, axis=0)
        db_ref[...] = _column_total(db)
        dws = [jnp.zeros((SUBLANES, LANES), F32) for _ in range(ML_CONV)]
        dx_sum = jnp.zeros((SUBLANES, LANES), F32)
        for r0 in range(0, s, ROW_TILE):
            dx_sum = dx_sum + _conv_grads_tile(pad_ref, dpad_ref, dx_ref, taps_w, dws, r0, min(ROW_TILE, s - r0))
        dxs_ref[...] = _column_total(dx_sum)
        for j in range(ML_CONV):
            dw_ref[j:j + 1, :] = _column_total(dws[j])

    return pl.pallas_call(
        body, name="ml_conv_bwd", grid=(nblk,),
        in_specs=[pl.BlockSpec((s, LANES), lambda j: (0, SEG_MQ + j)),
                  pl.BlockSpec((ML_CONV, LANES), lambda j: (0, j)),
                  pl.BlockSpec((1, LANES), lambda j: (0, j)),
                  pl.BlockSpec((s, LANES), lambda j: (0, j)),
                  pl.BlockSpec(memory_space=pl.ANY)],
        out_specs=[pl.BlockSpec((s, LANES), lambda j: (0, SEG_MQ + j)),
                   pl.BlockSpec((ML_CONV, LANES), lambda j: (0, j)),
                   pl.BlockSpec((1, LANES), lambda j: (0, j)),
                   pl.BlockSpec((1, LANES), lambda j: (0, j))],
        out_shape=[jax.ShapeDtypeStruct(d_proj.shape, d_proj.dtype),
                   jax.ShapeDtypeStruct((ML_CONV, 2 * D_GROUP), F32),
                   jax.ShapeDtypeStruct((1, 2 * D_GROUP), F32),
                   jax.ShapeDtypeStruct((1, 2 * D_GROUP), F32)],
        input_output_aliases={4: 0},
        scratch_shapes=[pltpu.VMEM((s + SUBLANES, LANES), F32), pltpu.VMEM((s + SUBLANES, LANES), F32)],
        compiler_params=_params(("parallel",)),
    )(proj, conv_w, conv_b, d_qk, d_proj)


def _gelu_mul(a, b):
    return jax.nn.gelu(a) * b


GELU_C = math.sqrt(2.0 / math.pi)
GELU_K = 0.044715


def _gelu_mul_grads(a, b, d):
    a2 = a * a
    t = jnp.tanh(GELU_C * (a + GELU_K * (a * a2)))
    cdf = 0.5 * (1.0 + t)
    slope = cdf + (0.5 * GELU_C) * a * (1.0 - t * t) * (1.0 + (3.0 * GELU_K) * a2)
    return d * b * slope, d * (a * cdf)


FFN_BLOCKS = D_FF_P // LANES


def _ffn_conv_fwd(u, conv_w, conv_b):
    s = u.shape[0]

    def body(g_ref, v_ref, wg_ref, wv_ref, bg_ref, bv_ref, o_ref, gpad_ref, vpad_ref):
        for pad_ref, x_ref in ((gpad_ref, g_ref), (vpad_ref, v_ref)):
            pad_ref[0:SUBLANES, :] = jnp.zeros((SUBLANES, LANES), F32)
            pad_ref[SUBLANES:, :] = x_ref[...].astype(F32)
        taps_g, bias_g = _vreg_rows(wg_ref, FFN_CONV), _vreg_rows(bg_ref, 1)[0]
        taps_v, bias_v = _vreg_rows(wv_ref, FFN_CONV), _vreg_rows(bv_ref, 1)[0]
        for r0 in range(0, s, ROW_TILE):
            rows = min(ROW_TILE, s - r0)
            ug = _conv_fwd_tile(gpad_ref, taps_g, bias_g, r0, rows)
            uv = _conv_fwd_tile(vpad_ref, taps_v, bias_v, r0, rows)
            o_ref[r0:r0 + rows, :] = _gelu_mul(ug, uv).reshape(rows, LANES).astype(o_ref.dtype)

    col = lambda off: (lambda j: (0, off + j))
    return pl.pallas_call(
        body, name="ffn_conv_fwd", grid=(FFN_BLOCKS,),
        in_specs=[pl.BlockSpec((s, LANES), col(0)), pl.BlockSpec((s, LANES), col(FFN_BLOCKS)),
                  pl.BlockSpec((FFN_CONV, LANES), col(0)), pl.BlockSpec((FFN_CONV, LANES), col(FFN_BLOCKS)),
                  pl.BlockSpec((1, LANES), col(0)), pl.BlockSpec((1, LANES), col(FFN_BLOCKS))],
        out_specs=pl.BlockSpec((s, LANES), col(0)),
        out_shape=jax.ShapeDtypeStruct((s, D_FF_P), BF16),
        scratch_shapes=[pltpu.VMEM((s + SUBLANES, LANES), F32), pltpu.VMEM((s + SUBLANES, LANES), F32)],
        compiler_params=_params(("parallel",)),
    )(u, u, conv_w, conv_w, conv_b, conv_b)


def _ffn_conv_bwd(u, conv_w, conv_b, d_h):
    s = u.shape[0]

    def body(g_ref, v_ref, wg_ref, wv_ref, bg_ref, bv_ref, dh_ref,
             dug_ref, duv_ref, dwg_ref, dwv_ref, dbg_ref, dbv_ref,
             gpad_ref, vpad_ref, dgpad_ref, dvpad_ref):
        for pad_ref, x_ref in ((gpad_ref, g_ref), (vpad_ref, v_ref)):
            pad_ref[0:SUBLANES, :] = jnp.zeros((SUBLANES, LANES), F32)
            pad_ref[SUBLANES:, :] = x_ref[...].astype(F32)
        dgpad_ref[s:, :] = jnp.zeros((SUBLANES, LANES), F32)
        dvpad_ref[s:, :] = jnp.zeros((SUBLANES, LANES), F32)
        taps_g, bias_g = _vreg_rows(wg_ref, FFN_CONV), _vreg_rows(bg_ref, 1)[0]
        taps_v, bias_v = _vreg_rows(wv_ref, FFN_CONV), _vreg_rows(bv_ref, 1)[0]
        dbg = jnp.zeros((SUBLANES, LANES), F32)
        dbv = jnp.zeros((SUBLANES, LANES), F32)
        for r0 in range(0, s, ROW_TILE):
            rows = min(ROW_TILE, s - r0)
            ug = _conv_fwd_tile(gpad_ref, taps_g, bias_g, r0, rows)
            uv = _conv_fwd_tile(vpad_ref, taps_v, bias_v, r0, rows)
            d_ug, d_uv = _gelu_mul_grads(ug, uv, _stack(dh_ref, r0, rows))
            dgpad_ref[r0:r0 + rows, :] = d_ug.reshape(rows, LANES)
            dvpad_ref[r0:r0 + rows, :] = d_uv.reshape(rows, LANES)
            dbg = dbg + jnp.sum(d_ug, axis=0)
            dbv = dbv + jnp.sum(d_uv, axis=0)
        dbg_ref[...] = _column_total(dbg)
        dbv_ref[...] = _column_total(dbv)
        for pad_ref, dpad_ref, taps_w, dx_ref, dw_ref in ((gpad_ref, dgpad_ref, taps_g, dug_ref, dwg_ref),
                                                          (vpad_ref, dvpad_ref, taps_v, duv_ref, dwv_ref)):
            dws = [jnp.zeros((SUBLANES, LANES), F32) for _ in range(FFN_CONV)]
            for r0 in range(0, s, ROW_TILE):
                _conv_grads_tile(pad_ref, dpad_ref, dx_ref, taps_w, dws, r0, min(ROW_TILE, s - r0))
            for j in range(FFN_CONV):
                dw_ref[j:j + 1, :] = _column_total(dws[j])

    col = lambda off: (lambda j: (0, off + j))
    seq = pl.BlockSpec((s, LANES), col(0))
    return pl.pallas_call(
        body, name="ffn_conv_bwd", grid=(FFN_BLOCKS,),
        in_specs=[pl.BlockSpec((s, LANES), col(0)), pl.BlockSpec((s, LANES), col(FFN_BLOCKS)),
                  pl.BlockSpec((FFN_CONV, LANES), col(0)), pl.BlockSpec((FFN_CONV, LANES), col(FFN_BLOCKS)),
                  pl.BlockSpec((1, LANES), col(0)), pl.BlockSpec((1, LANES), col(FFN_BLOCKS)), seq],
        out_specs=[seq, seq, pl.BlockSpec((FFN_CONV, LANES), col(0)), pl.BlockSpec((FFN_CONV, LANES), col(0)),
                   pl.BlockSpec((1, LANES), col(0)), pl.BlockSpec((1, LANES), col(0))],
        out_shape=[jax.ShapeDtypeStruct((s, D_FF_P), BF16), jax.ShapeDtypeStruct((s, D_FF_P), BF16),
                   jax.ShapeDtypeStruct((FFN_CONV, D_FF_P), F32), jax.ShapeDtypeStruct((FFN_CONV, D_FF_P), F32),
                   jax.ShapeDtypeStruct((1, D_FF_P), F32), jax.ShapeDtypeStruct((1, D_FF_P), F32)],
        scratch_shapes=[pltpu.VMEM((s + SUBLANES, LANES), F32) for _ in range(4)],
        compiler_params=_params(("parallel",)),
    )(u, u, conv_w, conv_w, conv_b, conv_b, d_h)


def _chunk_masks(c):
    row = lax.broadcasted_iota(jnp.int32, (c, c), 0)
    col = lax.broadcasted_iota(jnp.int32, (c, c), 1)
    return row, col


@jax.custom_vjp
def _split_heads(x):
    return tuple(x[:, h * D_HEAD:(h + 1) * D_HEAD] for h in range(N_HEADS))


_split_heads.defvjp(lambda x: (_split_heads(x), None), lambda _, gs: (jnp.concatenate(gs, axis=1),))


@jax.custom_vjp
def _merge_heads(xs):
    return jnp.concatenate(xs, axis=1)


_merge_heads.defvjp(lambda xs: (_merge_heads(xs), None), lambda _, g: (_split_heads(g),))


@jax.custom_vjp
def _split_chunks(x):
    return tuple(x[i * CHUNK:(i + 1) * CHUNK] for i in range(x.shape[0] // CHUNK))


_split_chunks.defvjp(lambda x: (_split_chunks(x), None), lambda _, gs: (jnp.concatenate(gs, axis=0),))


@jax.custom_vjp
def _merge_chunks(xs):
    return jnp.concatenate(xs, axis=0)


_merge_chunks.defvjp(lambda xs: (_merge_chunks(xs), None), lambda _, g: (_split_chunks(g),))


def _blocks(x):
    return [_split_heads(rows) for rows in _split_chunks(x)]


def _per_chunk_rows(per_chunk, rid):
    out = per_chunk[0]
    for i in range(1, len(per_chunk)):
        out = jnp.where(rid >= i * CHUNK, per_chunk[i], out)
    return out


HEADS = range(N_HEADS)
CHUNKS_PER_STEP = 8
ML_CHUNKS_PER_STEP = 1


def _hg_chunk(hq, hf, hi, hgate, l0, l1, nw, sts):
    n = hq.shape[0] // CHUNK
    causal = _chunk_masks(CHUNK)
    causal = causal[1] <= causal[0]
    mx = lax.stop_gradient(jnp.maximum(l0, l1))
    e0 = jnp.exp(l0 - mx)
    e1 = jnp.exp(l1 - mx)
    lb = e0 / (e0 + e1)
    sig = jax.nn.sigmoid(hf)
    lf = jnp.log(lb + (1.0 - lb) * sig)
    k = (1.0 - lb) * jax.nn.sigmoid(-hf)
    q = jax.nn.silu(hq)
    tri = causal.astype(F32)
    b = _merge_chunks(tuple(_dg(tri, rows, 1, 0, HIGHEST) for rows in _split_chunks(lf)))
    rid = lax.broadcasted_iota(jnp.int32, b.shape, 0)
    pick = lambda r: jnp.sum(jnp.where(rid == r, b, 0.0), axis=0, keepdims=True)
    b_last_c = [pick(i * CHUNK + CHUNK - 1) for i in range(n)]
    b_ref = _per_chunk_rows([pick(i * CHUNK + CHUNK // 2 - 1) for i in range(n)], rid)
    b_last = _per_chunk_rows(b_last_c, rid)
    qa = _blocks(q * jnp.exp(b - b_ref))
    ka = _blocks(k * jnp.exp(b_ref - b))
    qe = _blocks(q * jnp.exp(b))
    kd = _blocks(k * jnp.exp(b_last - b))
    decay = [_split_heads(jnp.exp(b_last_c[i])) for i in range(n)]
    v = _blocks(hi)
    chunks = range(n)
    attn = [[jnp.where(causal, _nt(qa[i][h], ka[i][h]), 0.0) for h in HEADS] for i in chunks]
    intra = [[_nn(attn[i][h], v[i][h]) for h in HEADS] for i in chunks]
    kv = [[_tn(v[i][h], kd[i][h]) for h in HEADS] for i in chunks]
    normed = []
    for i in chunks:
        inter = [_nt(qe[i][h], sts[h]) for h in HEADS]
        sts = tuple(decay[i][h] * sts[h] + kv[i][h] for h in HEADS)
        o = [intra[i][h] + inter[h] for h in HEADS]
        normed.append(_merge_heads(tuple(o[h] * lax.rsqrt(jnp.mean(o[h] * o[h], axis=-1, keepdims=True) + LN_EPS)
                                         for h in HEADS)))
    return _merge_chunks(tuple(normed)) * nw * jax.nn.silu(hgate), sts


def _seg(ref, seg):
    return ref[:, seg * D_GROUP:(seg + 1) * D_GROUP]


def _hgrn2_fwd(proj, logits, norm_w):
    s = proj.shape[0]
    rows = CHUNKS_PER_STEP * CHUNK
    nc = s // rows

    def body(p_ref, lg_ref, nw_ref, y_ref, st_out_ref, st_scr):
        @pl.when(pl.program_id(0) == 0)
        def _():
            st_scr[...] = jnp.zeros_like(st_scr)

        sts = tuple(st_scr[h] for h in HEADS)
        y, sts_new = _hg_chunk(_seg(p_ref, 0), _seg(p_ref, 1), _seg(p_ref, 2), _seg(p_ref, 3),
                               lg_ref[0:1, :], lg_ref[1:2, :], nw_ref[...], sts)
        y_ref[...] = y.astype(y_ref.dtype)
        for h in HEADS:
            st_out_ref[h] = sts[h]
            st_scr[h] = sts_new[h]

    return pl.pallas_call(
        body, name="hgrn2_fwd", grid=(nc,),
        in_specs=[pl.BlockSpec((rows, 4 * D_GROUP), lambda c: (c, 0)),
                  pl.BlockSpec((2, D_GROUP), lambda c: (0, 0)),
                  pl.BlockSpec((1, D_GROUP), lambda c: (0, 0))],
        out_specs=[pl.BlockSpec((rows, D_GROUP), lambda c: (c, 0)),
                   pl.BlockSpec((None, N_HEADS, D_HEAD, D_HEAD), lambda c: (c, 0, 0, 0))],
        out_shape=[jax.ShapeDtypeStruct((s, 2 * D_GROUP), BF16),
                   jax.ShapeDtypeStruct((nc, N_HEADS, D_HEAD, D_HEAD), F32)],
        scratch_shapes=[pltpu.VMEM((N_HEADS, D_HEAD, D_HEAD), F32)],
        compiler_params=_params(("arbitrary",)),
    )(proj, logits, norm_w)


def _hgrn2_bwd(proj, logits, norm_w, states, d_y):
    s = proj.shape[0]
    rows = CHUNKS_PER_STEP * CHUNK
    nc = s // rows

    def body(p_ref, lg_ref, nw_ref, st_ref, dy_ref, dp_ref, dl_ref, dnw_ref, dsum_ref, dst_scr):
        @pl.when(pl.program_id(0) == 0)
        def _():
            dst_scr[...] = jnp.zeros_like(dst_scr)
            dl_ref[...] = jnp.zeros_like(dl_ref)
            dnw_ref[...] = jnp.zeros_like(dnw_ref)
            dsum_ref[...] = jnp.zeros_like(dsum_ref)

        _, vjp = jax.vjp(_hg_chunk, _seg(p_ref, 0), _seg(p_ref, 1), _seg(p_ref, 2), _seg(p_ref, 3),
                         lg_ref[0:1, :], lg_ref[1:2, :], nw_ref[...], tuple(st_ref[h] for h in HEADS))
        d_hq, d_hf, d_hi, d_hg, d_l0, d_l1, d_nw, d_sts = vjp((dy_ref[...], tuple(dst_scr[h] for h in HEADS)))
        for seg, val in enumerate((d_hq, d_hf, d_hi, d_hg)):
            dp_ref[:, seg * D_GROUP:(seg + 1) * D_GROUP] = val.astype(dp_ref.dtype)
            dsum_ref[:, seg * D_GROUP:(seg + 1) * D_GROUP] += jnp.sum(val, axis=0, keepdims=True)
        dl_ref[0:1, :] += d_l0
        dl_ref[1:2, :] += d_l1
        dnw_ref[...] += d_nw
        for h in HEADS:
            dst_scr[h] = d_sts[h]

    rev = lambda c: nc - 1 - c
    return pl.pallas_call(
        body, name="hgrn2_bwd", grid=(nc,),
        in_specs=[pl.BlockSpec((rows, 4 * D_GROUP), lambda c: (rev(c), 0)),
                  pl.BlockSpec((2, D_GROUP), lambda c: (0, 0)),
                  pl.BlockSpec((1, D_GROUP), lambda c: (0, 0)),
                  pl.BlockSpec((None, N_HEADS, D_HEAD, D_HEAD), lambda c: (rev(c), 0, 0, 0)),
                  pl.BlockSpec((rows, D_GROUP), lambda c: (rev(c), 0))],
        out_specs=[pl.BlockSpec((rows, 4 * D_GROUP), lambda c: (rev(c), 0)),
                   pl.BlockSpec((2, D_GROUP), lambda c: (0, 0)),
                   pl.BlockSpec((1, D_GROUP), lambda c: (0, 0)),
                   pl.BlockSpec((1, 4 * D_GROUP), lambda c: (0, 0))],
        out_shape=[jax.ShapeDtypeStruct((s, D_IN_MAIN), BF16), jax.ShapeDtypeStruct((2, D_GROUP), F32),
                   jax.ShapeDtypeStruct((1, D_GROUP), F32), jax.ShapeDtypeStruct((1, 4 * D_GROUP), F32)],
        scratch_shapes=[pltpu.VMEM((N_HEADS, D_HEAD, D_HEAD), F32)],
        compiler_params=_params(("arbitrary",)),
    )(proj, logits, norm_w, states, d_y)


def _gate_column(gates, lane, idx):
    return jnp.sum(jnp.where(lane == idx, gates, 0.0), axis=1, keepdims=True)


def _head_layer_norm(h):
    mu = jnp.mean(h, axis=-1, keepdims=True)
    var = jnp.mean(jnp.square(h - mu), axis=-1, keepdims=True)
    return (h - mu) * lax.rsqrt(var + LN_EPS)


def _ml_chunk(qc, kc, v, mo, gates, nw, cts, ns, ms):
    n = qc.shape[0] // CHUNK
    row, col = _chunk_masks(CHUNK)
    mask = col <= row
    eye = col == row
    to_row = lambda t: jnp.sum(jnp.where(eye, t, 0.0), axis=0, keepdims=True)
    q = _blocks(qc * (D_HEAD ** -0.5))
    k = _blocks(kc)
    vs = _blocks(v)
    gate_rows = _split_chunks(gates)
    lane = lax.broadcasted_iota(jnp.int32, gate_rows[0].shape, 1)
    each = [(i, h) for i in range(n) for h in HEADS]
    on_each = lambda f: {ih: f(*ih) for ih in each}
    ig = on_each(lambda i, h: _gate_column(gate_rows[i], lane, h))
    lf = on_each(lambda i, h: jax.nn.log_sigmoid(_gate_column(gate_rows[i], lane, N_HEADS + h)))
    lf_row = on_each(lambda i, h: to_row(lf[i, h]))
    ig_row = on_each(lambda i, h: to_row(ig[i, h]))
    b_col = on_each(lambda i, h: jnp.sum(jnp.where(mask, lf_row[i, h], 0.0), axis=1, keepdims=True))
    b_row = on_each(lambda i, h: jnp.sum(jnp.where(row <= col, lf[i, h], 0.0), axis=0, keepdims=True))
    g = on_each(lambda i, h: jnp.sum(lf[i, h], axis=0, keepdims=True))
    d = on_each(lambda i, h: jnp.where(mask, b_col[i, h] - b_row[i, h] + ig_row[i, h], -jnp.inf))
    a = on_each(lambda i, h: g[i, h] - b_col[i, h] + ig[i, h])
    m_at = {(0, h): ms[h] for h in HEADS}
    for i, h in each:
        m_at[i + 1, h] = lax.stop_gradient(jnp.maximum(g[i, h] + m_at[i, h], jnp.max(a[i, h], axis=0, keepdims=True)))
    inter = on_each(lambda i, h: b_col[i, h] + m_at[i, h])
    m_t = on_each(lambda i, h: lax.stop_gradient(jnp.maximum(inter[i, h], jnp.max(d[i, h], axis=1, keepdims=True))))
    qk = on_each(lambda i, h: _nt(q[i][h], k[i][h]))
    sc = on_each(lambda i, h: qk[i, h] * jnp.exp(d[i, h] - m_t[i, h]))
    w_inter = on_each(lambda i, h: jnp.exp(inter[i, h] - m_t[i, h]))
    sv = on_each(lambda i, h: _nn(sc[i, h], vs[i][h]))
    decay = on_each(lambda i, h: jnp.exp(g[i, h] + m_at[i, h] - m_at[i + 1, h]))
    wk = on_each(lambda i, h: k[i][h] * jnp.exp(a[i, h] - m_at[i + 1, h]))
    kv = on_each(lambda i, h: _tn(vs[i][h], wk[i, h]))
    normed = []
    for i in range(n):
        qc_state = [_nt(q[i][h], cts[h]) for h in HEADS]
        num = [sv[i, h] + w_inter[i, h] * qc_state[h] for h in HEADS]
        den = [jnp.sum(sc[i, h], axis=1, keepdims=True)
               + w_inter[i, h] * jnp.sum(q[i][h] * ns[h], axis=1, keepdims=True) for h in HEADS]
        hh = [num[h] / jnp.maximum(jnp.abs(den[h]), jnp.exp(-m_t[i, h])) for h in HEADS]
        cts = tuple(decay[i, h] * cts[h] + kv[i, h] for h in HEADS)
        ns = tuple(decay[i, h] * ns[h] + jnp.sum(wk[i, h], axis=0, keepdims=True) for h in HEADS)
        normed.append(_merge_heads(tuple(_head_layer_norm(hh[h]) for h in HEADS)))
    y = jax.nn.sigmoid(mo) * (_merge_chunks(tuple(normed)) * nw)
    return y, cts, ns, tuple(m_at[n, h] for h in HEADS)


def _mlstm_fwd(qk, proj, gates, norm_w, y):
    s = proj.shape[0]
    rows = ML_CHUNKS_PER_STEP * CHUNK
    nc = s // rows

    def body(qk_ref, vo_ref, g_ref, nw_ref, _, y_ref, ct_out, n_out, m_out, ct_scr, n_scr, m_scr):
        @pl.when(pl.program_id(0) == 0)
        def _():
            ct_scr[...] = jnp.zeros_like(ct_scr)
            n_scr[...] = jnp.zeros_like(n_scr)
            m_scr[...] = jnp.full(m_scr.shape, NEG_BIG, F32)

        cts = tuple(ct_scr[h] for h in HEADS)
        ns = tuple(n_scr[h] for h in HEADS)
        ms = tuple(m_scr[h] for h in HEADS)
        y, cts_new, ns_new, ms_new = _ml_chunk(_seg(qk_ref, 0), _seg(qk_ref, 1), _seg(vo_ref, 0), _seg(vo_ref, 1),
                                               g_ref[...], nw_ref[...], cts, ns, ms)
        y_ref[...] = y.astype(y_ref.dtype)
        for h in HEADS:
            ct_out[h], n_out[h], m_out[h] = cts[h], ns[h], ms[h]
            ct_scr[h], n_scr[h], m_scr[h] = cts_new[h], ns_new[h], ms_new[h]

    st = lambda r, w: pl.BlockSpec((None, N_HEADS, r, w), lambda c: (c, 0, 0, 0))
    return pl.pallas_call(
        body, name="mlstm_fwd", grid=(nc,),
        in_specs=[pl.BlockSpec((rows, 2 * D_GROUP), lambda c: (c, 0)),
                  pl.BlockSpec((rows, 2 * D_GROUP), lambda c: (c, VO_BLOCK)),
                  pl.BlockSpec((rows, LANES), lambda c: (c, 0)),
                  pl.BlockSpec((1, D_GROUP), lambda c: (0, 0)),
                  pl.BlockSpec(memory_space=pl.ANY)],
        out_specs=[pl.BlockSpec((rows, D_GROUP), lambda c: (c, 1)),
                   st(D_HEAD, D_HEAD), st(1, D_HEAD), st(1, 1)],
        out_shape=[jax.ShapeDtypeStruct(y.shape, y.dtype),
                   jax.ShapeDtypeStruct((nc, N_HEADS, D_HEAD, D_HEAD), F32),
                   jax.ShapeDtypeStruct((nc, N_HEADS, 1, D_HEAD), F32),
                   jax.ShapeDtypeStruct((nc, N_HEADS, 1, 1), F32)],
        input_output_aliases={4: 0},
        scratch_shapes=[pltpu.VMEM((N_HEADS, D_HEAD, D_HEAD), F32), pltpu.VMEM((N_HEADS, 1, D_HEAD), F32),
                        pltpu.VMEM((N_HEADS, 1, 1), F32)],
        compiler_params=_params(("arbitrary",)),
    )(qk, proj, gates, norm_w, y)


def _mlstm_bwd(qk, proj, gates, norm_w, ct_s, n_s, m_s, d_y, d_proj):
    s = proj.shape[0]
    rows = ML_CHUNKS_PER_STEP * CHUNK
    nc = s // rows

    def body(qk_ref, vo_ref, g_ref, nw_ref, ct_ref, n_ref, m_ref, dy_ref, _,
             dp_ref, dqk_ref, dg_ref, dnw_ref, dsum_ref, dct_scr, dn_scr):
        @pl.when(pl.program_id(0) == 0)
        def _():
            dct_scr[...] = jnp.zeros_like(dct_scr)
            dn_scr[...] = jnp.zeros_like(dn_scr)
            dnw_ref[...] = jnp.zeros_like(dnw_ref)
            dsum_ref[...] = jnp.zeros_like(dsum_ref)

        ms = tuple(m_ref[h] for h in HEADS)
        step = lambda *a: _ml_chunk(*a, ms)[:3]
        _, vjp = jax.vjp(step, _seg(qk_ref, 0), _seg(qk_ref, 1), _seg(vo_ref, 0), _seg(vo_ref, 1), g_ref[...],
                         nw_ref[...], tuple(ct_ref[h] for h in HEADS), tuple(n_ref[h] for h in HEADS))
        d_q, d_k, d_v, d_o, d_gates, d_nw, d_cts, d_ns = vjp(
            (dy_ref[...], tuple(dct_scr[h] for h in HEADS), tuple(dn_scr[h] for h in HEADS)))
        dqk_ref[:, 0:D_GROUP] = d_q
        dqk_ref[:, D_GROUP:2 * D_GROUP] = d_k
        for seg, val in enumerate((d_v, d_o)):
            dp_ref[:, seg * D_GROUP:(seg + 1) * D_GROUP] = val.astype(dp_ref.dtype)
            dsum_ref[:, seg * D_GROUP:(seg + 1) * D_GROUP] += jnp.sum(val, axis=0, keepdims=True)
        dg_ref[...] = d_gates
        dnw_ref[...] += d_nw
        for h in HEADS:
            dct_scr[h] = d_cts[h]
            dn_scr[h] = d_ns[h]

    rev = lambda c: nc - 1 - c
    st = lambda r, w: pl.BlockSpec((None, N_HEADS, r, w), lambda c: (rev(c), 0, 0, 0))
    return pl.pallas_call(
        body, name="mlstm_bwd", grid=(nc,),
        in_specs=[pl.BlockSpec((rows, 2 * D_GROUP), lambda c: (rev(c), 0)),
                  pl.BlockSpec((rows, 2 * D_GROUP), lambda c: (rev(c), VO_BLOCK)),
                  pl.BlockSpec((rows, LANES), lambda c: (rev(c), 0)),
                  pl.BlockSpec((1, D_GROUP), lambda c: (0, 0)),
                  st(D_HEAD, D_HEAD), st(1, D_HEAD), st(1, 1),
                  pl.BlockSpec((rows, D_GROUP), lambda c: (rev(c), 1)),
                  pl.BlockSpec(memory_space=pl.ANY)],
        out_specs=[pl.BlockSpec((rows, 2 * D_GROUP), lambda c: (rev(c), VO_BLOCK)),
                   pl.BlockSpec((rows, 2 * D_GROUP), lambda c: (rev(c), 0)),
                   pl.BlockSpec((rows, LANES), lambda c: (rev(c), 0)),
                   pl.BlockSpec((1, D_GROUP), lambda c: (0, 0)),
                   pl.BlockSpec((1, 2 * D_GROUP), lambda c: (0, 0))],
        out_shape=[jax.ShapeDtypeStruct(d_proj.shape, d_proj.dtype), jax.ShapeDtypeStruct((s, 2 * D_GROUP), F32),
                   jax.ShapeDtypeStruct((s, LANES), F32), jax.ShapeDtypeStruct((1, D_GROUP), F32),
                   jax.ShapeDtypeStruct((1, 2 * D_GROUP), F32)],
        input_output_aliases={8: 0},
        scratch_shapes=[pltpu.VMEM((N_HEADS, D_HEAD, D_HEAD), F32), pltpu.VMEM((N_HEADS, 1, D_HEAD), F32)],
        compiler_params=_params(("arbitrary",)),
    )(qk, proj, gates, norm_w, ct_s, n_s, m_s, d_y, d_proj)


LN_TOKENS = 512
ATT_TOKENS = 512


def _proj_res_ln(a, w, xres, g, b, name):
    s, dm = xres.shape
    k = a.shape[1]
    tb = min(LN_TOKENS, s)

    def body(a_ref, w_ref, x_ref, g_ref, b_ref, z_ref, o_ref):
        halves = [slice(0, tb // 2), slice(tb // 2, tb)]
        zs = [ALPHA * x_ref[rows, :] + _nn_raw(a_ref[rows, :], w_ref[...]) for rows in halves]
        for rows, z in zip(halves, zs):
            z_ref[rows, :] = z
            o_ref[rows, :] = _layer_norm(z, g_ref[...], b_ref[...])

    tok = pl.BlockSpec((tb, dm), lambda i: (i, 0))
    vec = pl.BlockSpec((1, dm), lambda i: (0, 0))
    act = jax.ShapeDtypeStruct((s, dm), F32)
    return pl.pallas_call(
        body, name=name, grid=(s // tb,),
        in_specs=[pl.BlockSpec((tb, k), lambda i: (i, 0)), pl.BlockSpec((k, dm), lambda i: (0, 0)), tok, vec, vec],
        out_specs=[tok, tok], out_shape=[act, act], compiler_params=_params(("parallel",)),
    )(a, w, xres, g, b)


def _ln_bwd_proj(d_out, z, g, b, w, name):
    s, dm = z.shape
    k = w.shape[0]
    tb = min(LN_TOKENS, s)

    def body(do_ref, z_ref, g_ref, b_ref, w_ref, dz_ref, da_ref, dg_ref, db_ref):
        @pl.when(pl.program_id(0) == 0)
        def _():
            dg_ref[...] = jnp.zeros_like(dg_ref)
            db_ref[...] = jnp.zeros_like(db_ref)

        halves = [slice(0, tb // 2), slice(tb // 2, tb)]
        d_zs = []
        for rows in halves:
            _, vjp = jax.vjp(_layer_norm, z_ref[rows, :], g_ref[...], b_ref[...])
            d_z, d_g, d_b = vjp(do_ref[rows, :])
            dz_ref[rows, :] = d_z
            dg_ref[...] += d_g
            db_ref[...] += d_b
            d_zs.append(d_z)
        for rows, d_z in zip(halves, d_zs):
            da_ref[rows, :] = _nt_raw(d_z, w_ref[...])

    tok = pl.BlockSpec((tb, dm), lambda i: (i, 0))
    vec = pl.BlockSpec((1, dm), lambda i: (0, 0))
    return pl.pallas_call(
        body, name=name, grid=(s // tb,),
        in_specs=[tok, tok, vec, vec, pl.BlockSpec((k, dm), lambda i: (0, 0))],
        out_specs=[tok, pl.BlockSpec((tb, k), lambda i: (i, 0)), vec, vec],
        out_shape=[jax.ShapeDtypeStruct((s, dm), F32), jax.ShapeDtypeStruct((s, k), F32),
                   jax.ShapeDtypeStruct((1, dm), F32), jax.ShapeDtypeStruct((1, dm), F32)],
        compiler_params=_params(("arbitrary",)),
    )(d_out, z, g, b, w)


def _proj_loss_tail(a, w, xres, g, b, target):
    s, dm = xres.shape
    k = a.shape[1]
    tb = min(ATT_TOKENS, s)

    def loss_fn(z, gg, bb, tgt):
        err = jnp.square(_layer_norm(z, gg, bb) - tgt)
        return 0.5 * jnp.sum(jnp.mean(err, axis=-1, keepdims=True), axis=0, keepdims=True)

    def body(a_ref, w_ref, x_ref, g_ref, b_ref, t_ref, loss_ref, dz_ref, dg_ref, db_ref):
        @pl.when(pl.program_id(0) == 0)
        def _():
            loss_ref[...] = jnp.zeros_like(loss_ref)
            dg_ref[...] = jnp.zeros_like(dg_ref)
            db_ref[...] = jnp.zeros_like(db_ref)

        halves = [slice(0, tb // 2), slice(tb // 2, tb)]
        zs = [ALPHA * x_ref[rows, :] + _nn_raw(a_ref[rows, :], w_ref[...]) for rows in halves]
        for rows, z in zip(halves, zs):
            tgt = t_ref[rows, :]
            loss, vjp = jax.vjp(lambda zz, gg, bb, tgt=tgt: loss_fn(zz, gg, bb, tgt), z, g_ref[...], b_ref[...])
            d_z, d_g, d_b = vjp(jnp.ones((1, 1), F32))
            loss_ref[...] += loss
            dz_ref[rows, :] = d_z
            dg_ref[...] += d_g
            db_ref[...] += d_b

    tok = pl.BlockSpec((tb, dm), lambda i: (i, 0))
    vec = pl.BlockSpec((1, dm), lambda i: (0, 0))
    one = pl.BlockSpec((1, 1), lambda i: (0, 0))
    return pl.pallas_call(
        body, name="ffn_down_loss_tail", grid=(s // tb,),
        in_specs=[pl.BlockSpec((tb, k), lambda i: (i, 0)), pl.BlockSpec((k, dm), lambda i: (0, 0)), tok, vec, vec, tok],
        out_specs=[one, tok, vec, vec],
        out_shape=[jax.ShapeDtypeStruct((1, 1), F32), jax.ShapeDtypeStruct((s, dm), F32),
                   jax.ShapeDtypeStruct((1, dm), F32), jax.ShapeDtypeStruct((1, dm), F32)],
        compiler_params=_params(("arbitrary",)),
    )(a, w, xres, g, b, target)


def _att_heads(qs, ks, vs):
    sc = [_nt(q, k) * (CA_DH ** -0.5) for q, k in zip(qs, ks)]
    p = [jax.nn.softmax(s, axis=-1) for s in sc]
    return tuple(_nn(pp, v) for pp, v in zip(p, vs))


def _head_slices(ref_or_value, offset):
    return tuple(ref_or_value[:, offset + h * CA_DH:offset + (h + 1) * CA_DH] for h in range(CA_HEADS))


def _cross_attention_fwd(x1, kv, wq, wo, g, b):
    s = x1.shape[0]
    tb = min(ATT_TOKENS, s)

    def body(x_ref, kv_ref, wq_ref, wo_ref, g_ref, b_ref, att_ref, z_ref, o_ref):
        x_blk = x_ref[...]
        q = _nn_raw(x_blk, wq_ref[...])
        att = jnp.concatenate(_att_heads(_head_slices(q, 0), _head_slices(kv_ref, 0), _head_slices(kv_ref, D_MODEL)),
                              axis=1)
        att_ref[...] = att.astype(att_ref.dtype)
        z = ALPHA * x_blk + _nn_raw(att, wo_ref[...])
        z_ref[...] = z
        o_ref[...] = _layer_norm(z, g_ref[...], b_ref[...])

    tok = pl.BlockSpec((tb, D_MODEL), lambda i: (i, 0))
    mat = pl.BlockSpec((D_MODEL, D_MODEL), lambda i: (0, 0))
    vec = pl.BlockSpec((1, D_MODEL), lambda i: (0, 0))
    act = jax.ShapeDtypeStruct((s, D_MODEL), F32)
    return pl.pallas_call(
        body, name="cross_attention_fwd", grid=(s // tb,),
        in_specs=[tok, pl.BlockSpec((N_MEM, 2 * D_MODEL), lambda i: (0, 0)), mat, mat, vec, vec],
        out_specs=[tok, tok, tok],
        out_shape=[jax.ShapeDtypeStruct((s, D_MODEL), BF16), act, act],
        compiler_params=_params(("parallel",)),
    )(x1, kv, wq, wo, g, b)


def _cross_attention_bwd(d_x2, x1, z2, kv, wq, wo, g, b):
    s = x1.shape[0]
    tb = min(ATT_TOKENS, s)

    def body(dx2_ref, x_ref, z_ref, kv_ref, wq_ref, wo_ref, g_ref, b_ref,
             dx1_ref, dq_ref, dz_ref, dkv_ref, dg_ref, db_ref):
        @pl.when(pl.program_id(0) == 0)
        def _():
            dkv_ref[...] = jnp.zeros_like(dkv_ref)
            dg_ref[...] = jnp.zeros_like(dg_ref)
            db_ref[...] = jnp.zeros_like(db_ref)

        q = _nn_raw(x_ref[...], wq_ref[...])
        _, ln_vjp = jax.vjp(_layer_norm, z_ref[...], g_ref[...], b_ref[...])
        d_z, d_g, d_b = ln_vjp(dx2_ref[...])
        dg_ref[...] += d_g
        db_ref[...] += d_b
        dz_ref[...] = d_z.astype(dz_ref.dtype)
        d_att = _nt_raw(d_z, wo_ref[...])
        _, vjp = jax.vjp(_att_heads, _head_slices(q, 0), _head_slices(kv_ref, 0), _head_slices(kv_ref, D_MODEL))
        d_qs, d_ks, d_vs = vjp(_head_slices(d_att, 0))
        for h in range(CA_HEADS):
            lo = h * CA_DH
            dkv_ref[:, lo:lo + CA_DH] += d_ks[h]
            dkv_ref[:, D_MODEL + lo:D_MODEL + lo + CA_DH] += d_vs[h]
        d_q = jnp.concatenate(d_qs, axis=1)
        dq_ref[...] = d_q.astype(dq_ref.dtype)
        dx1_ref[...] = ALPHA * d_z + _nt_raw(d_q, wq_ref[...])

    tok = pl.BlockSpec((tb, D_MODEL), lambda i: (i, 0))
    mem = pl.BlockSpec((N_MEM, 2 * D_MODEL), lambda i: (0, 0))
    mat = pl.BlockSpec((D_MODEL, D_MODEL), lambda i: (0, 0))
    vec = pl.BlockSpec((1, D_MODEL), lambda i: (0, 0))
    low = jax.ShapeDtypeStruct((s, D_MODEL), BF16)
    return pl.pallas_call(
        body, name="cross_attention_bwd", grid=(s // tb,),
        in_specs=[tok, tok, tok, mem, mat, mat, vec, vec], out_specs=[tok, tok, tok, mem, vec, vec],
        out_shape=[jax.ShapeDtypeStruct((s, D_MODEL), F32), low, low,
                   jax.ShapeDtypeStruct((N_MEM, 2 * D_MODEL), F32),
                   jax.ShapeDtypeStruct((1, D_MODEL), F32), jax.ShapeDtypeStruct((1, D_MODEL), F32)],
        compiler_params=_params(("arbitrary",)),
    )(d_x2, x1, z2, kv, wq, wo, g, b)


def _local_step(x, mem, target, w, mid_weights=None, ffn_weights=None, down_weights=None, on_ffn_grads=None,
                on_mid_grads=None,
                on_small_grads=None, on_last_grads=None):
    w = dict(w)
    s = x.shape[0]
    tm = min(512, s)
    tt_big = min(1024, s)
    proj, gates = _input_projection(x, w["w_in_t"], w["w_in_gate_t"], w["b_in_main"], w["b_in_gate"],
                                    min(2048, s), 512)
    qk = _ml_conv_fwd(proj, w["ml_conv_w"], w["ml_conv_b"])
    y, hg_states = _hgrn2_fwd(proj, w["hg_lb_logits"], w["hg_norm_w"])
    y, ct_s, n_s, m_s = _mlstm_fwd(qk, proj, gates, w["ml_norm_w"], y)
    if mid_weights is not None:
        w.update(mid_weights(y))
    z1, x1 = _proj_res_ln(y, w["w_out"], x, w["ln1_g"], w["ln1_b"], "out_proj_ln1")
    kv = _matmul_nn(mem, w["ca_wkv"], None, N_MEM, CA_DH, "kv")
    att, z2, x2 = _cross_attention_fwd(x1, kv, w["ca_wq"], w["ca_wo"], w["ln2_g"], w["ln2_b"])
    if ffn_weights is not None:
        w.update(ffn_weights(x2))
    u = _matmul_nt(x2, w["ffn_w_up_t"], min(2048, s), UP_SHARD_P, "ffn_up", out_dtype=BF16)
    hid = _ffn_conv_fwd(u, w["ffn_conv_w"], w["ffn_conv_b"])
    if down_weights is not None:
        w.update(down_weights(hid))
    loss, d_z3, d_ln3_g, d_ln3_b = _proj_loss_tail(hid, w["ffn_w_down"], x2, w["ln3_g"], w["ln3_b"], target)
    grads = {"ln3_g": d_ln3_g, "ln3_b": d_ln3_b}
    grads["ffn_w_down"] = _matmul_tn(hid, d_z3, 1536, D_MODEL, tt_big, "d_w_down")
    d_hid = _matmul_nt(d_z3, w["ffn_w_down"], tm, D_FF_P, "d_hid", out_dtype=BF16)
    d_ug, d_uv, d_cwg, d_cwv, d_cbg, d_cbv = _ffn_conv_bwd(u, w["ffn_conv_w"], w["ffn_conv_b"], d_hid)
    grads["ffn_conv_w"] = jnp.concatenate([d_cwg, d_cwv], axis=-1)
    grads["ffn_conv_b"] = jnp.concatenate([d_cbg, d_cbv], axis=-1)
    d_w_up = _matmul_tn(d_ug, x2, D_FF_P // 2, D_MODEL, tt_big, "d_w_up_gate", rows=D_UP_P)
    grads["ffn_w_up"] = _matmul_tn(d_uv, x2, D_FF_P // 2, D_MODEL, tt_big, "d_w_up_val", rows=D_UP_P, row0=D_FF_P,
                                   into=d_w_up)
    d_x2 = _matmul_nn_sum([(d_ug, w["ffn_w_up_t"], 0), (d_uv, w["ffn_w_up_t"], D_FF_P)], d_z3, ALPHA,
                          min(256, s), "d_x2")
    if on_ffn_grads is not None:
        d_x2 = on_ffn_grads(grads, d_x2)
    d_x1, d_q, d_z2, d_kv, grads["ln2_g"], grads["ln2_b"] = _cross_attention_bwd(
        d_x2, x1, z2, kv, w["ca_wq"], w["ca_wo"], w["ln2_g"], w["ln2_b"])
    grads["ca_wo"] = _matmul_tn(att, d_z2, D_MODEL, D_MODEL, tt_big, "d_ca_wo")
    grads["ca_wq"] = _matmul_tn(x1, d_q, D_MODEL, D_MODEL, tt_big, "d_ca_wq")
    grads["ca_wkv"] = _matmul_tn(mem, d_kv, D_MODEL, CA_DH, N_MEM, "d_ca_wkv", shards=N_DEV, group=N_DEV)
    d_z1, d_y, grads["ln1_g"], grads["ln1_b"] = _ln_bwd_proj(d_x1, z1, w["ln1_g"], w["ln1_b"], w["w_out"],
                                                             "ln1_bwd_out_proj")
    grads["w_out"] = _matmul_tn(y, d_z1, D_MODEL, D_MODEL, tt_big, "d_w_out")
    if on_mid_grads is not None:
        d_y = on_mid_grads(grads, d_y)
    d_proj, grads["hg_lb_logits"], grads["hg_norm_w"], db_hg = _hgrn2_bwd(
        proj, w["hg_lb_logits"], w["hg_norm_w"], hg_states, d_y)
    d_proj, d_qk, d_gates, grads["ml_norm_w"], db_vo = _mlstm_bwd(
        qk, proj, gates, w["ml_norm_w"], ct_s, n_s, m_s, d_y, d_proj)
    d_proj, grads["ml_conv_w"], grads["ml_conv_b"], db_qk = _ml_conv_bwd(
        proj, w["ml_conv_w"], w["ml_conv_b"], d_qk, d_proj)
    grads["b_in_main"] = jnp.concatenate([db_hg, db_qk, db_vo], axis=-1)
    grads["b_in_gate"] = jnp.sum(d_gates, axis=0, keepdims=True)
    if on_small_grads is not None:
        d_proj = on_small_grads(grads, loss, d_proj)
    grads["w_in_main"], grads["w_in_gate"] = _input_projection_grads(d_proj, d_gates, x, min(2048, D_IN_MAIN), tt_big)
    if on_last_grads is not None:
        d_z1 = on_last_grads(grads, d_z1)
    grad_x = _matmul_nn_sum([(d_proj, w["w_in_t"], 0), (d_gates, w["w_in_gate_t"], 0)], d_z1, ALPHA, tm, "d_x")
    return loss, grad_x, grads


HBM_SPEC = pl.BlockSpec(memory_space=pltpu.HBM)


def _coords():
    return lax.axis_index("x"), lax.axis_index("y"), lax.axis_index("c")


def _other_chips(x, y):
    return [(1 - x, y), (x, 1 - y), (1 - x, 1 - y)]


def _my_slot():
    x, y, c = _coords()
    return 4 * x + 2 * y + c


SEM_SPEC = pl.BlockSpec(memory_space=pltpu.SEMAPHORE)
ANY_SPEC = pl.BlockSpec(memory_space=pl.ANY)
SIDE_EFFECT = pltpu.SideEffectType.DATAFLOW_SIDE_EFFECTING


def _peer(x, y, c, d):
    flip = lambda v, bit: 1 - v if bit else v
    p = (flip(x, d & 4), flip(y, d & 2), flip(c, d & 1))
    return p, 4 * p[0] + 2 * p[1] + p[2]


def _direct_copies(gather, src_refs, land_refs, send_sems, recv_sems):
    x, y, c = _coords()
    me = 4 * x + 2 * y + c
    copies = []
    for a in range(len(src_refs)):
        for d in range(1, N_DEV):
            peer, peer_slot = _peer(x, y, c, d)
            copies.append(pltpu.make_async_remote_copy(
                src_ref=src_refs[a] if gather else src_refs[a].at[peer_slot],
                dst_ref=land_refs[a].at[me] if gather else land_refs[a].at[d - 1],
                send_sem=send_sems.at[7 * a + d - 1], recv_sem=recv_sems.at[7 * a + d - 1],
                device_id=peer, device_id_type=MESH))
    return copies


def _hbm(t):
    return pltpu.HBM(t.shape, t.dtype)


def _chip_copies(src_refs, land_refs, send_sems, recv_sems):
    x, y, c = _coords()
    me = 4 * x + 2 * y + c
    targets = [(x, y, 1 - c)] + [(cx, cy, c) for cx, cy in _other_chips(x, y)]
    return [pltpu.make_async_remote_copy(
        src_ref=src_refs[a], dst_ref=land_refs[a].at[me], send_sem=send_sems.at[4 * a + k],
        recv_sem=recv_sems.at[4 * a + k], device_id=target, device_id_type=MESH)
        for a in range(len(src_refs)) for k, target in enumerate(targets)]


def _forward_copies(land_refs, send_sems, recv_sems):
    x, y, c = _coords()
    return [pltpu.make_async_remote_copy(
        src_ref=land_refs[a].at[4 * cx + 2 * cy + c], dst_ref=land_refs[a].at[4 * cx + 2 * cy + c],
        send_sem=send_sems.at[3 * a + j], recv_sem=recv_sems.at[3 * a + j],
        device_id=(x, y, 1 - c), device_id_type=MESH)
        for a in range(len(land_refs)) for j, (cx, cy) in enumerate(_other_chips(x, y))]


def _split_copy_start(make_copies, n_sems, operands, through, name):
    n_ops = len(operands)

    def body(*refs):
        for cp in make_copies(refs[:n_ops], refs[n_ops + 1], refs[n_ops + 2]):
            cp.start()

    ins = [pltpu.with_memory_space_constraint(t, pltpu.HBM) for t in (*operands, through)]
    sems = pltpu.SemaphoreType.DMA((n_sems,))
    res = pl.pallas_call(
        body, name=name, out_shape=(sems, sems, *[_hbm(t) for t in ins]),
        in_specs=[HBM_SPEC] * (n_ops + 1), out_specs=(SEM_SPEC, SEM_SPEC, *[HBM_SPEC] * (n_ops + 1)),
        input_output_aliases={i: 2 + i for i in range(n_ops + 1)},
        compiler_params=pltpu.CompilerParams(has_side_effects=SIDE_EFFECT),
    )(*ins)
    return (res[0], res[1], list(res[2:2 + n_ops])), res[2 + n_ops]


def _split_copy_wait(make_copies, started, after, name):
    send_sems, recv_sems, operands = started
    n_ops = len(operands)
    after = list(after) if isinstance(after, (list, tuple)) else [after]

    def body(*refs):
        for cp in make_copies(refs[:n_ops], refs[n_ops], refs[n_ops + 1]):
            cp.wait_send()
            cp.wait_recv()

    res = pl.pallas_call(
        body, name=name, out_shape=tuple(_hbm(t) for t in operands),
        in_specs=[HBM_SPEC] * n_ops + [SEM_SPEC, SEM_SPEC] + [ANY_SPEC] * len(after),
        out_specs=tuple([HBM_SPEC] * n_ops), input_output_aliases={i: i for i in range(n_ops)},
        compiler_params=pltpu.CompilerParams(has_side_effects=SIDE_EFFECT),
    )(*operands, send_sems, recv_sems, *after)
    return list(res)


def _halves(make_copies, na):
    return lambda refs, send_sems, recv_sems: make_copies(refs[:na], refs[na:], send_sems, recv_sems)


def _direct_start(gather, arrays, through, name):
    na = len(arrays)
    lands = [lax.empty((N_DEV,) + t.shape if gather else (N_DEV - 1,) + t.shape[1:], t.dtype) for t in arrays]
    return _split_copy_start(_halves(functools.partial(_direct_copies, gather), na), 7 * na, [*arrays, *lands],
                             through, name)


def _direct_wait(gather, started, after, name):
    na = len(started[2]) // 2
    operands = _split_copy_wait(_halves(functools.partial(_direct_copies, gather), na), started, after, name)
    return operands[:na], operands[na:]


def _two_level_gather(shards, glue, name):
    na = len(shards)
    lands = [lax.empty((N_DEV,) + t.shape, t.dtype) for t in shards]
    nothing = jnp.zeros((SUBLANES, LANES), F32)
    started, _ = _split_copy_start(_halves(_chip_copies, na), 4 * na, [*shards, *lands], nothing, name + "_start")
    operands = _split_copy_wait(_halves(_chip_copies, na), started, glue, name + "_wait")
    started, mine = _split_copy_start(_forward_copies, 3 * na, operands[na:], operands[0], name + "_forward_start")
    lands = _split_copy_wait(_forward_copies, started, mine, name + "_forward_wait")
    return [lax.dynamic_update_index_in_dim(land, own, _my_slot(), 0)
            for own, land in zip([mine, *operands[1:na]], lands)]


def _row_tile(rows):
    for t in (256, 176, 128):
        if rows % t == 0 and rows > t:
            return t
    return rows


def _adamw_math(g, w, m, v):
    m_new = ADAM_B1 * m + (1.0 - ADAM_B1) * g
    v_new = ADAM_B2 * v + (1.0 - ADAM_B2) * jnp.square(g)
    m_hat = m_new / (1.0 - ADAM_B1 ** ADAM_STEP)
    v_hat = v_new / (1.0 - ADAM_B2 ** ADAM_STEP)
    delta = -ADAM_LR * (m_hat / (jnp.sqrt(v_hat) + ADAM_EPS) + ADAM_WD * w)
    return delta, m_new, v_new


def _adamw_sharded(chip, sums, got, w, m, v, name):
    r, c = w.shape
    tr = _row_tile(r)
    n_got = got.shape[0]

    def body(chip_ref, s_ref, g_ref, w_ref, m_ref, v_ref, go_ref, d_ref, nm_ref, nv_ref):
        g = s_ref[...].astype(F32)
        for i in range(n_got):
            g = g + g_ref[i].astype(F32)
        delta, m_new, v_new = _adamw_math(g, w_ref[...], m_ref[...], v_ref[...])
        go_ref[...] = g
        d_ref[...] = delta
        nm_ref[...] = m_new
        nv_ref[...] = v_new

    blk = pl.BlockSpec((tr, c), lambda i, chip_ref: (i, 0))
    out = jax.ShapeDtypeStruct((r, c), F32)
    return pl.pallas_call(
        body, name=name,
        grid_spec=pltpu.PrefetchScalarGridSpec(
            num_scalar_prefetch=1, grid=(r // tr,),
            in_specs=[pl.BlockSpec((None, tr, c), lambda i, chip_ref: (chip_ref[0], i, 0)),
                      pl.BlockSpec((n_got, tr, c), lambda i, chip_ref: (0, i, 0)), blk, blk, blk],
            out_specs=[blk, blk, blk, blk]),
        out_shape=[out, out, out, out],
        compiler_params=_params(("parallel",)),
    )(chip, sums, got, w, m, v)


def _adamw_replicated(parts, w, m, v):
    p, r, c = parts.shape
    names = SMALL_NAMES
    shapes = [w[n].shape for n in names]

    def body(*refs):
        p_ref = refs[0]
        ins = refs[1:1 + 3 * len(names)]
        outs = refs[1 + 3 * len(names):-2]
        loss_ref, sum_scr = refs[-2], refs[-1]
        total = p_ref[0]
        for i in range(1, p):
            total = total + p_ref[i]
        sum_scr[...] = total
        for k, n in enumerate(names):
            w_ref, m_ref, v_ref = ins[3 * k:3 * k + 3]
            g_ref, d_ref, nm_ref, nv_ref = outs[4 * k:4 * k + 4]
            for row, lane0, width, src_row in _small_pieces(n, shapes[k]):
                here = (slice(row, row + 1), slice(lane0, lane0 + width))
                g = sum_scr[src_row:src_row + 1, 0:width]
                delta, m_new, v_new = _adamw_math(g, w_ref[here], m_ref[here], v_ref[here])
                g_ref[here] = g
                d_ref[here] = delta
                nm_ref[here] = m_new
                nv_ref[here] = v_new
        loss_ref[...] = sum_scr[SMALL_LOSS_ROW:SMALL_LOSS_ROW + 1, 0:1]

    whole = lambda shape: pl.BlockSpec(shape, lambda i: (0,) * len(shape))
    args = [parts] + [t[n] for n in names for t in (w, m, v)]
    out_shape = [jax.ShapeDtypeStruct(s, F32) for s in shapes for _ in range(4)] + [jax.ShapeDtypeStruct((1, 1), F32)]
    res = pl.pallas_call(
        body, name="adamw_replicated", grid=(1,),
        in_specs=[whole(t.shape) for t in args], out_specs=[whole(s.shape) for s in out_shape],
        out_shape=out_shape, scratch_shapes=[pltpu.VMEM((r, c), F32)],
        compiler_params=_params(("arbitrary",)),
    )(*args)
    results = [{n: res[4 * k + j] for k, n in enumerate(names)} for j in range(4)]
    return results, res[-1]


SHARDED_NAMES = ("w_in", "ml_conv_w", "w_out", "ca_wq", "ca_wkv", "ca_wo", "ffn_w_up", "ffn_conv_w", "ffn_w_down")
SMALL_NAMES = ("b_in", "hg_lb_logits", "hg_norm_w", "ml_conv_b", "ml_norm_w", "ln1_g", "ln1_b",
               "ln2_g", "ln2_b", "ffn_conv_b", "ln3_g", "ln3_b")
WEIGHT_NAMES = ("w_in", "b_in", "hg_lb_logits", "hg_norm_w", "ml_conv_w", "ml_conv_b", "ml_norm_w", "w_out",
                "ln1_g", "ln1_b", "ca_wq", "ca_wkv", "ca_wo", "ln2_g", "ln2_b", "ffn_w_up", "ffn_conv_w",
                "ffn_conv_b", "ffn_w_down", "ln3_g", "ln3_b")
PAD_TO = {"ffn_w_up": UP_SHARD_P, "ffn_conv_w": UP_SHARD_P}
SMALL_ROWS = 24
SMALL_W = D_MODEL
SMALL_SHAPES = {"b_in": (1, D_IN), "hg_lb_logits": (2, D_GROUP), "hg_norm_w": (1, D_GROUP),
                "ml_conv_b": (1, 2 * D_GROUP), "ml_norm_w": (1, D_GROUP), "ln1_g": (1, D_MODEL), "ln1_b": (1, D_MODEL),
                "ln2_g": (1, D_MODEL), "ln2_b": (1, D_MODEL), "ffn_conv_b": (1, D_UP), "ln3_g": (1, D_MODEL),
                "ln3_b": (1, D_MODEL)}


def _shard_2d(name, block):
    t = block[0]
    if name in PAD_TO:
        t = jnp.pad(t, ((0, 0), (0, PAD_TO[name] - t.shape[1])))
    return t


TRANSPOSED = ("w_in", "ffn_w_up")


def _update_shard(name, block):
    if name not in TRANSPOSED:
        return _shard_2d(name, block)
    t = jnp.transpose(block[0])
    rows = PAD_TO.get(name, t.shape[0])
    return jnp.pad(t, ((0, rows - t.shape[0]), (0, 0)))


def _shard_like(name, t, like):
    if name in TRANSPOSED:
        return jnp.transpose(t[:like.shape[2]])[None]
    return t[:, :like.shape[2]][None]


def _pad_cols(t, width):
    return jnp.pad(t, ((0, 0), (0, width - t.shape[1])))


FIRST_NAMES = ("w_in", "ml_conv_w")
FFN_NAMES = ("ffn_w_up", "ffn_w_down", "ffn_conv_w")
MID_NAMES = ("ca_wo", "ca_wq", "ca_wkv", "w_out")


def _first_weights(g, small):
    w = dict(small)
    w["w_in_t"] = g["w_in"].reshape(D_IN, D_MODEL)
    w["w_in_gate_t"] = jnp.pad(w["w_in_t"][D_IN_MAIN:], ((0, LANES - (D_IN - D_IN_MAIN)), (0, 0)))
    w["b_in_main"] = small["b_in"][:, :D_IN_MAIN]
    w["b_in_gate"] = _pad_cols(small["b_in"][:, D_IN_MAIN:], LANES)
    w["ml_conv_w"] = jnp.transpose(g["ml_conv_w"], (1, 0, 2)).reshape(ML_CONV, 2 * D_GROUP)
    return w


def _mid_weights(g):
    w = {n: g[n].reshape(D_MODEL, D_MODEL) for n in ("w_out", "ca_wq", "ca_wo")}
    w["ca_wkv"] = g["ca_wkv"]
    return w


FFN_UP_NAMES = ("ffn_w_up", "ffn_conv_w")
FFN_DOWN_NAMES = ("ffn_w_down",)


def _ffn_up_weights(g, small):
    w = {"ffn_w_up_t": g["ffn_w_up"].reshape(D_UP_P, D_MODEL)}
    w["ffn_conv_w"] = jnp.transpose(g["ffn_conv_w"], (1, 0, 2)).reshape(FFN_CONV, D_UP_P)
    w["ffn_conv_b"] = _pad_cols(small["ffn_conv_b"].reshape(N_DEV, UP_SHARD), UP_SHARD_P).reshape(1, D_UP_P)
    return w


def _ffn_down_weights(g):
    down = g["ffn_w_down"].reshape(N_DEV // 2, UP_SHARD, D_MODEL)
    return {"ffn_w_down": jnp.pad(down, ((0, 0), (0, UP_SHARD_P - UP_SHARD), (0, 0))).reshape(D_FF_P, D_MODEL)}


def _owner_stack(n, grads):
    if n == "w_in":
        rows = jnp.concatenate([grads["w_in_main"], grads["w_in_gate"][:D_IN - D_IN_MAIN]], axis=0)
        return rows.reshape(N_DEV, W_IN_SHARD, D_MODEL)
    if n == "ffn_w_up":
        return grads[n].reshape(N_DEV, UP_SHARD_P, D_MODEL)
    if n in ("w_out", "ca_wq", "ca_wo"):
        return grads[n].reshape(N_DEV, D_MODEL // N_DEV, D_MODEL)
    if n == "ffn_w_down":
        down = grads[n].reshape(N_DEV // 2, UP_SHARD_P, D_MODEL)[:, :UP_SHARD]
        return down.reshape(N_DEV, D_FF // N_DEV, D_MODEL)
    if n == "ml_conv_w":
        return jnp.transpose(grads[n].reshape(ML_CONV, N_DEV, LANES), (1, 0, 2))
    if n == "ffn_conv_w":
        return jnp.transpose(grads[n].reshape(FFN_CONV, N_DEV, UP_SHARD_P), (1, 0, 2))
    return grads[n]


def _small_grads(grads):
    out = {n: grads[n] for n in SMALL_NAMES if n in grads}
    out["b_in"] = jnp.concatenate([grads["b_in_main"], grads["b_in_gate"][:, :D_IN - D_IN_MAIN]], axis=1)
    out["ffn_conv_b"] = grads["ffn_conv_b"].reshape(N_DEV, UP_SHARD_P)[:, :UP_SHARD].reshape(1, D_UP)
    return out


def _small_rows(shape):
    return shape[0] if shape[1] <= SMALL_W else -(-shape[1] // SMALL_W)


SMALL_BASE = {n: sum(_small_rows(SMALL_SHAPES[k]) for k in SMALL_NAMES[:i]) for i, n in enumerate(SMALL_NAMES)}
SMALL_LOSS_ROW = sum(_small_rows(SMALL_SHAPES[n]) for n in SMALL_NAMES)
assert SMALL_LOSS_ROW < SMALL_ROWS


def _small_pieces(name, shape):
    base = SMALL_BASE[name]
    if shape[1] <= SMALL_W:
        return [(i, 0, shape[1], base + i) for i in range(shape[0])]
    return [(0, k * SMALL_W, min(SMALL_W, shape[1] - k * SMALL_W), base + k) for k in range(_small_rows(shape))]


def _pack_small(p, loss):
    rows = []
    for n in SMALL_NAMES:
        t = p[n]
        nrows = _small_rows(t.shape)
        if t.shape[1] <= SMALL_W:
            rows.append(_pad_cols(t, SMALL_W))
        else:
            rows.append(_pad_cols(t, nrows * SMALL_W).reshape(nrows, SMALL_W))
    rows.append(_pad_cols(loss, SMALL_W))
    slab = jnp.concatenate(rows, axis=0)
    return jnp.pad(slab, ((0, SMALL_ROWS - slab.shape[0]), (0, 0)))


def kernel(x, mem, w_in, b_in, hg_lb_logits, hg_norm_w, ml_conv_w, ml_conv_b, ml_norm_w, w_out, ln1_g, ln1_b, ca_wq, ca_wkv, ca_wo, ln2_g, ln2_b, ffn_w_up, ffn_conv_w, ffn_conv_b, ffn_w_down, ln3_g, ln3_b, loss_target, m_w_in, m_b_in, m_hg_lb_logits, m_hg_norm_w, m_ml_conv_w, m_ml_conv_b, m_ml_norm_w, m_w_out, m_ln1_g, m_ln1_b, m_ca_wq, m_ca_wkv, m_ca_wo, m_ln2_g, m_ln2_b, m_ffn_w_up, m_ffn_conv_w, m_ffn_conv_b, m_ffn_w_down, m_ln3_g, m_ln3_b, v_w_in, v_b_in, v_hg_lb_logits, v_hg_norm_w, v_ml_conv_w, v_ml_conv_b, v_ml_norm_w, v_w_out, v_ln1_g, v_ln1_b, v_ca_wq, v_ca_wkv, v_ca_wo, v_ln2_g, v_ln2_b, v_ffn_w_up, v_ffn_conv_w, v_ffn_conv_b, v_ffn_w_down, v_ln3_g, v_ln3_b):
    params = dict(w_in=w_in, b_in=b_in, hg_lb_logits=hg_lb_logits, hg_norm_w=hg_norm_w, ml_conv_w=ml_conv_w,
                  ml_conv_b=ml_conv_b, ml_norm_w=ml_norm_w, w_out=w_out, ln1_g=ln1_g, ln1_b=ln1_b, ca_wq=ca_wq,
                  ca_wkv=ca_wkv, ca_wo=ca_wo, ln2_g=ln2_g, ln2_b=ln2_b, ffn_w_up=ffn_w_up, ffn_conv_w=ffn_conv_w,
                  ffn_conv_b=ffn_conv_b, ffn_w_down=ffn_w_down, ln3_g=ln3_g, ln3_b=ln3_b)
    mom1 = dict(w_in=m_w_in, b_in=m_b_in, hg_lb_logits=m_hg_lb_logits, hg_norm_w=m_hg_norm_w,
                ml_conv_w=m_ml_conv_w, ml_conv_b=m_ml_conv_b, ml_norm_w=m_ml_norm_w, w_out=m_w_out, ln1_g=m_ln1_g,
                ln1_b=m_ln1_b, ca_wq=m_ca_wq, ca_wkv=m_ca_wkv, ca_wo=m_ca_wo, ln2_g=m_ln2_g, ln2_b=m_ln2_b,
                ffn_w_up=m_ffn_w_up, ffn_conv_w=m_ffn_conv_w, ffn_conv_b=m_ffn_conv_b, ffn_w_down=m_ffn_w_down,
                ln3_g=m_ln3_g, ln3_b=m_ln3_b)
    mom2 = dict(w_in=v_w_in, b_in=v_b_in, hg_lb_logits=v_hg_lb_logits, hg_norm_w=v_hg_norm_w,
                ml_conv_w=v_ml_conv_w, ml_conv_b=v_ml_conv_b, ml_norm_w=v_ml_norm_w, w_out=v_w_out, ln1_g=v_ln1_g,
                ln1_b=v_ln1_b, ca_wq=v_ca_wq, ca_wkv=v_ca_wkv, ca_wo=v_ca_wo, ln2_g=v_ln2_g, ln2_b=v_ln2_b,
                ffn_w_up=v_ffn_w_up, ffn_conv_w=v_ffn_conv_w, ffn_conv_b=v_ffn_conv_b, ffn_w_down=v_ffn_w_down,
                ln3_g=v_ln3_g, ln3_b=v_ln3_b)

    x_idx, y_idx, c_idx = _coords()
    as_index = lambda v: jnp.reshape(v, (1,)).astype(jnp.int32)
    me = as_index(4 * x_idx + 2 * y_idx + c_idx)
    small_params = {n: params[n] for n in SMALL_NAMES}

    shards = {n: _update_shard(n, params[n]) for n in SHARDED_NAMES}
    m_shards = {n: _update_shard(n, mom1[n]) for n in SHARDED_NAMES}
    v_shards = {n: _update_shard(n, mom2[n]) for n in SHARDED_NAMES}
    outgoing = {n: shards[n] if "conv" in n else shards[n].astype(BF16) for n in SHARDED_NAMES}
    to_send = lambda names: [outgoing[n] for n in names]
    glue = [*m_shards.values(), *v_shards.values(), *shards.values(),
            *[outgoing[n] for n in SHARDED_NAMES if n not in FIRST_NAMES]]
    first = dict(zip(FIRST_NAMES, _two_level_gather(to_send(FIRST_NAMES), glue, "weights_gather_first")))
    mid_started, through = _direct_start(True, to_send(MID_NAMES), first["w_in"], "weights_gather_start_mid")
    ffn_started, through = _direct_start(True, to_send(FFN_UP_NAMES), through, "weights_gather_start_ffn_up")
    down_started, first["w_in"] = _direct_start(True, to_send(FFN_DOWN_NAMES), through,
                                                "weights_gather_start_ffn_down")

    def gathered_weights(names, started, after, tag):
        mine, lands = _direct_wait(True, started, after, "weights_gather_wait_" + tag)
        return {n: lax.dynamic_update_index_in_dim(land, own, me[0], 0) for n, own, land in zip(names, mine, lands)}

    started, own_stacks = {}, {}

    def start_group(names, tag):
        def hook(grads, through):
            own_stacks[tag] = [_owner_stack(n, grads).astype(BF16) for n in names]
            started[tag], through = _direct_start(False, own_stacks[tag], through, "grads_start_" + tag)
            return through
        return hook

    def start_small(grads, loss, through):
        started["small"], through = _direct_start(True, [_pack_small(_small_grads(grads), loss)], through,
                                                  "small_gather_start")
        return through

    loss, grad_x, grads = _local_step(
        x[0], mem[0], loss_target[0], _first_weights(first, small_params),
        lambda y: _mid_weights(gathered_weights(MID_NAMES, mid_started, y, "mid")),
        lambda x2: _ffn_up_weights(gathered_weights(FFN_UP_NAMES, ffn_started, x2, "ffn_up"), small_params),
        lambda hid: _ffn_down_weights(gathered_weights(FFN_DOWN_NAMES, down_started, hid, "ffn_down")),
        start_group(FFN_NAMES, "ffn"), start_group(MID_NAMES, "mid"), start_small, start_group(FIRST_NAMES, "last"))

    sharded_out = {}

    def update_group(names, tag, after):
        _, lands = _direct_wait(False, started[tag], after, "grads_wait_" + tag)
        for n, st, land in zip(names, own_stacks[tag], lands):
            res = _adamw_sharded(me, st, land, shards[n], m_shards[n], v_shards[n], "adamw_" + n)
            sharded_out[n] = [_shard_like(n, t, params[n]) for t in res]

    update_group(FFN_NAMES, "ffn", grad_x)
    update_group(MID_NAMES, "mid", grad_x)
    own_small, small_lands = _direct_wait(True, started["small"], grad_x, "small_gather_wait")
    small_parts = lax.dynamic_update_index_in_dim(small_lands[0], own_small[0], me[0], 0)
    small_out, total_loss = _adamw_replicated(small_parts, small_params, {n: mom1[n] for n in SMALL_NAMES},
                                              {n: mom2[n] for n in SMALL_NAMES})
    done = [t for n in FFN_NAMES + MID_NAMES for t in sharded_out[n]]
    done += [t for small in small_out for t in small.values()]
    update_group(FIRST_NAMES, "last", done)

    outs = []
    for k, small in enumerate(small_out):
        outs.extend(sharded_out[n][k] if n in sharded_out else small[n] for n in WEIGHT_NAMES)
    return (total_loss[0, 0], grad_x[None], *outs)
```

```python
import functools
import math

import jax
import jax.numpy as jnp
from jax import lax
from jax.experimental import pallas as pl
from jax.experimental.pallas import tpu as pltpu

F32 = jnp.float32
BF16 = jnp.bfloat16
HIGHEST = lax.Precision.HIGHEST
MESH = pl.DeviceIdType.MESH

N_DEV = 8
D_MODEL = 1024
N_MEM = 256
N_HEADS = 4
D_HEAD = 128
D_GROUP = N_HEADS * D_HEAD
CHUNK = 64
ML_CONV = 4
FFN_CONV = 3
D_FF = 2816
D_UP = 2 * D_FF
CA_HEADS = 4
CA_DH = D_MODEL // CA_HEADS
LANES = 128
SUBLANES = 8
D_IN = 8 * D_GROUP + 2 * N_HEADS
D_IN_MAIN = 8 * D_GROUP
W_IN_SHARD = D_IN // N_DEV
UP_SHARD = D_UP // N_DEV
UP_SHARD_P = 768
UP_TILE = D_UP // 4
ALPHA = 2.0 ** 0.25
LN_EPS = 1e-5
NEG_BIG = -1e30
ADAM_LR = 0.001
ADAM_B1 = 0.9
ADAM_B2 = 0.999
ADAM_EPS = 1e-08
ADAM_WD = 0.01
ADAM_STEP = 10
VMEM_LIMIT = 56 * 1024 * 1024

SEG_MQ = 4 * D_GROUP // LANES
VO_BLOCK = 3


def _params(sem):
    return pltpu.CompilerParams(dimension_semantics=sem, vmem_limit_bytes=VMEM_LIMIT)


def _dg(a, b, ca, cb, precision=None):
    return lax.dot_general(a, b, (((ca,), (cb,)), ((), ())), precision=precision,
                           preferred_element_type=F32)


def _nn_raw(a, b):
    return _dg(a.astype(BF16), b.astype(BF16), 1, 0)


def _nt_raw(a, b):
    return _dg(a.astype(BF16), b.astype(BF16), 1, 1)


def _tn_raw(a, b):
    return _dg(a.astype(BF16), b.astype(BF16), 0, 0)


@jax.custom_vjp
def _nn(a, b):
    return _nn_raw(a, b)


_nn.defvjp(lambda a, b: (_nn_raw(a, b), (a, b)),
           lambda res, g: (_nt_raw(g, res[1]), _tn_raw(res[0], g)))


@jax.custom_vjp
def _nt(a, b):
    return _nt_raw(a, b)


_nt.defvjp(lambda a, b: (_nt_raw(a, b), (a, b)),
           lambda res, g: (_nn_raw(g, res[1]), _tn_raw(g, res[0])))


@jax.custom_vjp
def _tn(a, b):
    return _tn_raw(a, b)


_tn.defvjp(lambda a, b: (_tn_raw(a, b), (a, b)),
           lambda res, g: (_nt_raw(res[1], g), _nn_raw(res[0], g)))


def _layer_norm(z, g, b):
    mu = jnp.mean(z, axis=-1, keepdims=True)
    var = jnp.mean(jnp.square(z - mu), axis=-1, keepdims=True)
    return (z - mu) * lax.rsqrt(var + LN_EPS) * g + b


def _matmul_nn(a, w, bias, tm, tn, name, out_dtype=F32):
    m, k = a.shape
    if w.ndim == 3:
        n = w.shape[0] * w.shape[2]
        assert tn == w.shape[2]
        w_spec = pl.BlockSpec((None, k, tn), lambda i, j: (j, 0, 0))
    else:
        n = w.shape[1]
        w_spec = pl.BlockSpec((k, tn), lambda i, j: (0, j))

    def body(*refs):
        a_ref, w_ref = refs[0], refs[1]
        o_ref = refs[-1]
        acc = _nn_raw(a_ref[...], w_ref[...])
        if bias is not None:
            acc = acc + refs[2][...]
        o_ref[...] = acc.astype(o_ref.dtype)

    in_specs = [pl.BlockSpec((tm, k), lambda i, j: (i, 0)), w_spec]
    args = [a, w]
    if bias is not None:
        in_specs.append(pl.BlockSpec((1, tn), lambda i, j: (0, j)))
        args.append(bias)
    return pl.pallas_call(
        body, name=name, grid=(m // tm, n // tn), in_specs=in_specs,
        out_specs=pl.BlockSpec((tm, tn), lambda i, j: (i, j)),
        out_shape=jax.ShapeDtypeStruct((m, n), out_dtype),
        compiler_params=_params(("parallel", "parallel")),
    )(*args)


def _matmul_nt(d, w, tm, tk, name, k_out=None, bias=None, out_dtype=F32):
    m, n = d.shape
    k = k_out or w.shape[0]

    def body(*refs):
        acc = _nt_raw(refs[0][...], refs[1][...])
        if bias is not None:
            acc = acc + refs[2][...]
        refs[-1][...] = acc.astype(refs[-1].dtype)

    in_specs = [pl.BlockSpec((tm, n), lambda i, j: (i, 0)), pl.BlockSpec((tk, n), lambda i, j: (j, 0))]
    args = [d, w]
    if bias is not None:
        in_specs.append(pl.BlockSpec((1, tk), lambda i, j: (0, j)))
        args.append(bias)
    return pl.pallas_call(
        body, name=name, grid=(m // tm, k // tk), in_specs=in_specs,
        out_specs=pl.BlockSpec((tm, tk), lambda i, j: (i, j)),
        out_shape=jax.ShapeDtypeStruct((m, k), out_dtype),
        compiler_params=_params(("parallel", "parallel")),
    )(*args)


def _input_projection(x, w_t, w_gate_t, b_main, b_gate, tm, tk):
    m, n = x.shape

    def body(x_ref, w_ref, wg_ref, b_ref, bg_ref, o_ref, g_ref):
        lhs = x_ref[...].astype(BF16)
        o_ref[...] = _nt_raw(lhs, w_ref[...]) + b_ref[...]

        @pl.when(pl.program_id(1) == 0)
        def _():
            g_ref[...] = _nt_raw(lhs, wg_ref[...]) + bg_ref[...]

    return pl.pallas_call(
        body, name="proj", grid=(m // tm, D_IN_MAIN // tk),
        in_specs=[pl.BlockSpec((tm, n), lambda i, j: (i, 0)), pl.BlockSpec((tk, n), lambda i, j: (j, 0)),
                  pl.BlockSpec((LANES, n), lambda i, j: (0, 0)), pl.BlockSpec((1, tk), lambda i, j: (0, j)),
                  pl.BlockSpec((1, LANES), lambda i, j: (0, 0))],
        out_specs=[pl.BlockSpec((tm, tk), lambda i, j: (i, j)), pl.BlockSpec((tm, LANES), lambda i, j: (i, 0))],
        out_shape=[jax.ShapeDtypeStruct((m, D_IN_MAIN), F32), jax.ShapeDtypeStruct((m, LANES), F32)],
        compiler_params=_params(("parallel", "arbitrary")),
    )(x, w_t, w_gate_t, b_main, b_gate)


def _input_projection_grads(d_proj, d_gates, x, tm, tt):
    t, m = d_proj.shape
    n = x.shape[1]
    last = t // tt - 1

    def body(a_ref, g_ref, x_ref, o_ref, og_ref, acc_ref, accg_ref):
        i, kk = pl.program_id(0), pl.program_id(1)

        @pl.when(kk == 0)
        def _():
            acc_ref[...] = jnp.zeros_like(acc_ref)

        @pl.when((kk == 0) & (i == 0))
        def _():
            accg_ref[...] = jnp.zeros_like(accg_ref)

        rhs = x_ref[...].astype(BF16)
        acc_ref[...] += _tn_raw(a_ref[...], rhs)

        @pl.when(i == 0)
        def _():
            accg_ref[...] += _tn_raw(g_ref[...], rhs)

        @pl.when(kk == last)
        def _():
            o_ref[...] = acc_ref[...].astype(o_ref.dtype)

        @pl.when((kk == last) & (i == 0))
        def _():
            og_ref[...] = accg_ref[...].astype(og_ref.dtype)

    return pl.pallas_call(
        body, name="d_w_in", grid=(m // tm, t // tt),
        in_specs=[pl.BlockSpec((tt, tm), lambda i, kk: (kk, i)), pl.BlockSpec((tt, LANES), lambda i, kk: (kk, 0)),
                  pl.BlockSpec((tt, n), lambda i, kk: (kk, 0))],
        out_specs=[pl.BlockSpec((tm, n), lambda i, kk: (i, 0)), pl.BlockSpec((LANES, n), lambda i, kk: (0, 0))],
        out_shape=[jax.ShapeDtypeStruct((m, n), BF16), jax.ShapeDtypeStruct((LANES, n), BF16)],
        scratch_shapes=[pltpu.VMEM((tm, n), F32), pltpu.VMEM((LANES, n), F32)],
        compiler_params=_params(("arbitrary", "arbitrary")),
    )(d_proj, d_gates, x)


def _matmul_nn_sum(pairs, add, scale, tm, name):
    m = pairs[0][0].shape[0]
    n = pairs[0][1].shape[1]
    in_specs, args = [], []
    for a, w, row0 in pairs:
        kk = a.shape[1]
        in_specs += [pl.BlockSpec((tm, kk), lambda i: (i, 0)),
                     pl.BlockSpec((kk, n), lambda i, blk=row0 // kk: (blk, 0))]
        args += [a, w]
    if add is not None:
        in_specs.append(pl.BlockSpec((tm, n), lambda i: (i, 0)))
        args.append(add)

    def body(*refs):
        acc = None
        for p in range(len(pairs)):
            term = _nn_raw(refs[2 * p][...], refs[2 * p + 1][...])
            acc = term if acc is None else acc + term
        if add is not None:
            acc = acc + scale * refs[2 * len(pairs)][...]
        refs[-1][...] = acc

    return pl.pallas_call(
        body, name=name, grid=(m // tm,), in_specs=in_specs,
        out_specs=pl.BlockSpec((tm, n), lambda i: (i, 0)),
        out_shape=jax.ShapeDtypeStruct((m, n), F32),
        compiler_params=_params(("parallel",)),
    )(*args)


def _matmul_tn(a, b, tm, tn, tt, name, shards=None, shard0=0, group=1, into=None, colsum=False, rows=None, row0=0):
    t, m = a.shape
    n = b.shape[1]
    assert not colsum or tm == m
    n_in = 2 + (into is not None)
    out_dtype = BF16
    per_step = 1 if shards is None else group
    width = per_step * tn

    def body(*refs):
        a_ref, b_ref = refs[0], refs[1]
        o_ref, acc_ref = refs[n_in], refs[-1]
        first = pl.program_id(2) == 0

        @pl.when(first)
        def _():
            acc_ref[...] = jnp.zeros_like(acc_ref)

        if shards is None:
            acc_ref[...] += _tn_raw(a_ref[...], b_ref[...])
        else:
            lhs = a_ref[...].astype(BF16)
            for g in range(per_step):
                acc_ref[g] += _tn_raw(lhs, b_ref[:, g * tn:(g + 1) * tn])

        @pl.when(pl.program_id(2) == t // tt - 1)
        def _():
            o_ref[...] = acc_ref[...].astype(o_ref.dtype)

        if colsum:
            s_ref = refs[n_in + 1]

            @pl.when(first)
            def _():
                s_ref[...] = jnp.zeros_like(s_ref)

            s_ref[...] += jnp.sum(b_ref[...], axis=0, keepdims=True)

    in_specs = [pl.BlockSpec((tt, tm), lambda i, j, kk: (kk, i)),
                pl.BlockSpec((tt, width), lambda i, j, kk: (kk, j))]
    args = [a, b]
    aliases = {}
    if into is not None:
        in_specs.append(pl.BlockSpec(memory_space=pl.ANY))
        args.append(into)
        aliases = {2: 0}
    if shards is None:
        out_specs = [pl.BlockSpec((tm, tn), lambda i, j, kk: (row0 // tm + i, j))]
        out_shape = [jax.ShapeDtypeStruct((rows or m, n), out_dtype)]
        acc = pltpu.VMEM((tm, tn), F32)
    else:
        out_specs = [pl.BlockSpec((per_step, tm, tn), lambda i, j, kk: (shard0 // per_step + j, i, 0))]
        out_shape = [jax.ShapeDtypeStruct((shards, m, tn), out_dtype)]
        acc = pltpu.VMEM((per_step, tm, tn), F32)
    if colsum:
        out_specs.append(pl.BlockSpec((1, tn), lambda i, j, kk: (0, j)))
        out_shape.append(jax.ShapeDtypeStruct((1, n), F32))
    res = pl.pallas_call(
        body, name=name, grid=(m // tm, n // width, t // tt), in_specs=in_specs, out_specs=out_specs,
        out_shape=out_shape, input_output_aliases=aliases, scratch_shapes=[acc],
        compiler_params=_params(("parallel", "parallel", "arbitrary")),
    )(*args)
    return res if colsum else res[0]


ROW_TILE = 64


def _stack(ref, start, rows):
    return ref[pl.ds(start, rows), :].astype(F32).reshape(rows // SUBLANES, SUBLANES, LANES)


def _vreg_rows(ref, n):
    return [jnp.broadcast_to(ref[j:j + 1, :], (SUBLANES, LANES))[None] for j in range(n)]


def _column_total(acc):
    return jnp.sum(acc, axis=0, keepdims=True)


def _conv_fwd_tile(pad_ref, taps_w, bias, r0, rows):
    taps = len(taps_w)
    acc = bias
    for j in range(taps):
        acc = acc + _stack(pad_ref, SUBLANES - (taps - 1 - j) + r0, rows) * taps_w[j]
    return acc


def _conv_grads_tile(pad_ref, dpad_ref, dx_ref, taps_w, dws, r0, rows):
    taps = len(taps_w)
    x_rows = _stack(pad_ref, SUBLANES + r0, rows)
    dx = None
    for j in range(taps):
        d_shifted = _stack(dpad_ref, r0 + (taps - 1 - j), rows)
        term = d_shifted * taps_w[j]
        dx = term if dx is None else dx + term
        dws[j] = dws[j] + jnp.sum(d_shifted * x_rows, axis=0)
    dx_ref[r0:r0 + rows, :] = dx.reshape(rows, LANES).astype(dx_ref.dtype)
    return jnp.sum(dx, axis=0)


def _ml_conv_fwd(proj, conv_w, conv_b):
    s = proj.shape[0]
    nblk = 2 * D_GROUP // LANES

    def body(x_ref, w_ref, b_ref, o_ref, pad_ref):
        pad_ref[0:SUBLANES, :] = jnp.zeros((SUBLANES, LANES), F32)
        pad_ref[SUBLANES:, :] = x_ref[...].astype(F32)
        taps_w, bias = _vreg_rows(w_ref, ML_CONV), _vreg_rows(b_ref, 1)[0]
        for r0 in range(0, s, ROW_TILE):
            rows = min(ROW_TILE, s - r0)
            o_ref[r0:r0 + rows, :] = jax.nn.silu(_conv_fwd_tile(pad_ref, taps_w, bias, r0, rows)).reshape(rows, LANES)

    return pl.pallas_call(
        body, name="ml_conv_fwd", grid=(nblk,),
        in_specs=[pl.BlockSpec((s, LANES), lambda j: (0, SEG_MQ + j)),
                  pl.BlockSpec((ML_CONV, LANES), lambda j: (0, j)),
                  pl.BlockSpec((1, LANES), lambda j: (0, j))],
        out_specs=pl.BlockSpec((s, LANES), lambda j: (0, j)),
        out_shape=jax.ShapeDtypeStruct((s, 2 * D_GROUP), F32),
        scratch_shapes=[pltpu.VMEM((s + SUBLANES, LANES), F32)],
        compiler_params=_params(("parallel",)),
    )(proj, conv_w, conv_b)


def _ml_conv_bwd(proj, conv_w, conv_b, d_qk, d_proj):
    s = proj.shape[0]
    nblk = 2 * D_GROUP // LANES

    def body(x_ref, w_ref, b_ref, dy_ref, _, dx_ref, dw_ref, db_ref, dxs_ref, pad_ref, dpad_ref):
        pad_ref[0:SUBLANES, :] = jnp.zeros((SUBLANES, LANES), F32)
        pad_ref[SUBLANES:, :] = x_ref[...].astype(F32)
        dpad_ref[s:, :] = jnp.zeros((SUBLANES, LANES), F32)
        taps_w, bias = _vreg_rows(w_ref, ML_CONV), _vreg_rows(b_ref, 1)[0]
        db = jnp.zeros((SUBLANES, LANES), F32)
        for r0 in range(0, s, ROW_TILE):
            rows = min(ROW_TILE, s - r0)
            pre = _conv_fwd_tile(pad_ref, taps_w, bias, r0, rows)
            _, vjp = jax.vjp(jax.nn.silu, pre)
            d_pre, = vjp(_stack(dy_ref, r0, rows))
            dpad_ref[r0:r0 + rows, :] = d_pre.reshape(rows, LANES)
            db = db + jnp.sum(d_pre, axis=0)
        db_ref[...] = _column_total(db)
        dws = [jnp.zeros((SUBLANES, LANES), F32) for _ in range(ML_CONV)]
        dx_sum = jnp.zeros((SUBLANES, LANES), F32)
        for r0 in range(0, s, ROW_TILE):
            dx_sum = dx_sum + _conv_grads_tile(pad_ref, dpad_ref, dx_ref, taps_w, dws, r0, min(ROW_TILE, s - r0))
        dxs_ref[...] = _column_total(dx_sum)
        for j in range(ML_CONV):
            dw_ref[j:j + 1, :] = _column_total(dws[j])

    return pl.pallas_call(
        body, name="ml_conv_bwd", grid=(nblk,),
        in_specs=[pl.BlockSpec((s, LANES), lambda j: (0, SEG_MQ + j)),
                  pl.BlockSpec((ML_CONV, LANES), lambda j: (0, j)),
                  pl.BlockSpec((1, LANES), lambda j: (0, j)),
                  pl.BlockSpec((s, LANES), lambda j: (0, j)),
                  pl.BlockSpec(memory_space=pl.ANY)],
        out_specs=[pl.BlockSpec((s, LANES), lambda j: (0, SEG_MQ + j)),
                   pl.BlockSpec((ML_CONV, LANES), lambda j: (0, j)),
                   pl.BlockSpec((1, LANES), lambda j: (0, j)),
                   pl.BlockSpec((1, LANES), lambda j: (0, j))],
        out_shape=[jax.ShapeDtypeStruct(d_proj.shape, d_proj.dtype),
                   jax.ShapeDtypeStruct((ML_CONV, 2 * D_GROUP), F32),
                   jax.ShapeDtypeStruct((1, 2 * D_GROUP), F32),
                   jax.ShapeDtypeStruct((1, 2 * D_GROUP), F32)],
        input_output_aliases={4: 0},
        scratch_shapes=[pltpu.VMEM((s + SUBLANES, LANES), F32), pltpu.VMEM((s + SUBLANES, LANES), F32)],
        compiler_params=_params(("parallel",)),
    )(proj, conv_w, conv_b, d_qk, d_proj)


def _gelu_mul(a, b):
    return jax.nn.gelu(a) * b


GELU_C = math.sqrt(2.0 / math.pi)
GELU_K = 0.044715


def _gelu_mul_grads(a, b, d):
    a2 = a * a
    t = jnp.tanh(GELU_C * (a + GELU_K * (a * a2)))
    cdf = 0.5 * (1.0 + t)
    slope = cdf + (0.5 * GELU_C) * a * (1.0 - t * t) * (1.0 + (3.0 * GELU_K) * a2)
    return d * b * slope, d * (a * cdf)


FFN_BLOCKS = D_FF // LANES


def _ffn_conv_fwd(u, conv_w, conv_b):
    s = u.shape[0]

    def body(g_ref, v_ref, wg_ref, wv_ref, bg_ref, bv_ref, o_ref, gpad_ref, vpad_ref):
        for pad_ref, x_ref in ((gpad_ref, g_ref), (vpad_ref, v_ref)):
            pad_ref[0:SUBLANES, :] = jnp.zeros((SUBLANES, LANES), F32)
            pad_ref[SUBLANES:, :] = x_ref[...].astype(F32)
        taps_g, bias_g = _vreg_rows(wg_ref, FFN_CONV), _vreg_rows(bg_ref, 1)[0]
        taps_v, bias_v = _vreg_rows(wv_ref, FFN_CONV), _vreg_rows(bv_ref, 1)[0]
        for r0 in range(0, s, ROW_TILE):
            rows = min(ROW_TILE, s - r0)
            ug = _conv_fwd_tile(gpad_ref, taps_g, bias_g, r0, rows)
            uv = _conv_fwd_tile(vpad_ref, taps_v, bias_v, r0, rows)
            o_ref[r0:r0 + rows, :] = _gelu_mul(ug, uv).reshape(rows, LANES).astype(o_ref.dtype)

    col = lambda off: (lambda j: (0, off + j))
    return pl.pallas_call(
        body, name="ffn_conv_fwd", grid=(FFN_BLOCKS,),
        in_specs=[pl.BlockSpec((s, LANES), col(0)), pl.BlockSpec((s, LANES), col(FFN_BLOCKS)),
                  pl.BlockSpec((FFN_CONV, LANES), col(0)), pl.BlockSpec((FFN_CONV, LANES), col(FFN_BLOCKS)),
                  pl.BlockSpec((1, LANES), col(0)), pl.BlockSpec((1, LANES), col(FFN_BLOCKS))],
        out_specs=pl.BlockSpec((s, LANES), col(0)),
        out_shape=jax.ShapeDtypeStruct((s, D_FF), BF16),
        scratch_shapes=[pltpu.VMEM((s + SUBLANES, LANES), F32), pltpu.VMEM((s + SUBLANES, LANES), F32)],
        compiler_params=_params(("parallel",)),
    )(u, u, conv_w, conv_w, conv_b, conv_b)


def _ffn_conv_bwd(u, conv_w, conv_b, d_h):
    s = u.shape[0]

    def body(g_ref, v_ref, wg_ref, wv_ref, bg_ref, bv_ref, dh_ref,
             dug_ref, duv_ref, dwg_ref, dwv_ref, dbg_ref, dbv_ref,
             gpad_ref, vpad_ref, dgpad_ref, dvpad_ref):
        for pad_ref, x_ref in ((gpad_ref, g_ref), (vpad_ref, v_ref)):
            pad_ref[0:SUBLANES, :] = jnp.zeros((SUBLANES, LANES), F32)
            pad_ref[SUBLANES:, :] = x_ref[...].astype(F32)
        dgpad_ref[s:, :] = jnp.zeros((SUBLANES, LANES), F32)
        dvpad_ref[s:, :] = jnp.zeros((SUBLANES, LANES), F32)
        taps_g, bias_g = _vreg_rows(wg_ref, FFN_CONV), _vreg_rows(bg_ref, 1)[0]
        taps_v, bias_v = _vreg_rows(wv_ref, FFN_CONV), _vreg_rows(bv_ref, 1)[0]
        dbg = jnp.zeros((SUBLANES, LANES), F32)
        dbv = jnp.zeros((SUBLANES, LANES), F32)
        for r0 in range(0, s, ROW_TILE):
            rows = min(ROW_TILE, s - r0)
            ug = _conv_fwd_tile(gpad_ref, taps_g, bias_g, r0, rows)
            uv = _conv_fwd_tile(vpad_ref, taps_v, bias_v, r0, rows)
            d_ug, d_uv = _gelu_mul_grads(ug, uv, _stack(dh_ref, r0, rows))
            dgpad_ref[r0:r0 + rows, :] = d_ug.reshape(rows, LANES)
            dvpad_ref[r0:r0 + rows, :] = d_uv.reshape(rows, LANES)
            dbg = dbg + jnp.sum(d_ug, axis=0)
            dbv = dbv + jnp.sum(d_uv, axis=0)
        dbg_ref[...] = _column_total(dbg)
        dbv_ref[...] = _column_total(dbv)
        for pad_ref, dpad_ref, taps_w, dx_ref, dw_ref in ((gpad_ref, dgpad_ref, taps_g, dug_ref, dwg_ref),
                                                          (vpad_ref, dvpad_ref, taps_v, duv_ref, dwv_ref)):
            dws = [jnp.zeros((SUBLANES, LANES), F32) for _ in range(FFN_CONV)]
            for r0 in range(0, s, ROW_TILE):
                _conv_grads_tile(pad_ref, dpad_ref, dx_ref, taps_w, dws, r0, min(ROW_TILE, s - r0))
            for j in range(FFN_CONV):
                dw_ref[j:j + 1, :] = _column_total(dws[j])

    col = lambda off: (lambda j: (0, off + j))
    seq = pl.BlockSpec((s, LANES), col(0))
    return pl.pallas_call(
        body, name="ffn_conv_bwd", grid=(FFN_BLOCKS,),
        in_specs=[pl.BlockSpec((s, LANES), col(0)), pl.BlockSpec((s, LANES), col(FFN_BLOCKS)),
                  pl.BlockSpec((FFN_CONV, LANES), col(0)), pl.BlockSpec((FFN_CONV, LANES), col(FFN_BLOCKS)),
                  pl.BlockSpec((1, LANES), col(0)), pl.BlockSpec((1, LANES), col(FFN_BLOCKS)), seq],
        out_specs=[seq, seq, pl.BlockSpec((FFN_CONV, LANES), col(0)), pl.BlockSpec((FFN_CONV, LANES), col(0)),
                   pl.BlockSpec((1, LANES), col(0)), pl.BlockSpec((1, LANES), col(0))],
        out_shape=[jax.ShapeDtypeStruct((s, D_FF), BF16), jax.ShapeDtypeStruct((s, D_FF), BF16),
                   jax.ShapeDtypeStruct((FFN_CONV, D_FF), F32), jax.ShapeDtypeStruct((FFN_CONV, D_FF), F32),
                   jax.ShapeDtypeStruct((1, D_FF), F32), jax.ShapeDtypeStruct((1, D_FF), F32)],
        scratch_shapes=[pltpu.VMEM((s + SUBLANES, LANES), F32) for _ in range(4)],
        compiler_params=_params(("parallel",)),
    )(u, u, conv_w, conv_w, conv_b, conv_b, d_h)


def _chunk_masks(c):
    row = lax.broadcasted_iota(jnp.int32, (c, c), 0)
    col = lax.broadcasted_iota(jnp.int32, (c, c), 1)
    return row, col


@jax.custom_vjp
def _split_heads(x):
    return tuple(x[:, h * D_HEAD:(h + 1) * D_HEAD] for h in range(N_HEADS))


_split_heads.defvjp(lambda x: (_split_heads(x), None), lambda _, gs: (jnp.concatenate(gs, axis=1),))


@jax.custom_vjp
def _merge_heads(xs):
    return jnp.concatenate(xs, axis=1)


_merge_heads.defvjp(lambda xs: (_merge_heads(xs), None), lambda _, g: (_split_heads(g),))


@jax.custom_vjp
def _split_chunks(x):
    return tuple(x[i * CHUNK:(i + 1) * CHUNK] for i in range(x.shape[0] // CHUNK))


_split_chunks.defvjp(lambda x: (_split_chunks(x), None), lambda _, gs: (jnp.concatenate(gs, axis=0),))


@jax.custom_vjp
def _merge_chunks(xs):
    return jnp.concatenate(xs, axis=0)


_merge_chunks.defvjp(lambda xs: (_merge_chunks(xs), None), lambda _, g: (_split_chunks(g),))


def _blocks(x):
    return [_split_heads(rows) for rows in _split_chunks(x)]


def _per_chunk_rows(per_chunk, rid):
    out = per_chunk[0]
    for i in range(1, len(per_chunk)):
        out = jnp.where(rid >= i * CHUNK, per_chunk[i], out)
    return out


HEADS = range(N_HEADS)
CHUNKS_PER_STEP = 8
ML_CHUNKS_PER_STEP = 1


def _hg_chunk(hq, hf, hi, hgate, l0, l1, nw, sts):
    n = hq.shape[0] // CHUNK
    causal = _chunk_masks(CHUNK)
    causal = causal[1] <= causal[0]
    mx = lax.stop_gradient(jnp.maximum(l0, l1))
    e0 = jnp.exp(l0 - mx)
    e1 = jnp.exp(l1 - mx)
    lb = e0 / (e0 + e1)
    sig = jax.nn.sigmoid(hf)
    lf = jnp.log(lb + (1.0 - lb) * sig)
    k = (1.0 - lb) * jax.nn.sigmoid(-hf)
    q = jax.nn.silu(hq)
    tri = causal.astype(F32)
    b = _merge_chunks(tuple(_dg(tri, rows, 1, 0, HIGHEST) for rows in _split_chunks(lf)))
    rid = lax.broadcasted_iota(jnp.int32, b.shape, 0)
    pick = lambda r: jnp.sum(jnp.where(rid == r, b, 0.0), axis=0, keepdims=True)
    b_last_c = [pick(i * CHUNK + CHUNK - 1) for i in range(n)]
    b_ref = _per_chunk_rows([pick(i * CHUNK + CHUNK // 2 - 1) for i in range(n)], rid)
    b_last = _per_chunk_rows(b_last_c, rid)
    qa = _blocks(q * jnp.exp(b - b_ref))
    ka = _blocks(k * jnp.exp(b_ref - b))
    qe = _blocks(q * jnp.exp(b))
    kd = _blocks(k * jnp.exp(b_last - b))
    decay = [_split_heads(jnp.exp(b_last_c[i])) for i in range(n)]
    v = _blocks(hi)
    chunks = range(n)
    attn = [[jnp.where(causal, _nt(qa[i][h], ka[i][h]), 0.0) for h in HEADS] for i in chunks]
    intra = [[_nn(attn[i][h], v[i][h]) for h in HEADS] for i in chunks]
    kv = [[_tn(v[i][h], kd[i][h]) for h in HEADS] for i in chunks]
    normed = []
    for i in chunks:
        inter = [_nt(qe[i][h], sts[h]) for h in HEADS]
        sts = tuple(decay[i][h] * sts[h] + kv[i][h] for h in HEADS)
        o = [intra[i][h] + inter[h] for h in HEADS]
        normed.append(_merge_heads(tuple(o[h] * lax.rsqrt(jnp.mean(o[h] * o[h], axis=-1, keepdims=True) + LN_EPS)
                                         for h in HEADS)))
    return _merge_chunks(tuple(normed)) * nw * jax.nn.silu(hgate), sts


def _seg(ref, seg):
    return ref[:, seg * D_GROUP:(seg + 1) * D_GROUP]


def _hgrn2_fwd(proj, logits, norm_w):
    s = proj.shape[0]
    rows = CHUNKS_PER_STEP * CHUNK
    nc = s // rows

    def body(p_ref, lg_ref, nw_ref, y_ref, st_out_ref, st_scr):
        @pl.when(pl.program_id(0) == 0)
        def _():
            st_scr[...] = jnp.zeros_like(st_scr)

        sts = tuple(st_scr[h] for h in HEADS)
        y, sts_new = _hg_chunk(_seg(p_ref, 0), _seg(p_ref, 1), _seg(p_ref, 2), _seg(p_ref, 3),
                               lg_ref[0:1, :], lg_ref[1:2, :], nw_ref[...], sts)
        y_ref[...] = y.astype(y_ref.dtype)
        for h in HEADS:
            st_out_ref[h] = sts[h]
            st_scr[h] = sts_new[h]

    return pl.pallas_call(
        body, name="hgrn2_fwd", grid=(nc,),
        in_specs=[pl.BlockSpec((rows, 4 * D_GROUP), lambda c: (c, 0)),
                  pl.BlockSpec((2, D_GROUP), lambda c: (0, 0)),
                  pl.BlockSpec((1, D_GROUP), lambda c: (0, 0))],
        out_specs=[pl.BlockSpec((rows, D_GROUP), lambda c: (c, 0)),
                   pl.BlockSpec((None, N_HEADS, D_HEAD, D_HEAD), lambda c: (c, 0, 0, 0))],
        out_shape=[jax.ShapeDtypeStruct((s, 2 * D_GROUP), BF16),
                   jax.ShapeDtypeStruct((nc, N_HEADS, D_HEAD, D_HEAD), F32)],
        scratch_shapes=[pltpu.VMEM((N_HEADS, D_HEAD, D_HEAD), F32)],
        compiler_params=_params(("arbitrary",)),
    )(proj, logits, norm_w)


def _hgrn2_bwd(proj, logits, norm_w, states, d_y):
    s = proj.shape[0]
    rows = CHUNKS_PER_STEP * CHUNK
    nc = s // rows

    def body(p_ref, lg_ref, nw_ref, st_ref, dy_ref, dp_ref, dl_ref, dnw_ref, dsum_ref, dst_scr):
        @pl.when(pl.program_id(0) == 0)
        def _():
            dst_scr[...] = jnp.zeros_like(dst_scr)
            dl_ref[...] = jnp.zeros_like(dl_ref)
            dnw_ref[...] = jnp.zeros_like(dnw_ref)
            dsum_ref[...] = jnp.zeros_like(dsum_ref)

        _, vjp = jax.vjp(_hg_chunk, _seg(p_ref, 0), _seg(p_ref, 1), _seg(p_ref, 2), _seg(p_ref, 3),
                         lg_ref[0:1, :], lg_ref[1:2, :], nw_ref[...], tuple(st_ref[h] for h in HEADS))
        d_hq, d_hf, d_hi, d_hg, d_l0, d_l1, d_nw, d_sts = vjp((dy_ref[...], tuple(dst_scr[h] for h in HEADS)))
        for seg, val in enumerate((d_hq, d_hf, d_hi, d_hg)):
            dp_ref[:, seg * D_GROUP:(seg + 1) * D_GROUP] = val.astype(dp_ref.dtype)
            dsum_ref[:, seg * D_GROUP:(seg + 1) * D_GROUP] += jnp.sum(val, axis=0, keepdims=True)
        dl_ref[0:1, :] += d_l0
        dl_ref[1:2, :] += d_l1
        dnw_ref[...] += d_nw
        for h in HEADS:
            dst_scr[h] = d_sts[h]

    rev = lambda c: nc - 1 - c
    return pl.pallas_call(
        body, name="hgrn2_bwd", grid=(nc,),
        in_specs=[pl.BlockSpec((rows, 4 * D_GROUP), lambda c: (rev(c), 0)),
                  pl.BlockSpec((2, D_GROUP), lambda c: (0, 0)),
                  pl.BlockSpec((1, D_GROUP), lambda c: (0, 0)),
                  pl.BlockSpec((None, N_HEADS, D_HEAD, D_HEAD), lambda c: (rev(c), 0, 0, 0)),
                  pl.BlockSpec((rows, D_GROUP), lambda c: (rev(c), 0))],
        out_specs=[pl.BlockSpec((rows, 4 * D_GROUP), lambda c: (rev(c), 0)),
                   pl.BlockSpec((2, D_GROUP), lambda c: (0, 0)),
                   pl.BlockSpec((1, D_GROUP), lambda c: (0, 0)),
                   pl.BlockSpec((1, 4 * D_GROUP), lambda c: (0, 0))],
        out_shape=[jax.ShapeDtypeStruct((s, D_IN_MAIN), BF16), jax.ShapeDtypeStruct((2, D_GROUP), F32),
                   jax.ShapeDtypeStruct((1, D_GROUP), F32), jax.ShapeDtypeStruct((1, 4 * D_GROUP), F32)],
        scratch_shapes=[pltpu.VMEM((N_HEADS, D_HEAD, D_HEAD), F32)],
        compiler_params=_params(("arbitrary",)),
    )(proj, logits, norm_w, states, d_y)


def _gate_column(gates, lane, idx):
    return jnp.sum(jnp.where(lane == idx, gates, 0.0), axis=1, keepdims=True)


def _head_layer_norm(h):
    mu = jnp.mean(h, axis=-1, keepdims=True)
    var = jnp.mean(jnp.square(h - mu), axis=-1, keepdims=True)
    return (h - mu) * lax.rsqrt(var + LN_EPS)


def _ml_chunk(qc, kc, v, mo, gates, nw, cts, ns, ms):
    n = qc.shape[0] // CHUNK
    row, col = _chunk_masks(CHUNK)
    mask = col <= row
    eye = col == row
    to_row = lambda t: jnp.sum(jnp.where(eye, t, 0.0), axis=0, keepdims=True)
    q = _blocks(qc * (D_HEAD ** -0.5))
    k = _blocks(kc)
    vs = _blocks(v)
    gate_rows = _split_chunks(gates)
    lane = lax.broadcasted_iota(jnp.int32, gate_rows[0].shape, 1)
    each = [(i, h) for i in range(n) for h in HEADS]
    on_each = lambda f: {ih: f(*ih) for ih in each}
    ig = on_each(lambda i, h: _gate_column(gate_rows[i], lane, h))
    lf = on_each(lambda i, h: jax.nn.log_sigmoid(_gate_column(gate_rows[i], lane, N_HEADS + h)))
    lf_row = on_each(lambda i, h: to_row(lf[i, h]))
    ig_row = on_each(lambda i, h: to_row(ig[i, h]))
    b_col = on_each(lambda i, h: jnp.sum(jnp.where(mask, lf_row[i, h], 0.0), axis=1, keepdims=True))
    b_row = on_each(lambda i, h: jnp.sum(jnp.where(row <= col, lf[i, h], 0.0), axis=0, keepdims=True))
    g = on_each(lambda i, h: jnp.sum(lf[i, h], axis=0, keepdims=True))
    d = on_each(lambda i, h: jnp.where(mask, b_col[i, h] - b_row[i, h] + ig_row[i, h], -jnp.inf))
    a = on_each(lambda i, h: g[i, h] - b_col[i, h] + ig[i, h])
    m_at = {(0, h): ms[h] for h in HEADS}
    for i, h in each:
        m_at[i + 1, h] = lax.stop_gradient(jnp.maximum(g[i, h] + m_at[i, h], jnp.max(a[i, h], axis=0, keepdims=True)))
    inter = on_each(lambda i, h: b_col[i, h] + m_at[i, h])
    m_t = on_each(lambda i, h: lax.stop_gradient(jnp.maximum(inter[i, h], jnp.max(d[i, h], axis=1, keepdims=True))))
    qk = on_each(lambda i, h: _nt(q[i][h], k[i][h]))
    sc = on_each(lambda i, h: qk[i, h] * jnp.exp(d[i, h] - m_t[i, h]))
    w_inter = on_each(lambda i, h: jnp.exp(inter[i, h] - m_t[i, h]))
    sv = on_each(lambda i, h: _nn(sc[i, h], vs[i][h]))
    decay = on_each(lambda i, h: jnp.exp(g[i, h] + m_at[i, h] - m_at[i + 1, h]))
    wk = on_each(lambda i, h: k[i][h] * jnp.exp(a[i, h] - m_at[i + 1, h]))
    kv = on_each(lambda i, h: _tn(vs[i][h], wk[i, h]))
    normed = []
    for i in range(n):
        qc_state = [_nt(q[i][h], cts[h]) for h in HEADS]
        num = [sv[i, h] + w_inter[i, h] * qc_state[h] for h in HEADS]
        den = [jnp.sum(sc[i, h], axis=1, keepdims=True)
               + w_inter[i, h] * jnp.sum(q[i][h] * ns[h], axis=1, keepdims=True) for h in HEADS]
        hh = [num[h] / jnp.maximum(jnp.abs(den[h]), jnp.exp(-m_t[i, h])) for h in HEADS]
        cts = tuple(decay[i, h] * cts[h] + kv[i, h] for h in HEADS)
        ns = tuple(decay[i, h] * ns[h] + jnp.sum(wk[i, h], axis=0, keepdims=True) for h in HEADS)
        normed.append(_merge_heads(tuple(_head_layer_norm(hh[h]) for h in HEADS)))
    y = jax.nn.sigmoid(mo) * (_merge_chunks(tuple(normed)) * nw)
    return y, cts, ns, tuple(m_at[n, h] for h in HEADS)


def _mlstm_fwd(qk, proj, gates, norm_w, y):
    s = proj.shape[0]
    rows = ML_CHUNKS_PER_STEP * CHUNK
    nc = s // rows

    def body(qk_ref, vo_ref, g_ref, nw_ref, _, y_ref, ct_out, n_out, m_out, ct_scr, n_scr, m_scr):
        @pl.when(pl.program_id(0) == 0)
        def _():
            ct_scr[...] = jnp.zeros_like(ct_scr)
            n_scr[...] = jnp.zeros_like(n_scr)
            m_scr[...] = jnp.full(m_scr.shape, NEG_BIG, F32)

        cts = tuple(ct_scr[h] for h in HEADS)
        ns = tuple(n_scr[h] for h in HEADS)
        ms = tuple(m_scr[h] for h in HEADS)
        y, cts_new, ns_new, ms_new = _ml_chunk(_seg(qk_ref, 0), _seg(qk_ref, 1), _seg(vo_ref, 0), _seg(vo_ref, 1),
                                               g_ref[...], nw_ref[...], cts, ns, ms)
        y_ref[...] = y.astype(y_ref.dtype)
        for h in HEADS:
            ct_out[h], n_out[h], m_out[h] = cts[h], ns[h], ms[h]
            ct_scr[h], n_scr[h], m_scr[h] = cts_new[h], ns_new[h], ms_new[h]

    st = lambda r, w: pl.BlockSpec((None, N_HEADS, r, w), lambda c: (c, 0, 0, 0))
    return pl.pallas_call(
        body, name="mlstm_fwd", grid=(nc,),
        in_specs=[pl.BlockSpec((rows, 2 * D_GROUP), lambda c: (c, 0)),
                  pl.BlockSpec((rows, 2 * D_GROUP), lambda c: (c, VO_BLOCK)),
                  pl.BlockSpec((rows, LANES), lambda c: (c, 0)),
                  pl.BlockSpec((1, D_GROUP), lambda c: (0, 0)),
                  pl.BlockSpec(memory_space=pl.ANY)],
        out_specs=[pl.BlockSpec((rows, D_GROUP), lambda c: (c, 1)),
                   st(D_HEAD, D_HEAD), st(1, D_HEAD), st(1, 1)],
        out_shape=[jax.ShapeDtypeStruct(y.shape, y.dtype),
                   jax.ShapeDtypeStruct((nc, N_HEADS, D_HEAD, D_HEAD), F32),
                   jax.ShapeDtypeStruct((nc, N_HEADS, 1, D_HEAD), F32),
                   jax.ShapeDtypeStruct((nc, N_HEADS, 1, 1), F32)],
        input_output_aliases={4: 0},
        scratch_shapes=[pltpu.VMEM((N_HEADS, D_HEAD, D_HEAD), F32), pltpu.VMEM((N_HEADS, 1, D_HEAD), F32),
                        pltpu.VMEM((N_HEADS, 1, 1), F32)],
        compiler_params=_params(("arbitrary",)),
    )(qk, proj, gates, norm_w, y)


def _mlstm_bwd(qk, proj, gates, norm_w, ct_s, n_s, m_s, d_y, d_proj):
    s = proj.shape[0]
    rows = ML_CHUNKS_PER_STEP * CHUNK
    nc = s // rows

    def body(qk_ref, vo_ref, g_ref, nw_ref, ct_ref, n_ref, m_ref, dy_ref, _,
             dp_ref, dqk_ref, dg_ref, dnw_ref, dsum_ref, dct_scr, dn_scr):
        @pl.when(pl.program_id(0) == 0)
        def _():
            dct_scr[...] = jnp.zeros_like(dct_scr)
            dn_scr[...] = jnp.zeros_like(dn_scr)
            dnw_ref[...] = jnp.zeros_like(dnw_ref)
            dsum_ref[...] = jnp.zeros_like(dsum_ref)

        ms = tuple(m_ref[h] for h in HEADS)
        step = lambda *a: _ml_chunk(*a, ms)[:3]
        _, vjp = jax.vjp(step, _seg(qk_ref, 0), _seg(qk_ref, 1), _seg(vo_ref, 0), _seg(vo_ref, 1), g_ref[...],
                         nw_ref[...], tuple(ct_ref[h] for h in HEADS), tuple(n_ref[h] for h in HEADS))
        d_q, d_k, d_v, d_o, d_gates, d_nw, d_cts, d_ns = vjp(
            (dy_ref[...], tuple(dct_scr[h] for h in HEADS), tuple(dn_scr[h] for h in HEADS)))
        dqk_ref[:, 0:D_GROUP] = d_q
        dqk_ref[:, D_GROUP:2 * D_GROUP] = d_k
        for seg, val in enumerate((d_v, d_o)):
            dp_ref[:, seg * D_GROUP:(seg + 1) * D_GROUP] = val.astype(dp_ref.dtype)
            dsum_ref[:, seg * D_GROUP:(seg + 1) * D_GROUP] += jnp.sum(val, axis=0, keepdims=True)
        dg_ref[...] = d_gates
        dnw_ref[...] += d_nw
        for h in HEADS:
            dct_scr[h] = d_cts[h]
            dn_scr[h] = d_ns[h]

    rev = lambda c: nc - 1 - c
    st = lambda r, w: pl.BlockSpec((None, N_HEADS, r, w), lambda c: (rev(c), 0, 0, 0))
    return pl.pallas_call(
        body, name="mlstm_bwd", grid=(nc,),
        in_specs=[pl.BlockSpec((rows, 2 * D_GROUP), lambda c: (rev(c), 0)),
                  pl.BlockSpec((rows, 2 * D_GROUP), lambda c: (rev(c), VO_BLOCK)),
                  pl.BlockSpec((rows, LANES), lambda c: (rev(c), 0)),
                  pl.BlockSpec((1, D_GROUP), lambda c: (0, 0)),
                  st(D_HEAD, D_HEAD), st(1, D_HEAD), st(1, 1),
                  pl.BlockSpec((rows, D_GROUP), lambda c: (rev(c), 1)),
                  pl.BlockSpec(memory_space=pl.ANY)],
        out_specs=[pl.BlockSpec((rows, 2 * D_GROUP), lambda c: (rev(c), VO_BLOCK)),
                   pl.BlockSpec((rows, 2 * D_GROUP), lambda c: (rev(c), 0)),
                   pl.BlockSpec((rows, LANES), lambda c: (rev(c), 0)),
                   pl.BlockSpec((1, D_GROUP), lambda c: (0, 0)),
                   pl.BlockSpec((1, 2 * D_GROUP), lambda c: (0, 0))],
        out_shape=[jax.ShapeDtypeStruct(d_proj.shape, d_proj.dtype), jax.ShapeDtypeStruct((s, 2 * D_GROUP), F32),
                   jax.ShapeDtypeStruct((s, LANES), F32), jax.ShapeDtypeStruct((1, D_GROUP), F32),
                   jax.ShapeDtypeStruct((1, 2 * D_GROUP), F32)],
        input_output_aliases={8: 0},
        scratch_shapes=[pltpu.VMEM((N_HEADS, D_HEAD, D_HEAD), F32), pltpu.VMEM((N_HEADS, 1, D_HEAD), F32)],
        compiler_params=_params(("arbitrary",)),
    )(qk, proj, gates, norm_w, ct_s, n_s, m_s, d_y, d_proj)


LN_TOKENS = 512
ATT_TOKENS = 512


def _proj_res_ln(a, w, xres, g, b, name):
    s, dm = xres.shape
    k = a.shape[1]
    tb = min(LN_TOKENS, s)

    def body(a_ref, w_ref, x_ref, g_ref, b_ref, z_ref, o_ref):
        halves = [slice(0, tb // 2), slice(tb // 2, tb)]
        zs = [ALPHA * x_ref[rows, :] + _nn_raw(a_ref[rows, :], w_ref[...]) for rows in halves]
        for rows, z in zip(halves, zs):
            z_ref[rows, :] = z
            o_ref[rows, :] = _layer_norm(z, g_ref[...], b_ref[...])

    tok = pl.BlockSpec((tb, dm), lambda i: (i, 0))
    vec = pl.BlockSpec((1, dm), lambda i: (0, 0))
    act = jax.ShapeDtypeStruct((s, dm), F32)
    return pl.pallas_call(
        body, name=name, grid=(s // tb,),
        in_specs=[pl.BlockSpec((tb, k), lambda i: (i, 0)), pl.BlockSpec((k, dm), lambda i: (0, 0)), tok, vec, vec],
        out_specs=[tok, tok], out_shape=[act, act], compiler_params=_params(("parallel",)),
    )(a, w, xres, g, b)


def _ln_bwd_proj(d_out, z, g, b, w, name):
    s, dm = z.shape
    k = w.shape[0]
    tb = min(LN_TOKENS, s)

    def body(do_ref, z_ref, g_ref, b_ref, w_ref, dz_ref, da_ref, dg_ref, db_ref):
        @pl.when(pl.program_id(0) == 0)
        def _():
            dg_ref[...] = jnp.zeros_like(dg_ref)
            db_ref[...] = jnp.zeros_like(db_ref)

        halves = [slice(0, tb // 2), slice(tb // 2, tb)]
        d_zs = []
        for rows in halves:
            _, vjp = jax.vjp(_layer_norm, z_ref[rows, :], g_ref[...], b_ref[...])
            d_z, d_g, d_b = vjp(do_ref[rows, :])
            dz_ref[rows, :] = d_z
            dg_ref[...] += d_g
            db_ref[...] += d_b
            d_zs.append(d_z)
        for rows, d_z in zip(halves, d_zs):
            da_ref[rows, :] = _nt_raw(d_z, w_ref[...])

    tok = pl.BlockSpec((tb, dm), lambda i: (i, 0))
    vec = pl.BlockSpec((1, dm), lambda i: (0, 0))
    return pl.pallas_call(
        body, name=name, grid=(s // tb,),
        in_specs=[tok, tok, vec, vec, pl.BlockSpec((k, dm), lambda i: (0, 0))],
        out_specs=[tok, pl.BlockSpec((tb, k), lambda i: (i, 0)), vec, vec],
        out_shape=[jax.ShapeDtypeStruct((s, dm), F32), jax.ShapeDtypeStruct((s, k), F32),
                   jax.ShapeDtypeStruct((1, dm), F32), jax.ShapeDtypeStruct((1, dm), F32)],
        compiler_params=_params(("arbitrary",)),
    )(d_out, z, g, b, w)


def _proj_loss_tail(a, w, xres, g, b, target):
    s, dm = xres.shape
    k = a.shape[1]
    tb = min(ATT_TOKENS, s)

    def loss_fn(z, gg, bb, tgt):
        err = jnp.square(_layer_norm(z, gg, bb) - tgt)
        return 0.5 * jnp.sum(jnp.mean(err, axis=-1, keepdims=True), axis=0, keepdims=True)

    def body(a_ref, w_ref, x_ref, g_ref, b_ref, t_ref, loss_ref, dz_ref, dg_ref, db_ref):
        @pl.when(pl.program_id(0) == 0)
        def _():
            loss_ref[...] = jnp.zeros_like(loss_ref)
            dg_ref[...] = jnp.zeros_like(dg_ref)
            db_ref[...] = jnp.zeros_like(db_ref)

        halves = [slice(0, tb // 2), slice(tb // 2, tb)]
        zs = [ALPHA * x_ref[rows, :] + _nn_raw(a_ref[rows, :], w_ref[...]) for rows in halves]
        for rows, z in zip(halves, zs):
            tgt = t_ref[rows, :]
            loss, vjp = jax.vjp(lambda zz, gg, bb, tgt=tgt: loss_fn(zz, gg, bb, tgt), z, g_ref[...], b_ref[...])
            d_z, d_g, d_b = vjp(jnp.ones((1, 1), F32))
            loss_ref[...] += loss
            dz_ref[rows, :] = d_z
            dg_ref[...] += d_g
            db_ref[...] += d_b

    tok = pl.BlockSpec((tb, dm), lambda i: (i, 0))
    vec = pl.BlockSpec((1, dm), lambda i: (0, 0))
    one = pl.BlockSpec((1, 1), lambda i: (0, 0))
    return pl.pallas_call(
        body, name="ffn_down_loss_tail", grid=(s // tb,),
        in_specs=[pl.BlockSpec((tb, k), lambda i: (i, 0)), pl.BlockSpec((k, dm), lambda i: (0, 0)), tok, vec, vec, tok],
        out_specs=[one, tok, vec, vec],
        out_shape=[jax.ShapeDtypeStruct((1, 1), F32), jax.ShapeDtypeStruct((s, dm), F32),
                   jax.ShapeDtypeStruct((1, dm), F32), jax.ShapeDtypeStruct((1, dm), F32)],
        compiler_params=_params(("arbitrary",)),
    )(a, w, xres, g, b, target)


def _att_heads(qs, ks, vs):
    sc = [_nt(q, k) * (CA_DH ** -0.5) for q, k in zip(qs, ks)]
    p = [jax.nn.softmax(s, axis=-1) for s in sc]
    return tuple(_nn(pp, v) for pp, v in zip(p, vs))


def _head_slices(ref_or_value, offset):
    return tuple(ref_or_value[:, offset + h * CA_DH:offset + (h + 1) * CA_DH] for h in range(CA_HEADS))


def _cross_attention_fwd(x1, kv, wq, wo, g, b):
    s = x1.shape[0]
    tb = min(ATT_TOKENS, s)

    def body(x_ref, kv_ref, wq_ref, wo_ref, g_ref, b_ref, att_ref, z_ref, o_ref):
        x_blk = x_ref[...]
        q = _nn_raw(x_blk, wq_ref[...])
        att = jnp.concatenate(_att_heads(_head_slices(q, 0), _head_slices(kv_ref, 0), _head_slices(kv_ref, D_MODEL)),
                              axis=1)
        att_ref[...] = att.astype(att_ref.dtype)
        z = ALPHA * x_blk + _nn_raw(att, wo_ref[...])
        z_ref[...] = z
        o_ref[...] = _layer_norm(z, g_ref[...], b_ref[...])

    tok = pl.BlockSpec((tb, D_MODEL), lambda i: (i, 0))
    mat = pl.BlockSpec((D_MODEL, D_MODEL), lambda i: (0, 0))
    vec = pl.BlockSpec((1, D_MODEL), lambda i: (0, 0))
    act = jax.ShapeDtypeStruct((s, D_MODEL), F32)
    return pl.pallas_call(
        body, name="cross_attention_fwd", grid=(s // tb,),
        in_specs=[tok, pl.BlockSpec((N_MEM, 2 * D_MODEL), lambda i: (0, 0)), mat, mat, vec, vec],
        out_specs=[tok, tok, tok],
        out_shape=[jax.ShapeDtypeStruct((s, D_MODEL), BF16), act, act],
        compiler_params=_params(("parallel",)),
    )(x1, kv, wq, wo, g, b)


def _cross_attention_bwd(d_x2, x1, z2, kv, wq, wo, g, b):
    s = x1.shape[0]
    tb = min(ATT_TOKENS, s)

    def body(dx2_ref, x_ref, z_ref, kv_ref, wq_ref, wo_ref, g_ref, b_ref,
             dx1_ref, dq_ref, dz_ref, dkv_ref, dg_ref, db_ref):
        @pl.when(pl.program_id(0) == 0)
        def _():
            dkv_ref[...] = jnp.zeros_like(dkv_ref)
            dg_ref[...] = jnp.zeros_like(dg_ref)
            db_ref[...] = jnp.zeros_like(db_ref)

        q = _nn_raw(x_ref[...], wq_ref[...])
        _, ln_vjp = jax.vjp(_layer_norm, z_ref[...], g_ref[...], b_ref[...])
        d_z, d_g, d_b = ln_vjp(dx2_ref[...])
        dg_ref[...] += d_g
        db_ref[...] += d_b
        dz_ref[...] = d_z.astype(dz_ref.dtype)
        d_att = _nt_raw(d_z, wo_ref[...])
        _, vjp = jax.vjp(_att_heads, _head_slices(q, 0), _head_slices(kv_ref, 0), _head_slices(kv_ref, D_MODEL))
        d_qs, d_ks, d_vs = vjp(_head_slices(d_att, 0))
        for h in range(CA_HEADS):
            lo = h * CA_DH
            dkv_ref[:, lo:lo + CA_DH] += d_ks[h]
            dkv_ref[:, D_MODEL + lo:D_MODEL + lo + CA_DH] += d_vs[h]
        d_q = jnp.concatenate(d_qs, axis=1)
        dq_ref[...] = d_q.astype(dq_ref.dtype)
        dx1_ref[...] = ALPHA * d_z + _nt_raw(d_q, wq_ref[...])

    tok = pl.BlockSpec((tb, D_MODEL), lambda i: (i, 0))
    mem = pl.BlockSpec((N_MEM, 2 * D_MODEL), lambda i: (0, 0))
    mat = pl.BlockSpec((D_MODEL, D_MODEL), lambda i: (0, 0))
    vec = pl.BlockSpec((1, D_MODEL), lambda i: (0, 0))
    low = jax.ShapeDtypeStruct((s, D_MODEL), BF16)
    return pl.pallas_call(
        body, name="cross_attention_bwd", grid=(s // tb,),
        in_specs=[tok, tok, tok, mem, mat, mat, vec, vec], out_specs=[tok, tok, tok, mem, vec, vec],
        out_shape=[jax.ShapeDtypeStruct((s, D_MODEL), F32), low, low,
                   jax.ShapeDtypeStruct((N_MEM, 2 * D_MODEL), F32),
                   jax.ShapeDtypeStruct((1, D_MODEL), F32), jax.ShapeDtypeStruct((1, D_MODEL), F32)],
        compiler_params=_params(("arbitrary",)),
    )(d_x2, x1, z2, kv, wq, wo, g, b)


def _local_step(x, mem, target, w, mid_weights=None, ffn_weights=None, down_weights=None, on_ffn_grads=None,
                on_mid_grads=None,
                on_small_grads=None, on_last_grads=None):
    w = dict(w)
    s = x.shape[0]
    tm = min(512, s)
    tt_big = min(1024, s)
    proj, gates = _input_projection(x, w["w_in_t"], w["w_in_gate_t"], w["b_in_main"], w["b_in_gate"],
                                    min(2048, s), 512)
    qk = _ml_conv_fwd(proj, w["ml_conv_w"], w["ml_conv_b"])
    y, hg_states = _hgrn2_fwd(proj, w["hg_lb_logits"], w["hg_norm_w"])
    y, ct_s, n_s, m_s = _mlstm_fwd(qk, proj, gates, w["ml_norm_w"], y)
    if mid_weights is not None:
        w.update(mid_weights(y))
    z1, x1 = _proj_res_ln(y, w["w_out"], x, w["ln1_g"], w["ln1_b"], "out_proj_ln1")
    kv = _matmul_nn(mem, w["ca_wkv"], None, N_MEM, CA_DH, "kv")
    att, z2, x2 = _cross_attention_fwd(x1, kv, w["ca_wq"], w["ca_wo"], w["ln2_g"], w["ln2_b"])
    if ffn_weights is not None:
        w.update(ffn_weights(x2))
    u = _matmul_nt(x2, w["ffn_w_up_t"], min(1024, s), UP_TILE, "ffn_up", out_dtype=BF16)
    hid = _ffn_conv_fwd(u, w["ffn_conv_w"], w["ffn_conv_b"])
    if down_weights is not None:
        w.update(down_weights(hid))
    loss, d_z3, d_ln3_g, d_ln3_b = _proj_loss_tail(hid, w["ffn_w_down"], x2, w["ln3_g"], w["ln3_b"], target)
    grads = {"ln3_g": d_ln3_g, "ln3_b": d_ln3_b}
    grads["ffn_w_down"] = _matmul_tn(hid, d_z3, UP_TILE, D_MODEL, tt_big, "d_w_down")
    d_hid = _matmul_nt(d_z3, w["ffn_w_down"], tm, D_FF, "d_hid", out_dtype=BF16)
    d_ug, d_uv, d_cwg, d_cwv, d_cbg, d_cbv = _ffn_conv_bwd(u, w["ffn_conv_w"], w["ffn_conv_b"], d_hid)
    grads["ffn_conv_w"] = jnp.concatenate([d_cwg, d_cwv], axis=-1)
    grads["ffn_conv_b"] = jnp.concatenate([d_cbg, d_cbv], axis=-1)
    d_w_up = _matmul_tn(d_ug, x2, UP_TILE, D_MODEL, tt_big, "d_w_up_gate", rows=D_UP)
    grads["ffn_w_up"] = _matmul_tn(d_uv, x2, UP_TILE, D_MODEL, tt_big, "d_w_up_val", rows=D_UP, row0=D_FF,
                                   into=d_w_up)
    d_x2 = _matmul_nn_sum([(d_ug, w["ffn_w_up_t"], 0), (d_uv, w["ffn_w_up_t"], D_FF)], d_z3, ALPHA,
                          min(256, s), "d_x2")
    if on_ffn_grads is not None:
        d_x2 = on_ffn_grads(grads, d_x2)
    d_x1, d_q, d_z2, d_kv, grads["ln2_g"], grads["ln2_b"] = _cross_attention_bwd(
        d_x2, x1, z2, kv, w["ca_wq"], w["ca_wo"], w["ln2_g"], w["ln2_b"])
    grads["ca_wo"] = _matmul_tn(att, d_z2, D_MODEL, D_MODEL, tt_big, "d_ca_wo")
    grads["ca_wq"] = _matmul_tn(x1, d_q, D_MODEL, D_MODEL, tt_big, "d_ca_wq")
    grads["ca_wkv"] = _matmul_tn(mem, d_kv, D_MODEL, CA_DH, N_MEM, "d_ca_wkv", shards=N_DEV, group=N_DEV)
    d_z1, d_y, grads["ln1_g"], grads["ln1_b"] = _ln_bwd_proj(d_x1, z1, w["ln1_g"], w["ln1_b"], w["w_out"],
                                                             "ln1_bwd_out_proj")
    grads["w_out"] = _matmul_tn(y, d_z1, D_MODEL, D_MODEL, tt_big, "d_w_out")
    if on_mid_grads is not None:
        d_y = on_mid_grads(grads, d_y)
    d_proj, grads["hg_lb_logits"], grads["hg_norm_w"], db_hg = _hgrn2_bwd(
        proj, w["hg_lb_logits"], w["hg_norm_w"], hg_states, d_y)
    d_proj, d_qk, d_gates, grads["ml_norm_w"], db_vo = _mlstm_bwd(
        qk, proj, gates, w["ml_norm_w"], ct_s, n_s, m_s, d_y, d_proj)
    d_proj, grads["ml_conv_w"], grads["ml_conv_b"], db_qk = _ml_conv_bwd(
        proj, w["ml_conv_w"], w["ml_conv_b"], d_qk, d_proj)
    grads["b_in_main"] = jnp.concatenate([db_hg, db_qk, db_vo], axis=-1)
    grads["b_in_gate"] = jnp.sum(d_gates, axis=0, keepdims=True)
    if on_small_grads is not None:
        d_proj = on_small_grads(grads, loss, d_proj)
    grads["w_in_main"], grads["w_in_gate"] = _input_projection_grads(d_proj, d_gates, x, min(2048, D_IN_MAIN), tt_big)
    if on_last_grads is not None:
        d_z1 = on_last_grads(grads, d_z1)
    grad_x = _matmul_nn_sum([(d_proj, w["w_in_t"], 0), (d_gates, w["w_in_gate_t"], 0)], d_z1, ALPHA, tm, "d_x")
    return loss, grad_x, grads


HBM_SPEC = pl.BlockSpec(memory_space=pltpu.HBM)


def _coords():
    return lax.axis_index("x"), lax.axis_index("y"), lax.axis_index("c")


def _other_chips(x, y):
    return [(1 - x, y), (x, 1 - y), (1 - x, 1 - y)]


def _my_slot():
    x, y, c = _coords()
    return 4 * x + 2 * y + c


SEM_SPEC = pl.BlockSpec(memory_space=pltpu.SEMAPHORE)
ANY_SPEC = pl.BlockSpec(memory_space=pl.ANY)
SIDE_EFFECT = pltpu.SideEffectType.DATAFLOW_SIDE_EFFECTING


def _peer(x, y, c, d):
    flip = lambda v, bit: 1 - v if bit else v
    p = (flip(x, d & 4), flip(y, d & 2), flip(c, d & 1))
    return p, 4 * p[0] + 2 * p[1] + p[2]


def _direct_copies(gather, src_refs, land_refs, send_sems, recv_sems):
    x, y, c = _coords()
    me = 4 * x + 2 * y + c
    copies = []
    for a in range(len(src_refs)):
        for d in range(1, N_DEV):
            peer, peer_slot = _peer(x, y, c, d)
            copies.append(pltpu.make_async_remote_copy(
                src_ref=src_refs[a] if gather else src_refs[a].at[peer_slot],
                dst_ref=land_refs[a].at[me] if gather else land_refs[a].at[d - 1],
                send_sem=send_sems.at[7 * a + d - 1], recv_sem=recv_sems.at[7 * a + d - 1],
                device_id=peer, device_id_type=MESH))
    return copies


def _hbm(t):
    return pltpu.HBM(t.shape, t.dtype)


def _chip_copies(src_refs, land_refs, send_sems, recv_sems):
    x, y, c = _coords()
    me = 4 * x + 2 * y + c
    targets = [(x, y, 1 - c)] + [(cx, cy, c) for cx, cy in _other_chips(x, y)]
    return [pltpu.make_async_remote_copy(
        src_ref=src_refs[a], dst_ref=land_refs[a].at[me], send_sem=send_sems.at[4 * a + k],
        recv_sem=recv_sems.at[4 * a + k], device_id=target, device_id_type=MESH)
        for a in range(len(src_refs)) for k, target in enumerate(targets)]


def _forward_copies(land_refs, send_sems, recv_sems):
    x, y, c = _coords()
    return [pltpu.make_async_remote_copy(
        src_ref=land_refs[a].at[4 * cx + 2 * cy + c], dst_ref=land_refs[a].at[4 * cx + 2 * cy + c],
        send_sem=send_sems.at[3 * a + j], recv_sem=recv_sems.at[3 * a + j],
        device_id=(x, y, 1 - c), device_id_type=MESH)
        for a in range(len(land_refs)) for j, (cx, cy) in enumerate(_other_chips(x, y))]


def _split_copy_start(make_copies, n_sems, operands, through, name):
    n_ops = len(operands)

    def body(*refs):
        for cp in make_copies(refs[:n_ops], refs[n_ops + 1], refs[n_ops + 2]):
            cp.start()

    ins = [pltpu.with_memory_space_constraint(t, pltpu.HBM) for t in (*operands, through)]
    sems = pltpu.SemaphoreType.DMA((n_sems,))
    res = pl.pallas_call(
        body, name=name, out_shape=(sems, sems, *[_hbm(t) for t in ins]),
        in_specs=[HBM_SPEC] * (n_ops + 1), out_specs=(SEM_SPEC, SEM_SPEC, *[HBM_SPEC] * (n_ops + 1)),
        input_output_aliases={i: 2 + i for i in range(n_ops + 1)},
        compiler_params=pltpu.CompilerParams(has_side_effects=SIDE_EFFECT),
    )(*ins)
    return (res[0], res[1], list(res[2:2 + n_ops])), res[2 + n_ops]


def _split_copy_wait(make_copies, started, after, name):
    send_sems, recv_sems, operands = started
    n_ops = len(operands)
    after = list(after) if isinstance(after, (list, tuple)) else [after]

    def body(*refs):
        for cp in make_copies(refs[:n_ops], refs[n_ops], refs[n_ops + 1]):
            cp.wait_send()
            cp.wait_recv()

    res = pl.pallas_call(
        body, name=name, out_shape=tuple(_hbm(t) for t in operands),
        in_specs=[HBM_SPEC] * n_ops + [SEM_SPEC, SEM_SPEC] + [ANY_SPEC] * len(after),
        out_specs=tuple([HBM_SPEC] * n_ops), input_output_aliases={i: i for i in range(n_ops)},
        compiler_params=pltpu.CompilerParams(has_side_effects=SIDE_EFFECT),
    )(*operands, send_sems, recv_sems, *after)
    return list(res)


def _halves(make_copies, na):
    return lambda refs, send_sems, recv_sems: make_copies(refs[:na], refs[na:], send_sems, recv_sems)


def _direct_start(gather, arrays, through, name):
    na = len(arrays)
    lands = [lax.empty((N_DEV,) + t.shape if gather else (N_DEV - 1,) + t.shape[1:], t.dtype) for t in arrays]
    return _split_copy_start(_halves(functools.partial(_direct_copies, gather), na), 7 * na, [*arrays, *lands],
                             through, name)


def _direct_wait(gather, started, after, name):
    na = len(started[2]) // 2
    operands = _split_copy_wait(_halves(functools.partial(_direct_copies, gather), na), started, after, name)
    return operands[:na], operands[na:]


def _two_level_gather(shards, glue, name):
    na = len(shards)
    lands = [lax.empty((N_DEV,) + t.shape, t.dtype) for t in shards]
    nothing = jnp.zeros((SUBLANES, LANES), F32)
    started, _ = _split_copy_start(_halves(_chip_copies, na), 4 * na, [*shards, *lands], nothing, name + "_start")
    operands = _split_copy_wait(_halves(_chip_copies, na), started, glue, name + "_wait")
    started, mine = _split_copy_start(_forward_copies, 3 * na, operands[na:], operands[0], name + "_forward_start")
    lands = _split_copy_wait(_forward_copies, started, mine, name + "_forward_wait")
    return [lax.dynamic_update_index_in_dim(land, own, _my_slot(), 0)
            for own, land in zip([mine, *operands[1:na]], lands)]


def _row_tile(rows):
    for t in (256, 176, 128):
        if rows % t == 0 and rows > t:
            return t
    return rows


def _adamw_math(g, w, m, v):
    m_new = ADAM_B1 * m + (1.0 - ADAM_B1) * g
    v_new = ADAM_B2 * v + (1.0 - ADAM_B2) * jnp.square(g)
    m_hat = m_new / (1.0 - ADAM_B1 ** ADAM_STEP)
    v_hat = v_new / (1.0 - ADAM_B2 ** ADAM_STEP)
    delta = -ADAM_LR * (m_hat / (jnp.sqrt(v_hat) + ADAM_EPS) + ADAM_WD * w)
    return delta, m_new, v_new


def _adamw_sharded(chip, sums, got, w, m, v, name):
    r, c = w.shape
    tr = _row_tile(r)
    n_got = got.shape[0]

    def body(chip_ref, s_ref, g_ref, w_ref, m_ref, v_ref, go_ref, d_ref, nm_ref, nv_ref):
        g = s_ref[...].astype(F32)
        for i in range(n_got):
            g = g + g_ref[i].astype(F32)
        delta, m_new, v_new = _adamw_math(g, w_ref[...], m_ref[...], v_ref[...])
        go_ref[...] = g
        d_ref[...] = delta
        nm_ref[...] = m_new
        nv_ref[...] = v_new

    blk = pl.BlockSpec((tr, c), lambda i, chip_ref: (i, 0))
    out = jax.ShapeDtypeStruct((r, c), F32)
    return pl.pallas_call(
        body, name=name,
        grid_spec=pltpu.PrefetchScalarGridSpec(
            num_scalar_prefetch=1, grid=(r // tr,),
            in_specs=[pl.BlockSpec((None, tr, c), lambda i, chip_ref: (chip_ref[0], i, 0)),
                      pl.BlockSpec((n_got, tr, c), lambda i, chip_ref: (0, i, 0)), blk, blk, blk],
            out_specs=[blk, blk, blk, blk]),
        out_shape=[out, out, out, out],
        compiler_params=_params(("parallel",)),
    )(chip, sums, got, w, m, v)


def _adamw_replicated(parts, w, m, v):
    p, r, c = parts.shape
    names = SMALL_NAMES
    shapes = [w[n].shape for n in names]

    def body(*refs):
        p_ref = refs[0]
        ins = refs[1:1 + 3 * len(names)]
        outs = refs[1 + 3 * len(names):-2]
        loss_ref, sum_scr = refs[-2], refs[-1]
        total = p_ref[0]
        for i in range(1, p):
            total = total + p_ref[i]
        sum_scr[...] = total
        for k, n in enumerate(names):
            w_ref, m_ref, v_ref = ins[3 * k:3 * k + 3]
            g_ref, d_ref, nm_ref, nv_ref = outs[4 * k:4 * k + 4]
            for row, lane0, width, src_row in _small_pieces(n, shapes[k]):
                here = (slice(row, row + 1), slice(lane0, lane0 + width))
                g = sum_scr[src_row:src_row + 1, 0:width]
                delta, m_new, v_new = _adamw_math(g, w_ref[here], m_ref[here], v_ref[here])
                g_ref[here] = g
                d_ref[here] = delta
                nm_ref[here] = m_new
                nv_ref[here] = v_new
        loss_ref[...] = sum_scr[SMALL_LOSS_ROW:SMALL_LOSS_ROW + 1, 0:1]

    whole = lambda shape: pl.BlockSpec(shape, lambda i: (0,) * len(shape))
    args = [parts] + [t[n] for n in names for t in (w, m, v)]
    out_shape = [jax.ShapeDtypeStruct(s, F32) for s in shapes for _ in range(4)] + [jax.ShapeDtypeStruct((1, 1), F32)]
    res = pl.pallas_call(
        body, name="adamw_replicated", grid=(1,),
        in_specs=[whole(t.shape) for t in args], out_specs=[whole(s.shape) for s in out_shape],
        out_shape=out_shape, scratch_shapes=[pltpu.VMEM((r, c), F32)],
        compiler_params=_params(("arbitrary",)),
    )(*args)
    results = [{n: res[4 * k + j] for k, n in enumerate(names)} for j in range(4)]
    return results, res[-1]


SHARDED_NAMES = ("w_in", "ml_conv_w", "w_out", "ca_wq", "ca_wkv", "ca_wo", "ffn_w_up", "ffn_conv_w", "ffn_w_down")
SMALL_NAMES = ("b_in", "hg_lb_logits", "hg_norm_w", "ml_conv_b", "ml_norm_w", "ln1_g", "ln1_b",
               "ln2_g", "ln2_b", "ffn_conv_b", "ln3_g", "ln3_b")
WEIGHT_NAMES = ("w_in", "b_in", "hg_lb_logits", "hg_norm_w", "ml_conv_w", "ml_conv_b", "ml_norm_w", "w_out",
                "ln1_g", "ln1_b", "ca_wq", "ca_wkv", "ca_wo", "ln2_g", "ln2_b", "ffn_w_up", "ffn_conv_w",
                "ffn_conv_b", "ffn_w_down", "ln3_g", "ln3_b")
PAD_TO = {"ffn_conv_w": UP_SHARD_P}
SMALL_ROWS = 24
SMALL_W = D_MODEL
SMALL_SHAPES = {"b_in": (1, D_IN), "hg_lb_logits": (2, D_GROUP), "hg_norm_w": (1, D_GROUP),
                "ml_conv_b": (1, 2 * D_GROUP), "ml_norm_w": (1, D_GROUP), "ln1_g": (1, D_MODEL), "ln1_b": (1, D_MODEL),
                "ln2_g": (1, D_MODEL), "ln2_b": (1, D_MODEL), "ffn_conv_b": (1, D_UP), "ln3_g": (1, D_MODEL),
                "ln3_b": (1, D_MODEL)}


def _shard_2d(name, block):
    t = block[0]
    if name in PAD_TO:
        t = jnp.pad(t, ((0, 0), (0, PAD_TO[name] - t.shape[1])))
    return t


TRANSPOSED = ("w_in", "ffn_w_up")


def _update_shard(name, block):
    if name not in TRANSPOSED:
        return _shard_2d(name, block)
    t = jnp.transpose(block[0])
    rows = PAD_TO.get(name, t.shape[0])
    return jnp.pad(t, ((0, rows - t.shape[0]), (0, 0)))


def _shard_like(name, t, like):
    if name in TRANSPOSED:
        return jnp.transpose(t[:like.shape[2]])[None]
    return t[:, :like.shape[2]][None]


def _pad_cols(t, width):
    return jnp.pad(t, ((0, 0), (0, width - t.shape[1])))


FIRST_NAMES = ("w_in", "ml_conv_w")
FFN_NAMES = ("ffn_w_up", "ffn_w_down", "ffn_conv_w")
MID_NAMES = ("ca_wo", "ca_wq", "ca_wkv", "w_out")


def _first_weights(g, small):
    w = dict(small)
    w["w_in_t"] = g["w_in"].reshape(D_IN, D_MODEL)
    w["w_in_gate_t"] = jnp.pad(w["w_in_t"][D_IN_MAIN:], ((0, LANES - (D_IN - D_IN_MAIN)), (0, 0)))
    w["b_in_main"] = small["b_in"][:, :D_IN_MAIN]
    w["b_in_gate"] = _pad_cols(small["b_in"][:, D_IN_MAIN:], LANES)
    w["ml_conv_w"] = jnp.transpose(g["ml_conv_w"], (1, 0, 2)).reshape(ML_CONV, 2 * D_GROUP)
    return w


def _mid_weights(g):
    w = {n: g[n].reshape(D_MODEL, D_MODEL) for n in ("w_out", "ca_wq", "ca_wo")}
    w["ca_wkv"] = g["ca_wkv"]
    return w


FFN_UP_NAMES = ("ffn_w_up", "ffn_conv_w")
FFN_DOWN_NAMES = ("ffn_w_down",)


def _ffn_up_weights(g, small):
    w = {"ffn_w_up_t": g["ffn_w_up"].reshape(D_UP, D_MODEL)}
    w["ffn_conv_w"] = jnp.transpose(g["ffn_conv_w"][:, :, :UP_SHARD], (1, 0, 2)).reshape(FFN_CONV, D_UP)
    w["ffn_conv_b"] = small["ffn_conv_b"].reshape(1, D_UP)
    return w


def _ffn_down_weights(g):
    return {"ffn_w_down": g["ffn_w_down"].reshape(D_FF, D_MODEL)}


def _owner_stack(n, grads):
    if n == "w_in":
        rows = jnp.concatenate([grads["w_in_main"], grads["w_in_gate"][:D_IN - D_IN_MAIN]], axis=0)
        return rows.reshape(N_DEV, W_IN_SHARD, D_MODEL)
    if n == "ffn_w_up":
        return grads[n].reshape(N_DEV, UP_SHARD, D_MODEL)
    if n in ("w_out", "ca_wq", "ca_wo"):
        return grads[n].reshape(N_DEV, D_MODEL // N_DEV, D_MODEL)
    if n == "ffn_w_down":
        return grads[n].reshape(N_DEV, D_FF // N_DEV, D_MODEL)
    if n == "ml_conv_w":
        return jnp.transpose(grads[n].reshape(ML_CONV, N_DEV, LANES), (1, 0, 2))
    if n == "ffn_conv_w":
        shards = jnp.transpose(grads[n].reshape(FFN_CONV, N_DEV, UP_SHARD), (1, 0, 2))
        return jnp.pad(shards, ((0, 0), (0, 0), (0, UP_SHARD_P - UP_SHARD)))
    return grads[n]


def _small_grads(grads):
    out = {n: grads[n] for n in SMALL_NAMES if n in grads}
    out["b_in"] = jnp.concatenate([grads["b_in_main"], grads["b_in_gate"][:, :D_IN - D_IN_MAIN]], axis=1)
    return out


def _small_rows(shape):
    return shape[0] if shape[1] <= SMALL_W else -(-shape[1] // SMALL_W)


SMALL_BASE = {n: sum(_small_rows(SMALL_SHAPES[k]) for k in SMALL_NAMES[:i]) for i, n in enumerate(SMALL_NAMES)}
SMALL_LOSS_ROW = sum(_small_rows(SMALL_SHAPES[n]) for n in SMALL_NAMES)
assert SMALL_LOSS_ROW < SMALL_ROWS


def _small_pieces(name, shape):
    base = SMALL_BASE[name]
    if shape[1] <= SMALL_W:
        return [(i, 0, shape[1], base + i) for i in range(shape[0])]
    return [(0, k * SMALL_W, min(SMALL_W, shape[1] - k * SMALL_W), base + k) for k in range(_small_rows(shape))]


def _pack_small(p, loss):
    rows = []
    for n in SMALL_NAMES:
        t = p[n]
        nrows = _small_rows(t.shape)
        if t.shape[1] <= SMALL_W:
            rows.append(_pad_cols(t, SMALL_W))
        else:
            rows.append(_pad_cols(t, nrows * SMALL_W).reshape(nrows, SMALL_W))
    rows.append(_pad_cols(loss, SMALL_W))
    slab = jnp.concatenate(rows, axis=0)
    return jnp.pad(slab, ((0, SMALL_ROWS - slab.shape[0]), (0, 0)))


def kernel(x, mem, w_in, b_in, hg_lb_logits, hg_norm_w, ml_conv_w, ml_conv_b, ml_norm_w, w_out, ln1_g, ln1_b, ca_wq, ca_wkv, ca_wo, ln2_g, ln2_b, ffn_w_up, ffn_conv_w, ffn_conv_b, ffn_w_down, ln3_g, ln3_b, loss_target, m_w_in, m_b_in, m_hg_lb_logits, m_hg_norm_w, m_ml_conv_w, m_ml_conv_b, m_ml_norm_w, m_w_out, m_ln1_g, m_ln1_b, m_ca_wq, m_ca_wkv, m_ca_wo, m_ln2_g, m_ln2_b, m_ffn_w_up, m_ffn_conv_w, m_ffn_conv_b, m_ffn_w_down, m_ln3_g, m_ln3_b, v_w_in, v_b_in, v_hg_lb_logits, v_hg_norm_w, v_ml_conv_w, v_ml_conv_b, v_ml_norm_w, v_w_out, v_ln1_g, v_ln1_b, v_ca_wq, v_ca_wkv, v_ca_wo, v_ln2_g, v_ln2_b, v_ffn_w_up, v_ffn_conv_w, v_ffn_conv_b, v_ffn_w_down, v_ln3_g, v_ln3_b):
    params = dict(w_in=w_in, b_in=b_in, hg_lb_logits=hg_lb_logits, hg_norm_w=hg_norm_w, ml_conv_w=ml_conv_w,
                  ml_conv_b=ml_conv_b, ml_norm_w=ml_norm_w, w_out=w_out, ln1_g=ln1_g, ln1_b=ln1_b, ca_wq=ca_wq,
                  ca_wkv=ca_wkv, ca_wo=ca_wo, ln2_g=ln2_g, ln2_b=ln2_b, ffn_w_up=ffn_w_up, ffn_conv_w=ffn_conv_w,
                  ffn_conv_b=ffn_conv_b, ffn_w_down=ffn_w_down, ln3_g=ln3_g, ln3_b=ln3_b)
    mom1 = dict(w_in=m_w_in, b_in=m_b_in, hg_lb_logits=m_hg_lb_logits, hg_norm_w=m_hg_norm_w,
                ml_conv_w=m_ml_conv_w, ml_conv_b=m_ml_conv_b, ml_norm_w=m_ml_norm_w, w_out=m_w_out, ln1_g=m_ln1_g,
                ln1_b=m_ln1_b, ca_wq=m_ca_wq, ca_wkv=m_ca_wkv, ca_wo=m_ca_wo, ln2_g=m_ln2_g, ln2_b=m_ln2_b,
                ffn_w_up=m_ffn_w_up, ffn_conv_w=m_ffn_conv_w, ffn_conv_b=m_ffn_conv_b, ffn_w_down=m_ffn_w_down,
                ln3_g=m_ln3_g, ln3_b=m_ln3_b)
    mom2 = dict(w_in=v_w_in, b_in=v_b_in, hg_lb_logits=v_hg_lb_logits, hg_norm_w=v_hg_norm_w,
                ml_conv_w=v_ml_conv_w, ml_conv_b=v_ml_conv_b, ml_norm_w=v_ml_norm_w, w_out=v_w_out, ln1_g=v_ln1_g,
                ln1_b=v_ln1_b, ca_wq=v_ca_wq, ca_wkv=v_ca_wkv, ca_wo=v_ca_wo, ln2_g=v_ln2_g, ln2_b=v_ln2_b,
                ffn_w_up=v_ffn_w_up, ffn_conv_w=v_ffn_conv_w, ffn_conv_b=v_ffn_conv_b, ffn_w_down=v_ffn_w_down,
                ln3_g=v_ln3_g, ln3_b=v_ln3_b)

    x_idx, y_idx, c_idx = _coords()
    as_index = lambda v: jnp.reshape(v, (1,)).astype(jnp.int32)
    me = as_index(4 * x_idx + 2 * y_idx + c_idx)
    small_params = {n: params[n] for n in SMALL_NAMES}

    shards = {n: _update_shard(n, params[n]) for n in SHARDED_NAMES}
    m_shards = {n: _update_shard(n, mom1[n]) for n in SHARDED_NAMES}
    v_shards = {n: _update_shard(n, mom2[n]) for n in SHARDED_NAMES}
    outgoing = {n: shards[n] if "conv" in n else shards[n].astype(BF16) for n in SHARDED_NAMES}
    to_send = lambda names: [outgoing[n] for n in names]
    glue = [*m_shards.values(), *v_shards.values(), *shards.values(),
            *[outgoing[n] for n in SHARDED_NAMES if n not in FIRST_NAMES]]
    first = dict(zip(FIRST_NAMES, _two_level_gather(to_send(FIRST_NAMES), glue, "weights_gather_first")))
    mid_started, through = _direct_start(True, to_send(MID_NAMES), first["w_in"], "weights_gather_start_mid")
    ffn_started, through = _direct_start(True, to_send(FFN_UP_NAMES), through, "weights_gather_start_ffn_up")
    down_started, first["w_in"] = _direct_start(True, to_send(FFN_DOWN_NAMES), through,
                                                "weights_gather_start_ffn_down")

    def gathered_weights(names, started, after, tag):
        mine, lands = _direct_wait(True, started, after, "weights_gather_wait_" + tag)
        return {n: lax.dynamic_update_index_in_dim(land, own, me[0], 0) for n, own, land in zip(names, mine, lands)}

    started, own_stacks = {}, {}

    def start_group(names, tag):
        def hook(grads, through):
            own_stacks[tag] = [_owner_stack(n, grads).astype(BF16) for n in names]
            started[tag], through = _direct_start(False, own_stacks[tag], through, "grads_start_" + tag)
            return through
        return hook

    def start_small(grads, loss, through):
        started["small"], through = _direct_start(True, [_pack_small(_small_grads(grads), loss)], through,
                                                  "small_gather_start")
        return through

    loss, grad_x, grads = _local_step(
        x[0], mem[0], loss_target[0], _first_weights(first, small_params),
        lambda y: _mid_weights(gathered_weights(MID_NAMES, mid_started, y, "mid")),
        lambda x2: _ffn_up_weights(gathered_weights(FFN_UP_NAMES, ffn_started, x2, "ffn_up"), small_params),
        lambda hid: _ffn_down_weights(gathered_weights(FFN_DOWN_NAMES, down_started, hid, "ffn_down")),
        start_group(FFN_NAMES, "ffn"), start_group(MID_NAMES, "mid"), start_small, start_group(FIRST_NAMES, "last"))

    sharded_out = {}

    def update_group(names, tag, after):
        _, lands = _direct_wait(False, started[tag], after, "grads_wait_" + tag)
        for n, st, land in zip(names, own_stacks[tag], lands):
            res = _adamw_sharded(me, st, land, shards[n], m_shards[n], v_shards[n], "adamw_" + n)
            sharded_out[n] = [_shard_like(n, t, params[n]) for t in res]

    update_group(FFN_NAMES, "ffn", grad_x)
    update_group(MID_NAMES, "mid", grad_x)
    own_small, small_lands = _direct_wait(True, started["small"], grad_x, "small_gather_wait")
    small_parts = lax.dynamic_update_index_in_dim(small_lands[0], own_small[0], me[0], 0)
    small_out, total_loss = _adamw_replicated(small_parts, small_params, {n: mom1[n] for n in SMALL_NAMES},
                                              {n: mom2[n] for n in SMALL_NAMES})
    done = [t for n in FFN_NAMES + MID_NAMES for t in sharded_out[n]]
    done += [t for small in small_out for t in small.values()]
    update_group(FIRST_NAMES, "last", done)

    outs = []
    for k, small in enumerate(small_out):
        outs.extend(sharded_out[n][k] if n in sharded_out else small[n] for n in WEIGHT_NAMES)
    return (total_loss[0, 0], grad_x[None], *outs)
```

```python
import functools
import math

import jax
import jax.numpy as jnp
from jax import lax
from jax.experimental import pallas as pl
from jax.experimental.pallas import tpu as pltpu

F32 = jnp.float32
BF16 = jnp.bfloat16
HIGHEST = lax.Precision.HIGHEST
MESH = pl.DeviceIdType.MESH

N_DEV = 8
D_MODEL = 1024
N_MEM = 256
N_HEADS = 4
D_HEAD = 128
D_GROUP = N_HEADS * D_HEAD
CHUNK = 64
ML_CONV = 4
FFN_CONV = 3
D_FF = 2816
D_UP = 2 * D_FF
CA_HEADS = 4
CA_DH = D_MODEL // CA_HEADS
LANES = 128
SUBLANES = 8
D_IN = 8 * D_GROUP + 2 * N_HEADS
D_IN_MAIN = 8 * D_GROUP
W_IN_SHARD = D_IN // N_DEV
UP_SHARD = D_UP // N_DEV
UP_SHARD_P = 768
UP_TILE = D_UP // 4
ALPHA = 2.0 ** 0.25
LN_EPS = 1e-5
NEG_BIG = -1e30
ADAM_LR = 0.001
ADAM_B1 = 0.9
ADAM_B2 = 0.999
ADAM_EPS = 1e-08
ADAM_WD = 0.01
ADAM_STEP = 10
VMEM_LIMIT = 56 * 1024 * 1024

SEG_MQ = 4 * D_GROUP // LANES
VO_BLOCK = 3


def _params(sem):
    return pltpu.CompilerParams(dimension_semantics=sem, vmem_limit_bytes=VMEM_LIMIT)


def _dg(a, b, ca, cb, precision=None):
    return lax.dot_general(a, b, (((ca,), (cb,)), ((), ())), precision=precision,
                           preferred_element_type=F32)


def _nn_raw(a, b):
    return _dg(a.astype(BF16), b.astype(BF16), 1, 0)


def _nt_raw(a, b):
    return _dg(a.astype(BF16), b.astype(BF16), 1, 1)


def _tn_raw(a, b):
    return _dg(a.astype(BF16), b.astype(BF16), 0, 0)


@jax.custom_vjp
def _nn(a, b):
    return _nn_raw(a, b)


_nn.defvjp(lambda a, b: (_nn_raw(a, b), (a, b)),
           lambda res, g: (_nt_raw(g, res[1]), _tn_raw(res[0], g)))


@jax.custom_vjp
def _nt(a, b):
    return _nt_raw(a, b)


_nt.defvjp(lambda a, b: (_nt_raw(a, b), (a, b)),
           lambda res, g: (_nn_raw(g, res[1]), _tn_raw(g, res[0])))


@jax.custom_vjp
def _tn(a, b):
    return _tn_raw(a, b)


_tn.defvjp(lambda a, b: (_tn_raw(a, b), (a, b)),
           lambda res, g: (_nt_raw(res[1], g), _nn_raw(res[0], g)))


def _layer_norm(z, g, b):
    mu = jnp.mean(z, axis=-1, keepdims=True)
    var = jnp.mean(jnp.square(z - mu), axis=-1, keepdims=True)
    return (z - mu) * lax.rsqrt(var + LN_EPS) * g + b


def _matmul_nn(a, w, bias, tm, tn, name, out_dtype=F32):
    m, k = a.shape
    if w.ndim == 3:
        n = w.shape[0] * w.shape[2]
        assert tn == w.shape[2]
        w_spec = pl.BlockSpec((None, k, tn), lambda i, j: (j, 0, 0))
    else:
        n = w.shape[1]
        w_spec = pl.BlockSpec((k, tn), lambda i, j: (0, j))

    def body(*refs):
        a_ref, w_ref = refs[0], refs[1]
        o_ref = refs[-1]
        acc = _nn_raw(a_ref[...], w_ref[...])
        if bias is not None:
            acc = acc + refs[2][...]
        o_ref[...] = acc.astype(o_ref.dtype)

    in_specs = [pl.BlockSpec((tm, k), lambda i, j: (i, 0)), w_spec]
    args = [a, w]
    if bias is not None:
        in_specs.append(pl.BlockSpec((1, tn), lambda i, j: (0, j)))
        args.append(bias)
    return pl.pallas_call(
        body, name=name, grid=(m // tm, n // tn), in_specs=in_specs,
        out_specs=pl.BlockSpec((tm, tn), lambda i, j: (i, j)),
        out_shape=jax.ShapeDtypeStruct((m, n), out_dtype),
        compiler_params=_params(("parallel", "parallel")),
    )(*args)


def _matmul_nt(d, w, tm, tk, name, k_out=None, bias=None, out_dtype=F32):
    m, n = d.shape
    k = k_out or w.shape[0]

    def body(*refs):
        acc = _nt_raw(refs[0][...], refs[1][...])
        if bias is not None:
            acc = acc + refs[2][...]
        refs[-1][...] = acc.astype(refs[-1].dtype)

    in_specs = [pl.BlockSpec((tm, n), lambda i, j: (i, 0)), pl.BlockSpec((tk, n), lambda i, j: (j, 0))]
    args = [d, w]
    if bias is not None:
        in_specs.append(pl.BlockSpec((1, tk), lambda i, j: (0, j)))
        args.append(bias)
    return pl.pallas_call(
        body, name=name, grid=(m // tm, k // tk), in_specs=in_specs,
        out_specs=pl.BlockSpec((tm, tk), lambda i, j: (i, j)),
        out_shape=jax.ShapeDtypeStruct((m, k), out_dtype),
        compiler_params=_params(("parallel", "parallel")),
    )(*args)


def _input_projection(x, w_t, w_gate_t, b_main, b_gate, tm, tk):
    m, n = x.shape

    def body(x_ref, w_ref, wg_ref, b_ref, bg_ref, o_ref, g_ref):
        lhs = x_ref[...].astype(BF16)
        o_ref[...] = _nt_raw(lhs, w_ref[...]) + b_ref[...]

        @pl.when(pl.program_id(1) == 0)
        def _():
            g_ref[...] = _nt_raw(lhs, wg_ref[...]) + bg_ref[...]

    return pl.pallas_call(
        body, name="proj", grid=(m // tm, D_IN_MAIN // tk),
        in_specs=[pl.BlockSpec((tm, n), lambda i, j: (i, 0)), pl.BlockSpec((tk, n), lambda i, j: (j, 0)),
                  pl.BlockSpec((LANES, n), lambda i, j: (0, 0)), pl.BlockSpec((1, tk), lambda i, j: (0, j)),
                  pl.BlockSpec((1, LANES), lambda i, j: (0, 0))],
        out_specs=[pl.BlockSpec((tm, tk), lambda i, j: (i, j)), pl.BlockSpec((tm, LANES), lambda i, j: (i, 0))],
        out_shape=[jax.ShapeDtypeStruct((m, D_IN_MAIN), F32), jax.ShapeDtypeStruct((m, LANES), F32)],
        compiler_params=_params(("parallel", "arbitrary")),
    )(x, w_t, w_gate_t, b_main, b_gate)


def _input_projection_grads(d_proj, d_gates, x, tm, tt):
    t, m = d_proj.shape
    n = x.shape[1]
    last = t // tt - 1

    def body(a_ref, g_ref, x_ref, o_ref, og_ref, acc_ref, accg_ref):
        i, kk = pl.program_id(0), pl.program_id(1)

        @pl.when(kk == 0)
        def _():
            acc_ref[...] = jnp.zeros_like(acc_ref)

        @pl.when((kk == 0) & (i == 0))
        def _():
            accg_ref[...] = jnp.zeros_like(accg_ref)

        rhs = x_ref[...].astype(BF16)
        acc_ref[...] += _tn_raw(a_ref[...], rhs)

        @pl.when(i == 0)
        def _():
            accg_ref[...] += _tn_raw(g_ref[...], rhs)

        @pl.when(kk == last)
        def _():
            o_ref[...] = acc_ref[...].astype(o_ref.dtype)

        @pl.when((kk == last) & (i == 0))
        def _():
            og_ref[...] = accg_ref[...].astype(og_ref.dtype)

    return pl.pallas_call(
        body, name="d_w_in", grid=(m // tm, t // tt),
        in_specs=[pl.BlockSpec((tt, tm), lambda i, kk: (kk, i)), pl.BlockSpec((tt, LANES), lambda i, kk: (kk, 0)),
                  pl.BlockSpec((tt, n), lambda i, kk: (kk, 0))],
        out_specs=[pl.BlockSpec((tm, n), lambda i, kk: (i, 0)), pl.BlockSpec((LANES, n), lambda i, kk: (0, 0))],
        out_shape=[jax.ShapeDtypeStruct((m, n), BF16), jax.ShapeDtypeStruct((LANES, n), BF16)],
        scratch_shapes=[pltpu.VMEM((tm, n), F32), pltpu.VMEM((LANES, n), F32)],
        compiler_params=_params(("arbitrary", "arbitrary")),
    )(d_proj, d_gates, x)


def _matmul_nn_sum(pairs, add, scale, tm, name):
    m = pairs[0][0].shape[0]
    n = pairs[0][1].shape[1]
    in_specs, args = [], []
    for a, w, row0 in pairs:
        kk = a.shape[1]
        in_specs += [pl.BlockSpec((tm, kk), lambda i: (i, 0)),
                     pl.BlockSpec((kk, n), lambda i, blk=row0 // kk: (blk, 0))]
        args += [a, w]
    if add is not None:
        in_specs.append(pl.BlockSpec((tm, n), lambda i: (i, 0)))
        args.append(add)

    def body(*refs):
        acc = None
        for p in range(len(pairs)):
            term = _nn_raw(refs[2 * p][...], refs[2 * p + 1][...])
            acc = term if acc is None else acc + term
        if add is not None:
            acc = acc + scale * refs[2 * len(pairs)][...]
        refs[-1][...] = acc

    return pl.pallas_call(
        body, name=name, grid=(m // tm,), in_specs=in_specs,
        out_specs=pl.BlockSpec((tm, n), lambda i: (i, 0)),
        out_shape=jax.ShapeDtypeStruct((m, n), F32),
        compiler_params=_params(("parallel",)),
    )(*args)


def _matmul_tn(a, b, tm, tn, tt, name, shards=None, shard0=0, group=1, into=None, colsum=False, rows=None, row0=0):
    t, m = a.shape
    n = b.shape[1]
    assert not colsum or tm == m
    n_in = 2 + (into is not None)
    out_dtype = BF16
    per_step = 1 if shards is None else group
    width = per_step * tn

    def body(*refs):
        a_ref, b_ref = refs[0], refs[1]
        o_ref, acc_ref = refs[n_in], refs[-1]
        first = pl.program_id(2) == 0

        @pl.when(first)
        def _():
            acc_ref[...] = jnp.zeros_like(acc_ref)

        if shards is None:
            acc_ref[...] += _tn_raw(a_ref[...], b_ref[...])
        else:
            lhs = a_ref[...].astype(BF16)
            for g in range(per_step):
                acc_ref[g] += _tn_raw(lhs, b_ref[:, g * tn:(g + 1) * tn])

        @pl.when(pl.program_id(2) == t // tt - 1)
        def _():
            o_ref[...] = acc_ref[...].astype(o_ref.dtype)

        if colsum:
            s_ref = refs[n_in + 1]

            @pl.when(first)
            def _():
                s_ref[...] = jnp.zeros_like(s_ref)

            s_ref[...] += jnp.sum(b_ref[...], axis=0, keepdims=True)

    in_specs = [pl.BlockSpec((tt, tm), lambda i, j, kk: (kk, i)),
                pl.BlockSpec((tt, width), lambda i, j, kk: (kk, j))]
    args = [a, b]
    aliases = {}
    if into is not None:
        in_specs.append(pl.BlockSpec(memory_space=pl.ANY))
        args.append(into)
        aliases = {2: 0}
    if shards is None:
        out_specs = [pl.BlockSpec((tm, tn), lambda i, j, kk: (row0 // tm + i, j))]
        out_shape = [jax.ShapeDtypeStruct((rows or m, n), out_dtype)]
        acc = pltpu.VMEM((tm, tn), F32)
    else:
        out_specs = [pl.BlockSpec((per_step, tm, tn), lambda i, j, kk: (shard0 // per_step + j, i, 0))]
        out_shape = [jax.ShapeDtypeStruct((shards, m, tn), out_dtype)]
        acc = pltpu.VMEM((per_step, tm, tn), F32)
    if colsum:
        out_specs.append(pl.BlockSpec((1, tn), lambda i, j, kk: (0, j)))
        out_shape.append(jax.ShapeDtypeStruct((1, n), F32))
    res = pl.pallas_call(
        body, name=name, grid=(m // tm, n // width, t // tt), in_specs=in_specs, out_specs=out_specs,
        out_shape=out_shape, input_output_aliases=aliases, scratch_shapes=[acc],
        compiler_params=_params(("parallel", "parallel", "arbitrary")),
    )(*args)
    return res if colsum else res[0]


ROW_TILE = 64


def _stack(ref, start, rows):
    return ref[pl.ds(start, rows), :].astype(F32).reshape(rows // SUBLANES, SUBLANES, LANES)


def _vreg_rows(ref, n):
    return [jnp.broadcast_to(ref[j:j + 1, :], (SUBLANES, LANES))[None] for j in range(n)]


def _column_total(acc):
    return jnp.sum(acc, axis=0, keepdims=True)


def _conv_fwd_tile(pad_ref, taps_w, bias, r0, rows):
    taps = len(taps_w)
    acc = bias
    for j in range(taps):
        acc = acc + _stack(pad_ref, SUBLANES - (taps - 1 - j) + r0, rows) * taps_w[j]
    return acc


def _conv_grads_tile(pad_ref, dpad_ref, dx_ref, taps_w, dws, r0, rows):
    taps = len(taps_w)
    x_rows = _stack(pad_ref, SUBLANES + r0, rows)
    dx = None
    for j in range(taps):
        d_shifted = _stack(dpad_ref, r0 + (taps - 1 - j), rows)
        term = d_shifted * taps_w[j]
        dx = term if dx is None else dx + term
        dws[j] = dws[j] + jnp.sum(d_shifted * x_rows, axis=0)
    dx_ref[r0:r0 + rows, :] = dx.reshape(rows, LANES).astype(dx_ref.dtype)
    return jnp.sum(dx, axis=0)


def _ml_conv_fwd(proj, conv_w, conv_b):
    s = proj.shape[0]
    nblk = 2 * D_GROUP // LANES

    def body(x_ref, w_ref, b_ref, o_ref, pad_ref):
        pad_ref[0:SUBLANES, :] = jnp.zeros((SUBLANES, LANES), F32)
        pad_ref[SUBLANES:, :] = x_ref[...].astype(F32)
        taps_w, bias = _vreg_rows(w_ref, ML_CONV), _vreg_rows(b_ref, 1)[0]
        for r0 in range(0, s, ROW_TILE):
            rows = min(ROW_TILE, s - r0)
            o_ref[r0:r0 + rows, :] = jax.nn.silu(_conv_fwd_tile(pad_ref, taps_w, bias, r0, rows)).reshape(rows, LANES)

    return pl.pallas_call(
        body, name="ml_conv_fwd", grid=(nblk,),
        in_specs=[pl.BlockSpec((s, LANES), lambda j: (0, SEG_MQ + j)),
                  pl.BlockSpec((ML_CONV, LANES), lambda j: (0, j)),
                  pl.BlockSpec((1, LANES), lambda j: (0, j))],
        out_specs=pl.BlockSpec((s, LANES), lambda j: (0, j)),
        out_shape=jax.ShapeDtypeStruct((s, 2 * D_GROUP), F32),
        scratch_shapes=[pltpu.VMEM((s + SUBLANES, LANES), F32)],
        compiler_params=_params(("parallel",)),
    )(proj, conv_w, conv_b)


def _ml_conv_bwd(proj, conv_w, conv_b, d_qk, d_proj):
    s = proj.shape[0]
    nblk = 2 * D_GROUP // LANES

    def body(x_ref, w_ref, b_ref, dy_ref, _, dx_ref, dw_ref, db_ref, dxs_ref, pad_ref, dpad_ref):
        pad_ref[0:SUBLANES, :] = jnp.zeros((SUBLANES, LANES), F32)
        pad_ref[SUBLANES:, :] = x_ref[...].astype(F32)
        dpad_ref[s:, :] = jnp.zeros((SUBLANES, LANES), F32)
        taps_w, bias = _vreg_rows(w_ref, ML_CONV), _vreg_rows(b_ref, 1)[0]
        db = jnp.zeros((SUBLANES, LANES), F32)
        for r0 in range(0, s, ROW_TILE):
            rows = min(ROW_TILE, s - r0)
            pre = _conv_fwd_tile(pad_ref, taps_w, bias, r0, rows)
            _, vjp = jax.vjp(jax.nn.silu, pre)
            d_pre, = vjp(_stack(dy_ref, r0, rows))
            dpad_ref[r0:r0 + rows, :] = d_pre.reshape(rows, LANES)
            db = db + jnp.sum(d_pre, axis=0)
        db_ref[...] = _column_total(db)
        dws = [jnp.zeros((SUBLANES, LANES), F32) for _ in range(ML_CONV)]
        dx_sum = jnp.zeros((SUBLANES, LANES), F32)
        for r0 in range(0, s, ROW_TILE):
            dx_sum = dx_sum + _conv_grads_tile(pad_ref, dpad_ref, dx_ref, taps_w, dws, r0, min(ROW_TILE, s - r0))
        dxs_ref[...] = _column_total(dx_sum)
        for j in range(ML_CONV):
            dw_ref[j:j + 1, :] = _column_total(dws[j])

    return pl.pallas_call(
        body, name="ml_conv_bwd", grid=(nblk,),
        in_specs=[pl.BlockSpec((s, LANES), lambda j: (0, SEG_MQ + j)),
                  pl.BlockSpec((ML_CONV, LANES), lambda j: (0, j)),
                  pl.BlockSpec((1, LANES), lambda j: (0, j)),
                  pl.BlockSpec((s, LANES), lambda j: (0, j)),
                  pl.BlockSpec(memory_space=pl.ANY)],
        out_specs=[pl.BlockSpec((s, LANES), lambda j: (0, SEG_MQ + j)),
                   pl.BlockSpec((ML_CONV, LANES), lambda j: (0, j)),
                   pl.BlockSpec((1, LANES), lambda j: (0, j)),
                   pl.BlockSpec((1, LANES), lambda j: (0, j))],
        out_shape=[jax.ShapeDtypeStruct(d_proj.shape, d_proj.dtype),
                   jax.ShapeDtypeStruct((ML_CONV, 2 * D_GROUP), F32),
                   jax.ShapeDtypeStruct((1, 2 * D_GROUP), F32),
                   jax.ShapeDtypeStruct((1, 2 * D_GROUP), F32)],
        input_output_aliases={4: 0},
        scratch_shapes=[pltpu.VMEM((s + SUBLANES, LANES), F32), pltpu.VMEM((s + SUBLANES, LANES), F32)],
        compiler_params=_params(("parallel",)),
    )(proj, conv_w, conv_b, d_qk, d_proj)


def _gelu_mul(a, b):
    return jax.nn.gelu(a) * b


GELU_C = math.sqrt(2.0 / math.pi)
GELU_K = 0.044715


def _gelu_mul_grads(a, b, d):
    a2 = a * a
    t = jnp.tanh(GELU_C * (a + GELU_K * (a * a2)))
    cdf = 0.5 * (1.0 + t)
    slope = cdf + (0.5 * GELU_C) * a * (1.0 - t * t) * (1.0 + (3.0 * GELU_K) * a2)
    return d * b * slope, d * (a * cdf)


FFN_BLOCKS = D_FF // LANES


def _ffn_conv_fwd(u, conv_w, conv_b):
    s = u.shape[0]

    def body(g_ref, v_ref, wg_ref, wv_ref, bg_ref, bv_ref, o_ref, gpad_ref, vpad_ref):
        for pad_ref, x_ref in ((gpad_ref, g_ref), (vpad_ref, v_ref)):
            pad_ref[0:SUBLANES, :] = jnp.zeros((SUBLANES, LANES), F32)
            pad_ref[SUBLANES:, :] = x_ref[...].astype(F32)
        taps_g, bias_g = _vreg_rows(wg_ref, FFN_CONV), _vreg_rows(bg_ref, 1)[0]
        taps_v, bias_v = _vreg_rows(wv_ref, FFN_CONV), _vreg_rows(bv_ref, 1)[0]
        for r0 in range(0, s, ROW_TILE):
            rows = min(ROW_TILE, s - r0)
            ug = _conv_fwd_tile(gpad_ref, taps_g, bias_g, r0, rows)
            uv = _conv_fwd_tile(vpad_ref, taps_v, bias_v, r0, rows)
            o_ref[r0:r0 + rows, :] = _gelu_mul(ug, uv).reshape(rows, LANES).astype(o_ref.dtype)

    col = lambda off: (lambda j: (0, off + j))
    return pl.pallas_call(
        body, name="ffn_conv_fwd", grid=(FFN_BLOCKS,),
        in_specs=[pl.BlockSpec((s, LANES), col(0)), pl.BlockSpec((s, LANES), col(FFN_BLOCKS)),
                  pl.BlockSpec((FFN_CONV, LANES), col(0)), pl.BlockSpec((FFN_CONV, LANES), col(FFN_BLOCKS)),
                  pl.BlockSpec((1, LANES), col(0)), pl.BlockSpec((1, LANES), col(FFN_BLOCKS))],
        out_specs=pl.BlockSpec((s, LANES), col(0)),
        out_shape=jax.ShapeDtypeStruct((s, D_FF), BF16),
        scratch_shapes=[pltpu.VMEM((s + SUBLANES, LANES), F32), pltpu.VMEM((s + SUBLANES, LANES), F32)],
        compiler_params=_params(("parallel",)),
    )(u, u, conv_w, conv_w, conv_b, conv_b)


def _ffn_conv_bwd(u, conv_w, conv_b, d_h):
    s = u.shape[0]

    def body(g_ref, v_ref, wg_ref, wv_ref, bg_ref, bv_ref, dh_ref,
             dug_ref, duv_ref, dwg_ref, dwv_ref, dbg_ref, dbv_ref,
             gpad_ref, vpad_ref, dgpad_ref, dvpad_ref):
        for pad_ref, x_ref in ((gpad_ref, g_ref), (vpad_ref, v_ref)):
            pad_ref[0:SUBLANES, :] = jnp.zeros((SUBLANES, LANES), F32)
            pad_ref[SUBLANES:, :] = x_ref[...].astype(F32)
        dgpad_ref[s:, :] = jnp.zeros((SUBLANES, LANES), F32)
        dvpad_ref[s:, :] = jnp.zeros((SUBLANES, LANES), F32)
        taps_g, bias_g = _vreg_rows(wg_ref, FFN_CONV), _vreg_rows(bg_ref, 1)[0]
        taps_v, bias_v = _vreg_rows(wv_ref, FFN_CONV), _vreg_rows(bv_ref, 1)[0]
        dbg = jnp.zeros((SUBLANES, LANES), F32)
        dbv = jnp.zeros((SUBLANES, LANES), F32)
        for r0 in range(0, s, ROW_TILE):
            rows = min(ROW_TILE, s - r0)
            ug = _conv_fwd_tile(gpad_ref, taps_g, bias_g, r0, rows)
            uv = _conv_fwd_tile(vpad_ref, taps_v, bias_v, r0, rows)
            d_ug, d_uv = _gelu_mul_grads(ug, uv, _stack(dh_ref, r0, rows))
            dgpad_ref[r0:r0 + rows, :] = d_ug.reshape(rows, LANES)
            dvpad_ref[r0:r0 + rows, :] = d_uv.reshape(rows, LANES)
            dbg = dbg + jnp.sum(d_ug, axis=0)
            dbv = dbv + jnp.sum(d_uv, axis=0)
        dbg_ref[...] = _column_total(dbg)
        dbv_ref[...] = _column_total(dbv)
        for pad_ref, dpad_ref, taps_w, dx_ref, dw_ref in ((gpad_ref, dgpad_ref, taps_g, dug_ref, dwg_ref),
                                                          (vpad_ref, dvpad_ref, taps_v, duv_ref, dwv_ref)):
            dws = [jnp.zeros((SUBLANES, LANES), F32) for _ in range(FFN_CONV)]
            for r0 in range(0, s, ROW_TILE):
                _conv_grads_tile(pad_ref, dpad_ref, dx_ref, taps_w, dws, r0, min(ROW_TILE, s - r0))
            for j in range(FFN_CONV):
                dw_ref[j:j + 1, :] = _column_total(dws[j])

    col = lambda off: (lambda j: (0, off + j))
    seq = pl.BlockSpec((s, LANES), col(0))
    return pl.pallas_call(
        body, name="ffn_conv_bwd", grid=(FFN_BLOCKS,),
        in_specs=[pl.BlockSpec((s, LANES), col(0)), pl.BlockSpec((s, LANES), col(FFN_BLOCKS)),
                  pl.BlockSpec((FFN_CONV, LANES), col(0)), pl.BlockSpec((FFN_CONV, LANES), col(FFN_BLOCKS)),
                  pl.BlockSpec((1, LANES), col(0)), pl.BlockSpec((1, LANES), col(FFN_BLOCKS)), seq],
        out_specs=[seq, seq, pl.BlockSpec((FFN_CONV, LANES), col(0)), pl.BlockSpec((FFN_CONV, LANES), col(0)),
                   pl.BlockSpec((1, LANES), col(0)), pl.BlockSpec((1, LANES), col(0))],
        out_shape=[jax.ShapeDtypeStruct((s, D_FF), BF16), jax.ShapeDtypeStruct((s, D_FF), BF16),
                   jax.ShapeDtypeStruct((FFN_CONV, D_FF), F32), jax.ShapeDtypeStruct((FFN_CONV, D_FF), F32),
                   jax.ShapeDtypeStruct((1, D_FF), F32), jax.ShapeDtypeStruct((1, D_FF), F32)],
        scratch_shapes=[pltpu.VMEM((s + SUBLANES, LANES), F32) for _ in range(4)],
        compiler_params=_params(("parallel",)),
    )(u, u, conv_w, conv_w, conv_b, conv_b, d_h)


def _chunk_masks(c):
    row = lax.broadcasted_iota(jnp.int32, (c, c), 0)
    col = lax.broadcasted_iota(jnp.int32, (c, c), 1)
    return row, col


@jax.custom_vjp
def _split_heads(x):
    return tuple(x[:, h * D_HEAD:(h + 1) * D_HEAD] for h in range(N_HEADS))


_split_heads.defvjp(lambda x: (_split_heads(x), None), lambda _, gs: (jnp.concatenate(gs, axis=1),))


@jax.custom_vjp
def _merge_heads(xs):
    return jnp.concatenate(xs, axis=1)


_merge_heads.defvjp(lambda xs: (_merge_heads(xs), None), lambda _, g: (_split_heads(g),))


@jax.custom_vjp
def _split_chunks(x):
    return tuple(x[i * CHUNK:(i + 1) * CHUNK] for i in range(x.shape[0] // CHUNK))


_split_chunks.defvjp(lambda x: (_split_chunks(x), None), lambda _, gs: (jnp.concatenate(gs, axis=0),))


@jax.custom_vjp
def _merge_chunks(xs):
    return jnp.concatenate(xs, axis=0)


_merge_chunks.defvjp(lambda xs: (_merge_chunks(xs), None), lambda _, g: (_split_chunks(g),))


def _blocks(x):
    return [_split_heads(rows) for rows in _split_chunks(x)]


def _per_chunk_rows(per_chunk, rid):
    out = per_chunk[0]
    for i in range(1, len(per_chunk)):
        out = jnp.where(rid >= i * CHUNK, per_chunk[i], out)
    return out


HEADS = range(N_HEADS)
CHUNKS_PER_STEP = 8
ML_CHUNKS_PER_STEP = 1


def _hg_chunk(hq, hf, hi, hgate, l0, l1, nw, sts):
    n = hq.shape[0] // CHUNK
    causal = _chunk_masks(CHUNK)
    causal = causal[1] <= causal[0]
    mx = lax.stop_gradient(jnp.maximum(l0, l1))
    e0 = jnp.exp(l0 - mx)
    e1 = jnp.exp(l1 - mx)
    lb = e0 / (e0 + e1)
    sig = jax.nn.sigmoid(hf)
    lf = jnp.log(lb + (1.0 - lb) * sig)
    k = (1.0 - lb) * jax.nn.sigmoid(-hf)
    q = jax.nn.silu(hq)
    tri = causal.astype(F32)
    b = _merge_chunks(tuple(_dg(tri, rows, 1, 0, HIGHEST) for rows in _split_chunks(lf)))
    rid = lax.broadcasted_iota(jnp.int32, b.shape, 0)
    pick = lambda r: jnp.sum(jnp.where(rid == r, b, 0.0), axis=0, keepdims=True)
    b_last_c = [pick(i * CHUNK + CHUNK - 1) for i in range(n)]
    b_ref = _per_chunk_rows([pick(i * CHUNK + CHUNK // 2 - 1) for i in range(n)], rid)
    b_last = _per_chunk_rows(b_last_c, rid)
    qa = _blocks(q * jnp.exp(b - b_ref))
    ka = _blocks(k * jnp.exp(b_ref - b))
    qe = _blocks(q * jnp.exp(b))
    kd = _blocks(k * jnp.exp(b_last - b))
    decay = [_split_heads(jnp.exp(b_last_c[i])) for i in range(n)]
    v = _blocks(hi)
    chunks = range(n)
    attn = [[jnp.where(causal, _nt(qa[i][h], ka[i][h]), 0.0) for h in HEADS] for i in chunks]
    intra = [[_nn(attn[i][h], v[i][h]) for h in HEADS] for i in chunks]
    kv = [[_tn(v[i][h], kd[i][h]) for h in HEADS] for i in chunks]
    normed = []
    for i in chunks:
        inter = [_nt(qe[i][h], sts[h]) for h in HEADS]
        sts = tuple(decay[i][h] * sts[h] + kv[i][h] for h in HEADS)
        o = [intra[i][h] + inter[h] for h in HEADS]
        normed.append(_merge_heads(tuple(o[h] * lax.rsqrt(jnp.mean(o[h] * o[h], axis=-1, keepdims=True) + LN_EPS)
                                         for h in HEADS)))
    return _merge_chunks(tuple(normed)) * nw * jax.nn.silu(hgate), sts


def _seg(ref, seg):
    return ref[:, seg * D_GROUP:(seg + 1) * D_GROUP]


def _hgrn2_fwd(proj, logits, norm_w):
    s = proj.shape[0]
    rows = CHUNKS_PER_STEP * CHUNK
    nc = s // rows

    def body(p_ref, lg_ref, nw_ref, y_ref, st_out_ref, st_scr):
        @pl.when(pl.program_id(0) == 0)
        def _():
            st_scr[...] = jnp.zeros_like(st_scr)

        sts = tuple(st_scr[h] for h in HEADS)
        y, sts_new = _hg_chunk(_seg(p_ref, 0), _seg(p_ref, 1), _seg(p_ref, 2), _seg(p_ref, 3),
                               lg_ref[0:1, :], lg_ref[1:2, :], nw_ref[...], sts)
        y_ref[...] = y.astype(y_ref.dtype)
        for h in HEADS:
            st_out_ref[h] = sts[h]
            st_scr[h] = sts_new[h]

    return pl.pallas_call(
        body, name="hgrn2_fwd", grid=(nc,),
        in_specs=[pl.BlockSpec((rows, 4 * D_GROUP), lambda c: (c, 0)),
                  pl.BlockSpec((2, D_GROUP), lambda c: (0, 0)),
                  pl.BlockSpec((1, D_GROUP), lambda c: (0, 0))],
        out_specs=[pl.BlockSpec((rows, D_GROUP), lambda c: (c, 0)),
                   pl.BlockSpec((None, N_HEADS, D_HEAD, D_HEAD), lambda c: (c, 0, 0, 0))],
        out_shape=[jax.ShapeDtypeStruct((s, 2 * D_GROUP), BF16),
                   jax.ShapeDtypeStruct((nc, N_HEADS, D_HEAD, D_HEAD), F32)],
        scratch_shapes=[pltpu.VMEM((N_HEADS, D_HEAD, D_HEAD), F32)],
        compiler_params=_params(("arbitrary",)),
    )(proj, logits, norm_w)


def _hgrn2_bwd(proj, logits, norm_w, states, d_y):
    s = proj.shape[0]
    rows = CHUNKS_PER_STEP * CHUNK
    nc = s // rows

    def body(p_ref, lg_ref, nw_ref, st_ref, dy_ref, dp_ref, dl_ref, dnw_ref, dsum_ref, dst_scr):
        @pl.when(pl.program_id(0) == 0)
        def _():
            dst_scr[...] = jnp.zeros_like(dst_scr)
            dl_ref[...] = jnp.zeros_like(dl_ref)
            dnw_ref[...] = jnp.zeros_like(dnw_ref)
            dsum_ref[...] = jnp.zeros_like(dsum_ref)

        _, vjp = jax.vjp(_hg_chunk, _seg(p_ref, 0), _seg(p_ref, 1), _seg(p_ref, 2), _seg(p_ref, 3),
                         lg_ref[0:1, :], lg_ref[1:2, :], nw_ref[...], tuple(st_ref[h] for h in HEADS))
        d_hq, d_hf, d_hi, d_hg, d_l0, d_l1, d_nw, d_sts = vjp((dy_ref[...], tuple(dst_scr[h] for h in HEADS)))
        for seg, val in enumerate((d_hq, d_hf, d_hi, d_hg)):
            dp_ref[:, seg * D_GROUP:(seg + 1) * D_GROUP] = val.astype(dp_ref.dtype)
            dsum_ref[:, seg * D_GROUP:(seg + 1) * D_GROUP] += jnp.sum(val, axis=0, keepdims=True)
        dl_ref[0:1, :] += d_l0
        dl_ref[1:2, :] += d_l1
        dnw_ref[...] += d_nw
        for h in HEADS:
            dst_scr[h] = d_sts[h]

    rev = lambda c: nc - 1 - c
    return pl.pallas_call(
        body, name="hgrn2_bwd", grid=(nc,),
        in_specs=[pl.BlockSpec((rows, 4 * D_GROUP), lambda c: (rev(c), 0)),
                  pl.BlockSpec((2, D_GROUP), lambda c: (0, 0)),
                  pl.BlockSpec((1, D_GROUP), lambda c: (0, 0)),
                  pl.BlockSpec((None, N_HEADS, D_HEAD, D_HEAD), lambda c: (rev(c), 0, 0, 0)),
                  pl.BlockSpec((rows, D_GROUP), lambda c: (rev(c), 0))],
        out_specs=[pl.BlockSpec((rows, 4 * D_GROUP), lambda c: (rev(c), 0)),
                   pl.BlockSpec((2, D_GROUP), lambda c: (0, 0)),
                   pl.BlockSpec((1, D_GROUP), lambda c: (0, 0)),
                   pl.BlockSpec((1, 4 * D_GROUP), lambda c: (0, 0))],
        out_shape=[jax.ShapeDtypeStruct((s, D_IN_MAIN), BF16), jax.ShapeDtypeStruct((2, D_GROUP), F32),
                   jax.ShapeDtypeStruct((1, D_GROUP), F32), jax.ShapeDtypeStruct((1, 4 * D_GROUP), F32)],
        scratch_shapes=[pltpu.VMEM((N_HEADS, D_HEAD, D_HEAD), F32)],
        compiler_params=_params(("arbitrary",)),
    )(proj, logits, norm_w, states, d_y)


def _gate_column(gates, lane, idx):
    return jnp.sum(jnp.where(lane == idx, gates, 0.0), axis=1, keepdims=True)


def _head_layer_norm(h):
    mu = jnp.mean(h, axis=-1, keepdims=True)
    var = jnp.mean(jnp.square(h - mu), axis=-1, keepdims=True)
    return (h - mu) * lax.rsqrt(var + LN_EPS)


def _ml_chunk(qc, kc, v, mo, gates, nw, cts, ns, ms):
    n = qc.shape[0] // CHUNK
    row, col = _chunk_masks(CHUNK)
    mask = col <= row
    eye = col == row
    to_row = lambda t: jnp.sum(jnp.where(eye, t, 0.0), axis=0, keepdims=True)
    q = _blocks(qc * (D_HEAD ** -0.5))
    k = _blocks(kc)
    vs = _blocks(v)
    gate_rows = _split_chunks(gates)
    lane = lax.broadcasted_iota(jnp.int32, gate_rows[0].shape, 1)
    each = [(i, h) for i in range(n) for h in HEADS]
    on_each = lambda f: {ih: f(*ih) for ih in each}
    ig = on_each(lambda i, h: _gate_column(gate_rows[i], lane, h))
    lf = on_each(lambda i, h: jax.nn.log_sigmoid(_gate_column(gate_rows[i], lane, N_HEADS + h)))
    lf_row = on_each(lambda i, h: to_row(lf[i, h]))
    ig_row = on_each(lambda i, h: to_row(ig[i, h]))
    b_col = on_each(lambda i, h: jnp.sum(jnp.where(mask, lf_row[i, h], 0.0), axis=1, keepdims=True))
    b_row = on_each(lambda i, h: jnp.sum(jnp.where(row <= col, lf[i, h], 0.0), axis=0, keepdims=True))
    g = on_each(lambda i, h: jnp.sum(lf[i, h], axis=0, keepdims=True))
    d = on_each(lambda i, h: jnp.where(mask, b_col[i, h] - b_row[i, h] + ig_row[i, h], -jnp.inf))
    a = on_each(lambda i, h: g[i, h] - b_col[i, h] + ig[i, h])
    m_at = {(0, h): ms[h] for h in HEADS}
    for i, h in each:
        m_at[i + 1, h] = lax.stop_gradient(jnp.maximum(g[i, h] + m_at[i, h], jnp.max(a[i, h], axis=0, keepdims=True)))
    inter = on_each(lambda i, h: b_col[i, h] + m_at[i, h])
    m_t = on_each(lambda i, h: lax.stop_gradient(jnp.maximum(inter[i, h], jnp.max(d[i, h], axis=1, keepdims=True))))
    qk = on_each(lambda i, h: _nt(q[i][h], k[i][h]))
    sc = on_each(lambda i, h: qk[i, h] * jnp.exp(d[i, h] - m_t[i, h]))
    w_inter = on_each(lambda i, h: jnp.exp(inter[i, h] - m_t[i, h]))
    sv = on_each(lambda i, h: _nn(sc[i, h], vs[i][h]))
    decay = on_each(lambda i, h: jnp.exp(g[i, h] + m_at[i, h] - m_at[i + 1, h]))
    wk = on_each(lambda i, h: k[i][h] * jnp.exp(a[i, h] - m_at[i + 1, h]))
    kv = on_each(lambda i, h: _tn(vs[i][h], wk[i, h]))
    normed = []
    for i in range(n):
        qc_state = [_nt(q[i][h], cts[h]) for h in HEADS]
        num = [sv[i, h] + w_inter[i, h] * qc_state[h] for h in HEADS]
        den = [jnp.sum(sc[i, h], axis=1, keepdims=True)
               + w_inter[i, h] * jnp.sum(q[i][h] * ns[h], axis=1, keepdims=True) for h in HEADS]
        hh = [num[h] / jnp.maximum(jnp.abs(den[h]), jnp.exp(-m_t[i, h])) for h in HEADS]
        cts = tuple(decay[i, h] * cts[h] + kv[i, h] for h in HEADS)
        ns = tuple(decay[i, h] * ns[h] + jnp.sum(wk[i, h], axis=0, keepdims=True) for h in HEADS)
        normed.append(_merge_heads(tuple(_head_layer_norm(hh[h]) for h in HEADS)))
    y = jax.nn.sigmoid(mo) * (_merge_chunks(tuple(normed)) * nw)
    return y, cts, ns, tuple(m_at[n, h] for h in HEADS)


def _mlstm_fwd(qk, proj, gates, norm_w, y):
    s = proj.shape[0]
    rows = ML_CHUNKS_PER_STEP * CHUNK
    nc = s // rows

    def body(qk_ref, vo_ref, g_ref, nw_ref, _, y_ref, ct_out, n_out, m_out, ct_scr, n_scr, m_scr):
        @pl.when(pl.program_id(0) == 0)
        def _():
            ct_scr[...] = jnp.zeros_like(ct_scr)
            n_scr[...] = jnp.zeros_like(n_scr)
            m_scr[...] = jnp.full(m_scr.shape, NEG_BIG, F32)

        cts = tuple(ct_scr[h] for h in HEADS)
        ns = tuple(n_scr[h] for h in HEADS)
        ms = tuple(m_scr[h] for h in HEADS)
        y, cts_new, ns_new, ms_new = _ml_chunk(_seg(qk_ref, 0), _seg(qk_ref, 1), _seg(vo_ref, 0), _seg(vo_ref, 1),
                                               g_ref[...], nw_ref[...], cts, ns, ms)
        y_ref[...] = y.astype(y_ref.dtype)
        for h in HEADS:
            ct_out[h], n_out[h], m_out[h] = cts[h], ns[h], ms[h]
            ct_scr[h], n_scr[h], m_scr[h] = cts_new[h], ns_new[h], ms_new[h]

    st = lambda r, w: pl.BlockSpec((None, N_HEADS, r, w), lambda c: (c, 0, 0, 0))
    return pl.pallas_call(
        body, name="mlstm_fwd", grid=(nc,),
        in_specs=[pl.BlockSpec((rows, 2 * D_GROUP), lambda c: (c, 0)),
                  pl.BlockSpec((rows, 2 * D_GROUP), lambda c: (c, VO_BLOCK)),
                  pl.BlockSpec((rows, LANES), lambda c: (c, 0)),
                  pl.BlockSpec((1, D_GROUP), lambda c: (0, 0)),
                  pl.BlockSpec(memory_space=pl.ANY)],
        out_specs=[pl.BlockSpec((rows, D_GROUP), lambda c: (c, 1)),
                   st(D_HEAD, D_HEAD), st(1, D_HEAD), st(1, 1)],
        out_shape=[jax.ShapeDtypeStruct(y.shape, y.dtype),
                   jax.ShapeDtypeStruct((nc, N_HEADS, D_HEAD, D_HEAD), F32),
                   jax.ShapeDtypeStruct((nc, N_HEADS, 1, D_HEAD), F32),
                   jax.ShapeDtypeStruct((nc, N_HEADS, 1, 1), F32)],
        input_output_aliases={4: 0},
        scratch_shapes=[pltpu.VMEM((N_HEADS, D_HEAD, D_HEAD), F32), pltpu.VMEM((N_HEADS, 1, D_HEAD), F32),
                        pltpu.VMEM((N_HEADS, 1, 1), F32)],
        compiler_params=_params(("arbitrary",)),
    )(qk, proj, gates, norm_w, y)


def _mlstm_bwd(qk, proj, gates, norm_w, ct_s, n_s, m_s, d_y, d_proj):
    s = proj.shape[0]
    rows = ML_CHUNKS_PER_STEP * CHUNK
    nc = s // rows

    def body(qk_ref, vo_ref, g_ref, nw_ref, ct_ref, n_ref, m_ref, dy_ref, _,
             dp_ref, dqk_ref, dg_ref, dnw_ref, dsum_ref, dct_scr, dn_scr):
        @pl.when(pl.program_id(0) == 0)
        def _():
            dct_scr[...] = jnp.zeros_like(dct_scr)
            dn_scr[...] = jnp.zeros_like(dn_scr)
            dnw_ref[...] = jnp.zeros_like(dnw_ref)
            dsum_ref[...] = jnp.zeros_like(dsum_ref)

        ms = tuple(m_ref[h] for h in HEADS)
        step = lambda *a: _ml_chunk(*a, ms)[:3]
        _, vjp = jax.vjp(step, _seg(qk_ref, 0), _seg(qk_ref, 1), _seg(vo_ref, 0), _seg(vo_ref, 1), g_ref[...],
                         nw_ref[...], tuple(ct_ref[h] for h in HEADS), tuple(n_ref[h] for h in HEADS))
        d_q, d_k, d_v, d_o, d_gates, d_nw, d_cts, d_ns = vjp(
            (dy_ref[...], tuple(dct_scr[h] for h in HEADS), tuple(dn_scr[h] for h in HEADS)))
        dqk_ref[:, 0:D_GROUP] = d_q
        dqk_ref[:, D_GROUP:2 * D_GROUP] = d_k
        for seg, val in enumerate((d_v, d_o)):
            dp_ref[:, seg * D_GROUP:(seg + 1) * D_GROUP] = val.astype(dp_ref.dtype)
            dsum_ref[:, seg * D_GROUP:(seg + 1) * D_GROUP] += jnp.sum(val, axis=0, keepdims=True)
        dg_ref[...] = d_gates
        dnw_ref[...] += d_nw
        for h in HEADS:
            dct_scr[h] = d_cts[h]
            dn_scr[h] = d_ns[h]

    rev = lambda c: nc - 1 - c
    st = lambda r, w: pl.BlockSpec((None, N_HEADS, r, w), lambda c: (rev(c), 0, 0, 0))
    return pl.pallas_call(
        body, name="mlstm_bwd", grid=(nc,),
        in_specs=[pl.BlockSpec((rows, 2 * D_GROUP), lambda c: (rev(c), 0)),
                  pl.BlockSpec((rows, 2 * D_GROUP), lambda c: (rev(c), VO_BLOCK)),
                  pl.BlockSpec((rows, LANES), lambda c: (rev(c), 0)),
                  pl.BlockSpec((1, D_GROUP), lambda c: (0, 0)),
                  st(D_HEAD, D_HEAD), st(1, D_HEAD), st(1, 1),
                  pl.BlockSpec((rows, D_GROUP), lambda c: (rev(c), 1)),
                  pl.BlockSpec(memory_space=pl.ANY)],
        out_specs=[pl.BlockSpec((rows, 2 * D_GROUP), lambda c: (rev(c), VO_BLOCK)),
                   pl.BlockSpec((rows, 2 * D_GROUP), lambda c: (rev(c), 0)),
                   pl.BlockSpec((rows, LANES), lambda c: (rev(c), 0)),
                   pl.BlockSpec((1, D_GROUP), lambda c: (0, 0)),
                   pl.BlockSpec((1, 2 * D_GROUP), lambda c: (0, 0))],
        out_shape=[jax.ShapeDtypeStruct(d_proj.shape, d_proj.dtype), jax.ShapeDtypeStruct((s, 2 * D_GROUP), F32),
                   jax.ShapeDtypeStruct((s, LANES), F32), jax.ShapeDtypeStruct((1, D_GROUP), F32),
                   jax.ShapeDtypeStruct((1, 2 * D_GROUP), F32)],
        input_output_aliases={8: 0},
        scratch_shapes=[pltpu.VMEM((N_HEADS, D_HEAD, D_HEAD), F32), pltpu.VMEM((N_HEADS, 1, D_HEAD), F32)],
        compiler_params=_params(("arbitrary",)),
    )(qk, proj, gates, norm_w, ct_s, n_s, m_s, d_y, d_proj)


LN_TOKENS = 512
ATT_TOKENS = 512


def _proj_res_ln(a, w, xres, g, b, name):
    s, dm = xres.shape
    k = a.shape[1]
    tb = min(LN_TOKENS, s)

    def body(a_ref, w_ref, x_ref, g_ref, b_ref, z_ref, o_ref):
        halves = [slice(0, tb // 2), slice(tb // 2, tb)]
        zs = [ALPHA * x_ref[rows, :] + _nn_raw(a_ref[rows, :], w_ref[...]) for rows in halves]
        for rows, z in zip(halves, zs):
            z_ref[rows, :] = z
            o_ref[rows, :] = _layer_norm(z, g_ref[...], b_ref[...])

    tok = pl.BlockSpec((tb, dm), lambda i: (i, 0))
    vec = pl.BlockSpec((1, dm), lambda i: (0, 0))
    act = jax.ShapeDtypeStruct((s, dm), F32)
    return pl.pallas_call(
        body, name=name, grid=(s // tb,),
        in_specs=[pl.BlockSpec((tb, k), lambda i: (i, 0)), pl.BlockSpec((k, dm), lambda i: (0, 0)), tok, vec, vec],
        out_specs=[tok, tok], out_shape=[act, act], compiler_params=_params(("parallel",)),
    )(a, w, xres, g, b)


def _ln_bwd_proj(d_out, z, g, b, w, name):
    s, dm = z.shape
    k = w.shape[0]
    tb = min(LN_TOKENS, s)

    def body(do_ref, z_ref, g_ref, b_ref, w_ref, dz_ref, da_ref, dg_ref, db_ref):
        @pl.when(pl.program_id(0) == 0)
        def _():
            dg_ref[...] = jnp.zeros_like(dg_ref)
            db_ref[...] = jnp.zeros_like(db_ref)

        halves = [slice(0, tb // 2), slice(tb // 2, tb)]
        d_zs = []
        for rows in halves:
            _, vjp = jax.vjp(_layer_norm, z_ref[rows, :], g_ref[...], b_ref[...])
            d_z, d_g, d_b = vjp(do_ref[rows, :])
            dz_ref[rows, :] = d_z
            dg_ref[...] += d_g
            db_ref[...] += d_b
            d_zs.append(d_z)
        for rows, d_z in zip(halves, d_zs):
            da_ref[rows, :] = _nt_raw(d_z, w_ref[...])

    tok = pl.BlockSpec((tb, dm), lambda i: (i, 0))
    vec = pl.BlockSpec((1, dm), lambda i: (0, 0))
    return pl.pallas_call(
        body, name=name, grid=(s // tb,),
        in_specs=[tok, tok, vec, vec, pl.BlockSpec((k, dm), lambda i: (0, 0))],
        out_specs=[tok, pl.BlockSpec((tb, k), lambda i: (i, 0)), vec, vec],
        out_shape=[jax.ShapeDtypeStruct((s, dm), F32), jax.ShapeDtypeStruct((s, k), F32),
                   jax.ShapeDtypeStruct((1, dm), F32), jax.ShapeDtypeStruct((1, dm), F32)],
        compiler_params=_params(("arbitrary",)),
    )(d_out, z, g, b, w)


def _proj_loss_tail(a, w, xres, g, b, target):
    s, dm = xres.shape
    k = a.shape[1]
    tb = min(ATT_TOKENS, s)

    def loss_fn(z, gg, bb, tgt):
        err = jnp.square(_layer_norm(z, gg, bb) - tgt)
        return 0.5 * jnp.sum(jnp.mean(err, axis=-1, keepdims=True), axis=0, keepdims=True)

    def body(a_ref, w_ref, x_ref, g_ref, b_ref, t_ref, loss_ref, dz_ref, dg_ref, db_ref):
        @pl.when(pl.program_id(0) == 0)
        def _():
            loss_ref[...] = jnp.zeros_like(loss_ref)
            dg_ref[...] = jnp.zeros_like(dg_ref)
            db_ref[...] = jnp.zeros_like(db_ref)

        halves = [slice(0, tb // 2), slice(tb // 2, tb)]
        zs = [ALPHA * x_ref[rows, :] + _nn_raw(a_ref[rows, :], w_ref[...]) for rows in halves]
        for rows, z in zip(halves, zs):
            tgt = t_ref[rows, :]
            loss, vjp = jax.vjp(lambda zz, gg, bb, tgt=tgt: loss_fn(zz, gg, bb, tgt), z, g_ref[...], b_ref[...])
            d_z, d_g, d_b = vjp(jnp.ones((1, 1), F32))
            loss_ref[...] += loss
            dz_ref[rows, :] = d_z
            dg_ref[...] += d_g
            db_ref[...] += d_b

    tok = pl.BlockSpec((tb, dm), lambda i: (i, 0))
    vec = pl.BlockSpec((1, dm), lambda i: (0, 0))
    one = pl.BlockSpec((1, 1), lambda i: (0, 0))
    return pl.pallas_call(
        body, name="ffn_down_loss_tail", grid=(s // tb,),
        in_specs=[pl.BlockSpec((tb, k), lambda i: (i, 0)), pl.BlockSpec((k, dm), lambda i: (0, 0)), tok, vec, vec, tok],
        out_specs=[one, tok, vec, vec],
        out_shape=[jax.ShapeDtypeStruct((1, 1), F32), jax.ShapeDtypeStruct((s, dm), F32),
                   jax.ShapeDtypeStruct((1, dm), F32), jax.ShapeDtypeStruct((1, dm), F32)],
        compiler_params=_params(("arbitrary",)),
    )(a, w, xres, g, b, target)


def _att_heads(qs, ks, vs):
    sc = [_nt(q, k) * (CA_DH ** -0.5) for q, k in zip(qs, ks)]
    p = [jax.nn.softmax(s, axis=-1) for s in sc]
    return tuple(_nn(pp, v) for pp, v in zip(p, vs))


def _head_slices(ref_or_value, offset):
    return tuple(ref_or_value[:, offset + h * CA_DH:offset + (h + 1) * CA_DH] for h in range(CA_HEADS))


def _cross_attention_fwd(x1, kv, wq, wo, g, b):
    s = x1.shape[0]
    tb = min(ATT_TOKENS, s)

    def body(x_ref, kv_ref, wq_ref, wo_ref, g_ref, b_ref, att_ref, z_ref, o_ref):
        x_blk = x_ref[...]
        q = _nn_raw(x_blk, wq_ref[...])
        att = jnp.concatenate(_att_heads(_head_slices(q, 0), _head_slices(kv_ref, 0), _head_slices(kv_ref, D_MODEL)),
                              axis=1)
        att_ref[...] = att.astype(att_ref.dtype)
        z = ALPHA * x_blk + _nn_raw(att, wo_ref[...])
        z_ref[...] = z
        o_ref[...] = _layer_norm(z, g_ref[...], b_ref[...])

    tok = pl.BlockSpec((tb, D_MODEL), lambda i: (i, 0))
    mat = pl.BlockSpec((D_MODEL, D_MODEL), lambda i: (0, 0))
    vec = pl.BlockSpec((1, D_MODEL), lambda i: (0, 0))
    act = jax.ShapeDtypeStruct((s, D_MODEL), F32)
    return pl.pallas_call(
        body, name="cross_attention_fwd", grid=(s // tb,),
        in_specs=[tok, pl.BlockSpec((N_MEM, 2 * D_MODEL), lambda i: (0, 0)), mat, mat, vec, vec],
        out_specs=[tok, tok, tok],
        out_shape=[jax.ShapeDtypeStruct((s, D_MODEL), BF16), act, act],
        compiler_params=_params(("parallel",)),
    )(x1, kv, wq, wo, g, b)


def _cross_attention_bwd(d_x2, x1, z2, kv, wq, wo, g, b):
    s = x1.shape[0]
    tb = min(ATT_TOKENS, s)

    def body(dx2_ref, x_ref, z_ref, kv_ref, wq_ref, wo_ref, g_ref, b_ref,
             dx1_ref, dq_ref, dz_ref, dkv_ref, dg_ref, db_ref):
        @pl.when(pl.program_id(0) == 0)
        def _():
            dkv_ref[...] = jnp.zeros_like(dkv_ref)
            dg_ref[...] = jnp.zeros_like(dg_ref)
            db_ref[...] = jnp.zeros_like(db_ref)

        q = _nn_raw(x_ref[...], wq_ref[...])
        _, ln_vjp = jax.vjp(_layer_norm, z_ref[...], g_ref[...], b_ref[...])
        d_z, d_g, d_b = ln_vjp(dx2_ref[...])
        dg_ref[...] += d_g
        db_ref[...] += d_b
        dz_ref[...] = d_z.astype(dz_ref.dtype)
        d_att = _nt_raw(d_z, wo_ref[...])
        _, vjp = jax.vjp(_att_heads, _head_slices(q, 0), _head_slices(kv_ref, 0), _head_slices(kv_ref, D_MODEL))
        d_qs, d_ks, d_vs = vjp(_head_slices(d_att, 0))
        for h in range(CA_HEADS):
            lo = h * CA_DH
            dkv_ref[:, lo:lo + CA_DH] += d_ks[h]
            dkv_ref[:, D_MODEL + lo:D_MODEL + lo + CA_DH] += d_vs[h]
        d_q = jnp.concatenate(d_qs, axis=1)
        dq_ref[...] = d_q.astype(dq_ref.dtype)
        dx1_ref[...] = ALPHA * d_z + _nt_raw(d_q, wq_ref[...])

    tok = pl.BlockSpec((tb, D_MODEL), lambda i: (i, 0))
    mem = pl.BlockSpec((N_MEM, 2 * D_MODEL), lambda i: (0, 0))
    mat = pl.BlockSpec((D_MODEL, D_MODEL), lambda i: (0, 0))
    vec = pl.BlockSpec((1, D_MODEL), lambda i: (0, 0))
    low = jax.ShapeDtypeStruct((s, D_MODEL), BF16)
    return pl.pallas_call(
        body, name="cross_attention_bwd", grid=(s // tb,),
        in_specs=[tok, tok, tok, mem, mat, mat, vec, vec], out_specs=[tok, tok, tok, mem, vec, vec],
        out_shape=[jax.ShapeDtypeStruct((s, D_MODEL), F32), low, low,
                   jax.ShapeDtypeStruct((N_MEM, 2 * D_MODEL), F32),
                   jax.ShapeDtypeStruct((1, D_MODEL), F32), jax.ShapeDtypeStruct((1, D_MODEL), F32)],
        compiler_params=_params(("arbitrary",)),
    )(d_x2, x1, z2, kv, wq, wo, g, b)


def _local_step(x, mem, target, w, mid_weights=None, ffn_weights=None, down_weights=None, on_ffn_grads=None,
                on_mid_grads=None,
                on_small_grads=None, on_last_grads=None):
    w = dict(w)
    s = x.shape[0]
    tm = min(512, s)
    tt_big = min(1024, s)
    proj, gates = _input_projection(x, w["w_in_t"], w["w_in_gate_t"], w["b_in_main"], w["b_in_gate"],
                                    min(2048, s), 512)
    qk = _ml_conv_fwd(proj, w["ml_conv_w"], w["ml_conv_b"])
    y, hg_states = _hgrn2_fwd(proj, w["hg_lb_logits"], w["hg_norm_w"])
    y, ct_s, n_s, m_s = _mlstm_fwd(qk, proj, gates, w["ml_norm_w"], y)
    if mid_weights is not None:
        w.update(mid_weights(y))
    z1, x1 = _proj_res_ln(y, w["w_out"], x, w["ln1_g"], w["ln1_b"], "out_proj_ln1")
    kv = _matmul_nn(mem, w["ca_wkv"], None, N_MEM, CA_DH, "kv")
    att, z2, x2 = _cross_attention_fwd(x1, kv, w["ca_wq"], w["ca_wo"], w["ln2_g"], w["ln2_b"])
    if ffn_weights is not None:
        w.update(ffn_weights(x2))
    u = _matmul_nt(x2, w["ffn_w_up_t"], min(1024, s), UP_TILE, "ffn_up", out_dtype=BF16)
    hid = _ffn_conv_fwd(u, w["ffn_conv_w"], w["ffn_conv_b"])
    if down_weights is not None:
        w.update(down_weights(hid))
    loss, d_z3, d_ln3_g, d_ln3_b = _proj_loss_tail(hid, w["ffn_w_down"], x2, w["ln3_g"], w["ln3_b"], target)
    grads = {"ln3_g": d_ln3_g, "ln3_b": d_ln3_b}
    grads["ffn_w_down"] = _matmul_tn(hid, d_z3, UP_TILE, D_MODEL, tt_big, "d_w_down")
    d_hid = _matmul_nt(d_z3, w["ffn_w_down"], tm, D_FF, "d_hid", out_dtype=BF16)
    d_ug, d_uv, d_cwg, d_cwv, d_cbg, d_cbv = _ffn_conv_bwd(u, w["ffn_conv_w"], w["ffn_conv_b"], d_hid)
    grads["ffn_conv_w"] = jnp.concatenate([d_cwg, d_cwv], axis=-1)
    grads["ffn_conv_b"] = jnp.concatenate([d_cbg, d_cbv], axis=-1)
    d_w_up = _matmul_tn(d_ug, x2, UP_TILE, D_MODEL, tt_big, "d_w_up_gate", rows=D_UP)
    grads["ffn_w_up"] = _matmul_tn(d_uv, x2, UP_TILE, D_MODEL, tt_big, "d_w_up_val", rows=D_UP, row0=D_FF,
                                   into=d_w_up)
    d_x2 = _matmul_nn_sum([(d_ug, w["ffn_w_up_t"], 0), (d_uv, w["ffn_w_up_t"], D_FF)], d_z3, ALPHA,
                          min(256, s), "d_x2")
    if on_ffn_grads is not None:
        d_x2 = on_ffn_grads(grads, d_x2)
    d_x1, d_q, d_z2, d_kv, grads["ln2_g"], grads["ln2_b"] = _cross_attention_bwd(
        d_x2, x1, z2, kv, w["ca_wq"], w["ca_wo"], w["ln2_g"], w["ln2_b"])
    grads["ca_wo"] = _matmul_tn(att, d_z2, D_MODEL, D_MODEL, tt_big, "d_ca_wo")
    grads["ca_wq"] = _matmul_tn(x1, d_q, D_MODEL, D_MODEL, tt_big, "d_ca_wq")
    grads["ca_wkv"] = _matmul_tn(mem, d_kv, D_MODEL, CA_DH, N_MEM, "d_ca_wkv", shards=N_DEV, group=N_DEV)
    d_z1, d_y, grads["ln1_g"], grads["ln1_b"] = _ln_bwd_proj(d_x1, z1, w["ln1_g"], w["ln1_b"], w["w_out"],
                                                             "ln1_bwd_out_proj")
    grads["w_out"] = _matmul_tn(y, d_z1, D_MODEL, D_MODEL, tt_big, "d_w_out")
    if on_mid_grads is not None:
        d_y = on_mid_grads(grads, d_y)
    d_proj, grads["hg_lb_logits"], grads["hg_norm_w"], db_hg = _hgrn2_bwd(
        proj, w["hg_lb_logits"], w["hg_norm_w"], hg_states, d_y)
    d_proj, d_qk, d_gates, grads["ml_norm_w"], db_vo = _mlstm_bwd(
        qk, proj, gates, w["ml_norm_w"], ct_s, n_s, m_s, d_y, d_proj)
    d_proj, grads["ml_conv_w"], grads["ml_conv_b"], db_qk = _ml_conv_bwd(
        proj, w["ml_conv_w"], w["ml_conv_b"], d_qk, d_proj)
    grads["b_in_main"] = jnp.concatenate([db_hg, db_qk, db_vo], axis=-1)
    grads["b_in_gate"] = jnp.sum(d_gates, axis=0, keepdims=True)
    if on_small_grads is not None:
        d_proj = on_small_grads(grads, loss, d_proj)
    grads["w_in_main"], grads["w_in_gate"] = _input_projection_grads(d_proj, d_gates, x, min(2048, D_IN_MAIN), tt_big)
    if on_last_grads is not None:
        d_z1 = on_last_grads(grads, d_z1)
    grad_x = _matmul_nn_sum([(d_proj, w["w_in_t"], 0), (d_gates, w["w_in_gate_t"], 0)], d_z1, ALPHA, tm, "d_x")
    return loss, grad_x, grads


HBM_SPEC = pl.BlockSpec(memory_space=pltpu.HBM)


def _coords():
    return lax.axis_index("x"), lax.axis_index("y"), lax.axis_index("c")


def _other_chips(x, y):
    return [(1 - x, y), (x, 1 - y), (1 - x, 1 - y)]


def _my_slot():
    x, y, c = _coords()
    return 4 * x + 2 * y + c


SEM_SPEC = pl.BlockSpec(memory_space=pltpu.SEMAPHORE)
ANY_SPEC = pl.BlockSpec(memory_space=pl.ANY)
SIDE_EFFECT = pltpu.SideEffectType.DATAFLOW_SIDE_EFFECTING


def _peer(x, y, c, d):
    flip = lambda v, bit: 1 - v if bit else v
    p = (flip(x, d & 4), flip(y, d & 2), flip(c, d & 1))
    return p, 4 * p[0] + 2 * p[1] + p[2]


def _direct_copies(gather, src_refs, land_refs, send_sems, recv_sems):
    x, y, c = _coords()
    me = 4 * x + 2 * y + c
    copies = []
    for a in range(len(src_refs)):
        for d in range(1, N_DEV):
            peer, peer_slot = _peer(x, y, c, d)
            copies.append(pltpu.make_async_remote_copy(
                src_ref=src_refs[a] if gather else src_refs[a].at[peer_slot],
                dst_ref=land_refs[a].at[me] if gather else land_refs[a].at[d - 1],
                send_sem=send_sems.at[7 * a + d - 1], recv_sem=recv_sems.at[7 * a + d - 1],
                device_id=peer, device_id_type=MESH))
    return copies


def _hbm(t):
    return pltpu.HBM(t.shape, t.dtype)


def _chip_copies(src_refs, land_refs, send_sems, recv_sems):
    x, y, c = _coords()
    me = 4 * x + 2 * y + c
    targets = [(x, y, 1 - c)] + [(cx, cy, c) for cx, cy in _other_chips(x, y)]
    return [pltpu.make_async_remote_copy(
        src_ref=src_refs[a], dst_ref=land_refs[a].at[me], send_sem=send_sems.at[4 * a + k],
        recv_sem=recv_sems.at[4 * a + k], device_id=target, device_id_type=MESH)
        for a in range(len(src_refs)) for k, target in enumerate(targets)]


def _forward_copies(land_refs, send_sems, recv_sems):
    x, y, c = _coords()
    return [pltpu.make_async_remote_copy(
        src_ref=land_refs[a].at[4 * cx + 2 * cy + c], dst_ref=land_refs[a].at[4 * cx + 2 * cy + c],
        send_sem=send_sems.at[3 * a + j], recv_sem=recv_sems.at[3 * a + j],
        device_id=(x, y, 1 - c), device_id_type=MESH)
        for a in range(len(land_refs)) for j, (cx, cy) in enumerate(_other_chips(x, y))]


def _split_copy_start(make_copies, n_sems, operands, through, name):
    n_ops = len(operands)

    def body(*refs):
        for cp in make_copies(refs[:n_ops], refs[n_ops + 1], refs[n_ops + 2]):
            cp.start()

    ins = [pltpu.with_memory_space_constraint(t, pltpu.HBM) for t in (*operands, through)]
    sems = pltpu.SemaphoreType.DMA((n_sems,))
    res = pl.pallas_call(
        body, name=name, out_shape=(sems, sems, *[_hbm(t) for t in ins]),
        in_specs=[HBM_SPEC] * (n_ops + 1), out_specs=(SEM_SPEC, SEM_SPEC, *[HBM_SPEC] * (n_ops + 1)),
        input_output_aliases={i: 2 + i for i in range(n_ops + 1)},
        compiler_params=pltpu.CompilerParams(has_side_effects=SIDE_EFFECT),
    )(*ins)
    return (res[0], res[1], list(res[2:2 + n_ops])), res[2 + n_ops]


def _split_copy_wait(make_copies, started, after, name):
    send_sems, recv_sems, operands = started
    n_ops = len(operands)
    after = list(after) if isinstance(after, (list, tuple)) else [after]

    def body(*refs):
        for cp in make_copies(refs[:n_ops], refs[n_ops], refs[n_ops + 1]):
            cp.wait_send()
            cp.wait_recv()

    res = pl.pallas_call(
        body, name=name, out_shape=tuple(_hbm(t) for t in operands),
        in_specs=[HBM_SPEC] * n_ops + [SEM_SPEC, SEM_SPEC] + [ANY_SPEC] * len(after),
        out_specs=tuple([HBM_SPEC] * n_ops), input_output_aliases={i: i for i in range(n_ops)},
        compiler_params=pltpu.CompilerParams(has_side_effects=SIDE_EFFECT),
    )(*operands, send_sems, recv_sems, *after)
    return list(res)


def _halves(make_copies, na):
    return lambda refs, send_sems, recv_sems: make_copies(refs[:na], refs[na:], send_sems, recv_sems)


def _direct_start(gather, arrays, through, name):
    na = len(arrays)
    lands = [lax.empty((N_DEV,) + t.shape if gather else (N_DEV - 1,) + t.shape[1:], t.dtype) for t in arrays]
    return _split_copy_start(_halves(functools.partial(_direct_copies, gather), na), 7 * na, [*arrays, *lands],
                             through, name)


def _direct_wait(gather, started, after, name):
    na = len(started[2]) // 2
    operands = _split_copy_wait(_halves(functools.partial(_direct_copies, gather), na), started, after, name)
    return operands[:na], operands[na:]


def _two_level_gather(shards, glue, name):
    na = len(shards)
    lands = [lax.empty((N_DEV,) + t.shape, t.dtype) for t in shards]
    nothing = jnp.zeros((SUBLANES, LANES), F32)
    started, _ = _split_copy_start(_halves(_chip_copies, na), 4 * na, [*shards, *lands], nothing, name + "_start")
    operands = _split_copy_wait(_halves(_chip_copies, na), started, glue, name + "_wait")
    started, mine = _split_copy_start(_forward_copies, 3 * na, operands[na:], operands[0], name + "_forward_start")
    lands = _split_copy_wait(_forward_copies, started, mine, name + "_forward_wait")
    return [lax.dynamic_update_index_in_dim(land, own, _my_slot(), 0)
            for own, land in zip([mine, *operands[1:na]], lands)]


def _row_tile(rows):
    for t in (256, 176, 128):
        if rows % t == 0 and rows > t:
            return t
    return rows


def _adamw_math(g, w, m, v):
    m_new = ADAM_B1 * m + (1.0 - ADAM_B1) * g
    v_new = ADAM_B2 * v + (1.0 - ADAM_B2) * jnp.square(g)
    m_hat = m_new / (1.0 - ADAM_B1 ** ADAM_STEP)
    v_hat = v_new / (1.0 - ADAM_B2 ** ADAM_STEP)
    delta = -ADAM_LR * (m_hat / (jnp.sqrt(v_hat) + ADAM_EPS) + ADAM_WD * w)
    return delta, m_new, v_new


def _adamw_sharded(chip, sums, got, w, m, v, name):
    r, c = w.shape
    tr = _row_tile(r)
    n_got = got.shape[0]

    def body(chip_ref, s_ref, g_ref, w_ref, m_ref, v_ref, go_ref, d_ref, nm_ref, nv_ref):
        g = s_ref[...].astype(F32)
        for i in range(n_got):
            g = g + g_ref[i].astype(F32)
        delta, m_new, v_new = _adamw_math(g, w_ref[...], m_ref[...], v_ref[...])
        go_ref[...] = g
        d_ref[...] = delta
        nm_ref[...] = m_new
        nv_ref[...] = v_new

    blk = pl.BlockSpec((tr, c), lambda i, chip_ref: (i, 0))
    out = jax.ShapeDtypeStruct((r, c), F32)
    return pl.pallas_call(
        body, name=name,
        grid_spec=pltpu.PrefetchScalarGridSpec(
            num_scalar_prefetch=1, grid=(r // tr,),
            in_specs=[pl.BlockSpec((None, tr, c), lambda i, chip_ref: (chip_ref[0], i, 0)),
                      pl.BlockSpec((n_got, tr, c), lambda i, chip_ref: (0, i, 0)), blk, blk, blk],
            out_specs=[blk, blk, blk, blk]),
        out_shape=[out, out, out, out],
        compiler_params=_params(("parallel",)),
    )(chip, sums, got, w, m, v)


def _adamw_replicated(parts, w, m, v):
    p, r, c = parts.shape
    names = SMALL_NAMES
    shapes = [w[n].shape for n in names]

    def body(*refs):
        p_ref = refs[0]
        ins = refs[1:1 + 3 * len(names)]
        outs = refs[1 + 3 * len(names):-2]
        loss_ref, sum_scr = refs[-2], refs[-1]
        total = p_ref[0]
        for i in range(1, p):
            total = total + p_ref[i]
        sum_scr[...] = total
        for k, n in enumerate(names):
            w_ref, m_ref, v_ref = ins[3 * k:3 * k + 3]
            g_ref, d_ref, nm_ref, nv_ref = outs[4 * k:4 * k + 4]
            for row, lane0, width, src_row in _small_pieces(n, shapes[k]):
                here = (slice(row, row + 1), slice(lane0, lane0 + width))
                g = sum_scr[src_row:src_row + 1, 0:width]
                delta, m_new, v_new = _adamw_math(g, w_ref[here], m_ref[here], v_ref[here])
                g_ref[here] = g
                d_ref[here] = delta
                nm_ref[here] = m_new
                nv_ref[here] = v_new
        loss_ref[...] = sum_scr[SMALL_LOSS_ROW:SMALL_LOSS_ROW + 1, 0:1]

    whole = lambda shape: pl.BlockSpec(shape, lambda i: (0,) * len(shape))
    args = [parts] + [t[n] for n in names for t in (w, m, v)]
    out_shape = [jax.ShapeDtypeStruct(s, F32) for s in shapes for _ in range(4)] + [jax.ShapeDtypeStruct((1, 1), F32)]
    res = pl.pallas_call(
        body, name="adamw_replicated", grid=(1,),
        in_specs=[whole(t.shape) for t in args], out_specs=[whole(s.shape) for s in out_shape],
        out_shape=out_shape, scratch_shapes=[pltpu.VMEM((r, c), F32)],
        compiler_params=_params(("arbitrary",)),
    )(*args)
    results = [{n: res[4 * k + j] for k, n in enumerate(names)} for j in range(4)]
    return results, res[-1]


SHARDED_NAMES = ("w_in", "ml_conv_w", "w_out", "ca_wq", "ca_wkv", "ca_wo", "ffn_w_up", "ffn_conv_w", "ffn_w_down")
SMALL_NAMES = ("b_in", "hg_lb_logits", "hg_norm_w", "ml_conv_b", "ml_norm_w", "ln1_g", "ln1_b",
               "ln2_g", "ln2_b", "ffn_conv_b", "ln3_g", "ln3_b")
WEIGHT_NAMES = ("w_in", "b_in", "hg_lb_logits", "hg_norm_w", "ml_conv_w", "ml_conv_b", "ml_norm_w", "w_out",
                "ln1_g", "ln1_b", "ca_wq", "ca_wkv", "ca_wo", "ln2_g", "ln2_b", "ffn_w_up", "ffn_conv_w",
                "ffn_conv_b", "ffn_w_down", "ln3_g", "ln3_b")
PAD_TO = {"ffn_conv_w": UP_SHARD_P}
SMALL_ROWS = 24
SMALL_W = D_MODEL
SMALL_SHAPES = {"b_in": (1, D_IN), "hg_lb_logits": (2, D_GROUP), "hg_norm_w": (1, D_GROUP),
                "ml_conv_b": (1, 2 * D_GROUP), "ml_norm_w": (1, D_GROUP), "ln1_g": (1, D_MODEL), "ln1_b": (1, D_MODEL),
                "ln2_g": (1, D_MODEL), "ln2_b": (1, D_MODEL), "ffn_conv_b": (1, D_UP), "ln3_g": (1, D_MODEL),
                "ln3_b": (1, D_MODEL)}


def _shard_2d(name, block):
    t = block[0]
    if name in PAD_TO:
        t = jnp.pad(t, ((0, 0), (0, PAD_TO[name] - t.shape[1])))
    return t


TRANSPOSED = ("w_in", "ffn_w_up")


def _update_shard(name, block):
    if name not in TRANSPOSED:
        return _shard_2d(name, block)
    t = jnp.transpose(block[0])
    rows = PAD_TO.get(name, t.shape[0])
    return jnp.pad(t, ((0, rows - t.shape[0]), (0, 0)))


def _shard_like(name, t, like):
    if name in TRANSPOSED:
        return jnp.transpose(t[:like.shape[2]])[None]
    return t[:, :like.shape[2]][None]


def _pad_cols(t, width):
    return jnp.pad(t, ((0, 0), (0, width - t.shape[1])))


FIRST_NAMES = ("w_in", "ml_conv_w")
FFN_NAMES = ("ffn_w_up", "ffn_w_down", "ffn_conv_w")
MID_NAMES = ("ca_wo", "ca_wq", "ca_wkv", "w_out")


def _first_weights(g, small):
    w = dict(small)
    w["w_in_t"] = g["w_in"].reshape(D_IN, D_MODEL)
    w["w_in_gate_t"] = jnp.pad(w["w_in_t"][D_IN_MAIN:], ((0, LANES - (D_IN - D_IN_MAIN)), (0, 0)))
    w["b_in_main"] = small["b_in"][:, :D_IN_MAIN]
    w["b_in_gate"] = _pad_cols(small["b_in"][:, D_IN_MAIN:], LANES)
    w["ml_conv_w"] = jnp.transpose(g["ml_conv_w"], (1, 0, 2)).reshape(ML_CONV, 2 * D_GROUP)
    return w


def _mid_weights(g):
    w = {n: g[n].reshape(D_MODEL, D_MODEL) for n in ("w_out", "ca_wq", "ca_wo")}
    w["ca_wkv"] = g["ca_wkv"]
    return w


FFN_UP_NAMES = ("ffn_w_up", "ffn_conv_w")
FFN_DOWN_NAMES = ("ffn_w_down",)


def _ffn_up_weights(g, small):
    w = {"ffn_w_up_t": g["ffn_w_up"].reshape(D_UP, D_MODEL)}
    w["ffn_conv_w"] = jnp.transpose(g["ffn_conv_w"][:, :, :UP_SHARD], (1, 0, 2)).reshape(FFN_CONV, D_UP)
    w["ffn_conv_b"] = small["ffn_conv_b"].reshape(1, D_UP)
    return w


def _ffn_down_weights(g):
    return {"ffn_w_down": g["ffn_w_down"].reshape(D_FF, D_MODEL)}


def _owner_stack(n, grads):
    if n == "w_in":
        rows = jnp.concatenate([grads["w_in_main"], grads["w_in_gate"][:D_IN - D_IN_MAIN]], axis=0)
        return rows.reshape(N_DEV, W_IN_SHARD, D_MODEL)
    if n == "ffn_w_up":
        return grads[n].reshape(N_DEV, UP_SHARD, D_MODEL)
    if n in ("w_out", "ca_wq", "ca_wo"):
        return grads[n].reshape(N_DEV, D_MODEL // N_DEV, D_MODEL)
    if n == "ffn_w_down":
        return grads[n].reshape(N_DEV, D_FF // N_DEV, D_MODEL)
    if n == "ml_conv_w":
        return jnp.transpose(grads[n].reshape(ML_CONV, N_DEV, LANES), (1, 0, 2))
    if n == "ffn_conv_w":
        shards = jnp.transpose(grads[n].reshape(FFN_CONV, N_DEV, UP_SHARD), (1, 0, 2))
        return jnp.pad(shards, ((0, 0), (0, 0), (0, UP_SHARD_P - UP_SHARD)))
    return grads[n]


def _small_grads(grads):
    out = {n: grads[n] for n in SMALL_NAMES if n in grads}
    out["b_in"] = jnp.concatenate([grads["b_in_main"], grads["b_in_gate"][:, :D_IN - D_IN_MAIN]], axis=1)
    return out


def _small_rows(shape):
    return shape[0] if shape[1] <= SMALL_W else -(-shape[1] // SMALL_W)


SMALL_BASE = {n: sum(_small_rows(SMALL_SHAPES[k]) for k in SMALL_NAMES[:i]) for i, n in enumerate(SMALL_NAMES)}
SMALL_LOSS_ROW = sum(_small_rows(SMALL_SHAPES[n]) for n in SMALL_NAMES)
assert SMALL_LOSS_ROW < SMALL_ROWS


def _small_pieces(name, shape):
    base = SMALL_BASE[name]
    if shape[1] <= SMALL_W:
        return [(i, 0, shape[1], base + i) for i in range(shape[0])]
    return [(0, k * SMALL_W, min(SMALL_W, shape[1] - k * SMALL_W), base + k) for k in range(_small_rows(shape))]


def _pack_small(p, loss):
    rows = []
    for n in SMALL_NAMES:
        t = p[n]
        nrows = _small_rows(t.shape)
        if t.shape[1] <= SMALL_W:
            rows.append(_pad_cols(t, SMALL_W))
        else:
            rows.append(_pad_cols(t, nrows * SMALL_W).reshape(nrows, SMALL_W))
    rows.append(_pad_cols(loss, SMALL_W))
    slab = jnp.concatenate(rows, axis=0)
    return jnp.pad(slab, ((0, SMALL_ROWS - slab.shape[0]), (0, 0)))


def kernel(x, mem, w_in, b_in, hg_lb_logits, hg_norm_w, ml_conv_w, ml_conv_b, ml_norm_w, w_out, ln1_g, ln1_b, ca_wq, ca_wkv, ca_wo, ln2_g, ln2_b, ffn_w_up, ffn_conv_w, ffn_conv_b, ffn_w_down, ln3_g, ln3_b, loss_target, m_w_in, m_b_in, m_hg_lb_logits, m_hg_norm_w, m_ml_conv_w, m_ml_conv_b, m_ml_norm_w, m_w_out, m_ln1_g, m_ln1_b, m_ca_wq, m_ca_wkv, m_ca_wo, m_ln2_g, m_ln2_b, m_ffn_w_up, m_ffn_conv_w, m_ffn_conv_b, m_ffn_w_down, m_ln3_g, m_ln3_b, v_w_in, v_b_in, v_hg_lb_logits, v_hg_norm_w, v_ml_conv_w, v_ml_conv_b, v_ml_norm_w, v_w_out, v_ln1_g, v_ln1_b, v_ca_wq, v_ca_wkv, v_ca_wo, v_ln2_g, v_ln2_b, v_ffn_w_up, v_ffn_conv_w, v_ffn_conv_b, v_ffn_w_down, v_ln3_g, v_ln3_b):
    params = dict(w_in=w_in, b_in=b_in, hg_lb_logits=hg_lb_logits, hg_norm_w=hg_norm_w, ml_conv_w=ml_conv_w,
                  ml_conv_b=ml_conv_b, ml_norm_w=ml_norm_w, w_out=w_out, ln1_g=ln1_g, ln1_b=ln1_b, ca_wq=ca_wq,
                  ca_wkv=ca_wkv, ca_wo=ca_wo, ln2_g=ln2_g, ln2_b=ln2_b, ffn_w_up=ffn_w_up, ffn_conv_w=ffn_conv_w,
                  ffn_conv_b=ffn_conv_b, ffn_w_down=ffn_w_down, ln3_g=ln3_g, ln3_b=ln3_b)
    mom1 = dict(w_in=m_w_in, b_in=m_b_in, hg_lb_logits=m_hg_lb_logits, hg_norm_w=m_hg_norm_w,
                ml_conv_w=m_ml_conv_w, ml_conv_b=m_ml_conv_b, ml_norm_w=m_ml_norm_w, w_out=m_w_out, ln1_g=m_ln1_g,
                ln1_b=m_ln1_b, ca_wq=m_ca_wq, ca_wkv=m_ca_wkv, ca_wo=m_ca_wo, ln2_g=m_ln2_g, ln2_b=m_ln2_b,
                ffn_w_up=m_ffn_w_up, ffn_conv_w=m_ffn_conv_w, ffn_conv_b=m_ffn_conv_b, ffn_w_down=m_ffn_w_down,
                ln3_g=m_ln3_g, ln3_b=m_ln3_b)
    mom2 = dict(w_in=v_w_in, b_in=v_b_in, hg_lb_logits=v_hg_lb_logits, hg_norm_w=v_hg_norm_w,
                ml_conv_w=v_ml_conv_w, ml_conv_b=v_ml_conv_b, ml_norm_w=v_ml_norm_w, w_out=v_w_out, ln1_g=v_ln1_g,
                ln1_b=v_ln1_b, ca_wq=v_ca_wq, ca_wkv=v_ca_wkv, ca_wo=v_ca_wo, ln2_g=v_ln2_g, ln2_b=v_ln2_b,
                ffn_w_up=v_ffn_w_up, ffn_conv_w=v_ffn_conv_w, ffn_conv_b=v_ffn_conv_b, ffn_w_down=v_ffn_w_down,
                ln3_g=v_ln3_g, ln3_b=v_ln3_b)

    x_idx, y_idx, c_idx = _coords()
    as_index = lambda v: jnp.reshape(v, (1,)).astype(jnp.int32)
    me = as_index(4 * x_idx + 2 * y_idx + c_idx)
    small_params = {n: params[n] for n in SMALL_NAMES}

    shards = {n: _update_shard(n, params[n]) for n in SHARDED_NAMES}
    m_shards = {n: _update_shard(n, mom1[n]) for n in SHARDED_NAMES}
    v_shards = {n: _update_shard(n, mom2[n]) for n in SHARDED_NAMES}
    outgoing = {n: shards[n] if "conv" in n else shards[n].astype(BF16) for n in SHARDED_NAMES}
    to_send = lambda names: [outgoing[n] for n in names]
    glue = [*m_shards.values(), *v_shards.values(), *shards.values(),
            *[outgoing[n] for n in SHARDED_NAMES if n not in FIRST_NAMES]]
    first = dict(zip(FIRST_NAMES, _two_level_gather(to_send(FIRST_NAMES), glue, "weights_gather_first")))
    mid_started, through = _direct_start(True, to_send(MID_NAMES), first["w_in"], "weights_gather_start_mid")
    ffn_started, through = _direct_start(True, to_send(FFN_UP_NAMES), through, "weights_gather_start_ffn_up")
    down_started, first["w_in"] = _direct_start(True, to_send(FFN_DOWN_NAMES), through,
                                                "weights_gather_start_ffn_down")

    def gathered_weights(names, started, after, tag):
        mine, lands = _direct_wait(True, started, after, "weights_gather_wait_" + tag)
        return {n: lax.dynamic_update_index_in_dim(land, own, me[0], 0) for n, own, land in zip(names, mine, lands)}

    started = {}

    def start_group(names, tag):
        def hook(grads, through):
            stacks = [_owner_stack(n, grads).astype(BF16) for n in names]
            started[tag], through = _direct_start(False, stacks, through, "grads_start_" + tag)
            return through
        return hook

    def start_small(grads, loss, through):
        started["small"], through = _direct_start(True, [_pack_small(_small_grads(grads), loss)], through,
                                                  "small_gather_start")
        return through

    loss, grad_x, grads = _local_step(
        x[0], mem[0], loss_target[0], _first_weights(first, small_params),
        lambda y: _mid_weights(gathered_weights(MID_NAMES, mid_started, y, "mid")),
        lambda x2: _ffn_up_weights(gathered_weights(FFN_UP_NAMES, ffn_started, x2, "ffn_up"), small_params),
        lambda hid: _ffn_down_weights(gathered_weights(FFN_DOWN_NAMES, down_started, hid, "ffn_down")),
        start_group(FFN_NAMES, "ffn"), start_group(MID_NAMES, "mid"), start_small, start_group(FIRST_NAMES, "last"))

    sharded_out = {}

    def update_group(names, tag, after):
        stacks, lands = _direct_wait(False, started[tag], after, "grads_wait_" + tag)
        for n, st, land in zip(names, stacks, lands):
            res = _adamw_sharded(me, st, land, shards[n], m_shards[n], v_shards[n], "adamw_" + n)
            sharded_out[n] = [_shard_like(n, t, params[n]) for t in res]

    update_group(FFN_NAMES, "ffn", grad_x)
    update_group(MID_NAMES, "mid", grad_x)
    own_small, small_lands = _direct_wait(True, started["small"], grad_x, "small_gather_wait")
    small_parts = lax.dynamic_update_index_in_dim(small_lands[0], own_small[0], me[0], 0)
    small_out, total_loss = _adamw_replicated(small_parts, small_params, {n: mom1[n] for n in SMALL_NAMES},
                                              {n: mom2[n] for n in SMALL_NAMES})
    done = [t for n in FFN_NAMES + MID_NAMES for t in sharded_out[n]]
    done += [t for small in small_out for t in small.values()]
    update_group(FIRST_NAMES, "last", done)

    outs = []
    for k, small in enumerate(small_out):
        outs.extend(sharded_out[n][k] if n in sharded_out else small[n] for n in WEIGHT_NAMES)
    return (total_loss[0, 0], grad_x[None], *outs)
```

```python
import functools
import math

import jax
import jax.numpy as jnp
from jax import lax
from jax.experimental import pallas as pl
from jax.experimental.layout import Layout, with_layout_constraint
from jax.experimental.pallas import tpu as pltpu

F32 = jnp.float32
BF16 = jnp.bfloat16
HIGHEST = lax.Precision.HIGHEST
MESH = pl.DeviceIdType.MESH

N_DEV = 8
D_MODEL = 1024
N_MEM = 256
N_HEADS = 4
D_HEAD = 128
D_GROUP = N_HEADS * D_HEAD
CHUNK = 64
ML_CONV = 4
FFN_CONV = 3
D_FF = 2816
D_UP = 2 * D_FF
CA_HEADS = 4
CA_DH = D_MODEL // CA_HEADS
LANES = 128
SUBLANES = 8
D_IN = 8 * D_GROUP + 2 * N_HEADS
D_IN_MAIN = 8 * D_GROUP
W_IN_SHARD = D_IN // N_DEV
UP_SHARD = D_UP // N_DEV
UP_SHARD_P = 768
UP_TILE = D_UP // 4
ALPHA = 2.0 ** 0.25
LN_EPS = 1e-5
NEG_BIG = -1e30
ADAM_LR = 0.001
ADAM_B1 = 0.9
ADAM_B2 = 0.999
ADAM_EPS = 1e-08
ADAM_WD = 0.01
ADAM_STEP = 10
VMEM_LIMIT = 56 * 1024 * 1024

SEG_MQ = 4 * D_GROUP // LANES
VO_BLOCK = 3


def _params(sem):
    return pltpu.CompilerParams(dimension_semantics=sem, vmem_limit_bytes=VMEM_LIMIT)


def _dg(a, b, ca, cb, precision=None):
    return lax.dot_general(a, b, (((ca,), (cb,)), ((), ())), precision=precision,
                           preferred_element_type=F32)


def _nn_raw(a, b):
    return _dg(a.astype(BF16), b.astype(BF16), 1, 0)


def _nt_raw(a, b):
    return _dg(a.astype(BF16), b.astype(BF16), 1, 1)


def _tn_raw(a, b):
    return _dg(a.astype(BF16), b.astype(BF16), 0, 0)


@jax.custom_vjp
def _nn(a, b):
    return _nn_raw(a, b)


_nn.defvjp(lambda a, b: (_nn_raw(a, b), (a, b)),
           lambda res, g: (_nt_raw(g, res[1]), _tn_raw(res[0], g)))


@jax.custom_vjp
def _nt(a, b):
    return _nt_raw(a, b)


_nt.defvjp(lambda a, b: (_nt_raw(a, b), (a, b)),
           lambda res, g: (_nn_raw(g, res[1]), _tn_raw(g, res[0])))


@jax.custom_vjp
def _tn(a, b):
    return _tn_raw(a, b)


_tn.defvjp(lambda a, b: (_tn_raw(a, b), (a, b)),
           lambda res, g: (_nt_raw(res[1], g), _nn_raw(res[0], g)))


def _layer_norm(z, g, b):
    mu = jnp.mean(z, axis=-1, keepdims=True)
    var = jnp.mean(jnp.square(z - mu), axis=-1, keepdims=True)
    return (z - mu) * lax.rsqrt(var + LN_EPS) * g + b


def _matmul_nn(a, w, bias, tm, tn, name, out_dtype=F32):
    m, k = a.shape
    if w.ndim == 3:
        n = w.shape[0] * w.shape[2]
        assert tn == w.shape[2]
        w_spec = pl.BlockSpec((None, k, tn), lambda i, j: (j, 0, 0))
    else:
        n = w.shape[1]
        w_spec = pl.BlockSpec((k, tn), lambda i, j: (0, j))

    def body(*refs):
        a_ref, w_ref = refs[0], refs[1]
        o_ref = refs[-1]
        acc = _nn_raw(a_ref[...], w_ref[...])
        if bias is not None:
            acc = acc + refs[2][...]
        o_ref[...] = acc.astype(o_ref.dtype)

    in_specs = [pl.BlockSpec((tm, k), lambda i, j: (i, 0)), w_spec]
    args = [a, w]
    if bias is not None:
        in_specs.append(pl.BlockSpec((1, tn), lambda i, j: (0, j)))
        args.append(bias)
    return pl.pallas_call(
        body, name=name, grid=(m // tm, n // tn), in_specs=in_specs,
        out_specs=pl.BlockSpec((tm, tn), lambda i, j: (i, j)),
        out_shape=jax.ShapeDtypeStruct((m, n), out_dtype),
        compiler_params=_params(("parallel", "parallel")),
    )(*args)


def _matmul_nt(d, w, tm, tk, name, k_out=None, bias=None, out_dtype=F32):
    m, n = d.shape
    k = k_out or w.shape[0]

    def body(*refs):
        acc = _nt_raw(refs[0][...], refs[1][...])
        if bias is not None:
            acc = acc + refs[2][...]
        refs[-1][...] = acc.astype(refs[-1].dtype)

    in_specs = [pl.BlockSpec((tm, n), lambda i, j: (i, 0)), pl.BlockSpec((tk, n), lambda i, j: (j, 0))]
    args = [d, w]
    if bias is not None:
        in_specs.append(pl.BlockSpec((1, tk), lambda i, j: (0, j)))
        args.append(bias)
    return pl.pallas_call(
        body, name=name, grid=(m // tm, k // tk), in_specs=in_specs,
        out_specs=pl.BlockSpec((tm, tk), lambda i, j: (i, j)),
        out_shape=jax.ShapeDtypeStruct((m, k), out_dtype),
        compiler_params=_params(("parallel", "parallel")),
    )(*args)


def _input_projection(x, w_t, w_gate_t, b_main, b_gate, tm, tk):
    m, n = x.shape

    def body(x_ref, w_ref, wg_ref, b_ref, bg_ref, o_ref, g_ref):
        lhs = x_ref[...].astype(BF16)
        o_ref[...] = _nt_raw(lhs, w_ref[...]) + b_ref[...]

        @pl.when(pl.program_id(1) == 0)
        def _():
            g_ref[...] = _nt_raw(lhs, wg_ref[...]) + bg_ref[...]

    return pl.pallas_call(
        body, name="proj", grid=(m // tm, D_IN_MAIN // tk),
        in_specs=[pl.BlockSpec((tm, n), lambda i, j: (i, 0)), pl.BlockSpec((tk, n), lambda i, j: (j, 0)),
                  pl.BlockSpec((LANES, n), lambda i, j: (0, 0)), pl.BlockSpec((1, tk), lambda i, j: (0, j)),
                  pl.BlockSpec((1, LANES), lambda i, j: (0, 0))],
        out_specs=[pl.BlockSpec((tm, tk), lambda i, j: (i, j)), pl.BlockSpec((tm, LANES), lambda i, j: (i, 0))],
        out_shape=[jax.ShapeDtypeStruct((m, D_IN_MAIN), F32), jax.ShapeDtypeStruct((m, LANES), F32)],
        compiler_params=_params(("parallel", "arbitrary")),
    )(x, w_t, w_gate_t, b_main, b_gate)


def _input_projection_grads(d_proj, d_gates, x, tm, tt):
    t, m = d_proj.shape
    n = x.shape[1]
    last = t // tt - 1

    def body(a_ref, g_ref, x_ref, o_ref, og_ref, acc_ref, accg_ref):
        i, kk = pl.program_id(0), pl.program_id(1)

        @pl.when(kk == 0)
        def _():
            acc_ref[...] = jnp.zeros_like(acc_ref)

        @pl.when((kk == 0) & (i == 0))
        def _():
            accg_ref[...] = jnp.zeros_like(accg_ref)

        rhs = x_ref[...].astype(BF16)
        acc_ref[...] += _tn_raw(a_ref[...], rhs)

        @pl.when(i == 0)
        def _():
            accg_ref[...] += _tn_raw(g_ref[...], rhs)

        @pl.when(kk == last)
        def _():
            o_ref[...] = acc_ref[...].astype(o_ref.dtype)

        @pl.when((kk == last) & (i == 0))
        def _():
            og_ref[...] = accg_ref[...].astype(og_ref.dtype)

    return pl.pallas_call(
        body, name="d_w_in", grid=(m // tm, t // tt),
        in_specs=[pl.BlockSpec((tt, tm), lambda i, kk: (kk, i)), pl.BlockSpec((tt, LANES), lambda i, kk: (kk, 0)),
                  pl.BlockSpec((tt, n), lambda i, kk: (kk, 0))],
        out_specs=[pl.BlockSpec((tm, n), lambda i, kk: (i, 0)), pl.BlockSpec((LANES, n), lambda i, kk: (0, 0))],
        out_shape=[jax.ShapeDtypeStruct((m, n), BF16), jax.ShapeDtypeStruct((LANES, n), BF16)],
        scratch_shapes=[pltpu.VMEM((tm, n), F32), pltpu.VMEM((LANES, n), F32)],
        compiler_params=_params(("arbitrary", "arbitrary")),
    )(d_proj, d_gates, x)


def _matmul_nn_sum(pairs, add, scale, tm, name):
    m = pairs[0][0].shape[0]
    n = pairs[0][1].shape[1]
    in_specs, args = [], []
    for a, w, row0 in pairs:
        kk = a.shape[1]
        in_specs += [pl.BlockSpec((tm, kk), lambda i: (i, 0)),
                     pl.BlockSpec((kk, n), lambda i, blk=row0 // kk: (blk, 0))]
        args += [a, w]
    if add is not None:
        in_specs.append(pl.BlockSpec((tm, n), lambda i: (i, 0)))
        args.append(add)

    def body(*refs):
        acc = None
        for p in range(len(pairs)):
            term = _nn_raw(refs[2 * p][...], refs[2 * p + 1][...])
            acc = term if acc is None else acc + term
        if add is not None:
            acc = acc + scale * refs[2 * len(pairs)][...]
        refs[-1][...] = acc

    return pl.pallas_call(
        body, name=name, grid=(m // tm,), in_specs=in_specs,
        out_specs=pl.BlockSpec((tm, n), lambda i: (i, 0)),
        out_shape=jax.ShapeDtypeStruct((m, n), F32),
        compiler_params=_params(("parallel",)),
    )(*args)


def _matmul_tn(a, b, tm, tn, tt, name, shards=None, shard0=0, group=1, into=None, colsum=False, rows=None, row0=0):
    t, m = a.shape
    n = b.shape[1]
    assert not colsum or tm == m
    n_in = 2 + (into is not None)
    out_dtype = BF16
    per_step = 1 if shards is None else group
    width = per_step * tn

    def body(*refs):
        a_ref, b_ref = refs[0], refs[1]
        o_ref, acc_ref = refs[n_in], refs[-1]
        first = pl.program_id(2) == 0

        @pl.when(first)
        def _():
            acc_ref[...] = jnp.zeros_like(acc_ref)

        if shards is None:
            acc_ref[...] += _tn_raw(a_ref[...], b_ref[...])
        else:
            lhs = a_ref[...].astype(BF16)
            for g in range(per_step):
                acc_ref[g] += _tn_raw(lhs, b_ref[:, g * tn:(g + 1) * tn])

        @pl.when(pl.program_id(2) == t // tt - 1)
        def _():
            o_ref[...] = acc_ref[...].astype(o_ref.dtype)

        if colsum:
            s_ref = refs[n_in + 1]

            @pl.when(first)
            def _():
                s_ref[...] = jnp.zeros_like(s_ref)

            s_ref[...] += jnp.sum(b_ref[...], axis=0, keepdims=True)

    in_specs = [pl.BlockSpec((tt, tm), lambda i, j, kk: (kk, i)),
                pl.BlockSpec((tt, width), lambda i, j, kk: (kk, j))]
    args = [a, b]
    aliases = {}
    if into is not None:
        in_specs.append(pl.BlockSpec(memory_space=pl.ANY))
        args.append(into)
        aliases = {2: 0}
    if shards is None:
        out_specs = [pl.BlockSpec((tm, tn), lambda i, j, kk: (row0 // tm + i, j))]
        out_shape = [jax.ShapeDtypeStruct((rows or m, n), out_dtype)]
        acc = pltpu.VMEM((tm, tn), F32)
    else:
        out_specs = [pl.BlockSpec((per_step, tm, tn), lambda i, j, kk: (shard0 // per_step + j, i, 0))]
        out_shape = [jax.ShapeDtypeStruct((shards, m, tn), out_dtype)]
        acc = pltpu.VMEM((per_step, tm, tn), F32)
    if colsum:
        out_specs.append(pl.BlockSpec((1, tn), lambda i, j, kk: (0, j)))
        out_shape.append(jax.ShapeDtypeStruct((1, n), F32))
    res = pl.pallas_call(
        body, name=name, grid=(m // tm, n // width, t // tt), in_specs=in_specs, out_specs=out_specs,
        out_shape=out_shape, input_output_aliases=aliases, scratch_shapes=[acc],
        compiler_params=_params(("parallel", "parallel", "arbitrary")),
    )(*args)
    return res if colsum else res[0]


ROW_TILE = 64


def _stack(ref, start, rows):
    return ref[pl.ds(start, rows), :].astype(F32).reshape(rows // SUBLANES, SUBLANES, LANES)


def _vreg_rows(ref, n):
    return [jnp.broadcast_to(ref[j:j + 1, :], (SUBLANES, LANES))[None] for j in range(n)]


def _column_total(acc):
    return jnp.sum(acc, axis=0, keepdims=True)


def _conv_fwd_tile(pad_ref, taps_w, bias, r0, rows):
    taps = len(taps_w)
    acc = bias
    for j in range(taps):
        acc = acc + _stack(pad_ref, SUBLANES - (taps - 1 - j) + r0, rows) * taps_w[j]
    return acc


def _conv_grads_tile(pad_ref, dpad_ref, dx_ref, taps_w, dws, r0, rows):
    taps = len(taps_w)
    x_rows = _stack(pad_ref, SUBLANES + r0, rows)
    dx = None
    for j in range(taps):
        d_shifted = _stack(dpad_ref, r0 + (taps - 1 - j), rows)
        term = d_shifted * taps_w[j]
        dx = term if dx is None else dx + term
        dws[j] = dws[j] + jnp.sum(d_shifted * x_rows, axis=0)
    dx_ref[r0:r0 + rows, :] = dx.reshape(rows, LANES).astype(dx_ref.dtype)
    return jnp.sum(dx, axis=0)


def _ml_conv_fwd(proj, conv_w, conv_b):
    s = proj.shape[0]
    nblk = 2 * D_GROUP // LANES

    def body(x_ref, w_ref, b_ref, o_ref, pad_ref):
        pad_ref[0:SUBLANES, :] = jnp.zeros((SUBLANES, LANES), F32)
        pad_ref[SUBLANES:, :] = x_ref[...].astype(F32)
        taps_w, bias = _vreg_rows(w_ref, ML_CONV), _vreg_rows(b_ref, 1)[0]
        for r0 in range(0, s, ROW_TILE):
            rows = min(ROW_TILE, s - r0)
            o_ref[r0:r0 + rows, :] = jax.nn.silu(_conv_fwd_tile(pad_ref, taps_w, bias, r0, rows)).reshape(rows, LANES)

    return pl.pallas_call(
        body, name="ml_conv_fwd", grid=(nblk,),
        in_specs=[pl.BlockSpec((s, LANES), lambda j: (0, SEG_MQ + j)),
                  pl.BlockSpec((ML_CONV, LANES), lambda j: (0, j)),
                  pl.BlockSpec((1, LANES), lambda j: (0, j))],
        out_specs=pl.BlockSpec((s, LANES), lambda j: (0, j)),
        out_shape=jax.ShapeDtypeStruct((s, 2 * D_GROUP), F32),
        scratch_shapes=[pltpu.VMEM((s + SUBLANES, LANES), F32)],
        compiler_params=_params(("parallel",)),
    )(proj, conv_w, conv_b)


def _ml_conv_bwd(proj, conv_w, conv_b, d_qk, d_proj):
    s = proj.shape[0]
    nblk = 2 * D_GROUP // LANES

    def body(x_ref, w_ref, b_ref, dy_ref, _, dx_ref, dw_ref, db_ref, dxs_ref, pad_ref, dpad_ref):
        pad_ref[0:SUBLANES, :] = jnp.zeros((SUBLANES, LANES), F32)
        pad_ref[SUBLANES:, :] = x_ref[...].astype(F32)
        dpad_ref[s:, :] = jnp.zeros((SUBLANES, LANES), F32)
        taps_w, bias = _vreg_rows(w_ref, ML_CONV), _vreg_rows(b_ref, 1)[0]
        db = jnp.zeros((SUBLANES, LANES), F32)
        for r0 in range(0, s, ROW_TILE):
            rows = min(ROW_TILE, s - r0)
            pre = _conv_fwd_tile(pad_ref, taps_w, bias, r0, rows)
            _, vjp = jax.vjp(jax.nn.silu, pre)
            d_pre, = vjp(_stack(dy_ref, r0, rows))
            dpad_ref[r0:r0 + rows, :] = d_pre.reshape(rows, LANES)
            db = db + jnp.sum(d_pre, axis=0)
        db_ref[...] = _column_total(db)
        dws = [jnp.zeros((SUBLANES, LANES), F32) for _ in range(ML_CONV)]
        dx_sum = jnp.zeros((SUBLANES, LANES), F32)
        for r0 in range(0, s, ROW_TILE):
            dx_sum = dx_sum + _conv_grads_tile(pad_ref, dpad_ref, dx_ref, taps_w, dws, r0, min(ROW_TILE, s - r0))
        dxs_ref[...] = _column_total(dx_sum)
        for j in range(ML_CONV):
            dw_ref[j:j + 1, :] = _column_total(dws[j])

    return pl.pallas_call(
        body, name="ml_conv_bwd", grid=(nblk,),
        in_specs=[pl.BlockSpec((s, LANES), lambda j: (0, SEG_MQ + j)),
                  pl.BlockSpec((ML_CONV, LANES), lambda j: (0, j)),
                  pl.BlockSpec((1, LANES), lambda j: (0, j)),
                  pl.BlockSpec((s, LANES), lambda j: (0, j)),
                  pl.BlockSpec(memory_space=pl.ANY)],
        out_specs=[pl.BlockSpec((s, LANES), lambda j: (0, SEG_MQ + j)),
                   pl.BlockSpec((ML_CONV, LANES), lambda j: (0, j)),
                   pl.BlockSpec((1, LANES), lambda j: (0, j)),
                   pl.BlockSpec((1, LANES), lambda j: (0, j))],
        out_shape=[jax.ShapeDtypeStruct(d_proj.shape, d_proj.dtype),
                   jax.ShapeDtypeStruct((ML_CONV, 2 * D_GROUP), F32),
                   jax.ShapeDtypeStruct((1, 2 * D_GROUP), F32),
                   jax.ShapeDtypeStruct((1, 2 * D_GROUP), F32)],
        input_output_aliases={4: 0},
        scratch_shapes=[pltpu.VMEM((s + SUBLANES, LANES), F32), pltpu.VMEM((s + SUBLANES, LANES), F32)],
        compiler_params=_params(("parallel",)),
    )(proj, conv_w, conv_b, d_qk, d_proj)


def _gelu_mul(a, b):
    return jax.nn.gelu(a) * b


GELU_C = math.sqrt(2.0 / math.pi)
GELU_K = 0.044715


def _gelu_mul_grads(a, b, d):
    a2 = a * a
    t = jnp.tanh(GELU_C * (a + GELU_K * (a * a2)))
    cdf = 0.5 * (1.0 + t)
    slope = cdf + (0.5 * GELU_C) * a * (1.0 - t * t) * (1.0 + (3.0 * GELU_K) * a2)
    return d * b * slope, d * (a * cdf)


FFN_BLOCKS = D_FF // LANES


def _ffn_conv_fwd(u, conv_w, conv_b):
    s = u.shape[0]

    def body(g_ref, v_ref, wg_ref, wv_ref, bg_ref, bv_ref, o_ref, gpad_ref, vpad_ref):
        for pad_ref, x_ref in ((gpad_ref, g_ref), (vpad_ref, v_ref)):
            pad_ref[0:SUBLANES, :] = jnp.zeros((SUBLANES, LANES), F32)
            pad_ref[SUBLANES:, :] = x_ref[...].astype(F32)
        taps_g, bias_g = _vreg_rows(wg_ref, FFN_CONV), _vreg_rows(bg_ref, 1)[0]
        taps_v, bias_v = _vreg_rows(wv_ref, FFN_CONV), _vreg_rows(bv_ref, 1)[0]
        for r0 in range(0, s, ROW_TILE):
            rows = min(ROW_TILE, s - r0)
            ug = _conv_fwd_tile(gpad_ref, taps_g, bias_g, r0, rows)
            uv = _conv_fwd_tile(vpad_ref, taps_v, bias_v, r0, rows)
            o_ref[r0:r0 + rows, :] = _gelu_mul(ug, uv).reshape(rows, LANES).astype(o_ref.dtype)

    col = lambda off: (lambda j: (0, off + j))
    return pl.pallas_call(
        body, name="ffn_conv_fwd", grid=(FFN_BLOCKS,),
        in_specs=[pl.BlockSpec((s, LANES), col(0)), pl.BlockSpec((s, LANES), col(FFN_BLOCKS)),
                  pl.BlockSpec((FFN_CONV, LANES), col(0)), pl.BlockSpec((FFN_CONV, LANES), col(FFN_BLOCKS)),
                  pl.BlockSpec((1, LANES), col(0)), pl.BlockSpec((1, LANES), col(FFN_BLOCKS))],
        out_specs=pl.BlockSpec((s, LANES), col(0)),
        out_shape=jax.ShapeDtypeStruct((s, D_FF), BF16),
        scratch_shapes=[pltpu.VMEM((s + SUBLANES, LANES), F32), pltpu.VMEM((s + SUBLANES, LANES), F32)],
        compiler_params=_params(("parallel",)),
    )(u, u, conv_w, conv_w, conv_b, conv_b)


def _ffn_conv_bwd(u, conv_w, conv_b, d_h):
    s = u.shape[0]

    def body(g_ref, v_ref, wg_ref, wv_ref, bg_ref, bv_ref, dh_ref,
             dug_ref, duv_ref, dwg_ref, dwv_ref, dbg_ref, dbv_ref,
             gpad_ref, vpad_ref, dgpad_ref, dvpad_ref):
        for pad_ref, x_ref in ((gpad_ref, g_ref), (vpad_ref, v_ref)):
            pad_ref[0:SUBLANES, :] = jnp.zeros((SUBLANES, LANES), F32)
            pad_ref[SUBLANES:, :] = x_ref[...].astype(F32)
        dgpad_ref[s:, :] = jnp.zeros((SUBLANES, LANES), F32)
        dvpad_ref[s:, :] = jnp.zeros((SUBLANES, LANES), F32)
        taps_g, bias_g = _vreg_rows(wg_ref, FFN_CONV), _vreg_rows(bg_ref, 1)[0]
        taps_v, bias_v = _vreg_rows(wv_ref, FFN_CONV), _vreg_rows(bv_ref, 1)[0]
        dbg = jnp.zeros((SUBLANES, LANES), F32)
        dbv = jnp.zeros((SUBLANES, LANES), F32)
        for r0 in range(0, s, ROW_TILE):
            rows = min(ROW_TILE, s - r0)
            ug = _conv_fwd_tile(gpad_ref, taps_g, bias_g, r0, rows)
            uv = _conv_fwd_tile(vpad_ref, taps_v, bias_v, r0, rows)
            d_ug, d_uv = _gelu_mul_grads(ug, uv, _stack(dh_ref, r0, rows))
            dgpad_ref[r0:r0 + rows, :] = d_ug.reshape(rows, LANES)
            dvpad_ref[r0:r0 + rows, :] = d_uv.reshape(rows, LANES)
            dbg = dbg + jnp.sum(d_ug, axis=0)
            dbv = dbv + jnp.sum(d_uv, axis=0)
        dbg_ref[...] = _column_total(dbg)
        dbv_ref[...] = _column_total(dbv)
        for pad_ref, dpad_ref, taps_w, dx_ref, dw_ref in ((gpad_ref, dgpad_ref, taps_g, dug_ref, dwg_ref),
                                                          (vpad_ref, dvpad_ref, taps_v, duv_ref, dwv_ref)):
            dws = [jnp.zeros((SUBLANES, LANES), F32) for _ in range(FFN_CONV)]
            for r0 in range(0, s, ROW_TILE):
                _conv_grads_tile(pad_ref, dpad_ref, dx_ref, taps_w, dws, r0, min(ROW_TILE, s - r0))
            for j in range(FFN_CONV):
                dw_ref[j:j + 1, :] = _column_total(dws[j])

    col = lambda off: (lambda j: (0, off + j))
    seq = pl.BlockSpec((s, LANES), col(0))
    return pl.pallas_call(
        body, name="ffn_conv_bwd", grid=(FFN_BLOCKS,),
        in_specs=[pl.BlockSpec((s, LANES), col(0)), pl.BlockSpec((s, LANES), col(FFN_BLOCKS)),
                  pl.BlockSpec((FFN_CONV, LANES), col(0)), pl.BlockSpec((FFN_CONV, LANES), col(FFN_BLOCKS)),
                  pl.BlockSpec((1, LANES), col(0)), pl.BlockSpec((1, LANES), col(FFN_BLOCKS)), seq],
        out_specs=[seq, seq, pl.BlockSpec((FFN_CONV, LANES), col(0)), pl.BlockSpec((FFN_CONV, LANES), col(0)),
                   pl.BlockSpec((1, LANES), col(0)), pl.BlockSpec((1, LANES), col(0))],
        out_shape=[jax.ShapeDtypeStruct((s, D_FF), BF16), jax.ShapeDtypeStruct((s, D_FF), BF16),
                   jax.ShapeDtypeStruct((FFN_CONV, D_FF), F32), jax.ShapeDtypeStruct((FFN_CONV, D_FF), F32),
                   jax.ShapeDtypeStruct((1, D_FF), F32), jax.ShapeDtypeStruct((1, D_FF), F32)],
        scratch_shapes=[pltpu.VMEM((s + SUBLANES, LANES), F32) for _ in range(4)],
        compiler_params=_params(("parallel",)),
    )(u, u, conv_w, conv_w, conv_b, conv_b, d_h)


def _chunk_masks(c):
    row = lax.broadcasted_iota(jnp.int32, (c, c), 0)
    col = lax.broadcasted_iota(jnp.int32, (c, c), 1)
    return row, col


@jax.custom_vjp
def _split_heads(x):
    return tuple(x[:, h * D_HEAD:(h + 1) * D_HEAD] for h in range(N_HEADS))


_split_heads.defvjp(lambda x: (_split_heads(x), None), lambda _, gs: (jnp.concatenate(gs, axis=1),))


@jax.custom_vjp
def _merge_heads(xs):
    return jnp.concatenate(xs, axis=1)


_merge_heads.defvjp(lambda xs: (_merge_heads(xs), None), lambda _, g: (_split_heads(g),))


@jax.custom_vjp
def _split_chunks(x):
    return tuple(x[i * CHUNK:(i + 1) * CHUNK] for i in range(x.shape[0] // CHUNK))


_split_chunks.defvjp(lambda x: (_split_chunks(x), None), lambda _, gs: (jnp.concatenate(gs, axis=0),))


@jax.custom_vjp
def _merge_chunks(xs):
    return jnp.concatenate(xs, axis=0)


_merge_chunks.defvjp(lambda xs: (_merge_chunks(xs), None), lambda _, g: (_split_chunks(g),))


def _blocks(x):
    return [_split_heads(rows) for rows in _split_chunks(x)]


def _per_chunk_rows(per_chunk, rid):
    out = per_chunk[0]
    for i in range(1, len(per_chunk)):
        out = jnp.where(rid >= i * CHUNK, per_chunk[i], out)
    return out


HEADS = range(N_HEADS)
CHUNKS_PER_STEP = 8
ML_CHUNKS_PER_STEP = 1


def _hg_chunk(hq, hf, hi, hgate, l0, l1, nw, sts):
    n = hq.shape[0] // CHUNK
    causal = _chunk_masks(CHUNK)
    causal = causal[1] <= causal[0]
    mx = lax.stop_gradient(jnp.maximum(l0, l1))
    e0 = jnp.exp(l0 - mx)
    e1 = jnp.exp(l1 - mx)
    lb = e0 / (e0 + e1)
    sig = jax.nn.sigmoid(hf)
    lf = jnp.log(lb + (1.0 - lb) * sig)
    k = (1.0 - lb) * jax.nn.sigmoid(-hf)
    q = jax.nn.silu(hq)
    tri = causal.astype(F32)
    b = _merge_chunks(tuple(_dg(tri, rows, 1, 0, HIGHEST) for rows in _split_chunks(lf)))
    rid = lax.broadcasted_iota(jnp.int32, b.shape, 0)
    pick = lambda r: jnp.sum(jnp.where(rid == r, b, 0.0), axis=0, keepdims=True)
    b_last_c = [pick(i * CHUNK + CHUNK - 1) for i in range(n)]
    b_ref = _per_chunk_rows([pick(i * CHUNK + CHUNK // 2 - 1) for i in range(n)], rid)
    b_last = _per_chunk_rows(b_last_c, rid)
    qa = _blocks(q * jnp.exp(b - b_ref))
    ka = _blocks(k * jnp.exp(b_ref - b))
    qe = _blocks(q * jnp.exp(b))
    kd = _blocks(k * jnp.exp(b_last - b))
    decay = [_split_heads(jnp.exp(b_last_c[i])) for i in range(n)]
    v = _blocks(hi)
    chunks = range(n)
    attn = [[jnp.where(causal, _nt(qa[i][h], ka[i][h]), 0.0) for h in HEADS] for i in chunks]
    intra = [[_nn(attn[i][h], v[i][h]) for h in HEADS] for i in chunks]
    kv = [[_tn(v[i][h], kd[i][h]) for h in HEADS] for i in chunks]
    normed = []
    for i in chunks:
        inter = [_nt(qe[i][h], sts[h]) for h in HEADS]
        sts = tuple(decay[i][h] * sts[h] + kv[i][h] for h in HEADS)
        o = [intra[i][h] + inter[h] for h in HEADS]
        normed.append(_merge_heads(tuple(o[h] * lax.rsqrt(jnp.mean(o[h] * o[h], axis=-1, keepdims=True) + LN_EPS)
                                         for h in HEADS)))
    return _merge_chunks(tuple(normed)) * nw * jax.nn.silu(hgate), sts


def _seg(ref, seg):
    return ref[:, seg * D_GROUP:(seg + 1) * D_GROUP]


def _hgrn2_fwd(proj, logits, norm_w):
    s = proj.shape[0]
    rows = CHUNKS_PER_STEP * CHUNK
    nc = s // rows

    def body(p_ref, lg_ref, nw_ref, y_ref, st_out_ref, st_scr):
        @pl.when(pl.program_id(0) == 0)
        def _():
            st_scr[...] = jnp.zeros_like(st_scr)

        sts = tuple(st_scr[h] for h in HEADS)
        y, sts_new = _hg_chunk(_seg(p_ref, 0), _seg(p_ref, 1), _seg(p_ref, 2), _seg(p_ref, 3),
                               lg_ref[0:1, :], lg_ref[1:2, :], nw_ref[...], sts)
        y_ref[...] = y.astype(y_ref.dtype)
        for h in HEADS:
            st_out_ref[h] = sts[h]
            st_scr[h] = sts_new[h]

    return pl.pallas_call(
        body, name="hgrn2_fwd", grid=(nc,),
        in_specs=[pl.BlockSpec((rows, 4 * D_GROUP), lambda c: (c, 0)),
                  pl.BlockSpec((2, D_GROUP), lambda c: (0, 0)),
                  pl.BlockSpec((1, D_GROUP), lambda c: (0, 0))],
        out_specs=[pl.BlockSpec((rows, D_GROUP), lambda c: (c, 0)),
                   pl.BlockSpec((None, N_HEADS, D_HEAD, D_HEAD), lambda c: (c, 0, 0, 0))],
        out_shape=[jax.ShapeDtypeStruct((s, 2 * D_GROUP), BF16),
                   jax.ShapeDtypeStruct((nc, N_HEADS, D_HEAD, D_HEAD), F32)],
        scratch_shapes=[pltpu.VMEM((N_HEADS, D_HEAD, D_HEAD), F32)],
        compiler_params=_params(("arbitrary",)),
    )(proj, logits, norm_w)


def _hgrn2_bwd(proj, logits, norm_w, states, d_y):
    s = proj.shape[0]
    rows = CHUNKS_PER_STEP * CHUNK
    nc = s // rows

    def body(p_ref, lg_ref, nw_ref, st_ref, dy_ref, dp_ref, dl_ref, dnw_ref, dsum_ref, dst_scr):
        @pl.when(pl.program_id(0) == 0)
        def _():
            dst_scr[...] = jnp.zeros_like(dst_scr)
            dl_ref[...] = jnp.zeros_like(dl_ref)
            dnw_ref[...] = jnp.zeros_like(dnw_ref)
            dsum_ref[...] = jnp.zeros_like(dsum_ref)

        _, vjp = jax.vjp(_hg_chunk, _seg(p_ref, 0), _seg(p_ref, 1), _seg(p_ref, 2), _seg(p_ref, 3),
                         lg_ref[0:1, :], lg_ref[1:2, :], nw_ref[...], tuple(st_ref[h] for h in HEADS))
        d_hq, d_hf, d_hi, d_hg, d_l0, d_l1, d_nw, d_sts = vjp((dy_ref[...], tuple(dst_scr[h] for h in HEADS)))
        for seg, val in enumerate((d_hq, d_hf, d_hi, d_hg)):
            dp_ref[:, seg * D_GROUP:(seg + 1) * D_GROUP] = val.astype(dp_ref.dtype)
            dsum_ref[:, seg * D_GROUP:(seg + 1) * D_GROUP] += jnp.sum(val, axis=0, keepdims=True)
        dl_ref[0:1, :] += d_l0
        dl_ref[1:2, :] += d_l1
        dnw_ref[...] += d_nw
        for h in HEADS:
            dst_scr[h] = d_sts[h]

    rev = lambda c: nc - 1 - c
    return pl.pallas_call(
        body, name="hgrn2_bwd", grid=(nc,),
        in_specs=[pl.BlockSpec((rows, 4 * D_GROUP), lambda c: (rev(c), 0)),
                  pl.BlockSpec((2, D_GROUP), lambda c: (0, 0)),
                  pl.BlockSpec((1, D_GROUP), lambda c: (0, 0)),
                  pl.BlockSpec((None, N_HEADS, D_HEAD, D_HEAD), lambda c: (rev(c), 0, 0, 0)),
                  pl.BlockSpec((rows, D_GROUP), lambda c: (rev(c), 0))],
        out_specs=[pl.BlockSpec((rows, 4 * D_GROUP), lambda c: (rev(c), 0)),
                   pl.BlockSpec((2, D_GROUP), lambda c: (0, 0)),
                   pl.BlockSpec((1, D_GROUP), lambda c: (0, 0)),
                   pl.BlockSpec((1, 4 * D_GROUP), lambda c: (0, 0))],
        out_shape=[jax.ShapeDtypeStruct((s, D_IN_MAIN), BF16), jax.ShapeDtypeStruct((2, D_GROUP), F32),
                   jax.ShapeDtypeStruct((1, D_GROUP), F32), jax.ShapeDtypeStruct((1, 4 * D_GROUP), F32)],
        scratch_shapes=[pltpu.VMEM((N_HEADS, D_HEAD, D_HEAD), F32)],
        compiler_params=_params(("arbitrary",)),
    )(proj, logits, norm_w, states, d_y)


def _gate_column(gates, lane, idx):
    return jnp.sum(jnp.where(lane == idx, gates, 0.0), axis=1, keepdims=True)


def _head_layer_norm(h):
    mu = jnp.mean(h, axis=-1, keepdims=True)
    var = jnp.mean(jnp.square(h - mu), axis=-1, keepdims=True)
    return (h - mu) * lax.rsqrt(var + LN_EPS)


def _ml_chunk(qc, kc, v, mo, gates, nw, cts, ns, ms):
    n = qc.shape[0] // CHUNK
    row, col = _chunk_masks(CHUNK)
    mask = col <= row
    eye = col == row
    to_row = lambda t: jnp.sum(jnp.where(eye, t, 0.0), axis=0, keepdims=True)
    q = _blocks(qc * (D_HEAD ** -0.5))
    k = _blocks(kc)
    vs = _blocks(v)
    gate_rows = _split_chunks(gates)
    lane = lax.broadcasted_iota(jnp.int32, gate_rows[0].shape, 1)
    each = [(i, h) for i in range(n) for h in HEADS]
    on_each = lambda f: {ih: f(*ih) for ih in each}
    ig = on_each(lambda i, h: _gate_column(gate_rows[i], lane, h))
    lf = on_each(lambda i, h: jax.nn.log_sigmoid(_gate_column(gate_rows[i], lane, N_HEADS + h)))
    lf_row = on_each(lambda i, h: to_row(lf[i, h]))
    ig_row = on_each(lambda i, h: to_row(ig[i, h]))
    b_col = on_each(lambda i, h: jnp.sum(jnp.where(mask, lf_row[i, h], 0.0), axis=1, keepdims=True))
    b_row = on_each(lambda i, h: jnp.sum(jnp.where(row <= col, lf[i, h], 0.0), axis=0, keepdims=True))
    g = on_each(lambda i, h: jnp.sum(lf[i, h], axis=0, keepdims=True))
    d = on_each(lambda i, h: jnp.where(mask, b_col[i, h] - b_row[i, h] + ig_row[i, h], -jnp.inf))
    a = on_each(lambda i, h: g[i, h] - b_col[i, h] + ig[i, h])
    m_at = {(0, h): ms[h] for h in HEADS}
    for i, h in each:
        m_at[i + 1, h] = lax.stop_gradient(jnp.maximum(g[i, h] + m_at[i, h], jnp.max(a[i, h], axis=0, keepdims=True)))
    inter = on_each(lambda i, h: b_col[i, h] + m_at[i, h])
    m_t = on_each(lambda i, h: lax.stop_gradient(jnp.maximum(inter[i, h], jnp.max(d[i, h], axis=1, keepdims=True))))
    qk = on_each(lambda i, h: _nt(q[i][h], k[i][h]))
    sc = on_each(lambda i, h: qk[i, h] * jnp.exp(d[i, h] - m_t[i, h]))
    w_inter = on_each(lambda i, h: jnp.exp(inter[i, h] - m_t[i, h]))
    sv = on_each(lambda i, h: _nn(sc[i, h], vs[i][h]))
    decay = on_each(lambda i, h: jnp.exp(g[i, h] + m_at[i, h] - m_at[i + 1, h]))
    wk = on_each(lambda i, h: k[i][h] * jnp.exp(a[i, h] - m_at[i + 1, h]))
    kv = on_each(lambda i, h: _tn(vs[i][h], wk[i, h]))
    normed = []
    for i in range(n):
        qc_state = [_nt(q[i][h], cts[h]) for h in HEADS]
        num = [sv[i, h] + w_inter[i, h] * qc_state[h] for h in HEADS]
        den = [jnp.sum(sc[i, h], axis=1, keepdims=True)
               + w_inter[i, h] * jnp.sum(q[i][h] * ns[h], axis=1, keepdims=True) for h in HEADS]
        hh = [num[h] / jnp.maximum(jnp.abs(den[h]), jnp.exp(-m_t[i, h])) for h in HEADS]
        cts = tuple(decay[i, h] * cts[h] + kv[i, h] for h in HEADS)
        ns = tuple(decay[i, h] * ns[h] + jnp.sum(wk[i, h], axis=0, keepdims=True) for h in HEADS)
        normed.append(_merge_heads(tuple(_head_layer_norm(hh[h]) for h in HEADS)))
    y = jax.nn.sigmoid(mo) * (_merge_chunks(tuple(normed)) * nw)
    return y, cts, ns, tuple(m_at[n, h] for h in HEADS)


def _mlstm_fwd(qk, proj, gates, norm_w, y):
    s = proj.shape[0]
    rows = ML_CHUNKS_PER_STEP * CHUNK
    nc = s // rows

    def body(qk_ref, vo_ref, g_ref, nw_ref, _, y_ref, ct_out, n_out, m_out, ct_scr, n_scr, m_scr):
        @pl.when(pl.program_id(0) == 0)
        def _():
            ct_scr[...] = jnp.zeros_like(ct_scr)
            n_scr[...] = jnp.zeros_like(n_scr)
            m_scr[...] = jnp.full(m_scr.shape, NEG_BIG, F32)

        cts = tuple(ct_scr[h] for h in HEADS)
        ns = tuple(n_scr[h] for h in HEADS)
        ms = tuple(m_scr[h] for h in HEADS)
        y, cts_new, ns_new, ms_new = _ml_chunk(_seg(qk_ref, 0), _seg(qk_ref, 1), _seg(vo_ref, 0), _seg(vo_ref, 1),
                                               g_ref[...], nw_ref[...], cts, ns, ms)
        y_ref[...] = y.astype(y_ref.dtype)
        for h in HEADS:
            ct_out[h], n_out[h], m_out[h] = cts[h], ns[h], ms[h]
            ct_scr[h], n_scr[h], m_scr[h] = cts_new[h], ns_new[h], ms_new[h]

    st = lambda r, w: pl.BlockSpec((None, N_HEADS, r, w), lambda c: (c, 0, 0, 0))
    return pl.pallas_call(
        body, name="mlstm_fwd", grid=(nc,),
        in_specs=[pl.BlockSpec((rows, 2 * D_GROUP), lambda c: (c, 0)),
                  pl.BlockSpec((rows, 2 * D_GROUP), lambda c: (c, VO_BLOCK)),
                  pl.BlockSpec((rows, LANES), lambda c: (c, 0)),
                  pl.BlockSpec((1, D_GROUP), lambda c: (0, 0)),
                  pl.BlockSpec(memory_space=pl.ANY)],
        out_specs=[pl.BlockSpec((rows, D_GROUP), lambda c: (c, 1)),
                   st(D_HEAD, D_HEAD), st(1, D_HEAD), st(1, 1)],
        out_shape=[jax.ShapeDtypeStruct(y.shape, y.dtype),
                   jax.ShapeDtypeStruct((nc, N_HEADS, D_HEAD, D_HEAD), F32),
                   jax.ShapeDtypeStruct((nc, N_HEADS, 1, D_HEAD), F32),
                   jax.ShapeDtypeStruct((nc, N_HEADS, 1, 1), F32)],
        input_output_aliases={4: 0},
        scratch_shapes=[pltpu.VMEM((N_HEADS, D_HEAD, D_HEAD), F32), pltpu.VMEM((N_HEADS, 1, D_HEAD), F32),
                        pltpu.VMEM((N_HEADS, 1, 1), F32)],
        compiler_params=_params(("arbitrary",)),
    )(qk, proj, gates, norm_w, y)


def _mlstm_bwd(qk, proj, gates, norm_w, ct_s, n_s, m_s, d_y, d_proj):
    s = proj.shape[0]
    rows = ML_CHUNKS_PER_STEP * CHUNK
    nc = s // rows

    def body(qk_ref, vo_ref, g_ref, nw_ref, ct_ref, n_ref, m_ref, dy_ref, _,
             dp_ref, dqk_ref, dg_ref, dnw_ref, dsum_ref, dct_scr, dn_scr):
        @pl.when(pl.program_id(0) == 0)
        def _():
            dct_scr[...] = jnp.zeros_like(dct_scr)
            dn_scr[...] = jnp.zeros_like(dn_scr)
            dnw_ref[...] = jnp.zeros_like(dnw_ref)
            dsum_ref[...] = jnp.zeros_like(dsum_ref)

        ms = tuple(m_ref[h] for h in HEADS)
        step = lambda *a: _ml_chunk(*a, ms)[:3]
        _, vjp = jax.vjp(step, _seg(qk_ref, 0), _seg(qk_ref, 1), _seg(vo_ref, 0), _seg(vo_ref, 1), g_ref[...],
                         nw_ref[...], tuple(ct_ref[h] for h in HEADS), tuple(n_ref[h] for h in HEADS))
        d_q, d_k, d_v, d_o, d_gates, d_nw, d_cts, d_ns = vjp(
            (dy_ref[...], tuple(dct_scr[h] for h in HEADS), tuple(dn_scr[h] for h in HEADS)))
        dqk_ref[:, 0:D_GROUP] = d_q
        dqk_ref[:, D_GROUP:2 * D_GROUP] = d_k
        for seg, val in enumerate((d_v, d_o)):
            dp_ref[:, seg * D_GROUP:(seg + 1) * D_GROUP] = val.astype(dp_ref.dtype)
            dsum_ref[:, seg * D_GROUP:(seg + 1) * D_GROUP] += jnp.sum(val, axis=0, keepdims=True)
        dg_ref[...] = d_gates
        dnw_ref[...] += d_nw
        for h in HEADS:
            dct_scr[h] = d_cts[h]
            dn_scr[h] = d_ns[h]

    rev = lambda c: nc - 1 - c
    st = lambda r, w: pl.BlockSpec((None, N_HEADS, r, w), lambda c: (rev(c), 0, 0, 0))
    return pl.pallas_call(
        body, name="mlstm_bwd", grid=(nc,),
        in_specs=[pl.BlockSpec((rows, 2 * D_GROUP), lambda c: (rev(c), 0)),
                  pl.BlockSpec((rows, 2 * D_GROUP), lambda c: (rev(c), VO_BLOCK)),
                  pl.BlockSpec((rows, LANES), lambda c: (rev(c), 0)),
                  pl.BlockSpec((1, D_GROUP), lambda c: (0, 0)),
                  st(D_HEAD, D_HEAD), st(1, D_HEAD), st(1, 1),
                  pl.BlockSpec((rows, D_GROUP), lambda c: (rev(c), 1)),
                  pl.BlockSpec(memory_space=pl.ANY)],
        out_specs=[pl.BlockSpec((rows, 2 * D_GROUP), lambda c: (rev(c), VO_BLOCK)),
                   pl.BlockSpec((rows, 2 * D_GROUP), lambda c: (rev(c), 0)),
                   pl.BlockSpec((rows, LANES), lambda c: (rev(c), 0)),
                   pl.BlockSpec((1, D_GROUP), lambda c: (0, 0)),
                   pl.BlockSpec((1, 2 * D_GROUP), lambda c: (0, 0))],
        out_shape=[jax.ShapeDtypeStruct(d_proj.shape, d_proj.dtype), jax.ShapeDtypeStruct((s, 2 * D_GROUP), F32),
                   jax.ShapeDtypeStruct((s, LANES), F32), jax.ShapeDtypeStruct((1, D_GROUP), F32),
                   jax.ShapeDtypeStruct((1, 2 * D_GROUP), F32)],
        input_output_aliases={8: 0},
        scratch_shapes=[pltpu.VMEM((N_HEADS, D_HEAD, D_HEAD), F32), pltpu.VMEM((N_HEADS, 1, D_HEAD), F32)],
        compiler_params=_params(("arbitrary",)),
    )(qk, proj, gates, norm_w, ct_s, n_s, m_s, d_y, d_proj)


LN_TOKENS = 512
ATT_TOKENS = 512


def _proj_res_ln(a, w, xres, g, b, name):
    s, dm = xres.shape
    k = a.shape[1]
    tb = min(LN_TOKENS, s)

    def body(a_ref, w_ref, x_ref, g_ref, b_ref, z_ref, o_ref):
        halves = [slice(0, tb // 2), slice(tb // 2, tb)]
        zs = [ALPHA * x_ref[rows, :] + _nn_raw(a_ref[rows, :], w_ref[...]) for rows in halves]
        for rows, z in zip(halves, zs):
            z_ref[rows, :] = z
            o_ref[rows, :] = _layer_norm(z, g_ref[...], b_ref[...])

    tok = pl.BlockSpec((tb, dm), lambda i: (i, 0))
    vec = pl.BlockSpec((1, dm), lambda i: (0, 0))
    act = jax.ShapeDtypeStruct((s, dm), F32)
    return pl.pallas_call(
        body, name=name, grid=(s // tb,),
        in_specs=[pl.BlockSpec((tb, k), lambda i: (i, 0)), pl.BlockSpec((k, dm), lambda i: (0, 0)), tok, vec, vec],
        out_specs=[tok, tok], out_shape=[act, act], compiler_params=_params(("parallel",)),
    )(a, w, xres, g, b)


def _ln_bwd_proj(d_out, z, g, b, w, name):
    s, dm = z.shape
    k = w.shape[0]
    tb = min(LN_TOKENS, s)

    def body(do_ref, z_ref, g_ref, b_ref, w_ref, dz_ref, da_ref, dg_ref, db_ref):
        @pl.when(pl.program_id(0) == 0)
        def _():
            dg_ref[...] = jnp.zeros_like(dg_ref)
            db_ref[...] = jnp.zeros_like(db_ref)

        halves = [slice(0, tb // 2), slice(tb // 2, tb)]
        d_zs = []
        for rows in halves:
            _, vjp = jax.vjp(_layer_norm, z_ref[rows, :], g_ref[...], b_ref[...])
            d_z, d_g, d_b = vjp(do_ref[rows, :])
            dz_ref[rows, :] = d_z
            dg_ref[...] += d_g
            db_ref[...] += d_b
            d_zs.append(d_z)
        for rows, d_z in zip(halves, d_zs):
            da_ref[rows, :] = _nt_raw(d_z, w_ref[...])

    tok = pl.BlockSpec((tb, dm), lambda i: (i, 0))
    vec = pl.BlockSpec((1, dm), lambda i: (0, 0))
    return pl.pallas_call(
        body, name=name, grid=(s // tb,),
        in_specs=[tok, tok, vec, vec, pl.BlockSpec((k, dm), lambda i: (0, 0))],
        out_specs=[tok, pl.BlockSpec((tb, k), lambda i: (i, 0)), vec, vec],
        out_shape=[jax.ShapeDtypeStruct((s, dm), F32), jax.ShapeDtypeStruct((s, k), F32),
                   jax.ShapeDtypeStruct((1, dm), F32), jax.ShapeDtypeStruct((1, dm), F32)],
        compiler_params=_params(("arbitrary",)),
    )(d_out, z, g, b, w)


def _proj_loss_tail(a, w, xres, g, b, target):
    s, dm = xres.shape
    k = a.shape[1]
    tb = min(ATT_TOKENS, s)

    def loss_fn(z, gg, bb, tgt):
        err = jnp.square(_layer_norm(z, gg, bb) - tgt)
        return 0.5 * jnp.sum(jnp.mean(err, axis=-1, keepdims=True), axis=0, keepdims=True)

    def body(a_ref, w_ref, x_ref, g_ref, b_ref, t_ref, loss_ref, dz_ref, dg_ref, db_ref):
        @pl.when(pl.program_id(0) == 0)
        def _():
            loss_ref[...] = jnp.zeros_like(loss_ref)
            dg_ref[...] = jnp.zeros_like(dg_ref)
            db_ref[...] = jnp.zeros_like(db_ref)

        halves = [slice(0, tb // 2), slice(tb // 2, tb)]
        zs = [ALPHA * x_ref[rows, :] + _nn_raw(a_ref[rows, :], w_ref[...]) for rows in halves]
        for rows, z in zip(halves, zs):
            tgt = t_ref[rows, :]
            loss, vjp = jax.vjp(lambda zz, gg, bb, tgt=tgt: loss_fn(zz, gg, bb, tgt), z, g_ref[...], b_ref[...])
            d_z, d_g, d_b = vjp(jnp.ones((1, 1), F32))
            loss_ref[...] += loss
            dz_ref[rows, :] = d_z
            dg_ref[...] += d_g
            db_ref[...] += d_b

    tok = pl.BlockSpec((tb, dm), lambda i: (i, 0))
    vec = pl.BlockSpec((1, dm), lambda i: (0, 0))
    one = pl.BlockSpec((1, 1), lambda i: (0, 0))
    return pl.pallas_call(
        body, name="ffn_down_loss_tail", grid=(s // tb,),
        in_specs=[pl.BlockSpec((tb, k), lambda i: (i, 0)), pl.BlockSpec((k, dm), lambda i: (0, 0)), tok, vec, vec, tok],
        out_specs=[one, tok, vec, vec],
        out_shape=[jax.ShapeDtypeStruct((1, 1), F32), jax.ShapeDtypeStruct((s, dm), F32),
                   jax.ShapeDtypeStruct((1, dm), F32), jax.ShapeDtypeStruct((1, dm), F32)],
        compiler_params=_params(("arbitrary",)),
    )(a, w, xres, g, b, target)


def _att_heads(qs, ks, vs):
    sc = [_nt(q, k) * (CA_DH ** -0.5) for q, k in zip(qs, ks)]
    p = [jax.nn.softmax(s, axis=-1) for s in sc]
    return tuple(_nn(pp, v) for pp, v in zip(p, vs))


def _head_slices(ref_or_value, offset):
    return tuple(ref_or_value[:, offset + h * CA_DH:offset + (h + 1) * CA_DH] for h in range(CA_HEADS))


def _cross_attention_fwd(x1, kv, wq, wo, g, b):
    s = x1.shape[0]
    tb = min(ATT_TOKENS, s)

    def body(x_ref, kv_ref, wq_ref, wo_ref, g_ref, b_ref, att_ref, z_ref, o_ref):
        x_blk = x_ref[...]
        q = _nn_raw(x_blk, wq_ref[...])
        att = jnp.concatenate(_att_heads(_head_slices(q, 0), _head_slices(kv_ref, 0), _head_slices(kv_ref, D_MODEL)),
                              axis=1)
        att_ref[...] = att.astype(att_ref.dtype)
        z = ALPHA * x_blk + _nn_raw(att, wo_ref[...])
        z_ref[...] = z
        o_ref[...] = _layer_norm(z, g_ref[...], b_ref[...])

    tok = pl.BlockSpec((tb, D_MODEL), lambda i: (i, 0))
    mat = pl.BlockSpec((D_MODEL, D_MODEL), lambda i: (0, 0))
    vec = pl.BlockSpec((1, D_MODEL), lambda i: (0, 0))
    act = jax.ShapeDtypeStruct((s, D_MODEL), F32)
    return pl.pallas_call(
        body, name="cross_attention_fwd", grid=(s // tb,),
        in_specs=[tok, pl.BlockSpec((N_MEM, 2 * D_MODEL), lambda i: (0, 0)), mat, mat, vec, vec],
        out_specs=[tok, tok, tok],
        out_shape=[jax.ShapeDtypeStruct((s, D_MODEL), BF16), act, act],
        compiler_params=_params(("parallel",)),
    )(x1, kv, wq, wo, g, b)


def _cross_attention_bwd(d_x2, x1, z2, kv, wq, wo, g, b):
    s = x1.shape[0]
    tb = min(ATT_TOKENS, s)

    def body(dx2_ref, x_ref, z_ref, kv_ref, wq_ref, wo_ref, g_ref, b_ref,
             dx1_ref, dq_ref, dz_ref, dkv_ref, dg_ref, db_ref):
        @pl.when(pl.program_id(0) == 0)
        def _():
            dkv_ref[...] = jnp.zeros_like(dkv_ref)
            dg_ref[...] = jnp.zeros_like(dg_ref)
            db_ref[...] = jnp.zeros_like(db_ref)

        q = _nn_raw(x_ref[...], wq_ref[...])
        _, ln_vjp = jax.vjp(_layer_norm, z_ref[...], g_ref[...], b_ref[...])
        d_z, d_g, d_b = ln_vjp(dx2_ref[...])
        dg_ref[...] += d_g
        db_ref[...] += d_b
        dz_ref[...] = d_z.astype(dz_ref.dtype)
        d_att = _nt_raw(d_z, wo_ref[...])
        _, vjp = jax.vjp(_att_heads, _head_slices(q, 0), _head_slices(kv_ref, 0), _head_slices(kv_ref, D_MODEL))
        d_qs, d_ks, d_vs = vjp(_head_slices(d_att, 0))
        for h in range(CA_HEADS):
            lo = h * CA_DH
            dkv_ref[:, lo:lo + CA_DH] += d_ks[h]
            dkv_ref[:, D_MODEL + lo:D_MODEL + lo + CA_DH] += d_vs[h]
        d_q = jnp.concatenate(d_qs, axis=1)
        dq_ref[...] = d_q.astype(dq_ref.dtype)
        dx1_ref[...] = ALPHA * d_z + _nt_raw(d_q, wq_ref[...])

    tok = pl.BlockSpec((tb, D_MODEL), lambda i: (i, 0))
    mem = pl.BlockSpec((N_MEM, 2 * D_MODEL), lambda i: (0, 0))
    mat = pl.BlockSpec((D_MODEL, D_MODEL), lambda i: (0, 0))
    vec = pl.BlockSpec((1, D_MODEL), lambda i: (0, 0))
    low = jax.ShapeDtypeStruct((s, D_MODEL), BF16)
    return pl.pallas_call(
        body, name="cross_attention_bwd", grid=(s // tb,),
        in_specs=[tok, tok, tok, mem, mat, mat, vec, vec], out_specs=[tok, tok, tok, mem, vec, vec],
        out_shape=[jax.ShapeDtypeStruct((s, D_MODEL), F32), low, low,
                   jax.ShapeDtypeStruct((N_MEM, 2 * D_MODEL), F32),
                   jax.ShapeDtypeStruct((1, D_MODEL), F32), jax.ShapeDtypeStruct((1, D_MODEL), F32)],
        compiler_params=_params(("arbitrary",)),
    )(d_x2, x1, z2, kv, wq, wo, g, b)


def _local_step(x, mem, target, w, mid_weights=None, ffn_weights=None, down_weights=None, on_ffn_grads=None,
                on_mid_grads=None,
                on_small_grads=None, on_last_grads=None):
    w = dict(w)
    s = x.shape[0]
    tm = min(512, s)
    tt_big = min(1024, s)
    proj, gates = _input_projection(x, w["w_in_t"], w["w_in_gate_t"], w["b_in_main"], w["b_in_gate"],
                                    min(2048, s), 512)
    qk = _ml_conv_fwd(proj, w["ml_conv_w"], w["ml_conv_b"])
    y, hg_states = _hgrn2_fwd(proj, w["hg_lb_logits"], w["hg_norm_w"])
    y, ct_s, n_s, m_s = _mlstm_fwd(qk, proj, gates, w["ml_norm_w"], y)
    if mid_weights is not None:
        w.update(mid_weights(y))
    z1, x1 = _proj_res_ln(y, w["w_out"], x, w["ln1_g"], w["ln1_b"], "out_proj_ln1")
    kv = _matmul_nn(mem, w["ca_wkv"], None, N_MEM, CA_DH, "kv")
    att, z2, x2 = _cross_attention_fwd(x1, kv, w["ca_wq"], w["ca_wo"], w["ln2_g"], w["ln2_b"])
    if ffn_weights is not None:
        w.update(ffn_weights(x2))
    u = _matmul_nt(x2, w["ffn_w_up_t"], min(1024, s), UP_TILE, "ffn_up", out_dtype=BF16)
    hid = _ffn_conv_fwd(u, w["ffn_conv_w"], w["ffn_conv_b"])
    if down_weights is not None:
        w.update(down_weights(hid))
    loss, d_z3, d_ln3_g, d_ln3_b = _proj_loss_tail(hid, w["ffn_w_down"], x2, w["ln3_g"], w["ln3_b"], target)
    grads = {"ln3_g": d_ln3_g, "ln3_b": d_ln3_b}
    grads["ffn_w_down"] = _matmul_tn(hid, d_z3, UP_TILE, D_MODEL, tt_big, "d_w_down")
    d_hid = _matmul_nt(d_z3, w["ffn_w_down"], tm, D_FF, "d_hid", out_dtype=BF16)
    d_ug, d_uv, d_cwg, d_cwv, d_cbg, d_cbv = _ffn_conv_bwd(u, w["ffn_conv_w"], w["ffn_conv_b"], d_hid)
    grads["ffn_conv_w"] = jnp.concatenate([d_cwg, d_cwv], axis=-1)
    grads["ffn_conv_b"] = jnp.concatenate([d_cbg, d_cbv], axis=-1)
    d_w_up = _matmul_tn(d_ug, x2, UP_TILE, D_MODEL, tt_big, "d_w_up_gate", rows=D_UP)
    grads["ffn_w_up"] = _matmul_tn(d_uv, x2, UP_TILE, D_MODEL, tt_big, "d_w_up_val", rows=D_UP, row0=D_FF,
                                   into=d_w_up)
    d_x2 = _matmul_nn_sum([(d_ug, w["ffn_w_up_t"], 0), (d_uv, w["ffn_w_up_t"], D_FF)], d_z3, ALPHA,
                          min(256, s), "d_x2")
    if on_ffn_grads is not None:
        d_x2 = on_ffn_grads(grads, d_x2)
    d_x1, d_q, d_z2, d_kv, grads["ln2_g"], grads["ln2_b"] = _cross_attention_bwd(
        d_x2, x1, z2, kv, w["ca_wq"], w["ca_wo"], w["ln2_g"], w["ln2_b"])
    grads["ca_wo"] = _matmul_tn(att, d_z2, D_MODEL, D_MODEL, tt_big, "d_ca_wo")
    grads["ca_wq"] = _matmul_tn(x1, d_q, D_MODEL, D_MODEL, tt_big, "d_ca_wq")
    grads["ca_wkv"] = _matmul_tn(mem, d_kv, D_MODEL, CA_DH, N_MEM, "d_ca_wkv", shards=N_DEV, group=N_DEV)
    d_z1, d_y, grads["ln1_g"], grads["ln1_b"] = _ln_bwd_proj(d_x1, z1, w["ln1_g"], w["ln1_b"], w["w_out"],
                                                             "ln1_bwd_out_proj")
    grads["w_out"] = _matmul_tn(y, d_z1, D_MODEL, D_MODEL, tt_big, "d_w_out")
    if on_mid_grads is not None:
        d_y = on_mid_grads(grads, d_y)
    d_proj, grads["hg_lb_logits"], grads["hg_norm_w"], db_hg = _hgrn2_bwd(
        proj, w["hg_lb_logits"], w["hg_norm_w"], hg_states, d_y)
    d_proj, d_qk, d_gates, grads["ml_norm_w"], db_vo = _mlstm_bwd(
        qk, proj, gates, w["ml_norm_w"], ct_s, n_s, m_s, d_y, d_proj)
    d_proj, grads["ml_conv_w"], grads["ml_conv_b"], db_qk = _ml_conv_bwd(
        proj, w["ml_conv_w"], w["ml_conv_b"], d_qk, d_proj)
    grads["b_in_main"] = jnp.concatenate([db_hg, db_qk, db_vo], axis=-1)
    grads["b_in_gate"] = jnp.sum(d_gates, axis=0, keepdims=True)
    if on_small_grads is not None:
        d_proj = on_small_grads(grads, loss, d_proj)
    grads["w_in_main"], grads["w_in_gate"] = _input_projection_grads(d_proj, d_gates, x, min(2048, D_IN_MAIN), tt_big)
    if on_last_grads is not None:
        d_z1 = on_last_grads(grads, d_z1)
    grad_x = _matmul_nn_sum([(d_proj, w["w_in_t"], 0), (d_gates, w["w_in_gate_t"], 0)], d_z1, ALPHA, tm, "d_x")
    return loss, grad_x, grads


HBM_SPEC = pl.BlockSpec(memory_space=pltpu.HBM)


def _coords():
    return lax.axis_index("x"), lax.axis_index("y"), lax.axis_index("c")


def _other_chips(x, y):
    return [(1 - x, y), (x, 1 - y), (1 - x, 1 - y)]


def _my_slot():
    x, y, c = _coords()
    return 4 * x + 2 * y + c


SEM_SPEC = pl.BlockSpec(memory_space=pltpu.SEMAPHORE)
ANY_SPEC = pl.BlockSpec(memory_space=pl.ANY)
SIDE_EFFECT = pltpu.SideEffectType.DATAFLOW_SIDE_EFFECTING


def _peer(x, y, c, d):
    flip = lambda v, bit: 1 - v if bit else v
    p = (flip(x, d & 4), flip(y, d & 2), flip(c, d & 1))
    return p, 4 * p[0] + 2 * p[1] + p[2]


def _direct_copies(gather, src_refs, land_refs, send_sems, recv_sems):
    x, y, c = _coords()
    me = 4 * x + 2 * y + c
    copies = []
    for a in range(len(src_refs)):
        for d in range(1, N_DEV):
            peer, peer_slot = _peer(x, y, c, d)
            copies.append(pltpu.make_async_remote_copy(
                src_ref=src_refs[a] if gather else src_refs[a].at[peer_slot],
                dst_ref=land_refs[a].at[me] if gather else land_refs[a].at[d - 1],
                send_sem=send_sems.at[7 * a + d - 1], recv_sem=recv_sems.at[7 * a + d - 1],
                device_id=peer, device_id_type=MESH))
    return copies


def _hbm(t):
    return pltpu.HBM(t.shape, t.dtype)


def _chip_copies(src_refs, land_refs, send_sems, recv_sems):
    x, y, c = _coords()
    me = 4 * x + 2 * y + c
    targets = [(x, y, 1 - c)] + [(cx, cy, c) for cx, cy in _other_chips(x, y)]
    return [pltpu.make_async_remote_copy(
        src_ref=src_refs[a], dst_ref=land_refs[a].at[me], send_sem=send_sems.at[4 * a + k],
        recv_sem=recv_sems.at[4 * a + k], device_id=target, device_id_type=MESH)
        for a in range(len(src_refs)) for k, target in enumerate(targets)]


def _forward_copies(land_refs, send_sems, recv_sems):
    x, y, c = _coords()
    return [pltpu.make_async_remote_copy(
        src_ref=land_refs[a].at[4 * cx + 2 * cy + c], dst_ref=land_refs[a].at[4 * cx + 2 * cy + c],
        send_sem=send_sems.at[3 * a + j], recv_sem=recv_sems.at[3 * a + j],
        device_id=(x, y, 1 - c), device_id_type=MESH)
        for a in range(len(land_refs)) for j, (cx, cy) in enumerate(_other_chips(x, y))]


def _split_copy_start(make_copies, n_sems, operands, through, name):
    n_ops = len(operands)

    def body(*refs):
        for cp in make_copies(refs[:n_ops], refs[n_ops + 1], refs[n_ops + 2]):
            cp.start()

    ins = [pltpu.with_memory_space_constraint(t, pltpu.HBM) for t in (*operands, through)]
    sems = pltpu.SemaphoreType.DMA((n_sems,))
    res = pl.pallas_call(
        body, name=name, out_shape=(sems, sems, *[_hbm(t) for t in ins]),
        in_specs=[HBM_SPEC] * (n_ops + 1), out_specs=(SEM_SPEC, SEM_SPEC, *[HBM_SPEC] * (n_ops + 1)),
        input_output_aliases={i: 2 + i for i in range(n_ops + 1)},
        compiler_params=pltpu.CompilerParams(has_side_effects=SIDE_EFFECT),
    )(*ins)
    return (res[0], res[1], list(res[2:2 + n_ops])), res[2 + n_ops]


def _split_copy_wait(make_copies, started, after, name):
    send_sems, recv_sems, operands = started
    n_ops = len(operands)
    after = list(after) if isinstance(after, (list, tuple)) else [after]

    def body(*refs):
        for cp in make_copies(refs[:n_ops], refs[n_ops], refs[n_ops + 1]):
            cp.wait_send()
            cp.wait_recv()

    res = pl.pallas_call(
        body, name=name, out_shape=tuple(_hbm(t) for t in operands),
        in_specs=[HBM_SPEC] * n_ops + [SEM_SPEC, SEM_SPEC] + [ANY_SPEC] * len(after),
        out_specs=tuple([HBM_SPEC] * n_ops), input_output_aliases={i: i for i in range(n_ops)},
        compiler_params=pltpu.CompilerParams(has_side_effects=SIDE_EFFECT),
    )(*operands, send_sems, recv_sems, *after)
    return list(res)


def _halves(make_copies, na):
    return lambda refs, send_sems, recv_sems: make_copies(refs[:na], refs[na:], send_sems, recv_sems)


def _direct_start(gather, arrays, through, name):
    na = len(arrays)
    lands = [lax.empty((N_DEV,) + t.shape if gather else (N_DEV - 1,) + t.shape[1:], t.dtype) for t in arrays]
    return _split_copy_start(_halves(functools.partial(_direct_copies, gather), na), 7 * na, [*arrays, *lands],
                             through, name)


def _direct_wait(gather, started, after, name):
    na = len(started[2]) // 2
    operands = _split_copy_wait(_halves(functools.partial(_direct_copies, gather), na), started, after, name)
    return operands[:na], operands[na:]


def _two_level_gather(shards, glue, name):
    na = len(shards)
    lands = [lax.empty((N_DEV,) + t.shape, t.dtype) for t in shards]
    nothing = jnp.zeros((SUBLANES, LANES), F32)
    started, _ = _split_copy_start(_halves(_chip_copies, na), 4 * na, [*shards, *lands], nothing, name + "_start")
    operands = _split_copy_wait(_halves(_chip_copies, na), started, glue, name + "_wait")
    started, mine = _split_copy_start(_forward_copies, 3 * na, operands[na:], operands[0], name + "_forward_start")
    lands = _split_copy_wait(_forward_copies, started, mine, name + "_forward_wait")
    return [lax.dynamic_update_index_in_dim(land, own, _my_slot(), 0)
            for own, land in zip([mine, *operands[1:na]], lands)]


def _row_tile(rows):
    for t in (256, 176, 128):
        if rows % t == 0 and rows > t:
            return t
    return rows


def _adamw_math(g, w, m, v):
    m_new = ADAM_B1 * m + (1.0 - ADAM_B1) * g
    v_new = ADAM_B2 * v + (1.0 - ADAM_B2) * jnp.square(g)
    m_hat = m_new / (1.0 - ADAM_B1 ** ADAM_STEP)
    v_hat = v_new / (1.0 - ADAM_B2 ** ADAM_STEP)
    delta = -ADAM_LR * (m_hat / (jnp.sqrt(v_hat) + ADAM_EPS) + ADAM_WD * w)
    return delta, m_new, v_new


def _adamw_sharded(chip, sums, got, w, m, v, name):
    r, c = w.shape
    tr = _row_tile(r)
    n_got = got.shape[0]

    def body(chip_ref, s_ref, g_ref, w_ref, m_ref, v_ref, go_ref, d_ref, nm_ref, nv_ref):
        g = s_ref[...].astype(F32)
        for i in range(n_got):
            g = g + g_ref[i].astype(F32)
        delta, m_new, v_new = _adamw_math(g, w_ref[...], m_ref[...], v_ref[...])
        go_ref[...] = g
        d_ref[...] = delta
        nm_ref[...] = m_new
        nv_ref[...] = v_new

    blk = pl.BlockSpec((tr, c), lambda i, chip_ref: (i, 0))
    out = jax.ShapeDtypeStruct((r, c), F32)
    return pl.pallas_call(
        body, name=name,
        grid_spec=pltpu.PrefetchScalarGridSpec(
            num_scalar_prefetch=1, grid=(r // tr,),
            in_specs=[pl.BlockSpec((None, tr, c), lambda i, chip_ref: (chip_ref[0], i, 0)),
                      pl.BlockSpec((n_got, tr, c), lambda i, chip_ref: (0, i, 0)), blk, blk, blk],
            out_specs=[blk, blk, blk, blk]),
        out_shape=[out, out, out, out],
        compiler_params=_params(("parallel",)),
    )(chip, sums, got, w, m, v)


def _adamw_replicated(parts, w, m, v):
    p, r, c = parts.shape
    names = SMALL_NAMES
    shapes = [w[n].shape for n in names]

    def body(*refs):
        p_ref = refs[0]
        ins = refs[1:1 + 3 * len(names)]
        outs = refs[1 + 3 * len(names):-2]
        loss_ref, sum_scr = refs[-2], refs[-1]
        total = p_ref[0]
        for i in range(1, p):
            total = total + p_ref[i]
        sum_scr[...] = total
        for k, n in enumerate(names):
            w_ref, m_ref, v_ref = ins[3 * k:3 * k + 3]
            g_ref, d_ref, nm_ref, nv_ref = outs[4 * k:4 * k + 4]
            for row, lane0, width, src_row in _small_pieces(n, shapes[k]):
                here = (slice(row, row + 1), slice(lane0, lane0 + width))
                g = sum_scr[src_row:src_row + 1, 0:width]
                delta, m_new, v_new = _adamw_math(g, w_ref[here], m_ref[here], v_ref[here])
                g_ref[here] = g
                d_ref[here] = delta
                nm_ref[here] = m_new
                nv_ref[here] = v_new
        loss_ref[...] = sum_scr[SMALL_LOSS_ROW:SMALL_LOSS_ROW + 1, 0:1]

    whole = lambda shape: pl.BlockSpec(shape, lambda i: (0,) * len(shape))
    args = [parts] + [t[n] for n in names for t in (w, m, v)]
    out_shape = [jax.ShapeDtypeStruct(s, F32) for s in shapes for _ in range(4)] + [jax.ShapeDtypeStruct((1, 1), F32)]
    res = pl.pallas_call(
        body, name="adamw_replicated", grid=(1,),
        in_specs=[whole(t.shape) for t in args], out_specs=[whole(s.shape) for s in out_shape],
        out_shape=out_shape, scratch_shapes=[pltpu.VMEM((r, c), F32)],
        compiler_params=_params(("arbitrary",)),
    )(*args)
    results = [{n: res[4 * k + j] for k, n in enumerate(names)} for j in range(4)]
    return results, res[-1]


SHARDED_NAMES = ("w_in", "ml_conv_w", "w_out", "ca_wq", "ca_wkv", "ca_wo", "ffn_w_up", "ffn_conv_w", "ffn_w_down")
SMALL_NAMES = ("b_in", "hg_lb_logits", "hg_norm_w", "ml_conv_b", "ml_norm_w", "ln1_g", "ln1_b",
               "ln2_g", "ln2_b", "ffn_conv_b", "ln3_g", "ln3_b")
WEIGHT_NAMES = ("w_in", "b_in", "hg_lb_logits", "hg_norm_w", "ml_conv_w", "ml_conv_b", "ml_norm_w", "w_out",
                "ln1_g", "ln1_b", "ca_wq", "ca_wkv", "ca_wo", "ln2_g", "ln2_b", "ffn_w_up", "ffn_conv_w",
                "ffn_conv_b", "ffn_w_down", "ln3_g", "ln3_b")
PAD_TO = {"ffn_conv_w": UP_SHARD_P}
SMALL_ROWS = 24
SMALL_W = D_MODEL
SMALL_SHAPES = {"b_in": (1, D_IN), "hg_lb_logits": (2, D_GROUP), "hg_norm_w": (1, D_GROUP),
                "ml_conv_b": (1, 2 * D_GROUP), "ml_norm_w": (1, D_GROUP), "ln1_g": (1, D_MODEL), "ln1_b": (1, D_MODEL),
                "ln2_g": (1, D_MODEL), "ln2_b": (1, D_MODEL), "ffn_conv_b": (1, D_UP), "ln3_g": (1, D_MODEL),
                "ln3_b": (1, D_MODEL)}


def _shard_2d(name, block):
    t = block[0]
    if name in PAD_TO:
        t = jnp.pad(t, ((0, 0), (0, PAD_TO[name] - t.shape[1])))
    return t


TRANSPOSED = ("w_in", "ffn_w_up")


def _update_shard(name, block):
    if name not in TRANSPOSED:
        return _shard_2d(name, block)
    return jnp.transpose(block, (0, 2, 1))[0]


def _shard_like(name, t, like):
    if name in TRANSPOSED:
        out = jnp.transpose(t[None], (0, 2, 1))
        return with_layout_constraint(out, Layout(major_to_minor=(0, 2, 1))) if name == "ffn_w_up" else out
    return t[:, :like.shape[2]][None]


def _pad_cols(t, width):
    return jnp.pad(t, ((0, 0), (0, width - t.shape[1])))


FIRST_NAMES = ("w_in", "ml_conv_w")
FFN_NAMES = ("ffn_w_up", "ffn_w_down", "ffn_conv_w")
MID_NAMES = ("ca_wo", "ca_wq", "ca_wkv", "w_out")


def _first_weights(g, small):
    w = dict(small)
    w["w_in_t"] = g["w_in"].reshape(D_IN, D_MODEL)
    w["w_in_gate_t"] = jnp.pad(w["w_in_t"][D_IN_MAIN:], ((0, LANES - (D_IN - D_IN_MAIN)), (0, 0)))
    w["b_in_main"] = small["b_in"][:, :D_IN_MAIN]
    w["b_in_gate"] = _pad_cols(small["b_in"][:, D_IN_MAIN:], LANES)
    w["ml_conv_w"] = jnp.transpose(g["ml_conv_w"], (1, 0, 2)).reshape(ML_CONV, 2 * D_GROUP)
    return w


def _mid_weights(g):
    w = {n: g[n].reshape(D_MODEL, D_MODEL) for n in ("w_out", "ca_wq", "ca_wo")}
    w["ca_wkv"] = g["ca_wkv"]
    return w


FFN_UP_NAMES = ("ffn_w_up", "ffn_conv_w")
FFN_DOWN_NAMES = ("ffn_w_down",)


def _ffn_up_weights(g, small):
    w = {"ffn_w_up_t": g["ffn_w_up"].reshape(D_UP, D_MODEL)}
    w["ffn_conv_w"] = jnp.transpose(g["ffn_conv_w"][:, :, :UP_SHARD], (1, 0, 2)).reshape(FFN_CONV, D_UP)
    w["ffn_conv_b"] = small["ffn_conv_b"].reshape(1, D_UP)
    return w


def _ffn_down_weights(g):
    return {"ffn_w_down": g["ffn_w_down"].reshape(D_FF, D_MODEL)}


def _owner_stack(n, grads):
    if n == "w_in":
        rows = jnp.concatenate([grads["w_in_main"], grads["w_in_gate"][:D_IN - D_IN_MAIN]], axis=0)
        return rows.reshape(N_DEV, W_IN_SHARD, D_MODEL)
    if n == "ffn_w_up":
        return grads[n].reshape(N_DEV, UP_SHARD, D_MODEL)
    if n in ("w_out", "ca_wq", "ca_wo"):
        return grads[n].reshape(N_DEV, D_MODEL // N_DEV, D_MODEL)
    if n == "ffn_w_down":
        return grads[n].reshape(N_DEV, D_FF // N_DEV, D_MODEL)
    if n == "ml_conv_w":
        return jnp.transpose(grads[n].reshape(ML_CONV, N_DEV, LANES), (1, 0, 2))
    if n == "ffn_conv_w":
        shards = jnp.transpose(grads[n].reshape(FFN_CONV, N_DEV, UP_SHARD), (1, 0, 2))
        return jnp.pad(shards, ((0, 0), (0, 0), (0, UP_SHARD_P - UP_SHARD)))
    return grads[n]


def _small_grads(grads):
    out = {n: grads[n] for n in SMALL_NAMES if n in grads}
    out["b_in"] = jnp.concatenate([grads["b_in_main"], grads["b_in_gate"][:, :D_IN - D_IN_MAIN]], axis=1)
    return out


def _small_rows(shape):
    return shape[0] if shape[1] <= SMALL_W else -(-shape[1] // SMALL_W)


SMALL_BASE = {n: sum(_small_rows(SMALL_SHAPES[k]) for k in SMALL_NAMES[:i]) for i, n in enumerate(SMALL_NAMES)}
SMALL_LOSS_ROW = sum(_small_rows(SMALL_SHAPES[n]) for n in SMALL_NAMES)
assert SMALL_LOSS_ROW < SMALL_ROWS


def _small_pieces(name, shape):
    base = SMALL_BASE[name]
    if shape[1] <= SMALL_W:
        return [(i, 0, shape[1], base + i) for i in range(shape[0])]
    return [(0, k * SMALL_W, min(SMALL_W, shape[1] - k * SMALL_W), base + k) for k in range(_small_rows(shape))]


def _pack_small(p, loss):
    rows = []
    for n in SMALL_NAMES:
        t = p[n]
        nrows = _small_rows(t.shape)
        if t.shape[1] <= SMALL_W:
            rows.append(_pad_cols(t, SMALL_W))
        else:
            rows.append(_pad_cols(t, nrows * SMALL_W).reshape(nrows, SMALL_W))
    rows.append(_pad_cols(loss, SMALL_W))
    slab = jnp.concatenate(rows, axis=0)
    return jnp.pad(slab, ((0, SMALL_ROWS - slab.shape[0]), (0, 0)))


def kernel(x, mem, w_in, b_in, hg_lb_logits, hg_norm_w, ml_conv_w, ml_conv_b, ml_norm_w, w_out, ln1_g, ln1_b, ca_wq, ca_wkv, ca_wo, ln2_g, ln2_b, ffn_w_up, ffn_conv_w, ffn_conv_b, ffn_w_down, ln3_g, ln3_b, loss_target, m_w_in, m_b_in, m_hg_lb_logits, m_hg_norm_w, m_ml_conv_w, m_ml_conv_b, m_ml_norm_w, m_w_out, m_ln1_g, m_ln1_b, m_ca_wq, m_ca_wkv, m_ca_wo, m_ln2_g, m_ln2_b, m_ffn_w_up, m_ffn_conv_w, m_ffn_conv_b, m_ffn_w_down, m_ln3_g, m_ln3_b, v_w_in, v_b_in, v_hg_lb_logits, v_hg_norm_w, v_ml_conv_w, v_ml_conv_b, v_ml_norm_w, v_w_out, v_ln1_g, v_ln1_b, v_ca_wq, v_ca_wkv, v_ca_wo, v_ln2_g, v_ln2_b, v_ffn_w_up, v_ffn_conv_w, v_ffn_conv_b, v_ffn_w_down, v_ln3_g, v_ln3_b):
    params = dict(w_in=w_in, b_in=b_in, hg_lb_logits=hg_lb_logits, hg_norm_w=hg_norm_w, ml_conv_w=ml_conv_w,
                  ml_conv_b=ml_conv_b, ml_norm_w=ml_norm_w, w_out=w_out, ln1_g=ln1_g, ln1_b=ln1_b, ca_wq=ca_wq,
                  ca_wkv=ca_wkv, ca_wo=ca_wo, ln2_g=ln2_g, ln2_b=ln2_b, ffn_w_up=ffn_w_up, ffn_conv_w=ffn_conv_w,
                  ffn_conv_b=ffn_conv_b, ffn_w_down=ffn_w_down, ln3_g=ln3_g, ln3_b=ln3_b)
    mom1 = dict(w_in=m_w_in, b_in=m_b_in, hg_lb_logits=m_hg_lb_logits, hg_norm_w=m_hg_norm_w,
                ml_conv_w=m_ml_conv_w, ml_conv_b=m_ml_conv_b, ml_norm_w=m_ml_norm_w, w_out=m_w_out, ln1_g=m_ln1_g,
                ln1_b=m_ln1_b, ca_wq=m_ca_wq, ca_wkv=m_ca_wkv, ca_wo=m_ca_wo, ln2_g=m_ln2_g, ln2_b=m_ln2_b,
                ffn_w_up=m_ffn_w_up, ffn_conv_w=m_ffn_conv_w, ffn_conv_b=m_ffn_conv_b, ffn_w_down=m_ffn_w_down,
                ln3_g=m_ln3_g, ln3_b=m_ln3_b)
    mom2 = dict(w_in=v_w_in, b_in=v_b_in, hg_lb_logits=v_hg_lb_logits, hg_norm_w=v_hg_norm_w,
                ml_conv_w=v_ml_conv_w, ml_conv_b=v_ml_conv_b, ml_norm_w=v_ml_norm_w, w_out=v_w_out, ln1_g=v_ln1_g,
                ln1_b=v_ln1_b, ca_wq=v_ca_wq, ca_wkv=v_ca_wkv, ca_wo=v_ca_wo, ln2_g=v_ln2_g, ln2_b=v_ln2_b,
                ffn_w_up=v_ffn_w_up, ffn_conv_w=v_ffn_conv_w, ffn_conv_b=v_ffn_conv_b, ffn_w_down=v_ffn_w_down,
                ln3_g=v_ln3_g, ln3_b=v_ln3_b)

    x_idx, y_idx, c_idx = _coords()
    as_index = lambda v: jnp.reshape(v, (1,)).astype(jnp.int32)
    me = as_index(4 * x_idx + 2 * y_idx + c_idx)
    small_params = {n: params[n] for n in SMALL_NAMES}

    shards = {n: _update_shard(n, params[n]) for n in SHARDED_NAMES}
    m_shards = {n: _update_shard(n, mom1[n]) for n in SHARDED_NAMES}
    v_shards = {n: _update_shard(n, mom2[n]) for n in SHARDED_NAMES}
    outgoing = {n: shards[n] if "conv" in n else shards[n].astype(BF16) for n in SHARDED_NAMES}
    to_send = lambda names: [outgoing[n] for n in names]
    glue = [*m_shards.values(), *v_shards.values(), *shards.values(),
            *[outgoing[n] for n in SHARDED_NAMES if n not in FIRST_NAMES]]
    first = dict(zip(FIRST_NAMES, _two_level_gather(to_send(FIRST_NAMES), glue, "weights_gather_first")))
    mid_started, through = _direct_start(True, to_send(MID_NAMES), first["w_in"], "weights_gather_start_mid")
    ffn_started, through = _direct_start(True, to_send(FFN_UP_NAMES), through, "weights_gather_start_ffn_up")
    down_started, first["w_in"] = _direct_start(True, to_send(FFN_DOWN_NAMES), through,
                                                "weights_gather_start_ffn_down")

    def gathered_weights(names, started, after, tag):
        mine, lands = _direct_wait(True, started, after, "weights_gather_wait_" + tag)
        return {n: lax.dynamic_update_index_in_dim(land, own, me[0], 0) for n, own, land in zip(names, mine, lands)}

    started = {}

    def start_group(names, tag):
        def hook(grads, through):
            stacks = [_owner_stack(n, grads).astype(BF16) for n in names]
            started[tag], through = _direct_start(False, stacks, through, "grads_start_" + tag)
            return through
        return hook

    def start_small(grads, loss, through):
        started["small"], through = _direct_start(True, [_pack_small(_small_grads(grads), loss)], through,
                                                  "small_gather_start")
        return through

    loss, grad_x, grads = _local_step(
        x[0], mem[0], loss_target[0], _first_weights(first, small_params),
        lambda y: _mid_weights(gathered_weights(MID_NAMES, mid_started, y, "mid")),
        lambda x2: _ffn_up_weights(gathered_weights(FFN_UP_NAMES, ffn_started, x2, "ffn_up"), small_params),
        lambda hid: _ffn_down_weights(gathered_weights(FFN_DOWN_NAMES, down_started, hid, "ffn_down")),
        start_group(FFN_NAMES, "ffn"), start_group(MID_NAMES, "mid"), start_small, start_group(FIRST_NAMES, "last"))

    updated, sharded_out = {}, {}

    def update_group(names, tag, after):
        stacks, lands = _direct_wait(False, started[tag], after, "grads_wait_" + tag)
        for n, st, land in zip(names, stacks, lands):
            updated[n] = _adamw_sharded(me, st, land, shards[n], m_shards[n], v_shards[n], "adamw_" + n)
            sharded_out[n] = [_shard_like(n, t, params[n]) for t in updated[n]]

    update_group(FFN_NAMES, "ffn", grad_x)
    update_group(MID_NAMES, "mid", grad_x)
    own_small, small_lands = _direct_wait(True, started["small"], grad_x, "small_gather_wait")
    small_parts = lax.dynamic_update_index_in_dim(small_lands[0], own_small[0], me[0], 0)
    small_out, total_loss = _adamw_replicated(small_parts, small_params, {n: mom1[n] for n in SMALL_NAMES},
                                              {n: mom2[n] for n in SMALL_NAMES})
    done = [t for n in FFN_NAMES + MID_NAMES for t in updated[n]]
    done += [t for small in small_out for t in small.values()]
    update_group(FIRST_NAMES, "last", done)

    outs = []
    for k, small in enumerate(small_out):
        outs.extend(sharded_out[n][k] if n in sharded_out else small[n] for n in WEIGHT_NAMES)
    return (total_loss[0, 0], grad_x[None], *outs)
```

```python
import functools
import math

import jax
import jax.numpy as jnp
from jax import lax
from jax.experimental import pallas as pl
from jax.experimental.layout import Layout, with_layout_constraint
from jax.experimental.pallas import tpu as pltpu

F32 = jnp.float32
BF16 = jnp.bfloat16
HIGHEST = lax.Precision.HIGHEST
MESH = pl.DeviceIdType.MESH

N_DEV = 8
D_MODEL = 1024
N_MEM = 256
N_HEADS = 4
D_HEAD = 128
D_GROUP = N_HEADS * D_HEAD
CHUNK = 64
ML_CONV = 4
FFN_CONV = 3
D_FF = 2816
D_UP = 2 * D_FF
CA_HEADS = 4
CA_DH = D_MODEL // CA_HEADS
LANES = 128
SUBLANES = 8
D_IN = 8 * D_GROUP + 2 * N_HEADS
D_IN_MAIN = 8 * D_GROUP
W_IN_SHARD = D_IN // N_DEV
UP_SHARD = D_UP // N_DEV
UP_SHARD_P = 768
UP_TILE = D_UP // 4
ALPHA = 2.0 ** 0.25
LN_EPS = 1e-5
NEG_BIG = -1e30
ADAM_LR = 0.001
ADAM_B1 = 0.9
ADAM_B2 = 0.999
ADAM_EPS = 1e-08
ADAM_WD = 0.01
ADAM_STEP = 10
VMEM_LIMIT = 56 * 1024 * 1024

SEG_MQ = 4 * D_GROUP // LANES
VO_BLOCK = 3


def _params(sem):
    return pltpu.CompilerParams(dimension_semantics=sem, vmem_limit_bytes=VMEM_LIMIT)


def _dg(a, b, ca, cb, precision=None):
    return lax.dot_general(a, b, (((ca,), (cb,)), ((), ())), precision=precision,
                           preferred_element_type=F32)


def _nn_raw(a, b):
    return _dg(a.astype(BF16), b.astype(BF16), 1, 0)


def _nt_raw(a, b):
    return _dg(a.astype(BF16), b.astype(BF16), 1, 1)


def _tn_raw(a, b):
    return _dg(a.astype(BF16), b.astype(BF16), 0, 0)


@jax.custom_vjp
def _nn(a, b):
    return _nn_raw(a, b)


_nn.defvjp(lambda a, b: (_nn_raw(a, b), (a, b)),
           lambda res, g: (_nt_raw(g, res[1]), _tn_raw(res[0], g)))


@jax.custom_vjp
def _nt(a, b):
    return _nt_raw(a, b)


_nt.defvjp(lambda a, b: (_nt_raw(a, b), (a, b)),
           lambda res, g: (_nn_raw(g, res[1]), _tn_raw(g, res[0])))


@jax.custom_vjp
def _tn(a, b):
    return _tn_raw(a, b)


_tn.defvjp(lambda a, b: (_tn_raw(a, b), (a, b)),
           lambda res, g: (_nt_raw(res[1], g), _nn_raw(res[0], g)))


def _layer_norm(z, g, b):
    mu = jnp.mean(z, axis=-1, keepdims=True)
    var = jnp.mean(jnp.square(z - mu), axis=-1, keepdims=True)
    return (z - mu) * lax.rsqrt(var + LN_EPS) * g + b


def _matmul_nn(a, w, bias, tm, tn, name, out_dtype=F32):
    m, k = a.shape
    if w.ndim == 3:
        n = w.shape[0] * w.shape[2]
        assert tn == w.shape[2]
        w_spec = pl.BlockSpec((None, k, tn), lambda i, j: (j, 0, 0))
    else:
        n = w.shape[1]
        w_spec = pl.BlockSpec((k, tn), lambda i, j: (0, j))

    def body(*refs):
        a_ref, w_ref = refs[0], refs[1]
        o_ref = refs[-1]
        acc = _nn_raw(a_ref[...], w_ref[...])
        if bias is not None:
            acc = acc + refs[2][...]
        o_ref[...] = acc.astype(o_ref.dtype)

    in_specs = [pl.BlockSpec((tm, k), lambda i, j: (i, 0)), w_spec]
    args = [a, w]
    if bias is not None:
        in_specs.append(pl.BlockSpec((1, tn), lambda i, j: (0, j)))
        args.append(bias)
    return pl.pallas_call(
        body, name=name, grid=(m // tm, n // tn), in_specs=in_specs,
        out_specs=pl.BlockSpec((tm, tn), lambda i, j: (i, j)),
        out_shape=jax.ShapeDtypeStruct((m, n), out_dtype),
        compiler_params=_params(("parallel", "parallel")),
    )(*args)


def _matmul_nt(d, w, tm, tk, name, k_out=None, bias=None, out_dtype=F32):
    m, n = d.shape
    k = k_out or w.shape[0]

    def body(*refs):
        acc = _nt_raw(refs[0][...], refs[1][...])
        if bias is not None:
            acc = acc + refs[2][...]
        refs[-1][...] = acc.astype(refs[-1].dtype)

    in_specs = [pl.BlockSpec((tm, n), lambda i, j: (i, 0)), pl.BlockSpec((tk, n), lambda i, j: (j, 0))]
    args = [d, w]
    if bias is not None:
        in_specs.append(pl.BlockSpec((1, tk), lambda i, j: (0, j)))
        args.append(bias)
    return pl.pallas_call(
        body, name=name, grid=(m // tm, k // tk), in_specs=in_specs,
        out_specs=pl.BlockSpec((tm, tk), lambda i, j: (i, j)),
        out_shape=jax.ShapeDtypeStruct((m, k), out_dtype),
        compiler_params=_params(("parallel", "parallel")),
    )(*args)


def _input_projection(x, w_t, w_gate_t, b_main, b_gate, tm, tk):
    m, n = x.shape

    def body(x_ref, w_ref, wg_ref, b_ref, bg_ref, o_ref, g_ref):
        lhs = x_ref[...].astype(BF16)
        o_ref[...] = _nt_raw(lhs, w_ref[...]) + b_ref[...]

        @pl.when(pl.program_id(1) == 0)
        def _():
            g_ref[...] = _nt_raw(lhs, wg_ref[...]) + bg_ref[...]

    return pl.pallas_call(
        body, name="proj", grid=(m // tm, D_IN_MAIN // tk),
        in_specs=[pl.BlockSpec((tm, n), lambda i, j: (i, 0)), pl.BlockSpec((tk, n), lambda i, j: (j, 0)),
                  pl.BlockSpec((LANES, n), lambda i, j: (0, 0)), pl.BlockSpec((1, tk), lambda i, j: (0, j)),
                  pl.BlockSpec((1, LANES), lambda i, j: (0, 0))],
        out_specs=[pl.BlockSpec((tm, tk), lambda i, j: (i, j)), pl.BlockSpec((tm, LANES), lambda i, j: (i, 0))],
        out_shape=[jax.ShapeDtypeStruct((m, D_IN_MAIN), F32), jax.ShapeDtypeStruct((m, LANES), F32)],
        compiler_params=_params(("parallel", "arbitrary")),
    )(x, w_t, w_gate_t, b_main, b_gate)


def _input_projection_grads(d_proj, d_gates, x, tm, tt):
    t, m = d_proj.shape
    n = x.shape[1]
    last = t // tt - 1
    n_gate = D_IN - D_IN_MAIN

    def body(a_ref, g_ref, x_ref, o_ref, acc_ref, accg_ref):
        i, kk = pl.program_id(0), pl.program_id(1)

        @pl.when(kk == 0)
        def _():
            acc_ref[...] = jnp.zeros_like(acc_ref)

        @pl.when((kk == 0) & (i == 0))
        def _():
            accg_ref[...] = jnp.zeros_like(accg_ref)

        rhs = x_ref[...].astype(BF16)
        acc_ref[...] += _tn_raw(a_ref[...], rhs)

        @pl.when(i == 0)
        def _():
            accg_ref[...] += _tn_raw(g_ref[...], rhs)

        @pl.when(kk == last)
        def _():
            o_ref[pl.ds(pl.multiple_of(i * tm, tm), tm), :] = acc_ref[...].astype(o_ref.dtype)

        @pl.when((kk == last) & (i == 0))
        def _():
            o_ref[m:m + n_gate, :] = accg_ref[0:n_gate, :].astype(o_ref.dtype)

    return pl.pallas_call(
        body, name="d_w_in", grid=(m // tm, t // tt),
        in_specs=[pl.BlockSpec((tt, tm), lambda i, kk: (kk, i)), pl.BlockSpec((tt, LANES), lambda i, kk: (kk, 0)),
                  pl.BlockSpec((tt, n), lambda i, kk: (kk, 0))],
        out_specs=pl.BlockSpec((m + n_gate, n), lambda i, kk: (0, 0)),
        out_shape=jax.ShapeDtypeStruct((m + n_gate, n), BF16),
        scratch_shapes=[pltpu.VMEM((tm, n), F32), pltpu.VMEM((LANES, n), F32)],
        compiler_params=_params(("arbitrary", "arbitrary")),
    )(d_proj, d_gates, x)


def _matmul_nn_sum(pairs, add, scale, tm, name):
    m = pairs[0][0].shape[0]
    n = pairs[0][1].shape[1]
    in_specs, args = [], []
    for a, w, row0 in pairs:
        kk = a.shape[1]
        in_specs += [pl.BlockSpec((tm, kk), lambda i: (i, 0)),
                     pl.BlockSpec((kk, n), lambda i, blk=row0 // kk: (blk, 0))]
        args += [a, w]
    if add is not None:
        in_specs.append(pl.BlockSpec((tm, n), lambda i: (i, 0)))
        args.append(add)

    def body(*refs):
        acc = None
        for p in range(len(pairs)):
            term = _nn_raw(refs[2 * p][...], refs[2 * p + 1][...])
            acc = term if acc is None else acc + term
        if add is not None:
            acc = acc + scale * refs[2 * len(pairs)][...]
        refs[-1][...] = acc

    return pl.pallas_call(
        body, name=name, grid=(m // tm,), in_specs=in_specs,
        out_specs=pl.BlockSpec((tm, n), lambda i: (i, 0)),
        out_shape=jax.ShapeDtypeStruct((m, n), F32),
        compiler_params=_params(("parallel",)),
    )(*args)


def _matmul_tn(a, b, tm, tn, tt, name, shards=None, shard0=0, group=1, into=None, colsum=False, rows=None, row0=0):
    t, m = a.shape
    n = b.shape[1]
    assert not colsum or tm == m
    n_in = 2 + (into is not None)
    out_dtype = BF16
    per_step = 1 if shards is None else group
    width = per_step * tn

    def body(*refs):
        a_ref, b_ref = refs[0], refs[1]
        o_ref, acc_ref = refs[n_in], refs[-1]
        first = pl.program_id(2) == 0

        @pl.when(first)
        def _():
            acc_ref[...] = jnp.zeros_like(acc_ref)

        if shards is None:
            acc_ref[...] += _tn_raw(a_ref[...], b_ref[...])
        else:
            lhs = a_ref[...].astype(BF16)
            for g in range(per_step):
                acc_ref[g] += _tn_raw(lhs, b_ref[:, g * tn:(g + 1) * tn])

        @pl.when(pl.program_id(2) == t // tt - 1)
        def _():
            o_ref[...] = acc_ref[...].astype(o_ref.dtype)

        if colsum:
            s_ref = refs[n_in + 1]

            @pl.when(first)
            def _():
                s_ref[...] = jnp.zeros_like(s_ref)

            s_ref[...] += jnp.sum(b_ref[...], axis=0, keepdims=True)

    in_specs = [pl.BlockSpec((tt, tm), lambda i, j, kk: (kk, i)),
                pl.BlockSpec((tt, width), lambda i, j, kk: (kk, j))]
    args = [a, b]
    aliases = {}
    if into is not None:
        in_specs.append(pl.BlockSpec(memory_space=pl.ANY))
        args.append(into)
        aliases = {2: 0}
    if shards is None:
        out_specs = [pl.BlockSpec((tm, tn), lambda i, j, kk: (row0 // tm + i, j))]
        out_shape = [jax.ShapeDtypeStruct((rows or m, n), out_dtype)]
        acc = pltpu.VMEM((tm, tn), F32)
    else:
        out_specs = [pl.BlockSpec((per_step, tm, tn), lambda i, j, kk: (shard0 // per_step + j, i, 0))]
        out_shape = [jax.ShapeDtypeStruct((shards, m, tn), out_dtype)]
        acc = pltpu.VMEM((per_step, tm, tn), F32)
    if colsum:
        out_specs.append(pl.BlockSpec((1, tn), lambda i, j, kk: (0, j)))
        out_shape.append(jax.ShapeDtypeStruct((1, n), F32))
    res = pl.pallas_call(
        body, name=name, grid=(m // tm, n // width, t // tt), in_specs=in_specs, out_specs=out_specs,
        out_shape=out_shape, input_output_aliases=aliases, scratch_shapes=[acc],
        compiler_params=_params(("parallel", "parallel", "arbitrary")),
    )(*args)
    return res if colsum else res[0]


ROW_TILE = 64


def _stack(ref, start, rows):
    return ref[pl.ds(start, rows), :].astype(F32).reshape(rows // SUBLANES, SUBLANES, LANES)


def _vreg_rows(ref, n):
    return [jnp.broadcast_to(ref[j:j + 1, :], (SUBLANES, LANES))[None] for j in range(n)]


def _column_total(acc):
    return jnp.sum(acc, axis=0, keepdims=True)


def _conv_fwd_tile(pad_ref, taps_w, bias, r0, rows):
    taps = len(taps_w)
    acc = bias
    for j in range(taps):
        acc = acc + _stack(pad_ref, SUBLANES - (taps - 1 - j) + r0, rows) * taps_w[j]
    return acc


def _conv_grads_tile(pad_ref, dpad_ref, dx_ref, taps_w, dws, r0, rows):
    taps = len(taps_w)
    x_rows = _stack(pad_ref, SUBLANES + r0, rows)
    dx = None
    for j in range(taps):
        d_shifted = _stack(dpad_ref, r0 + (taps - 1 - j), rows)
        term = d_shifted * taps_w[j]
        dx = term if dx is None else dx + term
        dws[j] = dws[j] + jnp.sum(d_shifted * x_rows, axis=0)
    dx_ref[r0:r0 + rows, :] = dx.reshape(rows, LANES).astype(dx_ref.dtype)
    return jnp.sum(dx, axis=0)


def _ml_conv_fwd(proj, conv_w, conv_b):
    s = proj.shape[0]
    nblk = 2 * D_GROUP // LANES

    def body(x_ref, w_ref, b_ref, o_ref, pad_ref):
        pad_ref[0:SUBLANES, :] = jnp.zeros((SUBLANES, LANES), F32)
        pad_ref[SUBLANES:, :] = x_ref[...].astype(F32)
        taps_w, bias = _vreg_rows(w_ref, ML_CONV), _vreg_rows(b_ref, 1)[0]
        for r0 in range(0, s, ROW_TILE):
            rows = min(ROW_TILE, s - r0)
            o_ref[r0:r0 + rows, :] = jax.nn.silu(_conv_fwd_tile(pad_ref, taps_w, bias, r0, rows)).reshape(rows, LANES)

    return pl.pallas_call(
        body, name="ml_conv_fwd", grid=(nblk,),
        in_specs=[pl.BlockSpec((s, LANES), lambda j: (0, SEG_MQ + j)),
                  pl.BlockSpec((ML_CONV, LANES), lambda j: (0, j)),
                  pl.BlockSpec((1, LANES), lambda j: (0, j))],
        out_specs=pl.BlockSpec((s, LANES), lambda j: (0, j)),
        out_shape=jax.ShapeDtypeStruct((s, 2 * D_GROUP), F32),
        scratch_shapes=[pltpu.VMEM((s + SUBLANES, LANES), F32)],
        compiler_params=_params(("parallel",)),
    )(proj, conv_w, conv_b)


def _ml_conv_bwd(proj, conv_w, conv_b, d_qk, d_proj):
    s = proj.shape[0]
    nblk = 2 * D_GROUP // LANES

    def body(x_ref, w_ref, b_ref, dy_ref, _, dx_ref, dw_ref, db_ref, dxs_ref, pad_ref, dpad_ref):
        pad_ref[0:SUBLANES, :] = jnp.zeros((SUBLANES, LANES), F32)
        pad_ref[SUBLANES:, :] = x_ref[...].astype(F32)
        dpad_ref[s:, :] = jnp.zeros((SUBLANES, LANES), F32)
        taps_w, bias = _vreg_rows(w_ref, ML_CONV), _vreg_rows(b_ref, 1)[0]
        db = jnp.zeros((SUBLANES, LANES), F32)
        for r0 in range(0, s, ROW_TILE):
            rows = min(ROW_TILE, s - r0)
            pre = _conv_fwd_tile(pad_ref, taps_w, bias, r0, rows)
            _, vjp = jax.vjp(jax.nn.silu, pre)
            d_pre, = vjp(_stack(dy_ref, r0, rows))
            dpad_ref[r0:r0 + rows, :] = d_pre.reshape(rows, LANES)
            db = db + jnp.sum(d_pre, axis=0)
        db_ref[...] = _column_total(db)
        dws = [jnp.zeros((SUBLANES, LANES), F32) for _ in range(ML_CONV)]
        dx_sum = jnp.zeros((SUBLANES, LANES), F32)
        for r0 in range(0, s, ROW_TILE):
            dx_sum = dx_sum + _conv_grads_tile(pad_ref, dpad_ref, dx_ref, taps_w, dws, r0, min(ROW_TILE, s - r0))
        dxs_ref[...] = _column_total(dx_sum)
        for j in range(ML_CONV):
            dw_ref[j:j + 1, :] = _column_total(dws[j])

    return pl.pallas_call(
        body, name="ml_conv_bwd", grid=(nblk,),
        in_specs=[pl.BlockSpec((s, LANES), lambda j: (0, SEG_MQ + j)),
                  pl.BlockSpec((ML_CONV, LANES), lambda j: (0, j)),
                  pl.BlockSpec((1, LANES), lambda j: (0, j)),
                  pl.BlockSpec((s, LANES), lambda j: (0, j)),
                  pl.BlockSpec(memory_space=pl.ANY)],
        out_specs=[pl.BlockSpec((s, LANES), lambda j: (0, SEG_MQ + j)),
                   pl.BlockSpec((ML_CONV, LANES), lambda j: (0, j)),
                   pl.BlockSpec((1, LANES), lambda j: (0, j)),
                   pl.BlockSpec((1, LANES), lambda j: (0, j))],
        out_shape=[jax.ShapeDtypeStruct(d_proj.shape, d_proj.dtype),
                   jax.ShapeDtypeStruct((ML_CONV, 2 * D_GROUP), F32),
                   jax.ShapeDtypeStruct((1, 2 * D_GROUP), F32),
                   jax.ShapeDtypeStruct((1, 2 * D_GROUP), F32)],
        input_output_aliases={4: 0},
        scratch_shapes=[pltpu.VMEM((s + SUBLANES, LANES), F32), pltpu.VMEM((s + SUBLANES, LANES), F32)],
        compiler_params=_params(("parallel",)),
    )(proj, conv_w, conv_b, d_qk, d_proj)


def _gelu_mul(a, b):
    return jax.nn.gelu(a) * b


GELU_C = math.sqrt(2.0 / math.pi)
GELU_K = 0.044715


def _gelu_mul_grads(a, b, d):
    a2 = a * a
    t = jnp.tanh(GELU_C * (a + GELU_K * (a * a2)))
    cdf = 0.5 * (1.0 + t)
    slope = cdf + (0.5 * GELU_C) * a * (1.0 - t * t) * (1.0 + (3.0 * GELU_K) * a2)
    return d * b * slope, d * (a * cdf)


FFN_BLOCKS = D_FF // LANES


def _ffn_conv_fwd(u, conv_w, conv_b):
    s = u.shape[0]

    def body(g_ref, v_ref, wg_ref, wv_ref, bg_ref, bv_ref, o_ref, gpad_ref, vpad_ref):
        for pad_ref, x_ref in ((gpad_ref, g_ref), (vpad_ref, v_ref)):
            pad_ref[0:SUBLANES, :] = jnp.zeros((SUBLANES, LANES), F32)
            pad_ref[SUBLANES:, :] = x_ref[...].astype(F32)
        taps_g, bias_g = _vreg_rows(wg_ref, FFN_CONV), _vreg_rows(bg_ref, 1)[0]
        taps_v, bias_v = _vreg_rows(wv_ref, FFN_CONV), _vreg_rows(bv_ref, 1)[0]
        for r0 in range(0, s, ROW_TILE):
            rows = min(ROW_TILE, s - r0)
            ug = _conv_fwd_tile(gpad_ref, taps_g, bias_g, r0, rows)
            uv = _conv_fwd_tile(vpad_ref, taps_v, bias_v, r0, rows)
            o_ref[r0:r0 + rows, :] = _gelu_mul(ug, uv).reshape(rows, LANES).astype(o_ref.dtype)

    col = lambda off: (lambda j: (0, off + j))
    return pl.pallas_call(
        body, name="ffn_conv_fwd", grid=(FFN_BLOCKS,),
        in_specs=[pl.BlockSpec((s, LANES), col(0)), pl.BlockSpec((s, LANES), col(FFN_BLOCKS)),
                  pl.BlockSpec((FFN_CONV, LANES), col(0)), pl.BlockSpec((FFN_CONV, LANES), col(FFN_BLOCKS)),
                  pl.BlockSpec((1, LANES), col(0)), pl.BlockSpec((1, LANES), col(FFN_BLOCKS))],
        out_specs=pl.BlockSpec((s, LANES), col(0)),
        out_shape=jax.ShapeDtypeStruct((s, D_FF), BF16),
        scratch_shapes=[pltpu.VMEM((s + SUBLANES, LANES), F32), pltpu.VMEM((s + SUBLANES, LANES), F32)],
        compiler_params=_params(("parallel",)),
    )(u, u, conv_w, conv_w, conv_b, conv_b)


def _ffn_conv_bwd(u, conv_w, conv_b, d_h):
    s = u.shape[0]

    def body(g_ref, v_ref, wg_ref, wv_ref, bg_ref, bv_ref, dh_ref,
             dug_ref, duv_ref, dwg_ref, dwv_ref, dbg_ref, dbv_ref,
             gpad_ref, vpad_ref, dgpad_ref, dvpad_ref):
        for pad_ref, x_ref in ((gpad_ref, g_ref), (vpad_ref, v_ref)):
            pad_ref[0:SUBLANES, :] = jnp.zeros((SUBLANES, LANES), F32)
            pad_ref[SUBLANES:, :] = x_ref[...].astype(F32)
        dgpad_ref[s:, :] = jnp.zeros((SUBLANES, LANES), F32)
        dvpad_ref[s:, :] = jnp.zeros((SUBLANES, LANES), F32)
        taps_g, bias_g = _vreg_rows(wg_ref, FFN_CONV), _vreg_rows(bg_ref, 1)[0]
        taps_v, bias_v = _vreg_rows(wv_ref, FFN_CONV), _vreg_rows(bv_ref, 1)[0]
        dbg = jnp.zeros((SUBLANES, LANES), F32)
        dbv = jnp.zeros((SUBLANES, LANES), F32)
        for r0 in range(0, s, ROW_TILE):
            rows = min(ROW_TILE, s - r0)
            ug = _conv_fwd_tile(gpad_ref, taps_g, bias_g, r0, rows)
            uv = _conv_fwd_tile(vpad_ref, taps_v, bias_v, r0, rows)
            d_ug, d_uv = _gelu_mul_grads(ug, uv, _stack(dh_ref, r0, rows))
            dgpad_ref[r0:r0 + rows, :] = d_ug.reshape(rows, LANES)
            dvpad_ref[r0:r0 + rows, :] = d_uv.reshape(rows, LANES)
            dbg = dbg + jnp.sum(d_ug, axis=0)
            dbv = dbv + jnp.sum(d_uv, axis=0)
        dbg_ref[...] = _column_total(dbg)
        dbv_ref[...] = _column_total(dbv)
        for pad_ref, dpad_ref, taps_w, dx_ref, dw_ref in ((gpad_ref, dgpad_ref, taps_g, dug_ref, dwg_ref),
                                                          (vpad_ref, dvpad_ref, taps_v, duv_ref, dwv_ref)):
            dws = [jnp.zeros((SUBLANES, LANES), F32) for _ in range(FFN_CONV)]
            for r0 in range(0, s, ROW_TILE):
                _conv_grads_tile(pad_ref, dpad_ref, dx_ref, taps_w, dws, r0, min(ROW_TILE, s - r0))
            for j in range(FFN_CONV):
                dw_ref[j:j + 1, :] = _column_total(dws[j])

    col = lambda off: (lambda j: (0, off + j))
    seq = pl.BlockSpec((s, LANES), col(0))
    return pl.pallas_call(
        body, name="ffn_conv_bwd", grid=(FFN_BLOCKS,),
        in_specs=[pl.BlockSpec((s, LANES), col(0)), pl.BlockSpec((s, LANES), col(FFN_BLOCKS)),
                  pl.BlockSpec((FFN_CONV, LANES), col(0)), pl.BlockSpec((FFN_CONV, LANES), col(FFN_BLOCKS)),
                  pl.BlockSpec((1, LANES), col(0)), pl.BlockSpec((1, LANES), col(FFN_BLOCKS)), seq],
        out_specs=[seq, seq, pl.BlockSpec((FFN_CONV, LANES), col(0)), pl.BlockSpec((FFN_CONV, LANES), col(0)),
                   pl.BlockSpec((1, LANES), col(0)), pl.BlockSpec((1, LANES), col(0))],
        out_shape=[jax.ShapeDtypeStruct((s, D_FF), BF16), jax.ShapeDtypeStruct((s, D_FF), BF16),
                   jax.ShapeDtypeStruct((FFN_CONV, D_FF), F32), jax.ShapeDtypeStruct((FFN_CONV, D_FF), F32),
                   jax.ShapeDtypeStruct((1, D_FF), F32), jax.ShapeDtypeStruct((1, D_FF), F32)],
        scratch_shapes=[pltpu.VMEM((s + SUBLANES, LANES), F32) for _ in range(4)],
        compiler_params=_params(("parallel",)),
    )(u, u, conv_w, conv_w, conv_b, conv_b, d_h)


def _chunk_masks(c):
    row = lax.broadcasted_iota(jnp.int32, (c, c), 0)
    col = lax.broadcasted_iota(jnp.int32, (c, c), 1)
    return row, col


@jax.custom_vjp
def _split_heads(x):
    return tuple(x[:, h * D_HEAD:(h + 1) * D_HEAD] for h in range(N_HEADS))


_split_heads.defvjp(lambda x: (_split_heads(x), None), lambda _, gs: (jnp.concatenate(gs, axis=1),))


@jax.custom_vjp
def _merge_heads(xs):
    return jnp.concatenate(xs, axis=1)


_merge_heads.defvjp(lambda xs: (_merge_heads(xs), None), lambda _, g: (_split_heads(g),))


@jax.custom_vjp
def _split_chunks(x):
    return tuple(x[i * CHUNK:(i + 1) * CHUNK] for i in range(x.shape[0] // CHUNK))


_split_chunks.defvjp(lambda x: (_split_chunks(x), None), lambda _, gs: (jnp.concatenate(gs, axis=0),))


@jax.custom_vjp
def _merge_chunks(xs):
    return jnp.concatenate(xs, axis=0)


_merge_chunks.defvjp(lambda xs: (_merge_chunks(xs), None), lambda _, g: (_split_chunks(g),))


def _blocks(x):
    return [_split_heads(rows) for rows in _split_chunks(x)]


def _per_chunk_rows(per_chunk, rid):
    out = per_chunk[0]
    for i in range(1, len(per_chunk)):
        out = jnp.where(rid >= i * CHUNK, per_chunk[i], out)
    return out


HEADS = range(N_HEADS)
CHUNKS_PER_STEP = 8
ML_CHUNKS_PER_STEP = 1


def _hg_chunk(hq, hf, hi, hgate, l0, l1, nw, sts):
    n = hq.shape[0] // CHUNK
    causal = _chunk_masks(CHUNK)
    causal = causal[1] <= causal[0]
    mx = lax.stop_gradient(jnp.maximum(l0, l1))
    e0 = jnp.exp(l0 - mx)
    e1 = jnp.exp(l1 - mx)
    lb = e0 / (e0 + e1)
    sig = jax.nn.sigmoid(hf)
    lf = jnp.log(lb + (1.0 - lb) * sig)
    k = (1.0 - lb) * jax.nn.sigmoid(-hf)
    q = jax.nn.silu(hq)
    tri = causal.astype(F32)
    b = _merge_chunks(tuple(_dg(tri, rows, 1, 0, HIGHEST) for rows in _split_chunks(lf)))
    rid = lax.broadcasted_iota(jnp.int32, b.shape, 0)
    pick = lambda r: jnp.sum(jnp.where(rid == r, b, 0.0), axis=0, keepdims=True)
    b_last_c = [pick(i * CHUNK + CHUNK - 1) for i in range(n)]
    b_ref = _per_chunk_rows([pick(i * CHUNK + CHUNK // 2 - 1) for i in range(n)], rid)
    b_last = _per_chunk_rows(b_last_c, rid)
    qa = _blocks(q * jnp.exp(b - b_ref))
    ka = _blocks(k * jnp.exp(b_ref - b))
    qe = _blocks(q * jnp.exp(b))
    kd = _blocks(k * jnp.exp(b_last - b))
    decay = [_split_heads(jnp.exp(b_last_c[i])) for i in range(n)]
    v = _blocks(hi)
    chunks = range(n)
    attn = [[jnp.where(causal, _nt(qa[i][h], ka[i][h]), 0.0) for h in HEADS] for i in chunks]
    intra = [[_nn(attn[i][h], v[i][h]) for h in HEADS] for i in chunks]
    kv = [[_tn(v[i][h], kd[i][h]) for h in HEADS] for i in chunks]
    normed = []
    for i in chunks:
        inter = [_nt(qe[i][h], sts[h]) for h in HEADS]
        sts = tuple(decay[i][h] * sts[h] + kv[i][h] for h in HEADS)
        o = [intra[i][h] + inter[h] for h in HEADS]
        normed.append(_merge_heads(tuple(o[h] * lax.rsqrt(jnp.mean(o[h] * o[h], axis=-1, keepdims=True) + LN_EPS)
                                         for h in HEADS)))
    return _merge_chunks(tuple(normed)) * nw * jax.nn.silu(hgate), sts


def _seg(ref, seg):
    return ref[:, seg * D_GROUP:(seg + 1) * D_GROUP]


def _hgrn2_fwd(proj, logits, norm_w):
    s = proj.shape[0]
    rows = CHUNKS_PER_STEP * CHUNK
    nc = s // rows

    def body(p_ref, lg_ref, nw_ref, y_ref, st_out_ref, st_scr):
        @pl.when(pl.program_id(0) == 0)
        def _():
            st_scr[...] = jnp.zeros_like(st_scr)

        sts = tuple(st_scr[h] for h in HEADS)
        y, sts_new = _hg_chunk(_seg(p_ref, 0), _seg(p_ref, 1), _seg(p_ref, 2), _seg(p_ref, 3),
                               lg_ref[0:1, :], lg_ref[1:2, :], nw_ref[...], sts)
        y_ref[...] = y.astype(y_ref.dtype)
        for h in HEADS:
            st_out_ref[h] = sts[h]
            st_scr[h] = sts_new[h]

    return pl.pallas_call(
        body, name="hgrn2_fwd", grid=(nc,),
        in_specs=[pl.BlockSpec((rows, 4 * D_GROUP), lambda c: (c, 0)),
                  pl.BlockSpec((2, D_GROUP), lambda c: (0, 0)),
                  pl.BlockSpec((1, D_GROUP), lambda c: (0, 0))],
        out_specs=[pl.BlockSpec((rows, D_GROUP), lambda c: (c, 0)),
                   pl.BlockSpec((None, N_HEADS, D_HEAD, D_HEAD), lambda c: (c, 0, 0, 0))],
        out_shape=[jax.ShapeDtypeStruct((s, 2 * D_GROUP), BF16),
                   jax.ShapeDtypeStruct((nc, N_HEADS, D_HEAD, D_HEAD), F32)],
        scratch_shapes=[pltpu.VMEM((N_HEADS, D_HEAD, D_HEAD), F32)],
        compiler_params=_params(("arbitrary",)),
    )(proj, logits, norm_w)


def _hgrn2_bwd(proj, logits, norm_w, states, d_y):
    s = proj.shape[0]
    rows = CHUNKS_PER_STEP * CHUNK
    nc = s // rows

    def body(p_ref, lg_ref, nw_ref, st_ref, dy_ref, dp_ref, dl_ref, dnw_ref, dsum_ref, dst_scr):
        @pl.when(pl.program_id(0) == 0)
        def _():
            dst_scr[...] = jnp.zeros_like(dst_scr)
            dl_ref[...] = jnp.zeros_like(dl_ref)
            dnw_ref[...] = jnp.zeros_like(dnw_ref)
            dsum_ref[...] = jnp.zeros_like(dsum_ref)

        _, vjp = jax.vjp(_hg_chunk, _seg(p_ref, 0), _seg(p_ref, 1), _seg(p_ref, 2), _seg(p_ref, 3),
                         lg_ref[0:1, :], lg_ref[1:2, :], nw_ref[...], tuple(st_ref[h] for h in HEADS))
        d_hq, d_hf, d_hi, d_hg, d_l0, d_l1, d_nw, d_sts = vjp((dy_ref[...], tuple(dst_scr[h] for h in HEADS)))
        for seg, val in enumerate((d_hq, d_hf, d_hi, d_hg)):
            dp_ref[:, seg * D_GROUP:(seg + 1) * D_GROUP] = val.astype(dp_ref.dtype)
            dsum_ref[:, seg * D_GROUP:(seg + 1) * D_GROUP] += jnp.sum(val, axis=0, keepdims=True)
        dl_ref[0:1, :] += d_l0
        dl_ref[1:2, :] += d_l1
        dnw_ref[...] += d_nw
        for h in HEADS:
            dst_scr[h] = d_sts[h]

    rev = lambda c: nc - 1 - c
    return pl.pallas_call(
        body, name="hgrn2_bwd", grid=(nc,),
        in_specs=[pl.BlockSpec((rows, 4 * D_GROUP), lambda c: (rev(c), 0)),
                  pl.BlockSpec((2, D_GROUP), lambda c: (0, 0)),
                  pl.BlockSpec((1, D_GROUP), lambda c: (0, 0)),
                  pl.BlockSpec((None, N_HEADS, D_HEAD, D_HEAD), lambda c: (rev(c), 0, 0, 0)),
                  pl.BlockSpec((rows, D_GROUP), lambda c: (rev(c), 0))],
        out_specs=[pl.BlockSpec((rows, 4 * D_GROUP), lambda c: (rev(c), 0)),
                   pl.BlockSpec((2, D_GROUP), lambda c: (0, 0)),
                   pl.BlockSpec((1, D_GROUP), lambda c: (0, 0)),
                   pl.BlockSpec((1, 4 * D_GROUP), lambda c: (0, 0))],
        out_shape=[jax.ShapeDtypeStruct((s, D_IN_MAIN), BF16), jax.ShapeDtypeStruct((2, D_GROUP), F32),
                   jax.ShapeDtypeStruct((1, D_GROUP), F32), jax.ShapeDtypeStruct((1, 4 * D_GROUP), F32)],
        scratch_shapes=[pltpu.VMEM((N_HEADS, D_HEAD, D_HEAD), F32)],
        compiler_params=_params(("arbitrary",)),
    )(proj, logits, norm_w, states, d_y)


def _gate_column(gates, lane, idx):
    return jnp.sum(jnp.where(lane == idx, gates, 0.0), axis=1, keepdims=True)


def _head_layer_norm(h):
    mu = jnp.mean(h, axis=-1, keepdims=True)
    var = jnp.mean(jnp.square(h - mu), axis=-1, keepdims=True)
    return (h - mu) * lax.rsqrt(var + LN_EPS)


def _ml_chunk(qc, kc, v, mo, gates, nw, cts, ns, ms):
    n = qc.shape[0] // CHUNK
    row, col = _chunk_masks(CHUNK)
    mask = col <= row
    eye = col == row
    to_row = lambda t: jnp.sum(jnp.where(eye, t, 0.0), axis=0, keepdims=True)
    q = _blocks(qc * (D_HEAD ** -0.5))
    k = _blocks(kc)
    vs = _blocks(v)
    gate_rows = _split_chunks(gates)
    lane = lax.broadcasted_iota(jnp.int32, gate_rows[0].shape, 1)
    each = [(i, h) for i in range(n) for h in HEADS]
    on_each = lambda f: {ih: f(*ih) for ih in each}
    ig = on_each(lambda i, h: _gate_column(gate_rows[i], lane, h))
    lf = on_each(lambda i, h: jax.nn.log_sigmoid(_gate_column(gate_rows[i], lane, N_HEADS + h)))
    lf_row = on_each(lambda i, h: to_row(lf[i, h]))
    ig_row = on_each(lambda i, h: to_row(ig[i, h]))
    b_col = on_each(lambda i, h: jnp.sum(jnp.where(mask, lf_row[i, h], 0.0), axis=1, keepdims=True))
    b_row = on_each(lambda i, h: jnp.sum(jnp.where(row <= col, lf[i, h], 0.0), axis=0, keepdims=True))
    g = on_each(lambda i, h: jnp.sum(lf[i, h], axis=0, keepdims=True))
    d = on_each(lambda i, h: jnp.where(mask, b_col[i, h] - b_row[i, h] + ig_row[i, h], -jnp.inf))
    a = on_each(lambda i, h: g[i, h] - b_col[i, h] + ig[i, h])
    m_at = {(0, h): ms[h] for h in HEADS}
    for i, h in each:
        m_at[i + 1, h] = lax.stop_gradient(jnp.maximum(g[i, h] + m_at[i, h], jnp.max(a[i, h], axis=0, keepdims=True)))
    inter = on_each(lambda i, h: b_col[i, h] + m_at[i, h])
    m_t = on_each(lambda i, h: lax.stop_gradient(jnp.maximum(inter[i, h], jnp.max(d[i, h], axis=1, keepdims=True))))
    qk = on_each(lambda i, h: _nt(q[i][h], k[i][h]))
    sc = on_each(lambda i, h: qk[i, h] * jnp.exp(d[i, h] - m_t[i, h]))
    w_inter = on_each(lambda i, h: jnp.exp(inter[i, h] - m_t[i, h]))
    sv = on_each(lambda i, h: _nn(sc[i, h], vs[i][h]))
    decay = on_each(lambda i, h: jnp.exp(g[i, h] + m_at[i, h] - m_at[i + 1, h]))
    wk = on_each(lambda i, h: k[i][h] * jnp.exp(a[i, h] - m_at[i + 1, h]))
    kv = on_each(lambda i, h: _tn(vs[i][h], wk[i, h]))
    normed = []
    for i in range(n):
        qc_state = [_nt(q[i][h], cts[h]) for h in HEADS]
        num = [sv[i, h] + w_inter[i, h] * qc_state[h] for h in HEADS]
        den = [jnp.sum(sc[i, h], axis=1, keepdims=True)
               + w_inter[i, h] * jnp.sum(q[i][h] * ns[h], axis=1, keepdims=True) for h in HEADS]
        hh = [num[h] / jnp.maximum(jnp.abs(den[h]), jnp.exp(-m_t[i, h])) for h in HEADS]
        cts = tuple(decay[i, h] * cts[h] + kv[i, h] for h in HEADS)
        ns = tuple(decay[i, h] * ns[h] + jnp.sum(wk[i, h], axis=0, keepdims=True) for h in HEADS)
        normed.append(_merge_heads(tuple(_head_layer_norm(hh[h]) for h in HEADS)))
    y = jax.nn.sigmoid(mo) * (_merge_chunks(tuple(normed)) * nw)
    return y, cts, ns, tuple(m_at[n, h] for h in HEADS)


def _mlstm_fwd(qk, proj, gates, norm_w, y):
    s = proj.shape[0]
    rows = ML_CHUNKS_PER_STEP * CHUNK
    nc = s // rows

    def body(qk_ref, vo_ref, g_ref, nw_ref, _, y_ref, ct_out, n_out, m_out, ct_scr, n_scr, m_scr):
        @pl.when(pl.program_id(0) == 0)
        def _():
            ct_scr[...] = jnp.zeros_like(ct_scr)
            n_scr[...] = jnp.zeros_like(n_scr)
            m_scr[...] = jnp.full(m_scr.shape, NEG_BIG, F32)

        cts = tuple(ct_scr[h] for h in HEADS)
        ns = tuple(n_scr[h] for h in HEADS)
        ms = tuple(m_scr[h] for h in HEADS)
        y, cts_new, ns_new, ms_new = _ml_chunk(_seg(qk_ref, 0), _seg(qk_ref, 1), _seg(vo_ref, 0), _seg(vo_ref, 1),
                                               g_ref[...], nw_ref[...], cts, ns, ms)
        y_ref[...] = y.astype(y_ref.dtype)
        for h in HEADS:
            ct_out[h], n_out[h], m_out[h] = cts[h], ns[h], ms[h]
            ct_scr[h], n_scr[h], m_scr[h] = cts_new[h], ns_new[h], ms_new[h]

    st = lambda r, w: pl.BlockSpec((None, N_HEADS, r, w), lambda c: (c, 0, 0, 0))
    return pl.pallas_call(
        body, name="mlstm_fwd", grid=(nc,),
        in_specs=[pl.BlockSpec((rows, 2 * D_GROUP), lambda c: (c, 0)),
                  pl.BlockSpec((rows, 2 * D_GROUP), lambda c: (c, VO_BLOCK)),
                  pl.BlockSpec((rows, LANES), lambda c: (c, 0)),
                  pl.BlockSpec((1, D_GROUP), lambda c: (0, 0)),
                  pl.BlockSpec(memory_space=pl.ANY)],
        out_specs=[pl.BlockSpec((rows, D_GROUP), lambda c: (c, 1)),
                   st(D_HEAD, D_HEAD), st(1, D_HEAD), st(1, 1)],
        out_shape=[jax.ShapeDtypeStruct(y.shape, y.dtype),
                   jax.ShapeDtypeStruct((nc, N_HEADS, D_HEAD, D_HEAD), F32),
                   jax.ShapeDtypeStruct((nc, N_HEADS, 1, D_HEAD), F32),
                   jax.ShapeDtypeStruct((nc, N_HEADS, 1, 1), F32)],
        input_output_aliases={4: 0},
        scratch_shapes=[pltpu.VMEM((N_HEADS, D_HEAD, D_HEAD), F32), pltpu.VMEM((N_HEADS, 1, D_HEAD), F32),
                        pltpu.VMEM((N_HEADS, 1, 1), F32)],
        compiler_params=_params(("arbitrary",)),
    )(qk, proj, gates, norm_w, y)


def _mlstm_bwd(qk, proj, gates, norm_w, ct_s, n_s, m_s, d_y, d_proj):
    s = proj.shape[0]
    rows = ML_CHUNKS_PER_STEP * CHUNK
    nc = s // rows

    def body(qk_ref, vo_ref, g_ref, nw_ref, ct_ref, n_ref, m_ref, dy_ref, _,
             dp_ref, dqk_ref, dg_ref, dnw_ref, dsum_ref, dct_scr, dn_scr):
        @pl.when(pl.program_id(0) == 0)
        def _():
            dct_scr[...] = jnp.zeros_like(dct_scr)
            dn_scr[...] = jnp.zeros_like(dn_scr)
            dnw_ref[...] = jnp.zeros_like(dnw_ref)
            dsum_ref[...] = jnp.zeros_like(dsum_ref)

        ms = tuple(m_ref[h] for h in HEADS)
        step = lambda *a: _ml_chunk(*a, ms)[:3]
        _, vjp = jax.vjp(step, _seg(qk_ref, 0), _seg(qk_ref, 1), _seg(vo_ref, 0), _seg(vo_ref, 1), g_ref[...],
                         nw_ref[...], tuple(ct_ref[h] for h in HEADS), tuple(n_ref[h] for h in HEADS))
        d_q, d_k, d_v, d_o, d_gates, d_nw, d_cts, d_ns = vjp(
            (dy_ref[...], tuple(dct_scr[h] for h in HEADS), tuple(dn_scr[h] for h in HEADS)))
        dqk_ref[:, 0:D_GROUP] = d_q
        dqk_ref[:, D_GROUP:2 * D_GROUP] = d_k
        for seg, val in enumerate((d_v, d_o)):
            dp_ref[:, seg * D_GROUP:(seg + 1) * D_GROUP] = val.astype(dp_ref.dtype)
            dsum_ref[:, seg * D_GROUP:(seg + 1) * D_GROUP] += jnp.sum(val, axis=0, keepdims=True)
        dg_ref[...] = d_gates
        dnw_ref[...] += d_nw
        for h in HEADS:
            dct_scr[h] = d_cts[h]
            dn_scr[h] = d_ns[h]

    rev = lambda c: nc - 1 - c
    st = lambda r, w: pl.BlockSpec((None, N_HEADS, r, w), lambda c: (rev(c), 0, 0, 0))
    return pl.pallas_call(
        body, name="mlstm_bwd", grid=(nc,),
        in_specs=[pl.BlockSpec((rows, 2 * D_GROUP), lambda c: (rev(c), 0)),
                  pl.BlockSpec((rows, 2 * D_GROUP), lambda c: (rev(c), VO_BLOCK)),
                  pl.BlockSpec((rows, LANES), lambda c: (rev(c), 0)),
                  pl.BlockSpec((1, D_GROUP), lambda c: (0, 0)),
                  st(D_HEAD, D_HEAD), st(1, D_HEAD), st(1, 1),
                  pl.BlockSpec((rows, D_GROUP), lambda c: (rev(c), 1)),
                  pl.BlockSpec(memory_space=pl.ANY)],
        out_specs=[pl.BlockSpec((rows, 2 * D_GROUP), lambda c: (rev(c), VO_BLOCK)),
                   pl.BlockSpec((rows, 2 * D_GROUP), lambda c: (rev(c), 0)),
                   pl.BlockSpec((rows, LANES), lambda c: (rev(c), 0)),
                   pl.BlockSpec((1, D_GROUP), lambda c: (0, 0)),
                   pl.BlockSpec((1, 2 * D_GROUP), lambda c: (0, 0))],
        out_shape=[jax.ShapeDtypeStruct(d_proj.shape, d_proj.dtype), jax.ShapeDtypeStruct((s, 2 * D_GROUP), F32),
                   jax.ShapeDtypeStruct((s, LANES), F32), jax.ShapeDtypeStruct((1, D_GROUP), F32),
                   jax.ShapeDtypeStruct((1, 2 * D_GROUP), F32)],
        input_output_aliases={8: 0},
        scratch_shapes=[pltpu.VMEM((N_HEADS, D_HEAD, D_HEAD), F32), pltpu.VMEM((N_HEADS, 1, D_HEAD), F32)],
        compiler_params=_params(("arbitrary",)),
    )(qk, proj, gates, norm_w, ct_s, n_s, m_s, d_y, d_proj)


LN_TOKENS = 512
ATT_TOKENS = 512


def _proj_res_ln(a, w, xres, g, b, name):
    s, dm = xres.shape
    k = a.shape[1]
    tb = min(LN_TOKENS, s)

    def body(a_ref, w_ref, x_ref, g_ref, b_ref, z_ref, o_ref):
        halves = [slice(0, tb // 2), slice(tb // 2, tb)]
        zs = [ALPHA * x_ref[rows, :] + _nn_raw(a_ref[rows, :], w_ref[...]) for rows in halves]
        for rows, z in zip(halves, zs):
            z_ref[rows, :] = z
            o_ref[rows, :] = _layer_norm(z, g_ref[...], b_ref[...])

    tok = pl.BlockSpec((tb, dm), lambda i: (i, 0))
    vec = pl.BlockSpec((1, dm), lambda i: (0, 0))
    act = jax.ShapeDtypeStruct((s, dm), F32)
    return pl.pallas_call(
        body, name=name, grid=(s // tb,),
        in_specs=[pl.BlockSpec((tb, k), lambda i: (i, 0)), pl.BlockSpec((k, dm), lambda i: (0, 0)), tok, vec, vec],
        out_specs=[tok, tok], out_shape=[act, act], compiler_params=_params(("parallel",)),
    )(a, w, xres, g, b)


def _ln_bwd_proj(d_out, z, g, b, w, name):
    s, dm = z.shape
    k = w.shape[0]
    tb = min(LN_TOKENS, s)

    def body(do_ref, z_ref, g_ref, b_ref, w_ref, dz_ref, da_ref, dg_ref, db_ref):
        @pl.when(pl.program_id(0) == 0)
        def _():
            dg_ref[...] = jnp.zeros_like(dg_ref)
            db_ref[...] = jnp.zeros_like(db_ref)

        halves = [slice(0, tb // 2), slice(tb // 2, tb)]
        d_zs = []
        for rows in halves:
            _, vjp = jax.vjp(_layer_norm, z_ref[rows, :], g_ref[...], b_ref[...])
            d_z, d_g, d_b = vjp(do_ref[rows, :])
            dz_ref[rows, :] = d_z
            dg_ref[...] += d_g
            db_ref[...] += d_b
            d_zs.append(d_z)
        for rows, d_z in zip(halves, d_zs):
            da_ref[rows, :] = _nt_raw(d_z, w_ref[...])

    tok = pl.BlockSpec((tb, dm), lambda i: (i, 0))
    vec = pl.BlockSpec((1, dm), lambda i: (0, 0))
    return pl.pallas_call(
        body, name=name, grid=(s // tb,),
        in_specs=[tok, tok, vec, vec, pl.BlockSpec((k, dm), lambda i: (0, 0))],
        out_specs=[tok, pl.BlockSpec((tb, k), lambda i: (i, 0)), vec, vec],
        out_shape=[jax.ShapeDtypeStruct((s, dm), F32), jax.ShapeDtypeStruct((s, k), F32),
                   jax.ShapeDtypeStruct((1, dm), F32), jax.ShapeDtypeStruct((1, dm), F32)],
        compiler_params=_params(("arbitrary",)),
    )(d_out, z, g, b, w)


def _proj_loss_tail(a, w, xres, g, b, target):
    s, dm = xres.shape
    k = a.shape[1]
    tb = min(ATT_TOKENS, s)

    def loss_fn(z, gg, bb, tgt):
        err = jnp.square(_layer_norm(z, gg, bb) - tgt)
        return 0.5 * jnp.sum(jnp.mean(err, axis=-1, keepdims=True), axis=0, keepdims=True)

    def body(a_ref, w_ref, x_ref, g_ref, b_ref, t_ref, loss_ref, dz_ref, dg_ref, db_ref):
        @pl.when(pl.program_id(0) == 0)
        def _():
            loss_ref[...] = jnp.zeros_like(loss_ref)
            dg_ref[...] = jnp.zeros_like(dg_ref)
            db_ref[...] = jnp.zeros_like(db_ref)

        halves = [slice(0, tb // 2), slice(tb // 2, tb)]
        zs = [ALPHA * x_ref[rows, :] + _nn_raw(a_ref[rows, :], w_ref[...]) for rows in halves]
        for rows, z in zip(halves, zs):
            tgt = t_ref[rows, :]
            loss, vjp = jax.vjp(lambda zz, gg, bb, tgt=tgt: loss_fn(zz, gg, bb, tgt), z, g_ref[...], b_ref[...])
            d_z, d_g, d_b = vjp(jnp.ones((1, 1), F32))
            loss_ref[...] += loss
            dz_ref[rows, :] = d_z
            dg_ref[...] += d_g
            db_ref[...] += d_b

    tok = pl.BlockSpec((tb, dm), lambda i: (i, 0))
    vec = pl.BlockSpec((1, dm), lambda i: (0, 0))
    one = pl.BlockSpec((1, 1), lambda i: (0, 0))
    return pl.pallas_call(
        body, name="ffn_down_loss_tail", grid=(s // tb,),
        in_specs=[pl.BlockSpec((tb, k), lambda i: (i, 0)), pl.BlockSpec((k, dm), lambda i: (0, 0)), tok, vec, vec, tok],
        out_specs=[one, tok, vec, vec],
        out_shape=[jax.ShapeDtypeStruct((1, 1), F32), jax.ShapeDtypeStruct((s, dm), F32),
                   jax.ShapeDtypeStruct((1, dm), F32), jax.ShapeDtypeStruct((1, dm), F32)],
        compiler_params=_params(("arbitrary",)),
    )(a, w, xres, g, b, target)


def _att_heads(qs, ks, vs):
    sc = [_nt(q, k) * (CA_DH ** -0.5) for q, k in zip(qs, ks)]
    p = [jax.nn.softmax(s, axis=-1) for s in sc]
    return tuple(_nn(pp, v) for pp, v in zip(p, vs))


def _head_slices(ref_or_value, offset):
    return tuple(ref_or_value[:, offset + h * CA_DH:offset + (h + 1) * CA_DH] for h in range(CA_HEADS))


def _cross_attention_fwd(x1, kv, wq, wo, g, b):
    s = x1.shape[0]
    tb = min(ATT_TOKENS, s)

    def body(x_ref, kv_ref, wq_ref, wo_ref, g_ref, b_ref, att_ref, z_ref, o_ref):
        x_blk = x_ref[...]
        q = _nn_raw(x_blk, wq_ref[...])
        att = jnp.concatenate(_att_heads(_head_slices(q, 0), _head_slices(kv_ref, 0), _head_slices(kv_ref, D_MODEL)),
                              axis=1)
        att_ref[...] = att.astype(att_ref.dtype)
        z = ALPHA * x_blk + _nn_raw(att, wo_ref[...])
        z_ref[...] = z
        o_ref[...] = _layer_norm(z, g_ref[...], b_ref[...])

    tok = pl.BlockSpec((tb, D_MODEL), lambda i: (i, 0))
    mat = pl.BlockSpec((D_MODEL, D_MODEL), lambda i: (0, 0))
    vec = pl.BlockSpec((1, D_MODEL), lambda i: (0, 0))
    act = jax.ShapeDtypeStruct((s, D_MODEL), F32)
    return pl.pallas_call(
        body, name="cross_attention_fwd", grid=(s // tb,),
        in_specs=[tok, pl.BlockSpec((N_MEM, 2 * D_MODEL), lambda i: (0, 0)), mat, mat, vec, vec],
        out_specs=[tok, tok, tok],
        out_shape=[jax.ShapeDtypeStruct((s, D_MODEL), BF16), act, act],
        compiler_params=_params(("parallel",)),
    )(x1, kv, wq, wo, g, b)


def _cross_attention_bwd(d_x2, x1, z2, kv, wq, wo, g, b):
    s = x1.shape[0]
    tb = min(ATT_TOKENS, s)

    def body(dx2_ref, x_ref, z_ref, kv_ref, wq_ref, wo_ref, g_ref, b_ref,
             dx1_ref, dq_ref, dz_ref, dkv_ref, dg_ref, db_ref):
        @pl.when(pl.program_id(0) == 0)
        def _():
            dkv_ref[...] = jnp.zeros_like(dkv_ref)
            dg_ref[...] = jnp.zeros_like(dg_ref)
            db_ref[...] = jnp.zeros_like(db_ref)

        q = _nn_raw(x_ref[...], wq_ref[...])
        _, ln_vjp = jax.vjp(_layer_norm, z_ref[...], g_ref[...], b_ref[...])
        d_z, d_g, d_b = ln_vjp(dx2_ref[...])
        dg_ref[...] += d_g
        db_ref[...] += d_b
        dz_ref[...] = d_z.astype(dz_ref.dtype)
        d_att = _nt_raw(d_z, wo_ref[...])
        _, vjp = jax.vjp(_att_heads, _head_slices(q, 0), _head_slices(kv_ref, 0), _head_slices(kv_ref, D_MODEL))
        d_qs, d_ks, d_vs = vjp(_head_slices(d_att, 0))
        for h in range(CA_HEADS):
            lo = h * CA_DH
            dkv_ref[:, lo:lo + CA_DH] += d_ks[h]
            dkv_ref[:, D_MODEL + lo:D_MODEL + lo + CA_DH] += d_vs[h]
        d_q = jnp.concatenate(d_qs, axis=1)
        dq_ref[...] = d_q.astype(dq_ref.dtype)
        dx1_ref[...] = ALPHA * d_z + _nt_raw(d_q, wq_ref[...])

    tok = pl.BlockSpec((tb, D_MODEL), lambda i: (i, 0))
    mem = pl.BlockSpec((N_MEM, 2 * D_MODEL), lambda i: (0, 0))
    mat = pl.BlockSpec((D_MODEL, D_MODEL), lambda i: (0, 0))
    vec = pl.BlockSpec((1, D_MODEL), lambda i: (0, 0))
    low = jax.ShapeDtypeStruct((s, D_MODEL), BF16)
    return pl.pallas_call(
        body, name="cross_attention_bwd", grid=(s // tb,),
        in_specs=[tok, tok, tok, mem, mat, mat, vec, vec], out_specs=[tok, tok, tok, mem, vec, vec],
        out_shape=[jax.ShapeDtypeStruct((s, D_MODEL), F32), low, low,
                   jax.ShapeDtypeStruct((N_MEM, 2 * D_MODEL), F32),
                   jax.ShapeDtypeStruct((1, D_MODEL), F32), jax.ShapeDtypeStruct((1, D_MODEL), F32)],
        compiler_params=_params(("arbitrary",)),
    )(d_x2, x1, z2, kv, wq, wo, g, b)


def _local_step(x, mem, target, w, mid_weights=None, ffn_weights=None, down_weights=None, on_ffn_grads=None,
                on_mid_grads=None,
                on_small_grads=None, on_last_grads=None):
    w = dict(w)
    s = x.shape[0]
    tm = min(512, s)
    tt_big = min(1024, s)
    proj, gates = _input_projection(x, w["w_in_t"], w["w_in_gate_t"], w["b_in_main"], w["b_in_gate"],
                                    min(2048, s), 512)
    qk = _ml_conv_fwd(proj, w["ml_conv_w"], w["ml_conv_b"])
    y, hg_states = _hgrn2_fwd(proj, w["hg_lb_logits"], w["hg_norm_w"])
    y, ct_s, n_s, m_s = _mlstm_fwd(qk, proj, gates, w["ml_norm_w"], y)
    if mid_weights is not None:
        w.update(mid_weights(y))
    z1, x1 = _proj_res_ln(y, w["w_out"], x, w["ln1_g"], w["ln1_b"], "out_proj_ln1")
    kv = _matmul_nn(mem, w["ca_wkv"], None, N_MEM, CA_DH, "kv")
    att, z2, x2 = _cross_attention_fwd(x1, kv, w["ca_wq"], w["ca_wo"], w["ln2_g"], w["ln2_b"])
    if ffn_weights is not None:
        w.update(ffn_weights(x2))
    u = _matmul_nt(x2, w["ffn_w_up_t"], min(1024, s), UP_TILE, "ffn_up", out_dtype=BF16)
    hid = _ffn_conv_fwd(u, w["ffn_conv_w"], w["ffn_conv_b"])
    if down_weights is not None:
        w.update(down_weights(hid))
    loss, d_z3, d_ln3_g, d_ln3_b = _proj_loss_tail(hid, w["ffn_w_down"], x2, w["ln3_g"], w["ln3_b"], target)
    grads = {"ln3_g": d_ln3_g, "ln3_b": d_ln3_b}
    grads["ffn_w_down"] = _matmul_tn(hid, d_z3, UP_TILE, D_MODEL, tt_big, "d_w_down")
    d_hid = _matmul_nt(d_z3, w["ffn_w_down"], tm, D_FF, "d_hid", out_dtype=BF16)
    d_ug, d_uv, d_cwg, d_cwv, d_cbg, d_cbv = _ffn_conv_bwd(u, w["ffn_conv_w"], w["ffn_conv_b"], d_hid)
    grads["ffn_conv_w"] = jnp.concatenate([d_cwg, d_cwv], axis=-1)
    grads["ffn_conv_b"] = jnp.concatenate([d_cbg, d_cbv], axis=-1)
    d_w_up = _matmul_tn(d_ug, x2, UP_TILE, D_MODEL, tt_big, "d_w_up_gate", rows=D_UP)
    grads["ffn_w_up"] = _matmul_tn(d_uv, x2, UP_TILE, D_MODEL, tt_big, "d_w_up_val", rows=D_UP, row0=D_FF,
                                   into=d_w_up)
    d_x2 = _matmul_nn_sum([(d_ug, w["ffn_w_up_t"], 0), (d_uv, w["ffn_w_up_t"], D_FF)], d_z3, ALPHA,
                          min(256, s), "d_x2")
    if on_ffn_grads is not None:
        d_x2 = on_ffn_grads(grads, d_x2)
    d_x1, d_q, d_z2, d_kv, grads["ln2_g"], grads["ln2_b"] = _cross_attention_bwd(
        d_x2, x1, z2, kv, w["ca_wq"], w["ca_wo"], w["ln2_g"], w["ln2_b"])
    grads["ca_wo"] = _matmul_tn(att, d_z2, D_MODEL, D_MODEL, tt_big, "d_ca_wo")
    grads["ca_wq"] = _matmul_tn(x1, d_q, D_MODEL, D_MODEL, tt_big, "d_ca_wq")
    grads["ca_wkv"] = _matmul_tn(mem, d_kv, D_MODEL, CA_DH, N_MEM, "d_ca_wkv", shards=N_DEV, group=N_DEV)
    d_z1, d_y, grads["ln1_g"], grads["ln1_b"] = _ln_bwd_proj(d_x1, z1, w["ln1_g"], w["ln1_b"], w["w_out"],
                                                             "ln1_bwd_out_proj")
    grads["w_out"] = _matmul_tn(y, d_z1, D_MODEL, D_MODEL, tt_big, "d_w_out")
    if on_mid_grads is not None:
        d_y = on_mid_grads(grads, d_y)
    d_proj, grads["hg_lb_logits"], grads["hg_norm_w"], db_hg = _hgrn2_bwd(
        proj, w["hg_lb_logits"], w["hg_norm_w"], hg_states, d_y)
    d_proj, d_qk, d_gates, grads["ml_norm_w"], db_vo = _mlstm_bwd(
        qk, proj, gates, w["ml_norm_w"], ct_s, n_s, m_s, d_y, d_proj)
    d_proj, grads["ml_conv_w"], grads["ml_conv_b"], db_qk = _ml_conv_bwd(
        proj, w["ml_conv_w"], w["ml_conv_b"], d_qk, d_proj)
    grads["b_in_main"] = jnp.concatenate([db_hg, db_qk, db_vo], axis=-1)
    grads["b_in_gate"] = jnp.sum(d_gates, axis=0, keepdims=True)
    if on_small_grads is not None:
        d_proj = on_small_grads(grads, loss, d_proj)
    grads["w_in"] = _input_projection_grads(d_proj, d_gates, x, min(1024, D_IN_MAIN), tt_big)
    if on_last_grads is not None:
        d_z1 = on_last_grads(grads, d_z1)
    grad_x = _matmul_nn_sum([(d_proj, w["w_in_t"], 0), (d_gates, w["w_in_gate_t"], 0)], d_z1, ALPHA, tm, "d_x")
    return loss, grad_x, grads


HBM_SPEC = pl.BlockSpec(memory_space=pltpu.HBM)


def _coords():
    return lax.axis_index("x"), lax.axis_index("y"), lax.axis_index("c")


def _other_chips(x, y):
    return [(1 - x, y), (x, 1 - y), (1 - x, 1 - y)]


def _my_slot():
    x, y, c = _coords()
    return 4 * x + 2 * y + c


SEM_SPEC = pl.BlockSpec(memory_space=pltpu.SEMAPHORE)
ANY_SPEC = pl.BlockSpec(memory_space=pl.ANY)
SIDE_EFFECT = pltpu.SideEffectType.DATAFLOW_SIDE_EFFECTING


def _peer(x, y, c, d):
    flip = lambda v, bit: 1 - v if bit else v
    p = (flip(x, d & 4), flip(y, d & 2), flip(c, d & 1))
    return p, 4 * p[0] + 2 * p[1] + p[2]


def _direct_copies(gather, src_refs, land_refs, send_sems, recv_sems):
    x, y, c = _coords()
    me = 4 * x + 2 * y + c
    copies = []
    for a in range(len(src_refs)):
        for d in range(1, N_DEV):
            peer, peer_slot = _peer(x, y, c, d)
            copies.append(pltpu.make_async_remote_copy(
                src_ref=src_refs[a] if gather else src_refs[a].at[peer_slot],
                dst_ref=land_refs[a].at[me] if gather else land_refs[a].at[d - 1],
                send_sem=send_sems.at[7 * a + d - 1], recv_sem=recv_sems.at[7 * a + d - 1],
                device_id=peer, device_id_type=MESH))
    return copies


def _hbm(t):
    return pltpu.HBM(t.shape, t.dtype)


def _chip_copies(src_refs, land_refs, send_sems, recv_sems):
    x, y, c = _coords()
    me = 4 * x + 2 * y + c
    targets = [(x, y, 1 - c)] + [(cx, cy, c) for cx, cy in _other_chips(x, y)]
    return [pltpu.make_async_remote_copy(
        src_ref=src_refs[a], dst_ref=land_refs[a].at[me], send_sem=send_sems.at[4 * a + k],
        recv_sem=recv_sems.at[4 * a + k], device_id=target, device_id_type=MESH)
        for a in range(len(src_refs)) for k, target in enumerate(targets)]


def _forward_copies(land_refs, send_sems, recv_sems):
    x, y, c = _coords()
    return [pltpu.make_async_remote_copy(
        src_ref=land_refs[a].at[4 * cx + 2 * cy + c], dst_ref=land_refs[a].at[4 * cx + 2 * cy + c],
        send_sem=send_sems.at[3 * a + j], recv_sem=recv_sems.at[3 * a + j],
        device_id=(x, y, 1 - c), device_id_type=MESH)
        for a in range(len(land_refs)) for j, (cx, cy) in enumerate(_other_chips(x, y))]


def _split_copy_start(make_copies, n_sems, operands, through, name):
    n_ops = len(operands)

    def body(*refs):
        for cp in make_copies(refs[:n_ops], refs[n_ops + 1], refs[n_ops + 2]):
            cp.start()

    ins = [pltpu.with_memory_space_constraint(t, pltpu.HBM) for t in (*operands, through)]
    sems = pltpu.SemaphoreType.DMA((n_sems,))
    res = pl.pallas_call(
        body, name=name, out_shape=(sems, sems, *[_hbm(t) for t in ins]),
        in_specs=[HBM_SPEC] * (n_ops + 1), out_specs=(SEM_SPEC, SEM_SPEC, *[HBM_SPEC] * (n_ops + 1)),
        input_output_aliases={i: 2 + i for i in range(n_ops + 1)},
        compiler_params=pltpu.CompilerParams(has_side_effects=SIDE_EFFECT),
    )(*ins)
    return (res[0], res[1], list(res[2:2 + n_ops])), res[2 + n_ops]


def _split_copy_wait(make_copies, started, after, name):
    send_sems, recv_sems, operands = started
    n_ops = len(operands)
    after = list(after) if isinstance(after, (list, tuple)) else [after]

    def body(*refs):
        for cp in make_copies(refs[:n_ops], refs[n_ops], refs[n_ops + 1]):
            cp.wait_send()
            cp.wait_recv()

    res = pl.pallas_call(
        body, name=name, out_shape=tuple(_hbm(t) for t in operands),
        in_specs=[HBM_SPEC] * n_ops + [SEM_SPEC, SEM_SPEC] + [ANY_SPEC] * len(after),
        out_specs=tuple([HBM_SPEC] * n_ops), input_output_aliases={i: i for i in range(n_ops)},
        compiler_params=pltpu.CompilerParams(has_side_effects=SIDE_EFFECT),
    )(*operands, send_sems, recv_sems, *after)
    return list(res)


def _halves(make_copies, na):
    return lambda refs, send_sems, recv_sems: make_copies(refs[:na], refs[na:], send_sems, recv_sems)


def _direct_start(gather, arrays, through, name):
    na = len(arrays)
    lands = [lax.empty((N_DEV,) + t.shape if gather else (N_DEV - 1,) + t.shape[1:], t.dtype) for t in arrays]
    return _split_copy_start(_halves(functools.partial(_direct_copies, gather), na), 7 * na, [*arrays, *lands],
                             through, name)


def _direct_wait(gather, started, after, name):
    na = len(started[2]) // 2
    operands = _split_copy_wait(_halves(functools.partial(_direct_copies, gather), na), started, after, name)
    return operands[:na], operands[na:]


def _two_level_gather(shards, glue, name):
    na = len(shards)
    lands = [lax.empty((N_DEV,) + t.shape, t.dtype) for t in shards]
    nothing = jnp.zeros((SUBLANES, LANES), F32)
    started, _ = _split_copy_start(_halves(_chip_copies, na), 4 * na, [*shards, *lands], nothing, name + "_start")
    operands = _split_copy_wait(_halves(_chip_copies, na), started, glue, name + "_wait")
    started, mine = _split_copy_start(_forward_copies, 3 * na, operands[na:], operands[0], name + "_forward_start")
    lands = _split_copy_wait(_forward_copies, started, mine, name + "_forward_wait")
    return [lax.dynamic_update_index_in_dim(land, own, _my_slot(), 0)
            for own, land in zip([mine, *operands[1:na]], lands)]


def _row_tile(rows):
    for t in (256, 176, 128):
        if rows % t == 0 and rows > t:
            return t
    return rows


def _adamw_math(g, w, m, v):
    m_new = ADAM_B1 * m + (1.0 - ADAM_B1) * g
    v_new = ADAM_B2 * v + (1.0 - ADAM_B2) * jnp.square(g)
    m_hat = m_new / (1.0 - ADAM_B1 ** ADAM_STEP)
    v_hat = v_new / (1.0 - ADAM_B2 ** ADAM_STEP)
    delta = -ADAM_LR * (m_hat / (jnp.sqrt(v_hat) + ADAM_EPS) + ADAM_WD * w)
    return delta, m_new, v_new


def _adamw_sharded(chip, sums, got, w, m, v, name):
    r, c = w.shape
    tr = _row_tile(r)
    n_got = got.shape[0]

    def body(chip_ref, s_ref, g_ref, w_ref, m_ref, v_ref, go_ref, d_ref, nm_ref, nv_ref):
        g = s_ref[...].astype(F32)
        for i in range(n_got):
            g = g + g_ref[i].astype(F32)
        delta, m_new, v_new = _adamw_math(g, w_ref[...], m_ref[...], v_ref[...])
        go_ref[...] = g
        d_ref[...] = delta
        nm_ref[...] = m_new
        nv_ref[...] = v_new

    blk = pl.BlockSpec((tr, c), lambda i, chip_ref: (i, 0))
    out = jax.ShapeDtypeStruct((r, c), F32)
    return pl.pallas_call(
        body, name=name,
        grid_spec=pltpu.PrefetchScalarGridSpec(
            num_scalar_prefetch=1, grid=(r // tr,),
            in_specs=[pl.BlockSpec((None, tr, c), lambda i, chip_ref: (chip_ref[0], i, 0)),
                      pl.BlockSpec((n_got, tr, c), lambda i, chip_ref: (0, i, 0)), blk, blk, blk],
            out_specs=[blk, blk, blk, blk]),
        out_shape=[out, out, out, out],
        compiler_params=_params(("parallel",)),
    )(chip, sums, got, w, m, v)


def _adamw_replicated(parts, w, m, v):
    p, r, c = parts.shape
    names = SMALL_NAMES
    shapes = [w[n].shape for n in names]

    def body(*refs):
        p_ref = refs[0]
        ins = refs[1:1 + 3 * len(names)]
        outs = refs[1 + 3 * len(names):-2]
        loss_ref, sum_scr = refs[-2], refs[-1]
        total = p_ref[0]
        for i in range(1, p):
            total = total + p_ref[i]
        sum_scr[...] = total
        for k, n in enumerate(names):
            w_ref, m_ref, v_ref = ins[3 * k:3 * k + 3]
            g_ref, d_ref, nm_ref, nv_ref = outs[4 * k:4 * k + 4]
            for row, lane0, width, src_row in _small_pieces(n, shapes[k]):
                here = (slice(row, row + 1), slice(lane0, lane0 + width))
                g = sum_scr[src_row:src_row + 1, 0:width]
                delta, m_new, v_new = _adamw_math(g, w_ref[here], m_ref[here], v_ref[here])
                g_ref[here] = g
                d_ref[here] = delta
                nm_ref[here] = m_new
                nv_ref[here] = v_new
        loss_ref[...] = sum_scr[SMALL_LOSS_ROW:SMALL_LOSS_ROW + 1, 0:1]

    whole = lambda shape: pl.BlockSpec(shape, lambda i: (0,) * len(shape))
    args = [parts] + [t[n] for n in names for t in (w, m, v)]
    out_shape = [jax.ShapeDtypeStruct(s, F32) for s in shapes for _ in range(4)] + [jax.ShapeDtypeStruct((1, 1), F32)]
    res = pl.pallas_call(
        body, name="adamw_replicated", grid=(1,),
        in_specs=[whole(t.shape) for t in args], out_specs=[whole(s.shape) for s in out_shape],
        out_shape=out_shape, scratch_shapes=[pltpu.VMEM((r, c), F32)],
        compiler_params=_params(("arbitrary",)),
    )(*args)
    results = [{n: res[4 * k + j] for k, n in enumerate(names)} for j in range(4)]
    return results, res[-1]


SHARDED_NAMES = ("w_in", "ml_conv_w", "w_out", "ca_wq", "ca_wkv", "ca_wo", "ffn_w_up", "ffn_conv_w", "ffn_w_down")
SMALL_NAMES = ("b_in", "hg_lb_logits", "hg_norm_w", "ml_conv_b", "ml_norm_w", "ln1_g", "ln1_b",
               "ln2_g", "ln2_b", "ffn_conv_b", "ln3_g", "ln3_b")
WEIGHT_NAMES = ("w_in", "b_in", "hg_lb_logits", "hg_norm_w", "ml_conv_w", "ml_conv_b", "ml_norm_w", "w_out",
                "ln1_g", "ln1_b", "ca_wq", "ca_wkv", "ca_wo", "ln2_g", "ln2_b", "ffn_w_up", "ffn_conv_w",
                "ffn_conv_b", "ffn_w_down", "ln3_g", "ln3_b")
PAD_TO = {"ffn_conv_w": UP_SHARD_P}
SMALL_ROWS = 24
SMALL_W = D_MODEL
SMALL_SHAPES = {"b_in": (1, D_IN), "hg_lb_logits": (2, D_GROUP), "hg_norm_w": (1, D_GROUP),
                "ml_conv_b": (1, 2 * D_GROUP), "ml_norm_w": (1, D_GROUP), "ln1_g": (1, D_MODEL), "ln1_b": (1, D_MODEL),
                "ln2_g": (1, D_MODEL), "ln2_b": (1, D_MODEL), "ffn_conv_b": (1, D_UP), "ln3_g": (1, D_MODEL),
                "ln3_b": (1, D_MODEL)}


def _shard_2d(name, block):
    t = block[0]
    if name in PAD_TO:
        t = jnp.pad(t, ((0, 0), (0, PAD_TO[name] - t.shape[1])))
    return t


TRANSPOSED = ("w_in", "ffn_w_up")


def _update_shard(name, block):
    if name not in TRANSPOSED:
        return _shard_2d(name, block)
    return jnp.transpose(block, (0, 2, 1))[0]


def _shard_like(name, t, like):
    if name in TRANSPOSED:
        out = jnp.transpose(t[None], (0, 2, 1))
        return with_layout_constraint(out, Layout(major_to_minor=(0, 2, 1))) if name == "ffn_w_up" else out
    return t[:, :like.shape[2]][None]


def _pad_cols(t, width):
    return jnp.pad(t, ((0, 0), (0, width - t.shape[1])))


FIRST_NAMES = ("w_in", "ml_conv_w")
FFN_NAMES = ("ffn_w_up", "ffn_w_down", "ffn_conv_w")
MID_NAMES = ("ca_wo", "ca_wq", "ca_wkv", "w_out")


def _first_weights(g, small):
    w = dict(small)
    w["w_in_t"] = g["w_in"].reshape(D_IN, D_MODEL)
    w["w_in_gate_t"] = jnp.pad(w["w_in_t"][D_IN_MAIN:], ((0, LANES - (D_IN - D_IN_MAIN)), (0, 0)))
    w["b_in_main"] = small["b_in"][:, :D_IN_MAIN]
    w["b_in_gate"] = _pad_cols(small["b_in"][:, D_IN_MAIN:], LANES)
    w["ml_conv_w"] = jnp.transpose(g["ml_conv_w"], (1, 0, 2)).reshape(ML_CONV, 2 * D_GROUP)
    return w


def _mid_weights(g):
    w = {n: g[n].reshape(D_MODEL, D_MODEL) for n in ("w_out", "ca_wq", "ca_wo")}
    w["ca_wkv"] = g["ca_wkv"]
    return w


FFN_UP_NAMES = ("ffn_w_up", "ffn_conv_w")
FFN_DOWN_NAMES = ("ffn_w_down",)


def _ffn_up_weights(g, small):
    w = {"ffn_w_up_t": g["ffn_w_up"].reshape(D_UP, D_MODEL)}
    w["ffn_conv_w"] = jnp.transpose(g["ffn_conv_w"][:, :, :UP_SHARD], (1, 0, 2)).reshape(FFN_CONV, D_UP)
    w["ffn_conv_b"] = small["ffn_conv_b"].reshape(1, D_UP)
    return w


def _ffn_down_weights(g):
    return {"ffn_w_down": g["ffn_w_down"].reshape(D_FF, D_MODEL)}


def _owner_stack(n, grads):
    if n == "w_in":
        return grads[n].reshape(N_DEV, W_IN_SHARD, D_MODEL)
    if n == "ffn_w_up":
        return grads[n].reshape(N_DEV, UP_SHARD, D_MODEL)
    if n in ("w_out", "ca_wq", "ca_wo"):
        return grads[n].reshape(N_DEV, D_MODEL // N_DEV, D_MODEL)
    if n == "ffn_w_down":
        return grads[n].reshape(N_DEV, D_FF // N_DEV, D_MODEL)
    if n == "ml_conv_w":
        return jnp.transpose(grads[n].reshape(ML_CONV, N_DEV, LANES), (1, 0, 2))
    if n == "ffn_conv_w":
        shards = jnp.transpose(grads[n].reshape(FFN_CONV, N_DEV, UP_SHARD), (1, 0, 2))
        return jnp.pad(shards, ((0, 0), (0, 0), (0, UP_SHARD_P - UP_SHARD)))
    return grads[n]


def _small_grads(grads):
    out = {n: grads[n] for n in SMALL_NAMES if n in grads}
    out["b_in"] = jnp.concatenate([grads["b_in_main"], grads["b_in_gate"][:, :D_IN - D_IN_MAIN]], axis=1)
    return out


def _small_rows(shape):
    return shape[0] if shape[1] <= SMALL_W else -(-shape[1] // SMALL_W)


SMALL_BASE = {n: sum(_small_rows(SMALL_SHAPES[k]) for k in SMALL_NAMES[:i]) for i, n in enumerate(SMALL_NAMES)}
SMALL_LOSS_ROW = sum(_small_rows(SMALL_SHAPES[n]) for n in SMALL_NAMES)
assert SMALL_LOSS_ROW < SMALL_ROWS


def _small_pieces(name, shape):
    base = SMALL_BASE[name]
    if shape[1] <= SMALL_W:
        return [(i, 0, shape[1], base + i) for i in range(shape[0])]
    return [(0, k * SMALL_W, min(SMALL_W, shape[1] - k * SMALL_W), base + k) for k in range(_small_rows(shape))]


def _pack_small(p, loss):
    rows = []
    for n in SMALL_NAMES:
        t = p[n]
        nrows = _small_rows(t.shape)
        if t.shape[1] <= SMALL_W:
            rows.append(_pad_cols(t, SMALL_W))
        else:
            rows.append(_pad_cols(t, nrows * SMALL_W).reshape(nrows, SMALL_W))
    rows.append(_pad_cols(loss, SMALL_W))
    slab = jnp.concatenate(rows, axis=0)
    return jnp.pad(slab, ((0, SMALL_ROWS - slab.shape[0]), (0, 0)))


def kernel(x, mem, w_in, b_in, hg_lb_logits, hg_norm_w, ml_conv_w, ml_conv_b, ml_norm_w, w_out, ln1_g, ln1_b, ca_wq, ca_wkv, ca_wo, ln2_g, ln2_b, ffn_w_up, ffn_conv_w, ffn_conv_b, ffn_w_down, ln3_g, ln3_b, loss_target, m_w_in, m_b_in, m_hg_lb_logits, m_hg_norm_w, m_ml_conv_w, m_ml_conv_b, m_ml_norm_w, m_w_out, m_ln1_g, m_ln1_b, m_ca_wq, m_ca_wkv, m_ca_wo, m_ln2_g, m_ln2_b, m_ffn_w_up, m_ffn_conv_w, m_ffn_conv_b, m_ffn_w_down, m_ln3_g, m_ln3_b, v_w_in, v_b_in, v_hg_lb_logits, v_hg_norm_w, v_ml_conv_w, v_ml_conv_b, v_ml_norm_w, v_w_out, v_ln1_g, v_ln1_b, v_ca_wq, v_ca_wkv, v_ca_wo, v_ln2_g, v_ln2_b, v_ffn_w_up, v_ffn_conv_w, v_ffn_conv_b, v_ffn_w_down, v_ln3_g, v_ln3_b):
    params = dict(w_in=w_in, b_in=b_in, hg_lb_logits=hg_lb_logits, hg_norm_w=hg_norm_w, ml_conv_w=ml_conv_w,
                  ml_conv_b=ml_conv_b, ml_norm_w=ml_norm_w, w_out=w_out, ln1_g=ln1_g, ln1_b=ln1_b, ca_wq=ca_wq,
                  ca_wkv=ca_wkv, ca_wo=ca_wo, ln2_g=ln2_g, ln2_b=ln2_b, ffn_w_up=ffn_w_up, ffn_conv_w=ffn_conv_w,
                  ffn_conv_b=ffn_conv_b, ffn_w_down=ffn_w_down, ln3_g=ln3_g, ln3_b=ln3_b)
    mom1 = dict(w_in=m_w_in, b_in=m_b_in, hg_lb_logits=m_hg_lb_logits, hg_norm_w=m_hg_norm_w,
                ml_conv_w=m_ml_conv_w, ml_conv_b=m_ml_conv_b, ml_norm_w=m_ml_norm_w, w_out=m_w_out, ln1_g=m_ln1_g,
                ln1_b=m_ln1_b, ca_wq=m_ca_wq, ca_wkv=m_ca_wkv, ca_wo=m_ca_wo, ln2_g=m_ln2_g, ln2_b=m_ln2_b,
                ffn_w_up=m_ffn_w_up, ffn_conv_w=m_ffn_conv_w, ffn_conv_b=m_ffn_conv_b, ffn_w_down=m_ffn_w_down,
                ln3_g=m_ln3_g, ln3_b=m_ln3_b)
    mom2 = dict(w_in=v_w_in, b_in=v_b_in, hg_lb_logits=v_hg_lb_logits, hg_norm_w=v_hg_norm_w,
                ml_conv_w=v_ml_conv_w, ml_conv_b=v_ml_conv_b, ml_norm_w=v_ml_norm_w, w_out=v_w_out, ln1_g=v_ln1_g,
                ln1_b=v_ln1_b, ca_wq=v_ca_wq, ca_wkv=v_ca_wkv, ca_wo=v_ca_wo, ln2_g=v_ln2_g, ln2_b=v_ln2_b,
                ffn_w_up=v_ffn_w_up, ffn_conv_w=v_ffn_conv_w, ffn_conv_b=v_ffn_conv_b, ffn_w_down=v_ffn_w_down,
                ln3_g=v_ln3_g, ln3_b=v_ln3_b)

    x_idx, y_idx, c_idx = _coords()
    as_index = lambda v: jnp.reshape(v, (1,)).astype(jnp.int32)
    me = as_index(4 * x_idx + 2 * y_idx + c_idx)
    small_params = {n: params[n] for n in SMALL_NAMES}

    shards = {n: _update_shard(n, params[n]) for n in SHARDED_NAMES}
    m_shards = {n: _update_shard(n, mom1[n]) for n in SHARDED_NAMES}
    v_shards = {n: _update_shard(n, mom2[n]) for n in SHARDED_NAMES}
    outgoing = {n: shards[n] if "conv" in n else shards[n].astype(BF16) for n in SHARDED_NAMES}
    to_send = lambda names: [outgoing[n] for n in names]
    glue = [*m_shards.values(), *v_shards.values(), *shards.values(),
            *[outgoing[n] for n in SHARDED_NAMES if n not in FIRST_NAMES]]
    first = dict(zip(FIRST_NAMES, _two_level_gather(to_send(FIRST_NAMES), glue, "weights_gather_first")))
    mid_started, through = _direct_start(True, to_send(MID_NAMES), first["w_in"], "weights_gather_start_mid")
    ffn_started, through = _direct_start(True, to_send(FFN_UP_NAMES), through, "weights_gather_start_ffn_up")
    down_started, first["w_in"] = _direct_start(True, to_send(FFN_DOWN_NAMES), through,
                                                "weights_gather_start_ffn_down")

    def gathered_weights(names, started, after, tag):
        mine, lands = _direct_wait(True, started, after, "weights_gather_wait_" + tag)
        return {n: lax.dynamic_update_index_in_dim(land, own, me[0], 0) for n, own, land in zip(names, mine, lands)}

    started = {}

    def start_group(names, tag):
        def hook(grads, through):
            stacks = [_owner_stack(n, grads).astype(BF16) for n in names]
            started[tag], through = _direct_start(False, stacks, through, "grads_start_" + tag)
            return through
        return hook

    def start_small(grads, loss, through):
        started["small"], through = _direct_start(True, [_pack_small(_small_grads(grads), loss)], through,
                                                  "small_gather_start")
        return through

    loss, grad_x, grads = _local_step(
        x[0], mem[0], loss_target[0], _first_weights(first, small_params),
        lambda y: _mid_weights(gathered_weights(MID_NAMES, mid_started, y, "mid")),
        lambda x2: _ffn_up_weights(gathered_weights(FFN_UP_NAMES, ffn_started, x2, "ffn_up"), small_params),
        lambda hid: _ffn_down_weights(gathered_weights(FFN_DOWN_NAMES, down_started, hid, "ffn_down")),
        start_group(FFN_NAMES, "ffn"), start_group(MID_NAMES, "mid"), start_small, start_group(FIRST_NAMES, "last"))

    updated, sharded_out = {}, {}

    def update_group(names, tag, after):
        stacks, lands = _direct_wait(False, started[tag], after, "grads_wait_" + tag)
        for n, st, land in zip(names, stacks, lands):
            updated[n] = _adamw_sharded(me, st, land, shards[n], m_shards[n], v_shards[n], "adamw_" + n)
            sharded_out[n] = [_shard_like(n, t, params[n]) for t in updated[n]]

    update_group(FFN_NAMES, "ffn", grad_x)
    update_group(MID_NAMES, "mid", grad_x)
    own_small, small_lands = _direct_wait(True, started["small"], grad_x, "small_gather_wait")
    small_parts = lax.dynamic_update_index_in_dim(small_lands[0], own_small[0], me[0], 0)
    small_out, total_loss = _adamw_replicated(small_parts, small_params, {n: mom1[n] for n in SMALL_NAMES},
                                              {n: mom2[n] for n in SMALL_NAMES})
    done = [t for n in FFN_NAMES + MID_NAMES for t in updated[n]]
    done += [t for small in small_out for t in small.values()]
    update_group(FIRST_NAMES, "last", done)

    outs = []
    for k, small in enumerate(small_out):
        outs.extend(sharded_out[n][k] if n in sharded_out else small[n] for n in WEIGHT_NAMES)
    return (total_loss[0, 0], grad_x[None], *outs)
```

```python
import functools
import math

import jax
import jax.numpy as jnp
from jax import lax
from jax.experimental import pallas as pl
from jax.experimental.layout import Layout, with_layout_constraint
from jax.experimental.pallas import tpu as pltpu

F32 = jnp.float32
BF16 = jnp.bfloat16
HIGHEST = lax.Precision.HIGHEST
MESH = pl.DeviceIdType.MESH

N_DEV = 8
D_MODEL = 1024
N_MEM = 256
N_HEADS = 4
D_HEAD = 128
D_GROUP = N_HEADS * D_HEAD
CHUNK = 64
ML_CONV = 4
FFN_CONV = 3
D_FF = 2816
D_UP = 2 * D_FF
CA_HEADS = 4
CA_DH = D_MODEL // CA_HEADS
LANES = 128
SUBLANES = 8
D_IN = 8 * D_GROUP + 2 * N_HEADS
D_IN_MAIN = 8 * D_GROUP
W_IN_SHARD = D_IN // N_DEV
UP_SHARD = D_UP // N_DEV
UP_SHARD_P = 768
UP_TILE = D_UP // 4
ALPHA = 2.0 ** 0.25
LN_EPS = 1e-5
NEG_BIG = -1e30
ADAM_LR = 0.001
ADAM_B1 = 0.9
ADAM_B2 = 0.999
ADAM_EPS = 1e-08
ADAM_WD = 0.01
ADAM_STEP = 10
VMEM_LIMIT = 56 * 1024 * 1024

SEG_MQ = 4 * D_GROUP // LANES
VO_BLOCK = 3


def _params(sem):
    return pltpu.CompilerParams(dimension_semantics=sem, vmem_limit_bytes=VMEM_LIMIT)


def _dg(a, b, ca, cb, precision=None):
    return lax.dot_general(a, b, (((ca,), (cb,)), ((), ())), precision=precision,
                           preferred_element_type=F32)


def _nn_raw(a, b):
    return _dg(a.astype(BF16), b.astype(BF16), 1, 0)


def _nt_raw(a, b):
    return _dg(a.astype(BF16), b.astype(BF16), 1, 1)


def _tn_raw(a, b):
    return _dg(a.astype(BF16), b.astype(BF16), 0, 0)


@jax.custom_vjp
def _nn(a, b):
    return _nn_raw(a, b)


_nn.defvjp(lambda a, b: (_nn_raw(a, b), (a, b)),
           lambda res, g: (_nt_raw(g, res[1]), _tn_raw(res[0], g)))


@jax.custom_vjp
def _nt(a, b):
    return _nt_raw(a, b)


_nt.defvjp(lambda a, b: (_nt_raw(a, b), (a, b)),
           lambda res, g: (_nn_raw(g, res[1]), _tn_raw(g, res[0])))


@jax.custom_vjp
def _tn(a, b):
    return _tn_raw(a, b)


_tn.defvjp(lambda a, b: (_tn_raw(a, b), (a, b)),
           lambda res, g: (_nt_raw(res[1], g), _nn_raw(res[0], g)))


def _layer_norm(z, g, b):
    mu = jnp.mean(z, axis=-1, keepdims=True)
    var = jnp.mean(jnp.square(z - mu), axis=-1, keepdims=True)
    return (z - mu) * lax.rsqrt(var + LN_EPS) * g + b


def _matmul_nn(a, w, bias, tm, tn, name, out_dtype=F32):
    m, k = a.shape
    if w.ndim == 3:
        n = w.shape[0] * w.shape[2]
        assert tn == w.shape[2]
        w_spec = pl.BlockSpec((None, k, tn), lambda i, j: (j, 0, 0))
    else:
        n = w.shape[1]
        w_spec = pl.BlockSpec((k, tn), lambda i, j: (0, j))

    def body(*refs):
        a_ref, w_ref = refs[0], refs[1]
        o_ref = refs[-1]
        acc = _nn_raw(a_ref[...], w_ref[...])
        if bias is not None:
            acc = acc + refs[2][...]
        o_ref[...] = acc.astype(o_ref.dtype)

    in_specs = [pl.BlockSpec((tm, k), lambda i, j: (i, 0)), w_spec]
    args = [a, w]
    if bias is not None:
        in_specs.append(pl.BlockSpec((1, tn), lambda i, j: (0, j)))
        args.append(bias)
    return pl.pallas_call(
        body, name=name, grid=(m // tm, n // tn), in_specs=in_specs,
        out_specs=pl.BlockSpec((tm, tn), lambda i, j: (i, j)),
        out_shape=jax.ShapeDtypeStruct((m, n), out_dtype),
        compiler_params=_params(("parallel", "parallel")),
    )(*args)


def _matmul_nt(d, w, tm, tk, name, k_out=None, bias=None, out_dtype=F32):
    m, n = d.shape
    k = k_out or w.shape[0]

    def body(*refs):
        acc = _nt_raw(refs[0][...], refs[1][...])
        if bias is not None:
            acc = acc + refs[2][...]
        refs[-1][...] = acc.astype(refs[-1].dtype)

    in_specs = [pl.BlockSpec((tm, n), lambda i, j: (i, 0)), pl.BlockSpec((tk, n), lambda i, j: (j, 0))]
    args = [d, w]
    if bias is not None:
        in_specs.append(pl.BlockSpec((1, tk), lambda i, j: (0, j)))
        args.append(bias)
    return pl.pallas_call(
        body, name=name, grid=(m // tm, k // tk), in_specs=in_specs,
        out_specs=pl.BlockSpec((tm, tk), lambda i, j: (i, j)),
        out_shape=jax.ShapeDtypeStruct((m, k), out_dtype),
        compiler_params=_params(("parallel", "parallel")),
    )(*args)


def _input_projection(x, w_t, w_gate_t, b_main, b_gate, tm, tk):
    m, n = x.shape

    def body(x_ref, w_ref, wg_ref, b_ref, bg_ref, o_ref, g_ref):
        lhs = x_ref[...].astype(BF16)
        o_ref[...] = _nt_raw(lhs, w_ref[...]) + b_ref[...]

        @pl.when(pl.program_id(1) == 0)
        def _():
            g_ref[...] = _nt_raw(lhs, wg_ref[...]) + bg_ref[...]

    return pl.pallas_call(
        body, name="proj", grid=(m // tm, D_IN_MAIN // tk),
        in_specs=[pl.BlockSpec((tm, n), lambda i, j: (i, 0)), pl.BlockSpec((tk, n), lambda i, j: (j, 0)),
                  pl.BlockSpec((LANES, n), lambda i, j: (0, 0)), pl.BlockSpec((1, tk), lambda i, j: (0, j)),
                  pl.BlockSpec((1, LANES), lambda i, j: (0, 0))],
        out_specs=[pl.BlockSpec((tm, tk), lambda i, j: (i, j)), pl.BlockSpec((tm, LANES), lambda i, j: (i, 0))],
        out_shape=[jax.ShapeDtypeStruct((m, D_IN_MAIN), F32), jax.ShapeDtypeStruct((m, LANES), F32)],
        compiler_params=_params(("parallel", "arbitrary")),
    )(x, w_t, w_gate_t, b_main, b_gate)


def _input_projection_grads(d_proj, d_gates, x, tm, tt):
    t, m = d_proj.shape
    n = x.shape[1]
    last = t // tt - 1
    n_gate = D_IN - D_IN_MAIN

    def body(a_ref, g_ref, x_ref, o_ref, acc_ref, accg_ref):
        i, kk = pl.program_id(0), pl.program_id(1)

        @pl.when(kk == 0)
        def _():
            acc_ref[...] = jnp.zeros_like(acc_ref)

        @pl.when((kk == 0) & (i == 0))
        def _():
            accg_ref[...] = jnp.zeros_like(accg_ref)

        rhs = x_ref[...].astype(BF16)
        acc_ref[...] += _tn_raw(a_ref[...], rhs)

        @pl.when(i == 0)
        def _():
            accg_ref[...] += _tn_raw(g_ref[...], rhs)

        @pl.when(kk == last)
        def _():
            o_ref[pl.ds(pl.multiple_of(i * tm, tm), tm), :] = acc_ref[...].astype(o_ref.dtype)

        @pl.when((kk == last) & (i == 0))
        def _():
            o_ref[m:m + n_gate, :] = accg_ref[0:n_gate, :].astype(o_ref.dtype)

    return pl.pallas_call(
        body, name="d_w_in", grid=(m // tm, t // tt),
        in_specs=[pl.BlockSpec((tt, tm), lambda i, kk: (kk, i)), pl.BlockSpec((tt, LANES), lambda i, kk: (kk, 0)),
                  pl.BlockSpec((tt, n), lambda i, kk: (kk, 0))],
        out_specs=pl.BlockSpec((m + n_gate, n), lambda i, kk: (0, 0)),
        out_shape=jax.ShapeDtypeStruct((m + n_gate, n), BF16),
        scratch_shapes=[pltpu.VMEM((tm, n), F32), pltpu.VMEM((LANES, n), F32)],
        compiler_params=_params(("arbitrary", "arbitrary")),
    )(d_proj, d_gates, x)


def _matmul_nn_sum(pairs, add, scale, tm, name):
    m = pairs[0][0].shape[0]
    n = pairs[0][1].shape[1]
    in_specs, args = [], []
    for a, w, row0 in pairs:
        kk = a.shape[1]
        in_specs += [pl.BlockSpec((tm, kk), lambda i: (i, 0)),
                     pl.BlockSpec((kk, n), lambda i, blk=row0 // kk: (blk, 0))]
        args += [a, w]
    if add is not None:
        in_specs.append(pl.BlockSpec((tm, n), lambda i: (i, 0)))
        args.append(add)

    def body(*refs):
        acc = None
        for p in range(len(pairs)):
            term = _nn_raw(refs[2 * p][...], refs[2 * p + 1][...])
            acc = term if acc is None else acc + term
        if add is not None:
            acc = acc + scale * refs[2 * len(pairs)][...]
        refs[-1][...] = acc

    return pl.pallas_call(
        body, name=name, grid=(m // tm,), in_specs=in_specs,
        out_specs=pl.BlockSpec((tm, n), lambda i: (i, 0)),
        out_shape=jax.ShapeDtypeStruct((m, n), F32),
        compiler_params=_params(("parallel",)),
    )(*args)


def _matmul_tn(a, b, tm, tn, tt, name, shards=None, shard0=0, group=1, into=None, colsum=False, rows=None, row0=0):
    t, m = a.shape
    n = b.shape[1]
    assert not colsum or tm == m
    n_in = 2 + (into is not None)
    out_dtype = BF16
    per_step = 1 if shards is None else group
    width = per_step * tn

    def body(*refs):
        a_ref, b_ref = refs[0], refs[1]
        o_ref, acc_ref = refs[n_in], refs[-1]
        first = pl.program_id(2) == 0

        @pl.when(first)
        def _():
            acc_ref[...] = jnp.zeros_like(acc_ref)

        if shards is None:
            acc_ref[...] += _tn_raw(a_ref[...], b_ref[...])
        else:
            lhs = a_ref[...].astype(BF16)
            for g in range(per_step):
                acc_ref[g] += _tn_raw(lhs, b_ref[:, g * tn:(g + 1) * tn])

        @pl.when(pl.program_id(2) == t // tt - 1)
        def _():
            o_ref[...] = acc_ref[...].astype(o_ref.dtype)

        if colsum:
            s_ref = refs[n_in + 1]

            @pl.when(first)
            def _():
                s_ref[...] = jnp.zeros_like(s_ref)

            s_ref[...] += jnp.sum(b_ref[...], axis=0, keepdims=True)

    in_specs = [pl.BlockSpec((tt, tm), lambda i, j, kk: (kk, i)),
                pl.BlockSpec((tt, width), lambda i, j, kk: (kk, j))]
    args = [a, b]
    aliases = {}
    if into is not None:
        in_specs.append(pl.BlockSpec(memory_space=pl.ANY))
        args.append(into)
        aliases = {2: 0}
    if shards is None:
        out_specs = [pl.BlockSpec((tm, tn), lambda i, j, kk: (row0 // tm + i, j))]
        out_shape = [jax.ShapeDtypeStruct((rows or m, n), out_dtype)]
        acc = pltpu.VMEM((tm, tn), F32)
    else:
        out_specs = [pl.BlockSpec((per_step, tm, tn), lambda i, j, kk: (shard0 // per_step + j, i, 0))]
        out_shape = [jax.ShapeDtypeStruct((shards, m, tn), out_dtype)]
        acc = pltpu.VMEM((per_step, tm, tn), F32)
    if colsum:
        out_specs.append(pl.BlockSpec((1, tn), lambda i, j, kk: (0, j)))
        out_shape.append(jax.ShapeDtypeStruct((1, n), F32))
    res = pl.pallas_call(
        body, name=name, grid=(m // tm, n // width, t // tt), in_specs=in_specs, out_specs=out_specs,
        out_shape=out_shape, input_output_aliases=aliases, scratch_shapes=[acc],
        compiler_params=_params(("parallel", "parallel", "arbitrary")),
    )(*args)
    return res if colsum else res[0]


ROW_TILE = 64


def _stack(ref, start, rows):
    return ref[pl.ds(start, rows), :].astype(F32).reshape(rows // SUBLANES, SUBLANES, LANES)


def _vreg_rows(ref, n):
    return [jnp.broadcast_to(ref[j:j + 1, :], (SUBLANES, LANES))[None] for j in range(n)]


def _column_total(acc):
    return jnp.sum(acc, axis=0, keepdims=True)


def _conv_fwd_tile(pad_ref, taps_w, bias, r0, rows):
    taps = len(taps_w)
    acc = bias
    for j in range(taps):
        acc = acc + _stack(pad_ref, SUBLANES - (taps - 1 - j) + r0, rows) * taps_w[j]
    return acc


def _conv_grads_tile(pad_ref, dpad_ref, dx_ref, taps_w, dws, r0, rows):
    taps = len(taps_w)
    x_rows = _stack(pad_ref, SUBLANES + r0, rows)
    dx = None
    for j in range(taps):
        d_shifted = _stack(dpad_ref, r0 + (taps - 1 - j), rows)
        term = d_shifted * taps_w[j]
        dx = term if dx is None else dx + term
        dws[j] = dws[j] + jnp.sum(d_shifted * x_rows, axis=0)
    dx_ref[r0:r0 + rows, :] = dx.reshape(rows, LANES).astype(dx_ref.dtype)
    return jnp.sum(dx, axis=0)


def _ml_conv_fwd(proj, conv_w, conv_b):
    s = proj.shape[0]
    nblk = 2 * D_GROUP // LANES

    def body(x_ref, w_ref, b_ref, o_ref, pad_ref):
        pad_ref[0:SUBLANES, :] = jnp.zeros((SUBLANES, LANES), F32)
        pad_ref[SUBLANES:, :] = x_ref[...].astype(F32)
        taps_w, bias = _vreg_rows(w_ref, ML_CONV), _vreg_rows(b_ref, 1)[0]
        for r0 in range(0, s, ROW_TILE):
            rows = min(ROW_TILE, s - r0)
            o_ref[r0:r0 + rows, :] = jax.nn.silu(_conv_fwd_tile(pad_ref, taps_w, bias, r0, rows)).reshape(rows, LANES)

    return pl.pallas_call(
        body, name="ml_conv_fwd", grid=(nblk,),
        in_specs=[pl.BlockSpec((s, LANES), lambda j: (0, SEG_MQ + j)),
                  pl.BlockSpec((ML_CONV, LANES), lambda j: (0, j)),
                  pl.BlockSpec((1, LANES), lambda j: (0, j))],
        out_specs=pl.BlockSpec((s, LANES), lambda j: (0, j)),
        out_shape=jax.ShapeDtypeStruct((s, 2 * D_GROUP), F32),
        scratch_shapes=[pltpu.VMEM((s + SUBLANES, LANES), F32)],
        compiler_params=_params(("parallel",)),
    )(proj, conv_w, conv_b)


def _ml_conv_bwd(proj, conv_w, conv_b, d_qk, d_proj):
    s = proj.shape[0]
    nblk = 2 * D_GROUP // LANES

    def body(x_ref, w_ref, b_ref, dy_ref, _, dx_ref, dw_ref, db_ref, dxs_ref, pad_ref, dpad_ref):
        pad_ref[0:SUBLANES, :] = jnp.zeros((SUBLANES, LANES), F32)
        pad_ref[SUBLANES:, :] = x_ref[...].astype(F32)
        dpad_ref[s:, :] = jnp.zeros((SUBLANES, LANES), F32)
        taps_w, bias = _vreg_rows(w_ref, ML_CONV), _vreg_rows(b_ref, 1)[0]
        db = jnp.zeros((SUBLANES, LANES), F32)
        for r0 in range(0, s, ROW_TILE):
            rows = min(ROW_TILE, s - r0)
            pre = _conv_fwd_tile(pad_ref, taps_w, bias, r0, rows)
            _, vjp = jax.vjp(jax.nn.silu, pre)
            d_pre, = vjp(_stack(dy_ref, r0, rows))
            dpad_ref[r0:r0 + rows, :] = d_pre.reshape(rows, LANES)
            db = db + jnp.sum(d_pre, axis=0)
        db_ref[...] = _column_total(db)
        dws = [jnp.zeros((SUBLANES, LANES), F32) for _ in range(ML_CONV)]
        dx_sum = jnp.zeros((SUBLANES, LANES), F32)
        for r0 in range(0, s, ROW_TILE):
            dx_sum = dx_sum + _conv_grads_tile(pad_ref, dpad_ref, dx_ref, taps_w, dws, r0, min(ROW_TILE, s - r0))
        dxs_ref[...] = _column_total(dx_sum)
        for j in range(ML_CONV):
            dw_ref[j:j + 1, :] = _column_total(dws[j])

    return pl.pallas_call(
        body, name="ml_conv_bwd", grid=(nblk,),
        in_specs=[pl.BlockSpec((s, LANES), lambda j: (0, SEG_MQ + j)),
                  pl.BlockSpec((ML_CONV, LANES), lambda j: (0, j)),
                  pl.BlockSpec((1, LANES), lambda j: (0, j)),
                  pl.BlockSpec((s, LANES), lambda j: (0, j)),
                  pl.BlockSpec(memory_space=pl.ANY)],
        out_specs=[pl.BlockSpec((s, LANES), lambda j: (0, SEG_MQ + j)),
                   pl.BlockSpec((ML_CONV, LANES), lambda j: (0, j)),
                   pl.BlockSpec((1, LANES), lambda j: (0, j)),
                   pl.BlockSpec((1, LANES), lambda j: (0, j))],
        out_shape=[jax.ShapeDtypeStruct(d_proj.shape, d_proj.dtype),
                   jax.ShapeDtypeStruct((ML_CONV, 2 * D_GROUP), F32),
                   jax.ShapeDtypeStruct((1, 2 * D_GROUP), F32),
                   jax.ShapeDtypeStruct((1, 2 * D_GROUP), F32)],
        input_output_aliases={4: 0},
        scratch_shapes=[pltpu.VMEM((s + SUBLANES, LANES), F32), pltpu.VMEM((s + SUBLANES, LANES), F32)],
        compiler_params=_params(("parallel",)),
    )(proj, conv_w, conv_b, d_qk, d_proj)


def _gelu_mul(a, b):
    return jax.nn.gelu(a) * b


GELU_C = math.sqrt(2.0 / math.pi)
GELU_K = 0.044715


def _gelu_mul_grads(a, b, d):
    a2 = a * a
    t = jnp.tanh(GELU_C * (a + GELU_K * (a * a2)))
    cdf = 0.5 * (1.0 + t)
    slope = cdf + (0.5 * GELU_C) * a * (1.0 - t * t) * (1.0 + (3.0 * GELU_K) * a2)
    return d * b * slope, d * (a * cdf)


FFN_BLOCKS = D_FF // LANES


def _ffn_conv_fwd(u, conv_w, conv_b):
    s = u.shape[0]

    def body(g_ref, v_ref, wg_ref, wv_ref, bg_ref, bv_ref, o_ref, gpad_ref, vpad_ref):
        for pad_ref, x_ref in ((gpad_ref, g_ref), (vpad_ref, v_ref)):
            pad_ref[0:SUBLANES, :] = jnp.zeros((SUBLANES, LANES), F32)
            pad_ref[SUBLANES:, :] = x_ref[...].astype(F32)
        taps_g, bias_g = _vreg_rows(wg_ref, FFN_CONV), _vreg_rows(bg_ref, 1)[0]
        taps_v, bias_v = _vreg_rows(wv_ref, FFN_CONV), _vreg_rows(bv_ref, 1)[0]
        for r0 in range(0, s, ROW_TILE):
            rows = min(ROW_TILE, s - r0)
            ug = _conv_fwd_tile(gpad_ref, taps_g, bias_g, r0, rows)
            uv = _conv_fwd_tile(vpad_ref, taps_v, bias_v, r0, rows)
            o_ref[r0:r0 + rows, :] = _gelu_mul(ug, uv).reshape(rows, LANES).astype(o_ref.dtype)

    col = lambda off: (lambda j: (0, off + j))
    return pl.pallas_call(
        body, name="ffn_conv_fwd", grid=(FFN_BLOCKS,),
        in_specs=[pl.BlockSpec((s, LANES), col(0)), pl.BlockSpec((s, LANES), col(FFN_BLOCKS)),
                  pl.BlockSpec((FFN_CONV, LANES), col(0)), pl.BlockSpec((FFN_CONV, LANES), col(FFN_BLOCKS)),
                  pl.BlockSpec((1, LANES), col(0)), pl.BlockSpec((1, LANES), col(FFN_BLOCKS))],
        out_specs=pl.BlockSpec((s, LANES), col(0)),
        out_shape=jax.ShapeDtypeStruct((s, D_FF), BF16),
        scratch_shapes=[pltpu.VMEM((s + SUBLANES, LANES), F32), pltpu.VMEM((s + SUBLANES, LANES), F32)],
        compiler_params=_params(("parallel",)),
    )(u, u, conv_w, conv_w, conv_b, conv_b)


def _ffn_conv_bwd(u, conv_w, conv_b, d_h):
    s = u.shape[0]

    def body(g_ref, v_ref, wg_ref, wv_ref, bg_ref, bv_ref, dh_ref,
             dug_ref, duv_ref, dwg_ref, dwv_ref, dbg_ref, dbv_ref,
             gpad_ref, vpad_ref, dgpad_ref, dvpad_ref):
        for pad_ref, x_ref in ((gpad_ref, g_ref), (vpad_ref, v_ref)):
            pad_ref[0:SUBLANES, :] = jnp.zeros((SUBLANES, LANES), F32)
            pad_ref[SUBLANES:, :] = x_ref[...].astype(F32)
        dgpad_ref[s:, :] = jnp.zeros((SUBLANES, LANES), F32)
        dvpad_ref[s:, :] = jnp.zeros((SUBLANES, LANES), F32)
        taps_g, bias_g = _vreg_rows(wg_ref, FFN_CONV), _vreg_rows(bg_ref, 1)[0]
        taps_v, bias_v = _vreg_rows(wv_ref, FFN_CONV), _vreg_rows(bv_ref, 1)[0]
        dbg = jnp.zeros((SUBLANES, LANES), F32)
        dbv = jnp.zeros((SUBLANES, LANES), F32)
        for r0 in range(0, s, ROW_TILE):
            rows = min(ROW_TILE, s - r0)
            ug = _conv_fwd_tile(gpad_ref, taps_g, bias_g, r0, rows)
            uv = _conv_fwd_tile(vpad_ref, taps_v, bias_v, r0, rows)
            d_ug, d_uv = _gelu_mul_grads(ug, uv, _stack(dh_ref, r0, rows))
            dgpad_ref[r0:r0 + rows, :] = d_ug.reshape(rows, LANES)
            dvpad_ref[r0:r0 + rows, :] = d_uv.reshape(rows, LANES)
            dbg = dbg + jnp.sum(d_ug, axis=0)
            dbv = dbv + jnp.sum(d_uv, axis=0)
        dbg_ref[...] = _column_total(dbg)
        dbv_ref[...] = _column_total(dbv)
        for pad_ref, dpad_ref, taps_w, dx_ref, dw_ref in ((gpad_ref, dgpad_ref, taps_g, dug_ref, dwg_ref),
                                                          (vpad_ref, dvpad_ref, taps_v, duv_ref, dwv_ref)):
            dws = [jnp.zeros((SUBLANES, LANES), F32) for _ in range(FFN_CONV)]
            for r0 in range(0, s, ROW_TILE):
                _conv_grads_tile(pad_ref, dpad_ref, dx_ref, taps_w, dws, r0, min(ROW_TILE, s - r0))
            for j in range(FFN_CONV):
                dw_ref[j:j + 1, :] = _column_total(dws[j])

    col = lambda off: (lambda j: (0, off + j))
    seq = pl.BlockSpec((s, LANES), col(0))
    return pl.pallas_call(
        body, name="ffn_conv_bwd", grid=(FFN_BLOCKS,),
        in_specs=[pl.BlockSpec((s, LANES), col(0)), pl.BlockSpec((s, LANES), col(FFN_BLOCKS)),
                  pl.BlockSpec((FFN_CONV, LANES), col(0)), pl.BlockSpec((FFN_CONV, LANES), col(FFN_BLOCKS)),
                  pl.BlockSpec((1, LANES), col(0)), pl.BlockSpec((1, LANES), col(FFN_BLOCKS)), seq],
        out_specs=[seq, seq, pl.BlockSpec((FFN_CONV, LANES), col(0)), pl.BlockSpec((FFN_CONV, LANES), col(0)),
                   pl.BlockSpec((1, LANES), col(0)), pl.BlockSpec((1, LANES), col(0))],
        out_shape=[jax.ShapeDtypeStruct((s, D_FF), BF16), jax.ShapeDtypeStruct((s, D_FF), BF16),
                   jax.ShapeDtypeStruct((FFN_CONV, D_FF), F32), jax.ShapeDtypeStruct((FFN_CONV, D_FF), F32),
                   jax.ShapeDtypeStruct((1, D_FF), F32), jax.ShapeDtypeStruct((1, D_FF), F32)],
        scratch_shapes=[pltpu.VMEM((s + SUBLANES, LANES), F32) for _ in range(4)],
        compiler_params=_params(("parallel",)),
    )(u, u, conv_w, conv_w, conv_b, conv_b, d_h)


def _chunk_masks(c):
    row = lax.broadcasted_iota(jnp.int32, (c, c), 0)
    col = lax.broadcasted_iota(jnp.int32, (c, c), 1)
    return row, col


@jax.custom_vjp
def _split_heads(x):
    return tuple(x[:, h * D_HEAD:(h + 1) * D_HEAD] for h in range(N_HEADS))


_split_heads.defvjp(lambda x: (_split_heads(x), None), lambda _, gs: (jnp.concatenate(gs, axis=1),))


@jax.custom_vjp
def _merge_heads(xs):
    return jnp.concatenate(xs, axis=1)


_merge_heads.defvjp(lambda xs: (_merge_heads(xs), None), lambda _, g: (_split_heads(g),))


@jax.custom_vjp
def _split_chunks(x):
    return tuple(x[i * CHUNK:(i + 1) * CHUNK] for i in range(x.shape[0] // CHUNK))


_split_chunks.defvjp(lambda x: (_split_chunks(x), None), lambda _, gs: (jnp.concatenate(gs, axis=0),))


@jax.custom_vjp
def _merge_chunks(xs):
    return jnp.concatenate(xs, axis=0)


_merge_chunks.defvjp(lambda xs: (_merge_chunks(xs), None), lambda _, g: (_split_chunks(g),))


def _blocks(x):
    return [_split_heads(rows) for rows in _split_chunks(x)]


def _per_chunk_rows(per_chunk, rid):
    out = per_chunk[0]
    for i in range(1, len(per_chunk)):
        out = jnp.where(rid >= i * CHUNK, per_chunk[i], out)
    return out


HEADS = range(N_HEADS)
CHUNKS_PER_STEP = 8
ML_CHUNKS_PER_STEP = 1


def _hg_chunk(hq, hf, hi, hgate, l0, l1, nw, sts):
    n = hq.shape[0] // CHUNK
    causal = _chunk_masks(CHUNK)
    causal = causal[1] <= causal[0]
    mx = lax.stop_gradient(jnp.maximum(l0, l1))
    e0 = jnp.exp(l0 - mx)
    e1 = jnp.exp(l1 - mx)
    lb = e0 / (e0 + e1)
    sig = jax.nn.sigmoid(hf)
    lf = jnp.log(lb + (1.0 - lb) * sig)
    k = (1.0 - lb) * jax.nn.sigmoid(-hf)
    q = jax.nn.silu(hq)
    tri = causal.astype(F32)
    b = _merge_chunks(tuple(_dg(tri, rows, 1, 0, HIGHEST) for rows in _split_chunks(lf)))
    rid = lax.broadcasted_iota(jnp.int32, b.shape, 0)
    pick = lambda r: jnp.sum(jnp.where(rid == r, b, 0.0), axis=0, keepdims=True)
    b_last_c = [pick(i * CHUNK + CHUNK - 1) for i in range(n)]
    b_ref = _per_chunk_rows([pick(i * CHUNK + CHUNK // 2 - 1) for i in range(n)], rid)
    b_last = _per_chunk_rows(b_last_c, rid)
    qa = _blocks(q * jnp.exp(b - b_ref))
    ka = _blocks(k * jnp.exp(b_ref - b))
    qe = _blocks(q * jnp.exp(b))
    kd = _blocks(k * jnp.exp(b_last - b))
    decay = [_split_heads(jnp.exp(b_last_c[i])) for i in range(n)]
    v = _blocks(hi)
    chunks = range(n)
    attn = [[jnp.where(causal, _nt(qa[i][h], ka[i][h]), 0.0) for h in HEADS] for i in chunks]
    intra = [[_nn(attn[i][h], v[i][h]) for h in HEADS] for i in chunks]
    kv = [[_tn(v[i][h], kd[i][h]) for h in HEADS] for i in chunks]
    normed = []
    for i in chunks:
        inter = [_nt(qe[i][h], sts[h]) for h in HEADS]
        sts = tuple(decay[i][h] * sts[h] + kv[i][h] for h in HEADS)
        o = [intra[i][h] + inter[h] for h in HEADS]
        normed.append(_merge_heads(tuple(o[h] * lax.rsqrt(jnp.mean(o[h] * o[h], axis=-1, keepdims=True) + LN_EPS)
                                         for h in HEADS)))
    return _merge_chunks(tuple(normed)) * nw * jax.nn.silu(hgate), sts


def _seg(ref, seg):
    return ref[:, seg * D_GROUP:(seg + 1) * D_GROUP]


def _hgrn2_fwd(proj, logits, norm_w):
    s = proj.shape[0]
    rows = CHUNKS_PER_STEP * CHUNK
    nc = s // rows

    def body(p_ref, lg_ref, nw_ref, y_ref, st_out_ref, st_scr):
        @pl.when(pl.program_id(0) == 0)
        def _():
            st_scr[...] = jnp.zeros_like(st_scr)

        sts = tuple(st_scr[h] for h in HEADS)
        y, sts_new = _hg_chunk(_seg(p_ref, 0), _seg(p_ref, 1), _seg(p_ref, 2), _seg(p_ref, 3),
                               lg_ref[0:1, :], lg_ref[1:2, :], nw_ref[...], sts)
        y_ref[...] = y.astype(y_ref.dtype)
        for h in HEADS:
            st_out_ref[h] = sts[h]
            st_scr[h] = sts_new[h]

    return pl.pallas_call(
        body, name="hgrn2_fwd", grid=(nc,),
        in_specs=[pl.BlockSpec((rows, 4 * D_GROUP), lambda c: (c, 0)),
                  pl.BlockSpec((2, D_GROUP), lambda c: (0, 0)),
                  pl.BlockSpec((1, D_GROUP), lambda c: (0, 0))],
        out_specs=[pl.BlockSpec((rows, D_GROUP), lambda c: (c, 0)),
                   pl.BlockSpec((None, N_HEADS, D_HEAD, D_HEAD), lambda c: (c, 0, 0, 0))],
        out_shape=[jax.ShapeDtypeStruct((s, 2 * D_GROUP), BF16),
                   jax.ShapeDtypeStruct((nc, N_HEADS, D_HEAD, D_HEAD), F32)],
        scratch_shapes=[pltpu.VMEM((N_HEADS, D_HEAD, D_HEAD), F32)],
        compiler_params=_params(("arbitrary",)),
    )(proj, logits, norm_w)


def _hgrn2_bwd(proj, logits, norm_w, states, d_y):
    s = proj.shape[0]
    rows = CHUNKS_PER_STEP * CHUNK
    nc = s // rows

    def body(p_ref, lg_ref, nw_ref, st_ref, dy_ref, dp_ref, dl_ref, dnw_ref, dsum_ref, dst_scr):
        @pl.when(pl.program_id(0) == 0)
        def _():
            dst_scr[...] = jnp.zeros_like(dst_scr)
            dl_ref[...] = jnp.zeros_like(dl_ref)
            dnw_ref[...] = jnp.zeros_like(dnw_ref)
            dsum_ref[...] = jnp.zeros_like(dsum_ref)

        _, vjp = jax.vjp(_hg_chunk, _seg(p_ref, 0), _seg(p_ref, 1), _seg(p_ref, 2), _seg(p_ref, 3),
                         lg_ref[0:1, :], lg_ref[1:2, :], nw_ref[...], tuple(st_ref[h] for h in HEADS))
        d_hq, d_hf, d_hi, d_hg, d_l0, d_l1, d_nw, d_sts = vjp((dy_ref[...], tuple(dst_scr[h] for h in HEADS)))
        for seg, val in enumerate((d_hq, d_hf, d_hi, d_hg)):
            dp_ref[:, seg * D_GROUP:(seg + 1) * D_GROUP] = val.astype(dp_ref.dtype)
            dsum_ref[:, seg * D_GROUP:(seg + 1) * D_GROUP] += jnp.sum(val, axis=0, keepdims=True)
        dl_ref[0:1, :] += d_l0
        dl_ref[1:2, :] += d_l1
        dnw_ref[...] += d_nw
        for h in HEADS:
            dst_scr[h] = d_sts[h]

    rev = lambda c: nc - 1 - c
    return pl.pallas_call(
        body, name="hgrn2_bwd", grid=(nc,),
        in_specs=[pl.BlockSpec((rows, 4 * D_GROUP), lambda c: (rev(c), 0)),
                  pl.BlockSpec((2, D_GROUP), lambda c: (0, 0)),
                  pl.BlockSpec((1, D_GROUP), lambda c: (0, 0)),
                  pl.BlockSpec((None, N_HEADS, D_HEAD, D_HEAD), lambda c: (rev(c), 0, 0, 0)),
                  pl.BlockSpec((rows, D_GROUP), lambda c: (rev(c), 0))],
        out_specs=[pl.BlockSpec((rows, 4 * D_GROUP), lambda c: (rev(c), 0)),
                   pl.BlockSpec((2, D_GROUP), lambda c: (0, 0)),
                   pl.BlockSpec((1, D_GROUP), lambda c: (0, 0)),
                   pl.BlockSpec((1, 4 * D_GROUP), lambda c: (0, 0))],
        out_shape=[jax.ShapeDtypeStruct((s, D_IN_MAIN), BF16), jax.ShapeDtypeStruct((2, D_GROUP), F32),
                   jax.ShapeDtypeStruct((1, D_GROUP), F32), jax.ShapeDtypeStruct((1, 4 * D_GROUP), F32)],
        scratch_shapes=[pltpu.VMEM((N_HEADS, D_HEAD, D_HEAD), F32)],
        compiler_params=_params(("arbitrary",)),
    )(proj, logits, norm_w, states, d_y)


def _gate_column(gates, lane, idx):
    return jnp.sum(jnp.where(lane == idx, gates, 0.0), axis=1, keepdims=True)


def _head_layer_norm(h):
    mu = jnp.mean(h, axis=-1, keepdims=True)
    var = jnp.mean(jnp.square(h - mu), axis=-1, keepdims=True)
    return (h - mu) * lax.rsqrt(var + LN_EPS)


def _ml_chunk(qc, kc, v, mo, gates, nw, cts, ns, ms):
    n = qc.shape[0] // CHUNK
    row, col = _chunk_masks(CHUNK)
    mask = col <= row
    eye = col == row
    to_row = lambda t: jnp.sum(jnp.where(eye, t, 0.0), axis=0, keepdims=True)
    q = _blocks(qc * (D_HEAD ** -0.5))
    k = _blocks(kc)
    vs = _blocks(v)
    gate_rows = _split_chunks(gates)
    lane = lax.broadcasted_iota(jnp.int32, gate_rows[0].shape, 1)
    each = [(i, h) for i in range(n) for h in HEADS]
    on_each = lambda f: {ih: f(*ih) for ih in each}
    ig = on_each(lambda i, h: _gate_column(gate_rows[i], lane, h))
    lf = on_each(lambda i, h: jax.nn.log_sigmoid(_gate_column(gate_rows[i], lane, N_HEADS + h)))
    lf_row = on_each(lambda i, h: to_row(lf[i, h]))
    ig_row = on_each(lambda i, h: to_row(ig[i, h]))
    b_col = on_each(lambda i, h: jnp.sum(jnp.where(mask, lf_row[i, h], 0.0), axis=1, keepdims=True))
    b_row = on_each(lambda i, h: jnp.sum(jnp.where(row <= col, lf[i, h], 0.0), axis=0, keepdims=True))
    g = on_each(lambda i, h: jnp.sum(lf[i, h], axis=0, keepdims=True))
    d = on_each(lambda i, h: jnp.where(mask, b_col[i, h] - b_row[i, h] + ig_row[i, h], -jnp.inf))
    a = on_each(lambda i, h: g[i, h] - b_col[i, h] + ig[i, h])
    m_at = {(0, h): ms[h] for h in HEADS}
    for i, h in each:
        m_at[i + 1, h] = lax.stop_gradient(jnp.maximum(g[i, h] + m_at[i, h], jnp.max(a[i, h], axis=0, keepdims=True)))
    inter = on_each(lambda i, h: b_col[i, h] + m_at[i, h])
    m_t = on_each(lambda i, h: lax.stop_gradient(jnp.maximum(inter[i, h], jnp.max(d[i, h], axis=1, keepdims=True))))
    qk = on_each(lambda i, h: _nt(q[i][h], k[i][h]))
    sc = on_each(lambda i, h: qk[i, h] * jnp.exp(d[i, h] - m_t[i, h]))
    w_inter = on_each(lambda i, h: jnp.exp(inter[i, h] - m_t[i, h]))
    sv = on_each(lambda i, h: _nn(sc[i, h], vs[i][h]))
    decay = on_each(lambda i, h: jnp.exp(g[i, h] + m_at[i, h] - m_at[i + 1, h]))
    wk = on_each(lambda i, h: k[i][h] * jnp.exp(a[i, h] - m_at[i + 1, h]))
    kv = on_each(lambda i, h: _tn(vs[i][h], wk[i, h]))
    normed = []
    for i in range(n):
        qc_state = [_nt(q[i][h], cts[h]) for h in HEADS]
        num = [sv[i, h] + w_inter[i, h] * qc_state[h] for h in HEADS]
        den = [jnp.sum(sc[i, h], axis=1, keepdims=True)
               + w_inter[i, h] * jnp.sum(q[i][h] * ns[h], axis=1, keepdims=True) for h in HEADS]
        hh = [num[h] / jnp.maximum(jnp.abs(den[h]), jnp.exp(-m_t[i, h])) for h in HEADS]
        cts = tuple(decay[i, h] * cts[h] + kv[i, h] for h in HEADS)
        ns = tuple(decay[i, h] * ns[h] + jnp.sum(wk[i, h], axis=0, keepdims=True) for h in HEADS)
        normed.append(_merge_heads(tuple(_head_layer_norm(hh[h]) for h in HEADS)))
    y = jax.nn.sigmoid(mo) * (_merge_chunks(tuple(normed)) * nw)
    return y, cts, ns, tuple(m_at[n, h] for h in HEADS)


def _mlstm_fwd(qk, proj, gates, norm_w, y):
    s = proj.shape[0]
    rows = ML_CHUNKS_PER_STEP * CHUNK
    nc = s // rows

    def body(qk_ref, vo_ref, g_ref, nw_ref, _, y_ref, ct_out, n_out, m_out, ct_scr, n_scr, m_scr):
        @pl.when(pl.program_id(0) == 0)
        def _():
            ct_scr[...] = jnp.zeros_like(ct_scr)
            n_scr[...] = jnp.zeros_like(n_scr)
            m_scr[...] = jnp.full(m_scr.shape, NEG_BIG, F32)

        cts = tuple(ct_scr[h] for h in HEADS)
        ns = tuple(n_scr[h] for h in HEADS)
        ms = tuple(m_scr[h] for h in HEADS)
        y, cts_new, ns_new, ms_new = _ml_chunk(_seg(qk_ref, 0), _seg(qk_ref, 1), _seg(vo_ref, 0), _seg(vo_ref, 1),
                                               g_ref[...], nw_ref[...], cts, ns, ms)
        y_ref[...] = y.astype(y_ref.dtype)
        for h in HEADS:
            ct_out[h], n_out[h], m_out[h] = cts[h], ns[h], ms[h]
            ct_scr[h], n_scr[h], m_scr[h] = cts_new[h], ns_new[h], ms_new[h]

    st = lambda r, w: pl.BlockSpec((None, N_HEADS, r, w), lambda c: (c, 0, 0, 0))
    return pl.pallas_call(
        body, name="mlstm_fwd", grid=(nc,),
        in_specs=[pl.BlockSpec((rows, 2 * D_GROUP), lambda c: (c, 0)),
                  pl.BlockSpec((rows, 2 * D_GROUP), lambda c: (c, VO_BLOCK)),
                  pl.BlockSpec((rows, LANES), lambda c: (c, 0)),
                  pl.BlockSpec((1, D_GROUP), lambda c: (0, 0)),
                  pl.BlockSpec(memory_space=pl.ANY)],
        out_specs=[pl.BlockSpec((rows, D_GROUP), lambda c: (c, 1)),
                   st(D_HEAD, D_HEAD), st(1, D_HEAD), st(1, 1)],
        out_shape=[jax.ShapeDtypeStruct(y.shape, y.dtype),
                   jax.ShapeDtypeStruct((nc, N_HEADS, D_HEAD, D_HEAD), F32),
                   jax.ShapeDtypeStruct((nc, N_HEADS, 1, D_HEAD), F32),
                   jax.ShapeDtypeStruct((nc, N_HEADS, 1, 1), F32)],
        input_output_aliases={4: 0},
        scratch_shapes=[pltpu.VMEM((N_HEADS, D_HEAD, D_HEAD), F32), pltpu.VMEM((N_HEADS, 1, D_HEAD), F32),
                        pltpu.VMEM((N_HEADS, 1, 1), F32)],
        compiler_params=_params(("arbitrary",)),
    )(qk, proj, gates, norm_w, y)


def _mlstm_bwd(qk, proj, gates, norm_w, ct_s, n_s, m_s, d_y, d_proj):
    s = proj.shape[0]
    rows = ML_CHUNKS_PER_STEP * CHUNK
    nc = s // rows

    def body(qk_ref, vo_ref, g_ref, nw_ref, ct_ref, n_ref, m_ref, dy_ref, _,
             dp_ref, dqk_ref, dg_ref, dnw_ref, dsum_ref, dct_scr, dn_scr):
        @pl.when(pl.program_id(0) == 0)
        def _():
            dct_scr[...] = jnp.zeros_like(dct_scr)
            dn_scr[...] = jnp.zeros_like(dn_scr)
            dnw_ref[...] = jnp.zeros_like(dnw_ref)
            dsum_ref[...] = jnp.zeros_like(dsum_ref)

        ms = tuple(m_ref[h] for h in HEADS)
        step = lambda *a: _ml_chunk(*a, ms)[:3]
        _, vjp = jax.vjp(step, _seg(qk_ref, 0), _seg(qk_ref, 1), _seg(vo_ref, 0), _seg(vo_ref, 1), g_ref[...],
                         nw_ref[...], tuple(ct_ref[h] for h in HEADS), tuple(n_ref[h] for h in HEADS))
        d_q, d_k, d_v, d_o, d_gates, d_nw, d_cts, d_ns = vjp(
            (dy_ref[...], tuple(dct_scr[h] for h in HEADS), tuple(dn_scr[h] for h in HEADS)))
        dqk_ref[:, 0:D_GROUP] = d_q
        dqk_ref[:, D_GROUP:2 * D_GROUP] = d_k
        for seg, val in enumerate((d_v, d_o)):
            dp_ref[:, seg * D_GROUP:(seg + 1) * D_GROUP] = val.astype(dp_ref.dtype)
            dsum_ref[:, seg * D_GROUP:(seg + 1) * D_GROUP] += jnp.sum(val, axis=0, keepdims=True)
        dg_ref[...] = d_gates
        dnw_ref[...] += d_nw
        for h in HEADS:
            dct_scr[h] = d_cts[h]
            dn_scr[h] = d_ns[h]

    rev = lambda c: nc - 1 - c
    st = lambda r, w: pl.BlockSpec((None, N_HEADS, r, w), lambda c: (rev(c), 0, 0, 0))
    return pl.pallas_call(
        body, name="mlstm_bwd", grid=(nc,),
        in_specs=[pl.BlockSpec((rows, 2 * D_GROUP), lambda c: (rev(c), 0)),
                  pl.BlockSpec((rows, 2 * D_GROUP), lambda c: (rev(c), VO_BLOCK)),
                  pl.BlockSpec((rows, LANES), lambda c: (rev(c), 0)),
                  pl.BlockSpec((1, D_GROUP), lambda c: (0, 0)),
                  st(D_HEAD, D_HEAD), st(1, D_HEAD), st(1, 1),
                  pl.BlockSpec((rows, D_GROUP), lambda c: (rev(c), 1)),
                  pl.BlockSpec(memory_space=pl.ANY)],
        out_specs=[pl.BlockSpec((rows, 2 * D_GROUP), lambda c: (rev(c), VO_BLOCK)),
                   pl.BlockSpec((rows, 2 * D_GROUP), lambda c: (rev(c), 0)),
                   pl.BlockSpec((rows, LANES), lambda c: (rev(c), 0)),
                   pl.BlockSpec((1, D_GROUP), lambda c: (0, 0)),
                   pl.BlockSpec((1, 2 * D_GROUP), lambda c: (0, 0))],
        out_shape=[jax.ShapeDtypeStruct(d_proj.shape, d_proj.dtype), jax.ShapeDtypeStruct((s, 2 * D_GROUP), F32),
                   jax.ShapeDtypeStruct((s, LANES), F32), jax.ShapeDtypeStruct((1, D_GROUP), F32),
                   jax.ShapeDtypeStruct((1, 2 * D_GROUP), F32)],
        input_output_aliases={8: 0},
        scratch_shapes=[pltpu.VMEM((N_HEADS, D_HEAD, D_HEAD), F32), pltpu.VMEM((N_HEADS, 1, D_HEAD), F32)],
        compiler_params=_params(("arbitrary",)),
    )(qk, proj, gates, norm_w, ct_s, n_s, m_s, d_y, d_proj)


LN_TOKENS = 512
ATT_TOKENS = 512


def _proj_res_ln(a, w, xres, g, b, name):
    s, dm = xres.shape
    k = a.shape[1]
    tb = min(LN_TOKENS, s)

    def body(a_ref, w_ref, x_ref, g_ref, b_ref, z_ref, o_ref):
        halves = [slice(0, tb // 2), slice(tb // 2, tb)]
        zs = [ALPHA * x_ref[rows, :] + _nn_raw(a_ref[rows, :], w_ref[...]) for rows in halves]
        for rows, z in zip(halves, zs):
            z_ref[rows, :] = z
            o_ref[rows, :] = _layer_norm(z, g_ref[...], b_ref[...])

    tok = pl.BlockSpec((tb, dm), lambda i: (i, 0))
    vec = pl.BlockSpec((1, dm), lambda i: (0, 0))
    act = jax.ShapeDtypeStruct((s, dm), F32)
    return pl.pallas_call(
        body, name=name, grid=(s // tb,),
        in_specs=[pl.BlockSpec((tb, k), lambda i: (i, 0)), pl.BlockSpec((k, dm), lambda i: (0, 0)), tok, vec, vec],
        out_specs=[tok, tok], out_shape=[act, act], compiler_params=_params(("parallel",)),
    )(a, w, xres, g, b)


def _ln_bwd_proj(d_out, z, g, b, w, name):
    s, dm = z.shape
    k = w.shape[0]
    tb = min(LN_TOKENS, s)

    def body(do_ref, z_ref, g_ref, b_ref, w_ref, dz_ref, da_ref, dg_ref, db_ref):
        @pl.when(pl.program_id(0) == 0)
        def _():
            dg_ref[...] = jnp.zeros_like(dg_ref)
            db_ref[...] = jnp.zeros_like(db_ref)

        halves = [slice(0, tb // 2), slice(tb // 2, tb)]
        d_zs = []
        for rows in halves:
            _, vjp = jax.vjp(_layer_norm, z_ref[rows, :], g_ref[...], b_ref[...])
            d_z, d_g, d_b = vjp(do_ref[rows, :])
            dz_ref[rows, :] = d_z
            dg_ref[...] += d_g
            db_ref[...] += d_b
            d_zs.append(d_z)
        for rows, d_z in zip(halves, d_zs):
            da_ref[rows, :] = _nt_raw(d_z, w_ref[...])

    tok = pl.BlockSpec((tb, dm), lambda i: (i, 0))
    vec = pl.BlockSpec((1, dm), lambda i: (0, 0))
    return pl.pallas_call(
        body, name=name, grid=(s // tb,),
        in_specs=[tok, tok, vec, vec, pl.BlockSpec((k, dm), lambda i: (0, 0))],
        out_specs=[tok, pl.BlockSpec((tb, k), lambda i: (i, 0)), vec, vec],
        out_shape=[jax.ShapeDtypeStruct((s, dm), F32), jax.ShapeDtypeStruct((s, k), F32),
                   jax.ShapeDtypeStruct((1, dm), F32), jax.ShapeDtypeStruct((1, dm), F32)],
        compiler_params=_params(("arbitrary",)),
    )(d_out, z, g, b, w)


def _proj_loss_tail(a, w, xres, g, b, target):
    s, dm = xres.shape
    k = a.shape[1]
    tb = min(ATT_TOKENS, s)

    def loss_fn(z, gg, bb, tgt):
        err = jnp.square(_layer_norm(z, gg, bb) - tgt)
        return 0.5 * jnp.sum(jnp.mean(err, axis=-1, keepdims=True), axis=0, keepdims=True)

    def body(a_ref, w_ref, x_ref, g_ref, b_ref, t_ref, loss_ref, dz_ref, dg_ref, db_ref):
        @pl.when(pl.program_id(0) == 0)
        def _():
            loss_ref[...] = jnp.zeros_like(loss_ref)
            dg_ref[...] = jnp.zeros_like(dg_ref)
            db_ref[...] = jnp.zeros_like(db_ref)

        halves = [slice(0, tb // 2), slice(tb // 2, tb)]
        zs = [ALPHA * x_ref[rows, :] + _nn_raw(a_ref[rows, :], w_ref[...]) for rows in halves]
        for rows, z in zip(halves, zs):
            tgt = t_ref[rows, :]
            loss, vjp = jax.vjp(lambda zz, gg, bb, tgt=tgt: loss_fn(zz, gg, bb, tgt), z, g_ref[...], b_ref[...])
            d_z, d_g, d_b = vjp(jnp.ones((1, 1), F32))
            loss_ref[...] += loss
            dz_ref[rows, :] = d_z
            dg_ref[...] += d_g
            db_ref[...] += d_b

    tok = pl.BlockSpec((tb, dm), lambda i: (i, 0))
    vec = pl.BlockSpec((1, dm), lambda i: (0, 0))
    one = pl.BlockSpec((1, 1), lambda i: (0, 0))
    return pl.pallas_call(
        body, name="ffn_down_loss_tail", grid=(s // tb,),
        in_specs=[pl.BlockSpec((tb, k), lambda i: (i, 0)), pl.BlockSpec((k, dm), lambda i: (0, 0)), tok, vec, vec, tok],
        out_specs=[one, tok, vec, vec],
        out_shape=[jax.ShapeDtypeStruct((1, 1), F32), jax.ShapeDtypeStruct((s, dm), F32),
                   jax.ShapeDtypeStruct((1, dm), F32), jax.ShapeDtypeStruct((1, dm), F32)],
        compiler_params=_params(("arbitrary",)),
    )(a, w, xres, g, b, target)


def _att_heads(qs, ks, vs):
    sc = [_nt(q, k) * (CA_DH ** -0.5) for q, k in zip(qs, ks)]
    p = [jax.nn.softmax(s, axis=-1) for s in sc]
    return tuple(_nn(pp, v) for pp, v in zip(p, vs))


def _head_slices(ref_or_value, offset):
    return tuple(ref_or_value[:, offset + h * CA_DH:offset + (h + 1) * CA_DH] for h in range(CA_HEADS))


def _cross_attention_fwd(x1, kv, wq, wo, g, b):
    s = x1.shape[0]
    tb = min(ATT_TOKENS, s)

    def body(x_ref, kv_ref, wq_ref, wo_ref, g_ref, b_ref, att_ref, z_ref, o_ref):
        x_blk = x_ref[...]
        q = _nn_raw(x_blk, wq_ref[...])
        att = jnp.concatenate(_att_heads(_head_slices(q, 0), _head_slices(kv_ref, 0), _head_slices(kv_ref, D_MODEL)),
                              axis=1)
        att_ref[...] = att.astype(att_ref.dtype)
        z = ALPHA * x_blk + _nn_raw(att, wo_ref[...])
        z_ref[...] = z
        o_ref[...] = _layer_norm(z, g_ref[...], b_ref[...])

    tok = pl.BlockSpec((tb, D_MODEL), lambda i: (i, 0))
    mat = pl.BlockSpec((D_MODEL, D_MODEL), lambda i: (0, 0))
    vec = pl.BlockSpec((1, D_MODEL), lambda i: (0, 0))
    act = jax.ShapeDtypeStruct((s, D_MODEL), F32)
    return pl.pallas_call(
        body, name="cross_attention_fwd", grid=(s // tb,),
        in_specs=[tok, pl.BlockSpec((N_MEM, 2 * D_MODEL), lambda i: (0, 0)), mat, mat, vec, vec],
        out_specs=[tok, tok, tok],
        out_shape=[jax.ShapeDtypeStruct((s, D_MODEL), BF16), act, act],
        compiler_params=_params(("parallel",)),
    )(x1, kv, wq, wo, g, b)


def _cross_attention_bwd(d_x2, x1, z2, kv, wq, wo, g, b):
    s = x1.shape[0]
    tb = min(ATT_TOKENS, s)

    def body(dx2_ref, x_ref, z_ref, kv_ref, wq_ref, wo_ref, g_ref, b_ref,
             dx1_ref, dq_ref, dz_ref, dkv_ref, dg_ref, db_ref):
        @pl.when(pl.program_id(0) == 0)
        def _():
            dkv_ref[...] = jnp.zeros_like(dkv_ref)
            dg_ref[...] = jnp.zeros_like(dg_ref)
            db_ref[...] = jnp.zeros_like(db_ref)

        q = _nn_raw(x_ref[...], wq_ref[...])
        _, ln_vjp = jax.vjp(_layer_norm, z_ref[...], g_ref[...], b_ref[...])
        d_z, d_g, d_b = ln_vjp(dx2_ref[...])
        dg_ref[...] += d_g
        db_ref[...] += d_b
        dz_ref[...] = d_z.astype(dz_ref.dtype)
        d_att = _nt_raw(d_z, wo_ref[...])
        _, vjp = jax.vjp(_att_heads, _head_slices(q, 0), _head_slices(kv_ref, 0), _head_slices(kv_ref, D_MODEL))
        d_qs, d_ks, d_vs = vjp(_head_slices(d_att, 0))
        for h in range(CA_HEADS):
            lo = h * CA_DH
            dkv_ref[:, lo:lo + CA_DH] += d_ks[h]
            dkv_ref[:, D_MODEL + lo:D_MODEL + lo + CA_DH] += d_vs[h]
        d_q = jnp.concatenate(d_qs, axis=1)
        dq_ref[...] = d_q.astype(dq_ref.dtype)
        dx1_ref[...] = ALPHA * d_z + _nt_raw(d_q, wq_ref[...])

    tok = pl.BlockSpec((tb, D_MODEL), lambda i: (i, 0))
    mem = pl.BlockSpec((N_MEM, 2 * D_MODEL), lambda i: (0, 0))
    mat = pl.BlockSpec((D_MODEL, D_MODEL), lambda i: (0, 0))
    vec = pl.BlockSpec((1, D_MODEL), lambda i: (0, 0))
    low = jax.ShapeDtypeStruct((s, D_MODEL), BF16)
    return pl.pallas_call(
        body, name="cross_attention_bwd", grid=(s // tb,),
        in_specs=[tok, tok, tok, mem, mat, mat, vec, vec], out_specs=[tok, tok, tok, mem, vec, vec],
        out_shape=[jax.ShapeDtypeStruct((s, D_MODEL), F32), low, low,
                   jax.ShapeDtypeStruct((N_MEM, 2 * D_MODEL), F32),
                   jax.ShapeDtypeStruct((1, D_MODEL), F32), jax.ShapeDtypeStruct((1, D_MODEL), F32)],
        compiler_params=_params(("arbitrary",)),
    )(d_x2, x1, z2, kv, wq, wo, g, b)


def _local_step(x, mem, target, w, mid_weights=None, ffn_weights=None, down_weights=None, on_ffn_grads=None,
                on_mid_grads=None,
                on_small_grads=None, on_last_grads=None):
    w = dict(w)
    s = x.shape[0]
    tm = min(512, s)
    tt_big = min(1024, s)
    proj, gates = _input_projection(x, w["w_in_t"], w["w_in_gate_t"], w["b_in_main"], w["b_in_gate"],
                                    min(2048, s), 512)
    qk = _ml_conv_fwd(proj, w["ml_conv_w"], w["ml_conv_b"])
    y, hg_states = _hgrn2_fwd(proj, w["hg_lb_logits"], w["hg_norm_w"])
    y, ct_s, n_s, m_s = _mlstm_fwd(qk, proj, gates, w["ml_norm_w"], y)
    if mid_weights is not None:
        w.update(mid_weights(y))
    z1, x1 = _proj_res_ln(y, w["w_out"], x, w["ln1_g"], w["ln1_b"], "out_proj_ln1")
    kv = _matmul_nn(mem, w["ca_wkv"], None, N_MEM, CA_DH, "kv")
    att, z2, x2 = _cross_attention_fwd(x1, kv, w["ca_wq"], w["ca_wo"], w["ln2_g"], w["ln2_b"])
    if ffn_weights is not None:
        w.update(ffn_weights(x2))
    u = _matmul_nt(x2, w["ffn_w_up_t"], min(1024, s), UP_TILE, "ffn_up", out_dtype=BF16)
    hid = _ffn_conv_fwd(u, w["ffn_conv_w"], w["ffn_conv_b"])
    if down_weights is not None:
        w.update(down_weights(hid))
    loss, d_z3, d_ln3_g, d_ln3_b = _proj_loss_tail(hid, w["ffn_w_down"], x2, w["ln3_g"], w["ln3_b"], target)
    grads = {"ln3_g": d_ln3_g, "ln3_b": d_ln3_b}
    grads["ffn_w_down"] = _matmul_tn(hid, d_z3, UP_TILE, D_MODEL, tt_big, "d_w_down")
    d_hid = _matmul_nt(d_z3, w["ffn_w_down"], tm, D_FF, "d_hid", out_dtype=BF16)
    d_ug, d_uv, d_cwg, d_cwv, d_cbg, d_cbv = _ffn_conv_bwd(u, w["ffn_conv_w"], w["ffn_conv_b"], d_hid)
    grads["ffn_conv_w"] = jnp.concatenate([d_cwg, d_cwv], axis=-1)
    grads["ffn_conv_b"] = jnp.concatenate([d_cbg, d_cbv], axis=-1)
    d_w_up = _matmul_tn(d_ug, x2, UP_TILE, D_MODEL, tt_big, "d_w_up_gate", rows=D_UP)
    grads["ffn_w_up"] = _matmul_tn(d_uv, x2, UP_TILE, D_MODEL, tt_big, "d_w_up_val", rows=D_UP, row0=D_FF,
                                   into=d_w_up)
    d_x2 = _matmul_nn_sum([(d_ug, w["ffn_w_up_t"], 0), (d_uv, w["ffn_w_up_t"], D_FF)], d_z3, ALPHA,
                          min(256, s), "d_x2")
    if on_ffn_grads is not None:
        d_x2 = on_ffn_grads(grads, d_x2)
    d_x1, d_q, d_z2, d_kv, grads["ln2_g"], grads["ln2_b"] = _cross_attention_bwd(
        d_x2, x1, z2, kv, w["ca_wq"], w["ca_wo"], w["ln2_g"], w["ln2_b"])
    grads["ca_wo"] = _matmul_tn(att, d_z2, D_MODEL, D_MODEL, tt_big, "d_ca_wo")
    grads["ca_wq"] = _matmul_tn(x1, d_q, D_MODEL, D_MODEL, tt_big, "d_ca_wq")
    grads["ca_wkv"] = _matmul_tn(mem, d_kv, D_MODEL, CA_DH, N_MEM, "d_ca_wkv", shards=N_DEV, group=N_DEV)
    d_z1, d_y, grads["ln1_g"], grads["ln1_b"] = _ln_bwd_proj(d_x1, z1, w["ln1_g"], w["ln1_b"], w["w_out"],
                                                             "ln1_bwd_out_proj")
    grads["w_out"] = _matmul_tn(y, d_z1, D_MODEL, D_MODEL, tt_big, "d_w_out")
    if on_mid_grads is not None:
        d_y = on_mid_grads(grads, d_y)
    d_proj, grads["hg_lb_logits"], grads["hg_norm_w"], db_hg = _hgrn2_bwd(
        proj, w["hg_lb_logits"], w["hg_norm_w"], hg_states, d_y)
    d_proj, d_qk, d_gates, grads["ml_norm_w"], db_vo = _mlstm_bwd(
        qk, proj, gates, w["ml_norm_w"], ct_s, n_s, m_s, d_y, d_proj)
    d_proj, grads["ml_conv_w"], grads["ml_conv_b"], db_qk = _ml_conv_bwd(
        proj, w["ml_conv_w"], w["ml_conv_b"], d_qk, d_proj)
    grads["b_in_main"] = jnp.concatenate([db_hg, db_qk, db_vo], axis=-1)
    grads["b_in_gate"] = jnp.sum(d_gates, axis=0, keepdims=True)
    if on_small_grads is not None:
        d_proj = on_small_grads(grads, loss, d_proj)
    grads["w_in"] = _input_projection_grads(d_proj, d_gates, x, min(1024, D_IN_MAIN), tt_big)
    if on_last_grads is not None:
        d_z1 = on_last_grads(grads, d_z1)
    grad_x = _matmul_nn_sum([(d_proj, w["w_in_t"], 0), (d_gates, w["w_in_gate_t"], 0)], d_z1, ALPHA, tm, "d_x")
    return loss, grad_x, grads


HBM_SPEC = pl.BlockSpec(memory_space=pltpu.HBM)


def _coords():
    return lax.axis_index("x"), lax.axis_index("y"), lax.axis_index("c")


def _other_chips(x, y):
    return [(1 - x, y), (x, 1 - y), (1 - x, 1 - y)]


def _my_slot():
    x, y, c = _coords()
    return 4 * x + 2 * y + c


SEM_SPEC = pl.BlockSpec(memory_space=pltpu.SEMAPHORE)
ANY_SPEC = pl.BlockSpec(memory_space=pl.ANY)
SIDE_EFFECT = pltpu.SideEffectType.DATAFLOW_SIDE_EFFECTING


def _peer(x, y, c, d):
    flip = lambda v, bit: 1 - v if bit else v
    p = (flip(x, d & 4), flip(y, d & 2), flip(c, d & 1))
    return p, 4 * p[0] + 2 * p[1] + p[2]


def _direct_copies(gather, src_refs, land_refs, send_sems, recv_sems):
    x, y, c = _coords()
    me = 4 * x + 2 * y + c
    copies = []
    for a in range(len(src_refs)):
        for d in range(1, N_DEV):
            peer, peer_slot = _peer(x, y, c, d)
            copies.append(pltpu.make_async_remote_copy(
                src_ref=src_refs[a] if gather else src_refs[a].at[peer_slot],
                dst_ref=land_refs[a].at[me] if gather else land_refs[a].at[d - 1],
                send_sem=send_sems.at[7 * a + d - 1], recv_sem=recv_sems.at[7 * a + d - 1],
                device_id=peer, device_id_type=MESH))
    return copies


def _hbm(t):
    return pltpu.HBM(t.shape, t.dtype)


def _chip_copies(src_refs, land_refs, send_sems, recv_sems):
    x, y, c = _coords()
    me = 4 * x + 2 * y + c
    targets = [(x, y, 1 - c)] + [(cx, cy, c) for cx, cy in _other_chips(x, y)]
    return [pltpu.make_async_remote_copy(
        src_ref=src_refs[a], dst_ref=land_refs[a].at[me], send_sem=send_sems.at[4 * a + k],
        recv_sem=recv_sems.at[4 * a + k], device_id=target, device_id_type=MESH)
        for a in range(len(src_refs)) for k, target in enumerate(targets)]


def _forward_copies(land_refs, send_sems, recv_sems):
    x, y, c = _coords()
    return [pltpu.make_async_remote_copy(
        src_ref=land_refs[a].at[4 * cx + 2 * cy + c], dst_ref=land_refs[a].at[4 * cx + 2 * cy + c],
        send_sem=send_sems.at[3 * a + j], recv_sem=recv_sems.at[3 * a + j],
        device_id=(x, y, 1 - c), device_id_type=MESH)
        for a in range(len(land_refs)) for j, (cx, cy) in enumerate(_other_chips(x, y))]


def _split_copy_start(make_copies, n_sems, operands, through, name):
    n_ops = len(operands)

    def body(*refs):
        for cp in make_copies(refs[:n_ops], refs[n_ops + 1], refs[n_ops + 2]):
            cp.start()

    ins = [pltpu.with_memory_space_constraint(t, pltpu.HBM) for t in (*operands, through)]
    sems = pltpu.SemaphoreType.DMA((n_sems,))
    res = pl.pallas_call(
        body, name=name, out_shape=(sems, sems, *[_hbm(t) for t in ins]),
        in_specs=[HBM_SPEC] * (n_ops + 1), out_specs=(SEM_SPEC, SEM_SPEC, *[HBM_SPEC] * (n_ops + 1)),
        input_output_aliases={i: 2 + i for i in range(n_ops + 1)},
        compiler_params=pltpu.CompilerParams(has_side_effects=SIDE_EFFECT),
    )(*ins)
    return (res[0], res[1], list(res[2:2 + n_ops])), res[2 + n_ops]


def _split_copy_wait(make_copies, started, after, name):
    send_sems, recv_sems, operands = started
    n_ops = len(operands)
    after = list(after) if isinstance(after, (list, tuple)) else [after]

    def body(*refs):
        for cp in make_copies(refs[:n_ops], refs[n_ops], refs[n_ops + 1]):
            cp.wait_send()
            cp.wait_recv()

    res = pl.pallas_call(
        body, name=name, out_shape=tuple(_hbm(t) for t in operands),
        in_specs=[HBM_SPEC] * n_ops + [SEM_SPEC, SEM_SPEC] + [ANY_SPEC] * len(after),
        out_specs=tuple([HBM_SPEC] * n_ops), input_output_aliases={i: i for i in range(n_ops)},
        compiler_params=pltpu.CompilerParams(has_side_effects=SIDE_EFFECT),
    )(*operands, send_sems, recv_sems, *after)
    return list(res)


def _halves(make_copies, na):
    return lambda refs, send_sems, recv_sems: make_copies(refs[:na], refs[na:], send_sems, recv_sems)


def _direct_start(gather, arrays, through, name):
    na = len(arrays)
    lands = [lax.empty((N_DEV,) + t.shape if gather else (N_DEV - 1,) + t.shape[1:], t.dtype) for t in arrays]
    return _split_copy_start(_halves(functools.partial(_direct_copies, gather), na), 7 * na, [*arrays, *lands],
                             through, name)


def _direct_wait(gather, started, after, name):
    na = len(started[2]) // 2
    operands = _split_copy_wait(_halves(functools.partial(_direct_copies, gather), na), started, after, name)
    return operands[:na], operands[na:]


def _slot_copies(land_refs, send_sems, recv_sems):
    x, y, c = _coords()
    me = 4 * x + 2 * y + c
    return [pltpu.make_async_remote_copy(
        src_ref=land.at[me], dst_ref=land.at[me], send_sem=send_sems.at[7 * a + d - 1],
        recv_sem=recv_sems.at[7 * a + d - 1], device_id=_peer(x, y, c, d)[0], device_id_type=MESH)
        for a, land in enumerate(land_refs) for d in range(1, N_DEV)]


def _own_slot_filled(block):
    return lax.dynamic_update_index_in_dim(lax.empty((N_DEV,) + block.shape, block.dtype), block, _my_slot(), 0)


def _two_level_gather(shards, glue, name):
    na = len(shards)
    lands = [lax.empty((N_DEV,) + t.shape, t.dtype) for t in shards]
    nothing = jnp.zeros((SUBLANES, LANES), F32)
    started, _ = _split_copy_start(_halves(_chip_copies, na), 4 * na, [*shards, *lands], nothing, name + "_start")
    operands = _split_copy_wait(_halves(_chip_copies, na), started, glue, name + "_wait")
    started, mine = _split_copy_start(_forward_copies, 3 * na, operands[na:], operands[0], name + "_forward_start")
    lands = _split_copy_wait(_forward_copies, started, mine, name + "_forward_wait")
    return [lax.dynamic_update_index_in_dim(land, own, _my_slot(), 0)
            for own, land in zip([mine, *operands[1:na]], lands)]


def _row_tile(rows):
    for t in (256, 176, 128):
        if rows % t == 0 and rows > t:
            return t
    return rows


def _adamw_math(g, w, m, v):
    m_new = ADAM_B1 * m + (1.0 - ADAM_B1) * g
    v_new = ADAM_B2 * v + (1.0 - ADAM_B2) * jnp.square(g)
    m_hat = m_new / (1.0 - ADAM_B1 ** ADAM_STEP)
    v_hat = v_new / (1.0 - ADAM_B2 ** ADAM_STEP)
    delta = -ADAM_LR * (m_hat / (jnp.sqrt(v_hat) + ADAM_EPS) + ADAM_WD * w)
    return delta, m_new, v_new


def _adamw_sharded(chip, sums, got, w, m, v, name):
    r, c = w.shape
    tr = _row_tile(r)
    n_got = got.shape[0]

    def body(chip_ref, s_ref, g_ref, w_ref, m_ref, v_ref, go_ref, d_ref, nm_ref, nv_ref):
        g = s_ref[...].astype(F32)
        for i in range(n_got):
            g = g + g_ref[i].astype(F32)
        delta, m_new, v_new = _adamw_math(g, w_ref[...], m_ref[...], v_ref[...])
        go_ref[...] = g
        d_ref[...] = delta
        nm_ref[...] = m_new
        nv_ref[...] = v_new

    blk = pl.BlockSpec((tr, c), lambda i, chip_ref: (i, 0))
    out = jax.ShapeDtypeStruct((r, c), F32)
    return pl.pallas_call(
        body, name=name,
        grid_spec=pltpu.PrefetchScalarGridSpec(
            num_scalar_prefetch=1, grid=(r // tr,),
            in_specs=[pl.BlockSpec((None, tr, c), lambda i, chip_ref: (chip_ref[0], i, 0)),
                      pl.BlockSpec((n_got, tr, c), lambda i, chip_ref: (0, i, 0)), blk, blk, blk],
            out_specs=[blk, blk, blk, blk]),
        out_shape=[out, out, out, out],
        compiler_params=_params(("parallel",)),
    )(chip, sums, got, w, m, v)


def _adamw_replicated(parts, w, m, v):
    p, r, c = parts.shape
    names = SMALL_NAMES
    shapes = [w[n].shape for n in names]

    def body(*refs):
        p_ref = refs[0]
        ins = refs[1:1 + 3 * len(names)]
        outs = refs[1 + 3 * len(names):-2]
        loss_ref, sum_scr = refs[-2], refs[-1]
        total = p_ref[0]
        for i in range(1, p):
            total = total + p_ref[i]
        sum_scr[...] = total
        for k, n in enumerate(names):
            w_ref, m_ref, v_ref = ins[3 * k:3 * k + 3]
            g_ref, d_ref, nm_ref, nv_ref = outs[4 * k:4 * k + 4]
            for row, lane0, width, src_row in _small_pieces(n, shapes[k]):
                here = (slice(row, row + 1), slice(lane0, lane0 + width))
                g = sum_scr[src_row:src_row + 1, 0:width]
                delta, m_new, v_new = _adamw_math(g, w_ref[here], m_ref[here], v_ref[here])
                g_ref[here] = g
                d_ref[here] = delta
                nm_ref[here] = m_new
                nv_ref[here] = v_new
        loss_ref[...] = sum_scr[SMALL_LOSS_ROW:SMALL_LOSS_ROW + 1, 0:1]

    whole = lambda shape: pl.BlockSpec(shape, lambda i: (0,) * len(shape))
    args = [parts] + [t[n] for n in names for t in (w, m, v)]
    out_shape = [jax.ShapeDtypeStruct(s, F32) for s in shapes for _ in range(4)] + [jax.ShapeDtypeStruct((1, 1), F32)]
    res = pl.pallas_call(
        body, name="adamw_replicated", grid=(1,),
        in_specs=[whole(t.shape) for t in args], out_specs=[whole(s.shape) for s in out_shape],
        out_shape=out_shape, scratch_shapes=[pltpu.VMEM((r, c), F32)],
        compiler_params=_params(("arbitrary",)),
    )(*args)
    results = [{n: res[4 * k + j] for k, n in enumerate(names)} for j in range(4)]
    return results, res[-1]


SHARDED_NAMES = ("w_in", "ml_conv_w", "w_out", "ca_wq", "ca_wkv", "ca_wo", "ffn_w_up", "ffn_conv_w", "ffn_w_down")
SMALL_NAMES = ("b_in", "hg_lb_logits", "hg_norm_w", "ml_conv_b", "ml_norm_w", "ln1_g", "ln1_b",
               "ln2_g", "ln2_b", "ffn_conv_b", "ln3_g", "ln3_b")
WEIGHT_NAMES = ("w_in", "b_in", "hg_lb_logits", "hg_norm_w", "ml_conv_w", "ml_conv_b", "ml_norm_w", "w_out",
                "ln1_g", "ln1_b", "ca_wq", "ca_wkv", "ca_wo", "ln2_g", "ln2_b", "ffn_w_up", "ffn_conv_w",
                "ffn_conv_b", "ffn_w_down", "ln3_g", "ln3_b")
PAD_TO = {"ffn_conv_w": UP_SHARD_P}
SMALL_ROWS = 24
SMALL_W = D_MODEL
SMALL_SHAPES = {"b_in": (1, D_IN), "hg_lb_logits": (2, D_GROUP), "hg_norm_w": (1, D_GROUP),
                "ml_conv_b": (1, 2 * D_GROUP), "ml_norm_w": (1, D_GROUP), "ln1_g": (1, D_MODEL), "ln1_b": (1, D_MODEL),
                "ln2_g": (1, D_MODEL), "ln2_b": (1, D_MODEL), "ffn_conv_b": (1, D_UP), "ln3_g": (1, D_MODEL),
                "ln3_b": (1, D_MODEL)}


def _shard_2d(name, block):
    t = block[0]
    if name in PAD_TO:
        t = jnp.pad(t, ((0, 0), (0, PAD_TO[name] - t.shape[1])))
    return t


TRANSPOSED = ("w_in", "ffn_w_up")


def _update_shard(name, block):
    if name not in TRANSPOSED:
        return _shard_2d(name, block)
    return jnp.transpose(block, (0, 2, 1))[0]


def _shard_like(name, t, like):
    if name in TRANSPOSED:
        out = jnp.transpose(t[None], (0, 2, 1))
        return with_layout_constraint(out, Layout(major_to_minor=(0, 2, 1))) if name == "ffn_w_up" else out
    return t[:, :like.shape[2]][None]


def _pad_cols(t, width):
    return jnp.pad(t, ((0, 0), (0, width - t.shape[1])))


FIRST_NAMES = ("w_in", "ml_conv_w")
FFN_NAMES = ("ffn_w_up", "ffn_w_down", "ffn_conv_w")
MID_NAMES = ("ca_wo", "ca_wq", "ca_wkv", "w_out")


def _first_weights(g, small):
    w = dict(small)
    w["w_in_t"] = g["w_in"].reshape(D_IN, D_MODEL)
    w["w_in_gate_t"] = jnp.pad(w["w_in_t"][D_IN_MAIN:], ((0, LANES - (D_IN - D_IN_MAIN)), (0, 0)))
    w["b_in_main"] = small["b_in"][:, :D_IN_MAIN]
    w["b_in_gate"] = _pad_cols(small["b_in"][:, D_IN_MAIN:], LANES)
    w["ml_conv_w"] = jnp.transpose(g["ml_conv_w"], (1, 0, 2)).reshape(ML_CONV, 2 * D_GROUP)
    return w


def _mid_weights(g):
    w = {n: g[n].reshape(D_MODEL, D_MODEL) for n in ("w_out", "ca_wq", "ca_wo")}
    w["ca_wkv"] = g["ca_wkv"]
    return w


FFN_UP_NAMES = ("ffn_w_up", "ffn_conv_w")
FFN_DOWN_NAMES = ("ffn_w_down",)


def _ffn_up_weights(g, small):
    w = {"ffn_w_up_t": g["ffn_w_up"].reshape(D_UP, D_MODEL)}
    w["ffn_conv_w"] = jnp.transpose(g["ffn_conv_w"][:, :, :UP_SHARD], (1, 0, 2)).reshape(FFN_CONV, D_UP)
    w["ffn_conv_b"] = small["ffn_conv_b"].reshape(1, D_UP)
    return w


def _ffn_down_weights(g):
    return {"ffn_w_down": g["ffn_w_down"].reshape(D_FF, D_MODEL)}


def _owner_stack(n, grads):
    if n == "w_in":
        return grads[n].reshape(N_DEV, W_IN_SHARD, D_MODEL)
    if n == "ffn_w_up":
        return grads[n].reshape(N_DEV, UP_SHARD, D_MODEL)
    if n in ("w_out", "ca_wq", "ca_wo"):
        return grads[n].reshape(N_DEV, D_MODEL // N_DEV, D_MODEL)
    if n == "ffn_w_down":
        return grads[n].reshape(N_DEV, D_FF // N_DEV, D_MODEL)
    if n == "ml_conv_w":
        return jnp.transpose(grads[n].reshape(ML_CONV, N_DEV, LANES), (1, 0, 2))
    if n == "ffn_conv_w":
        shards = jnp.transpose(grads[n].reshape(FFN_CONV, N_DEV, UP_SHARD), (1, 0, 2))
        return jnp.pad(shards, ((0, 0), (0, 0), (0, UP_SHARD_P - UP_SHARD)))
    return grads[n]


def _small_grads(grads):
    out = {n: grads[n] for n in SMALL_NAMES if n in grads}
    out["b_in"] = jnp.concatenate([grads["b_in_main"], grads["b_in_gate"][:, :D_IN - D_IN_MAIN]], axis=1)
    return out


def _small_rows(shape):
    return shape[0] if shape[1] <= SMALL_W else -(-shape[1] // SMALL_W)


SMALL_BASE = {n: sum(_small_rows(SMALL_SHAPES[k]) for k in SMALL_NAMES[:i]) for i, n in enumerate(SMALL_NAMES)}
SMALL_LOSS_ROW = sum(_small_rows(SMALL_SHAPES[n]) for n in SMALL_NAMES)
assert SMALL_LOSS_ROW < SMALL_ROWS


def _small_pieces(name, shape):
    base = SMALL_BASE[name]
    if shape[1] <= SMALL_W:
        return [(i, 0, shape[1], base + i) for i in range(shape[0])]
    return [(0, k * SMALL_W, min(SMALL_W, shape[1] - k * SMALL_W), base + k) for k in range(_small_rows(shape))]


def _pack_small(p, loss):
    rows = []
    for n in SMALL_NAMES:
        t = p[n]
        nrows = _small_rows(t.shape)
        if t.shape[1] <= SMALL_W:
            rows.append(_pad_cols(t, SMALL_W))
        else:
            rows.append(_pad_cols(t, nrows * SMALL_W).reshape(nrows, SMALL_W))
    rows.append(_pad_cols(loss, SMALL_W))
    slab = jnp.concatenate(rows, axis=0)
    return jnp.pad(slab, ((0, SMALL_ROWS - slab.shape[0]), (0, 0)))


def kernel(x, mem, w_in, b_in, hg_lb_logits, hg_norm_w, ml_conv_w, ml_conv_b, ml_norm_w, w_out, ln1_g, ln1_b, ca_wq, ca_wkv, ca_wo, ln2_g, ln2_b, ffn_w_up, ffn_conv_w, ffn_conv_b, ffn_w_down, ln3_g, ln3_b, loss_target, m_w_in, m_b_in, m_hg_lb_logits, m_hg_norm_w, m_ml_conv_w, m_ml_conv_b, m_ml_norm_w, m_w_out, m_ln1_g, m_ln1_b, m_ca_wq, m_ca_wkv, m_ca_wo, m_ln2_g, m_ln2_b, m_ffn_w_up, m_ffn_conv_w, m_ffn_conv_b, m_ffn_w_down, m_ln3_g, m_ln3_b, v_w_in, v_b_in, v_hg_lb_logits, v_hg_norm_w, v_ml_conv_w, v_ml_conv_b, v_ml_norm_w, v_w_out, v_ln1_g, v_ln1_b, v_ca_wq, v_ca_wkv, v_ca_wo, v_ln2_g, v_ln2_b, v_ffn_w_up, v_ffn_conv_w, v_ffn_conv_b, v_ffn_w_down, v_ln3_g, v_ln3_b):
    params = dict(w_in=w_in, b_in=b_in, hg_lb_logits=hg_lb_logits, hg_norm_w=hg_norm_w, ml_conv_w=ml_conv_w,
                  ml_conv_b=ml_conv_b, ml_norm_w=ml_norm_w, w_out=w_out, ln1_g=ln1_g, ln1_b=ln1_b, ca_wq=ca_wq,
                  ca_wkv=ca_wkv, ca_wo=ca_wo, ln2_g=ln2_g, ln2_b=ln2_b, ffn_w_up=ffn_w_up, ffn_conv_w=ffn_conv_w,
                  ffn_conv_b=ffn_conv_b, ffn_w_down=ffn_w_down, ln3_g=ln3_g, ln3_b=ln3_b)
    mom1 = dict(w_in=m_w_in, b_in=m_b_in, hg_lb_logits=m_hg_lb_logits, hg_norm_w=m_hg_norm_w,
                ml_conv_w=m_ml_conv_w, ml_conv_b=m_ml_conv_b, ml_norm_w=m_ml_norm_w, w_out=m_w_out, ln1_g=m_ln1_g,
                ln1_b=m_ln1_b, ca_wq=m_ca_wq, ca_wkv=m_ca_wkv, ca_wo=m_ca_wo, ln2_g=m_ln2_g, ln2_b=m_ln2_b,
                ffn_w_up=m_ffn_w_up, ffn_conv_w=m_ffn_conv_w, ffn_conv_b=m_ffn_conv_b, ffn_w_down=m_ffn_w_down,
                ln3_g=m_ln3_g, ln3_b=m_ln3_b)
    mom2 = dict(w_in=v_w_in, b_in=v_b_in, hg_lb_logits=v_hg_lb_logits, hg_norm_w=v_hg_norm_w,
                ml_conv_w=v_ml_conv_w, ml_conv_b=v_ml_conv_b, ml_norm_w=v_ml_norm_w, w_out=v_w_out, ln1_g=v_ln1_g,
                ln1_b=v_ln1_b, ca_wq=v_ca_wq, ca_wkv=v_ca_wkv, ca_wo=v_ca_wo, ln2_g=v_ln2_g, ln2_b=v_ln2_b,
                ffn_w_up=v_ffn_w_up, ffn_conv_w=v_ffn_conv_w, ffn_conv_b=v_ffn_conv_b, ffn_w_down=v_ffn_w_down,
                ln3_g=v_ln3_g, ln3_b=v_ln3_b)

    x_idx, y_idx, c_idx = _coords()
    as_index = lambda v: jnp.reshape(v, (1,)).astype(jnp.int32)
    me = as_index(4 * x_idx + 2 * y_idx + c_idx)
    small_params = {n: params[n] for n in SMALL_NAMES}

    shards = {n: _update_shard(n, params[n]) for n in SHARDED_NAMES}
    m_shards = {n: _update_shard(n, mom1[n]) for n in SHARDED_NAMES}
    v_shards = {n: _update_shard(n, mom2[n]) for n in SHARDED_NAMES}
    outgoing = {n: shards[n] if "conv" in n else shards[n].astype(BF16) for n in SHARDED_NAMES}
    to_send = lambda names: [outgoing[n] for n in names]
    late = {n: _own_slot_filled(outgoing[n]) for n in SHARDED_NAMES if n not in FIRST_NAMES}
    glue = [*m_shards.values(), *v_shards.values(), *shards.values(), *late.values()]
    first = dict(zip(FIRST_NAMES, _two_level_gather(to_send(FIRST_NAMES), glue, "weights_gather_first")))

    def start_gather(names, through, tag):
        return _split_copy_start(_slot_copies, 7 * len(names), [late[n] for n in names], through,
                                 "weights_gather_start_" + tag)

    mid_started, through = start_gather(MID_NAMES, first["w_in"], "mid")
    ffn_started, through = start_gather(FFN_UP_NAMES, through, "ffn_up")
    down_started, first["w_in"] = start_gather(FFN_DOWN_NAMES, through, "ffn_down")

    def gathered_weights(names, started, after, tag):
        return dict(zip(names, _split_copy_wait(_slot_copies, started, after, "weights_gather_wait_" + tag)))

    started = {}

    def start_group(names, tag):
        def hook(grads, through):
            stacks = [_owner_stack(n, grads).astype(BF16) for n in names]
            started[tag], through = _direct_start(False, stacks, through, "grads_start_" + tag)
            return through
        return hook

    def start_small(grads, loss, through):
        started["small"], through = _direct_start(True, [_pack_small(_small_grads(grads), loss)], through,
                                                  "small_gather_start")
        return through

    loss, grad_x, grads = _local_step(
        x[0], mem[0], loss_target[0], _first_weights(first, small_params),
        lambda y: _mid_weights(gathered_weights(MID_NAMES, mid_started, y, "mid")),
        lambda x2: _ffn_up_weights(gathered_weights(FFN_UP_NAMES, ffn_started, x2, "ffn_up"), small_params),
        lambda hid: _ffn_down_weights(gathered_weights(FFN_DOWN_NAMES, down_started, hid, "ffn_down")),
        start_group(FFN_NAMES, "ffn"), start_group(MID_NAMES, "mid"), start_small, start_group(FIRST_NAMES, "last"))

    updated, sharded_out = {}, {}

    def update_group(names, tag, after):
        stacks, lands = _direct_wait(False, started[tag], after, "grads_wait_" + tag)
        for n, st, land in zip(names, stacks, lands):
            updated[n] = _adamw_sharded(me, st, land, shards[n], m_shards[n], v_shards[n], "adamw_" + n)
            sharded_out[n] = [_shard_like(n, t, params[n]) for t in updated[n]]

    update_group(FFN_NAMES, "ffn", grad_x)
    update_group(MID_NAMES, "mid", grad_x)
    own_small, small_lands = _direct_wait(True, started["small"], grad_x, "small_gather_wait")
    small_parts = lax.dynamic_update_index_in_dim(small_lands[0], own_small[0], me[0], 0)
    small_out, total_loss = _adamw_replicated(small_parts, small_params, {n: mom1[n] for n in SMALL_NAMES},
                                              {n: mom2[n] for n in SMALL_NAMES})
    done = [t for n in FFN_NAMES + MID_NAMES for t in updated[n]]
    done += [t for small in small_out for t in small.values()]
    update_group(FIRST_NAMES, "last", done)

    outs = []
    for k, small in enumerate(small_out):
        outs.extend(sharded_out[n][k] if n in sharded_out else small[n] for n in WEIGHT_NAMES)
    return (total_loss[0, 0], grad_x[None], *outs)
```

```python
import functools
import math

import jax
import jax.numpy as jnp
from jax import lax
from jax.experimental import pallas as pl
from jax.experimental.layout import Layout, with_layout_constraint
from jax.experimental.pallas import tpu as pltpu

F32 = jnp.float32
BF16 = jnp.bfloat16
HIGHEST = lax.Precision.HIGHEST
MESH = pl.DeviceIdType.MESH

N_DEV = 8
D_MODEL = 1024
N_MEM = 256
N_HEADS = 4
D_HEAD = 128
D_GROUP = N_HEADS * D_HEAD
CHUNK = 64
ML_CONV = 4
FFN_CONV = 3
D_FF = 2816
D_UP = 2 * D_FF
CA_HEADS = 4
CA_DH = D_MODEL // CA_HEADS
LANES = 128
SUBLANES = 8
D_IN = 8 * D_GROUP + 2 * N_HEADS
D_IN_MAIN = 8 * D_GROUP
W_IN_SHARD = D_IN // N_DEV
UP_SHARD = D_UP // N_DEV
UP_SHARD_P = 768
UP_TILE = D_UP // 4
ALPHA = 2.0 ** 0.25
LN_EPS = 1e-5
NEG_BIG = -1e30
ADAM_LR = 0.001
ADAM_B1 = 0.9
ADAM_B2 = 0.999
ADAM_EPS = 1e-08
ADAM_WD = 0.01
ADAM_STEP = 10
VMEM_LIMIT = 56 * 1024 * 1024

SEG_MQ = 4 * D_GROUP // LANES
VO_BLOCK = 3


def _params(sem):
    return pltpu.CompilerParams(dimension_semantics=sem, vmem_limit_bytes=VMEM_LIMIT)


def _dg(a, b, ca, cb, precision=None):
    return lax.dot_general(a, b, (((ca,), (cb,)), ((), ())), precision=precision,
                           preferred_element_type=F32)


def _nn_raw(a, b):
    return _dg(a.astype(BF16), b.astype(BF16), 1, 0)


def _nt_raw(a, b):
    return _dg(a.astype(BF16), b.astype(BF16), 1, 1)


def _tn_raw(a, b):
    return _dg(a.astype(BF16), b.astype(BF16), 0, 0)


@jax.custom_vjp
def _nn(a, b):
    return _nn_raw(a, b)


_nn.defvjp(lambda a, b: (_nn_raw(a, b), (a, b)),
           lambda res, g: (_nt_raw(g, res[1]), _tn_raw(res[0], g)))


@jax.custom_vjp
def _nt(a, b):
    return _nt_raw(a, b)


_nt.defvjp(lambda a, b: (_nt_raw(a, b), (a, b)),
           lambda res, g: (_nn_raw(g, res[1]), _tn_raw(g, res[0])))


@jax.custom_vjp
def _tn(a, b):
    return _tn_raw(a, b)


_tn.defvjp(lambda a, b: (_tn_raw(a, b), (a, b)),
           lambda res, g: (_nt_raw(res[1], g), _nn_raw(res[0], g)))


def _layer_norm(z, g, b):
    mu = jnp.mean(z, axis=-1, keepdims=True)
    var = jnp.mean(jnp.square(z - mu), axis=-1, keepdims=True)
    return (z - mu) * lax.rsqrt(var + LN_EPS) * g + b


def _matmul_nn(a, w, bias, tm, tn, name, out_dtype=F32):
    m, k = a.shape
    if w.ndim == 3:
        n = w.shape[0] * w.shape[2]
        assert tn == w.shape[2]
        w_spec = pl.BlockSpec((None, k, tn), lambda i, j: (j, 0, 0))
    else:
        n = w.shape[1]
        w_spec = pl.BlockSpec((k, tn), lambda i, j: (0, j))

    def body(*refs):
        a_ref, w_ref = refs[0], refs[1]
        o_ref = refs[-1]
        acc = _nn_raw(a_ref[...], w_ref[...])
        if bias is not None:
            acc = acc + refs[2][...]
        o_ref[...] = acc.astype(o_ref.dtype)

    in_specs = [pl.BlockSpec((tm, k), lambda i, j: (i, 0)), w_spec]
    args = [a, w]
    if bias is not None:
        in_specs.append(pl.BlockSpec((1, tn), lambda i, j: (0, j)))
        args.append(bias)
    return pl.pallas_call(
        body, name=name, grid=(m // tm, n // tn), in_specs=in_specs,
        out_specs=pl.BlockSpec((tm, tn), lambda i, j: (i, j)),
        out_shape=jax.ShapeDtypeStruct((m, n), out_dtype),
        compiler_params=_params(("parallel", "parallel")),
    )(*args)


def _matmul_nt(d, w, tm, tk, name, k_out=None, bias=None, out_dtype=F32):
    m, n = d.shape
    k = k_out or w.shape[0]

    def body(*refs):
        acc = _nt_raw(refs[0][...], refs[1][...])
        if bias is not None:
            acc = acc + refs[2][...]
        refs[-1][...] = acc.astype(refs[-1].dtype)

    in_specs = [pl.BlockSpec((tm, n), lambda i, j: (i, 0)), pl.BlockSpec((tk, n), lambda i, j: (j, 0))]
    args = [d, w]
    if bias is not None:
        in_specs.append(pl.BlockSpec((1, tk), lambda i, j: (0, j)))
        args.append(bias)
    return pl.pallas_call(
        body, name=name, grid=(m // tm, k // tk), in_specs=in_specs,
        out_specs=pl.BlockSpec((tm, tk), lambda i, j: (i, j)),
        out_shape=jax.ShapeDtypeStruct((m, k), out_dtype),
        compiler_params=_params(("parallel", "parallel")),
    )(*args)


def _input_projection(x, w_t, w_gate_t, b_main, b_gate, tm, tk):
    m, n = x.shape

    def body(x_ref, w_ref, wg_ref, b_ref, bg_ref, o_ref, g_ref):
        lhs = x_ref[...].astype(BF16)
        o_ref[...] = _nt_raw(lhs, w_ref[...]) + b_ref[...]

        @pl.when(pl.program_id(1) == 0)
        def _():
            g_ref[...] = _nt_raw(lhs, wg_ref[...]) + bg_ref[...]

    return pl.pallas_call(
        body, name="proj", grid=(m // tm, D_IN_MAIN // tk),
        in_specs=[pl.BlockSpec((tm, n), lambda i, j: (i, 0)), pl.BlockSpec((tk, n), lambda i, j: (j, 0)),
                  pl.BlockSpec((LANES, n), lambda i, j: (0, 0)), pl.BlockSpec((1, tk), lambda i, j: (0, j)),
                  pl.BlockSpec((1, LANES), lambda i, j: (0, 0))],
        out_specs=[pl.BlockSpec((tm, tk), lambda i, j: (i, j)), pl.BlockSpec((tm, LANES), lambda i, j: (i, 0))],
        out_shape=[jax.ShapeDtypeStruct((m, D_IN_MAIN), F32), jax.ShapeDtypeStruct((m, LANES), F32)],
        compiler_params=_params(("parallel", "arbitrary")),
    )(x, w_t, w_gate_t, b_main, b_gate)


def _input_projection_grads(d_proj, d_gates, x, tm, tt):
    t, m = d_proj.shape
    n = x.shape[1]
    last = t // tt - 1
    n_gate = D_IN - D_IN_MAIN

    def body(a_ref, g_ref, x_ref, o_ref, acc_ref, accg_ref):
        i, kk = pl.program_id(0), pl.program_id(1)

        @pl.when(kk == 0)
        def _():
            acc_ref[...] = jnp.zeros_like(acc_ref)

        @pl.when((kk == 0) & (i == 0))
        def _():
            accg_ref[...] = jnp.zeros_like(accg_ref)

        rhs = x_ref[...].astype(BF16)
        acc_ref[...] += _tn_raw(a_ref[...], rhs)

        @pl.when(i == 0)
        def _():
            accg_ref[...] += _tn_raw(g_ref[...], rhs)

        @pl.when(kk == last)
        def _():
            o_ref[pl.ds(pl.multiple_of(i * tm, tm), tm), :] = acc_ref[...].astype(o_ref.dtype)

        @pl.when((kk == last) & (i == 0))
        def _():
            o_ref[m:m + n_gate, :] = accg_ref[0:n_gate, :].astype(o_ref.dtype)

    return pl.pallas_call(
        body, name="d_w_in", grid=(m // tm, t // tt),
        in_specs=[pl.BlockSpec((tt, tm), lambda i, kk: (kk, i)), pl.BlockSpec((tt, LANES), lambda i, kk: (kk, 0)),
                  pl.BlockSpec((tt, n), lambda i, kk: (kk, 0))],
        out_specs=pl.BlockSpec((m + n_gate, n), lambda i, kk: (0, 0)),
        out_shape=jax.ShapeDtypeStruct((m + n_gate, n), BF16),
        scratch_shapes=[pltpu.VMEM((tm, n), F32), pltpu.VMEM((LANES, n), F32)],
        compiler_params=_params(("arbitrary", "arbitrary")),
    )(d_proj, d_gates, x)


def _matmul_nn_sum(pairs, add, scale, tm, name):
    m = pairs[0][0].shape[0]
    n = pairs[0][1].shape[1]
    in_specs, args = [], []
    for a, w, row0 in pairs:
        kk = a.shape[1]
        in_specs += [pl.BlockSpec((tm, kk), lambda i: (i, 0)),
                     pl.BlockSpec((kk, n), lambda i, blk=row0 // kk: (blk, 0))]
        args += [a, w]
    if add is not None:
        in_specs.append(pl.BlockSpec((tm, n), lambda i: (i, 0)))
        args.append(add)

    def body(*refs):
        acc = None
        for p in range(len(pairs)):
            term = _nn_raw(refs[2 * p][...], refs[2 * p + 1][...])
            acc = term if acc is None else acc + term
        if add is not None:
            acc = acc + scale * refs[2 * len(pairs)][...]
        refs[-1][...] = acc

    return pl.pallas_call(
        body, name=name, grid=(m // tm,), in_specs=in_specs,
        out_specs=pl.BlockSpec((tm, n), lambda i: (i, 0)),
        out_shape=jax.ShapeDtypeStruct((m, n), F32),
        compiler_params=_params(("parallel",)),
    )(*args)


def _matmul_tn(a, b, tm, tn, tt, name, shards=None, shard0=0, group=1, into=None, colsum=False, rows=None, row0=0):
    t, m = a.shape
    n = b.shape[1]
    assert not colsum or tm == m
    n_in = 2 + (into is not None)
    out_dtype = BF16
    per_step = 1 if shards is None else group
    width = per_step * tn

    def body(*refs):
        a_ref, b_ref = refs[0], refs[1]
        o_ref, acc_ref = refs[n_in], refs[-1]
        first = pl.program_id(2) == 0

        @pl.when(first)
        def _():
            acc_ref[...] = jnp.zeros_like(acc_ref)

        if shards is None:
            acc_ref[...] += _tn_raw(a_ref[...], b_ref[...])
        else:
            lhs = a_ref[...].astype(BF16)
            for g in range(per_step):
                acc_ref[g] += _tn_raw(lhs, b_ref[:, g * tn:(g + 1) * tn])

        @pl.when(pl.program_id(2) == t // tt - 1)
        def _():
            o_ref[...] = acc_ref[...].astype(o_ref.dtype)

        if colsum:
            s_ref = refs[n_in + 1]

            @pl.when(first)
            def _():
                s_ref[...] = jnp.zeros_like(s_ref)

            s_ref[...] += jnp.sum(b_ref[...], axis=0, keepdims=True)

    in_specs = [pl.BlockSpec((tt, tm), lambda i, j, kk: (kk, i)),
                pl.BlockSpec((tt, width), lambda i, j, kk: (kk, j))]
    args = [a, b]
    aliases = {}
    if into is not None:
        in_specs.append(pl.BlockSpec(memory_space=pl.ANY))
        args.append(into)
        aliases = {2: 0}
    if shards is None:
        out_specs = [pl.BlockSpec((tm, tn), lambda i, j, kk: (row0 // tm + i, j))]
        out_shape = [jax.ShapeDtypeStruct((rows or m, n), out_dtype)]
        acc = pltpu.VMEM((tm, tn), F32)
    else:
        out_specs = [pl.BlockSpec((per_step, tm, tn), lambda i, j, kk: (shard0 // per_step + j, i, 0))]
        out_shape = [jax.ShapeDtypeStruct((shards, m, tn), out_dtype)]
        acc = pltpu.VMEM((per_step, tm, tn), F32)
    if colsum:
        out_specs.append(pl.BlockSpec((1, tn), lambda i, j, kk: (0, j)))
        out_shape.append(jax.ShapeDtypeStruct((1, n), F32))
    res = pl.pallas_call(
        body, name=name, grid=(m // tm, n // width, t // tt), in_specs=in_specs, out_specs=out_specs,
        out_shape=out_shape, input_output_aliases=aliases, scratch_shapes=[acc],
        compiler_params=_params(("parallel", "parallel", "arbitrary")),
    )(*args)
    return res if colsum else res[0]


ROW_TILE = 64


def _stack(ref, start, rows):
    return ref[pl.ds(start, rows), :].astype(F32).reshape(rows // SUBLANES, SUBLANES, LANES)


def _vreg_rows(ref, n):
    return [jnp.broadcast_to(ref[j:j + 1, :], (SUBLANES, LANES))[None] for j in range(n)]


def _column_total(acc):
    return jnp.sum(acc, axis=0, keepdims=True)


def _conv_fwd_tile(pad_ref, taps_w, bias, r0, rows):
    taps = len(taps_w)
    acc = bias
    for j in range(taps):
        acc = acc + _stack(pad_ref, SUBLANES - (taps - 1 - j) + r0, rows) * taps_w[j]
    return acc


def _conv_grads_tile(pad_ref, dpad_ref, dx_ref, taps_w, dws, r0, rows):
    taps = len(taps_w)
    x_rows = _stack(pad_ref, SUBLANES + r0, rows)
    dx = None
    for j in range(taps):
        d_shifted = _stack(dpad_ref, r0 + (taps - 1 - j), rows)
        term = d_shifted * taps_w[j]
        dx = term if dx is None else dx + term
        dws[j] = dws[j] + jnp.sum(d_shifted * x_rows, axis=0)
    dx_ref[r0:r0 + rows, :] = dx.reshape(rows, LANES).astype(dx_ref.dtype)
    return jnp.sum(dx, axis=0)


def _ml_conv_fwd(proj, conv_w, conv_b):
    s = proj.shape[0]
    nblk = 2 * D_GROUP // LANES

    def body(x_ref, w_ref, b_ref, o_ref, pad_ref):
        pad_ref[0:SUBLANES, :] = jnp.zeros((SUBLANES, LANES), F32)
        pad_ref[SUBLANES:, :] = x_ref[...].astype(F32)
        taps_w, bias = _vreg_rows(w_ref, ML_CONV), _vreg_rows(b_ref, 1)[0]
        for r0 in range(0, s, ROW_TILE):
            rows = min(ROW_TILE, s - r0)
            o_ref[r0:r0 + rows, :] = jax.nn.silu(_conv_fwd_tile(pad_ref, taps_w, bias, r0, rows)).reshape(rows, LANES)

    return pl.pallas_call(
        body, name="ml_conv_fwd", grid=(nblk,),
        in_specs=[pl.BlockSpec((s, LANES), lambda j: (0, SEG_MQ + j)),
                  pl.BlockSpec((ML_CONV, LANES), lambda j: (0, j)),
                  pl.BlockSpec((1, LANES), lambda j: (0, j))],
        out_specs=pl.BlockSpec((s, LANES), lambda j: (0, j)),
        out_shape=jax.ShapeDtypeStruct((s, 2 * D_GROUP), F32),
        scratch_shapes=[pltpu.VMEM((s + SUBLANES, LANES), F32)],
        compiler_params=_params(("parallel",)),
    )(proj, conv_w, conv_b)


def _ml_conv_bwd(proj, conv_w, conv_b, d_qk, d_proj):
    s = proj.shape[0]
    nblk = 2 * D_GROUP // LANES

    def body(x_ref, w_ref, b_ref, dy_ref, _, dx_ref, dw_ref, db_ref, dxs_ref, pad_ref, dpad_ref):
        pad_ref[0:SUBLANES, :] = jnp.zeros((SUBLANES, LANES), F32)
        pad_ref[SUBLANES:, :] = x_ref[...].astype(F32)
        dpad_ref[s:, :] = jnp.zeros((SUBLANES, LANES), F32)
        taps_w, bias = _vreg_rows(w_ref, ML_CONV), _vreg_rows(b_ref, 1)[0]
        db = jnp.zeros((SUBLANES, LANES), F32)
        for r0 in range(0, s, ROW_TILE):
            rows = min(ROW_TILE, s - r0)
            pre = _conv_fwd_tile(pad_ref, taps_w, bias, r0, rows)
            _, vjp = jax.vjp(jax.nn.silu, pre)
            d_pre, = vjp(_stack(dy_ref, r0, rows))
            dpad_ref[r0:r0 + rows, :] = d_pre.reshape(rows, LANES)
            db = db + jnp.sum(d_pre, axis=0)
        db_ref[...] = _column_total(db)
        dws = [jnp.zeros((SUBLANES, LANES), F32) for _ in range(ML_CONV)]
        dx_sum = jnp.zeros((SUBLANES, LANES), F32)
        for r0 in range(0, s, ROW_TILE):
            dx_sum = dx_sum + _conv_grads_tile(pad_ref, dpad_ref, dx_ref, taps_w, dws, r0, min(ROW_TILE, s - r0))
        dxs_ref[...] = _column_total(dx_sum)
        for j in range(ML_CONV):
            dw_ref[j:j + 1, :] = _column_total(dws[j])

    return pl.pallas_call(
        body, name="ml_conv_bwd", grid=(nblk,),
        in_specs=[pl.BlockSpec((s, LANES), lambda j: (0, SEG_MQ + j)),
                  pl.BlockSpec((ML_CONV, LANES), lambda j: (0, j)),
                  pl.BlockSpec((1, LANES), lambda j: (0, j)),
                  pl.BlockSpec((s, LANES), lambda j: (0, j)),
                  pl.BlockSpec(memory_space=pl.ANY)],
        out_specs=[pl.BlockSpec((s, LANES), lambda j: (0, SEG_MQ + j)),
                   pl.BlockSpec((ML_CONV, LANES), lambda j: (0, j)),
                   pl.BlockSpec((1, LANES), lambda j: (0, j)),
                   pl.BlockSpec((1, LANES), lambda j: (0, j))],
        out_shape=[jax.ShapeDtypeStruct(d_proj.shape, d_proj.dtype),
                   jax.ShapeDtypeStruct((ML_CONV, 2 * D_GROUP), F32),
                   jax.ShapeDtypeStruct((1, 2 * D_GROUP), F32),
                   jax.ShapeDtypeStruct((1, 2 * D_GROUP), F32)],
        input_output_aliases={4: 0},
        scratch_shapes=[pltpu.VMEM((s + SUBLANES, LANES), F32), pltpu.VMEM((s + SUBLANES, LANES), F32)],
        compiler_params=_params(("parallel",)),
    )(proj, conv_w, conv_b, d_qk, d_proj)


def _gelu_mul(a, b):
    return jax.nn.gelu(a) * b


GELU_C = math.sqrt(2.0 / math.pi)
GELU_K = 0.044715


def _gelu_mul_grads(a, b, d):
    a2 = a * a
    t = jnp.tanh(GELU_C * (a + GELU_K * (a * a2)))
    cdf = 0.5 * (1.0 + t)
    slope = cdf + (0.5 * GELU_C) * a * (1.0 - t * t) * (1.0 + (3.0 * GELU_K) * a2)
    return d * b * slope, d * (a * cdf)


FFN_BLOCKS = D_FF // LANES


def _ffn_conv_fwd(u, conv_w, conv_b):
    s = u.shape[0]

    def body(g_ref, v_ref, wg_ref, wv_ref, bg_ref, bv_ref, o_ref, gpad_ref, vpad_ref):
        for pad_ref, x_ref in ((gpad_ref, g_ref), (vpad_ref, v_ref)):
            pad_ref[0:SUBLANES, :] = jnp.zeros((SUBLANES, LANES), F32)
            pad_ref[SUBLANES:, :] = x_ref[...].astype(F32)
        taps_g, bias_g = _vreg_rows(wg_ref, FFN_CONV), _vreg_rows(bg_ref, 1)[0]
        taps_v, bias_v = _vreg_rows(wv_ref, FFN_CONV), _vreg_rows(bv_ref, 1)[0]
        for r0 in range(0, s, ROW_TILE):
            rows = min(ROW_TILE, s - r0)
            ug = _conv_fwd_tile(gpad_ref, taps_g, bias_g, r0, rows)
            uv = _conv_fwd_tile(vpad_ref, taps_v, bias_v, r0, rows)
            o_ref[r0:r0 + rows, :] = _gelu_mul(ug, uv).reshape(rows, LANES).astype(o_ref.dtype)

    col = lambda off: (lambda j: (0, off + j))
    return pl.pallas_call(
        body, name="ffn_conv_fwd", grid=(FFN_BLOCKS,),
        in_specs=[pl.BlockSpec((s, LANES), col(0)), pl.BlockSpec((s, LANES), col(FFN_BLOCKS)),
                  pl.BlockSpec((FFN_CONV, LANES), col(0)), pl.BlockSpec((FFN_CONV, LANES), col(FFN_BLOCKS)),
                  pl.BlockSpec((1, LANES), col(0)), pl.BlockSpec((1, LANES), col(FFN_BLOCKS))],
        out_specs=pl.BlockSpec((s, LANES), col(0)),
        out_shape=jax.ShapeDtypeStruct((s, D_FF), BF16),
        scratch_shapes=[pltpu.VMEM((s + SUBLANES, LANES), F32), pltpu.VMEM((s + SUBLANES, LANES), F32)],
        compiler_params=_params(("parallel",)),
    )(u, u, conv_w, conv_w, conv_b, conv_b)


def _ffn_conv_bwd(u, conv_w, conv_b, d_h):
    s = u.shape[0]

    def body(g_ref, v_ref, wg_ref, wv_ref, bg_ref, bv_ref, dh_ref,
             dug_ref, duv_ref, dwg_ref, dwv_ref, dbg_ref, dbv_ref,
             gpad_ref, vpad_ref, dgpad_ref, dvpad_ref):
        for pad_ref, x_ref in ((gpad_ref, g_ref), (vpad_ref, v_ref)):
            pad_ref[0:SUBLANES, :] = jnp.zeros((SUBLANES, LANES), F32)
            pad_ref[SUBLANES:, :] = x_ref[...].astype(F32)
        dgpad_ref[s:, :] = jnp.zeros((SUBLANES, LANES), F32)
        dvpad_ref[s:, :] = jnp.zeros((SUBLANES, LANES), F32)
        taps_g, bias_g = _vreg_rows(wg_ref, FFN_CONV), _vreg_rows(bg_ref, 1)[0]
        taps_v, bias_v = _vreg_rows(wv_ref, FFN_CONV), _vreg_rows(bv_ref, 1)[0]
        dbg = jnp.zeros((SUBLANES, LANES), F32)
        dbv = jnp.zeros((SUBLANES, LANES), F32)
        for r0 in range(0, s, ROW_TILE):
            rows = min(ROW_TILE, s - r0)
            ug = _conv_fwd_tile(gpad_ref, taps_g, bias_g, r0, rows)
            uv = _conv_fwd_tile(vpad_ref, taps_v, bias_v, r0, rows)
            d_ug, d_uv = _gelu_mul_grads(ug, uv, _stack(dh_ref, r0, rows))
            dgpad_ref[r0:r0 + rows, :] = d_ug.reshape(rows, LANES)
            dvpad_ref[r0:r0 + rows, :] = d_uv.reshape(rows, LANES)
            dbg = dbg + jnp.sum(d_ug, axis=0)
            dbv = dbv + jnp.sum(d_uv, axis=0)
        dbg_ref[...] = _column_total(dbg)
        dbv_ref[...] = _column_total(dbv)
        for pad_ref, dpad_ref, taps_w, dx_ref, dw_ref in ((gpad_ref, dgpad_ref, taps_g, dug_ref, dwg_ref),
                                                          (vpad_ref, dvpad_ref, taps_v, duv_ref, dwv_ref)):
            dws = [jnp.zeros((SUBLANES, LANES), F32) for _ in range(FFN_CONV)]
            for r0 in range(0, s, ROW_TILE):
                _conv_grads_tile(pad_ref, dpad_ref, dx_ref, taps_w, dws, r0, min(ROW_TILE, s - r0))
            for j in range(FFN_CONV):
                dw_ref[j:j + 1, :] = _column_total(dws[j])

    col = lambda off: (lambda j: (0, off + j))
    seq = pl.BlockSpec((s, LANES), col(0))
    return pl.pallas_call(
        body, name="ffn_conv_bwd", grid=(FFN_BLOCKS,),
        in_specs=[pl.BlockSpec((s, LANES), col(0)), pl.BlockSpec((s, LANES), col(FFN_BLOCKS)),
                  pl.BlockSpec((FFN_CONV, LANES), col(0)), pl.BlockSpec((FFN_CONV, LANES), col(FFN_BLOCKS)),
                  pl.BlockSpec((1, LANES), col(0)), pl.BlockSpec((1, LANES), col(FFN_BLOCKS)), seq],
        out_specs=[seq, seq, pl.BlockSpec((FFN_CONV, LANES), col(0)), pl.BlockSpec((FFN_CONV, LANES), col(0)),
                   pl.BlockSpec((1, LANES), col(0)), pl.BlockSpec((1, LANES), col(0))],
        out_shape=[jax.ShapeDtypeStruct((s, D_FF), BF16), jax.ShapeDtypeStruct((s, D_FF), BF16),
                   jax.ShapeDtypeStruct((FFN_CONV, D_FF), F32), jax.ShapeDtypeStruct((FFN_CONV, D_FF), F32),
                   jax.ShapeDtypeStruct((1, D_FF), F32), jax.ShapeDtypeStruct((1, D_FF), F32)],
        scratch_shapes=[pltpu.VMEM((s + SUBLANES, LANES), F32) for _ in range(4)],
        compiler_params=_params(("parallel",)),
    )(u, u, conv_w, conv_w, conv_b, conv_b, d_h)


def _chunk_masks(c):
    row = lax.broadcasted_iota(jnp.int32, (c, c), 0)
    col = lax.broadcasted_iota(jnp.int32, (c, c), 1)
    return row, col


@jax.custom_vjp
def _split_heads(x):
    return tuple(x[:, h * D_HEAD:(h + 1) * D_HEAD] for h in range(N_HEADS))


_split_heads.defvjp(lambda x: (_split_heads(x), None), lambda _, gs: (jnp.concatenate(gs, axis=1),))


@jax.custom_vjp
def _merge_heads(xs):
    return jnp.concatenate(xs, axis=1)


_merge_heads.defvjp(lambda xs: (_merge_heads(xs), None), lambda _, g: (_split_heads(g),))


@jax.custom_vjp
def _split_chunks(x):
    return tuple(x[i * CHUNK:(i + 1) * CHUNK] for i in range(x.shape[0] // CHUNK))


_split_chunks.defvjp(lambda x: (_split_chunks(x), None), lambda _, gs: (jnp.concatenate(gs, axis=0),))


@jax.custom_vjp
def _merge_chunks(xs):
    return jnp.concatenate(xs, axis=0)


_merge_chunks.defvjp(lambda xs: (_merge_chunks(xs), None), lambda _, g: (_split_chunks(g),))


def _blocks(x):
    return [_split_heads(rows) for rows in _split_chunks(x)]


def _per_chunk_rows(per_chunk, rid):
    out = per_chunk[0]
    for i in range(1, len(per_chunk)):
        out = jnp.where(rid >= i * CHUNK, per_chunk[i], out)
    return out


HEADS = range(N_HEADS)
CHUNKS_PER_STEP = 8
ML_CHUNKS_PER_STEP = 1


def _hg_chunk(hq, hf, hi, hgate, l0, l1, nw, sts):
    n = hq.shape[0] // CHUNK
    causal = _chunk_masks(CHUNK)
    causal = causal[1] <= causal[0]
    mx = lax.stop_gradient(jnp.maximum(l0, l1))
    e0 = jnp.exp(l0 - mx)
    e1 = jnp.exp(l1 - mx)
    lb = e0 / (e0 + e1)
    sig = jax.nn.sigmoid(hf)
    lf = jnp.log(lb + (1.0 - lb) * sig)
    k = (1.0 - lb) * jax.nn.sigmoid(-hf)
    q = jax.nn.silu(hq)
    tri = causal.astype(F32)
    b = _merge_chunks(tuple(_dg(tri, rows, 1, 0, HIGHEST) for rows in _split_chunks(lf)))
    rid = lax.broadcasted_iota(jnp.int32, b.shape, 0)
    pick = lambda r: jnp.sum(jnp.where(rid == r, b, 0.0), axis=0, keepdims=True)
    b_last_c = [pick(i * CHUNK + CHUNK - 1) for i in range(n)]
    b_ref = _per_chunk_rows([pick(i * CHUNK + CHUNK // 2 - 1) for i in range(n)], rid)
    b_last = _per_chunk_rows(b_last_c, rid)
    qa = _blocks(q * jnp.exp(b - b_ref))
    ka = _blocks(k * jnp.exp(b_ref - b))
    qe = _blocks(q * jnp.exp(b))
    kd = _blocks(k * jnp.exp(b_last - b))
    decay = [_split_heads(jnp.exp(b_last_c[i])) for i in range(n)]
    v = _blocks(hi)
    chunks = range(n)
    attn = [[jnp.where(causal, _nt(qa[i][h], ka[i][h]), 0.0) for h in HEADS] for i in chunks]
    intra = [[_nn(attn[i][h], v[i][h]) for h in HEADS] for i in chunks]
    kv = [[_tn(v[i][h], kd[i][h]) for h in HEADS] for i in chunks]
    normed = []
    for i in chunks:
        inter = [_nt(qe[i][h], sts[h]) for h in HEADS]
        sts = tuple(decay[i][h] * sts[h] + kv[i][h] for h in HEADS)
        o = [intra[i][h] + inter[h] for h in HEADS]
        normed.append(_merge_heads(tuple(o[h] * lax.rsqrt(jnp.mean(o[h] * o[h], axis=-1, keepdims=True) + LN_EPS)
                                         for h in HEADS)))
    return _merge_chunks(tuple(normed)) * nw * jax.nn.silu(hgate), sts


def _seg(ref, seg):
    return ref[:, seg * D_GROUP:(seg + 1) * D_GROUP]


def _hgrn2_fwd(proj, logits, norm_w):
    s = proj.shape[0]
    rows = CHUNKS_PER_STEP * CHUNK
    nc = s // rows

    def body(p_ref, lg_ref, nw_ref, y_ref, st_out_ref, st_scr):
        @pl.when(pl.program_id(0) == 0)
        def _():
            st_scr[...] = jnp.zeros_like(st_scr)

        sts = tuple(st_scr[h] for h in HEADS)
        y, sts_new = _hg_chunk(_seg(p_ref, 0), _seg(p_ref, 1), _seg(p_ref, 2), _seg(p_ref, 3),
                               lg_ref[0:1, :], lg_ref[1:2, :], nw_ref[...], sts)
        y_ref[...] = y.astype(y_ref.dtype)
        for h in HEADS:
            st_out_ref[h] = sts[h]
            st_scr[h] = sts_new[h]

    return pl.pallas_call(
        body, name="hgrn2_fwd", grid=(nc,),
        in_specs=[pl.BlockSpec((rows, 4 * D_GROUP), lambda c: (c, 0)),
                  pl.BlockSpec((2, D_GROUP), lambda c: (0, 0)),
                  pl.BlockSpec((1, D_GROUP), lambda c: (0, 0))],
        out_specs=[pl.BlockSpec((rows, D_GROUP), lambda c: (c, 0)),
                   pl.BlockSpec((None, N_HEADS, D_HEAD, D_HEAD), lambda c: (c, 0, 0, 0))],
        out_shape=[jax.ShapeDtypeStruct((s, 2 * D_GROUP), BF16),
                   jax.ShapeDtypeStruct((nc, N_HEADS, D_HEAD, D_HEAD), F32)],
        scratch_shapes=[pltpu.VMEM((N_HEADS, D_HEAD, D_HEAD), F32)],
        compiler_params=_params(("arbitrary",)),
    )(proj, logits, norm_w)


def _hgrn2_bwd(proj, logits, norm_w, states, d_y):
    s = proj.shape[0]
    rows = CHUNKS_PER_STEP * CHUNK
    nc = s // rows

    def body(p_ref, lg_ref, nw_ref, st_ref, dy_ref, dp_ref, dl_ref, dnw_ref, dsum_ref, dst_scr):
        @pl.when(pl.program_id(0) == 0)
        def _():
            dst_scr[...] = jnp.zeros_like(dst_scr)
            dl_ref[...] = jnp.zeros_like(dl_ref)
            dnw_ref[...] = jnp.zeros_like(dnw_ref)
            dsum_ref[...] = jnp.zeros_like(dsum_ref)

        _, vjp = jax.vjp(_hg_chunk, _seg(p_ref, 0), _seg(p_ref, 1), _seg(p_ref, 2), _seg(p_ref, 3),
                         lg_ref[0:1, :], lg_ref[1:2, :], nw_ref[...], tuple(st_ref[h] for h in HEADS))
        d_hq, d_hf, d_hi, d_hg, d_l0, d_l1, d_nw, d_sts = vjp((dy_ref[...], tuple(dst_scr[h] for h in HEADS)))
        for seg, val in enumerate((d_hq, d_hf, d_hi, d_hg)):
            dp_ref[:, seg * D_GROUP:(seg + 1) * D_GROUP] = val.astype(dp_ref.dtype)
            dsum_ref[:, seg * D_GROUP:(seg + 1) * D_GROUP] += jnp.sum(val, axis=0, keepdims=True)
        dl_ref[0:1, :] += d_l0
        dl_ref[1:2, :] += d_l1
        dnw_ref[...] += d_nw
        for h in HEADS:
            dst_scr[h] = d_sts[h]

    rev = lambda c: nc - 1 - c
    return pl.pallas_call(
        body, name="hgrn2_bwd", grid=(nc,),
        in_specs=[pl.BlockSpec((rows, 4 * D_GROUP), lambda c: (rev(c), 0)),
                  pl.BlockSpec((2, D_GROUP), lambda c: (0, 0)),
                  pl.BlockSpec((1, D_GROUP), lambda c: (0, 0)),
                  pl.BlockSpec((None, N_HEADS, D_HEAD, D_HEAD), lambda c: (rev(c), 0, 0, 0)),
                  pl.BlockSpec((rows, D_GROUP), lambda c: (rev(c), 0))],
        out_specs=[pl.BlockSpec((rows, 4 * D_GROUP), lambda c: (rev(c), 0)),
                   pl.BlockSpec((2, D_GROUP), lambda c: (0, 0)),
                   pl.BlockSpec((1, D_GROUP), lambda c: (0, 0)),
                   pl.BlockSpec((1, 4 * D_GROUP), lambda c: (0, 0))],
        out_shape=[jax.ShapeDtypeStruct((s, D_IN_MAIN), BF16), jax.ShapeDtypeStruct((2, D_GROUP), F32),
                   jax.ShapeDtypeStruct((1, D_GROUP), F32), jax.ShapeDtypeStruct((1, 4 * D_GROUP), F32)],
        scratch_shapes=[pltpu.VMEM((N_HEADS, D_HEAD, D_HEAD), F32)],
        compiler_params=_params(("arbitrary",)),
    )(proj, logits, norm_w, states, d_y)


def _gate_column(gates, lane, idx):
    return jnp.sum(jnp.where(lane == idx, gates, 0.0), axis=1, keepdims=True)


def _head_layer_norm(h):
    mu = jnp.mean(h, axis=-1, keepdims=True)
    var = jnp.mean(jnp.square(h - mu), axis=-1, keepdims=True)
    return (h - mu) * lax.rsqrt(var + LN_EPS)


def _ml_chunk(qc, kc, v, mo, gates, nw, cts, ns, ms):
    n = qc.shape[0] // CHUNK
    row, col = _chunk_masks(CHUNK)
    mask = col <= row
    eye = col == row
    to_row = lambda t: jnp.sum(jnp.where(eye, t, 0.0), axis=0, keepdims=True)
    q = _blocks(qc * (D_HEAD ** -0.5))
    k = _blocks(kc)
    vs = _blocks(v)
    gate_rows = _split_chunks(gates)
    lane = lax.broadcasted_iota(jnp.int32, gate_rows[0].shape, 1)
    each = [(i, h) for i in range(n) for h in HEADS]
    on_each = lambda f: {ih: f(*ih) for ih in each}
    ig = on_each(lambda i, h: _gate_column(gate_rows[i], lane, h))
    lf = on_each(lambda i, h: jax.nn.log_sigmoid(_gate_column(gate_rows[i], lane, N_HEADS + h)))
    lf_row = on_each(lambda i, h: to_row(lf[i, h]))
    ig_row = on_each(lambda i, h: to_row(ig[i, h]))
    b_col = on_each(lambda i, h: jnp.sum(jnp.where(mask, lf_row[i, h], 0.0), axis=1, keepdims=True))
    b_row = on_each(lambda i, h: jnp.sum(jnp.where(row <= col, lf[i, h], 0.0), axis=0, keepdims=True))
    g = on_each(lambda i, h: jnp.sum(lf[i, h], axis=0, keepdims=True))
    d = on_each(lambda i, h: jnp.where(mask, b_col[i, h] - b_row[i, h] + ig_row[i, h], -jnp.inf))
    a = on_each(lambda i, h: g[i, h] - b_col[i, h] + ig[i, h])
    m_at = {(0, h): ms[h] for h in HEADS}
    for i, h in each:
        m_at[i + 1, h] = lax.stop_gradient(jnp.maximum(g[i, h] + m_at[i, h], jnp.max(a[i, h], axis=0, keepdims=True)))
    inter = on_each(lambda i, h: b_col[i, h] + m_at[i, h])
    m_t = on_each(lambda i, h: lax.stop_gradient(jnp.maximum(inter[i, h], jnp.max(d[i, h], axis=1, keepdims=True))))
    qk = on_each(lambda i, h: _nt(q[i][h], k[i][h]))
    sc = on_each(lambda i, h: qk[i, h] * jnp.exp(d[i, h] - m_t[i, h]))
    w_inter = on_each(lambda i, h: jnp.exp(inter[i, h] - m_t[i, h]))
    sv = on_each(lambda i, h: _nn(sc[i, h], vs[i][h]))
    decay = on_each(lambda i, h: jnp.exp(g[i, h] + m_at[i, h] - m_at[i + 1, h]))
    wk = on_each(lambda i, h: k[i][h] * jnp.exp(a[i, h] - m_at[i + 1, h]))
    kv = on_each(lambda i, h: _tn(vs[i][h], wk[i, h]))
    normed = []
    for i in range(n):
        qc_state = [_nt(q[i][h], cts[h]) for h in HEADS]
        num = [sv[i, h] + w_inter[i, h] * qc_state[h] for h in HEADS]
        den = [jnp.sum(sc[i, h], axis=1, keepdims=True)
               + w_inter[i, h] * jnp.sum(q[i][h] * ns[h], axis=1, keepdims=True) for h in HEADS]
        hh = [num[h] / jnp.maximum(jnp.abs(den[h]), jnp.exp(-m_t[i, h])) for h in HEADS]
        cts = tuple(decay[i, h] * cts[h] + kv[i, h] for h in HEADS)
        ns = tuple(decay[i, h] * ns[h] + jnp.sum(wk[i, h], axis=0, keepdims=True) for h in HEADS)
        normed.append(_merge_heads(tuple(_head_layer_norm(hh[h]) for h in HEADS)))
    y = jax.nn.sigmoid(mo) * (_merge_chunks(tuple(normed)) * nw)
    return y, cts, ns, tuple(m_at[n, h] for h in HEADS)


def _mlstm_fwd(qk, proj, gates, norm_w, y):
    s = proj.shape[0]
    rows = ML_CHUNKS_PER_STEP * CHUNK
    nc = s // rows

    def body(qk_ref, vo_ref, g_ref, nw_ref, _, y_ref, ct_out, n_out, m_out, ct_scr, n_scr, m_scr):
        @pl.when(pl.program_id(0) == 0)
        def _():
            ct_scr[...] = jnp.zeros_like(ct_scr)
            n_scr[...] = jnp.zeros_like(n_scr)
            m_scr[...] = jnp.full(m_scr.shape, NEG_BIG, F32)

        cts = tuple(ct_scr[h] for h in HEADS)
        ns = tuple(n_scr[h] for h in HEADS)
        ms = tuple(m_scr[h] for h in HEADS)
        y, cts_new, ns_new, ms_new = _ml_chunk(_seg(qk_ref, 0), _seg(qk_ref, 1), _seg(vo_ref, 0), _seg(vo_ref, 1),
                                               g_ref[...], nw_ref[...], cts, ns, ms)
        y_ref[...] = y.astype(y_ref.dtype)
        for h in HEADS:
            ct_out[h], n_out[h], m_out[h] = cts[h], ns[h], ms[h]
            ct_scr[h], n_scr[h], m_scr[h] = cts_new[h], ns_new[h], ms_new[h]

    st = lambda r, w: pl.BlockSpec((None, N_HEADS, r, w), lambda c: (c, 0, 0, 0))
    return pl.pallas_call(
        body, name="mlstm_fwd", grid=(nc,),
        in_specs=[pl.BlockSpec((rows, 2 * D_GROUP), lambda c: (c, 0)),
                  pl.BlockSpec((rows, 2 * D_GROUP), lambda c: (c, VO_BLOCK)),
                  pl.BlockSpec((rows, LANES), lambda c: (c, 0)),
                  pl.BlockSpec((1, D_GROUP), lambda c: (0, 0)),
                  pl.BlockSpec(memory_space=pl.ANY)],
        out_specs=[pl.BlockSpec((rows, D_GROUP), lambda c: (c, 1)),
                   st(D_HEAD, D_HEAD), st(1, D_HEAD), st(1, 1)],
        out_shape=[jax.ShapeDtypeStruct(y.shape, y.dtype),
                   jax.ShapeDtypeStruct((nc, N_HEADS, D_HEAD, D_HEAD), F32),
                   jax.ShapeDtypeStruct((nc, N_HEADS, 1, D_HEAD), F32),
                   jax.ShapeDtypeStruct((nc, N_HEADS, 1, 1), F32)],
        input_output_aliases={4: 0},
        scratch_shapes=[pltpu.VMEM((N_HEADS, D_HEAD, D_HEAD), F32), pltpu.VMEM((N_HEADS, 1, D_HEAD), F32),
                        pltpu.VMEM((N_HEADS, 1, 1), F32)],
        compiler_params=_params(("arbitrary",)),
    )(qk, proj, gates, norm_w, y)


def _mlstm_bwd(qk, proj, gates, norm_w, ct_s, n_s, m_s, d_y, d_proj):
    s = proj.shape[0]
    rows = ML_CHUNKS_PER_STEP * CHUNK
    nc = s // rows

    def body(qk_ref, vo_ref, g_ref, nw_ref, ct_ref, n_ref, m_ref, dy_ref, _,
             dp_ref, dqk_ref, dg_ref, dnw_ref, dsum_ref, dct_scr, dn_scr):
        @pl.when(pl.program_id(0) == 0)
        def _():
            dct_scr[...] = jnp.zeros_like(dct_scr)
            dn_scr[...] = jnp.zeros_like(dn_scr)
            dnw_ref[...] = jnp.zeros_like(dnw_ref)
            dsum_ref[...] = jnp.zeros_like(dsum_ref)

        ms = tuple(m_ref[h] for h in HEADS)
        step = lambda *a: _ml_chunk(*a, ms)[:3]
        _, vjp = jax.vjp(step, _seg(qk_ref, 0), _seg(qk_ref, 1), _seg(vo_ref, 0), _seg(vo_ref, 1), g_ref[...],
                         nw_ref[...], tuple(ct_ref[h] for h in HEADS), tuple(n_ref[h] for h in HEADS))
        d_q, d_k, d_v, d_o, d_gates, d_nw, d_cts, d_ns = vjp(
            (dy_ref[...], tuple(dct_scr[h] for h in HEADS), tuple(dn_scr[h] for h in HEADS)))
        dqk_ref[:, 0:D_GROUP] = d_q
        dqk_ref[:, D_GROUP:2 * D_GROUP] = d_k
        for seg, val in enumerate((d_v, d_o)):
            dp_ref[:, seg * D_GROUP:(seg + 1) * D_GROUP] = val.astype(dp_ref.dtype)
            dsum_ref[:, seg * D_GROUP:(seg + 1) * D_GROUP] += jnp.sum(val, axis=0, keepdims=True)
        dg_ref[...] = d_gates
        dnw_ref[...] += d_nw
        for h in HEADS:
            dct_scr[h] = d_cts[h]
            dn_scr[h] = d_ns[h]

    rev = lambda c: nc - 1 - c
    st = lambda r, w: pl.BlockSpec((None, N_HEADS, r, w), lambda c: (rev(c), 0, 0, 0))
    return pl.pallas_call(
        body, name="mlstm_bwd", grid=(nc,),
        in_specs=[pl.BlockSpec((rows, 2 * D_GROUP), lambda c: (rev(c), 0)),
                  pl.BlockSpec((rows, 2 * D_GROUP), lambda c: (rev(c), VO_BLOCK)),
                  pl.BlockSpec((rows, LANES), lambda c: (rev(c), 0)),
                  pl.BlockSpec((1, D_GROUP), lambda c: (0, 0)),
                  st(D_HEAD, D_HEAD), st(1, D_HEAD), st(1, 1),
                  pl.BlockSpec((rows, D_GROUP), lambda c: (rev(c), 1)),
                  pl.BlockSpec(memory_space=pl.ANY)],
        out_specs=[pl.BlockSpec((rows, 2 * D_GROUP), lambda c: (rev(c), VO_BLOCK)),
                   pl.BlockSpec((rows, 2 * D_GROUP), lambda c: (rev(c), 0)),
                   pl.BlockSpec((rows, LANES), lambda c: (rev(c), 0)),
                   pl.BlockSpec((1, D_GROUP), lambda c: (0, 0)),
                   pl.BlockSpec((1, 2 * D_GROUP), lambda c: (0, 0))],
        out_shape=[jax.ShapeDtypeStruct(d_proj.shape, d_proj.dtype), jax.ShapeDtypeStruct((s, 2 * D_GROUP), F32),
                   jax.ShapeDtypeStruct((s, LANES), F32), jax.ShapeDtypeStruct((1, D_GROUP), F32),
                   jax.ShapeDtypeStruct((1, 2 * D_GROUP), F32)],
        input_output_aliases={8: 0},
        scratch_shapes=[pltpu.VMEM((N_HEADS, D_HEAD, D_HEAD), F32), pltpu.VMEM((N_HEADS, 1, D_HEAD), F32)],
        compiler_params=_params(("arbitrary",)),
    )(qk, proj, gates, norm_w, ct_s, n_s, m_s, d_y, d_proj)


LN_TOKENS = 512
ATT_TOKENS = 512


def _proj_res_ln(a, w, xres, g, b, name):
    s, dm = xres.shape
    k = a.shape[1]
    tb = min(LN_TOKENS, s)

    def body(a_ref, w_ref, x_ref, g_ref, b_ref, z_ref, o_ref):
        halves = [slice(0, tb // 2), slice(tb // 2, tb)]
        zs = [ALPHA * x_ref[rows, :] + _nn_raw(a_ref[rows, :], w_ref[...]) for rows in halves]
        for rows, z in zip(halves, zs):
            z_ref[rows, :] = z
            o_ref[rows, :] = _layer_norm(z, g_ref[...], b_ref[...])

    tok = pl.BlockSpec((tb, dm), lambda i: (i, 0))
    vec = pl.BlockSpec((1, dm), lambda i: (0, 0))
    act = jax.ShapeDtypeStruct((s, dm), F32)
    return pl.pallas_call(
        body, name=name, grid=(s // tb,),
        in_specs=[pl.BlockSpec((tb, k), lambda i: (i, 0)), pl.BlockSpec((k, dm), lambda i: (0, 0)), tok, vec, vec],
        out_specs=[tok, tok], out_shape=[act, act], compiler_params=_params(("parallel",)),
    )(a, w, xres, g, b)


def _ln_bwd_proj(d_out, z, g, b, w, name):
    s, dm = z.shape
    k = w.shape[0]
    tb = min(LN_TOKENS, s)

    def body(do_ref, z_ref, g_ref, b_ref, w_ref, dz_ref, da_ref, dg_ref, db_ref):
        @pl.when(pl.program_id(0) == 0)
        def _():
            dg_ref[...] = jnp.zeros_like(dg_ref)
            db_ref[...] = jnp.zeros_like(db_ref)

        halves = [slice(0, tb // 2), slice(tb // 2, tb)]
        d_zs = []
        for rows in halves:
            _, vjp = jax.vjp(_layer_norm, z_ref[rows, :], g_ref[...], b_ref[...])
            d_z, d_g, d_b = vjp(do_ref[rows, :])
            dz_ref[rows, :] = d_z
            dg_ref[...] += d_g
            db_ref[...] += d_b
            d_zs.append(d_z)
        for rows, d_z in zip(halves, d_zs):
            da_ref[rows, :] = _nt_raw(d_z, w_ref[...])

    tok = pl.BlockSpec((tb, dm), lambda i: (i, 0))
    vec = pl.BlockSpec((1, dm), lambda i: (0, 0))
    return pl.pallas_call(
        body, name=name, grid=(s // tb,),
        in_specs=[tok, tok, vec, vec, pl.BlockSpec((k, dm), lambda i: (0, 0))],
        out_specs=[tok, pl.BlockSpec((tb, k), lambda i: (i, 0)), vec, vec],
        out_shape=[jax.ShapeDtypeStruct((s, dm), F32), jax.ShapeDtypeStruct((s, k), F32),
                   jax.ShapeDtypeStruct((1, dm), F32), jax.ShapeDtypeStruct((1, dm), F32)],
        compiler_params=_params(("arbitrary",)),
    )(d_out, z, g, b, w)


def _proj_loss_tail(a, w, xres, g, b, target):
    s, dm = xres.shape
    k = a.shape[1]
    tb = min(ATT_TOKENS, s)

    def loss_fn(z, gg, bb, tgt):
        err = jnp.square(_layer_norm(z, gg, bb) - tgt)
        return 0.5 * jnp.sum(jnp.mean(err, axis=-1, keepdims=True), axis=0, keepdims=True)

    def body(a_ref, w_ref, x_ref, g_ref, b_ref, t_ref, loss_ref, dz_ref, dg_ref, db_ref):
        @pl.when(pl.program_id(0) == 0)
        def _():
            loss_ref[...] = jnp.zeros_like(loss_ref)
            dg_ref[...] = jnp.zeros_like(dg_ref)
            db_ref[...] = jnp.zeros_like(db_ref)

        halves = [slice(0, tb // 2), slice(tb // 2, tb)]
        zs = [ALPHA * x_ref[rows, :] + _nn_raw(a_ref[rows, :], w_ref[...]) for rows in halves]
        for rows, z in zip(halves, zs):
            tgt = t_ref[rows, :]
            loss, vjp = jax.vjp(lambda zz, gg, bb, tgt=tgt: loss_fn(zz, gg, bb, tgt), z, g_ref[...], b_ref[...])
            d_z, d_g, d_b = vjp(jnp.ones((1, 1), F32))
            loss_ref[...] += loss
            dz_ref[rows, :] = d_z
            dg_ref[...] += d_g
            db_ref[...] += d_b

    tok = pl.BlockSpec((tb, dm), lambda i: (i, 0))
    vec = pl.BlockSpec((1, dm), lambda i: (0, 0))
    one = pl.BlockSpec((1, 1), lambda i: (0, 0))
    return pl.pallas_call(
        body, name="ffn_down_loss_tail", grid=(s // tb,),
        in_specs=[pl.BlockSpec((tb, k), lambda i: (i, 0)), pl.BlockSpec((k, dm), lambda i: (0, 0)), tok, vec, vec, tok],
        out_specs=[one, tok, vec, vec],
        out_shape=[jax.ShapeDtypeStruct((1, 1), F32), jax.ShapeDtypeStruct((s, dm), F32),
                   jax.ShapeDtypeStruct((1, dm), F32), jax.ShapeDtypeStruct((1, dm), F32)],
        compiler_params=_params(("arbitrary",)),
    )(a, w, xres, g, b, target)


def _att_heads(qs, ks, vs):
    sc = [_nt(q, k) * (CA_DH ** -0.5) for q, k in zip(qs, ks)]
    p = [jax.nn.softmax(s, axis=-1) for s in sc]
    return tuple(_nn(pp, v) for pp, v in zip(p, vs))


def _head_slices(ref_or_value, offset):
    return tuple(ref_or_value[:, offset + h * CA_DH:offset + (h + 1) * CA_DH] for h in range(CA_HEADS))


def _cross_attention_fwd(x1, kv, wq, wo, g, b):
    s = x1.shape[0]
    tb = min(ATT_TOKENS, s)

    def body(x_ref, kv_ref, wq_ref, wo_ref, g_ref, b_ref, att_ref, z_ref, o_ref):
        x_blk = x_ref[...]
        q = _nn_raw(x_blk, wq_ref[...])
        att = jnp.concatenate(_att_heads(_head_slices(q, 0), _head_slices(kv_ref, 0), _head_slices(kv_ref, D_MODEL)),
                              axis=1)
        att_ref[...] = att.astype(att_ref.dtype)
        z = ALPHA * x_blk + _nn_raw(att, wo_ref[...])
        z_ref[...] = z
        o_ref[...] = _layer_norm(z, g_ref[...], b_ref[...])

    tok = pl.BlockSpec((tb, D_MODEL), lambda i: (i, 0))
    mat = pl.BlockSpec((D_MODEL, D_MODEL), lambda i: (0, 0))
    vec = pl.BlockSpec((1, D_MODEL), lambda i: (0, 0))
    act = jax.ShapeDtypeStruct((s, D_MODEL), F32)
    return pl.pallas_call(
        body, name="cross_attention_fwd", grid=(s // tb,),
        in_specs=[tok, pl.BlockSpec((N_MEM, 2 * D_MODEL), lambda i: (0, 0)), mat, mat, vec, vec],
        out_specs=[tok, tok, tok],
        out_shape=[jax.ShapeDtypeStruct((s, D_MODEL), BF16), act, act],
        compiler_params=_params(("parallel",)),
    )(x1, kv, wq, wo, g, b)


def _cross_attention_bwd(d_x2, x1, z2, kv, wq, wo, g, b):
    s = x1.shape[0]
    tb = min(ATT_TOKENS, s)

    def body(dx2_ref, x_ref, z_ref, kv_ref, wq_ref, wo_ref, g_ref, b_ref,
             dx1_ref, dq_ref, dz_ref, dkv_ref, dg_ref, db_ref):
        @pl.when(pl.program_id(0) == 0)
        def _():
            dkv_ref[...] = jnp.zeros_like(dkv_ref)
            dg_ref[...] = jnp.zeros_like(dg_ref)
            db_ref[...] = jnp.zeros_like(db_ref)

        q = _nn_raw(x_ref[...], wq_ref[...])
        _, ln_vjp = jax.vjp(_layer_norm, z_ref[...], g_ref[...], b_ref[...])
        d_z, d_g, d_b = ln_vjp(dx2_ref[...])
        dg_ref[...] += d_g
        db_ref[...] += d_b
        dz_ref[...] = d_z.astype(dz_ref.dtype)
        d_att = _nt_raw(d_z, wo_ref[...])
        _, vjp = jax.vjp(_att_heads, _head_slices(q, 0), _head_slices(kv_ref, 0), _head_slices(kv_ref, D_MODEL))
        d_qs, d_ks, d_vs = vjp(_head_slices(d_att, 0))
        for h in range(CA_HEADS):
            lo = h * CA_DH
            dkv_ref[:, lo:lo + CA_DH] += d_ks[h]
            dkv_ref[:, D_MODEL + lo:D_MODEL + lo + CA_DH] += d_vs[h]
        d_q = jnp.concatenate(d_qs, axis=1)
        dq_ref[...] = d_q.astype(dq_ref.dtype)
        dx1_ref[...] = ALPHA * d_z + _nt_raw(d_q, wq_ref[...])

    tok = pl.BlockSpec((tb, D_MODEL), lambda i: (i, 0))
    mem = pl.BlockSpec((N_MEM, 2 * D_MODEL), lambda i: (0, 0))
    mat = pl.BlockSpec((D_MODEL, D_MODEL), lambda i: (0, 0))
    vec = pl.BlockSpec((1, D_MODEL), lambda i: (0, 0))
    low = jax.ShapeDtypeStruct((s, D_MODEL), BF16)
    return pl.pallas_call(
        body, name="cross_attention_bwd", grid=(s // tb,),
        in_specs=[tok, tok, tok, mem, mat, mat, vec, vec], out_specs=[tok, tok, tok, mem, vec, vec],
        out_shape=[jax.ShapeDtypeStruct((s, D_MODEL), F32), low, low,
                   jax.ShapeDtypeStruct((N_MEM, 2 * D_MODEL), F32),
                   jax.ShapeDtypeStruct((1, D_MODEL), F32), jax.ShapeDtypeStruct((1, D_MODEL), F32)],
        compiler_params=_params(("arbitrary",)),
    )(d_x2, x1, z2, kv, wq, wo, g, b)


def _local_step(x, mem, target, w, mid_weights=None, ffn_weights=None, down_weights=None, on_ffn_grads=None,
                on_mid_grads=None,
                on_small_grads=None, on_last_grads=None):
    w = dict(w)
    s = x.shape[0]
    tm = min(512, s)
    tt_big = min(1024, s)
    proj, gates = _input_projection(x, w["w_in_t"], w["w_in_gate_t"], w["b_in_main"], w["b_in_gate"],
                                    min(2048, s), 512)
    qk = _ml_conv_fwd(proj, w["ml_conv_w"], w["ml_conv_b"])
    y, hg_states = _hgrn2_fwd(proj, w["hg_lb_logits"], w["hg_norm_w"])
    y, ct_s, n_s, m_s = _mlstm_fwd(qk, proj, gates, w["ml_norm_w"], y)
    if mid_weights is not None:
        w.update(mid_weights(y))
    z1, x1 = _proj_res_ln(y, w["w_out"], x, w["ln1_g"], w["ln1_b"], "out_proj_ln1")
    kv = _matmul_nn(mem, w["ca_wkv"], None, N_MEM, CA_DH, "kv")
    att, z2, x2 = _cross_attention_fwd(x1, kv, w["ca_wq"], w["ca_wo"], w["ln2_g"], w["ln2_b"])
    if ffn_weights is not None:
        w.update(ffn_weights(x2))
    u = _matmul_nt(x2, w["ffn_w_up_t"], min(1024, s), UP_TILE, "ffn_up", out_dtype=BF16)
    hid = _ffn_conv_fwd(u, w["ffn_conv_w"], w["ffn_conv_b"])
    if down_weights is not None:
        w.update(down_weights(hid))
    loss, d_z3, d_ln3_g, d_ln3_b = _proj_loss_tail(hid, w["ffn_w_down"], x2, w["ln3_g"], w["ln3_b"], target)
    grads = {"ln3_g": d_ln3_g, "ln3_b": d_ln3_b}
    grads["ffn_w_down"] = _matmul_tn(hid, d_z3, UP_TILE, D_MODEL, tt_big, "d_w_down")
    d_hid = _matmul_nt(d_z3, w["ffn_w_down"], tm, D_FF, "d_hid", out_dtype=BF16)
    d_ug, d_uv, d_cwg, d_cwv, d_cbg, d_cbv = _ffn_conv_bwd(u, w["ffn_conv_w"], w["ffn_conv_b"], d_hid)
    grads["ffn_conv_w"] = jnp.concatenate([d_cwg, d_cwv], axis=-1)
    grads["ffn_conv_b"] = jnp.concatenate([d_cbg, d_cbv], axis=-1)
    d_w_up = _matmul_tn(d_ug, x2, UP_TILE, D_MODEL, tt_big, "d_w_up_gate", rows=D_UP)
    grads["ffn_w_up"] = _matmul_tn(d_uv, x2, UP_TILE, D_MODEL, tt_big, "d_w_up_val", rows=D_UP, row0=D_FF,
                                   into=d_w_up)
    d_x2 = _matmul_nn_sum([(d_ug, w["ffn_w_up_t"], 0), (d_uv, w["ffn_w_up_t"], D_FF)], d_z3, ALPHA,
                          min(256, s), "d_x2")
    if on_ffn_grads is not None:
        d_x2 = on_ffn_grads(grads, d_x2)
    d_x1, d_q, d_z2, d_kv, grads["ln2_g"], grads["ln2_b"] = _cross_attention_bwd(
        d_x2, x1, z2, kv, w["ca_wq"], w["ca_wo"], w["ln2_g"], w["ln2_b"])
    grads["ca_wo"] = _matmul_tn(att, d_z2, D_MODEL, D_MODEL, tt_big, "d_ca_wo")
    grads["ca_wq"] = _matmul_tn(x1, d_q, D_MODEL, D_MODEL, tt_big, "d_ca_wq")
    grads["ca_wkv"] = _matmul_tn(mem, d_kv, D_MODEL, CA_DH, N_MEM, "d_ca_wkv", shards=N_DEV, group=N_DEV)
    d_z1, d_y, grads["ln1_g"], grads["ln1_b"] = _ln_bwd_proj(d_x1, z1, w["ln1_g"], w["ln1_b"], w["w_out"],
                                                             "ln1_bwd_out_proj")
    grads["w_out"] = _matmul_tn(y, d_z1, D_MODEL, D_MODEL, tt_big, "d_w_out")
    if on_mid_grads is not None:
        d_y = on_mid_grads(grads, d_y)
    d_proj, grads["hg_lb_logits"], grads["hg_norm_w"], db_hg = _hgrn2_bwd(
        proj, w["hg_lb_logits"], w["hg_norm_w"], hg_states, d_y)
    d_proj, d_qk, d_gates, grads["ml_norm_w"], db_vo = _mlstm_bwd(
        qk, proj, gates, w["ml_norm_w"], ct_s, n_s, m_s, d_y, d_proj)
    d_proj, grads["ml_conv_w"], grads["ml_conv_b"], db_qk = _ml_conv_bwd(
        proj, w["ml_conv_w"], w["ml_conv_b"], d_qk, d_proj)
    grads["b_in_main"] = jnp.concatenate([db_hg, db_qk, db_vo], axis=-1)
    grads["b_in_gate"] = jnp.sum(d_gates, axis=0, keepdims=True)
    if on_small_grads is not None:
        d_proj = on_small_grads(grads, loss, d_proj)
    grads["w_in"] = _input_projection_grads(d_proj, d_gates, x, min(1024, D_IN_MAIN), tt_big)
    if on_last_grads is not None:
        d_z1 = on_last_grads(grads, d_z1)
    grad_x = _matmul_nn_sum([(d_proj, w["w_in_t"], 0), (d_gates, w["w_in_gate_t"], 0)], d_z1, ALPHA, tm, "d_x")
    return loss, grad_x, grads


HBM_SPEC = pl.BlockSpec(memory_space=pltpu.HBM)


def _coords():
    return lax.axis_index("x"), lax.axis_index("y"), lax.axis_index("c")


def _other_chips(x, y):
    return [(1 - x, y), (x, 1 - y), (1 - x, 1 - y)]


def _my_slot():
    x, y, c = _coords()
    return 4 * x + 2 * y + c


SEM_SPEC = pl.BlockSpec(memory_space=pltpu.SEMAPHORE)
ANY_SPEC = pl.BlockSpec(memory_space=pl.ANY)
SIDE_EFFECT = pltpu.SideEffectType.DATAFLOW_SIDE_EFFECTING


def _peer(x, y, c, d):
    flip = lambda v, bit: 1 - v if bit else v
    p = (flip(x, d & 4), flip(y, d & 2), flip(c, d & 1))
    return p, 4 * p[0] + 2 * p[1] + p[2]


def _direct_copies(gather, src_refs, land_refs, send_sems, recv_sems):
    x, y, c = _coords()
    me = 4 * x + 2 * y + c
    copies = []
    for a in range(len(src_refs)):
        for d in range(1, N_DEV):
            peer, peer_slot = _peer(x, y, c, d)
            copies.append(pltpu.make_async_remote_copy(
                src_ref=src_refs[a] if gather else src_refs[a].at[peer_slot],
                dst_ref=land_refs[a].at[me] if gather else land_refs[a].at[d - 1],
                send_sem=send_sems.at[7 * a + d - 1], recv_sem=recv_sems.at[7 * a + d - 1],
                device_id=peer, device_id_type=MESH))
    return copies


def _hbm(t):
    return pltpu.HBM(t.shape, t.dtype)


def _chip_copies(src_refs, land_refs, send_sems, recv_sems):
    x, y, c = _coords()
    me = 4 * x + 2 * y + c
    targets = [(x, y, 1 - c)] + [(cx, cy, c) for cx, cy in _other_chips(x, y)]
    return [pltpu.make_async_remote_copy(
        src_ref=src_refs[a], dst_ref=land_refs[a].at[me], send_sem=send_sems.at[4 * a + k],
        recv_sem=recv_sems.at[4 * a + k], device_id=target, device_id_type=MESH)
        for a in range(len(src_refs)) for k, target in enumerate(targets)]


def _forward_copies(land_refs, send_sems, recv_sems):
    x, y, c = _coords()
    return [pltpu.make_async_remote_copy(
        src_ref=land_refs[a].at[4 * cx + 2 * cy + c], dst_ref=land_refs[a].at[4 * cx + 2 * cy + c],
        send_sem=send_sems.at[3 * a + j], recv_sem=recv_sems.at[3 * a + j],
        device_id=(x, y, 1 - c), device_id_type=MESH)
        for a in range(len(land_refs)) for j, (cx, cy) in enumerate(_other_chips(x, y))]


def _split_copy_start(make_copies, n_sems, operands, through, name):
    n_ops = len(operands)

    def body(*refs):
        for cp in make_copies(refs[:n_ops], refs[n_ops + 1], refs[n_ops + 2]):
            cp.start()

    ins = [pltpu.with_memory_space_constraint(t, pltpu.HBM) for t in (*operands, through)]
    sems = pltpu.SemaphoreType.DMA((n_sems,))
    res = pl.pallas_call(
        body, name=name, out_shape=(sems, sems, *[_hbm(t) for t in ins]),
        in_specs=[HBM_SPEC] * (n_ops + 1), out_specs=(SEM_SPEC, SEM_SPEC, *[HBM_SPEC] * (n_ops + 1)),
        input_output_aliases={i: 2 + i for i in range(n_ops + 1)},
        compiler_params=pltpu.CompilerParams(has_side_effects=SIDE_EFFECT),
    )(*ins)
    return (res[0], res[1], list(res[2:2 + n_ops])), res[2 + n_ops]


def _split_copy_wait(make_copies, started, after, name):
    send_sems, recv_sems, operands = started
    n_ops = len(operands)
    after = list(after) if isinstance(after, (list, tuple)) else [after]

    def body(*refs):
        for cp in make_copies(refs[:n_ops], refs[n_ops], refs[n_ops + 1]):
            cp.wait_send()
            cp.wait_recv()

    res = pl.pallas_call(
        body, name=name, out_shape=tuple(_hbm(t) for t in operands),
        in_specs=[HBM_SPEC] * n_ops + [SEM_SPEC, SEM_SPEC] + [ANY_SPEC] * len(after),
        out_specs=tuple([HBM_SPEC] * n_ops), input_output_aliases={i: i for i in range(n_ops)},
        compiler_params=pltpu.CompilerParams(has_side_effects=SIDE_EFFECT),
    )(*operands, send_sems, recv_sems, *after)
    return list(res)


def _halves(make_copies, na):
    return lambda refs, send_sems, recv_sems: make_copies(refs[:na], refs[na:], send_sems, recv_sems)


def _direct_start(gather, arrays, through, name):
    na = len(arrays)
    lands = [lax.empty((N_DEV,) + t.shape if gather else (N_DEV - 1,) + t.shape[1:], t.dtype) for t in arrays]
    return _split_copy_start(_halves(functools.partial(_direct_copies, gather), na), 7 * na, [*arrays, *lands],
                             through, name)


def _direct_wait(gather, started, after, name):
    na = len(started[2]) // 2
    operands = _split_copy_wait(_halves(functools.partial(_direct_copies, gather), na), started, after, name)
    return operands[:na], operands[na:]


def _slot_copies(land_refs, send_sems, recv_sems):
    x, y, c = _coords()
    me = 4 * x + 2 * y + c
    return [pltpu.make_async_remote_copy(
        src_ref=land.at[me], dst_ref=land.at[me], send_sem=send_sems.at[7 * a + d - 1],
        recv_sem=recv_sems.at[7 * a + d - 1], device_id=_peer(x, y, c, d)[0], device_id_type=MESH)
        for a, land in enumerate(land_refs) for d in range(1, N_DEV)]


def _own_slot_filled(block, slot):
    return lax.dynamic_update_index_in_dim(lax.empty((N_DEV,) + block.shape, block.dtype), block, slot, 0)


def _two_level_gather(shards, glue, name):
    na = len(shards)
    lands = [lax.empty((N_DEV,) + t.shape, t.dtype) for t in shards]
    slot = jnp.full((SUBLANES, LANES), _my_slot(), jnp.int32)
    started, slot = _split_copy_start(_halves(_chip_copies, na), 4 * na, [*shards, *lands], slot, name + "_start")
    operands = _split_copy_wait(_halves(_chip_copies, na), started, glue(slot[0, 0]), name + "_wait")
    started, mine = _split_copy_start(_forward_copies, 3 * na, operands[na:], operands[0], name + "_forward_start")
    lands = _split_copy_wait(_forward_copies, started, mine, name + "_forward_wait")
    return [lax.dynamic_update_index_in_dim(land, own, _my_slot(), 0)
            for own, land in zip([mine, *operands[1:na]], lands)]


def _row_tile(rows):
    for t in (256, 176, 128):
        if rows % t == 0 and rows > t:
            return t
    return rows


def _adamw_math(g, w, m, v):
    m_new = ADAM_B1 * m + (1.0 - ADAM_B1) * g
    v_new = ADAM_B2 * v + (1.0 - ADAM_B2) * jnp.square(g)
    m_hat = m_new / (1.0 - ADAM_B1 ** ADAM_STEP)
    v_hat = v_new / (1.0 - ADAM_B2 ** ADAM_STEP)
    delta = -ADAM_LR * (m_hat / (jnp.sqrt(v_hat) + ADAM_EPS) + ADAM_WD * w)
    return delta, m_new, v_new


def _adamw_sharded(chip, sums, got, w, m, v, name):
    r, c = w.shape
    tr = _row_tile(r)
    n_got = got.shape[0]

    def body(chip_ref, s_ref, g_ref, w_ref, m_ref, v_ref, go_ref, d_ref, nm_ref, nv_ref):
        g = s_ref[...].astype(F32)
        for i in range(n_got):
            g = g + g_ref[i].astype(F32)
        delta, m_new, v_new = _adamw_math(g, w_ref[...], m_ref[...], v_ref[...])
        go_ref[...] = g
        d_ref[...] = delta
        nm_ref[...] = m_new
        nv_ref[...] = v_new

    blk = pl.BlockSpec((tr, c), lambda i, chip_ref: (i, 0))
    out = jax.ShapeDtypeStruct((r, c), F32)
    return pl.pallas_call(
        body, name=name,
        grid_spec=pltpu.PrefetchScalarGridSpec(
            num_scalar_prefetch=1, grid=(r // tr,),
            in_specs=[pl.BlockSpec((None, tr, c), lambda i, chip_ref: (chip_ref[0], i, 0)),
                      pl.BlockSpec((n_got, tr, c), lambda i, chip_ref: (0, i, 0)), blk, blk, blk],
            out_specs=[blk, blk, blk, blk]),
        out_shape=[out, out, out, out],
        compiler_params=_params(("parallel",)),
    )(chip, sums, got, w, m, v)


def _adamw_replicated(parts, w, m, v):
    p, r, c = parts.shape
    names = SMALL_NAMES
    shapes = [w[n].shape for n in names]

    def body(*refs):
        p_ref = refs[0]
        ins = refs[1:1 + 3 * len(names)]
        outs = refs[1 + 3 * len(names):-2]
        loss_ref, sum_scr = refs[-2], refs[-1]
        total = p_ref[0]
        for i in range(1, p):
            total = total + p_ref[i]
        sum_scr[...] = total
        for k, n in enumerate(names):
            w_ref, m_ref, v_ref = ins[3 * k:3 * k + 3]
            g_ref, d_ref, nm_ref, nv_ref = outs[4 * k:4 * k + 4]
            for row, lane0, width, src_row in _small_pieces(n, shapes[k]):
                here = (slice(row, row + 1), slice(lane0, lane0 + width))
                g = sum_scr[src_row:src_row + 1, 0:width]
                delta, m_new, v_new = _adamw_math(g, w_ref[here], m_ref[here], v_ref[here])
                g_ref[here] = g
                d_ref[here] = delta
                nm_ref[here] = m_new
                nv_ref[here] = v_new
        loss_ref[...] = sum_scr[SMALL_LOSS_ROW:SMALL_LOSS_ROW + 1, 0:1]

    whole = lambda shape: pl.BlockSpec(shape, lambda i: (0,) * len(shape))
    args = [parts] + [t[n] for n in names for t in (w, m, v)]
    out_shape = [jax.ShapeDtypeStruct(s, F32) for s in shapes for _ in range(4)] + [jax.ShapeDtypeStruct((1, 1), F32)]
    res = pl.pallas_call(
        body, name="adamw_replicated", grid=(1,),
        in_specs=[whole(t.shape) for t in args], out_specs=[whole(s.shape) for s in out_shape],
        out_shape=out_shape, scratch_shapes=[pltpu.VMEM((r, c), F32)],
        compiler_params=_params(("arbitrary",)),
    )(*args)
    results = [{n: res[4 * k + j] for k, n in enumerate(names)} for j in range(4)]
    return results, res[-1]


SHARDED_NAMES = ("w_in", "ml_conv_w", "w_out", "ca_wq", "ca_wkv", "ca_wo", "ffn_w_up", "ffn_conv_w", "ffn_w_down")
SMALL_NAMES = ("b_in", "hg_lb_logits", "hg_norm_w", "ml_conv_b", "ml_norm_w", "ln1_g", "ln1_b",
               "ln2_g", "ln2_b", "ffn_conv_b", "ln3_g", "ln3_b")
WEIGHT_NAMES = ("w_in", "b_in", "hg_lb_logits", "hg_norm_w", "ml_conv_w", "ml_conv_b", "ml_norm_w", "w_out",
                "ln1_g", "ln1_b", "ca_wq", "ca_wkv", "ca_wo", "ln2_g", "ln2_b", "ffn_w_up", "ffn_conv_w",
                "ffn_conv_b", "ffn_w_down", "ln3_g", "ln3_b")
PAD_TO = {"ffn_conv_w": UP_SHARD_P}
SMALL_ROWS = 24
SMALL_W = D_MODEL
SMALL_SHAPES = {"b_in": (1, D_IN), "hg_lb_logits": (2, D_GROUP), "hg_norm_w": (1, D_GROUP),
                "ml_conv_b": (1, 2 * D_GROUP), "ml_norm_w": (1, D_GROUP), "ln1_g": (1, D_MODEL), "ln1_b": (1, D_MODEL),
                "ln2_g": (1, D_MODEL), "ln2_b": (1, D_MODEL), "ffn_conv_b": (1, D_UP), "ln3_g": (1, D_MODEL),
                "ln3_b": (1, D_MODEL)}


def _shard_2d(name, block):
    t = block[0]
    if name in PAD_TO:
        t = jnp.pad(t, ((0, 0), (0, PAD_TO[name] - t.shape[1])))
    return t


TRANSPOSED = ("w_in", "ffn_w_up")


def _update_shard(name, block):
    if name not in TRANSPOSED:
        return _shard_2d(name, block)
    return jnp.transpose(block, (0, 2, 1))[0]


def _shard_like(name, t, like):
    if name in TRANSPOSED:
        out = jnp.transpose(t[None], (0, 2, 1))
        return with_layout_constraint(out, Layout(major_to_minor=(0, 2, 1))) if name == "ffn_w_up" else out
    return t[:, :like.shape[2]][None]


def _pad_cols(t, width):
    return jnp.pad(t, ((0, 0), (0, width - t.shape[1])))


FIRST_NAMES = ("w_in", "ml_conv_w")
FFN_NAMES = ("ffn_w_up", "ffn_w_down", "ffn_conv_w")
MID_NAMES = ("ca_wo", "ca_wq", "ca_wkv", "w_out")


def _first_weights(g, small):
    w = dict(small)
    w["w_in_t"] = g["w_in"].reshape(D_IN, D_MODEL)
    w["w_in_gate_t"] = jnp.pad(w["w_in_t"][D_IN_MAIN:], ((0, LANES - (D_IN - D_IN_MAIN)), (0, 0)))
    w["b_in_main"] = small["b_in"][:, :D_IN_MAIN]
    w["b_in_gate"] = _pad_cols(small["b_in"][:, D_IN_MAIN:], LANES)
    w["ml_conv_w"] = jnp.transpose(g["ml_conv_w"], (1, 0, 2)).reshape(ML_CONV, 2 * D_GROUP)
    return w


def _mid_weights(g):
    w = {n: g[n].reshape(D_MODEL, D_MODEL) for n in ("w_out", "ca_wq", "ca_wo")}
    w["ca_wkv"] = g["ca_wkv"]
    return w


FFN_UP_NAMES = ("ffn_w_up", "ffn_conv_w")
FFN_DOWN_NAMES = ("ffn_w_down",)


def _ffn_up_weights(g, small):
    w = {"ffn_w_up_t": g["ffn_w_up"].reshape(D_UP, D_MODEL)}
    w["ffn_conv_w"] = jnp.transpose(g["ffn_conv_w"][:, :, :UP_SHARD], (1, 0, 2)).reshape(FFN_CONV, D_UP)
    w["ffn_conv_b"] = small["ffn_conv_b"].reshape(1, D_UP)
    return w


def _ffn_down_weights(g):
    return {"ffn_w_down": g["ffn_w_down"].reshape(D_FF, D_MODEL)}


def _owner_stack(n, grads):
    if n == "w_in":
        return grads[n].reshape(N_DEV, W_IN_SHARD, D_MODEL)
    if n == "ffn_w_up":
        return grads[n].reshape(N_DEV, UP_SHARD, D_MODEL)
    if n in ("w_out", "ca_wq", "ca_wo"):
        return grads[n].reshape(N_DEV, D_MODEL // N_DEV, D_MODEL)
    if n == "ffn_w_down":
        return grads[n].reshape(N_DEV, D_FF // N_DEV, D_MODEL)
    if n == "ml_conv_w":
        return jnp.transpose(grads[n].reshape(ML_CONV, N_DEV, LANES), (1, 0, 2))
    if n == "ffn_conv_w":
        shards = jnp.transpose(grads[n].reshape(FFN_CONV, N_DEV, UP_SHARD), (1, 0, 2))
        return jnp.pad(shards, ((0, 0), (0, 0), (0, UP_SHARD_P - UP_SHARD)))
    return grads[n]


def _small_grads(grads):
    out = {n: grads[n] for n in SMALL_NAMES if n in grads}
    out["b_in"] = jnp.concatenate([grads["b_in_main"], grads["b_in_gate"][:, :D_IN - D_IN_MAIN]], axis=1)
    return out


def _small_rows(shape):
    return shape[0] if shape[1] <= SMALL_W else -(-shape[1] // SMALL_W)


SMALL_BASE = {n: sum(_small_rows(SMALL_SHAPES[k]) for k in SMALL_NAMES[:i]) for i, n in enumerate(SMALL_NAMES)}
SMALL_LOSS_ROW = sum(_small_rows(SMALL_SHAPES[n]) for n in SMALL_NAMES)
assert SMALL_LOSS_ROW < SMALL_ROWS


def _small_pieces(name, shape):
    base = SMALL_BASE[name]
    if shape[1] <= SMALL_W:
        return [(i, 0, shape[1], base + i) for i in range(shape[0])]
    return [(0, k * SMALL_W, min(SMALL_W, shape[1] - k * SMALL_W), base + k) for k in range(_small_rows(shape))]


def _pack_small(p, loss):
    rows = []
    for n in SMALL_NAMES:
        t = p[n]
        nrows = _small_rows(t.shape)
        if t.shape[1] <= SMALL_W:
            rows.append(_pad_cols(t, SMALL_W))
        else:
            rows.append(_pad_cols(t, nrows * SMALL_W).reshape(nrows, SMALL_W))
    rows.append(_pad_cols(loss, SMALL_W))
    slab = jnp.concatenate(rows, axis=0)
    return jnp.pad(slab, ((0, SMALL_ROWS - slab.shape[0]), (0, 0)))


def kernel(x, mem, w_in, b_in, hg_lb_logits, hg_norm_w, ml_conv_w, ml_conv_b, ml_norm_w, w_out, ln1_g, ln1_b, ca_wq, ca_wkv, ca_wo, ln2_g, ln2_b, ffn_w_up, ffn_conv_w, ffn_conv_b, ffn_w_down, ln3_g, ln3_b, loss_target, m_w_in, m_b_in, m_hg_lb_logits, m_hg_norm_w, m_ml_conv_w, m_ml_conv_b, m_ml_norm_w, m_w_out, m_ln1_g, m_ln1_b, m_ca_wq, m_ca_wkv, m_ca_wo, m_ln2_g, m_ln2_b, m_ffn_w_up, m_ffn_conv_w, m_ffn_conv_b, m_ffn_w_down, m_ln3_g, m_ln3_b, v_w_in, v_b_in, v_hg_lb_logits, v_hg_norm_w, v_ml_conv_w, v_ml_conv_b, v_ml_norm_w, v_w_out, v_ln1_g, v_ln1_b, v_ca_wq, v_ca_wkv, v_ca_wo, v_ln2_g, v_ln2_b, v_ffn_w_up, v_ffn_conv_w, v_ffn_conv_b, v_ffn_w_down, v_ln3_g, v_ln3_b):
    params = dict(w_in=w_in, b_in=b_in, hg_lb_logits=hg_lb_logits, hg_norm_w=hg_norm_w, ml_conv_w=ml_conv_w,
                  ml_conv_b=ml_conv_b, ml_norm_w=ml_norm_w, w_out=w_out, ln1_g=ln1_g, ln1_b=ln1_b, ca_wq=ca_wq,
                  ca_wkv=ca_wkv, ca_wo=ca_wo, ln2_g=ln2_g, ln2_b=ln2_b, ffn_w_up=ffn_w_up, ffn_conv_w=ffn_conv_w,
                  ffn_conv_b=ffn_conv_b, ffn_w_down=ffn_w_down, ln3_g=ln3_g, ln3_b=ln3_b)
    mom1 = dict(w_in=m_w_in, b_in=m_b_in, hg_lb_logits=m_hg_lb_logits, hg_norm_w=m_hg_norm_w,
                ml_conv_w=m_ml_conv_w, ml_conv_b=m_ml_conv_b, ml_norm_w=m_ml_norm_w, w_out=m_w_out, ln1_g=m_ln1_g,
                ln1_b=m_ln1_b, ca_wq=m_ca_wq, ca_wkv=m_ca_wkv, ca_wo=m_ca_wo, ln2_g=m_ln2_g, ln2_b=m_ln2_b,
                ffn_w_up=m_ffn_w_up, ffn_conv_w=m_ffn_conv_w, ffn_conv_b=m_ffn_conv_b, ffn_w_down=m_ffn_w_down,
                ln3_g=m_ln3_g, ln3_b=m_ln3_b)
    mom2 = dict(w_in=v_w_in, b_in=v_b_in, hg_lb_logits=v_hg_lb_logits, hg_norm_w=v_hg_norm_w,
                ml_conv_w=v_ml_conv_w, ml_conv_b=v_ml_conv_b, ml_norm_w=v_ml_norm_w, w_out=v_w_out, ln1_g=v_ln1_g,
                ln1_b=v_ln1_b, ca_wq=v_ca_wq, ca_wkv=v_ca_wkv, ca_wo=v_ca_wo, ln2_g=v_ln2_g, ln2_b=v_ln2_b,
                ffn_w_up=v_ffn_w_up, ffn_conv_w=v_ffn_conv_w, ffn_conv_b=v_ffn_conv_b, ffn_w_down=v_ffn_w_down,
                ln3_g=v_ln3_g, ln3_b=v_ln3_b)

    x_idx, y_idx, c_idx = _coords()
    as_index = lambda v: jnp.reshape(v, (1,)).astype(jnp.int32)
    me = as_index(4 * x_idx + 2 * y_idx + c_idx)
    small_params = {n: params[n] for n in SMALL_NAMES}

    shards = {n: _update_shard(n, params[n]) for n in SHARDED_NAMES}
    m_shards = {n: _update_shard(n, mom1[n]) for n in SHARDED_NAMES}
    v_shards = {n: _update_shard(n, mom2[n]) for n in SHARDED_NAMES}
    outgoing = {n: shards[n] if "conv" in n else shards[n].astype(BF16) for n in SHARDED_NAMES}
    to_send = lambda names: [outgoing[n] for n in names]
    late = {}

    def glue(slot):
        late.update({n: _own_slot_filled(outgoing[n], slot) for n in SHARDED_NAMES if n not in FIRST_NAMES})
        return [*m_shards.values(), *v_shards.values(), *shards.values(), *late.values()]

    first = dict(zip(FIRST_NAMES, _two_level_gather(to_send(FIRST_NAMES), glue, "weights_gather_first")))

    def start_gather(names, through, tag):
        return _split_copy_start(_slot_copies, 7 * len(names), [late[n] for n in names], through,
                                 "weights_gather_start_" + tag)

    mid_started, through = start_gather(MID_NAMES, first["w_in"], "mid")
    ffn_started, through = start_gather(FFN_UP_NAMES, through, "ffn_up")
    down_started, first["w_in"] = start_gather(FFN_DOWN_NAMES, through, "ffn_down")

    def gathered_weights(names, started, after, tag):
        return dict(zip(names, _split_copy_wait(_slot_copies, started, after, "weights_gather_wait_" + tag)))

    started = {}

    def start_group(names, tag):
        def hook(grads, through):
            stacks = [_owner_stack(n, grads).astype(BF16) for n in names]
            started[tag], through = _direct_start(False, stacks, through, "grads_start_" + tag)
            return through
        return hook

    def start_small(grads, loss, through):
        started["small"], through = _direct_start(True, [_pack_small(_small_grads(grads), loss)], through,
                                                  "small_gather_start")
        return through

    loss, grad_x, grads = _local_step(
        x[0], mem[0], loss_target[0], _first_weights(first, small_params),
        lambda y: _mid_weights(gathered_weights(MID_NAMES, mid_started, y, "mid")),
        lambda x2: _ffn_up_weights(gathered_weights(FFN_UP_NAMES, ffn_started, x2, "ffn_up"), small_params),
        lambda hid: _ffn_down_weights(gathered_weights(FFN_DOWN_NAMES, down_started, hid, "ffn_down")),
        start_group(FFN_NAMES, "ffn"), start_group(MID_NAMES, "mid"), start_small, start_group(FIRST_NAMES, "last"))

    updated, sharded_out = {}, {}

    def update_group(names, tag, after):
        stacks, lands = _direct_wait(False, started[tag], after, "grads_wait_" + tag)
        for n, st, land in zip(names, stacks, lands):
            updated[n] = _adamw_sharded(me, st, land, shards[n], m_shards[n], v_shards[n], "adamw_" + n)
            sharded_out[n] = [_shard_like(n, t, params[n]) for t in updated[n]]

    update_group(FFN_NAMES, "ffn", grad_x)
    update_group(MID_NAMES, "mid", grad_x)
    own_small, small_lands = _direct_wait(True, started["small"], grad_x, "small_gather_wait")
    small_parts = lax.dynamic_update_index_in_dim(small_lands[0], own_small[0], me[0], 0)
    small_out, total_loss = _adamw_replicated(small_parts, small_params, {n: mom1[n] for n in SMALL_NAMES},
                                              {n: mom2[n] for n in SMALL_NAMES})
    done = [t for n in FFN_NAMES + MID_NAMES for t in updated[n]]
    done += [t for small in small_out for t in small.values()]
    update_group(FIRST_NAMES, "last", done)

    outs = []
    for k, small in enumerate(small_out):
        outs.extend(sharded_out[n][k] if n in sharded_out else small[n] for n in WEIGHT_NAMES)
    return (total_loss[0, 0], grad_x[None], *outs)
```

```python
import functools
import math

import jax
import jax.numpy as jnp
from jax import lax
from jax.experimental import pallas as pl
from jax.experimental.layout import Layout, with_layout_constraint
from jax.experimental.pallas import tpu as pltpu

F32 = jnp.float32
BF16 = jnp.bfloat16
HIGHEST = lax.Precision.HIGHEST
MESH = pl.DeviceIdType.MESH

N_DEV = 8
D_MODEL = 1024
N_MEM = 256
N_HEADS = 4
D_HEAD = 128
D_GROUP = N_HEADS * D_HEAD
CHUNK = 64
ML_CONV = 4
FFN_CONV = 3
D_FF = 2816
D_UP = 2 * D_FF
CA_HEADS = 4
CA_DH = D_MODEL // CA_HEADS
LANES = 128
SUBLANES = 8
D_IN = 8 * D_GROUP + 2 * N_HEADS
D_IN_MAIN = 8 * D_GROUP
W_IN_SHARD = D_IN // N_DEV
UP_SHARD = D_UP // N_DEV
UP_SHARD_P = 768
UP_TILE = D_UP // 4
ALPHA = 2.0 ** 0.25
LN_EPS = 1e-5
NEG_BIG = -1e30
ADAM_LR = 0.001
ADAM_B1 = 0.9
ADAM_B2 = 0.999
ADAM_EPS = 1e-08
ADAM_WD = 0.01
ADAM_STEP = 10
VMEM_LIMIT = 56 * 1024 * 1024

SEG_MQ = 4 * D_GROUP // LANES
VO_BLOCK = 3


def _params(sem):
    return pltpu.CompilerParams(dimension_semantics=sem, vmem_limit_bytes=VMEM_LIMIT)


def _dg(a, b, ca, cb, precision=None):
    return lax.dot_general(a, b, (((ca,), (cb,)), ((), ())), precision=precision,
                           preferred_element_type=F32)


def _nn_raw(a, b):
    return _dg(a.astype(BF16), b.astype(BF16), 1, 0)


def _nt_raw(a, b):
    return _dg(a.astype(BF16), b.astype(BF16), 1, 1)


def _tn_raw(a, b):
    return _dg(a.astype(BF16), b.astype(BF16), 0, 0)


@jax.custom_vjp
def _nn(a, b):
    return _nn_raw(a, b)


_nn.defvjp(lambda a, b: (_nn_raw(a, b), (a, b)),
           lambda res, g: (_nt_raw(g, res[1]), _tn_raw(res[0], g)))


@jax.custom_vjp
def _nt(a, b):
    return _nt_raw(a, b)


_nt.defvjp(lambda a, b: (_nt_raw(a, b), (a, b)),
           lambda res, g: (_nn_raw(g, res[1]), _tn_raw(g, res[0])))


@jax.custom_vjp
def _tn(a, b):
    return _tn_raw(a, b)


_tn.defvjp(lambda a, b: (_tn_raw(a, b), (a, b)),
           lambda res, g: (_nt_raw(res[1], g), _nn_raw(res[0], g)))


def _layer_norm(z, g, b):
    mu = jnp.mean(z, axis=-1, keepdims=True)
    var = jnp.mean(jnp.square(z - mu), axis=-1, keepdims=True)
    return (z - mu) * lax.rsqrt(var + LN_EPS) * g + b


def _matmul_nn(a, w, bias, tm, tn, name, out_dtype=F32):
    m, k = a.shape
    if w.ndim == 3:
        n = w.shape[0] * w.shape[2]
        assert tn == w.shape[2]
        w_spec = pl.BlockSpec((None, k, tn), lambda i, j: (j, 0, 0))
    else:
        n = w.shape[1]
        w_spec = pl.BlockSpec((k, tn), lambda i, j: (0, j))

    def body(*refs):
        a_ref, w_ref = refs[0], refs[1]
        o_ref = refs[-1]
        acc = _nn_raw(a_ref[...], w_ref[...])
        if bias is not None:
            acc = acc + refs[2][...]
        o_ref[...] = acc.astype(o_ref.dtype)

    in_specs = [pl.BlockSpec((tm, k), lambda i, j: (i, 0)), w_spec]
    args = [a, w]
    if bias is not None:
        in_specs.append(pl.BlockSpec((1, tn), lambda i, j: (0, j)))
        args.append(bias)
    return pl.pallas_call(
        body, name=name, grid=(m // tm, n // tn), in_specs=in_specs,
        out_specs=pl.BlockSpec((tm, tn), lambda i, j: (i, j)),
        out_shape=jax.ShapeDtypeStruct((m, n), out_dtype),
        compiler_params=_params(("parallel", "parallel")),
    )(*args)


def _matmul_nt(d, w, tm, tk, name, k_out=None, bias=None, out_dtype=F32):
    m, n = d.shape
    k = k_out or w.shape[0]

    def body(*refs):
        acc = _nt_raw(refs[0][...], refs[1][...])
        if bias is not None:
            acc = acc + refs[2][...]
        refs[-1][...] = acc.astype(refs[-1].dtype)

    in_specs = [pl.BlockSpec((tm, n), lambda i, j: (i, 0)), pl.BlockSpec((tk, n), lambda i, j: (j, 0))]
    args = [d, w]
    if bias is not None:
        in_specs.append(pl.BlockSpec((1, tk), lambda i, j: (0, j)))
        args.append(bias)
    return pl.pallas_call(
        body, name=name, grid=(m // tm, k // tk), in_specs=in_specs,
        out_specs=pl.BlockSpec((tm, tk), lambda i, j: (i, j)),
        out_shape=jax.ShapeDtypeStruct((m, k), out_dtype),
        compiler_params=_params(("parallel", "parallel")),
    )(*args)


def _input_projection(x, w_t, w_gate_t, b_main, b_gate, tm, tk):
    m, n = x.shape

    def body(x_ref, w_ref, wg_ref, b_ref, bg_ref, o_ref, g_ref):
        lhs = x_ref[...].astype(BF16)
        o_ref[...] = _nt_raw(lhs, w_ref[...]) + b_ref[...]

        @pl.when(pl.program_id(1) == 0)
        def _():
            g_ref[...] = _nt_raw(lhs, wg_ref[...]) + bg_ref[...]

    return pl.pallas_call(
        body, name="proj", grid=(m // tm, D_IN_MAIN // tk),
        in_specs=[pl.BlockSpec((tm, n), lambda i, j: (i, 0)), pl.BlockSpec((tk, n), lambda i, j: (j, 0)),
                  pl.BlockSpec((LANES, n), lambda i, j: (0, 0)), pl.BlockSpec((1, tk), lambda i, j: (0, j)),
                  pl.BlockSpec((1, LANES), lambda i, j: (0, 0))],
        out_specs=[pl.BlockSpec((tm, tk), lambda i, j: (i, j)), pl.BlockSpec((tm, LANES), lambda i, j: (i, 0))],
        out_shape=[jax.ShapeDtypeStruct((m, D_IN_MAIN), F32), jax.ShapeDtypeStruct((m, LANES), F32)],
        compiler_params=_params(("parallel", "arbitrary")),
    )(x, w_t, w_gate_t, b_main, b_gate)


def _input_projection_grads(d_proj, d_gates, x, tm, tt):
    t, m = d_proj.shape
    n = x.shape[1]
    last = t // tt - 1
    n_gate = D_IN - D_IN_MAIN

    def body(a_ref, g_ref, x_ref, o_ref, acc_ref, accg_ref):
        i, kk = pl.program_id(0), pl.program_id(1)

        @pl.when(kk == 0)
        def _():
            acc_ref[...] = jnp.zeros_like(acc_ref)

        @pl.when((kk == 0) & (i == 0))
        def _():
            accg_ref[...] = jnp.zeros_like(accg_ref)

        rhs = x_ref[...].astype(BF16)
        acc_ref[...] += _tn_raw(a_ref[...], rhs)

        @pl.when(i == 0)
        def _():
            accg_ref[...] += _tn_raw(g_ref[...], rhs)

        @pl.when(kk == last)
        def _():
            o_ref[pl.ds(pl.multiple_of(i * tm, tm), tm), :] = acc_ref[...].astype(o_ref.dtype)

        @pl.when((kk == last) & (i == 0))
        def _():
            o_ref[m:m + n_gate, :] = accg_ref[0:n_gate, :].astype(o_ref.dtype)

    return pl.pallas_call(
        body, name="d_w_in", grid=(m // tm, t // tt),
        in_specs=[pl.BlockSpec((tt, tm), lambda i, kk: (kk, i)), pl.BlockSpec((tt, LANES), lambda i, kk: (kk, 0)),
                  pl.BlockSpec((tt, n), lambda i, kk: (kk, 0))],
        out_specs=pl.BlockSpec((m + n_gate, n), lambda i, kk: (0, 0)),
        out_shape=jax.ShapeDtypeStruct((m + n_gate, n), BF16),
        scratch_shapes=[pltpu.VMEM((tm, n), F32), pltpu.VMEM((LANES, n), F32)],
        compiler_params=_params(("arbitrary", "arbitrary")),
    )(d_proj, d_gates, x)


def _matmul_nn_sum(pairs, add, scale, tm, name):
    m = pairs[0][0].shape[0]
    n = pairs[0][1].shape[1]
    in_specs, args = [], []
    for a, w, row0 in pairs:
        kk = a.shape[1]
        in_specs += [pl.BlockSpec((tm, kk), lambda i: (i, 0)),
                     pl.BlockSpec((kk, n), lambda i, blk=row0 // kk: (blk, 0))]
        args += [a, w]
    if add is not None:
        in_specs.append(pl.BlockSpec((tm, n), lambda i: (i, 0)))
        args.append(add)

    def body(*refs):
        acc = None
        for p in range(len(pairs)):
            term = _nn_raw(refs[2 * p][...], refs[2 * p + 1][...])
            acc = term if acc is None else acc + term
        if add is not None:
            acc = acc + scale * refs[2 * len(pairs)][...]
        refs[-1][...] = acc

    return pl.pallas_call(
        body, name=name, grid=(m // tm,), in_specs=in_specs,
        out_specs=pl.BlockSpec((tm, n), lambda i: (i, 0)),
        out_shape=jax.ShapeDtypeStruct((m, n), F32),
        compiler_params=_params(("parallel",)),
    )(*args)


def _matmul_tn(a, b, tm, tn, tt, name, shards=None, shard0=0, group=1, into=None, colsum=False, rows=None, row0=0):
    t, m = a.shape
    n = b.shape[1]
    assert not colsum or tm == m
    n_in = 2 + (into is not None)
    out_dtype = BF16
    per_step = 1 if shards is None else group
    width = per_step * tn

    def body(*refs):
        a_ref, b_ref = refs[0], refs[1]
        o_ref, acc_ref = refs[n_in], refs[-1]
        first = pl.program_id(2) == 0

        @pl.when(first)
        def _():
            acc_ref[...] = jnp.zeros_like(acc_ref)

        if shards is None:
            acc_ref[...] += _tn_raw(a_ref[...], b_ref[...])
        else:
            lhs = a_ref[...].astype(BF16)
            for g in range(per_step):
                acc_ref[g] += _tn_raw(lhs, b_ref[:, g * tn:(g + 1) * tn])

        @pl.when(pl.program_id(2) == t // tt - 1)
        def _():
            o_ref[...] = acc_ref[...].astype(o_ref.dtype)

        if colsum:
            s_ref = refs[n_in + 1]

            @pl.when(first)
            def _():
                s_ref[...] = jnp.zeros_like(s_ref)

            s_ref[...] += jnp.sum(b_ref[...], axis=0, keepdims=True)

    in_specs = [pl.BlockSpec((tt, tm), lambda i, j, kk: (kk, i)),
                pl.BlockSpec((tt, width), lambda i, j, kk: (kk, j))]
    args = [a, b]
    aliases = {}
    if into is not None:
        in_specs.append(pl.BlockSpec(memory_space=pl.ANY))
        args.append(into)
        aliases = {2: 0}
    if shards is None:
        out_specs = [pl.BlockSpec((tm, tn), lambda i, j, kk: (row0 // tm + i, j))]
        out_shape = [jax.ShapeDtypeStruct((rows or m, n), out_dtype)]
        acc = pltpu.VMEM((tm, tn), F32)
    else:
        out_specs = [pl.BlockSpec((per_step, tm, tn), lambda i, j, kk: (shard0 // per_step + j, i, 0))]
        out_shape = [jax.ShapeDtypeStruct((shards, m, tn), out_dtype)]
        acc = pltpu.VMEM((per_step, tm, tn), F32)
    if colsum:
        out_specs.append(pl.BlockSpec((1, tn), lambda i, j, kk: (0, j)))
        out_shape.append(jax.ShapeDtypeStruct((1, n), F32))
    res = pl.pallas_call(
        body, name=name, grid=(m // tm, n // width, t // tt), in_specs=in_specs, out_specs=out_specs,
        out_shape=out_shape, input_output_aliases=aliases, scratch_shapes=[acc],
        compiler_params=_params(("parallel", "parallel", "arbitrary")),
    )(*args)
    return res if colsum else res[0]


ROW_TILE = 64


def _stack(ref, start, rows):
    return ref[pl.ds(start, rows), :].astype(F32).reshape(rows // SUBLANES, SUBLANES, LANES)


def _vreg_rows(ref, n):
    return [jnp.broadcast_to(ref[j:j + 1, :], (SUBLANES, LANES))[None] for j in range(n)]


def _column_total(acc):
    return jnp.sum(acc, axis=0, keepdims=True)


def _conv_fwd_tile(pad_ref, taps_w, bias, r0, rows):
    taps = len(taps_w)
    acc = bias
    for j in range(taps):
        acc = acc + _stack(pad_ref, SUBLANES - (taps - 1 - j) + r0, rows) * taps_w[j]
    return acc


def _conv_grads_tile(pad_ref, dpad_ref, dx_ref, taps_w, dws, r0, rows):
    taps = len(taps_w)
    x_rows = _stack(pad_ref, SUBLANES + r0, rows)
    dx = None
    for j in range(taps):
        d_shifted = _stack(dpad_ref, r0 + (taps - 1 - j), rows)
        term = d_shifted * taps_w[j]
        dx = term if dx is None else dx + term
        dws[j] = dws[j] + jnp.sum(d_shifted * x_rows, axis=0)
    dx_ref[r0:r0 + rows, :] = dx.reshape(rows, LANES).astype(dx_ref.dtype)
    return jnp.sum(dx, axis=0)


def _ml_conv_fwd(proj, conv_w, conv_b):
    s = proj.shape[0]
    nblk = 2 * D_GROUP // LANES

    def body(x_ref, w_ref, b_ref, o_ref, pad_ref):
        pad_ref[0:SUBLANES, :] = jnp.zeros((SUBLANES, LANES), F32)
        pad_ref[SUBLANES:, :] = x_ref[...].astype(F32)
        taps_w, bias = _vreg_rows(w_ref, ML_CONV), _vreg_rows(b_ref, 1)[0]
        for r0 in range(0, s, ROW_TILE):
            rows = min(ROW_TILE, s - r0)
            o_ref[r0:r0 + rows, :] = jax.nn.silu(_conv_fwd_tile(pad_ref, taps_w, bias, r0, rows)).reshape(rows, LANES)

    return pl.pallas_call(
        body, name="ml_conv_fwd", grid=(nblk,),
        in_specs=[pl.BlockSpec((s, LANES), lambda j: (0, SEG_MQ + j)),
                  pl.BlockSpec((ML_CONV, LANES), lambda j: (0, j)),
                  pl.BlockSpec((1, LANES), lambda j: (0, j))],
        out_specs=pl.BlockSpec((s, LANES), lambda j: (0, j)),
        out_shape=jax.ShapeDtypeStruct((s, 2 * D_GROUP), F32),
        scratch_shapes=[pltpu.VMEM((s + SUBLANES, LANES), F32)],
        compiler_params=_params(("parallel",)),
    )(proj, conv_w, conv_b)


def _ml_conv_bwd(proj, conv_w, conv_b, d_qk, d_proj):
    s = proj.shape[0]
    nblk = 2 * D_GROUP // LANES

    def body(x_ref, w_ref, b_ref, dy_ref, _, dx_ref, dw_ref, db_ref, dxs_ref, pad_ref, dpad_ref):
        pad_ref[0:SUBLANES, :] = jnp.zeros((SUBLANES, LANES), F32)
        pad_ref[SUBLANES:, :] = x_ref[...].astype(F32)
        dpad_ref[s:, :] = jnp.zeros((SUBLANES, LANES), F32)
        taps_w, bias = _vreg_rows(w_ref, ML_CONV), _vreg_rows(b_ref, 1)[0]
        db = jnp.zeros((SUBLANES, LANES), F32)
        for r0 in range(0, s, ROW_TILE):
            rows = min(ROW_TILE, s - r0)
            pre = _conv_fwd_tile(pad_ref, taps_w, bias, r0, rows)
            _, vjp = jax.vjp(jax.nn.silu, pre)
            d_pre, = vjp(_stack(dy_ref, r0, rows))
            dpad_ref[r0:r0 + rows, :] = d_pre.reshape(rows, LANES)
            db = db + jnp.sum(d_pre, axis=0)
        db_ref[...] = _column_total(db)
        dws = [jnp.zeros((SUBLANES, LANES), F32) for _ in range(ML_CONV)]
        dx_sum = jnp.zeros((SUBLANES, LANES), F32)
        for r0 in range(0, s, ROW_TILE):
            dx_sum = dx_sum + _conv_grads_tile(pad_ref, dpad_ref, dx_ref, taps_w, dws, r0, min(ROW_TILE, s - r0))
        dxs_ref[...] = _column_total(dx_sum)
        for j in range(ML_CONV):
            dw_ref[j:j + 1, :] = _column_total(dws[j])

    return pl.pallas_call(
        body, name="ml_conv_bwd", grid=(nblk,),
        in_specs=[pl.BlockSpec((s, LANES), lambda j: (0, SEG_MQ + j)),
                  pl.BlockSpec((ML_CONV, LANES), lambda j: (0, j)),
                  pl.BlockSpec((1, LANES), lambda j: (0, j)),
                  pl.BlockSpec((s, LANES), lambda j: (0, j)),
                  pl.BlockSpec(memory_space=pl.ANY)],
        out_specs=[pl.BlockSpec((s, LANES), lambda j: (0, SEG_MQ + j)),
                   pl.BlockSpec((ML_CONV, LANES), lambda j: (0, j)),
                   pl.BlockSpec((1, LANES), lambda j: (0, j)),
                   pl.BlockSpec((1, LANES), lambda j: (0, j))],
        out_shape=[jax.ShapeDtypeStruct(d_proj.shape, d_proj.dtype),
                   jax.ShapeDtypeStruct((ML_CONV, 2 * D_GROUP), F32),
                   jax.ShapeDtypeStruct((1, 2 * D_GROUP), F32),
                   jax.ShapeDtypeStruct((1, 2 * D_GROUP), F32)],
        input_output_aliases={4: 0},
        scratch_shapes=[pltpu.VMEM((s + SUBLANES, LANES), F32), pltpu.VMEM((s + SUBLANES, LANES), F32)],
        compiler_params=_params(("parallel",)),
    )(proj, conv_w, conv_b, d_qk, d_proj)


def _gelu_mul(a, b):
    return jax.nn.gelu(a) * b


GELU_C = math.sqrt(2.0 / math.pi)
GELU_K = 0.044715


def _gelu_mul_grads(a, b, d):
    a2 = a * a
    t = jnp.tanh(GELU_C * (a + GELU_K * (a * a2)))
    cdf = 0.5 * (1.0 + t)
    slope = cdf + (0.5 * GELU_C) * a * (1.0 - t * t) * (1.0 + (3.0 * GELU_K) * a2)
    return d * b * slope, d * (a * cdf)


FFN_BLOCKS = D_FF // LANES


def _ffn_conv_fwd(u, conv_w, conv_b):
    s = u.shape[0]

    def body(g_ref, v_ref, wg_ref, wv_ref, bg_ref, bv_ref, o_ref, gpad_ref, vpad_ref):
        for pad_ref, x_ref in ((gpad_ref, g_ref), (vpad_ref, v_ref)):
            pad_ref[0:SUBLANES, :] = jnp.zeros((SUBLANES, LANES), F32)
            pad_ref[SUBLANES:, :] = x_ref[...].astype(F32)
        taps_g, bias_g = _vreg_rows(wg_ref, FFN_CONV), _vreg_rows(bg_ref, 1)[0]
        taps_v, bias_v = _vreg_rows(wv_ref, FFN_CONV), _vreg_rows(bv_ref, 1)[0]
        for r0 in range(0, s, ROW_TILE):
            rows = min(ROW_TILE, s - r0)
            ug = _conv_fwd_tile(gpad_ref, taps_g, bias_g, r0, rows)
            uv = _conv_fwd_tile(vpad_ref, taps_v, bias_v, r0, rows)
            o_ref[r0:r0 + rows, :] = _gelu_mul(ug, uv).reshape(rows, LANES).astype(o_ref.dtype)

    col = lambda off: (lambda j: (0, off + j))
    return pl.pallas_call(
        body, name="ffn_conv_fwd", grid=(FFN_BLOCKS,),
        in_specs=[pl.BlockSpec((s, LANES), col(0)), pl.BlockSpec((s, LANES), col(FFN_BLOCKS)),
                  pl.BlockSpec((FFN_CONV, LANES), col(0)), pl.BlockSpec((FFN_CONV, LANES), col(FFN_BLOCKS)),
                  pl.BlockSpec((1, LANES), col(0)), pl.BlockSpec((1, LANES), col(FFN_BLOCKS))],
        out_specs=pl.BlockSpec((s, LANES), col(0)),
        out_shape=jax.ShapeDtypeStruct((s, D_FF), BF16),
        scratch_shapes=[pltpu.VMEM((s + SUBLANES, LANES), F32), pltpu.VMEM((s + SUBLANES, LANES), F32)],
        compiler_params=_params(("parallel",)),
    )(u, u, conv_w, conv_w, conv_b, conv_b)


def _ffn_conv_bwd(u, conv_w, conv_b, d_h):
    s = u.shape[0]

    def body(g_ref, v_ref, wg_ref, wv_ref, bg_ref, bv_ref, dh_ref,
             dug_ref, duv_ref, dwg_ref, dwv_ref, dbg_ref, dbv_ref,
             gpad_ref, vpad_ref, dgpad_ref, dvpad_ref):
        for pad_ref, x_ref in ((gpad_ref, g_ref), (vpad_ref, v_ref)):
            pad_ref[0:SUBLANES, :] = jnp.zeros((SUBLANES, LANES), F32)
            pad_ref[SUBLANES:, :] = x_ref[...].astype(F32)
        dgpad_ref[s:, :] = jnp.zeros((SUBLANES, LANES), F32)
        dvpad_ref[s:, :] = jnp.zeros((SUBLANES, LANES), F32)
        taps_g, bias_g = _vreg_rows(wg_ref, FFN_CONV), _vreg_rows(bg_ref, 1)[0]
        taps_v, bias_v = _vreg_rows(wv_ref, FFN_CONV), _vreg_rows(bv_ref, 1)[0]
        dbg = jnp.zeros((SUBLANES, LANES), F32)
        dbv = jnp.zeros((SUBLANES, LANES), F32)
        for r0 in range(0, s, ROW_TILE):
            rows = min(ROW_TILE, s - r0)
            ug = _conv_fwd_tile(gpad_ref, taps_g, bias_g, r0, rows)
            uv = _conv_fwd_tile(vpad_ref, taps_v, bias_v, r0, rows)
            d_ug, d_uv = _gelu_mul_grads(ug, uv, _stack(dh_ref, r0, rows))
            dgpad_ref[r0:r0 + rows, :] = d_ug.reshape(rows, LANES)
            dvpad_ref[r0:r0 + rows, :] = d_uv.reshape(rows, LANES)
            dbg = dbg + jnp.sum(d_ug, axis=0)
            dbv = dbv + jnp.sum(d_uv, axis=0)
        dbg_ref[...] = _column_total(dbg)
        dbv_ref[...] = _column_total(dbv)
        for pad_ref, dpad_ref, taps_w, dx_ref, dw_ref in ((gpad_ref, dgpad_ref, taps_g, dug_ref, dwg_ref),
                                                          (vpad_ref, dvpad_ref, taps_v, duv_ref, dwv_ref)):
            dws = [jnp.zeros((SUBLANES, LANES), F32) for _ in range(FFN_CONV)]
            for r0 in range(0, s, ROW_TILE):
                _conv_grads_tile(pad_ref, dpad_ref, dx_ref, taps_w, dws, r0, min(ROW_TILE, s - r0))
            for j in range(FFN_CONV):
                dw_ref[j:j + 1, :] = _column_total(dws[j])

    col = lambda off: (lambda j: (0, off + j))
    seq = pl.BlockSpec((s, LANES), col(0))
    return pl.pallas_call(
        body, name="ffn_conv_bwd", grid=(FFN_BLOCKS,),
        in_specs=[pl.BlockSpec((s, LANES), col(0)), pl.BlockSpec((s, LANES), col(FFN_BLOCKS)),
                  pl.BlockSpec((FFN_CONV, LANES), col(0)), pl.BlockSpec((FFN_CONV, LANES), col(FFN_BLOCKS)),
                  pl.BlockSpec((1, LANES), col(0)), pl.BlockSpec((1, LANES), col(FFN_BLOCKS)), seq],
        out_specs=[seq, seq, pl.BlockSpec((FFN_CONV, LANES), col(0)), pl.BlockSpec((FFN_CONV, LANES), col(0)),
                   pl.BlockSpec((1, LANES), col(0)), pl.BlockSpec((1, LANES), col(0))],
        out_shape=[jax.ShapeDtypeStruct((s, D_FF), BF16), jax.ShapeDtypeStruct((s, D_FF), BF16),
                   jax.ShapeDtypeStruct((FFN_CONV, D_FF), F32), jax.ShapeDtypeStruct((FFN_CONV, D_FF), F32),
                   jax.ShapeDtypeStruct((1, D_FF), F32), jax.ShapeDtypeStruct((1, D_FF), F32)],
        scratch_shapes=[pltpu.VMEM((s + SUBLANES, LANES), F32) for _ in range(4)],
        compiler_params=_params(("parallel",)),
    )(u, u, conv_w, conv_w, conv_b, conv_b, d_h)


def _chunk_masks(c):
    row = lax.broadcasted_iota(jnp.int32, (c, c), 0)
    col = lax.broadcasted_iota(jnp.int32, (c, c), 1)
    return row, col


@jax.custom_vjp
def _split_heads(x):
    return tuple(x[:, h * D_HEAD:(h + 1) * D_HEAD] for h in range(N_HEADS))


_split_heads.defvjp(lambda x: (_split_heads(x), None), lambda _, gs: (jnp.concatenate(gs, axis=1),))


@jax.custom_vjp
def _merge_heads(xs):
    return jnp.concatenate(xs, axis=1)


_merge_heads.defvjp(lambda xs: (_merge_heads(xs), None), lambda _, g: (_split_heads(g),))


@jax.custom_vjp
def _split_chunks(x):
    return tuple(x[i * CHUNK:(i + 1) * CHUNK] for i in range(x.shape[0] // CHUNK))


_split_chunks.defvjp(lambda x: (_split_chunks(x), None), lambda _, gs: (jnp.concatenate(gs, axis=0),))


@jax.custom_vjp
def _merge_chunks(xs):
    return jnp.concatenate(xs, axis=0)


_merge_chunks.defvjp(lambda xs: (_merge_chunks(xs), None), lambda _, g: (_split_chunks(g),))


def _blocks(x):
    return [_split_heads(rows) for rows in _split_chunks(x)]


def _per_chunk_rows(per_chunk, rid):
    out = per_chunk[0]
    for i in range(1, len(per_chunk)):
        out = jnp.where(rid >= i * CHUNK, per_chunk[i], out)
    return out


HEADS = range(N_HEADS)
CHUNKS_PER_STEP = 8
ML_CHUNKS_PER_STEP = 1


def _hg_chunk(hq, hf, hi, hgate, l0, l1, nw, sts):
    n = hq.shape[0] // CHUNK
    causal = _chunk_masks(CHUNK)
    causal = causal[1] <= causal[0]
    mx = lax.stop_gradient(jnp.maximum(l0, l1))
    e0 = jnp.exp(l0 - mx)
    e1 = jnp.exp(l1 - mx)
    lb = e0 / (e0 + e1)
    sig = jax.nn.sigmoid(hf)
    lf = jnp.log(lb + (1.0 - lb) * sig)
    k = (1.0 - lb) * jax.nn.sigmoid(-hf)
    q = jax.nn.silu(hq)
    tri = causal.astype(F32)
    b = _merge_chunks(tuple(_dg(tri, rows, 1, 0, HIGHEST) for rows in _split_chunks(lf)))
    rid = lax.broadcasted_iota(jnp.int32, b.shape, 0)
    pick = lambda r: jnp.sum(jnp.where(rid == r, b, 0.0), axis=0, keepdims=True)
    b_last_c = [pick(i * CHUNK + CHUNK - 1) for i in range(n)]
    b_ref = _per_chunk_rows([pick(i * CHUNK + CHUNK // 2 - 1) for i in range(n)], rid)
    b_last = _per_chunk_rows(b_last_c, rid)
    qa = _blocks(q * jnp.exp(b - b_ref))
    ka = _blocks(k * jnp.exp(b_ref - b))
    qe = _blocks(q * jnp.exp(b))
    kd = _blocks(k * jnp.exp(b_last - b))
    decay = [_split_heads(jnp.exp(b_last_c[i])) for i in range(n)]
    v = _blocks(hi)
    chunks = range(n)
    attn = [[jnp.where(causal, _nt(qa[i][h], ka[i][h]), 0.0) for h in HEADS] for i in chunks]
    intra = [[_nn(attn[i][h], v[i][h]) for h in HEADS] for i in chunks]
    kv = [[_tn(v[i][h], kd[i][h]) for h in HEADS] for i in chunks]
    normed = []
    for i in chunks:
        inter = [_nt(qe[i][h], sts[h]) for h in HEADS]
        sts = tuple(decay[i][h] * sts[h] + kv[i][h] for h in HEADS)
        o = [intra[i][h] + inter[h] for h in HEADS]
        normed.append(_merge_heads(tuple(o[h] * lax.rsqrt(jnp.mean(o[h] * o[h], axis=-1, keepdims=True) + LN_EPS)
                                         for h in HEADS)))
    return _merge_chunks(tuple(normed)) * nw * jax.nn.silu(hgate), sts


def _seg(ref, seg):
    return ref[:, seg * D_GROUP:(seg + 1) * D_GROUP]


def _hgrn2_fwd(proj, logits, norm_w):
    s = proj.shape[0]
    rows = CHUNKS_PER_STEP * CHUNK
    nc = s // rows

    def body(p_ref, lg_ref, nw_ref, y_ref, st_out_ref, st_scr):
        @pl.when(pl.program_id(0) == 0)
        def _():
            st_scr[...] = jnp.zeros_like(st_scr)

        sts = tuple(st_scr[h] for h in HEADS)
        y, sts_new = _hg_chunk(_seg(p_ref, 0), _seg(p_ref, 1), _seg(p_ref, 2), _seg(p_ref, 3),
                               lg_ref[0:1, :], lg_ref[1:2, :], nw_ref[...], sts)
        y_ref[...] = y.astype(y_ref.dtype)
        for h in HEADS:
            st_out_ref[h] = sts[h]
            st_scr[h] = sts_new[h]

    return pl.pallas_call(
        body, name="hgrn2_fwd", grid=(nc,),
        in_specs=[pl.BlockSpec((rows, 4 * D_GROUP), lambda c: (c, 0)),
                  pl.BlockSpec((2, D_GROUP), lambda c: (0, 0)),
                  pl.BlockSpec((1, D_GROUP), lambda c: (0, 0))],
        out_specs=[pl.BlockSpec((rows, D_GROUP), lambda c: (c, 0)),
                   pl.BlockSpec((None, N_HEADS, D_HEAD, D_HEAD), lambda c: (c, 0, 0, 0))],
        out_shape=[jax.ShapeDtypeStruct((s, 2 * D_GROUP), BF16),
                   jax.ShapeDtypeStruct((nc, N_HEADS, D_HEAD, D_HEAD), F32)],
        scratch_shapes=[pltpu.VMEM((N_HEADS, D_HEAD, D_HEAD), F32)],
        compiler_params=_params(("arbitrary",)),
    )(proj, logits, norm_w)


def _hgrn2_bwd(proj, logits, norm_w, states, d_y):
    s = proj.shape[0]
    rows = CHUNKS_PER_STEP * CHUNK
    nc = s // rows

    def body(p_ref, lg_ref, nw_ref, st_ref, dy_ref, dp_ref, dl_ref, dnw_ref, dsum_ref, dst_scr):
        @pl.when(pl.program_id(0) == 0)
        def _():
            dst_scr[...] = jnp.zeros_like(dst_scr)
            dl_ref[...] = jnp.zeros_like(dl_ref)
            dnw_ref[...] = jnp.zeros_like(dnw_ref)
            dsum_ref[...] = jnp.zeros_like(dsum_ref)

        _, vjp = jax.vjp(_hg_chunk, _seg(p_ref, 0), _seg(p_ref, 1), _seg(p_ref, 2), _seg(p_ref, 3),
                         lg_ref[0:1, :], lg_ref[1:2, :], nw_ref[...], tuple(st_ref[h] for h in HEADS))
        d_hq, d_hf, d_hi, d_hg, d_l0, d_l1, d_nw, d_sts = vjp((dy_ref[...], tuple(dst_scr[h] for h in HEADS)))
        for seg, val in enumerate((d_hq, d_hf, d_hi, d_hg)):
            dp_ref[:, seg * D_GROUP:(seg + 1) * D_GROUP] = val.astype(dp_ref.dtype)
            dsum_ref[:, seg * D_GROUP:(seg + 1) * D_GROUP] += jnp.sum(val, axis=0, keepdims=True)
        dl_ref[0:1, :] += d_l0
        dl_ref[1:2, :] += d_l1
        dnw_ref[...] += d_nw
        for h in HEADS:
            dst_scr[h] = d_sts[h]

    rev = lambda c: nc - 1 - c
    return pl.pallas_call(
        body, name="hgrn2_bwd", grid=(nc,),
        in_specs=[pl.BlockSpec((rows, 4 * D_GROUP), lambda c: (rev(c), 0)),
                  pl.BlockSpec((2, D_GROUP), lambda c: (0, 0)),
                  pl.BlockSpec((1, D_GROUP), lambda c: (0, 0)),
                  pl.BlockSpec((None, N_HEADS, D_HEAD, D_HEAD), lambda c: (rev(c), 0, 0, 0)),
                  pl.BlockSpec((rows, D_GROUP), lambda c: (rev(c), 0))],
        out_specs=[pl.BlockSpec((rows, 4 * D_GROUP), lambda c: (rev(c), 0)),
                   pl.BlockSpec((2, D_GROUP), lambda c: (0, 0)),
                   pl.BlockSpec((1, D_GROUP), lambda c: (0, 0)),
                   pl.BlockSpec((1, 4 * D_GROUP), lambda c: (0, 0))],
        out_shape=[jax.ShapeDtypeStruct((s, D_IN_MAIN), BF16), jax.ShapeDtypeStruct((2, D_GROUP), F32),
                   jax.ShapeDtypeStruct((1, D_GROUP), F32), jax.ShapeDtypeStruct((1, 4 * D_GROUP), F32)],
        scratch_shapes=[pltpu.VMEM((N_HEADS, D_HEAD, D_HEAD), F32)],
        compiler_params=_params(("arbitrary",)),
    )(proj, logits, norm_w, states, d_y)


def _gate_column(gates, lane, idx):
    return jnp.sum(jnp.where(lane == idx, gates, 0.0), axis=1, keepdims=True)


def _head_layer_norm(h):
    mu = jnp.mean(h, axis=-1, keepdims=True)
    var = jnp.mean(jnp.square(h - mu), axis=-1, keepdims=True)
    return (h - mu) * lax.rsqrt(var + LN_EPS)


def _ml_chunk(qc, kc, v, mo, gates, nw, cts, ns, ms):
    n = qc.shape[0] // CHUNK
    row, col = _chunk_masks(CHUNK)
    mask = col <= row
    eye = col == row
    to_row = lambda t: jnp.sum(jnp.where(eye, t, 0.0), axis=0, keepdims=True)
    q = _blocks(qc * (D_HEAD ** -0.5))
    k = _blocks(kc)
    vs = _blocks(v)
    gate_rows = _split_chunks(gates)
    lane = lax.broadcasted_iota(jnp.int32, gate_rows[0].shape, 1)
    each = [(i, h) for i in range(n) for h in HEADS]
    on_each = lambda f: {ih: f(*ih) for ih in each}
    ig = on_each(lambda i, h: _gate_column(gate_rows[i], lane, h))
    lf = on_each(lambda i, h: jax.nn.log_sigmoid(_gate_column(gate_rows[i], lane, N_HEADS + h)))
    lf_row = on_each(lambda i, h: to_row(lf[i, h]))
    ig_row = on_each(lambda i, h: to_row(ig[i, h]))
    b_col = on_each(lambda i, h: jnp.sum(jnp.where(mask, lf_row[i, h], 0.0), axis=1, keepdims=True))
    b_row = on_each(lambda i, h: jnp.sum(jnp.where(row <= col, lf[i, h], 0.0), axis=0, keepdims=True))
    g = on_each(lambda i, h: jnp.sum(lf[i, h], axis=0, keepdims=True))
    d = on_each(lambda i, h: jnp.where(mask, b_col[i, h] - b_row[i, h] + ig_row[i, h], -jnp.inf))
    a = on_each(lambda i, h: g[i, h] - b_col[i, h] + ig[i, h])
    m_at = {(0, h): ms[h] for h in HEADS}
    for i, h in each:
        m_at[i + 1, h] = lax.stop_gradient(jnp.maximum(g[i, h] + m_at[i, h], jnp.max(a[i, h], axis=0, keepdims=True)))
    inter = on_each(lambda i, h: b_col[i, h] + m_at[i, h])
    m_t = on_each(lambda i, h: lax.stop_gradient(jnp.maximum(inter[i, h], jnp.max(d[i, h], axis=1, keepdims=True))))
    qk = on_each(lambda i, h: _nt(q[i][h], k[i][h]))
    sc = on_each(lambda i, h: qk[i, h] * jnp.exp(d[i, h] - m_t[i, h]))
    w_inter = on_each(lambda i, h: jnp.exp(inter[i, h] - m_t[i, h]))
    sv = on_each(lambda i, h: _nn(sc[i, h], vs[i][h]))
    decay = on_each(lambda i, h: jnp.exp(g[i, h] + m_at[i, h] - m_at[i + 1, h]))
    wk = on_each(lambda i, h: k[i][h] * jnp.exp(a[i, h] - m_at[i + 1, h]))
    kv = on_each(lambda i, h: _tn(vs[i][h], wk[i, h]))
    normed = []
    for i in range(n):
        qc_state = [_nt(q[i][h], cts[h]) for h in HEADS]
        num = [sv[i, h] + w_inter[i, h] * qc_state[h] for h in HEADS]
        den = [jnp.sum(sc[i, h], axis=1, keepdims=True)
               + w_inter[i, h] * jnp.sum(q[i][h] * ns[h], axis=1, keepdims=True) for h in HEADS]
        hh = [num[h] / jnp.maximum(jnp.abs(den[h]), jnp.exp(-m_t[i, h])) for h in HEADS]
        cts = tuple(decay[i, h] * cts[h] + kv[i, h] for h in HEADS)
        ns = tuple(decay[i, h] * ns[h] + jnp.sum(wk[i, h], axis=0, keepdims=True) for h in HEADS)
        normed.append(_merge_heads(tuple(_head_layer_norm(hh[h]) for h in HEADS)))
    y = jax.nn.sigmoid(mo) * (_merge_chunks(tuple(normed)) * nw)
    return y, cts, ns, tuple(m_at[n, h] for h in HEADS)


def _mlstm_fwd(qk, proj, gates, norm_w, y):
    s = proj.shape[0]
    rows = ML_CHUNKS_PER_STEP * CHUNK
    nc = s // rows

    def body(qk_ref, vo_ref, g_ref, nw_ref, _, y_ref, ct_out, n_out, m_out, ct_scr, n_scr, m_scr):
        @pl.when(pl.program_id(0) == 0)
        def _():
            ct_scr[...] = jnp.zeros_like(ct_scr)
            n_scr[...] = jnp.zeros_like(n_scr)
            m_scr[...] = jnp.full(m_scr.shape, NEG_BIG, F32)

        cts = tuple(ct_scr[h] for h in HEADS)
        ns = tuple(n_scr[h] for h in HEADS)
        ms = tuple(m_scr[h] for h in HEADS)
        y, cts_new, ns_new, ms_new = _ml_chunk(_seg(qk_ref, 0), _seg(qk_ref, 1), _seg(vo_ref, 0), _seg(vo_ref, 1),
                                               g_ref[...], nw_ref[...], cts, ns, ms)
        y_ref[...] = y.astype(y_ref.dtype)
        for h in HEADS:
            ct_out[h], n_out[h], m_out[h] = cts[h], ns[h], ms[h]
            ct_scr[h], n_scr[h], m_scr[h] = cts_new[h], ns_new[h], ms_new[h]

    st = lambda r, w: pl.BlockSpec((None, N_HEADS, r, w), lambda c: (c, 0, 0, 0))
    return pl.pallas_call(
        body, name="mlstm_fwd", grid=(nc,),
        in_specs=[pl.BlockSpec((rows, 2 * D_GROUP), lambda c: (c, 0)),
                  pl.BlockSpec((rows, 2 * D_GROUP), lambda c: (c, VO_BLOCK)),
                  pl.BlockSpec((rows, LANES), lambda c: (c, 0)),
                  pl.BlockSpec((1, D_GROUP), lambda c: (0, 0)),
                  pl.BlockSpec(memory_space=pl.ANY)],
        out_specs=[pl.BlockSpec((rows, D_GROUP), lambda c: (c, 1)),
                   st(D_HEAD, D_HEAD), st(1, D_HEAD), st(1, 1)],
        out_shape=[jax.ShapeDtypeStruct(y.shape, y.dtype),
                   jax.ShapeDtypeStruct((nc, N_HEADS, D_HEAD, D_HEAD), F32),
                   jax.ShapeDtypeStruct((nc, N_HEADS, 1, D_HEAD), F32),
                   jax.ShapeDtypeStruct((nc, N_HEADS, 1, 1), F32)],
        input_output_aliases={4: 0},
        scratch_shapes=[pltpu.VMEM((N_HEADS, D_HEAD, D_HEAD), F32), pltpu.VMEM((N_HEADS, 1, D_HEAD), F32),
                        pltpu.VMEM((N_HEADS, 1, 1), F32)],
        compiler_params=_params(("arbitrary",)),
    )(qk, proj, gates, norm_w, y)


def _mlstm_bwd(qk, proj, gates, norm_w, ct_s, n_s, m_s, d_y, d_proj):
    s = proj.shape[0]
    rows = ML_CHUNKS_PER_STEP * CHUNK
    nc = s // rows

    def body(qk_ref, vo_ref, g_ref, nw_ref, ct_ref, n_ref, m_ref, dy_ref, _,
             dp_ref, dqk_ref, dg_ref, dnw_ref, dsum_ref, dct_scr, dn_scr):
        @pl.when(pl.program_id(0) == 0)
        def _():
            dct_scr[...] = jnp.zeros_like(dct_scr)
            dn_scr[...] = jnp.zeros_like(dn_scr)
            dnw_ref[...] = jnp.zeros_like(dnw_ref)
            dsum_ref[...] = jnp.zeros_like(dsum_ref)

        ms = tuple(m_ref[h] for h in HEADS)
        step = lambda *a: _ml_chunk(*a, ms)[:3]
        _, vjp = jax.vjp(step, _seg(qk_ref, 0), _seg(qk_ref, 1), _seg(vo_ref, 0), _seg(vo_ref, 1), g_ref[...],
                         nw_ref[...], tuple(ct_ref[h] for h in HEADS), tuple(n_ref[h] for h in HEADS))
        d_q, d_k, d_v, d_o, d_gates, d_nw, d_cts, d_ns = vjp(
            (dy_ref[...], tuple(dct_scr[h] for h in HEADS), tuple(dn_scr[h] for h in HEADS)))
        dqk_ref[:, 0:D_GROUP] = d_q
        dqk_ref[:, D_GROUP:2 * D_GROUP] = d_k
        for seg, val in enumerate((d_v, d_o)):
            dp_ref[:, seg * D_GROUP:(seg + 1) * D_GROUP] = val.astype(dp_ref.dtype)
            dsum_ref[:, seg * D_GROUP:(seg + 1) * D_GROUP] += jnp.sum(val, axis=0, keepdims=True)
        dg_ref[...] = d_gates
        dnw_ref[...] += d_nw
        for h in HEADS:
            dct_scr[h] = d_cts[h]
            dn_scr[h] = d_ns[h]

    rev = lambda c: nc - 1 - c
    st = lambda r, w: pl.BlockSpec((None, N_HEADS, r, w), lambda c: (rev(c), 0, 0, 0))
    return pl.pallas_call(
        body, name="mlstm_bwd", grid=(nc,),
        in_specs=[pl.BlockSpec((rows, 2 * D_GROUP), lambda c: (rev(c), 0)),
                  pl.BlockSpec((rows, 2 * D_GROUP), lambda c: (rev(c), VO_BLOCK)),
                  pl.BlockSpec((rows, LANES), lambda c: (rev(c), 0)),
                  pl.BlockSpec((1, D_GROUP), lambda c: (0, 0)),
                  st(D_HEAD, D_HEAD), st(1, D_HEAD), st(1, 1),
                  pl.BlockSpec((rows, D_GROUP), lambda c: (rev(c), 1)),
                  pl.BlockSpec(memory_space=pl.ANY)],
        out_specs=[pl.BlockSpec((rows, 2 * D_GROUP), lambda c: (rev(c), VO_BLOCK)),
                   pl.BlockSpec((rows, 2 * D_GROUP), lambda c: (rev(c), 0)),
                   pl.BlockSpec((rows, LANES), lambda c: (rev(c), 0)),
                   pl.BlockSpec((1, D_GROUP), lambda c: (0, 0)),
                   pl.BlockSpec((1, 2 * D_GROUP), lambda c: (0, 0))],
        out_shape=[jax.ShapeDtypeStruct(d_proj.shape, d_proj.dtype), jax.ShapeDtypeStruct((s, 2 * D_GROUP), F32),
                   jax.ShapeDtypeStruct((s, LANES), F32), jax.ShapeDtypeStruct((1, D_GROUP), F32),
                   jax.ShapeDtypeStruct((1, 2 * D_GROUP), F32)],
        input_output_aliases={8: 0},
        scratch_shapes=[pltpu.VMEM((N_HEADS, D_HEAD, D_HEAD), F32), pltpu.VMEM((N_HEADS, 1, D_HEAD), F32)],
        compiler_params=_params(("arbitrary",)),
    )(qk, proj, gates, norm_w, ct_s, n_s, m_s, d_y, d_proj)


LN_TOKENS = 512
ATT_TOKENS = 512


def _proj_res_ln(a, w, xres, g, b, name):
    s, dm = xres.shape
    k = a.shape[1]
    tb = min(LN_TOKENS, s)

    def body(a_ref, w_ref, x_ref, g_ref, b_ref, z_ref, o_ref):
        halves = [slice(0, tb // 2), slice(tb // 2, tb)]
        zs = [ALPHA * x_ref[rows, :] + _nn_raw(a_ref[rows, :], w_ref[...]) for rows in halves]
        for rows, z in zip(halves, zs):
            z_ref[rows, :] = z
            o_ref[rows, :] = _layer_norm(z, g_ref[...], b_ref[...])

    tok = pl.BlockSpec((tb, dm), lambda i: (i, 0))
    vec = pl.BlockSpec((1, dm), lambda i: (0, 0))
    act = jax.ShapeDtypeStruct((s, dm), F32)
    return pl.pallas_call(
        body, name=name, grid=(s // tb,),
        in_specs=[pl.BlockSpec((tb, k), lambda i: (i, 0)), pl.BlockSpec((k, dm), lambda i: (0, 0)), tok, vec, vec],
        out_specs=[tok, tok], out_shape=[act, act], compiler_params=_params(("parallel",)),
    )(a, w, xres, g, b)


def _ln_bwd_proj(d_out, z, g, b, w, name):
    s, dm = z.shape
    k = w.shape[0]
    tb = min(LN_TOKENS, s)

    def body(do_ref, z_ref, g_ref, b_ref, w_ref, dz_ref, da_ref, dg_ref, db_ref):
        @pl.when(pl.program_id(0) == 0)
        def _():
            dg_ref[...] = jnp.zeros_like(dg_ref)
            db_ref[...] = jnp.zeros_like(db_ref)

        halves = [slice(0, tb // 2), slice(tb // 2, tb)]
        d_zs = []
        for rows in halves:
            _, vjp = jax.vjp(_layer_norm, z_ref[rows, :], g_ref[...], b_ref[...])
            d_z, d_g, d_b = vjp(do_ref[rows, :])
            dz_ref[rows, :] = d_z
            dg_ref[...] += d_g
            db_ref[...] += d_b
            d_zs.append(d_z)
        for rows, d_z in zip(halves, d_zs):
            da_ref[rows, :] = _nt_raw(d_z, w_ref[...])

    tok = pl.BlockSpec((tb, dm), lambda i: (i, 0))
    vec = pl.BlockSpec((1, dm), lambda i: (0, 0))
    return pl.pallas_call(
        body, name=name, grid=(s // tb,),
        in_specs=[tok, tok, vec, vec, pl.BlockSpec((k, dm), lambda i: (0, 0))],
        out_specs=[tok, pl.BlockSpec((tb, k), lambda i: (i, 0)), vec, vec],
        out_shape=[jax.ShapeDtypeStruct((s, dm), F32), jax.ShapeDtypeStruct((s, k), F32),
                   jax.ShapeDtypeStruct((1, dm), F32), jax.ShapeDtypeStruct((1, dm), F32)],
        compiler_params=_params(("arbitrary",)),
    )(d_out, z, g, b, w)


def _proj_loss_tail(a, w, xres, g, b, target):
    s, dm = xres.shape
    k = a.shape[1]
    tb = min(ATT_TOKENS, s)

    def loss_fn(z, gg, bb, tgt):
        err = jnp.square(_layer_norm(z, gg, bb) - tgt)
        return 0.5 * jnp.sum(jnp.mean(err, axis=-1, keepdims=True), axis=0, keepdims=True)

    def body(a_ref, w_ref, x_ref, g_ref, b_ref, t_ref, loss_ref, dz_ref, dg_ref, db_ref):
        @pl.when(pl.program_id(0) == 0)
        def _():
            loss_ref[...] = jnp.zeros_like(loss_ref)
            dg_ref[...] = jnp.zeros_like(dg_ref)
            db_ref[...] = jnp.zeros_like(db_ref)

        halves = [slice(0, tb // 2), slice(tb // 2, tb)]
        zs = [ALPHA * x_ref[rows, :] + _nn_raw(a_ref[rows, :], w_ref[...]) for rows in halves]
        for rows, z in zip(halves, zs):
            tgt = t_ref[rows, :]
            loss, vjp = jax.vjp(lambda zz, gg, bb, tgt=tgt: loss_fn(zz, gg, bb, tgt), z, g_ref[...], b_ref[...])
            d_z, d_g, d_b = vjp(jnp.ones((1, 1), F32))
            loss_ref[...] += loss
            dz_ref[rows, :] = d_z
            dg_ref[...] += d_g
            db_ref[...] += d_b

    tok = pl.BlockSpec((tb, dm), lambda i: (i, 0))
    vec = pl.BlockSpec((1, dm), lambda i: (0, 0))
    one = pl.BlockSpec((1, 1), lambda i: (0, 0))
    return pl.pallas_call(
        body, name="ffn_down_loss_tail", grid=(s // tb,),
        in_specs=[pl.BlockSpec((tb, k), lambda i: (i, 0)), pl.BlockSpec((k, dm), lambda i: (0, 0)), tok, vec, vec, tok],
        out_specs=[one, tok, vec, vec],
        out_shape=[jax.ShapeDtypeStruct((1, 1), F32), jax.ShapeDtypeStruct((s, dm), F32),
                   jax.ShapeDtypeStruct((1, dm), F32), jax.ShapeDtypeStruct((1, dm), F32)],
        compiler_params=_params(("arbitrary",)),
    )(a, w, xres, g, b, target)


def _att_heads(qs, ks, vs):
    sc = [_nt(q, k) * (CA_DH ** -0.5) for q, k in zip(qs, ks)]
    p = [jax.nn.softmax(s, axis=-1) for s in sc]
    return tuple(_nn(pp, v) for pp, v in zip(p, vs))


def _head_slices(ref_or_value, offset):
    return tuple(ref_or_value[:, offset + h * CA_DH:offset + (h + 1) * CA_DH] for h in range(CA_HEADS))


def _cross_attention_fwd(x1, kv, wq, wo, g, b):
    s = x1.shape[0]
    tb = min(ATT_TOKENS, s)

    def body(x_ref, kv_ref, wq_ref, wo_ref, g_ref, b_ref, att_ref, z_ref, o_ref):
        x_blk = x_ref[...]
        q = _nn_raw(x_blk, wq_ref[...])
        att = jnp.concatenate(_att_heads(_head_slices(q, 0), _head_slices(kv_ref, 0), _head_slices(kv_ref, D_MODEL)),
                              axis=1)
        att_ref[...] = att.astype(att_ref.dtype)
        z = ALPHA * x_blk + _nn_raw(att, wo_ref[...])
        z_ref[...] = z
        o_ref[...] = _layer_norm(z, g_ref[...], b_ref[...])

    tok = pl.BlockSpec((tb, D_MODEL), lambda i: (i, 0))
    mat = pl.BlockSpec((D_MODEL, D_MODEL), lambda i: (0, 0))
    vec = pl.BlockSpec((1, D_MODEL), lambda i: (0, 0))
    act = jax.ShapeDtypeStruct((s, D_MODEL), F32)
    return pl.pallas_call(
        body, name="cross_attention_fwd", grid=(s // tb,),
        in_specs=[tok, pl.BlockSpec((N_MEM, 2 * D_MODEL), lambda i: (0, 0)), mat, mat, vec, vec],
        out_specs=[tok, tok, tok],
        out_shape=[jax.ShapeDtypeStruct((s, D_MODEL), BF16), act, act],
        compiler_params=_params(("parallel",)),
    )(x1, kv, wq, wo, g, b)


def _cross_attention_bwd(d_x2, x1, z2, kv, wq, wo, g, b):
    s = x1.shape[0]
    tb = min(ATT_TOKENS, s)

    def body(dx2_ref, x_ref, z_ref, kv_ref, wq_ref, wo_ref, g_ref, b_ref,
             dx1_ref, dq_ref, dz_ref, dkv_ref, dg_ref, db_ref):
        @pl.when(pl.program_id(0) == 0)
        def _():
            dkv_ref[...] = jnp.zeros_like(dkv_ref)
            dg_ref[...] = jnp.zeros_like(dg_ref)
            db_ref[...] = jnp.zeros_like(db_ref)

        q = _nn_raw(x_ref[...], wq_ref[...])
        _, ln_vjp = jax.vjp(_layer_norm, z_ref[...], g_ref[...], b_ref[...])
        d_z, d_g, d_b = ln_vjp(dx2_ref[...])
        dg_ref[...] += d_g
        db_ref[...] += d_b
        dz_ref[...] = d_z.astype(dz_ref.dtype)
        d_att = _nt_raw(d_z, wo_ref[...])
        _, vjp = jax.vjp(_att_heads, _head_slices(q, 0), _head_slices(kv_ref, 0), _head_slices(kv_ref, D_MODEL))
        d_qs, d_ks, d_vs = vjp(_head_slices(d_att, 0))
        for h in range(CA_HEADS):
            lo = h * CA_DH
            dkv_ref[:, lo:lo + CA_DH] += d_ks[h]
            dkv_ref[:, D_MODEL + lo:D_MODEL + lo + CA_DH] += d_vs[h]
        d_q = jnp.concatenate(d_qs, axis=1)
        dq_ref[...] = d_q.astype(dq_ref.dtype)
        dx1_ref[...] = ALPHA * d_z + _nt_raw(d_q, wq_ref[...])

    tok = pl.BlockSpec((tb, D_MODEL), lambda i: (i, 0))
    mem = pl.BlockSpec((N_MEM, 2 * D_MODEL), lambda i: (0, 0))
    mat = pl.BlockSpec((D_MODEL, D_MODEL), lambda i: (0, 0))
    vec = pl.BlockSpec((1, D_MODEL), lambda i: (0, 0))
    low = jax.ShapeDtypeStruct((s, D_MODEL), BF16)
    return pl.pallas_call(
        body, name="cross_attention_bwd", grid=(s // tb,),
        in_specs=[tok, tok, tok, mem, mat, mat, vec, vec], out_specs=[tok, tok, tok, mem, vec, vec],
        out_shape=[jax.ShapeDtypeStruct((s, D_MODEL), F32), low, low,
                   jax.ShapeDtypeStruct((N_MEM, 2 * D_MODEL), F32),
                   jax.ShapeDtypeStruct((1, D_MODEL), F32), jax.ShapeDtypeStruct((1, D_MODEL), F32)],
        compiler_params=_params(("arbitrary",)),
    )(d_x2, x1, z2, kv, wq, wo, g, b)


def _local_step(x, mem, target, w, mid_weights=None, ffn_weights=None, down_weights=None, on_ffn_grads=None,
                on_mid_grads=None,
                on_small_grads=None, on_last_grads=None):
    w = dict(w)
    s = x.shape[0]
    tm = min(512, s)
    tt_big = min(1024, s)
    proj, gates = _input_projection(x, w["w_in_t"], w["w_in_gate_t"], w["b_in_main"], w["b_in_gate"],
                                    min(2048, s), 512)
    qk = _ml_conv_fwd(proj, w["ml_conv_w"], w["ml_conv_b"])
    y, hg_states = _hgrn2_fwd(proj, w["hg_lb_logits"], w["hg_norm_w"])
    y, ct_s, n_s, m_s = _mlstm_fwd(qk, proj, gates, w["ml_norm_w"], y)
    if mid_weights is not None:
        w.update(mid_weights(y))
    z1, x1 = _proj_res_ln(y, w["w_out"], x, w["ln1_g"], w["ln1_b"], "out_proj_ln1")
    kv = _matmul_nn(mem, w["ca_wkv"], None, N_MEM, CA_DH, "kv")
    att, z2, x2 = _cross_attention_fwd(x1, kv, w["ca_wq"], w["ca_wo"], w["ln2_g"], w["ln2_b"])
    if ffn_weights is not None:
        w.update(ffn_weights(x2))
    u = _matmul_nt(x2, w["ffn_w_up_t"], min(1024, s), UP_TILE, "ffn_up", out_dtype=BF16)
    hid = _ffn_conv_fwd(u, w["ffn_conv_w"], w["ffn_conv_b"])
    if down_weights is not None:
        w.update(down_weights(hid))
    loss, d_z3, d_ln3_g, d_ln3_b = _proj_loss_tail(hid, w["ffn_w_down"], x2, w["ln3_g"], w["ln3_b"], target)
    grads = {"ln3_g": d_ln3_g, "ln3_b": d_ln3_b}
    grads["ffn_w_down"] = _matmul_tn(hid, d_z3, UP_TILE, D_MODEL, tt_big, "d_w_down")
    d_hid = _matmul_nt(d_z3, w["ffn_w_down"], tm, D_FF, "d_hid", out_dtype=BF16)
    d_ug, d_uv, d_cwg, d_cwv, d_cbg, d_cbv = _ffn_conv_bwd(u, w["ffn_conv_w"], w["ffn_conv_b"], d_hid)
    grads["ffn_conv_w"] = jnp.concatenate([d_cwg, d_cwv], axis=-1)
    grads["ffn_conv_b"] = jnp.concatenate([d_cbg, d_cbv], axis=-1)
    d_w_up = _matmul_tn(d_ug, x2, UP_TILE, D_MODEL, tt_big, "d_w_up_gate", rows=D_UP)
    grads["ffn_w_up"] = _matmul_tn(d_uv, x2, UP_TILE, D_MODEL, tt_big, "d_w_up_val", rows=D_UP, row0=D_FF,
                                   into=d_w_up)
    d_x2 = _matmul_nn_sum([(d_ug, w["ffn_w_up_t"], 0), (d_uv, w["ffn_w_up_t"], D_FF)], d_z3, ALPHA,
                          min(256, s), "d_x2")
    if on_ffn_grads is not None:
        d_x2 = on_ffn_grads(grads, d_x2)
    d_x1, d_q, d_z2, d_kv, grads["ln2_g"], grads["ln2_b"] = _cross_attention_bwd(
        d_x2, x1, z2, kv, w["ca_wq"], w["ca_wo"], w["ln2_g"], w["ln2_b"])
    grads["ca_wo"] = _matmul_tn(att, d_z2, D_MODEL, D_MODEL, tt_big, "d_ca_wo")
    grads["ca_wq"] = _matmul_tn(x1, d_q, D_MODEL, D_MODEL, tt_big, "d_ca_wq")
    grads["ca_wkv"] = _matmul_tn(mem, d_kv, D_MODEL, CA_DH, N_MEM, "d_ca_wkv", shards=N_DEV, group=N_DEV)
    d_z1, d_y, grads["ln1_g"], grads["ln1_b"] = _ln_bwd_proj(d_x1, z1, w["ln1_g"], w["ln1_b"], w["w_out"],
                                                             "ln1_bwd_out_proj")
    grads["w_out"] = _matmul_tn(y, d_z1, D_MODEL, D_MODEL, tt_big, "d_w_out")
    if on_mid_grads is not None:
        d_y = on_mid_grads(grads, d_y)
    d_proj, grads["hg_lb_logits"], grads["hg_norm_w"], db_hg = _hgrn2_bwd(
        proj, w["hg_lb_logits"], w["hg_norm_w"], hg_states, d_y)
    d_proj, d_qk, d_gates, grads["ml_norm_w"], db_vo = _mlstm_bwd(
        qk, proj, gates, w["ml_norm_w"], ct_s, n_s, m_s, d_y, d_proj)
    d_proj, grads["ml_conv_w"], grads["ml_conv_b"], db_qk = _ml_conv_bwd(
        proj, w["ml_conv_w"], w["ml_conv_b"], d_qk, d_proj)
    grads["b_in_main"] = jnp.concatenate([db_hg, db_qk, db_vo], axis=-1)
    grads["b_in_gate"] = jnp.sum(d_gates, axis=0, keepdims=True)
    if on_small_grads is not None:
        d_proj = on_small_grads(grads, loss, d_proj)
    grads["w_in"] = _input_projection_grads(d_proj, d_gates, x, min(1024, D_IN_MAIN), tt_big)
    if on_last_grads is not None:
        d_z1 = on_last_grads(grads, d_z1)
    grad_x = _matmul_nn_sum([(d_proj, w["w_in_t"], 0), (d_gates, w["w_in_gate_t"], 0)], d_z1, ALPHA, tm, "d_x")
    return loss, grad_x, grads


HBM_SPEC = pl.BlockSpec(memory_space=pltpu.HBM)


def _coords():
    return lax.axis_index("x"), lax.axis_index("y"), lax.axis_index("c")


def _other_chips(x, y):
    return [(1 - x, y), (x, 1 - y), (1 - x, 1 - y)]


def _my_slot():
    x, y, c = _coords()
    return 4 * x + 2 * y + c


SEM_SPEC = pl.BlockSpec(memory_space=pltpu.SEMAPHORE)
ANY_SPEC = pl.BlockSpec(memory_space=pl.ANY)
SIDE_EFFECT = pltpu.SideEffectType.DATAFLOW_SIDE_EFFECTING


def _peer(x, y, c, d):
    flip = lambda v, bit: 1 - v if bit else v
    p = (flip(x, d & 4), flip(y, d & 2), flip(c, d & 1))
    return p, 4 * p[0] + 2 * p[1] + p[2]


def _direct_copies(gather, src_refs, land_refs, send_sems, recv_sems):
    x, y, c = _coords()
    me = 4 * x + 2 * y + c
    copies = []
    for a in range(len(src_refs)):
        for d in range(1, N_DEV):
            peer, peer_slot = _peer(x, y, c, d)
            copies.append(pltpu.make_async_remote_copy(
                src_ref=src_refs[a] if gather else src_refs[a].at[peer_slot],
                dst_ref=land_refs[a].at[me] if gather else land_refs[a].at[d - 1],
                send_sem=send_sems.at[7 * a + d - 1], recv_sem=recv_sems.at[7 * a + d - 1],
                device_id=peer, device_id_type=MESH))
    return copies


def _hbm(t):
    return pltpu.HBM(t.shape, t.dtype)


def _chip_copies(src_refs, land_refs, send_sems, recv_sems):
    x, y, c = _coords()
    me = 4 * x + 2 * y + c
    targets = [(x, y, 1 - c)] + [(cx, cy, c) for cx, cy in _other_chips(x, y)]
    return [pltpu.make_async_remote_copy(
        src_ref=src_refs[a], dst_ref=land_refs[a].at[me], send_sem=send_sems.at[4 * a + k],
        recv_sem=recv_sems.at[4 * a + k], device_id=target, device_id_type=MESH)
        for a in range(len(src_refs)) for k, target in enumerate(targets)]


def _forward_copies(land_refs, send_sems, recv_sems):
    x, y, c = _coords()
    return [pltpu.make_async_remote_copy(
        src_ref=land_refs[a].at[4 * cx + 2 * cy + c], dst_ref=land_refs[a].at[4 * cx + 2 * cy + c],
        send_sem=send_sems.at[3 * a + j], recv_sem=recv_sems.at[3 * a + j],
        device_id=(x, y, 1 - c), device_id_type=MESH)
        for a in range(len(land_refs)) for j, (cx, cy) in enumerate(_other_chips(x, y))]


def _split_copy_start(make_copies, n_sems, operands, through, name):
    n_ops = len(operands)

    def body(*refs):
        for cp in make_copies(refs[:n_ops], refs[n_ops + 1], refs[n_ops + 2]):
            cp.start()

    ins = [pltpu.with_memory_space_constraint(t, pltpu.HBM) for t in (*operands, through)]
    sems = pltpu.SemaphoreType.DMA((n_sems,))
    res = pl.pallas_call(
        body, name=name, out_shape=(sems, sems, *[_hbm(t) for t in ins]),
        in_specs=[HBM_SPEC] * (n_ops + 1), out_specs=(SEM_SPEC, SEM_SPEC, *[HBM_SPEC] * (n_ops + 1)),
        input_output_aliases={i: 2 + i for i in range(n_ops + 1)},
        compiler_params=pltpu.CompilerParams(has_side_effects=SIDE_EFFECT),
    )(*ins)
    return (res[0], res[1], list(res[2:2 + n_ops])), res[2 + n_ops]


def _split_copy_wait(make_copies, started, after, name):
    send_sems, recv_sems, operands = started
    n_ops = len(operands)
    after = list(after) if isinstance(after, (list, tuple)) else [after]

    def body(*refs):
        for cp in make_copies(refs[:n_ops], refs[n_ops], refs[n_ops + 1]):
            cp.wait_send()
            cp.wait_recv()

    res = pl.pallas_call(
        body, name=name, out_shape=tuple(_hbm(t) for t in operands),
        in_specs=[HBM_SPEC] * n_ops + [SEM_SPEC, SEM_SPEC] + [ANY_SPEC] * len(after),
        out_specs=tuple([HBM_SPEC] * n_ops), input_output_aliases={i: i for i in range(n_ops)},
        compiler_params=pltpu.CompilerParams(has_side_effects=SIDE_EFFECT),
    )(*operands, send_sems, recv_sems, *after)
    return list(res)


def _halves(make_copies, na):
    return lambda refs, send_sems, recv_sems: make_copies(refs[:na], refs[na:], send_sems, recv_sems)


def _direct_start(gather, arrays, through, name):
    na = len(arrays)
    lands = [lax.empty((N_DEV,) + t.shape if gather else (N_DEV - 1,) + t.shape[1:], t.dtype) for t in arrays]
    return _split_copy_start(_halves(functools.partial(_direct_copies, gather), na), 7 * na, [*arrays, *lands],
                             through, name)


def _direct_wait(gather, started, after, name):
    na = len(started[2]) // 2
    operands = _split_copy_wait(_halves(functools.partial(_direct_copies, gather), na), started, after, name)
    return operands[:na], operands[na:]


def _slot_copies(land_refs, send_sems, recv_sems, first=0):
    x, y, c = _coords()
    me = 4 * x + 2 * y + c
    return [pltpu.make_async_remote_copy(
        src_ref=land.at[me], dst_ref=land.at[me], send_sem=send_sems.at[7 * a + d - 1],
        recv_sem=recv_sems.at[7 * a + d - 1], device_id=_peer(x, y, c, d)[0], device_id_type=MESH)
        for a, land in enumerate(land_refs, first) for d in range(1, N_DEV)]


def _own_slot_filled(block, slot):
    return lax.dynamic_update_index_in_dim(lax.empty((N_DEV,) + block.shape, block.dtype), block, slot, 0)


def _two_level_gather(shards, glue, name):
    na = len(shards)
    lands = [lax.empty((N_DEV,) + t.shape, t.dtype) for t in shards]
    slot = jnp.full((SUBLANES, LANES), _my_slot(), jnp.int32)
    started, slot = _split_copy_start(_halves(_chip_copies, na), 4 * na, [*shards, *lands], slot, name + "_start")
    operands = _split_copy_wait(_halves(_chip_copies, na), started, glue(slot[0, 0]), name + "_wait")
    started, mine = _split_copy_start(_forward_copies, 3 * na, operands[na:], operands[0], name + "_forward_start")
    lands = _split_copy_wait(_forward_copies, started, mine, name + "_forward_wait")
    return [lax.dynamic_update_index_in_dim(land, own, _my_slot(), 0)
            for own, land in zip([mine, *operands[1:na]], lands)]


def _row_tile(rows):
    for t in (256, 176, 128):
        if rows % t == 0 and rows > t:
            return t
    return rows


def _adamw_math(g, w, m, v):
    m_new = ADAM_B1 * m + (1.0 - ADAM_B1) * g
    v_new = ADAM_B2 * v + (1.0 - ADAM_B2) * jnp.square(g)
    m_hat = m_new / (1.0 - ADAM_B1 ** ADAM_STEP)
    v_hat = v_new / (1.0 - ADAM_B2 ** ADAM_STEP)
    delta = -ADAM_LR * (m_hat / (jnp.sqrt(v_hat) + ADAM_EPS) + ADAM_WD * w)
    return delta, m_new, v_new


def _adamw_sharded(chip, sums, got, w, m, v, name):
    r, c = w.shape
    tr = _row_tile(r)
    n_got = got.shape[0]

    def body(chip_ref, s_ref, g_ref, w_ref, m_ref, v_ref, go_ref, d_ref, nm_ref, nv_ref):
        g = s_ref[...].astype(F32)
        for i in range(n_got):
            g = g + g_ref[i].astype(F32)
        delta, m_new, v_new = _adamw_math(g, w_ref[...], m_ref[...], v_ref[...])
        go_ref[...] = g
        d_ref[...] = delta
        nm_ref[...] = m_new
        nv_ref[...] = v_new

    blk = pl.BlockSpec((tr, c), lambda i, chip_ref: (i, 0))
    out = jax.ShapeDtypeStruct((r, c), F32)
    return pl.pallas_call(
        body, name=name,
        grid_spec=pltpu.PrefetchScalarGridSpec(
            num_scalar_prefetch=1, grid=(r // tr,),
            in_specs=[pl.BlockSpec((None, tr, c), lambda i, chip_ref: (chip_ref[0], i, 0)),
                      pl.BlockSpec((n_got, tr, c), lambda i, chip_ref: (0, i, 0)), blk, blk, blk],
            out_specs=[blk, blk, blk, blk]),
        out_shape=[out, out, out, out],
        compiler_params=_params(("parallel",)),
    )(chip, sums, got, w, m, v)


def _adamw_replicated(parts, w, m, v):
    p, r, c = parts.shape
    names = SMALL_NAMES
    shapes = [w[n].shape for n in names]

    def body(*refs):
        p_ref = refs[0]
        ins = refs[1:1 + 3 * len(names)]
        outs = refs[1 + 3 * len(names):-2]
        loss_ref, sum_scr = refs[-2], refs[-1]
        total = p_ref[0]
        for i in range(1, p):
            total = total + p_ref[i]
        sum_scr[...] = total
        for k, n in enumerate(names):
            w_ref, m_ref, v_ref = ins[3 * k:3 * k + 3]
            g_ref, d_ref, nm_ref, nv_ref = outs[4 * k:4 * k + 4]
            for row, lane0, width, src_row in _small_pieces(n, shapes[k]):
                here = (slice(row, row + 1), slice(lane0, lane0 + width))
                g = sum_scr[src_row:src_row + 1, 0:width]
                delta, m_new, v_new = _adamw_math(g, w_ref[here], m_ref[here], v_ref[here])
                g_ref[here] = g
                d_ref[here] = delta
                nm_ref[here] = m_new
                nv_ref[here] = v_new
        loss_ref[...] = sum_scr[SMALL_LOSS_ROW:SMALL_LOSS_ROW + 1, 0:1]

    whole = lambda shape: pl.BlockSpec(shape, lambda i: (0,) * len(shape))
    args = [parts] + [t[n] for n in names for t in (w, m, v)]
    out_shape = [jax.ShapeDtypeStruct(s, F32) for s in shapes for _ in range(4)] + [jax.ShapeDtypeStruct((1, 1), F32)]
    res = pl.pallas_call(
        body, name="adamw_replicated", grid=(1,),
        in_specs=[whole(t.shape) for t in args], out_specs=[whole(s.shape) for s in out_shape],
        out_shape=out_shape, scratch_shapes=[pltpu.VMEM((r, c), F32)],
        compiler_params=_params(("arbitrary",)),
    )(*args)
    results = [{n: res[4 * k + j] for k, n in enumerate(names)} for j in range(4)]
    return results, res[-1]


SHARDED_NAMES = ("w_in", "ml_conv_w", "w_out", "ca_wq", "ca_wkv", "ca_wo", "ffn_w_up", "ffn_conv_w", "ffn_w_down")
SMALL_NAMES = ("b_in", "hg_lb_logits", "hg_norm_w", "ml_conv_b", "ml_norm_w", "ln1_g", "ln1_b",
               "ln2_g", "ln2_b", "ffn_conv_b", "ln3_g", "ln3_b")
WEIGHT_NAMES = ("w_in", "b_in", "hg_lb_logits", "hg_norm_w", "ml_conv_w", "ml_conv_b", "ml_norm_w", "w_out",
                "ln1_g", "ln1_b", "ca_wq", "ca_wkv", "ca_wo", "ln2_g", "ln2_b", "ffn_w_up", "ffn_conv_w",
                "ffn_conv_b", "ffn_w_down", "ln3_g", "ln3_b")
PAD_TO = {"ffn_conv_w": UP_SHARD_P}
SMALL_ROWS = 24
SMALL_W = D_MODEL
SMALL_SHAPES = {"b_in": (1, D_IN), "hg_lb_logits": (2, D_GROUP), "hg_norm_w": (1, D_GROUP),
                "ml_conv_b": (1, 2 * D_GROUP), "ml_norm_w": (1, D_GROUP), "ln1_g": (1, D_MODEL), "ln1_b": (1, D_MODEL),
                "ln2_g": (1, D_MODEL), "ln2_b": (1, D_MODEL), "ffn_conv_b": (1, D_UP), "ln3_g": (1, D_MODEL),
                "ln3_b": (1, D_MODEL)}


def _shard_2d(name, block):
    t = block[0]
    if name in PAD_TO:
        t = jnp.pad(t, ((0, 0), (0, PAD_TO[name] - t.shape[1])))
    return t


TRANSPOSED = ("w_in", "ffn_w_up")


def _update_shard(name, block):
    if name not in TRANSPOSED:
        return _shard_2d(name, block)
    return jnp.transpose(block, (0, 2, 1))[0]


def _shard_like(name, t, like):
    if name in TRANSPOSED:
        out = jnp.transpose(t[None], (0, 2, 1))
        return with_layout_constraint(out, Layout(major_to_minor=(0, 2, 1))) if name == "ffn_w_up" else out
    return t[:, :like.shape[2]][None]


def _pad_cols(t, width):
    return jnp.pad(t, ((0, 0), (0, width - t.shape[1])))


FIRST_NAMES = ("w_in", "ml_conv_w")
FFN_NAMES = ("ffn_w_up", "ffn_w_down", "ffn_conv_w")
MID_NAMES = ("ca_wo", "ca_wq", "ca_wkv", "w_out")


def _first_weights(g, small):
    w = dict(small)
    w["w_in_t"] = g["w_in"].reshape(D_IN, D_MODEL)
    w["w_in_gate_t"] = jnp.pad(w["w_in_t"][D_IN_MAIN:], ((0, LANES - (D_IN - D_IN_MAIN)), (0, 0)))
    w["b_in_main"] = small["b_in"][:, :D_IN_MAIN]
    w["b_in_gate"] = _pad_cols(small["b_in"][:, D_IN_MAIN:], LANES)
    w["ml_conv_w"] = jnp.transpose(g["ml_conv_w"], (1, 0, 2)).reshape(ML_CONV, 2 * D_GROUP)
    return w


def _mid_weights(g):
    w = {n: g[n].reshape(D_MODEL, D_MODEL) for n in ("w_out", "ca_wq", "ca_wo")}
    w["ca_wkv"] = g["ca_wkv"]
    return w


FFN_UP_NAMES = ("ffn_w_up", "ffn_conv_w")
FFN_DOWN_NAMES = ("ffn_w_down",)


def _ffn_up_weights(g, small):
    w = {"ffn_w_up_t": g["ffn_w_up"].reshape(D_UP, D_MODEL)}
    w["ffn_conv_w"] = jnp.transpose(g["ffn_conv_w"][:, :, :UP_SHARD], (1, 0, 2)).reshape(FFN_CONV, D_UP)
    w["ffn_conv_b"] = small["ffn_conv_b"].reshape(1, D_UP)
    return w


def _ffn_down_weights(g):
    return {"ffn_w_down": g["ffn_w_down"].reshape(D_FF, D_MODEL)}


def _owner_stack(n, grads):
    if n == "w_in":
        return grads[n].reshape(N_DEV, W_IN_SHARD, D_MODEL)
    if n == "ffn_w_up":
        return grads[n].reshape(N_DEV, UP_SHARD, D_MODEL)
    if n in ("w_out", "ca_wq", "ca_wo"):
        return grads[n].reshape(N_DEV, D_MODEL // N_DEV, D_MODEL)
    if n == "ffn_w_down":
        return grads[n].reshape(N_DEV, D_FF // N_DEV, D_MODEL)
    if n == "ml_conv_w":
        return jnp.transpose(grads[n].reshape(ML_CONV, N_DEV, LANES), (1, 0, 2))
    if n == "ffn_conv_w":
        shards = jnp.transpose(grads[n].reshape(FFN_CONV, N_DEV, UP_SHARD), (1, 0, 2))
        return jnp.pad(shards, ((0, 0), (0, 0), (0, UP_SHARD_P - UP_SHARD)))
    return grads[n]


def _small_grads(grads):
    out = {n: grads[n] for n in SMALL_NAMES if n in grads}
    out["b_in"] = jnp.concatenate([grads["b_in_main"], grads["b_in_gate"][:, :D_IN - D_IN_MAIN]], axis=1)
    return out


def _small_rows(shape):
    return shape[0] if shape[1] <= SMALL_W else -(-shape[1] // SMALL_W)


SMALL_BASE = {n: sum(_small_rows(SMALL_SHAPES[k]) for k in SMALL_NAMES[:i]) for i, n in enumerate(SMALL_NAMES)}
SMALL_LOSS_ROW = sum(_small_rows(SMALL_SHAPES[n]) for n in SMALL_NAMES)
assert SMALL_LOSS_ROW < SMALL_ROWS


def _small_pieces(name, shape):
    base = SMALL_BASE[name]
    if shape[1] <= SMALL_W:
        return [(i, 0, shape[1], base + i) for i in range(shape[0])]
    return [(0, k * SMALL_W, min(SMALL_W, shape[1] - k * SMALL_W), base + k) for k in range(_small_rows(shape))]


def _pack_small(p, loss):
    rows = []
    for n in SMALL_NAMES:
        t = p[n]
        nrows = _small_rows(t.shape)
        if t.shape[1] <= SMALL_W:
            rows.append(_pad_cols(t, SMALL_W))
        else:
            rows.append(_pad_cols(t, nrows * SMALL_W).reshape(nrows, SMALL_W))
    rows.append(_pad_cols(loss, SMALL_W))
    slab = jnp.concatenate(rows, axis=0)
    return jnp.pad(slab, ((0, SMALL_ROWS - slab.shape[0]), (0, 0)))


def kernel(x, mem, w_in, b_in, hg_lb_logits, hg_norm_w, ml_conv_w, ml_conv_b, ml_norm_w, w_out, ln1_g, ln1_b, ca_wq, ca_wkv, ca_wo, ln2_g, ln2_b, ffn_w_up, ffn_conv_w, ffn_conv_b, ffn_w_down, ln3_g, ln3_b, loss_target, m_w_in, m_b_in, m_hg_lb_logits, m_hg_norm_w, m_ml_conv_w, m_ml_conv_b, m_ml_norm_w, m_w_out, m_ln1_g, m_ln1_b, m_ca_wq, m_ca_wkv, m_ca_wo, m_ln2_g, m_ln2_b, m_ffn_w_up, m_ffn_conv_w, m_ffn_conv_b, m_ffn_w_down, m_ln3_g, m_ln3_b, v_w_in, v_b_in, v_hg_lb_logits, v_hg_norm_w, v_ml_conv_w, v_ml_conv_b, v_ml_norm_w, v_w_out, v_ln1_g, v_ln1_b, v_ca_wq, v_ca_wkv, v_ca_wo, v_ln2_g, v_ln2_b, v_ffn_w_up, v_ffn_conv_w, v_ffn_conv_b, v_ffn_w_down, v_ln3_g, v_ln3_b):
    params = dict(w_in=w_in, b_in=b_in, hg_lb_logits=hg_lb_logits, hg_norm_w=hg_norm_w, ml_conv_w=ml_conv_w,
                  ml_conv_b=ml_conv_b, ml_norm_w=ml_norm_w, w_out=w_out, ln1_g=ln1_g, ln1_b=ln1_b, ca_wq=ca_wq,
                  ca_wkv=ca_wkv, ca_wo=ca_wo, ln2_g=ln2_g, ln2_b=ln2_b, ffn_w_up=ffn_w_up, ffn_conv_w=ffn_conv_w,
                  ffn_conv_b=ffn_conv_b, ffn_w_down=ffn_w_down, ln3_g=ln3_g, ln3_b=ln3_b)
    mom1 = dict(w_in=m_w_in, b_in=m_b_in, hg_lb_logits=m_hg_lb_logits, hg_norm_w=m_hg_norm_w,
                ml_conv_w=m_ml_conv_w, ml_conv_b=m_ml_conv_b, ml_norm_w=m_ml_norm_w, w_out=m_w_out, ln1_g=m_ln1_g,
                ln1_b=m_ln1_b, ca_wq=m_ca_wq, ca_wkv=m_ca_wkv, ca_wo=m_ca_wo, ln2_g=m_ln2_g, ln2_b=m_ln2_b,
                ffn_w_up=m_ffn_w_up, ffn_conv_w=m_ffn_conv_w, ffn_conv_b=m_ffn_conv_b, ffn_w_down=m_ffn_w_down,
                ln3_g=m_ln3_g, ln3_b=m_ln3_b)
    mom2 = dict(w_in=v_w_in, b_in=v_b_in, hg_lb_logits=v_hg_lb_logits, hg_norm_w=v_hg_norm_w,
                ml_conv_w=v_ml_conv_w, ml_conv_b=v_ml_conv_b, ml_norm_w=v_ml_norm_w, w_out=v_w_out, ln1_g=v_ln1_g,
                ln1_b=v_ln1_b, ca_wq=v_ca_wq, ca_wkv=v_ca_wkv, ca_wo=v_ca_wo, ln2_g=v_ln2_g, ln2_b=v_ln2_b,
                ffn_w_up=v_ffn_w_up, ffn_conv_w=v_ffn_conv_w, ffn_conv_b=v_ffn_conv_b, ffn_w_down=v_ffn_w_down,
                ln3_g=v_ln3_g, ln3_b=v_ln3_b)

    x_idx, y_idx, c_idx = _coords()
    as_index = lambda v: jnp.reshape(v, (1,)).astype(jnp.int32)
    me = as_index(4 * x_idx + 2 * y_idx + c_idx)
    small_params = {n: params[n] for n in SMALL_NAMES}

    shards = {n: _update_shard(n, params[n]) for n in SHARDED_NAMES}
    m_shards = {n: _update_shard(n, mom1[n]) for n in SHARDED_NAMES}
    v_shards = {n: _update_shard(n, mom2[n]) for n in SHARDED_NAMES}
    outgoing = {n: shards[n] if "conv" in n else shards[n].astype(BF16) for n in SHARDED_NAMES}
    to_send = lambda names: [outgoing[n] for n in names]
    late = {}

    def glue(slot):
        late.update({n: _own_slot_filled(outgoing[n], slot) for n in SHARDED_NAMES if n not in FIRST_NAMES})
        return [*m_shards.values(), *v_shards.values(), *shards.values(), *late.values()]

    first = dict(zip(FIRST_NAMES, _two_level_gather(to_send(FIRST_NAMES), glue, "weights_gather_first")))

    late_names = MID_NAMES + FFN_UP_NAMES + FFN_DOWN_NAMES
    (late_send, late_recv, late_lands), first["w_in"] = _split_copy_start(
        _slot_copies, 7 * len(late_names), [late[n] for n in late_names], first["w_in"], "weights_gather_start_late")

    def late_group(names):
        a0 = late_names.index(names[0])
        return a0, (late_send, late_recv, late_lands[a0:a0 + len(names)])

    mid_started, ffn_started, down_started = (late_group(g) for g in (MID_NAMES, FFN_UP_NAMES, FFN_DOWN_NAMES))

    def gathered_weights(names, started, after, tag):
        a0, started = started
        return dict(zip(names, _split_copy_wait(functools.partial(_slot_copies, first=a0), started, after,
                                                "weights_gather_wait_" + tag)))

    started = {}

    def start_group(names, tag):
        def hook(grads, through):
            stacks = [_owner_stack(n, grads).astype(BF16) for n in names]
            started[tag], through = _direct_start(False, stacks, through, "grads_start_" + tag)
            return through
        return hook

    def start_small(grads, loss, through):
        started["small"], through = _direct_start(True, [_pack_small(_small_grads(grads), loss)], through,
                                                  "small_gather_start")
        return through

    loss, grad_x, grads = _local_step(
        x[0], mem[0], loss_target[0], _first_weights(first, small_params),
        lambda y: _mid_weights(gathered_weights(MID_NAMES, mid_started, y, "mid")),
        lambda x2: _ffn_up_weights(gathered_weights(FFN_UP_NAMES, ffn_started, x2, "ffn_up"), small_params),
        lambda hid: _ffn_down_weights(gathered_weights(FFN_DOWN_NAMES, down_started, hid, "ffn_down")),
        start_group(FFN_NAMES, "ffn"), start_group(MID_NAMES, "mid"), start_small, start_group(FIRST_NAMES, "last"))

    updated, sharded_out = {}, {}

    def update_group(names, tag, after):
        stacks, lands = _direct_wait(False, started[tag], after, "grads_wait_" + tag)
        for n, st, land in zip(names, stacks, lands):
            updated[n] = _adamw_sharded(me, st, land, shards[n], m_shards[n], v_shards[n], "adamw_" + n)
            sharded_out[n] = [_shard_like(n, t, params[n]) for t in updated[n]]

    update_group(FFN_NAMES, "ffn", grad_x)
    update_group(MID_NAMES, "mid", grad_x)
    own_small, small_lands = _direct_wait(True, started["small"], grad_x, "small_gather_wait")
    small_parts = lax.dynamic_update_index_in_dim(small_lands[0], own_small[0], me[0], 0)
    small_out, total_loss = _adamw_replicated(small_parts, small_params, {n: mom1[n] for n in SMALL_NAMES},
                                              {n: mom2[n] for n in SMALL_NAMES})
    done = [t for n in FFN_NAMES + MID_NAMES for t in updated[n]]
    done += [t for small in small_out for t in small.values()]
    update_group(FIRST_NAMES, "last", done)

    outs = []
    for k, small in enumerate(small_out):
        outs.extend(sharded_out[n][k] if n in sharded_out else small[n] for n in WEIGHT_NAMES)
    return (total_loss[0, 0], grad_x[None], *outs)
```

```python
import functools
import math

import jax
import jax.numpy as jnp
from jax import lax
from jax.experimental import pallas as pl
from jax.experimental.layout import Layout, with_layout_constraint
from jax.experimental.pallas import tpu as pltpu

F32 = jnp.float32
BF16 = jnp.bfloat16
HIGHEST = lax.Precision.HIGHEST
MESH = pl.DeviceIdType.MESH

N_DEV = 8
D_MODEL = 1024
N_MEM = 256
N_HEADS = 4
D_HEAD = 128
D_GROUP = N_HEADS * D_HEAD
CHUNK = 64
ML_CONV = 4
FFN_CONV = 3
D_FF = 2816
D_UP = 2 * D_FF
CA_HEADS = 4
CA_DH = D_MODEL // CA_HEADS
LANES = 128
SUBLANES = 8
D_IN = 8 * D_GROUP + 2 * N_HEADS
D_IN_MAIN = 8 * D_GROUP
W_IN_SHARD = D_IN // N_DEV
UP_SHARD = D_UP // N_DEV
UP_SHARD_P = 768
UP_TILE = D_UP // 4
ALPHA = 2.0 ** 0.25
LN_EPS = 1e-5
NEG_BIG = -1e30
ADAM_LR = 0.001
ADAM_B1 = 0.9
ADAM_B2 = 0.999
ADAM_EPS = 1e-08
ADAM_WD = 0.01
ADAM_STEP = 10
VMEM_LIMIT = 56 * 1024 * 1024

SEG_MQ = 4 * D_GROUP // LANES
VO_BLOCK = 3


def _params(sem):
    return pltpu.CompilerParams(dimension_semantics=sem, vmem_limit_bytes=VMEM_LIMIT)


def _dg(a, b, ca, cb, precision=None):
    return lax.dot_general(a, b, (((ca,), (cb,)), ((), ())), precision=precision,
                           preferred_element_type=F32)


def _nn_raw(a, b):
    return _dg(a.astype(BF16), b.astype(BF16), 1, 0)


def _nt_raw(a, b):
    return _dg(a.astype(BF16), b.astype(BF16), 1, 1)


def _tn_raw(a, b):
    return _dg(a.astype(BF16), b.astype(BF16), 0, 0)


@jax.custom_vjp
def _nn(a, b):
    return _nn_raw(a, b)


_nn.defvjp(lambda a, b: (_nn_raw(a, b), (a, b)),
           lambda res, g: (_nt_raw(g, res[1]), _tn_raw(res[0], g)))


@jax.custom_vjp
def _nt(a, b):
    return _nt_raw(a, b)


_nt.defvjp(lambda a, b: (_nt_raw(a, b), (a, b)),
           lambda res, g: (_nn_raw(g, res[1]), _tn_raw(g, res[0])))


@jax.custom_vjp
def _tn(a, b):
    return _tn_raw(a, b)


_tn.defvjp(lambda a, b: (_tn_raw(a, b), (a, b)),
           lambda res, g: (_nt_raw(res[1], g), _nn_raw(res[0], g)))


def _layer_norm(z, g, b):
    mu = jnp.mean(z, axis=-1, keepdims=True)
    var = jnp.mean(jnp.square(z - mu), axis=-1, keepdims=True)
    return (z - mu) * lax.rsqrt(var + LN_EPS) * g + b


def _matmul_nn(a, w, bias, tm, tn, name, out_dtype=F32):
    m, k = a.shape
    if w.ndim == 3:
        n = w.shape[0] * w.shape[2]
        assert tn == w.shape[2]
        w_spec = pl.BlockSpec((None, k, tn), lambda i, j: (j, 0, 0))
    else:
        n = w.shape[1]
        w_spec = pl.BlockSpec((k, tn), lambda i, j: (0, j))

    def body(*refs):
        a_ref, w_ref = refs[0], refs[1]
        o_ref = refs[-1]
        acc = _nn_raw(a_ref[...], w_ref[...])
        if bias is not None:
            acc = acc + refs[2][...]
        o_ref[...] = acc.astype(o_ref.dtype)

    in_specs = [pl.BlockSpec((tm, k), lambda i, j: (i, 0)), w_spec]
    args = [a, w]
    if bias is not None:
        in_specs.append(pl.BlockSpec((1, tn), lambda i, j: (0, j)))
        args.append(bias)
    return pl.pallas_call(
        body, name=name, grid=(m // tm, n // tn), in_specs=in_specs,
        out_specs=pl.BlockSpec((tm, tn), lambda i, j: (i, j)),
        out_shape=jax.ShapeDtypeStruct((m, n), out_dtype),
        compiler_params=_params(("parallel", "parallel")),
    )(*args)


def _matmul_nt(d, w, tm, tk, name, k_out=None, bias=None, out_dtype=F32):
    m, n = d.shape
    k = k_out or w.shape[0]

    def body(*refs):
        acc = _nt_raw(refs[0][...], refs[1][...])
        if bias is not None:
            acc = acc + refs[2][...]
        refs[-1][...] = acc.astype(refs[-1].dtype)

    in_specs = [pl.BlockSpec((tm, n), lambda i, j: (i, 0)), pl.BlockSpec((tk, n), lambda i, j: (j, 0))]
    args = [d, w]
    if bias is not None:
        in_specs.append(pl.BlockSpec((1, tk), lambda i, j: (0, j)))
        args.append(bias)
    return pl.pallas_call(
        body, name=name, grid=(m // tm, k // tk), in_specs=in_specs,
        out_specs=pl.BlockSpec((tm, tk), lambda i, j: (i, j)),
        out_shape=jax.ShapeDtypeStruct((m, k), out_dtype),
        compiler_params=_params(("parallel", "parallel")),
    )(*args)


def _input_projection(x, w_t, w_gate_t, b_main, b_gate, tm, tk):
    m, n = x.shape

    def body(x_ref, w_ref, wg_ref, b_ref, bg_ref, o_ref, g_ref):
        lhs = x_ref[...].astype(BF16)
        o_ref[...] = _nt_raw(lhs, w_ref[...]) + b_ref[...]

        @pl.when(pl.program_id(1) == 0)
        def _():
            g_ref[...] = _nt_raw(lhs, wg_ref[...]) + bg_ref[...]

    return pl.pallas_call(
        body, name="proj", grid=(m // tm, D_IN_MAIN // tk),
        in_specs=[pl.BlockSpec((tm, n), lambda i, j: (i, 0)), pl.BlockSpec((tk, n), lambda i, j: (j, 0)),
                  pl.BlockSpec((LANES, n), lambda i, j: (0, 0)), pl.BlockSpec((1, tk), lambda i, j: (0, j)),
                  pl.BlockSpec((1, LANES), lambda i, j: (0, 0))],
        out_specs=[pl.BlockSpec((tm, tk), lambda i, j: (i, j)), pl.BlockSpec((tm, LANES), lambda i, j: (i, 0))],
        out_shape=[jax.ShapeDtypeStruct((m, D_IN_MAIN), F32), jax.ShapeDtypeStruct((m, LANES), F32)],
        compiler_params=_params(("parallel", "arbitrary")),
    )(x, w_t, w_gate_t, b_main, b_gate)


def _input_projection_grads(d_proj, d_gates, x, tm, tt):
    t, m = d_proj.shape
    n = x.shape[1]
    last = t // tt - 1
    n_gate = D_IN - D_IN_MAIN

    def body(a_ref, g_ref, x_ref, o_ref, acc_ref, accg_ref):
        i, kk = pl.program_id(0), pl.program_id(1)

        @pl.when(kk == 0)
        def _():
            acc_ref[...] = jnp.zeros_like(acc_ref)

        @pl.when((kk == 0) & (i == 0))
        def _():
            accg_ref[...] = jnp.zeros_like(accg_ref)

        rhs = x_ref[...].astype(BF16)
        acc_ref[...] += _tn_raw(a_ref[...], rhs)

        @pl.when(i == 0)
        def _():
            accg_ref[...] += _tn_raw(g_ref[...], rhs)

        @pl.when(kk == last)
        def _():
            o_ref[pl.ds(pl.multiple_of(i * tm, tm), tm), :] = acc_ref[...].astype(o_ref.dtype)

        @pl.when((kk == last) & (i == 0))
        def _():
            o_ref[m:m + n_gate, :] = accg_ref[0:n_gate, :].astype(o_ref.dtype)

    return pl.pallas_call(
        body, name="d_w_in", grid=(m // tm, t // tt),
        in_specs=[pl.BlockSpec((tt, tm), lambda i, kk: (kk, i)), pl.BlockSpec((tt, LANES), lambda i, kk: (kk, 0)),
                  pl.BlockSpec((tt, n), lambda i, kk: (kk, 0))],
        out_specs=pl.BlockSpec((m + n_gate, n), lambda i, kk: (0, 0)),
        out_shape=jax.ShapeDtypeStruct((m + n_gate, n), BF16),
        scratch_shapes=[pltpu.VMEM((tm, n), F32), pltpu.VMEM((LANES, n), F32)],
        compiler_params=_params(("arbitrary", "arbitrary")),
    )(d_proj, d_gates, x)


def _matmul_nn_sum(pairs, add, scale, tm, name):
    m = pairs[0][0].shape[0]
    n = pairs[0][1].shape[1]
    in_specs, args = [], []
    for a, w, row0 in pairs:
        kk = a.shape[1]
        in_specs += [pl.BlockSpec((tm, kk), lambda i: (i, 0)),
                     pl.BlockSpec((kk, n), lambda i, blk=row0 // kk: (blk, 0))]
        args += [a, w]
    if add is not None:
        in_specs.append(pl.BlockSpec((tm, n), lambda i: (i, 0)))
        args.append(add)

    def body(*refs):
        acc = None
        for p in range(len(pairs)):
            term = _nn_raw(refs[2 * p][...], refs[2 * p + 1][...])
            acc = term if acc is None else acc + term
        if add is not None:
            acc = acc + scale * refs[2 * len(pairs)][...]
        refs[-1][...] = acc

    return pl.pallas_call(
        body, name=name, grid=(m // tm,), in_specs=in_specs,
        out_specs=pl.BlockSpec((tm, n), lambda i: (i, 0)),
        out_shape=jax.ShapeDtypeStruct((m, n), F32),
        compiler_params=_params(("parallel",)),
    )(*args)


def _matmul_tn(a, b, tm, tn, tt, name, shards=None, shard0=0, group=1, into=None, colsum=False, rows=None, row0=0):
    t, m = a.shape
    n = b.shape[1]
    assert not colsum or tm == m
    n_in = 2 + (into is not None)
    out_dtype = BF16
    per_step = 1 if shards is None else group
    width = per_step * tn

    def body(*refs):
        a_ref, b_ref = refs[0], refs[1]
        o_ref, acc_ref = refs[n_in], refs[-1]
        first = pl.program_id(2) == 0

        @pl.when(first)
        def _():
            acc_ref[...] = jnp.zeros_like(acc_ref)

        if shards is None:
            acc_ref[...] += _tn_raw(a_ref[...], b_ref[...])
        else:
            lhs = a_ref[...].astype(BF16)
            for g in range(per_step):
                acc_ref[g] += _tn_raw(lhs, b_ref[:, g * tn:(g + 1) * tn])

        @pl.when(pl.program_id(2) == t // tt - 1)
        def _():
            o_ref[...] = acc_ref[...].astype(o_ref.dtype)

        if colsum:
            s_ref = refs[n_in + 1]

            @pl.when(first)
            def _():
                s_ref[...] = jnp.zeros_like(s_ref)

            s_ref[...] += jnp.sum(b_ref[...], axis=0, keepdims=True)

    in_specs = [pl.BlockSpec((tt, tm), lambda i, j, kk: (kk, i)),
                pl.BlockSpec((tt, width), lambda i, j, kk: (kk, j))]
    args = [a, b]
    aliases = {}
    if into is not None:
        in_specs.append(pl.BlockSpec(memory_space=pl.ANY))
        args.append(into)
        aliases = {2: 0}
    if shards is None:
        out_specs = [pl.BlockSpec((tm, tn), lambda i, j, kk: (row0 // tm + i, j))]
        out_shape = [jax.ShapeDtypeStruct((rows or m, n), out_dtype)]
        acc = pltpu.VMEM((tm, tn), F32)
    else:
        out_specs = [pl.BlockSpec((per_step, tm, tn), lambda i, j, kk: (shard0 // per_step + j, i, 0))]
        out_shape = [jax.ShapeDtypeStruct((shards, m, tn), out_dtype)]
        acc = pltpu.VMEM((per_step, tm, tn), F32)
    if colsum:
        out_specs.append(pl.BlockSpec((1, tn), lambda i, j, kk: (0, j)))
        out_shape.append(jax.ShapeDtypeStruct((1, n), F32))
    res = pl.pallas_call(
        body, name=name, grid=(m // tm, n // width, t // tt), in_specs=in_specs, out_specs=out_specs,
        out_shape=out_shape, input_output_aliases=aliases, scratch_shapes=[acc],
        compiler_params=_params(("parallel", "parallel", "arbitrary")),
    )(*args)
    return res if colsum else res[0]


ROW_TILE = 64


def _stack(ref, start, rows):
    return ref[pl.ds(start, rows), :].astype(F32).reshape(rows // SUBLANES, SUBLANES, LANES)


def _vreg_rows(ref, n):
    return [jnp.broadcast_to(ref[j:j + 1, :], (SUBLANES, LANES))[None] for j in range(n)]


def _column_total(acc):
    return jnp.sum(acc, axis=0, keepdims=True)


def _conv_fwd_tile(pad_ref, taps_w, bias, r0, rows):
    taps = len(taps_w)
    acc = bias
    for j in range(taps):
        acc = acc + _stack(pad_ref, SUBLANES - (taps - 1 - j) + r0, rows) * taps_w[j]
    return acc


def _conv_grads_tile(pad_ref, dpad_ref, dx_ref, taps_w, dws, r0, rows):
    taps = len(taps_w)
    x_rows = _stack(pad_ref, SUBLANES + r0, rows)
    dx = None
    for j in range(taps):
        d_shifted = _stack(dpad_ref, r0 + (taps - 1 - j), rows)
        term = d_shifted * taps_w[j]
        dx = term if dx is None else dx + term
        dws[j] = dws[j] + jnp.sum(d_shifted * x_rows, axis=0)
    dx_ref[r0:r0 + rows, :] = dx.reshape(rows, LANES).astype(dx_ref.dtype)
    return jnp.sum(dx, axis=0)


def _ml_conv_fwd(proj, conv_w, conv_b):
    s = proj.shape[0]
    nblk = 2 * D_GROUP // LANES

    def body(x_ref, w_ref, b_ref, o_ref, pad_ref):
        pad_ref[0:SUBLANES, :] = jnp.zeros((SUBLANES, LANES), F32)
        pad_ref[SUBLANES:, :] = x_ref[...].astype(F32)
        taps_w, bias = _vreg_rows(w_ref, ML_CONV), _vreg_rows(b_ref, 1)[0]
        for r0 in range(0, s, ROW_TILE):
            rows = min(ROW_TILE, s - r0)
            o_ref[r0:r0 + rows, :] = jax.nn.silu(_conv_fwd_tile(pad_ref, taps_w, bias, r0, rows)).reshape(rows, LANES)

    return pl.pallas_call(
        body, name="ml_conv_fwd", grid=(nblk,),
        in_specs=[pl.BlockSpec((s, LANES), lambda j: (0, SEG_MQ + j)),
                  pl.BlockSpec((ML_CONV, LANES), lambda j: (0, j)),
                  pl.BlockSpec((1, LANES), lambda j: (0, j))],
        out_specs=pl.BlockSpec((s, LANES), lambda j: (0, j)),
        out_shape=jax.ShapeDtypeStruct((s, 2 * D_GROUP), F32),
        scratch_shapes=[pltpu.VMEM((s + SUBLANES, LANES), F32)],
        compiler_params=_params(("parallel",)),
    )(proj, conv_w, conv_b)


def _ml_conv_bwd(proj, conv_w, conv_b, d_qk, d_proj):
    s = proj.shape[0]
    nblk = 2 * D_GROUP // LANES

    def body(x_ref, w_ref, b_ref, dy_ref, _, dx_ref, dw_ref, db_ref, dxs_ref, pad_ref, dpad_ref):
        pad_ref[0:SUBLANES, :] = jnp.zeros((SUBLANES, LANES), F32)
        pad_ref[SUBLANES:, :] = x_ref[...].astype(F32)
        dpad_ref[s:, :] = jnp.zeros((SUBLANES, LANES), F32)
        taps_w, bias = _vreg_rows(w_ref, ML_CONV), _vreg_rows(b_ref, 1)[0]
        db = jnp.zeros((SUBLANES, LANES), F32)
        for r0 in range(0, s, ROW_TILE):
            rows = min(ROW_TILE, s - r0)
            pre = _conv_fwd_tile(pad_ref, taps_w, bias, r0, rows)
            _, vjp = jax.vjp(jax.nn.silu, pre)
            d_pre, = vjp(_stack(dy_ref, r0, rows))
            dpad_ref[r0:r0 + rows, :] = d_pre.reshape(rows, LANES)
            db = db + jnp.sum(d_pre, axis=0)
        db_ref[...] = _column_total(db)
        dws = [jnp.zeros((SUBLANES, LANES), F32) for _ in range(ML_CONV)]
        dx_sum = jnp.zeros((SUBLANES, LANES), F32)
        for r0 in range(0, s, ROW_TILE):
            dx_sum = dx_sum + _conv_grads_tile(pad_ref, dpad_ref, dx_ref, taps_w, dws, r0, min(ROW_TILE, s - r0))
        dxs_ref[...] = _column_total(dx_sum)
        for j in range(ML_CONV):
            dw_ref[j:j + 1, :] = _column_total(dws[j])

    return pl.pallas_call(
        body, name="ml_conv_bwd", grid=(nblk,),
        in_specs=[pl.BlockSpec((s, LANES), lambda j: (0, SEG_MQ + j)),
                  pl.BlockSpec((ML_CONV, LANES), lambda j: (0, j)),
                  pl.BlockSpec((1, LANES), lambda j: (0, j)),
                  pl.BlockSpec((s, LANES), lambda j: (0, j)),
                  pl.BlockSpec(memory_space=pl.ANY)],
        out_specs=[pl.BlockSpec((s, LANES), lambda j: (0, SEG_MQ + j)),
                   pl.BlockSpec((ML_CONV, LANES), lambda j: (0, j)),
                   pl.BlockSpec((1, LANES), lambda j: (0, j)),
                   pl.BlockSpec((1, LANES), lambda j: (0, j))],
        out_shape=[jax.ShapeDtypeStruct(d_proj.shape, d_proj.dtype),
                   jax.ShapeDtypeStruct((ML_CONV, 2 * D_GROUP), F32),
                   jax.ShapeDtypeStruct((1, 2 * D_GROUP), F32),
                   jax.ShapeDtypeStruct((1, 2 * D_GROUP), F32)],
        input_output_aliases={4: 0},
        scratch_shapes=[pltpu.VMEM((s + SUBLANES, LANES), F32), pltpu.VMEM((s + SUBLANES, LANES), F32)],
        compiler_params=_params(("parallel",)),
    )(proj, conv_w, conv_b, d_qk, d_proj)


def _gelu_mul(a, b):
    return jax.nn.gelu(a) * b


GELU_C = math.sqrt(2.0 / math.pi)
GELU_K = 0.044715


def _gelu_mul_grads(a, b, d):
    a2 = a * a
    t = jnp.tanh(GELU_C * (a + GELU_K * (a * a2)))
    cdf = 0.5 * (1.0 + t)
    slope = cdf + (0.5 * GELU_C) * a * (1.0 - t * t) * (1.0 + (3.0 * GELU_K) * a2)
    return d * b * slope, d * (a * cdf)


FFN_BLOCKS = D_FF // LANES


def _ffn_conv_fwd(u, conv_w, conv_b):
    s = u.shape[0]

    def body(g_ref, v_ref, wg_ref, wv_ref, bg_ref, bv_ref, o_ref, gpad_ref, vpad_ref):
        for pad_ref, x_ref in ((gpad_ref, g_ref), (vpad_ref, v_ref)):
            pad_ref[0:SUBLANES, :] = jnp.zeros((SUBLANES, LANES), F32)
            pad_ref[SUBLANES:, :] = x_ref[...].astype(F32)
        taps_g, bias_g = _vreg_rows(wg_ref, FFN_CONV), _vreg_rows(bg_ref, 1)[0]
        taps_v, bias_v = _vreg_rows(wv_ref, FFN_CONV), _vreg_rows(bv_ref, 1)[0]
        for r0 in range(0, s, ROW_TILE):
            rows = min(ROW_TILE, s - r0)
            ug = _conv_fwd_tile(gpad_ref, taps_g, bias_g, r0, rows)
            uv = _conv_fwd_tile(vpad_ref, taps_v, bias_v, r0, rows)
            o_ref[r0:r0 + rows, :] = _gelu_mul(ug, uv).reshape(rows, LANES).astype(o_ref.dtype)

    col = lambda off: (lambda j: (0, off + j))
    return pl.pallas_call(
        body, name="ffn_conv_fwd", grid=(FFN_BLOCKS,),
        in_specs=[pl.BlockSpec((s, LANES), col(0)), pl.BlockSpec((s, LANES), col(FFN_BLOCKS)),
                  pl.BlockSpec((FFN_CONV, LANES), col(0)), pl.BlockSpec((FFN_CONV, LANES), col(FFN_BLOCKS)),
                  pl.BlockSpec((1, LANES), col(0)), pl.BlockSpec((1, LANES), col(FFN_BLOCKS))],
        out_specs=pl.BlockSpec((s, LANES), col(0)),
        out_shape=jax.ShapeDtypeStruct((s, D_FF), BF16),
        scratch_shapes=[pltpu.VMEM((s + SUBLANES, LANES), F32), pltpu.VMEM((s + SUBLANES, LANES), F32)],
        compiler_params=_params(("parallel",)),
    )(u, u, conv_w, conv_w, conv_b, conv_b)


def _ffn_conv_bwd(u, conv_w, conv_b, d_h):
    s = u.shape[0]

    def body(g_ref, v_ref, wg_ref, wv_ref, bg_ref, bv_ref, dh_ref,
             dug_ref, duv_ref, dwg_ref, dwv_ref, dbg_ref, dbv_ref,
             gpad_ref, vpad_ref, dgpad_ref, dvpad_ref):
        for pad_ref, x_ref in ((gpad_ref, g_ref), (vpad_ref, v_ref)):
            pad_ref[0:SUBLANES, :] = jnp.zeros((SUBLANES, LANES), F32)
            pad_ref[SUBLANES:, :] = x_ref[...].astype(F32)
        dgpad_ref[s:, :] = jnp.zeros((SUBLANES, LANES), F32)
        dvpad_ref[s:, :] = jnp.zeros((SUBLANES, LANES), F32)
        taps_g, bias_g = _vreg_rows(wg_ref, FFN_CONV), _vreg_rows(bg_ref, 1)[0]
        taps_v, bias_v = _vreg_rows(wv_ref, FFN_CONV), _vreg_rows(bv_ref, 1)[0]
        dbg = jnp.zeros((SUBLANES, LANES), F32)
        dbv = jnp.zeros((SUBLANES, LANES), F32)
        for r0 in range(0, s, ROW_TILE):
            rows = min(ROW_TILE, s - r0)
            ug = _conv_fwd_tile(gpad_ref, taps_g, bias_g, r0, rows)
            uv = _conv_fwd_tile(vpad_ref, taps_v, bias_v, r0, rows)
            d_ug, d_uv = _gelu_mul_grads(ug, uv, _stack(dh_ref, r0, rows))
            dgpad_ref[r0:r0 + rows, :] = d_ug.reshape(rows, LANES)
            dvpad_ref[r0:r0 + rows, :] = d_uv.reshape(rows, LANES)
            dbg = dbg + jnp.sum(d_ug, axis=0)
            dbv = dbv + jnp.sum(d_uv, axis=0)
        dbg_ref[...] = _column_total(dbg)
        dbv_ref[...] = _column_total(dbv)
        for pad_ref, dpad_ref, taps_w, dx_ref, dw_ref in ((gpad_ref, dgpad_ref, taps_g, dug_ref, dwg_ref),
                                                          (vpad_ref, dvpad_ref, taps_v, duv_ref, dwv_ref)):
            dws = [jnp.zeros((SUBLANES, LANES), F32) for _ in range(FFN_CONV)]
            for r0 in range(0, s, ROW_TILE):
                _conv_grads_tile(pad_ref, dpad_ref, dx_ref, taps_w, dws, r0, min(ROW_TILE, s - r0))
            for j in range(FFN_CONV):
                dw_ref[j:j + 1, :] = _column_total(dws[j])

    col = lambda off: (lambda j: (0, off + j))
    seq = pl.BlockSpec((s, LANES), col(0))
    return pl.pallas_call(
        body, name="ffn_conv_bwd", grid=(FFN_BLOCKS,),
        in_specs=[pl.BlockSpec((s, LANES), col(0)), pl.BlockSpec((s, LANES), col(FFN_BLOCKS)),
                  pl.BlockSpec((FFN_CONV, LANES), col(0)), pl.BlockSpec((FFN_CONV, LANES), col(FFN_BLOCKS)),
                  pl.BlockSpec((1, LANES), col(0)), pl.BlockSpec((1, LANES), col(FFN_BLOCKS)), seq],
        out_specs=[seq, seq, pl.BlockSpec((FFN_CONV, LANES), col(0)), pl.BlockSpec((FFN_CONV, LANES), col(0)),
                   pl.BlockSpec((1, LANES), col(0)), pl.BlockSpec((1, LANES), col(0))],
        out_shape=[jax.ShapeDtypeStruct((s, D_FF), BF16), jax.ShapeDtypeStruct((s, D_FF), BF16),
                   jax.ShapeDtypeStruct((FFN_CONV, D_FF), F32), jax.ShapeDtypeStruct((FFN_CONV, D_FF), F32),
                   jax.ShapeDtypeStruct((1, D_FF), F32), jax.ShapeDtypeStruct((1, D_FF), F32)],
        scratch_shapes=[pltpu.VMEM((s + SUBLANES, LANES), F32) for _ in range(4)],
        compiler_params=_params(("parallel",)),
    )(u, u, conv_w, conv_w, conv_b, conv_b, d_h)


def _chunk_masks(c):
    row = lax.broadcasted_iota(jnp.int32, (c, c), 0)
    col = lax.broadcasted_iota(jnp.int32, (c, c), 1)
    return row, col


@jax.custom_vjp
def _split_heads(x):
    return tuple(x[:, h * D_HEAD:(h + 1) * D_HEAD] for h in range(N_HEADS))


_split_heads.defvjp(lambda x: (_split_heads(x), None), lambda _, gs: (jnp.concatenate(gs, axis=1),))


@jax.custom_vjp
def _merge_heads(xs):
    return jnp.concatenate(xs, axis=1)


_merge_heads.defvjp(lambda xs: (_merge_heads(xs), None), lambda _, g: (_split_heads(g),))


@jax.custom_vjp
def _split_chunks(x):
    return tuple(x[i * CHUNK:(i + 1) * CHUNK] for i in range(x.shape[0] // CHUNK))


_split_chunks.defvjp(lambda x: (_split_chunks(x), None), lambda _, gs: (jnp.concatenate(gs, axis=0),))


@jax.custom_vjp
def _merge_chunks(xs):
    return jnp.concatenate(xs, axis=0)


_merge_chunks.defvjp(lambda xs: (_merge_chunks(xs), None), lambda _, g: (_split_chunks(g),))


def _blocks(x):
    return [_split_heads(rows) for rows in _split_chunks(x)]


def _per_chunk_rows(per_chunk, rid):
    out = per_chunk[0]
    for i in range(1, len(per_chunk)):
        out = jnp.where(rid >= i * CHUNK, per_chunk[i], out)
    return out


HEADS = range(N_HEADS)
CHUNKS_PER_STEP = 8
ML_CHUNKS_PER_STEP = 1


def _hg_chunk(hq, hf, hi, hgate, l0, l1, nw, sts):
    n = hq.shape[0] // CHUNK
    causal = _chunk_masks(CHUNK)
    causal = causal[1] <= causal[0]
    mx = lax.stop_gradient(jnp.maximum(l0, l1))
    e0 = jnp.exp(l0 - mx)
    e1 = jnp.exp(l1 - mx)
    lb = e0 / (e0 + e1)
    sig = jax.nn.sigmoid(hf)
    lf = jnp.log(lb + (1.0 - lb) * sig)
    k = (1.0 - lb) * jax.nn.sigmoid(-hf)
    q = jax.nn.silu(hq)
    tri = causal.astype(F32)
    b = _merge_chunks(tuple(_dg(tri, rows, 1, 0, HIGHEST) for rows in _split_chunks(lf)))
    rid = lax.broadcasted_iota(jnp.int32, b.shape, 0)
    pick = lambda r: jnp.sum(jnp.where(rid == r, b, 0.0), axis=0, keepdims=True)
    b_last_c = [pick(i * CHUNK + CHUNK - 1) for i in range(n)]
    b_ref = _per_chunk_rows([pick(i * CHUNK + CHUNK // 2 - 1) for i in range(n)], rid)
    b_last = _per_chunk_rows(b_last_c, rid)
    qa = _blocks(q * jnp.exp(b - b_ref))
    ka = _blocks(k * jnp.exp(b_ref - b))
    qe = _blocks(q * jnp.exp(b))
    kd = _blocks(k * jnp.exp(b_last - b))
    decay = [_split_heads(jnp.exp(b_last_c[i])) for i in range(n)]
    v = _blocks(hi)
    chunks = range(n)
    attn = [[jnp.where(causal, _nt(qa[i][h], ka[i][h]), 0.0) for h in HEADS] for i in chunks]
    intra = [[_nn(attn[i][h], v[i][h]) for h in HEADS] for i in chunks]
    kv = [[_tn(v[i][h], kd[i][h]) for h in HEADS] for i in chunks]
    normed = []
    for i in chunks:
        inter = [_nt(qe[i][h], sts[h]) for h in HEADS]
        sts = tuple(decay[i][h] * sts[h] + kv[i][h] for h in HEADS)
        o = [intra[i][h] + inter[h] for h in HEADS]
        normed.append(_merge_heads(tuple(o[h] * lax.rsqrt(jnp.mean(o[h] * o[h], axis=-1, keepdims=True) + LN_EPS)
                                         for h in HEADS)))
    return _merge_chunks(tuple(normed)) * nw * jax.nn.silu(hgate), sts


def _seg(ref, seg):
    return ref[:, seg * D_GROUP:(seg + 1) * D_GROUP]


def _hgrn2_fwd(proj, logits, norm_w):
    s = proj.shape[0]
    rows = CHUNKS_PER_STEP * CHUNK
    nc = s // rows

    def body(p_ref, lg_ref, nw_ref, y_ref, st_out_ref, st_scr):
        @pl.when(pl.program_id(0) == 0)
        def _():
            st_scr[...] = jnp.zeros_like(st_scr)

        sts = tuple(st_scr[h] for h in HEADS)
        y, sts_new = _hg_chunk(_seg(p_ref, 0), _seg(p_ref, 1), _seg(p_ref, 2), _seg(p_ref, 3),
                               lg_ref[0:1, :], lg_ref[1:2, :], nw_ref[...], sts)
        y_ref[...] = y.astype(y_ref.dtype)
        for h in HEADS:
            st_out_ref[h] = sts[h]
            st_scr[h] = sts_new[h]

    return pl.pallas_call(
        body, name="hgrn2_fwd", grid=(nc,),
        in_specs=[pl.BlockSpec((rows, 4 * D_GROUP), lambda c: (c, 0)),
                  pl.BlockSpec((2, D_GROUP), lambda c: (0, 0)),
                  pl.BlockSpec((1, D_GROUP), lambda c: (0, 0))],
        out_specs=[pl.BlockSpec((rows, D_GROUP), lambda c: (c, 0)),
                   pl.BlockSpec((None, N_HEADS, D_HEAD, D_HEAD), lambda c: (c, 0, 0, 0))],
        out_shape=[jax.ShapeDtypeStruct((s, 2 * D_GROUP), BF16),
                   jax.ShapeDtypeStruct((nc, N_HEADS, D_HEAD, D_HEAD), F32)],
        scratch_shapes=[pltpu.VMEM((N_HEADS, D_HEAD, D_HEAD), F32)],
        compiler_params=_params(("arbitrary",)),
    )(proj, logits, norm_w)


def _hgrn2_bwd(proj, logits, norm_w, states, d_y):
    s = proj.shape[0]
    rows = CHUNKS_PER_STEP * CHUNK
    nc = s // rows

    def body(p_ref, lg_ref, nw_ref, st_ref, dy_ref, dp_ref, dl_ref, dnw_ref, dsum_ref, dst_scr):
        @pl.when(pl.program_id(0) == 0)
        def _():
            dst_scr[...] = jnp.zeros_like(dst_scr)
            dl_ref[...] = jnp.zeros_like(dl_ref)
            dnw_ref[...] = jnp.zeros_like(dnw_ref)
            dsum_ref[...] = jnp.zeros_like(dsum_ref)

        _, vjp = jax.vjp(_hg_chunk, _seg(p_ref, 0), _seg(p_ref, 1), _seg(p_ref, 2), _seg(p_ref, 3),
                         lg_ref[0:1, :], lg_ref[1:2, :], nw_ref[...], tuple(st_ref[h] for h in HEADS))
        d_hq, d_hf, d_hi, d_hg, d_l0, d_l1, d_nw, d_sts = vjp((dy_ref[...], tuple(dst_scr[h] for h in HEADS)))
        for seg, val in enumerate((d_hq, d_hf, d_hi, d_hg)):
            dp_ref[:, seg * D_GROUP:(seg + 1) * D_GROUP] = val.astype(dp_ref.dtype)
            dsum_ref[:, seg * D_GROUP:(seg + 1) * D_GROUP] += jnp.sum(val, axis=0, keepdims=True)
        dl_ref[0:1, :] += d_l0
        dl_ref[1:2, :] += d_l1
        dnw_ref[...] += d_nw
        for h in HEADS:
            dst_scr[h] = d_sts[h]

    rev = lambda c: nc - 1 - c
    return pl.pallas_call(
        body, name="hgrn2_bwd", grid=(nc,),
        in_specs=[pl.BlockSpec((rows, 4 * D_GROUP), lambda c: (rev(c), 0)),
                  pl.BlockSpec((2, D_GROUP), lambda c: (0, 0)),
                  pl.BlockSpec((1, D_GROUP), lambda c: (0, 0)),
                  pl.BlockSpec((None, N_HEADS, D_HEAD, D_HEAD), lambda c: (rev(c), 0, 0, 0)),
                  pl.BlockSpec((rows, D_GROUP), lambda c: (rev(c), 0))],
        out_specs=[pl.BlockSpec((rows, 4 * D_GROUP), lambda c: (rev(c), 0)),
                   pl.BlockSpec((2, D_GROUP), lambda c: (0, 0)),
                   pl.BlockSpec((1, D_GROUP), lambda c: (0, 0)),
                   pl.BlockSpec((1, 4 * D_GROUP), lambda c: (0, 0))],
        out_shape=[jax.ShapeDtypeStruct((s, D_IN_MAIN), BF16), jax.ShapeDtypeStruct((2, D_GROUP), F32),
                   jax.ShapeDtypeStruct((1, D_GROUP), F32), jax.ShapeDtypeStruct((1, 4 * D_GROUP), F32)],
        scratch_shapes=[pltpu.VMEM((N_HEADS, D_HEAD, D_HEAD), F32)],
        compiler_params=_params(("arbitrary",)),
    )(proj, logits, norm_w, states, d_y)


def _gate_column(gates, lane, idx):
    return jnp.sum(jnp.where(lane == idx, gates, 0.0), axis=1, keepdims=True)


def _head_layer_norm(h):
    mu = jnp.mean(h, axis=-1, keepdims=True)
    var = jnp.mean(jnp.square(h - mu), axis=-1, keepdims=True)
    return (h - mu) * lax.rsqrt(var + LN_EPS)


def _ml_chunk(qc, kc, v, mo, gates, nw, cts, ns, ms):
    n = qc.shape[0] // CHUNK
    row, col = _chunk_masks(CHUNK)
    mask = col <= row
    eye = col == row
    to_row = lambda t: jnp.sum(jnp.where(eye, t, 0.0), axis=0, keepdims=True)
    q = _blocks(qc * (D_HEAD ** -0.5))
    k = _blocks(kc)
    vs = _blocks(v)
    gate_rows = _split_chunks(gates)
    lane = lax.broadcasted_iota(jnp.int32, gate_rows[0].shape, 1)
    each = [(i, h) for i in range(n) for h in HEADS]
    on_each = lambda f: {ih: f(*ih) for ih in each}
    ig = on_each(lambda i, h: _gate_column(gate_rows[i], lane, h))
    lf = on_each(lambda i, h: jax.nn.log_sigmoid(_gate_column(gate_rows[i], lane, N_HEADS + h)))
    lf_row = on_each(lambda i, h: to_row(lf[i, h]))
    ig_row = on_each(lambda i, h: to_row(ig[i, h]))
    b_col = on_each(lambda i, h: jnp.sum(jnp.where(mask, lf_row[i, h], 0.0), axis=1, keepdims=True))
    b_row = on_each(lambda i, h: jnp.sum(jnp.where(row <= col, lf[i, h], 0.0), axis=0, keepdims=True))
    g = on_each(lambda i, h: jnp.sum(lf[i, h], axis=0, keepdims=True))
    d = on_each(lambda i, h: jnp.where(mask, b_col[i, h] - b_row[i, h] + ig_row[i, h], -jnp.inf))
    a = on_each(lambda i, h: g[i, h] - b_col[i, h] + ig[i, h])
    m_at = {(0, h): ms[h] for h in HEADS}
    for i, h in each:
        m_at[i + 1, h] = lax.stop_gradient(jnp.maximum(g[i, h] + m_at[i, h], jnp.max(a[i, h], axis=0, keepdims=True)))
    inter = on_each(lambda i, h: b_col[i, h] + m_at[i, h])
    m_t = on_each(lambda i, h: lax.stop_gradient(jnp.maximum(inter[i, h], jnp.max(d[i, h], axis=1, keepdims=True))))
    qk = on_each(lambda i, h: _nt(q[i][h], k[i][h]))
    sc = on_each(lambda i, h: qk[i, h] * jnp.exp(d[i, h] - m_t[i, h]))
    w_inter = on_each(lambda i, h: jnp.exp(inter[i, h] - m_t[i, h]))
    sv = on_each(lambda i, h: _nn(sc[i, h], vs[i][h]))
    decay = on_each(lambda i, h: jnp.exp(g[i, h] + m_at[i, h] - m_at[i + 1, h]))
    wk = on_each(lambda i, h: k[i][h] * jnp.exp(a[i, h] - m_at[i + 1, h]))
    kv = on_each(lambda i, h: _tn(vs[i][h], wk[i, h]))
    normed = []
    for i in range(n):
        qc_state = [_nt(q[i][h], cts[h]) for h in HEADS]
        num = [sv[i, h] + w_inter[i, h] * qc_state[h] for h in HEADS]
        den = [jnp.sum(sc[i, h], axis=1, keepdims=True)
               + w_inter[i, h] * jnp.sum(q[i][h] * ns[h], axis=1, keepdims=True) for h in HEADS]
        hh = [num[h] / jnp.maximum(jnp.abs(den[h]), jnp.exp(-m_t[i, h])) for h in HEADS]
        cts = tuple(decay[i, h] * cts[h] + kv[i, h] for h in HEADS)
        ns = tuple(decay[i, h] * ns[h] + jnp.sum(wk[i, h], axis=0, keepdims=True) for h in HEADS)
        normed.append(_merge_heads(tuple(_head_layer_norm(hh[h]) for h in HEADS)))
    y = jax.nn.sigmoid(mo) * (_merge_chunks(tuple(normed)) * nw)
    return y, cts, ns, tuple(m_at[n, h] for h in HEADS)


def _mlstm_fwd(qk, proj, gates, norm_w, y):
    s = proj.shape[0]
    rows = ML_CHUNKS_PER_STEP * CHUNK
    nc = s // rows

    def body(qk_ref, vo_ref, g_ref, nw_ref, _, y_ref, ct_out, n_out, m_out, ct_scr, n_scr, m_scr):
        @pl.when(pl.program_id(0) == 0)
        def _():
            ct_scr[...] = jnp.zeros_like(ct_scr)
            n_scr[...] = jnp.zeros_like(n_scr)
            m_scr[...] = jnp.full(m_scr.shape, NEG_BIG, F32)

        cts = tuple(ct_scr[h] for h in HEADS)
        ns = tuple(n_scr[h] for h in HEADS)
        ms = tuple(m_scr[h] for h in HEADS)
        y, cts_new, ns_new, ms_new = _ml_chunk(_seg(qk_ref, 0), _seg(qk_ref, 1), _seg(vo_ref, 0), _seg(vo_ref, 1),
                                               g_ref[...], nw_ref[...], cts, ns, ms)
        y_ref[...] = y.astype(y_ref.dtype)
        for h in HEADS:
            ct_out[h], n_out[h], m_out[h] = cts[h], ns[h], ms[h]
            ct_scr[h], n_scr[h], m_scr[h] = cts_new[h], ns_new[h], ms_new[h]

    st = lambda r, w: pl.BlockSpec((None, N_HEADS, r, w), lambda c: (c, 0, 0, 0))
    return pl.pallas_call(
        body, name="mlstm_fwd", grid=(nc,),
        in_specs=[pl.BlockSpec((rows, 2 * D_GROUP), lambda c: (c, 0)),
                  pl.BlockSpec((rows, 2 * D_GROUP), lambda c: (c, VO_BLOCK)),
                  pl.BlockSpec((rows, LANES), lambda c: (c, 0)),
                  pl.BlockSpec((1, D_GROUP), lambda c: (0, 0)),
                  pl.BlockSpec(memory_space=pl.ANY)],
        out_specs=[pl.BlockSpec((rows, D_GROUP), lambda c: (c, 1)),
                   st(D_HEAD, D_HEAD), st(1, D_HEAD), st(1, 1)],
        out_shape=[jax.ShapeDtypeStruct(y.shape, y.dtype),
                   jax.ShapeDtypeStruct((nc, N_HEADS, D_HEAD, D_HEAD), F32),
                   jax.ShapeDtypeStruct((nc, N_HEADS, 1, D_HEAD), F32),
                   jax.ShapeDtypeStruct((nc, N_HEADS, 1, 1), F32)],
        input_output_aliases={4: 0},
        scratch_shapes=[pltpu.VMEM((N_HEADS, D_HEAD, D_HEAD), F32), pltpu.VMEM((N_HEADS, 1, D_HEAD), F32),
                        pltpu.VMEM((N_HEADS, 1, 1), F32)],
        compiler_params=_params(("arbitrary",)),
    )(qk, proj, gates, norm_w, y)


def _mlstm_bwd(qk, proj, gates, norm_w, ct_s, n_s, m_s, d_y, d_proj):
    s = proj.shape[0]
    rows = ML_CHUNKS_PER_STEP * CHUNK
    nc = s // rows

    def body(qk_ref, vo_ref, g_ref, nw_ref, ct_ref, n_ref, m_ref, dy_ref, _,
             dp_ref, dqk_ref, dg_ref, dnw_ref, dsum_ref, dct_scr, dn_scr):
        @pl.when(pl.program_id(0) == 0)
        def _():
            dct_scr[...] = jnp.zeros_like(dct_scr)
            dn_scr[...] = jnp.zeros_like(dn_scr)
            dnw_ref[...] = jnp.zeros_like(dnw_ref)
            dsum_ref[...] = jnp.zeros_like(dsum_ref)

        ms = tuple(m_ref[h] for h in HEADS)
        step = lambda *a: _ml_chunk(*a, ms)[:3]
        _, vjp = jax.vjp(step, _seg(qk_ref, 0), _seg(qk_ref, 1), _seg(vo_ref, 0), _seg(vo_ref, 1), g_ref[...],
                         nw_ref[...], tuple(ct_ref[h] for h in HEADS), tuple(n_ref[h] for h in HEADS))
        d_q, d_k, d_v, d_o, d_gates, d_nw, d_cts, d_ns = vjp(
            (dy_ref[...], tuple(dct_scr[h] for h in HEADS), tuple(dn_scr[h] for h in HEADS)))
        dqk_ref[:, 0:D_GROUP] = d_q
        dqk_ref[:, D_GROUP:2 * D_GROUP] = d_k
        for seg, val in enumerate((d_v, d_o)):
            dp_ref[:, seg * D_GROUP:(seg + 1) * D_GROUP] = val.astype(dp_ref.dtype)
            dsum_ref[:, seg * D_GROUP:(seg + 1) * D_GROUP] += jnp.sum(val, axis=0, keepdims=True)
        dg_ref[...] = d_gates
        dnw_ref[...] += d_nw
        for h in HEADS:
            dct_scr[h] = d_cts[h]
            dn_scr[h] = d_ns[h]

    rev = lambda c: nc - 1 - c
    st = lambda r, w: pl.BlockSpec((None, N_HEADS, r, w), lambda c: (rev(c), 0, 0, 0))
    return pl.pallas_call(
        body, name="mlstm_bwd", grid=(nc,),
        in_specs=[pl.BlockSpec((rows, 2 * D_GROUP), lambda c: (rev(c), 0)),
                  pl.BlockSpec((rows, 2 * D_GROUP), lambda c: (rev(c), VO_BLOCK)),
                  pl.BlockSpec((rows, LANES), lambda c: (rev(c), 0)),
                  pl.BlockSpec((1, D_GROUP), lambda c: (0, 0)),
                  st(D_HEAD, D_HEAD), st(1, D_HEAD), st(1, 1),
                  pl.BlockSpec((rows, D_GROUP), lambda c: (rev(c), 1)),
                  pl.BlockSpec(memory_space=pl.ANY)],
        out_specs=[pl.BlockSpec((rows, 2 * D_GROUP), lambda c: (rev(c), VO_BLOCK)),
                   pl.BlockSpec((rows, 2 * D_GROUP), lambda c: (rev(c), 0)),
                   pl.BlockSpec((rows, LANES), lambda c: (rev(c), 0)),
                   pl.BlockSpec((1, D_GROUP), lambda c: (0, 0)),
                   pl.BlockSpec((1, 2 * D_GROUP), lambda c: (0, 0))],
        out_shape=[jax.ShapeDtypeStruct(d_proj.shape, d_proj.dtype), jax.ShapeDtypeStruct((s, 2 * D_GROUP), F32),
                   jax.ShapeDtypeStruct((s, LANES), F32), jax.ShapeDtypeStruct((1, D_GROUP), F32),
                   jax.ShapeDtypeStruct((1, 2 * D_GROUP), F32)],
        input_output_aliases={8: 0},
        scratch_shapes=[pltpu.VMEM((N_HEADS, D_HEAD, D_HEAD), F32), pltpu.VMEM((N_HEADS, 1, D_HEAD), F32)],
        compiler_params=_params(("arbitrary",)),
    )(qk, proj, gates, norm_w, ct_s, n_s, m_s, d_y, d_proj)


LN_TOKENS = 512
ATT_TOKENS = 512


def _proj_res_ln(a, w, xres, g, b, name):
    s, dm = xres.shape
    k = a.shape[1]
    tb = min(LN_TOKENS, s)

    def body(a_ref, w_ref, x_ref, g_ref, b_ref, z_ref, o_ref):
        halves = [slice(0, tb // 2), slice(tb // 2, tb)]
        zs = [ALPHA * x_ref[rows, :] + _nn_raw(a_ref[rows, :], w_ref[...]) for rows in halves]
        for rows, z in zip(halves, zs):
            z_ref[rows, :] = z
            o_ref[rows, :] = _layer_norm(z, g_ref[...], b_ref[...])

    tok = pl.BlockSpec((tb, dm), lambda i: (i, 0))
    vec = pl.BlockSpec((1, dm), lambda i: (0, 0))
    act = jax.ShapeDtypeStruct((s, dm), F32)
    return pl.pallas_call(
        body, name=name, grid=(s // tb,),
        in_specs=[pl.BlockSpec((tb, k), lambda i: (i, 0)), pl.BlockSpec((k, dm), lambda i: (0, 0)), tok, vec, vec],
        out_specs=[tok, tok], out_shape=[act, act], compiler_params=_params(("parallel",)),
    )(a, w, xres, g, b)


def _ln_bwd_proj(d_out, z, g, b, w, name):
    s, dm = z.shape
    k = w.shape[0]
    tb = min(LN_TOKENS, s)

    def body(do_ref, z_ref, g_ref, b_ref, w_ref, dz_ref, da_ref, dg_ref, db_ref):
        @pl.when(pl.program_id(0) == 0)
        def _():
            dg_ref[...] = jnp.zeros_like(dg_ref)
            db_ref[...] = jnp.zeros_like(db_ref)

        halves = [slice(0, tb // 2), slice(tb // 2, tb)]
        d_zs = []
        for rows in halves:
            _, vjp = jax.vjp(_layer_norm, z_ref[rows, :], g_ref[...], b_ref[...])
            d_z, d_g, d_b = vjp(do_ref[rows, :])
            dz_ref[rows, :] = d_z
            dg_ref[...] += d_g
            db_ref[...] += d_b
            d_zs.append(d_z)
        for rows, d_z in zip(halves, d_zs):
            da_ref[rows, :] = _nt_raw(d_z, w_ref[...])

    tok = pl.BlockSpec((tb, dm), lambda i: (i, 0))
    vec = pl.BlockSpec((1, dm), lambda i: (0, 0))
    return pl.pallas_call(
        body, name=name, grid=(s // tb,),
        in_specs=[tok, tok, vec, vec, pl.BlockSpec((k, dm), lambda i: (0, 0))],
        out_specs=[tok, pl.BlockSpec((tb, k), lambda i: (i, 0)), vec, vec],
        out_shape=[jax.ShapeDtypeStruct((s, dm), F32), jax.ShapeDtypeStruct((s, k), F32),
                   jax.ShapeDtypeStruct((1, dm), F32), jax.ShapeDtypeStruct((1, dm), F32)],
        compiler_params=_params(("arbitrary",)),
    )(d_out, z, g, b, w)


def _proj_loss_tail(a, w, xres, g, b, target):
    s, dm = xres.shape
    k = a.shape[1]
    tb = min(ATT_TOKENS, s)

    def loss_fn(z, gg, bb, tgt):
        err = jnp.square(_layer_norm(z, gg, bb) - tgt)
        return 0.5 * jnp.sum(jnp.mean(err, axis=-1, keepdims=True), axis=0, keepdims=True)

    def body(a_ref, w_ref, x_ref, g_ref, b_ref, t_ref, loss_ref, dz_ref, dg_ref, db_ref):
        @pl.when(pl.program_id(0) == 0)
        def _():
            loss_ref[...] = jnp.zeros_like(loss_ref)
            dg_ref[...] = jnp.zeros_like(dg_ref)
            db_ref[...] = jnp.zeros_like(db_ref)

        halves = [slice(0, tb // 2), slice(tb // 2, tb)]
        zs = [ALPHA * x_ref[rows, :] + _nn_raw(a_ref[rows, :], w_ref[...]) for rows in halves]
        for rows, z in zip(halves, zs):
            tgt = t_ref[rows, :]
            loss, vjp = jax.vjp(lambda zz, gg, bb, tgt=tgt: loss_fn(zz, gg, bb, tgt), z, g_ref[...], b_ref[...])
            d_z, d_g, d_b = vjp(jnp.ones((1, 1), F32))
            loss_ref[...] += loss
            dz_ref[rows, :] = d_z
            dg_ref[...] += d_g
            db_ref[...] += d_b

    tok = pl.BlockSpec((tb, dm), lambda i: (i, 0))
    vec = pl.BlockSpec((1, dm), lambda i: (0, 0))
    one = pl.BlockSpec((1, 1), lambda i: (0, 0))
    return pl.pallas_call(
        body, name="ffn_down_loss_tail", grid=(s // tb,),
        in_specs=[pl.BlockSpec((tb, k), lambda i: (i, 0)), pl.BlockSpec((k, dm), lambda i: (0, 0)), tok, vec, vec, tok],
        out_specs=[one, tok, vec, vec],
        out_shape=[jax.ShapeDtypeStruct((1, 1), F32), jax.ShapeDtypeStruct((s, dm), F32),
                   jax.ShapeDtypeStruct((1, dm), F32), jax.ShapeDtypeStruct((1, dm), F32)],
        compiler_params=_params(("arbitrary",)),
    )(a, w, xres, g, b, target)


def _att_heads(qs, ks, vs):
    sc = [_nt(q, k) * (CA_DH ** -0.5) for q, k in zip(qs, ks)]
    p = [jax.nn.softmax(s, axis=-1) for s in sc]
    return tuple(_nn(pp, v) for pp, v in zip(p, vs))


def _head_slices(ref_or_value, offset):
    return tuple(ref_or_value[:, offset + h * CA_DH:offset + (h + 1) * CA_DH] for h in range(CA_HEADS))


def _cross_attention_fwd(x1, kv, wq, wo, g, b):
    s = x1.shape[0]
    tb = min(ATT_TOKENS, s)

    def body(x_ref, kv_ref, wq_ref, wo_ref, g_ref, b_ref, att_ref, z_ref, o_ref):
        x_blk = x_ref[...]
        q = _nn_raw(x_blk, wq_ref[...])
        att = jnp.concatenate(_att_heads(_head_slices(q, 0), _head_slices(kv_ref, 0), _head_slices(kv_ref, D_MODEL)),
                              axis=1)
        att_ref[...] = att.astype(att_ref.dtype)
        z = ALPHA * x_blk + _nn_raw(att, wo_ref[...])
        z_ref[...] = z
        o_ref[...] = _layer_norm(z, g_ref[...], b_ref[...])

    tok = pl.BlockSpec((tb, D_MODEL), lambda i: (i, 0))
    mat = pl.BlockSpec((D_MODEL, D_MODEL), lambda i: (0, 0))
    vec = pl.BlockSpec((1, D_MODEL), lambda i: (0, 0))
    act = jax.ShapeDtypeStruct((s, D_MODEL), F32)
    return pl.pallas_call(
        body, name="cross_attention_fwd", grid=(s // tb,),
        in_specs=[tok, pl.BlockSpec((N_MEM, 2 * D_MODEL), lambda i: (0, 0)), mat, mat, vec, vec],
        out_specs=[tok, tok, tok],
        out_shape=[jax.ShapeDtypeStruct((s, D_MODEL), BF16), act, act],
        compiler_params=_params(("parallel",)),
    )(x1, kv, wq, wo, g, b)


def _cross_attention_bwd(d_x2, x1, z2, kv, wq, wo, g, b):
    s = x1.shape[0]
    tb = min(ATT_TOKENS, s)

    def body(dx2_ref, x_ref, z_ref, kv_ref, wq_ref, wo_ref, g_ref, b_ref,
             dx1_ref, dq_ref, dz_ref, dkv_ref, dg_ref, db_ref):
        @pl.when(pl.program_id(0) == 0)
        def _():
            dkv_ref[...] = jnp.zeros_like(dkv_ref)
            dg_ref[...] = jnp.zeros_like(dg_ref)
            db_ref[...] = jnp.zeros_like(db_ref)

        q = _nn_raw(x_ref[...], wq_ref[...])
        _, ln_vjp = jax.vjp(_layer_norm, z_ref[...], g_ref[...], b_ref[...])
        d_z, d_g, d_b = ln_vjp(dx2_ref[...])
        dg_ref[...] += d_g
        db_ref[...] += d_b
        dz_ref[...] = d_z.astype(dz_ref.dtype)
        d_att = _nt_raw(d_z, wo_ref[...])
        _, vjp = jax.vjp(_att_heads, _head_slices(q, 0), _head_slices(kv_ref, 0), _head_slices(kv_ref, D_MODEL))
        d_qs, d_ks, d_vs = vjp(_head_slices(d_att, 0))
        for h in range(CA_HEADS):
            lo = h * CA_DH
            dkv_ref[:, lo:lo + CA_DH] += d_ks[h]
            dkv_ref[:, D_MODEL + lo:D_MODEL + lo + CA_DH] += d_vs[h]
        d_q = jnp.concatenate(d_qs, axis=1)
        dq_ref[...] = d_q.astype(dq_ref.dtype)
        dx1_ref[...] = ALPHA * d_z + _nt_raw(d_q, wq_ref[...])

    tok = pl.BlockSpec((tb, D_MODEL), lambda i: (i, 0))
    mem = pl.BlockSpec((N_MEM, 2 * D_MODEL), lambda i: (0, 0))
    mat = pl.BlockSpec((D_MODEL, D_MODEL), lambda i: (0, 0))
    vec = pl.BlockSpec((1, D_MODEL), lambda i: (0, 0))
    low = jax.ShapeDtypeStruct((s, D_MODEL), BF16)
    return pl.pallas_call(
        body, name="cross_attention_bwd", grid=(s // tb,),
        in_specs=[tok, tok, tok, mem, mat, mat, vec, vec], out_specs=[tok, tok, tok, mem, vec, vec],
        out_shape=[jax.ShapeDtypeStruct((s, D_MODEL), F32), low, low,
                   jax.ShapeDtypeStruct((N_MEM, 2 * D_MODEL), F32),
                   jax.ShapeDtypeStruct((1, D_MODEL), F32), jax.ShapeDtypeStruct((1, D_MODEL), F32)],
        compiler_params=_params(("arbitrary",)),
    )(d_x2, x1, z2, kv, wq, wo, g, b)


def _local_step(x, mem, target, w, mid_weights=None, ffn_weights=None, down_weights=None, on_ffn_grads=None,
                on_mid_grads=None,
                on_small_grads=None, on_last_grads=None):
    w = dict(w)
    s = x.shape[0]
    tm = min(512, s)
    tt_big = min(1024, s)
    proj, gates = _input_projection(x, w["w_in_t"], w["w_in_gate_t"], w["b_in_main"], w["b_in_gate"],
                                    min(2048, s), 512)
    qk = _ml_conv_fwd(proj, w["ml_conv_w"], w["ml_conv_b"])
    y, hg_states = _hgrn2_fwd(proj, w["hg_lb_logits"], w["hg_norm_w"])
    y, ct_s, n_s, m_s = _mlstm_fwd(qk, proj, gates, w["ml_norm_w"], y)
    if mid_weights is not None:
        w.update(mid_weights(y))
    z1, x1 = _proj_res_ln(y, w["w_out"], x, w["ln1_g"], w["ln1_b"], "out_proj_ln1")
    kv = _matmul_nn(mem, w["ca_wkv"], None, N_MEM, CA_DH, "kv")
    att, z2, x2 = _cross_attention_fwd(x1, kv, w["ca_wq"], w["ca_wo"], w["ln2_g"], w["ln2_b"])
    if ffn_weights is not None:
        w.update(ffn_weights(x2))
    u = _matmul_nt(x2, w["ffn_w_up_t"], min(1024, s), UP_TILE, "ffn_up", out_dtype=BF16)
    hid = _ffn_conv_fwd(u, w["ffn_conv_w"], w["ffn_conv_b"])
    if down_weights is not None:
        w.update(down_weights(hid))
    loss, d_z3, d_ln3_g, d_ln3_b = _proj_loss_tail(hid, w["ffn_w_down"], x2, w["ln3_g"], w["ln3_b"], target)
    grads = {"ln3_g": d_ln3_g, "ln3_b": d_ln3_b}
    grads["ffn_w_down"] = _matmul_tn(hid, d_z3, UP_TILE, D_MODEL, tt_big, "d_w_down")
    d_hid = _matmul_nt(d_z3, w["ffn_w_down"], tm, D_FF, "d_hid", out_dtype=BF16)
    d_ug, d_uv, d_cwg, d_cwv, d_cbg, d_cbv = _ffn_conv_bwd(u, w["ffn_conv_w"], w["ffn_conv_b"], d_hid)
    grads["ffn_conv_w"] = jnp.concatenate([d_cwg, d_cwv], axis=-1)
    grads["ffn_conv_b"] = jnp.concatenate([d_cbg, d_cbv], axis=-1)
    d_w_up = _matmul_tn(d_ug, x2, UP_TILE, D_MODEL, tt_big, "d_w_up_gate", rows=D_UP)
    grads["ffn_w_up"] = _matmul_tn(d_uv, x2, UP_TILE, D_MODEL, tt_big, "d_w_up_val", rows=D_UP, row0=D_FF,
                                   into=d_w_up)
    d_x2 = _matmul_nn_sum([(d_ug, w["ffn_w_up_t"], 0), (d_uv, w["ffn_w_up_t"], D_FF)], d_z3, ALPHA,
                          min(256, s), "d_x2")
    if on_ffn_grads is not None:
        d_x2 = on_ffn_grads(grads, d_x2)
    d_x1, d_q, d_z2, d_kv, grads["ln2_g"], grads["ln2_b"] = _cross_attention_bwd(
        d_x2, x1, z2, kv, w["ca_wq"], w["ca_wo"], w["ln2_g"], w["ln2_b"])
    grads["ca_wo"] = _matmul_tn(att, d_z2, D_MODEL, D_MODEL, tt_big, "d_ca_wo")
    grads["ca_wq"] = _matmul_tn(x1, d_q, D_MODEL, D_MODEL, tt_big, "d_ca_wq")
    grads["ca_wkv"] = _matmul_tn(mem, d_kv, D_MODEL, CA_DH, N_MEM, "d_ca_wkv", shards=N_DEV, group=N_DEV)
    d_z1, d_y, grads["ln1_g"], grads["ln1_b"] = _ln_bwd_proj(d_x1, z1, w["ln1_g"], w["ln1_b"], w["w_out"],
                                                             "ln1_bwd_out_proj")
    grads["w_out"] = _matmul_tn(y, d_z1, D_MODEL, D_MODEL, tt_big, "d_w_out")
    if on_mid_grads is not None:
        d_y = on_mid_grads(grads, d_y)
    d_proj, grads["hg_lb_logits"], grads["hg_norm_w"], db_hg = _hgrn2_bwd(
        proj, w["hg_lb_logits"], w["hg_norm_w"], hg_states, d_y)
    d_proj, d_qk, d_gates, grads["ml_norm_w"], db_vo = _mlstm_bwd(
        qk, proj, gates, w["ml_norm_w"], ct_s, n_s, m_s, d_y, d_proj)
    d_proj, grads["ml_conv_w"], grads["ml_conv_b"], db_qk = _ml_conv_bwd(
        proj, w["ml_conv_w"], w["ml_conv_b"], d_qk, d_proj)
    grads["b_in_main"] = jnp.concatenate([db_hg, db_qk, db_vo], axis=-1)
    grads["b_in_gate"] = jnp.sum(d_gates, axis=0, keepdims=True)
    if on_small_grads is not None:
        d_proj = on_small_grads(grads, loss, d_proj)
    grads["w_in"] = _input_projection_grads(d_proj, d_gates, x, min(1024, D_IN_MAIN), tt_big)
    if on_last_grads is not None:
        d_z1 = on_last_grads(grads, d_z1)
    grad_x = _matmul_nn_sum([(d_proj, w["w_in_t"], 0), (d_gates, w["w_in_gate_t"], 0)], d_z1, ALPHA, tm, "d_x")
    return loss, grad_x, grads


HBM_SPEC = pl.BlockSpec(memory_space=pltpu.HBM)


def _coords():
    return lax.axis_index("x"), lax.axis_index("y"), lax.axis_index("c")


def _other_chips(x, y):
    return [(1 - x, y), (x, 1 - y), (1 - x, 1 - y)]


def _my_slot():
    x, y, c = _coords()
    return 4 * x + 2 * y + c


SEM_SPEC = pl.BlockSpec(memory_space=pltpu.SEMAPHORE)
ANY_SPEC = pl.BlockSpec(memory_space=pl.ANY)
SIDE_EFFECT = pltpu.SideEffectType.DATAFLOW_SIDE_EFFECTING


def _peer(x, y, c, d):
    flip = lambda v, bit: 1 - v if bit else v
    p = (flip(x, d & 4), flip(y, d & 2), flip(c, d & 1))
    return p, 4 * p[0] + 2 * p[1] + p[2]


def _direct_copies(gather, src_refs, land_refs, send_sems, recv_sems):
    x, y, c = _coords()
    me = 4 * x + 2 * y + c
    copies = []
    for a in range(len(src_refs)):
        for d in range(1, N_DEV):
            peer, peer_slot = _peer(x, y, c, d)
            copies.append(pltpu.make_async_remote_copy(
                src_ref=src_refs[a] if gather else src_refs[a].at[peer_slot],
                dst_ref=land_refs[a].at[me] if gather else land_refs[a].at[d - 1],
                send_sem=send_sems.at[7 * a + d - 1], recv_sem=recv_sems.at[7 * a + d - 1],
                device_id=peer, device_id_type=MESH))
    return copies


def _hbm(t):
    return pltpu.HBM(t.shape, t.dtype)


def _chip_copies(land_refs, send_sems, recv_sems):
    x, y, c = _coords()
    me = 4 * x + 2 * y + c
    targets = [(x, y, 1 - c)] + [(cx, cy, c) for cx, cy in _other_chips(x, y)]
    return [pltpu.make_async_remote_copy(
        src_ref=land.at[me], dst_ref=land.at[me], send_sem=send_sems.at[4 * a + k],
        recv_sem=recv_sems.at[4 * a + k], device_id=target, device_id_type=MESH)
        for a, land in enumerate(land_refs) for k, target in enumerate(targets)]


def _forward_copies(land_refs, send_sems, recv_sems):
    x, y, c = _coords()
    return [pltpu.make_async_remote_copy(
        src_ref=land_refs[a].at[4 * cx + 2 * cy + c], dst_ref=land_refs[a].at[4 * cx + 2 * cy + c],
        send_sem=send_sems.at[3 * a + j], recv_sem=recv_sems.at[3 * a + j],
        device_id=(x, y, 1 - c), device_id_type=MESH)
        for a in range(len(land_refs)) for j, (cx, cy) in enumerate(_other_chips(x, y))]


def _split_copy_start(make_copies, n_sems, operands, through, name):
    n_ops = len(operands)

    def body(*refs):
        for cp in make_copies(refs[:n_ops], refs[n_ops + 1], refs[n_ops + 2]):
            cp.start()

    ins = [pltpu.with_memory_space_constraint(t, pltpu.HBM) for t in (*operands, through)]
    sems = pltpu.SemaphoreType.DMA((n_sems,))
    res = pl.pallas_call(
        body, name=name, out_shape=(sems, sems, *[_hbm(t) for t in ins]),
        in_specs=[HBM_SPEC] * (n_ops + 1), out_specs=(SEM_SPEC, SEM_SPEC, *[HBM_SPEC] * (n_ops + 1)),
        input_output_aliases={i: 2 + i for i in range(n_ops + 1)},
        compiler_params=pltpu.CompilerParams(has_side_effects=SIDE_EFFECT),
    )(*ins)
    return (res[0], res[1], list(res[2:2 + n_ops])), res[2 + n_ops]


def _split_copy_wait(make_copies, started, after, name):
    send_sems, recv_sems, operands = started
    n_ops = len(operands)
    after = list(after) if isinstance(after, (list, tuple)) else [after]

    def body(*refs):
        for cp in make_copies(refs[:n_ops], refs[n_ops], refs[n_ops + 1]):
            cp.wait_send()
            cp.wait_recv()

    res = pl.pallas_call(
        body, name=name, out_shape=tuple(_hbm(t) for t in operands),
        in_specs=[HBM_SPEC] * n_ops + [SEM_SPEC, SEM_SPEC] + [ANY_SPEC] * len(after),
        out_specs=tuple([HBM_SPEC] * n_ops), input_output_aliases={i: i for i in range(n_ops)},
        compiler_params=pltpu.CompilerParams(has_side_effects=SIDE_EFFECT),
    )(*operands, send_sems, recv_sems, *after)
    return list(res)


def _halves(make_copies, na):
    return lambda refs, send_sems, recv_sems: make_copies(refs[:na], refs[na:], send_sems, recv_sems)


def _direct_start(gather, arrays, through, name):
    na = len(arrays)
    lands = [lax.empty((N_DEV,) + t.shape if gather else (N_DEV - 1,) + t.shape[1:], t.dtype) for t in arrays]
    return _split_copy_start(_halves(functools.partial(_direct_copies, gather), na), 7 * na, [*arrays, *lands],
                             through, name)


def _direct_wait(gather, started, after, name):
    na = len(started[2]) // 2
    operands = _split_copy_wait(_halves(functools.partial(_direct_copies, gather), na), started, after, name)
    return operands[:na], operands[na:]


def _slot_copies(land_refs, send_sems, recv_sems, first=0):
    x, y, c = _coords()
    me = 4 * x + 2 * y + c
    return [pltpu.make_async_remote_copy(
        src_ref=land.at[me], dst_ref=land.at[me], send_sem=send_sems.at[7 * a + d - 1],
        recv_sem=recv_sems.at[7 * a + d - 1], device_id=_peer(x, y, c, d)[0], device_id_type=MESH)
        for a, land in enumerate(land_refs, first) for d in range(1, N_DEV)]


def _own_slot_filled(block, slot):
    return lax.dynamic_update_index_in_dim(lax.empty((N_DEV,) + block.shape, block.dtype), block, slot, 0)


def _two_level_gather(shards, glue, name):
    na = len(shards)
    lands = [_own_slot_filled(t, _my_slot()) for t in shards]
    slot = jnp.full((SUBLANES, LANES), _my_slot(), jnp.int32)
    started, slot = _split_copy_start(_chip_copies, 4 * na, lands, slot, name + "_start")
    lands = _split_copy_wait(_chip_copies, started, glue(slot[0, 0]), name + "_wait")
    started, slot = _split_copy_start(_forward_copies, 3 * na, lands, slot, name + "_forward_start")
    return _split_copy_wait(_forward_copies, started, slot, name + "_forward_wait")


def _row_tile(rows):
    for t in (256, 176, 128):
        if rows % t == 0 and rows > t:
            return t
    return rows


def _adamw_math(g, w, m, v):
    m_new = ADAM_B1 * m + (1.0 - ADAM_B1) * g
    v_new = ADAM_B2 * v + (1.0 - ADAM_B2) * jnp.square(g)
    m_hat = m_new / (1.0 - ADAM_B1 ** ADAM_STEP)
    v_hat = v_new / (1.0 - ADAM_B2 ** ADAM_STEP)
    delta = -ADAM_LR * (m_hat / (jnp.sqrt(v_hat) + ADAM_EPS) + ADAM_WD * w)
    return delta, m_new, v_new


def _adamw_sharded(chip, sums, got, w, m, v, name):
    r, c = w.shape
    tr = _row_tile(r)
    n_got = got.shape[0]

    def body(chip_ref, s_ref, g_ref, w_ref, m_ref, v_ref, go_ref, d_ref, nm_ref, nv_ref):
        g = s_ref[...].astype(F32)
        for i in range(n_got):
            g = g + g_ref[i].astype(F32)
        delta, m_new, v_new = _adamw_math(g, w_ref[...], m_ref[...], v_ref[...])
        go_ref[...] = g
        d_ref[...] = delta
        nm_ref[...] = m_new
        nv_ref[...] = v_new

    blk = pl.BlockSpec((tr, c), lambda i, chip_ref: (i, 0))
    out = jax.ShapeDtypeStruct((r, c), F32)
    return pl.pallas_call(
        body, name=name,
        grid_spec=pltpu.PrefetchScalarGridSpec(
            num_scalar_prefetch=1, grid=(r // tr,),
            in_specs=[pl.BlockSpec((None, tr, c), lambda i, chip_ref: (chip_ref[0], i, 0)),
                      pl.BlockSpec((n_got, tr, c), lambda i, chip_ref: (0, i, 0)), blk, blk, blk],
            out_specs=[blk, blk, blk, blk]),
        out_shape=[out, out, out, out],
        compiler_params=_params(("parallel",)),
    )(chip, sums, got, w, m, v)


def _adamw_replicated(parts, w, m, v):
    p, r, c = parts.shape
    names = SMALL_NAMES
    shapes = [w[n].shape for n in names]

    def body(*refs):
        p_ref = refs[0]
        ins = refs[1:1 + 3 * len(names)]
        outs = refs[1 + 3 * len(names):-2]
        loss_ref, sum_scr = refs[-2], refs[-1]
        total = p_ref[0]
        for i in range(1, p):
            total = total + p_ref[i]
        sum_scr[...] = total
        for k, n in enumerate(names):
            w_ref, m_ref, v_ref = ins[3 * k:3 * k + 3]
            g_ref, d_ref, nm_ref, nv_ref = outs[4 * k:4 * k + 4]
            for row, lane0, width, src_row in _small_pieces(n, shapes[k]):
                here = (slice(row, row + 1), slice(lane0, lane0 + width))
                g = sum_scr[src_row:src_row + 1, 0:width]
                delta, m_new, v_new = _adamw_math(g, w_ref[here], m_ref[here], v_ref[here])
                g_ref[here] = g
                d_ref[here] = delta
                nm_ref[here] = m_new
                nv_ref[here] = v_new
        loss_ref[...] = sum_scr[SMALL_LOSS_ROW:SMALL_LOSS_ROW + 1, 0:1]

    whole = lambda shape: pl.BlockSpec(shape, lambda i: (0,) * len(shape))
    args = [parts] + [t[n] for n in names for t in (w, m, v)]
    out_shape = [jax.ShapeDtypeStruct(s, F32) for s in shapes for _ in range(4)] + [jax.ShapeDtypeStruct((1, 1), F32)]
    res = pl.pallas_call(
        body, name="adamw_replicated", grid=(1,),
        in_specs=[whole(t.shape) for t in args], out_specs=[whole(s.shape) for s in out_shape],
        out_shape=out_shape, scratch_shapes=[pltpu.VMEM((r, c), F32)],
        compiler_params=_params(("arbitrary",)),
    )(*args)
    results = [{n: res[4 * k + j] for k, n in enumerate(names)} for j in range(4)]
    return results, res[-1]


SHARDED_NAMES = ("w_in", "ml_conv_w", "w_out", "ca_wq", "ca_wkv", "ca_wo", "ffn_w_up", "ffn_conv_w", "ffn_w_down")
SMALL_NAMES = ("b_in", "hg_lb_logits", "hg_norm_w", "ml_conv_b", "ml_norm_w", "ln1_g", "ln1_b",
               "ln2_g", "ln2_b", "ffn_conv_b", "ln3_g", "ln3_b")
WEIGHT_NAMES = ("w_in", "b_in", "hg_lb_logits", "hg_norm_w", "ml_conv_w", "ml_conv_b", "ml_norm_w", "w_out",
                "ln1_g", "ln1_b", "ca_wq", "ca_wkv", "ca_wo", "ln2_g", "ln2_b", "ffn_w_up", "ffn_conv_w",
                "ffn_conv_b", "ffn_w_down", "ln3_g", "ln3_b")
PAD_TO = {"ffn_conv_w": UP_SHARD_P}
SMALL_ROWS = 24
SMALL_W = D_MODEL
SMALL_SHAPES = {"b_in": (1, D_IN), "hg_lb_logits": (2, D_GROUP), "hg_norm_w": (1, D_GROUP),
                "ml_conv_b": (1, 2 * D_GROUP), "ml_norm_w": (1, D_GROUP), "ln1_g": (1, D_MODEL), "ln1_b": (1, D_MODEL),
                "ln2_g": (1, D_MODEL), "ln2_b": (1, D_MODEL), "ffn_conv_b": (1, D_UP), "ln3_g": (1, D_MODEL),
                "ln3_b": (1, D_MODEL)}


def _shard_2d(name, block):
    t = block[0]
    if name in PAD_TO:
        t = jnp.pad(t, ((0, 0), (0, PAD_TO[name] - t.shape[1])))
    return t


TRANSPOSED = ("w_in", "ffn_w_up")


def _update_shard(name, block):
    if name not in TRANSPOSED:
        return _shard_2d(name, block)
    return jnp.transpose(block, (0, 2, 1))[0]


def _shard_like(name, t, like):
    if name in TRANSPOSED:
        out = jnp.transpose(t[None], (0, 2, 1))
        return with_layout_constraint(out, Layout(major_to_minor=(0, 2, 1))) if name == "ffn_w_up" else out
    return t[:, :like.shape[2]][None]


def _pad_cols(t, width):
    return jnp.pad(t, ((0, 0), (0, width - t.shape[1])))


FIRST_NAMES = ("w_in", "ml_conv_w")
FFN_NAMES = ("ffn_w_up", "ffn_w_down", "ffn_conv_w")
MID_NAMES = ("ca_wo", "ca_wq", "ca_wkv", "w_out")


def _first_weights(g, small):
    w = dict(small)
    w["w_in_t"] = g["w_in"].reshape(D_IN, D_MODEL)
    w["w_in_gate_t"] = jnp.pad(w["w_in_t"][D_IN_MAIN:], ((0, LANES - (D_IN - D_IN_MAIN)), (0, 0)))
    w["b_in_main"] = small["b_in"][:, :D_IN_MAIN]
    w["b_in_gate"] = _pad_cols(small["b_in"][:, D_IN_MAIN:], LANES)
    w["ml_conv_w"] = jnp.transpose(g["ml_conv_w"], (1, 0, 2)).reshape(ML_CONV, 2 * D_GROUP)
    return w


def _mid_weights(g):
    w = {n: g[n].reshape(D_MODEL, D_MODEL) for n in ("w_out", "ca_wq", "ca_wo")}
    w["ca_wkv"] = g["ca_wkv"]
    return w


FFN_UP_NAMES = ("ffn_w_up", "ffn_conv_w")
FFN_DOWN_NAMES = ("ffn_w_down",)


def _ffn_up_weights(g, small):
    w = {"ffn_w_up_t": g["ffn_w_up"].reshape(D_UP, D_MODEL)}
    w["ffn_conv_w"] = jnp.transpose(g["ffn_conv_w"][:, :, :UP_SHARD], (1, 0, 2)).reshape(FFN_CONV, D_UP)
    w["ffn_conv_b"] = small["ffn_conv_b"].reshape(1, D_UP)
    return w


def _ffn_down_weights(g):
    return {"ffn_w_down": g["ffn_w_down"].reshape(D_FF, D_MODEL)}


def _owner_stack(n, grads):
    if n == "w_in":
        return grads[n].reshape(N_DEV, W_IN_SHARD, D_MODEL)
    if n == "ffn_w_up":
        return grads[n].reshape(N_DEV, UP_SHARD, D_MODEL)
    if n in ("w_out", "ca_wq", "ca_wo"):
        return grads[n].reshape(N_DEV, D_MODEL // N_DEV, D_MODEL)
    if n == "ffn_w_down":
        return grads[n].reshape(N_DEV, D_FF // N_DEV, D_MODEL)
    if n == "ml_conv_w":
        return jnp.transpose(grads[n].reshape(ML_CONV, N_DEV, LANES), (1, 0, 2))
    if n == "ffn_conv_w":
        shards = jnp.transpose(grads[n].reshape(FFN_CONV, N_DEV, UP_SHARD), (1, 0, 2))
        return jnp.pad(shards, ((0, 0), (0, 0), (0, UP_SHARD_P - UP_SHARD)))
    return grads[n]


def _small_grads(grads):
    out = {n: grads[n] for n in SMALL_NAMES if n in grads}
    out["b_in"] = jnp.concatenate([grads["b_in_main"], grads["b_in_gate"][:, :D_IN - D_IN_MAIN]], axis=1)
    return out


def _small_rows(shape):
    return shape[0] if shape[1] <= SMALL_W else -(-shape[1] // SMALL_W)


SMALL_BASE = {n: sum(_small_rows(SMALL_SHAPES[k]) for k in SMALL_NAMES[:i]) for i, n in enumerate(SMALL_NAMES)}
SMALL_LOSS_ROW = sum(_small_rows(SMALL_SHAPES[n]) for n in SMALL_NAMES)
assert SMALL_LOSS_ROW < SMALL_ROWS


def _small_pieces(name, shape):
    base = SMALL_BASE[name]
    if shape[1] <= SMALL_W:
        return [(i, 0, shape[1], base + i) for i in range(shape[0])]
    return [(0, k * SMALL_W, min(SMALL_W, shape[1] - k * SMALL_W), base + k) for k in range(_small_rows(shape))]


def _pack_small(p, loss):
    rows = []
    for n in SMALL_NAMES:
        t = p[n]
        nrows = _small_rows(t.shape)
        if t.shape[1] <= SMALL_W:
            rows.append(_pad_cols(t, SMALL_W))
        else:
            rows.append(_pad_cols(t, nrows * SMALL_W).reshape(nrows, SMALL_W))
    rows.append(_pad_cols(loss, SMALL_W))
    slab = jnp.concatenate(rows, axis=0)
    return jnp.pad(slab, ((0, SMALL_ROWS - slab.shape[0]), (0, 0)))


def kernel(x, mem, w_in, b_in, hg_lb_logits, hg_norm_w, ml_conv_w, ml_conv_b, ml_norm_w, w_out, ln1_g, ln1_b, ca_wq, ca_wkv, ca_wo, ln2_g, ln2_b, ffn_w_up, ffn_conv_w, ffn_conv_b, ffn_w_down, ln3_g, ln3_b, loss_target, m_w_in, m_b_in, m_hg_lb_logits, m_hg_norm_w, m_ml_conv_w, m_ml_conv_b, m_ml_norm_w, m_w_out, m_ln1_g, m_ln1_b, m_ca_wq, m_ca_wkv, m_ca_wo, m_ln2_g, m_ln2_b, m_ffn_w_up, m_ffn_conv_w, m_ffn_conv_b, m_ffn_w_down, m_ln3_g, m_ln3_b, v_w_in, v_b_in, v_hg_lb_logits, v_hg_norm_w, v_ml_conv_w, v_ml_conv_b, v_ml_norm_w, v_w_out, v_ln1_g, v_ln1_b, v_ca_wq, v_ca_wkv, v_ca_wo, v_ln2_g, v_ln2_b, v_ffn_w_up, v_ffn_conv_w, v_ffn_conv_b, v_ffn_w_down, v_ln3_g, v_ln3_b):
    params = dict(w_in=w_in, b_in=b_in, hg_lb_logits=hg_lb_logits, hg_norm_w=hg_norm_w, ml_conv_w=ml_conv_w,
                  ml_conv_b=ml_conv_b, ml_norm_w=ml_norm_w, w_out=w_out, ln1_g=ln1_g, ln1_b=ln1_b, ca_wq=ca_wq,
                  ca_wkv=ca_wkv, ca_wo=ca_wo, ln2_g=ln2_g, ln2_b=ln2_b, ffn_w_up=ffn_w_up, ffn_conv_w=ffn_conv_w,
                  ffn_conv_b=ffn_conv_b, ffn_w_down=ffn_w_down, ln3_g=ln3_g, ln3_b=ln3_b)
    mom1 = dict(w_in=m_w_in, b_in=m_b_in, hg_lb_logits=m_hg_lb_logits, hg_norm_w=m_hg_norm_w,
                ml_conv_w=m_ml_conv_w, ml_conv_b=m_ml_conv_b, ml_norm_w=m_ml_norm_w, w_out=m_w_out, ln1_g=m_ln1_g,
                ln1_b=m_ln1_b, ca_wq=m_ca_wq, ca_wkv=m_ca_wkv, ca_wo=m_ca_wo, ln2_g=m_ln2_g, ln2_b=m_ln2_b,
                ffn_w_up=m_ffn_w_up, ffn_conv_w=m_ffn_conv_w, ffn_conv_b=m_ffn_conv_b, ffn_w_down=m_ffn_w_down,
                ln3_g=m_ln3_g, ln3_b=m_ln3_b)
    mom2 = dict(w_in=v_w_in, b_in=v_b_in, hg_lb_logits=v_hg_lb_logits, hg_norm_w=v_hg_norm_w,
                ml_conv_w=v_ml_conv_w, ml_conv_b=v_ml_conv_b, ml_norm_w=v_ml_norm_w, w_out=v_w_out, ln1_g=v_ln1_g,
                ln1_b=v_ln1_b, ca_wq=v_ca_wq, ca_wkv=v_ca_wkv, ca_wo=v_ca_wo, ln2_g=v_ln2_g, ln2_b=v_ln2_b,
                ffn_w_up=v_ffn_w_up, ffn_conv_w=v_ffn_conv_w, ffn_conv_b=v_ffn_conv_b, ffn_w_down=v_ffn_w_down,
                ln3_g=v_ln3_g, ln3_b=v_ln3_b)

    x_idx, y_idx, c_idx = _coords()
    as_index = lambda v: jnp.reshape(v, (1,)).astype(jnp.int32)
    me = as_index(4 * x_idx + 2 * y_idx + c_idx)
    small_params = {n: params[n] for n in SMALL_NAMES}

    shards = {n: _update_shard(n, params[n]) for n in SHARDED_NAMES}
    m_shards = {n: _update_shard(n, mom1[n]) for n in SHARDED_NAMES}
    v_shards = {n: _update_shard(n, mom2[n]) for n in SHARDED_NAMES}
    outgoing = {n: shards[n] if "conv" in n else shards[n].astype(BF16) for n in SHARDED_NAMES}
    to_send = lambda names: [outgoing[n] for n in names]
    late = {}

    def glue(slot):
        late.update({n: _own_slot_filled(outgoing[n], slot) for n in SHARDED_NAMES if n not in FIRST_NAMES})
        return [*m_shards.values(), *v_shards.values(), *shards.values(), *late.values()]

    first = dict(zip(FIRST_NAMES, _two_level_gather(to_send(FIRST_NAMES), glue, "weights_gather_first")))

    late_names = MID_NAMES + FFN_UP_NAMES + FFN_DOWN_NAMES
    (late_send, late_recv, late_lands), first["w_in"] = _split_copy_start(
        _slot_copies, 7 * len(late_names), [late[n] for n in late_names], first["w_in"], "weights_gather_start_late")

    def late_group(names):
        a0 = late_names.index(names[0])
        return a0, (late_send, late_recv, late_lands[a0:a0 + len(names)])

    mid_started, ffn_started, down_started = (late_group(g) for g in (MID_NAMES, FFN_UP_NAMES, FFN_DOWN_NAMES))

    def gathered_weights(names, started, after, tag):
        a0, started = started
        return dict(zip(names, _split_copy_wait(functools.partial(_slot_copies, first=a0), started, after,
                                                "weights_gather_wait_" + tag)))

    started = {}

    def start_group(names, tag):
        def hook(grads, through):
            stacks = [_owner_stack(n, grads).astype(BF16) for n in names]
            started[tag], through = _direct_start(False, stacks, through, "grads_start_" + tag)
            return through
        return hook

    def start_small(grads, loss, through):
        started["small"], through = _direct_start(True, [_pack_small(_small_grads(grads), loss)], through,
                                                  "small_gather_start")
        return through

    loss, grad_x, grads = _local_step(
        x[0], mem[0], loss_target[0], _first_weights(first, small_params),
        lambda y: _mid_weights(gathered_weights(MID_NAMES, mid_started, y, "mid")),
        lambda x2: _ffn_up_weights(gathered_weights(FFN_UP_NAMES, ffn_started, x2, "ffn_up"), small_params),
        lambda hid: _ffn_down_weights(gathered_weights(FFN_DOWN_NAMES, down_started, hid, "ffn_down")),
        start_group(FFN_NAMES, "ffn"), start_group(MID_NAMES, "mid"), start_small, start_group(FIRST_NAMES, "last"))

    updated, sharded_out = {}, {}

    def update_group(names, tag, after):
        stacks, lands = _direct_wait(False, started[tag], after, "grads_wait_" + tag)
        for n, st, land in zip(names, stacks, lands):
            updated[n] = _adamw_sharded(me, st, land, shards[n], m_shards[n], v_shards[n], "adamw_" + n)
            sharded_out[n] = [_shard_like(n, t, params[n]) for t in updated[n]]

    update_group(FFN_NAMES, "ffn", grad_x)
    update_group(MID_NAMES, "mid", grad_x)
    own_small, small_lands = _direct_wait(True, started["small"], grad_x, "small_gather_wait")
    small_parts = lax.dynamic_update_index_in_dim(small_lands[0], own_small[0], me[0], 0)
    small_out, total_loss = _adamw_replicated(small_parts, small_params, {n: mom1[n] for n in SMALL_NAMES},
                                              {n: mom2[n] for n in SMALL_NAMES})
    done = [t for n in FFN_NAMES + MID_NAMES for t in updated[n]]
    done += [t for small in small_out for t in small.values()]
    update_group(FIRST_NAMES, "last", done)

    outs = []
    for k, small in enumerate(small_out):
        outs.extend(sharded_out[n][k] if n in sharded_out else small[n] for n in WEIGHT_NAMES)
    return (total_loss[0, 0], grad_x[None], *outs)
```
